```python
import math
import jax, jax.numpy as jnp
from jax import lax
import numpy as np

D_MODEL = 1024
BATCH = 4
SEQ = 8192
DEPTH = 2

CHUNK = 64
Q_BLOCK = 128
N_EVEN = (DEPTH + 1) // 2
N_ODD = DEPTH // 2

A_HEADS = 4
A_DK = 128
A_DV = 128
CONV_WIDTH = 4
B_HEADS = 4
B_DH = 128
IDX_HEADS = 8
IDX_DIM = 64
TOPK_MAX = 256
C_HEADS = 4
C_DH = 128
D_HEADS = 4
D_DK = 128
D_DV = 128
REL_BUCKETS = 32
REL_MAX_DIST = 128
D_FF = ((8 * D_MODEL // 3 + 255) // 256) * 256

A_QKV_W = 2 * A_HEADS * A_DK + A_HEADS * A_DV
B_W = B_HEADS * B_DH
EVEN_WIDTHS = (A_QKV_W, A_HEADS * A_DV, A_HEADS, A_HEADS, B_W, B_W, B_W, IDX_HEADS * IDX_DIM, IDX_DIM, IDX_HEADS)
EVEN_IN = A_QKV_W + A_HEADS * A_DV + 2 * A_HEADS + 3 * B_W + IDX_HEADS * IDX_DIM + IDX_DIM + IDX_HEADS
MIX_EVEN = A_HEADS * A_DV + B_HEADS * B_DH
C_W = C_HEADS * C_DH
ODD_WIDTHS = (C_W, C_W, C_W, D_HEADS * D_DK, D_HEADS * D_DK, D_HEADS * D_DV, D_HEADS * D_DV)
ODD_IN = 3 * C_W + 2 * D_HEADS * D_DK + 2 * D_HEADS * D_DV
MIX_ODD = C_HEADS * C_DH + D_HEADS * D_DV

kernel_name = "hybrid_deltanet_dsa_stickbreak_hgrn2_trunk"


def _split(t, widths):
    out = []
    start = 0
    for w in widths:
        out.append(t[..., start:start + w])
        start += w
    return out


def _heads(t, n):
    return t.reshape(t.shape[0], t.shape[1], n, -1)


def rms_norm(x, g, eps=1e-6):
    xf = x.astype(jnp.float32)
    y = xf * lax.rsqrt(jnp.mean(xf * xf, axis=-1, keepdims=True) + eps)
    return (y * g.astype(jnp.float32)).astype(x.dtype)


def causal_conv(x, w):
    width = w.shape[0]
    s = x.shape[1]
    xp = jnp.pad(x, ((0, 0), (width - 1, 0), (0, 0)))
    out = xp[:, 0:s] * w[0]
    for j in range(1, width):
        out = out + xp[:, j:j + s] * w[j]
    return out


def t5_bucket(rel):
    nb = REL_BUCKETS // 2
    max_exact = nb // 2
    ret = jnp.where(rel > 0, nb, 0)
    n = jnp.abs(rel)
    large = max_exact + (jnp.log(jnp.maximum(n, 1).astype(jnp.float32) / max_exact)
                         / math.log(REL_MAX_DIST / max_exact) * (nb - max_exact)).astype(jnp.int32)
    large = jnp.minimum(large, nb - 1)
    return ret + jnp.where(n < max_exact, n, large)


def _to_chunks(t):
    b, s, h, d = t.shape
    return t.reshape(b, s // CHUNK, CHUNK, h, d).transpose(0, 3, 1, 2, 4)


def _from_scan(o):
    n, b, h, c, d = o.shape
    return o.transpose(1, 0, 3, 2, 4).reshape(b, n * c, h, d)


def _l2norm(t, eps=1e-6):
    return t * lax.rsqrt(jnp.sum(t * t, axis=-1, keepdims=True) + eps)


def gated_deltanet(q, k, v, a, b, a_log, dt_bias):
    bsz, _, h, dk = q.shape
    dv = v.shape[-1]
    q = _l2norm(q) * (dk ** -0.5)
    k = _l2norm(k)
    beta = jax.nn.sigmoid(b)
    g = -jnp.exp(a_log) * jax.nn.softplus(a + dt_bias)
    q, k, v = _to_chunks(q), _to_chunks(k), _to_chunks(v)
    beta = _to_chunks(beta[..., None])[..., 0]
    gc = jnp.cumsum(_to_chunks(g[..., None])[..., 0], axis=-1)
    tri = jnp.tril(jnp.ones((CHUNK, CHUNK), bool))
    strict = jnp.tril(jnp.ones((CHUNK, CHUNK), bool), k=-1)
    decay = jnp.exp(jnp.where(tri, gc[..., :, None] - gc[..., None, :], -jnp.inf))
    kk = jnp.einsum('bhncd,bhnjd->bhncj', k, k)
    m = jnp.eye(CHUNK, dtype=q.dtype) + jnp.where(strict, beta[..., None] * kk * decay, 0.0)
    rhs = jnp.concatenate([v * beta[..., None], k * (beta * jnp.exp(gc))[..., None]], axis=-1)
    sol = lax.linalg.triangular_solve(m, rhs, left_side=True, lower=True, unit_diagonal=True)
    u, w = sol[..., :dv], sol[..., dv:]
    qk = jnp.einsum('bhncd,bhnjd->bhncj', q, k) * decay
    q_dec = q * jnp.exp(gc)[..., None]
    k_dec = k * jnp.exp(gc[..., -1:] - gc)[..., None]
    g_last = jnp.exp(gc[..., -1])

    def step(state, xs):
        q_c, qk_c, u_c, w_c, kd_c, gl_c = xs
        v_new = u_c - jnp.einsum('bhcd,bhdv->bhcv', w_c, state)
        o = jnp.einsum('bhcd,bhdv->bhcv', q_c, state) + jnp.einsum('bhcj,bhjv->bhcv', qk_c, v_new)
        state = state * gl_c[..., None, None] + jnp.einsum('bhcd,bhcv->bhdv', kd_c, v_new)
        return state, o

    xs = tuple(jnp.moveaxis(t, 2, 0) for t in (q_dec, qk, u, w, k_dec, g_last))
    s0 = jnp.zeros((bsz, h, dk, dv), q.dtype)
    _, o = lax.scan(step, s0, xs)
    return _from_scan(o)


def dsa_attention(q, k, v, qi, ki, wi, rel_table):
    bsz, s, h, dh = q.shape
    k_sel = min(TOPK_MAX, s // 4)
    nblk = s // Q_BLOCK
    key_pos = jnp.arange(s)
    wi = wi * (IDX_HEADS ** -0.5)

    def block(start):
        qb = lax.dynamic_slice_in_dim(q, start, Q_BLOCK, axis=1)
        qib = lax.dynamic_slice_in_dim(qi, start, Q_BLOCK, axis=1)
        wib = lax.dynamic_slice_in_dim(wi, start, Q_BLOCK, axis=1)
        qpos = start + jnp.arange(Q_BLOCK)
        limit = (qpos // CHUNK + 1) * CHUNK
        admissible = key_pos[None, :] < limit[:, None]
        score = jnp.einsum('bthd,bsd->bths', qib, ki) * (IDX_DIM ** -0.5)
        score = jnp.einsum('bths,bth->bts', jax.nn.relu(score), wib)
        score = jnp.where(admissible[None], score, -jnp.inf)
        _, idx = lax.top_k(score, k_sel)
        k_g = jax.vmap(lambda kk, ii: kk[ii])(k, idx)
        v_g = jax.vmap(lambda vv, ii: vv[ii])(v, idx)
        valid = idx < limit[None, :, None]
        bias = rel_table[t5_bucket(idx - qpos[None, :, None])]
        logits = jnp.einsum('bthd,btkhd->bthk', qb, k_g) * (dh ** -0.5) + bias.transpose(0, 1, 3, 2)
        logits = jnp.where(valid[:, :, None, :], logits, -jnp.inf)
        p = jax.nn.softmax(logits, axis=-1)
        return jnp.einsum('bthk,btkhd->bthd', p, v_g)

    out = lax.map(block, jnp.arange(nblk) * Q_BLOCK)
    return out.transpose(1, 0, 2, 3, 4).reshape(bsz, s, h, dh)


def stick_breaking(q, k, v):
    bsz, s, h, dh = q.shape
    nblk = s // Q_BLOCK
    kh = k.transpose(0, 2, 1, 3)
    vh = v.transpose(0, 2, 1, 3)
    key_pos = jnp.arange(s)

    def block(start):
        qb = lax.dynamic_slice_in_dim(q, start, Q_BLOCK, axis=1)
        qpos = start + jnp.arange(Q_BLOCK)
        causal = key_pos[None, :] < qpos[:, None]
        z = jnp.einsum('bthd,bhsd->bhts', qb, kh) * (dh ** -0.5)
        log_1mb = jnp.where(causal, jax.nn.log_sigmoid(-z), 0.0)
        rest = lax.cumsum(log_1mb, axis=3, reverse=True) - log_1mb
        logw = jnp.where(causal, jax.nn.log_sigmoid(z) + rest, -jnp.inf)
        return jnp.einsum('bhts,bhsd->bthd', jnp.exp(logw), vh)

    out = lax.map(block, jnp.arange(nblk) * Q_BLOCK)
    return out.transpose(1, 0, 2, 3, 4).reshape(bsz, s, h, dh)


def hgrn2(q, f_raw, i, lb):
    bsz, _, h, dk = q.shape
    dv = i.shape[-1]
    log_f = jnp.logaddexp(jnp.log(lb), jnp.log1p(-lb) + jax.nn.log_sigmoid(f_raw))
    key = (1.0 - lb) * jax.nn.sigmoid(-f_raw)
    q, key, i = _to_chunks(q), _to_chunks(key), _to_chunks(i)
    gc = jnp.cumsum(_to_chunks(log_f), axis=3)
    tri = jnp.tril(jnp.ones((CHUNK, CHUNK), bool))[:, :, None]

    def step(state, xs):
        q_c, k_c, v_c, g_c = xs
        diff = g_c[:, :, :, None, :] - g_c[:, :, None, :, :]
        decay = jnp.exp(jnp.where(tri, diff, -jnp.inf))
        attn = jnp.einsum('bhid,bhijd,bhjd->bhij', q_c, decay, k_c)
        o = jnp.einsum('bhid,bhdv->bhiv', q_c * jnp.exp(g_c), state) + jnp.einsum('bhij,bhjv->bhiv', attn, v_c)
        g_last = g_c[:, :, -1:, :]
        state = jnp.exp(g_last)[:, :, 0, :, None] * state + jnp.einsum('bhjd,bhjv->bhdv', k_c * jnp.exp(g_last - g_c), v_c)
        return state, o

    xs = tuple(jnp.moveaxis(t, 2, 0) for t in (q, key, i, gc))
    s0 = jnp.zeros((bsz, h, dk, dv), q.dtype)
    _, o = lax.scan(step, s0, xs)
    return _from_scan(o)


def even_mixer(y, w_in, conv_w, a_log, dt_bias, a_norm_g, rel_table, w_out):
    f32 = jnp.float32
    p = (y @ w_in).astype(f32)
    qkv_a, z_a, a_a, b_a, q_b, k_b, v_b, q_i, k_i, w_i = _split(p, EVEN_WIDTHS)
    qkv_a = jax.nn.silu(causal_conv(qkv_a, conv_w.astype(f32)))
    q_a, k_a, v_a = _split(qkv_a, (A_HEADS * A_DK, A_HEADS * A_DK, A_HEADS * A_DV))
    o_a = gated_deltanet(_heads(q_a, A_HEADS), _heads(k_a, A_HEADS), _heads(v_a, A_HEADS),
                         a_a, b_a, a_log.astype(f32), dt_bias.astype(f32))
    o_a = rms_norm(o_a, a_norm_g) * jax.nn.silu(_heads(z_a, A_HEADS))
    o_b = dsa_attention(_heads(q_b, B_HEADS), _heads(k_b, B_HEADS), _heads(v_b, B_HEADS),
                        _heads(q_i, IDX_HEADS), k_i, w_i, rel_table.astype(f32))
    bsz, s = y.shape[0], y.shape[1]
    cat = jnp.concatenate([o_a.reshape(bsz, s, -1), o_b.reshape(bsz, s, -1)], axis=-1)
    return cat.astype(y.dtype) @ w_out


def odd_mixer(y, w_in, lb, d_norm_g, w_out):
    f32 = jnp.float32
    p = (y @ w_in).astype(f32)
    q_c, k_c, v_c, q_d, f_d, i_d, g_d = _split(p, ODD_WIDTHS)
    o_c = stick_breaking(_heads(q_c, C_HEADS), _heads(k_c, C_HEADS), _heads(v_c, C_HEADS))
    lb_h = lb.astype(f32).reshape(D_HEADS, D_DK)
    o_d = hgrn2(jax.nn.silu(_heads(q_d, D_HEADS)), _heads(f_d, D_HEADS), _heads(i_d, D_HEADS), lb_h)
    o_d = rms_norm(o_d, d_norm_g) * jax.nn.silu(_heads(g_d, D_HEADS))
    bsz, s = y.shape[0], y.shape[1]
    cat = jnp.concatenate([o_c.reshape(bsz, s, -1), o_d.reshape(bsz, s, -1)], axis=-1)
    return cat.astype(y.dtype) @ w_out


def swiglu(y, wg, wu, wd):
    return (jax.nn.silu(y @ wg) * (y @ wu)) @ wd


def setup_inputs(seed: int = 0) -> dict:
    key = jax.random.key(seed)
    ks = jax.random.split(key, 16)
    f32 = jnp.float32

    def normal(k, shape, scale):
        return jax.random.normal(k, shape, f32) * scale

    x = normal(ks[0], (BATCH, SEQ, D_MODEL), 1.0)
    norm_g = 1.0 + normal(ks[1], (DEPTH, 4, D_MODEL), 0.02)
    w_in_even = normal(ks[2], (N_EVEN, D_MODEL, EVEN_IN), D_MODEL ** -0.5)
    conv_w_even = normal(ks[3], (N_EVEN, CONV_WIDTH, A_QKV_W), CONV_WIDTH ** -0.5)
    a_log_even = jnp.log(jax.random.uniform(ks[4], (N_EVEN, A_HEADS), f32, 1.0, 16.0))
    dt = jnp.exp(jax.random.uniform(ks[5], (N_EVEN, A_HEADS), f32, math.log(1e-3), math.log(1e-1)))
    dt_bias_even = dt + jnp.log(-jnp.expm1(-dt))
    a_norm_even = 1.0 + normal(ks[6], (N_EVEN, A_DV), 0.02)
    w_out_even = normal(ks[7], (N_EVEN, MIX_EVEN, D_MODEL), MIX_EVEN ** -0.5)
    rel_bias = normal(ks[8], (REL_BUCKETS, B_HEADS), 0.5)
    w_in_odd = normal(ks[9], (N_ODD, D_MODEL, ODD_IN), D_MODEL ** -0.5)
    lb_logits = normal(ks[10], (DEPTH, D_HEADS * D_DK), 0.1)
    d_norm_odd = 1.0 + normal(ks[11], (N_ODD, D_DV), 0.02)
    w_out_odd = normal(ks[12], (N_ODD, MIX_ODD, D_MODEL), MIX_ODD ** -0.5)
    w_gate = normal(ks[13], (DEPTH, D_MODEL, D_FF), D_MODEL ** -0.5)
    w_up = normal(ks[14], (DEPTH, D_MODEL, D_FF), D_MODEL ** -0.5)
    w_down = normal(ks[15], (DEPTH, D_FF, D_MODEL), D_FF ** -0.5)
    return {"x": x, "norm_g": norm_g, "w_in_even": w_in_even, "conv_w_even": conv_w_even,
            "a_log_even": a_log_even, "dt_bias_even": dt_bias_even, "a_norm_even": a_norm_even,
            "w_out_even": w_out_even, "rel_bias": rel_bias, "w_in_odd": w_in_odd,
            "lb_logits": lb_logits, "d_norm_odd": d_norm_odd, "w_out_odd": w_out_odd,
            "w_gate": w_gate, "w_up": w_up, "w_down": w_down}


def reference(x, norm_g, w_in_even, conv_w_even, a_log_even, dt_bias_even, a_norm_even,
              w_out_even, rel_bias, w_in_odd, lb_logits, d_norm_odd, w_out_odd,
              w_gate, w_up, w_down):
    lb_all = jnp.cumsum(jax.nn.softmax(lb_logits.astype(jnp.float32), axis=0), axis=0)
    lb_all = lb_all - lb_all[:1]
    h = x
    for l in range(DEPTH):
        y = rms_norm(h, norm_g[l, 0])
        if l % 2 == 0:
            e = l // 2
            y = even_mixer(y, w_in_even[e], conv_w_even[e], a_log_even[e], dt_bias_even[e],
                           a_norm_even[e], rel_bias, w_out_even[e])
        else:
            o = l // 2
            y = odd_mixer(y, w_in_odd[o], lb_all[l], d_norm_odd[o], w_out_odd[o])
        h = h + rms_norm(y, norm_g[l, 1])
        y = swiglu(rms_norm(h, norm_g[l, 2]), w_gate[l], w_up[l], w_down[l])
        h = h + rms_norm(y, norm_g[l, 3])
    return h
```

```python
import functools
import math

import jax
import jax.numpy as jnp
from jax import lax
from jax.experimental import pallas as pl
from jax.experimental.pallas import tpu as pltpu

F32 = jnp.float32
BF16 = jnp.bfloat16
HIGHEST = lax.Precision.HIGHEST

CHUNK = 64
HEAD_DIM = 128
N_HEADS = 4
IDX_HEADS = 8
IDX_DIM = 64
TOPK_MAX = 256
CONV_WIDTH = 4
REL_BUCKETS = 32
REL_MAX_DIST = 128
EPS = 1e-6
NEG_BIG = -1e30
VMEM_LIMIT = 56 * 1024 * 1024


def _mm(a, b):
    return jnp.dot(a.astype(BF16), b.astype(BF16), preferred_element_type=F32)


def _mm_nt(a, b):
    return lax.dot_general(a.astype(BF16), b.astype(BF16), (((1,), (1,)), ((), ())),
                           preferred_element_type=F32)


def _mm_tn(a, b):
    return lax.dot_general(a.astype(BF16), b.astype(BF16), (((0,), (0,)), ((), ())),
                           preferred_element_type=F32)


def _mm_f32(a, b):
    return jnp.dot(a, b, precision=HIGHEST, preferred_element_type=F32)


def _sigmoid(x):
    return 1.0 / (1.0 + jnp.exp(-x))


def _silu(x):
    return x * _sigmoid(x)


def _softplus(x):
    return jnp.maximum(x, 0.0) + jnp.log1p(jnp.exp(-jnp.abs(x)))


def _rms(x, g):
    return x * lax.rsqrt(jnp.mean(x * x, axis=-1, keepdims=True) + EPS) * g


def _iota(shape, dim):
    return lax.broadcasted_iota(jnp.int32, shape, dim)


def _ind(mask):
    return jnp.where(mask, 1.0, 0.0)


def _norm_matmul_kernel(x_ref, g_ref, w_ref, o32_ref, o16_ref, xn_ref):
    @pl.when(pl.program_id(1) == 0)
    def _():
        xn_ref[...] = _rms(x_ref[...], g_ref[...]).astype(BF16)

    y = jnp.dot(xn_ref[...], w_ref[...], preferred_element_type=F32)
    o32_ref[...] = y
    o16_ref[...] = y.astype(BF16)


def _norm_matmul(x, g, w, *, tm, tn):
    t, d = x.shape
    n = w.shape[1]
    return pl.pallas_call(
        _norm_matmul_kernel,
        grid=(t // tm, n // tn),
        in_specs=[pl.BlockSpec((tm, d), lambda i, j: (i, 0)),
                  pl.BlockSpec((1, d), lambda i, j: (0, 0)),
                  pl.BlockSpec((d, tn), lambda i, j: (0, j))],
        out_specs=[pl.BlockSpec((tm, tn), lambda i, j: (i, j)),
                   pl.BlockSpec((tm, tn), lambda i, j: (i, j))],
        out_shape=[jax.ShapeDtypeStruct((t, n), F32), jax.ShapeDtypeStruct((t, n), BF16)],
        scratch_shapes=[pltpu.VMEM((tm, d), BF16)],
        compiler_params=pltpu.CompilerParams(
            dimension_semantics=("parallel", "arbitrary"), vmem_limit_bytes=VMEM_LIMIT),
        name="norm_matmul",
    )(x, g.reshape(1, d), w)


def _outproj_kernel(ca_ref, cb_ref, wa_ref, wb_ref, h_ref, g_ref, o_ref):
    y = (jnp.dot(ca_ref[...].astype(BF16), wa_ref[...], preferred_element_type=F32)
         + jnp.dot(cb_ref[...].astype(BF16), wb_ref[...], preferred_element_type=F32))
    o_ref[...] = h_ref[...] + _rms(y, g_ref[...])


def _outproj(ca, cb, w, h, g, *, tm):
    t, d = h.shape
    wa_n = ca.shape[1]
    wb_n = cb.shape[1]
    wa = w[:wa_n].astype(BF16)
    wb = w[wa_n:].astype(BF16)
    return pl.pallas_call(
        _outproj_kernel,
        grid=(t // tm,),
        in_specs=[pl.BlockSpec((tm, wa_n), lambda i: (i, 0)),
                  pl.BlockSpec((tm, wb_n), lambda i: (i, 0)),
                  pl.BlockSpec((wa_n, d), lambda i: (0, 0)),
                  pl.BlockSpec((wb_n, d), lambda i: (0, 0)),
                  pl.BlockSpec((tm, d), lambda i: (i, 0)),
                  pl.BlockSpec((1, d), lambda i: (0, 0))],
        out_specs=pl.BlockSpec((tm, d), lambda i: (i, 0)),
        out_shape=jax.ShapeDtypeStruct((t, d), F32),
        compiler_params=pltpu.CompilerParams(
            dimension_semantics=("parallel",), vmem_limit_bytes=VMEM_LIMIT),
        name="outproj",
    )(ca, cb, wa, wb, h, g.reshape(1, d))


def _ffn_kernel(h_ref, gpre_ref, gpost_ref, wg_ref, wu_ref, wd_ref, o_ref, xn_ref, acc_ref):
    f = pl.program_id(1)

    @pl.when(f == 0)
    def _():
        xn_ref[...] = _rms(h_ref[...], gpre_ref[...]).astype(BF16)
        acc_ref[...] = jnp.zeros_like(acc_ref)

    xn = xn_ref[...]
    gate = jnp.dot(xn, wg_ref[...], preferred_element_type=F32)
    up = jnp.dot(xn, wu_ref[...], preferred_element_type=F32)
    act = (_silu(gate) * up).astype(BF16)
    acc_ref[...] += jnp.dot(act, wd_ref[...], preferred_element_type=F32)

    @pl.when(f == pl.num_programs(1) - 1)
    def _():
        o_ref[...] = h_ref[...] + _rms(acc_ref[...], gpost_ref[...])


def _ffn(h, g_pre, g_post, wg, wu, wd, *, tm, tf):
    t, d = h.shape
    ff = wg.shape[1]
    return pl.pallas_call(
        _ffn_kernel,
        grid=(t // tm, ff // tf),
        in_specs=[pl.BlockSpec((tm, d), lambda i, f: (i, 0)),
                  pl.BlockSpec((1, d), lambda i, f: (0, 0)),
                  pl.BlockSpec((1, d), lambda i, f: (0, 0)),
                  pl.BlockSpec((d, tf), lambda i, f: (0, f)),
                  pl.BlockSpec((d, tf), lambda i, f: (0, f)),
                  pl.BlockSpec((tf, d), lambda i, f: (f, 0))],
        out_specs=pl.BlockSpec((tm, d), lambda i, f: (i, 0)),
        out_shape=jax.ShapeDtypeStruct((t, d), F32),
        scratch_shapes=[pltpu.VMEM((tm, d), BF16), pltpu.VMEM((tm, d), F32)],
        compiler_params=pltpu.CompilerParams(
            dimension_semantics=("parallel", "arbitrary"), vmem_limit_bytes=VMEM_LIMIT),
        name="ffn",
    )(h, g_pre.reshape(1, d), g_post.reshape(1, d),
      wg.astype(BF16), wu.astype(BF16), wd.astype(BF16))


def _deltanet_kernel(xq_ref, xk_ref, xv_ref, z_ref, sm_ref, cwq_ref, cwk_ref, cwv_ref,
                     alog_ref, dtb_ref, gn_ref, o_ref,
                     xpad_ref, q_ref, k_ref, v_ref, gb_ref, bb_ref, st_ref, *, ts, a_col, b_col):
    h = pl.program_id(1)
    s = pl.program_id(2)
    c = CHUNK
    d = HEAD_DIM

    @pl.when(s == 0)
    def _():
        xpad_ref[:, 0:8, :] = jnp.zeros((3, 8, d), F32)
        st_ref[...] = jnp.zeros_like(st_ref)

    @pl.when(s != 0)
    def _():
        xpad_ref[:, 0:8, :] = xpad_ref[:, ts:ts + 8, :]

    xpad_ref[0, 8:ts + 8, :] = xq_ref[...]
    xpad_ref[1, 8:ts + 8, :] = xk_ref[...]
    xpad_ref[2, 8:ts + 8, :] = xv_ref[...]

    def conv_silu(idx, cw_ref):
        cw = cw_ref[...]
        acc = xpad_ref[idx, 8 - (CONV_WIDTH - 1):8 - (CONV_WIDTH - 1) + ts, :] * cw[0:1, :]
        for j in range(1, CONV_WIDTH):
            off = 8 - (CONV_WIDTH - 1) + j
            acc = acc + xpad_ref[idx, off:off + ts, :] * cw[j:j + 1, :]
        return _silu(acc)

    def l2norm(t):
        return t * lax.rsqrt(jnp.sum(t * t, axis=-1, keepdims=True) + EPS)

    q_ref[...] = l2norm(conv_silu(0, cwq_ref)) * (d ** -0.5)
    k_ref[...] = l2norm(conv_silu(1, cwk_ref))
    v_ref[...] = conv_silu(2, cwv_ref)

    sm = sm_ref[...]
    lane = _iota(sm.shape, 1)
    a_raw = jnp.sum(jnp.where(lane == a_col + h, sm, 0.0), axis=-1, keepdims=True)
    b_raw = jnp.sum(jnp.where(lane == b_col + h, sm, 0.0), axis=-1, keepdims=True)
    hl = _iota((1, d), 1)
    a_log = jnp.sum(jnp.where(hl == h, alog_ref[...], 0.0), axis=-1, keepdims=True)
    dtb = jnp.sum(jnp.where(hl == h, dtb_ref[...], 0.0), axis=-1, keepdims=True)
    g = -jnp.exp(a_log) * _softplus(a_raw + dtb)
    gb_ref[...] = jnp.broadcast_to(g, (ts, d))
    bb_ref[...] = jnp.broadcast_to(_sigmoid(b_raw), (ts, d))

    row = _iota((c, c), 0)
    col = _iota((c, c), 1)
    tri = (col <= row)
    strict = (col < row)
    tri_f = tri.astype(F32)
    upper_f = (row <= col).astype(F32)
    eye = (row == col).astype(F32)
    ones_cc = jnp.ones((c, c), F32)
    gnorm = gn_ref[...]

    def chunk_body(ci, carry):
        r0 = pl.multiple_of(ci * c, c)
        q = q_ref[pl.ds(r0, c), :]
        k = k_ref[pl.ds(r0, c), :]
        v = v_ref[pl.ds(r0, c), :]
        gb = gb_ref[pl.ds(r0, c), :]
        beta = bb_ref[pl.ds(r0, c), :]
        gc = _mm_f32(tri_f, gb)
        gc_row = _mm_f32(ones_cc, gb[:, :c] * upper_f)
        decay = jnp.where(tri, jnp.exp(jnp.minimum(gc[:, :c] - gc_row, 0.0)), 0.0)
        kk = _mm_nt(k, k)
        a_mat = jnp.where(strict, beta[:, :c] * kk * decay, 0.0)
        n = -a_mat
        inv = eye + n
        for _ in range(5):
            n = _mm_f32(n, n)
            inv = inv + _mm_f32(inv, n)
        egc = jnp.exp(gc)
        u = _mm_f32(inv, v * beta)
        w = _mm_f32(inv, k * (beta * egc))
        qk = _mm_nt(q, k) * decay
        q_dec = q * egc
        gl = gc[c - 1:c, :]
        k_dec = k * jnp.exp(gl - gc)
        st = st_ref[...]
        v_new = u - _mm(w, st)
        o = _mm(q_dec, st) + _mm(qk, v_new)
        st_ref[...] = st * jnp.exp(gl) + _mm_tn(k_dec, v_new)
        zc = z_ref[pl.ds(r0, c), :]
        o_ref[pl.ds(r0, c), :] = _rms(o, gnorm) * _silu(zc)
        return carry

    lax.fori_loop(0, ts // c, chunk_body, 0)


def _deltanet(p32, conv_w, a_log, dt_bias, a_norm_g, *, ts, cols):
    bsz, s, _ = p32.shape
    d = HEAD_DIM
    nh = N_HEADS
    qb, kb, vb, zb, smb = cols["qa"], cols["ka"], cols["va"], cols["za"], cols["small"]
    pad = lambda t: jnp.pad(t.astype(F32), (0, d - t.shape[0])).reshape(1, d)
    kernel = functools.partial(_deltanet_kernel, ts=ts, a_col=cols["a_lane"], b_col=cols["b_lane"])
    return pl.pallas_call(
        kernel,
        grid=(bsz, nh, s // ts),
        in_specs=[pl.BlockSpec((None, ts, d), lambda b, h, i: (b, i, qb + h)),
                  pl.BlockSpec((None, ts, d), lambda b, h, i: (b, i, kb + h)),
                  pl.BlockSpec((None, ts, d), lambda b, h, i: (b, i, vb + h)),
                  pl.BlockSpec((None, ts, d), lambda b, h, i: (b, i, zb + h)),
                  pl.BlockSpec((None, ts, d), lambda b, h, i: (b, i, smb)),
                  pl.BlockSpec((CONV_WIDTH, d), lambda b, h, i: (0, h)),
                  pl.BlockSpec((CONV_WIDTH, d), lambda b, h, i: (0, nh + h)),
                  pl.BlockSpec((CONV_WIDTH, d), lambda b, h, i: (0, 2 * nh + h)),
                  pl.BlockSpec((1, d), lambda b, h, i: (0, 0)),
                  pl.BlockSpec((1, d), lambda b, h, i: (0, 0)),
                  pl.BlockSpec((1, d), lambda b, h, i: (0, 0))],
        out_specs=pl.BlockSpec((None, ts, d), lambda b, h, i: (b, i, h)),
        out_shape=jax.ShapeDtypeStruct((bsz, s, nh * d), F32),
        scratch_shapes=[pltpu.VMEM((3, ts + 8, d), F32),
                        pltpu.VMEM((ts, d), F32), pltpu.VMEM((ts, d), F32), pltpu.VMEM((ts, d), F32),
                        pltpu.VMEM((ts, d), F32), pltpu.VMEM((ts, d), F32),
                        pltpu.VMEM((d, d), F32)],
        compiler_params=pltpu.CompilerParams(
            dimension_semantics=("parallel", "parallel", "arbitrary"), vmem_limit_bytes=VMEM_LIMIT),
        name="deltanet",
    )(p32, p32, p32, p32, p32, conv_w.astype(F32), conv_w.astype(F32), conv_w.astype(F32),
      pad(a_log), pad(dt_bias), a_norm_g.astype(F32).reshape(1, d))


def _hgrn2_kernel(q_ref, f_ref, i_ref, gate_ref, lb_ref, gn_ref, o_ref,
                  qs_ref, ks_ref, gc_ref, st_ref, *, ts):
    s = pl.program_id(2)
    c = CHUNK
    d = HEAD_DIM

    @pl.when(s == 0)
    def _():
        st_ref[...] = jnp.zeros_like(st_ref)

    lb = lb_ref[...]
    f_raw = f_ref[...]
    log_sig = jnp.minimum(f_raw, 0.0) - jnp.log1p(jnp.exp(-jnp.abs(f_raw)))
    la = jnp.log(lb)
    lbb = jnp.log1p(-lb) + log_sig
    log_f = jnp.maximum(la, lbb) + jnp.log1p(jnp.exp(-jnp.abs(la - lbb)))
    qs_ref[...] = _silu(q_ref[...])
    ks_ref[...] = (1.0 - lb) * _sigmoid(-f_raw)

    row = _iota((c, c), 0)
    col = _iota((c, c), 1)
    tri_f = (col <= row).astype(F32)
    ones_dd = jnp.ones((d, d), BF16)
    rows_cd = _iota((c, d), 0)
    gnorm = gn_ref[...]

    for ci in range(ts // c):
        gc_ref[ci * c:(ci + 1) * c, :] = _mm_f32(tri_f, log_f[ci * c:(ci + 1) * c, :])

    def chunk_loop(ci, carry):
        r0 = pl.multiple_of(ci * c, c)
        q = qs_ref[pl.ds(r0, c), :]
        k = ks_ref[pl.ds(r0, c), :]
        v = i_ref[pl.ds(r0, c), :]
        gc = gc_ref[pl.ds(r0, c), :]

        def key_body(j, o_acc):
            k_j = ks_ref[pl.ds(r0 + j, 1), :]
            g_j = gc_ref[pl.ds(r0 + j, 1), :]
            v_j = i_ref[pl.ds(r0 + j, 1), :]
            e = jnp.where(rows_cd >= j, jnp.exp(jnp.minimum(gc - g_j, 0.0)), 0.0)
            p = q * k_j * e
            a = jnp.dot(p.astype(BF16), ones_dd, preferred_element_type=F32)
            return o_acc + a * v_j

        o_intra = lax.fori_loop(0, c, key_body, jnp.zeros((c, d), F32))

        st = st_ref[...]
        gl = gc[c - 1:c, :]
        o = o_intra + _mm_nt(q * jnp.exp(gc), st)
        st_ref[...] = st * jnp.exp(gl) + _mm_tn(v, k * jnp.exp(gl - gc))
        o_ref[pl.ds(r0, c), :] = _rms(o, gnorm) * _silu(gate_ref[pl.ds(r0, c), :])
        return carry

    lax.fori_loop(0, ts // c, chunk_loop, 0)


def _hgrn2(p32, lb, d_norm_g, *, ts, cols):
    bsz, s, _ = p32.shape
    d = HEAD_DIM
    nh = N_HEADS
    qb, fb, ib, gb = cols["qd"], cols["fd"], cols["id"], cols["gd"]
    kernel = functools.partial(_hgrn2_kernel, ts=ts)
    return pl.pallas_call(
        kernel,
        grid=(bsz, nh, s // ts),
        in_specs=[pl.BlockSpec((None, ts, d), lambda b, h, i: (b, i, qb + h)),
                  pl.BlockSpec((None, ts, d), lambda b, h, i: (b, i, fb + h)),
                  pl.BlockSpec((None, ts, d), lambda b, h, i: (b, i, ib + h)),
                  pl.BlockSpec((None, ts, d), lambda b, h, i: (b, i, gb + h)),
                  pl.BlockSpec((1, d), lambda b, h, i: (0, h)),
                  pl.BlockSpec((1, d), lambda b, h, i: (0, 0))],
        out_specs=pl.BlockSpec((None, ts, d), lambda b, h, i: (b, i, h)),
        out_shape=jax.ShapeDtypeStruct((bsz, s, nh * d), F32),
        scratch_shapes=[pltpu.VMEM((ts, d), F32), pltpu.VMEM((ts, d), F32),
                        pltpu.VMEM((ts, d), F32), pltpu.VMEM((d, d), F32)],
        compiler_params=pltpu.CompilerParams(
            dimension_semantics=("parallel", "parallel", "arbitrary"), vmem_limit_bytes=VMEM_LIMIT),
        name="hgrn2",
    )(p32, p32, p32, p32, lb.astype(F32).reshape(1, nh * d), d_norm_g.astype(F32).reshape(1, d))


def _stickbreak_kernel(q_ref, k_ref, v_ref, o_ref, *, tq):
    i = pl.program_id(2)
    d = HEAD_DIM
    q = q_ref[...]
    row = _iota((tq, tq), 0)
    col = _iota((tq, tq), 1)
    causal = col < row
    later = (row > col).astype(BF16)

    def block(j, carry, acc, diag):
        r0 = pl.multiple_of(j * tq, tq)
        z = _mm_nt(q, k_ref[pl.ds(r0, tq), :]) * (d ** -0.5)
        sp = _softplus(z)
        l1m = jnp.where(causal, -sp, 0.0) if diag else -sp
        l_hi = l1m.astype(BF16)
        l_lo = (l1m - l_hi.astype(F32)).astype(BF16)
        rest = (jnp.dot(l_hi, later, preferred_element_type=F32)
                + jnp.dot(l_lo, later, preferred_element_type=F32))
        logw = (z - sp) + rest + carry
        p = jnp.exp(logw)
        if diag:
            p = jnp.where(causal, p, 0.0)
        acc = acc + _mm(p, v_ref[pl.ds(r0, tq), :])
        carry = carry + jnp.sum(l1m, axis=-1, keepdims=True)
        return carry, acc

    carry, acc = block(i, jnp.zeros((tq, 1), F32), jnp.zeros((tq, d), F32), True)

    def body(it, ca):
        return block(i - 1 - it, ca[0], ca[1], False)

    carry, acc = lax.fori_loop(0, i, body, (carry, acc))
    o_ref[...] = acc


def _stickbreak(p16, *, tq, cols):
    bsz, s, _ = p16.shape
    d = HEAD_DIM
    nh = N_HEADS
    qb, kb, vb = cols["qc"], cols["kc"], cols["vc"]
    kernel = functools.partial(_stickbreak_kernel, tq=tq)
    return pl.pallas_call(
        kernel,
        grid=(bsz, nh, s // tq),
        in_specs=[pl.BlockSpec((None, tq, d), lambda b, h, i: (b, i, qb + h)),
                  pl.BlockSpec((None, s, d), lambda b, h, i: (b, 0, kb + h)),
                  pl.BlockSpec((None, s, d), lambda b, h, i: (b, 0, vb + h))],
        out_specs=pl.BlockSpec((None, tq, d), lambda b, h, i: (b, i, h)),
        out_shape=jax.ShapeDtypeStruct((bsz, s, nh * d), F32),
        compiler_params=pltpu.CompilerParams(
            dimension_semantics=("parallel", "parallel", "arbitrary"), vmem_limit_bytes=VMEM_LIMIT),
        name="stickbreak",
    )(p16, p16, p16)


def _dsa_kernel(qi_ref, smq_ref, q_ref, sm_ref, k_ref, v_ref, bias_ref, o_ref,
                sc_ref, wb_ref, qc_ref, m_ref, l_ref, acc_ref, *, tq, k_sel, wi_lane, wide):
    i = pl.program_id(1)
    tk = tq
    d = HEAD_DIM
    nh = N_HEADS
    ksel = float(k_sel)
    per_wide = wide // tk
    n_wide = (i + per_wide) // per_wide

    smq = smq_ref[...]
    lane = _iota(smq.shape, 1)
    for hh in range(IDX_HEADS):
        qh = qi_ref[:, hh * IDX_DIM:(hh + 1) * IDX_DIM]
        hi = qh.astype(BF16)
        lo = (qh - hi.astype(F32)).astype(BF16)
        qc_ref[hh] = jnp.concatenate([hi, hi, lo], axis=-1)
        w = jnp.sum(jnp.where(lane == wi_lane + hh, smq, 0.0), axis=-1, keepdims=True)
        wb_ref[hh] = jnp.broadcast_to(w * ((IDX_HEADS ** -0.5) * (IDX_DIM ** -0.5)), (tq, tk))

    def score_block(j):
        r0 = pl.multiple_of(j * tk, tk)
        ki = sm_ref[pl.ds(r0, tk), :][:, :IDX_DIM]
        hi = ki.astype(BF16)
        lo = (ki - hi.astype(F32)).astype(BF16)
        kc = jnp.concatenate([hi, lo, hi], axis=-1)
        sc = jnp.zeros((tq, tk), F32)
        for hh in range(IDX_HEADS):
            s_h = lax.dot_general(qc_ref[hh], kc, (((1,), (1,)), ((), ())),
                                  preferred_element_type=F32)
            sc = sc + jnp.maximum(s_h, 0.0) * wb_ref[hh]
        return sc

    def score_body(j, mm):
        sc = score_block(j)
        sc_ref[:, pl.ds(pl.multiple_of(j * tk, tk), tk)] = sc
        return jnp.minimum(mm[0], sc), jnp.maximum(mm[1], sc)

    mn, mx = lax.fori_loop(0, i, score_body,
                           (jnp.full((tq, tk), jnp.inf, F32), jnp.full((tq, tk), -jnp.inf, F32)))
    row = _iota((tq, tk), 0)
    col = _iota((tq, tk), 1)
    adm = col < (row // CHUNK + 1) * CHUNK
    sc = score_block(i)
    d0 = pl.multiple_of(i * tk, tk)
    sc_ref[:, pl.ds(d0, tk)] = jnp.where(adm, sc, -jnp.inf)
    mn = jnp.minimum(mn, jnp.where(adm, sc, jnp.inf))
    mx = jnp.maximum(mx, jnp.where(adm, sc, -jnp.inf))
    for pb in range(1, per_wide):
        sc_ref[:, pl.ds(pl.multiple_of((i + pb) * tk, tk), tk)] = jnp.full((tq, tk), -jnp.inf, F32)
    rmin = jnp.min(mn, axis=-1, keepdims=True)
    rmax = jnp.max(mx, axis=-1, keepdims=True)

    def fold(x):
        acc = x[:, 0:tk]
        for pb in range(1, per_wide):
            acc = acc + x[:, pb * tk:(pb + 1) * tk]
        return acc

    def count(pred):
        def body(g, acc):
            g0 = pl.multiple_of(g * wide, wide)
            blk = sc_ref[:, pl.ds(g0, wide)]
            return acc + fold(pred(blk, g0))
        acc = lax.fori_loop(0, n_wide, body, jnp.zeros((tq, tk), F32))
        return jnp.sum(acc, axis=-1, keepdims=True)

    def max_below(x):
        def body(g, acc):
            blk = sc_ref[:, pl.ds(pl.multiple_of(g * wide, wide), wide)]
            mb = jnp.where(blk < x, blk, -jnp.inf)
            m = mb[:, 0:tk]
            for pb in range(1, per_wide):
                m = jnp.maximum(m, mb[:, pb * tk:(pb + 1) * tk])
            return jnp.maximum(acc, m)
        acc = lax.fori_loop(0, n_wide, body, jnp.full((tq, tk), -jnp.inf, F32))
        return jnp.max(acc, axis=-1, keepdims=True)

    rows1 = _iota((tq, 1), 0)
    n_adm = (i * tq + (rows1 // CHUNK + 1) * CHUNK).astype(F32)
    all_sel = n_adm <= ksel
    c_max = count(lambda blk, g0: _ind(blk >= rmax))
    done0 = jnp.where(all_sel, 1.0, _ind(c_max >= ksel))
    v0 = jnp.where(all_sel, -jnp.inf, rmax)

    def search_cond(c):
        return jnp.min(c[4]) < 0.5

    def search_body(c):
        rnd, lo, hi, v, done = c
        n_it = jnp.where(rnd == 0, 20, 6)

        def bis(_, lh):
            lo, hi = lh
            mid = 0.5 * lo + 0.5 * hi
            ge = count(lambda blk, g0: _ind(blk >= mid)) >= ksel
            return jnp.where(ge, mid, lo), jnp.where(ge, hi, mid)

        lo, hi = lax.fori_loop(0, n_it, bis, (lo, hi))
        cand = max_below(hi)
        ok = count(lambda blk, g0: _ind(blk >= cand)) >= ksel
        v = jnp.where(done > 0.5, v, cand)
        done = jnp.where(ok, 1.0, done)
        return rnd + 1, lo, hi, v, done

    _, _, _, vth, _ = lax.while_loop(search_cond, search_body,
                                     (jnp.int32(0), rmin, rmax, v0, done0))

    c_gt = count(lambda blk, g0: _ind(blk > vth))
    c_eq = count(lambda blk, g0: _ind(blk == vth))
    need = ksel - c_gt
    tie = jnp.where(all_sel, 0.0, _ind(c_eq > need))
    last_idx = float(sc_ref.shape[1])
    j_all = jnp.where(all_sel, -1.0, last_idx)

    def tie_search(_):
        def bis(_, lh):
            lo, hi = lh
            mid = jnp.floor(0.5 * (lo + hi))

            def pred(blk, g0):
                idx = (g0 + _iota(blk.shape, 1)).astype(F32)
                return jnp.where(blk == vth, _ind(idx <= mid), 0.0)

            ge = count(pred) >= need
            return jnp.where(ge, lo, mid), jnp.where(ge, mid, hi)

        n_steps = int(math.ceil(math.log2(sc_ref.shape[1] + 1))) + 1
        _, hi = lax.fori_loop(0, n_steps, bis,
                              (jnp.full((tq, 1), -1.0, F32), jnp.full((tq, 1), last_idx, F32)))
        return jnp.where(tie > 0.5, hi, j_all)

    jth = lax.cond(jnp.max(tie) > 0.5, tie_search, lambda _: j_all, 0)

    for hh in range(nh):
        m_ref[hh] = jnp.full((tq, d), NEG_BIG, F32)
        l_ref[hh] = jnp.zeros((tq, d), F32)
        acc_ref[hh] = jnp.zeros((tq, d), F32)

    def att_body(j, carry):
        r0 = pl.multiple_of(j * tk, tk)
        scb = sc_ref[:, pl.ds(r0, tk)]
        idx = (r0 + col).astype(F32)
        sel = jnp.where(scb == vth, _ind(idx <= jth), _ind(scb > vth)) > 0.5
        bidx = jnp.minimum(i - j, 2)
        for hh in range(nh):
            qh = q_ref[:, hh * d:(hh + 1) * d]
            kh = k_ref[pl.ds(r0, tk), hh * d:(hh + 1) * d]
            vh = v_ref[pl.ds(r0, tk), hh * d:(hh + 1) * d]
            logits = lax.dot_general(qh, kh, (((1,), (1,)), ((), ())),
                                     preferred_element_type=F32) * (d ** -0.5) + bias_ref[bidx, hh]
            m_old = m_ref[hh]
            m_blk = jnp.max(jnp.where(sel, logits, NEG_BIG), axis=-1, keepdims=True)
            m_new = jnp.maximum(m_old, m_blk)
            p = jnp.where(sel, jnp.exp(logits - m_new[:, :tk]), 0.0)
            alpha = jnp.exp(m_old - m_new)
            l_ref[hh] = alpha * l_ref[hh] + jnp.sum(p, axis=-1, keepdims=True)
            acc_ref[hh] = alpha * acc_ref[hh] + jnp.dot(p.astype(BF16), vh, preferred_element_type=F32)
            m_ref[hh] = m_new
        return carry

    lax.fori_loop(0, i + 1, att_body, 0)
    for hh in range(nh):
        o_ref[:, hh * d:(hh + 1) * d] = acc_ref[hh] / l_ref[hh]


def _dsa(p32, p16, bias_tiles, *, tq, cols):
    bsz, s, _ = p32.shape
    d = HEAD_DIM
    nh = N_HEADS
    wide = 4 * tq
    k_sel = min(TOPK_MAX, s // 4)
    w512 = nh * d
    kernel = functools.partial(_dsa_kernel, tq=tq, k_sel=k_sel, wi_lane=cols["wi_lane"], wide=wide)
    resident = dict(pipeline_mode=pl.Buffered(1))
    return pl.pallas_call(
        kernel,
        grid=(bsz, s // tq),
        in_specs=[pl.BlockSpec((None, tq, w512), lambda b, i: (b, i, cols["qi"] // nh)),
                  pl.BlockSpec((None, tq, d), lambda b, i: (b, i, cols["small"])),
                  pl.BlockSpec((None, tq, w512), lambda b, i: (b, i, cols["qb"] // nh)),
                  pl.BlockSpec((None, s, d), lambda b, i: (b, 0, cols["small"]), **resident),
                  pl.BlockSpec((None, s, w512), lambda b, i: (b, 0, cols["kb"] // nh), **resident),
                  pl.BlockSpec((None, s, w512), lambda b, i: (b, 0, cols["vb"] // nh), **resident),
                  pl.BlockSpec((3, nh, tq, tq), lambda b, i: (0, 0, 0, 0), **resident)],
        out_specs=pl.BlockSpec((None, tq, w512), lambda b, i: (b, i, 0)),
        out_shape=jax.ShapeDtypeStruct((bsz, s, w512), F32),
        scratch_shapes=[pltpu.VMEM((tq, s + wide - tq), F32),
                        pltpu.VMEM((IDX_HEADS, tq, tq), F32),
                        pltpu.VMEM((IDX_HEADS, tq, 3 * IDX_DIM), BF16),
                        pltpu.VMEM((nh, tq, d), F32), pltpu.VMEM((nh, tq, d), F32),
                        pltpu.VMEM((nh, tq, d), F32)],
        compiler_params=pltpu.CompilerParams(
            dimension_semantics=("parallel", "arbitrary"), vmem_limit_bytes=VMEM_LIMIT),
        name="dsa",
    )(p32, p32, p16, p32, p16, p16, bias_tiles)


def _t5_bucket(rel):
    nb = REL_BUCKETS // 2
    max_exact = nb // 2
    ret = jnp.where(rel > 0, nb, 0)
    n = jnp.abs(rel)
    large = max_exact + (jnp.log(jnp.maximum(n, 1).astype(F32) / max_exact)
                         / math.log(REL_MAX_DIST / max_exact) * (nb - max_exact)).astype(jnp.int32)
    large = jnp.minimum(large, nb - 1)
    return ret + jnp.where(n < max_exact, n, large)


def _bias_tiles(rel_table, tq):
    assert tq >= REL_MAX_DIST
    t = jnp.arange(tq)
    tiles = []
    for back in range(3):
        rel = (t[None, :] - back * tq) - t[:, None]
        tiles.append(rel_table.astype(F32)[_t5_bucket(rel)].transpose(2, 0, 1))
    return jnp.stack(tiles)


def _even_layout(w_in):
    d = HEAD_DIM
    a_w = 2 * N_HEADS * d + N_HEADS * d
    offs = {}
    o = 0
    for name, w in (("qkv", a_w), ("z", N_HEADS * d), ("a", N_HEADS), ("b", N_HEADS),
                    ("qb", N_HEADS * d), ("kb", N_HEADS * d), ("vb", N_HEADS * d),
                    ("qi", IDX_HEADS * IDX_DIM), ("ki", IDX_DIM), ("wi", IDX_HEADS)):
        offs[name] = (o, o + w)
        o += w
    assert o == w_in.shape[1]
    sl = lambda n: w_in[:, offs[n][0]:offs[n][1]]
    small_w = IDX_DIM + 2 * N_HEADS + IDX_HEADS
    small_pad = -small_w % d
    w = jnp.concatenate([sl("qkv"), sl("z"), sl("qb"), sl("kb"), sl("vb"), sl("qi"),
                         sl("ki"), sl("a"), sl("b"), sl("wi"),
                         jnp.zeros((w_in.shape[0], small_pad), w_in.dtype)], axis=1)
    nh = N_HEADS
    cols = dict(qa=0, ka=nh, va=2 * nh, za=3 * nh, qb=4 * nh, kb=5 * nh, vb=6 * nh, qi=7 * nh,
                small=8 * nh, a_lane=IDX_DIM, b_lane=IDX_DIM + nh, wi_lane=IDX_DIM + 2 * nh)
    return w.astype(BF16), cols


def kernel(x, norm_g, w_in_even, conv_w_even, a_log_even, dt_bias_even, a_norm_even, w_out_even,
           rel_bias, w_in_odd, lb_logits, d_norm_odd, w_out_odd, w_gate, w_up, w_down):
    bsz, s, d = x.shape
    t = bsz * s
    depth = norm_g.shape[0]
    nh = N_HEADS
    tq = 128
    lb_all = jnp.cumsum(jax.nn.softmax(lb_logits.astype(F32), axis=0), axis=0)
    lb_all = lb_all - lb_all[:1]
    odd_cols = dict(qc=0, kc=nh, vc=2 * nh, qd=3 * nh, fd=4 * nh, id=5 * nh, gd=6 * nh)
    bias_tiles = _bias_tiles(rel_bias, tq)

    h = x.reshape(t, d)
    for l in range(depth):
        if l % 2 == 0:
            e = l // 2
            w_even, cols = _even_layout(w_in_even[e])
            p32, p16 = _norm_matmul(h, norm_g[l, 0], w_even, tm=512, tn=w_even.shape[1] // 3)
            p32 = p32.reshape(bsz, s, -1)
            p16 = p16.reshape(bsz, s, -1)
            o_1 = _deltanet(p32, conv_w_even[e], a_log_even[e], dt_bias_even[e], a_norm_even[e],
                            ts=min(512, s), cols=cols)
            o_2 = _dsa(p32, p16, bias_tiles, tq=tq, cols=cols)
            w_out = w_out_even[e]
        else:
            o = l // 2
            p32, p16 = _norm_matmul(h, norm_g[l, 0], w_in_odd[o].astype(BF16), tm=512, tn=512)
            p32 = p32.reshape(bsz, s, -1)
            p16 = p16.reshape(bsz, s, -1)
            o_1 = _stickbreak(p16, tq=tq, cols=odd_cols)
            o_2 = _hgrn2(p32, lb_all[l], d_norm_odd[o], ts=min(512, s), cols=odd_cols)
            w_out = w_out_odd[o]
        h = _outproj(o_1.reshape(t, -1), o_2.reshape(t, -1), w_out, h, norm_g[l, 1], tm=512)
        h = _ffn(h, norm_g[l, 2], norm_g[l, 3], w_gate[l], w_up[l], w_down[l], tm=1024, tf=256)
    return h.reshape(bsz, s, d)
```

```python
import functools
import math

import jax
import jax.numpy as jnp
from jax import lax
from jax.experimental import pallas as pl
from jax.experimental.pallas import tpu as pltpu

F32 = jnp.float32
BF16 = jnp.bfloat16
HIGHEST = lax.Precision.HIGHEST

CHUNK = 64
HEAD_DIM = 128
N_HEADS = 4
IDX_HEADS = 8
IDX_DIM = 64
TOPK_MAX = 256
CONV_WIDTH = 4
REL_BUCKETS = 32
REL_MAX_DIST = 128
EPS = 1e-6
NEG_BIG = -1e30
EXP_ZERO_BELOW = -104.0
VMEM_LIMIT = 56 * 1024 * 1024


def _mm(a, b):
    return jnp.dot(a.astype(BF16), b.astype(BF16), preferred_element_type=F32)


def _mm_nt(a, b):
    return lax.dot_general(a.astype(BF16), b.astype(BF16), (((1,), (1,)), ((), ())),
                           preferred_element_type=F32)


def _mm_tn(a, b):
    return lax.dot_general(a.astype(BF16), b.astype(BF16), (((0,), (0,)), ((), ())),
                           preferred_element_type=F32)


def _mm_f32(a, b):
    return jnp.dot(a, b, precision=HIGHEST, preferred_element_type=F32)


def _sigmoid(x):
    return 1.0 / (1.0 + jnp.exp(-x))


def _silu(x):
    return x * _sigmoid(x)


def _softplus(x):
    return jnp.maximum(x, 0.0) + jnp.log1p(jnp.exp(-jnp.abs(x)))


def _rms(x, g):
    return x * lax.rsqrt(jnp.mean(x * x, axis=-1, keepdims=True) + EPS) * g


def _iota(shape, dim):
    return lax.broadcasted_iota(jnp.int32, shape, dim)


def _ind(mask):
    return jnp.where(mask, 1.0, 0.0)


def _norm_matmul_kernel(x_ref, g_ref, w_ref, o32_ref, o16_ref, xn_ref):
    @pl.when(pl.program_id(1) == 0)
    def _():
        xn_ref[...] = _rms(x_ref[...], g_ref[...]).astype(BF16)

    y = jnp.dot(xn_ref[...], w_ref[...], preferred_element_type=F32)
    o32_ref[...] = y
    o16_ref[...] = y.astype(BF16)


def _norm_matmul(x, g, w, *, tm, tn):
    t, d = x.shape
    n = w.shape[1]
    return pl.pallas_call(
        _norm_matmul_kernel,
        grid=(t // tm, n // tn),
        in_specs=[pl.BlockSpec((tm, d), lambda i, j: (i, 0)),
                  pl.BlockSpec((1, d), lambda i, j: (0, 0)),
                  pl.BlockSpec((d, tn), lambda i, j: (0, j))],
        out_specs=[pl.BlockSpec((tm, tn), lambda i, j: (i, j)),
                   pl.BlockSpec((tm, tn), lambda i, j: (i, j))],
        out_shape=[jax.ShapeDtypeStruct((t, n), F32), jax.ShapeDtypeStruct((t, n), BF16)],
        scratch_shapes=[pltpu.VMEM((tm, d), BF16)],
        compiler_params=pltpu.CompilerParams(
            dimension_semantics=("parallel", "arbitrary"), vmem_limit_bytes=VMEM_LIMIT),
        name="norm_matmul",
    )(x, g.reshape(1, d), w)


def _outproj_kernel(ca_ref, cb_ref, wa_ref, wb_ref, h_ref, g_ref, o_ref):
    y = (jnp.dot(ca_ref[...].astype(BF16), wa_ref[...], preferred_element_type=F32)
         + jnp.dot(cb_ref[...].astype(BF16), wb_ref[...], preferred_element_type=F32))
    o_ref[...] = h_ref[...] + _rms(y, g_ref[...])


def _outproj(ca, cb, w, h, g, *, tm):
    t, d = h.shape
    wa_n = ca.shape[1]
    wb_n = cb.shape[1]
    wa = w[:wa_n].astype(BF16)
    wb = w[wa_n:].astype(BF16)
    return pl.pallas_call(
        _outproj_kernel,
        grid=(t // tm,),
        in_specs=[pl.BlockSpec((tm, wa_n), lambda i: (i, 0)),
                  pl.BlockSpec((tm, wb_n), lambda i: (i, 0)),
                  pl.BlockSpec((wa_n, d), lambda i: (0, 0)),
                  pl.BlockSpec((wb_n, d), lambda i: (0, 0)),
                  pl.BlockSpec((tm, d), lambda i: (i, 0)),
                  pl.BlockSpec((1, d), lambda i: (0, 0))],
        out_specs=pl.BlockSpec((tm, d), lambda i: (i, 0)),
        out_shape=jax.ShapeDtypeStruct((t, d), F32),
        compiler_params=pltpu.CompilerParams(
            dimension_semantics=("parallel",), vmem_limit_bytes=VMEM_LIMIT),
        name="outproj",
    )(ca, cb, wa, wb, h, g.reshape(1, d))


def _ffn_kernel(h_ref, gpre_ref, gpost_ref, wg_ref, wu_ref, wd_ref, o_ref, xn_ref, acc_ref):
    f = pl.program_id(1)

    @pl.when(f == 0)
    def _():
        xn_ref[...] = _rms(h_ref[...], gpre_ref[...]).astype(BF16)
        acc_ref[...] = jnp.zeros_like(acc_ref)

    xn = xn_ref[...]
    gate = jnp.dot(xn, wg_ref[...], preferred_element_type=F32)
    up = jnp.dot(xn, wu_ref[...], preferred_element_type=F32)
    act = (_silu(gate) * up).astype(BF16)
    acc_ref[...] += jnp.dot(act, wd_ref[...], preferred_element_type=F32)

    @pl.when(f == pl.num_programs(1) - 1)
    def _():
        o_ref[...] = h_ref[...] + _rms(acc_ref[...], gpost_ref[...])


def _ffn(h, g_pre, g_post, wg, wu, wd, *, tm, tf):
    t, d = h.shape
    ff = wg.shape[1]
    return pl.pallas_call(
        _ffn_kernel,
        grid=(t // tm, ff // tf),
        in_specs=[pl.BlockSpec((tm, d), lambda i, f: (i, 0)),
                  pl.BlockSpec((1, d), lambda i, f: (0, 0)),
                  pl.BlockSpec((1, d), lambda i, f: (0, 0)),
                  pl.BlockSpec((d, tf), lambda i, f: (0, f)),
                  pl.BlockSpec((d, tf), lambda i, f: (0, f)),
                  pl.BlockSpec((tf, d), lambda i, f: (f, 0))],
        out_specs=pl.BlockSpec((tm, d), lambda i, f: (i, 0)),
        out_shape=jax.ShapeDtypeStruct((t, d), F32),
        scratch_shapes=[pltpu.VMEM((tm, d), BF16), pltpu.VMEM((tm, d), F32)],
        compiler_params=pltpu.CompilerParams(
            dimension_semantics=("parallel", "arbitrary"), vmem_limit_bytes=VMEM_LIMIT),
        name="ffn",
    )(h, g_pre.reshape(1, d), g_post.reshape(1, d),
      wg.astype(BF16), wu.astype(BF16), wd.astype(BF16))


def _deltanet_kernel(xq_ref, xk_ref, xv_ref, z_ref, sm_ref, cwq_ref, cwk_ref, cwv_ref,
                     alog_ref, dtb_ref, gn_ref, o_ref,
                     xpad_ref, q_ref, k_ref, v_ref, gb_ref, bb_ref, u_ref, w_ref, qk_ref, st_ref,
                     *, ts, a_col, b_col):
    h = pl.program_id(1)
    s = pl.program_id(2)
    c = CHUNK
    d = HEAD_DIM

    @pl.when(s == 0)
    def _():
        xpad_ref[:, 0:8, :] = jnp.zeros((3, 8, d), F32)
        st_ref[...] = jnp.zeros_like(st_ref)

    @pl.when(s != 0)
    def _():
        xpad_ref[:, 0:8, :] = xpad_ref[:, ts:ts + 8, :]

    xpad_ref[0, 8:ts + 8, :] = xq_ref[...]
    xpad_ref[1, 8:ts + 8, :] = xk_ref[...]
    xpad_ref[2, 8:ts + 8, :] = xv_ref[...]

    def conv_silu(idx, cw_ref):
        cw = cw_ref[...]
        acc = xpad_ref[idx, 8 - (CONV_WIDTH - 1):8 - (CONV_WIDTH - 1) + ts, :] * cw[0:1, :]
        for j in range(1, CONV_WIDTH):
            off = 8 - (CONV_WIDTH - 1) + j
            acc = acc + xpad_ref[idx, off:off + ts, :] * cw[j:j + 1, :]
        return _silu(acc)

    def l2norm(t):
        return t * lax.rsqrt(jnp.sum(t * t, axis=-1, keepdims=True) + EPS)

    q_ref[...] = l2norm(conv_silu(0, cwq_ref)) * (d ** -0.5)
    k_ref[...] = l2norm(conv_silu(1, cwk_ref))
    v_ref[...] = conv_silu(2, cwv_ref)

    sm = sm_ref[...]
    lane = _iota(sm.shape, 1)
    a_raw = jnp.sum(jnp.where(lane == a_col + h, sm, 0.0), axis=-1, keepdims=True)
    b_raw = jnp.sum(jnp.where(lane == b_col + h, sm, 0.0), axis=-1, keepdims=True)
    hl = _iota((1, d), 1)
    a_log = jnp.sum(jnp.where(hl == h, alog_ref[...], 0.0), axis=-1, keepdims=True)
    dtb = jnp.sum(jnp.where(hl == h, dtb_ref[...], 0.0), axis=-1, keepdims=True)
    g = -jnp.exp(a_log) * _softplus(a_raw + dtb)
    gb_ref[...] = jnp.broadcast_to(g, (ts, d))
    bb_ref[...] = jnp.broadcast_to(_sigmoid(b_raw), (ts, d))

    row = _iota((c, c), 0)
    col = _iota((c, c), 1)
    tri = (col <= row)
    strict = (col < row)
    tri_f = tri.astype(F32)
    upper_f = (row <= col).astype(F32)
    eye = (row == col).astype(F32)
    ones_cc = jnp.ones((c, c), F32)
    gnorm = gn_ref[...]

    for ci in range(ts // c):
        rs = slice(ci * c, (ci + 1) * c)
        q = q_ref[rs, :]
        k = k_ref[rs, :]
        v = v_ref[rs, :]
        gb = gb_ref[rs, :]
        beta = bb_ref[rs, :]
        gc = _mm_f32(tri_f, gb)
        gc_row = _mm_f32(ones_cc, gb[:, :c] * upper_f)
        decay = jnp.where(tri, jnp.exp(jnp.minimum(gc[:, :c] - gc_row, 0.0)), 0.0)
        kk = _mm_nt(k, k)
        a_mat = jnp.where(strict, beta[:, :c] * kk * decay, 0.0)
        n = -a_mat
        inv = eye + n
        for _ in range(5):
            n = _mm_f32(n, n)
            inv = inv + _mm_f32(inv, n)
        egc = jnp.exp(gc)
        gl = gc[c - 1:c, :]
        u_ref[rs, :] = _mm_f32(inv, v * beta)
        w_ref[rs, :] = _mm_f32(inv, k * (beta * egc))
        qk_ref[rs, :] = _mm_nt(q, k) * decay
        q_ref[rs, :] = q * egc
        k_ref[rs, :] = k * jnp.exp(gl - gc)
        gb_ref[rs, :] = jnp.broadcast_to(jnp.exp(gl), (c, d))

    def chunk_body(ci, carry):
        r0 = pl.multiple_of(ci * c, c)
        st = st_ref[...]
        v_new = u_ref[pl.ds(r0, c), :] - _mm(w_ref[pl.ds(r0, c), :], st)
        o = _mm(q_ref[pl.ds(r0, c), :], st) + _mm(qk_ref[pl.ds(r0, c), :], v_new)
        st_ref[...] = st * gb_ref[pl.ds(r0, 1), :] + _mm_tn(k_ref[pl.ds(r0, c), :], v_new)
        zc = z_ref[pl.ds(r0, c), :]
        o_ref[pl.ds(r0, c), :] = _rms(o, gnorm) * _silu(zc)
        return carry

    lax.fori_loop(0, ts // c, chunk_body, 0)


def _deltanet(p32, conv_w, a_log, dt_bias, a_norm_g, *, ts, cols):
    bsz, s, _ = p32.shape
    d = HEAD_DIM
    nh = N_HEADS
    qb, kb, vb, zb, smb = cols["qa"], cols["ka"], cols["va"], cols["za"], cols["small"]
    pad = lambda t: jnp.pad(t.astype(F32), (0, d - t.shape[0])).reshape(1, d)
    kernel = functools.partial(_deltanet_kernel, ts=ts, a_col=cols["a_lane"], b_col=cols["b_lane"])
    return pl.pallas_call(
        kernel,
        grid=(bsz, nh, s // ts),
        in_specs=[pl.BlockSpec((None, ts, d), lambda b, h, i: (b, i, qb + h)),
                  pl.BlockSpec((None, ts, d), lambda b, h, i: (b, i, kb + h)),
                  pl.BlockSpec((None, ts, d), lambda b, h, i: (b, i, vb + h)),
                  pl.BlockSpec((None, ts, d), lambda b, h, i: (b, i, zb + h)),
                  pl.BlockSpec((None, ts, d), lambda b, h, i: (b, i, smb)),
                  pl.BlockSpec((CONV_WIDTH, d), lambda b, h, i: (0, h)),
                  pl.BlockSpec((CONV_WIDTH, d), lambda b, h, i: (0, nh + h)),
                  pl.BlockSpec((CONV_WIDTH, d), lambda b, h, i: (0, 2 * nh + h)),
                  pl.BlockSpec((1, d), lambda b, h, i: (0, 0)),
                  pl.BlockSpec((1, d), lambda b, h, i: (0, 0)),
                  pl.BlockSpec((1, d), lambda b, h, i: (0, 0))],
        out_specs=pl.BlockSpec((None, ts, d), lambda b, h, i: (b, i, h)),
        out_shape=jax.ShapeDtypeStruct((bsz, s, nh * d), F32),
        scratch_shapes=[pltpu.VMEM((3, ts + 8, d), F32),
                        pltpu.VMEM((ts, d), F32), pltpu.VMEM((ts, d), F32), pltpu.VMEM((ts, d), F32),
                        pltpu.VMEM((ts, d), F32), pltpu.VMEM((ts, d), F32),
                        pltpu.VMEM((ts, d), F32), pltpu.VMEM((ts, d), F32),
                        pltpu.VMEM((ts, CHUNK), F32),
                        pltpu.VMEM((d, d), F32)],
        compiler_params=pltpu.CompilerParams(
            dimension_semantics=("parallel", "parallel", "arbitrary"), vmem_limit_bytes=VMEM_LIMIT),
        name="deltanet",
    )(p32, p32, p32, p32, p32, conv_w.astype(F32), conv_w.astype(F32), conv_w.astype(F32),
      pad(a_log), pad(dt_bias), a_norm_g.astype(F32).reshape(1, d))


def _hgrn2_kernel(q_ref, f_ref, i_ref, gate_ref, lb_ref, gn_ref, o_ref,
                  qs_ref, ks_ref, gc_ref, st_ref, *, ts):
    s = pl.program_id(2)
    c = CHUNK
    d = HEAD_DIM

    @pl.when(s == 0)
    def _():
        st_ref[...] = jnp.zeros_like(st_ref)

    lb = lb_ref[...]
    f_raw = f_ref[...]
    log_sig = jnp.minimum(f_raw, 0.0) - jnp.log1p(jnp.exp(-jnp.abs(f_raw)))
    la = jnp.log(lb)
    lbb = jnp.log1p(-lb) + log_sig
    log_f = jnp.maximum(la, lbb) + jnp.log1p(jnp.exp(-jnp.abs(la - lbb)))
    qs_ref[...] = _silu(q_ref[...])
    ks_ref[...] = (1.0 - lb) * _sigmoid(-f_raw)

    row = _iota((c, c), 0)
    col = _iota((c, c), 1)
    tri_f = (col <= row).astype(F32)
    ones_dd = jnp.ones((d, d), BF16)
    rows_8d = _iota((8, d), 0)
    gnorm = gn_ref[...]

    for ci in range(ts // c):
        gc_ref[ci * c:(ci + 1) * c, :] = _mm_f32(tri_f, log_f[ci * c:(ci + 1) * c, :])

    def chunk_loop(ci, carry):
        r0 = pl.multiple_of(ci * c, c)
        q = qs_ref[pl.ds(r0, c), :]
        k = ks_ref[pl.ds(r0, c), :]
        v = i_ref[pl.ds(r0, c), :]
        gc = gc_ref[pl.ds(r0, c), :]

        groups = [jnp.zeros((8, d), F32) for _ in range(c // 8)]
        for j in range(c):
            g0 = j // 8
            lo = g0 * 8
            k_j = ks_ref[pl.ds(r0 + j, 1), :]
            g_j = gc_ref[pl.ds(r0 + j, 1), :]
            v_j = i_ref[pl.ds(r0 + j, 1), :]
            e = jnp.exp(jnp.minimum(gc[lo:, :] - g_j, 0.0))
            if j % 8:
                head = jnp.where(rows_8d >= j - lo, e[:8], 0.0)
                e = jnp.concatenate([head, e[8:]], axis=0) if lo + 8 < c else head
            p = q[lo:, :] * k_j * e
            a = jnp.dot(p.astype(BF16), ones_dd, preferred_element_type=F32)
            av = a * v_j
            for g in range(g0, c // 8):
                groups[g] = groups[g] + av[(g - g0) * 8:(g - g0 + 1) * 8, :]
        o_intra = jnp.concatenate(groups, axis=0)

        st = st_ref[...]
        gl = gc[c - 1:c, :]
        o = o_intra + _mm_nt(q * jnp.exp(gc), st)
        st_ref[...] = st * jnp.exp(gl) + _mm_tn(v, k * jnp.exp(gl - gc))
        o_ref[pl.ds(r0, c), :] = _rms(o, gnorm) * _silu(gate_ref[pl.ds(r0, c), :])
        return carry

    lax.fori_loop(0, ts // c, chunk_loop, 0)


def _hgrn2(p32, lb, d_norm_g, *, ts, cols):
    bsz, s, _ = p32.shape
    d = HEAD_DIM
    nh = N_HEADS
    qb, fb, ib, gb = cols["qd"], cols["fd"], cols["id"], cols["gd"]
    kernel = functools.partial(_hgrn2_kernel, ts=ts)
    return pl.pallas_call(
        kernel,
        grid=(bsz, nh, s // ts),
        in_specs=[pl.BlockSpec((None, ts, d), lambda b, h, i: (b, i, qb + h)),
                  pl.BlockSpec((None, ts, d), lambda b, h, i: (b, i, fb + h)),
                  pl.BlockSpec((None, ts, d), lambda b, h, i: (b, i, ib + h)),
                  pl.BlockSpec((None, ts, d), lambda b, h, i: (b, i, gb + h)),
                  pl.BlockSpec((1, d), lambda b, h, i: (0, h)),
                  pl.BlockSpec((1, d), lambda b, h, i: (0, 0))],
        out_specs=pl.BlockSpec((None, ts, d), lambda b, h, i: (b, i, h)),
        out_shape=jax.ShapeDtypeStruct((bsz, s, nh * d), F32),
        scratch_shapes=[pltpu.VMEM((ts, d), F32), pltpu.VMEM((ts, d), F32),
                        pltpu.VMEM((ts, d), F32), pltpu.VMEM((d, d), F32)],
        compiler_params=pltpu.CompilerParams(
            dimension_semantics=("parallel", "parallel", "arbitrary"), vmem_limit_bytes=VMEM_LIMIT),
        name="hgrn2",
    )(p32, p32, p32, p32, lb.astype(F32).reshape(1, nh * d), d_norm_g.astype(F32).reshape(1, d))


def _stickbreak_kernel(q_ref, k_ref, v_ref, o_ref, *, tq):
    i = pl.program_id(2)
    d = HEAD_DIM
    q = q_ref[...]
    row = _iota((tq, tq), 0)
    col = _iota((tq, tq), 1)
    causal = col < row
    later = (row > col).astype(BF16)

    def block(j, carry, acc, diag):
        r0 = pl.multiple_of(j * tq, tq)
        z = _mm_nt(q, k_ref[pl.ds(r0, tq), :]) * (d ** -0.5)
        sp = _softplus(z)
        l1m = jnp.where(causal, -sp, 0.0) if diag else -sp
        l_hi = l1m.astype(BF16)
        l_lo = (l1m - l_hi.astype(F32)).astype(BF16)
        rest = (jnp.dot(l_hi, later, preferred_element_type=F32)
                + jnp.dot(l_lo, later, preferred_element_type=F32))
        logw = (z - sp) + rest + carry
        p = jnp.exp(logw)
        if diag:
            p = jnp.where(causal, p, 0.0)
        acc = acc + _mm(p, v_ref[pl.ds(r0, tq), :])
        carry = carry + jnp.sum(l1m, axis=-1, keepdims=True)
        return carry, acc

    carry, acc = block(i, jnp.zeros((tq, 1), F32), jnp.zeros((tq, d), F32), True)

    def cond(c):
        return jnp.logical_and(c[0] >= 0, jnp.max(c[1]) >= EXP_ZERO_BELOW)

    def body(c):
        carry, acc = block(c[0], c[1], c[2], False)
        return c[0] - 1, carry, acc

    _, _, acc = lax.while_loop(cond, body, (i - 1, carry, acc))
    o_ref[...] = acc


def _stickbreak(p16, *, tq, cols):
    bsz, s, _ = p16.shape
    d = HEAD_DIM
    nh = N_HEADS
    qb, kb, vb = cols["qc"], cols["kc"], cols["vc"]
    kernel = functools.partial(_stickbreak_kernel, tq=tq)
    return pl.pallas_call(
        kernel,
        grid=(bsz, nh, s // tq),
        in_specs=[pl.BlockSpec((None, tq, d), lambda b, h, i: (b, i, qb + h)),
                  pl.BlockSpec((None, s, d), lambda b, h, i: (b, 0, kb + h)),
                  pl.BlockSpec((None, s, d), lambda b, h, i: (b, 0, vb + h))],
        out_specs=pl.BlockSpec((None, tq, d), lambda b, h, i: (b, i, h)),
        out_shape=jax.ShapeDtypeStruct((bsz, s, nh * d), F32),
        compiler_params=pltpu.CompilerParams(
            dimension_semantics=("parallel", "parallel", "arbitrary"), vmem_limit_bytes=VMEM_LIMIT),
        name="stickbreak",
    )(p16, p16, p16)


def _dsa_kernel(qi_ref, smq_ref, q_ref, sm_ref, k_ref, v_ref, bias_ref, o_ref,
                sc_ref, wb_ref, qc_ref, m_ref, l_ref, acc_ref, *, tq, k_sel, wi_lane, wide):
    i = pl.program_id(1)
    tk = tq
    d = HEAD_DIM
    nh = N_HEADS
    ksel = float(k_sel)
    per_wide = wide // tk
    n_wide = (i + per_wide) // per_wide

    smq = smq_ref[...]
    lane = _iota(smq.shape, 1)
    for hh in range(IDX_HEADS):
        qh = qi_ref[:, hh * IDX_DIM:(hh + 1) * IDX_DIM]
        hi = qh.astype(BF16)
        lo = (qh - hi.astype(F32)).astype(BF16)
        qc_ref[hh] = jnp.concatenate([hi, hi, lo], axis=-1)
        w = jnp.sum(jnp.where(lane == wi_lane + hh, smq, 0.0), axis=-1, keepdims=True)
        wb_ref[hh] = jnp.broadcast_to(w * ((IDX_HEADS ** -0.5) * (IDX_DIM ** -0.5)), (tq, tk))

    def score_group(g):
        g0 = pl.multiple_of(g * wide, wide)
        ki = sm_ref[pl.ds(g0, wide), :][:, :IDX_DIM]
        hi = ki.astype(BF16)
        lo = (ki - hi.astype(F32)).astype(BF16)
        kc = jnp.concatenate([hi, lo, hi], axis=-1)
        sc = jnp.zeros((tq, wide), F32)
        for hh in range(IDX_HEADS):
            s_h = lax.dot_general(qc_ref[hh], kc, (((1,), (1,)), ((), ())),
                                  preferred_element_type=F32)
            sc = sc + jnp.maximum(s_h, 0.0) * wb_ref[hh][:, :1]
        return sc

    def fold(x, op=jnp.add):
        acc = x[:, 0:tk]
        for pb in range(1, per_wide):
            acc = op(acc, x[:, pb * tk:(pb + 1) * tk])
        return acc

    def score_body(g, mm):
        sc = score_group(g)
        sc_ref[:, pl.ds(pl.multiple_of(g * wide, wide), wide)] = sc
        return jnp.minimum(mm[0], fold(sc, jnp.minimum)), jnp.maximum(mm[1], fold(sc, jnp.maximum))

    mn, mx = lax.fori_loop(0, n_wide - 1, score_body,
                           (jnp.full((tq, tk), jnp.inf, F32), jnp.full((tq, tk), -jnp.inf, F32)))
    row = _iota((tq, tk), 0)
    col = _iota((tq, tk), 1)
    g_last = pl.multiple_of((n_wide - 1) * wide, wide)
    limit = i * tq + (_iota((tq, wide), 0) // CHUNK + 1) * CHUNK
    adm = (g_last + _iota((tq, wide), 1)) < limit
    sc = score_group(n_wide - 1)
    sc_ref[:, pl.ds(g_last, wide)] = jnp.where(adm, sc, -jnp.inf)
    mn = jnp.minimum(mn, fold(jnp.where(adm, sc, jnp.inf), jnp.minimum))
    mx = jnp.maximum(mx, fold(jnp.where(adm, sc, -jnp.inf), jnp.maximum))
    rmin = jnp.min(mn, axis=-1, keepdims=True)
    rmax = jnp.max(mx, axis=-1, keepdims=True)


    def count(pred):
        def body(g, acc):
            g0 = pl.multiple_of(g * wide, wide)
            blk = sc_ref[:, pl.ds(g0, wide)]
            return acc + fold(pred(blk, g0))
        acc = lax.fori_loop(0, n_wide, body, jnp.zeros((tq, tk), F32))
        return jnp.sum(acc, axis=-1, keepdims=True)

    def max_below(x):
        def body(g, acc):
            blk = sc_ref[:, pl.ds(pl.multiple_of(g * wide, wide), wide)]
            mb = jnp.where(blk < x, blk, -jnp.inf)
            m = mb[:, 0:tk]
            for pb in range(1, per_wide):
                m = jnp.maximum(m, mb[:, pb * tk:(pb + 1) * tk])
            return jnp.maximum(acc, m)
        acc = lax.fori_loop(0, n_wide, body, jnp.full((tq, tk), -jnp.inf, F32))
        return jnp.max(acc, axis=-1, keepdims=True)

    rows1 = _iota((tq, 1), 0)
    n_adm = (i * tq + (rows1 // CHUNK + 1) * CHUNK).astype(F32)
    all_sel = n_adm <= ksel
    c_max = count(lambda blk, g0: _ind(blk >= rmax))
    done0 = jnp.where(all_sel, 1.0, _ind(c_max >= ksel))
    v0 = jnp.where(all_sel, -jnp.inf, rmax)

    def search_cond(c):
        return jnp.min(c[4]) < 0.5

    def search_body(c):
        rnd, lo, hi, v, done = c
        n_it = jnp.where(rnd == 0, 20, 6)

        def bis(_, lh):
            lo, hi = lh
            mid = 0.5 * lo + 0.5 * hi
            ge = count(lambda blk, g0: _ind(blk >= mid)) >= ksel
            return jnp.where(ge, mid, lo), jnp.where(ge, hi, mid)

        lo, hi = lax.fori_loop(0, n_it, bis, (lo, hi))
        cand = max_below(hi)
        ok = count(lambda blk, g0: _ind(blk >= cand)) >= ksel
        v = jnp.where(done > 0.5, v, cand)
        done = jnp.where(ok, 1.0, done)
        return rnd + 1, lo, hi, v, done

    _, _, _, vth, _ = lax.while_loop(search_cond, search_body,
                                     (jnp.int32(0), rmin, rmax, v0, done0))

    c_gt = count(lambda blk, g0: _ind(blk > vth))
    c_eq = count(lambda blk, g0: _ind(blk == vth))
    need = ksel - c_gt
    tie = jnp.where(all_sel, 0.0, _ind(c_eq > need))
    last_idx = float(sc_ref.shape[1])
    j_all = jnp.where(all_sel, -1.0, last_idx)

    def tie_search(_):
        def bis(_, lh):
            lo, hi = lh
            mid = jnp.floor(0.5 * (lo + hi))

            def pred(blk, g0):
                idx = (g0 + _iota(blk.shape, 1)).astype(F32)
                return jnp.where(blk == vth, _ind(idx <= mid), 0.0)

            ge = count(pred) >= need
            return jnp.where(ge, lo, mid), jnp.where(ge, mid, hi)

        n_steps = int(math.ceil(math.log2(sc_ref.shape[1] + 1))) + 1
        _, hi = lax.fori_loop(0, n_steps, bis,
                              (jnp.full((tq, 1), -1.0, F32), jnp.full((tq, 1), last_idx, F32)))
        return jnp.where(tie > 0.5, hi, j_all)

    jth = lax.cond(jnp.max(tie) > 0.5, tie_search, lambda _: j_all, 0)

    for hh in range(nh):
        m_ref[hh] = jnp.full((tq, d), NEG_BIG, F32)
        l_ref[hh] = jnp.zeros((tq, d), F32)
        acc_ref[hh] = jnp.zeros((tq, d), F32)

    col_w = _iota((tq, wide), 1)

    def att_body(g, carry):
        g0 = pl.multiple_of(g * wide, wide)
        scb = sc_ref[:, pl.ds(g0, wide)]
        idx = (g0 + col_w).astype(F32)
        sel = jnp.where(scb == vth, _ind(idx <= jth), _ind(scb > vth)) > 0.5
        back = [jnp.clip(i - (g * per_wide + pb), 0, 2) for pb in range(per_wide)]
        for hh in range(nh):
            qh = q_ref[:, hh * d:(hh + 1) * d]
            kh = k_ref[pl.ds(g0, wide), hh * d:(hh + 1) * d]
            vh = v_ref[pl.ds(g0, wide), hh * d:(hh + 1) * d]
            bias = jnp.concatenate([bias_ref[back[pb], hh] for pb in range(per_wide)], axis=-1)
            logits = lax.dot_general(qh, kh, (((1,), (1,)), ((), ())),
                                     preferred_element_type=F32) * (d ** -0.5) + bias
            m_old = m_ref[hh]
            m_blk = jnp.max(jnp.where(sel, logits, NEG_BIG), axis=-1, keepdims=True)
            m_new = jnp.maximum(m_old, m_blk)
            p = jnp.where(sel, jnp.exp(logits - m_new[:, :1]), 0.0)
            alpha = jnp.exp(m_old - m_new)
            l_ref[hh] = alpha * l_ref[hh] + jnp.sum(p, axis=-1, keepdims=True)
            acc_ref[hh] = alpha * acc_ref[hh] + jnp.dot(p.astype(BF16), vh, preferred_element_type=F32)
            m_ref[hh] = m_new
        return carry

    lax.fori_loop(0, n_wide, att_body, 0)
    for hh in range(nh):
        o_ref[:, hh * d:(hh + 1) * d] = acc_ref[hh] / l_ref[hh]


def _dsa(p32, p16, bias_tiles, *, tq, cols):
    bsz, s, _ = p32.shape
    d = HEAD_DIM
    nh = N_HEADS
    wide = 4 * tq
    k_sel = min(TOPK_MAX, s // 4)
    w512 = nh * d
    kernel = functools.partial(_dsa_kernel, tq=tq, k_sel=k_sel, wi_lane=cols["wi_lane"], wide=wide)
    resident = dict(pipeline_mode=pl.Buffered(1))
    return pl.pallas_call(
        kernel,
        grid=(bsz, s // tq),
        in_specs=[pl.BlockSpec((None, tq, w512), lambda b, i: (b, i, cols["qi"] // nh)),
                  pl.BlockSpec((None, tq, d), lambda b, i: (b, i, cols["small"])),
                  pl.BlockSpec((None, tq, w512), lambda b, i: (b, i, cols["qb"] // nh)),
                  pl.BlockSpec((None, s, d), lambda b, i: (b, 0, cols["small"]), **resident),
                  pl.BlockSpec((None, s, w512), lambda b, i: (b, 0, cols["kb"] // nh), **resident),
                  pl.BlockSpec((None, s, w512), lambda b, i: (b, 0, cols["vb"] // nh), **resident),
                  pl.BlockSpec((3, nh, tq, tq), lambda b, i: (0, 0, 0, 0), **resident)],
        out_specs=pl.BlockSpec((None, tq, w512), lambda b, i: (b, i, 0)),
        out_shape=jax.ShapeDtypeStruct((bsz, s, w512), F32),
        scratch_shapes=[pltpu.VMEM((tq, s + wide - tq), F32),
                        pltpu.VMEM((IDX_HEADS, tq, tq), F32),
                        pltpu.VMEM((IDX_HEADS, tq, 3 * IDX_DIM), BF16),
                        pltpu.VMEM((nh, tq, d), F32), pltpu.VMEM((nh, tq, d), F32),
                        pltpu.VMEM((nh, tq, d), F32)],
        compiler_params=pltpu.CompilerParams(
            dimension_semantics=("parallel", "arbitrary"), vmem_limit_bytes=VMEM_LIMIT),
        name="dsa",
    )(p32, p32, p16, p32, p16, p16, bias_tiles)


def _t5_bucket(rel):
    nb = REL_BUCKETS // 2
    max_exact = nb // 2
    ret = jnp.where(rel > 0, nb, 0)
    n = jnp.abs(rel)
    large = max_exact + (jnp.log(jnp.maximum(n, 1).astype(F32) / max_exact)
                         / math.log(REL_MAX_DIST / max_exact) * (nb - max_exact)).astype(jnp.int32)
    large = jnp.minimum(large, nb - 1)
    return ret + jnp.where(n < max_exact, n, large)


def _bias_tiles(rel_table, tq):
    assert tq >= REL_MAX_DIST
    t = jnp.arange(tq)
    tiles = []
    for back in range(3):
        rel = (t[None, :] - back * tq) - t[:, None]
        tiles.append(rel_table.astype(F32)[_t5_bucket(rel)].transpose(2, 0, 1))
    return jnp.stack(tiles)


def _even_layout(w_in):
    d = HEAD_DIM
    a_w = 2 * N_HEADS * d + N_HEADS * d
    offs = {}
    o = 0
    for name, w in (("qkv", a_w), ("z", N_HEADS * d), ("a", N_HEADS), ("b", N_HEADS),
                    ("qb", N_HEADS * d), ("kb", N_HEADS * d), ("vb", N_HEADS * d),
                    ("qi", IDX_HEADS * IDX_DIM), ("ki", IDX_DIM), ("wi", IDX_HEADS)):
        offs[name] = (o, o + w)
        o += w
    assert o == w_in.shape[1]
    sl = lambda n: w_in[:, offs[n][0]:offs[n][1]]
    small_w = IDX_DIM + 2 * N_HEADS + IDX_HEADS
    small_pad = -small_w % d
    w = jnp.concatenate([sl("qkv"), sl("z"), sl("qb"), sl("kb"), sl("vb"), sl("qi"),
                         sl("ki"), sl("a"), sl("b"), sl("wi"),
                         jnp.zeros((w_in.shape[0], small_pad), w_in.dtype)], axis=1)
    nh = N_HEADS
    cols = dict(qa=0, ka=nh, va=2 * nh, za=3 * nh, qb=4 * nh, kb=5 * nh, vb=6 * nh, qi=7 * nh,
                small=8 * nh, a_lane=IDX_DIM, b_lane=IDX_DIM + nh, wi_lane=IDX_DIM + 2 * nh)
    return w.astype(BF16), cols


def kernel(x, norm_g, w_in_even, conv_w_even, a_log_even, dt_bias_even, a_norm_even, w_out_even,
           rel_bias, w_in_odd, lb_logits, d_norm_odd, w_out_odd, w_gate, w_up, w_down):
    bsz, s, d = x.shape
    t = bsz * s
    depth = norm_g.shape[0]
    nh = N_HEADS
    tq = 128
    lb_all = jnp.cumsum(jax.nn.softmax(lb_logits.astype(F32), axis=0), axis=0)
    lb_all = lb_all - lb_all[:1]
    odd_cols = dict(qc=0, kc=nh, vc=2 * nh, qd=3 * nh, fd=4 * nh, id=5 * nh, gd=6 * nh)
    bias_tiles = _bias_tiles(rel_bias, tq)

    h = x.reshape(t, d)
    for l in range(depth):
        if l % 2 == 0:
            e = l // 2
            w_even, cols = _even_layout(w_in_even[e])
            p32, p16 = _norm_matmul(h, norm_g[l, 0], w_even, tm=512, tn=w_even.shape[1] // 3)
            p32 = p32.reshape(bsz, s, -1)
            p16 = p16.reshape(bsz, s, -1)
            o_1 = _deltanet(p32, conv_w_even[e], a_log_even[e], dt_bias_even[e], a_norm_even[e],
                            ts=min(512, s), cols=cols)
            o_2 = _dsa(p32, p16, bias_tiles, tq=tq, cols=cols)
            w_out = w_out_even[e]
        else:
            o = l // 2
            p32, p16 = _norm_matmul(h, norm_g[l, 0], w_in_odd[o].astype(BF16), tm=512, tn=512)
            p32 = p32.reshape(bsz, s, -1)
            p16 = p16.reshape(bsz, s, -1)
            o_1 = _stickbreak(p16, tq=tq, cols=odd_cols)
            o_2 = _hgrn2(p32, lb_all[l], d_norm_odd[o], ts=min(512, s), cols=odd_cols)
            w_out = w_out_odd[o]
        h = _outproj(o_1.reshape(t, -1), o_2.reshape(t, -1), w_out, h, norm_g[l, 1], tm=512)
        h = _ffn(h, norm_g[l, 2], norm_g[l, 3], w_gate[l], w_up[l], w_down[l], tm=1024, tf=256)
    return h.reshape(bsz, s, d)
```

```python
import functools
import math

import jax
import jax.numpy as jnp
from jax import lax
from jax.experimental import pallas as pl
from jax.experimental.pallas import tpu as pltpu

F32 = jnp.float32
BF16 = jnp.bfloat16
HIGHEST = lax.Precision.HIGHEST

CHUNK = 64
HEAD_DIM = 128
N_HEADS = 4
IDX_HEADS = 8
IDX_DIM = 64
TOPK_MAX = 256
CONV_WIDTH = 4
REL_BUCKETS = 32
REL_MAX_DIST = 128
EPS = 1e-6
NEG_BIG = -1e30
LOG2E = 1.4426950408889634
BISECT_FIXED = 16
BISECT_EXTRA = 12
EXP_ZERO_BELOW = -104.0
VMEM_LIMIT = 56 * 1024 * 1024


def _mm(a, b):
    return jnp.dot(a.astype(BF16), b.astype(BF16), preferred_element_type=F32)


def _mm_nt(a, b):
    return lax.dot_general(a.astype(BF16), b.astype(BF16), (((1,), (1,)), ((), ())),
                           preferred_element_type=F32)


def _mm_tn(a, b):
    return lax.dot_general(a.astype(BF16), b.astype(BF16), (((0,), (0,)), ((), ())),
                           preferred_element_type=F32)


def _mm_f32(a, b):
    return jnp.dot(a, b, precision=HIGHEST, preferred_element_type=F32)


def _split(x):
    hi = x.astype(BF16)
    return hi, (x - hi.astype(F32)).astype(BF16)


def _mm_x3(a, b):
    a_hi, a_lo = _split(a)
    b_hi, b_lo = _split(b)
    return jnp.dot(jnp.concatenate([a_hi, a_hi, a_lo], axis=1),
                   jnp.concatenate([b_hi, b_lo, b_hi], axis=0), preferred_element_type=F32)


def _sigmoid(x):
    return 1.0 / (1.0 + jnp.exp(-x))


def _silu(x):
    return x * _sigmoid(x)


def _softplus(x):
    return jnp.maximum(x, 0.0) + jnp.log1p(jnp.exp(-jnp.abs(x)))


def _rms(x, g):
    return x * lax.rsqrt(jnp.mean(x * x, axis=-1, keepdims=True) + EPS) * g


def _iota(shape, dim):
    return lax.broadcasted_iota(jnp.int32, shape, dim)


def _ind(mask):
    return jnp.where(mask, 1.0, 0.0)


def _norm_matmul_kernel(x_ref, g_ref, w_ref, o32_ref, o16_ref, xn_ref):
    @pl.when(pl.program_id(1) == 0)
    def _():
        xn_ref[...] = _rms(x_ref[...], g_ref[...]).astype(BF16)

    y = jnp.dot(xn_ref[...], w_ref[...], preferred_element_type=F32)
    o32_ref[...] = y
    o16_ref[...] = y.astype(BF16)


def _norm_matmul(x, g, w, *, tm, tn):
    t, d = x.shape
    n = w.shape[1]
    return pl.pallas_call(
        _norm_matmul_kernel,
        grid=(t // tm, n // tn),
        in_specs=[pl.BlockSpec((tm, d), lambda i, j: (i, 0)),
                  pl.BlockSpec((1, d), lambda i, j: (0, 0)),
                  pl.BlockSpec((d, tn), lambda i, j: (0, j))],
        out_specs=[pl.BlockSpec((tm, tn), lambda i, j: (i, j)),
                   pl.BlockSpec((tm, tn), lambda i, j: (i, j))],
        out_shape=[jax.ShapeDtypeStruct((t, n), F32), jax.ShapeDtypeStruct((t, n), BF16)],
        scratch_shapes=[pltpu.VMEM((tm, d), BF16)],
        compiler_params=pltpu.CompilerParams(
            dimension_semantics=("parallel", "arbitrary"), vmem_limit_bytes=VMEM_LIMIT),
        name="norm_matmul",
    )(x, g.reshape(1, d), w)


def _outproj_kernel(ca_ref, cb_ref, wa_ref, wb_ref, h_ref, g_ref, o_ref):
    y = (jnp.dot(ca_ref[...].astype(BF16), wa_ref[...], preferred_element_type=F32)
         + jnp.dot(cb_ref[...].astype(BF16), wb_ref[...], preferred_element_type=F32))
    o_ref[...] = h_ref[...] + _rms(y, g_ref[...])


def _outproj(ca, cb, w, h, g, *, tm):
    t, d = h.shape
    wa_n = ca.shape[1]
    wb_n = cb.shape[1]
    wa = w[:wa_n].astype(BF16)
    wb = w[wa_n:].astype(BF16)
    return pl.pallas_call(
        _outproj_kernel,
        grid=(t // tm,),
        in_specs=[pl.BlockSpec((tm, wa_n), lambda i: (i, 0)),
                  pl.BlockSpec((tm, wb_n), lambda i: (i, 0)),
                  pl.BlockSpec((wa_n, d), lambda i: (0, 0)),
                  pl.BlockSpec((wb_n, d), lambda i: (0, 0)),
                  pl.BlockSpec((tm, d), lambda i: (i, 0)),
                  pl.BlockSpec((1, d), lambda i: (0, 0))],
        out_specs=pl.BlockSpec((tm, d), lambda i: (i, 0)),
        out_shape=jax.ShapeDtypeStruct((t, d), F32),
        compiler_params=pltpu.CompilerParams(
            dimension_semantics=("parallel",), vmem_limit_bytes=VMEM_LIMIT),
        name="outproj",
    )(ca, cb, wa, wb, h, g.reshape(1, d))


def _ffn_kernel(h_ref, gpre_ref, gpost_ref, wg_ref, wu_ref, wd_ref, o_ref, xn_ref, acc_ref):
    f = pl.program_id(1)

    @pl.when(f == 0)
    def _():
        xn_ref[...] = _rms(h_ref[...], gpre_ref[...]).astype(BF16)
        acc_ref[...] = jnp.zeros_like(acc_ref)

    xn = xn_ref[...]
    gate = jnp.dot(xn, wg_ref[...], preferred_element_type=F32)
    up = jnp.dot(xn, wu_ref[...], preferred_element_type=F32)
    act = (_silu(gate) * up).astype(BF16)
    acc_ref[...] += jnp.dot(act, wd_ref[...], preferred_element_type=F32)

    @pl.when(f == pl.num_programs(1) - 1)
    def _():
        o_ref[...] = h_ref[...] + _rms(acc_ref[...], gpost_ref[...])


def _ffn(h, g_pre, g_post, wg, wu, wd, *, tm, tf):
    t, d = h.shape
    ff = wg.shape[1]
    return pl.pallas_call(
        _ffn_kernel,
        grid=(t // tm, ff // tf),
        in_specs=[pl.BlockSpec((tm, d), lambda i, f: (i, 0)),
                  pl.BlockSpec((1, d), lambda i, f: (0, 0)),
                  pl.BlockSpec((1, d), lambda i, f: (0, 0)),
                  pl.BlockSpec((d, tf), lambda i, f: (0, f)),
                  pl.BlockSpec((d, tf), lambda i, f: (0, f)),
                  pl.BlockSpec((tf, d), lambda i, f: (f, 0))],
        out_specs=pl.BlockSpec((tm, d), lambda i, f: (i, 0)),
        out_shape=jax.ShapeDtypeStruct((t, d), F32),
        scratch_shapes=[pltpu.VMEM((tm, d), BF16), pltpu.VMEM((tm, d), F32)],
        compiler_params=pltpu.CompilerParams(
            dimension_semantics=("parallel", "arbitrary"), vmem_limit_bytes=VMEM_LIMIT),
        name="ffn",
    )(h, g_pre.reshape(1, d), g_post.reshape(1, d),
      wg.astype(BF16), wu.astype(BF16), wd.astype(BF16))


def _deltanet_kernel(xq_ref, xk_ref, xv_ref, z_ref, sm_ref, cwq_ref, cwk_ref, cwv_ref,
                     alog_ref, dtb_ref, gn_ref, o_ref,
                     xpad_ref, q_ref, k_ref, v_ref, gb_ref, bb_ref, u_ref, w_ref, qk_ref, st_ref,
                     *, ts, a_col, b_col):
    h = pl.program_id(1)
    s = pl.program_id(2)
    c = CHUNK
    d = HEAD_DIM

    @pl.when(s == 0)
    def _():
        xpad_ref[:, 0:8, :] = jnp.zeros((3, 8, d), F32)
        st_ref[...] = jnp.zeros_like(st_ref)

    @pl.when(s != 0)
    def _():
        xpad_ref[:, 0:8, :] = xpad_ref[:, ts:ts + 8, :]

    xpad_ref[0, 8:ts + 8, :] = xq_ref[...]
    xpad_ref[1, 8:ts + 8, :] = xk_ref[...]
    xpad_ref[2, 8:ts + 8, :] = xv_ref[...]

    def conv_silu(idx, cw_ref):
        cw = cw_ref[...]
        acc = xpad_ref[idx, 8 - (CONV_WIDTH - 1):8 - (CONV_WIDTH - 1) + ts, :] * cw[0:1, :]
        for j in range(1, CONV_WIDTH):
            off = 8 - (CONV_WIDTH - 1) + j
            acc = acc + xpad_ref[idx, off:off + ts, :] * cw[j:j + 1, :]
        return _silu(acc)

    def l2norm(t):
        return t * lax.rsqrt(jnp.sum(t * t, axis=-1, keepdims=True) + EPS)

    q_ref[...] = l2norm(conv_silu(0, cwq_ref)) * (d ** -0.5)
    k_ref[...] = l2norm(conv_silu(1, cwk_ref))
    v_ref[...] = conv_silu(2, cwv_ref)

    sm = sm_ref[...]
    lane = _iota(sm.shape, 1)
    a_raw = jnp.sum(jnp.where(lane == a_col + h, sm, 0.0), axis=-1, keepdims=True)
    b_raw = jnp.sum(jnp.where(lane == b_col + h, sm, 0.0), axis=-1, keepdims=True)
    hl = _iota((1, d), 1)
    a_log = jnp.sum(jnp.where(hl == h, alog_ref[...], 0.0), axis=-1, keepdims=True)
    dtb = jnp.sum(jnp.where(hl == h, dtb_ref[...], 0.0), axis=-1, keepdims=True)
    g = -jnp.exp(a_log) * _softplus(a_raw + dtb)
    gb_ref[...] = jnp.broadcast_to(g, (ts, d))
    bb_ref[...] = jnp.broadcast_to(_sigmoid(b_raw), (ts, d))

    row = _iota((c, c), 0)
    col = _iota((c, c), 1)
    tri = (col <= row)
    strict = (col < row)
    tri_f = tri.astype(F32)
    upper_f = (row <= col).astype(F32)
    eye = (row == col).astype(F32)
    ones_cc = jnp.ones((c, c), F32)
    gnorm = gn_ref[...]

    chunks = range(ts // c)
    rs = [slice(ci * c, (ci + 1) * c) for ci in chunks]
    tri2 = jnp.concatenate([tri_f, tri_f], axis=1).astype(BF16)
    ones2 = jnp.ones((c, 2 * c), BF16)

    def cum2(lhs2, x):
        hi, lo = _split(x)
        return jnp.dot(lhs2, jnp.concatenate([hi, lo], axis=0), preferred_element_type=F32)

    q = [q_ref[r, :] for r in rs]
    k = [k_ref[r, :] for r in rs]
    beta = [bb_ref[r, :] for r in rs]
    gb = [gb_ref[r, :] for r in rs]
    gc = [cum2(tri2, x) for x in gb]
    gc_row = [cum2(ones2, x[:, :c] * upper_f) for x in gb]
    decay = [jnp.where(tri, jnp.exp(jnp.minimum(a[:, :c] - b, 0.0)), 0.0) for a, b in zip(gc, gc_row)]
    kk = [_mm_nt(x, x) for x in k]
    n = [-jnp.where(strict, b[:, :c] * x * dc, 0.0) for b, x, dc in zip(beta, kk, decay)]
    inv = [eye + x for x in n]
    for _ in range(5):
        n = [_mm_x3(x, x) for x in n]
        inv = [iv + _mm_x3(iv, x) for iv, x in zip(inv, n)]
    egc = [jnp.exp(x) for x in gc]
    gl = [x[c - 1:c, :] for x in gc]
    for ci in chunks:
        r = rs[ci]
        u_ref[r, :] = _mm_x3(inv[ci], v_ref[r, :] * beta[ci])
        w_ref[r, :] = _mm_x3(inv[ci], k[ci] * (beta[ci] * egc[ci]))
        qk_ref[r, :] = _mm_nt(q[ci], k[ci]) * decay[ci]
        q_ref[r, :] = q[ci] * egc[ci]
        k_ref[r, :] = k[ci] * jnp.exp(gl[ci] - gc[ci])
        gb_ref[r, :] = jnp.broadcast_to(jnp.exp(gl[ci]), (c, d))

    def chunk_body(ci, carry):
        r0 = pl.multiple_of(ci * c, c)
        st = st_ref[...]
        v_new = u_ref[pl.ds(r0, c), :] - _mm(w_ref[pl.ds(r0, c), :], st)
        o = _mm(q_ref[pl.ds(r0, c), :], st) + _mm(qk_ref[pl.ds(r0, c), :], v_new)
        st_ref[...] = st * gb_ref[pl.ds(r0, 1), :] + _mm_tn(k_ref[pl.ds(r0, c), :], v_new)
        zc = z_ref[pl.ds(r0, c), :]
        o_ref[pl.ds(r0, c), :] = _rms(o, gnorm) * _silu(zc)
        return carry

    lax.fori_loop(0, ts // c, chunk_body, 0)


def _deltanet(p32, conv_w, a_log, dt_bias, a_norm_g, *, ts, cols):
    bsz, s, _ = p32.shape
    d = HEAD_DIM
    nh = N_HEADS
    qb, kb, vb, zb, smb = cols["qa"], cols["ka"], cols["va"], cols["za"], cols["small"]
    pad = lambda t: jnp.pad(t.astype(F32), (0, d - t.shape[0])).reshape(1, d)
    kernel = functools.partial(_deltanet_kernel, ts=ts, a_col=cols["a_lane"], b_col=cols["b_lane"])
    return pl.pallas_call(
        kernel,
        grid=(bsz, nh, s // ts),
        in_specs=[pl.BlockSpec((None, ts, d), lambda b, h, i: (b, i, qb + h)),
                  pl.BlockSpec((None, ts, d), lambda b, h, i: (b, i, kb + h)),
                  pl.BlockSpec((None, ts, d), lambda b, h, i: (b, i, vb + h)),
                  pl.BlockSpec((None, ts, d), lambda b, h, i: (b, i, zb + h)),
                  pl.BlockSpec((None, ts, d), lambda b, h, i: (b, i, smb)),
                  pl.BlockSpec((CONV_WIDTH, d), lambda b, h, i: (0, h)),
                  pl.BlockSpec((CONV_WIDTH, d), lambda b, h, i: (0, nh + h)),
                  pl.BlockSpec((CONV_WIDTH, d), lambda b, h, i: (0, 2 * nh + h)),
                  pl.BlockSpec((1, d), lambda b, h, i: (0, 0)),
                  pl.BlockSpec((1, d), lambda b, h, i: (0, 0)),
                  pl.BlockSpec((1, d), lambda b, h, i: (0, 0))],
        out_specs=pl.BlockSpec((None, ts, d), lambda b, h, i: (b, i, h)),
        out_shape=jax.ShapeDtypeStruct((bsz, s, nh * d), F32),
        scratch_shapes=[pltpu.VMEM((3, ts + 8, d), F32),
                        pltpu.VMEM((ts, d), F32), pltpu.VMEM((ts, d), F32), pltpu.VMEM((ts, d), F32),
                        pltpu.VMEM((ts, d), F32), pltpu.VMEM((ts, d), F32),
                        pltpu.VMEM((ts, d), F32), pltpu.VMEM((ts, d), F32),
                        pltpu.VMEM((ts, CHUNK), F32),
                        pltpu.VMEM((d, d), F32)],
        compiler_params=pltpu.CompilerParams(
            dimension_semantics=("parallel", "parallel", "arbitrary"), vmem_limit_bytes=VMEM_LIMIT),
        name="deltanet",
    )(p32, p32, p32, p32, p32, conv_w.astype(F32), conv_w.astype(F32), conv_w.astype(F32),
      pad(a_log), pad(dt_bias), a_norm_g.astype(F32).reshape(1, d))


def _hgrn2_kernel(q_ref, f_ref, i_ref, gate_ref, lb_ref, gn_ref, o_ref,
                  qs_ref, ks_ref, gc_ref, st_ref, *, ts):
    s = pl.program_id(2)
    c = CHUNK
    d = HEAD_DIM

    @pl.when(s == 0)
    def _():
        st_ref[...] = jnp.zeros_like(st_ref)

    lb = lb_ref[...]
    f_raw = f_ref[...]
    log_sig = jnp.minimum(f_raw, 0.0) - jnp.log1p(jnp.exp(-jnp.abs(f_raw)))
    la = jnp.log(lb)
    lbb = jnp.log1p(-lb) + log_sig
    log_f = jnp.maximum(la, lbb) + jnp.log1p(jnp.exp(-jnp.abs(la - lbb)))
    qs_ref[...] = _silu(q_ref[...])
    ks_ref[...] = (1.0 - lb) * _sigmoid(-f_raw)

    row = _iota((c, c), 0)
    col = _iota((c, c), 1)
    tri_f = (col <= row).astype(F32)
    ones_dd = jnp.ones((d, d), BF16)
    rows_8d = _iota((8, d), 0)
    gnorm = gn_ref[...]

    for ci in range(ts // c):
        gc_ref[ci * c:(ci + 1) * c, :] = _mm_f32(tri_f, log_f[ci * c:(ci + 1) * c, :])

    def chunk_loop(ci, carry):
        r0 = pl.multiple_of(ci * c, c)
        q = qs_ref[pl.ds(r0, c), :]
        k = ks_ref[pl.ds(r0, c), :]
        v = i_ref[pl.ds(r0, c), :]
        gc = gc_ref[pl.ds(r0, c), :]

        groups = [jnp.zeros((8, d), F32) for _ in range(c // 8)]
        for j in range(c):
            g0 = j // 8
            lo = g0 * 8
            k_j = ks_ref[pl.ds(r0 + j, 1), :]
            g_j = gc_ref[pl.ds(r0 + j, 1), :]
            v_j = i_ref[pl.ds(r0 + j, 1), :]
            e = jnp.exp(jnp.minimum(gc[lo:, :] - g_j, 0.0))
            if j % 8:
                head = jnp.where(rows_8d >= j - lo, e[:8], 0.0)
                e = jnp.concatenate([head, e[8:]], axis=0) if lo + 8 < c else head
            p = q[lo:, :] * k_j * e
            a = jnp.dot(p.astype(BF16), ones_dd, preferred_element_type=F32)
            av = a * v_j
            for g in range(g0, c // 8):
                groups[g] = groups[g] + av[(g - g0) * 8:(g - g0 + 1) * 8, :]
        o_intra = jnp.concatenate(groups, axis=0)

        st = st_ref[...]
        gl = gc[c - 1:c, :]
        o = o_intra + _mm_nt(q * jnp.exp(gc), st)
        st_ref[...] = st * jnp.exp(gl) + _mm_tn(v, k * jnp.exp(gl - gc))
        o_ref[pl.ds(r0, c), :] = _rms(o, gnorm) * _silu(gate_ref[pl.ds(r0, c), :])
        return carry

    lax.fori_loop(0, ts // c, chunk_loop, 0)


def _hgrn2(p32, lb, d_norm_g, *, ts, cols):
    bsz, s, _ = p32.shape
    d = HEAD_DIM
    nh = N_HEADS
    qb, fb, ib, gb = cols["qd"], cols["fd"], cols["id"], cols["gd"]
    kernel = functools.partial(_hgrn2_kernel, ts=ts)
    return pl.pallas_call(
        kernel,
        grid=(bsz, nh, s // ts),
        in_specs=[pl.BlockSpec((None, ts, d), lambda b, h, i: (b, i, qb + h)),
                  pl.BlockSpec((None, ts, d), lambda b, h, i: (b, i, fb + h)),
                  pl.BlockSpec((None, ts, d), lambda b, h, i: (b, i, ib + h)),
                  pl.BlockSpec((None, ts, d), lambda b, h, i: (b, i, gb + h)),
                  pl.BlockSpec((1, d), lambda b, h, i: (0, h)),
                  pl.BlockSpec((1, d), lambda b, h, i: (0, 0))],
        out_specs=pl.BlockSpec((None, ts, d), lambda b, h, i: (b, i, h)),
        out_shape=jax.ShapeDtypeStruct((bsz, s, nh * d), F32),
        scratch_shapes=[pltpu.VMEM((ts, d), F32), pltpu.VMEM((ts, d), F32),
                        pltpu.VMEM((ts, d), F32), pltpu.VMEM((d, d), F32)],
        compiler_params=pltpu.CompilerParams(
            dimension_semantics=("parallel", "parallel", "arbitrary"), vmem_limit_bytes=VMEM_LIMIT),
        name="hgrn2",
    )(p32, p32, p32, p32, lb.astype(F32).reshape(1, nh * d), d_norm_g.astype(F32).reshape(1, d))


def _stickbreak_kernel(q_ref, k_ref, v_ref, o_ref, *, tq):
    i = pl.program_id(2)
    d = HEAD_DIM
    q = q_ref[...]
    row = _iota((tq, tq), 0)
    col = _iota((tq, tq), 1)
    causal = col < row
    later = (row > col).astype(BF16)

    def block(j, carry, acc, diag):
        r0 = pl.multiple_of(j * tq, tq)
        z = _mm_nt(q, k_ref[pl.ds(r0, tq), :]) * (d ** -0.5)
        sp = _softplus(z)
        l1m = jnp.where(causal, -sp, 0.0) if diag else -sp
        l_hi = l1m.astype(BF16)
        l_lo = (l1m - l_hi.astype(F32)).astype(BF16)
        rest = (jnp.dot(l_hi, later, preferred_element_type=F32)
                + jnp.dot(l_lo, later, preferred_element_type=F32))
        logw = (z - sp) + rest + carry
        p = jnp.exp(logw)
        if diag:
            p = jnp.where(causal, p, 0.0)
        acc = acc + _mm(p, v_ref[pl.ds(r0, tq), :])
        carry = carry + jnp.sum(l1m, axis=-1, keepdims=True)
        return carry, acc

    carry, acc = block(i, jnp.zeros((tq, 1), F32), jnp.zeros((tq, d), F32), True)

    def cond(c):
        return jnp.logical_and(c[0] >= 0, jnp.max(c[1]) >= EXP_ZERO_BELOW)

    def body(c):
        carry, acc = block(c[0], c[1], c[2], False)
        return c[0] - 1, carry, acc

    _, _, acc = lax.while_loop(cond, body, (i - 1, carry, acc))
    o_ref[...] = acc


def _stickbreak(p16, *, tq, cols):
    bsz, s, _ = p16.shape
    d = HEAD_DIM
    nh = N_HEADS
    qb, kb, vb = cols["qc"], cols["kc"], cols["vc"]
    kernel = functools.partial(_stickbreak_kernel, tq=tq)
    return pl.pallas_call(
        kernel,
        grid=(bsz, nh, s // tq),
        in_specs=[pl.BlockSpec((None, tq, d), lambda b, h, i: (b, i, qb + h)),
                  pl.BlockSpec((None, s, d), lambda b, h, i: (b, 0, kb + h)),
                  pl.BlockSpec((None, s, d), lambda b, h, i: (b, 0, vb + h))],
        out_specs=pl.BlockSpec((None, tq, d), lambda b, h, i: (b, i, h)),
        out_shape=jax.ShapeDtypeStruct((bsz, s, nh * d), F32),
        compiler_params=pltpu.CompilerParams(
            dimension_semantics=("parallel", "parallel", "arbitrary"), vmem_limit_bytes=VMEM_LIMIT),
        name="stickbreak",
    )(p16, p16, p16)


def _dsa_kernel(qi_ref, smq_ref, q_ref, sm_ref, k_ref, v_ref, bias_ref, o_ref,
                sc_ref, wb_ref, qc_ref, kct_ref, q2_ref, lg_ref, l_ref, acc_ref, *, tq, k_sel, wi_lane, wide):
    i = pl.program_id(1)
    tk = tq
    d = HEAD_DIM
    nh = N_HEADS
    ksel = float(k_sel)
    per_wide = wide // tk
    n_wide = (i + per_wide) // per_wide
    sub = 2 * tk

    q2_ref[...] = (q_ref[...] * ((d ** -0.5) * LOG2E)).astype(BF16)

    @pl.when(i == 0)
    def _():
        def prep(g, carry):
            g0 = pl.multiple_of(g * wide, wide)
            kt = sm_ref[pl.ds(g0, wide), :].T[:IDX_DIM, :]
            hi, lo = _split(kt)
            kct_ref[:, pl.ds(g0, wide)] = jnp.concatenate([hi, lo, hi], axis=0)
            return carry
        lax.fori_loop(0, sm_ref.shape[0] // wide, prep, 0)

    smq = smq_ref[...]
    lane = _iota(smq.shape, 1)
    for hh in range(IDX_HEADS):
        qh = qi_ref[:, hh * IDX_DIM:(hh + 1) * IDX_DIM]
        hi = qh.astype(BF16)
        lo = (qh - hi.astype(F32)).astype(BF16)
        qc_ref[hh] = jnp.concatenate([hi, hi, lo], axis=-1)
        w = jnp.sum(jnp.where(lane == wi_lane + hh, smq, 0.0), axis=-1, keepdims=True)
        wb_ref[hh] = jnp.broadcast_to(w * ((IDX_HEADS ** -0.5) * (IDX_DIM ** -0.5)), (tq, tk))

    def fold(x, op=jnp.add):
        acc = x[:, 0:tk]
        for pb in range(1, x.shape[1] // tk):
            acc = op(acc, x[:, pb * tk:(pb + 1) * tk])
        return acc

    limit = i * tq + (_iota((tq, sub), 0) // CHUNK + 1) * CHUNK
    col_s = _iota((tq, sub), 1)

    def score_group(g, mm, masked):
        mn, mx = mm
        for sb in range(wide // sub):
            k0 = pl.multiple_of(g * wide + sb * sub, sub)
            kct = kct_ref[:, pl.ds(k0, sub)]
            tiles = [jnp.zeros((tq, tk), F32) for _ in range(sub // tk)]
            for hh in range(IDX_HEADS):
                s_h = jnp.dot(qc_ref[hh], kct, preferred_element_type=F32)
                for ti in range(sub // tk):
                    tiles[ti] = tiles[ti] + jnp.maximum(s_h[:, ti * tk:(ti + 1) * tk], 0.0) * wb_ref[hh]
            sc = jnp.concatenate(tiles, axis=-1)
            if masked:
                adm = (k0 + col_s) < limit
                sc_ref[:, pl.ds(k0, sub)] = jnp.where(adm, sc, -jnp.inf)
                mn = jnp.minimum(mn, fold(jnp.where(adm, sc, jnp.inf), jnp.minimum))
                mx = jnp.maximum(mx, fold(jnp.where(adm, sc, -jnp.inf), jnp.maximum))
            else:
                sc_ref[:, pl.ds(k0, sub)] = sc
                mn = jnp.minimum(mn, fold(sc, jnp.minimum))
                mx = jnp.maximum(mx, fold(sc, jnp.maximum))
        return mn, mx

    mm = lax.fori_loop(0, n_wide - 1, functools.partial(score_group, masked=False),
                       (jnp.full((tq, tk), jnp.inf, F32), jnp.full((tq, tk), -jnp.inf, F32)))
    mn, mx = score_group(n_wide - 1, mm, True)
    rmin = jnp.min(mn, axis=-1, keepdims=True)
    rmax = jnp.max(mx, axis=-1, keepdims=True)


    def count(pred):
        def body(g, acc):
            g0 = pl.multiple_of(g * wide, wide)
            blk = sc_ref[:, pl.ds(g0, wide)]
            return acc + fold(pred(blk, g0))
        acc = lax.fori_loop(0, n_wide, body, jnp.zeros((tq, tk), F32))
        return jnp.sum(acc, axis=-1, keepdims=True)

    def max_below(x):
        def body(g, acc):
            blk = sc_ref[:, pl.ds(pl.multiple_of(g * wide, wide), wide)]
            mb = jnp.where(blk < x, blk, -jnp.inf)
            m = mb[:, 0:tk]
            for pb in range(1, per_wide):
                m = jnp.maximum(m, mb[:, pb * tk:(pb + 1) * tk])
            return jnp.maximum(acc, m)
        acc = lax.fori_loop(0, n_wide, body, jnp.full((tq, tk), -jnp.inf, F32))
        return jnp.max(acc, axis=-1, keepdims=True)

    rows1 = _iota((tq, 1), 0)
    n_adm = (i * tq + (rows1 // CHUNK + 1) * CHUNK).astype(F32)
    all_sel = n_adm <= ksel
    last_idx = float(sc_ref.shape[1])
    j_all = jnp.where(all_sel, -1.0, last_idx)

    def bisect(c):
        lo, hi, c_lo = c
        mid = 0.5 * lo + 0.5 * hi
        cm = count(lambda blk, g0: _ind(blk >= mid))
        ge = cm >= ksel
        return jnp.where(ge, mid, lo), jnp.where(ge, hi, mid), jnp.where(ge, cm, c_lo)

    def pending(c_lo):
        return jnp.where(all_sel, 0.0, _ind(c_lo != ksel))

    hi0 = rmax + (jnp.abs(rmax) * (2.0 ** -20) + 1e-30)
    state = lax.fori_loop(0, BISECT_FIXED, lambda _, c: bisect(c), (rmin, hi0, n_adm))

    def more_cond(c):
        return jnp.logical_and(c[0] < BISECT_EXTRA, jnp.max(pending(c[1][2])) > 0.5)

    def more_body(c):
        return c[0] + 1, bisect(c[1])

    _, (lo_f, hi_f, c_lo_f) = lax.while_loop(more_cond, more_body, (jnp.int32(0), state))
    unresolved = pending(c_lo_f)
    v_fast = jnp.where(all_sel, -jnp.inf, lo_f)

    def slow_path(_):
        def search_cond(c):
            return jnp.min(c[3]) < 0.5

        def search_body(c):
            lo, hi, v, done = c
            cand = max_below(hi)
            ok = count(lambda blk, g0: _ind(blk >= cand)) >= ksel
            v = jnp.where(done > 0.5, v, cand)
            done = jnp.where(ok, 1.0, done)
            lo, hi, _ = lax.fori_loop(0, 6, lambda _, s: bisect(s), (lo, hi, c_lo_f))
            return lo, hi, v, done

        _, _, vth, _ = lax.while_loop(search_cond, search_body,
                                      (lo_f, hi_f, v_fast, 1.0 - unresolved))
        c_gt = count(lambda blk, g0: _ind(blk > vth))
        need = ksel - c_gt

        def bis(_, lh):
            lo, hi = lh
            mid = jnp.floor(0.5 * (lo + hi))

            def pred(blk, g0):
                idx = (g0 + _iota(blk.shape, 1)).astype(F32)
                return jnp.where(blk == vth, _ind(idx <= mid), 0.0)

            ge = count(pred) >= need
            return jnp.where(ge, lo, mid), jnp.where(ge, mid, hi)

        n_steps = int(math.ceil(math.log2(sc_ref.shape[1] + 1))) + 1
        _, j_tie = lax.fori_loop(0, n_steps, bis,
                                 (jnp.full((tq, 1), -1.0, F32), jnp.full((tq, 1), last_idx, F32)))
        return vth, jnp.where(unresolved > 0.5, j_tie, j_all)

    vth, jth = lax.cond(jnp.max(unresolved) > 0.5, slow_path, lambda _: (v_fast, j_all), 0)

    col_w = _iota((tq, wide), 1)
    g_near = jnp.maximum(i - 1, 0) // per_wide

    def logit_group(g, mx, near):
        g0 = pl.multiple_of(g * wide, wide)
        scb = sc_ref[:, pl.ds(g0, wide)]
        idx = (g0 + col_w).astype(F32)
        sel_f = jnp.where(scb == vth, _ind(idx <= jth), _ind(scb > vth))
        mb = jnp.where(sel_f > 0.5, 0.0, NEG_BIG)
        out = []
        for hh in range(nh):
            kh = k_ref[pl.ds(g0, wide), hh * d:(hh + 1) * d]
            lm = lax.dot_general(q2_ref[:, hh * d:(hh + 1) * d], kh, (((1,), (1,)), ((), ())),
                                 preferred_element_type=F32) + mb
            if near:
                back = [jnp.clip(i - (g * per_wide + pb), 0, 2) for pb in range(per_wide)]
                lm = lm + jnp.concatenate([bias_ref[back[pb], hh] for pb in range(per_wide)], axis=-1)
            lg_ref[hh, :, pl.ds(g0, wide)] = lm
            out.append(jnp.maximum(mx[hh], fold(lm, jnp.maximum)))
        return tuple(out)

    mx = tuple(jnp.full((tq, tk), NEG_BIG, F32) for _ in range(nh))
    mx = lax.fori_loop(0, g_near, functools.partial(logit_group, near=False), mx)
    mx = lax.fori_loop(g_near, n_wide, functools.partial(logit_group, near=True), mx)

    m_rows = [jnp.broadcast_to(jnp.max(mx[hh], axis=-1, keepdims=True), (tq, tk)) for hh in range(nh)]
    for hh in range(nh):
        l_ref[hh] = jnp.zeros((tq, tk), F32)
        acc_ref[hh] = jnp.zeros((tq, d), F32)

    def pv_body(g, carry):
        g0 = pl.multiple_of(g * wide, wide)
        for hh in range(nh):
            lm = lg_ref[hh, :, pl.ds(g0, wide)]
            p = jnp.concatenate([jnp.exp2(lm[:, pb * tk:(pb + 1) * tk] - m_rows[hh])
                                 for pb in range(per_wide)], axis=-1)
            vh = v_ref[pl.ds(g0, wide), hh * d:(hh + 1) * d]
            l_ref[hh] += fold(p)
            acc_ref[hh] += jnp.dot(p.astype(BF16), vh, preferred_element_type=F32)
        return carry

    lax.fori_loop(0, n_wide, pv_body, 0)
    for hh in range(nh):
        o_ref[:, hh * d:(hh + 1) * d] = acc_ref[hh] / jnp.sum(l_ref[hh], axis=-1, keepdims=True)


def _dsa(p32, p16, bias_tiles, *, tq, cols):
    bsz, s, _ = p32.shape
    d = HEAD_DIM
    nh = N_HEADS
    wide = 4 * tq
    k_sel = min(TOPK_MAX, s // 4)
    w512 = nh * d
    kernel = functools.partial(_dsa_kernel, tq=tq, k_sel=k_sel, wi_lane=cols["wi_lane"], wide=wide)
    resident = dict(pipeline_mode=pl.Buffered(1))
    return pl.pallas_call(
        kernel,
        grid=(bsz, s // tq),
        in_specs=[pl.BlockSpec((None, tq, w512), lambda b, i: (b, i, cols["qi"] // nh)),
                  pl.BlockSpec((None, tq, d), lambda b, i: (b, i, cols["small"])),
                  pl.BlockSpec((None, tq, w512), lambda b, i: (b, i, cols["qb"] // nh)),
                  pl.BlockSpec((None, s, d), lambda b, i: (b, 0, cols["small"]), **resident),
                  pl.BlockSpec((None, s, w512), lambda b, i: (b, 0, cols["kb"] // nh), **resident),
                  pl.BlockSpec((None, s, w512), lambda b, i: (b, 0, cols["vb"] // nh), **resident),
                  pl.BlockSpec((3, nh, tq, tq), lambda b, i: (0, 0, 0, 0), **resident)],
        out_specs=pl.BlockSpec((None, tq, w512), lambda b, i: (b, i, 0)),
        out_shape=jax.ShapeDtypeStruct((bsz, s, w512), F32),
        scratch_shapes=[pltpu.VMEM((tq, s + wide - tq), F32),
                        pltpu.VMEM((IDX_HEADS, tq, tq), F32),
                        pltpu.VMEM((IDX_HEADS, tq, 3 * IDX_DIM), BF16),
                        pltpu.VMEM((3 * IDX_DIM, s), BF16),
                        pltpu.VMEM((tq, w512), BF16),
                        pltpu.VMEM((nh, tq, s), F32),
                        pltpu.VMEM((nh, tq, tq), F32), pltpu.VMEM((nh, tq, d), F32)],
        compiler_params=pltpu.CompilerParams(
            dimension_semantics=("parallel", "arbitrary"), vmem_limit_bytes=VMEM_LIMIT),
        name="dsa",
    )(p32, p32, p32, p32, p16, p16, bias_tiles)


def _t5_bucket(rel):
    nb = REL_BUCKETS // 2
    max_exact = nb // 2
    ret = jnp.where(rel > 0, nb, 0)
    n = jnp.abs(rel)
    large = max_exact + (jnp.log(jnp.maximum(n, 1).astype(F32) / max_exact)
                         / math.log(REL_MAX_DIST / max_exact) * (nb - max_exact)).astype(jnp.int32)
    large = jnp.minimum(large, nb - 1)
    return ret + jnp.where(n < max_exact, n, large)


def _bias_tiles(rel_table, tq):
    assert tq >= REL_MAX_DIST
    t = jnp.arange(tq)
    tiles = []
    for back in range(3):
        rel = (t[None, :] - back * tq) - t[:, None]
        tiles.append(rel_table.astype(F32)[_t5_bucket(rel)].transpose(2, 0, 1))
    tiles = jnp.stack(tiles)
    return (tiles - tiles[2:3]) * LOG2E


def _even_layout(w_in):
    d = HEAD_DIM
    a_w = 2 * N_HEADS * d + N_HEADS * d
    offs = {}
    o = 0
    for name, w in (("qkv", a_w), ("z", N_HEADS * d), ("a", N_HEADS), ("b", N_HEADS),
                    ("qb", N_HEADS * d), ("kb", N_HEADS * d), ("vb", N_HEADS * d),
                    ("qi", IDX_HEADS * IDX_DIM), ("ki", IDX_DIM), ("wi", IDX_HEADS)):
        offs[name] = (o, o + w)
        o += w
    assert o == w_in.shape[1]
    sl = lambda n: w_in[:, offs[n][0]:offs[n][1]]
    small_w = IDX_DIM + 2 * N_HEADS + IDX_HEADS
    small_pad = -small_w % d
    w = jnp.concatenate([sl("qkv"), sl("z"), sl("qb"), sl("kb"), sl("vb"), sl("qi"),
                         sl("ki"), sl("a"), sl("b"), sl("wi"),
                         jnp.zeros((w_in.shape[0], small_pad), w_in.dtype)], axis=1)
    nh = N_HEADS
    cols = dict(qa=0, ka=nh, va=2 * nh, za=3 * nh, qb=4 * nh, kb=5 * nh, vb=6 * nh, qi=7 * nh,
                small=8 * nh, a_lane=IDX_DIM, b_lane=IDX_DIM + nh, wi_lane=IDX_DIM + 2 * nh)
    return w.astype(BF16), cols


def kernel(x, norm_g, w_in_even, conv_w_even, a_log_even, dt_bias_even, a_norm_even, w_out_even,
           rel_bias, w_in_odd, lb_logits, d_norm_odd, w_out_odd, w_gate, w_up, w_down):
    bsz, s, d = x.shape
    t = bsz * s
    depth = norm_g.shape[0]
    nh = N_HEADS
    tq = 128
    lb_all = jnp.cumsum(jax.nn.softmax(lb_logits.astype(F32), axis=0), axis=0)
    lb_all = lb_all - lb_all[:1]
    odd_cols = dict(qc=0, kc=nh, vc=2 * nh, qd=3 * nh, fd=4 * nh, id=5 * nh, gd=6 * nh)
    bias_tiles = _bias_tiles(rel_bias, tq)

    h = x.reshape(t, d)
    for l in range(depth):
        if l % 2 == 0:
            e = l // 2
            w_even, cols = _even_layout(w_in_even[e])
            p32, p16 = _norm_matmul(h, norm_g[l, 0], w_even, tm=512, tn=w_even.shape[1] // 3)
            p32 = p32.reshape(bsz, s, -1)
            p16 = p16.reshape(bsz, s, -1)
            o_1 = _deltanet(p32, conv_w_even[e], a_log_even[e], dt_bias_even[e], a_norm_even[e],
                            ts=min(512, s), cols=cols)
            o_2 = _dsa(p32, p16, bias_tiles, tq=tq, cols=cols)
            w_out = w_out_even[e]
        else:
            o = l // 2
            p32, p16 = _norm_matmul(h, norm_g[l, 0], w_in_odd[o].astype(BF16), tm=512, tn=512)
            p32 = p32.reshape(bsz, s, -1)
            p16 = p16.reshape(bsz, s, -1)
            o_1 = _stickbreak(p16, tq=tq, cols=odd_cols)
            o_2 = _hgrn2(p32, lb_all[l], d_norm_odd[o], ts=min(512, s), cols=odd_cols)
            w_out = w_out_odd[o]
        h = _outproj(o_1.reshape(t, -1), o_2.reshape(t, -1), w_out, h, norm_g[l, 1], tm=512)
        h = _ffn(h, norm_g[l, 2], norm_g[l, 3], w_gate[l], w_up[l], w_down[l], tm=1024, tf=256)
    return h.reshape(bsz, s, d)
```

```python
import functools
import math

import jax
import jax.numpy as jnp
from jax import lax
from jax.experimental import pallas as pl
from jax.experimental.pallas import tpu as pltpu

F32 = jnp.float32
BF16 = jnp.bfloat16
HIGHEST = lax.Precision.HIGHEST

CHUNK = 64
HEAD_DIM = 128
N_HEADS = 4
IDX_HEADS = 8
IDX_DIM = 64
TOPK_MAX = 256
CONV_WIDTH = 4
REL_BUCKETS = 32
REL_MAX_DIST = 128
EPS = 1e-6
NEG_BIG = -1e30
LOG2E = 1.4426950408889634
BISECT_FIXED = 16
BISECT_EXTRA = 12
EXP_ZERO_BELOW = -104.0
VMEM_LIMIT = 56 * 1024 * 1024


def _mm(a, b):
    return jnp.dot(a.astype(BF16), b.astype(BF16), preferred_element_type=F32)


def _mm_nt(a, b):
    return lax.dot_general(a.astype(BF16), b.astype(BF16), (((1,), (1,)), ((), ())),
                           preferred_element_type=F32)


def _mm_tn(a, b):
    return lax.dot_general(a.astype(BF16), b.astype(BF16), (((0,), (0,)), ((), ())),
                           preferred_element_type=F32)


def _mm_f32(a, b):
    return jnp.dot(a, b, precision=HIGHEST, preferred_element_type=F32)


def _split(x):
    hi = x.astype(BF16)
    return hi, (x - hi.astype(F32)).astype(BF16)


def _mm_x3(a, b):
    a_hi, a_lo = _split(a)
    b_hi, b_lo = _split(b)
    return jnp.dot(jnp.concatenate([a_hi, a_hi, a_lo], axis=1),
                   jnp.concatenate([b_hi, b_lo, b_hi], axis=0), preferred_element_type=F32)


def _sigmoid(x):
    return 1.0 / (1.0 + jnp.exp(-x))


def _silu(x):
    return x * _sigmoid(x)


def _softplus(x):
    return jnp.maximum(x, 0.0) + jnp.log1p(jnp.exp(-jnp.abs(x)))


def _rms(x, g):
    return x * lax.rsqrt(jnp.mean(x * x, axis=-1, keepdims=True) + EPS) * g


def _iota(shape, dim):
    return lax.broadcasted_iota(jnp.int32, shape, dim)


def _ind(mask):
    return jnp.where(mask, 1.0, 0.0)


def _norm_matmul_kernel(x_ref, g_ref, w_ref, o32_ref, o16_ref, xn_ref):
    @pl.when(pl.program_id(1) == 0)
    def _():
        xn_ref[...] = _rms(x_ref[...], g_ref[...]).astype(BF16)

    y = jnp.dot(xn_ref[...], w_ref[...], preferred_element_type=F32)
    o32_ref[...] = y
    o16_ref[...] = y.astype(BF16)


def _norm_matmul(x, g, w, *, tm, tn):
    t, d = x.shape
    n = w.shape[1]
    return pl.pallas_call(
        _norm_matmul_kernel,
        grid=(t // tm, n // tn),
        in_specs=[pl.BlockSpec((tm, d), lambda i, j: (i, 0)),
                  pl.BlockSpec((1, d), lambda i, j: (0, 0)),
                  pl.BlockSpec((d, tn), lambda i, j: (0, j))],
        out_specs=[pl.BlockSpec((tm, tn), lambda i, j: (i, j)),
                   pl.BlockSpec((tm, tn), lambda i, j: (i, j))],
        out_shape=[jax.ShapeDtypeStruct((t, n), F32), jax.ShapeDtypeStruct((t, n), BF16)],
        scratch_shapes=[pltpu.VMEM((tm, d), BF16)],
        compiler_params=pltpu.CompilerParams(
            dimension_semantics=("parallel", "arbitrary"), vmem_limit_bytes=VMEM_LIMIT),
        name="norm_matmul",
    )(x, g.reshape(1, d), w)


def _outproj_kernel(ca_ref, cb_ref, wa_ref, wb_ref, h_ref, g_ref, o_ref):
    y = (jnp.dot(ca_ref[...].astype(BF16), wa_ref[...], preferred_element_type=F32)
         + jnp.dot(cb_ref[...].astype(BF16), wb_ref[...], preferred_element_type=F32))
    o_ref[...] = h_ref[...] + _rms(y, g_ref[...])


def _outproj(ca, cb, w, h, g, *, tm):
    t, d = h.shape
    wa_n = ca.shape[1]
    wb_n = cb.shape[1]
    wa = w[:wa_n].astype(BF16)
    wb = w[wa_n:].astype(BF16)
    return pl.pallas_call(
        _outproj_kernel,
        grid=(t // tm,),
        in_specs=[pl.BlockSpec((tm, wa_n), lambda i: (i, 0)),
                  pl.BlockSpec((tm, wb_n), lambda i: (i, 0)),
                  pl.BlockSpec((wa_n, d), lambda i: (0, 0)),
                  pl.BlockSpec((wb_n, d), lambda i: (0, 0)),
                  pl.BlockSpec((tm, d), lambda i: (i, 0)),
                  pl.BlockSpec((1, d), lambda i: (0, 0))],
        out_specs=pl.BlockSpec((tm, d), lambda i: (i, 0)),
        out_shape=jax.ShapeDtypeStruct((t, d), F32),
        compiler_params=pltpu.CompilerParams(
            dimension_semantics=("parallel",), vmem_limit_bytes=VMEM_LIMIT),
        name="outproj",
    )(ca, cb, wa, wb, h, g.reshape(1, d))


def _ffn_kernel(h_ref, gpre_ref, gpost_ref, wg_ref, wu_ref, wd_ref, o_ref, xn_ref, acc_ref):
    f = pl.program_id(1)

    @pl.when(f == 0)
    def _():
        xn_ref[...] = _rms(h_ref[...], gpre_ref[...]).astype(BF16)
        acc_ref[...] = jnp.zeros_like(acc_ref)

    xn = xn_ref[...]
    gate = jnp.dot(xn, wg_ref[...], preferred_element_type=F32)
    up = jnp.dot(xn, wu_ref[...], preferred_element_type=F32)
    act = (_silu(gate) * up).astype(BF16)
    acc_ref[...] += jnp.dot(act, wd_ref[...], preferred_element_type=F32)

    @pl.when(f == pl.num_programs(1) - 1)
    def _():
        o_ref[...] = h_ref[...] + _rms(acc_ref[...], gpost_ref[...])


def _ffn(h, g_pre, g_post, wg, wu, wd, *, tm, tf):
    t, d = h.shape
    ff = wg.shape[1]
    return pl.pallas_call(
        _ffn_kernel,
        grid=(t // tm, ff // tf),
        in_specs=[pl.BlockSpec((tm, d), lambda i, f: (i, 0)),
                  pl.BlockSpec((1, d), lambda i, f: (0, 0)),
                  pl.BlockSpec((1, d), lambda i, f: (0, 0)),
                  pl.BlockSpec((d, tf), lambda i, f: (0, f)),
                  pl.BlockSpec((d, tf), lambda i, f: (0, f)),
                  pl.BlockSpec((tf, d), lambda i, f: (f, 0))],
        out_specs=pl.BlockSpec((tm, d), lambda i, f: (i, 0)),
        out_shape=jax.ShapeDtypeStruct((t, d), F32),
        scratch_shapes=[pltpu.VMEM((tm, d), BF16), pltpu.VMEM((tm, d), F32)],
        compiler_params=pltpu.CompilerParams(
            dimension_semantics=("parallel", "arbitrary"), vmem_limit_bytes=VMEM_LIMIT),
        name="ffn",
    )(h, g_pre.reshape(1, d), g_post.reshape(1, d),
      wg.astype(BF16), wu.astype(BF16), wd.astype(BF16))


def _deltanet_kernel(xq_ref, xk_ref, xv_ref, z_ref, sm_ref, cwq_ref, cwk_ref, cwv_ref,
                     alog_ref, dtb_ref, gn_ref, o_ref,
                     xpad_ref, q_ref, k_ref, v_ref, gb_ref, bb_ref, u_ref, w_ref, qk_ref, st_ref,
                     *, ts, a_col, b_col):
    h = pl.program_id(1)
    s = pl.program_id(2)
    c = CHUNK
    d = HEAD_DIM

    @pl.when(s == 0)
    def _():
        xpad_ref[:, 0:8, :] = jnp.zeros((3, 8, d), F32)
        st_ref[...] = jnp.zeros_like(st_ref)

    @pl.when(s != 0)
    def _():
        xpad_ref[:, 0:8, :] = xpad_ref[:, ts:ts + 8, :]

    xpad_ref[0, 8:ts + 8, :] = xq_ref[...]
    xpad_ref[1, 8:ts + 8, :] = xk_ref[...]
    xpad_ref[2, 8:ts + 8, :] = xv_ref[...]

    def conv_silu(idx, cw_ref):
        cw = cw_ref[...]
        acc = xpad_ref[idx, 8 - (CONV_WIDTH - 1):8 - (CONV_WIDTH - 1) + ts, :] * cw[0:1, :]
        for j in range(1, CONV_WIDTH):
            off = 8 - (CONV_WIDTH - 1) + j
            acc = acc + xpad_ref[idx, off:off + ts, :] * cw[j:j + 1, :]
        return _silu(acc)

    def l2norm(t):
        return t * lax.rsqrt(jnp.sum(t * t, axis=-1, keepdims=True) + EPS)

    q_ref[...] = l2norm(conv_silu(0, cwq_ref)) * (d ** -0.5)
    k_ref[...] = l2norm(conv_silu(1, cwk_ref))
    v_ref[...] = conv_silu(2, cwv_ref)

    sm = sm_ref[...]
    lane = _iota(sm.shape, 1)
    a_raw = jnp.sum(jnp.where(lane == a_col + h, sm, 0.0), axis=-1, keepdims=True)
    b_raw = jnp.sum(jnp.where(lane == b_col + h, sm, 0.0), axis=-1, keepdims=True)
    hl = _iota((1, d), 1)
    a_log = jnp.sum(jnp.where(hl == h, alog_ref[...], 0.0), axis=-1, keepdims=True)
    dtb = jnp.sum(jnp.where(hl == h, dtb_ref[...], 0.0), axis=-1, keepdims=True)
    g = -jnp.exp(a_log) * _softplus(a_raw + dtb)
    gb_ref[...] = jnp.broadcast_to(g, (ts, d))
    bb_ref[...] = jnp.broadcast_to(_sigmoid(b_raw), (ts, d))

    row = _iota((c, c), 0)
    col = _iota((c, c), 1)
    tri = (col <= row)
    strict = (col < row)
    tri_f = tri.astype(F32)
    upper_f = (row <= col).astype(F32)
    eye = (row == col).astype(F32)
    ones_cc = jnp.ones((c, c), F32)
    gnorm = gn_ref[...]

    chunks = range(ts // c)
    rs = [slice(ci * c, (ci + 1) * c) for ci in chunks]
    tri2 = jnp.concatenate([tri_f, tri_f], axis=1).astype(BF16)
    ones2 = jnp.ones((c, 2 * c), BF16)

    def cum2(lhs2, x):
        hi, lo = _split(x)
        return jnp.dot(lhs2, jnp.concatenate([hi, lo], axis=0), preferred_element_type=F32)

    q = [q_ref[r, :] for r in rs]
    k = [k_ref[r, :] for r in rs]
    beta = [bb_ref[r, :] for r in rs]
    gb = [gb_ref[r, :] for r in rs]
    gc = [cum2(tri2, x) for x in gb]
    gc_row = [cum2(ones2, x[:, :c] * upper_f) for x in gb]
    decay = [jnp.where(tri, jnp.exp(jnp.minimum(a[:, :c] - b, 0.0)), 0.0) for a, b in zip(gc, gc_row)]
    kk = [_mm_nt(x, x) for x in k]
    n = [-jnp.where(strict, b[:, :c] * x * dc, 0.0) for b, x, dc in zip(beta, kk, decay)]
    inv = [eye + x for x in n]
    for _ in range(5):
        n = [_mm_x3(x, x) for x in n]
        inv = [iv + _mm_x3(iv, x) for iv, x in zip(inv, n)]
    egc = [jnp.exp(x) for x in gc]
    gl = [x[c - 1:c, :] for x in gc]
    for ci in chunks:
        r = rs[ci]
        u_ref[r, :] = _mm_x3(inv[ci], v_ref[r, :] * beta[ci])
        w_ref[r, :] = _mm_x3(inv[ci], k[ci] * (beta[ci] * egc[ci]))
        qk_ref[r, :] = _mm_nt(q[ci], k[ci]) * decay[ci]
        q_ref[r, :] = q[ci] * egc[ci]
        k_ref[r, :] = k[ci] * jnp.exp(gl[ci] - gc[ci])
        gb_ref[r, :] = jnp.broadcast_to(jnp.exp(gl[ci]), (c, d))

    def chunk_body(ci, carry):
        r0 = pl.multiple_of(ci * c, c)
        st = st_ref[...]
        v_new = u_ref[pl.ds(r0, c), :] - _mm(w_ref[pl.ds(r0, c), :], st)
        o = _mm(q_ref[pl.ds(r0, c), :], st) + _mm(qk_ref[pl.ds(r0, c), :], v_new)
        st_ref[...] = st * gb_ref[pl.ds(r0, 1), :] + _mm_tn(k_ref[pl.ds(r0, c), :], v_new)
        zc = z_ref[pl.ds(r0, c), :]
        o_ref[pl.ds(r0, c), :] = _rms(o, gnorm) * _silu(zc)
        return carry

    lax.fori_loop(0, ts // c, chunk_body, 0)


def _deltanet(p32, conv_w, a_log, dt_bias, a_norm_g, *, ts, cols):
    bsz, s, _ = p32.shape
    d = HEAD_DIM
    nh = N_HEADS
    qb, kb, vb, zb, smb = cols["qa"], cols["ka"], cols["va"], cols["za"], cols["small"]
    pad = lambda t: jnp.pad(t.astype(F32), (0, d - t.shape[0])).reshape(1, d)
    kernel = functools.partial(_deltanet_kernel, ts=ts, a_col=cols["a_lane"], b_col=cols["b_lane"])
    return pl.pallas_call(
        kernel,
        grid=(bsz, nh, s // ts),
        in_specs=[pl.BlockSpec((None, ts, d), lambda b, h, i: (b, i, qb + h)),
                  pl.BlockSpec((None, ts, d), lambda b, h, i: (b, i, kb + h)),
                  pl.BlockSpec((None, ts, d), lambda b, h, i: (b, i, vb + h)),
                  pl.BlockSpec((None, ts, d), lambda b, h, i: (b, i, zb + h)),
                  pl.BlockSpec((None, ts, d), lambda b, h, i: (b, i, smb)),
                  pl.BlockSpec((CONV_WIDTH, d), lambda b, h, i: (0, h)),
                  pl.BlockSpec((CONV_WIDTH, d), lambda b, h, i: (0, nh + h)),
                  pl.BlockSpec((CONV_WIDTH, d), lambda b, h, i: (0, 2 * nh + h)),
                  pl.BlockSpec((1, d), lambda b, h, i: (0, 0)),
                  pl.BlockSpec((1, d), lambda b, h, i: (0, 0)),
                  pl.BlockSpec((1, d), lambda b, h, i: (0, 0))],
        out_specs=pl.BlockSpec((None, ts, d), lambda b, h, i: (b, i, h)),
        out_shape=jax.ShapeDtypeStruct((bsz, s, nh * d), F32),
        scratch_shapes=[pltpu.VMEM((3, ts + 8, d), F32),
                        pltpu.VMEM((ts, d), F32), pltpu.VMEM((ts, d), F32), pltpu.VMEM((ts, d), F32),
                        pltpu.VMEM((ts, d), F32), pltpu.VMEM((ts, d), F32),
                        pltpu.VMEM((ts, d), F32), pltpu.VMEM((ts, d), F32),
                        pltpu.VMEM((ts, CHUNK), F32),
                        pltpu.VMEM((d, d), F32)],
        compiler_params=pltpu.CompilerParams(
            dimension_semantics=("parallel", "parallel", "arbitrary"), vmem_limit_bytes=VMEM_LIMIT),
        name="deltanet",
    )(p32, p32, p32, p32, p32, conv_w.astype(F32), conv_w.astype(F32), conv_w.astype(F32),
      pad(a_log), pad(dt_bias), a_norm_g.astype(F32).reshape(1, d))


def _hgrn2_kernel(q_ref, f_ref, i_ref, gate_ref, lb_ref, gn_ref, o_ref,
                  qs_ref, ks_ref, gc_ref, st_ref, *, ts):
    s = pl.program_id(2)
    c = CHUNK
    d = HEAD_DIM
    SUB = 16

    @pl.when(s == 0)
    def _():
        st_ref[...] = jnp.zeros_like(st_ref)

    lb = lb_ref[...]
    f_raw = f_ref[...]
    log_sig = jnp.minimum(f_raw, 0.0) - jnp.log1p(jnp.exp(-jnp.abs(f_raw)))
    la = jnp.log(lb)
    lbb = jnp.log1p(-lb) + log_sig
    log_f = jnp.maximum(la, lbb) + jnp.log1p(jnp.exp(-jnp.abs(la - lbb)))
    qs_ref[...] = _silu(q_ref[...])
    ks_ref[...] = (1.0 - lb) * _sigmoid(-f_raw)

    row = _iota((c, c), 0)
    col = _iota((c, c), 1)
    tri_f = (col <= row).astype(F32)
    ones_dd = jnp.ones((d, d), BF16)
    rows_8d = _iota((8, d), 0)
    gnorm = gn_ref[...]

    for ci in range(ts // c):
        gc_ref[ci * c:(ci + 1) * c, :] = _mm_f32(tri_f, log_f[ci * c:(ci + 1) * c, :])

    def chunk_loop(ci, carry):
        r0 = pl.multiple_of(ci * c, c)
        q = qs_ref[pl.ds(r0, c), :]
        k = ks_ref[pl.ds(r0, c), :]
        v = i_ref[pl.ds(r0, c), :]
        gc = gc_ref[pl.ds(r0, c), :]

        blocks = [(sb * SUB, (sb + 1) * SUB) for sb in range(c // SUB)]
        prods = []
        for top, end in blocks:
            for j in range(top, end):
                lo = (j // 8) * 8
                k_j = ks_ref[pl.ds(r0 + j, 1), :]
                g_j = gc_ref[pl.ds(r0 + j, 1), :]
                e = jnp.exp(jnp.minimum(gc[lo:end, :] - g_j, 0.0))
                if j % 8:
                    head = jnp.where(rows_8d >= j - lo, e[:8], 0.0)
                    e = jnp.concatenate([head, e[8:]], axis=0) if lo + 8 < end else head
                prods.append(q[lo:end, :] * k_j * e)
        sums = jnp.dot(jnp.concatenate(prods, axis=0).astype(BF16), ones_dd,
                       preferred_element_type=F32)
        qk_far = []
        for top, end in blocks[1:]:
            g_b = gc[top - 1:top, :]
            qe = q[top:end, :] * jnp.exp(gc[top:end, :] - g_b)
            ke = k[:top, :] * jnp.exp(jnp.minimum(g_b - gc[:top, :], 0.0))
            qk_far.append(_mm_nt(qe, ke))
        far = [_mm(a, v[:top, :]) for a, (top, _) in zip(qk_far, blocks[1:])]

        groups = [jnp.zeros((8, d), F32) for _ in range(c // 8)]
        at = 0
        for top, end in blocks:
            for j in range(top, end):
                v_j = i_ref[pl.ds(r0 + j, 1), :]
                for g in range(j // 8, end // 8):
                    groups[g] = groups[g] + sums[at:at + 8, :] * v_j
                    at += 8
        for f, (top, end) in zip(far, blocks[1:]):
            for g in range(top // 8, end // 8):
                groups[g] = groups[g] + f[(g * 8 - top):(g * 8 - top + 8), :]
        o_intra = jnp.concatenate(groups, axis=0)

        st = st_ref[...]
        gl = gc[c - 1:c, :]
        o = o_intra + _mm_nt(q * jnp.exp(gc), st)
        st_ref[...] = st * jnp.exp(gl) + _mm_tn(v, k * jnp.exp(gl - gc))
        o_ref[pl.ds(r0, c), :] = _rms(o, gnorm) * _silu(gate_ref[pl.ds(r0, c), :])
        return carry

    lax.fori_loop(0, ts // c, chunk_loop, 0)


def _hgrn2(p32, lb, d_norm_g, *, ts, cols):
    bsz, s, _ = p32.shape
    d = HEAD_DIM
    nh = N_HEADS
    qb, fb, ib, gb = cols["qd"], cols["fd"], cols["id"], cols["gd"]
    kernel = functools.partial(_hgrn2_kernel, ts=ts)
    return pl.pallas_call(
        kernel,
        grid=(bsz, nh, s // ts),
        in_specs=[pl.BlockSpec((None, ts, d), lambda b, h, i: (b, i, qb + h)),
                  pl.BlockSpec((None, ts, d), lambda b, h, i: (b, i, fb + h)),
                  pl.BlockSpec((None, ts, d), lambda b, h, i: (b, i, ib + h)),
                  pl.BlockSpec((None, ts, d), lambda b, h, i: (b, i, gb + h)),
                  pl.BlockSpec((1, d), lambda b, h, i: (0, h)),
                  pl.BlockSpec((1, d), lambda b, h, i: (0, 0))],
        out_specs=pl.BlockSpec((None, ts, d), lambda b, h, i: (b, i, h)),
        out_shape=jax.ShapeDtypeStruct((bsz, s, nh * d), F32),
        scratch_shapes=[pltpu.VMEM((ts, d), F32), pltpu.VMEM((ts, d), F32),
                        pltpu.VMEM((ts, d), F32), pltpu.VMEM((d, d), F32)],
        compiler_params=pltpu.CompilerParams(
            dimension_semantics=("parallel", "parallel", "arbitrary"), vmem_limit_bytes=VMEM_LIMIT),
        name="hgrn2",
    )(p32, p32, p32, p32, lb.astype(F32).reshape(1, nh * d), d_norm_g.astype(F32).reshape(1, d))


def _stickbreak_kernel(q_ref, k_ref, v_ref, o_ref, *, tq):
    i = pl.program_id(1)
    d = HEAD_DIM
    nh = N_HEADS
    row = _iota((tq, tq), 0)
    col = _iota((tq, tq), 1)
    causal = col < row
    later = (row > col).astype(BF16)

    def block(j, carries, diag):
        r0 = pl.multiple_of(j * tq, tq)
        out = []
        for hh in range(nh):
            hs = slice(hh * d, (hh + 1) * d)
            z = _mm_nt(q_ref[:, hs], k_ref[pl.ds(r0, tq), hs]) * (d ** -0.5)
            sp = _softplus(z)
            l1m = jnp.where(causal, -sp, 0.0) if diag else -sp
            l_hi, l_lo = _split(l1m)
            rest = (jnp.dot(l_hi, later, preferred_element_type=F32)
                    + jnp.dot(l_lo, later, preferred_element_type=F32))
            p = jnp.exp((z - sp) + rest + carries[hh])
            if diag:
                p = jnp.where(causal, p, 0.0)
            pv = _mm(p, v_ref[pl.ds(r0, tq), hs])
            if diag:
                o_ref[:, hs] = pv
            else:
                o_ref[:, hs] += pv
            out.append(carries[hh] + jnp.sum(l1m, axis=-1, keepdims=True))
        return tuple(out)

    carries = block(i, tuple(jnp.zeros((tq, 1), F32) for _ in range(nh)), True)

    def cond(c):
        worst = functools.reduce(jnp.maximum, c[1])
        return jnp.logical_and(c[0] >= 0, jnp.max(worst) >= EXP_ZERO_BELOW)

    def body(c):
        return c[0] - 1, block(c[0], c[1], False)

    lax.while_loop(cond, body, (i - 1, carries))


def _stickbreak(p16, *, tq, cols):
    bsz, s, _ = p16.shape
    nh = N_HEADS
    w = nh * HEAD_DIM
    kernel = functools.partial(_stickbreak_kernel, tq=tq)
    resident = dict(pipeline_mode=pl.Buffered(1))
    return pl.pallas_call(
        kernel,
        grid=(bsz, s // tq),
        in_specs=[pl.BlockSpec((None, tq, w), lambda b, i: (b, i, cols["qc"] // nh)),
                  pl.BlockSpec((None, s, w), lambda b, i: (b, 0, cols["kc"] // nh), **resident),
                  pl.BlockSpec((None, s, w), lambda b, i: (b, 0, cols["vc"] // nh), **resident)],
        out_specs=pl.BlockSpec((None, tq, w), lambda b, i: (b, i, 0)),
        out_shape=jax.ShapeDtypeStruct((bsz, s, w), F32),
        compiler_params=pltpu.CompilerParams(
            dimension_semantics=("parallel", "arbitrary"), vmem_limit_bytes=VMEM_LIMIT),
        name="stickbreak",
    )(p16, p16, p16)


def _dsa_kernel(qi_ref, smq_ref, q_ref, sm_ref, k_ref, v_ref, bias_ref, o_ref,
                sc_ref, wb_ref, qc_ref, kct_ref, q2_ref, lg_ref, l_ref, acc_ref, *, tq, k_sel, wi_lane, wide):
    i = pl.program_id(1)
    tk = tq
    d = HEAD_DIM
    nh = N_HEADS
    ksel = float(k_sel)
    per_wide = wide // tk
    n_wide = (i + per_wide) // per_wide
    sub = 2 * tk

    q2_ref[...] = (q_ref[...] * ((d ** -0.5) * LOG2E)).astype(BF16)

    @pl.when(i == 0)
    def _():
        def prep(g, carry):
            g0 = pl.multiple_of(g * wide, wide)
            kt = sm_ref[pl.ds(g0, wide), :].T[:IDX_DIM, :]
            hi, lo = _split(kt)
            kct_ref[:, pl.ds(g0, wide)] = jnp.concatenate([hi, lo, hi], axis=0)
            return carry
        lax.fori_loop(0, sm_ref.shape[0] // wide, prep, 0)

    smq = smq_ref[...]
    lane = _iota(smq.shape, 1)
    for hh in range(IDX_HEADS):
        qh = qi_ref[:, hh * IDX_DIM:(hh + 1) * IDX_DIM]
        hi = qh.astype(BF16)
        lo = (qh - hi.astype(F32)).astype(BF16)
        qc_ref[hh] = jnp.concatenate([hi, hi, lo], axis=-1)
        w = jnp.sum(jnp.where(lane == wi_lane + hh, smq, 0.0), axis=-1, keepdims=True)
        wb_ref[hh] = jnp.broadcast_to(w * ((IDX_HEADS ** -0.5) * (IDX_DIM ** -0.5)), (tq, tk))

    def fold(x, op=jnp.add):
        acc = x[:, 0:tk]
        for pb in range(1, x.shape[1] // tk):
            acc = op(acc, x[:, pb * tk:(pb + 1) * tk])
        return acc

    limit = i * tq + (_iota((tq, sub), 0) // CHUNK + 1) * CHUNK
    col_s = _iota((tq, sub), 1)

    def score_group(g, mm, masked):
        mn, mx = mm
        for sb in range(wide // sub):
            k0 = pl.multiple_of(g * wide + sb * sub, sub)
            kct = kct_ref[:, pl.ds(k0, sub)]
            tiles = [jnp.zeros((tq, tk), F32) for _ in range(sub // tk)]
            for hh in range(IDX_HEADS):
                s_h = jnp.dot(qc_ref[hh], kct, preferred_element_type=F32)
                for ti in range(sub // tk):
                    tiles[ti] = tiles[ti] + jnp.maximum(s_h[:, ti * tk:(ti + 1) * tk], 0.0) * wb_ref[hh]
            sc = jnp.concatenate(tiles, axis=-1)
            if masked:
                adm = (k0 + col_s) < limit
                sc_ref[:, pl.ds(k0, sub)] = jnp.where(adm, sc, -jnp.inf)
                mn = jnp.minimum(mn, fold(jnp.where(adm, sc, jnp.inf), jnp.minimum))
                mx = jnp.maximum(mx, fold(jnp.where(adm, sc, -jnp.inf), jnp.maximum))
            else:
                sc_ref[:, pl.ds(k0, sub)] = sc
                mn = jnp.minimum(mn, fold(sc, jnp.minimum))
                mx = jnp.maximum(mx, fold(sc, jnp.maximum))
        return mn, mx

    mm = lax.fori_loop(0, n_wide - 1, functools.partial(score_group, masked=False),
                       (jnp.full((tq, tk), jnp.inf, F32), jnp.full((tq, tk), -jnp.inf, F32)))
    mn, mx = score_group(n_wide - 1, mm, True)
    rmin = jnp.min(mn, axis=-1, keepdims=True)
    rmax = jnp.max(mx, axis=-1, keepdims=True)


    def count(pred):
        def body(g, acc):
            g0 = pl.multiple_of(g * wide, wide)
            blk = sc_ref[:, pl.ds(g0, wide)]
            return acc + fold(pred(blk, g0))
        acc = lax.fori_loop(0, n_wide, body, jnp.zeros((tq, tk), F32))
        return jnp.sum(acc, axis=-1, keepdims=True)

    def max_below(x):
        def body(g, acc):
            blk = sc_ref[:, pl.ds(pl.multiple_of(g * wide, wide), wide)]
            mb = jnp.where(blk < x, blk, -jnp.inf)
            m = mb[:, 0:tk]
            for pb in range(1, per_wide):
                m = jnp.maximum(m, mb[:, pb * tk:(pb + 1) * tk])
            return jnp.maximum(acc, m)
        acc = lax.fori_loop(0, n_wide, body, jnp.full((tq, tk), -jnp.inf, F32))
        return jnp.max(acc, axis=-1, keepdims=True)

    rows1 = _iota((tq, 1), 0)
    n_adm = (i * tq + (rows1 // CHUNK + 1) * CHUNK).astype(F32)
    all_sel = n_adm <= ksel
    last_idx = float(sc_ref.shape[1])
    j_all = jnp.where(all_sel, -1.0, last_idx)

    def bisect(c):
        lo, hi, c_lo = c
        mid = 0.5 * lo + 0.5 * hi
        cm = count(lambda blk, g0: _ind(blk >= mid))
        ge = cm >= ksel
        return jnp.where(ge, mid, lo), jnp.where(ge, hi, mid), jnp.where(ge, cm, c_lo)

    def pending(c_lo):
        return jnp.where(all_sel, 0.0, _ind(c_lo != ksel))

    hi0 = rmax + (jnp.abs(rmax) * (2.0 ** -20) + 1e-30)
    state = lax.fori_loop(0, BISECT_FIXED, lambda _, c: bisect(c), (rmin, hi0, n_adm))

    def more_cond(c):
        return jnp.logical_and(c[0] < BISECT_EXTRA, jnp.max(pending(c[1][2])) > 0.5)

    def more_body(c):
        return c[0] + 1, bisect(c[1])

    _, (lo_f, hi_f, c_lo_f) = lax.while_loop(more_cond, more_body, (jnp.int32(0), state))
    unresolved = pending(c_lo_f)
    v_fast = jnp.where(all_sel, -jnp.inf, lo_f)

    def slow_path(_):
        def search_cond(c):
            return jnp.min(c[3]) < 0.5

        def search_body(c):
            lo, hi, v, done = c
            cand = max_below(hi)
            ok = count(lambda blk, g0: _ind(blk >= cand)) >= ksel
            v = jnp.where(done > 0.5, v, cand)
            done = jnp.where(ok, 1.0, done)
            lo, hi, _ = lax.fori_loop(0, 6, lambda _, s: bisect(s), (lo, hi, c_lo_f))
            return lo, hi, v, done

        _, _, vth, _ = lax.while_loop(search_cond, search_body,
                                      (lo_f, hi_f, v_fast, 1.0 - unresolved))
        c_gt = count(lambda blk, g0: _ind(blk > vth))
        need = ksel - c_gt

        def bis(_, lh):
            lo, hi = lh
            mid = jnp.floor(0.5 * (lo + hi))

            def pred(blk, g0):
                idx = (g0 + _iota(blk.shape, 1)).astype(F32)
                return jnp.where(blk == vth, _ind(idx <= mid), 0.0)

            ge = count(pred) >= need
            return jnp.where(ge, lo, mid), jnp.where(ge, mid, hi)

        n_steps = int(math.ceil(math.log2(sc_ref.shape[1] + 1))) + 1
        _, j_tie = lax.fori_loop(0, n_steps, bis,
                                 (jnp.full((tq, 1), -1.0, F32), jnp.full((tq, 1), last_idx, F32)))
        return vth, jnp.where(unresolved > 0.5, j_tie, j_all)

    vth, jth = lax.cond(jnp.max(unresolved) > 0.5, slow_path, lambda _: (v_fast, j_all), 0)

    col_w = _iota((tq, wide), 1)
    g_near = jnp.maximum(i - 1, 0) // per_wide

    def logit_group(g, mx, near):
        g0 = pl.multiple_of(g * wide, wide)
        scb = sc_ref[:, pl.ds(g0, wide)]
        idx = (g0 + col_w).astype(F32)
        sel_f = jnp.where(scb == vth, _ind(idx <= jth), _ind(scb > vth))
        mb = jnp.where(sel_f > 0.5, 0.0, NEG_BIG)
        out = []
        for hh in range(nh):
            kh = k_ref[pl.ds(g0, wide), hh * d:(hh + 1) * d]
            lm = lax.dot_general(q2_ref[:, hh * d:(hh + 1) * d], kh, (((1,), (1,)), ((), ())),
                                 preferred_element_type=F32) + mb
            if near:
                back = [jnp.clip(i - (g * per_wide + pb), 0, 2) for pb in range(per_wide)]
                lm = lm + jnp.concatenate([bias_ref[back[pb], hh] for pb in range(per_wide)], axis=-1)
            lg_ref[hh, :, pl.ds(g0, wide)] = lm
            out.append(jnp.maximum(mx[hh], fold(lm, jnp.maximum)))
        return tuple(out)

    mx = tuple(jnp.full((tq, tk), NEG_BIG, F32) for _ in range(nh))
    mx = lax.fori_loop(0, g_near, functools.partial(logit_group, near=False), mx)
    mx = lax.fori_loop(g_near, n_wide, functools.partial(logit_group, near=True), mx)

    m_rows = [jnp.broadcast_to(jnp.max(mx[hh], axis=-1, keepdims=True), (tq, tk)) for hh in range(nh)]
    for hh in range(nh):
        l_ref[hh] = jnp.zeros((tq, tk), F32)
        acc_ref[hh] = jnp.zeros((tq, d), F32)

    def pv_body(g, carry):
        g0 = pl.multiple_of(g * wide, wide)
        for hh in range(nh):
            lm = lg_ref[hh, :, pl.ds(g0, wide)]
            p = jnp.concatenate([jnp.exp2(lm[:, pb * tk:(pb + 1) * tk] - m_rows[hh])
                                 for pb in range(per_wide)], axis=-1)
            vh = v_ref[pl.ds(g0, wide), hh * d:(hh + 1) * d]
            l_ref[hh] += fold(p)
            acc_ref[hh] += jnp.dot(p.astype(BF16), vh, preferred_element_type=F32)
        return carry

    lax.fori_loop(0, n_wide, pv_body, 0)
    for hh in range(nh):
        o_ref[:, hh * d:(hh + 1) * d] = acc_ref[hh] / jnp.sum(l_ref[hh], axis=-1, keepdims=True)


def _dsa(p32, p16, bias_tiles, *, tq, cols):
    bsz, s, _ = p32.shape
    d = HEAD_DIM
    nh = N_HEADS
    wide = 4 * tq
    k_sel = min(TOPK_MAX, s // 4)
    w512 = nh * d
    kernel = functools.partial(_dsa_kernel, tq=tq, k_sel=k_sel, wi_lane=cols["wi_lane"], wide=wide)
    resident = dict(pipeline_mode=pl.Buffered(1))
    return pl.pallas_call(
        kernel,
        grid=(bsz, s // tq),
        in_specs=[pl.BlockSpec((None, tq, w512), lambda b, i: (b, i, cols["qi"] // nh)),
                  pl.BlockSpec((None, tq, d), lambda b, i: (b, i, cols["small"])),
                  pl.BlockSpec((None, tq, w512), lambda b, i: (b, i, cols["qb"] // nh)),
                  pl.BlockSpec((None, s, d), lambda b, i: (b, 0, cols["small"]), **resident),
                  pl.BlockSpec((None, s, w512), lambda b, i: (b, 0, cols["kb"] // nh), **resident),
                  pl.BlockSpec((None, s, w512), lambda b, i: (b, 0, cols["vb"] // nh), **resident),
                  pl.BlockSpec((3, nh, tq, tq), lambda b, i: (0, 0, 0, 0), **resident)],
        out_specs=pl.BlockSpec((None, tq, w512), lambda b, i: (b, i, 0)),
        out_shape=jax.ShapeDtypeStruct((bsz, s, w512), F32),
        scratch_shapes=[pltpu.VMEM((tq, s + wide - tq), F32),
                        pltpu.VMEM((IDX_HEADS, tq, tq), F32),
                        pltpu.VMEM((IDX_HEADS, tq, 3 * IDX_DIM), BF16),
                        pltpu.VMEM((3 * IDX_DIM, s), BF16),
                        pltpu.VMEM((tq, w512), BF16),
                        pltpu.VMEM((nh, tq, s), F32),
                        pltpu.VMEM((nh, tq, tq), F32), pltpu.VMEM((nh, tq, d), F32)],
        compiler_params=pltpu.CompilerParams(
            dimension_semantics=("parallel", "arbitrary"), vmem_limit_bytes=VMEM_LIMIT),
        name="dsa",
    )(p32, p32, p32, p32, p16, p16, bias_tiles)


def _t5_bucket(rel):
    nb = REL_BUCKETS // 2
    max_exact = nb // 2
    ret = jnp.where(rel > 0, nb, 0)
    n = jnp.abs(rel)
    large = max_exact + (jnp.log(jnp.maximum(n, 1).astype(F32) / max_exact)
                         / math.log(REL_MAX_DIST / max_exact) * (nb - max_exact)).astype(jnp.int32)
    large = jnp.minimum(large, nb - 1)
    return ret + jnp.where(n < max_exact, n, large)


def _bias_tiles(rel_table, tq):
    assert tq >= REL_MAX_DIST
    t = jnp.arange(tq)
    tiles = []
    for back in range(3):
        rel = (t[None, :] - back * tq) - t[:, None]
        tiles.append(rel_table.astype(F32)[_t5_bucket(rel)].transpose(2, 0, 1))
    tiles = jnp.stack(tiles)
    return (tiles - tiles[2:3]) * LOG2E


def _even_layout(w_in):
    d = HEAD_DIM
    a_w = 2 * N_HEADS * d + N_HEADS * d
    offs = {}
    o = 0
    for name, w in (("qkv", a_w), ("z", N_HEADS * d), ("a", N_HEADS), ("b", N_HEADS),
                    ("qb", N_HEADS * d), ("kb", N_HEADS * d), ("vb", N_HEADS * d),
                    ("qi", IDX_HEADS * IDX_DIM), ("ki", IDX_DIM), ("wi", IDX_HEADS)):
        offs[name] = (o, o + w)
        o += w
    assert o == w_in.shape[1]
    sl = lambda n: w_in[:, offs[n][0]:offs[n][1]]
    small_w = IDX_DIM + 2 * N_HEADS + IDX_HEADS
    small_pad = -small_w % d
    w = jnp.concatenate([sl("qkv"), sl("z"), sl("qb"), sl("kb"), sl("vb"), sl("qi"),
                         sl("ki"), sl("a"), sl("b"), sl("wi"),
                         jnp.zeros((w_in.shape[0], small_pad), w_in.dtype)], axis=1)
    nh = N_HEADS
    cols = dict(qa=0, ka=nh, va=2 * nh, za=3 * nh, qb=4 * nh, kb=5 * nh, vb=6 * nh, qi=7 * nh,
                small=8 * nh, a_lane=IDX_DIM, b_lane=IDX_DIM + nh, wi_lane=IDX_DIM + 2 * nh)
    return w.astype(BF16), cols


def kernel(x, norm_g, w_in_even, conv_w_even, a_log_even, dt_bias_even, a_norm_even, w_out_even,
           rel_bias, w_in_odd, lb_logits, d_norm_odd, w_out_odd, w_gate, w_up, w_down):
    bsz, s, d = x.shape
    t = bsz * s
    depth = norm_g.shape[0]
    nh = N_HEADS
    tq = 128
    lb_all = jnp.cumsum(jax.nn.softmax(lb_logits.astype(F32), axis=0), axis=0)
    lb_all = lb_all - lb_all[:1]
    odd_cols = dict(qc=0, kc=nh, vc=2 * nh, qd=3 * nh, fd=4 * nh, id=5 * nh, gd=6 * nh)
    bias_tiles = _bias_tiles(rel_bias, tq)

    h = x.reshape(t, d)
    for l in range(depth):
        if l % 2 == 0:
            e = l // 2
            w_even, cols = _even_layout(w_in_even[e])
            p32, p16 = _norm_matmul(h, norm_g[l, 0], w_even, tm=512, tn=w_even.shape[1] // 3)
            p32 = p32.reshape(bsz, s, -1)
            p16 = p16.reshape(bsz, s, -1)
            o_1 = _deltanet(p32, conv_w_even[e], a_log_even[e], dt_bias_even[e], a_norm_even[e],
                            ts=min(512, s), cols=cols)
            o_2 = _dsa(p32, p16, bias_tiles, tq=tq, cols=cols)
            w_out = w_out_even[e]
        else:
            o = l // 2
            p32, p16 = _norm_matmul(h, norm_g[l, 0], w_in_odd[o].astype(BF16), tm=512, tn=512)
            p32 = p32.reshape(bsz, s, -1)
            p16 = p16.reshape(bsz, s, -1)
            o_1 = _stickbreak(p16, tq=tq, cols=odd_cols)
            o_2 = _hgrn2(p32, lb_all[l], d_norm_odd[o], ts=min(512, s), cols=odd_cols)
            w_out = w_out_odd[o]
        h = _outproj(o_1.reshape(t, -1), o_2.reshape(t, -1), w_out, h, norm_g[l, 1], tm=512)
        h = _ffn(h, norm_g[l, 2], norm_g[l, 3], w_gate[l], w_up[l], w_down[l], tm=1024, tf=256)
    return h.reshape(bsz, s, d)
```

```python
import functools
import math

import jax
import jax.numpy as jnp
from jax import lax
from jax.experimental import pallas as pl
from jax.experimental.pallas import tpu as pltpu

F32 = jnp.float32
BF16 = jnp.bfloat16
HIGHEST = lax.Precision.HIGHEST

CHUNK = 64
HEAD_DIM = 128
N_HEADS = 4
IDX_HEADS = 8
IDX_DIM = 64
TOPK_MAX = 256
CONV_WIDTH = 4
REL_BUCKETS = 32
REL_MAX_DIST = 128
EPS = 1e-6
NEG_BIG = -1e30
LOG2E = 1.4426950408889634
BISECT_COARSE = 12
BISECT_FIXED = 4
BISECT_EXTRA = 20
EXP_ZERO_BELOW = -104.0
VMEM_LIMIT = 56 * 1024 * 1024


def _mm(a, b):
    return jnp.dot(a.astype(BF16), b.astype(BF16), preferred_element_type=F32)


def _mm_nt(a, b):
    return lax.dot_general(a.astype(BF16), b.astype(BF16), (((1,), (1,)), ((), ())),
                           preferred_element_type=F32)


def _mm_tn(a, b):
    return lax.dot_general(a.astype(BF16), b.astype(BF16), (((0,), (0,)), ((), ())),
                           preferred_element_type=F32)


def _mm_f32(a, b):
    return jnp.dot(a, b, precision=HIGHEST, preferred_element_type=F32)


def _split(x):
    hi = x.astype(BF16)
    return hi, (x - hi.astype(F32)).astype(BF16)


def _mm_x3(a, b):
    a_hi, a_lo = _split(a)
    b_hi, b_lo = _split(b)
    return jnp.dot(jnp.concatenate([a_hi, a_hi, a_lo], axis=1),
                   jnp.concatenate([b_hi, b_lo, b_hi], axis=0), preferred_element_type=F32)


def _floor_bf16(x):
    bits = pltpu.bitcast(x, jnp.int32)
    down = jnp.where(bits >= 0, bits, bits + 0xFFFF) & jnp.int32(-65536)
    return pltpu.bitcast(down, F32).astype(BF16)


def _sigmoid(x):
    return 1.0 / (1.0 + jnp.exp(-x))


def _silu(x):
    return x * _sigmoid(x)


def _softplus(x):
    return jnp.maximum(x, 0.0) + jnp.log1p(jnp.exp(-jnp.abs(x)))


def _rms(x, g):
    return x * lax.rsqrt(jnp.mean(x * x, axis=-1, keepdims=True) + EPS) * g


def _iota(shape, dim):
    return lax.broadcasted_iota(jnp.int32, shape, dim)


def _ind(mask):
    return jnp.where(mask, 1.0, 0.0)


def _norm_matmul_kernel(x_ref, g_ref, w_ref, o32_ref, o16_ref, xn_ref):
    @pl.when(pl.program_id(1) == 0)
    def _():
        xn_ref[...] = _rms(x_ref[...], g_ref[...]).astype(BF16)

    y = jnp.dot(xn_ref[...], w_ref[...], preferred_element_type=F32)
    o32_ref[...] = y
    o16_ref[...] = y.astype(BF16)


def _norm_matmul(x, g, w, *, tm, tn):
    t, d = x.shape
    n = w.shape[1]
    return pl.pallas_call(
        _norm_matmul_kernel,
        grid=(t // tm, n // tn),
        in_specs=[pl.BlockSpec((tm, d), lambda i, j: (i, 0)),
                  pl.BlockSpec((1, d), lambda i, j: (0, 0)),
                  pl.BlockSpec((d, tn), lambda i, j: (0, j))],
        out_specs=[pl.BlockSpec((tm, tn), lambda i, j: (i, j)),
                   pl.BlockSpec((tm, tn), lambda i, j: (i, j))],
        out_shape=[jax.ShapeDtypeStruct((t, n), F32), jax.ShapeDtypeStruct((t, n), BF16)],
        scratch_shapes=[pltpu.VMEM((tm, d), BF16)],
        compiler_params=pltpu.CompilerParams(
            dimension_semantics=("parallel", "arbitrary"), vmem_limit_bytes=VMEM_LIMIT),
        name="norm_matmul",
    )(x, g.reshape(1, d), w)


def _outproj_kernel(ca_ref, cb_ref, wa_ref, wb_ref, h_ref, g_ref, o_ref):
    y = (jnp.dot(ca_ref[...].astype(BF16), wa_ref[...], preferred_element_type=F32)
         + jnp.dot(cb_ref[...].astype(BF16), wb_ref[...], preferred_element_type=F32))
    o_ref[...] = h_ref[...] + _rms(y, g_ref[...])


def _outproj(ca, cb, w, h, g, *, tm):
    t, d = h.shape
    wa_n = ca.shape[1]
    wb_n = cb.shape[1]
    wa = w[:wa_n].astype(BF16)
    wb = w[wa_n:].astype(BF16)
    return pl.pallas_call(
        _outproj_kernel,
        grid=(t // tm,),
        in_specs=[pl.BlockSpec((tm, wa_n), lambda i: (i, 0)),
                  pl.BlockSpec((tm, wb_n), lambda i: (i, 0)),
                  pl.BlockSpec((wa_n, d), lambda i: (0, 0)),
                  pl.BlockSpec((wb_n, d), lambda i: (0, 0)),
                  pl.BlockSpec((tm, d), lambda i: (i, 0)),
                  pl.BlockSpec((1, d), lambda i: (0, 0))],
        out_specs=pl.BlockSpec((tm, d), lambda i: (i, 0)),
        out_shape=jax.ShapeDtypeStruct((t, d), F32),
        compiler_params=pltpu.CompilerParams(
            dimension_semantics=("parallel",), vmem_limit_bytes=VMEM_LIMIT),
        name="outproj",
    )(ca, cb, wa, wb, h, g.reshape(1, d))


def _ffn_kernel(h_ref, gpre_ref, gpost_ref, wg_ref, wu_ref, wd_ref, o_ref, xn_ref, acc_ref):
    f = pl.program_id(1)

    @pl.when(f == 0)
    def _():
        xn_ref[...] = _rms(h_ref[...], gpre_ref[...]).astype(BF16)
        acc_ref[...] = jnp.zeros_like(acc_ref)

    xn = xn_ref[...]
    gate = jnp.dot(xn, wg_ref[...], preferred_element_type=F32)
    up = jnp.dot(xn, wu_ref[...], preferred_element_type=F32)
    act = (_silu(gate) * up).astype(BF16)
    acc_ref[...] += jnp.dot(act, wd_ref[...], preferred_element_type=F32)

    @pl.when(f == pl.num_programs(1) - 1)
    def _():
        o_ref[...] = h_ref[...] + _rms(acc_ref[...], gpost_ref[...])


def _ffn(h, g_pre, g_post, wg, wu, wd, *, tm, tf):
    t, d = h.shape
    ff = wg.shape[1]
    return pl.pallas_call(
        _ffn_kernel,
        grid=(t // tm, ff // tf),
        in_specs=[pl.BlockSpec((tm, d), lambda i, f: (i, 0)),
                  pl.BlockSpec((1, d), lambda i, f: (0, 0)),
                  pl.BlockSpec((1, d), lambda i, f: (0, 0)),
                  pl.BlockSpec((d, tf), lambda i, f: (0, f)),
                  pl.BlockSpec((d, tf), lambda i, f: (0, f)),
                  pl.BlockSpec((tf, d), lambda i, f: (f, 0))],
        out_specs=pl.BlockSpec((tm, d), lambda i, f: (i, 0)),
        out_shape=jax.ShapeDtypeStruct((t, d), F32),
        scratch_shapes=[pltpu.VMEM((tm, d), BF16), pltpu.VMEM((tm, d), F32)],
        compiler_params=pltpu.CompilerParams(
            dimension_semantics=("parallel", "arbitrary"), vmem_limit_bytes=VMEM_LIMIT),
        name="ffn",
    )(h, g_pre.reshape(1, d), g_post.reshape(1, d),
      wg.astype(BF16), wu.astype(BF16), wd.astype(BF16))


def _deltanet_kernel(xq_ref, xk_ref, xv_ref, z_ref, sm_ref, cwq_ref, cwk_ref, cwv_ref,
                     alog_ref, dtb_ref, gn_ref, o_ref,
                     xpad_ref, q_ref, k_ref, v_ref, gb_ref, bb_ref, u_ref, w_ref, qk_ref, st_ref,
                     *, ts, a_col, b_col):
    h = pl.program_id(1)
    s = pl.program_id(2)
    c = CHUNK
    d = HEAD_DIM

    @pl.when(s == 0)
    def _():
        xpad_ref[:, 0:8, :] = jnp.zeros((3, 8, d), F32)
        st_ref[...] = jnp.zeros_like(st_ref)

    @pl.when(s != 0)
    def _():
        xpad_ref[:, 0:8, :] = xpad_ref[:, ts:ts + 8, :]

    xpad_ref[0, 8:ts + 8, :] = xq_ref[...]
    xpad_ref[1, 8:ts + 8, :] = xk_ref[...]
    xpad_ref[2, 8:ts + 8, :] = xv_ref[...]

    def conv_silu(idx, cw_ref):
        cw = cw_ref[...]
        acc = xpad_ref[idx, 8 - (CONV_WIDTH - 1):8 - (CONV_WIDTH - 1) + ts, :] * cw[0:1, :]
        for j in range(1, CONV_WIDTH):
            off = 8 - (CONV_WIDTH - 1) + j
            acc = acc + xpad_ref[idx, off:off + ts, :] * cw[j:j + 1, :]
        return _silu(acc)

    def l2norm(t):
        return t * lax.rsqrt(jnp.sum(t * t, axis=-1, keepdims=True) + EPS)

    q_ref[...] = l2norm(conv_silu(0, cwq_ref)) * (d ** -0.5)
    k_ref[...] = l2norm(conv_silu(1, cwk_ref))
    v_ref[...] = conv_silu(2, cwv_ref)

    sm = sm_ref[...]
    lane = _iota(sm.shape, 1)
    a_raw = jnp.sum(jnp.where(lane == a_col + h, sm, 0.0), axis=-1, keepdims=True)
    b_raw = jnp.sum(jnp.where(lane == b_col + h, sm, 0.0), axis=-1, keepdims=True)
    hl = _iota((1, d), 1)
    a_log = jnp.sum(jnp.where(hl == h, alog_ref[...], 0.0), axis=-1, keepdims=True)
    dtb = jnp.sum(jnp.where(hl == h, dtb_ref[...], 0.0), axis=-1, keepdims=True)
    g = -jnp.exp(a_log) * _softplus(a_raw + dtb)
    gb_ref[...] = jnp.broadcast_to(g, (ts, d))
    bb_ref[...] = jnp.broadcast_to(_sigmoid(b_raw), (ts, d))

    row = _iota((c, c), 0)
    col = _iota((c, c), 1)
    tri = (col <= row)
    strict = (col < row)
    tri_f = tri.astype(F32)
    upper_f = (row <= col).astype(F32)
    eye = (row == col).astype(F32)
    ones_cc = jnp.ones((c, c), F32)
    gnorm = gn_ref[...]

    chunks = range(ts // c)
    rs = [slice(ci * c, (ci + 1) * c) for ci in chunks]
    tri2 = jnp.concatenate([tri_f, tri_f], axis=1).astype(BF16)
    ones2 = jnp.ones((c, 2 * c), BF16)

    def cum2(lhs2, x):
        hi, lo = _split(x)
        return jnp.dot(lhs2, jnp.concatenate([hi, lo], axis=0), preferred_element_type=F32)

    q = [q_ref[r, :] for r in rs]
    k = [k_ref[r, :] for r in rs]
    beta = [bb_ref[r, :] for r in rs]
    gb = [gb_ref[r, :] for r in rs]
    gc = [cum2(tri2, x) for x in gb]
    gc_row = [cum2(ones2, x[:, :c] * upper_f) for x in gb]
    decay = [jnp.where(tri, jnp.exp(jnp.minimum(a[:, :c] - b, 0.0)), 0.0) for a, b in zip(gc, gc_row)]
    kk = [_mm_nt(x, x) for x in k]
    n = [-jnp.where(strict, b[:, :c] * x * dc, 0.0) for b, x, dc in zip(beta, kk, decay)]
    inv = [eye + x for x in n]
    for _ in range(5):
        n = [_mm_x3(x, x) for x in n]
        inv = [iv + _mm_x3(iv, x) for iv, x in zip(inv, n)]
    egc = [jnp.exp(x) for x in gc]
    gl = [x[c - 1:c, :] for x in gc]
    for ci in chunks:
        r = rs[ci]
        u_ref[r, :] = _mm_x3(inv[ci], v_ref[r, :] * beta[ci])
        w_ref[r, :] = _mm_x3(inv[ci], k[ci] * (beta[ci] * egc[ci]))
        qk_ref[r, :] = _mm_nt(q[ci], k[ci]) * decay[ci]
        q_ref[r, :] = q[ci] * egc[ci]
        k_ref[r, :] = k[ci] * jnp.exp(gl[ci] - gc[ci])
        gb_ref[r, :] = jnp.broadcast_to(jnp.exp(gl[ci]), (c, d))

    def chunk_body(ci, carry):
        r0 = pl.multiple_of(ci * c, c)
        st = st_ref[...]
        v_new = u_ref[pl.ds(r0, c), :] - _mm(w_ref[pl.ds(r0, c), :], st)
        o = _mm(q_ref[pl.ds(r0, c), :], st) + _mm(qk_ref[pl.ds(r0, c), :], v_new)
        st_ref[...] = st * gb_ref[pl.ds(r0, 1), :] + _mm_tn(k_ref[pl.ds(r0, c), :], v_new)
        zc = z_ref[pl.ds(r0, c), :]
        o_ref[pl.ds(r0, c), :] = _rms(o, gnorm) * _silu(zc)
        return carry

    lax.fori_loop(0, ts // c, chunk_body, 0)


def _deltanet(p32, conv_w, a_log, dt_bias, a_norm_g, *, ts, cols):
    bsz, s, _ = p32.shape
    d = HEAD_DIM
    nh = N_HEADS
    qb, kb, vb, zb, smb = cols["qa"], cols["ka"], cols["va"], cols["za"], cols["small"]
    pad = lambda t: jnp.pad(t.astype(F32), (0, d - t.shape[0])).reshape(1, d)
    kernel = functools.partial(_deltanet_kernel, ts=ts, a_col=cols["a_lane"], b_col=cols["b_lane"])
    return pl.pallas_call(
        kernel,
        grid=(bsz, nh, s // ts),
        in_specs=[pl.BlockSpec((None, ts, d), lambda b, h, i: (b, i, qb + h)),
                  pl.BlockSpec((None, ts, d), lambda b, h, i: (b, i, kb + h)),
                  pl.BlockSpec((None, ts, d), lambda b, h, i: (b, i, vb + h)),
                  pl.BlockSpec((None, ts, d), lambda b, h, i: (b, i, zb + h)),
                  pl.BlockSpec((None, ts, d), lambda b, h, i: (b, i, smb)),
                  pl.BlockSpec((CONV_WIDTH, d), lambda b, h, i: (0, h)),
                  pl.BlockSpec((CONV_WIDTH, d), lambda b, h, i: (0, nh + h)),
                  pl.BlockSpec((CONV_WIDTH, d), lambda b, h, i: (0, 2 * nh + h)),
                  pl.BlockSpec((1, d), lambda b, h, i: (0, 0)),
                  pl.BlockSpec((1, d), lambda b, h, i: (0, 0)),
                  pl.BlockSpec((1, d), lambda b, h, i: (0, 0))],
        out_specs=pl.BlockSpec((None, ts, d), lambda b, h, i: (b, i, h)),
        out_shape=jax.ShapeDtypeStruct((bsz, s, nh * d), F32),
        scratch_shapes=[pltpu.VMEM((3, ts + 8, d), F32),
                        pltpu.VMEM((ts, d), F32), pltpu.VMEM((ts, d), F32), pltpu.VMEM((ts, d), F32),
                        pltpu.VMEM((ts, d), F32), pltpu.VMEM((ts, d), F32),
                        pltpu.VMEM((ts, d), F32), pltpu.VMEM((ts, d), F32),
                        pltpu.VMEM((ts, CHUNK), F32),
                        pltpu.VMEM((d, d), F32)],
        compiler_params=pltpu.CompilerParams(
            dimension_semantics=("parallel", "parallel", "arbitrary"), vmem_limit_bytes=VMEM_LIMIT),
        name="deltanet",
    )(p32, p32, p32, p32, p32, conv_w.astype(F32), conv_w.astype(F32), conv_w.astype(F32),
      pad(a_log), pad(dt_bias), a_norm_g.astype(F32).reshape(1, d))


def _hgrn2_kernel(q_ref, f_ref, i_ref, gate_ref, lb_ref, gn_ref, o_ref,
                  qs_ref, ks_ref, gc_ref, st_ref, *, ts):
    s = pl.program_id(2)
    c = CHUNK
    d = HEAD_DIM
    SUB = 16

    @pl.when(s == 0)
    def _():
        st_ref[...] = jnp.zeros_like(st_ref)

    lb = lb_ref[...]
    f_raw = f_ref[...]
    log_sig = jnp.minimum(f_raw, 0.0) - jnp.log1p(jnp.exp(-jnp.abs(f_raw)))
    la = jnp.log(lb)
    lbb = jnp.log1p(-lb) + log_sig
    log_f = jnp.maximum(la, lbb) + jnp.log1p(jnp.exp(-jnp.abs(la - lbb)))
    qs_ref[...] = _silu(q_ref[...])
    ks_ref[...] = (1.0 - lb) * _sigmoid(-f_raw)

    row = _iota((c, c), 0)
    col = _iota((c, c), 1)
    tri_f = (col <= row).astype(F32)
    ones_dd = jnp.ones((d, d), BF16)
    rows_8d = _iota((8, d), 0)
    gnorm = gn_ref[...]

    for ci in range(ts // c):
        gc_ref[ci * c:(ci + 1) * c, :] = _mm_f32(tri_f, log_f[ci * c:(ci + 1) * c, :])

    def chunk_loop(ci, carry):
        r0 = pl.multiple_of(ci * c, c)
        q = qs_ref[pl.ds(r0, c), :]
        k = ks_ref[pl.ds(r0, c), :]
        v = i_ref[pl.ds(r0, c), :]
        gc = gc_ref[pl.ds(r0, c), :]

        blocks = [(sb * SUB, (sb + 1) * SUB) for sb in range(c // SUB)]
        prods = []
        for top, end in blocks:
            for j in range(top, end):
                lo = (j // 8) * 8
                k_j = ks_ref[pl.ds(r0 + j, 1), :]
                g_j = gc_ref[pl.ds(r0 + j, 1), :]
                e = jnp.exp(jnp.minimum(gc[lo:end, :] - g_j, 0.0))
                if j % 8:
                    head = jnp.where(rows_8d >= j - lo, e[:8], 0.0)
                    e = jnp.concatenate([head, e[8:]], axis=0) if lo + 8 < end else head
                prods.append(q[lo:end, :] * k_j * e)
        sums = jnp.dot(jnp.concatenate(prods, axis=0).astype(BF16), ones_dd,
                       preferred_element_type=F32)
        qk_far = []
        for top, end in blocks[1:]:
            g_b = gc[top - 1:top, :]
            qe = q[top:end, :] * jnp.exp(gc[top:end, :] - g_b)
            ke = k[:top, :] * jnp.exp(jnp.minimum(g_b - gc[:top, :], 0.0))
            qk_far.append(_mm_nt(qe, ke))
        far = [_mm(a, v[:top, :]) for a, (top, _) in zip(qk_far, blocks[1:])]

        groups = [jnp.zeros((8, d), F32) for _ in range(c // 8)]
        at = 0
        for top, end in blocks:
            for j in range(top, end):
                v_j = i_ref[pl.ds(r0 + j, 1), :]
                for g in range(j // 8, end // 8):
                    groups[g] = groups[g] + sums[at:at + 8, :] * v_j
                    at += 8
        for f, (top, end) in zip(far, blocks[1:]):
            for g in range(top // 8, end // 8):
                groups[g] = groups[g] + f[(g * 8 - top):(g * 8 - top + 8), :]
        o_intra = jnp.concatenate(groups, axis=0)

        st = st_ref[...]
        gl = gc[c - 1:c, :]
        o = o_intra + _mm_nt(q * jnp.exp(gc), st)
        st_ref[...] = st * jnp.exp(gl) + _mm_tn(v, k * jnp.exp(gl - gc))
        o_ref[pl.ds(r0, c), :] = _rms(o, gnorm) * _silu(gate_ref[pl.ds(r0, c), :])
        return carry

    lax.fori_loop(0, ts // c, chunk_loop, 0)


def _hgrn2(p32, lb, d_norm_g, *, ts, cols):
    bsz, s, _ = p32.shape
    d = HEAD_DIM
    nh = N_HEADS
    qb, fb, ib, gb = cols["qd"], cols["fd"], cols["id"], cols["gd"]
    kernel = functools.partial(_hgrn2_kernel, ts=ts)
    return pl.pallas_call(
        kernel,
        grid=(bsz, nh, s // ts),
        in_specs=[pl.BlockSpec((None, ts, d), lambda b, h, i: (b, i, qb + h)),
                  pl.BlockSpec((None, ts, d), lambda b, h, i: (b, i, fb + h)),
                  pl.BlockSpec((None, ts, d), lambda b, h, i: (b, i, ib + h)),
                  pl.BlockSpec((None, ts, d), lambda b, h, i: (b, i, gb + h)),
                  pl.BlockSpec((1, d), lambda b, h, i: (0, h)),
                  pl.BlockSpec((1, d), lambda b, h, i: (0, 0))],
        out_specs=pl.BlockSpec((None, ts, d), lambda b, h, i: (b, i, h)),
        out_shape=jax.ShapeDtypeStruct((bsz, s, nh * d), F32),
        scratch_shapes=[pltpu.VMEM((ts, d), F32), pltpu.VMEM((ts, d), F32),
                        pltpu.VMEM((ts, d), F32), pltpu.VMEM((d, d), F32)],
        compiler_params=pltpu.CompilerParams(
            dimension_semantics=("parallel", "parallel", "arbitrary"), vmem_limit_bytes=VMEM_LIMIT),
        name="hgrn2",
    )(p32, p32, p32, p32, lb.astype(F32).reshape(1, nh * d), d_norm_g.astype(F32).reshape(1, d))


def _stickbreak_kernel(q_ref, k_ref, v_ref, o_ref, *, tq):
    i = pl.program_id(1)
    d = HEAD_DIM
    nh = N_HEADS
    row = _iota((tq, tq), 0)
    col = _iota((tq, tq), 1)
    causal = col < row
    later = (row > col).astype(BF16)

    def block(j, carries, diag):
        r0 = pl.multiple_of(j * tq, tq)
        out = []
        for hh in range(nh):
            hs = slice(hh * d, (hh + 1) * d)
            z = _mm_nt(q_ref[:, hs], k_ref[pl.ds(r0, tq), hs]) * (d ** -0.5)
            sp = _softplus(z)
            l1m = jnp.where(causal, -sp, 0.0) if diag else -sp
            l_hi, l_lo = _split(l1m)
            rest = (jnp.dot(l_hi, later, preferred_element_type=F32)
                    + jnp.dot(l_lo, later, preferred_element_type=F32))
            p = jnp.exp((z - sp) + rest + carries[hh])
            if diag:
                p = jnp.where(causal, p, 0.0)
            pv = _mm(p, v_ref[pl.ds(r0, tq), hs])
            if diag:
                o_ref[:, hs] = pv
            else:
                o_ref[:, hs] += pv
            out.append(carries[hh] + jnp.sum(l1m, axis=-1, keepdims=True))
        return tuple(out)

    carries = block(i, tuple(jnp.zeros((tq, 1), F32) for _ in range(nh)), True)

    def cond(c):
        worst = functools.reduce(jnp.maximum, c[1])
        return jnp.logical_and(c[0] >= 0, jnp.max(worst) >= EXP_ZERO_BELOW)

    def body(c):
        return c[0] - 1, block(c[0], c[1], False)

    lax.while_loop(cond, body, (i - 1, carries))


def _stickbreak(p16, *, tq, cols):
    bsz, s, _ = p16.shape
    nh = N_HEADS
    w = nh * HEAD_DIM
    kernel = functools.partial(_stickbreak_kernel, tq=tq)
    resident = dict(pipeline_mode=pl.Buffered(1))
    return pl.pallas_call(
        kernel,
        grid=(bsz, s // tq),
        in_specs=[pl.BlockSpec((None, tq, w), lambda b, i: (b, i, cols["qc"] // nh)),
                  pl.BlockSpec((None, s, w), lambda b, i: (b, 0, cols["kc"] // nh), **resident),
                  pl.BlockSpec((None, s, w), lambda b, i: (b, 0, cols["vc"] // nh), **resident)],
        out_specs=pl.BlockSpec((None, tq, w), lambda b, i: (b, i, 0)),
        out_shape=jax.ShapeDtypeStruct((bsz, s, w), F32),
        compiler_params=pltpu.CompilerParams(
            dimension_semantics=("parallel", "arbitrary"), vmem_limit_bytes=VMEM_LIMIT),
        name="stickbreak",
    )(p16, p16, p16)


def _dsa_kernel(qi_ref, smq_ref, q_ref, sm_ref, k_ref, v_ref, bias_ref, o_ref,
                sc_ref, scb_ref, wb_ref, qc_ref, kct_ref, q2_ref, lg_ref, l_ref, acc_ref, *, tq, k_sel, wi_lane, wide):
    i = pl.program_id(1)
    tk = tq
    d = HEAD_DIM
    nh = N_HEADS
    ksel = float(k_sel)
    per_wide = wide // tk
    n_wide = (i + per_wide) // per_wide
    sub = 2 * tk

    q2_ref[...] = (q_ref[...] * ((d ** -0.5) * LOG2E)).astype(BF16)

    @pl.when(i == 0)
    def _():
        def prep(g, carry):
            g0 = pl.multiple_of(g * wide, wide)
            kt = sm_ref[pl.ds(g0, wide), :].T[:IDX_DIM, :]
            hi, lo = _split(kt)
            kct_ref[:, pl.ds(g0, wide)] = jnp.concatenate([hi, lo, hi], axis=0)
            return carry
        lax.fori_loop(0, sm_ref.shape[0] // wide, prep, 0)

    smq = smq_ref[...]
    lane = _iota(smq.shape, 1)
    for hh in range(IDX_HEADS):
        qh = qi_ref[:, hh * IDX_DIM:(hh + 1) * IDX_DIM]
        hi = qh.astype(BF16)
        lo = (qh - hi.astype(F32)).astype(BF16)
        qc_ref[hh] = jnp.concatenate([hi, hi, lo], axis=-1)
        w = jnp.sum(jnp.where(lane == wi_lane + hh, smq, 0.0), axis=-1, keepdims=True)
        wb_ref[hh] = jnp.broadcast_to(w * ((IDX_HEADS ** -0.5) * (IDX_DIM ** -0.5)), (tq, tk))

    def fold(x, op=jnp.add):
        acc = x[:, 0:tk]
        for pb in range(1, x.shape[1] // tk):
            acc = op(acc, x[:, pb * tk:(pb + 1) * tk])
        return acc

    limit = i * tq + (_iota((tq, sub), 0) // CHUNK + 1) * CHUNK
    col_s = _iota((tq, sub), 1)

    def score_group(g, mm, masked):
        mn, mx = mm
        for sb in range(wide // sub):
            k0 = pl.multiple_of(g * wide + sb * sub, sub)
            kct = kct_ref[:, pl.ds(k0, sub)]
            tiles = [jnp.zeros((tq, tk), F32) for _ in range(sub // tk)]
            for hh in range(IDX_HEADS):
                s_h = jnp.dot(qc_ref[hh], kct, preferred_element_type=F32)
                for ti in range(sub // tk):
                    tiles[ti] = tiles[ti] + jnp.maximum(s_h[:, ti * tk:(ti + 1) * tk], 0.0) * wb_ref[hh]
            sc = jnp.concatenate(tiles, axis=-1)
            if masked:
                adm = (k0 + col_s) < limit
                sc = jnp.where(adm, sc, -jnp.inf)
                sc_ref[:, pl.ds(k0, sub)] = sc
                scb_ref[:, pl.ds(k0, sub)] = _floor_bf16(sc)
                mn = jnp.minimum(mn, fold(jnp.where(adm, sc, jnp.inf), jnp.minimum))
                mx = jnp.maximum(mx, fold(jnp.where(adm, sc, -jnp.inf), jnp.maximum))
            else:
                sc_ref[:, pl.ds(k0, sub)] = sc
                scb_ref[:, pl.ds(k0, sub)] = _floor_bf16(sc)
                mn = jnp.minimum(mn, fold(sc, jnp.minimum))
                mx = jnp.maximum(mx, fold(sc, jnp.maximum))
        return mn, mx

    mm = lax.fori_loop(0, n_wide - 1, functools.partial(score_group, masked=False),
                       (jnp.full((tq, tk), jnp.inf, F32), jnp.full((tq, tk), -jnp.inf, F32)))
    mn, mx = score_group(n_wide - 1, mm, True)
    rmin = jnp.min(mn, axis=-1, keepdims=True)
    rmax = jnp.max(mx, axis=-1, keepdims=True)


    def count(pred):
        def body(g, acc):
            g0 = pl.multiple_of(g * wide, wide)
            blk = sc_ref[:, pl.ds(g0, wide)]
            return acc + fold(pred(blk, g0))
        acc = lax.fori_loop(0, n_wide, body, jnp.zeros((tq, tk), F32))
        return jnp.sum(acc, axis=-1, keepdims=True)

    def max_below(x):
        def body(g, acc):
            blk = sc_ref[:, pl.ds(pl.multiple_of(g * wide, wide), wide)]
            mb = jnp.where(blk < x, blk, -jnp.inf)
            m = mb[:, 0:tk]
            for pb in range(1, per_wide):
                m = jnp.maximum(m, mb[:, pb * tk:(pb + 1) * tk])
            return jnp.maximum(acc, m)
        acc = lax.fori_loop(0, n_wide, body, jnp.full((tq, tk), -jnp.inf, F32))
        return jnp.max(acc, axis=-1, keepdims=True)

    rows1 = _iota((tq, 1), 0)
    n_adm = (i * tq + (rows1 // CHUNK + 1) * CHUNK).astype(F32)
    all_sel = n_adm <= ksel
    last_idx = float(sc_ref.shape[1])
    j_all = jnp.where(all_sel, -1.0, last_idx)

    def bisect(c):
        lo, hi, c_lo = c
        mid = 0.5 * lo + 0.5 * hi
        cm = count(lambda blk, g0: _ind(blk >= mid))
        ge = cm >= ksel
        return jnp.where(ge, mid, lo), jnp.where(ge, hi, mid), jnp.where(ge, cm, c_lo)

    def pending(c_lo):
        return jnp.where(all_sel, 0.0, _ind(c_lo != ksel))

    def bisect_coarse(_, c):
        lo, hi, c_lo = c
        mid = _floor_bf16(0.5 * lo + 0.5 * hi).astype(F32)
        t_b = jnp.broadcast_to(mid, (tq, tk)).astype(BF16)
        one_b = jnp.ones((tq, tk), BF16)
        zero_b = jnp.zeros((tq, tk), BF16)

        def body(g, acc):
            blk = scb_ref[:, pl.ds(pl.multiple_of(g * wide, wide), wide)]
            for pb in range(per_wide):
                acc = acc + jnp.where(blk[:, pb * tk:(pb + 1) * tk] >= t_b, one_b, zero_b)
            return acc

        acc = lax.fori_loop(0, n_wide, body, jnp.zeros((tq, tk), BF16))
        cm = jnp.sum(acc.astype(F32), axis=-1, keepdims=True)
        ge = cm >= ksel
        return jnp.where(ge, mid, lo), jnp.where(ge, hi, mid), jnp.where(ge, cm, c_lo)

    lo0 = _floor_bf16(rmin).astype(F32)
    hi0 = _floor_bf16(rmax + (jnp.abs(rmax) * (2.0 ** -6) + 1e-30)).astype(F32)
    state = lax.fori_loop(0, BISECT_COARSE, bisect_coarse, (lo0, hi0, n_adm))
    state = lax.fori_loop(0, BISECT_FIXED, lambda _, c: bisect(c), state)

    def more_cond(c):
        return jnp.logical_and(c[0] < BISECT_EXTRA, jnp.max(pending(c[1][2])) > 0.5)

    def more_body(c):
        return c[0] + 1, bisect(c[1])

    _, (lo_f, hi_f, c_lo_f) = lax.while_loop(more_cond, more_body, (jnp.int32(0), state))
    unresolved = pending(c_lo_f)
    v_fast = jnp.where(all_sel, -jnp.inf, lo_f)

    def slow_path(_):
        def search_cond(c):
            return jnp.min(c[3]) < 0.5

        def search_body(c):
            lo, hi, v, done = c
            cand = max_below(hi)
            ok = count(lambda blk, g0: _ind(blk >= cand)) >= ksel
            v = jnp.where(done > 0.5, v, cand)
            done = jnp.where(ok, 1.0, done)
            lo, hi, _ = lax.fori_loop(0, 6, lambda _, s: bisect(s), (lo, hi, c_lo_f))
            return lo, hi, v, done

        _, _, vth, _ = lax.while_loop(search_cond, search_body,
                                      (lo_f, hi_f, v_fast, 1.0 - unresolved))
        c_gt = count(lambda blk, g0: _ind(blk > vth))
        need = ksel - c_gt

        def bis(_, lh):
            lo, hi = lh
            mid = jnp.floor(0.5 * (lo + hi))

            def pred(blk, g0):
                idx = (g0 + _iota(blk.shape, 1)).astype(F32)
                return jnp.where(blk == vth, _ind(idx <= mid), 0.0)

            ge = count(pred) >= need
            return jnp.where(ge, lo, mid), jnp.where(ge, mid, hi)

        n_steps = int(math.ceil(math.log2(sc_ref.shape[1] + 1))) + 1
        _, j_tie = lax.fori_loop(0, n_steps, bis,
                                 (jnp.full((tq, 1), -1.0, F32), jnp.full((tq, 1), last_idx, F32)))
        return vth, jnp.where(unresolved > 0.5, j_tie, j_all)

    vth, jth = lax.cond(jnp.max(unresolved) > 0.5, slow_path, lambda _: (v_fast, j_all), 0)

    col_w = _iota((tq, wide), 1)
    g_near = jnp.maximum(i - 1, 0) // per_wide

    def logit_group(g, mx, near):
        g0 = pl.multiple_of(g * wide, wide)
        scb = sc_ref[:, pl.ds(g0, wide)]
        idx = (g0 + col_w).astype(F32)
        sel_f = jnp.where(scb == vth, _ind(idx <= jth), _ind(scb > vth))
        mb = jnp.where(sel_f > 0.5, 0.0, NEG_BIG)
        out = []
        for hh in range(nh):
            kh = k_ref[pl.ds(g0, wide), hh * d:(hh + 1) * d]
            lm = lax.dot_general(q2_ref[:, hh * d:(hh + 1) * d], kh, (((1,), (1,)), ((), ())),
                                 preferred_element_type=F32) + mb
            if near:
                back = [jnp.clip(i - (g * per_wide + pb), 0, 2) for pb in range(per_wide)]
                lm = lm + jnp.concatenate([bias_ref[back[pb], hh] for pb in range(per_wide)], axis=-1)
            lg_ref[hh, :, pl.ds(g0, wide)] = lm
            out.append(jnp.maximum(mx[hh], fold(lm, jnp.maximum)))
        return tuple(out)

    mx = tuple(jnp.full((tq, tk), NEG_BIG, F32) for _ in range(nh))
    mx = lax.fori_loop(0, g_near, functools.partial(logit_group, near=False), mx)
    mx = lax.fori_loop(g_near, n_wide, functools.partial(logit_group, near=True), mx)

    m_rows = [jnp.broadcast_to(jnp.max(mx[hh], axis=-1, keepdims=True), (tq, tk)) for hh in range(nh)]
    for hh in range(nh):
        l_ref[hh] = jnp.zeros((tq, tk), F32)
        acc_ref[hh] = jnp.zeros((tq, d), F32)

    def pv_body(g, carry):
        g0 = pl.multiple_of(g * wide, wide)
        for hh in range(nh):
            lm = lg_ref[hh, :, pl.ds(g0, wide)]
            p = jnp.concatenate([jnp.exp2(lm[:, pb * tk:(pb + 1) * tk] - m_rows[hh])
                                 for pb in range(per_wide)], axis=-1)
            vh = v_ref[pl.ds(g0, wide), hh * d:(hh + 1) * d]
            l_ref[hh] += fold(p)
            acc_ref[hh] += jnp.dot(p.astype(BF16), vh, preferred_element_type=F32)
        return carry

    lax.fori_loop(0, n_wide, pv_body, 0)
    for hh in range(nh):
        o_ref[:, hh * d:(hh + 1) * d] = acc_ref[hh] / jnp.sum(l_ref[hh], axis=-1, keepdims=True)


def _dsa(p32, p16, bias_tiles, *, tq, cols):
    bsz, s, _ = p32.shape
    d = HEAD_DIM
    nh = N_HEADS
    wide = 4 * tq
    k_sel = min(TOPK_MAX, s // 4)
    w512 = nh * d
    kernel = functools.partial(_dsa_kernel, tq=tq, k_sel=k_sel, wi_lane=cols["wi_lane"], wide=wide)
    resident = dict(pipeline_mode=pl.Buffered(1))
    return pl.pallas_call(
        kernel,
        grid=(bsz, s // tq),
        in_specs=[pl.BlockSpec((None, tq, w512), lambda b, i: (b, i, cols["qi"] // nh)),
                  pl.BlockSpec((None, tq, d), lambda b, i: (b, i, cols["small"])),
                  pl.BlockSpec((None, tq, w512), lambda b, i: (b, i, cols["qb"] // nh)),
                  pl.BlockSpec((None, s, d), lambda b, i: (b, 0, cols["small"]), **resident),
                  pl.BlockSpec((None, s, w512), lambda b, i: (b, 0, cols["kb"] // nh), **resident),
                  pl.BlockSpec((None, s, w512), lambda b, i: (b, 0, cols["vb"] // nh), **resident),
                  pl.BlockSpec((3, nh, tq, tq), lambda b, i: (0, 0, 0, 0), **resident)],
        out_specs=pl.BlockSpec((None, tq, w512), lambda b, i: (b, i, 0)),
        out_shape=jax.ShapeDtypeStruct((bsz, s, w512), F32),
        scratch_shapes=[pltpu.VMEM((tq, s + wide - tq), F32),
                        pltpu.VMEM((tq, s + wide - tq), BF16),
                        pltpu.VMEM((IDX_HEADS, tq, tq), F32),
                        pltpu.VMEM((IDX_HEADS, tq, 3 * IDX_DIM), BF16),
                        pltpu.VMEM((3 * IDX_DIM, s), BF16),
                        pltpu.VMEM((tq, w512), BF16),
                        pltpu.VMEM((nh, tq, s), F32),
                        pltpu.VMEM((nh, tq, tq), F32), pltpu.VMEM((nh, tq, d), F32)],
        compiler_params=pltpu.CompilerParams(
            dimension_semantics=("parallel", "arbitrary"), vmem_limit_bytes=VMEM_LIMIT),
        name="dsa",
    )(p32, p32, p32, p32, p16, p16, bias_tiles)


def _t5_bucket(rel):
    nb = REL_BUCKETS // 2
    max_exact = nb // 2
    ret = jnp.where(rel > 0, nb, 0)
    n = jnp.abs(rel)
    large = max_exact + (jnp.log(jnp.maximum(n, 1).astype(F32) / max_exact)
                         / math.log(REL_MAX_DIST / max_exact) * (nb - max_exact)).astype(jnp.int32)
    large = jnp.minimum(large, nb - 1)
    return ret + jnp.where(n < max_exact, n, large)


def _bias_tiles(rel_table, tq):
    assert tq >= REL_MAX_DIST
    t = jnp.arange(tq)
    tiles = []
    for back in range(3):
        rel = (t[None, :] - back * tq) - t[:, None]
        tiles.append(rel_table.astype(F32)[_t5_bucket(rel)].transpose(2, 0, 1))
    tiles = jnp.stack(tiles)
    return (tiles - tiles[2:3]) * LOG2E


def _even_layout(w_in):
    d = HEAD_DIM
    a_w = 2 * N_HEADS * d + N_HEADS * d
    offs = {}
    o = 0
    for name, w in (("qkv", a_w), ("z", N_HEADS * d), ("a", N_HEADS), ("b", N_HEADS),
                    ("qb", N_HEADS * d), ("kb", N_HEADS * d), ("vb", N_HEADS * d),
                    ("qi", IDX_HEADS * IDX_DIM), ("ki", IDX_DIM), ("wi", IDX_HEADS)):
        offs[name] = (o, o + w)
        o += w
    assert o == w_in.shape[1]
    sl = lambda n: w_in[:, offs[n][0]:offs[n][1]]
    small_w = IDX_DIM + 2 * N_HEADS + IDX_HEADS
    small_pad = -small_w % d
    w = jnp.concatenate([sl("qkv"), sl("z"), sl("qb"), sl("kb"), sl("vb"), sl("qi"),
                         sl("ki"), sl("a"), sl("b"), sl("wi"),
                         jnp.zeros((w_in.shape[0], small_pad), w_in.dtype)], axis=1)
    nh = N_HEADS
    cols = dict(qa=0, ka=nh, va=2 * nh, za=3 * nh, qb=4 * nh, kb=5 * nh, vb=6 * nh, qi=7 * nh,
                small=8 * nh, a_lane=IDX_DIM, b_lane=IDX_DIM + nh, wi_lane=IDX_DIM + 2 * nh)
    return w.astype(BF16), cols


def kernel(x, norm_g, w_in_even, conv_w_even, a_log_even, dt_bias_even, a_norm_even, w_out_even,
           rel_bias, w_in_odd, lb_logits, d_norm_odd, w_out_odd, w_gate, w_up, w_down):
    bsz, s, d = x.shape
    t = bsz * s
    depth = norm_g.shape[0]
    nh = N_HEADS
    tq = 128
    lb_all = jnp.cumsum(jax.nn.softmax(lb_logits.astype(F32), axis=0), axis=0)
    lb_all = lb_all - lb_all[:1]
    odd_cols = dict(qc=0, kc=nh, vc=2 * nh, qd=3 * nh, fd=4 * nh, id=5 * nh, gd=6 * nh)
    bias_tiles = _bias_tiles(rel_bias, tq)

    h = x.reshape(t, d)
    for l in range(depth):
        if l % 2 == 0:
            e = l // 2
            w_even, cols = _even_layout(w_in_even[e])
            p32, p16 = _norm_matmul(h, norm_g[l, 0], w_even, tm=512, tn=w_even.shape[1] // 3)
            p32 = p32.reshape(bsz, s, -1)
            p16 = p16.reshape(bsz, s, -1)
            o_1 = _deltanet(p32, conv_w_even[e], a_log_even[e], dt_bias_even[e], a_norm_even[e],
                            ts=min(512, s), cols=cols)
            o_2 = _dsa(p32, p16, bias_tiles, tq=tq, cols=cols)
            w_out = w_out_even[e]
        else:
            o = l // 2
            p32, p16 = _norm_matmul(h, norm_g[l, 0], w_in_odd[o].astype(BF16), tm=512, tn=512)
            p32 = p32.reshape(bsz, s, -1)
            p16 = p16.reshape(bsz, s, -1)
            o_1 = _stickbreak(p16, tq=tq, cols=odd_cols)
            o_2 = _hgrn2(p32, lb_all[l], d_norm_odd[o], ts=min(512, s), cols=odd_cols)
            w_out = w_out_odd[o]
        h = _outproj(o_1.reshape(t, -1), o_2.reshape(t, -1), w_out, h, norm_g[l, 1], tm=512)
        h = _ffn(h, norm_g[l, 2], norm_g[l, 3], w_gate[l], w_up[l], w_down[l], tm=1024, tf=256)
    return h.reshape(bsz, s, d)
```

```python
import functools
import math

import jax
import jax.numpy as jnp
from jax import lax
from jax.experimental import pallas as pl
from jax.experimental.pallas import tpu as pltpu

F32 = jnp.float32
BF16 = jnp.bfloat16
HIGHEST = lax.Precision.HIGHEST

CHUNK = 64
HEAD_DIM = 128
N_HEADS = 4
IDX_HEADS = 8
IDX_DIM = 64
TOPK_MAX = 256
CONV_WIDTH = 4
REL_BUCKETS = 32
REL_MAX_DIST = 128
EPS = 1e-6
NEG_BIG = -1e30
LOG2E = 1.4426950408889634
BISECT_COARSE = 12
BISECT_FIXED = 4
BISECT_EXTRA = 6
F32_LOWEST = -3.4028234663852886e38
EXP_ZERO_BELOW = -104.0
VMEM_LIMIT = 56 * 1024 * 1024


def _mm(a, b):
    return jnp.dot(a.astype(BF16), b.astype(BF16), preferred_element_type=F32)


def _mm_nt(a, b):
    return lax.dot_general(a.astype(BF16), b.astype(BF16), (((1,), (1,)), ((), ())),
                           preferred_element_type=F32)


def _mm_tn(a, b):
    return lax.dot_general(a.astype(BF16), b.astype(BF16), (((0,), (0,)), ((), ())),
                           preferred_element_type=F32)


def _mm_f32(a, b):
    return jnp.dot(a, b, precision=HIGHEST, preferred_element_type=F32)


def _split(x):
    hi = x.astype(BF16)
    return hi, (x - hi.astype(F32)).astype(BF16)


def _mm_x3(a, b):
    a_hi, a_lo = _split(a)
    b_hi, b_lo = _split(b)
    return jnp.dot(jnp.concatenate([a_hi, a_hi, a_lo], axis=1),
                   jnp.concatenate([b_hi, b_lo, b_hi], axis=0), preferred_element_type=F32)


def _floor_bf16(x):
    bits = pltpu.bitcast(x, jnp.int32)
    down = jnp.where(bits >= 0, bits, bits + 0xFFFF) & jnp.int32(-65536)
    return pltpu.bitcast(down, F32).astype(BF16)


def _sigmoid(x):
    return 1.0 / (1.0 + jnp.exp(-x))


def _silu(x):
    return x * _sigmoid(x)


def _softplus(x):
    return jnp.maximum(x, 0.0) + jnp.log1p(jnp.exp(-jnp.abs(x)))


def _rms(x, g):
    return x * lax.rsqrt(jnp.mean(x * x, axis=-1, keepdims=True) + EPS) * g


def _iota(shape, dim):
    return lax.broadcasted_iota(jnp.int32, shape, dim)


def _ind(mask):
    return jnp.where(mask, 1.0, 0.0)


def _norm_matmul_kernel(x_ref, g_ref, w_ref, o32_ref, o16_ref, xn_ref):
    @pl.when(pl.program_id(1) == 0)
    def _():
        xn_ref[...] = _rms(x_ref[...], g_ref[...]).astype(BF16)

    y = jnp.dot(xn_ref[...], w_ref[...], preferred_element_type=F32)
    o32_ref[...] = y
    o16_ref[...] = y.astype(BF16)


def _norm_matmul(x, g, w, *, tm, tn):
    t, d = x.shape
    n = w.shape[1]
    return pl.pallas_call(
        _norm_matmul_kernel,
        grid=(t // tm, n // tn),
        in_specs=[pl.BlockSpec((tm, d), lambda i, j: (i, 0)),
                  pl.BlockSpec((1, d), lambda i, j: (0, 0)),
                  pl.BlockSpec((d, tn), lambda i, j: (0, j))],
        out_specs=[pl.BlockSpec((tm, tn), lambda i, j: (i, j)),
                   pl.BlockSpec((tm, tn), lambda i, j: (i, j))],
        out_shape=[jax.ShapeDtypeStruct((t, n), F32), jax.ShapeDtypeStruct((t, n), BF16)],
        scratch_shapes=[pltpu.VMEM((tm, d), BF16)],
        compiler_params=pltpu.CompilerParams(
            dimension_semantics=("parallel", "arbitrary"), vmem_limit_bytes=VMEM_LIMIT),
        name="norm_matmul",
    )(x, g.reshape(1, d), w)


def _outproj_kernel(ca_ref, cb_ref, wa_ref, wb_ref, h_ref, g_ref, o_ref):
    y = (jnp.dot(ca_ref[...].astype(BF16), wa_ref[...], preferred_element_type=F32)
         + jnp.dot(cb_ref[...].astype(BF16), wb_ref[...], preferred_element_type=F32))
    o_ref[...] = h_ref[...] + _rms(y, g_ref[...])


def _outproj(ca, cb, w, h, g, *, tm):
    t, d = h.shape
    wa_n = ca.shape[1]
    wb_n = cb.shape[1]
    wa = w[:wa_n].astype(BF16)
    wb = w[wa_n:].astype(BF16)
    return pl.pallas_call(
        _outproj_kernel,
        grid=(t // tm,),
        in_specs=[pl.BlockSpec((tm, wa_n), lambda i: (i, 0)),
                  pl.BlockSpec((tm, wb_n), lambda i: (i, 0)),
                  pl.BlockSpec((wa_n, d), lambda i: (0, 0)),
                  pl.BlockSpec((wb_n, d), lambda i: (0, 0)),
                  pl.BlockSpec((tm, d), lambda i: (i, 0)),
                  pl.BlockSpec((1, d), lambda i: (0, 0))],
        out_specs=pl.BlockSpec((tm, d), lambda i: (i, 0)),
        out_shape=jax.ShapeDtypeStruct((t, d), F32),
        compiler_params=pltpu.CompilerParams(
            dimension_semantics=("parallel",), vmem_limit_bytes=VMEM_LIMIT),
        name="outproj",
    )(ca, cb, wa, wb, h, g.reshape(1, d))


def _ffn_kernel(h_ref, gpre_ref, gpost_ref, wg_ref, wu_ref, wd_ref, o_ref, xn_ref, acc_ref):
    f = pl.program_id(1)

    @pl.when(f == 0)
    def _():
        xn_ref[...] = _rms(h_ref[...], gpre_ref[...]).astype(BF16)
        acc_ref[...] = jnp.zeros_like(acc_ref)

    xn = xn_ref[...]
    gate = jnp.dot(xn, wg_ref[...], preferred_element_type=F32)
    up = jnp.dot(xn, wu_ref[...], preferred_element_type=F32)
    act = (_silu(gate) * up).astype(BF16)
    acc_ref[...] += jnp.dot(act, wd_ref[...], preferred_element_type=F32)

    @pl.when(f == pl.num_programs(1) - 1)
    def _():
        o_ref[...] = h_ref[...] + _rms(acc_ref[...], gpost_ref[...])


def _ffn(h, g_pre, g_post, wg, wu, wd, *, tm, tf):
    t, d = h.shape
    ff = wg.shape[1]
    return pl.pallas_call(
        _ffn_kernel,
        grid=(t // tm, ff // tf),
        in_specs=[pl.BlockSpec((tm, d), lambda i, f: (i, 0)),
                  pl.BlockSpec((1, d), lambda i, f: (0, 0)),
                  pl.BlockSpec((1, d), lambda i, f: (0, 0)),
                  pl.BlockSpec((d, tf), lambda i, f: (0, f)),
                  pl.BlockSpec((d, tf), lambda i, f: (0, f)),
                  pl.BlockSpec((tf, d), lambda i, f: (f, 0))],
        out_specs=pl.BlockSpec((tm, d), lambda i, f: (i, 0)),
        out_shape=jax.ShapeDtypeStruct((t, d), F32),
        scratch_shapes=[pltpu.VMEM((tm, d), BF16), pltpu.VMEM((tm, d), F32)],
        compiler_params=pltpu.CompilerParams(
            dimension_semantics=("parallel", "arbitrary"), vmem_limit_bytes=VMEM_LIMIT),
        name="ffn",
    )(h, g_pre.reshape(1, d), g_post.reshape(1, d),
      wg.astype(BF16), wu.astype(BF16), wd.astype(BF16))


def _deltanet_kernel(xq_ref, xk_ref, xv_ref, z_ref, sm_ref, cwq_ref, cwk_ref, cwv_ref,
                     alog_ref, dtb_ref, gn_ref, o_ref,
                     xpad_ref, q_ref, k_ref, v_ref, gb_ref, bb_ref, u_ref, w_ref, qk_ref, st_ref,
                     *, ts, a_col, b_col):
    h = pl.program_id(1)
    s = pl.program_id(2)
    c = CHUNK
    d = HEAD_DIM

    @pl.when(s == 0)
    def _():
        xpad_ref[:, 0:8, :] = jnp.zeros((3, 8, d), F32)
        st_ref[...] = jnp.zeros_like(st_ref)

    @pl.when(s != 0)
    def _():
        xpad_ref[:, 0:8, :] = xpad_ref[:, ts:ts + 8, :]

    xpad_ref[0, 8:ts + 8, :] = xq_ref[...]
    xpad_ref[1, 8:ts + 8, :] = xk_ref[...]
    xpad_ref[2, 8:ts + 8, :] = xv_ref[...]

    def conv_silu(idx, cw_ref):
        cw = cw_ref[...]
        acc = xpad_ref[idx, 8 - (CONV_WIDTH - 1):8 - (CONV_WIDTH - 1) + ts, :] * cw[0:1, :]
        for j in range(1, CONV_WIDTH):
            off = 8 - (CONV_WIDTH - 1) + j
            acc = acc + xpad_ref[idx, off:off + ts, :] * cw[j:j + 1, :]
        return _silu(acc)

    def l2norm(t):
        return t * lax.rsqrt(jnp.sum(t * t, axis=-1, keepdims=True) + EPS)

    q_ref[...] = l2norm(conv_silu(0, cwq_ref)) * (d ** -0.5)
    k_ref[...] = l2norm(conv_silu(1, cwk_ref))
    v_ref[...] = conv_silu(2, cwv_ref)

    sm = sm_ref[...]
    lane = _iota(sm.shape, 1)
    a_raw = jnp.sum(jnp.where(lane == a_col + h, sm, 0.0), axis=-1, keepdims=True)
    b_raw = jnp.sum(jnp.where(lane == b_col + h, sm, 0.0), axis=-1, keepdims=True)
    hl = _iota((1, d), 1)
    a_log = jnp.sum(jnp.where(hl == h, alog_ref[...], 0.0), axis=-1, keepdims=True)
    dtb = jnp.sum(jnp.where(hl == h, dtb_ref[...], 0.0), axis=-1, keepdims=True)
    g = -jnp.exp(a_log) * _softplus(a_raw + dtb)
    gb_ref[...] = jnp.broadcast_to(g, (ts, d))
    bb_ref[...] = jnp.broadcast_to(_sigmoid(b_raw), (ts, d))

    row = _iota((c, c), 0)
    col = _iota((c, c), 1)
    tri = (col <= row)
    strict = (col < row)
    tri_f = tri.astype(F32)
    upper_f = (row <= col).astype(F32)
    eye = (row == col).astype(F32)
    ones_cc = jnp.ones((c, c), F32)
    gnorm = gn_ref[...]

    chunks = range(ts // c)
    rs = [slice(ci * c, (ci + 1) * c) for ci in chunks]
    tri2 = jnp.concatenate([tri_f, tri_f], axis=1).astype(BF16)
    ones2 = jnp.ones((c, 2 * c), BF16)

    def cum2(lhs2, x):
        hi, lo = _split(x)
        return jnp.dot(lhs2, jnp.concatenate([hi, lo], axis=0), preferred_element_type=F32)

    q = [q_ref[r, :] for r in rs]
    k = [k_ref[r, :] for r in rs]
    beta = [bb_ref[r, :] for r in rs]
    gb = [gb_ref[r, :] for r in rs]
    gc = [cum2(tri2, x) for x in gb]
    gc_row = [cum2(ones2, x[:, :c] * upper_f) for x in gb]
    decay = [jnp.where(tri, jnp.exp(jnp.minimum(a[:, :c] - b, 0.0)), 0.0) for a, b in zip(gc, gc_row)]
    kk = [_mm_nt(x, x) for x in k]
    n = [-jnp.where(strict, b[:, :c] * x * dc, 0.0) for b, x, dc in zip(beta, kk, decay)]
    inv = [eye + x for x in n]
    for _ in range(5):
        n = [_mm_x3(x, x) for x in n]
        inv = [iv + _mm_x3(iv, x) for iv, x in zip(inv, n)]
    egc = [jnp.exp(x) for x in gc]
    gl = [x[c - 1:c, :] for x in gc]
    for ci in chunks:
        r = rs[ci]
        u_ref[r, :] = _mm_x3(inv[ci], v_ref[r, :] * beta[ci])
        w_ref[r, :] = _mm_x3(inv[ci], k[ci] * (beta[ci] * egc[ci]))
        qk_ref[r, :] = _mm_nt(q[ci], k[ci]) * decay[ci]
        q_ref[r, :] = q[ci] * egc[ci]
        k_ref[r, :] = k[ci] * jnp.exp(gl[ci] - gc[ci])
        gb_ref[r, :] = jnp.broadcast_to(jnp.exp(gl[ci]), (c, d))

    def chunk_body(ci, carry):
        r0 = pl.multiple_of(ci * c, c)
        st = st_ref[...]
        v_new = u_ref[pl.ds(r0, c), :] - _mm(w_ref[pl.ds(r0, c), :], st)
        o = _mm(q_ref[pl.ds(r0, c), :], st) + _mm(qk_ref[pl.ds(r0, c), :], v_new)
        st_ref[...] = st * gb_ref[pl.ds(r0, 1), :] + _mm_tn(k_ref[pl.ds(r0, c), :], v_new)
        zc = z_ref[pl.ds(r0, c), :]
        o_ref[pl.ds(r0, c), :] = _rms(o, gnorm) * _silu(zc)
        return carry

    lax.fori_loop(0, ts // c, chunk_body, 0)


def _deltanet(p32, conv_w, a_log, dt_bias, a_norm_g, *, ts, cols):
    bsz, s, _ = p32.shape
    d = HEAD_DIM
    nh = N_HEADS
    qb, kb, vb, zb, smb = cols["qa"], cols["ka"], cols["va"], cols["za"], cols["small"]
    pad = lambda t: jnp.pad(t.astype(F32), (0, d - t.shape[0])).reshape(1, d)
    kernel = functools.partial(_deltanet_kernel, ts=ts, a_col=cols["a_lane"], b_col=cols["b_lane"])
    return pl.pallas_call(
        kernel,
        grid=(bsz, nh, s // ts),
        in_specs=[pl.BlockSpec((None, ts, d), lambda b, h, i: (b, i, qb + h)),
                  pl.BlockSpec((None, ts, d), lambda b, h, i: (b, i, kb + h)),
                  pl.BlockSpec((None, ts, d), lambda b, h, i: (b, i, vb + h)),
                  pl.BlockSpec((None, ts, d), lambda b, h, i: (b, i, zb + h)),
                  pl.BlockSpec((None, ts, d), lambda b, h, i: (b, i, smb)),
                  pl.BlockSpec((CONV_WIDTH, d), lambda b, h, i: (0, h)),
                  pl.BlockSpec((CONV_WIDTH, d), lambda b, h, i: (0, nh + h)),
                  pl.BlockSpec((CONV_WIDTH, d), lambda b, h, i: (0, 2 * nh + h)),
                  pl.BlockSpec((1, d), lambda b, h, i: (0, 0)),
                  pl.BlockSpec((1, d), lambda b, h, i: (0, 0)),
                  pl.BlockSpec((1, d), lambda b, h, i: (0, 0))],
        out_specs=pl.BlockSpec((None, ts, d), lambda b, h, i: (b, i, h)),
        out_shape=jax.ShapeDtypeStruct((bsz, s, nh * d), F32),
        scratch_shapes=[pltpu.VMEM((3, ts + 8, d), F32),
                        pltpu.VMEM((ts, d), F32), pltpu.VMEM((ts, d), F32), pltpu.VMEM((ts, d), F32),
                        pltpu.VMEM((ts, d), F32), pltpu.VMEM((ts, d), F32),
                        pltpu.VMEM((ts, d), F32), pltpu.VMEM((ts, d), F32),
                        pltpu.VMEM((ts, CHUNK), F32),
                        pltpu.VMEM((d, d), F32)],
        compiler_params=pltpu.CompilerParams(
            dimension_semantics=("parallel", "parallel", "arbitrary"), vmem_limit_bytes=VMEM_LIMIT),
        name="deltanet",
    )(p32, p32, p32, p32, p32, conv_w.astype(F32), conv_w.astype(F32), conv_w.astype(F32),
      pad(a_log), pad(dt_bias), a_norm_g.astype(F32).reshape(1, d))


def _hgrn2_kernel(q_ref, f_ref, i_ref, gate_ref, lb_ref, gn_ref, o_ref,
                  qs_ref, ks_ref, gc_ref, st_ref, *, ts):
    s = pl.program_id(2)
    c = CHUNK
    d = HEAD_DIM
    SUB = 16

    @pl.when(s == 0)
    def _():
        st_ref[...] = jnp.zeros_like(st_ref)

    lb = lb_ref[...]
    f_raw = f_ref[...]
    log_sig = jnp.minimum(f_raw, 0.0) - jnp.log1p(jnp.exp(-jnp.abs(f_raw)))
    la = jnp.log(lb)
    lbb = jnp.log1p(-lb) + log_sig
    log_f = jnp.maximum(la, lbb) + jnp.log1p(jnp.exp(-jnp.abs(la - lbb)))
    qs_ref[...] = _silu(q_ref[...])
    ks_ref[...] = (1.0 - lb) * _sigmoid(-f_raw)

    row = _iota((c, c), 0)
    col = _iota((c, c), 1)
    tri_f = (col <= row).astype(F32)
    ones_dd = jnp.ones((d, d), BF16)
    rows_8d = _iota((8, d), 0)
    gnorm = gn_ref[...]

    for ci in range(ts // c):
        gc_ref[ci * c:(ci + 1) * c, :] = _mm_f32(tri_f, log_f[ci * c:(ci + 1) * c, :])

    def chunk_loop(ci, carry):
        r0 = pl.multiple_of(ci * c, c)
        q = qs_ref[pl.ds(r0, c), :]
        k = ks_ref[pl.ds(r0, c), :]
        v = i_ref[pl.ds(r0, c), :]
        gc = gc_ref[pl.ds(r0, c), :]

        blocks = [(sb * SUB, (sb + 1) * SUB) for sb in range(c // SUB)]
        prods = []
        for top, end in blocks:
            for j in range(top, end):
                lo = (j // 8) * 8
                k_j = ks_ref[pl.ds(r0 + j, 1), :]
                g_j = gc_ref[pl.ds(r0 + j, 1), :]
                e = jnp.exp(jnp.minimum(gc[lo:end, :] - g_j, 0.0))
                if j % 8:
                    head = jnp.where(rows_8d >= j - lo, e[:8], 0.0)
                    e = jnp.concatenate([head, e[8:]], axis=0) if lo + 8 < end else head
                prods.append(q[lo:end, :] * k_j * e)
        sums = jnp.dot(jnp.concatenate(prods, axis=0).astype(BF16), ones_dd,
                       preferred_element_type=F32)
        qk_far = []
        for top, end in blocks[1:]:
            g_b = gc[top - 1:top, :]
            qe = q[top:end, :] * jnp.exp(gc[top:end, :] - g_b)
            ke = k[:top, :] * jnp.exp(jnp.minimum(g_b - gc[:top, :], 0.0))
            qk_far.append(_mm_nt(qe, ke))
        far = [_mm(a, v[:top, :]) for a, (top, _) in zip(qk_far, blocks[1:])]

        groups = [jnp.zeros((8, d), F32) for _ in range(c // 8)]
        at = 0
        for top, end in blocks:
            for j in range(top, end):
                v_j = i_ref[pl.ds(r0 + j, 1), :]
                for g in range(j // 8, end // 8):
                    groups[g] = groups[g] + sums[at:at + 8, :] * v_j
                    at += 8
        for f, (top, end) in zip(far, blocks[1:]):
            for g in range(top // 8, end // 8):
                groups[g] = groups[g] + f[(g * 8 - top):(g * 8 - top + 8), :]
        o_intra = jnp.concatenate(groups, axis=0)

        st = st_ref[...]
        gl = gc[c - 1:c, :]
        o = o_intra + _mm_nt(q * jnp.exp(gc), st)
        st_ref[...] = st * jnp.exp(gl) + _mm_tn(v, k * jnp.exp(gl - gc))
        o_ref[pl.ds(r0, c), :] = _rms(o, gnorm) * _silu(gate_ref[pl.ds(r0, c), :])
        return carry

    lax.fori_loop(0, ts // c, chunk_loop, 0)


def _hgrn2(p32, lb, d_norm_g, *, ts, cols):
    bsz, s, _ = p32.shape
    d = HEAD_DIM
    nh = N_HEADS
    qb, fb, ib, gb = cols["qd"], cols["fd"], cols["id"], cols["gd"]
    kernel = functools.partial(_hgrn2_kernel, ts=ts)
    return pl.pallas_call(
        kernel,
        grid=(bsz, nh, s // ts),
        in_specs=[pl.BlockSpec((None, ts, d), lambda b, h, i: (b, i, qb + h)),
                  pl.BlockSpec((None, ts, d), lambda b, h, i: (b, i, fb + h)),
                  pl.BlockSpec((None, ts, d), lambda b, h, i: (b, i, ib + h)),
                  pl.BlockSpec((None, ts, d), lambda b, h, i: (b, i, gb + h)),
                  pl.BlockSpec((1, d), lambda b, h, i: (0, h)),
                  pl.BlockSpec((1, d), lambda b, h, i: (0, 0))],
        out_specs=pl.BlockSpec((None, ts, d), lambda b, h, i: (b, i, h)),
        out_shape=jax.ShapeDtypeStruct((bsz, s, nh * d), F32),
        scratch_shapes=[pltpu.VMEM((ts, d), F32), pltpu.VMEM((ts, d), F32),
                        pltpu.VMEM((ts, d), F32), pltpu.VMEM((d, d), F32)],
        compiler_params=pltpu.CompilerParams(
            dimension_semantics=("parallel", "parallel", "arbitrary"), vmem_limit_bytes=VMEM_LIMIT),
        name="hgrn2",
    )(p32, p32, p32, p32, lb.astype(F32).reshape(1, nh * d), d_norm_g.astype(F32).reshape(1, d))


def _stickbreak_kernel(q_ref, k_ref, v_ref, o_ref, *, tq):
    i = pl.program_id(1)
    d = HEAD_DIM
    nh = N_HEADS
    row = _iota((tq, tq), 0)
    col = _iota((tq, tq), 1)
    causal = col < row
    later = (row > col).astype(BF16)

    def block(j, carries, diag):
        r0 = pl.multiple_of(j * tq, tq)
        out = []
        for hh in range(nh):
            hs = slice(hh * d, (hh + 1) * d)
            z = _mm_nt(q_ref[:, hs], k_ref[pl.ds(r0, tq), hs]) * (d ** -0.5)
            sp = _softplus(z)
            l1m = jnp.where(causal, -sp, 0.0) if diag else -sp
            l_hi, l_lo = _split(l1m)
            rest = (jnp.dot(l_hi, later, preferred_element_type=F32)
                    + jnp.dot(l_lo, later, preferred_element_type=F32))
            p = jnp.exp((z - sp) + rest + carries[hh])
            if diag:
                p = jnp.where(causal, p, 0.0)
            pv = _mm(p, v_ref[pl.ds(r0, tq), hs])
            if diag:
                o_ref[:, hs] = pv
            else:
                o_ref[:, hs] += pv
            out.append(carries[hh] + jnp.sum(l1m, axis=-1, keepdims=True))
        return tuple(out)

    carries = block(i, tuple(jnp.zeros((tq, 1), F32) for _ in range(nh)), True)

    def cond(c):
        worst = functools.reduce(jnp.maximum, c[1])
        return jnp.logical_and(c[0] >= 0, jnp.max(worst) >= EXP_ZERO_BELOW)

    def body(c):
        return c[0] - 1, block(c[0], c[1], False)

    lax.while_loop(cond, body, (i - 1, carries))


def _stickbreak(p16, *, tq, cols):
    bsz, s, _ = p16.shape
    nh = N_HEADS
    w = nh * HEAD_DIM
    kernel = functools.partial(_stickbreak_kernel, tq=tq)
    resident = dict(pipeline_mode=pl.Buffered(1))
    return pl.pallas_call(
        kernel,
        grid=(bsz, s // tq),
        in_specs=[pl.BlockSpec((None, tq, w), lambda b, i: (b, i, cols["qc"] // nh)),
                  pl.BlockSpec((None, s, w), lambda b, i: (b, 0, cols["kc"] // nh), **resident),
                  pl.BlockSpec((None, s, w), lambda b, i: (b, 0, cols["vc"] // nh), **resident)],
        out_specs=pl.BlockSpec((None, tq, w), lambda b, i: (b, i, 0)),
        out_shape=jax.ShapeDtypeStruct((bsz, s, w), F32),
        compiler_params=pltpu.CompilerParams(
            dimension_semantics=("parallel", "arbitrary"), vmem_limit_bytes=VMEM_LIMIT),
        name="stickbreak",
    )(p16, p16, p16)


def _dsa_kernel(qi_ref, smq_ref, q_ref, sm_ref, k_ref, v_ref, bias_ref, o_ref,
                sc_ref, scb_ref, wb_ref, qc_ref, kct_ref, q2_ref, lg_ref, l_ref, acc_ref, *, tq, k_sel, wi_lane, wide):
    i = pl.program_id(1)
    tk = tq
    d = HEAD_DIM
    nh = N_HEADS
    ksel = float(k_sel)
    per_wide = wide // tk
    n_wide = (i + per_wide) // per_wide
    sub = 2 * tk

    q2_ref[...] = (q_ref[...] * ((d ** -0.5) * LOG2E)).astype(BF16)

    @pl.when(i == 0)
    def _():
        def prep(g, carry):
            g0 = pl.multiple_of(g * wide, wide)
            kt = sm_ref[pl.ds(g0, wide), :].T[:IDX_DIM, :]
            hi, lo = _split(kt)
            kct_ref[:, pl.ds(g0, wide)] = jnp.concatenate([hi, lo, hi], axis=0)
            return carry
        lax.fori_loop(0, sm_ref.shape[0] // wide, prep, 0)

    smq = smq_ref[...]
    lane = _iota(smq.shape, 1)
    for hh in range(IDX_HEADS):
        qh = qi_ref[:, hh * IDX_DIM:(hh + 1) * IDX_DIM]
        hi = qh.astype(BF16)
        lo = (qh - hi.astype(F32)).astype(BF16)
        qc_ref[hh] = jnp.concatenate([hi, hi, lo], axis=-1)
        w = jnp.sum(jnp.where(lane == wi_lane + hh, smq, 0.0), axis=-1, keepdims=True)
        wb_ref[hh] = jnp.broadcast_to(w * ((IDX_HEADS ** -0.5) * (IDX_DIM ** -0.5)), (tq, tk))

    def fold(x, op=jnp.add):
        acc = x[:, 0:tk]
        for pb in range(1, x.shape[1] // tk):
            acc = op(acc, x[:, pb * tk:(pb + 1) * tk])
        return acc

    limit = i * tq + (_iota((tq, sub), 0) // CHUNK + 1) * CHUNK
    col_s = _iota((tq, sub), 1)

    def score_group(g, mm, masked):
        mn, mx = mm
        for sb in range(wide // sub):
            k0 = pl.multiple_of(g * wide + sb * sub, sub)
            kct = kct_ref[:, pl.ds(k0, sub)]
            tiles = [jnp.zeros((tq, tk), F32) for _ in range(sub // tk)]
            for hh in range(IDX_HEADS):
                s_h = jnp.dot(qc_ref[hh], kct, preferred_element_type=F32)
                for ti in range(sub // tk):
                    tiles[ti] = tiles[ti] + jnp.maximum(s_h[:, ti * tk:(ti + 1) * tk], 0.0) * wb_ref[hh]
            sc = jnp.concatenate(tiles, axis=-1)
            if masked:
                adm = (k0 + col_s) < limit
                sc = jnp.where(adm, sc, -jnp.inf)
                sc_ref[:, pl.ds(k0, sub)] = sc
                scb_ref[:, pl.ds(k0, sub)] = _floor_bf16(sc)
                mn = jnp.minimum(mn, fold(jnp.where(adm, sc, jnp.inf), jnp.minimum))
                mx = jnp.maximum(mx, fold(jnp.where(adm, sc, -jnp.inf), jnp.maximum))
            else:
                sc_ref[:, pl.ds(k0, sub)] = sc
                scb_ref[:, pl.ds(k0, sub)] = _floor_bf16(sc)
                mn = jnp.minimum(mn, fold(sc, jnp.minimum))
                mx = jnp.maximum(mx, fold(sc, jnp.maximum))
        return mn, mx

    mm = lax.fori_loop(0, n_wide - 1, functools.partial(score_group, masked=False),
                       (jnp.full((tq, tk), jnp.inf, F32), jnp.full((tq, tk), -jnp.inf, F32)))
    mn, mx = score_group(n_wide - 1, mm, True)
    rmin = jnp.min(mn, axis=-1, keepdims=True)
    rmax = jnp.max(mx, axis=-1, keepdims=True)


    def count(pred):
        def body(g, acc):
            g0 = pl.multiple_of(g * wide, wide)
            blk = sc_ref[:, pl.ds(g0, wide)]
            return acc + fold(pred(blk, g0))
        acc = lax.fori_loop(0, n_wide, body, jnp.zeros((tq, tk), F32))
        return jnp.sum(acc, axis=-1, keepdims=True)

    def max_below(x):
        def body(g, acc):
            blk = sc_ref[:, pl.ds(pl.multiple_of(g * wide, wide), wide)]
            mb = jnp.where(blk < x, blk, -jnp.inf)
            m = mb[:, 0:tk]
            for pb in range(1, per_wide):
                m = jnp.maximum(m, mb[:, pb * tk:(pb + 1) * tk])
            return jnp.maximum(acc, m)
        acc = lax.fori_loop(0, n_wide, body, jnp.full((tq, tk), -jnp.inf, F32))
        return jnp.max(acc, axis=-1, keepdims=True)

    rows1 = _iota((tq, 1), 0)
    n_adm = (i * tq + (rows1 // CHUNK + 1) * CHUNK).astype(F32)
    all_sel = n_adm <= ksel
    def bisect(c):
        lo, hi, c_lo = c
        mid = 0.5 * lo + 0.5 * hi
        cm = count(lambda blk, g0: _ind(blk >= mid))
        ge = cm >= ksel
        return jnp.where(ge, mid, lo), jnp.where(ge, hi, mid), jnp.where(ge, cm, c_lo)

    def pending(c_lo, tied):
        return jnp.where(all_sel, 0.0, jnp.where(tied > 0.5, 0.0, _ind(c_lo != ksel)))

    def bisect_coarse(_, c):
        lo, hi, c_lo = c
        mid = _floor_bf16(0.5 * lo + 0.5 * hi).astype(F32)
        t_b = jnp.broadcast_to(mid, (tq, tk)).astype(BF16)
        one_b = jnp.ones((tq, tk), BF16)
        zero_b = jnp.zeros((tq, tk), BF16)

        def body(g, acc):
            blk = scb_ref[:, pl.ds(pl.multiple_of(g * wide, wide), wide)]
            for pb in range(per_wide):
                acc = acc + jnp.where(blk[:, pb * tk:(pb + 1) * tk] >= t_b, one_b, zero_b)
            return acc

        acc = lax.fori_loop(0, n_wide, body, jnp.zeros((tq, tk), BF16))
        cm = jnp.sum(acc.astype(F32), axis=-1, keepdims=True)
        ge = cm >= ksel
        return jnp.where(ge, mid, lo), jnp.where(ge, hi, mid), jnp.where(ge, cm, c_lo)

    lo0 = _floor_bf16(rmin).astype(F32)
    hi0 = _floor_bf16(rmax + (jnp.abs(rmax) * (2.0 ** -6) + 1e-30)).astype(F32)
    state = lax.fori_loop(0, BISECT_COARSE, bisect_coarse, (lo0, hi0, n_adm))
    state = lax.fori_loop(0, BISECT_FIXED, lambda _, c: bisect(c), state)

    def round_cond(c):
        return jnp.max(pending(c[0][2], c[1])) > 0.5

    def round_body(c):
        st, tied, v, need = c

        def more_cond(s):
            return jnp.logical_and(s[0] < BISECT_EXTRA, jnp.max(pending(s[1][2], tied)) > 0.5)

        _, st = lax.while_loop(more_cond, lambda s: (s[0] + 1, bisect(s[1])), (jnp.int32(0), st))
        pend = pending(st[2], tied)

        def check(_):
            cand = max_below(st[1])
            c_ge = count(lambda blk, g0: _ind(blk >= cand))
            c_gt = count(lambda blk, g0: _ind(blk > cand))
            ok = jnp.where(pend > 0.5, _ind(c_ge >= ksel), 0.0)
            return (jnp.where(ok > 0.5, 1.0, tied), jnp.where(ok > 0.5, cand, v),
                    jnp.where(ok > 0.5, ksel - c_gt, need))

        tied, v, need = lax.cond(jnp.max(pend) > 0.5, check, lambda _: (tied, v, need), 0)
        return st, tied, v, need

    zeros1 = jnp.zeros((tq, 1), F32)
    (lo_f, _, _), tied, v_tie, need = lax.while_loop(round_cond, round_body, (state, zeros1, zeros1, zeros1))
    vth = jnp.where(all_sel, F32_LOWEST, jnp.where(tied > 0.5, v_tie, lo_f))

    @pl.when(jnp.max(tied) > 0.5)
    def _():
        v_eq = jnp.where(tied > 0.5, v_tie, jnp.inf)
        incl = (_iota((tk, tk), 0) <= _iota((tk, tk), 1)).astype(BF16)

        def demote(b, seen):
            b0 = pl.multiple_of(b * tk, tk)
            x = sc_ref[:, pl.ds(b0, tk)]
            eq = _ind(x == v_eq)
            rank = jnp.dot(eq.astype(BF16), incl, preferred_element_type=F32) + seen
            sc_ref[:, pl.ds(b0, tk)] = jnp.where(eq * _ind(rank > need) > 0.5, -jnp.inf, x)
            return seen + jnp.sum(eq, axis=-1, keepdims=True)

        lax.fori_loop(0, i + 1, demote, zeros1)

    col_w = _iota((tq, wide), 1)
    g_near = jnp.maximum(i - 1, 0) // per_wide

    def logit_group(g, mx, near):
        g0 = pl.multiple_of(g * wide, wide)
        mb = jnp.where(sc_ref[:, pl.ds(g0, wide)] >= vth, 0.0, NEG_BIG)
        out = []
        for hh in range(nh):
            kh = k_ref[pl.ds(g0, wide), hh * d:(hh + 1) * d]
            lm = lax.dot_general(q2_ref[:, hh * d:(hh + 1) * d], kh, (((1,), (1,)), ((), ())),
                                 preferred_element_type=F32) + mb
            if near:
                back = [jnp.clip(i - (g * per_wide + pb), 0, 2) for pb in range(per_wide)]
                lm = lm + jnp.concatenate([bias_ref[back[pb], hh] for pb in range(per_wide)], axis=-1)
            lg_ref[hh, :, pl.ds(g0, wide)] = lm
            out.append(jnp.maximum(mx[hh], fold(lm, jnp.maximum)))
        return tuple(out)

    mx = tuple(jnp.full((tq, tk), NEG_BIG, F32) for _ in range(nh))
    mx = lax.fori_loop(0, g_near, functools.partial(logit_group, near=False), mx)
    mx = lax.fori_loop(g_near, n_wide, functools.partial(logit_group, near=True), mx)

    m_rows = [jnp.broadcast_to(jnp.max(mx[hh], axis=-1, keepdims=True), (tq, tk)) for hh in range(nh)]
    for hh in range(nh):
        l_ref[hh] = jnp.zeros((tq, tk), F32)
        acc_ref[hh] = jnp.zeros((tq, d), F32)

    def pv_body(g, carry):
        g0 = pl.multiple_of(g * wide, wide)
        for hh in range(nh):
            lm = lg_ref[hh, :, pl.ds(g0, wide)]
            p = jnp.concatenate([jnp.exp2(lm[:, pb * tk:(pb + 1) * tk] - m_rows[hh])
                                 for pb in range(per_wide)], axis=-1)
            vh = v_ref[pl.ds(g0, wide), hh * d:(hh + 1) * d]
            l_ref[hh] += fold(p)
            acc_ref[hh] += jnp.dot(p.astype(BF16), vh, preferred_element_type=F32)
        return carry

    lax.fori_loop(0, n_wide, pv_body, 0)
    for hh in range(nh):
        o_ref[:, hh * d:(hh + 1) * d] = acc_ref[hh] / jnp.sum(l_ref[hh], axis=-1, keepdims=True)


def _dsa(p32, p16, bias_tiles, *, tq, cols):
    bsz, s, _ = p32.shape
    d = HEAD_DIM
    nh = N_HEADS
    wide = 4 * tq
    k_sel = min(TOPK_MAX, s // 4)
    w512 = nh * d
    kernel = functools.partial(_dsa_kernel, tq=tq, k_sel=k_sel, wi_lane=cols["wi_lane"], wide=wide)
    resident = dict(pipeline_mode=pl.Buffered(1))
    return pl.pallas_call(
        kernel,
        grid=(bsz, s // tq),
        in_specs=[pl.BlockSpec((None, tq, w512), lambda b, i: (b, i, cols["qi"] // nh)),
                  pl.BlockSpec((None, tq, d), lambda b, i: (b, i, cols["small"])),
                  pl.BlockSpec((None, tq, w512), lambda b, i: (b, i, cols["qb"] // nh)),
                  pl.BlockSpec((None, s, d), lambda b, i: (b, 0, cols["small"]), **resident),
                  pl.BlockSpec((None, s, w512), lambda b, i: (b, 0, cols["kb"] // nh), **resident),
                  pl.BlockSpec((None, s, w512), lambda b, i: (b, 0, cols["vb"] // nh), **resident),
                  pl.BlockSpec((3, nh, tq, tq), lambda b, i: (0, 0, 0, 0), **resident)],
        out_specs=pl.BlockSpec((None, tq, w512), lambda b, i: (b, i, 0)),
        out_shape=jax.ShapeDtypeStruct((bsz, s, w512), F32),
        scratch_shapes=[pltpu.VMEM((tq, s + wide - tq), F32),
                        pltpu.VMEM((tq, s + wide - tq), BF16),
                        pltpu.VMEM((IDX_HEADS, tq, tq), F32),
                        pltpu.VMEM((IDX_HEADS, tq, 3 * IDX_DIM), BF16),
                        pltpu.VMEM((3 * IDX_DIM, s), BF16),
                        pltpu.VMEM((tq, w512), BF16),
                        pltpu.VMEM((nh, tq, s), F32),
                        pltpu.VMEM((nh, tq, tq), F32), pltpu.VMEM((nh, tq, d), F32)],
        compiler_params=pltpu.CompilerParams(
            dimension_semantics=("parallel", "arbitrary"), vmem_limit_bytes=VMEM_LIMIT),
        name="dsa",
    )(p32, p32, p32, p32, p16, p16, bias_tiles)


def _t5_bucket(rel):
    nb = REL_BUCKETS // 2
    max_exact = nb // 2
    ret = jnp.where(rel > 0, nb, 0)
    n = jnp.abs(rel)
    large = max_exact + (jnp.log(jnp.maximum(n, 1).astype(F32) / max_exact)
                         / math.log(REL_MAX_DIST / max_exact) * (nb - max_exact)).astype(jnp.int32)
    large = jnp.minimum(large, nb - 1)
    return ret + jnp.where(n < max_exact, n, large)


def _bias_tiles(rel_table, tq):
    assert tq >= REL_MAX_DIST
    t = jnp.arange(tq)
    tiles = []
    for back in range(3):
        rel = (t[None, :] - back * tq) - t[:, None]
        tiles.append(rel_table.astype(F32)[_t5_bucket(rel)].transpose(2, 0, 1))
    tiles = jnp.stack(tiles)
    return (tiles - tiles[2:3]) * LOG2E


def _even_layout(w_in):
    d = HEAD_DIM
    a_w = 2 * N_HEADS * d + N_HEADS * d
    offs = {}
    o = 0
    for name, w in (("qkv", a_w), ("z", N_HEADS * d), ("a", N_HEADS), ("b", N_HEADS),
                    ("qb", N_HEADS * d), ("kb", N_HEADS * d), ("vb", N_HEADS * d),
                    ("qi", IDX_HEADS * IDX_DIM), ("ki", IDX_DIM), ("wi", IDX_HEADS)):
        offs[name] = (o, o + w)
        o += w
    assert o == w_in.shape[1]
    sl = lambda n: w_in[:, offs[n][0]:offs[n][1]]
    small_w = IDX_DIM + 2 * N_HEADS + IDX_HEADS
    small_pad = -small_w % d
    w = jnp.concatenate([sl("qkv"), sl("z"), sl("qb"), sl("kb"), sl("vb"), sl("qi"),
                         sl("ki"), sl("a"), sl("b"), sl("wi"),
                         jnp.zeros((w_in.shape[0], small_pad), w_in.dtype)], axis=1)
    nh = N_HEADS
    cols = dict(qa=0, ka=nh, va=2 * nh, za=3 * nh, qb=4 * nh, kb=5 * nh, vb=6 * nh, qi=7 * nh,
                small=8 * nh, a_lane=IDX_DIM, b_lane=IDX_DIM + nh, wi_lane=IDX_DIM + 2 * nh)
    return w.astype(BF16), cols


def kernel(x, norm_g, w_in_even, conv_w_even, a_log_even, dt_bias_even, a_norm_even, w_out_even,
           rel_bias, w_in_odd, lb_logits, d_norm_odd, w_out_odd, w_gate, w_up, w_down):
    bsz, s, d = x.shape
    t = bsz * s
    depth = norm_g.shape[0]
    nh = N_HEADS
    tq = 128
    lb_all = jnp.cumsum(jax.nn.softmax(lb_logits.astype(F32), axis=0), axis=0)
    lb_all = lb_all - lb_all[:1]
    odd_cols = dict(qc=0, kc=nh, vc=2 * nh, qd=3 * nh, fd=4 * nh, id=5 * nh, gd=6 * nh)
    bias_tiles = _bias_tiles(rel_bias, tq)

    h = x.reshape(t, d)
    for l in range(depth):
        if l % 2 == 0:
            e = l // 2
            w_even, cols = _even_layout(w_in_even[e])
            p32, p16 = _norm_matmul(h, norm_g[l, 0], w_even, tm=512, tn=w_even.shape[1] // 3)
            p32 = p32.reshape(bsz, s, -1)
            p16 = p16.reshape(bsz, s, -1)
            o_1 = _deltanet(p32, conv_w_even[e], a_log_even[e], dt_bias_even[e], a_norm_even[e],
                            ts=min(512, s), cols=cols)
            o_2 = _dsa(p32, p16, bias_tiles, tq=tq, cols=cols)
            w_out = w_out_even[e]
        else:
            o = l // 2
            p32, p16 = _norm_matmul(h, norm_g[l, 0], w_in_odd[o].astype(BF16), tm=512, tn=512)
            p32 = p32.reshape(bsz, s, -1)
            p16 = p16.reshape(bsz, s, -1)
            o_1 = _stickbreak(p16, tq=tq, cols=odd_cols)
            o_2 = _hgrn2(p32, lb_all[l], d_norm_odd[o], ts=min(512, s), cols=odd_cols)
            w_out = w_out_odd[o]
        h = _outproj(o_1.reshape(t, -1), o_2.reshape(t, -1), w_out, h, norm_g[l, 1], tm=512)
        h = _ffn(h, norm_g[l, 2], norm_g[l, 3], w_gate[l], w_up[l], w_down[l], tm=1024, tf=256)
    return h.reshape(bsz, s, d)
```

```python
import functools
import math

import jax
import jax.numpy as jnp
from jax import lax
from jax.experimental import pallas as pl
from jax.experimental.pallas import tpu as pltpu

F32 = jnp.float32
BF16 = jnp.bfloat16
HIGHEST = lax.Precision.HIGHEST

CHUNK = 64
HEAD_DIM = 128
N_HEADS = 4
IDX_HEADS = 8
IDX_DIM = 64
TOPK_MAX = 256
CONV_WIDTH = 4
REL_BUCKETS = 32
REL_MAX_DIST = 128
EPS = 1e-6
NEG_BIG = -1e30
LOG2E = 1.4426950408889634
BISECT_COARSE = 12
BISECT_FIXED = 8
BISECT_EXTRA = 6
F32_LOWEST = -3.4028234663852886e38
EXP_ZERO_BELOW = -104.0
VMEM_LIMIT = 56 * 1024 * 1024


def _mm(a, b):
    return jnp.dot(a.astype(BF16), b.astype(BF16), preferred_element_type=F32)


def _mm_nt(a, b):
    return lax.dot_general(a.astype(BF16), b.astype(BF16), (((1,), (1,)), ((), ())),
                           preferred_element_type=F32)


def _mm_tn(a, b):
    return lax.dot_general(a.astype(BF16), b.astype(BF16), (((0,), (0,)), ((), ())),
                           preferred_element_type=F32)


def _mm_f32(a, b):
    return jnp.dot(a, b, precision=HIGHEST, preferred_element_type=F32)


def _split(x):
    hi = x.astype(BF16)
    return hi, (x - hi.astype(F32)).astype(BF16)


def _mm_x3(a, b):
    a_hi, a_lo = _split(a)
    b_hi, b_lo = _split(b)
    return jnp.dot(jnp.concatenate([a_hi, a_hi, a_lo], axis=1),
                   jnp.concatenate([b_hi, b_lo, b_hi], axis=0), preferred_element_type=F32)


def _floor_bf16(x):
    bits = pltpu.bitcast(x, jnp.int32)
    down = jnp.where(bits >= 0, bits, bits + 0xFFFF) & jnp.int32(-65536)
    return pltpu.bitcast(down, F32).astype(BF16)


def _sigmoid(x):
    return 1.0 / (1.0 + jnp.exp(-x))


def _silu(x):
    return x * _sigmoid(x)


def _softplus(x):
    return jnp.maximum(x, 0.0) + jnp.log1p(jnp.exp(-jnp.abs(x)))


def _rms(x, g):
    return x * lax.rsqrt(jnp.mean(x * x, axis=-1, keepdims=True) + EPS) * g


def _iota(shape, dim):
    return lax.broadcasted_iota(jnp.int32, shape, dim)


def _ind(mask):
    return jnp.where(mask, 1.0, 0.0)


def _norm_matmul_kernel(x_ref, g_ref, w_ref, *rest, n_t):
    if n_t:
        wt_ref, o32_ref, o16_ref, ot_ref, xn_ref = rest
    else:
        o32_ref, o16_ref, xn_ref = rest

    @pl.when(pl.program_id(1) == 0)
    def _():
        xn_ref[...] = _rms(x_ref[...], g_ref[...]).astype(BF16)
        if n_t:
            ot_ref[...] = lax.dot_general(wt_ref[...], xn_ref[...], (((1,), (1,)), ((), ())),
                                          preferred_element_type=F32).astype(BF16)

    y = jnp.dot(xn_ref[...], w_ref[...], preferred_element_type=F32)
    o32_ref[...] = y
    o16_ref[...] = y.astype(BF16)


def _norm_matmul(x, g, w, *, tm, tn, w_t=None):
    t, d = x.shape
    n = w.shape[1]
    n_t = 0 if w_t is None else w_t.shape[0]
    in_specs = [pl.BlockSpec((tm, d), lambda i, j: (i, 0)),
                pl.BlockSpec((1, d), lambda i, j: (0, 0)),
                pl.BlockSpec((d, tn), lambda i, j: (0, j))]
    out_specs = [pl.BlockSpec((tm, tn), lambda i, j: (i, j)),
                 pl.BlockSpec((tm, tn), lambda i, j: (i, j))]
    out_shape = [jax.ShapeDtypeStruct((t, n), F32), jax.ShapeDtypeStruct((t, n), BF16)]
    args = [x, g.reshape(1, d), w]
    if n_t:
        in_specs.append(pl.BlockSpec((n_t, d), lambda i, j: (0, 0)))
        out_specs.append(pl.BlockSpec((n_t, tm), lambda i, j: (0, i)))
        out_shape.append(jax.ShapeDtypeStruct((n_t, t), BF16))
        args.append(w_t)
    return pl.pallas_call(
        functools.partial(_norm_matmul_kernel, n_t=n_t),
        grid=(t // tm, n // tn),
        in_specs=in_specs,
        out_specs=out_specs,
        out_shape=out_shape,
        scratch_shapes=[pltpu.VMEM((tm, d), BF16)],
        compiler_params=pltpu.CompilerParams(
            dimension_semantics=("parallel", "arbitrary"), vmem_limit_bytes=VMEM_LIMIT),
        name="norm_matmul",
    )(*args)


def _outproj_kernel(ca_ref, cb_ref, wa_ref, wb_ref, h_ref, g_ref, o_ref):
    y = (jnp.dot(ca_ref[...].astype(BF16), wa_ref[...], preferred_element_type=F32)
         + jnp.dot(cb_ref[...].astype(BF16), wb_ref[...], preferred_element_type=F32))
    o_ref[...] = h_ref[...] + _rms(y, g_ref[...])


def _outproj(ca, cb, w, h, g, *, tm):
    t, d = h.shape
    wa_n = ca.shape[1]
    wb_n = cb.shape[1]
    wa = w[:wa_n].astype(BF16)
    wb = w[wa_n:].astype(BF16)
    return pl.pallas_call(
        _outproj_kernel,
        grid=(t // tm,),
        in_specs=[pl.BlockSpec((tm, wa_n), lambda i: (i, 0)),
                  pl.BlockSpec((tm, wb_n), lambda i: (i, 0)),
                  pl.BlockSpec((wa_n, d), lambda i: (0, 0)),
                  pl.BlockSpec((wb_n, d), lambda i: (0, 0)),
                  pl.BlockSpec((tm, d), lambda i: (i, 0)),
                  pl.BlockSpec((1, d), lambda i: (0, 0))],
        out_specs=pl.BlockSpec((tm, d), lambda i: (i, 0)),
        out_shape=jax.ShapeDtypeStruct((t, d), F32),
        compiler_params=pltpu.CompilerParams(
            dimension_semantics=("parallel",), vmem_limit_bytes=VMEM_LIMIT),
        name="outproj",
    )(ca, cb, wa, wb, h, g.reshape(1, d))


def _ffn_kernel(h_ref, gpre_ref, gpost_ref, wg_ref, wu_ref, wd_ref, o_ref, xn_ref, acc_ref):
    f = pl.program_id(1)

    @pl.when(f == 0)
    def _():
        xn_ref[...] = _rms(h_ref[...], gpre_ref[...]).astype(BF16)
        acc_ref[...] = jnp.zeros_like(acc_ref)

    xn = xn_ref[...]
    gate = jnp.dot(xn, wg_ref[...], preferred_element_type=F32)
    up = jnp.dot(xn, wu_ref[...], preferred_element_type=F32)
    act = (_silu(gate) * up).astype(BF16)
    acc_ref[...] += jnp.dot(act, wd_ref[...], preferred_element_type=F32)

    @pl.when(f == pl.num_programs(1) - 1)
    def _():
        o_ref[...] = h_ref[...] + _rms(acc_ref[...], gpost_ref[...])


def _ffn(h, g_pre, g_post, wg, wu, wd, *, tm, tf):
    t, d = h.shape
    ff = wg.shape[1]
    return pl.pallas_call(
        _ffn_kernel,
        grid=(t // tm, ff // tf),
        in_specs=[pl.BlockSpec((tm, d), lambda i, f: (i, 0)),
                  pl.BlockSpec((1, d), lambda i, f: (0, 0)),
                  pl.BlockSpec((1, d), lambda i, f: (0, 0)),
                  pl.BlockSpec((d, tf), lambda i, f: (0, f)),
                  pl.BlockSpec((d, tf), lambda i, f: (0, f)),
                  pl.BlockSpec((tf, d), lambda i, f: (f, 0))],
        out_specs=pl.BlockSpec((tm, d), lambda i, f: (i, 0)),
        out_shape=jax.ShapeDtypeStruct((t, d), F32),
        scratch_shapes=[pltpu.VMEM((tm, d), BF16), pltpu.VMEM((tm, d), F32)],
        compiler_params=pltpu.CompilerParams(
            dimension_semantics=("parallel", "arbitrary"), vmem_limit_bytes=VMEM_LIMIT),
        name="ffn",
    )(h, g_pre.reshape(1, d), g_post.reshape(1, d),
      wg.astype(BF16), wu.astype(BF16), wd.astype(BF16))


def _deltanet_kernel(xq_ref, xk_ref, xv_ref, z_ref, sm_ref, cwq_ref, cwk_ref, cwv_ref,
                     alog_ref, dtb_ref, gn_ref, o_ref,
                     xpad_ref, q_ref, k_ref, v_ref, gb_ref, bb_ref, u_ref, w_ref, qk_ref, st_ref,
                     *, ts, a_col, b_col):
    h = pl.program_id(1)
    s = pl.program_id(2)
    c = CHUNK
    d = HEAD_DIM

    @pl.when(s == 0)
    def _():
        xpad_ref[:, 0:8, :] = jnp.zeros((3, 8, d), F32)
        st_ref[...] = jnp.zeros_like(st_ref)

    @pl.when(s != 0)
    def _():
        xpad_ref[:, 0:8, :] = xpad_ref[:, ts:ts + 8, :]

    xpad_ref[0, 8:ts + 8, :] = xq_ref[...]
    xpad_ref[1, 8:ts + 8, :] = xk_ref[...]
    xpad_ref[2, 8:ts + 8, :] = xv_ref[...]

    def conv_silu(idx, cw_ref):
        cw = cw_ref[...]
        acc = xpad_ref[idx, 8 - (CONV_WIDTH - 1):8 - (CONV_WIDTH - 1) + ts, :] * cw[0:1, :]
        for j in range(1, CONV_WIDTH):
            off = 8 - (CONV_WIDTH - 1) + j
            acc = acc + xpad_ref[idx, off:off + ts, :] * cw[j:j + 1, :]
        return _silu(acc)

    def l2norm(t):
        return t * lax.rsqrt(jnp.sum(t * t, axis=-1, keepdims=True) + EPS)

    q_ref[...] = l2norm(conv_silu(0, cwq_ref)) * (d ** -0.5)
    k_ref[...] = l2norm(conv_silu(1, cwk_ref))
    v_ref[...] = conv_silu(2, cwv_ref)

    sm = sm_ref[...]
    lane = _iota(sm.shape, 1)
    a_raw = jnp.sum(jnp.where(lane == a_col + h, sm, 0.0), axis=-1, keepdims=True)
    b_raw = jnp.sum(jnp.where(lane == b_col + h, sm, 0.0), axis=-1, keepdims=True)
    hl = _iota((1, d), 1)
    a_log = jnp.sum(jnp.where(hl == h, alog_ref[...], 0.0), axis=-1, keepdims=True)
    dtb = jnp.sum(jnp.where(hl == h, dtb_ref[...], 0.0), axis=-1, keepdims=True)
    g = -jnp.exp(a_log) * _softplus(a_raw + dtb)
    gb_ref[...] = jnp.broadcast_to(g, (ts, d))
    bb_ref[...] = jnp.broadcast_to(_sigmoid(b_raw), (ts, d))

    row = _iota((c, c), 0)
    col = _iota((c, c), 1)
    tri = (col <= row)
    strict = (col < row)
    tri_f = tri.astype(F32)
    upper_f = (row <= col).astype(F32)
    eye = (row == col).astype(F32)
    gnorm = gn_ref[...]

    chunks = range(ts // c)
    rs = [slice(ci * c, (ci + 1) * c) for ci in chunks]
    tri2 = jnp.concatenate([tri_f, tri_f], axis=1).astype(BF16)
    ones2 = jnp.ones((c, 2 * c), BF16)

    def cum2(lhs2, x):
        hi, lo = _split(x)
        return jnp.dot(lhs2, jnp.concatenate([hi, lo], axis=0), preferred_element_type=F32)

    q = [q_ref[r, :] for r in rs]
    k = [k_ref[r, :] for r in rs]
    beta = [bb_ref[r, :] for r in rs]
    gb = [gb_ref[r, :] for r in rs]
    gc = [cum2(tri2, x) for x in gb]
    gc_row = [cum2(ones2, x[:, :c] * upper_f) for x in gb]
    decay = [jnp.where(tri, jnp.exp(jnp.minimum(a[:, :c] - b, 0.0)), 0.0) for a, b in zip(gc, gc_row)]
    kk = [_mm_nt(x, x) for x in k]
    n = [-jnp.where(strict, b[:, :c] * x * dc, 0.0) for b, x, dc in zip(beta, kk, decay)]
    inv = [eye + x for x in n]
    for _ in range(5):
        n = [_mm_x3(x, x) for x in n]
        inv = [iv + _mm_x3(iv, x) for iv, x in zip(inv, n)]
    egc = [jnp.exp(x) for x in gc]
    gl = [x[c - 1:c, :] for x in gc]
    for ci in chunks:
        r = rs[ci]
        u_ref[r, :] = _mm_x3(inv[ci], v_ref[r, :] * beta[ci])
        w_ref[r, :] = _mm_x3(inv[ci], k[ci] * (beta[ci] * egc[ci]))
        qk_ref[r, :] = _mm_nt(q[ci], k[ci]) * decay[ci]
        q_ref[r, :] = q[ci] * egc[ci]
        k_ref[r, :] = k[ci] * jnp.exp(gl[ci] - gc[ci])
        gb_ref[r, :] = jnp.broadcast_to(jnp.exp(gl[ci]), (c, d))

    def chunk_body(ci, carry):
        r0 = pl.multiple_of(ci * c, c)
        st = st_ref[...]
        v_new = u_ref[pl.ds(r0, c), :] - _mm(w_ref[pl.ds(r0, c), :], st)
        o = _mm(q_ref[pl.ds(r0, c), :], st) + _mm(qk_ref[pl.ds(r0, c), :], v_new)
        st_ref[...] = st * gb_ref[pl.ds(r0, 1), :] + _mm_tn(k_ref[pl.ds(r0, c), :], v_new)
        zc = z_ref[pl.ds(r0, c), :]
        o_ref[pl.ds(r0, c), :] = _rms(o, gnorm) * _silu(zc)
        return carry

    lax.fori_loop(0, ts // c, chunk_body, 0)


def _deltanet(p32, conv_w, a_log, dt_bias, a_norm_g, *, ts, cols):
    bsz, s, _ = p32.shape
    d = HEAD_DIM
    nh = N_HEADS
    qb, kb, vb, zb, smb = cols["qa"], cols["ka"], cols["va"], cols["za"], cols["small"]
    pad = lambda t: jnp.pad(t.astype(F32), (0, d - t.shape[0])).reshape(1, d)
    kernel = functools.partial(_deltanet_kernel, ts=ts, a_col=cols["a_lane"], b_col=cols["b_lane"])
    return pl.pallas_call(
        kernel,
        grid=(bsz, nh, s // ts),
        in_specs=[pl.BlockSpec((None, ts, d), lambda b, h, i: (b, i, qb + h)),
                  pl.BlockSpec((None, ts, d), lambda b, h, i: (b, i, kb + h)),
                  pl.BlockSpec((None, ts, d), lambda b, h, i: (b, i, vb + h)),
                  pl.BlockSpec((None, ts, d), lambda b, h, i: (b, i, zb + h)),
                  pl.BlockSpec((None, ts, d), lambda b, h, i: (b, i, smb)),
                  pl.BlockSpec((CONV_WIDTH, d), lambda b, h, i: (0, h)),
                  pl.BlockSpec((CONV_WIDTH, d), lambda b, h, i: (0, nh + h)),
                  pl.BlockSpec((CONV_WIDTH, d), lambda b, h, i: (0, 2 * nh + h)),
                  pl.BlockSpec((1, d), lambda b, h, i: (0, 0)),
                  pl.BlockSpec((1, d), lambda b, h, i: (0, 0)),
                  pl.BlockSpec((1, d), lambda b, h, i: (0, 0))],
        out_specs=pl.BlockSpec((None, ts, d), lambda b, h, i: (b, i, h)),
        out_shape=jax.ShapeDtypeStruct((bsz, s, nh * d), F32),
        scratch_shapes=[pltpu.VMEM((3, ts + 8, d), F32),
                        pltpu.VMEM((ts, d), F32), pltpu.VMEM((ts, d), F32), pltpu.VMEM((ts, d), F32),
                        pltpu.VMEM((ts, d), F32), pltpu.VMEM((ts, d), F32),
                        pltpu.VMEM((ts, d), F32), pltpu.VMEM((ts, d), F32),
                        pltpu.VMEM((ts, CHUNK), F32),
                        pltpu.VMEM((d, d), F32)],
        compiler_params=pltpu.CompilerParams(
            dimension_semantics=("parallel", "parallel", "arbitrary"), vmem_limit_bytes=VMEM_LIMIT),
        name="deltanet",
    )(p32, p32, p32, p32, p32, conv_w.astype(F32), conv_w.astype(F32), conv_w.astype(F32),
      pad(a_log), pad(dt_bias), a_norm_g.astype(F32).reshape(1, d))


def _hgrn2_kernel(q_ref, f_ref, i_ref, gate_ref, lb_ref, gn_ref, o_ref,
                  qs_ref, ks_ref, gc_ref, st_ref, *, ts):
    s = pl.program_id(2)
    c = CHUNK
    d = HEAD_DIM
    SUB = 16

    @pl.when(s == 0)
    def _():
        st_ref[...] = jnp.zeros_like(st_ref)

    lb = lb_ref[...]
    f_raw = f_ref[...]
    log_sig = jnp.minimum(f_raw, 0.0) - jnp.log1p(jnp.exp(-jnp.abs(f_raw)))
    la = jnp.log(lb)
    lbb = jnp.log1p(-lb) + log_sig
    log_f = jnp.maximum(la, lbb) + jnp.log1p(jnp.exp(-jnp.abs(la - lbb)))
    qs_ref[...] = _silu(q_ref[...])
    ks_ref[...] = (1.0 - lb) * _sigmoid(-f_raw)

    row = _iota((c, c), 0)
    col = _iota((c, c), 1)
    tri_f = (col <= row).astype(F32)
    ones_dd = jnp.ones((d, d), BF16)
    rows_8d = _iota((8, d), 0)
    gnorm = gn_ref[...]

    for ci in range(ts // c):
        gc_ref[ci * c:(ci + 1) * c, :] = _mm_f32(tri_f, log_f[ci * c:(ci + 1) * c, :])

    def chunk_loop(ci, carry):
        r0 = pl.multiple_of(ci * c, c)
        q = qs_ref[pl.ds(r0, c), :]
        k = ks_ref[pl.ds(r0, c), :]
        v = i_ref[pl.ds(r0, c), :]
        gc = gc_ref[pl.ds(r0, c), :]

        blocks = [(sb * SUB, (sb + 1) * SUB) for sb in range(c // SUB)]
        prods = []
        for top, end in blocks:
            for j in range(top, end):
                lo = (j // 8) * 8
                k_j = ks_ref[pl.ds(r0 + j, 1), :]
                g_j = gc_ref[pl.ds(r0 + j, 1), :]
                e = jnp.exp(jnp.minimum(gc[lo:end, :] - g_j, 0.0))
                if j % 8:
                    head = jnp.where(rows_8d >= j - lo, e[:8], 0.0)
                    e = jnp.concatenate([head, e[8:]], axis=0) if lo + 8 < end else head
                prods.append(q[lo:end, :] * k_j * e)
        sums = jnp.dot(jnp.concatenate(prods, axis=0).astype(BF16), ones_dd,
                       preferred_element_type=F32)
        qk_far = []
        for top, end in blocks[1:]:
            g_b = gc[top - 1:top, :]
            qe = q[top:end, :] * jnp.exp(gc[top:end, :] - g_b)
            ke = k[:top, :] * jnp.exp(jnp.minimum(g_b - gc[:top, :], 0.0))
            qk_far.append(_mm_nt(qe, ke))
        far = [_mm(a, v[:top, :]) for a, (top, _) in zip(qk_far, blocks[1:])]

        groups = [jnp.zeros((8, d), F32) for _ in range(c // 8)]
        at = 0
        for top, end in blocks:
            for j in range(top, end):
                v_j = i_ref[pl.ds(r0 + j, 1), :]
                for g in range(j // 8, end // 8):
                    groups[g] = groups[g] + sums[at:at + 8, :] * v_j
                    at += 8
        for f, (top, end) in zip(far, blocks[1:]):
            for g in range(top // 8, end // 8):
                groups[g] = groups[g] + f[(g * 8 - top):(g * 8 - top + 8), :]
        o_intra = jnp.concatenate(groups, axis=0)

        st = st_ref[...]
        gl = gc[c - 1:c, :]
        o = o_intra + _mm_nt(q * jnp.exp(gc), st)
        st_ref[...] = st * jnp.exp(gl) + _mm_tn(v, k * jnp.exp(gl - gc))
        o_ref[pl.ds(r0, c), :] = _rms(o, gnorm) * _silu(gate_ref[pl.ds(r0, c), :])
        return carry

    lax.fori_loop(0, ts // c, chunk_loop, 0)


def _hgrn2(p32, lb, d_norm_g, *, ts, cols):
    bsz, s, _ = p32.shape
    d = HEAD_DIM
    nh = N_HEADS
    qb, fb, ib, gb = cols["qd"], cols["fd"], cols["id"], cols["gd"]
    kernel = functools.partial(_hgrn2_kernel, ts=ts)
    return pl.pallas_call(
        kernel,
        grid=(bsz, nh, s // ts),
        in_specs=[pl.BlockSpec((None, ts, d), lambda b, h, i: (b, i, qb + h)),
                  pl.BlockSpec((None, ts, d), lambda b, h, i: (b, i, fb + h)),
                  pl.BlockSpec((None, ts, d), lambda b, h, i: (b, i, ib + h)),
                  pl.BlockSpec((None, ts, d), lambda b, h, i: (b, i, gb + h)),
                  pl.BlockSpec((1, d), lambda b, h, i: (0, h)),
                  pl.BlockSpec((1, d), lambda b, h, i: (0, 0))],
        out_specs=pl.BlockSpec((None, ts, d), lambda b, h, i: (b, i, h)),
        out_shape=jax.ShapeDtypeStruct((bsz, s, nh * d), F32),
        scratch_shapes=[pltpu.VMEM((ts, d), F32), pltpu.VMEM((ts, d), F32),
                        pltpu.VMEM((ts, d), F32), pltpu.VMEM((d, d), F32)],
        compiler_params=pltpu.CompilerParams(
            dimension_semantics=("parallel", "parallel", "arbitrary"), vmem_limit_bytes=VMEM_LIMIT),
        name="hgrn2",
    )(p32, p32, p32, p32, lb.astype(F32).reshape(1, nh * d), d_norm_g.astype(F32).reshape(1, d))


def _stickbreak_kernel(q_ref, k_ref, v_ref, o_ref, *, tq):
    i = pl.program_id(1)
    d = HEAD_DIM
    nh = N_HEADS
    row = _iota((tq, tq), 0)
    col = _iota((tq, tq), 1)
    causal = col < row
    later = (row > col).astype(BF16)

    def block(j, carries, diag):
        r0 = pl.multiple_of(j * tq, tq)
        out = []
        for hh in range(nh):
            hs = slice(hh * d, (hh + 1) * d)
            z = _mm_nt(q_ref[:, hs], k_ref[pl.ds(r0, tq), hs]) * (d ** -0.5)
            sp = _softplus(z)
            l1m = jnp.where(causal, -sp, 0.0) if diag else -sp
            l_hi, l_lo = _split(l1m)
            rest = (jnp.dot(l_hi, later, preferred_element_type=F32)
                    + jnp.dot(l_lo, later, preferred_element_type=F32))
            p = jnp.exp((z - sp) + rest + carries[hh])
            if diag:
                p = jnp.where(causal, p, 0.0)
            pv = _mm(p, v_ref[pl.ds(r0, tq), hs])
            if diag:
                o_ref[:, hs] = pv
            else:
                o_ref[:, hs] += pv
            out.append(carries[hh] + jnp.sum(l1m, axis=-1, keepdims=True))
        return tuple(out)

    carries = block(i, tuple(jnp.zeros((tq, 1), F32) for _ in range(nh)), True)

    def cond(c):
        worst = functools.reduce(jnp.maximum, c[1])
        return jnp.logical_and(c[0] >= 0, jnp.max(worst) >= EXP_ZERO_BELOW)

    def body(c):
        return c[0] - 1, block(c[0], c[1], False)

    lax.while_loop(cond, body, (i - 1, carries))


def _stickbreak(p16, *, tq, cols):
    bsz, s, _ = p16.shape
    nh = N_HEADS
    w = nh * HEAD_DIM
    kernel = functools.partial(_stickbreak_kernel, tq=tq)
    resident = dict(pipeline_mode=pl.Buffered(1))
    return pl.pallas_call(
        kernel,
        grid=(bsz, s // tq),
        in_specs=[pl.BlockSpec((None, tq, w), lambda b, i: (b, i, cols["qc"] // nh)),
                  pl.BlockSpec((None, s, w), lambda b, i: (b, 0, cols["kc"] // nh), **resident),
                  pl.BlockSpec((None, s, w), lambda b, i: (b, 0, cols["vc"] // nh), **resident)],
        out_specs=pl.BlockSpec((None, tq, w), lambda b, i: (b, i, 0)),
        out_shape=jax.ShapeDtypeStruct((bsz, s, w), F32),
        compiler_params=pltpu.CompilerParams(
            dimension_semantics=("parallel", "arbitrary"), vmem_limit_bytes=VMEM_LIMIT),
        name="stickbreak",
    )(p16, p16, p16)


def _dsa_kernel(qi_ref, smq_ref, q_ref, sm_ref, k_ref, vt_ref, bias_ref, o_ref,
                sc_ref, scb_ref, wb_ref, qc_ref, kct_ref, bd_ref, lg_ref, *, tq, k_sel, wi_lane, wide):
    i = pl.program_id(1)
    tk = tq
    d = HEAD_DIM
    nh = N_HEADS
    ksel = float(k_sel)
    per_wide = wide // tk
    n_wide = (i + per_wide) // per_wide
    sub = 2 * tk
    lane_q = _iota((1, tq), 1)

    def tree(parts, op):
        while len(parts) > 1:
            parts = [op(parts[j], parts[j + 1]) if j + 1 < len(parts) else parts[j]
                     for j in range(0, len(parts), 2)]
        return parts[0]

    def col_fold(x, op=jnp.add, rows=8):
        return tree([x[r * rows:(r + 1) * rows] for r in range(x.shape[0] // rows)], op)

    @pl.when(i == 0)
    def _():
        def prep(g, carry):
            g0 = pl.multiple_of(g * wide, wide)
            kt = sm_ref[pl.ds(g0, wide), :].T[:IDX_DIM, :]
            hi, lo = _split(kt)
            kct_ref[:, pl.ds(g0, wide)] = jnp.concatenate([hi, lo, hi], axis=0)
            return carry
        lax.fori_loop(0, sm_ref.shape[0] // wide, prep, 0)

    smq = smq_ref[...]
    lane = _iota(smq.shape, 1)
    for hh in range(IDX_HEADS):
        qh = qi_ref[:, hh * IDX_DIM:(hh + 1) * IDX_DIM]
        hi, lo = _split(qh)
        qc_ref[hh] = jnp.concatenate([hi, hi, lo], axis=-1)
        w = jnp.sum(jnp.where(lane == wi_lane + hh, smq, 0.0), axis=-1, keepdims=True)
        wb_ref[hh] = jnp.broadcast_to(w * ((IDX_HEADS ** -0.5) * (IDX_DIM ** -0.5)), (tq, tk))

    q2t = (q_ref[...] * ((d ** -0.5) * LOG2E)).T.astype(BF16)
    zero_dq = jnp.zeros((d, tq), BF16)
    for p in range(nh // 2):
        top = jnp.concatenate([q2t[2 * p * d:(2 * p + 1) * d], zero_dq], axis=1)
        bot = jnp.concatenate([zero_dq, q2t[(2 * p + 1) * d:(2 * p + 2) * d]], axis=1)
        bd_ref[p] = jnp.concatenate([top, bot], axis=0)

    limit = i * tq + (lane_q // CHUNK + 1) * CHUNK
    rows_t = _iota((tk, tq), 0)

    def score_group(g, mm, masked):
        mn, mx = mm
        for sb in range(wide // sub):
            k0 = pl.multiple_of(g * wide + sb * sub, sub)
            kct = kct_ref[:, pl.ds(k0, sub)]
            tiles = [jnp.zeros((tq, tk), F32) for _ in range(sub // tk)]
            for hh in range(IDX_HEADS):
                s_h = jnp.dot(qc_ref[hh], kct, preferred_element_type=F32)
                for ti in range(sub // tk):
                    tiles[ti] = tiles[ti] + jnp.maximum(s_h[:, ti * tk:(ti + 1) * tk], 0.0) * wb_ref[hh]
            for ti in range(sub // tk):
                kb = pl.multiple_of(k0 + ti * tk, tk)
                sct = tiles[ti].T
                if masked:
                    adm = (kb + rows_t) < limit
                    mn = jnp.minimum(mn, col_fold(jnp.where(adm, sct, jnp.inf), jnp.minimum))
                    sct = jnp.where(adm, sct, -jnp.inf)
                else:
                    mn = jnp.minimum(mn, col_fold(sct, jnp.minimum))
                mx = jnp.maximum(mx, col_fold(sct, jnp.maximum))
                sc_ref[pl.ds(kb, tk), :] = sct
                scb_ref[pl.ds(kb, tk), :] = _floor_bf16(sct)
        return mn, mx

    mm = lax.fori_loop(0, n_wide - 1, functools.partial(score_group, masked=False),
                       (jnp.full((8, tq), jnp.inf, F32), jnp.full((8, tq), -jnp.inf, F32)))
    mn, mx = score_group(n_wide - 1, mm, True)
    rmin = jnp.min(mn, axis=0, keepdims=True)
    rmax = jnp.max(mx, axis=0, keepdims=True)

    def count(pred):
        def body(g, acc):
            blk = sc_ref[pl.ds(pl.multiple_of(g * wide, wide), wide), :]
            return acc + col_fold(pred(blk))
        return jnp.sum(lax.fori_loop(0, n_wide, body, jnp.zeros((8, tq), F32)), axis=0, keepdims=True)

    def max_below(x):
        def body(g, acc):
            blk = sc_ref[pl.ds(pl.multiple_of(g * wide, wide), wide), :]
            return jnp.maximum(acc, col_fold(jnp.where(blk < x, blk, -jnp.inf), jnp.maximum))
        return jnp.max(lax.fori_loop(0, n_wide, body, jnp.full((8, tq), -jnp.inf, F32)), axis=0, keepdims=True)

    n_adm = limit.astype(F32)
    all_sel = n_adm <= ksel

    def bisect(c):
        lo, hi, c_lo = c
        mid = 0.5 * lo + 0.5 * hi
        cm = count(lambda blk: _ind(blk >= mid))
        ge = cm >= ksel
        return jnp.where(ge, mid, lo), jnp.where(ge, hi, mid), jnp.where(ge, cm, c_lo)

    def pending(c_lo, tied):
        return jnp.where(all_sel, 0.0, jnp.where(tied > 0.5, 0.0, _ind(c_lo != ksel)))

    def bisect_coarse(_, c):
        lo, hi, c_lo = c
        mid = _floor_bf16(0.5 * lo + 0.5 * hi).astype(F32)
        t_b = jnp.broadcast_to(mid, (16, tq)).astype(BF16)
        one_b = jnp.ones((16, tq), BF16)
        zero_b = jnp.zeros((16, tq), BF16)

        def body(g, acc):
            blk = scb_ref[pl.ds(pl.multiple_of(g * wide, wide), wide), :]
            ind = [jnp.where(blk[r * 16:(r + 1) * 16] >= t_b, one_b, zero_b) for r in range(wide // 16)]
            return acc + tree(ind, jnp.add).astype(F32)

        acc = lax.fori_loop(0, n_wide, body, jnp.zeros((16, tq), F32))
        cm = jnp.sum(acc, axis=0, keepdims=True)
        ge = cm >= ksel
        return jnp.where(ge, mid, lo), jnp.where(ge, hi, mid), jnp.where(ge, cm, c_lo)

    lo0 = _floor_bf16(rmin).astype(F32)
    hi0 = _floor_bf16(rmax + (jnp.abs(rmax) * (2.0 ** -6) + 1e-30)).astype(F32)
    state = lax.fori_loop(0, BISECT_COARSE, bisect_coarse, (lo0, hi0, n_adm))
    state = lax.fori_loop(0, BISECT_FIXED, lambda _, c: bisect(c), state)

    def round_cond(c):
        return jnp.max(pending(c[0][2], c[1])) > 0.5

    def round_body(c):
        st, tied, v, need = c

        def more_cond(s):
            return jnp.logical_and(s[0] < BISECT_EXTRA, jnp.max(pending(s[1][2], tied)) > 0.5)

        _, st = lax.while_loop(more_cond, lambda s: (s[0] + 1, bisect(s[1])), (jnp.int32(0), st))
        pend = pending(st[2], tied)

        def check(_):
            cand = max_below(st[1])
            c_ge = count(lambda blk: _ind(blk >= cand))
            c_gt = count(lambda blk: _ind(blk > cand))
            ok = jnp.where(pend > 0.5, _ind(c_ge >= ksel), 0.0)
            return (jnp.where(ok > 0.5, 1.0, tied), jnp.where(ok > 0.5, cand, v),
                    jnp.where(ok > 0.5, ksel - c_gt, need))

        tied, v, need = lax.cond(jnp.max(pend) > 0.5, check, lambda _: (tied, v, need), 0)
        return st, tied, v, need

    zeros1 = jnp.zeros((1, tq), F32)
    (lo_f, _, _), tied, v_tie, need = lax.while_loop(round_cond, round_body, (state, zeros1, zeros1, zeros1))
    vth = jnp.where(all_sel, F32_LOWEST, jnp.where(tied > 0.5, v_tie, lo_f))

    @pl.when(jnp.max(tied) > 0.5)
    def _():
        v_eq = jnp.where(tied > 0.5, v_tie, jnp.inf)
        incl = (_iota((tk, tk), 1) <= _iota((tk, tk), 0)).astype(BF16)

        def demote(g, seen):
            g0 = pl.multiple_of(g * wide, wide)
            xs = [sc_ref[pl.ds(g0 + pb * tk, tk), :] for pb in range(per_wide)]
            eqs = [_ind(x == v_eq) for x in xs]
            inblk = [jnp.dot(incl, e.astype(BF16), preferred_element_type=F32) for e in eqs]
            for pb in range(per_wide):
                rank = inblk[pb] + seen
                sc_ref[pl.ds(g0 + pb * tk, tk), :] = jnp.where(eqs[pb] * _ind(rank > need) > 0.5,
                                                               -jnp.inf, xs[pb])
                seen = seen + jnp.sum(col_fold(eqs[pb]), axis=0, keepdims=True)
            return seen

        lax.fori_loop(0, n_wide, demote, zeros1)

    g_near = jnp.maximum(i - 1, 0) // per_wide

    def logit_group(g, mx, near):
        out = list(mx)
        for sb in range(wide // sub):
            k0 = pl.multiple_of(g * wide + sb * sub, sub)
            sel = sc_ref[pl.ds(k0, sub), :] >= vth
            for p in range(nh // 2):
                pair = jnp.dot(k_ref[pl.ds(k0, sub), 2 * p * d:(2 * p + 2) * d], bd_ref[p],
                               preferred_element_type=F32)
                for hh in (2 * p, 2 * p + 1):
                    lm = pair[:, (hh - 2 * p) * tq:(hh - 2 * p + 1) * tq]
                    if near:
                        back = [jnp.clip(i - (g * per_wide + sb * (sub // tk) + pb), 0, 2)
                                for pb in range(sub // tk)]
                        lm = lm + jnp.concatenate([bias_ref[bk, hh] for bk in back], axis=0)
                    lm = jnp.where(sel, lm, NEG_BIG)
                    lg_ref[hh, pl.ds(k0, sub), :] = lm
                    out[hh] = jnp.maximum(out[hh], col_fold(lm, jnp.maximum))
        return tuple(out)

    mx = tuple(jnp.full((8, tq), NEG_BIG, F32) for _ in range(nh))
    mx = lax.fori_loop(0, g_near, functools.partial(logit_group, near=False), mx)
    mx = lax.fori_loop(g_near, n_wide, functools.partial(logit_group, near=True), mx)
    m_q = [jnp.max(mx[hh], axis=0, keepdims=True) for hh in range(nh)]

    def pv_body(g, carry):
        g0 = pl.multiple_of(g * wide, wide)
        ls, accs = carry
        new_l, new_a = [], []
        for hh in range(nh):
            p = jnp.exp2(lg_ref[hh, pl.ds(g0, wide), :] - m_q[hh])
            new_l.append(ls[hh] + col_fold(p))
            new_a.append(accs[hh] + jnp.dot(vt_ref[hh * d:(hh + 1) * d, pl.ds(g0, wide)], p.astype(BF16),
                                            preferred_element_type=F32))
        return tuple(new_l), tuple(new_a)

    ls, accs = lax.fori_loop(0, n_wide, pv_body,
                             (tuple(jnp.zeros((8, tq), F32) for _ in range(nh)),
                              tuple(jnp.zeros((d, tq), F32) for _ in range(nh))))
    for hh in range(nh):
        o_ref[:, hh * d:(hh + 1) * d] = (accs[hh] / jnp.sum(ls[hh], axis=0, keepdims=True)).T


def _dsa(p32, p16, vt, bias_tiles, *, tq, cols):
    bsz, s, _ = p32.shape
    d = HEAD_DIM
    nh = N_HEADS
    wide = 4 * tq
    k_sel = min(TOPK_MAX, s // 4)
    w512 = nh * d
    kernel = functools.partial(_dsa_kernel, tq=tq, k_sel=k_sel, wi_lane=cols["wi_lane"], wide=wide)
    resident = dict(pipeline_mode=pl.Buffered(1))
    return pl.pallas_call(
        kernel,
        grid=(bsz, s // tq),
        in_specs=[pl.BlockSpec((None, tq, w512), lambda b, i: (b, i, cols["qi"] // nh)),
                  pl.BlockSpec((None, tq, d), lambda b, i: (b, i, cols["small"])),
                  pl.BlockSpec((None, tq, w512), lambda b, i: (b, i, cols["qb"] // nh)),
                  pl.BlockSpec((None, s, d), lambda b, i: (b, 0, cols["small"]), **resident),
                  pl.BlockSpec((None, s, w512), lambda b, i: (b, 0, cols["kb"] // nh), **resident),
                  pl.BlockSpec((w512, s), lambda b, i: (0, b), **resident),
                  pl.BlockSpec((3, nh, tq, tq), lambda b, i: (0, 0, 0, 0), **resident)],
        out_specs=pl.BlockSpec((None, tq, w512), lambda b, i: (b, i, 0)),
        out_shape=jax.ShapeDtypeStruct((bsz, s, w512), F32),
        scratch_shapes=[pltpu.VMEM((s, tq), F32),
                        pltpu.VMEM((s, tq), BF16),
                        pltpu.VMEM((IDX_HEADS, tq, tq), F32),
                        pltpu.VMEM((IDX_HEADS, tq, 3 * IDX_DIM), BF16),
                        pltpu.VMEM((3 * IDX_DIM, s), BF16),
                        pltpu.VMEM((nh // 2, 2 * d, 2 * tq), BF16),
                        pltpu.VMEM((nh, s, tq), F32)],
        compiler_params=pltpu.CompilerParams(
            dimension_semantics=("parallel", "arbitrary"), vmem_limit_bytes=VMEM_LIMIT),
        name="dsa",
    )(p32, p32, p32, p32, p16, vt, bias_tiles)


def _t5_bucket(rel):
    nb = REL_BUCKETS // 2
    max_exact = nb // 2
    ret = jnp.where(rel > 0, nb, 0)
    n = jnp.abs(rel)
    large = max_exact + (jnp.log(jnp.maximum(n, 1).astype(F32) / max_exact)
                         / math.log(REL_MAX_DIST / max_exact) * (nb - max_exact)).astype(jnp.int32)
    large = jnp.minimum(large, nb - 1)
    return ret + jnp.where(n < max_exact, n, large)


def _bias_tiles(rel_table, tq):
    assert tq >= REL_MAX_DIST
    t = jnp.arange(tq)
    tiles = []
    for back in range(3):
        rel = (t[None, :] - back * tq) - t[:, None]
        tiles.append(rel_table.astype(F32)[_t5_bucket(rel)].transpose(2, 1, 0))
    tiles = jnp.stack(tiles)
    return (tiles - tiles[2:3]) * LOG2E


def _even_layout(w_in):
    d = HEAD_DIM
    a_w = 2 * N_HEADS * d + N_HEADS * d
    offs = {}
    o = 0
    for name, w in (("qkv", a_w), ("z", N_HEADS * d), ("a", N_HEADS), ("b", N_HEADS),
                    ("qb", N_HEADS * d), ("kb", N_HEADS * d), ("vb", N_HEADS * d),
                    ("qi", IDX_HEADS * IDX_DIM), ("ki", IDX_DIM), ("wi", IDX_HEADS)):
        offs[name] = (o, o + w)
        o += w
    assert o == w_in.shape[1]
    sl = lambda n: w_in[:, offs[n][0]:offs[n][1]]
    small_w = IDX_DIM + 2 * N_HEADS + IDX_HEADS
    small_pad = -small_w % d
    w = jnp.concatenate([sl("qkv"), sl("z"), sl("qb"), sl("kb"), sl("vb"), sl("qi"),
                         sl("ki"), sl("a"), sl("b"), sl("wi"),
                         jnp.zeros((w_in.shape[0], small_pad), w_in.dtype)], axis=1)
    nh = N_HEADS
    cols = dict(qa=0, ka=nh, va=2 * nh, za=3 * nh, qb=4 * nh, kb=5 * nh, vb=6 * nh, qi=7 * nh,
                small=8 * nh, a_lane=IDX_DIM, b_lane=IDX_DIM + nh, wi_lane=IDX_DIM + 2 * nh)
    return w.astype(BF16), cols


def kernel(x, norm_g, w_in_even, conv_w_even, a_log_even, dt_bias_even, a_norm_even, w_out_even,
           rel_bias, w_in_odd, lb_logits, d_norm_odd, w_out_odd, w_gate, w_up, w_down):
    bsz, s, d = x.shape
    t = bsz * s
    depth = norm_g.shape[0]
    nh = N_HEADS
    tq = 128
    lb_all = jnp.cumsum(jax.nn.softmax(lb_logits.astype(F32), axis=0), axis=0)
    lb_all = lb_all - lb_all[:1]
    odd_cols = dict(qc=0, kc=nh, vc=2 * nh, qd=3 * nh, fd=4 * nh, id=5 * nh, gd=6 * nh)
    bias_tiles = _bias_tiles(rel_bias, tq)

    h = x.reshape(t, d)
    for l in range(depth):
        if l % 2 == 0:
            e = l // 2
            w_even, cols = _even_layout(w_in_even[e])
            vb0 = cols["vb"] * HEAD_DIM
            w_vt = w_even[:, vb0:vb0 + nh * HEAD_DIM].T
            p32, p16, vt = _norm_matmul(h, norm_g[l, 0], w_even, tm=512, tn=w_even.shape[1] // 3, w_t=w_vt)
            p32 = p32.reshape(bsz, s, -1)
            p16 = p16.reshape(bsz, s, -1)
            o_1 = _deltanet(p32, conv_w_even[e], a_log_even[e], dt_bias_even[e], a_norm_even[e],
                            ts=min(512, s), cols=cols)
            o_2 = _dsa(p32, p16, vt, bias_tiles, tq=tq, cols=cols)
            w_out = w_out_even[e]
        else:
            o = l // 2
            p32, p16 = _norm_matmul(h, norm_g[l, 0], w_in_odd[o].astype(BF16), tm=512, tn=512)
            p32 = p32.reshape(bsz, s, -1)
            p16 = p16.reshape(bsz, s, -1)
            o_1 = _stickbreak(p16, tq=tq, cols=odd_cols)
            o_2 = _hgrn2(p32, lb_all[l], d_norm_odd[o], ts=min(512, s), cols=odd_cols)
            w_out = w_out_odd[o]
        h = _outproj(o_1.reshape(t, -1), o_2.reshape(t, -1), w_out, h, norm_g[l, 1], tm=512)
        h = _ffn(h, norm_g[l, 2], norm_g[l, 3], w_gate[l], w_up[l], w_down[l], tm=1024, tf=256)
    return h.reshape(bsz, s, d)
```

```python
import functools
import math

import jax
import jax.numpy as jnp
from jax import lax
from jax.experimental import pallas as pl
from jax.experimental.pallas import tpu as pltpu

F32 = jnp.float32
BF16 = jnp.bfloat16
HIGHEST = lax.Precision.HIGHEST

CHUNK = 64
HEAD_DIM = 128
N_HEADS = 4
IDX_HEADS = 8
IDX_DIM = 64
TOPK_MAX = 256
CONV_WIDTH = 4
REL_BUCKETS = 32
REL_MAX_DIST = 128
EPS = 1e-6
NEG_BIG = -1e30
LOG2E = 1.4426950408889634
BISECT_COARSE = 12
BISECT_FIXED = 8
BISECT_EXTRA = 6
F32_LOWEST = -3.4028234663852886e38
EXP_ZERO_BELOW = -104.0
VMEM_LIMIT = 56 * 1024 * 1024


def _mm(a, b):
    return jnp.dot(a.astype(BF16), b.astype(BF16), preferred_element_type=F32)


def _mm_nt(a, b):
    return lax.dot_general(a.astype(BF16), b.astype(BF16), (((1,), (1,)), ((), ())),
                           preferred_element_type=F32)


def _mm_tn(a, b):
    return lax.dot_general(a.astype(BF16), b.astype(BF16), (((0,), (0,)), ((), ())),
                           preferred_element_type=F32)


def _mm_f32(a, b):
    return jnp.dot(a, b, precision=HIGHEST, preferred_element_type=F32)


def _split(x):
    hi = x.astype(BF16)
    return hi, (x - hi.astype(F32)).astype(BF16)


def _mm_x3(a, b):
    a_hi, a_lo = _split(a)
    b_hi, b_lo = _split(b)
    return jnp.dot(jnp.concatenate([a_hi, a_hi, a_lo], axis=1),
                   jnp.concatenate([b_hi, b_lo, b_hi], axis=0), preferred_element_type=F32)


def _floor_bf16(x):
    bits = pltpu.bitcast(x, jnp.int32)
    down = jnp.where(bits >= 0, bits, bits + 0xFFFF) & jnp.int32(-65536)
    return pltpu.bitcast(down, F32).astype(BF16)


def _sigmoid(x):
    return 1.0 / (1.0 + jnp.exp(-x))


def _silu(x):
    return x * _sigmoid(x)


def _softplus(x):
    return jnp.maximum(x, 0.0) + jnp.log1p(jnp.exp(-jnp.abs(x)))


def _rms(x, g):
    return x * lax.rsqrt(jnp.mean(x * x, axis=-1, keepdims=True) + EPS) * g


def _iota(shape, dim):
    return lax.broadcasted_iota(jnp.int32, shape, dim)


def _ind(mask):
    return jnp.where(mask, 1.0, 0.0)


def _norm_matmul_kernel(x_ref, g_ref, w_ref, *rest, n_t):
    if n_t:
        wt_ref, o32_ref, o16_ref, ot_ref, xn_ref = rest
    else:
        o32_ref, o16_ref, xn_ref = rest

    @pl.when(pl.program_id(1) == 0)
    def _():
        xn_ref[...] = _rms(x_ref[...], g_ref[...]).astype(BF16)
        if n_t:
            ot_ref[...] = lax.dot_general(wt_ref[...], xn_ref[...], (((1,), (1,)), ((), ())),
                                          preferred_element_type=F32).astype(BF16)

    y = jnp.dot(xn_ref[...], w_ref[...], preferred_element_type=F32)
    o32_ref[...] = y
    o16_ref[...] = y.astype(BF16)


def _norm_matmul(x, g, w, *, tm, tn, w_t=None):
    t, d = x.shape
    n = w.shape[1]
    n_t = 0 if w_t is None else w_t.shape[0]
    in_specs = [pl.BlockSpec((tm, d), lambda i, j: (i, 0)),
                pl.BlockSpec((1, d), lambda i, j: (0, 0)),
                pl.BlockSpec((d, tn), lambda i, j: (0, j))]
    out_specs = [pl.BlockSpec((tm, tn), lambda i, j: (i, j)),
                 pl.BlockSpec((tm, tn), lambda i, j: (i, j))]
    out_shape = [jax.ShapeDtypeStruct((t, n), F32), jax.ShapeDtypeStruct((t, n), BF16)]
    args = [x, g.reshape(1, d), w]
    if n_t:
        in_specs.append(pl.BlockSpec((n_t, d), lambda i, j: (0, 0)))
        out_specs.append(pl.BlockSpec((n_t, tm), lambda i, j: (0, i)))
        out_shape.append(jax.ShapeDtypeStruct((n_t, t), BF16))
        args.append(w_t)
    return pl.pallas_call(
        functools.partial(_norm_matmul_kernel, n_t=n_t),
        grid=(t // tm, n // tn),
        in_specs=in_specs,
        out_specs=out_specs,
        out_shape=out_shape,
        scratch_shapes=[pltpu.VMEM((tm, d), BF16)],
        compiler_params=pltpu.CompilerParams(
            dimension_semantics=("parallel", "arbitrary"), vmem_limit_bytes=VMEM_LIMIT),
        name="norm_matmul",
    )(*args)


def _outproj_kernel(ca_ref, cb_ref, wa_ref, wb_ref, h_ref, g_ref, o_ref):
    y = (jnp.dot(ca_ref[...].astype(BF16), wa_ref[...], preferred_element_type=F32)
         + jnp.dot(cb_ref[...].astype(BF16), wb_ref[...], preferred_element_type=F32))
    o_ref[...] = h_ref[...] + _rms(y, g_ref[...])


def _outproj(ca, cb, w, h, g, *, tm):
    t, d = h.shape
    wa_n = ca.shape[1]
    wb_n = cb.shape[1]
    wa = w[:wa_n].astype(BF16)
    wb = w[wa_n:].astype(BF16)
    return pl.pallas_call(
        _outproj_kernel,
        grid=(t // tm,),
        in_specs=[pl.BlockSpec((tm, wa_n), lambda i: (i, 0)),
                  pl.BlockSpec((tm, wb_n), lambda i: (i, 0)),
                  pl.BlockSpec((wa_n, d), lambda i: (0, 0)),
                  pl.BlockSpec((wb_n, d), lambda i: (0, 0)),
                  pl.BlockSpec((tm, d), lambda i: (i, 0)),
                  pl.BlockSpec((1, d), lambda i: (0, 0))],
        out_specs=pl.BlockSpec((tm, d), lambda i: (i, 0)),
        out_shape=jax.ShapeDtypeStruct((t, d), F32),
        compiler_params=pltpu.CompilerParams(
            dimension_semantics=("parallel",), vmem_limit_bytes=VMEM_LIMIT),
        name="outproj",
    )(ca, cb, wa, wb, h, g.reshape(1, d))


def _ffn_kernel(h_ref, gpre_ref, gpost_ref, wg_ref, wu_ref, wd_ref, o_ref, xn_ref, acc_ref):
    f = pl.program_id(1)

    @pl.when(f == 0)
    def _():
        xn_ref[...] = _rms(h_ref[...], gpre_ref[...]).astype(BF16)
        acc_ref[...] = jnp.zeros_like(acc_ref)

    xn = xn_ref[...]
    gate = jnp.dot(xn, wg_ref[...], preferred_element_type=F32)
    up = jnp.dot(xn, wu_ref[...], preferred_element_type=F32)
    act = (_silu(gate) * up).astype(BF16)
    acc_ref[...] += jnp.dot(act, wd_ref[...], preferred_element_type=F32)

    @pl.when(f == pl.num_programs(1) - 1)
    def _():
        o_ref[...] = h_ref[...] + _rms(acc_ref[...], gpost_ref[...])


def _ffn(h, g_pre, g_post, wg, wu, wd, *, tm, tf):
    t, d = h.shape
    ff = wg.shape[1]
    return pl.pallas_call(
        _ffn_kernel,
        grid=(t // tm, ff // tf),
        in_specs=[pl.BlockSpec((tm, d), lambda i, f: (i, 0)),
                  pl.BlockSpec((1, d), lambda i, f: (0, 0)),
                  pl.BlockSpec((1, d), lambda i, f: (0, 0)),
                  pl.BlockSpec((d, tf), lambda i, f: (0, f)),
                  pl.BlockSpec((d, tf), lambda i, f: (0, f)),
                  pl.BlockSpec((tf, d), lambda i, f: (f, 0))],
        out_specs=pl.BlockSpec((tm, d), lambda i, f: (i, 0)),
        out_shape=jax.ShapeDtypeStruct((t, d), F32),
        scratch_shapes=[pltpu.VMEM((tm, d), BF16), pltpu.VMEM((tm, d), F32)],
        compiler_params=pltpu.CompilerParams(
            dimension_semantics=("parallel", "arbitrary"), vmem_limit_bytes=VMEM_LIMIT),
        name="ffn",
    )(h, g_pre.reshape(1, d), g_post.reshape(1, d),
      wg.astype(BF16), wu.astype(BF16), wd.astype(BF16))


def _deltanet_kernel(xq_ref, xk_ref, xv_ref, z_ref, sm_ref, cwq_ref, cwk_ref, cwv_ref,
                     alog_ref, dtb_ref, gn_ref, o_ref,
                     xpad_ref, q_ref, k_ref, v_ref, gb_ref, bb_ref, u_ref, w_ref, qk_ref, st_ref,
                     *, ts, a_col, b_col):
    h = pl.program_id(1)
    s = pl.program_id(2)
    c = CHUNK
    d = HEAD_DIM

    @pl.when(s == 0)
    def _():
        xpad_ref[:, 0:8, :] = jnp.zeros((3, 8, d), F32)
        st_ref[...] = jnp.zeros_like(st_ref)

    @pl.when(s != 0)
    def _():
        xpad_ref[:, 0:8, :] = xpad_ref[:, ts:ts + 8, :]

    xpad_ref[0, 8:ts + 8, :] = xq_ref[...]
    xpad_ref[1, 8:ts + 8, :] = xk_ref[...]
    xpad_ref[2, 8:ts + 8, :] = xv_ref[...]

    def conv_silu(idx, cw_ref):
        cw = cw_ref[...]
        acc = xpad_ref[idx, 8 - (CONV_WIDTH - 1):8 - (CONV_WIDTH - 1) + ts, :] * cw[0:1, :]
        for j in range(1, CONV_WIDTH):
            off = 8 - (CONV_WIDTH - 1) + j
            acc = acc + xpad_ref[idx, off:off + ts, :] * cw[j:j + 1, :]
        return _silu(acc)

    def l2norm(t):
        return t * lax.rsqrt(jnp.sum(t * t, axis=-1, keepdims=True) + EPS)

    q_ref[...] = l2norm(conv_silu(0, cwq_ref)) * (d ** -0.5)
    k_ref[...] = l2norm(conv_silu(1, cwk_ref))
    v_ref[...] = conv_silu(2, cwv_ref)

    sm = sm_ref[...]
    lane = _iota(sm.shape, 1)
    a_raw = jnp.sum(jnp.where(lane == a_col + h, sm, 0.0), axis=-1, keepdims=True)
    b_raw = jnp.sum(jnp.where(lane == b_col + h, sm, 0.0), axis=-1, keepdims=True)
    hl = _iota((1, d), 1)
    a_log = jnp.sum(jnp.where(hl == h, alog_ref[...], 0.0), axis=-1, keepdims=True)
    dtb = jnp.sum(jnp.where(hl == h, dtb_ref[...], 0.0), axis=-1, keepdims=True)
    g = -jnp.exp(a_log) * _softplus(a_raw + dtb)
    gb_ref[...] = jnp.broadcast_to(g, (ts, d))
    bb_ref[...] = jnp.broadcast_to(_sigmoid(b_raw), (ts, d))

    row = _iota((c, c), 0)
    col = _iota((c, c), 1)
    tri = (col <= row)
    strict = (col < row)
    tri_f = tri.astype(F32)
    upper_f = (row <= col).astype(F32)
    eye = (row == col).astype(F32)
    gnorm = gn_ref[...]

    chunks = range(ts // c)
    rs = [slice(ci * c, (ci + 1) * c) for ci in chunks]
    tri2 = jnp.concatenate([tri_f, tri_f], axis=1).astype(BF16)
    ones2 = jnp.ones((c, 2 * c), BF16)

    def cum2(lhs2, x):
        hi, lo = _split(x)
        return jnp.dot(lhs2, jnp.concatenate([hi, lo], axis=0), preferred_element_type=F32)

    q = [q_ref[r, :] for r in rs]
    k = [k_ref[r, :] for r in rs]
    beta = [bb_ref[r, :] for r in rs]
    gb = [gb_ref[r, :] for r in rs]
    gc = [cum2(tri2, x) for x in gb]
    gc_row = [cum2(ones2, x[:, :c] * upper_f) for x in gb]
    decay = [jnp.where(tri, jnp.exp(jnp.minimum(a[:, :c] - b, 0.0)), 0.0) for a, b in zip(gc, gc_row)]
    kk = [_mm_nt(x, x) for x in k]
    n = [-jnp.where(strict, b[:, :c] * x * dc, 0.0) for b, x, dc in zip(beta, kk, decay)]
    inv = [eye + x for x in n]
    for _ in range(5):
        n = [_mm_x3(x, x) for x in n]
        inv = [iv + _mm_x3(iv, x) for iv, x in zip(inv, n)]
    egc = [jnp.exp(x) for x in gc]
    gl = [x[c - 1:c, :] for x in gc]
    for ci in chunks:
        r = rs[ci]
        u_ref[r, :] = _mm_x3(inv[ci], v_ref[r, :] * beta[ci])
        w_ref[r, :] = _mm_x3(inv[ci], k[ci] * (beta[ci] * egc[ci]))
        qk_ref[r, :] = _mm_nt(q[ci], k[ci]) * decay[ci]
        q_ref[r, :] = q[ci] * egc[ci]
        k_ref[r, :] = k[ci] * jnp.exp(gl[ci] - gc[ci])
        gb_ref[r, :] = jnp.broadcast_to(jnp.exp(gl[ci]), (c, d))

    def chunk_body(ci, carry):
        r0 = pl.multiple_of(ci * c, c)
        st = st_ref[...]
        v_new = u_ref[pl.ds(r0, c), :] - _mm(w_ref[pl.ds(r0, c), :], st)
        o = _mm(q_ref[pl.ds(r0, c), :], st) + _mm(qk_ref[pl.ds(r0, c), :], v_new)
        st_ref[...] = st * gb_ref[pl.ds(r0, 1), :] + _mm_tn(k_ref[pl.ds(r0, c), :], v_new)
        zc = z_ref[pl.ds(r0, c), :]
        o_ref[pl.ds(r0, c), :] = _rms(o, gnorm) * _silu(zc)
        return carry

    lax.fori_loop(0, ts // c, chunk_body, 0)


def _deltanet(p32, conv_w, a_log, dt_bias, a_norm_g, *, ts, cols):
    bsz, s, _ = p32.shape
    d = HEAD_DIM
    nh = N_HEADS
    qb, kb, vb, zb, smb = cols["qa"], cols["ka"], cols["va"], cols["za"], cols["small"]
    pad = lambda t: jnp.pad(t.astype(F32), (0, d - t.shape[0])).reshape(1, d)
    kernel = functools.partial(_deltanet_kernel, ts=ts, a_col=cols["a_lane"], b_col=cols["b_lane"])
    return pl.pallas_call(
        kernel,
        grid=(bsz, nh, s // ts),
        in_specs=[pl.BlockSpec((None, ts, d), lambda b, h, i: (b, i, qb + h)),
                  pl.BlockSpec((None, ts, d), lambda b, h, i: (b, i, kb + h)),
                  pl.BlockSpec((None, ts, d), lambda b, h, i: (b, i, vb + h)),
                  pl.BlockSpec((None, ts, d), lambda b, h, i: (b, i, zb + h)),
                  pl.BlockSpec((None, ts, d), lambda b, h, i: (b, i, smb)),
                  pl.BlockSpec((CONV_WIDTH, d), lambda b, h, i: (0, h)),
                  pl.BlockSpec((CONV_WIDTH, d), lambda b, h, i: (0, nh + h)),
                  pl.BlockSpec((CONV_WIDTH, d), lambda b, h, i: (0, 2 * nh + h)),
                  pl.BlockSpec((1, d), lambda b, h, i: (0, 0)),
                  pl.BlockSpec((1, d), lambda b, h, i: (0, 0)),
                  pl.BlockSpec((1, d), lambda b, h, i: (0, 0))],
        out_specs=pl.BlockSpec((None, ts, d), lambda b, h, i: (b, i, h)),
        out_shape=jax.ShapeDtypeStruct((bsz, s, nh * d), F32),
        scratch_shapes=[pltpu.VMEM((3, ts + 8, d), F32),
                        pltpu.VMEM((ts, d), F32), pltpu.VMEM((ts, d), F32), pltpu.VMEM((ts, d), F32),
                        pltpu.VMEM((ts, d), F32), pltpu.VMEM((ts, d), F32),
                        pltpu.VMEM((ts, d), F32), pltpu.VMEM((ts, d), F32),
                        pltpu.VMEM((ts, CHUNK), F32),
                        pltpu.VMEM((d, d), F32)],
        compiler_params=pltpu.CompilerParams(
            dimension_semantics=("parallel", "parallel", "arbitrary"), vmem_limit_bytes=VMEM_LIMIT),
        name="deltanet",
    )(p32, p32, p32, p32, p32, conv_w.astype(F32), conv_w.astype(F32), conv_w.astype(F32),
      pad(a_log), pad(dt_bias), a_norm_g.astype(F32).reshape(1, d))


def _hgrn2_kernel(q_ref, f_ref, i_ref, gate_ref, lb_ref, gn_ref, o_ref,
                  qs_ref, ks_ref, gc_ref, st_ref, *, ts):
    s = pl.program_id(2)
    c = CHUNK
    d = HEAD_DIM
    SUB = 16

    @pl.when(s == 0)
    def _():
        st_ref[...] = jnp.zeros_like(st_ref)

    lb = lb_ref[...]
    f_raw = f_ref[...]
    log_sig = jnp.minimum(f_raw, 0.0) - jnp.log1p(jnp.exp(-jnp.abs(f_raw)))
    la = jnp.log(lb)
    lbb = jnp.log1p(-lb) + log_sig
    log_f = jnp.maximum(la, lbb) + jnp.log1p(jnp.exp(-jnp.abs(la - lbb)))
    qs_ref[...] = _silu(q_ref[...])
    ks_ref[...] = (1.0 - lb) * _sigmoid(-f_raw)

    row = _iota((c, c), 0)
    col = _iota((c, c), 1)
    tri_f = (col <= row).astype(F32)
    ones_dd = jnp.ones((d, d), BF16)
    rows_8d = _iota((8, d), 0)
    gnorm = gn_ref[...]

    for ci in range(ts // c):
        gc_ref[ci * c:(ci + 1) * c, :] = _mm_f32(tri_f, log_f[ci * c:(ci + 1) * c, :])

    def chunk_loop(ci, carry):
        r0 = pl.multiple_of(ci * c, c)
        q = qs_ref[pl.ds(r0, c), :]
        k = ks_ref[pl.ds(r0, c), :]
        v = i_ref[pl.ds(r0, c), :]
        gc = gc_ref[pl.ds(r0, c), :]

        blocks = [(sb * SUB, (sb + 1) * SUB) for sb in range(c // SUB)]
        prods = []
        for top, end in blocks:
            for j in range(top, end):
                lo = (j // 8) * 8
                k_j = ks_ref[pl.ds(r0 + j, 1), :]
                g_j = gc_ref[pl.ds(r0 + j, 1), :]
                e = jnp.exp(jnp.minimum(gc[lo:end, :] - g_j, 0.0))
                if j % 8:
                    head = jnp.where(rows_8d >= j - lo, e[:8], 0.0)
                    e = jnp.concatenate([head, e[8:]], axis=0) if lo + 8 < end else head
                prods.append(q[lo:end, :] * k_j * e)
        sums = jnp.dot(jnp.concatenate(prods, axis=0).astype(BF16), ones_dd,
                       preferred_element_type=F32)
        qk_far = []
        for top, end in blocks[1:]:
            g_b = gc[top - 1:top, :]
            qe = q[top:end, :] * jnp.exp(gc[top:end, :] - g_b)
            ke = k[:top, :] * jnp.exp(jnp.minimum(g_b - gc[:top, :], 0.0))
            qk_far.append(_mm_nt(qe, ke))
        far = [_mm(a, v[:top, :]) for a, (top, _) in zip(qk_far, blocks[1:])]

        groups = [jnp.zeros((8, d), F32) for _ in range(c // 8)]
        at = 0
        for top, end in blocks:
            for j in range(top, end):
                v_j = i_ref[pl.ds(r0 + j, 1), :]
                for g in range(j // 8, end // 8):
                    groups[g] = groups[g] + sums[at:at + 8, :] * v_j
                    at += 8
        for f, (top, end) in zip(far, blocks[1:]):
            for g in range(top // 8, end // 8):
                groups[g] = groups[g] + f[(g * 8 - top):(g * 8 - top + 8), :]
        o_intra = jnp.concatenate(groups, axis=0)

        st = st_ref[...]
        gl = gc[c - 1:c, :]
        o = o_intra + _mm_nt(q * jnp.exp(gc), st)
        st_ref[...] = st * jnp.exp(gl) + _mm_tn(v, k * jnp.exp(gl - gc))
        o_ref[pl.ds(r0, c), :] = _rms(o, gnorm) * _silu(gate_ref[pl.ds(r0, c), :])
        return carry

    lax.fori_loop(0, ts // c, chunk_loop, 0)


def _hgrn2(p32, lb, d_norm_g, *, ts, cols):
    bsz, s, _ = p32.shape
    d = HEAD_DIM
    nh = N_HEADS
    qb, fb, ib, gb = cols["qd"], cols["fd"], cols["id"], cols["gd"]
    kernel = functools.partial(_hgrn2_kernel, ts=ts)
    return pl.pallas_call(
        kernel,
        grid=(bsz, nh, s // ts),
        in_specs=[pl.BlockSpec((None, ts, d), lambda b, h, i: (b, i, qb + h)),
                  pl.BlockSpec((None, ts, d), lambda b, h, i: (b, i, fb + h)),
                  pl.BlockSpec((None, ts, d), lambda b, h, i: (b, i, ib + h)),
                  pl.BlockSpec((None, ts, d), lambda b, h, i: (b, i, gb + h)),
                  pl.BlockSpec((1, d), lambda b, h, i: (0, h)),
                  pl.BlockSpec((1, d), lambda b, h, i: (0, 0))],
        out_specs=pl.BlockSpec((None, ts, d), lambda b, h, i: (b, i, h)),
        out_shape=jax.ShapeDtypeStruct((bsz, s, nh * d), F32),
        scratch_shapes=[pltpu.VMEM((ts, d), F32), pltpu.VMEM((ts, d), F32),
                        pltpu.VMEM((ts, d), F32), pltpu.VMEM((d, d), F32)],
        compiler_params=pltpu.CompilerParams(
            dimension_semantics=("parallel", "parallel", "arbitrary"), vmem_limit_bytes=VMEM_LIMIT),
        name="hgrn2",
    )(p32, p32, p32, p32, lb.astype(F32).reshape(1, nh * d), d_norm_g.astype(F32).reshape(1, d))


def _stickbreak_kernel(q_ref, k_ref, v_ref, o_ref, *, tq):
    i = pl.program_id(1)
    d = HEAD_DIM
    nh = N_HEADS
    row = _iota((tq, tq), 0)
    col = _iota((tq, tq), 1)
    causal = col < row
    later = (row > col).astype(BF16)

    def block(j, carries, diag):
        r0 = pl.multiple_of(j * tq, tq)
        out = []
        for hh in range(nh):
            hs = slice(hh * d, (hh + 1) * d)
            z = _mm_nt(q_ref[:, hs], k_ref[pl.ds(r0, tq), hs]) * (d ** -0.5)
            sp = _softplus(z)
            l1m = jnp.where(causal, -sp, 0.0) if diag else -sp
            l_hi, l_lo = _split(l1m)
            rest = (jnp.dot(l_hi, later, preferred_element_type=F32)
                    + jnp.dot(l_lo, later, preferred_element_type=F32))
            p = jnp.exp((z - sp) + rest + carries[hh])
            if diag:
                p = jnp.where(causal, p, 0.0)
            pv = _mm(p, v_ref[pl.ds(r0, tq), hs])
            if diag:
                o_ref[:, hs] = pv
            else:
                o_ref[:, hs] += pv
            out.append(carries[hh] + jnp.sum(l1m, axis=-1, keepdims=True))
        return tuple(out)

    carries = block(i, tuple(jnp.zeros((tq, 1), F32) for _ in range(nh)), True)

    def cond(c):
        worst = functools.reduce(jnp.maximum, c[1])
        return jnp.logical_and(c[0] >= 0, jnp.max(worst) >= EXP_ZERO_BELOW)

    def body(c):
        return c[0] - 1, block(c[0], c[1], False)

    lax.while_loop(cond, body, (i - 1, carries))


def _stickbreak(p16, *, tq, cols):
    bsz, s, _ = p16.shape
    nh = N_HEADS
    w = nh * HEAD_DIM
    kernel = functools.partial(_stickbreak_kernel, tq=tq)
    resident = dict(pipeline_mode=pl.Buffered(1))
    return pl.pallas_call(
        kernel,
        grid=(bsz, s // tq),
        in_specs=[pl.BlockSpec((None, tq, w), lambda b, i: (b, i, cols["qc"] // nh)),
                  pl.BlockSpec((None, s, w), lambda b, i: (b, 0, cols["kc"] // nh), **resident),
                  pl.BlockSpec((None, s, w), lambda b, i: (b, 0, cols["vc"] // nh), **resident)],
        out_specs=pl.BlockSpec((None, tq, w), lambda b, i: (b, i, 0)),
        out_shape=jax.ShapeDtypeStruct((bsz, s, w), F32),
        compiler_params=pltpu.CompilerParams(
            dimension_semantics=("parallel", "arbitrary"), vmem_limit_bytes=VMEM_LIMIT),
        name="stickbreak",
    )(p16, p16, p16)


def _dsa_kernel(qi_ref, smq_ref, q_ref, sm_ref, k_ref, vt_ref, bias_ref, o_ref,
                sc_ref, scb_ref, wb_ref, qc_ref, kct_ref, bd_ref, lg_ref, *, tq, k_sel, wi_lane, wide):
    i = pl.program_id(1)
    tk = tq
    d = HEAD_DIM
    nh = N_HEADS
    ksel = float(k_sel)
    per_wide = wide // tk
    n_wide = (i + per_wide) // per_wide
    sub = 2 * tk
    lane_q = _iota((1, tq), 1)

    def tree(parts, op):
        while len(parts) > 1:
            parts = [op(parts[j], parts[j + 1]) if j + 1 < len(parts) else parts[j]
                     for j in range(0, len(parts), 2)]
        return parts[0]

    def col_fold(x, op=jnp.add, rows=8):
        return tree([x[r * rows:(r + 1) * rows] for r in range(x.shape[0] // rows)], op)

    @pl.when(i == 0)
    def _():
        def prep(g, carry):
            g0 = pl.multiple_of(g * wide, wide)
            kt = sm_ref[pl.ds(g0, wide), :].T[:IDX_DIM, :]
            hi, lo = _split(kt)
            kct_ref[:, pl.ds(g0, wide)] = jnp.concatenate([hi, lo, hi], axis=0)
            return carry
        lax.fori_loop(0, sm_ref.shape[0] // wide, prep, 0)

    smq = smq_ref[...]
    lane = _iota(smq.shape, 1)
    for hh in range(IDX_HEADS):
        qh = qi_ref[:, hh * IDX_DIM:(hh + 1) * IDX_DIM]
        hi, lo = _split(qh)
        qc_ref[hh] = jnp.concatenate([hi, hi, lo], axis=-1)
        w = jnp.sum(jnp.where(lane == wi_lane + hh, smq, 0.0), axis=-1, keepdims=True)
        wb_ref[hh] = jnp.broadcast_to(w * ((IDX_HEADS ** -0.5) * (IDX_DIM ** -0.5)), (tq, tk))

    q2t = (q_ref[...] * ((d ** -0.5) * LOG2E)).T.astype(BF16)
    zero_dq = jnp.zeros((d, tq), BF16)
    for p in range(nh // 2):
        top = jnp.concatenate([q2t[2 * p * d:(2 * p + 1) * d], zero_dq], axis=1)
        bot = jnp.concatenate([zero_dq, q2t[(2 * p + 1) * d:(2 * p + 2) * d]], axis=1)
        bd_ref[p] = jnp.concatenate([top, bot], axis=0)

    limit = i * tq + (lane_q // CHUNK + 1) * CHUNK
    rows_t = _iota((tk, tq), 0)

    def score_group(g, mm, masked):
        mn, mx = mm
        for sb in range(wide // sub):
            k0 = pl.multiple_of(g * wide + sb * sub, sub)
            kct = kct_ref[:, pl.ds(k0, sub)]
            tiles = [jnp.zeros((tq, tk), F32) for _ in range(sub // tk)]
            for hh in range(IDX_HEADS):
                s_h = jnp.dot(qc_ref[hh], kct, preferred_element_type=F32)
                for ti in range(sub // tk):
                    tiles[ti] = tiles[ti] + jnp.maximum(s_h[:, ti * tk:(ti + 1) * tk], 0.0) * wb_ref[hh]
            for ti in range(sub // tk):
                kb = pl.multiple_of(k0 + ti * tk, tk)
                sct = tiles[ti].T
                if masked:
                    adm = (kb + rows_t) < limit
                    mn = jnp.minimum(mn, col_fold(jnp.where(adm, sct, jnp.inf), jnp.minimum))
                    sct = jnp.where(adm, sct, -jnp.inf)
                else:
                    mn = jnp.minimum(mn, col_fold(sct, jnp.minimum))
                mx = jnp.maximum(mx, col_fold(sct, jnp.maximum))
                sc_ref[pl.ds(kb, tk), :] = sct
                scb_ref[pl.ds(kb, tk), :] = _floor_bf16(sct)
        return mn, mx

    def score_pair(j, mm):
        return score_group(2 * j + 1, score_group(2 * j, mm, False), False)

    n_full = n_wide - 1
    mm = lax.fori_loop(0, n_full // 2, score_pair,
                       (jnp.full((8, tq), jnp.inf, F32), jnp.full((8, tq), -jnp.inf, F32)))
    mm = lax.cond(n_full % 2 == 1, lambda c: score_group(n_full - 1, c, False), lambda c: c, mm)
    mn, mx = score_group(n_wide - 1, mm, True)

    n_pairs = (n_wide + 1) // 2

    @pl.when(n_wide % 2 == 1)
    def _():
        sc_ref[pl.ds(pl.multiple_of(n_wide * wide, wide), wide), :] = jnp.full((wide, tq), -jnp.inf, F32)
    rmin = jnp.min(mn, axis=0, keepdims=True)
    rmax = jnp.max(mx, axis=0, keepdims=True)

    def count(pred):
        def body(g, acc):
            blk = sc_ref[pl.ds(pl.multiple_of(g * wide, wide), wide), :]
            return acc + col_fold(pred(blk))
        return jnp.sum(lax.fori_loop(0, n_wide, body, jnp.zeros((8, tq), F32)), axis=0, keepdims=True)

    def max_below(x):
        def body(g, acc):
            blk = sc_ref[pl.ds(pl.multiple_of(g * wide, wide), wide), :]
            return jnp.maximum(acc, col_fold(jnp.where(blk < x, blk, -jnp.inf), jnp.maximum))
        return jnp.max(lax.fori_loop(0, n_wide, body, jnp.full((8, tq), -jnp.inf, F32)), axis=0, keepdims=True)

    n_adm = limit.astype(F32)
    all_sel = n_adm <= ksel

    def bisect(c):
        lo, hi, c_lo = c
        mid = 0.5 * lo + 0.5 * hi
        cm = count(lambda blk: _ind(blk >= mid))
        ge = cm >= ksel
        return jnp.where(ge, mid, lo), jnp.where(ge, hi, mid), jnp.where(ge, cm, c_lo)

    def pending(c_lo, tied):
        return jnp.where(all_sel, 0.0, jnp.where(tied > 0.5, 0.0, _ind(c_lo != ksel)))

    def bisect_coarse(_, c):
        lo, hi, c_lo = c
        mid = _floor_bf16(0.5 * lo + 0.5 * hi).astype(F32)
        t_b = jnp.broadcast_to(mid, (16, tq)).astype(BF16)
        one_b = jnp.ones((16, tq), BF16)
        zero_b = jnp.zeros((16, tq), BF16)

        def body(g, acc):
            blk = scb_ref[pl.ds(pl.multiple_of(g * wide, wide), wide), :]
            ind = [jnp.where(blk[r * 16:(r + 1) * 16] >= t_b, one_b, zero_b) for r in range(wide // 16)]
            return acc + tree(ind, jnp.add).astype(F32)

        acc = lax.fori_loop(0, n_wide, body, jnp.zeros((16, tq), F32))
        cm = jnp.sum(acc, axis=0, keepdims=True)
        ge = cm >= ksel
        return jnp.where(ge, mid, lo), jnp.where(ge, hi, mid), jnp.where(ge, cm, c_lo)

    lo0 = _floor_bf16(rmin).astype(F32)
    hi0 = _floor_bf16(rmax + (jnp.abs(rmax) * (2.0 ** -6) + 1e-30)).astype(F32)
    state = lax.fori_loop(0, BISECT_COARSE, bisect_coarse, (lo0, hi0, n_adm))
    state = lax.fori_loop(0, BISECT_FIXED, lambda _, c: bisect(c), state)

    def round_cond(c):
        return jnp.max(pending(c[0][2], c[1])) > 0.5

    def round_body(c):
        st, tied, v, need = c

        def more_cond(s):
            return jnp.logical_and(s[0] < BISECT_EXTRA, jnp.max(pending(s[1][2], tied)) > 0.5)

        _, st = lax.while_loop(more_cond, lambda s: (s[0] + 1, bisect(s[1])), (jnp.int32(0), st))
        pend = pending(st[2], tied)

        def check(_):
            cand = max_below(st[1])
            c_ge = count(lambda blk: _ind(blk >= cand))
            c_gt = count(lambda blk: _ind(blk > cand))
            ok = jnp.where(pend > 0.5, _ind(c_ge >= ksel), 0.0)
            return (jnp.where(ok > 0.5, 1.0, tied), jnp.where(ok > 0.5, cand, v),
                    jnp.where(ok > 0.5, ksel - c_gt, need))

        tied, v, need = lax.cond(jnp.max(pend) > 0.5, check, lambda _: (tied, v, need), 0)
        return st, tied, v, need

    zeros1 = jnp.zeros((1, tq), F32)
    (lo_f, _, _), tied, v_tie, need = lax.while_loop(round_cond, round_body, (state, zeros1, zeros1, zeros1))
    vth = jnp.where(all_sel, F32_LOWEST, jnp.where(tied > 0.5, v_tie, lo_f))

    @pl.when(jnp.max(tied) > 0.5)
    def _():
        v_eq = jnp.where(tied > 0.5, v_tie, jnp.inf)
        incl = (_iota((tk, tk), 1) <= _iota((tk, tk), 0)).astype(BF16)

        def demote(g, seen):
            g0 = pl.multiple_of(g * wide, wide)
            xs = [sc_ref[pl.ds(g0 + pb * tk, tk), :] for pb in range(per_wide)]
            eqs = [_ind(x == v_eq) for x in xs]
            inblk = [jnp.dot(incl, e.astype(BF16), preferred_element_type=F32) for e in eqs]
            for pb in range(per_wide):
                rank = inblk[pb] + seen
                sc_ref[pl.ds(g0 + pb * tk, tk), :] = jnp.where(eqs[pb] * _ind(rank > need) > 0.5,
                                                               -jnp.inf, xs[pb])
                seen = seen + jnp.sum(col_fold(eqs[pb]), axis=0, keepdims=True)
            return seen

        lax.fori_loop(0, n_wide, demote, zeros1)

    g_near = jnp.maximum(i - 1, 0) // per_wide

    def logit_group(g, mx, near):
        out = list(mx)
        for sb in range(wide // sub):
            k0 = pl.multiple_of(g * wide + sb * sub, sub)
            sel = sc_ref[pl.ds(k0, sub), :] >= vth
            for p in range(nh // 2):
                pair = jnp.dot(k_ref[pl.ds(k0, sub), 2 * p * d:(2 * p + 2) * d], bd_ref[p],
                               preferred_element_type=F32)
                for hh in (2 * p, 2 * p + 1):
                    lm = pair[:, (hh - 2 * p) * tq:(hh - 2 * p + 1) * tq]
                    if near:
                        back = [jnp.clip(i - (g * per_wide + sb * (sub // tk) + pb), 0, 2)
                                for pb in range(sub // tk)]
                        lm = lm + jnp.concatenate([bias_ref[bk, hh] for bk in back], axis=0)
                    lm = jnp.where(sel, lm, NEG_BIG)
                    lg_ref[hh, pl.ds(k0, sub), :] = lm
                    out[hh] = jnp.maximum(out[hh], col_fold(lm, jnp.maximum))
        return tuple(out)

    mx = tuple(jnp.full((8, tq), NEG_BIG, F32) for _ in range(nh))
    def logit_pair(j, mx, near):
        return logit_group(2 * j + 1, logit_group(2 * j, mx, near), near)

    far_pairs = g_near // 2
    mx = lax.fori_loop(0, far_pairs, functools.partial(logit_pair, near=False), mx)
    mx = lax.fori_loop(far_pairs, n_pairs, functools.partial(logit_pair, near=True), mx)
    m_q = [jnp.max(mx[hh], axis=0, keepdims=True) for hh in range(nh)]

    def pv_body(g, carry):
        g0 = pl.multiple_of(g * wide, wide)
        ls, accs = carry
        new_l, new_a = [], []
        for hh in range(nh):
            p = jnp.exp2(lg_ref[hh, pl.ds(g0, wide), :] - m_q[hh])
            new_l.append(ls[hh] + col_fold(p))
            new_a.append(accs[hh] + jnp.dot(vt_ref[hh * d:(hh + 1) * d, pl.ds(g0, wide)], p.astype(BF16),
                                            preferred_element_type=F32))
        return tuple(new_l), tuple(new_a)

    ls, accs = lax.fori_loop(0, n_pairs, lambda j, c: pv_body(2 * j + 1, pv_body(2 * j, c)),
                             (tuple(jnp.zeros((8, tq), F32) for _ in range(nh)),
                              tuple(jnp.zeros((d, tq), F32) for _ in range(nh))))
    for hh in range(nh):
        o_ref[:, hh * d:(hh + 1) * d] = (accs[hh] / jnp.sum(ls[hh], axis=0, keepdims=True)).T


def _dsa(p32, p16, vt, bias_tiles, *, tq, cols):
    bsz, s, _ = p32.shape
    d = HEAD_DIM
    nh = N_HEADS
    wide = 4 * tq
    k_sel = min(TOPK_MAX, s // 4)
    w512 = nh * d
    kernel = functools.partial(_dsa_kernel, tq=tq, k_sel=k_sel, wi_lane=cols["wi_lane"], wide=wide)
    resident = dict(pipeline_mode=pl.Buffered(1))
    return pl.pallas_call(
        kernel,
        grid=(bsz, s // tq),
        in_specs=[pl.BlockSpec((None, tq, w512), lambda b, i: (b, i, cols["qi"] // nh)),
                  pl.BlockSpec((None, tq, d), lambda b, i: (b, i, cols["small"])),
                  pl.BlockSpec((None, tq, w512), lambda b, i: (b, i, cols["qb"] // nh)),
                  pl.BlockSpec((None, s, d), lambda b, i: (b, 0, cols["small"]), **resident),
                  pl.BlockSpec((None, s, w512), lambda b, i: (b, 0, cols["kb"] // nh), **resident),
                  pl.BlockSpec((w512, s), lambda b, i: (0, b), **resident),
                  pl.BlockSpec((3, nh, tq, tq), lambda b, i: (0, 0, 0, 0), **resident)],
        out_specs=pl.BlockSpec((None, tq, w512), lambda b, i: (b, i, 0)),
        out_shape=jax.ShapeDtypeStruct((bsz, s, w512), F32),
        scratch_shapes=[pltpu.VMEM((s, tq), F32),
                        pltpu.VMEM((s, tq), BF16),
                        pltpu.VMEM((IDX_HEADS, tq, tq), F32),
                        pltpu.VMEM((IDX_HEADS, tq, 3 * IDX_DIM), BF16),
                        pltpu.VMEM((3 * IDX_DIM, s), BF16),
                        pltpu.VMEM((nh // 2, 2 * d, 2 * tq), BF16),
                        pltpu.VMEM((nh, s, tq), F32)],
        compiler_params=pltpu.CompilerParams(
            dimension_semantics=("parallel", "arbitrary"), vmem_limit_bytes=VMEM_LIMIT),
        name="dsa",
    )(p32, p32, p32, p32, p16, vt, bias_tiles)


def _t5_bucket(rel):
    nb = REL_BUCKETS // 2
    max_exact = nb // 2
    ret = jnp.where(rel > 0, nb, 0)
    n = jnp.abs(rel)
    large = max_exact + (jnp.log(jnp.maximum(n, 1).astype(F32) / max_exact)
                         / math.log(REL_MAX_DIST / max_exact) * (nb - max_exact)).astype(jnp.int32)
    large = jnp.minimum(large, nb - 1)
    return ret + jnp.where(n < max_exact, n, large)


def _bias_tiles(rel_table, tq):
    assert tq >= REL_MAX_DIST
    t = jnp.arange(tq)
    tiles = []
    for back in range(3):
        rel = (t[None, :] - back * tq) - t[:, None]
        tiles.append(rel_table.astype(F32)[_t5_bucket(rel)].transpose(2, 1, 0))
    tiles = jnp.stack(tiles)
    return (tiles - tiles[2:3]) * LOG2E


def _even_layout(w_in):
    d = HEAD_DIM
    a_w = 2 * N_HEADS * d + N_HEADS * d
    offs = {}
    o = 0
    for name, w in (("qkv", a_w), ("z", N_HEADS * d), ("a", N_HEADS), ("b", N_HEADS),
                    ("qb", N_HEADS * d), ("kb", N_HEADS * d), ("vb", N_HEADS * d),
                    ("qi", IDX_HEADS * IDX_DIM), ("ki", IDX_DIM), ("wi", IDX_HEADS)):
        offs[name] = (o, o + w)
        o += w
    assert o == w_in.shape[1]
    sl = lambda n: w_in[:, offs[n][0]:offs[n][1]]
    small_w = IDX_DIM + 2 * N_HEADS + IDX_HEADS
    small_pad = -small_w % d
    w = jnp.concatenate([sl("qkv"), sl("z"), sl("qb"), sl("kb"), sl("vb"), sl("qi"),
                         sl("ki"), sl("a"), sl("b"), sl("wi"),
                         jnp.zeros((w_in.shape[0], small_pad), w_in.dtype)], axis=1)
    nh = N_HEADS
    cols = dict(qa=0, ka=nh, va=2 * nh, za=3 * nh, qb=4 * nh, kb=5 * nh, vb=6 * nh, qi=7 * nh,
                small=8 * nh, a_lane=IDX_DIM, b_lane=IDX_DIM + nh, wi_lane=IDX_DIM + 2 * nh)
    return w.astype(BF16), cols


def kernel(x, norm_g, w_in_even, conv_w_even, a_log_even, dt_bias_even, a_norm_even, w_out_even,
           rel_bias, w_in_odd, lb_logits, d_norm_odd, w_out_odd, w_gate, w_up, w_down):
    bsz, s, d = x.shape
    t = bsz * s
    depth = norm_g.shape[0]
    nh = N_HEADS
    tq = 128
    lb_all = jnp.cumsum(jax.nn.softmax(lb_logits.astype(F32), axis=0), axis=0)
    lb_all = lb_all - lb_all[:1]
    odd_cols = dict(qc=0, kc=nh, vc=2 * nh, qd=3 * nh, fd=4 * nh, id=5 * nh, gd=6 * nh)
    bias_tiles = _bias_tiles(rel_bias, tq)

    h = x.reshape(t, d)
    for l in range(depth):
        if l % 2 == 0:
            e = l // 2
            w_even, cols = _even_layout(w_in_even[e])
            vb0 = cols["vb"] * HEAD_DIM
            w_vt = w_even[:, vb0:vb0 + nh * HEAD_DIM].T
            p32, p16, vt = _norm_matmul(h, norm_g[l, 0], w_even, tm=512, tn=w_even.shape[1] // 3, w_t=w_vt)
            p32 = p32.reshape(bsz, s, -1)
            p16 = p16.reshape(bsz, s, -1)
            o_1 = _deltanet(p32, conv_w_even[e], a_log_even[e], dt_bias_even[e], a_norm_even[e],
                            ts=min(512, s), cols=cols)
            o_2 = _dsa(p32, p16, vt, bias_tiles, tq=tq, cols=cols)
            w_out = w_out_even[e]
        else:
            o = l // 2
            p32, p16 = _norm_matmul(h, norm_g[l, 0], w_in_odd[o].astype(BF16), tm=512, tn=512)
            p32 = p32.reshape(bsz, s, -1)
            p16 = p16.reshape(bsz, s, -1)
            o_1 = _stickbreak(p16, tq=tq, cols=odd_cols)
            o_2 = _hgrn2(p32, lb_all[l], d_norm_odd[o], ts=min(512, s), cols=odd_cols)
            w_out = w_out_odd[o]
        h = _outproj(o_1.reshape(t, -1), o_2.reshape(t, -1), w_out, h, norm_g[l, 1], tm=512)
        h = _ffn(h, norm_g[l, 2], norm_g[l, 3], w_gate[l], w_up[l], w_down[l], tm=1024, tf=256)
    return h.reshape(bsz, s, d)
```

```python
import functools
import math

import jax
import jax.numpy as jnp
from jax import lax
from jax.experimental import pallas as pl
from jax.experimental.pallas import tpu as pltpu

F32 = jnp.float32
BF16 = jnp.bfloat16
HIGHEST = lax.Precision.HIGHEST

CHUNK = 64
HEAD_DIM = 128
N_HEADS = 4
IDX_HEADS = 8
IDX_DIM = 64
TOPK_MAX = 256
CONV_WIDTH = 4
REL_BUCKETS = 32
REL_MAX_DIST = 128
EPS = 1e-6
NEG_BIG = -1e30
LOG2E = 1.4426950408889634
BISECT_COARSE = 12
BISECT_FIXED = 8
BISECT_EXTRA = 6
F32_LOWEST = -3.4028234663852886e38
EXP_ZERO_BELOW = -104.0
VMEM_LIMIT = 56 * 1024 * 1024


def _mm(a, b):
    return jnp.dot(a.astype(BF16), b.astype(BF16), preferred_element_type=F32)


def _mm_nt(a, b):
    return lax.dot_general(a.astype(BF16), b.astype(BF16), (((1,), (1,)), ((), ())),
                           preferred_element_type=F32)


def _mm_tn(a, b):
    return lax.dot_general(a.astype(BF16), b.astype(BF16), (((0,), (0,)), ((), ())),
                           preferred_element_type=F32)


def _mm_f32(a, b):
    return jnp.dot(a, b, precision=HIGHEST, preferred_element_type=F32)


def _split(x):
    hi = x.astype(BF16)
    return hi, (x - hi.astype(F32)).astype(BF16)


def _mm_x3(a, b):
    a_hi, a_lo = _split(a)
    b_hi, b_lo = _split(b)
    return jnp.dot(jnp.concatenate([a_hi, a_hi, a_lo], axis=1),
                   jnp.concatenate([b_hi, b_lo, b_hi], axis=0), preferred_element_type=F32)


def _floor_bf16(x):
    bits = pltpu.bitcast(x, jnp.int32)
    down = jnp.where(bits >= 0, bits, bits + 0xFFFF) & jnp.int32(-65536)
    return pltpu.bitcast(down, F32).astype(BF16)


def _sigmoid(x):
    return 1.0 / (1.0 + jnp.exp(-x))


def _silu(x):
    return x * _sigmoid(x)


def _softplus(x):
    return jnp.maximum(x, 0.0) + jnp.log1p(jnp.exp(-jnp.abs(x)))


def _rms(x, g):
    return x * lax.rsqrt(jnp.mean(x * x, axis=-1, keepdims=True) + EPS) * g


def _iota(shape, dim):
    return lax.broadcasted_iota(jnp.int32, shape, dim)


def _ind(mask):
    return jnp.where(mask, 1.0, 0.0)


def _norm_matmul_kernel(x_ref, g_ref, w_ref, *rest, n_t):
    if n_t:
        wt_ref, o32_ref, o16_ref, ot_ref, xn_ref = rest
    else:
        o32_ref, o16_ref, xn_ref = rest

    @pl.when(pl.program_id(1) == 0)
    def _():
        xn_ref[...] = _rms(x_ref[...], g_ref[...]).astype(BF16)
        if n_t:
            ot_ref[...] = lax.dot_general(wt_ref[...], xn_ref[...], (((1,), (1,)), ((), ())),
                                          preferred_element_type=F32).astype(BF16)

    y = jnp.dot(xn_ref[...], w_ref[...], preferred_element_type=F32)
    o32_ref[...] = y
    o16_ref[...] = y.astype(BF16)


def _norm_matmul(x, g, w, *, tm, tn, w_t=None):
    t, d = x.shape
    n = w.shape[1]
    n_t = 0 if w_t is None else w_t.shape[0]
    in_specs = [pl.BlockSpec((tm, d), lambda i, j: (i, 0)),
                pl.BlockSpec((1, d), lambda i, j: (0, 0)),
                pl.BlockSpec((d, tn), lambda i, j: (0, j))]
    out_specs = [pl.BlockSpec((tm, tn), lambda i, j: (i, j)),
                 pl.BlockSpec((tm, tn), lambda i, j: (i, j))]
    out_shape = [jax.ShapeDtypeStruct((t, n), F32), jax.ShapeDtypeStruct((t, n), BF16)]
    args = [x, g.reshape(1, d), w]
    if n_t:
        in_specs.append(pl.BlockSpec((n_t, d), lambda i, j: (0, 0)))
        out_specs.append(pl.BlockSpec((n_t, tm), lambda i, j: (0, i)))
        out_shape.append(jax.ShapeDtypeStruct((n_t, t), BF16))
        args.append(w_t)
    return pl.pallas_call(
        functools.partial(_norm_matmul_kernel, n_t=n_t),
        grid=(t // tm, n // tn),
        in_specs=in_specs,
        out_specs=out_specs,
        out_shape=out_shape,
        scratch_shapes=[pltpu.VMEM((tm, d), BF16)],
        compiler_params=pltpu.CompilerParams(
            dimension_semantics=("parallel", "arbitrary"), vmem_limit_bytes=VMEM_LIMIT),
        name="norm_matmul",
    )(*args)


def _outproj_kernel(ca_ref, cb_ref, wa_ref, wb_ref, h_ref, g_ref, o_ref):
    y = (jnp.dot(ca_ref[...].astype(BF16), wa_ref[...], preferred_element_type=F32)
         + jnp.dot(cb_ref[...].astype(BF16), wb_ref[...], preferred_element_type=F32))
    o_ref[...] = h_ref[...] + _rms(y, g_ref[...])


def _outproj(ca, cb, w, h, g, *, tm):
    t, d = h.shape
    wa_n = ca.shape[1]
    wb_n = cb.shape[1]
    wa = w[:wa_n].astype(BF16)
    wb = w[wa_n:].astype(BF16)
    return pl.pallas_call(
        _outproj_kernel,
        grid=(t // tm,),
        in_specs=[pl.BlockSpec((tm, wa_n), lambda i: (i, 0)),
                  pl.BlockSpec((tm, wb_n), lambda i: (i, 0)),
                  pl.BlockSpec((wa_n, d), lambda i: (0, 0)),
                  pl.BlockSpec((wb_n, d), lambda i: (0, 0)),
                  pl.BlockSpec((tm, d), lambda i: (i, 0)),
                  pl.BlockSpec((1, d), lambda i: (0, 0))],
        out_specs=pl.BlockSpec((tm, d), lambda i: (i, 0)),
        out_shape=jax.ShapeDtypeStruct((t, d), F32),
        compiler_params=pltpu.CompilerParams(
            dimension_semantics=("parallel",), vmem_limit_bytes=VMEM_LIMIT),
        name="outproj",
    )(ca, cb, wa, wb, h, g.reshape(1, d))


def _ffn_kernel(h_ref, gpre_ref, gpost_ref, wg_ref, wu_ref, wd_ref, o_ref, xn_ref, acc_ref):
    f = pl.program_id(1)

    @pl.when(f == 0)
    def _():
        xn_ref[...] = _rms(h_ref[...], gpre_ref[...]).astype(BF16)
        acc_ref[...] = jnp.zeros_like(acc_ref)

    xn = xn_ref[...]
    gate = jnp.dot(xn, wg_ref[...], preferred_element_type=F32)
    up = jnp.dot(xn, wu_ref[...], preferred_element_type=F32)
    act = (_silu(gate) * up).astype(BF16)
    acc_ref[...] += jnp.dot(act, wd_ref[...], preferred_element_type=F32)

    @pl.when(f == pl.num_programs(1) - 1)
    def _():
        o_ref[...] = h_ref[...] + _rms(acc_ref[...], gpost_ref[...])


def _ffn(h, g_pre, g_post, wg, wu, wd, *, tm, tf):
    t, d = h.shape
    ff = wg.shape[1]
    return pl.pallas_call(
        _ffn_kernel,
        grid=(t // tm, ff // tf),
        in_specs=[pl.BlockSpec((tm, d), lambda i, f: (i, 0)),
                  pl.BlockSpec((1, d), lambda i, f: (0, 0)),
                  pl.BlockSpec((1, d), lambda i, f: (0, 0)),
                  pl.BlockSpec((d, tf), lambda i, f: (0, f)),
                  pl.BlockSpec((d, tf), lambda i, f: (0, f)),
                  pl.BlockSpec((tf, d), lambda i, f: (f, 0))],
        out_specs=pl.BlockSpec((tm, d), lambda i, f: (i, 0)),
        out_shape=jax.ShapeDtypeStruct((t, d), F32),
        scratch_shapes=[pltpu.VMEM((tm, d), BF16), pltpu.VMEM((tm, d), F32)],
        compiler_params=pltpu.CompilerParams(
            dimension_semantics=("parallel", "arbitrary"), vmem_limit_bytes=VMEM_LIMIT),
        name="ffn",
    )(h, g_pre.reshape(1, d), g_post.reshape(1, d),
      wg.astype(BF16), wu.astype(BF16), wd.astype(BF16))


def _deltanet_kernel(xq_ref, xk_ref, xv_ref, z_ref, sm_ref, cwq_ref, cwk_ref, cwv_ref,
                     alog_ref, dtb_ref, gn_ref, o_ref,
                     xpad_ref, q_ref, k_ref, v_ref, gb_ref, bb_ref, u_ref, w_ref, qk_ref, st_ref,
                     *, ts, a_col, b_col):
    s = pl.program_id(1)
    c = CHUNK
    d = HEAD_DIM
    nh = N_HEADS

    @pl.when(s == 0)
    def _():
        xpad_ref[:, 0:8, :] = jnp.zeros((3, 8, nh * d), F32)
        st_ref[...] = jnp.zeros_like(st_ref)

    @pl.when(s != 0)
    def _():
        xpad_ref[:, 0:8, :] = xpad_ref[:, ts:ts + 8, :]

    xpad_ref[0, 8:ts + 8, :] = xq_ref[...]
    xpad_ref[1, 8:ts + 8, :] = xk_ref[...]
    xpad_ref[2, 8:ts + 8, :] = xv_ref[...]

    def conv_silu(idx, cw_ref, hs):
        cw = cw_ref[:, hs]
        acc = xpad_ref[idx, 8 - (CONV_WIDTH - 1):8 - (CONV_WIDTH - 1) + ts, hs] * cw[0:1, :]
        for j in range(1, CONV_WIDTH):
            off = 8 - (CONV_WIDTH - 1) + j
            acc = acc + xpad_ref[idx, off:off + ts, hs] * cw[j:j + 1, :]
        return _silu(acc)

    def l2norm(t):
        return t * lax.rsqrt(jnp.sum(t * t, axis=-1, keepdims=True) + EPS)

    row = _iota((c, c), 0)
    col = _iota((c, c), 1)
    tri = (col <= row)
    strict = (col < row)
    tri_f = tri.astype(F32)
    upper_f = (row <= col).astype(F32)
    eye = (row == col).astype(F32)
    gnorm = gn_ref[...]
    chunks = range(ts // c)
    rs = [slice(ci * c, (ci + 1) * c) for ci in chunks]
    tri2 = jnp.concatenate([tri_f, tri_f], axis=1).astype(BF16)
    ones2 = jnp.ones((c, 2 * c), BF16)

    def cum2(lhs2, x):
        hi, lo = _split(x)
        return jnp.dot(lhs2, jnp.concatenate([hi, lo], axis=0), preferred_element_type=F32)

    for hh in range(nh):
        hs = slice(hh * d, (hh + 1) * d)
        q_ref[:, hs] = l2norm(conv_silu(0, cwq_ref, hs)) * (d ** -0.5)
        k_ref[:, hs] = l2norm(conv_silu(1, cwk_ref, hs))
        v_ref[:, hs] = conv_silu(2, cwv_ref, hs)

        a_raw = sm_ref[:, a_col + hh:a_col + hh + 1]
        b_raw = sm_ref[:, b_col + hh:b_col + hh + 1]
        g = -jnp.exp(alog_ref[:, hh:hh + 1]) * _softplus(a_raw + dtb_ref[:, hh:hh + 1])
        gb_ref[:, hs] = jnp.broadcast_to(g, (ts, d))
        bb_ref[:, hs] = jnp.broadcast_to(_sigmoid(b_raw), (ts, d))

        q = [q_ref[r, hs] for r in rs]
        k = [k_ref[r, hs] for r in rs]
        beta = [bb_ref[r, hs] for r in rs]
        gb = [gb_ref[r, hs] for r in rs]
        gc = [cum2(tri2, x) for x in gb]
        gc_row = [cum2(ones2, x[:, :c] * upper_f) for x in gb]
        decay = [jnp.where(tri, jnp.exp(jnp.minimum(a[:, :c] - b, 0.0)), 0.0) for a, b in zip(gc, gc_row)]
        kk = [_mm_nt(x, x) for x in k]
        n = [-jnp.where(strict, b[:, :c] * x * dc, 0.0) for b, x, dc in zip(beta, kk, decay)]
        inv = [eye + x for x in n]
        for _ in range(5):
            n = [_mm_x3(x, x) for x in n]
            inv = [iv + _mm_x3(iv, x) for iv, x in zip(inv, n)]
        egc = [jnp.exp(x) for x in gc]
        gl = [x[c - 1:c, :] for x in gc]
        for ci in chunks:
            r = rs[ci]
            u_ref[r, hs] = _mm_x3(inv[ci], v_ref[r, hs] * beta[ci])
            w_ref[r, hs] = _mm_x3(inv[ci], k[ci] * (beta[ci] * egc[ci]))
            qk_ref[hh, r, :] = _mm_nt(q[ci], k[ci]) * decay[ci]
            q_ref[r, hs] = q[ci] * egc[ci]
            k_ref[r, hs] = k[ci] * jnp.exp(gl[ci] - gc[ci])
            gb_ref[r, hs] = jnp.broadcast_to(jnp.exp(gl[ci]), (c, d))

    def chunk_body(ci, carry):
        r0 = pl.multiple_of(ci * c, c)
        for hh in range(nh):
            hs = slice(hh * d, (hh + 1) * d)
            st = st_ref[hh]
            v_new = u_ref[pl.ds(r0, c), hs] - _mm(w_ref[pl.ds(r0, c), hs], st)
            o = _mm(q_ref[pl.ds(r0, c), hs], st) + _mm(qk_ref[hh, pl.ds(r0, c), :], v_new)
            st_ref[hh] = st * gb_ref[pl.ds(r0, 1), hs] + _mm_tn(k_ref[pl.ds(r0, c), hs], v_new)
            o_ref[pl.ds(r0, c), hs] = _rms(o, gnorm) * _silu(z_ref[pl.ds(r0, c), hs])
        return carry

    lax.fori_loop(0, ts // c, chunk_body, 0)


def _deltanet(p32, conv_w, a_log, dt_bias, a_norm_g, *, ts, cols):
    bsz, s, _ = p32.shape
    d = HEAD_DIM
    nh = N_HEADS
    w = nh * d
    pad = lambda t: jnp.pad(t.astype(F32), (0, d - t.shape[0])).reshape(1, d)
    kernel = functools.partial(_deltanet_kernel, ts=ts, a_col=cols["a_lane"], b_col=cols["b_lane"])
    tile = lambda name: pl.BlockSpec((None, ts, w), lambda b, i: (b, i, cols[name] // nh))
    conv = lambda k: pl.BlockSpec((CONV_WIDTH, w), lambda b, i: (0, k))
    row = pl.BlockSpec((1, d), lambda b, i: (0, 0))
    return pl.pallas_call(
        kernel,
        grid=(bsz, s // ts),
        in_specs=[tile("qa"), tile("ka"), tile("va"), tile("za"),
                  pl.BlockSpec((None, ts, d), lambda b, i: (b, i, cols["small"])),
                  conv(0), conv(1), conv(2), row, row, row],
        out_specs=pl.BlockSpec((None, ts, w), lambda b, i: (b, i, 0)),
        out_shape=jax.ShapeDtypeStruct((bsz, s, w), F32),
        scratch_shapes=[pltpu.VMEM((3, ts + 8, w), F32)]
        + [pltpu.VMEM((ts, w), F32) for _ in range(7)]
        + [pltpu.VMEM((nh, ts, CHUNK), F32), pltpu.VMEM((nh, d, d), F32)],
        compiler_params=pltpu.CompilerParams(
            dimension_semantics=("parallel", "arbitrary"), vmem_limit_bytes=VMEM_LIMIT),
        name="deltanet",
    )(p32, p32, p32, p32, p32, conv_w.astype(F32), conv_w.astype(F32), conv_w.astype(F32),
      pad(a_log), pad(dt_bias), a_norm_g.astype(F32).reshape(1, d))


def _hgrn2_kernel(q_ref, f_ref, i_ref, gate_ref, lb_ref, gn_ref, o_ref,
                  qs_ref, ks_ref, gc_ref, st_ref, *, ts):
    s = pl.program_id(2)
    c = CHUNK
    d = HEAD_DIM
    SUB = 16

    @pl.when(s == 0)
    def _():
        st_ref[...] = jnp.zeros_like(st_ref)

    lb = lb_ref[...]
    f_raw = f_ref[...]
    log_sig = jnp.minimum(f_raw, 0.0) - jnp.log1p(jnp.exp(-jnp.abs(f_raw)))
    la = jnp.log(lb)
    lbb = jnp.log1p(-lb) + log_sig
    log_f = jnp.maximum(la, lbb) + jnp.log1p(jnp.exp(-jnp.abs(la - lbb)))
    qs_ref[...] = _silu(q_ref[...])
    ks_ref[...] = (1.0 - lb) * _sigmoid(-f_raw)

    row = _iota((c, c), 0)
    col = _iota((c, c), 1)
    tri_f = (col <= row).astype(F32)
    ones_dd = jnp.ones((d, d), BF16)
    rows_8d = _iota((8, d), 0)
    gnorm = gn_ref[...]

    for ci in range(ts // c):
        gc_ref[ci * c:(ci + 1) * c, :] = _mm_f32(tri_f, log_f[ci * c:(ci + 1) * c, :])

    def chunk_loop(ci, carry):
        r0 = pl.multiple_of(ci * c, c)
        q = qs_ref[pl.ds(r0, c), :]
        k = ks_ref[pl.ds(r0, c), :]
        v = i_ref[pl.ds(r0, c), :]
        gc = gc_ref[pl.ds(r0, c), :]

        blocks = [(sb * SUB, (sb + 1) * SUB) for sb in range(c // SUB)]
        prods = []
        for top, end in blocks:
            for j in range(top, end):
                lo = (j // 8) * 8
                k_j = ks_ref[pl.ds(r0 + j, 1), :]
                g_j = gc_ref[pl.ds(r0 + j, 1), :]
                e = jnp.exp(jnp.minimum(gc[lo:end, :] - g_j, 0.0))
                if j % 8:
                    head = jnp.where(rows_8d >= j - lo, e[:8], 0.0)
                    e = jnp.concatenate([head, e[8:]], axis=0) if lo + 8 < end else head
                prods.append(q[lo:end, :] * k_j * e)
        sums = jnp.dot(jnp.concatenate(prods, axis=0).astype(BF16), ones_dd,
                       preferred_element_type=F32)
        qk_far = []
        for top, end in blocks[1:]:
            g_b = gc[top - 1:top, :]
            qe = q[top:end, :] * jnp.exp(gc[top:end, :] - g_b)
            ke = k[:top, :] * jnp.exp(jnp.minimum(g_b - gc[:top, :], 0.0))
            qk_far.append(_mm_nt(qe, ke))
        far = [_mm(a, v[:top, :]) for a, (top, _) in zip(qk_far, blocks[1:])]

        groups = [jnp.zeros((8, d), F32) for _ in range(c // 8)]
        at = 0
        for top, end in blocks:
            for j in range(top, end):
                v_j = i_ref[pl.ds(r0 + j, 1), :]
                for g in range(j // 8, end // 8):
                    groups[g] = groups[g] + sums[at:at + 8, :] * v_j
                    at += 8
        for f, (top, end) in zip(far, blocks[1:]):
            for g in range(top // 8, end // 8):
                groups[g] = groups[g] + f[(g * 8 - top):(g * 8 - top + 8), :]
        o_intra = jnp.concatenate(groups, axis=0)

        st = st_ref[...]
        gl = gc[c - 1:c, :]
        o = o_intra + _mm_nt(q * jnp.exp(gc), st)
        st_ref[...] = st * jnp.exp(gl) + _mm_tn(v, k * jnp.exp(gl - gc))
        o_ref[pl.ds(r0, c), :] = _rms(o, gnorm) * _silu(gate_ref[pl.ds(r0, c), :])
        return carry

    lax.fori_loop(0, ts // c, chunk_loop, 0)


def _hgrn2(p32, lb, d_norm_g, *, ts, cols):
    bsz, s, _ = p32.shape
    d = HEAD_DIM
    nh = N_HEADS
    qb, fb, ib, gb = cols["qd"], cols["fd"], cols["id"], cols["gd"]
    kernel = functools.partial(_hgrn2_kernel, ts=ts)
    return pl.pallas_call(
        kernel,
        grid=(bsz, nh, s // ts),
        in_specs=[pl.BlockSpec((None, ts, d), lambda b, h, i: (b, i, qb + h)),
                  pl.BlockSpec((None, ts, d), lambda b, h, i: (b, i, fb + h)),
                  pl.BlockSpec((None, ts, d), lambda b, h, i: (b, i, ib + h)),
                  pl.BlockSpec((None, ts, d), lambda b, h, i: (b, i, gb + h)),
                  pl.BlockSpec((1, d), lambda b, h, i: (0, h)),
                  pl.BlockSpec((1, d), lambda b, h, i: (0, 0))],
        out_specs=pl.BlockSpec((None, ts, d), lambda b, h, i: (b, i, h)),
        out_shape=jax.ShapeDtypeStruct((bsz, s, nh * d), F32),
        scratch_shapes=[pltpu.VMEM((ts, d), F32), pltpu.VMEM((ts, d), F32),
                        pltpu.VMEM((ts, d), F32), pltpu.VMEM((d, d), F32)],
        compiler_params=pltpu.CompilerParams(
            dimension_semantics=("parallel", "parallel", "arbitrary"), vmem_limit_bytes=VMEM_LIMIT),
        name="hgrn2",
    )(p32, p32, p32, p32, lb.astype(F32).reshape(1, nh * d), d_norm_g.astype(F32).reshape(1, d))


def _stickbreak_kernel(q_ref, k_ref, v_ref, o_ref, *, tq):
    i = pl.program_id(1)
    d = HEAD_DIM
    nh = N_HEADS
    row = _iota((tq, tq), 0)
    col = _iota((tq, tq), 1)
    causal = col < row
    later = (row > col).astype(BF16)

    def block(j, carries, diag):
        r0 = pl.multiple_of(j * tq, tq)
        out = []
        for hh in range(nh):
            hs = slice(hh * d, (hh + 1) * d)
            z = _mm_nt(q_ref[:, hs], k_ref[pl.ds(r0, tq), hs]) * (d ** -0.5)
            sp = _softplus(z)
            l1m = jnp.where(causal, -sp, 0.0) if diag else -sp
            l_hi, l_lo = _split(l1m)
            rest = (jnp.dot(l_hi, later, preferred_element_type=F32)
                    + jnp.dot(l_lo, later, preferred_element_type=F32))
            p = jnp.exp((z - sp) + rest + carries[hh])
            if diag:
                p = jnp.where(causal, p, 0.0)
            pv = _mm(p, v_ref[pl.ds(r0, tq), hs])
            if diag:
                o_ref[:, hs] = pv
            else:
                o_ref[:, hs] += pv
            out.append(carries[hh] + jnp.sum(l1m, axis=-1, keepdims=True))
        return tuple(out)

    carries = block(i, tuple(jnp.zeros((tq, 1), F32) for _ in range(nh)), True)

    def cond(c):
        worst = functools.reduce(jnp.maximum, c[1])
        return jnp.logical_and(c[0] >= 0, jnp.max(worst) >= EXP_ZERO_BELOW)

    def body(c):
        return c[0] - 1, block(c[0], c[1], False)

    lax.while_loop(cond, body, (i - 1, carries))


def _stickbreak(p16, *, tq, cols):
    bsz, s, _ = p16.shape
    nh = N_HEADS
    w = nh * HEAD_DIM
    kernel = functools.partial(_stickbreak_kernel, tq=tq)
    resident = dict(pipeline_mode=pl.Buffered(1))
    return pl.pallas_call(
        kernel,
        grid=(bsz, s // tq),
        in_specs=[pl.BlockSpec((None, tq, w), lambda b, i: (b, i, cols["qc"] // nh)),
                  pl.BlockSpec((None, s, w), lambda b, i: (b, 0, cols["kc"] // nh), **resident),
                  pl.BlockSpec((None, s, w), lambda b, i: (b, 0, cols["vc"] // nh), **resident)],
        out_specs=pl.BlockSpec((None, tq, w), lambda b, i: (b, i, 0)),
        out_shape=jax.ShapeDtypeStruct((bsz, s, w), F32),
        compiler_params=pltpu.CompilerParams(
            dimension_semantics=("parallel", "arbitrary"), vmem_limit_bytes=VMEM_LIMIT),
        name="stickbreak",
    )(p16, p16, p16)


def _dsa_kernel(qi_ref, smq_ref, q_ref, sm_ref, k_ref, vt_ref, bias_ref, o_ref,
                sc_ref, scb_ref, wb_ref, qc_ref, kct_ref, bd_ref, lg_ref, *, tq, k_sel, wi_lane, wide):
    i = pl.program_id(1)
    tk = tq
    d = HEAD_DIM
    nh = N_HEADS
    ksel = float(k_sel)
    per_wide = wide // tk
    n_wide = (i + per_wide) // per_wide
    sub = 2 * tk
    lane_q = _iota((1, tq), 1)

    def tree(parts, op):
        while len(parts) > 1:
            parts = [op(parts[j], parts[j + 1]) if j + 1 < len(parts) else parts[j]
                     for j in range(0, len(parts), 2)]
        return parts[0]

    def col_fold(x, op=jnp.add, rows=8):
        return tree([x[r * rows:(r + 1) * rows] for r in range(x.shape[0] // rows)], op)

    @pl.when(i == 0)
    def _():
        def prep(g, carry):
            g0 = pl.multiple_of(g * wide, wide)
            kt = sm_ref[pl.ds(g0, wide), :].T[:IDX_DIM, :]
            hi, lo = _split(kt)
            kct_ref[:, pl.ds(g0, wide)] = jnp.concatenate([hi, lo, hi], axis=0)
            return carry
        lax.fori_loop(0, sm_ref.shape[0] // wide, prep, 0)

    smq = smq_ref[...]
    lane = _iota(smq.shape, 1)
    for hh in range(IDX_HEADS):
        qh = qi_ref[:, hh * IDX_DIM:(hh + 1) * IDX_DIM]
        hi, lo = _split(qh)
        qc_ref[hh] = jnp.concatenate([hi, hi, lo], axis=-1)
        w = jnp.sum(jnp.where(lane == wi_lane + hh, smq, 0.0), axis=-1, keepdims=True)
        wb_ref[hh] = jnp.broadcast_to(w * ((IDX_HEADS ** -0.5) * (IDX_DIM ** -0.5)), (tq, tk))

    q2t = (q_ref[...] * ((d ** -0.5) * LOG2E)).T.astype(BF16)
    zero_dq = jnp.zeros((d, tq), BF16)
    for p in range(nh // 2):
        top = jnp.concatenate([q2t[2 * p * d:(2 * p + 1) * d], zero_dq], axis=1)
        bot = jnp.concatenate([zero_dq, q2t[(2 * p + 1) * d:(2 * p + 2) * d]], axis=1)
        bd_ref[p] = jnp.concatenate([top, bot], axis=0)

    limit = i * tq + (lane_q // CHUNK + 1) * CHUNK
    rows_t = _iota((tk, tq), 0)

    def score_group(g, mm, masked):
        mn, mx = mm
        for sb in range(wide // sub):
            k0 = pl.multiple_of(g * wide + sb * sub, sub)
            kct = kct_ref[:, pl.ds(k0, sub)]
            tiles = [jnp.zeros((tq, tk), F32) for _ in range(sub // tk)]
            for hh in range(IDX_HEADS):
                s_h = jnp.dot(qc_ref[hh], kct, preferred_element_type=F32)
                for ti in range(sub // tk):
                    tiles[ti] = tiles[ti] + jnp.maximum(s_h[:, ti * tk:(ti + 1) * tk], 0.0) * wb_ref[hh]
            for ti in range(sub // tk):
                kb = pl.multiple_of(k0 + ti * tk, tk)
                sct = tiles[ti].T
                if masked:
                    adm = (kb + rows_t) < limit
                    mn = jnp.minimum(mn, col_fold(jnp.where(adm, sct, jnp.inf), jnp.minimum))
                    sct = jnp.where(adm, sct, -jnp.inf)
                else:
                    mn = jnp.minimum(mn, col_fold(sct, jnp.minimum))
                mx = jnp.maximum(mx, col_fold(sct, jnp.maximum))
                sc_ref[pl.ds(kb, tk), :] = sct
                scb_ref[pl.ds(kb, tk), :] = _floor_bf16(sct)
        return mn, mx

    def score_pair(j, mm):
        return score_group(2 * j + 1, score_group(2 * j, mm, False), False)

    n_full = n_wide - 1
    mm = lax.fori_loop(0, n_full // 2, score_pair,
                       (jnp.full((8, tq), jnp.inf, F32), jnp.full((8, tq), -jnp.inf, F32)))
    mm = lax.cond(n_full % 2 == 1, lambda c: score_group(n_full - 1, c, False), lambda c: c, mm)
    mn, mx = score_group(n_wide - 1, mm, True)

    n_pairs = (n_wide + 1) // 2

    @pl.when(n_wide % 2 == 1)
    def _():
        sc_ref[pl.ds(pl.multiple_of(n_wide * wide, wide), wide), :] = jnp.full((wide, tq), -jnp.inf, F32)
    rmin = jnp.min(mn, axis=0, keepdims=True)
    rmax = jnp.max(mx, axis=0, keepdims=True)

    def count(pred):
        def body(g, acc):
            blk = sc_ref[pl.ds(pl.multiple_of(g * wide, wide), wide), :]
            return acc + col_fold(pred(blk))
        return jnp.sum(lax.fori_loop(0, n_wide, body, jnp.zeros((8, tq), F32)), axis=0, keepdims=True)

    def max_below(x):
        def body(g, acc):
            blk = sc_ref[pl.ds(pl.multiple_of(g * wide, wide), wide), :]
            return jnp.maximum(acc, col_fold(jnp.where(blk < x, blk, -jnp.inf), jnp.maximum))
        return jnp.max(lax.fori_loop(0, n_wide, body, jnp.full((8, tq), -jnp.inf, F32)), axis=0, keepdims=True)

    n_adm = limit.astype(F32)
    all_sel = n_adm <= ksel

    def bisect(c):
        lo, hi, c_lo = c
        mid = 0.5 * lo + 0.5 * hi
        cm = count(lambda blk: _ind(blk >= mid))
        ge = cm >= ksel
        return jnp.where(ge, mid, lo), jnp.where(ge, hi, mid), jnp.where(ge, cm, c_lo)

    def pending(c_lo, tied):
        return jnp.where(all_sel, 0.0, jnp.where(tied > 0.5, 0.0, _ind(c_lo != ksel)))

    def bisect_coarse(_, c):
        lo, hi, c_lo = c
        mid = _floor_bf16(0.5 * lo + 0.5 * hi).astype(F32)
        t_b = jnp.broadcast_to(mid, (16, tq)).astype(BF16)
        one_b = jnp.ones((16, tq), BF16)
        zero_b = jnp.zeros((16, tq), BF16)

        def body(g, acc):
            blk = scb_ref[pl.ds(pl.multiple_of(g * wide, wide), wide), :]
            ind = [jnp.where(blk[r * 16:(r + 1) * 16] >= t_b, one_b, zero_b) for r in range(wide // 16)]
            return acc + tree(ind, jnp.add).astype(F32)

        acc = lax.fori_loop(0, n_wide, body, jnp.zeros((16, tq), F32))
        cm = jnp.sum(acc, axis=0, keepdims=True)
        ge = cm >= ksel
        return jnp.where(ge, mid, lo), jnp.where(ge, hi, mid), jnp.where(ge, cm, c_lo)

    lo0 = _floor_bf16(rmin).astype(F32)
    hi0 = _floor_bf16(rmax + (jnp.abs(rmax) * (2.0 ** -6) + 1e-30)).astype(F32)
    state = lax.fori_loop(0, BISECT_COARSE, bisect_coarse, (lo0, hi0, n_adm))
    state = lax.fori_loop(0, BISECT_FIXED, lambda _, c: bisect(c), state)

    def round_cond(c):
        return jnp.max(pending(c[0][2], c[1])) > 0.5

    def round_body(c):
        st, tied, v, need = c

        def more_cond(s):
            return jnp.logical_and(s[0] < BISECT_EXTRA, jnp.max(pending(s[1][2], tied)) > 0.5)

        _, st = lax.while_loop(more_cond, lambda s: (s[0] + 1, bisect(s[1])), (jnp.int32(0), st))
        pend = pending(st[2], tied)

        def check(_):
            cand = max_below(st[1])
            c_ge = count(lambda blk: _ind(blk >= cand))
            c_gt = count(lambda blk: _ind(blk > cand))
            ok = jnp.where(pend > 0.5, _ind(c_ge >= ksel), 0.0)
            return (jnp.where(ok > 0.5, 1.0, tied), jnp.where(ok > 0.5, cand, v),
                    jnp.where(ok > 0.5, ksel - c_gt, need))

        tied, v, need = lax.cond(jnp.max(pend) > 0.5, check, lambda _: (tied, v, need), 0)
        return st, tied, v, need

    zeros1 = jnp.zeros((1, tq), F32)
    (lo_f, _, _), tied, v_tie, need = lax.while_loop(round_cond, round_body, (state, zeros1, zeros1, zeros1))
    vth = jnp.where(all_sel, F32_LOWEST, jnp.where(tied > 0.5, v_tie, lo_f))

    @pl.when(jnp.max(tied) > 0.5)
    def _():
        v_eq = jnp.where(tied > 0.5, v_tie, jnp.inf)
        incl = (_iota((tk, tk), 1) <= _iota((tk, tk), 0)).astype(BF16)

        def demote(g, seen):
            g0 = pl.multiple_of(g * wide, wide)
            xs = [sc_ref[pl.ds(g0 + pb * tk, tk), :] for pb in range(per_wide)]
            eqs = [_ind(x == v_eq) for x in xs]
            inblk = [jnp.dot(incl, e.astype(BF16), preferred_element_type=F32) for e in eqs]
            for pb in range(per_wide):
                rank = inblk[pb] + seen
                sc_ref[pl.ds(g0 + pb * tk, tk), :] = jnp.where(eqs[pb] * _ind(rank > need) > 0.5,
                                                               -jnp.inf, xs[pb])
                seen = seen + jnp.sum(col_fold(eqs[pb]), axis=0, keepdims=True)
            return seen

        lax.fori_loop(0, n_wide, demote, zeros1)

    g_near = jnp.maximum(i - 1, 0) // per_wide

    def logit_group(g, mx, near):
        out = list(mx)
        for sb in range(wide // sub):
            k0 = pl.multiple_of(g * wide + sb * sub, sub)
            sel = sc_ref[pl.ds(k0, sub), :] >= vth
            for p in range(nh // 2):
                pair = jnp.dot(k_ref[pl.ds(k0, sub), 2 * p * d:(2 * p + 2) * d], bd_ref[p],
                               preferred_element_type=F32)
                for hh in (2 * p, 2 * p + 1):
                    lm = pair[:, (hh - 2 * p) * tq:(hh - 2 * p + 1) * tq]
                    if near:
                        back = [jnp.clip(i - (g * per_wide + sb * (sub // tk) + pb), 0, 2)
                                for pb in range(sub // tk)]
                        lm = lm + jnp.concatenate([bias_ref[bk, hh] for bk in back], axis=0)
                    lm = jnp.where(sel, lm, NEG_BIG)
                    lg_ref[hh, pl.ds(k0, sub), :] = lm
                    out[hh] = jnp.maximum(out[hh], col_fold(lm, jnp.maximum))
        return tuple(out)

    mx = tuple(jnp.full((8, tq), NEG_BIG, F32) for _ in range(nh))
    def logit_pair(j, mx, near):
        return logit_group(2 * j + 1, logit_group(2 * j, mx, near), near)

    far_pairs = g_near // 2
    mx = lax.fori_loop(0, far_pairs, functools.partial(logit_pair, near=False), mx)
    mx = lax.fori_loop(far_pairs, n_pairs, functools.partial(logit_pair, near=True), mx)
    m_q = [jnp.max(mx[hh], axis=0, keepdims=True) for hh in range(nh)]

    def pv_body(g, carry):
        g0 = pl.multiple_of(g * wide, wide)
        ls, accs = carry
        new_l, new_a = [], []
        for hh in range(nh):
            p = jnp.exp2(lg_ref[hh, pl.ds(g0, wide), :] - m_q[hh])
            new_l.append(ls[hh] + col_fold(p))
            new_a.append(accs[hh] + jnp.dot(vt_ref[hh * d:(hh + 1) * d, pl.ds(g0, wide)], p.astype(BF16),
                                            preferred_element_type=F32))
        return tuple(new_l), tuple(new_a)

    ls, accs = lax.fori_loop(0, n_pairs, lambda j, c: pv_body(2 * j + 1, pv_body(2 * j, c)),
                             (tuple(jnp.zeros((8, tq), F32) for _ in range(nh)),
                              tuple(jnp.zeros((d, tq), F32) for _ in range(nh))))
    for hh in range(nh):
        o_ref[:, hh * d:(hh + 1) * d] = (accs[hh] / jnp.sum(ls[hh], axis=0, keepdims=True)).T


def _dsa(p32, p16, vt, bias_tiles, *, tq, cols):
    bsz, s, _ = p32.shape
    d = HEAD_DIM
    nh = N_HEADS
    wide = 4 * tq
    k_sel = min(TOPK_MAX, s // 4)
    w512 = nh * d
    kernel = functools.partial(_dsa_kernel, tq=tq, k_sel=k_sel, wi_lane=cols["wi_lane"], wide=wide)
    resident = dict(pipeline_mode=pl.Buffered(1))
    return pl.pallas_call(
        kernel,
        grid=(bsz, s // tq),
        in_specs=[pl.BlockSpec((None, tq, w512), lambda b, i: (b, i, cols["qi"] // nh)),
                  pl.BlockSpec((None, tq, d), lambda b, i: (b, i, cols["small"])),
                  pl.BlockSpec((None, tq, w512), lambda b, i: (b, i, cols["qb"] // nh)),
                  pl.BlockSpec((None, s, d), lambda b, i: (b, 0, cols["small"]), **resident),
                  pl.BlockSpec((None, s, w512), lambda b, i: (b, 0, cols["kb"] // nh), **resident),
                  pl.BlockSpec((w512, s), lambda b, i: (0, b), **resident),
                  pl.BlockSpec((3, nh, tq, tq), lambda b, i: (0, 0, 0, 0), **resident)],
        out_specs=pl.BlockSpec((None, tq, w512), lambda b, i: (b, i, 0)),
        out_shape=jax.ShapeDtypeStruct((bsz, s, w512), F32),
        scratch_shapes=[pltpu.VMEM((s, tq), F32),
                        pltpu.VMEM((s, tq), BF16),
                        pltpu.VMEM((IDX_HEADS, tq, tq), F32),
                        pltpu.VMEM((IDX_HEADS, tq, 3 * IDX_DIM), BF16),
                        pltpu.VMEM((3 * IDX_DIM, s), BF16),
                        pltpu.VMEM((nh // 2, 2 * d, 2 * tq), BF16),
                        pltpu.VMEM((nh, s, tq), F32)],
        compiler_params=pltpu.CompilerParams(
            dimension_semantics=("parallel", "arbitrary"), vmem_limit_bytes=VMEM_LIMIT),
        name="dsa",
    )(p32, p32, p32, p32, p16, vt, bias_tiles)


def _t5_bucket(rel):
    nb = REL_BUCKETS // 2
    max_exact = nb // 2
    ret = jnp.where(rel > 0, nb, 0)
    n = jnp.abs(rel)
    large = max_exact + (jnp.log(jnp.maximum(n, 1).astype(F32) / max_exact)
                         / math.log(REL_MAX_DIST / max_exact) * (nb - max_exact)).astype(jnp.int32)
    large = jnp.minimum(large, nb - 1)
    return ret + jnp.where(n < max_exact, n, large)


def _bias_tiles(rel_table, tq):
    assert tq >= REL_MAX_DIST
    t = jnp.arange(tq)
    back = jnp.arange(3)
    rel = (t[None, None, :] - back[:, None, None] * tq) - t[None, :, None]
    onehot = (_t5_bucket(rel)[..., None] == jnp.arange(REL_BUCKETS)).astype(F32)
    tiles = jnp.einsum("bqkn,nh->bhkq", onehot, rel_table.astype(F32),
                       precision=HIGHEST)
    return (tiles - tiles[2:3]) * LOG2E


def _even_layout(w_in):
    d = HEAD_DIM
    a_w = 2 * N_HEADS * d + N_HEADS * d
    offs = {}
    o = 0
    for name, w in (("qkv", a_w), ("z", N_HEADS * d), ("a", N_HEADS), ("b", N_HEADS),
                    ("qb", N_HEADS * d), ("kb", N_HEADS * d), ("vb", N_HEADS * d),
                    ("qi", IDX_HEADS * IDX_DIM), ("ki", IDX_DIM), ("wi", IDX_HEADS)):
        offs[name] = (o, o + w)
        o += w
    assert o == w_in.shape[1]
    sl = lambda n: w_in[:, offs[n][0]:offs[n][1]]
    small_w = IDX_DIM + 2 * N_HEADS + IDX_HEADS
    small_pad = -small_w % d
    w = jnp.concatenate([sl("qkv"), sl("z"), sl("qb"), sl("kb"), sl("vb"), sl("qi"),
                         sl("ki"), sl("a"), sl("b"), sl("wi"),
                         jnp.zeros((w_in.shape[0], small_pad), w_in.dtype)], axis=1)
    nh = N_HEADS
    cols = dict(qa=0, ka=nh, va=2 * nh, za=3 * nh, qb=4 * nh, kb=5 * nh, vb=6 * nh, qi=7 * nh,
                small=8 * nh, a_lane=IDX_DIM, b_lane=IDX_DIM + nh, wi_lane=IDX_DIM + 2 * nh)
    return w.astype(BF16), cols


def kernel(x, norm_g, w_in_even, conv_w_even, a_log_even, dt_bias_even, a_norm_even, w_out_even,
           rel_bias, w_in_odd, lb_logits, d_norm_odd, w_out_odd, w_gate, w_up, w_down):
    bsz, s, d = x.shape
    t = bsz * s
    depth = norm_g.shape[0]
    nh = N_HEADS
    tq = 128
    lb_all = jnp.cumsum(jax.nn.softmax(lb_logits.astype(F32), axis=0), axis=0)
    lb_all = lb_all - lb_all[:1]
    odd_cols = dict(qc=0, kc=nh, vc=2 * nh, qd=3 * nh, fd=4 * nh, id=5 * nh, gd=6 * nh)
    bias_tiles = _bias_tiles(rel_bias, tq)

    h = x.reshape(t, d)
    for l in range(depth):
        if l % 2 == 0:
            e = l // 2
            w_even, cols = _even_layout(w_in_even[e])
            vb0 = cols["vb"] * HEAD_DIM
            w_vt = w_even[:, vb0:vb0 + nh * HEAD_DIM].T
            p32, p16, vt = _norm_matmul(h, norm_g[l, 0], w_even, tm=512, tn=w_even.shape[1] // 3, w_t=w_vt)
            p32 = p32.reshape(bsz, s, -1)
            p16 = p16.reshape(bsz, s, -1)
            o_1 = _deltanet(p32, conv_w_even[e], a_log_even[e], dt_bias_even[e], a_norm_even[e],
                            ts=min(512, s), cols=cols)
            o_2 = _dsa(p32, p16, vt, bias_tiles, tq=tq, cols=cols)
            w_out = w_out_even[e]
        else:
            o = l // 2
            p32, p16 = _norm_matmul(h, norm_g[l, 0], w_in_odd[o].astype(BF16), tm=512, tn=512)
            p32 = p32.reshape(bsz, s, -1)
            p16 = p16.reshape(bsz, s, -1)
            o_1 = _stickbreak(p16, tq=tq, cols=odd_cols)
            o_2 = _hgrn2(p32, lb_all[l], d_norm_odd[o], ts=min(512, s), cols=odd_cols)
            w_out = w_out_odd[o]
        h = _outproj(o_1.reshape(t, -1), o_2.reshape(t, -1), w_out, h, norm_g[l, 1], tm=512)
        h = _ffn(h, norm_g[l, 2], norm_g[l, 3], w_gate[l], w_up[l], w_down[l], tm=1024, tf=256)
    return h.reshape(bsz, s, d)
```

```python
import functools
import math

import jax
import jax.numpy as jnp
from jax import lax
from jax.experimental import pallas as pl
from jax.experimental.pallas import tpu as pltpu

F32 = jnp.float32
BF16 = jnp.bfloat16
HIGHEST = lax.Precision.HIGHEST

CHUNK = 64
HEAD_DIM = 128
N_HEADS = 4
IDX_HEADS = 8
IDX_DIM = 64
TOPK_MAX = 256
CONV_WIDTH = 4
REL_BUCKETS = 32
REL_MAX_DIST = 128
EPS = 1e-6
NEG_BIG = -1e30
LOG2E = 1.4426950408889634
BISECT_COARSE = 12
BISECT_FIXED = 8
BISECT_EXTRA = 6
F32_LOWEST = -3.4028234663852886e38
EXP_ZERO_BELOW = -104.0
VMEM_LIMIT = 56 * 1024 * 1024


def _mm(a, b):
    return jnp.dot(a.astype(BF16), b.astype(BF16), preferred_element_type=F32)


def _mm_nt(a, b):
    return lax.dot_general(a.astype(BF16), b.astype(BF16), (((1,), (1,)), ((), ())),
                           preferred_element_type=F32)


def _mm_tn(a, b):
    return lax.dot_general(a.astype(BF16), b.astype(BF16), (((0,), (0,)), ((), ())),
                           preferred_element_type=F32)


def _mm_f32(a, b):
    return jnp.dot(a, b, precision=HIGHEST, preferred_element_type=F32)


def _split(x):
    hi = x.astype(BF16)
    return hi, (x - hi.astype(F32)).astype(BF16)


def _mm_x3(a, b):
    a_hi, a_lo = _split(a)
    b_hi, b_lo = _split(b)
    return jnp.dot(jnp.concatenate([a_hi, a_hi, a_lo], axis=1),
                   jnp.concatenate([b_hi, b_lo, b_hi], axis=0), preferred_element_type=F32)


def _floor_bf16(x):
    bits = pltpu.bitcast(x, jnp.int32)
    down = jnp.where(bits >= 0, bits, bits + 0xFFFF) & jnp.int32(-65536)
    return pltpu.bitcast(down, F32).astype(BF16)


def _sigmoid(x):
    return 1.0 / (1.0 + jnp.exp(-x))


def _silu(x):
    return x * _sigmoid(x)


def _softplus(x):
    return jnp.maximum(x, 0.0) + jnp.log1p(jnp.exp(-jnp.abs(x)))


def _rms(x, g):
    return x * lax.rsqrt(jnp.mean(x * x, axis=-1, keepdims=True) + EPS) * g


def _iota(shape, dim):
    return lax.broadcasted_iota(jnp.int32, shape, dim)


def _ind(mask):
    return jnp.where(mask, 1.0, 0.0)


def _norm_matmul_kernel(x_ref, g_ref, w_ref, *rest, n_t):
    if n_t:
        wt_ref, o32_ref, o16_ref, ot_ref, xn_ref = rest
    else:
        o32_ref, o16_ref, xn_ref = rest

    @pl.when(pl.program_id(1) == 0)
    def _():
        xn_ref[...] = _rms(x_ref[...], g_ref[...]).astype(BF16)
        if n_t:
            ot_ref[...] = lax.dot_general(wt_ref[...], xn_ref[...], (((1,), (1,)), ((), ())),
                                          preferred_element_type=F32).astype(BF16)

    y = jnp.dot(xn_ref[...], w_ref[...], preferred_element_type=F32)
    o32_ref[...] = y
    o16_ref[...] = y.astype(BF16)


def _norm_matmul(x, g, w, *, tm, tn, w_t=None):
    t, d = x.shape
    n = w.shape[1]
    n_t = 0 if w_t is None else w_t.shape[0]
    in_specs = [pl.BlockSpec((tm, d), lambda i, j: (i, 0)),
                pl.BlockSpec((1, d), lambda i, j: (0, 0)),
                pl.BlockSpec((d, tn), lambda i, j: (0, j))]
    out_specs = [pl.BlockSpec((tm, tn), lambda i, j: (i, j)),
                 pl.BlockSpec((tm, tn), lambda i, j: (i, j))]
    out_shape = [jax.ShapeDtypeStruct((t, n), F32), jax.ShapeDtypeStruct((t, n), BF16)]
    args = [x, g.reshape(1, d), w]
    if n_t:
        in_specs.append(pl.BlockSpec((n_t, d), lambda i, j: (0, 0)))
        out_specs.append(pl.BlockSpec((n_t, tm), lambda i, j: (0, i)))
        out_shape.append(jax.ShapeDtypeStruct((n_t, t), BF16))
        args.append(w_t)
    return pl.pallas_call(
        functools.partial(_norm_matmul_kernel, n_t=n_t),
        grid=(t // tm, n // tn),
        in_specs=in_specs,
        out_specs=out_specs,
        out_shape=out_shape,
        scratch_shapes=[pltpu.VMEM((tm, d), BF16)],
        compiler_params=pltpu.CompilerParams(
            dimension_semantics=("parallel", "arbitrary"), vmem_limit_bytes=VMEM_LIMIT),
        name="norm_matmul",
    )(*args)


def _outproj_kernel(ca_ref, cb_ref, wa_ref, wb_ref, h_ref, g_ref, o_ref):
    y = (jnp.dot(ca_ref[...].astype(BF16), wa_ref[...], preferred_element_type=F32)
         + jnp.dot(cb_ref[...].astype(BF16), wb_ref[...], preferred_element_type=F32))
    o_ref[...] = h_ref[...] + _rms(y, g_ref[...])


def _outproj(ca, cb, w, h, g, *, tm):
    t, d = h.shape
    wa_n = ca.shape[1]
    wb_n = cb.shape[1]
    wa = w[:wa_n].astype(BF16)
    wb = w[wa_n:].astype(BF16)
    return pl.pallas_call(
        _outproj_kernel,
        grid=(t // tm,),
        in_specs=[pl.BlockSpec((tm, wa_n), lambda i: (i, 0)),
                  pl.BlockSpec((tm, wb_n), lambda i: (i, 0)),
                  pl.BlockSpec((wa_n, d), lambda i: (0, 0)),
                  pl.BlockSpec((wb_n, d), lambda i: (0, 0)),
                  pl.BlockSpec((tm, d), lambda i: (i, 0)),
                  pl.BlockSpec((1, d), lambda i: (0, 0))],
        out_specs=pl.BlockSpec((tm, d), lambda i: (i, 0)),
        out_shape=jax.ShapeDtypeStruct((t, d), F32),
        compiler_params=pltpu.CompilerParams(
            dimension_semantics=("parallel",), vmem_limit_bytes=VMEM_LIMIT),
        name="outproj",
    )(ca, cb, wa, wb, h, g.reshape(1, d))


def _ffn_kernel(h_ref, gpre_ref, gpost_ref, wg_ref, wu_ref, wd_ref, o_ref, xn_ref, acc_ref):
    f = pl.program_id(1)

    @pl.when(f == 0)
    def _():
        xn_ref[...] = _rms(h_ref[...], gpre_ref[...]).astype(BF16)
        acc_ref[...] = jnp.zeros_like(acc_ref)

    xn = xn_ref[...]
    gate = jnp.dot(xn, wg_ref[...], preferred_element_type=F32)
    up = jnp.dot(xn, wu_ref[...], preferred_element_type=F32)
    act = (_silu(gate) * up).astype(BF16)
    acc_ref[...] += jnp.dot(act, wd_ref[...], preferred_element_type=F32)

    @pl.when(f == pl.num_programs(1) - 1)
    def _():
        o_ref[...] = h_ref[...] + _rms(acc_ref[...], gpost_ref[...])


def _ffn(h, g_pre, g_post, wg, wu, wd, *, tm, tf):
    t, d = h.shape
    ff = wg.shape[1]
    return pl.pallas_call(
        _ffn_kernel,
        grid=(t // tm, ff // tf),
        in_specs=[pl.BlockSpec((tm, d), lambda i, f: (i, 0)),
                  pl.BlockSpec((1, d), lambda i, f: (0, 0)),
                  pl.BlockSpec((1, d), lambda i, f: (0, 0)),
                  pl.BlockSpec((d, tf), lambda i, f: (0, f)),
                  pl.BlockSpec((d, tf), lambda i, f: (0, f)),
                  pl.BlockSpec((tf, d), lambda i, f: (f, 0))],
        out_specs=pl.BlockSpec((tm, d), lambda i, f: (i, 0)),
        out_shape=jax.ShapeDtypeStruct((t, d), F32),
        scratch_shapes=[pltpu.VMEM((tm, d), BF16), pltpu.VMEM((tm, d), F32)],
        compiler_params=pltpu.CompilerParams(
            dimension_semantics=("parallel", "arbitrary"), vmem_limit_bytes=VMEM_LIMIT),
        name="ffn",
    )(h, g_pre.reshape(1, d), g_post.reshape(1, d),
      wg.astype(BF16), wu.astype(BF16), wd.astype(BF16))


def _deltanet_kernel(xq_ref, xk_ref, xv_ref, z_ref, sm_ref, cwq_ref, cwk_ref, cwv_ref,
                     alog_ref, dtb_ref, gn_ref, o_ref,
                     xpad_ref, q_ref, k_ref, v_ref, gb_ref, bb_ref, u_ref, w_ref, qk_ref, st_ref,
                     *, ts, a_col, b_col):
    s = pl.program_id(1)
    c = CHUNK
    d = HEAD_DIM
    nh = N_HEADS

    @pl.when(s == 0)
    def _():
        xpad_ref[:, 0:8, :] = jnp.zeros((3, 8, nh * d), F32)
        st_ref[...] = jnp.zeros_like(st_ref)

    @pl.when(s != 0)
    def _():
        xpad_ref[:, 0:8, :] = xpad_ref[:, ts:ts + 8, :]

    xpad_ref[0, 8:ts + 8, :] = xq_ref[...]
    xpad_ref[1, 8:ts + 8, :] = xk_ref[...]
    xpad_ref[2, 8:ts + 8, :] = xv_ref[...]

    def conv_silu(idx, cw_ref, hs):
        cw = cw_ref[:, hs]
        acc = xpad_ref[idx, 8 - (CONV_WIDTH - 1):8 - (CONV_WIDTH - 1) + ts, hs] * cw[0:1, :]
        for j in range(1, CONV_WIDTH):
            off = 8 - (CONV_WIDTH - 1) + j
            acc = acc + xpad_ref[idx, off:off + ts, hs] * cw[j:j + 1, :]
        return _silu(acc)

    def l2norm(t):
        return t * lax.rsqrt(jnp.sum(t * t, axis=-1, keepdims=True) + EPS)

    row = _iota((c, c), 0)
    col = _iota((c, c), 1)
    tri = (col <= row)
    strict = (col < row)
    tri_f = tri.astype(F32)
    upper_f = (row <= col).astype(F32)
    eye = (row == col).astype(F32)
    gnorm = gn_ref[...]
    chunks = range(ts // c)
    rs = [slice(ci * c, (ci + 1) * c) for ci in chunks]
    tri2 = jnp.concatenate([tri_f, tri_f], axis=1).astype(BF16)
    ones2 = jnp.ones((c, 2 * c), BF16)

    def cum2(lhs2, x):
        hi, lo = _split(x)
        return jnp.dot(lhs2, jnp.concatenate([hi, lo], axis=0), preferred_element_type=F32)

    for hh in range(nh):
        hs = slice(hh * d, (hh + 1) * d)
        q_ref[:, hs] = l2norm(conv_silu(0, cwq_ref, hs)) * (d ** -0.5)
        k_ref[:, hs] = l2norm(conv_silu(1, cwk_ref, hs))
        v_ref[:, hs] = conv_silu(2, cwv_ref, hs)

        a_raw = sm_ref[:, a_col + hh:a_col + hh + 1]
        b_raw = sm_ref[:, b_col + hh:b_col + hh + 1]
        g = -jnp.exp(alog_ref[:, hh:hh + 1]) * _softplus(a_raw + dtb_ref[:, hh:hh + 1])
        gb_ref[:, hs] = jnp.broadcast_to(g, (ts, d))
        bb_ref[:, hs] = jnp.broadcast_to(_sigmoid(b_raw), (ts, d))

        q = [q_ref[r, hs] for r in rs]
        k = [k_ref[r, hs] for r in rs]
        beta = [bb_ref[r, hs] for r in rs]
        gb = [gb_ref[r, hs] for r in rs]
        gc = [cum2(tri2, x) for x in gb]
        gc_row = [cum2(ones2, x[:, :c] * upper_f) for x in gb]
        decay = [jnp.where(tri, jnp.exp(jnp.minimum(a[:, :c] - b, 0.0)), 0.0) for a, b in zip(gc, gc_row)]
        kk = [_mm_nt(x, x) for x in k]
        n = [-jnp.where(strict, b[:, :c] * x * dc, 0.0) for b, x, dc in zip(beta, kk, decay)]
        inv = [eye + x for x in n]
        for _ in range(5):
            n = [_mm_x3(x, x) for x in n]
            inv = [iv + _mm_x3(iv, x) for iv, x in zip(inv, n)]
        egc = [jnp.exp(x) for x in gc]
        gl = [x[c - 1:c, :] for x in gc]
        for ci in chunks:
            r = rs[ci]
            u_ref[r, hs] = _mm_x3(inv[ci], v_ref[r, hs] * beta[ci])
            w_ref[r, hs] = _mm_x3(inv[ci], k[ci] * (beta[ci] * egc[ci]))
            qk_ref[hh, r, :] = _mm_nt(q[ci], k[ci]) * decay[ci]
            q_ref[r, hs] = q[ci] * egc[ci]
            k_ref[r, hs] = k[ci] * jnp.exp(gl[ci] - gc[ci])
            gb_ref[r, hs] = jnp.broadcast_to(jnp.exp(gl[ci]), (c, d))

    def chunk_body(ci, carry):
        r0 = pl.multiple_of(ci * c, c)
        for hh in range(nh):
            hs = slice(hh * d, (hh + 1) * d)
            st = st_ref[hh]
            v_new = u_ref[pl.ds(r0, c), hs] - _mm(w_ref[pl.ds(r0, c), hs], st)
            o = _mm(q_ref[pl.ds(r0, c), hs], st) + _mm(qk_ref[hh, pl.ds(r0, c), :], v_new)
            st_ref[hh] = st * gb_ref[pl.ds(r0, 1), hs] + _mm_tn(k_ref[pl.ds(r0, c), hs], v_new)
            o_ref[pl.ds(r0, c), hs] = _rms(o, gnorm) * _silu(z_ref[pl.ds(r0, c), hs])
        return carry

    lax.fori_loop(0, ts // c, chunk_body, 0)


def _deltanet(p32, conv_w, a_log, dt_bias, a_norm_g, *, ts, cols):
    bsz, s, _ = p32.shape
    d = HEAD_DIM
    nh = N_HEADS
    w = nh * d
    pad = lambda t: jnp.pad(t.astype(F32), (0, d - t.shape[0])).reshape(1, d)
    kernel = functools.partial(_deltanet_kernel, ts=ts, a_col=cols["a_lane"], b_col=cols["b_lane"])
    tile = lambda name: pl.BlockSpec((None, ts, w), lambda b, i: (b, i, cols[name] // nh))
    conv = lambda k: pl.BlockSpec((CONV_WIDTH, w), lambda b, i: (0, k))
    row = pl.BlockSpec((1, d), lambda b, i: (0, 0))
    return pl.pallas_call(
        kernel,
        grid=(bsz, s // ts),
        in_specs=[tile("qa"), tile("ka"), tile("va"), tile("za"),
                  pl.BlockSpec((None, ts, d), lambda b, i: (b, i, cols["small"])),
                  conv(0), conv(1), conv(2), row, row, row],
        out_specs=pl.BlockSpec((None, ts, w), lambda b, i: (b, i, 0)),
        out_shape=jax.ShapeDtypeStruct((bsz, s, w), F32),
        scratch_shapes=[pltpu.VMEM((3, ts + 8, w), F32)]
        + [pltpu.VMEM((ts, w), F32) for _ in range(7)]
        + [pltpu.VMEM((nh, ts, CHUNK), F32), pltpu.VMEM((nh, d, d), F32)],
        compiler_params=pltpu.CompilerParams(
            dimension_semantics=("parallel", "arbitrary"), vmem_limit_bytes=VMEM_LIMIT),
        name="deltanet",
    )(p32, p32, p32, p32, p32, conv_w.astype(F32), conv_w.astype(F32), conv_w.astype(F32),
      pad(a_log), pad(dt_bias), a_norm_g.astype(F32).reshape(1, d))


def _hgrn2_kernel(q_ref, f_ref, i_ref, gate_ref, lb_ref, gn_ref, o_ref,
                  qs_ref, ks_ref, gc_ref, st_ref, *, ts):
    s = pl.program_id(1)
    c = CHUNK
    d = HEAD_DIM
    nh = N_HEADS
    SUB = 16

    @pl.when(s == 0)
    def _():
        st_ref[...] = jnp.zeros_like(st_ref)

    lb = lb_ref[...]
    f_raw = f_ref[...]
    log_sig = jnp.minimum(f_raw, 0.0) - jnp.log1p(jnp.exp(-jnp.abs(f_raw)))
    la = jnp.log(lb)
    lbb = jnp.log1p(-lb) + log_sig
    log_f = jnp.maximum(la, lbb) + jnp.log1p(jnp.exp(-jnp.abs(la - lbb)))
    qs_ref[...] = _silu(q_ref[...])
    ks_ref[...] = (1.0 - lb) * _sigmoid(-f_raw)

    row = _iota((c, c), 0)
    col = _iota((c, c), 1)
    tri_f = (col <= row).astype(F32)
    ones_dd = jnp.ones((d, d), BF16)
    rows_8d = _iota((8, d), 0)
    gnorm = gn_ref[...]

    for ci in range(ts // c):
        gc_ref[ci * c:(ci + 1) * c, :] = _mm_f32(tri_f, log_f[ci * c:(ci + 1) * c, :])

    blocks = [(sb * SUB, (sb + 1) * SUB) for sb in range(c // SUB)]

    def head_chunk(r0, hh):
        hs = slice(hh * d, (hh + 1) * d)
        q = qs_ref[pl.ds(r0, c), hs]
        k = ks_ref[pl.ds(r0, c), hs]
        v = i_ref[pl.ds(r0, c), hs]
        gc = gc_ref[pl.ds(r0, c), hs]

        prods = []
        for top, end in blocks:
            for j in range(top, end):
                lo = (j // 8) * 8
                k_j = k[j:j + 1, :]
                g_j = gc[j:j + 1, :]
                e = jnp.exp(jnp.minimum(gc[lo:end, :] - g_j, 0.0))
                if j % 8:
                    head = jnp.where(rows_8d >= j - lo, e[:8], 0.0)
                    e = jnp.concatenate([head, e[8:]], axis=0) if lo + 8 < end else head
                prods.append(q[lo:end, :] * k_j * e)
        sums = jnp.dot(jnp.concatenate(prods, axis=0).astype(BF16), ones_dd,
                       preferred_element_type=F32)
        qk_far = []
        for top, end in blocks[1:]:
            g_b = gc[top - 1:top, :]
            qe = q[top:end, :] * jnp.exp(gc[top:end, :] - g_b)
            ke = k[:top, :] * jnp.exp(jnp.minimum(g_b - gc[:top, :], 0.0))
            qk_far.append(_mm_nt(qe, ke))
        far = [_mm(a, v[:top, :]) for a, (top, _) in zip(qk_far, blocks[1:])]

        groups = [jnp.zeros((8, d), F32) for _ in range(c // 8)]
        at = 0
        for top, end in blocks:
            for j in range(top, end):
                v_j = v[j:j + 1, :]
                for g in range(j // 8, end // 8):
                    groups[g] = groups[g] + sums[at:at + 8, :] * v_j
                    at += 8
        for f, (top, end) in zip(far, blocks[1:]):
            for g in range(top // 8, end // 8):
                groups[g] = groups[g] + f[(g * 8 - top):(g * 8 - top + 8), :]
        o_intra = jnp.concatenate(groups, axis=0)

        st = st_ref[hh]
        gl = gc[c - 1:c, :]
        o = o_intra + _mm_nt(q * jnp.exp(gc), st)
        st_ref[hh] = st * jnp.exp(gl) + _mm_tn(v, k * jnp.exp(gl - gc))
        o_ref[pl.ds(r0, c), hs] = _rms(o, gnorm) * _silu(gate_ref[pl.ds(r0, c), hs])

    def chunk_loop(ci, carry):
        r0 = pl.multiple_of(ci * c, c)
        for hh in range(nh):
            head_chunk(r0, hh)
        return carry

    lax.fori_loop(0, ts // c, chunk_loop, 0)


def _hgrn2(p32, lb, d_norm_g, *, ts, cols):
    bsz, s, _ = p32.shape
    d = HEAD_DIM
    nh = N_HEADS
    w = nh * d
    kernel = functools.partial(_hgrn2_kernel, ts=ts)
    tile = lambda name: pl.BlockSpec((None, ts, w), lambda b, i: (b, i, cols[name] // nh))
    return pl.pallas_call(
        kernel,
        grid=(bsz, s // ts),
        in_specs=[tile("qd"), tile("fd"), tile("id"), tile("gd"),
                  pl.BlockSpec((1, w), lambda b, i: (0, 0)),
                  pl.BlockSpec((1, d), lambda b, i: (0, 0))],
        out_specs=pl.BlockSpec((None, ts, w), lambda b, i: (b, i, 0)),
        out_shape=jax.ShapeDtypeStruct((bsz, s, w), F32),
        scratch_shapes=[pltpu.VMEM((ts, w), F32), pltpu.VMEM((ts, w), F32),
                        pltpu.VMEM((ts, w), F32), pltpu.VMEM((nh, d, d), F32)],
        compiler_params=pltpu.CompilerParams(
            dimension_semantics=("parallel", "arbitrary"), vmem_limit_bytes=VMEM_LIMIT),
        name="hgrn2",
    )(p32, p32, p32, p32, lb.astype(F32).reshape(1, w), d_norm_g.astype(F32).reshape(1, d))


def _stickbreak_kernel(q_ref, k_ref, v_ref, o_ref, *, tq):
    i = pl.program_id(1)
    d = HEAD_DIM
    nh = N_HEADS
    row = _iota((tq, tq), 0)
    col = _iota((tq, tq), 1)
    causal = col < row
    later = (row > col).astype(BF16)

    def block(j, carries, diag):
        r0 = pl.multiple_of(j * tq, tq)
        out = []
        for hh in range(nh):
            hs = slice(hh * d, (hh + 1) * d)
            z = _mm_nt(q_ref[:, hs], k_ref[pl.ds(r0, tq), hs]) * (d ** -0.5)
            sp = _softplus(z)
            l1m = jnp.where(causal, -sp, 0.0) if diag else -sp
            l_hi, l_lo = _split(l1m)
            rest = (jnp.dot(l_hi, later, preferred_element_type=F32)
                    + jnp.dot(l_lo, later, preferred_element_type=F32))
            p = jnp.exp((z - sp) + rest + carries[hh])
            if diag:
                p = jnp.where(causal, p, 0.0)
            pv = _mm(p, v_ref[pl.ds(r0, tq), hs])
            if diag:
                o_ref[:, hs] = pv
            else:
                o_ref[:, hs] += pv
            out.append(carries[hh] + jnp.sum(l1m, axis=-1, keepdims=True))
        return tuple(out)

    carries = block(i, tuple(jnp.zeros((tq, 1), F32) for _ in range(nh)), True)

    def cond(c):
        worst = functools.reduce(jnp.maximum, c[1])
        return jnp.logical_and(c[0] >= 0, jnp.max(worst) >= EXP_ZERO_BELOW)

    def body(c):
        return c[0] - 1, block(c[0], c[1], False)

    lax.while_loop(cond, body, (i - 1, carries))


def _stickbreak(p16, *, tq, cols):
    bsz, s, _ = p16.shape
    nh = N_HEADS
    w = nh * HEAD_DIM
    kernel = functools.partial(_stickbreak_kernel, tq=tq)
    resident = dict(pipeline_mode=pl.Buffered(1))
    return pl.pallas_call(
        kernel,
        grid=(bsz, s // tq),
        in_specs=[pl.BlockSpec((None, tq, w), lambda b, i: (b, i, cols["qc"] // nh)),
                  pl.BlockSpec((None, s, w), lambda b, i: (b, 0, cols["kc"] // nh), **resident),
                  pl.BlockSpec((None, s, w), lambda b, i: (b, 0, cols["vc"] // nh), **resident)],
        out_specs=pl.BlockSpec((None, tq, w), lambda b, i: (b, i, 0)),
        out_shape=jax.ShapeDtypeStruct((bsz, s, w), F32),
        compiler_params=pltpu.CompilerParams(
            dimension_semantics=("parallel", "arbitrary"), vmem_limit_bytes=VMEM_LIMIT),
        name="stickbreak",
    )(p16, p16, p16)


def _dsa_kernel(qi_ref, smq_ref, q_ref, sm_ref, k_ref, vt_ref, bias_ref, o_ref,
                sc_ref, scb_ref, wb_ref, qc_ref, kct_ref, bd_ref, lg_ref, *, tq, k_sel, wi_lane, wide):
    i = pl.program_id(1)
    tk = tq
    d = HEAD_DIM
    nh = N_HEADS
    ksel = float(k_sel)
    per_wide = wide // tk
    n_wide = (i + per_wide) // per_wide
    sub = 2 * tk
    lane_q = _iota((1, tq), 1)

    def tree(parts, op):
        while len(parts) > 1:
            parts = [op(parts[j], parts[j + 1]) if j + 1 < len(parts) else parts[j]
                     for j in range(0, len(parts), 2)]
        return parts[0]

    def col_fold(x, op=jnp.add, rows=8):
        return tree([x[r * rows:(r + 1) * rows] for r in range(x.shape[0] // rows)], op)

    @pl.when(i == 0)
    def _():
        def prep(g, carry):
            g0 = pl.multiple_of(g * wide, wide)
            kt = sm_ref[pl.ds(g0, wide), :].T[:IDX_DIM, :]
            hi, lo = _split(kt)
            kct_ref[:, pl.ds(g0, wide)] = jnp.concatenate([hi, lo, hi], axis=0)
            return carry
        lax.fori_loop(0, sm_ref.shape[0] // wide, prep, 0)

    smq = smq_ref[...]
    lane = _iota(smq.shape, 1)
    for hh in range(IDX_HEADS):
        qh = qi_ref[:, hh * IDX_DIM:(hh + 1) * IDX_DIM]
        hi, lo = _split(qh)
        qc_ref[hh] = jnp.concatenate([hi, hi, lo], axis=-1)
        w = jnp.sum(jnp.where(lane == wi_lane + hh, smq, 0.0), axis=-1, keepdims=True)
        wb_ref[hh] = jnp.broadcast_to(w * ((IDX_HEADS ** -0.5) * (IDX_DIM ** -0.5)), (tq, tk))

    q2t = (q_ref[...] * ((d ** -0.5) * LOG2E)).T.astype(BF16)
    zero_dq = jnp.zeros((d, tq), BF16)
    for p in range(nh // 2):
        top = jnp.concatenate([q2t[2 * p * d:(2 * p + 1) * d], zero_dq], axis=1)
        bot = jnp.concatenate([zero_dq, q2t[(2 * p + 1) * d:(2 * p + 2) * d]], axis=1)
        bd_ref[p] = jnp.concatenate([top, bot], axis=0)

    limit = i * tq + (lane_q // CHUNK + 1) * CHUNK
    rows_t = _iota((tk, tq), 0)

    def score_group(g, mm, masked):
        mn, mx = mm
        for sb in range(wide // sub):
            k0 = pl.multiple_of(g * wide + sb * sub, sub)
            kct = kct_ref[:, pl.ds(k0, sub)]
            tiles = [jnp.zeros((tq, tk), F32) for _ in range(sub // tk)]
            for hh in range(IDX_HEADS):
                s_h = jnp.dot(qc_ref[hh], kct, preferred_element_type=F32)
                for ti in range(sub // tk):
                    tiles[ti] = tiles[ti] + jnp.maximum(s_h[:, ti * tk:(ti + 1) * tk], 0.0) * wb_ref[hh]
            for ti in range(sub // tk):
                kb = pl.multiple_of(k0 + ti * tk, tk)
                sct = tiles[ti].T
                if masked:
                    adm = (kb + rows_t) < limit
                    mn = jnp.minimum(mn, col_fold(jnp.where(adm, sct, jnp.inf), jnp.minimum))
                    sct = jnp.where(adm, sct, -jnp.inf)
                else:
                    mn = jnp.minimum(mn, col_fold(sct, jnp.minimum))
                mx = jnp.maximum(mx, col_fold(sct, jnp.maximum))
                sc_ref[pl.ds(kb, tk), :] = sct
                scb_ref[pl.ds(kb, tk), :] = _floor_bf16(sct)
        return mn, mx

    def score_pair(j, mm):
        return score_group(2 * j + 1, score_group(2 * j, mm, False), False)

    n_full = n_wide - 1
    mm = lax.fori_loop(0, n_full // 2, score_pair,
                       (jnp.full((8, tq), jnp.inf, F32), jnp.full((8, tq), -jnp.inf, F32)))
    mm = lax.cond(n_full % 2 == 1, lambda c: score_group(n_full - 1, c, False), lambda c: c, mm)
    mn, mx = score_group(n_wide - 1, mm, True)

    n_pairs = (n_wide + 1) // 2

    @pl.when(n_wide % 2 == 1)
    def _():
        sc_ref[pl.ds(pl.multiple_of(n_wide * wide, wide), wide), :] = jnp.full((wide, tq), -jnp.inf, F32)
    rmin = jnp.min(mn, axis=0, keepdims=True)
    rmax = jnp.max(mx, axis=0, keepdims=True)

    def count(pred):
        def body(g, acc):
            blk = sc_ref[pl.ds(pl.multiple_of(g * wide, wide), wide), :]
            return acc + col_fold(pred(blk))
        return jnp.sum(lax.fori_loop(0, n_wide, body, jnp.zeros((8, tq), F32)), axis=0, keepdims=True)

    def max_below(x):
        def body(g, acc):
            blk = sc_ref[pl.ds(pl.multiple_of(g * wide, wide), wide), :]
            return jnp.maximum(acc, col_fold(jnp.where(blk < x, blk, -jnp.inf), jnp.maximum))
        return jnp.max(lax.fori_loop(0, n_wide, body, jnp.full((8, tq), -jnp.inf, F32)), axis=0, keepdims=True)

    n_adm = limit.astype(F32)
    all_sel = n_adm <= ksel

    def bisect(c):
        lo, hi, c_lo = c
        mid = 0.5 * lo + 0.5 * hi
        cm = count(lambda blk: _ind(blk >= mid))
        ge = cm >= ksel
        return jnp.where(ge, mid, lo), jnp.where(ge, hi, mid), jnp.where(ge, cm, c_lo)

    def pending(c_lo, tied):
        return jnp.where(all_sel, 0.0, jnp.where(tied > 0.5, 0.0, _ind(c_lo != ksel)))

    def bisect_coarse(_, c):
        lo, hi, c_lo = c
        mid = _floor_bf16(0.5 * lo + 0.5 * hi).astype(F32)
        t_b = jnp.broadcast_to(mid, (16, tq)).astype(BF16)
        one_b = jnp.ones((16, tq), BF16)
        zero_b = jnp.zeros((16, tq), BF16)

        def body(g, acc):
            blk = scb_ref[pl.ds(pl.multiple_of(g * wide, wide), wide), :]
            ind = [jnp.where(blk[r * 16:(r + 1) * 16] >= t_b, one_b, zero_b) for r in range(wide // 16)]
            return acc + tree(ind, jnp.add).astype(F32)

        acc = lax.fori_loop(0, n_wide, body, jnp.zeros((16, tq), F32))
        cm = jnp.sum(acc, axis=0, keepdims=True)
        ge = cm >= ksel
        return jnp.where(ge, mid, lo), jnp.where(ge, hi, mid), jnp.where(ge, cm, c_lo)

    lo0 = _floor_bf16(rmin).astype(F32)
    hi0 = _floor_bf16(rmax + (jnp.abs(rmax) * (2.0 ** -6) + 1e-30)).astype(F32)
    state = lax.fori_loop(0, BISECT_COARSE, bisect_coarse, (lo0, hi0, n_adm))
    state = lax.fori_loop(0, BISECT_FIXED, lambda _, c: bisect(c), state)

    def round_cond(c):
        return jnp.max(pending(c[0][2], c[1])) > 0.5

    def round_body(c):
        st, tied, v, need = c

        def more_cond(s):
            return jnp.logical_and(s[0] < BISECT_EXTRA, jnp.max(pending(s[1][2], tied)) > 0.5)

        _, st = lax.while_loop(more_cond, lambda s: (s[0] + 1, bisect(s[1])), (jnp.int32(0), st))
        pend = pending(st[2], tied)

        def check(_):
            cand = max_below(st[1])
            c_ge = count(lambda blk: _ind(blk >= cand))
            c_gt = count(lambda blk: _ind(blk > cand))
            ok = jnp.where(pend > 0.5, _ind(c_ge >= ksel), 0.0)
            return (jnp.where(ok > 0.5, 1.0, tied), jnp.where(ok > 0.5, cand, v),
                    jnp.where(ok > 0.5, ksel - c_gt, need))

        tied, v, need = lax.cond(jnp.max(pend) > 0.5, check, lambda _: (tied, v, need), 0)
        return st, tied, v, need

    zeros1 = jnp.zeros((1, tq), F32)
    (lo_f, _, _), tied, v_tie, need = lax.while_loop(round_cond, round_body, (state, zeros1, zeros1, zeros1))
    vth = jnp.where(all_sel, F32_LOWEST, jnp.where(tied > 0.5, v_tie, lo_f))

    @pl.when(jnp.max(tied) > 0.5)
    def _():
        v_eq = jnp.where(tied > 0.5, v_tie, jnp.inf)
        incl = (_iota((tk, tk), 1) <= _iota((tk, tk), 0)).astype(BF16)

        def demote(g, seen):
            g0 = pl.multiple_of(g * wide, wide)
            xs = [sc_ref[pl.ds(g0 + pb * tk, tk), :] for pb in range(per_wide)]
            eqs = [_ind(x == v_eq) for x in xs]
            inblk = [jnp.dot(incl, e.astype(BF16), preferred_element_type=F32) for e in eqs]
            for pb in range(per_wide):
                rank = inblk[pb] + seen
                sc_ref[pl.ds(g0 + pb * tk, tk), :] = jnp.where(eqs[pb] * _ind(rank > need) > 0.5,
                                                               -jnp.inf, xs[pb])
                seen = seen + jnp.sum(col_fold(eqs[pb]), axis=0, keepdims=True)
            return seen

        lax.fori_loop(0, n_wide, demote, zeros1)

    g_near = jnp.maximum(i - 1, 0) // per_wide

    def logit_group(g, mx, near):
        out = list(mx)
        for sb in range(wide // sub):
            k0 = pl.multiple_of(g * wide + sb * sub, sub)
            sel = sc_ref[pl.ds(k0, sub), :] >= vth
            for p in range(nh // 2):
                pair = jnp.dot(k_ref[pl.ds(k0, sub), 2 * p * d:(2 * p + 2) * d], bd_ref[p],
                               preferred_element_type=F32)
                for hh in (2 * p, 2 * p + 1):
                    lm = pair[:, (hh - 2 * p) * tq:(hh - 2 * p + 1) * tq]
                    if near:
                        back = [jnp.clip(i - (g * per_wide + sb * (sub // tk) + pb), 0, 2)
                                for pb in range(sub // tk)]
                        lm = lm + jnp.concatenate([bias_ref[bk, hh] for bk in back], axis=0)
                    lm = jnp.where(sel, lm, NEG_BIG)
                    lg_ref[hh, pl.ds(k0, sub), :] = lm
                    out[hh] = jnp.maximum(out[hh], col_fold(lm, jnp.maximum))
        return tuple(out)

    mx = tuple(jnp.full((8, tq), NEG_BIG, F32) for _ in range(nh))
    def logit_pair(j, mx, near):
        return logit_group(2 * j + 1, logit_group(2 * j, mx, near), near)

    far_pairs = g_near // 2
    mx = lax.fori_loop(0, far_pairs, functools.partial(logit_pair, near=False), mx)
    mx = lax.fori_loop(far_pairs, n_pairs, functools.partial(logit_pair, near=True), mx)
    m_q = [jnp.max(mx[hh], axis=0, keepdims=True) for hh in range(nh)]

    def pv_body(g, carry):
        g0 = pl.multiple_of(g * wide, wide)
        ls, accs = carry
        new_l, new_a = [], []
        for hh in range(nh):
            p = jnp.exp2(lg_ref[hh, pl.ds(g0, wide), :] - m_q[hh])
            new_l.append(ls[hh] + col_fold(p))
            new_a.append(accs[hh] + jnp.dot(vt_ref[hh * d:(hh + 1) * d, pl.ds(g0, wide)], p.astype(BF16),
                                            preferred_element_type=F32))
        return tuple(new_l), tuple(new_a)

    ls, accs = lax.fori_loop(0, n_pairs, lambda j, c: pv_body(2 * j + 1, pv_body(2 * j, c)),
                             (tuple(jnp.zeros((8, tq), F32) for _ in range(nh)),
                              tuple(jnp.zeros((d, tq), F32) for _ in range(nh))))
    for hh in range(nh):
        o_ref[:, hh * d:(hh + 1) * d] = (accs[hh] / jnp.sum(ls[hh], axis=0, keepdims=True)).T


def _dsa(p32, p16, vt, bias_tiles, *, tq, cols):
    bsz, s, _ = p32.shape
    d = HEAD_DIM
    nh = N_HEADS
    wide = 4 * tq
    k_sel = min(TOPK_MAX, s // 4)
    w512 = nh * d
    kernel = functools.partial(_dsa_kernel, tq=tq, k_sel=k_sel, wi_lane=cols["wi_lane"], wide=wide)
    resident = dict(pipeline_mode=pl.Buffered(1))
    return pl.pallas_call(
        kernel,
        grid=(bsz, s // tq),
        in_specs=[pl.BlockSpec((None, tq, w512), lambda b, i: (b, i, cols["qi"] // nh)),
                  pl.BlockSpec((None, tq, d), lambda b, i: (b, i, cols["small"])),
                  pl.BlockSpec((None, tq, w512), lambda b, i: (b, i, cols["qb"] // nh)),
                  pl.BlockSpec((None, s, d), lambda b, i: (b, 0, cols["small"]), **resident),
                  pl.BlockSpec((None, s, w512), lambda b, i: (b, 0, cols["kb"] // nh), **resident),
                  pl.BlockSpec((w512, s), lambda b, i: (0, b), **resident),
                  pl.BlockSpec((3, nh, tq, tq), lambda b, i: (0, 0, 0, 0), **resident)],
        out_specs=pl.BlockSpec((None, tq, w512), lambda b, i: (b, i, 0)),
        out_shape=jax.ShapeDtypeStruct((bsz, s, w512), F32),
        scratch_shapes=[pltpu.VMEM((s, tq), F32),
                        pltpu.VMEM((s, tq), BF16),
                        pltpu.VMEM((IDX_HEADS, tq, tq), F32),
                        pltpu.VMEM((IDX_HEADS, tq, 3 * IDX_DIM), BF16),
                        pltpu.VMEM((3 * IDX_DIM, s), BF16),
                        pltpu.VMEM((nh // 2, 2 * d, 2 * tq), BF16),
                        pltpu.VMEM((nh, s, tq), F32)],
        compiler_params=pltpu.CompilerParams(
            dimension_semantics=("parallel", "arbitrary"), vmem_limit_bytes=VMEM_LIMIT),
        name="dsa",
    )(p32, p32, p32, p32, p16, vt, bias_tiles)


def _t5_bucket(rel):
    nb = REL_BUCKETS // 2
    max_exact = nb // 2
    ret = jnp.where(rel > 0, nb, 0)
    n = jnp.abs(rel)
    large = max_exact + (jnp.log(jnp.maximum(n, 1).astype(F32) / max_exact)
                         / math.log(REL_MAX_DIST / max_exact) * (nb - max_exact)).astype(jnp.int32)
    large = jnp.minimum(large, nb - 1)
    return ret + jnp.where(n < max_exact, n, large)


def _bias_tiles(rel_table, tq):
    assert tq >= REL_MAX_DIST
    t = jnp.arange(tq)
    back = jnp.arange(3)
    rel = (t[None, None, :] - back[:, None, None] * tq) - t[None, :, None]
    onehot = (_t5_bucket(rel)[..., None] == jnp.arange(REL_BUCKETS)).astype(F32)
    tiles = jnp.einsum("bqkn,nh->bhkq", onehot, rel_table.astype(F32),
                       precision=HIGHEST)
    return (tiles - tiles[2:3]) * LOG2E


def _even_layout(w_in):
    d = HEAD_DIM
    a_w = 2 * N_HEADS * d + N_HEADS * d
    offs = {}
    o = 0
    for name, w in (("qkv", a_w), ("z", N_HEADS * d), ("a", N_HEADS), ("b", N_HEADS),
                    ("qb", N_HEADS * d), ("kb", N_HEADS * d), ("vb", N_HEADS * d),
                    ("qi", IDX_HEADS * IDX_DIM), ("ki", IDX_DIM), ("wi", IDX_HEADS)):
        offs[name] = (o, o + w)
        o += w
    assert o == w_in.shape[1]
    sl = lambda n: w_in[:, offs[n][0]:offs[n][1]]
    small_w = IDX_DIM + 2 * N_HEADS + IDX_HEADS
    small_pad = -small_w % d
    w = jnp.concatenate([sl("qkv"), sl("z"), sl("qb"), sl("kb"), sl("vb"), sl("qi"),
                         sl("ki"), sl("a"), sl("b"), sl("wi"),
                         jnp.zeros((w_in.shape[0], small_pad), w_in.dtype)], axis=1)
    nh = N_HEADS
    cols = dict(qa=0, ka=nh, va=2 * nh, za=3 * nh, qb=4 * nh, kb=5 * nh, vb=6 * nh, qi=7 * nh,
                small=8 * nh, a_lane=IDX_DIM, b_lane=IDX_DIM + nh, wi_lane=IDX_DIM + 2 * nh)
    return w.astype(BF16), cols


def kernel(x, norm_g, w_in_even, conv_w_even, a_log_even, dt_bias_even, a_norm_even, w_out_even,
           rel_bias, w_in_odd, lb_logits, d_norm_odd, w_out_odd, w_gate, w_up, w_down):
    bsz, s, d = x.shape
    t = bsz * s
    depth = norm_g.shape[0]
    nh = N_HEADS
    tq = 128
    lb_all = jnp.cumsum(jax.nn.softmax(lb_logits.astype(F32), axis=0), axis=0)
    lb_all = lb_all - lb_all[:1]
    odd_cols = dict(qc=0, kc=nh, vc=2 * nh, qd=3 * nh, fd=4 * nh, id=5 * nh, gd=6 * nh)
    bias_tiles = _bias_tiles(rel_bias, tq)

    h = x.reshape(t, d)
    for l in range(depth):
        if l % 2 == 0:
            e = l // 2
            w_even, cols = _even_layout(w_in_even[e])
            vb0 = cols["vb"] * HEAD_DIM
            w_vt = w_even[:, vb0:vb0 + nh * HEAD_DIM].T
            p32, p16, vt = _norm_matmul(h, norm_g[l, 0], w_even, tm=512, tn=w_even.shape[1] // 3, w_t=w_vt)
            p32 = p32.reshape(bsz, s, -1)
            p16 = p16.reshape(bsz, s, -1)
            o_1 = _deltanet(p32, conv_w_even[e], a_log_even[e], dt_bias_even[e], a_norm_even[e],
                            ts=min(512, s), cols=cols)
            o_2 = _dsa(p32, p16, vt, bias_tiles, tq=tq, cols=cols)
            w_out = w_out_even[e]
        else:
            o = l // 2
            p32, p16 = _norm_matmul(h, norm_g[l, 0], w_in_odd[o].astype(BF16), tm=512, tn=512)
            p32 = p32.reshape(bsz, s, -1)
            p16 = p16.reshape(bsz, s, -1)
            o_1 = _stickbreak(p16, tq=tq, cols=odd_cols)
            o_2 = _hgrn2(p32, lb_all[l], d_norm_odd[o], ts=min(512, s), cols=odd_cols)
            w_out = w_out_odd[o]
        h = _outproj(o_1.reshape(t, -1), o_2.reshape(t, -1), w_out, h, norm_g[l, 1], tm=512)
        h = _ffn(h, norm_g[l, 2], norm_g[l, 3], w_gate[l], w_up[l], w_down[l], tm=1024, tf=256)
    return h.reshape(bsz, s, d)
```

```python
import functools
import math

import jax
import jax.numpy as jnp
from jax import lax
from jax.experimental import pallas as pl
from jax.experimental.pallas import tpu as pltpu

F32 = jnp.float32
BF16 = jnp.bfloat16
HIGHEST = lax.Precision.HIGHEST

CHUNK = 64
HEAD_DIM = 128
N_HEADS = 4
IDX_HEADS = 8
IDX_DIM = 64
TOPK_MAX = 256
CONV_WIDTH = 4
REL_BUCKETS = 32
REL_MAX_DIST = 128
EPS = 1e-6
NEG_BIG = -1e30
LOG2E = 1.4426950408889634
BISECT_COARSE = 12
BISECT_FIXED = 8
BISECT_EXTRA = 6
F32_LOWEST = -3.4028234663852886e38
EXP_ZERO_BELOW = -104.0
VMEM_LIMIT = 56 * 1024 * 1024


def _mm(a, b):
    return jnp.dot(a.astype(BF16), b.astype(BF16), preferred_element_type=F32)


def _mm_nt(a, b):
    return lax.dot_general(a.astype(BF16), b.astype(BF16), (((1,), (1,)), ((), ())),
                           preferred_element_type=F32)


def _mm_tn(a, b):
    return lax.dot_general(a.astype(BF16), b.astype(BF16), (((0,), (0,)), ((), ())),
                           preferred_element_type=F32)


def _mm_f32(a, b):
    return jnp.dot(a, b, precision=HIGHEST, preferred_element_type=F32)


def _split(x):
    hi = x.astype(BF16)
    return hi, (x - hi.astype(F32)).astype(BF16)


def _mm_x3(a, b):
    a_hi, a_lo = _split(a)
    b_hi, b_lo = _split(b)
    return jnp.dot(jnp.concatenate([a_hi, a_hi, a_lo], axis=1),
                   jnp.concatenate([b_hi, b_lo, b_hi], axis=0), preferred_element_type=F32)


def _floor_bf16(x):
    bits = pltpu.bitcast(x, jnp.int32)
    down = jnp.where(bits >= 0, bits, bits + 0xFFFF) & jnp.int32(-65536)
    return pltpu.bitcast(down, F32).astype(BF16)


def _sigmoid(x):
    return 1.0 / (1.0 + jnp.exp(-x))


def _silu(x):
    return x * _sigmoid(x)


def _softplus(x):
    return jnp.maximum(x, 0.0) + jnp.log1p(jnp.exp(-jnp.abs(x)))


def _rms(x, g):
    return x * lax.rsqrt(jnp.mean(x * x, axis=-1, keepdims=True) + EPS) * g


def _iota(shape, dim):
    return lax.broadcasted_iota(jnp.int32, shape, dim)


def _ind(mask):
    return jnp.where(mask, 1.0, 0.0)


def _norm_matmul_kernel(x_ref, g_ref, w_ref, *rest, n_t, tiles32):
    if n_t:
        wt_ref, o32_ref, o16_ref, ot_ref, xn_ref = rest
    else:
        o32_ref, o16_ref, xn_ref = rest
    j = pl.program_id(1)

    @pl.when(j == 0)
    def _():
        xn_ref[...] = _rms(x_ref[...], g_ref[...]).astype(BF16)
        if n_t:
            ot_ref[...] = lax.dot_general(wt_ref[...], xn_ref[...], (((1,), (1,)), ((), ())),
                                          preferred_element_type=F32).astype(BF16)

    y = jnp.dot(xn_ref[...], w_ref[...], preferred_element_type=F32)

    @pl.when(j < tiles32)
    def _():
        o32_ref[...] = y

    @pl.when(j >= tiles32)
    def _():
        o16_ref[...] = y.astype(BF16)


def _norm_matmul(x, g, w, *, tm, tn, n32, w_t=None):
    t, d = x.shape
    n = w.shape[1]
    n_t = 0 if w_t is None else w_t.shape[0]
    tiles32 = n32 // tn
    assert tiles32 * tn == n32 and (n - n32) % tn == 0 and 0 < n32 < n
    in_specs = [pl.BlockSpec((tm, d), lambda i, j: (i, 0)),
                pl.BlockSpec((1, d), lambda i, j: (0, 0)),
                pl.BlockSpec((d, tn), lambda i, j: (0, j))]
    out_specs = [pl.BlockSpec((tm, tn), lambda i, j: (i, jnp.minimum(j, tiles32 - 1))),
                 pl.BlockSpec((tm, tn), lambda i, j: (i, jnp.maximum(j - tiles32, 0)))]
    out_shape = [jax.ShapeDtypeStruct((t, n32), F32), jax.ShapeDtypeStruct((t, n - n32), BF16)]
    args = [x, g.reshape(1, d), w]
    if n_t:
        in_specs.append(pl.BlockSpec((n_t, d), lambda i, j: (0, 0)))
        out_specs.append(pl.BlockSpec((n_t, tm), lambda i, j: (0, i)))
        out_shape.append(jax.ShapeDtypeStruct((n_t, t), BF16))
        args.append(w_t)
    return pl.pallas_call(
        functools.partial(_norm_matmul_kernel, n_t=n_t, tiles32=tiles32),
        grid=(t // tm, n // tn),
        in_specs=in_specs,
        out_specs=out_specs,
        out_shape=out_shape,
        scratch_shapes=[pltpu.VMEM((tm, d), BF16)],
        compiler_params=pltpu.CompilerParams(
            dimension_semantics=("parallel", "arbitrary"), vmem_limit_bytes=VMEM_LIMIT),
        name="norm_matmul",
    )(*args)


def _outproj_kernel(ca_ref, cb_ref, wa_ref, wb_ref, h_ref, g_ref, o_ref):
    y = (jnp.dot(ca_ref[...].astype(BF16), wa_ref[...], preferred_element_type=F32)
         + jnp.dot(cb_ref[...].astype(BF16), wb_ref[...], preferred_element_type=F32))
    o_ref[...] = h_ref[...] + _rms(y, g_ref[...])


def _outproj(ca, cb, w, h, g, *, tm):
    t, d = h.shape
    wa_n = ca.shape[1]
    wb_n = cb.shape[1]
    wa = w[:wa_n].astype(BF16)
    wb = w[wa_n:].astype(BF16)
    return pl.pallas_call(
        _outproj_kernel,
        grid=(t // tm,),
        in_specs=[pl.BlockSpec((tm, wa_n), lambda i: (i, 0)),
                  pl.BlockSpec((tm, wb_n), lambda i: (i, 0)),
                  pl.BlockSpec((wa_n, d), lambda i: (0, 0)),
                  pl.BlockSpec((wb_n, d), lambda i: (0, 0)),
                  pl.BlockSpec((tm, d), lambda i: (i, 0)),
                  pl.BlockSpec((1, d), lambda i: (0, 0))],
        out_specs=pl.BlockSpec((tm, d), lambda i: (i, 0)),
        out_shape=jax.ShapeDtypeStruct((t, d), F32),
        compiler_params=pltpu.CompilerParams(
            dimension_semantics=("parallel",), vmem_limit_bytes=VMEM_LIMIT),
        name="outproj",
    )(ca, cb, wa, wb, h, g.reshape(1, d))


def _ffn_kernel(h_ref, gpre_ref, gpost_ref, wg_ref, wu_ref, wd_ref, o_ref, xn_ref, acc_ref):
    f = pl.program_id(1)

    @pl.when(f == 0)
    def _():
        xn_ref[...] = _rms(h_ref[...], gpre_ref[...]).astype(BF16)
        acc_ref[...] = jnp.zeros_like(acc_ref)

    xn = xn_ref[...]
    gate = jnp.dot(xn, wg_ref[...], preferred_element_type=F32)
    up = jnp.dot(xn, wu_ref[...], preferred_element_type=F32)
    act = (_silu(gate) * up).astype(BF16)
    acc_ref[...] += jnp.dot(act, wd_ref[...], preferred_element_type=F32)

    @pl.when(f == pl.num_programs(1) - 1)
    def _():
        o_ref[...] = h_ref[...] + _rms(acc_ref[...], gpost_ref[...])


def _ffn(h, g_pre, g_post, wg, wu, wd, *, tm, tf):
    t, d = h.shape
    ff = wg.shape[1]
    return pl.pallas_call(
        _ffn_kernel,
        grid=(t // tm, ff // tf),
        in_specs=[pl.BlockSpec((tm, d), lambda i, f: (i, 0)),
                  pl.BlockSpec((1, d), lambda i, f: (0, 0)),
                  pl.BlockSpec((1, d), lambda i, f: (0, 0)),
                  pl.BlockSpec((d, tf), lambda i, f: (0, f)),
                  pl.BlockSpec((d, tf), lambda i, f: (0, f)),
                  pl.BlockSpec((tf, d), lambda i, f: (f, 0))],
        out_specs=pl.BlockSpec((tm, d), lambda i, f: (i, 0)),
        out_shape=jax.ShapeDtypeStruct((t, d), F32),
        scratch_shapes=[pltpu.VMEM((tm, d), BF16), pltpu.VMEM((tm, d), F32)],
        compiler_params=pltpu.CompilerParams(
            dimension_semantics=("parallel", "arbitrary"), vmem_limit_bytes=VMEM_LIMIT),
        name="ffn",
    )(h, g_pre.reshape(1, d), g_post.reshape(1, d),
      wg.astype(BF16), wu.astype(BF16), wd.astype(BF16))


def _deltanet_kernel(xq_ref, xk_ref, xv_ref, z_ref, sm_ref, cwq_ref, cwk_ref, cwv_ref,
                     alog_ref, dtb_ref, gn_ref, o_ref,
                     xpad_ref, q_ref, k_ref, v_ref, gb_ref, bb_ref, u_ref, w_ref, qk_ref, st_ref,
                     *, ts, a_col, b_col):
    s = pl.program_id(1)
    c = CHUNK
    d = HEAD_DIM
    nh = N_HEADS

    @pl.when(s == 0)
    def _():
        xpad_ref[:, 0:8, :] = jnp.zeros((3, 8, nh * d), F32)
        st_ref[...] = jnp.zeros_like(st_ref)

    @pl.when(s != 0)
    def _():
        xpad_ref[:, 0:8, :] = xpad_ref[:, ts:ts + 8, :]

    xpad_ref[0, 8:ts + 8, :] = xq_ref[...]
    xpad_ref[1, 8:ts + 8, :] = xk_ref[...]
    xpad_ref[2, 8:ts + 8, :] = xv_ref[...]

    def conv_silu(idx, cw_ref, hs):
        cw = cw_ref[:, hs]
        acc = xpad_ref[idx, 8 - (CONV_WIDTH - 1):8 - (CONV_WIDTH - 1) + ts, hs] * cw[0:1, :]
        for j in range(1, CONV_WIDTH):
            off = 8 - (CONV_WIDTH - 1) + j
            acc = acc + xpad_ref[idx, off:off + ts, hs] * cw[j:j + 1, :]
        return _silu(acc)

    def l2norm(t):
        return t * lax.rsqrt(jnp.sum(t * t, axis=-1, keepdims=True) + EPS)

    row = _iota((c, c), 0)
    col = _iota((c, c), 1)
    tri = (col <= row)
    strict = (col < row)
    tri_f = tri.astype(F32)
    upper_f = (row <= col).astype(F32)
    eye = (row == col).astype(F32)
    gnorm = gn_ref[...]
    chunks = range(ts // c)
    rs = [slice(ci * c, (ci + 1) * c) for ci in chunks]
    tri2 = jnp.concatenate([tri_f, tri_f], axis=1).astype(BF16)
    ones2 = jnp.ones((c, 2 * c), BF16)

    def cum2(lhs2, x):
        hi, lo = _split(x)
        return jnp.dot(lhs2, jnp.concatenate([hi, lo], axis=0), preferred_element_type=F32)

    for hh in range(nh):
        hs = slice(hh * d, (hh + 1) * d)
        q_ref[:, hs] = l2norm(conv_silu(0, cwq_ref, hs)) * (d ** -0.5)
        k_ref[:, hs] = l2norm(conv_silu(1, cwk_ref, hs))
        v_ref[:, hs] = conv_silu(2, cwv_ref, hs)

        a_raw = sm_ref[:, a_col + hh:a_col + hh + 1]
        b_raw = sm_ref[:, b_col + hh:b_col + hh + 1]
        g = -jnp.exp(alog_ref[:, hh:hh + 1]) * _softplus(a_raw + dtb_ref[:, hh:hh + 1])
        gb_ref[:, hs] = jnp.broadcast_to(g, (ts, d))
        bb_ref[:, hs] = jnp.broadcast_to(_sigmoid(b_raw), (ts, d))

        q = [q_ref[r, hs] for r in rs]
        k = [k_ref[r, hs] for r in rs]
        beta = [bb_ref[r, hs] for r in rs]
        gb = [gb_ref[r, hs] for r in rs]
        gc = [cum2(tri2, x) for x in gb]
        gc_row = [cum2(ones2, x[:, :c] * upper_f) for x in gb]
        decay = [jnp.where(tri, jnp.exp(jnp.minimum(a[:, :c] - b, 0.0)), 0.0) for a, b in zip(gc, gc_row)]
        kk = [_mm_nt(x, x) for x in k]
        n = [-jnp.where(strict, b[:, :c] * x * dc, 0.0) for b, x, dc in zip(beta, kk, decay)]
        inv = [eye + x for x in n]
        for _ in range(5):
            n = [_mm_x3(x, x) for x in n]
            inv = [iv + _mm_x3(iv, x) for iv, x in zip(inv, n)]
        egc = [jnp.exp(x) for x in gc]
        gl = [x[c - 1:c, :] for x in gc]
        for ci in chunks:
            r = rs[ci]
            u_ref[r, hs] = _mm_x3(inv[ci], v_ref[r, hs] * beta[ci])
            w_ref[r, hs] = _mm_x3(inv[ci], k[ci] * (beta[ci] * egc[ci]))
            qk_ref[hh, r, :] = _mm_nt(q[ci], k[ci]) * decay[ci]
            q_ref[r, hs] = q[ci] * egc[ci]
            k_ref[r, hs] = k[ci] * jnp.exp(gl[ci] - gc[ci])
            gb_ref[r, hs] = jnp.broadcast_to(jnp.exp(gl[ci]), (c, d))

    def chunk_body(ci, carry):
        r0 = pl.multiple_of(ci * c, c)
        for hh in range(nh):
            hs = slice(hh * d, (hh + 1) * d)
            st = st_ref[hh]
            v_new = u_ref[pl.ds(r0, c), hs] - _mm(w_ref[pl.ds(r0, c), hs], st)
            o = _mm(q_ref[pl.ds(r0, c), hs], st) + _mm(qk_ref[hh, pl.ds(r0, c), :], v_new)
            st_ref[hh] = st * gb_ref[pl.ds(r0, 1), hs] + _mm_tn(k_ref[pl.ds(r0, c), hs], v_new)
            o_ref[pl.ds(r0, c), hs] = _rms(o, gnorm) * _silu(z_ref[pl.ds(r0, c), hs])
        return carry

    lax.fori_loop(0, ts // c, chunk_body, 0)


def _deltanet(p32, conv_w, a_log, dt_bias, a_norm_g, *, ts, cols):
    bsz, s, _ = p32.shape
    d = HEAD_DIM
    nh = N_HEADS
    w = nh * d
    pad = lambda t: jnp.pad(t.astype(F32), (0, d - t.shape[0])).reshape(1, d)
    kernel = functools.partial(_deltanet_kernel, ts=ts, a_col=cols["a_lane"], b_col=cols["b_lane"])
    tile = lambda name: pl.BlockSpec((None, ts, w), lambda b, i: (b, i, cols[name] // nh))
    conv = lambda k: pl.BlockSpec((CONV_WIDTH, w), lambda b, i: (0, k))
    row = pl.BlockSpec((1, d), lambda b, i: (0, 0))
    return pl.pallas_call(
        kernel,
        grid=(bsz, s // ts),
        in_specs=[tile("qa"), tile("ka"), tile("va"), tile("za"),
                  pl.BlockSpec((None, ts, d), lambda b, i: (b, i, cols["small"])),
                  conv(0), conv(1), conv(2), row, row, row],
        out_specs=pl.BlockSpec((None, ts, w), lambda b, i: (b, i, 0)),
        out_shape=jax.ShapeDtypeStruct((bsz, s, w), F32),
        scratch_shapes=[pltpu.VMEM((3, ts + 8, w), F32)]
        + [pltpu.VMEM((ts, w), F32) for _ in range(7)]
        + [pltpu.VMEM((nh, ts, CHUNK), F32), pltpu.VMEM((nh, d, d), F32)],
        compiler_params=pltpu.CompilerParams(
            dimension_semantics=("parallel", "arbitrary"), vmem_limit_bytes=VMEM_LIMIT),
        name="deltanet",
    )(p32, p32, p32, p32, p32, conv_w.astype(F32), conv_w.astype(F32), conv_w.astype(F32),
      pad(a_log), pad(dt_bias), a_norm_g.astype(F32).reshape(1, d))


def _hgrn2_kernel(q_ref, f_ref, i_ref, gate_ref, lb_ref, gn_ref, o_ref,
                  qs_ref, ks_ref, gc_ref, st_ref, *, ts):
    s = pl.program_id(1)
    c = CHUNK
    d = HEAD_DIM
    nh = N_HEADS
    SUB = 16

    @pl.when(s == 0)
    def _():
        st_ref[...] = jnp.zeros_like(st_ref)

    lb = lb_ref[...]
    f_raw = f_ref[...]
    log_sig = jnp.minimum(f_raw, 0.0) - jnp.log1p(jnp.exp(-jnp.abs(f_raw)))
    la = jnp.log(lb)
    lbb = jnp.log1p(-lb) + log_sig
    log_f = jnp.maximum(la, lbb) + jnp.log1p(jnp.exp(-jnp.abs(la - lbb)))
    qs_ref[...] = _silu(q_ref[...])
    ks_ref[...] = (1.0 - lb) * _sigmoid(-f_raw)

    row = _iota((c, c), 0)
    col = _iota((c, c), 1)
    tri_f = (col <= row).astype(F32)
    ones_dd = jnp.ones((d, d), BF16)
    rows_8d = _iota((8, d), 0)
    gnorm = gn_ref[...]

    for ci in range(ts // c):
        gc_ref[ci * c:(ci + 1) * c, :] = _mm_f32(tri_f, log_f[ci * c:(ci + 1) * c, :])

    blocks = [(sb * SUB, (sb + 1) * SUB) for sb in range(c // SUB)]

    def head_chunk(r0, hh):
        hs = slice(hh * d, (hh + 1) * d)
        q = qs_ref[pl.ds(r0, c), hs]
        k = ks_ref[pl.ds(r0, c), hs]
        v = i_ref[pl.ds(r0, c), hs]
        gc = gc_ref[pl.ds(r0, c), hs]

        prods = []
        for top, end in blocks:
            for j in range(top, end):
                lo = (j // 8) * 8
                k_j = k[j:j + 1, :]
                g_j = gc[j:j + 1, :]
                e = jnp.exp(jnp.minimum(gc[lo:end, :] - g_j, 0.0))
                if j % 8:
                    head = jnp.where(rows_8d >= j - lo, e[:8], 0.0)
                    e = jnp.concatenate([head, e[8:]], axis=0) if lo + 8 < end else head
                prods.append(q[lo:end, :] * k_j * e)
        sums = jnp.dot(jnp.concatenate(prods, axis=0).astype(BF16), ones_dd,
                       preferred_element_type=F32)
        qk_far = []
        for top, end in blocks[1:]:
            g_b = gc[top - 1:top, :]
            qe = q[top:end, :] * jnp.exp(gc[top:end, :] - g_b)
            ke = k[:top, :] * jnp.exp(jnp.minimum(g_b - gc[:top, :], 0.0))
            qk_far.append(_mm_nt(qe, ke))
        far = [_mm(a, v[:top, :]) for a, (top, _) in zip(qk_far, blocks[1:])]

        groups = [jnp.zeros((8, d), F32) for _ in range(c // 8)]
        at = 0
        for top, end in blocks:
            for j in range(top, end):
                v_j = v[j:j + 1, :]
                for g in range(j // 8, end // 8):
                    groups[g] = groups[g] + sums[at:at + 8, :] * v_j
                    at += 8
        for f, (top, end) in zip(far, blocks[1:]):
            for g in range(top // 8, end // 8):
                groups[g] = groups[g] + f[(g * 8 - top):(g * 8 - top + 8), :]
        o_intra = jnp.concatenate(groups, axis=0)

        st = st_ref[hh]
        gl = gc[c - 1:c, :]
        o = o_intra + _mm_nt(q * jnp.exp(gc), st)
        st_ref[hh] = st * jnp.exp(gl) + _mm_tn(v, k * jnp.exp(gl - gc))
        o_ref[pl.ds(r0, c), hs] = _rms(o, gnorm) * _silu(gate_ref[pl.ds(r0, c), hs])

    def chunk_loop(ci, carry):
        r0 = pl.multiple_of(ci * c, c)
        for hh in range(nh):
            head_chunk(r0, hh)
        return carry

    lax.fori_loop(0, ts // c, chunk_loop, 0)


def _hgrn2(p32, lb, d_norm_g, *, ts, cols):
    bsz, s, _ = p32.shape
    d = HEAD_DIM
    nh = N_HEADS
    w = nh * d
    kernel = functools.partial(_hgrn2_kernel, ts=ts)
    tile = lambda name: pl.BlockSpec((None, ts, w), lambda b, i: (b, i, cols[name] // nh))
    return pl.pallas_call(
        kernel,
        grid=(bsz, s // ts),
        in_specs=[tile("qd"), tile("fd"), tile("id"), tile("gd"),
                  pl.BlockSpec((1, w), lambda b, i: (0, 0)),
                  pl.BlockSpec((1, d), lambda b, i: (0, 0))],
        out_specs=pl.BlockSpec((None, ts, w), lambda b, i: (b, i, 0)),
        out_shape=jax.ShapeDtypeStruct((bsz, s, w), F32),
        scratch_shapes=[pltpu.VMEM((ts, w), F32), pltpu.VMEM((ts, w), F32),
                        pltpu.VMEM((ts, w), F32), pltpu.VMEM((nh, d, d), F32)],
        compiler_params=pltpu.CompilerParams(
            dimension_semantics=("parallel", "arbitrary"), vmem_limit_bytes=VMEM_LIMIT),
        name="hgrn2",
    )(p32, p32, p32, p32, lb.astype(F32).reshape(1, w), d_norm_g.astype(F32).reshape(1, d))


def _stickbreak_kernel(q_ref, k_ref, v_ref, o_ref, *, tq):
    i = pl.program_id(1)
    d = HEAD_DIM
    nh = N_HEADS
    row = _iota((tq, tq), 0)
    col = _iota((tq, tq), 1)
    causal = col < row
    later = (row > col).astype(BF16)

    def block(j, carries, diag):
        r0 = pl.multiple_of(j * tq, tq)
        out = []
        for hh in range(nh):
            hs = slice(hh * d, (hh + 1) * d)
            z = _mm_nt(q_ref[:, hs], k_ref[pl.ds(r0, tq), hs]) * (d ** -0.5)
            sp = _softplus(z)
            l1m = jnp.where(causal, -sp, 0.0) if diag else -sp
            l_hi, l_lo = _split(l1m)
            rest = (jnp.dot(l_hi, later, preferred_element_type=F32)
                    + jnp.dot(l_lo, later, preferred_element_type=F32))
            p = jnp.exp((z - sp) + rest + carries[hh])
            if diag:
                p = jnp.where(causal, p, 0.0)
            pv = _mm(p, v_ref[pl.ds(r0, tq), hs])
            if diag:
                o_ref[:, hs] = pv
            else:
                o_ref[:, hs] += pv
            out.append(carries[hh] + jnp.sum(l1m, axis=-1, keepdims=True))
        return tuple(out)

    carries = block(i, tuple(jnp.zeros((tq, 1), F32) for _ in range(nh)), True)

    def cond(c):
        worst = functools.reduce(jnp.maximum, c[1])
        return jnp.logical_and(c[0] >= 0, jnp.max(worst) >= EXP_ZERO_BELOW)

    def body(c):
        return c[0] - 1, block(c[0], c[1], False)

    lax.while_loop(cond, body, (i - 1, carries))


def _stickbreak(p16, *, tq, cols):
    bsz, s, _ = p16.shape
    nh = N_HEADS
    w = nh * HEAD_DIM
    kernel = functools.partial(_stickbreak_kernel, tq=tq)
    resident = dict(pipeline_mode=pl.Buffered(1))
    return pl.pallas_call(
        kernel,
        grid=(bsz, s // tq),
        in_specs=[pl.BlockSpec((None, tq, w), lambda b, i: (b, i, cols["qc"] // nh)),
                  pl.BlockSpec((None, s, w), lambda b, i: (b, 0, cols["kc"] // nh), **resident),
                  pl.BlockSpec((None, s, w), lambda b, i: (b, 0, cols["vc"] // nh), **resident)],
        out_specs=pl.BlockSpec((None, tq, w), lambda b, i: (b, i, 0)),
        out_shape=jax.ShapeDtypeStruct((bsz, s, w), F32),
        compiler_params=pltpu.CompilerParams(
            dimension_semantics=("parallel", "arbitrary"), vmem_limit_bytes=VMEM_LIMIT),
        name="stickbreak",
    )(p16, p16, p16)


def _dsa_kernel(qi_ref, smq_ref, q_ref, sm_ref, k_ref, vt_ref, bias_ref, o_ref,
                sc_ref, scb_ref, wb_ref, qc_ref, kct_ref, bd_ref, lg_ref, *, tq, k_sel, wi_lane, wide):
    i = pl.program_id(1)
    tk = tq
    d = HEAD_DIM
    nh = N_HEADS
    ksel = float(k_sel)
    per_wide = wide // tk
    n_wide = (i + per_wide) // per_wide
    sub = 2 * tk
    lane_q = _iota((1, tq), 1)

    def tree(parts, op):
        while len(parts) > 1:
            parts = [op(parts[j], parts[j + 1]) if j + 1 < len(parts) else parts[j]
                     for j in range(0, len(parts), 2)]
        return parts[0]

    def col_fold(x, op=jnp.add, rows=8):
        return tree([x[r * rows:(r + 1) * rows] for r in range(x.shape[0] // rows)], op)

    @pl.when(i == 0)
    def _():
        def prep(g, carry):
            g0 = pl.multiple_of(g * wide, wide)
            kt = sm_ref[pl.ds(g0, wide), :].T[:IDX_DIM, :]
            hi, lo = _split(kt)
            kct_ref[:, pl.ds(g0, wide)] = jnp.concatenate([hi, lo, hi], axis=0)
            return carry
        lax.fori_loop(0, sm_ref.shape[0] // wide, prep, 0)

    smq = smq_ref[...]
    lane = _iota(smq.shape, 1)
    for hh in range(IDX_HEADS):
        qh = qi_ref[:, hh * IDX_DIM:(hh + 1) * IDX_DIM]
        hi, lo = _split(qh)
        qc_ref[hh] = jnp.concatenate([hi, hi, lo], axis=-1)
        w = jnp.sum(jnp.where(lane == wi_lane + hh, smq, 0.0), axis=-1, keepdims=True)
        wb_ref[hh] = jnp.broadcast_to(w * ((IDX_HEADS ** -0.5) * (IDX_DIM ** -0.5)), (tq, tk))

    q2t = (q_ref[...] * ((d ** -0.5) * LOG2E)).T.astype(BF16)
    zero_dq = jnp.zeros((d, tq), BF16)
    for p in range(nh // 2):
        top = jnp.concatenate([q2t[2 * p * d:(2 * p + 1) * d], zero_dq], axis=1)
        bot = jnp.concatenate([zero_dq, q2t[(2 * p + 1) * d:(2 * p + 2) * d]], axis=1)
        bd_ref[p] = jnp.concatenate([top, bot], axis=0)

    limit = i * tq + (lane_q // CHUNK + 1) * CHUNK
    rows_t = _iota((tk, tq), 0)

    def score_group(g, mm, masked):
        mn, mx = mm
        for sb in range(wide // sub):
            k0 = pl.multiple_of(g * wide + sb * sub, sub)
            kct = kct_ref[:, pl.ds(k0, sub)]
            tiles = [jnp.zeros((tq, tk), F32) for _ in range(sub // tk)]
            for hh in range(IDX_HEADS):
                s_h = jnp.dot(qc_ref[hh], kct, preferred_element_type=F32)
                for ti in range(sub // tk):
                    tiles[ti] = tiles[ti] + jnp.maximum(s_h[:, ti * tk:(ti + 1) * tk], 0.0) * wb_ref[hh]
            for ti in range(sub // tk):
                kb = pl.multiple_of(k0 + ti * tk, tk)
                sct = tiles[ti].T
                if masked:
                    adm = (kb + rows_t) < limit
                    mn = jnp.minimum(mn, col_fold(jnp.where(adm, sct, jnp.inf), jnp.minimum))
                    sct = jnp.where(adm, sct, -jnp.inf)
                else:
                    mn = jnp.minimum(mn, col_fold(sct, jnp.minimum))
                mx = jnp.maximum(mx, col_fold(sct, jnp.maximum))
                sc_ref[pl.ds(kb, tk), :] = sct
                scb_ref[pl.ds(kb, tk), :] = _floor_bf16(sct)
        return mn, mx

    def score_pair(j, mm):
        return score_group(2 * j + 1, score_group(2 * j, mm, False), False)

    n_full = n_wide - 1
    mm = lax.fori_loop(0, n_full // 2, score_pair,
                       (jnp.full((8, tq), jnp.inf, F32), jnp.full((8, tq), -jnp.inf, F32)))
    mm = lax.cond(n_full % 2 == 1, lambda c: score_group(n_full - 1, c, False), lambda c: c, mm)
    mn, mx = score_group(n_wide - 1, mm, True)

    n_pairs = (n_wide + 1) // 2

    @pl.when(n_wide % 2 == 1)
    def _():
        sc_ref[pl.ds(pl.multiple_of(n_wide * wide, wide), wide), :] = jnp.full((wide, tq), -jnp.inf, F32)
    rmin = jnp.min(mn, axis=0, keepdims=True)
    rmax = jnp.max(mx, axis=0, keepdims=True)

    def count(pred):
        def body(g, acc):
            blk = sc_ref[pl.ds(pl.multiple_of(g * wide, wide), wide), :]
            return acc + col_fold(pred(blk))
        return jnp.sum(lax.fori_loop(0, n_wide, body, jnp.zeros((8, tq), F32)), axis=0, keepdims=True)

    def max_below(x):
        def body(g, acc):
            blk = sc_ref[pl.ds(pl.multiple_of(g * wide, wide), wide), :]
            return jnp.maximum(acc, col_fold(jnp.where(blk < x, blk, -jnp.inf), jnp.maximum))
        return jnp.max(lax.fori_loop(0, n_wide, body, jnp.full((8, tq), -jnp.inf, F32)), axis=0, keepdims=True)

    n_adm = limit.astype(F32)
    all_sel = n_adm <= ksel

    def bisect(c):
        lo, hi, c_lo = c
        mid = 0.5 * lo + 0.5 * hi
        cm = count(lambda blk: _ind(blk >= mid))
        ge = cm >= ksel
        return jnp.where(ge, mid, lo), jnp.where(ge, hi, mid), jnp.where(ge, cm, c_lo)

    def pending(c_lo, tied):
        return jnp.where(all_sel, 0.0, jnp.where(tied > 0.5, 0.0, _ind(c_lo != ksel)))

    def bisect_coarse(_, c):
        lo, hi, c_lo = c
        mid = _floor_bf16(0.5 * lo + 0.5 * hi).astype(F32)
        t_b = jnp.broadcast_to(mid, (16, tq)).astype(BF16)
        one_b = jnp.ones((16, tq), BF16)
        zero_b = jnp.zeros((16, tq), BF16)

        def body(g, acc):
            blk = scb_ref[pl.ds(pl.multiple_of(g * wide, wide), wide), :]
            ind = [jnp.where(blk[r * 16:(r + 1) * 16] >= t_b, one_b, zero_b) for r in range(wide // 16)]
            return acc + tree(ind, jnp.add).astype(F32)

        acc = lax.fori_loop(0, n_wide, body, jnp.zeros((16, tq), F32))
        cm = jnp.sum(acc, axis=0, keepdims=True)
        ge = cm >= ksel
        return jnp.where(ge, mid, lo), jnp.where(ge, hi, mid), jnp.where(ge, cm, c_lo)

    lo0 = _floor_bf16(rmin).astype(F32)
    hi0 = _floor_bf16(rmax + (jnp.abs(rmax) * (2.0 ** -6) + 1e-30)).astype(F32)
    state = lax.fori_loop(0, BISECT_COARSE, bisect_coarse, (lo0, hi0, n_adm))
    state = lax.fori_loop(0, BISECT_FIXED, lambda _, c: bisect(c), state)

    def round_cond(c):
        return jnp.max(pending(c[0][2], c[1])) > 0.5

    def round_body(c):
        st, tied, v, need = c

        def more_cond(s):
            return jnp.logical_and(s[0] < BISECT_EXTRA, jnp.max(pending(s[1][2], tied)) > 0.5)

        _, st = lax.while_loop(more_cond, lambda s: (s[0] + 1, bisect(s[1])), (jnp.int32(0), st))
        pend = pending(st[2], tied)

        def check(_):
            cand = max_below(st[1])
            c_ge = count(lambda blk: _ind(blk >= cand))
            c_gt = count(lambda blk: _ind(blk > cand))
            ok = jnp.where(pend > 0.5, _ind(c_ge >= ksel), 0.0)
            return (jnp.where(ok > 0.5, 1.0, tied), jnp.where(ok > 0.5, cand, v),
                    jnp.where(ok > 0.5, ksel - c_gt, need))

        tied, v, need = lax.cond(jnp.max(pend) > 0.5, check, lambda _: (tied, v, need), 0)
        return st, tied, v, need

    zeros1 = jnp.zeros((1, tq), F32)
    (lo_f, _, _), tied, v_tie, need = lax.while_loop(round_cond, round_body, (state, zeros1, zeros1, zeros1))
    vth = jnp.where(all_sel, F32_LOWEST, jnp.where(tied > 0.5, v_tie, lo_f))

    @pl.when(jnp.max(tied) > 0.5)
    def _():
        v_eq = jnp.where(tied > 0.5, v_tie, jnp.inf)
        incl = (_iota((tk, tk), 1) <= _iota((tk, tk), 0)).astype(BF16)

        def demote(g, seen):
            g0 = pl.multiple_of(g * wide, wide)
            xs = [sc_ref[pl.ds(g0 + pb * tk, tk), :] for pb in range(per_wide)]
            eqs = [_ind(x == v_eq) for x in xs]
            inblk = [jnp.dot(incl, e.astype(BF16), preferred_element_type=F32) for e in eqs]
            for pb in range(per_wide):
                rank = inblk[pb] + seen
                sc_ref[pl.ds(g0 + pb * tk, tk), :] = jnp.where(eqs[pb] * _ind(rank > need) > 0.5,
                                                               -jnp.inf, xs[pb])
                seen = seen + jnp.sum(col_fold(eqs[pb]), axis=0, keepdims=True)
            return seen

        lax.fori_loop(0, n_wide, demote, zeros1)

    g_near = jnp.maximum(i - 1, 0) // per_wide

    def logit_group(g, mx, near):
        out = list(mx)
        for sb in range(wide // sub):
            k0 = pl.multiple_of(g * wide + sb * sub, sub)
            sel = sc_ref[pl.ds(k0, sub), :] >= vth
            for p in range(nh // 2):
                pair = jnp.dot(k_ref[pl.ds(k0, sub), 2 * p * d:(2 * p + 2) * d], bd_ref[p],
                               preferred_element_type=F32)
                for hh in (2 * p, 2 * p + 1):
                    lm = pair[:, (hh - 2 * p) * tq:(hh - 2 * p + 1) * tq]
                    if near:
                        back = [jnp.clip(i - (g * per_wide + sb * (sub // tk) + pb), 0, 2)
                                for pb in range(sub // tk)]
                        lm = lm + jnp.concatenate([bias_ref[bk, hh] for bk in back], axis=0)
                    lm = jnp.where(sel, lm, NEG_BIG)
                    lg_ref[hh, pl.ds(k0, sub), :] = lm
                    out[hh] = jnp.maximum(out[hh], col_fold(lm, jnp.maximum))
        return tuple(out)

    mx = tuple(jnp.full((8, tq), NEG_BIG, F32) for _ in range(nh))
    def logit_pair(j, mx, near):
        return logit_group(2 * j + 1, logit_group(2 * j, mx, near), near)

    far_pairs = g_near // 2
    mx = lax.fori_loop(0, far_pairs, functools.partial(logit_pair, near=False), mx)
    mx = lax.fori_loop(far_pairs, n_pairs, functools.partial(logit_pair, near=True), mx)
    m_q = [jnp.max(mx[hh], axis=0, keepdims=True) for hh in range(nh)]

    def pv_body(g, carry):
        g0 = pl.multiple_of(g * wide, wide)
        ls, accs = carry
        new_l, new_a = [], []
        for hh in range(nh):
            p = jnp.exp2(lg_ref[hh, pl.ds(g0, wide), :] - m_q[hh])
            new_l.append(ls[hh] + col_fold(p))
            new_a.append(accs[hh] + jnp.dot(vt_ref[hh * d:(hh + 1) * d, pl.ds(g0, wide)], p.astype(BF16),
                                            preferred_element_type=F32))
        return tuple(new_l), tuple(new_a)

    ls, accs = lax.fori_loop(0, n_pairs, lambda j, c: pv_body(2 * j + 1, pv_body(2 * j, c)),
                             (tuple(jnp.zeros((8, tq), F32) for _ in range(nh)),
                              tuple(jnp.zeros((d, tq), F32) for _ in range(nh))))
    for hh in range(nh):
        o_ref[:, hh * d:(hh + 1) * d] = (accs[hh] / jnp.sum(ls[hh], axis=0, keepdims=True)).T


def _dsa(p32, p16, vt, bias_tiles, *, tq, cols):
    bsz, s, _ = p32.shape
    d = HEAD_DIM
    nh = N_HEADS
    wide = 4 * tq
    k_sel = min(TOPK_MAX, s // 4)
    w512 = nh * d
    kernel = functools.partial(_dsa_kernel, tq=tq, k_sel=k_sel, wi_lane=cols["wi_lane"], wide=wide)
    resident = dict(pipeline_mode=pl.Buffered(1))
    return pl.pallas_call(
        kernel,
        grid=(bsz, s // tq),
        in_specs=[pl.BlockSpec((None, tq, w512), lambda b, i: (b, i, cols["qi"] // nh)),
                  pl.BlockSpec((None, tq, d), lambda b, i: (b, i, cols["small"])),
                  pl.BlockSpec((None, tq, w512), lambda b, i: (b, i, cols["qb"] // nh)),
                  pl.BlockSpec((None, s, d), lambda b, i: (b, 0, cols["small"]), **resident),
                  pl.BlockSpec((None, s, w512), lambda b, i: (b, 0, cols["kb"] // nh), **resident),
                  pl.BlockSpec((w512, s), lambda b, i: (0, b), **resident),
                  pl.BlockSpec((3, nh, tq, tq), lambda b, i: (0, 0, 0, 0), **resident)],
        out_specs=pl.BlockSpec((None, tq, w512), lambda b, i: (b, i, 0)),
        out_shape=jax.ShapeDtypeStruct((bsz, s, w512), F32),
        scratch_shapes=[pltpu.VMEM((s, tq), F32),
                        pltpu.VMEM((s, tq), BF16),
                        pltpu.VMEM((IDX_HEADS, tq, tq), F32),
                        pltpu.VMEM((IDX_HEADS, tq, 3 * IDX_DIM), BF16),
                        pltpu.VMEM((3 * IDX_DIM, s), BF16),
                        pltpu.VMEM((nh // 2, 2 * d, 2 * tq), BF16),
                        pltpu.VMEM((nh, s, tq), F32)],
        compiler_params=pltpu.CompilerParams(
            dimension_semantics=("parallel", "arbitrary"), vmem_limit_bytes=VMEM_LIMIT),
        name="dsa",
    )(p32, p32, p32, p32, p16, vt, bias_tiles)


def _t5_bucket(rel):
    nb = REL_BUCKETS // 2
    max_exact = nb // 2
    ret = jnp.where(rel > 0, nb, 0)
    n = jnp.abs(rel)
    large = max_exact + (jnp.log(jnp.maximum(n, 1).astype(F32) / max_exact)
                         / math.log(REL_MAX_DIST / max_exact) * (nb - max_exact)).astype(jnp.int32)
    large = jnp.minimum(large, nb - 1)
    return ret + jnp.where(n < max_exact, n, large)


def _bias_tiles(rel_table, tq):
    assert tq >= REL_MAX_DIST
    t = jnp.arange(tq)
    back = jnp.arange(3)
    rel = (t[None, None, :] - back[:, None, None] * tq) - t[None, :, None]
    onehot = (_t5_bucket(rel)[..., None] == jnp.arange(REL_BUCKETS)).astype(F32)
    tiles = jnp.einsum("bqkn,nh->bhkq", onehot, rel_table.astype(F32),
                       precision=HIGHEST)
    return (tiles - tiles[2:3]) * LOG2E


def _even_layout(w_in):
    d = HEAD_DIM
    a_w = 2 * N_HEADS * d + N_HEADS * d
    offs = {}
    o = 0
    for name, w in (("qkv", a_w), ("z", N_HEADS * d), ("a", N_HEADS), ("b", N_HEADS),
                    ("qb", N_HEADS * d), ("kb", N_HEADS * d), ("vb", N_HEADS * d),
                    ("qi", IDX_HEADS * IDX_DIM), ("ki", IDX_DIM), ("wi", IDX_HEADS)):
        offs[name] = (o, o + w)
        o += w
    assert o == w_in.shape[1]
    sl = lambda n: w_in[:, offs[n][0]:offs[n][1]]
    small_w = IDX_DIM + 2 * N_HEADS + IDX_HEADS
    small_pad = -small_w % d
    zeros = lambda n: jnp.zeros((w_in.shape[0], n), w_in.dtype)
    w32 = jnp.concatenate([sl("qkv"), sl("z"), sl("qb"), sl("qi"),
                           sl("ki"), sl("a"), sl("b"), sl("wi"), zeros(small_pad)], axis=1)
    n32 = w32.shape[1]
    tn = n32 // 5
    assert tn * 5 == n32 and tn % d == 0
    w16 = jnp.concatenate([sl("kb"), zeros(tn - N_HEADS * d)], axis=1)
    nh = N_HEADS
    cols = dict(qa=0, ka=nh, va=2 * nh, za=3 * nh, qb=4 * nh, qi=5 * nh, small=6 * nh, kb=0,
                a_lane=IDX_DIM, b_lane=IDX_DIM + nh, wi_lane=IDX_DIM + 2 * nh, n32=n32, tn=tn)
    return jnp.concatenate([w32, w16], axis=1).astype(BF16), sl("vb").T.astype(BF16), cols


def kernel(x, norm_g, w_in_even, conv_w_even, a_log_even, dt_bias_even, a_norm_even, w_out_even,
           rel_bias, w_in_odd, lb_logits, d_norm_odd, w_out_odd, w_gate, w_up, w_down):
    bsz, s, d = x.shape
    t = bsz * s
    depth = norm_g.shape[0]
    nh = N_HEADS
    tq = 128
    lb_all = jnp.cumsum(jax.nn.softmax(lb_logits.astype(F32), axis=0), axis=0)
    lb_all = lb_all - lb_all[:1]
    odd_cols = dict(qc=0, kc=nh, vc=2 * nh, qd=0, fd=nh, id=2 * nh, gd=3 * nh)
    bias_tiles = _bias_tiles(rel_bias, tq)

    h = x.reshape(t, d)
    for l in range(depth):
        if l % 2 == 0:
            e = l // 2
            w_even, w_vt, cols = _even_layout(w_in_even[e])
            p32, p16, vt = _norm_matmul(h, norm_g[l, 0], w_even, tm=1024, tn=cols["tn"], n32=cols["n32"],
                                        w_t=w_vt)
            p32 = p32.reshape(bsz, s, -1)
            p16 = p16.reshape(bsz, s, -1)
            o_1 = _deltanet(p32, conv_w_even[e], a_log_even[e], dt_bias_even[e], a_norm_even[e],
                            ts=min(512, s), cols=cols)
            o_2 = _dsa(p32, p16, vt, bias_tiles, tq=tq, cols=cols)
            w_out = w_out_even[e]
        else:
            o = l // 2
            n16 = 3 * nh * HEAD_DIM
            w_odd = jnp.concatenate([w_in_odd[o][:, n16:], w_in_odd[o][:, :n16]], axis=1).astype(BF16)
            p32, p16 = _norm_matmul(h, norm_g[l, 0], w_odd, tm=1024, tn=512, n32=w_odd.shape[1] - n16)
            p32 = p32.reshape(bsz, s, -1)
            p16 = p16.reshape(bsz, s, -1)
            o_1 = _stickbreak(p16, tq=tq, cols=odd_cols)
            o_2 = _hgrn2(p32, lb_all[l], d_norm_odd[o], ts=min(512, s), cols=odd_cols)
            w_out = w_out_odd[o]
        h = _outproj(o_1.reshape(t, -1), o_2.reshape(t, -1), w_out, h, norm_g[l, 1], tm=512)
        h = _ffn(h, norm_g[l, 2], norm_g[l, 3], w_gate[l], w_up[l], w_down[l], tm=1024, tf=256)
    return h.reshape(bsz, s, d)
```

```python
import functools
import math

import jax
import jax.numpy as jnp
from jax import lax
from jax.experimental import pallas as pl
from jax.experimental.pallas import tpu as pltpu

F32 = jnp.float32
BF16 = jnp.bfloat16
HIGHEST = lax.Precision.HIGHEST

CHUNK = 64
HEAD_DIM = 128
N_HEADS = 4
IDX_HEADS = 8
IDX_DIM = 64
TOPK_MAX = 256
CONV_WIDTH = 4
REL_BUCKETS = 32
REL_MAX_DIST = 128
EPS = 1e-6
NEG_BIG = -1e30
LOG2E = 1.4426950408889634
BISECT_COARSE = 12
BISECT_FIXED = 8
BISECT_EXTRA = 6
F32_LOWEST = -3.4028234663852886e38
EXP_ZERO_BELOW = -104.0
VMEM_LIMIT = 56 * 1024 * 1024


def _mm(a, b):
    return jnp.dot(a.astype(BF16), b.astype(BF16), preferred_element_type=F32)


def _mm_nt(a, b):
    return lax.dot_general(a.astype(BF16), b.astype(BF16), (((1,), (1,)), ((), ())),
                           preferred_element_type=F32)


def _mm_tn(a, b):
    return lax.dot_general(a.astype(BF16), b.astype(BF16), (((0,), (0,)), ((), ())),
                           preferred_element_type=F32)


def _mm_f32(a, b):
    return jnp.dot(a, b, precision=HIGHEST, preferred_element_type=F32)


def _split(x):
    hi = x.astype(BF16)
    return hi, (x - hi.astype(F32)).astype(BF16)


def _x3_parts(x):
    hi, lo = _split(x)
    return jnp.concatenate([hi, hi, lo], axis=1), jnp.concatenate([hi, lo, hi], axis=0)


def _mm_x3(a, b):
    return jnp.dot(_x3_parts(a)[0], _x3_parts(b)[1], preferred_element_type=F32)


def _floor_bf16(x):
    bits = pltpu.bitcast(x, jnp.int32)
    down = jnp.where(bits >= 0, bits, bits + 0xFFFF) & jnp.int32(-65536)
    return pltpu.bitcast(down, F32).astype(BF16)


def _sigmoid(x):
    return 1.0 / (1.0 + jnp.exp(-x))


def _silu(x):
    return x * _sigmoid(x)


def _softplus(x):
    return jnp.maximum(x, 0.0) + jnp.log1p(jnp.exp(-jnp.abs(x)))


def _rms(x, g):
    return x * lax.rsqrt(jnp.mean(x * x, axis=-1, keepdims=True) + EPS) * g


def _iota(shape, dim):
    return lax.broadcasted_iota(jnp.int32, shape, dim)


def _ind(mask):
    return jnp.where(mask, 1.0, 0.0)


def _norm_matmul_kernel(x_ref, g_ref, w_ref, *rest, n_t, tiles32):
    if n_t:
        wt_ref, o32_ref, o16_ref, ot_ref, xn_ref = rest
    else:
        o32_ref, o16_ref, xn_ref = rest
    j = pl.program_id(1)

    @pl.when(j == 0)
    def _():
        xn_ref[...] = _rms(x_ref[...], g_ref[...]).astype(BF16)
        if n_t:
            ot_ref[...] = lax.dot_general(wt_ref[...], xn_ref[...], (((1,), (1,)), ((), ())),
                                          preferred_element_type=F32).astype(BF16)

    y = jnp.dot(xn_ref[...], w_ref[...], preferred_element_type=F32)

    @pl.when(j < tiles32)
    def _():
        o32_ref[...] = y

    @pl.when(j >= tiles32)
    def _():
        o16_ref[...] = y.astype(BF16)


def _norm_matmul(x, g, w, *, tm, tn, n32, w_t=None):
    t, d = x.shape
    n = w.shape[1]
    n_t = 0 if w_t is None else w_t.shape[0]
    tiles32 = n32 // tn
    assert tiles32 * tn == n32 and (n - n32) % tn == 0 and 0 < n32 < n
    in_specs = [pl.BlockSpec((tm, d), lambda i, j: (i, 0)),
                pl.BlockSpec((1, d), lambda i, j: (0, 0)),
                pl.BlockSpec((d, tn), lambda i, j: (0, j))]
    out_specs = [pl.BlockSpec((tm, tn), lambda i, j: (i, jnp.minimum(j, tiles32 - 1))),
                 pl.BlockSpec((tm, tn), lambda i, j: (i, jnp.maximum(j - tiles32, 0)))]
    out_shape = [jax.ShapeDtypeStruct((t, n32), F32), jax.ShapeDtypeStruct((t, n - n32), BF16)]
    args = [x, g.reshape(1, d), w]
    if n_t:
        in_specs.append(pl.BlockSpec((n_t, d), lambda i, j: (0, 0)))
        out_specs.append(pl.BlockSpec((n_t, tm), lambda i, j: (0, i)))
        out_shape.append(jax.ShapeDtypeStruct((n_t, t), BF16))
        args.append(w_t)
    return pl.pallas_call(
        functools.partial(_norm_matmul_kernel, n_t=n_t, tiles32=tiles32),
        grid=(t // tm, n // tn),
        in_specs=in_specs,
        out_specs=out_specs,
        out_shape=out_shape,
        scratch_shapes=[pltpu.VMEM((tm, d), BF16)],
        compiler_params=pltpu.CompilerParams(
            dimension_semantics=("parallel", "arbitrary"), vmem_limit_bytes=VMEM_LIMIT),
        name="norm_matmul",
    )(*args)


def _outproj_kernel(ca_ref, cb_ref, wa_ref, wb_ref, h_ref, g_ref, o_ref):
    y = (jnp.dot(ca_ref[...].astype(BF16), wa_ref[...], preferred_element_type=F32)
         + jnp.dot(cb_ref[...].astype(BF16), wb_ref[...], preferred_element_type=F32))
    o_ref[...] = h_ref[...] + _rms(y, g_ref[...])


def _outproj(ca, cb, w, h, g, *, tm):
    t, d = h.shape
    wa_n = ca.shape[1]
    wb_n = cb.shape[1]
    wa = w[:wa_n].astype(BF16)
    wb = w[wa_n:].astype(BF16)
    return pl.pallas_call(
        _outproj_kernel,
        grid=(t // tm,),
        in_specs=[pl.BlockSpec((tm, wa_n), lambda i: (i, 0)),
                  pl.BlockSpec((tm, wb_n), lambda i: (i, 0)),
                  pl.BlockSpec((wa_n, d), lambda i: (0, 0)),
                  pl.BlockSpec((wb_n, d), lambda i: (0, 0)),
                  pl.BlockSpec((tm, d), lambda i: (i, 0)),
                  pl.BlockSpec((1, d), lambda i: (0, 0))],
        out_specs=pl.BlockSpec((tm, d), lambda i: (i, 0)),
        out_shape=jax.ShapeDtypeStruct((t, d), F32),
        compiler_params=pltpu.CompilerParams(
            dimension_semantics=("parallel",), vmem_limit_bytes=VMEM_LIMIT),
        name="outproj",
    )(ca, cb, wa, wb, h, g.reshape(1, d))


def _ffn_kernel(h_ref, gpre_ref, gpost_ref, wg_ref, wu_ref, wd_ref, o_ref, xn_ref, acc_ref):
    f = pl.program_id(1)

    @pl.when(f == 0)
    def _():
        xn_ref[...] = _rms(h_ref[...], gpre_ref[...]).astype(BF16)
        acc_ref[...] = jnp.zeros_like(acc_ref)

    xn = xn_ref[...]
    gate = jnp.dot(xn, wg_ref[...], preferred_element_type=F32)
    up = jnp.dot(xn, wu_ref[...], preferred_element_type=F32)
    act = (_silu(gate) * up).astype(BF16)
    acc_ref[...] += jnp.dot(act, wd_ref[...], preferred_element_type=F32)

    @pl.when(f == pl.num_programs(1) - 1)
    def _():
        o_ref[...] = h_ref[...] + _rms(acc_ref[...], gpost_ref[...])


def _ffn(h, g_pre, g_post, wg, wu, wd, *, tm, tf):
    t, d = h.shape
    ff = wg.shape[1]
    return pl.pallas_call(
        _ffn_kernel,
        grid=(t // tm, ff // tf),
        in_specs=[pl.BlockSpec((tm, d), lambda i, f: (i, 0)),
                  pl.BlockSpec((1, d), lambda i, f: (0, 0)),
                  pl.BlockSpec((1, d), lambda i, f: (0, 0)),
                  pl.BlockSpec((d, tf), lambda i, f: (0, f)),
                  pl.BlockSpec((d, tf), lambda i, f: (0, f)),
                  pl.BlockSpec((tf, d), lambda i, f: (f, 0))],
        out_specs=pl.BlockSpec((tm, d), lambda i, f: (i, 0)),
        out_shape=jax.ShapeDtypeStruct((t, d), F32),
        scratch_shapes=[pltpu.VMEM((tm, d), BF16), pltpu.VMEM((tm, d), F32)],
        compiler_params=pltpu.CompilerParams(
            dimension_semantics=("parallel", "arbitrary"), vmem_limit_bytes=VMEM_LIMIT),
        name="ffn",
    )(h, g_pre.reshape(1, d), g_post.reshape(1, d),
      wg.astype(BF16), wu.astype(BF16), wd.astype(BF16))


def _deltanet_kernel(xq_ref, xk_ref, xv_ref, z_ref, sm_ref, cwq_ref, cwk_ref, cwv_ref,
                     alog_ref, dtb_ref, gn_ref, o_ref,
                     xpad_ref, q_ref, k_ref, v_ref, gb_ref, bb_ref, u_ref, w_ref, qk_ref, st_ref,
                     *, ts, a_col, b_col):
    s = pl.program_id(1)
    c = CHUNK
    d = HEAD_DIM
    nh = N_HEADS

    @pl.when(s == 0)
    def _():
        xpad_ref[:, 0:8, :] = jnp.zeros((3, 8, nh * d), F32)
        st_ref[...] = jnp.zeros_like(st_ref)

    @pl.when(s != 0)
    def _():
        xpad_ref[:, 0:8, :] = xpad_ref[:, ts:ts + 8, :]

    xpad_ref[0, 8:ts + 8, :] = xq_ref[...]
    xpad_ref[1, 8:ts + 8, :] = xk_ref[...]
    xpad_ref[2, 8:ts + 8, :] = xv_ref[...]

    def conv_silu(idx, cw_ref, hs):
        cw = cw_ref[:, hs]
        acc = xpad_ref[idx, 8 - (CONV_WIDTH - 1):8 - (CONV_WIDTH - 1) + ts, hs] * cw[0:1, :]
        for j in range(1, CONV_WIDTH):
            off = 8 - (CONV_WIDTH - 1) + j
            acc = acc + xpad_ref[idx, off:off + ts, hs] * cw[j:j + 1, :]
        return _silu(acc)

    def l2norm(t):
        return t * lax.rsqrt(jnp.sum(t * t, axis=-1, keepdims=True) + EPS)

    row = _iota((c, c), 0)
    col = _iota((c, c), 1)
    tri = (col <= row)
    strict = (col < row)
    tri_f = tri.astype(F32)
    upper_f = (row <= col).astype(F32)
    eye = (row == col).astype(F32)
    gnorm = gn_ref[...]
    chunks = range(ts // c)
    rs = [slice(ci * c, (ci + 1) * c) for ci in chunks]
    tri2 = jnp.concatenate([tri_f, tri_f], axis=1).astype(BF16)
    ones2 = jnp.ones((c, 2 * c), BF16)

    def cum2(lhs2, x):
        hi, lo = _split(x)
        return jnp.dot(lhs2, jnp.concatenate([hi, lo], axis=0), preferred_element_type=F32)

    for hh in range(nh):
        hs = slice(hh * d, (hh + 1) * d)
        q_ref[:, hs] = l2norm(conv_silu(0, cwq_ref, hs)) * (d ** -0.5)
        k_ref[:, hs] = l2norm(conv_silu(1, cwk_ref, hs))
        v_ref[:, hs] = conv_silu(2, cwv_ref, hs)

        a_raw = sm_ref[:, a_col + hh:a_col + hh + 1]
        b_raw = sm_ref[:, b_col + hh:b_col + hh + 1]
        g = -jnp.exp(alog_ref[:, hh:hh + 1]) * _softplus(a_raw + dtb_ref[:, hh:hh + 1])
        gb_ref[:, hs] = jnp.broadcast_to(g, (ts, d))
        bb_ref[:, hs] = jnp.broadcast_to(_sigmoid(b_raw), (ts, d))

        q = [q_ref[r, hs] for r in rs]
        k = [k_ref[r, hs] for r in rs]
        beta = [bb_ref[r, hs] for r in rs]
        gb = [gb_ref[r, hs] for r in rs]
        gc = [cum2(tri2, x) for x in gb]
        gc_row = [cum2(ones2, x[:, :c] * upper_f) for x in gb]
        decay = [jnp.where(tri, jnp.exp(jnp.minimum(a[:, :c] - b, 0.0)), 0.0) for a, b in zip(gc, gc_row)]
        kk = [_mm_nt(x, x) for x in k]
        n = [-jnp.where(strict, b[:, :c] * x * dc, 0.0) for b, x, dc in zip(beta, kk, decay)]
        inv = [eye + x for x in n]
        n_parts = [_x3_parts(x) for x in n]
        for step in range(5):
            n = [jnp.dot(a, b, preferred_element_type=F32) for a, b in n_parts]
            n_parts = [_x3_parts(x) for x in n]
            inv = [iv + jnp.dot(_x3_parts(iv)[0], b, preferred_element_type=F32)
                   for iv, (_, b) in zip(inv, n_parts)]
        egc = [jnp.exp(x) for x in gc]
        gl = [x[c - 1:c, :] for x in gc]
        for ci in chunks:
            r = rs[ci]
            u_ref[r, hs] = _mm_x3(inv[ci], v_ref[r, hs] * beta[ci])
            w_ref[r, hs] = _mm_x3(inv[ci], k[ci] * (beta[ci] * egc[ci]))
            qk_ref[hh, r, :] = _mm_nt(q[ci], k[ci]) * decay[ci]
            q_ref[r, hs] = q[ci] * egc[ci]
            k_ref[r, hs] = k[ci] * jnp.exp(gl[ci] - gc[ci])
            gb_ref[r, hs] = jnp.broadcast_to(jnp.exp(gl[ci]), (c, d))

    def chunk_body(ci, carry):
        r0 = pl.multiple_of(ci * c, c)
        for hh in range(nh):
            hs = slice(hh * d, (hh + 1) * d)
            st = st_ref[hh]
            v_new = u_ref[pl.ds(r0, c), hs] - _mm(w_ref[pl.ds(r0, c), hs], st)
            o = _mm(q_ref[pl.ds(r0, c), hs], st) + _mm(qk_ref[hh, pl.ds(r0, c), :], v_new)
            st_ref[hh] = st * gb_ref[pl.ds(r0, 1), hs] + _mm_tn(k_ref[pl.ds(r0, c), hs], v_new)
            o_ref[pl.ds(r0, c), hs] = _rms(o, gnorm) * _silu(z_ref[pl.ds(r0, c), hs])
        return carry

    lax.fori_loop(0, ts // c, chunk_body, 0)


def _deltanet(p32, conv_w, a_log, dt_bias, a_norm_g, *, ts, cols):
    bsz, s, _ = p32.shape
    d = HEAD_DIM
    nh = N_HEADS
    w = nh * d
    pad = lambda t: jnp.pad(t.astype(F32), (0, d - t.shape[0])).reshape(1, d)
    kernel = functools.partial(_deltanet_kernel, ts=ts, a_col=cols["a_lane"], b_col=cols["b_lane"])
    tile = lambda name: pl.BlockSpec((None, ts, w), lambda b, i: (b, i, cols[name] // nh))
    conv = lambda k: pl.BlockSpec((CONV_WIDTH, w), lambda b, i: (0, k))
    row = pl.BlockSpec((1, d), lambda b, i: (0, 0))
    return pl.pallas_call(
        kernel,
        grid=(bsz, s // ts),
        in_specs=[tile("qa"), tile("ka"), tile("va"), tile("za"),
                  pl.BlockSpec((None, ts, d), lambda b, i: (b, i, cols["small"])),
                  conv(0), conv(1), conv(2), row, row, row],
        out_specs=pl.BlockSpec((None, ts, w), lambda b, i: (b, i, 0)),
        out_shape=jax.ShapeDtypeStruct((bsz, s, w), F32),
        scratch_shapes=[pltpu.VMEM((3, ts + 8, w), F32)]
        + [pltpu.VMEM((ts, w), F32) for _ in range(7)]
        + [pltpu.VMEM((nh, ts, CHUNK), F32), pltpu.VMEM((nh, d, d), F32)],
        compiler_params=pltpu.CompilerParams(
            dimension_semantics=("parallel", "arbitrary"), vmem_limit_bytes=VMEM_LIMIT),
        name="deltanet",
    )(p32, p32, p32, p32, p32, conv_w.astype(F32), conv_w.astype(F32), conv_w.astype(F32),
      pad(a_log), pad(dt_bias), a_norm_g.astype(F32).reshape(1, d))


def _hgrn2_kernel(q_ref, f_ref, i_ref, gate_ref, lb_ref, gn_ref, o_ref,
                  qs_ref, ks_ref, gc_ref, st_ref, *, ts):
    s = pl.program_id(1)
    c = CHUNK
    d = HEAD_DIM
    nh = N_HEADS
    SUB = 16

    @pl.when(s == 0)
    def _():
        st_ref[...] = jnp.zeros_like(st_ref)

    lb = lb_ref[...]
    f_raw = f_ref[...]
    log_sig = jnp.minimum(f_raw, 0.0) - jnp.log1p(jnp.exp(-jnp.abs(f_raw)))
    la = jnp.log(lb)
    lbb = jnp.log1p(-lb) + log_sig
    log_f = jnp.maximum(la, lbb) + jnp.log1p(jnp.exp(-jnp.abs(la - lbb)))
    qs_ref[...] = _silu(q_ref[...])
    ks_ref[...] = (1.0 - lb) * _sigmoid(-f_raw)

    row = _iota((c, c), 0)
    col = _iota((c, c), 1)
    tri_f = (col <= row).astype(F32)
    ones_dd = jnp.ones((d, d), BF16)
    rows_8d = _iota((8, d), 0)
    gnorm = gn_ref[...]

    for ci in range(ts // c):
        gc_ref[ci * c:(ci + 1) * c, :] = _mm_f32(tri_f, log_f[ci * c:(ci + 1) * c, :])

    blocks = [(sb * SUB, (sb + 1) * SUB) for sb in range(c // SUB)]

    def head_chunk(r0, hh):
        hs = slice(hh * d, (hh + 1) * d)
        q = qs_ref[pl.ds(r0, c), hs]
        k = ks_ref[pl.ds(r0, c), hs]
        v = i_ref[pl.ds(r0, c), hs]
        gc = gc_ref[pl.ds(r0, c), hs]

        prods = []
        for top, end in blocks:
            for j in range(top, end):
                lo = (j // 8) * 8
                k_j = k[j:j + 1, :]
                g_j = gc[j:j + 1, :]
                e = jnp.exp(jnp.minimum(gc[lo:end, :] - g_j, 0.0))
                if j % 8:
                    head = jnp.where(rows_8d >= j - lo, e[:8], 0.0)
                    e = jnp.concatenate([head, e[8:]], axis=0) if lo + 8 < end else head
                prods.append(q[lo:end, :] * k_j * e)
        sums = jnp.dot(jnp.concatenate(prods, axis=0).astype(BF16), ones_dd,
                       preferred_element_type=F32)
        qk_far = []
        for top, end in blocks[1:]:
            g_b = gc[top - 1:top, :]
            qe = q[top:end, :] * jnp.exp(gc[top:end, :] - g_b)
            ke = k[:top, :] * jnp.exp(jnp.minimum(g_b - gc[:top, :], 0.0))
            qk_far.append(_mm_nt(qe, ke))
        far = [_mm(a, v[:top, :]) for a, (top, _) in zip(qk_far, blocks[1:])]

        groups = [jnp.zeros((8, d), F32) for _ in range(c // 8)]
        at = 0
        for top, end in blocks:
            for j in range(top, end):
                v_j = v[j:j + 1, :]
                for g in range(j // 8, end // 8):
                    groups[g] = groups[g] + sums[at:at + 8, :] * v_j
                    at += 8
        for f, (top, end) in zip(far, blocks[1:]):
            for g in range(top // 8, end // 8):
                groups[g] = groups[g] + f[(g * 8 - top):(g * 8 - top + 8), :]
        o_intra = jnp.concatenate(groups, axis=0)

        st = st_ref[hh]
        gl = gc[c - 1:c, :]
        o = o_intra + _mm_nt(q * jnp.exp(gc), st)
        st_ref[hh] = st * jnp.exp(gl) + _mm_tn(v, k * jnp.exp(gl - gc))
        o_ref[pl.ds(r0, c), hs] = _rms(o, gnorm) * _silu(gate_ref[pl.ds(r0, c), hs])

    def chunk_loop(ci, carry):
        r0 = pl.multiple_of(ci * c, c)
        for hh in range(nh):
            head_chunk(r0, hh)
        return carry

    lax.fori_loop(0, ts // c, chunk_loop, 0)


def _hgrn2(p32, lb, d_norm_g, *, ts, cols):
    bsz, s, _ = p32.shape
    d = HEAD_DIM
    nh = N_HEADS
    w = nh * d
    kernel = functools.partial(_hgrn2_kernel, ts=ts)
    tile = lambda name: pl.BlockSpec((None, ts, w), lambda b, i: (b, i, cols[name] // nh))
    return pl.pallas_call(
        kernel,
        grid=(bsz, s // ts),
        in_specs=[tile("qd"), tile("fd"), tile("id"), tile("gd"),
                  pl.BlockSpec((1, w), lambda b, i: (0, 0)),
                  pl.BlockSpec((1, d), lambda b, i: (0, 0))],
        out_specs=pl.BlockSpec((None, ts, w), lambda b, i: (b, i, 0)),
        out_shape=jax.ShapeDtypeStruct((bsz, s, w), F32),
        scratch_shapes=[pltpu.VMEM((ts, w), F32), pltpu.VMEM((ts, w), F32),
                        pltpu.VMEM((ts, w), F32), pltpu.VMEM((nh, d, d), F32)],
        compiler_params=pltpu.CompilerParams(
            dimension_semantics=("parallel", "arbitrary"), vmem_limit_bytes=VMEM_LIMIT),
        name="hgrn2",
    )(p32, p32, p32, p32, lb.astype(F32).reshape(1, w), d_norm_g.astype(F32).reshape(1, d))


def _stickbreak_kernel(q_ref, k_ref, v_ref, o_ref, *, tq):
    i = pl.program_id(1)
    d = HEAD_DIM
    nh = N_HEADS
    row = _iota((tq, tq), 0)
    col = _iota((tq, tq), 1)
    causal = col < row
    later = (row > col).astype(BF16)
    later2 = jnp.concatenate([later, later], axis=0)

    def scores(j, hs, diag):
        z = _mm_nt(q_ref[:, hs], k_ref[pl.ds(pl.multiple_of(j * tq, tq), tq), hs]) * (d ** -0.5)
        sp = _softplus(z)
        l1m = jnp.where(causal, -sp, 0.0) if diag else -sp
        rest = jnp.dot(jnp.concatenate(_split(l1m), axis=1), later2,
                       preferred_element_type=F32)
        return (z - sp) + rest, l1m

    def block(j, carries):
        out = []
        for hh in range(nh):
            hs = slice(hh * d, (hh + 1) * d)
            logw, l1m = scores(j, hs, False)
            o_ref[:, hs] += _mm(jnp.exp(logw + carries[hh]), v_ref[pl.ds(pl.multiple_of(j * tq, tq), tq), hs])
            out.append(carries[hh] + jnp.sum(l1m, axis=-1, keepdims=True))
        return tuple(out)

    jp = jnp.maximum(i - 1, 0)
    live = jnp.where(i > 0, 1.0, 0.0)
    heads = [slice(hh * d, (hh + 1) * d) for hh in range(nh)]
    sd = [scores(i, hs, True) for hs in heads]
    sp_ = [scores(jp, hs, False) for hs in heads]
    carries = []
    for hh, hs in enumerate(heads):
        c1 = jnp.sum(sd[hh][1], axis=-1, keepdims=True)
        p_d = jnp.where(causal, jnp.exp(sd[hh][0]), 0.0)
        p_p = jnp.exp(sp_[hh][0] + c1) * live
        o_ref[:, hs] = (_mm(p_d, v_ref[pl.ds(pl.multiple_of(i * tq, tq), tq), hs])
                        + _mm(p_p, v_ref[pl.ds(pl.multiple_of(jp * tq, tq), tq), hs]))
        carries.append(c1 + jnp.sum(sp_[hh][1], axis=-1, keepdims=True))
    carries = tuple(carries)

    def cond(c):
        worst = functools.reduce(jnp.maximum, c[1])
        return jnp.logical_and(c[0] >= 0, jnp.max(worst) >= EXP_ZERO_BELOW)

    def body(c):
        return c[0] - 1, block(c[0], c[1])

    lax.while_loop(cond, body, (i - 2, carries))


def _stickbreak(p16, *, tq, cols):
    bsz, s, _ = p16.shape
    nh = N_HEADS
    w = nh * HEAD_DIM
    kernel = functools.partial(_stickbreak_kernel, tq=tq)
    resident = dict(pipeline_mode=pl.Buffered(1))
    return pl.pallas_call(
        kernel,
        grid=(bsz, s // tq),
        in_specs=[pl.BlockSpec((None, tq, w), lambda b, i: (b, i, cols["qc"] // nh)),
                  pl.BlockSpec((None, s, w), lambda b, i: (b, 0, cols["kc"] // nh), **resident),
                  pl.BlockSpec((None, s, w), lambda b, i: (b, 0, cols["vc"] // nh), **resident)],
        out_specs=pl.BlockSpec((None, tq, w), lambda b, i: (b, i, 0)),
        out_shape=jax.ShapeDtypeStruct((bsz, s, w), F32),
        compiler_params=pltpu.CompilerParams(
            dimension_semantics=("parallel", "arbitrary"), vmem_limit_bytes=VMEM_LIMIT),
        name="stickbreak",
    )(p16, p16, p16)


def _dsa_kernel(qi_ref, smq_ref, q_ref, sm_ref, k_ref, vt_ref, bias_ref, o_ref,
                sc_ref, scb_ref, wb_ref, qc_ref, kct_ref, bd_ref, lg_ref, *, tq, k_sel, wi_lane, wide):
    i = pl.program_id(1)
    tk = tq
    d = HEAD_DIM
    nh = N_HEADS
    ksel = float(k_sel)
    per_wide = wide // tk
    n_wide = (i + per_wide) // per_wide
    sub = 2 * tk
    lane_q = _iota((1, tq), 1)

    def tree(parts, op):
        while len(parts) > 1:
            parts = [op(parts[j], parts[j + 1]) if j + 1 < len(parts) else parts[j]
                     for j in range(0, len(parts), 2)]
        return parts[0]

    def col_fold(x, op=jnp.add, rows=8):
        return tree([x[r * rows:(r + 1) * rows] for r in range(x.shape[0] // rows)], op)

    @pl.when(i == 0)
    def _():
        def prep(g, carry):
            g0 = pl.multiple_of(g * wide, wide)
            kt = sm_ref[pl.ds(g0, wide), :].T[:IDX_DIM, :]
            hi, lo = _split(kt)
            kct_ref[:, pl.ds(g0, wide)] = jnp.concatenate([hi, lo, hi], axis=0)
            return carry
        lax.fori_loop(0, sm_ref.shape[0] // wide, prep, 0)

    smq = smq_ref[...]
    lane = _iota(smq.shape, 1)
    for hh in range(IDX_HEADS):
        qh = qi_ref[:, hh * IDX_DIM:(hh + 1) * IDX_DIM]
        hi, lo = _split(qh)
        qc_ref[hh] = jnp.concatenate([hi, hi, lo], axis=-1)
        w = jnp.sum(jnp.where(lane == wi_lane + hh, smq, 0.0), axis=-1, keepdims=True)
        wb_ref[hh] = jnp.broadcast_to(w * ((IDX_HEADS ** -0.5) * (IDX_DIM ** -0.5)), (tq, tk))

    q2t = (q_ref[...] * ((d ** -0.5) * LOG2E)).T.astype(BF16)
    zero_dq = jnp.zeros((d, tq), BF16)
    for p in range(nh // 2):
        top = jnp.concatenate([q2t[2 * p * d:(2 * p + 1) * d], zero_dq], axis=1)
        bot = jnp.concatenate([zero_dq, q2t[(2 * p + 1) * d:(2 * p + 2) * d]], axis=1)
        bd_ref[p] = jnp.concatenate([top, bot], axis=0)

    limit = i * tq + (lane_q // CHUNK + 1) * CHUNK
    rows_t = _iota((tk, tq), 0)

    def score_group(g, mm, masked):
        mn, mx = mm
        for sb in range(wide // sub):
            k0 = pl.multiple_of(g * wide + sb * sub, sub)
            kct = kct_ref[:, pl.ds(k0, sub)]
            tiles = [jnp.zeros((tq, tk), F32) for _ in range(sub // tk)]
            for hh in range(IDX_HEADS):
                s_h = jnp.dot(qc_ref[hh], kct, preferred_element_type=F32)
                for ti in range(sub // tk):
                    tiles[ti] = tiles[ti] + jnp.maximum(s_h[:, ti * tk:(ti + 1) * tk], 0.0) * wb_ref[hh]
            for ti in range(sub // tk):
                kb = pl.multiple_of(k0 + ti * tk, tk)
                sct = tiles[ti].T
                if masked:
                    adm = (kb + rows_t) < limit
                    mn = jnp.minimum(mn, col_fold(jnp.where(adm, sct, jnp.inf), jnp.minimum))
                    sct = jnp.where(adm, sct, -jnp.inf)
                else:
                    mn = jnp.minimum(mn, col_fold(sct, jnp.minimum))
                mx = jnp.maximum(mx, col_fold(sct, jnp.maximum))
                sc_ref[pl.ds(kb, tk), :] = sct
                scb_ref[pl.ds(kb, tk), :] = _floor_bf16(sct)
        return mn, mx

    def score_pair(j, mm):
        return score_group(2 * j + 1, score_group(2 * j, mm, False), False)

    n_full = n_wide - 1
    mm = lax.fori_loop(0, n_full // 2, score_pair,
                       (jnp.full((8, tq), jnp.inf, F32), jnp.full((8, tq), -jnp.inf, F32)))
    mm = lax.cond(n_full % 2 == 1, lambda c: score_group(n_full - 1, c, False), lambda c: c, mm)
    mn, mx = score_group(n_wide - 1, mm, True)

    n_pairs = (n_wide + 1) // 2

    @pl.when(n_wide % 2 == 1)
    def _():
        sc_ref[pl.ds(pl.multiple_of(n_wide * wide, wide), wide), :] = jnp.full((wide, tq), -jnp.inf, F32)
    rmin = jnp.min(mn, axis=0, keepdims=True)
    rmax = jnp.max(mx, axis=0, keepdims=True)

    def count(pred):
        def body(g, acc):
            blk = sc_ref[pl.ds(pl.multiple_of(g * wide, wide), wide), :]
            return acc + col_fold(pred(blk))
        return jnp.sum(lax.fori_loop(0, n_wide, body, jnp.zeros((8, tq), F32)), axis=0, keepdims=True)

    def max_below(x):
        def body(g, acc):
            blk = sc_ref[pl.ds(pl.multiple_of(g * wide, wide), wide), :]
            return jnp.maximum(acc, col_fold(jnp.where(blk < x, blk, -jnp.inf), jnp.maximum))
        return jnp.max(lax.fori_loop(0, n_wide, body, jnp.full((8, tq), -jnp.inf, F32)), axis=0, keepdims=True)

    n_adm = limit.astype(F32)
    all_sel = n_adm <= ksel

    def bisect(c):
        lo, hi, c_lo = c
        mid = 0.5 * lo + 0.5 * hi
        cm = count(lambda blk: _ind(blk >= mid))
        ge = cm >= ksel
        return jnp.where(ge, mid, lo), jnp.where(ge, hi, mid), jnp.where(ge, cm, c_lo)

    def pending(c_lo, tied):
        return jnp.where(all_sel, 0.0, jnp.where(tied > 0.5, 0.0, _ind(c_lo != ksel)))

    def bisect_coarse(_, c):
        lo, hi, c_lo = c
        mid = _floor_bf16(0.5 * lo + 0.5 * hi).astype(F32)
        t_b = jnp.broadcast_to(mid, (16, tq)).astype(BF16)
        one_b = jnp.ones((16, tq), BF16)
        zero_b = jnp.zeros((16, tq), BF16)

        def body(g, acc):
            blk = scb_ref[pl.ds(pl.multiple_of(g * wide, wide), wide), :]
            ind = [jnp.where(blk[r * 16:(r + 1) * 16] >= t_b, one_b, zero_b) for r in range(wide // 16)]
            return acc + tree(ind, jnp.add).astype(F32)

        acc = lax.fori_loop(0, n_wide, body, jnp.zeros((16, tq), F32))
        cm = jnp.sum(acc, axis=0, keepdims=True)
        ge = cm >= ksel
        return jnp.where(ge, mid, lo), jnp.where(ge, hi, mid), jnp.where(ge, cm, c_lo)

    lo0 = _floor_bf16(rmin).astype(F32)
    hi0 = _floor_bf16(rmax + (jnp.abs(rmax) * (2.0 ** -6) + 1e-30)).astype(F32)
    state = lax.fori_loop(0, BISECT_COARSE, bisect_coarse, (lo0, hi0, n_adm))
    state = lax.fori_loop(0, BISECT_FIXED, lambda _, c: bisect(c), state)

    def round_cond(c):
        return jnp.max(pending(c[0][2], c[1])) > 0.5

    def round_body(c):
        st, tied, v, need = c

        def more_cond(s):
            return jnp.logical_and(s[0] < BISECT_EXTRA, jnp.max(pending(s[1][2], tied)) > 0.5)

        _, st = lax.while_loop(more_cond, lambda s: (s[0] + 1, bisect(s[1])), (jnp.int32(0), st))
        pend = pending(st[2], tied)

        def check(_):
            cand = max_below(st[1])
            c_ge = count(lambda blk: _ind(blk >= cand))
            c_gt = count(lambda blk: _ind(blk > cand))
            ok = jnp.where(pend > 0.5, _ind(c_ge >= ksel), 0.0)
            return (jnp.where(ok > 0.5, 1.0, tied), jnp.where(ok > 0.5, cand, v),
                    jnp.where(ok > 0.5, ksel - c_gt, need))

        tied, v, need = lax.cond(jnp.max(pend) > 0.5, check, lambda _: (tied, v, need), 0)
        return st, tied, v, need

    zeros1 = jnp.zeros((1, tq), F32)
    (lo_f, _, _), tied, v_tie, need = lax.while_loop(round_cond, round_body, (state, zeros1, zeros1, zeros1))
    vth = jnp.where(all_sel, F32_LOWEST, jnp.where(tied > 0.5, v_tie, lo_f))

    @pl.when(jnp.max(tied) > 0.5)
    def _():
        v_eq = jnp.where(tied > 0.5, v_tie, jnp.inf)
        incl = (_iota((tk, tk), 1) <= _iota((tk, tk), 0)).astype(BF16)

        def demote(g, seen):
            g0 = pl.multiple_of(g * wide, wide)
            xs = [sc_ref[pl.ds(g0 + pb * tk, tk), :] for pb in range(per_wide)]
            eqs = [_ind(x == v_eq) for x in xs]
            inblk = [jnp.dot(incl, e.astype(BF16), preferred_element_type=F32) for e in eqs]
            for pb in range(per_wide):
                rank = inblk[pb] + seen
                sc_ref[pl.ds(g0 + pb * tk, tk), :] = jnp.where(eqs[pb] * _ind(rank > need) > 0.5,
                                                               -jnp.inf, xs[pb])
                seen = seen + jnp.sum(col_fold(eqs[pb]), axis=0, keepdims=True)
            return seen

        lax.fori_loop(0, n_wide, demote, zeros1)

    g_near = jnp.maximum(i - 1, 0) // per_wide

    def logit_group(g, mx, near):
        out = list(mx)
        for sb in range(wide // sub):
            k0 = pl.multiple_of(g * wide + sb * sub, sub)
            sel = sc_ref[pl.ds(k0, sub), :] >= vth
            for p in range(nh // 2):
                pair = jnp.dot(k_ref[pl.ds(k0, sub), 2 * p * d:(2 * p + 2) * d], bd_ref[p],
                               preferred_element_type=F32)
                for hh in (2 * p, 2 * p + 1):
                    lm = pair[:, (hh - 2 * p) * tq:(hh - 2 * p + 1) * tq]
                    if near:
                        back = [jnp.clip(i - (g * per_wide + sb * (sub // tk) + pb), 0, 2)
                                for pb in range(sub // tk)]
                        lm = lm + jnp.concatenate([bias_ref[bk, hh] for bk in back], axis=0)
                    lm = jnp.where(sel, lm, NEG_BIG)
                    lg_ref[hh, pl.ds(k0, sub), :] = lm
                    out[hh] = jnp.maximum(out[hh], col_fold(lm, jnp.maximum))
        return tuple(out)

    mx = tuple(jnp.full((8, tq), NEG_BIG, F32) for _ in range(nh))
    def logit_pair(j, mx, near):
        return logit_group(2 * j + 1, logit_group(2 * j, mx, near), near)

    far_pairs = g_near // 2
    mx = lax.fori_loop(0, far_pairs, functools.partial(logit_pair, near=False), mx)
    mx = lax.fori_loop(far_pairs, n_pairs, functools.partial(logit_pair, near=True), mx)
    m_q = [jnp.max(mx[hh], axis=0, keepdims=True) for hh in range(nh)]

    def pv_body(g, carry):
        g0 = pl.multiple_of(g * wide, wide)
        ls, accs = carry
        new_l, new_a = [], []
        for hh in range(nh):
            p = jnp.exp2(lg_ref[hh, pl.ds(g0, wide), :] - m_q[hh])
            new_l.append(ls[hh] + col_fold(p))
            new_a.append(accs[hh] + jnp.dot(vt_ref[hh * d:(hh + 1) * d, pl.ds(g0, wide)], p.astype(BF16),
                                            preferred_element_type=F32))
        return tuple(new_l), tuple(new_a)

    ls, accs = lax.fori_loop(0, n_pairs, lambda j, c: pv_body(2 * j + 1, pv_body(2 * j, c)),
                             (tuple(jnp.zeros((8, tq), F32) for _ in range(nh)),
                              tuple(jnp.zeros((d, tq), F32) for _ in range(nh))))
    for hh in range(nh):
        o_ref[:, hh * d:(hh + 1) * d] = (accs[hh] / jnp.sum(ls[hh], axis=0, keepdims=True)).T


def _dsa(p32, p16, vt, bias_tiles, *, tq, cols):
    bsz, s, _ = p32.shape
    d = HEAD_DIM
    nh = N_HEADS
    wide = 4 * tq
    k_sel = min(TOPK_MAX, s // 4)
    w512 = nh * d
    kernel = functools.partial(_dsa_kernel, tq=tq, k_sel=k_sel, wi_lane=cols["wi_lane"], wide=wide)
    resident = dict(pipeline_mode=pl.Buffered(1))
    return pl.pallas_call(
        kernel,
        grid=(bsz, s // tq),
        in_specs=[pl.BlockSpec((None, tq, w512), lambda b, i: (b, i, cols["qi"] // nh)),
                  pl.BlockSpec((None, tq, d), lambda b, i: (b, i, cols["small"])),
                  pl.BlockSpec((None, tq, w512), lambda b, i: (b, i, cols["qb"] // nh)),
                  pl.BlockSpec((None, s, d), lambda b, i: (b, 0, cols["small"]), **resident),
                  pl.BlockSpec((None, s, w512), lambda b, i: (b, 0, cols["kb"] // nh), **resident),
                  pl.BlockSpec((w512, s), lambda b, i: (0, b), **resident),
                  pl.BlockSpec((3, nh, tq, tq), lambda b, i: (0, 0, 0, 0), **resident)],
        out_specs=pl.BlockSpec((None, tq, w512), lambda b, i: (b, i, 0)),
        out_shape=jax.ShapeDtypeStruct((bsz, s, w512), F32),
        scratch_shapes=[pltpu.VMEM((s, tq), F32),
                        pltpu.VMEM((s, tq), BF16),
                        pltpu.VMEM((IDX_HEADS, tq, tq), F32),
                        pltpu.VMEM((IDX_HEADS, tq, 3 * IDX_DIM), BF16),
                        pltpu.VMEM((3 * IDX_DIM, s), BF16),
                        pltpu.VMEM((nh // 2, 2 * d, 2 * tq), BF16),
                        pltpu.VMEM((nh, s, tq), F32)],
        compiler_params=pltpu.CompilerParams(
            dimension_semantics=("parallel", "arbitrary"), vmem_limit_bytes=VMEM_LIMIT),
        name="dsa",
    )(p32, p32, p32, p32, p16, vt, bias_tiles)


def _t5_bucket(rel):
    nb = REL_BUCKETS // 2
    max_exact = nb // 2
    ret = jnp.where(rel > 0, nb, 0)
    n = jnp.abs(rel)
    large = max_exact + (jnp.log(jnp.maximum(n, 1).astype(F32) / max_exact)
                         / math.log(REL_MAX_DIST / max_exact) * (nb - max_exact)).astype(jnp.int32)
    large = jnp.minimum(large, nb - 1)
    return ret + jnp.where(n < max_exact, n, large)


def _bias_tiles(rel_table, tq):
    assert tq >= REL_MAX_DIST
    t = jnp.arange(tq)
    back = jnp.arange(3)
    rel = (t[None, None, :] - back[:, None, None] * tq) - t[None, :, None]
    onehot = (_t5_bucket(rel)[..., None] == jnp.arange(REL_BUCKETS)).astype(F32)
    tiles = jnp.einsum("bqkn,nh->bhkq", onehot, rel_table.astype(F32),
                       precision=HIGHEST)
    return (tiles - tiles[2:3]) * LOG2E


def _even_layout(w_in):
    d = HEAD_DIM
    a_w = 2 * N_HEADS * d + N_HEADS * d
    offs = {}
    o = 0
    for name, w in (("qkv", a_w), ("z", N_HEADS * d), ("a", N_HEADS), ("b", N_HEADS),
                    ("qb", N_HEADS * d), ("kb", N_HEADS * d), ("vb", N_HEADS * d),
                    ("qi", IDX_HEADS * IDX_DIM), ("ki", IDX_DIM), ("wi", IDX_HEADS)):
        offs[name] = (o, o + w)
        o += w
    assert o == w_in.shape[1]
    sl = lambda n: w_in[:, offs[n][0]:offs[n][1]]
    small_w = IDX_DIM + 2 * N_HEADS + IDX_HEADS
    small_pad = -small_w % d
    zeros = lambda n: jnp.zeros((w_in.shape[0], n), w_in.dtype)
    w32 = jnp.concatenate([sl("qkv"), sl("z"), sl("qb"), sl("qi"),
                           sl("ki"), sl("a"), sl("b"), sl("wi"), zeros(small_pad)], axis=1)
    n32 = w32.shape[1]
    tn = n32 // 5
    assert tn * 5 == n32 and tn % d == 0
    w16 = jnp.concatenate([sl("kb"), zeros(tn - N_HEADS * d)], axis=1)
    nh = N_HEADS
    cols = dict(qa=0, ka=nh, va=2 * nh, za=3 * nh, qb=4 * nh, qi=5 * nh, small=6 * nh, kb=0,
                a_lane=IDX_DIM, b_lane=IDX_DIM + nh, wi_lane=IDX_DIM + 2 * nh, n32=n32, tn=tn)
    return jnp.concatenate([w32, w16], axis=1).astype(BF16), sl("vb").T.astype(BF16), cols


def kernel(x, norm_g, w_in_even, conv_w_even, a_log_even, dt_bias_even, a_norm_even, w_out_even,
           rel_bias, w_in_odd, lb_logits, d_norm_odd, w_out_odd, w_gate, w_up, w_down):
    bsz, s, d = x.shape
    t = bsz * s
    depth = norm_g.shape[0]
    nh = N_HEADS
    tq = 128
    lb_all = jnp.cumsum(jax.nn.softmax(lb_logits.astype(F32), axis=0), axis=0)
    lb_all = lb_all - lb_all[:1]
    odd_cols = dict(qc=0, kc=nh, vc=2 * nh, qd=0, fd=nh, id=2 * nh, gd=3 * nh)
    bias_tiles = _bias_tiles(rel_bias, tq)

    h = x.reshape(t, d)
    for l in range(depth):
        if l % 2 == 0:
            e = l // 2
            w_even, w_vt, cols = _even_layout(w_in_even[e])
            p32, p16, vt = _norm_matmul(h, norm_g[l, 0], w_even, tm=1024, tn=cols["tn"], n32=cols["n32"],
                                        w_t=w_vt)
            p32 = p32.reshape(bsz, s, -1)
            p16 = p16.reshape(bsz, s, -1)
            o_1 = _deltanet(p32, conv_w_even[e], a_log_even[e], dt_bias_even[e], a_norm_even[e],
                            ts=min(512, s), cols=cols)
            o_2 = _dsa(p32, p16, vt, bias_tiles, tq=tq, cols=cols)
            w_out = w_out_even[e]
        else:
            o = l // 2
            n16 = 3 * nh * HEAD_DIM
            w_odd = jnp.concatenate([w_in_odd[o][:, n16:], w_in_odd[o][:, :n16]], axis=1).astype(BF16)
            p32, p16 = _norm_matmul(h, norm_g[l, 0], w_odd, tm=1024, tn=512, n32=w_odd.shape[1] - n16)
            p32 = p32.reshape(bsz, s, -1)
            p16 = p16.reshape(bsz, s, -1)
            o_1 = _stickbreak(p16, tq=tq, cols=odd_cols)
            o_2 = _hgrn2(p32, lb_all[l], d_norm_odd[o], ts=min(512, s), cols=odd_cols)
            w_out = w_out_odd[o]
        h = _outproj(o_1.reshape(t, -1), o_2.reshape(t, -1), w_out, h, norm_g[l, 1], tm=512)
        h = _ffn(h, norm_g[l, 2], norm_g[l, 3], w_gate[l], w_up[l], w_down[l], tm=1024, tf=256)
    return h.reshape(bsz, s, d)
```

```python
import functools
import math

import jax
import jax.numpy as jnp
from jax import lax
from jax.experimental import pallas as pl
from jax.experimental.pallas import tpu as pltpu

F32 = jnp.float32
BF16 = jnp.bfloat16
HIGHEST = lax.Precision.HIGHEST

CHUNK = 64
HEAD_DIM = 128
N_HEADS = 4
IDX_HEADS = 8
IDX_DIM = 64
TOPK_MAX = 256
CONV_WIDTH = 4
REL_BUCKETS = 32
REL_MAX_DIST = 128
EPS = 1e-6
NEG_BIG = -1e30
LOG2E = 1.4426950408889634
BISECT_COARSE = 12
BISECT_FIXED = 8
BISECT_EXTRA = 6
F32_LOWEST = -3.4028234663852886e38
EXP_ZERO_BELOW = -104.0
VMEM_LIMIT = 56 * 1024 * 1024


def _mm(a, b):
    return jnp.dot(a.astype(BF16), b.astype(BF16), preferred_element_type=F32)


def _mm_nt(a, b):
    return lax.dot_general(a.astype(BF16), b.astype(BF16), (((1,), (1,)), ((), ())),
                           preferred_element_type=F32)


def _mm_tn(a, b):
    return lax.dot_general(a.astype(BF16), b.astype(BF16), (((0,), (0,)), ((), ())),
                           preferred_element_type=F32)


def _mm_f32(a, b):
    return jnp.dot(a, b, precision=HIGHEST, preferred_element_type=F32)


def _split(x):
    hi = x.astype(BF16)
    return hi, (x - hi.astype(F32)).astype(BF16)


def _x3_parts(x):
    hi, lo = _split(x)
    return jnp.concatenate([hi, hi, lo], axis=1), jnp.concatenate([hi, lo, hi], axis=0)


def _mm_x3(a, b):
    return jnp.dot(_x3_parts(a)[0], _x3_parts(b)[1], preferred_element_type=F32)


def _floor_bf16(x):
    bits = pltpu.bitcast(x, jnp.int32)
    down = jnp.where(bits >= 0, bits, bits + 0xFFFF) & jnp.int32(-65536)
    return pltpu.bitcast(down, F32).astype(BF16)


def _sigmoid(x):
    return 1.0 / (1.0 + jnp.exp(-x))


def _silu(x):
    return x * _sigmoid(x)


def _softplus(x):
    return jnp.maximum(x, 0.0) + jnp.log1p(jnp.exp(-jnp.abs(x)))


def _rms(x, g):
    return x * lax.rsqrt(jnp.mean(x * x, axis=-1, keepdims=True) + EPS) * g


def _iota(shape, dim):
    return lax.broadcasted_iota(jnp.int32, shape, dim)


def _ind(mask):
    return jnp.where(mask, 1.0, 0.0)


def _norm_matmul_kernel(x_ref, g_ref, w_ref, *rest, n_t, tiles32):
    if n_t:
        wt_ref, o32_ref, o16_ref, ot_ref, xn_ref = rest
    else:
        o32_ref, o16_ref, xn_ref = rest
    j = pl.program_id(1)

    @pl.when(j == 0)
    def _():
        xn_ref[...] = _rms(x_ref[...], g_ref[...]).astype(BF16)
        if n_t:
            ot_ref[...] = lax.dot_general(wt_ref[...], xn_ref[...], (((1,), (1,)), ((), ())),
                                          preferred_element_type=F32).astype(BF16)

    y = jnp.dot(xn_ref[...], w_ref[...], preferred_element_type=F32)

    @pl.when(j < tiles32)
    def _():
        o32_ref[...] = y

    @pl.when(j >= tiles32)
    def _():
        o16_ref[...] = y.astype(BF16)


def _norm_matmul(x, g, w, *, tm, tn, n32, w_t=None):
    t, d = x.shape
    n = w.shape[1]
    n_t = 0 if w_t is None else w_t.shape[0]
    tiles32 = n32 // tn
    assert tiles32 * tn == n32 and (n - n32) % tn == 0 and 0 < n32 < n
    in_specs = [pl.BlockSpec((tm, d), lambda i, j: (i, 0)),
                pl.BlockSpec((1, d), lambda i, j: (0, 0)),
                pl.BlockSpec((d, tn), lambda i, j: (0, j))]
    out_specs = [pl.BlockSpec((tm, tn), lambda i, j: (i, jnp.minimum(j, tiles32 - 1))),
                 pl.BlockSpec((tm, tn), lambda i, j: (i, jnp.maximum(j - tiles32, 0)))]
    out_shape = [jax.ShapeDtypeStruct((t, n32), F32), jax.ShapeDtypeStruct((t, n - n32), BF16)]
    args = [x, g.reshape(1, d), w]
    if n_t:
        in_specs.append(pl.BlockSpec((n_t, d), lambda i, j: (0, 0)))
        out_specs.append(pl.BlockSpec((n_t, tm), lambda i, j: (0, i)))
        out_shape.append(jax.ShapeDtypeStruct((n_t, t), BF16))
        args.append(w_t)
    return pl.pallas_call(
        functools.partial(_norm_matmul_kernel, n_t=n_t, tiles32=tiles32),
        grid=(t // tm, n // tn),
        in_specs=in_specs,
        out_specs=out_specs,
        out_shape=out_shape,
        scratch_shapes=[pltpu.VMEM((tm, d), BF16)],
        compiler_params=pltpu.CompilerParams(
            dimension_semantics=("parallel", "arbitrary"), vmem_limit_bytes=VMEM_LIMIT),
        name="norm_matmul",
    )(*args)


def _outproj_kernel(ca_ref, cb_ref, wa_ref, wb_ref, h_ref, g_ref, o_ref):
    y = (jnp.dot(ca_ref[...].astype(BF16), wa_ref[...], preferred_element_type=F32)
         + jnp.dot(cb_ref[...].astype(BF16), wb_ref[...], preferred_element_type=F32))
    o_ref[...] = h_ref[...] + _rms(y, g_ref[...])


def _outproj(ca, cb, w, h, g, *, tm):
    t, d = h.shape
    wa_n = ca.shape[1]
    wb_n = cb.shape[1]
    wa = w[:wa_n].astype(BF16)
    wb = w[wa_n:].astype(BF16)
    return pl.pallas_call(
        _outproj_kernel,
        grid=(t // tm,),
        in_specs=[pl.BlockSpec((tm, wa_n), lambda i: (i, 0)),
                  pl.BlockSpec((tm, wb_n), lambda i: (i, 0)),
                  pl.BlockSpec((wa_n, d), lambda i: (0, 0)),
                  pl.BlockSpec((wb_n, d), lambda i: (0, 0)),
                  pl.BlockSpec((tm, d), lambda i: (i, 0)),
                  pl.BlockSpec((1, d), lambda i: (0, 0))],
        out_specs=pl.BlockSpec((tm, d), lambda i: (i, 0)),
        out_shape=jax.ShapeDtypeStruct((t, d), F32),
        compiler_params=pltpu.CompilerParams(
            dimension_semantics=("parallel",), vmem_limit_bytes=VMEM_LIMIT),
        name="outproj",
    )(ca, cb, wa, wb, h, g.reshape(1, d))


def _ffn_kernel(h_ref, gpre_ref, gpost_ref, wg_ref, wu_ref, wd_ref, o_ref, xn_ref, acc_ref):
    f = pl.program_id(1)

    @pl.when(f == 0)
    def _():
        xn_ref[...] = _rms(h_ref[...], gpre_ref[...]).astype(BF16)
        acc_ref[...] = jnp.zeros_like(acc_ref)

    xn = xn_ref[...]
    gate = jnp.dot(xn, wg_ref[...], preferred_element_type=F32)
    up = jnp.dot(xn, wu_ref[...], preferred_element_type=F32)
    act = (_silu(gate) * up).astype(BF16)
    acc_ref[...] += jnp.dot(act, wd_ref[...], preferred_element_type=F32)

    @pl.when(f == pl.num_programs(1) - 1)
    def _():
        o_ref[...] = h_ref[...] + _rms(acc_ref[...], gpost_ref[...])


def _ffn(h, g_pre, g_post, wg, wu, wd, *, tm, tf):
    t, d = h.shape
    ff = wg.shape[1]
    return pl.pallas_call(
        _ffn_kernel,
        grid=(t // tm, ff // tf),
        in_specs=[pl.BlockSpec((tm, d), lambda i, f: (i, 0)),
                  pl.BlockSpec((1, d), lambda i, f: (0, 0)),
                  pl.BlockSpec((1, d), lambda i, f: (0, 0)),
                  pl.BlockSpec((d, tf), lambda i, f: (0, f)),
                  pl.BlockSpec((d, tf), lambda i, f: (0, f)),
                  pl.BlockSpec((tf, d), lambda i, f: (f, 0))],
        out_specs=pl.BlockSpec((tm, d), lambda i, f: (i, 0)),
        out_shape=jax.ShapeDtypeStruct((t, d), F32),
        scratch_shapes=[pltpu.VMEM((tm, d), BF16), pltpu.VMEM((tm, d), F32)],
        compiler_params=pltpu.CompilerParams(
            dimension_semantics=("parallel", "arbitrary"), vmem_limit_bytes=VMEM_LIMIT),
        name="ffn",
    )(h, g_pre.reshape(1, d), g_post.reshape(1, d),
      wg.astype(BF16), wu.astype(BF16), wd.astype(BF16))


def _deltanet_kernel(xq_ref, xk_ref, xv_ref, z_ref, sm_ref, cwq_ref, cwk_ref, cwv_ref,
                     alog_ref, dtb_ref, gn_ref, o_ref,
                     xpad_ref, q_ref, k_ref, v_ref, gb_ref, bb_ref, u_ref, w_ref, qk_ref, st_ref,
                     *, ts, a_col, b_col):
    s = pl.program_id(1)
    c = CHUNK
    d = HEAD_DIM
    nh = N_HEADS

    @pl.when(s == 0)
    def _():
        xpad_ref[:, 0:8, :] = jnp.zeros((3, 8, nh * d), F32)
        st_ref[...] = jnp.zeros_like(st_ref)

    @pl.when(s != 0)
    def _():
        xpad_ref[:, 0:8, :] = xpad_ref[:, ts:ts + 8, :]

    xpad_ref[0, 8:ts + 8, :] = xq_ref[...]
    xpad_ref[1, 8:ts + 8, :] = xk_ref[...]
    xpad_ref[2, 8:ts + 8, :] = xv_ref[...]

    def conv_silu(idx, cw_ref, hs):
        cw = cw_ref[:, hs]
        acc = xpad_ref[idx, 8 - (CONV_WIDTH - 1):8 - (CONV_WIDTH - 1) + ts, hs] * cw[0:1, :]
        for j in range(1, CONV_WIDTH):
            off = 8 - (CONV_WIDTH - 1) + j
            acc = acc + xpad_ref[idx, off:off + ts, hs] * cw[j:j + 1, :]
        return _silu(acc)

    def l2norm(t):
        return t * lax.rsqrt(jnp.sum(t * t, axis=-1, keepdims=True) + EPS)

    row = _iota((c, c), 0)
    col = _iota((c, c), 1)
    tri = (col <= row)
    strict = (col < row)
    tri_f = tri.astype(F32)
    upper_f = (row <= col).astype(F32)
    eye = (row == col).astype(F32)
    gnorm = gn_ref[...]
    chunks = range(ts // c)
    rs = [slice(ci * c, (ci + 1) * c) for ci in chunks]
    tri2 = jnp.concatenate([tri_f, tri_f], axis=1).astype(BF16)
    ones2 = jnp.ones((c, 2 * c), BF16)

    def cum2(lhs2, x):
        hi, lo = _split(x)
        return jnp.dot(lhs2, jnp.concatenate([hi, lo], axis=0), preferred_element_type=F32)

    for hh in range(nh):
        hs = slice(hh * d, (hh + 1) * d)
        q_ref[:, hs] = l2norm(conv_silu(0, cwq_ref, hs)) * (d ** -0.5)
        k_ref[:, hs] = l2norm(conv_silu(1, cwk_ref, hs))
        v_ref[:, hs] = conv_silu(2, cwv_ref, hs)

        a_raw = sm_ref[:, a_col + hh:a_col + hh + 1]
        b_raw = sm_ref[:, b_col + hh:b_col + hh + 1]
        g = -jnp.exp(alog_ref[:, hh:hh + 1]) * _softplus(a_raw + dtb_ref[:, hh:hh + 1])
        gb_ref[:, hs] = jnp.broadcast_to(g, (ts, d))
        bb_ref[:, hs] = jnp.broadcast_to(_sigmoid(b_raw), (ts, d))

        q = [q_ref[r, hs] for r in rs]
        k = [k_ref[r, hs] for r in rs]
        beta = [bb_ref[r, hs] for r in rs]
        gb = [gb_ref[r, hs] for r in rs]
        gc = [cum2(tri2, x) for x in gb]
        gc_row = [cum2(ones2, x[:, :c] * upper_f) for x in gb]
        decay = [jnp.where(tri, jnp.exp(jnp.minimum(a[:, :c] - b, 0.0)), 0.0) for a, b in zip(gc, gc_row)]
        kk = [_mm_nt(x, x) for x in k]
        n = [-jnp.where(strict, b[:, :c] * x * dc, 0.0) for b, x, dc in zip(beta, kk, decay)]
        inv = [eye + x for x in n]
        n_parts = [_x3_parts(x) for x in n]
        for step in range(5):
            n = [jnp.dot(a, b, preferred_element_type=F32) for a, b in n_parts]
            n_parts = [_x3_parts(x) for x in n]
            inv = [iv + jnp.dot(_x3_parts(iv)[0], b, preferred_element_type=F32)
                   for iv, (_, b) in zip(inv, n_parts)]
        egc = [jnp.exp(x) for x in gc]
        gl = [x[c - 1:c, :] for x in gc]
        for ci in chunks:
            r = rs[ci]
            u_ref[r, hs] = _mm_x3(inv[ci], v_ref[r, hs] * beta[ci])
            w_ref[r, hs] = _mm_x3(inv[ci], k[ci] * (beta[ci] * egc[ci]))
            qk_ref[hh, r, :] = _mm_nt(q[ci], k[ci]) * decay[ci]
            q_ref[r, hs] = q[ci] * egc[ci]
            k_ref[r, hs] = k[ci] * jnp.exp(gl[ci] - gc[ci])
            gb_ref[r, hs] = jnp.broadcast_to(jnp.exp(gl[ci]), (c, d))

    def chunk_body(ci, carry):
        r0 = pl.multiple_of(ci * c, c)
        for hh in range(nh):
            hs = slice(hh * d, (hh + 1) * d)
            st = st_ref[hh]
            v_new = u_ref[pl.ds(r0, c), hs] - _mm(w_ref[pl.ds(r0, c), hs], st)
            o = _mm(q_ref[pl.ds(r0, c), hs], st) + _mm(qk_ref[hh, pl.ds(r0, c), :], v_new)
            st_ref[hh] = st * gb_ref[pl.ds(r0, 1), hs] + _mm_tn(k_ref[pl.ds(r0, c), hs], v_new)
            o_ref[pl.ds(r0, c), hs] = _rms(o, gnorm) * _silu(z_ref[pl.ds(r0, c), hs])
        return carry

    lax.fori_loop(0, ts // c, chunk_body, 0)


def _deltanet(p32, conv_w, a_log, dt_bias, a_norm_g, *, ts, cols):
    bsz, s, _ = p32.shape
    d = HEAD_DIM
    nh = N_HEADS
    w = nh * d
    pad = lambda t: jnp.pad(t.astype(F32), (0, d - t.shape[0])).reshape(1, d)
    kernel = functools.partial(_deltanet_kernel, ts=ts, a_col=cols["a_lane"], b_col=cols["b_lane"])
    tile = lambda name: pl.BlockSpec((None, ts, w), lambda b, i: (b, i, cols[name] // nh))
    conv = lambda k: pl.BlockSpec((CONV_WIDTH, w), lambda b, i: (0, k))
    row = pl.BlockSpec((1, d), lambda b, i: (0, 0))
    return pl.pallas_call(
        kernel,
        grid=(bsz, s // ts),
        in_specs=[tile("qa"), tile("ka"), tile("va"), tile("za"),
                  pl.BlockSpec((None, ts, d), lambda b, i: (b, i, cols["small"])),
                  conv(0), conv(1), conv(2), row, row, row],
        out_specs=pl.BlockSpec((None, ts, w), lambda b, i: (b, i, 0)),
        out_shape=jax.ShapeDtypeStruct((bsz, s, w), F32),
        scratch_shapes=[pltpu.VMEM((3, ts + 8, w), F32)]
        + [pltpu.VMEM((ts, w), F32) for _ in range(7)]
        + [pltpu.VMEM((nh, ts, CHUNK), F32), pltpu.VMEM((nh, d, d), F32)],
        compiler_params=pltpu.CompilerParams(
            dimension_semantics=("parallel", "arbitrary"), vmem_limit_bytes=VMEM_LIMIT),
        name="deltanet",
    )(p32, p32, p32, p32, p32, conv_w.astype(F32), conv_w.astype(F32), conv_w.astype(F32),
      pad(a_log), pad(dt_bias), a_norm_g.astype(F32).reshape(1, d))


def _hgrn2_kernel(q_ref, f_ref, i_ref, gate_ref, lb_ref, gn_ref, o_ref,
                  qs_ref, ks_ref, gc_ref, st_ref, *, ts):
    s = pl.program_id(1)
    c = CHUNK
    d = HEAD_DIM
    nh = N_HEADS
    SUB = 16

    @pl.when(s == 0)
    def _():
        st_ref[...] = jnp.zeros_like(st_ref)

    lb = lb_ref[...]
    f_raw = f_ref[...]
    log_sig = jnp.minimum(f_raw, 0.0) - jnp.log1p(jnp.exp(-jnp.abs(f_raw)))
    la = jnp.log(lb)
    lbb = jnp.log1p(-lb) + log_sig
    log_f = jnp.maximum(la, lbb) + jnp.log1p(jnp.exp(-jnp.abs(la - lbb)))
    qs_ref[...] = _silu(q_ref[...])
    ks_ref[...] = (1.0 - lb) * _sigmoid(-f_raw)

    row = _iota((c, c), 0)
    col = _iota((c, c), 1)
    tri_f = (col <= row).astype(F32)
    ones_dd = jnp.ones((d, d), BF16)
    rows_8d = _iota((8, d), 0)
    gnorm = gn_ref[...]

    for ci in range(ts // c):
        gc_ref[ci * c:(ci + 1) * c, :] = _mm_f32(tri_f, log_f[ci * c:(ci + 1) * c, :])

    blocks = [(sb * SUB, (sb + 1) * SUB) for sb in range(c // SUB)]

    def head_chunk(r0, hh):
        hs = slice(hh * d, (hh + 1) * d)
        q = qs_ref[pl.ds(r0, c), hs]
        k = ks_ref[pl.ds(r0, c), hs]
        v = i_ref[pl.ds(r0, c), hs]
        gc = gc_ref[pl.ds(r0, c), hs]

        prods = []
        for top, end in blocks:
            for j in range(top, end):
                lo = (j // 8) * 8
                k_j = k[j:j + 1, :]
                g_j = gc[j:j + 1, :]
                e = jnp.exp(jnp.minimum(gc[lo:end, :] - g_j, 0.0))
                if j % 8:
                    head = jnp.where(rows_8d >= j - lo, e[:8], 0.0)
                    e = jnp.concatenate([head, e[8:]], axis=0) if lo + 8 < end else head
                prods.append(q[lo:end, :] * k_j * e)
        sums = jnp.dot(jnp.concatenate(prods, axis=0).astype(BF16), ones_dd,
                       preferred_element_type=F32)
        qk_far = []
        for top, end in blocks[1:]:
            g_b = gc[top - 1:top, :]
            qe = q[top:end, :] * jnp.exp(gc[top:end, :] - g_b)
            ke = k[:top, :] * jnp.exp(jnp.minimum(g_b - gc[:top, :], 0.0))
            qk_far.append(_mm_nt(qe, ke))
        far = [_mm(a, v[:top, :]) for a, (top, _) in zip(qk_far, blocks[1:])]

        groups = [jnp.zeros((8, d), F32) for _ in range(c // 8)]
        at = 0
        for top, end in blocks:
            for j in range(top, end):
                v_j = v[j:j + 1, :]
                for g in range(j // 8, end // 8):
                    groups[g] = groups[g] + sums[at:at + 8, :] * v_j
                    at += 8
        for f, (top, end) in zip(far, blocks[1:]):
            for g in range(top // 8, end // 8):
                groups[g] = groups[g] + f[(g * 8 - top):(g * 8 - top + 8), :]
        o_intra = jnp.concatenate(groups, axis=0)

        st = st_ref[hh]
        gl = gc[c - 1:c, :]
        o = o_intra + _mm_nt(q * jnp.exp(gc), st)
        st_ref[hh] = st * jnp.exp(gl) + _mm_tn(v, k * jnp.exp(gl - gc))
        o_ref[pl.ds(r0, c), hs] = _rms(o, gnorm) * _silu(gate_ref[pl.ds(r0, c), hs])

    def chunk_loop(ci, carry):
        r0 = pl.multiple_of(ci * c, c)
        for hh in range(nh):
            head_chunk(r0, hh)
        return carry

    lax.fori_loop(0, ts // c, chunk_loop, 0)


def _hgrn2(p32, lb, d_norm_g, *, ts, cols):
    bsz, s, _ = p32.shape
    d = HEAD_DIM
    nh = N_HEADS
    w = nh * d
    kernel = functools.partial(_hgrn2_kernel, ts=ts)
    tile = lambda name: pl.BlockSpec((None, ts, w), lambda b, i: (b, i, cols[name] // nh))
    return pl.pallas_call(
        kernel,
        grid=(bsz, s // ts),
        in_specs=[tile("qd"), tile("fd"), tile("id"), tile("gd"),
                  pl.BlockSpec((1, w), lambda b, i: (0, 0)),
                  pl.BlockSpec((1, d), lambda b, i: (0, 0))],
        out_specs=pl.BlockSpec((None, ts, w), lambda b, i: (b, i, 0)),
        out_shape=jax.ShapeDtypeStruct((bsz, s, w), F32),
        scratch_shapes=[pltpu.VMEM((ts, w), F32), pltpu.VMEM((ts, w), F32),
                        pltpu.VMEM((ts, w), F32), pltpu.VMEM((nh, d, d), F32)],
        compiler_params=pltpu.CompilerParams(
            dimension_semantics=("parallel", "arbitrary"), vmem_limit_bytes=VMEM_LIMIT),
        name="hgrn2",
    )(p32, p32, p32, p32, lb.astype(F32).reshape(1, w), d_norm_g.astype(F32).reshape(1, d))


def _stickbreak_kernel(q_ref, k_ref, v_ref, o_ref, *, tq):
    i = pl.program_id(1)
    d = HEAD_DIM
    nh = N_HEADS
    row = _iota((tq, tq), 0)
    col = _iota((tq, tq), 1)
    causal = col < row
    later = (row > col).astype(BF16)
    later2 = jnp.concatenate([later, later], axis=0)

    def scores(j, hs, diag):
        z = _mm_nt(q_ref[:, hs], k_ref[pl.ds(pl.multiple_of(j * tq, tq), tq), hs]) * (d ** -0.5)
        sp = _softplus(z)
        l1m = jnp.where(causal, -sp, 0.0) if diag else -sp
        rest = jnp.dot(jnp.concatenate(_split(l1m), axis=1), later2,
                       preferred_element_type=F32)
        return (z - sp) + rest, l1m

    def block(j, carries):
        out = []
        for hh in range(nh):
            hs = slice(hh * d, (hh + 1) * d)
            logw, l1m = scores(j, hs, False)
            o_ref[:, hs] += _mm(jnp.exp(logw + carries[hh]), v_ref[pl.ds(pl.multiple_of(j * tq, tq), tq), hs])
            out.append(carries[hh] + jnp.sum(l1m, axis=-1, keepdims=True))
        return tuple(out)

    jp = jnp.maximum(i - 1, 0)
    live = jnp.where(i > 0, 1.0, 0.0)
    heads = [slice(hh * d, (hh + 1) * d) for hh in range(nh)]
    sd = [scores(i, hs, True) for hs in heads]
    sp_ = [scores(jp, hs, False) for hs in heads]
    carries = []
    for hh, hs in enumerate(heads):
        c1 = jnp.sum(sd[hh][1], axis=-1, keepdims=True)
        p_d = jnp.where(causal, jnp.exp(sd[hh][0]), 0.0)
        p_p = jnp.exp(sp_[hh][0] + c1) * live
        o_ref[:, hs] = (_mm(p_d, v_ref[pl.ds(pl.multiple_of(i * tq, tq), tq), hs])
                        + _mm(p_p, v_ref[pl.ds(pl.multiple_of(jp * tq, tq), tq), hs]))
        carries.append(c1 + jnp.sum(sp_[hh][1], axis=-1, keepdims=True))
    carries = tuple(carries)

    def cond(c):
        worst = functools.reduce(jnp.maximum, c[1])
        return jnp.logical_and(c[0] >= 0, jnp.max(worst) >= EXP_ZERO_BELOW)

    def body(c):
        return c[0] - 1, block(c[0], c[1])

    lax.while_loop(cond, body, (i - 2, carries))


def _stickbreak(p16, *, tq, cols):
    bsz, s, _ = p16.shape
    nh = N_HEADS
    w = nh * HEAD_DIM
    kernel = functools.partial(_stickbreak_kernel, tq=tq)
    resident = dict(pipeline_mode=pl.Buffered(1))
    return pl.pallas_call(
        kernel,
        grid=(bsz, s // tq),
        in_specs=[pl.BlockSpec((None, tq, w), lambda b, i: (b, i, cols["qc"] // nh)),
                  pl.BlockSpec((None, s, w), lambda b, i: (b, 0, cols["kc"] // nh), **resident),
                  pl.BlockSpec((None, s, w), lambda b, i: (b, 0, cols["vc"] // nh), **resident)],
        out_specs=pl.BlockSpec((None, tq, w), lambda b, i: (b, i, 0)),
        out_shape=jax.ShapeDtypeStruct((bsz, s, w), F32),
        compiler_params=pltpu.CompilerParams(
            dimension_semantics=("parallel", "arbitrary"), vmem_limit_bytes=VMEM_LIMIT),
        name="stickbreak",
    )(p16, p16, p16)


def _dsa_kernel(qi_ref, smq_ref, q_ref, sm_ref, k_ref, vt_ref, bias_ref, o_ref,
                sc_ref, scb_ref, wb_ref, qc_ref, kct_ref, bd_ref, lg_ref, *, tq, k_sel, wi_lane, wide):
    i = pl.program_id(1)
    tk = tq
    d = HEAD_DIM
    nh = N_HEADS
    ksel = float(k_sel)
    per_wide = wide // tk
    n_wide = (i + per_wide) // per_wide
    sub = 2 * tk
    lane_q = _iota((1, tq), 1)

    def tree(parts, op):
        while len(parts) > 1:
            parts = [op(parts[j], parts[j + 1]) if j + 1 < len(parts) else parts[j]
                     for j in range(0, len(parts), 2)]
        return parts[0]

    def col_fold(x, op=jnp.add, rows=8):
        return tree([x[r * rows:(r + 1) * rows] for r in range(x.shape[0] // rows)], op)

    @pl.when(i == 0)
    def _():
        def prep(g, carry):
            g0 = pl.multiple_of(g * wide, wide)
            kt = sm_ref[pl.ds(g0, wide), :].T[:IDX_DIM, :]
            hi, lo = _split(kt)
            kct_ref[:, pl.ds(g0, wide)] = jnp.concatenate([hi, lo, hi], axis=0)
            return carry
        lax.fori_loop(0, sm_ref.shape[0] // wide, prep, 0)

    smq = smq_ref[...]
    lane = _iota(smq.shape, 1)
    for hh in range(IDX_HEADS):
        qh = qi_ref[:, hh * IDX_DIM:(hh + 1) * IDX_DIM]
        hi, lo = _split(qh)
        qc_ref[hh] = jnp.concatenate([hi, hi, lo], axis=-1)
        w = jnp.sum(jnp.where(lane == wi_lane + hh, smq, 0.0), axis=-1, keepdims=True)
        wb_ref[hh] = jnp.broadcast_to(w * ((IDX_HEADS ** -0.5) * (IDX_DIM ** -0.5)), (tq, tk))

    q2t = (q_ref[...] * ((d ** -0.5) * LOG2E)).T.astype(BF16)
    zero_dq = jnp.zeros((d, tq), BF16)
    for p in range(nh // 2):
        top = jnp.concatenate([q2t[2 * p * d:(2 * p + 1) * d], zero_dq], axis=1)
        bot = jnp.concatenate([zero_dq, q2t[(2 * p + 1) * d:(2 * p + 2) * d]], axis=1)
        bd_ref[p] = jnp.concatenate([top, bot], axis=0)

    limit = i * tq + (lane_q // CHUNK + 1) * CHUNK
    rows_t = _iota((tk, tq), 0)

    def score_group(g, mm, masked):
        mn, mx = mm
        for sb in range(wide // sub):
            k0 = pl.multiple_of(g * wide + sb * sub, sub)
            kct = kct_ref[:, pl.ds(k0, sub)]
            tiles = [jnp.zeros((tq, tk), F32) for _ in range(sub // tk)]
            for hh in range(IDX_HEADS):
                s_h = jnp.dot(qc_ref[hh], kct, preferred_element_type=F32)
                for ti in range(sub // tk):
                    tiles[ti] = tiles[ti] + jnp.maximum(s_h[:, ti * tk:(ti + 1) * tk], 0.0) * wb_ref[hh]
            for ti in range(sub // tk):
                kb = pl.multiple_of(k0 + ti * tk, tk)
                sct = tiles[ti].T
                if masked:
                    adm = (kb + rows_t) < limit
                    mn = jnp.minimum(mn, col_fold(jnp.where(adm, sct, jnp.inf), jnp.minimum))
                    sct = jnp.where(adm, sct, -jnp.inf)
                else:
                    mn = jnp.minimum(mn, col_fold(sct, jnp.minimum))
                mx = jnp.maximum(mx, col_fold(sct, jnp.maximum))
                sc_ref[pl.ds(kb, tk), :] = sct
                scb_ref[pl.ds(kb, tk), :] = _floor_bf16(sct)
        return mn, mx

    def score_pair(j, mm):
        return score_group(2 * j + 1, score_group(2 * j, mm, False), False)

    n_full = n_wide - 1
    mm = lax.fori_loop(0, n_full // 2, score_pair,
                       (jnp.full((8, tq), jnp.inf, F32), jnp.full((8, tq), -jnp.inf, F32)))
    mm = lax.cond(n_full % 2 == 1, lambda c: score_group(n_full - 1, c, False), lambda c: c, mm)
    mn, mx = score_group(n_wide - 1, mm, True)

    n_pairs = (n_wide + 1) // 2

    @pl.when(n_wide % 2 == 1)
    def _():
        sc_ref[pl.ds(pl.multiple_of(n_wide * wide, wide), wide), :] = jnp.full((wide, tq), -jnp.inf, F32)
    rmin = jnp.min(mn, axis=0, keepdims=True)
    rmax = jnp.max(mx, axis=0, keepdims=True)

    def count(pred):
        def body(g, acc):
            blk = sc_ref[pl.ds(pl.multiple_of(g * wide, wide), wide), :]
            return acc + col_fold(pred(blk))
        return jnp.sum(lax.fori_loop(0, n_wide, body, jnp.zeros((8, tq), F32)), axis=0, keepdims=True)

    def max_below(x):
        def body(g, acc):
            blk = sc_ref[pl.ds(pl.multiple_of(g * wide, wide), wide), :]
            return jnp.maximum(acc, col_fold(jnp.where(blk < x, blk, -jnp.inf), jnp.maximum))
        return jnp.max(lax.fori_loop(0, n_wide, body, jnp.full((8, tq), -jnp.inf, F32)), axis=0, keepdims=True)

    n_adm = limit.astype(F32)
    all_sel = n_adm <= ksel

    def bisect(c):
        lo, hi, c_lo = c
        mid = 0.5 * lo + 0.5 * hi
        cm = count(lambda blk: _ind(blk >= mid))
        ge = cm >= ksel
        return jnp.where(ge, mid, lo), jnp.where(ge, hi, mid), jnp.where(ge, cm, c_lo)

    def pending(c_lo, tied):
        return jnp.where(all_sel, 0.0, jnp.where(tied > 0.5, 0.0, _ind(c_lo != ksel)))

    def bisect_coarse(_, c):
        lo, hi, c_lo = c
        mid = _floor_bf16(0.5 * lo + 0.5 * hi).astype(F32)
        t_b = jnp.broadcast_to(mid, (16, tq)).astype(BF16)
        one_b = jnp.ones((16, tq), BF16)
        zero_b = jnp.zeros((16, tq), BF16)

        def body(g, acc):
            blk = scb_ref[pl.ds(pl.multiple_of(g * wide, wide), wide), :]
            ind = [jnp.where(blk[r * 16:(r + 1) * 16] >= t_b, one_b, zero_b) for r in range(wide // 16)]
            return acc + tree(ind, jnp.add).astype(F32)

        acc = lax.fori_loop(0, n_wide, body, jnp.zeros((16, tq), F32))
        cm = jnp.sum(acc, axis=0, keepdims=True)
        ge = cm >= ksel
        return jnp.where(ge, mid, lo), jnp.where(ge, hi, mid), jnp.where(ge, cm, c_lo)

    lo0 = _floor_bf16(rmin).astype(F32)
    hi0 = _floor_bf16(rmax + (jnp.abs(rmax) * (2.0 ** -6) + 1e-30)).astype(F32)
    state = lax.fori_loop(0, BISECT_COARSE, bisect_coarse, (lo0, hi0, n_adm))
    state = lax.fori_loop(0, BISECT_FIXED, lambda _, c: bisect(c), state)

    def round_cond(c):
        return jnp.max(pending(c[0][2], c[1])) > 0.5

    def round_body(c):
        st, tied, v, need = c

        def more_cond(s):
            return jnp.logical_and(s[0] < BISECT_EXTRA, jnp.max(pending(s[1][2], tied)) > 0.5)

        _, st = lax.while_loop(more_cond, lambda s: (s[0] + 1, bisect(s[1])), (jnp.int32(0), st))
        pend = pending(st[2], tied)

        def check(_):
            cand = max_below(st[1])
            c_ge = count(lambda blk: _ind(blk >= cand))
            c_gt = count(lambda blk: _ind(blk > cand))
            ok = jnp.where(pend > 0.5, _ind(c_ge >= ksel), 0.0)
            return (jnp.where(ok > 0.5, 1.0, tied), jnp.where(ok > 0.5, cand, v),
                    jnp.where(ok > 0.5, ksel - c_gt, need))

        tied, v, need = lax.cond(jnp.max(pend) > 0.5, check, lambda _: (tied, v, need), 0)
        return st, tied, v, need

    zeros1 = jnp.zeros((1, tq), F32)
    (lo_f, _, _), tied, v_tie, need = lax.while_loop(round_cond, round_body, (state, zeros1, zeros1, zeros1))
    vth = jnp.where(all_sel, F32_LOWEST, jnp.where(tied > 0.5, v_tie, lo_f))

    @pl.when(jnp.max(tied) > 0.5)
    def _():
        v_eq = jnp.where(tied > 0.5, v_tie, jnp.inf)
        incl = (_iota((tk, tk), 1) <= _iota((tk, tk), 0)).astype(BF16)

        def demote(g, seen):
            g0 = pl.multiple_of(g * wide, wide)
            xs = [sc_ref[pl.ds(g0 + pb * tk, tk), :] for pb in range(per_wide)]
            eqs = [_ind(x == v_eq) for x in xs]
            inblk = [jnp.dot(incl, e.astype(BF16), preferred_element_type=F32) for e in eqs]
            for pb in range(per_wide):
                rank = inblk[pb] + seen
                sc_ref[pl.ds(g0 + pb * tk, tk), :] = jnp.where(eqs[pb] * _ind(rank > need) > 0.5,
                                                               -jnp.inf, xs[pb])
                seen = seen + jnp.sum(col_fold(eqs[pb]), axis=0, keepdims=True)
            return seen

        lax.fori_loop(0, n_wide, demote, zeros1)

    g_near = jnp.maximum(i - 1, 0) // per_wide

    def logit_group(g, mx, near):
        out = list(mx)
        for sb in range(wide // sub):
            k0 = pl.multiple_of(g * wide + sb * sub, sub)
            sel = sc_ref[pl.ds(k0, sub), :] >= vth
            for p in range(nh // 2):
                pair = jnp.dot(k_ref[pl.ds(k0, sub), 2 * p * d:(2 * p + 2) * d], bd_ref[p],
                               preferred_element_type=F32)
                for hh in (2 * p, 2 * p + 1):
                    lm = pair[:, (hh - 2 * p) * tq:(hh - 2 * p + 1) * tq]
                    if near:
                        back = [jnp.clip(i - (g * per_wide + sb * (sub // tk) + pb), 0, 2)
                                for pb in range(sub // tk)]
                        lm = lm + jnp.concatenate([bias_ref[bk, hh] for bk in back], axis=0)
                    lm = jnp.where(sel, lm, NEG_BIG)
                    lg_ref[hh, pl.ds(k0, sub), :] = lm
                    out[hh] = jnp.maximum(out[hh], col_fold(lm, jnp.maximum))
        return tuple(out)

    mx = tuple(jnp.full((8, tq), NEG_BIG, F32) for _ in range(nh))
    def logit_pair(j, mx, near):
        return logit_group(2 * j + 1, logit_group(2 * j, mx, near), near)

    far_pairs = g_near // 2
    mx = lax.fori_loop(0, far_pairs, functools.partial(logit_pair, near=False), mx)
    mx = lax.fori_loop(far_pairs, n_pairs, functools.partial(logit_pair, near=True), mx)
    m_q = [jnp.max(mx[hh], axis=0, keepdims=True) for hh in range(nh)]

    def pv_body(g, carry):
        g0 = pl.multiple_of(g * wide, wide)
        ls, accs = carry
        new_l, new_a = [], []
        for hh in range(nh):
            p = jnp.exp2(lg_ref[hh, pl.ds(g0, wide), :] - m_q[hh])
            lhs = jnp.concatenate([vt_ref[hh * d:(hh + 1) * d, pl.ds(g0, wide)], ones_rows], axis=0)
            out = jnp.dot(lhs, p.astype(BF16), preferred_element_type=F32)
            new_l.append(ls[hh] + out[d:])
            new_a.append(accs[hh] + out[:d])
        return tuple(new_l), tuple(new_a)

    ones_rows = jnp.ones((8, wide), BF16)

    ls, accs = lax.fori_loop(0, n_pairs, lambda j, c: pv_body(2 * j + 1, pv_body(2 * j, c)),
                             (tuple(jnp.zeros((8, tq), F32) for _ in range(nh)),
                              tuple(jnp.zeros((d, tq), F32) for _ in range(nh))))
    for hh in range(nh):
        o_ref[:, hh * d:(hh + 1) * d] = (accs[hh] / ls[hh][0:1]).T


def _dsa(p32, p16, vt, bias_tiles, *, tq, cols):
    bsz, s, _ = p32.shape
    d = HEAD_DIM
    nh = N_HEADS
    wide = 4 * tq
    k_sel = min(TOPK_MAX, s // 4)
    w512 = nh * d
    kernel = functools.partial(_dsa_kernel, tq=tq, k_sel=k_sel, wi_lane=cols["wi_lane"], wide=wide)
    resident = dict(pipeline_mode=pl.Buffered(1))
    return pl.pallas_call(
        kernel,
        grid=(bsz, s // tq),
        in_specs=[pl.BlockSpec((None, tq, w512), lambda b, i: (b, i, cols["qi"] // nh)),
                  pl.BlockSpec((None, tq, d), lambda b, i: (b, i, cols["small"])),
                  pl.BlockSpec((None, tq, w512), lambda b, i: (b, i, cols["qb"] // nh)),
                  pl.BlockSpec((None, s, d), lambda b, i: (b, 0, cols["small"]), **resident),
                  pl.BlockSpec((None, s, w512), lambda b, i: (b, 0, cols["kb"] // nh), **resident),
                  pl.BlockSpec((w512, s), lambda b, i: (0, b), **resident),
                  pl.BlockSpec((3, nh, tq, tq), lambda b, i: (0, 0, 0, 0), **resident)],
        out_specs=pl.BlockSpec((None, tq, w512), lambda b, i: (b, i, 0)),
        out_shape=jax.ShapeDtypeStruct((bsz, s, w512), F32),
        scratch_shapes=[pltpu.VMEM((s, tq), F32),
                        pltpu.VMEM((s, tq), BF16),
                        pltpu.VMEM((IDX_HEADS, tq, tq), F32),
                        pltpu.VMEM((IDX_HEADS, tq, 3 * IDX_DIM), BF16),
                        pltpu.VMEM((3 * IDX_DIM, s), BF16),
                        pltpu.VMEM((nh // 2, 2 * d, 2 * tq), BF16),
                        pltpu.VMEM((nh, s, tq), F32)],
        compiler_params=pltpu.CompilerParams(
            dimension_semantics=("parallel", "arbitrary"), vmem_limit_bytes=VMEM_LIMIT),
        name="dsa",
    )(p32, p32, p32, p32, p16, vt, bias_tiles)


def _t5_bucket(rel):
    nb = REL_BUCKETS // 2
    max_exact = nb // 2
    ret = jnp.where(rel > 0, nb, 0)
    n = jnp.abs(rel)
    large = max_exact + (jnp.log(jnp.maximum(n, 1).astype(F32) / max_exact)
                         / math.log(REL_MAX_DIST / max_exact) * (nb - max_exact)).astype(jnp.int32)
    large = jnp.minimum(large, nb - 1)
    return ret + jnp.where(n < max_exact, n, large)


def _bias_tiles(rel_table, tq):
    assert tq >= REL_MAX_DIST
    t = jnp.arange(tq)
    back = jnp.arange(3)
    rel = (t[None, None, :] - back[:, None, None] * tq) - t[None, :, None]
    onehot = (_t5_bucket(rel)[..., None] == jnp.arange(REL_BUCKETS)).astype(F32)
    tiles = jnp.einsum("bqkn,nh->bhkq", onehot, rel_table.astype(F32),
                       precision=HIGHEST)
    return (tiles - tiles[2:3]) * LOG2E


def _even_layout(w_in):
    d = HEAD_DIM
    a_w = 2 * N_HEADS * d + N_HEADS * d
    offs = {}
    o = 0
    for name, w in (("qkv", a_w), ("z", N_HEADS * d), ("a", N_HEADS), ("b", N_HEADS),
                    ("qb", N_HEADS * d), ("kb", N_HEADS * d), ("vb", N_HEADS * d),
                    ("qi", IDX_HEADS * IDX_DIM), ("ki", IDX_DIM), ("wi", IDX_HEADS)):
        offs[name] = (o, o + w)
        o += w
    assert o == w_in.shape[1]
    sl = lambda n: w_in[:, offs[n][0]:offs[n][1]]
    small_w = IDX_DIM + 2 * N_HEADS + IDX_HEADS
    small_pad = -small_w % d
    zeros = lambda n: jnp.zeros((w_in.shape[0], n), w_in.dtype)
    w32 = jnp.concatenate([sl("qkv"), sl("z"), sl("qb"), sl("qi"),
                           sl("ki"), sl("a"), sl("b"), sl("wi"), zeros(small_pad)], axis=1)
    n32 = w32.shape[1]
    tn = n32 // 5
    assert tn * 5 == n32 and tn % d == 0
    w16 = jnp.concatenate([sl("kb"), zeros(tn - N_HEADS * d)], axis=1)
    nh = N_HEADS
    cols = dict(qa=0, ka=nh, va=2 * nh, za=3 * nh, qb=4 * nh, qi=5 * nh, small=6 * nh, kb=0,
                a_lane=IDX_DIM, b_lane=IDX_DIM + nh, wi_lane=IDX_DIM + 2 * nh, n32=n32, tn=tn)
    return jnp.concatenate([w32, w16], axis=1).astype(BF16), sl("vb").T.astype(BF16), cols


def kernel(x, norm_g, w_in_even, conv_w_even, a_log_even, dt_bias_even, a_norm_even, w_out_even,
           rel_bias, w_in_odd, lb_logits, d_norm_odd, w_out_odd, w_gate, w_up, w_down):
    bsz, s, d = x.shape
    t = bsz * s
    depth = norm_g.shape[0]
    nh = N_HEADS
    tq = 128
    lb_all = jnp.cumsum(jax.nn.softmax(lb_logits.astype(F32), axis=0), axis=0)
    lb_all = lb_all - lb_all[:1]
    odd_cols = dict(qc=0, kc=nh, vc=2 * nh, qd=0, fd=nh, id=2 * nh, gd=3 * nh)
    bias_tiles = _bias_tiles(rel_bias, tq)

    h = x.reshape(t, d)
    for l in range(depth):
        if l % 2 == 0:
            e = l // 2
            w_even, w_vt, cols = _even_layout(w_in_even[e])
            p32, p16, vt = _norm_matmul(h, norm_g[l, 0], w_even, tm=1024, tn=cols["tn"], n32=cols["n32"],
                                        w_t=w_vt)
            p32 = p32.reshape(bsz, s, -1)
            p16 = p16.reshape(bsz, s, -1)
            o_1 = _deltanet(p32, conv_w_even[e], a_log_even[e], dt_bias_even[e], a_norm_even[e],
                            ts=min(512, s), cols=cols)
            o_2 = _dsa(p32, p16, vt, bias_tiles, tq=tq, cols=cols)
            w_out = w_out_even[e]
        else:
            o = l // 2
            n16 = 3 * nh * HEAD_DIM
            w_odd = jnp.concatenate([w_in_odd[o][:, n16:], w_in_odd[o][:, :n16]], axis=1).astype(BF16)
            p32, p16 = _norm_matmul(h, norm_g[l, 0], w_odd, tm=1024, tn=512, n32=w_odd.shape[1] - n16)
            p32 = p32.reshape(bsz, s, -1)
            p16 = p16.reshape(bsz, s, -1)
            o_1 = _stickbreak(p16, tq=tq, cols=odd_cols)
            o_2 = _hgrn2(p32, lb_all[l], d_norm_odd[o], ts=min(512, s), cols=odd_cols)
            w_out = w_out_odd[o]
        h = _outproj(o_1.reshape(t, -1), o_2.reshape(t, -1), w_out, h, norm_g[l, 1], tm=512)
        h = _ffn(h, norm_g[l, 2], norm_g[l, 3], w_gate[l], w_up[l], w_down[l], tm=1024, tf=256)
    return h.reshape(bsz, s, d)
```

```python
import functools
import math

import jax
import jax.numpy as jnp
from jax import lax
from jax.experimental import pallas as pl
from jax.experimental.pallas import tpu as pltpu

F32 = jnp.float32
BF16 = jnp.bfloat16
HIGHEST = lax.Precision.HIGHEST

CHUNK = 64
HEAD_DIM = 128
N_HEADS = 4
IDX_HEADS = 8
IDX_DIM = 64
TOPK_MAX = 256
CONV_WIDTH = 4
REL_BUCKETS = 32
REL_MAX_DIST = 128
EPS = 1e-6
NEG_BIG = -1e30
LOG2E = 1.4426950408889634
BISECT_COARSE = 12
BISECT_FIXED = 8
BISECT_EXTRA = 6
F32_LOWEST = -3.4028234663852886e38
EXP_ZERO_BELOW = -104.0
VMEM_LIMIT = 56 * 1024 * 1024

ROW_TILE = 1024
OUT_TILE = 512
SEQ_TILE = 512
Q_TILE = 128
ODD_COL_TILE = 512
FFN_TILE = 256


def _mm(a, b):
    return jnp.dot(a.astype(BF16), b.astype(BF16), preferred_element_type=F32)


def _mm_nt(a, b):
    return lax.dot_general(a.astype(BF16), b.astype(BF16), (((1,), (1,)), ((), ())),
                           preferred_element_type=F32)


def _mm_tn(a, b):
    return lax.dot_general(a.astype(BF16), b.astype(BF16), (((0,), (0,)), ((), ())),
                           preferred_element_type=F32)


def _split(x):
    hi = x.astype(BF16)
    return hi, (x - hi.astype(F32)).astype(BF16)


def _x3_parts(x):
    hi, lo = _split(x)
    return jnp.concatenate([hi, hi, lo], axis=1), jnp.concatenate([hi, lo, hi], axis=0)


def _mm_x3(a, b):
    return jnp.dot(_x3_parts(a)[0], _x3_parts(b)[1], preferred_element_type=F32)


def _floor_bf16(x):
    bits = pltpu.bitcast(x, jnp.int32)
    down = jnp.where(bits >= 0, bits, bits + 0xFFFF) & jnp.int32(-65536)
    return pltpu.bitcast(down, F32).astype(BF16)


def _sigmoid(x):
    return 1.0 / (1.0 + jnp.exp(-x))


def _silu(x):
    return x * _sigmoid(x)


def _softplus(x):
    return jnp.maximum(x, 0.0) + jnp.log1p(jnp.exp(-jnp.abs(x)))


def _rms(x, g):
    return x * lax.rsqrt(jnp.mean(x * x, axis=-1, keepdims=True) + EPS) * g


def _iota(shape, dim):
    return lax.broadcasted_iota(jnp.int32, shape, dim)


def _ind(mask):
    return jnp.where(mask, 1.0, 0.0)


def _norm_matmul_kernel(x_ref, g_ref, w_ref, *rest, n_t, tiles32):
    if n_t:
        wt_ref, o32_ref, o16_ref, ot_ref, xn_ref = rest
    else:
        o32_ref, o16_ref, xn_ref = rest
    j = pl.program_id(1)

    @pl.when(j == 0)
    def _():
        xn_ref[...] = _rms(x_ref[...], g_ref[...]).astype(BF16)
        if n_t:
            ot_ref[...] = lax.dot_general(wt_ref[...], xn_ref[...], (((1,), (1,)), ((), ())),
                                          preferred_element_type=F32).astype(BF16)

    y = jnp.dot(xn_ref[...], w_ref[...], preferred_element_type=F32)

    @pl.when(j < tiles32)
    def _():
        o32_ref[...] = y

    @pl.when(j >= tiles32)
    def _():
        o16_ref[...] = y.astype(BF16)


def _norm_matmul(x, g, w, *, tm, tn, n32, w_t=None):
    t, d = x.shape
    n = w.shape[1]
    n_t = 0 if w_t is None else w_t.shape[0]
    tiles32 = n32 // tn
    assert tiles32 * tn == n32 and (n - n32) % tn == 0 and 0 < n32 < n
    in_specs = [pl.BlockSpec((tm, d), lambda i, j: (i, 0)),
                pl.BlockSpec((1, d), lambda i, j: (0, 0)),
                pl.BlockSpec((d, tn), lambda i, j: (0, j))]
    out_specs = [pl.BlockSpec((tm, tn), lambda i, j: (i, jnp.minimum(j, tiles32 - 1))),
                 pl.BlockSpec((tm, tn), lambda i, j: (i, jnp.maximum(j - tiles32, 0)))]
    out_shape = [jax.ShapeDtypeStruct((t, n32), F32), jax.ShapeDtypeStruct((t, n - n32), BF16)]
    args = [x, g.reshape(1, d), w]
    if n_t:
        in_specs.append(pl.BlockSpec((n_t, d), lambda i, j: (0, 0)))
        out_specs.append(pl.BlockSpec((n_t, tm), lambda i, j: (0, i)))
        out_shape.append(jax.ShapeDtypeStruct((n_t, t), BF16))
        args.append(w_t)
    return pl.pallas_call(
        functools.partial(_norm_matmul_kernel, n_t=n_t, tiles32=tiles32),
        grid=(t // tm, n // tn),
        in_specs=in_specs,
        out_specs=out_specs,
        out_shape=out_shape,
        scratch_shapes=[pltpu.VMEM((tm, d), BF16)],
        compiler_params=pltpu.CompilerParams(
            dimension_semantics=("parallel", "arbitrary"), vmem_limit_bytes=VMEM_LIMIT),
        name="norm_matmul",
    )(*args)


def _outproj_kernel(ca_ref, cb_ref, wa_ref, wb_ref, h_ref, g_ref, o_ref):
    y = (jnp.dot(ca_ref[...].astype(BF16), wa_ref[...], preferred_element_type=F32)
         + jnp.dot(cb_ref[...].astype(BF16), wb_ref[...], preferred_element_type=F32))
    o_ref[...] = h_ref[...] + _rms(y, g_ref[...])


def _outproj(ca, cb, w, h, g, *, tm):
    t, d = h.shape
    wa_n = ca.shape[1]
    wb_n = cb.shape[1]
    wa = w[:wa_n].astype(BF16)
    wb = w[wa_n:].astype(BF16)
    return pl.pallas_call(
        _outproj_kernel,
        grid=(t // tm,),
        in_specs=[pl.BlockSpec((tm, wa_n), lambda i: (i, 0)),
                  pl.BlockSpec((tm, wb_n), lambda i: (i, 0)),
                  pl.BlockSpec((wa_n, d), lambda i: (0, 0)),
                  pl.BlockSpec((wb_n, d), lambda i: (0, 0)),
                  pl.BlockSpec((tm, d), lambda i: (i, 0)),
                  pl.BlockSpec((1, d), lambda i: (0, 0))],
        out_specs=pl.BlockSpec((tm, d), lambda i: (i, 0)),
        out_shape=jax.ShapeDtypeStruct((t, d), F32),
        compiler_params=pltpu.CompilerParams(
            dimension_semantics=("parallel",), vmem_limit_bytes=VMEM_LIMIT),
        name="outproj",
    )(ca, cb, wa, wb, h, g.reshape(1, d))


def _ffn_kernel(h_ref, gpre_ref, gpost_ref, wg_ref, wu_ref, wd_ref, o_ref, xn_ref, acc_ref):
    f = pl.program_id(1)

    @pl.when(f == 0)
    def _():
        xn_ref[...] = _rms(h_ref[...], gpre_ref[...]).astype(BF16)
        acc_ref[...] = jnp.zeros_like(acc_ref)

    xn = xn_ref[...]
    gate = jnp.dot(xn, wg_ref[...], preferred_element_type=F32)
    up = jnp.dot(xn, wu_ref[...], preferred_element_type=F32)
    act = (_silu(gate) * up).astype(BF16)
    acc_ref[...] += jnp.dot(act, wd_ref[...], preferred_element_type=F32)

    @pl.when(f == pl.num_programs(1) - 1)
    def _():
        o_ref[...] = h_ref[...] + _rms(acc_ref[...], gpost_ref[...])


def _ffn(h, g_pre, g_post, wg, wu, wd, *, tm, tf):
    t, d = h.shape
    ff = wg.shape[1]
    return pl.pallas_call(
        _ffn_kernel,
        grid=(t // tm, ff // tf),
        in_specs=[pl.BlockSpec((tm, d), lambda i, f: (i, 0)),
                  pl.BlockSpec((1, d), lambda i, f: (0, 0)),
                  pl.BlockSpec((1, d), lambda i, f: (0, 0)),
                  pl.BlockSpec((d, tf), lambda i, f: (0, f)),
                  pl.BlockSpec((d, tf), lambda i, f: (0, f)),
                  pl.BlockSpec((tf, d), lambda i, f: (f, 0))],
        out_specs=pl.BlockSpec((tm, d), lambda i, f: (i, 0)),
        out_shape=jax.ShapeDtypeStruct((t, d), F32),
        scratch_shapes=[pltpu.VMEM((tm, d), BF16), pltpu.VMEM((tm, d), F32)],
        compiler_params=pltpu.CompilerParams(
            dimension_semantics=("parallel", "arbitrary"), vmem_limit_bytes=VMEM_LIMIT),
        name="ffn",
    )(h, g_pre.reshape(1, d), g_post.reshape(1, d),
      wg.astype(BF16), wu.astype(BF16), wd.astype(BF16))


def _deltanet_kernel(xq_ref, xk_ref, xv_ref, z_ref, sm_ref, cwq_ref, cwk_ref, cwv_ref,
                     alog_ref, dtb_ref, gn_ref, o_ref,
                     xpad_ref, q_ref, k_ref, v_ref, gb_ref, bb_ref, u_ref, w_ref, qk_ref, st_ref,
                     *, ts, a_col, b_col):
    s = pl.program_id(1)
    c = CHUNK
    d = HEAD_DIM
    nh = N_HEADS

    @pl.when(s == 0)
    def _():
        xpad_ref[:, 0:8, :] = jnp.zeros((3, 8, nh * d), F32)
        st_ref[...] = jnp.zeros_like(st_ref)

    @pl.when(s != 0)
    def _():
        xpad_ref[:, 0:8, :] = xpad_ref[:, ts:ts + 8, :]

    xpad_ref[0, 8:ts + 8, :] = xq_ref[...]
    xpad_ref[1, 8:ts + 8, :] = xk_ref[...]
    xpad_ref[2, 8:ts + 8, :] = xv_ref[...]

    def conv_silu(idx, cw_ref, hs):
        cw = cw_ref[:, hs]
        acc = xpad_ref[idx, 8 - (CONV_WIDTH - 1):8 - (CONV_WIDTH - 1) + ts, hs] * cw[0:1, :]
        for j in range(1, CONV_WIDTH):
            off = 8 - (CONV_WIDTH - 1) + j
            acc = acc + xpad_ref[idx, off:off + ts, hs] * cw[j:j + 1, :]
        return _silu(acc)

    def l2norm(t):
        return t * lax.rsqrt(jnp.sum(t * t, axis=-1, keepdims=True) + EPS)

    row = _iota((c, c), 0)
    col = _iota((c, c), 1)
    tri = (col <= row)
    strict = (col < row)
    tri_f = tri.astype(F32)
    upper_f = (row <= col).astype(F32)
    eye = (row == col).astype(F32)
    gnorm = gn_ref[...]
    chunks = range(ts // c)
    rs = [slice(ci * c, (ci + 1) * c) for ci in chunks]
    tri2 = jnp.concatenate([tri_f, tri_f], axis=1).astype(BF16)
    ones2 = jnp.ones((c, 2 * c), BF16)

    def cum2(lhs2, x):
        hi, lo = _split(x)
        return jnp.dot(lhs2, jnp.concatenate([hi, lo], axis=0), preferred_element_type=F32)

    for hh in range(nh):
        hs = slice(hh * d, (hh + 1) * d)
        q_ref[:, hs] = l2norm(conv_silu(0, cwq_ref, hs)) * (d ** -0.5)
        k_ref[:, hs] = l2norm(conv_silu(1, cwk_ref, hs))
        v_ref[:, hs] = conv_silu(2, cwv_ref, hs)

        a_raw = sm_ref[:, a_col + hh:a_col + hh + 1]
        b_raw = sm_ref[:, b_col + hh:b_col + hh + 1]
        g = -jnp.exp(alog_ref[:, hh:hh + 1]) * _softplus(a_raw + dtb_ref[:, hh:hh + 1])
        gb_ref[:, hs] = jnp.broadcast_to(g, (ts, d))
        bb_ref[:, hs] = jnp.broadcast_to(_sigmoid(b_raw), (ts, d))

        q = [q_ref[r, hs] for r in rs]
        k = [k_ref[r, hs] for r in rs]
        beta = [bb_ref[r, hs] for r in rs]
        gb = [gb_ref[r, hs] for r in rs]
        gc = [cum2(tri2, x) for x in gb]
        gc_row = [cum2(ones2, x[:, :c] * upper_f) for x in gb]
        decay = [jnp.where(tri, jnp.exp(jnp.minimum(a[:, :c] - b, 0.0)), 0.0) for a, b in zip(gc, gc_row)]
        kk = [_mm_nt(x, x) for x in k]
        n = [-jnp.where(strict, b[:, :c] * x * dc, 0.0) for b, x, dc in zip(beta, kk, decay)]
        inv = [eye + x for x in n]
        n_parts = [_x3_parts(x) for x in n]
        for step in range(5):
            n = [jnp.dot(a, b, preferred_element_type=F32) for a, b in n_parts]
            n_parts = [_x3_parts(x) for x in n]
            inv = [iv + jnp.dot(_x3_parts(iv)[0], b, preferred_element_type=F32)
                   for iv, (_, b) in zip(inv, n_parts)]
        egc = [jnp.exp(x) for x in gc]
        gl = [x[c - 1:c, :] for x in gc]
        for ci in chunks:
            r = rs[ci]
            u_ref[r, hs] = _mm_x3(inv[ci], v_ref[r, hs] * beta[ci])
            w_ref[r, hs] = _mm_x3(inv[ci], k[ci] * (beta[ci] * egc[ci]))
            qk_ref[hh, r, :] = _mm_nt(q[ci], k[ci]) * decay[ci]
            q_ref[r, hs] = q[ci] * egc[ci]
            k_ref[r, hs] = k[ci] * jnp.exp(gl[ci] - gc[ci])
            gb_ref[r, hs] = jnp.broadcast_to(jnp.exp(gl[ci]), (c, d))

    def chunk_body(ci, carry):
        r0 = pl.multiple_of(ci * c, c)
        for hh in range(nh):
            hs = slice(hh * d, (hh + 1) * d)
            st = st_ref[hh]
            v_new = u_ref[pl.ds(r0, c), hs] - _mm(w_ref[pl.ds(r0, c), hs], st)
            o = _mm(q_ref[pl.ds(r0, c), hs], st) + _mm(qk_ref[hh, pl.ds(r0, c), :], v_new)
            st_ref[hh] = st * gb_ref[pl.ds(r0, 1), hs] + _mm_tn(k_ref[pl.ds(r0, c), hs], v_new)
            o_ref[pl.ds(r0, c), hs] = _rms(o, gnorm) * _silu(z_ref[pl.ds(r0, c), hs])
        return carry

    lax.fori_loop(0, ts // c, chunk_body, 0)


def _deltanet(p32, conv_w, a_log, dt_bias, a_norm_g, *, ts, cols):
    bsz, s, _ = p32.shape
    d = HEAD_DIM
    nh = N_HEADS
    w = nh * d
    pad = lambda t: jnp.pad(t.astype(F32), (0, d - t.shape[0])).reshape(1, d)
    kernel = functools.partial(_deltanet_kernel, ts=ts, a_col=cols["a_lane"], b_col=cols["b_lane"])
    tile = lambda name: pl.BlockSpec((None, ts, w), lambda b, i: (b, i, cols[name] // nh))
    conv = lambda k: pl.BlockSpec((CONV_WIDTH, w), lambda b, i: (0, k))
    row = pl.BlockSpec((1, d), lambda b, i: (0, 0))
    return pl.pallas_call(
        kernel,
        grid=(bsz, s // ts),
        in_specs=[tile("qa"), tile("ka"), tile("va"), tile("za"),
                  pl.BlockSpec((None, ts, d), lambda b, i: (b, i, cols["small"])),
                  conv(0), conv(1), conv(2), row, row, row],
        out_specs=pl.BlockSpec((None, ts, w), lambda b, i: (b, i, 0)),
        out_shape=jax.ShapeDtypeStruct((bsz, s, w), F32),
        scratch_shapes=[pltpu.VMEM((3, ts + 8, w), F32)]
        + [pltpu.VMEM((ts, w), F32) for _ in range(7)]
        + [pltpu.VMEM((nh, ts, CHUNK), F32), pltpu.VMEM((nh, d, d), F32)],
        compiler_params=pltpu.CompilerParams(
            dimension_semantics=("parallel", "arbitrary"), vmem_limit_bytes=VMEM_LIMIT),
        name="deltanet",
    )(p32, p32, p32, p32, p32, conv_w.astype(F32), conv_w.astype(F32), conv_w.astype(F32),
      pad(a_log), pad(dt_bias), a_norm_g.astype(F32).reshape(1, d))


def _hgrn2_kernel(q_ref, f_ref, i_ref, gate_ref, lb_ref, gn_ref, o_ref,
                  qs_ref, ks_ref, gc_ref, st_ref, *, ts):
    s = pl.program_id(1)
    c = CHUNK
    d = HEAD_DIM
    nh = N_HEADS
    SUB = 16

    @pl.when(s == 0)
    def _():
        st_ref[...] = jnp.zeros_like(st_ref)

    lb = lb_ref[...]
    f_raw = f_ref[...]
    log_sig = jnp.minimum(f_raw, 0.0) - jnp.log1p(jnp.exp(-jnp.abs(f_raw)))
    la = jnp.log(lb)
    lbb = jnp.log1p(-lb) + log_sig
    log_f = jnp.maximum(la, lbb) + jnp.log1p(jnp.exp(-jnp.abs(la - lbb)))
    qs_ref[...] = _silu(q_ref[...])
    ks_ref[...] = (1.0 - lb) * _sigmoid(-f_raw)

    row = _iota((c, c), 0)
    col = _iota((c, c), 1)
    tri_f = (col <= row).astype(F32)
    ones_dd = jnp.ones((d, d), BF16)
    rows_8d = _iota((8, d), 0)
    gnorm = gn_ref[...]

    tri2 = jnp.concatenate([tri_f, tri_f], axis=1).astype(BF16)
    for ci in range(ts // c):
        hi, lo = _split(log_f[ci * c:(ci + 1) * c, :])
        gc_ref[ci * c:(ci + 1) * c, :] = jnp.dot(tri2, jnp.concatenate([hi, lo], axis=0),
                                                 preferred_element_type=F32)

    blocks = [(sb * SUB, (sb + 1) * SUB) for sb in range(c // SUB)]

    def head_chunk(r0, hh):
        hs = slice(hh * d, (hh + 1) * d)
        q = qs_ref[pl.ds(r0, c), hs]
        k = ks_ref[pl.ds(r0, c), hs]
        v = i_ref[pl.ds(r0, c), hs]
        gc = gc_ref[pl.ds(r0, c), hs]

        prods = []
        for top, end in blocks:
            for j in range(top, end):
                lo = (j // 8) * 8
                k_j = k[j:j + 1, :]
                g_j = gc[j:j + 1, :]
                e = jnp.exp(gc[lo:end, :] - g_j)
                if j % 8:
                    head = jnp.where(rows_8d >= j - lo, e[:8], 0.0)
                    e = jnp.concatenate([head, e[8:]], axis=0) if lo + 8 < end else head
                prods.append(q[lo:end, :] * k_j * e)
        sums = jnp.dot(jnp.concatenate(prods, axis=0).astype(BF16), ones_dd,
                       preferred_element_type=F32)
        qk_far = []
        for top, end in blocks[1:]:
            g_b = gc[top - 1:top, :]
            qe = q[top:end, :] * jnp.exp(gc[top:end, :] - g_b)
            ke = k[:top, :] * jnp.exp(jnp.minimum(g_b - gc[:top, :], 0.0))
            qk_far.append(_mm_nt(qe, ke))
        far = [_mm(a, v[:top, :]) for a, (top, _) in zip(qk_far, blocks[1:])]

        groups = [jnp.zeros((8, d), F32) for _ in range(c // 8)]
        at = 0
        for top, end in blocks:
            for j in range(top, end):
                v_j = v[j:j + 1, :]
                for g in range(j // 8, end // 8):
                    groups[g] = groups[g] + sums[at:at + 8, :] * v_j
                    at += 8
        for f, (top, end) in zip(far, blocks[1:]):
            for g in range(top // 8, end // 8):
                groups[g] = groups[g] + f[(g * 8 - top):(g * 8 - top + 8), :]
        o_intra = jnp.concatenate(groups, axis=0)

        st = st_ref[hh]
        gl = gc[c - 1:c, :]
        o = o_intra + _mm_nt(q * jnp.exp(gc), st)
        st_ref[hh] = st * jnp.exp(gl) + _mm_tn(v, k * jnp.exp(gl - gc))
        o_ref[pl.ds(r0, c), hs] = _rms(o, gnorm) * _silu(gate_ref[pl.ds(r0, c), hs])

    def chunk_loop(ci, carry):
        r0 = pl.multiple_of(ci * c, c)
        for hh in range(nh):
            head_chunk(r0, hh)
        return carry

    lax.fori_loop(0, ts // c, chunk_loop, 0)


def _hgrn2(p32, lb, d_norm_g, *, ts, cols):
    bsz, s, _ = p32.shape
    d = HEAD_DIM
    nh = N_HEADS
    w = nh * d
    kernel = functools.partial(_hgrn2_kernel, ts=ts)
    tile = lambda name: pl.BlockSpec((None, ts, w), lambda b, i: (b, i, cols[name] // nh))
    return pl.pallas_call(
        kernel,
        grid=(bsz, s // ts),
        in_specs=[tile("qd"), tile("fd"), tile("id"), tile("gd"),
                  pl.BlockSpec((1, w), lambda b, i: (0, 0)),
                  pl.BlockSpec((1, d), lambda b, i: (0, 0))],
        out_specs=pl.BlockSpec((None, ts, w), lambda b, i: (b, i, 0)),
        out_shape=jax.ShapeDtypeStruct((bsz, s, w), F32),
        scratch_shapes=[pltpu.VMEM((ts, w), F32), pltpu.VMEM((ts, w), F32),
                        pltpu.VMEM((ts, w), F32), pltpu.VMEM((nh, d, d), F32)],
        compiler_params=pltpu.CompilerParams(
            dimension_semantics=("parallel", "arbitrary"), vmem_limit_bytes=VMEM_LIMIT),
        name="hgrn2",
    )(p32, p32, p32, p32, lb.astype(F32).reshape(1, w), d_norm_g.astype(F32).reshape(1, d))


def _stickbreak_kernel(q_ref, k_ref, v_ref, o_ref, *, tq):
    i = pl.program_id(1)
    d = HEAD_DIM
    nh = N_HEADS
    row = _iota((tq, tq), 0)
    col = _iota((tq, tq), 1)
    causal = col < row
    later = (row > col).astype(BF16)
    later2 = jnp.concatenate([later, later], axis=0)

    def scores(j, hs, diag):
        z = _mm_nt(q_ref[:, hs], k_ref[pl.ds(pl.multiple_of(j * tq, tq), tq), hs]) * (d ** -0.5)
        sp = _softplus(z)
        l1m = jnp.where(causal, -sp, 0.0) if diag else -sp
        rest = jnp.dot(jnp.concatenate(_split(l1m), axis=1), later2,
                       preferred_element_type=F32)
        return (z - sp) + rest, l1m

    def block(j, carries):
        out = []
        for hh in range(nh):
            hs = slice(hh * d, (hh + 1) * d)
            logw, l1m = scores(j, hs, False)
            o_ref[:, hs] += _mm(jnp.exp(logw + carries[hh]), v_ref[pl.ds(pl.multiple_of(j * tq, tq), tq), hs])
            out.append(carries[hh] + jnp.sum(l1m, axis=-1, keepdims=True))
        return tuple(out)

    jp = jnp.maximum(i - 1, 0)
    live = jnp.where(i > 0, 1.0, 0.0)
    heads = [slice(hh * d, (hh + 1) * d) for hh in range(nh)]
    sd = [scores(i, hs, True) for hs in heads]
    sp_ = [scores(jp, hs, False) for hs in heads]
    carries = []
    for hh, hs in enumerate(heads):
        c1 = jnp.sum(sd[hh][1], axis=-1, keepdims=True)
        p_d = jnp.where(causal, jnp.exp(sd[hh][0]), 0.0)
        p_p = jnp.exp(sp_[hh][0] + c1) * live
        o_ref[:, hs] = (_mm(p_d, v_ref[pl.ds(pl.multiple_of(i * tq, tq), tq), hs])
                        + _mm(p_p, v_ref[pl.ds(pl.multiple_of(jp * tq, tq), tq), hs]))
        carries.append(c1 + jnp.sum(sp_[hh][1], axis=-1, keepdims=True))
    carries = tuple(carries)

    def cond(c):
        worst = functools.reduce(jnp.maximum, c[1])
        return jnp.logical_and(c[0] >= 0, jnp.max(worst) >= EXP_ZERO_BELOW)

    def body(c):
        return c[0] - 1, block(c[0], c[1])

    lax.while_loop(cond, body, (i - 2, carries))


def _stickbreak(p16, *, tq, cols):
    bsz, s, _ = p16.shape
    nh = N_HEADS
    w = nh * HEAD_DIM
    kernel = functools.partial(_stickbreak_kernel, tq=tq)
    resident = dict(pipeline_mode=pl.Buffered(1))
    return pl.pallas_call(
        kernel,
        grid=(bsz, s // tq),
        in_specs=[pl.BlockSpec((None, tq, w), lambda b, i: (b, i, cols["qc"] // nh)),
                  pl.BlockSpec((None, s, w), lambda b, i: (b, 0, cols["kc"] // nh), **resident),
                  pl.BlockSpec((None, s, w), lambda b, i: (b, 0, cols["vc"] // nh), **resident)],
        out_specs=pl.BlockSpec((None, tq, w), lambda b, i: (b, i, 0)),
        out_shape=jax.ShapeDtypeStruct((bsz, s, w), F32),
        compiler_params=pltpu.CompilerParams(
            dimension_semantics=("parallel", "arbitrary"), vmem_limit_bytes=VMEM_LIMIT),
        name="stickbreak",
    )(p16, p16, p16)


def _dsa_kernel(qi_ref, smq_ref, q_ref, sm_ref, k_ref, vt_ref, bias_ref, o_ref,
                sc_ref, scb_ref, wb_ref, qc_ref, kct_ref, bd_ref, lg_ref, *, tq, k_sel, wi_lane, wide):
    i = pl.program_id(1)
    tk = tq
    d = HEAD_DIM
    nh = N_HEADS
    ksel = float(k_sel)
    per_wide = wide // tk
    n_wide = (i + per_wide) // per_wide
    sub = 2 * tk
    lane_q = _iota((1, tq), 1)

    def tree(parts, op):
        while len(parts) > 1:
            parts = [op(parts[j], parts[j + 1]) if j + 1 < len(parts) else parts[j]
                     for j in range(0, len(parts), 2)]
        return parts[0]

    def col_fold(x, op=jnp.add, rows=8):
        return tree([x[r * rows:(r + 1) * rows] for r in range(x.shape[0] // rows)], op)

    @pl.when(i == 0)
    def _():
        def prep(g, carry):
            g0 = pl.multiple_of(g * wide, wide)
            kt = sm_ref[pl.ds(g0, wide), :].T[:IDX_DIM, :]
            hi, lo = _split(kt)
            kct_ref[:, pl.ds(g0, wide)] = jnp.concatenate([hi, lo, hi], axis=0)
            return carry
        lax.fori_loop(0, sm_ref.shape[0] // wide, prep, 0)

    smq = smq_ref[...]
    lane = _iota(smq.shape, 1)
    for hh in range(IDX_HEADS):
        qh = qi_ref[:, hh * IDX_DIM:(hh + 1) * IDX_DIM]
        hi, lo = _split(qh)
        qc_ref[hh] = jnp.concatenate([hi, hi, lo], axis=-1)
        w = jnp.sum(jnp.where(lane == wi_lane + hh, smq, 0.0), axis=-1, keepdims=True)
        wb_ref[hh] = jnp.broadcast_to(w * ((IDX_HEADS ** -0.5) * (IDX_DIM ** -0.5)), (tq, tk))

    q2t = (q_ref[...] * ((d ** -0.5) * LOG2E)).T.astype(BF16)
    zero_dq = jnp.zeros((d, tq), BF16)
    for p in range(nh // 2):
        top = jnp.concatenate([q2t[2 * p * d:(2 * p + 1) * d], zero_dq], axis=1)
        bot = jnp.concatenate([zero_dq, q2t[(2 * p + 1) * d:(2 * p + 2) * d]], axis=1)
        bd_ref[p] = jnp.concatenate([top, bot], axis=0)

    limit = i * tq + (lane_q // CHUNK + 1) * CHUNK
    rows_t = _iota((tk, tq), 0)

    def score_group(g, mm, masked):
        mn, mx = mm
        for sb in range(wide // sub):
            k0 = pl.multiple_of(g * wide + sb * sub, sub)
            kct = kct_ref[:, pl.ds(k0, sub)]
            tiles = [jnp.zeros((tq, tk), F32) for _ in range(sub // tk)]
            for hh in range(IDX_HEADS):
                s_h = jnp.dot(qc_ref[hh], kct, preferred_element_type=F32)
                for ti in range(sub // tk):
                    tiles[ti] = tiles[ti] + jnp.maximum(s_h[:, ti * tk:(ti + 1) * tk], 0.0) * wb_ref[hh]
            for ti in range(sub // tk):
                kb = pl.multiple_of(k0 + ti * tk, tk)
                sct = tiles[ti].T
                if masked:
                    adm = (kb + rows_t) < limit
                    mn = jnp.minimum(mn, col_fold(jnp.where(adm, sct, jnp.inf), jnp.minimum))
                    sct = jnp.where(adm, sct, -jnp.inf)
                else:
                    mn = jnp.minimum(mn, col_fold(sct, jnp.minimum))
                mx = jnp.maximum(mx, col_fold(sct, jnp.maximum))
                sc_ref[pl.ds(kb, tk), :] = sct
                scb_ref[pl.ds(kb, tk), :] = _floor_bf16(sct)
        return mn, mx

    def score_pair(j, mm):
        return score_group(2 * j + 1, score_group(2 * j, mm, False), False)

    n_full = n_wide - 1
    mm = lax.fori_loop(0, n_full // 2, score_pair,
                       (jnp.full((8, tq), jnp.inf, F32), jnp.full((8, tq), -jnp.inf, F32)))
    mm = lax.cond(n_full % 2 == 1, lambda c: score_group(n_full - 1, c, False), lambda c: c, mm)
    mn, mx = score_group(n_wide - 1, mm, True)

    n_pairs = (n_wide + 1) // 2

    @pl.when(n_wide % 2 == 1)
    def _():
        sc_ref[pl.ds(pl.multiple_of(n_wide * wide, wide), wide), :] = jnp.full((wide, tq), -jnp.inf, F32)
    rmin = jnp.min(mn, axis=0, keepdims=True)
    rmax = jnp.max(mx, axis=0, keepdims=True)

    def count(pred):
        def body(g, acc):
            blk = sc_ref[pl.ds(pl.multiple_of(g * wide, wide), wide), :]
            return acc + col_fold(pred(blk))
        return jnp.sum(lax.fori_loop(0, n_wide, body, jnp.zeros((8, tq), F32)), axis=0, keepdims=True)

    def max_below(x):
        def body(g, acc):
            blk = sc_ref[pl.ds(pl.multiple_of(g * wide, wide), wide), :]
            return jnp.maximum(acc, col_fold(jnp.where(blk < x, blk, -jnp.inf), jnp.maximum))
        return jnp.max(lax.fori_loop(0, n_wide, body, jnp.full((8, tq), -jnp.inf, F32)), axis=0, keepdims=True)

    n_adm = limit.astype(F32)
    all_sel = n_adm <= ksel

    def bisect(c):
        lo, hi, c_lo = c
        mid = 0.5 * lo + 0.5 * hi
        cm = count(lambda blk: _ind(blk >= mid))
        ge = cm >= ksel
        return jnp.where(ge, mid, lo), jnp.where(ge, hi, mid), jnp.where(ge, cm, c_lo)

    def pending(c_lo, tied):
        return jnp.where(all_sel, 0.0, jnp.where(tied > 0.5, 0.0, _ind(c_lo != ksel)))

    def bisect_coarse(_, c):
        lo, hi, c_lo = c
        mid = _floor_bf16(0.5 * lo + 0.5 * hi).astype(F32)
        t_b = jnp.broadcast_to(mid, (16, tq)).astype(BF16)
        one_b = jnp.ones((16, tq), BF16)
        zero_b = jnp.zeros((16, tq), BF16)

        def body(g, acc):
            blk = scb_ref[pl.ds(pl.multiple_of(g * wide, wide), wide), :]
            ind = [jnp.where(blk[r * 16:(r + 1) * 16] >= t_b, one_b, zero_b) for r in range(wide // 16)]
            return acc + tree(ind, jnp.add).astype(F32)

        acc = lax.fori_loop(0, n_wide, body, jnp.zeros((16, tq), F32))
        cm = jnp.sum(acc, axis=0, keepdims=True)
        ge = cm >= ksel
        return jnp.where(ge, mid, lo), jnp.where(ge, hi, mid), jnp.where(ge, cm, c_lo)

    lo0 = _floor_bf16(rmin).astype(F32)
    hi0 = _floor_bf16(rmax + (jnp.abs(rmax) * (2.0 ** -6) + 1e-30)).astype(F32)
    state = lax.fori_loop(0, BISECT_COARSE, bisect_coarse, (lo0, hi0, n_adm))
    state = lax.fori_loop(0, BISECT_FIXED, lambda _, c: bisect(c), state)

    def round_cond(c):
        return jnp.max(pending(c[0][2], c[1])) > 0.5

    def round_body(c):
        st, tied, v, need = c

        def more_cond(s):
            return jnp.logical_and(s[0] < BISECT_EXTRA, jnp.max(pending(s[1][2], tied)) > 0.5)

        _, st = lax.while_loop(more_cond, lambda s: (s[0] + 1, bisect(s[1])), (jnp.int32(0), st))
        pend = pending(st[2], tied)

        def check(_):
            cand = max_below(st[1])
            c_ge = count(lambda blk: _ind(blk >= cand))
            c_gt = count(lambda blk: _ind(blk > cand))
            ok = jnp.where(pend > 0.5, _ind(c_ge >= ksel), 0.0)
            return (jnp.where(ok > 0.5, 1.0, tied), jnp.where(ok > 0.5, cand, v),
                    jnp.where(ok > 0.5, ksel - c_gt, need))

        tied, v, need = lax.cond(jnp.max(pend) > 0.5, check, lambda _: (tied, v, need), 0)
        return st, tied, v, need

    zeros1 = jnp.zeros((1, tq), F32)
    (lo_f, _, _), tied, v_tie, need = lax.while_loop(round_cond, round_body, (state, zeros1, zeros1, zeros1))
    vth = jnp.where(all_sel, F32_LOWEST, jnp.where(tied > 0.5, v_tie, lo_f))

    @pl.when(jnp.max(tied) > 0.5)
    def _():
        v_eq = jnp.where(tied > 0.5, v_tie, jnp.inf)
        incl = (_iota((tk, tk), 1) <= _iota((tk, tk), 0)).astype(BF16)

        def demote(g, seen):
            g0 = pl.multiple_of(g * wide, wide)
            xs = [sc_ref[pl.ds(g0 + pb * tk, tk), :] for pb in range(per_wide)]
            eqs = [_ind(x == v_eq) for x in xs]
            inblk = [jnp.dot(incl, e.astype(BF16), preferred_element_type=F32) for e in eqs]
            for pb in range(per_wide):
                rank = inblk[pb] + seen
                sc_ref[pl.ds(g0 + pb * tk, tk), :] = jnp.where(eqs[pb] * _ind(rank > need) > 0.5,
                                                               -jnp.inf, xs[pb])
                seen = seen + jnp.sum(col_fold(eqs[pb]), axis=0, keepdims=True)
            return seen

        lax.fori_loop(0, n_wide, demote, zeros1)

    g_near = jnp.maximum(i - 1, 0) // per_wide

    def logit_group(g, mx, near):
        out = list(mx)
        for sb in range(wide // sub):
            k0 = pl.multiple_of(g * wide + sb * sub, sub)
            sel = sc_ref[pl.ds(k0, sub), :] >= vth
            for p in range(nh // 2):
                pair = jnp.dot(k_ref[pl.ds(k0, sub), 2 * p * d:(2 * p + 2) * d], bd_ref[p],
                               preferred_element_type=F32)
                for hh in (2 * p, 2 * p + 1):
                    lm = pair[:, (hh - 2 * p) * tq:(hh - 2 * p + 1) * tq]
                    if near:
                        back = [jnp.clip(i - (g * per_wide + sb * (sub // tk) + pb), 0, 2)
                                for pb in range(sub // tk)]
                        lm = lm + jnp.concatenate([bias_ref[bk, hh] for bk in back], axis=0)
                    lm = jnp.where(sel, lm, NEG_BIG)
                    lg_ref[hh, pl.ds(k0, sub), :] = lm
                    out[hh] = jnp.maximum(out[hh], col_fold(lm, jnp.maximum))
        return tuple(out)

    mx = tuple(jnp.full((8, tq), NEG_BIG, F32) for _ in range(nh))
    def logit_pair(j, mx, near):
        return logit_group(2 * j + 1, logit_group(2 * j, mx, near), near)

    far_pairs = g_near // 2
    mx = lax.fori_loop(0, far_pairs, functools.partial(logit_pair, near=False), mx)
    mx = lax.fori_loop(far_pairs, n_pairs, functools.partial(logit_pair, near=True), mx)
    m_q = [jnp.max(mx[hh], axis=0, keepdims=True) for hh in range(nh)]

    def pv_body(g, carry):
        g0 = pl.multiple_of(g * wide, wide)
        ls, accs = carry
        new_l, new_a = [], []
        for hh in range(nh):
            p = jnp.exp2(lg_ref[hh, pl.ds(g0, wide), :] - m_q[hh])
            lhs = jnp.concatenate([vt_ref[hh * d:(hh + 1) * d, pl.ds(g0, wide)], ones_rows], axis=0)
            out = jnp.dot(lhs, p.astype(BF16), preferred_element_type=F32)
            new_l.append(ls[hh] + out[d:])
            new_a.append(accs[hh] + out[:d])
        return tuple(new_l), tuple(new_a)

    ones_rows = jnp.ones((8, wide), BF16)

    ls, accs = lax.fori_loop(0, n_pairs, lambda j, c: pv_body(2 * j + 1, pv_body(2 * j, c)),
                             (tuple(jnp.zeros((8, tq), F32) for _ in range(nh)),
                              tuple(jnp.zeros((d, tq), F32) for _ in range(nh))))
    for hh in range(nh):
        o_ref[:, hh * d:(hh + 1) * d] = (accs[hh] / ls[hh][0:1]).T


def _dsa(p32, p16, vt, bias_tiles, *, tq, cols):
    bsz, s, _ = p32.shape
    d = HEAD_DIM
    nh = N_HEADS
    wide = 4 * tq
    k_sel = min(TOPK_MAX, s // 4)
    w512 = nh * d
    kernel = functools.partial(_dsa_kernel, tq=tq, k_sel=k_sel, wi_lane=cols["wi_lane"], wide=wide)
    resident = dict(pipeline_mode=pl.Buffered(1))
    return pl.pallas_call(
        kernel,
        grid=(bsz, s // tq),
        in_specs=[pl.BlockSpec((None, tq, w512), lambda b, i: (b, i, cols["qi"] // nh)),
                  pl.BlockSpec((None, tq, d), lambda b, i: (b, i, cols["small"])),
                  pl.BlockSpec((None, tq, w512), lambda b, i: (b, i, cols["qb"] // nh)),
                  pl.BlockSpec((None, s, d), lambda b, i: (b, 0, cols["small"]), **resident),
                  pl.BlockSpec((None, s, w512), lambda b, i: (b, 0, cols["kb"] // nh), **resident),
                  pl.BlockSpec((w512, s), lambda b, i: (0, b), **resident),
                  pl.BlockSpec((3, nh, tq, tq), lambda b, i: (0, 0, 0, 0), **resident)],
        out_specs=pl.BlockSpec((None, tq, w512), lambda b, i: (b, i, 0)),
        out_shape=jax.ShapeDtypeStruct((bsz, s, w512), F32),
        scratch_shapes=[pltpu.VMEM((s, tq), F32),
                        pltpu.VMEM((s, tq), BF16),
                        pltpu.VMEM((IDX_HEADS, tq, tq), F32),
                        pltpu.VMEM((IDX_HEADS, tq, 3 * IDX_DIM), BF16),
                        pltpu.VMEM((3 * IDX_DIM, s), BF16),
                        pltpu.VMEM((nh // 2, 2 * d, 2 * tq), BF16),
                        pltpu.VMEM((nh, s, tq), F32)],
        compiler_params=pltpu.CompilerParams(
            dimension_semantics=("parallel", "arbitrary"), vmem_limit_bytes=VMEM_LIMIT),
        name="dsa",
    )(p32, p32, p32, p32, p16, vt, bias_tiles)


def _t5_bucket(rel):
    nb = REL_BUCKETS // 2
    max_exact = nb // 2
    ret = jnp.where(rel > 0, nb, 0)
    n = jnp.abs(rel)
    large = max_exact + (jnp.log(jnp.maximum(n, 1).astype(F32) / max_exact)
                         / math.log(REL_MAX_DIST / max_exact) * (nb - max_exact)).astype(jnp.int32)
    large = jnp.minimum(large, nb - 1)
    return ret + jnp.where(n < max_exact, n, large)


def _bias_tiles(rel_table, tq):
    assert tq >= REL_MAX_DIST
    t = jnp.arange(tq)
    back = jnp.arange(3)
    rel = (t[None, None, :] - back[:, None, None] * tq) - t[None, :, None]
    onehot = (_t5_bucket(rel)[..., None] == jnp.arange(REL_BUCKETS)).astype(F32)
    tiles = jnp.einsum("bqkn,nh->bhkq", onehot, rel_table.astype(F32),
                       precision=HIGHEST)
    return (tiles - tiles[2:3]) * LOG2E


def _even_layout(w_in):
    d = HEAD_DIM
    a_w = 2 * N_HEADS * d + N_HEADS * d
    offs = {}
    o = 0
    for name, w in (("qkv", a_w), ("z", N_HEADS * d), ("a", N_HEADS), ("b", N_HEADS),
                    ("qb", N_HEADS * d), ("kb", N_HEADS * d), ("vb", N_HEADS * d),
                    ("qi", IDX_HEADS * IDX_DIM), ("ki", IDX_DIM), ("wi", IDX_HEADS)):
        offs[name] = (o, o + w)
        o += w
    assert o == w_in.shape[1]
    sl = lambda n: w_in[:, offs[n][0]:offs[n][1]]
    small_w = IDX_DIM + 2 * N_HEADS + IDX_HEADS
    small_pad = -small_w % d
    zeros = lambda n: jnp.zeros((w_in.shape[0], n), w_in.dtype)
    w32 = jnp.concatenate([sl("qkv"), sl("z"), sl("qb"), sl("qi"),
                           sl("ki"), sl("a"), sl("b"), sl("wi"), zeros(small_pad)], axis=1)
    n32 = w32.shape[1]
    tn = n32 // 5
    assert tn * 5 == n32 and tn % d == 0
    w16 = jnp.concatenate([sl("kb"), zeros(tn - N_HEADS * d)], axis=1)
    nh = N_HEADS
    cols = dict(qa=0, ka=nh, va=2 * nh, za=3 * nh, qb=4 * nh, qi=5 * nh, small=6 * nh, kb=0,
                a_lane=IDX_DIM, b_lane=IDX_DIM + nh, wi_lane=IDX_DIM + 2 * nh, n32=n32, tn=tn)
    return jnp.concatenate([w32, w16], axis=1).astype(BF16), sl("vb").T.astype(BF16), cols


def kernel(x, norm_g, w_in_even, conv_w_even, a_log_even, dt_bias_even, a_norm_even, w_out_even,
           rel_bias, w_in_odd, lb_logits, d_norm_odd, w_out_odd, w_gate, w_up, w_down):
    bsz, s, d = x.shape
    t = bsz * s
    depth = norm_g.shape[0]
    nh = N_HEADS
    tq = Q_TILE
    lb_all = jnp.cumsum(jax.nn.softmax(lb_logits.astype(F32), axis=0), axis=0)
    lb_all = lb_all - lb_all[:1]
    odd_cols = dict(qc=0, kc=nh, vc=2 * nh, qd=0, fd=nh, id=2 * nh, gd=3 * nh)
    bias_tiles = _bias_tiles(rel_bias, tq)

    h = x.reshape(t, d)
    for l in range(depth):
        if l % 2 == 0:
            e = l // 2
            w_even, w_vt, cols = _even_layout(w_in_even[e])
            p32, p16, vt = _norm_matmul(h, norm_g[l, 0], w_even, tm=ROW_TILE, tn=cols["tn"], n32=cols["n32"],
                                        w_t=w_vt)
            p32 = p32.reshape(bsz, s, -1)
            p16 = p16.reshape(bsz, s, -1)
            o_1 = _deltanet(p32, conv_w_even[e], a_log_even[e], dt_bias_even[e], a_norm_even[e],
                            ts=min(SEQ_TILE, s), cols=cols)
            o_2 = _dsa(p32, p16, vt, bias_tiles, tq=tq, cols=cols)
            w_out = w_out_even[e]
        else:
            o = l // 2
            n16 = 3 * nh * HEAD_DIM
            w_odd = jnp.concatenate([w_in_odd[o][:, n16:], w_in_odd[o][:, :n16]], axis=1).astype(BF16)
            p32, p16 = _norm_matmul(h, norm_g[l, 0], w_odd, tm=ROW_TILE, tn=ODD_COL_TILE, n32=w_odd.shape[1] - n16)
            p32 = p32.reshape(bsz, s, -1)
            p16 = p16.reshape(bsz, s, -1)
            o_1 = _stickbreak(p16, tq=tq, cols=odd_cols)
            o_2 = _hgrn2(p32, lb_all[l], d_norm_odd[o], ts=min(SEQ_TILE, s), cols=odd_cols)
            w_out = w_out_odd[o]
        h = _outproj(o_1.reshape(t, -1), o_2.reshape(t, -1), w_out, h, norm_g[l, 1], tm=OUT_TILE)
        h = _ffn(h, norm_g[l, 2], norm_g[l, 3], w_gate[l], w_up[l], w_down[l], tm=ROW_TILE, tf=FFN_TILE)
    return h.reshape(bsz, s, d)
```

```python
import functools
import math

import jax
import jax.numpy as jnp
from jax import lax
from jax.experimental import pallas as pl
from jax.experimental.pallas import tpu as pltpu

F32 = jnp.float32
BF16 = jnp.bfloat16
HIGHEST = lax.Precision.HIGHEST

CHUNK = 64
HEAD_DIM = 128
N_HEADS = 4
IDX_HEADS = 8
IDX_DIM = 64
TOPK_MAX = 256
CONV_WIDTH = 4
REL_BUCKETS = 32
REL_MAX_DIST = 128
EPS = 1e-6
NEG_BIG = -1e30
LOG2E = 1.4426950408889634
BISECT_COARSE = 12
BISECT_FIXED = 8
BISECT_EXTRA = 6
F32_LOWEST = -3.4028234663852886e38
EXP_ZERO_BELOW = -104.0
VMEM_LIMIT = 56 * 1024 * 1024

ROW_TILE = 1024
OUT_TILE = 512
SEQ_TILE = 512
Q_TILE = 128
ODD_COL_TILE = 512
FFN_TILE = 256


def _mm(a, b):
    return jnp.dot(a.astype(BF16), b.astype(BF16), preferred_element_type=F32)


def _mm_nt(a, b):
    return lax.dot_general(a.astype(BF16), b.astype(BF16), (((1,), (1,)), ((), ())),
                           preferred_element_type=F32)


def _mm_tn(a, b):
    return lax.dot_general(a.astype(BF16), b.astype(BF16), (((0,), (0,)), ((), ())),
                           preferred_element_type=F32)


def _split(x):
    hi = x.astype(BF16)
    return hi, (x - hi.astype(F32)).astype(BF16)


def _x3_parts(x):
    hi, lo = _split(x)
    return jnp.concatenate([hi, hi, lo], axis=1), jnp.concatenate([hi, lo, hi], axis=0)


def _mm_x3(a, b):
    return jnp.dot(_x3_parts(a)[0], _x3_parts(b)[1], preferred_element_type=F32)


def _floor_bf16(x):
    bits = pltpu.bitcast(x, jnp.int32)
    down = jnp.where(bits >= 0, bits, bits + 0xFFFF) & jnp.int32(-65536)
    return pltpu.bitcast(down, F32).astype(BF16)


def _sigmoid(x):
    return 1.0 / (1.0 + jnp.exp(-x))


def _silu(x):
    return x * _sigmoid(x)


def _softplus(x):
    return jnp.maximum(x, 0.0) + jnp.log1p(jnp.exp(-jnp.abs(x)))


def _rms(x, g):
    return x * lax.rsqrt(jnp.mean(x * x, axis=-1, keepdims=True) + EPS) * g


def _iota(shape, dim):
    return lax.broadcasted_iota(jnp.int32, shape, dim)


def _ind(mask):
    return jnp.where(mask, 1.0, 0.0)


def _norm_matmul_kernel(x_ref, g_ref, w_ref, *rest, n_t, tiles32):
    if n_t:
        wt_ref, o32_ref, o16_ref, ot_ref, xn_ref = rest
    else:
        o32_ref, o16_ref, xn_ref = rest
    j = pl.program_id(1)

    @pl.when(j == 0)
    def _():
        xn_ref[...] = _rms(x_ref[...], g_ref[...]).astype(BF16)
        if n_t:
            ot_ref[...] = lax.dot_general(wt_ref[...], xn_ref[...], (((1,), (1,)), ((), ())),
                                          preferred_element_type=F32).astype(BF16)

    y = jnp.dot(xn_ref[...], w_ref[...], preferred_element_type=F32)

    @pl.when(j < tiles32)
    def _():
        o32_ref[...] = y

    @pl.when(j >= tiles32)
    def _():
        o16_ref[...] = y.astype(BF16)


def _norm_matmul(x, g, w, *, tm, tn, n32, w_t=None):
    t, d = x.shape
    n = w.shape[1]
    n_t = 0 if w_t is None else w_t.shape[0]
    tiles32 = n32 // tn
    assert tiles32 * tn == n32 and (n - n32) % tn == 0 and 0 < n32 < n
    in_specs = [pl.BlockSpec((tm, d), lambda i, j: (i, 0)),
                pl.BlockSpec((1, d), lambda i, j: (0, 0)),
                pl.BlockSpec((d, tn), lambda i, j: (0, j))]
    out_specs = [pl.BlockSpec((tm, tn), lambda i, j: (i, jnp.minimum(j, tiles32 - 1))),
                 pl.BlockSpec((tm, tn), lambda i, j: (i, jnp.maximum(j - tiles32, 0)))]
    out_shape = [jax.ShapeDtypeStruct((t, n32), F32), jax.ShapeDtypeStruct((t, n - n32), BF16)]
    args = [x, g.reshape(1, d), w]
    if n_t:
        in_specs.append(pl.BlockSpec((n_t, d), lambda i, j: (0, 0)))
        out_specs.append(pl.BlockSpec((n_t, tm), lambda i, j: (0, i)))
        out_shape.append(jax.ShapeDtypeStruct((n_t, t), BF16))
        args.append(w_t)
    return pl.pallas_call(
        functools.partial(_norm_matmul_kernel, n_t=n_t, tiles32=tiles32),
        grid=(t // tm, n // tn),
        in_specs=in_specs,
        out_specs=out_specs,
        out_shape=out_shape,
        scratch_shapes=[pltpu.VMEM((tm, d), BF16)],
        compiler_params=pltpu.CompilerParams(
            dimension_semantics=("parallel", "arbitrary"), vmem_limit_bytes=VMEM_LIMIT),
        name="norm_matmul",
    )(*args)


def _outproj_kernel(ca_ref, cb_ref, wa_ref, wb_ref, h_ref, g_ref, o_ref):
    y = (jnp.dot(ca_ref[...].astype(BF16), wa_ref[...], preferred_element_type=F32)
         + jnp.dot(cb_ref[...].astype(BF16), wb_ref[...], preferred_element_type=F32))
    o_ref[...] = h_ref[...] + _rms(y, g_ref[...])


def _outproj(ca, cb, w, h, g, *, tm):
    t, d = h.shape
    wa_n = ca.shape[1]
    wb_n = cb.shape[1]
    wa = w[:wa_n].astype(BF16)
    wb = w[wa_n:].astype(BF16)
    return pl.pallas_call(
        _outproj_kernel,
        grid=(t // tm,),
        in_specs=[pl.BlockSpec((tm, wa_n), lambda i: (i, 0)),
                  pl.BlockSpec((tm, wb_n), lambda i: (i, 0)),
                  pl.BlockSpec((wa_n, d), lambda i: (0, 0)),
                  pl.BlockSpec((wb_n, d), lambda i: (0, 0)),
                  pl.BlockSpec((tm, d), lambda i: (i, 0)),
                  pl.BlockSpec((1, d), lambda i: (0, 0))],
        out_specs=pl.BlockSpec((tm, d), lambda i: (i, 0)),
        out_shape=jax.ShapeDtypeStruct((t, d), F32),
        compiler_params=pltpu.CompilerParams(
            dimension_semantics=("parallel",), vmem_limit_bytes=VMEM_LIMIT),
        name="outproj",
    )(ca, cb, wa, wb, h, g.reshape(1, d))


def _ffn_kernel(h_ref, gpre_ref, gpost_ref, wg_ref, wu_ref, wd_ref, o_ref, xn_ref, acc_ref):
    f = pl.program_id(1)

    @pl.when(f == 0)
    def _():
        xn_ref[...] = _rms(h_ref[...], gpre_ref[...]).astype(BF16)
        acc_ref[...] = jnp.zeros_like(acc_ref)

    xn = xn_ref[...]
    gate = jnp.dot(xn, wg_ref[...], preferred_element_type=F32)
    up = jnp.dot(xn, wu_ref[...], preferred_element_type=F32)
    act = (_silu(gate) * up).astype(BF16)
    acc_ref[...] += jnp.dot(act, wd_ref[...], preferred_element_type=F32)

    @pl.when(f == pl.num_programs(1) - 1)
    def _():
        o_ref[...] = h_ref[...] + _rms(acc_ref[...], gpost_ref[...])


def _ffn(h, g_pre, g_post, wg, wu, wd, *, tm, tf):
    t, d = h.shape
    ff = wg.shape[1]
    return pl.pallas_call(
        _ffn_kernel,
        grid=(t // tm, ff // tf),
        in_specs=[pl.BlockSpec((tm, d), lambda i, f: (i, 0)),
                  pl.BlockSpec((1, d), lambda i, f: (0, 0)),
                  pl.BlockSpec((1, d), lambda i, f: (0, 0)),
                  pl.BlockSpec((d, tf), lambda i, f: (0, f)),
                  pl.BlockSpec((d, tf), lambda i, f: (0, f)),
                  pl.BlockSpec((tf, d), lambda i, f: (f, 0))],
        out_specs=pl.BlockSpec((tm, d), lambda i, f: (i, 0)),
        out_shape=jax.ShapeDtypeStruct((t, d), F32),
        scratch_shapes=[pltpu.VMEM((tm, d), BF16), pltpu.VMEM((tm, d), F32)],
        compiler_params=pltpu.CompilerParams(
            dimension_semantics=("parallel", "arbitrary"), vmem_limit_bytes=VMEM_LIMIT),
        name="ffn",
    )(h, g_pre.reshape(1, d), g_post.reshape(1, d),
      wg.astype(BF16), wu.astype(BF16), wd.astype(BF16))


def _deltanet_kernel(xq_ref, xk_ref, xv_ref, z_ref, sm_ref, cwq_ref, cwk_ref, cwv_ref,
                     alog_ref, dtb_ref, gn_ref, o_ref,
                     xpad_ref, q_ref, k_ref, v_ref, gb_ref, bb_ref, u_ref, w_ref, qk_ref, st_ref,
                     *, ts, a_col, b_col):
    s = pl.program_id(1)
    c = CHUNK
    d = HEAD_DIM
    nh = N_HEADS

    @pl.when(s == 0)
    def _():
        xpad_ref[:, 0:8, :] = jnp.zeros((3, 8, nh * d), F32)
        st_ref[...] = jnp.zeros_like(st_ref)

    @pl.when(s != 0)
    def _():
        xpad_ref[:, 0:8, :] = xpad_ref[:, ts:ts + 8, :]

    xpad_ref[0, 8:ts + 8, :] = xq_ref[...]
    xpad_ref[1, 8:ts + 8, :] = xk_ref[...]
    xpad_ref[2, 8:ts + 8, :] = xv_ref[...]

    def conv_silu(idx, cw_ref, hs):
        cw = cw_ref[:, hs]
        acc = xpad_ref[idx, 8 - (CONV_WIDTH - 1):8 - (CONV_WIDTH - 1) + ts, hs] * cw[0:1, :]
        for j in range(1, CONV_WIDTH):
            off = 8 - (CONV_WIDTH - 1) + j
            acc = acc + xpad_ref[idx, off:off + ts, hs] * cw[j:j + 1, :]
        return _silu(acc)

    def l2norm(t):
        return t * lax.rsqrt(jnp.sum(t * t, axis=-1, keepdims=True) + EPS)

    row = _iota((c, c), 0)
    col = _iota((c, c), 1)
    tri = (col <= row)
    strict = (col < row)
    tri_f = tri.astype(F32)
    upper_f = (row <= col).astype(F32)
    eye = (row == col).astype(F32)
    gnorm = gn_ref[...]
    chunks = range(ts // c)
    rs = [slice(ci * c, (ci + 1) * c) for ci in chunks]
    tri2 = jnp.concatenate([tri_f, tri_f], axis=1).astype(BF16)
    ones2 = jnp.ones((c, 2 * c), BF16)

    def cum2(lhs2, x):
        hi, lo = _split(x)
        return jnp.dot(lhs2, jnp.concatenate([hi, lo], axis=0), preferred_element_type=F32)

    for hh in range(nh):
        hs = slice(hh * d, (hh + 1) * d)
        q_ref[:, hs] = l2norm(conv_silu(0, cwq_ref, hs)) * (d ** -0.5)
        k_ref[:, hs] = l2norm(conv_silu(1, cwk_ref, hs))
        v_ref[:, hs] = conv_silu(2, cwv_ref, hs)

        a_raw = sm_ref[:, a_col + hh:a_col + hh + 1]
        b_raw = sm_ref[:, b_col + hh:b_col + hh + 1]
        g = -jnp.exp(alog_ref[:, hh:hh + 1]) * _softplus(a_raw + dtb_ref[:, hh:hh + 1])
        gb_ref[:, hs] = jnp.broadcast_to(g, (ts, d))
        bb_ref[:, hs] = jnp.broadcast_to(_sigmoid(b_raw), (ts, d))

        q = [q_ref[r, hs] for r in rs]
        k = [k_ref[r, hs] for r in rs]
        beta = [bb_ref[r, hs] for r in rs]
        gb = [gb_ref[r, hs] for r in rs]
        gc = [cum2(tri2, x) for x in gb]
        gc_row = [cum2(ones2, x[:, :c] * upper_f) for x in gb]
        decay = [jnp.where(tri, jnp.exp(jnp.minimum(a[:, :c] - b, 0.0)), 0.0) for a, b in zip(gc, gc_row)]
        kk = [_mm_nt(x, x) for x in k]
        n = [-jnp.where(strict, b[:, :c] * x * dc, 0.0) for b, x, dc in zip(beta, kk, decay)]
        inv = [eye + x for x in n]
        n_parts = [_x3_parts(x) for x in n]
        for step in range(5):
            n = [jnp.dot(a, b, preferred_element_type=F32) for a, b in n_parts]
            n_parts = [_x3_parts(x) for x in n]
            inv = [iv + jnp.dot(_x3_parts(iv)[0], b, preferred_element_type=F32)
                   for iv, (_, b) in zip(inv, n_parts)]
        egc = [jnp.exp(x) for x in gc]
        gl = [x[c - 1:c, :] for x in gc]
        for ci in chunks:
            r = rs[ci]
            u_ref[r, hs] = _mm_x3(inv[ci], v_ref[r, hs] * beta[ci])
            w_ref[r, hs] = _mm_x3(inv[ci], k[ci] * (beta[ci] * egc[ci]))
            qk_ref[hh, r, :] = _mm_nt(q[ci], k[ci]) * decay[ci]
            q_ref[r, hs] = q[ci] * egc[ci]
            k_ref[r, hs] = k[ci] * jnp.exp(gl[ci] - gc[ci])
            gb_ref[r, hs] = jnp.broadcast_to(jnp.exp(gl[ci]), (c, d))

    def chunk_body(ci, carry):
        r0 = pl.multiple_of(ci * c, c)
        for hh in range(nh):
            hs = slice(hh * d, (hh + 1) * d)
            st = st_ref[hh]
            v_new = u_ref[pl.ds(r0, c), hs] - _mm(w_ref[pl.ds(r0, c), hs], st)
            o = _mm(q_ref[pl.ds(r0, c), hs], st) + _mm(qk_ref[hh, pl.ds(r0, c), :], v_new)
            st_ref[hh] = st * gb_ref[pl.ds(r0, 1), hs] + _mm_tn(k_ref[pl.ds(r0, c), hs], v_new)
            o_ref[pl.ds(r0, c), hs] = _rms(o, gnorm) * _silu(z_ref[pl.ds(r0, c), hs])
        return carry

    lax.fori_loop(0, ts // c, chunk_body, 0)


def _deltanet(p32, conv_w, a_log, dt_bias, a_norm_g, *, ts, cols):
    bsz, s, _ = p32.shape
    d = HEAD_DIM
    nh = N_HEADS
    w = nh * d
    pad = lambda t: jnp.pad(t.astype(F32), (0, d - t.shape[0])).reshape(1, d)
    kernel = functools.partial(_deltanet_kernel, ts=ts, a_col=cols["a_lane"], b_col=cols["b_lane"])
    tile = lambda name: pl.BlockSpec((None, ts, w), lambda b, i: (b, i, cols[name] // nh))
    conv = lambda k: pl.BlockSpec((CONV_WIDTH, w), lambda b, i: (0, k))
    row = pl.BlockSpec((1, d), lambda b, i: (0, 0))
    return pl.pallas_call(
        kernel,
        grid=(bsz, s // ts),
        in_specs=[tile("qa"), tile("ka"), tile("va"), tile("za"),
                  pl.BlockSpec((None, ts, d), lambda b, i: (b, i, cols["small"])),
                  conv(0), conv(1), conv(2), row, row, row],
        out_specs=pl.BlockSpec((None, ts, w), lambda b, i: (b, i, 0)),
        out_shape=jax.ShapeDtypeStruct((bsz, s, w), F32),
        scratch_shapes=[pltpu.VMEM((3, ts + 8, w), F32)]
        + [pltpu.VMEM((ts, w), F32) for _ in range(7)]
        + [pltpu.VMEM((nh, ts, CHUNK), F32), pltpu.VMEM((nh, d, d), F32)],
        compiler_params=pltpu.CompilerParams(
            dimension_semantics=("parallel", "arbitrary"), vmem_limit_bytes=VMEM_LIMIT),
        name="deltanet",
    )(p32, p32, p32, p32, p32, conv_w.astype(F32), conv_w.astype(F32), conv_w.astype(F32),
      pad(a_log), pad(dt_bias), a_norm_g.astype(F32).reshape(1, d))


def _hgrn2_kernel(q_ref, f_ref, i_ref, gate_ref, lb_ref, gn_ref, o_ref,
                  qs_ref, ks_ref, gc_ref, st_ref, *, ts):
    s = pl.program_id(1)
    c = CHUNK
    d = HEAD_DIM
    nh = N_HEADS
    SUB = 16

    @pl.when(s == 0)
    def _():
        st_ref[...] = jnp.zeros_like(st_ref)

    lb = lb_ref[...]
    f_raw = f_ref[...]
    log_sig = jnp.minimum(f_raw, 0.0) - jnp.log1p(jnp.exp(-jnp.abs(f_raw)))
    la = jnp.log(lb)
    lbb = jnp.log1p(-lb) + log_sig
    log_f = jnp.maximum(la, lbb) + jnp.log1p(jnp.exp(-jnp.abs(la - lbb)))
    qs_ref[...] = _silu(q_ref[...])
    ks_ref[...] = (1.0 - lb) * _sigmoid(-f_raw)

    row = _iota((c, c), 0)
    col = _iota((c, c), 1)
    tri_f = (col <= row).astype(F32)
    ones_dd = jnp.ones((d, d), BF16)
    rows_8d = _iota((8, d), 0)
    gnorm = gn_ref[...]

    tri2 = jnp.concatenate([tri_f, tri_f], axis=1).astype(BF16)
    for ci in range(ts // c):
        hi, lo = _split(log_f[ci * c:(ci + 1) * c, :])
        gc_ref[ci * c:(ci + 1) * c, :] = jnp.dot(tri2, jnp.concatenate([hi, lo], axis=0),
                                                 preferred_element_type=F32)

    blocks = [(sb * SUB, (sb + 1) * SUB) for sb in range(c // SUB)]

    def head_chunk(r0, hh):
        hs = slice(hh * d, (hh + 1) * d)
        q = qs_ref[pl.ds(r0, c), hs]
        k = ks_ref[pl.ds(r0, c), hs]
        v = i_ref[pl.ds(r0, c), hs]
        gc = gc_ref[pl.ds(r0, c), hs]

        prods = []
        for top, end in blocks:
            for j in range(top, end):
                lo = (j // 8) * 8
                k_j = k[j:j + 1, :]
                g_j = gc[j:j + 1, :]
                e = jnp.exp(gc[lo:end, :] - g_j)
                if j % 8:
                    head = jnp.where(rows_8d >= j - lo, e[:8], 0.0)
                    e = jnp.concatenate([head, e[8:]], axis=0) if lo + 8 < end else head
                prods.append(q[lo:end, :] * k_j * e)
        sums = jnp.dot(jnp.concatenate(prods, axis=0).astype(BF16), ones_dd,
                       preferred_element_type=F32)
        qk_far = []
        for top, end in blocks[1:]:
            g_b = gc[top - 1:top, :]
            qe = q[top:end, :] * jnp.exp(gc[top:end, :] - g_b)
            ke = k[:top, :] * jnp.exp(jnp.minimum(g_b - gc[:top, :], 0.0))
            qk_far.append(_mm_nt(qe, ke))
        far = [_mm(a, v[:top, :]) for a, (top, _) in zip(qk_far, blocks[1:])]

        groups = [jnp.zeros((8, d), F32) for _ in range(c // 8)]
        at = 0
        for top, end in blocks:
            for j in range(top, end):
                v_j = v[j:j + 1, :]
                for g in range(j // 8, end // 8):
                    groups[g] = groups[g] + sums[at:at + 8, :] * v_j
                    at += 8
        for f, (top, end) in zip(far, blocks[1:]):
            for g in range(top // 8, end // 8):
                groups[g] = groups[g] + f[(g * 8 - top):(g * 8 - top + 8), :]
        o_intra = jnp.concatenate(groups, axis=0)

        st = st_ref[hh]
        gl = gc[c - 1:c, :]
        o = o_intra + _mm_nt(q * jnp.exp(gc), st)
        st_ref[hh] = st * jnp.exp(gl) + _mm_tn(v, k * jnp.exp(gl - gc))
        o_ref[pl.ds(r0, c), hs] = _rms(o, gnorm) * _silu(gate_ref[pl.ds(r0, c), hs])

    def chunk_loop(ci, carry):
        r0 = pl.multiple_of(ci * c, c)
        for hh in range(nh):
            head_chunk(r0, hh)
        return carry

    lax.fori_loop(0, ts // c, chunk_loop, 0)


def _hgrn2(p32, lb, d_norm_g, *, ts, cols):
    bsz, s, _ = p32.shape
    d = HEAD_DIM
    nh = N_HEADS
    w = nh * d
    kernel = functools.partial(_hgrn2_kernel, ts=ts)
    tile = lambda name: pl.BlockSpec((None, ts, w), lambda b, i: (b, i, cols[name] // nh))
    return pl.pallas_call(
        kernel,
        grid=(bsz, s // ts),
        in_specs=[tile("qd"), tile("fd"), tile("id"), tile("gd"),
                  pl.BlockSpec((1, w), lambda b, i: (0, 0)),
                  pl.BlockSpec((1, d), lambda b, i: (0, 0))],
        out_specs=pl.BlockSpec((None, ts, w), lambda b, i: (b, i, 0)),
        out_shape=jax.ShapeDtypeStruct((bsz, s, w), F32),
        scratch_shapes=[pltpu.VMEM((ts, w), F32), pltpu.VMEM((ts, w), F32),
                        pltpu.VMEM((ts, w), F32), pltpu.VMEM((nh, d, d), F32)],
        compiler_params=pltpu.CompilerParams(
            dimension_semantics=("parallel", "arbitrary"), vmem_limit_bytes=VMEM_LIMIT),
        name="hgrn2",
    )(p32, p32, p32, p32, lb.astype(F32).reshape(1, w), d_norm_g.astype(F32).reshape(1, d))


def _stickbreak_kernel(q_ref, k_ref, v_ref, o_ref, *, tq):
    i = pl.program_id(1)
    d = HEAD_DIM
    nh = N_HEADS
    row = _iota((tq, tq), 0)
    col = _iota((tq, tq), 1)
    causal = col < row
    later = (row > col).astype(BF16)
    later2 = jnp.concatenate([later, later], axis=0)

    def scores(j, hs, diag):
        z = _mm_nt(q_ref[:, hs], k_ref[pl.ds(pl.multiple_of(j * tq, tq), tq), hs]) * (d ** -0.5)
        sp = _softplus(z)
        l1m = jnp.where(causal, -sp, 0.0) if diag else -sp
        rest = jnp.dot(jnp.concatenate(_split(l1m), axis=1), later2,
                       preferred_element_type=F32)
        return (z - sp) + rest, l1m

    def block(j, carries):
        out = []
        for hh in range(nh):
            hs = slice(hh * d, (hh + 1) * d)
            logw, l1m = scores(j, hs, False)
            o_ref[:, hs] += _mm(jnp.exp(logw + carries[hh]), v_ref[pl.ds(pl.multiple_of(j * tq, tq), tq), hs])
            out.append(carries[hh] + jnp.sum(l1m, axis=-1, keepdims=True))
        return tuple(out)

    jp = jnp.maximum(i - 1, 0)
    live = jnp.where(i > 0, 1.0, 0.0)
    heads = [slice(hh * d, (hh + 1) * d) for hh in range(nh)]
    sd = [scores(i, hs, True) for hs in heads]
    sp_ = [scores(jp, hs, False) for hs in heads]
    carries = []
    for hh, hs in enumerate(heads):
        c1 = jnp.sum(sd[hh][1], axis=-1, keepdims=True)
        p_d = jnp.where(causal, jnp.exp(sd[hh][0]), 0.0)
        p_p = jnp.exp(sp_[hh][0] + c1) * live
        o_ref[:, hs] = (_mm(p_d, v_ref[pl.ds(pl.multiple_of(i * tq, tq), tq), hs])
                        + _mm(p_p, v_ref[pl.ds(pl.multiple_of(jp * tq, tq), tq), hs]))
        carries.append(c1 + jnp.sum(sp_[hh][1], axis=-1, keepdims=True))
    carries = tuple(carries)

    def cond(c):
        worst = functools.reduce(jnp.maximum, c[1])
        return jnp.logical_and(c[0] >= 0, jnp.max(worst) >= EXP_ZERO_BELOW)

    def body(c):
        return c[0] - 1, block(c[0], c[1])

    lax.while_loop(cond, body, (i - 2, carries))


def _stickbreak(p16, *, tq, cols):
    bsz, s, _ = p16.shape
    nh = N_HEADS
    w = nh * HEAD_DIM
    kernel = functools.partial(_stickbreak_kernel, tq=tq)
    resident = dict(pipeline_mode=pl.Buffered(1))
    return pl.pallas_call(
        kernel,
        grid=(bsz, s // tq),
        in_specs=[pl.BlockSpec((None, tq, w), lambda b, i: (b, i, cols["qc"] // nh)),
                  pl.BlockSpec((None, s, w), lambda b, i: (b, 0, cols["kc"] // nh), **resident),
                  pl.BlockSpec((None, s, w), lambda b, i: (b, 0, cols["vc"] // nh), **resident)],
        out_specs=pl.BlockSpec((None, tq, w), lambda b, i: (b, i, 0)),
        out_shape=jax.ShapeDtypeStruct((bsz, s, w), F32),
        compiler_params=pltpu.CompilerParams(
            dimension_semantics=("parallel", "arbitrary"), vmem_limit_bytes=VMEM_LIMIT),
        name="stickbreak",
    )(p16, p16, p16)


def _dsa_kernel(qi_ref, smq_ref, q_ref, sm_ref, k_ref, vt_ref, bias_ref, o_ref,
                sc_ref, scb_ref, wb_ref, qc_ref, kct_ref, bd_ref, lg_ref, *, tq, k_sel, wi_lane, wide):
    i = pl.program_id(1)
    tk = tq
    d = HEAD_DIM
    nh = N_HEADS
    ksel = float(k_sel)
    per_wide = wide // tk
    n_wide = (i + per_wide) // per_wide
    sub = 2 * tk
    lane_q = _iota((1, tq), 1)

    def tree(parts, op):
        while len(parts) > 1:
            parts = [op(parts[j], parts[j + 1]) if j + 1 < len(parts) else parts[j]
                     for j in range(0, len(parts), 2)]
        return parts[0]

    def col_fold(x, op=jnp.add, rows=8):
        return tree([x[r * rows:(r + 1) * rows] for r in range(x.shape[0] // rows)], op)

    @pl.when(i == 0)
    def _():
        def prep(g, carry):
            g0 = pl.multiple_of(g * wide, wide)
            kt = sm_ref[pl.ds(g0, wide), :].T[:IDX_DIM, :]
            hi, lo = _split(kt)
            kct_ref[:, pl.ds(g0, wide)] = jnp.concatenate([hi, lo, hi], axis=0)
            return carry
        lax.fori_loop(0, sm_ref.shape[0] // wide, prep, 0)

    smq = smq_ref[...]
    lane = _iota(smq.shape, 1)
    for hh in range(IDX_HEADS):
        qh = qi_ref[:, hh * IDX_DIM:(hh + 1) * IDX_DIM]
        hi, lo = _split(qh)
        qc_ref[hh] = jnp.concatenate([hi, hi, lo], axis=-1)
        w = jnp.sum(jnp.where(lane == wi_lane + hh, smq, 0.0), axis=-1, keepdims=True)
        wb_ref[hh] = jnp.broadcast_to(w * ((IDX_HEADS ** -0.5) * (IDX_DIM ** -0.5)), (tq, tk))

    q2t = (q_ref[...] * ((d ** -0.5) * LOG2E)).T.astype(BF16)
    zero_dq = jnp.zeros((d, tq), BF16)
    for p in range(nh // 2):
        top = jnp.concatenate([q2t[2 * p * d:(2 * p + 1) * d], zero_dq], axis=1)
        bot = jnp.concatenate([zero_dq, q2t[(2 * p + 1) * d:(2 * p + 2) * d]], axis=1)
        bd_ref[p] = jnp.concatenate([top, bot], axis=0)

    limit = i * tq + (lane_q // CHUNK + 1) * CHUNK
    rows_t = _iota((tk, tq), 0)

    def score_group(g, mm, masked):
        mn, mx = mm
        for sb in range(wide // sub):
            k0 = pl.multiple_of(g * wide + sb * sub, sub)
            kct = kct_ref[:, pl.ds(k0, sub)]
            tiles = [jnp.zeros((tq, tk), F32) for _ in range(sub // tk)]
            for hh in range(IDX_HEADS):
                s_h = jnp.dot(qc_ref[hh], kct, preferred_element_type=F32)
                for ti in range(sub // tk):
                    tiles[ti] = tiles[ti] + jnp.maximum(s_h[:, ti * tk:(ti + 1) * tk], 0.0) * wb_ref[hh]
            for ti in range(sub // tk):
                kb = pl.multiple_of(k0 + ti * tk, tk)
                sct = tiles[ti].T
                if masked:
                    adm = (kb + rows_t) < limit
                    mn = jnp.minimum(mn, col_fold(jnp.where(adm, sct, jnp.inf), jnp.minimum))
                    sct = jnp.where(adm, sct, -jnp.inf)
                else:
                    mn = jnp.minimum(mn, col_fold(sct, jnp.minimum))
                mx = jnp.maximum(mx, col_fold(sct, jnp.maximum))
                sc_ref[pl.ds(kb, tk), :] = sct
                scb_ref[pl.ds(kb, tk), :] = _floor_bf16(sct)
        return mn, mx

    def score_pair(j, mm):
        return score_group(2 * j + 1, score_group(2 * j, mm, False), False)

    n_full = n_wide - 1
    mm = lax.fori_loop(0, n_full // 2, score_pair,
                       (jnp.full((8, tq), jnp.inf, F32), jnp.full((8, tq), -jnp.inf, F32)))
    mm = lax.cond(n_full % 2 == 1, lambda c: score_group(n_full - 1, c, False), lambda c: c, mm)
    mn, mx = score_group(n_wide - 1, mm, True)

    n_pairs = (n_wide + 1) // 2

    @pl.when(n_wide % 2 == 1)
    def _():
        sc_ref[pl.ds(pl.multiple_of(n_wide * wide, wide), wide), :] = jnp.full((wide, tq), -jnp.inf, F32)
        scb_ref[pl.ds(pl.multiple_of(n_wide * wide, wide), wide), :] = jnp.full((wide, tq), -jnp.inf, BF16)
    rmin = jnp.min(mn, axis=0, keepdims=True)
    rmax = jnp.max(mx, axis=0, keepdims=True)

    def count(pred):
        def body(j, acc):
            for g in (2 * j, 2 * j + 1):
                acc = acc + col_fold(pred(sc_ref[pl.ds(pl.multiple_of(g * wide, wide), wide), :]))
            return acc
        return jnp.sum(lax.fori_loop(0, n_pairs, body, jnp.zeros((8, tq), F32)), axis=0, keepdims=True)

    def max_below(x):
        def body(j, acc):
            for g in (2 * j, 2 * j + 1):
                blk = sc_ref[pl.ds(pl.multiple_of(g * wide, wide), wide), :]
                acc = jnp.maximum(acc, col_fold(jnp.where(blk < x, blk, -jnp.inf), jnp.maximum))
            return acc
        return jnp.max(lax.fori_loop(0, n_pairs, body, jnp.full((8, tq), -jnp.inf, F32)), axis=0, keepdims=True)

    n_adm = limit.astype(F32)
    all_sel = n_adm <= ksel

    def bisect(c):
        lo, hi, c_lo = c
        mid = 0.5 * lo + 0.5 * hi
        cm = count(lambda blk: _ind(blk >= mid))
        ge = cm >= ksel
        return jnp.where(ge, mid, lo), jnp.where(ge, hi, mid), jnp.where(ge, cm, c_lo)

    def pending(c_lo, tied):
        return jnp.where(all_sel, 0.0, jnp.where(tied > 0.5, 0.0, _ind(c_lo != ksel)))

    def bisect_coarse(_, c):
        lo, hi, c_lo = c
        mid = _floor_bf16(0.5 * lo + 0.5 * hi).astype(F32)
        t_b = jnp.broadcast_to(mid, (16, tq)).astype(BF16)
        one_b = jnp.ones((16, tq), BF16)
        zero_b = jnp.zeros((16, tq), BF16)

        def body(j, acc):
            for g in (2 * j, 2 * j + 1):
                blk = scb_ref[pl.ds(pl.multiple_of(g * wide, wide), wide), :]
                ind = [jnp.where(blk[r * 16:(r + 1) * 16] >= t_b, one_b, zero_b) for r in range(wide // 16)]
                acc = acc + tree(ind, jnp.add).astype(F32)
            return acc

        acc = lax.fori_loop(0, n_pairs, body, jnp.zeros((16, tq), F32))
        cm = jnp.sum(acc, axis=0, keepdims=True)
        ge = cm >= ksel
        return jnp.where(ge, mid, lo), jnp.where(ge, hi, mid), jnp.where(ge, cm, c_lo)

    lo0 = _floor_bf16(rmin).astype(F32)
    hi0 = _floor_bf16(rmax + (jnp.abs(rmax) * (2.0 ** -6) + 1e-30)).astype(F32)
    state = lax.fori_loop(0, BISECT_COARSE, bisect_coarse, (lo0, hi0, n_adm))
    state = lax.fori_loop(0, BISECT_FIXED, lambda _, c: bisect(c), state)

    def round_cond(c):
        return jnp.max(pending(c[0][2], c[1])) > 0.5

    def round_body(c):
        st, tied, v, need = c

        def more_cond(s):
            return jnp.logical_and(s[0] < BISECT_EXTRA, jnp.max(pending(s[1][2], tied)) > 0.5)

        _, st = lax.while_loop(more_cond, lambda s: (s[0] + 1, bisect(s[1])), (jnp.int32(0), st))
        pend = pending(st[2], tied)

        def check(_):
            cand = max_below(st[1])
            c_ge = count(lambda blk: _ind(blk >= cand))
            c_gt = count(lambda blk: _ind(blk > cand))
            ok = jnp.where(pend > 0.5, _ind(c_ge >= ksel), 0.0)
            return (jnp.where(ok > 0.5, 1.0, tied), jnp.where(ok > 0.5, cand, v),
                    jnp.where(ok > 0.5, ksel - c_gt, need))

        tied, v, need = lax.cond(jnp.max(pend) > 0.5, check, lambda _: (tied, v, need), 0)
        return st, tied, v, need

    zeros1 = jnp.zeros((1, tq), F32)
    (lo_f, _, _), tied, v_tie, need = lax.while_loop(round_cond, round_body, (state, zeros1, zeros1, zeros1))
    vth = jnp.where(all_sel, F32_LOWEST, jnp.where(tied > 0.5, v_tie, lo_f))

    @pl.when(jnp.max(tied) > 0.5)
    def _():
        v_eq = jnp.where(tied > 0.5, v_tie, jnp.inf)
        incl = (_iota((tk, tk), 1) <= _iota((tk, tk), 0)).astype(BF16)

        def demote(g, seen):
            g0 = pl.multiple_of(g * wide, wide)
            xs = [sc_ref[pl.ds(g0 + pb * tk, tk), :] for pb in range(per_wide)]
            eqs = [_ind(x == v_eq) for x in xs]
            inblk = [jnp.dot(incl, e.astype(BF16), preferred_element_type=F32) for e in eqs]
            for pb in range(per_wide):
                rank = inblk[pb] + seen
                sc_ref[pl.ds(g0 + pb * tk, tk), :] = jnp.where(eqs[pb] * _ind(rank > need) > 0.5,
                                                               -jnp.inf, xs[pb])
                seen = seen + jnp.sum(col_fold(eqs[pb]), axis=0, keepdims=True)
            return seen

        lax.fori_loop(0, n_wide, demote, zeros1)

    g_near = jnp.maximum(i - 1, 0) // per_wide

    def logit_group(g, mx, near):
        out = list(mx)
        for sb in range(wide // sub):
            k0 = pl.multiple_of(g * wide + sb * sub, sub)
            sel = sc_ref[pl.ds(k0, sub), :] >= vth
            for p in range(nh // 2):
                pair = jnp.dot(k_ref[pl.ds(k0, sub), 2 * p * d:(2 * p + 2) * d], bd_ref[p],
                               preferred_element_type=F32)
                for hh in (2 * p, 2 * p + 1):
                    lm = pair[:, (hh - 2 * p) * tq:(hh - 2 * p + 1) * tq]
                    if near:
                        back = [jnp.clip(i - (g * per_wide + sb * (sub // tk) + pb), 0, 2)
                                for pb in range(sub // tk)]
                        lm = lm + jnp.concatenate([bias_ref[bk, hh] for bk in back], axis=0)
                    lm = jnp.where(sel, lm, NEG_BIG)
                    lg_ref[hh, pl.ds(k0, sub), :] = lm
                    out[hh] = jnp.maximum(out[hh], col_fold(lm, jnp.maximum))
        return tuple(out)

    mx = tuple(jnp.full((8, tq), NEG_BIG, F32) for _ in range(nh))
    def logit_pair(j, mx, near):
        return logit_group(2 * j + 1, logit_group(2 * j, mx, near), near)

    far_pairs = g_near // 2
    mx = lax.fori_loop(0, far_pairs, functools.partial(logit_pair, near=False), mx)
    mx = lax.fori_loop(far_pairs, n_pairs, functools.partial(logit_pair, near=True), mx)
    m_q = [jnp.max(mx[hh], axis=0, keepdims=True) for hh in range(nh)]

    def pv_body(g, carry):
        g0 = pl.multiple_of(g * wide, wide)
        ls, accs = carry
        new_l, new_a = [], []
        for hh in range(nh):
            p = jnp.exp2(lg_ref[hh, pl.ds(g0, wide), :] - m_q[hh])
            lhs = jnp.concatenate([vt_ref[hh * d:(hh + 1) * d, pl.ds(g0, wide)], ones_rows], axis=0)
            out = jnp.dot(lhs, p.astype(BF16), preferred_element_type=F32)
            new_l.append(ls[hh] + out[d:])
            new_a.append(accs[hh] + out[:d])
        return tuple(new_l), tuple(new_a)

    ones_rows = jnp.ones((8, wide), BF16)

    ls, accs = lax.fori_loop(0, n_pairs, lambda j, c: pv_body(2 * j + 1, pv_body(2 * j, c)),
                             (tuple(jnp.zeros((8, tq), F32) for _ in range(nh)),
                              tuple(jnp.zeros((d, tq), F32) for _ in range(nh))))
    for hh in range(nh):
        o_ref[:, hh * d:(hh + 1) * d] = (accs[hh] / ls[hh][0:1]).T


def _dsa(p32, p16, vt, bias_tiles, *, tq, cols):
    bsz, s, _ = p32.shape
    d = HEAD_DIM
    nh = N_HEADS
    wide = 4 * tq
    k_sel = min(TOPK_MAX, s // 4)
    w512 = nh * d
    kernel = functools.partial(_dsa_kernel, tq=tq, k_sel=k_sel, wi_lane=cols["wi_lane"], wide=wide)
    resident = dict(pipeline_mode=pl.Buffered(1))
    return pl.pallas_call(
        kernel,
        grid=(bsz, s // tq),
        in_specs=[pl.BlockSpec((None, tq, w512), lambda b, i: (b, i, cols["qi"] // nh)),
                  pl.BlockSpec((None, tq, d), lambda b, i: (b, i, cols["small"])),
                  pl.BlockSpec((None, tq, w512), lambda b, i: (b, i, cols["qb"] // nh)),
                  pl.BlockSpec((None, s, d), lambda b, i: (b, 0, cols["small"]), **resident),
                  pl.BlockSpec((None, s, w512), lambda b, i: (b, 0, cols["kb"] // nh), **resident),
                  pl.BlockSpec((w512, s), lambda b, i: (0, b), **resident),
                  pl.BlockSpec((3, nh, tq, tq), lambda b, i: (0, 0, 0, 0), **resident)],
        out_specs=pl.BlockSpec((None, tq, w512), lambda b, i: (b, i, 0)),
        out_shape=jax.ShapeDtypeStruct((bsz, s, w512), F32),
        scratch_shapes=[pltpu.VMEM((s, tq), F32),
                        pltpu.VMEM((s, tq), BF16),
                        pltpu.VMEM((IDX_HEADS, tq, tq), F32),
                        pltpu.VMEM((IDX_HEADS, tq, 3 * IDX_DIM), BF16),
                        pltpu.VMEM((3 * IDX_DIM, s), BF16),
                        pltpu.VMEM((nh // 2, 2 * d, 2 * tq), BF16),
                        pltpu.VMEM((nh, s, tq), F32)],
        compiler_params=pltpu.CompilerParams(
            dimension_semantics=("parallel", "arbitrary"), vmem_limit_bytes=VMEM_LIMIT),
        name="dsa",
    )(p32, p32, p32, p32, p16, vt, bias_tiles)


def _t5_bucket(rel):
    nb = REL_BUCKETS // 2
    max_exact = nb // 2
    ret = jnp.where(rel > 0, nb, 0)
    n = jnp.abs(rel)
    large = max_exact + (jnp.log(jnp.maximum(n, 1).astype(F32) / max_exact)
                         / math.log(REL_MAX_DIST / max_exact) * (nb - max_exact)).astype(jnp.int32)
    large = jnp.minimum(large, nb - 1)
    return ret + jnp.where(n < max_exact, n, large)


def _bias_tiles(rel_table, tq):
    assert tq >= REL_MAX_DIST
    t = jnp.arange(tq)
    back = jnp.arange(3)
    rel = (t[None, None, :] - back[:, None, None] * tq) - t[None, :, None]
    onehot = (_t5_bucket(rel)[..., None] == jnp.arange(REL_BUCKETS)).astype(F32)
    tiles = jnp.einsum("bqkn,nh->bhkq", onehot, rel_table.astype(F32),
                       precision=HIGHEST)
    return (tiles - tiles[2:3]) * LOG2E


def _even_layout(w_in):
    d = HEAD_DIM
    a_w = 2 * N_HEADS * d + N_HEADS * d
    offs = {}
    o = 0
    for name, w in (("qkv", a_w), ("z", N_HEADS * d), ("a", N_HEADS), ("b", N_HEADS),
                    ("qb", N_HEADS * d), ("kb", N_HEADS * d), ("vb", N_HEADS * d),
                    ("qi", IDX_HEADS * IDX_DIM), ("ki", IDX_DIM), ("wi", IDX_HEADS)):
        offs[name] = (o, o + w)
        o += w
    assert o == w_in.shape[1]
    sl = lambda n: w_in[:, offs[n][0]:offs[n][1]]
    small_w = IDX_DIM + 2 * N_HEADS + IDX_HEADS
    small_pad = -small_w % d
    zeros = lambda n: jnp.zeros((w_in.shape[0], n), w_in.dtype)
    w32 = jnp.concatenate([sl("qkv"), sl("z"), sl("qb"), sl("qi"),
                           sl("ki"), sl("a"), sl("b"), sl("wi"), zeros(small_pad)], axis=1)
    n32 = w32.shape[1]
    tn = n32 // 5
    assert tn * 5 == n32 and tn % d == 0
    w16 = jnp.concatenate([sl("kb"), zeros(tn - N_HEADS * d)], axis=1)
    nh = N_HEADS
    cols = dict(qa=0, ka=nh, va=2 * nh, za=3 * nh, qb=4 * nh, qi=5 * nh, small=6 * nh, kb=0,
                a_lane=IDX_DIM, b_lane=IDX_DIM + nh, wi_lane=IDX_DIM + 2 * nh, n32=n32, tn=tn)
    return jnp.concatenate([w32, w16], axis=1).astype(BF16), sl("vb").T.astype(BF16), cols


def kernel(x, norm_g, w_in_even, conv_w_even, a_log_even, dt_bias_even, a_norm_even, w_out_even,
           rel_bias, w_in_odd, lb_logits, d_norm_odd, w_out_odd, w_gate, w_up, w_down):
    bsz, s, d = x.shape
    t = bsz * s
    depth = norm_g.shape[0]
    nh = N_HEADS
    tq = Q_TILE
    lb_all = jnp.cumsum(jax.nn.softmax(lb_logits.astype(F32), axis=0), axis=0)
    lb_all = lb_all - lb_all[:1]
    odd_cols = dict(qc=0, kc=nh, vc=2 * nh, qd=0, fd=nh, id=2 * nh, gd=3 * nh)
    bias_tiles = _bias_tiles(rel_bias, tq)

    h = x.reshape(t, d)
    for l in range(depth):
        if l % 2 == 0:
            e = l // 2
            w_even, w_vt, cols = _even_layout(w_in_even[e])
            p32, p16, vt = _norm_matmul(h, norm_g[l, 0], w_even, tm=ROW_TILE, tn=cols["tn"], n32=cols["n32"],
                                        w_t=w_vt)
            p32 = p32.reshape(bsz, s, -1)
            p16 = p16.reshape(bsz, s, -1)
            o_1 = _deltanet(p32, conv_w_even[e], a_log_even[e], dt_bias_even[e], a_norm_even[e],
                            ts=min(SEQ_TILE, s), cols=cols)
            o_2 = _dsa(p32, p16, vt, bias_tiles, tq=tq, cols=cols)
            w_out = w_out_even[e]
        else:
            o = l // 2
            n16 = 3 * nh * HEAD_DIM
            w_odd = jnp.concatenate([w_in_odd[o][:, n16:], w_in_odd[o][:, :n16]], axis=1).astype(BF16)
            p32, p16 = _norm_matmul(h, norm_g[l, 0], w_odd, tm=ROW_TILE, tn=ODD_COL_TILE, n32=w_odd.shape[1] - n16)
            p32 = p32.reshape(bsz, s, -1)
            p16 = p16.reshape(bsz, s, -1)
            o_1 = _stickbreak(p16, tq=tq, cols=odd_cols)
            o_2 = _hgrn2(p32, lb_all[l], d_norm_odd[o], ts=min(SEQ_TILE, s), cols=odd_cols)
            w_out = w_out_odd[o]
        h = _outproj(o_1.reshape(t, -1), o_2.reshape(t, -1), w_out, h, norm_g[l, 1], tm=OUT_TILE)
        h = _ffn(h, norm_g[l, 2], norm_g[l, 3], w_gate[l], w_up[l], w_down[l], tm=ROW_TILE, tf=FFN_TILE)
    return h.reshape(bsz, s, d)
```

```python
import functools
import math

import jax
import jax.numpy as jnp
from jax import lax
from jax.experimental import pallas as pl
from jax.experimental.pallas import tpu as pltpu

F32 = jnp.float32
BF16 = jnp.bfloat16
HIGHEST = lax.Precision.HIGHEST

CHUNK = 64
HEAD_DIM = 128
N_HEADS = 4
IDX_HEADS = 8
IDX_DIM = 64
TOPK_MAX = 256
CONV_WIDTH = 4
REL_BUCKETS = 32
REL_MAX_DIST = 128
EPS = 1e-6
NEG_BIG = -1e30
LOG2E = 1.4426950408889634
BISECT_COARSE = 12
BISECT_FIXED = 8
BISECT_EXTRA = 6
F32_LOWEST = -3.4028234663852886e38
EXP_ZERO_BELOW = -104.0
VMEM_LIMIT = 56 * 1024 * 1024

ROW_TILE = 1024
OUT_TILE = 512
SEQ_TILE = 512
Q_TILE = 128
ODD_COL_TILE = 512
FFN_TILE = 256


def _mm(a, b):
    return jnp.dot(a.astype(BF16), b.astype(BF16), preferred_element_type=F32)


def _mm_nt(a, b):
    return lax.dot_general(a.astype(BF16), b.astype(BF16), (((1,), (1,)), ((), ())),
                           preferred_element_type=F32)


def _mm_tn(a, b):
    return lax.dot_general(a.astype(BF16), b.astype(BF16), (((0,), (0,)), ((), ())),
                           preferred_element_type=F32)


def _split(x):
    hi = x.astype(BF16)
    return hi, (x - hi.astype(F32)).astype(BF16)


def _x3_parts(x):
    hi, lo = _split(x)
    return jnp.concatenate([hi, hi, lo], axis=1), jnp.concatenate([hi, lo, hi], axis=0)


def _mm_x3(a, b):
    return jnp.dot(_x3_parts(a)[0], _x3_parts(b)[1], preferred_element_type=F32)


def _floor_bf16(x):
    bits = pltpu.bitcast(x, jnp.int32)
    down = jnp.where(bits >= 0, bits, bits + 0xFFFF) & jnp.int32(-65536)
    return pltpu.bitcast(down, F32).astype(BF16)


def _sigmoid(x):
    return 1.0 / (1.0 + jnp.exp(-x))


def _silu(x):
    return x * _sigmoid(x)


def _softplus(x):
    return jnp.maximum(x, 0.0) + jnp.log1p(jnp.exp(-jnp.abs(x)))


def _rms(x, g):
    return x * lax.rsqrt(jnp.mean(x * x, axis=-1, keepdims=True) + EPS) * g


def _iota(shape, dim):
    return lax.broadcasted_iota(jnp.int32, shape, dim)


def _ind(mask):
    return jnp.where(mask, 1.0, 0.0)


def _norm_matmul_kernel(x_ref, g_ref, w_ref, *rest, n_t, tiles32):
    if n_t:
        wt_ref, o32_ref, o16_ref, ot_ref, xn_ref = rest
    else:
        o32_ref, o16_ref, xn_ref = rest
    j = pl.program_id(1)

    @pl.when(j == 0)
    def _():
        xn_ref[...] = _rms(x_ref[...], g_ref[...]).astype(BF16)
        if n_t:
            ot_ref[...] = lax.dot_general(wt_ref[...], xn_ref[...], (((1,), (1,)), ((), ())),
                                          preferred_element_type=F32).astype(BF16)

    y = jnp.dot(xn_ref[...], w_ref[...], preferred_element_type=F32)

    @pl.when(j < tiles32)
    def _():
        o32_ref[...] = y

    @pl.when(j >= tiles32)
    def _():
        o16_ref[...] = y.astype(BF16)


def _norm_matmul(x, g, w, *, tm, tn, n32, w_t=None):
    t, d = x.shape
    n = w.shape[1]
    n_t = 0 if w_t is None else w_t.shape[0]
    tiles32 = n32 // tn
    assert tiles32 * tn == n32 and (n - n32) % tn == 0 and 0 < n32 < n
    in_specs = [pl.BlockSpec((tm, d), lambda i, j: (i, 0)),
                pl.BlockSpec((1, d), lambda i, j: (0, 0)),
                pl.BlockSpec((d, tn), lambda i, j: (0, j))]
    out_specs = [pl.BlockSpec((tm, tn), lambda i, j: (i, jnp.minimum(j, tiles32 - 1))),
                 pl.BlockSpec((tm, tn), lambda i, j: (i, jnp.maximum(j - tiles32, 0)))]
    out_shape = [jax.ShapeDtypeStruct((t, n32), F32), jax.ShapeDtypeStruct((t, n - n32), BF16)]
    args = [x, g.reshape(1, d), w]
    if n_t:
        in_specs.append(pl.BlockSpec((n_t, d), lambda i, j: (0, 0)))
        out_specs.append(pl.BlockSpec((n_t, tm), lambda i, j: (0, i)))
        out_shape.append(jax.ShapeDtypeStruct((n_t, t), BF16))
        args.append(w_t)
    return pl.pallas_call(
        functools.partial(_norm_matmul_kernel, n_t=n_t, tiles32=tiles32),
        grid=(t // tm, n // tn),
        in_specs=in_specs,
        out_specs=out_specs,
        out_shape=out_shape,
        scratch_shapes=[pltpu.VMEM((tm, d), BF16)],
        compiler_params=pltpu.CompilerParams(
            dimension_semantics=("parallel", "arbitrary"), vmem_limit_bytes=VMEM_LIMIT),
        name="norm_matmul",
    )(*args)


def _outproj_kernel(ca_ref, cb_ref, wa_ref, wb_ref, h_ref, g_ref, o_ref):
    y = (jnp.dot(ca_ref[...].astype(BF16), wa_ref[...], preferred_element_type=F32)
         + jnp.dot(cb_ref[...].astype(BF16), wb_ref[...], preferred_element_type=F32))
    o_ref[...] = h_ref[...] + _rms(y, g_ref[...])


def _outproj(ca, cb, w, h, g, *, tm):
    t, d = h.shape
    wa_n = ca.shape[1]
    wb_n = cb.shape[1]
    wa = w[:wa_n].astype(BF16)
    wb = w[wa_n:].astype(BF16)
    return pl.pallas_call(
        _outproj_kernel,
        grid=(t // tm,),
        in_specs=[pl.BlockSpec((tm, wa_n), lambda i: (i, 0)),
                  pl.BlockSpec((tm, wb_n), lambda i: (i, 0)),
                  pl.BlockSpec((wa_n, d), lambda i: (0, 0)),
                  pl.BlockSpec((wb_n, d), lambda i: (0, 0)),
                  pl.BlockSpec((tm, d), lambda i: (i, 0)),
                  pl.BlockSpec((1, d), lambda i: (0, 0))],
        out_specs=pl.BlockSpec((tm, d), lambda i: (i, 0)),
        out_shape=jax.ShapeDtypeStruct((t, d), F32),
        compiler_params=pltpu.CompilerParams(
            dimension_semantics=("parallel",), vmem_limit_bytes=VMEM_LIMIT),
        name="outproj",
    )(ca, cb, wa, wb, h, g.reshape(1, d))


def _ffn_kernel(h_ref, gpre_ref, gpost_ref, wg_ref, wu_ref, wd_ref, o_ref, xn_ref, acc_ref):
    f = pl.program_id(1)

    @pl.when(f == 0)
    def _():
        xn_ref[...] = _rms(h_ref[...], gpre_ref[...]).astype(BF16)
        acc_ref[...] = jnp.zeros_like(acc_ref)

    xn = xn_ref[...]
    gate = jnp.dot(xn, wg_ref[...], preferred_element_type=F32)
    up = jnp.dot(xn, wu_ref[...], preferred_element_type=F32)
    act = (_silu(gate) * up).astype(BF16)
    acc_ref[...] += jnp.dot(act, wd_ref[...], preferred_element_type=F32)

    @pl.when(f == pl.num_programs(1) - 1)
    def _():
        o_ref[...] = h_ref[...] + _rms(acc_ref[...], gpost_ref[...])


def _ffn(h, g_pre, g_post, wg, wu, wd, *, tm, tf):
    t, d = h.shape
    ff = wg.shape[1]
    return pl.pallas_call(
        _ffn_kernel,
        grid=(t // tm, ff // tf),
        in_specs=[pl.BlockSpec((tm, d), lambda i, f: (i, 0)),
                  pl.BlockSpec((1, d), lambda i, f: (0, 0)),
                  pl.BlockSpec((1, d), lambda i, f: (0, 0)),
                  pl.BlockSpec((d, tf), lambda i, f: (0, f)),
                  pl.BlockSpec((d, tf), lambda i, f: (0, f)),
                  pl.BlockSpec((tf, d), lambda i, f: (f, 0))],
        out_specs=pl.BlockSpec((tm, d), lambda i, f: (i, 0)),
        out_shape=jax.ShapeDtypeStruct((t, d), F32),
        scratch_shapes=[pltpu.VMEM((tm, d), BF16), pltpu.VMEM((tm, d), F32)],
        compiler_params=pltpu.CompilerParams(
            dimension_semantics=("parallel", "arbitrary"), vmem_limit_bytes=VMEM_LIMIT),
        name="ffn",
    )(h, g_pre.reshape(1, d), g_post.reshape(1, d),
      wg.astype(BF16), wu.astype(BF16), wd.astype(BF16))


def _deltanet_kernel(xq_ref, xk_ref, xv_ref, z_ref, sm_ref, cwq_ref, cwk_ref, cwv_ref,
                     alog_ref, dtb_ref, gn_ref, o_ref,
                     xpad_ref, q_ref, k_ref, v_ref, gb_ref, bb_ref, u_ref, w_ref, qk_ref, st_ref,
                     *, ts, a_col, b_col):
    s = pl.program_id(1)
    c = CHUNK
    d = HEAD_DIM
    nh = N_HEADS

    @pl.when(s == 0)
    def _():
        xpad_ref[:, 0:8, :] = jnp.zeros((3, 8, nh * d), F32)
        st_ref[...] = jnp.zeros_like(st_ref)

    @pl.when(s != 0)
    def _():
        xpad_ref[:, 0:8, :] = xpad_ref[:, ts:ts + 8, :]

    xpad_ref[0, 8:ts + 8, :] = xq_ref[...]
    xpad_ref[1, 8:ts + 8, :] = xk_ref[...]
    xpad_ref[2, 8:ts + 8, :] = xv_ref[...]

    def conv_silu(idx, cw_ref, hs):
        cw = cw_ref[:, hs]
        acc = xpad_ref[idx, 8 - (CONV_WIDTH - 1):8 - (CONV_WIDTH - 1) + ts, hs] * cw[0:1, :]
        for j in range(1, CONV_WIDTH):
            off = 8 - (CONV_WIDTH - 1) + j
            acc = acc + xpad_ref[idx, off:off + ts, hs] * cw[j:j + 1, :]
        return _silu(acc)

    def l2norm(t):
        return t * lax.rsqrt(jnp.sum(t * t, axis=-1, keepdims=True) + EPS)

    row = _iota((c, c), 0)
    col = _iota((c, c), 1)
    tri = (col <= row)
    strict = (col < row)
    tri_f = tri.astype(F32)
    upper_f = (row <= col).astype(F32)
    eye = (row == col).astype(F32)
    gnorm = gn_ref[...]
    chunks = range(ts // c)
    rs = [slice(ci * c, (ci + 1) * c) for ci in chunks]
    tri2 = jnp.concatenate([tri_f, tri_f], axis=1).astype(BF16)
    ones2 = jnp.ones((c, 2 * c), BF16)

    def cum2(lhs2, x):
        hi, lo = _split(x)
        return jnp.dot(lhs2, jnp.concatenate([hi, lo], axis=0), preferred_element_type=F32)

    for hh in range(nh):
        hs = slice(hh * d, (hh + 1) * d)
        q_ref[:, hs] = l2norm(conv_silu(0, cwq_ref, hs)) * (d ** -0.5)
        k_ref[:, hs] = l2norm(conv_silu(1, cwk_ref, hs))
        v_ref[:, hs] = conv_silu(2, cwv_ref, hs)

        a_raw = sm_ref[:, a_col + hh:a_col + hh + 1]
        b_raw = sm_ref[:, b_col + hh:b_col + hh + 1]
        g = -jnp.exp(alog_ref[:, hh:hh + 1]) * _softplus(a_raw + dtb_ref[:, hh:hh + 1])
        gb_ref[:, hs] = jnp.broadcast_to(g, (ts, d))
        bb_ref[:, hs] = jnp.broadcast_to(_sigmoid(b_raw), (ts, d))

        q = [q_ref[r, hs] for r in rs]
        k = [k_ref[r, hs] for r in rs]
        beta = [bb_ref[r, hs] for r in rs]
        gb = [gb_ref[r, hs] for r in rs]
        gc = [cum2(tri2, x) for x in gb]
        gc_row = [cum2(ones2, x[:, :c] * upper_f) for x in gb]
        decay = [jnp.where(tri, jnp.exp(jnp.minimum(a[:, :c] - b, 0.0)), 0.0) for a, b in zip(gc, gc_row)]
        kk = [_mm_nt(x, x) for x in k]
        n = [-jnp.where(strict, b[:, :c] * x * dc, 0.0) for b, x, dc in zip(beta, kk, decay)]
        inv = [eye + x for x in n]
        n_parts = [_x3_parts(x) for x in n]
        for step in range(5):
            n = [jnp.dot(a, b, preferred_element_type=F32) for a, b in n_parts]
            n_parts = [_x3_parts(x) for x in n]
            inv = [iv + jnp.dot(_x3_parts(iv)[0], b, preferred_element_type=F32)
                   for iv, (_, b) in zip(inv, n_parts)]
        egc = [jnp.exp(x) for x in gc]
        gl = [x[c - 1:c, :] for x in gc]
        for ci in chunks:
            r = rs[ci]
            u_ref[r, hs] = _mm_x3(inv[ci], v_ref[r, hs] * beta[ci])
            w_ref[r, hs] = _mm_x3(inv[ci], k[ci] * (beta[ci] * egc[ci]))
            qk_ref[hh, r, :] = _mm_nt(q[ci], k[ci]) * decay[ci]
            q_ref[r, hs] = q[ci] * egc[ci]
            k_ref[r, hs] = k[ci] * jnp.exp(gl[ci] - gc[ci])
            gb_ref[r, hs] = jnp.broadcast_to(jnp.exp(gl[ci]), (c, d))

    def chunk_body(ci, carry):
        r0 = pl.multiple_of(ci * c, c)
        rows = pl.ds(r0, c)
        hss = [slice(hh * d, (hh + 1) * d) for hh in range(nh)]
        st = [st_ref[hh] for hh in range(nh)]
        w_st = [_mm(w_ref[rows, hs], s_) for hs, s_ in zip(hss, st)]
        q_st = [_mm(q_ref[rows, hs], s_) for hs, s_ in zip(hss, st)]
        v_new = [u_ref[rows, hs] - x for hs, x in zip(hss, w_st)]
        o = [a + _mm(qk_ref[hh, rows, :], v) for hh, (a, v) in enumerate(zip(q_st, v_new))]
        kv = [_mm_tn(k_ref[rows, hs], v) for hs, v in zip(hss, v_new)]
        for hh, hs in enumerate(hss):
            st_ref[hh] = st[hh] * gb_ref[pl.ds(r0, 1), hs] + kv[hh]
            o_ref[rows, hs] = _rms(o[hh], gnorm) * _silu(z_ref[rows, hs])
        return carry

    lax.fori_loop(0, ts // c, chunk_body, 0)


def _deltanet(p32, conv_w, a_log, dt_bias, a_norm_g, *, ts, cols):
    bsz, s, _ = p32.shape
    d = HEAD_DIM
    nh = N_HEADS
    w = nh * d
    pad = lambda t: jnp.pad(t.astype(F32), (0, d - t.shape[0])).reshape(1, d)
    kernel = functools.partial(_deltanet_kernel, ts=ts, a_col=cols["a_lane"], b_col=cols["b_lane"])
    tile = lambda name: pl.BlockSpec((None, ts, w), lambda b, i: (b, i, cols[name] // nh))
    conv = lambda k: pl.BlockSpec((CONV_WIDTH, w), lambda b, i: (0, k))
    row = pl.BlockSpec((1, d), lambda b, i: (0, 0))
    return pl.pallas_call(
        kernel,
        grid=(bsz, s // ts),
        in_specs=[tile("qa"), tile("ka"), tile("va"), tile("za"),
                  pl.BlockSpec((None, ts, d), lambda b, i: (b, i, cols["small"])),
                  conv(0), conv(1), conv(2), row, row, row],
        out_specs=pl.BlockSpec((None, ts, w), lambda b, i: (b, i, 0)),
        out_shape=jax.ShapeDtypeStruct((bsz, s, w), F32),
        scratch_shapes=[pltpu.VMEM((3, ts + 8, w), F32)]
        + [pltpu.VMEM((ts, w), F32) for _ in range(7)]
        + [pltpu.VMEM((nh, ts, CHUNK), F32), pltpu.VMEM((nh, d, d), F32)],
        compiler_params=pltpu.CompilerParams(
            dimension_semantics=("parallel", "arbitrary"), vmem_limit_bytes=VMEM_LIMIT),
        name="deltanet",
    )(p32, p32, p32, p32, p32, conv_w.astype(F32), conv_w.astype(F32), conv_w.astype(F32),
      pad(a_log), pad(dt_bias), a_norm_g.astype(F32).reshape(1, d))


def _hgrn2_kernel(q_ref, f_ref, i_ref, gate_ref, lb_ref, gn_ref, o_ref,
                  qs_ref, ks_ref, gc_ref, st_ref, *, ts):
    s = pl.program_id(1)
    c = CHUNK
    d = HEAD_DIM
    nh = N_HEADS
    SUB = 16

    @pl.when(s == 0)
    def _():
        st_ref[...] = jnp.zeros_like(st_ref)

    lb = lb_ref[...]
    f_raw = f_ref[...]
    log_sig = jnp.minimum(f_raw, 0.0) - jnp.log1p(jnp.exp(-jnp.abs(f_raw)))
    la = jnp.log(lb)
    lbb = jnp.log1p(-lb) + log_sig
    log_f = jnp.maximum(la, lbb) + jnp.log1p(jnp.exp(-jnp.abs(la - lbb)))
    qs_ref[...] = _silu(q_ref[...])
    ks_ref[...] = (1.0 - lb) * _sigmoid(-f_raw)

    row = _iota((c, c), 0)
    col = _iota((c, c), 1)
    tri_f = (col <= row).astype(F32)
    ones_dd = jnp.ones((d, d), BF16)
    rows_8d = _iota((8, d), 0)
    gnorm = gn_ref[...]

    tri2 = jnp.concatenate([tri_f, tri_f], axis=1).astype(BF16)
    for ci in range(ts // c):
        hi, lo = _split(log_f[ci * c:(ci + 1) * c, :])
        gc_ref[ci * c:(ci + 1) * c, :] = jnp.dot(tri2, jnp.concatenate([hi, lo], axis=0),
                                                 preferred_element_type=F32)

    blocks = [(sb * SUB, (sb + 1) * SUB) for sb in range(c // SUB)]

    def head_chunk(r0, hh):
        hs = slice(hh * d, (hh + 1) * d)
        q = qs_ref[pl.ds(r0, c), hs]
        k = ks_ref[pl.ds(r0, c), hs]
        v = i_ref[pl.ds(r0, c), hs]
        gc = gc_ref[pl.ds(r0, c), hs]

        prods = []
        for top, end in blocks:
            for j in range(top, end):
                lo = (j // 8) * 8
                k_j = k[j:j + 1, :]
                g_j = gc[j:j + 1, :]
                e = jnp.exp(gc[lo:end, :] - g_j)
                if j % 8:
                    head = jnp.where(rows_8d >= j - lo, e[:8], 0.0)
                    e = jnp.concatenate([head, e[8:]], axis=0) if lo + 8 < end else head
                prods.append(q[lo:end, :] * k_j * e)
        sums = jnp.dot(jnp.concatenate(prods, axis=0).astype(BF16), ones_dd,
                       preferred_element_type=F32)
        qk_far = []
        for top, end in blocks[1:]:
            g_b = gc[top - 1:top, :]
            qe = q[top:end, :] * jnp.exp(gc[top:end, :] - g_b)
            ke = k[:top, :] * jnp.exp(jnp.minimum(g_b - gc[:top, :], 0.0))
            qk_far.append(_mm_nt(qe, ke))
        far = [_mm(a, v[:top, :]) for a, (top, _) in zip(qk_far, blocks[1:])]

        groups = [jnp.zeros((8, d), F32) for _ in range(c // 8)]
        at = 0
        for top, end in blocks:
            for j in range(top, end):
                v_j = v[j:j + 1, :]
                for g in range(j // 8, end // 8):
                    groups[g] = groups[g] + sums[at:at + 8, :] * v_j
                    at += 8
        for f, (top, end) in zip(far, blocks[1:]):
            for g in range(top // 8, end // 8):
                groups[g] = groups[g] + f[(g * 8 - top):(g * 8 - top + 8), :]
        o_intra = jnp.concatenate(groups, axis=0)

        st = st_ref[hh]
        gl = gc[c - 1:c, :]
        o = o_intra + _mm_nt(q * jnp.exp(gc), st)
        st_ref[hh] = st * jnp.exp(gl) + _mm_tn(v, k * jnp.exp(gl - gc))
        o_ref[pl.ds(r0, c), hs] = _rms(o, gnorm) * _silu(gate_ref[pl.ds(r0, c), hs])

    def chunk_loop(ci, carry):
        r0 = pl.multiple_of(ci * c, c)
        for hh in range(nh):
            head_chunk(r0, hh)
        return carry

    lax.fori_loop(0, ts // c, chunk_loop, 0)


def _hgrn2(p32, lb, d_norm_g, *, ts, cols):
    bsz, s, _ = p32.shape
    d = HEAD_DIM
    nh = N_HEADS
    w = nh * d
    kernel = functools.partial(_hgrn2_kernel, ts=ts)
    tile = lambda name: pl.BlockSpec((None, ts, w), lambda b, i: (b, i, cols[name] // nh))
    return pl.pallas_call(
        kernel,
        grid=(bsz, s // ts),
        in_specs=[tile("qd"), tile("fd"), tile("id"), tile("gd"),
                  pl.BlockSpec((1, w), lambda b, i: (0, 0)),
                  pl.BlockSpec((1, d), lambda b, i: (0, 0))],
        out_specs=pl.BlockSpec((None, ts, w), lambda b, i: (b, i, 0)),
        out_shape=jax.ShapeDtypeStruct((bsz, s, w), F32),
        scratch_shapes=[pltpu.VMEM((ts, w), F32), pltpu.VMEM((ts, w), F32),
                        pltpu.VMEM((ts, w), F32), pltpu.VMEM((nh, d, d), F32)],
        compiler_params=pltpu.CompilerParams(
            dimension_semantics=("parallel", "arbitrary"), vmem_limit_bytes=VMEM_LIMIT),
        name="hgrn2",
    )(p32, p32, p32, p32, lb.astype(F32).reshape(1, w), d_norm_g.astype(F32).reshape(1, d))


def _stickbreak_kernel(q_ref, k_ref, v_ref, o_ref, *, tq):
    i = pl.program_id(1)
    d = HEAD_DIM
    nh = N_HEADS
    row = _iota((tq, tq), 0)
    col = _iota((tq, tq), 1)
    causal = col < row
    later = (row > col).astype(BF16)
    later2 = jnp.concatenate([later, later], axis=0)

    heads = [slice(hh * d, (hh + 1) * d) for hh in range(nh)]

    def scores(blocks):
        jobs = [(j, dg, hs) for j, dg in blocks for hs in heads]
        z = [_mm_nt(q_ref[:, hs], k_ref[pl.ds(pl.multiple_of(j * tq, tq), tq), hs]) * (d ** -0.5)
             for j, _, hs in jobs]
        sp = [_softplus(x) for x in z]
        l1m = [jnp.where(causal, -x, 0.0) if dg else -x for x, (_, dg, _) in zip(sp, jobs)]
        rest = [jnp.dot(jnp.concatenate(_split(x), axis=1), later2, preferred_element_type=F32)
                for x in l1m]
        out = [((a - b) + r, l) for a, b, r, l in zip(z, sp, rest, l1m)]
        return [out[b * nh:(b + 1) * nh] for b in range(len(blocks))]

    def block(j, carries):
        (sc,) = scores([(j, False)])
        ps = [jnp.exp(logw + c) for (logw, _), c in zip(sc, carries)]
        pv = [_mm(p, v_ref[pl.ds(pl.multiple_of(j * tq, tq), tq), hs]) for p, hs in zip(ps, heads)]
        for hs, x in zip(heads, pv):
            o_ref[:, hs] += x
        return tuple(c + jnp.sum(l1m, axis=-1, keepdims=True) for (_, l1m), c in zip(sc, carries))

    jp = jnp.maximum(i - 1, 0)
    live = jnp.where(i > 0, 1.0, 0.0)
    sd, sp_ = scores([(i, True), (jp, False)])
    carries = []
    for hh, hs in enumerate(heads):
        c1 = jnp.sum(sd[hh][1], axis=-1, keepdims=True)
        p_d = jnp.where(causal, jnp.exp(sd[hh][0]), 0.0)
        p_p = jnp.exp(sp_[hh][0] + c1) * live
        o_ref[:, hs] = (_mm(p_d, v_ref[pl.ds(pl.multiple_of(i * tq, tq), tq), hs])
                        + _mm(p_p, v_ref[pl.ds(pl.multiple_of(jp * tq, tq), tq), hs]))
        carries.append(c1 + jnp.sum(sp_[hh][1], axis=-1, keepdims=True))
    carries = tuple(carries)

    def cond(c):
        worst = functools.reduce(jnp.maximum, c[1])
        return jnp.logical_and(c[0] >= 0, jnp.max(worst) >= EXP_ZERO_BELOW)

    def body(c):
        return c[0] - 1, block(c[0], c[1])

    lax.while_loop(cond, body, (i - 2, carries))


def _stickbreak(p16, *, tq, cols):
    bsz, s, _ = p16.shape
    nh = N_HEADS
    w = nh * HEAD_DIM
    kernel = functools.partial(_stickbreak_kernel, tq=tq)
    resident = dict(pipeline_mode=pl.Buffered(1))
    return pl.pallas_call(
        kernel,
        grid=(bsz, s // tq),
        in_specs=[pl.BlockSpec((None, tq, w), lambda b, i: (b, i, cols["qc"] // nh)),
                  pl.BlockSpec((None, s, w), lambda b, i: (b, 0, cols["kc"] // nh), **resident),
                  pl.BlockSpec((None, s, w), lambda b, i: (b, 0, cols["vc"] // nh), **resident)],
        out_specs=pl.BlockSpec((None, tq, w), lambda b, i: (b, i, 0)),
        out_shape=jax.ShapeDtypeStruct((bsz, s, w), F32),
        compiler_params=pltpu.CompilerParams(
            dimension_semantics=("parallel", "arbitrary"), vmem_limit_bytes=VMEM_LIMIT),
        name="stickbreak",
    )(p16, p16, p16)


def _dsa_kernel(qi_ref, smq_ref, q_ref, sm_ref, k_ref, vt_ref, bias_ref, o_ref,
                sc_ref, scb_ref, wb_ref, qc_ref, kct_ref, bd_ref, lg_ref, *, tq, k_sel, wi_lane, wide):
    i = pl.program_id(1)
    tk = tq
    d = HEAD_DIM
    nh = N_HEADS
    ksel = float(k_sel)
    per_wide = wide // tk
    n_wide = (i + per_wide) // per_wide
    sub = 2 * tk
    lane_q = _iota((1, tq), 1)

    def tree(parts, op):
        while len(parts) > 1:
            parts = [op(parts[j], parts[j + 1]) if j + 1 < len(parts) else parts[j]
                     for j in range(0, len(parts), 2)]
        return parts[0]

    def col_fold(x, op=jnp.add, rows=8):
        return tree([x[r * rows:(r + 1) * rows] for r in range(x.shape[0] // rows)], op)

    @pl.when(i == 0)
    def _():
        def prep(g, carry):
            g0 = pl.multiple_of(g * wide, wide)
            kt = sm_ref[pl.ds(g0, wide), :].T[:IDX_DIM, :]
            hi, lo = _split(kt)
            kct_ref[:, pl.ds(g0, wide)] = jnp.concatenate([hi, lo, hi], axis=0)
            return carry
        lax.fori_loop(0, sm_ref.shape[0] // wide, prep, 0)

    smq = smq_ref[...]
    lane = _iota(smq.shape, 1)
    for hh in range(IDX_HEADS):
        qh = qi_ref[:, hh * IDX_DIM:(hh + 1) * IDX_DIM]
        hi, lo = _split(qh)
        qc_ref[hh] = jnp.concatenate([hi, hi, lo], axis=-1)
        w = jnp.sum(jnp.where(lane == wi_lane + hh, smq, 0.0), axis=-1, keepdims=True)
        wb_ref[hh] = jnp.broadcast_to(w * ((IDX_HEADS ** -0.5) * (IDX_DIM ** -0.5)), (tq, tk))

    q2t = (q_ref[...] * ((d ** -0.5) * LOG2E)).T.astype(BF16)
    zero_dq = jnp.zeros((d, tq), BF16)
    for p in range(nh // 2):
        top = jnp.concatenate([q2t[2 * p * d:(2 * p + 1) * d], zero_dq], axis=1)
        bot = jnp.concatenate([zero_dq, q2t[(2 * p + 1) * d:(2 * p + 2) * d]], axis=1)
        bd_ref[p] = jnp.concatenate([top, bot], axis=0)

    limit = i * tq + (lane_q // CHUNK + 1) * CHUNK
    rows_t = _iota((tk, tq), 0)

    def score_group(g, mm, masked):
        mn, mx = mm
        for sb in range(wide // sub):
            k0 = pl.multiple_of(g * wide + sb * sub, sub)
            kct = kct_ref[:, pl.ds(k0, sub)]
            tiles = [jnp.zeros((tq, tk), F32) for _ in range(sub // tk)]
            for hh in range(IDX_HEADS):
                s_h = jnp.dot(qc_ref[hh], kct, preferred_element_type=F32)
                for ti in range(sub // tk):
                    tiles[ti] = tiles[ti] + jnp.maximum(s_h[:, ti * tk:(ti + 1) * tk], 0.0) * wb_ref[hh]
            for ti in range(sub // tk):
                kb = pl.multiple_of(k0 + ti * tk, tk)
                sct = tiles[ti].T
                if masked:
                    adm = (kb + rows_t) < limit
                    mn = jnp.minimum(mn, col_fold(jnp.where(adm, sct, jnp.inf), jnp.minimum))
                    sct = jnp.where(adm, sct, -jnp.inf)
                else:
                    mn = jnp.minimum(mn, col_fold(sct, jnp.minimum))
                mx = jnp.maximum(mx, col_fold(sct, jnp.maximum))
                sc_ref[pl.ds(kb, tk), :] = sct
                scb_ref[pl.ds(kb, tk), :] = _floor_bf16(sct)
        return mn, mx

    def score_pair(j, mm):
        return score_group(2 * j + 1, score_group(2 * j, mm, False), False)

    n_full = n_wide - 1
    mm = lax.fori_loop(0, n_full // 2, score_pair,
                       (jnp.full((8, tq), jnp.inf, F32), jnp.full((8, tq), -jnp.inf, F32)))
    mm = lax.cond(n_full % 2 == 1, lambda c: score_group(n_full - 1, c, False), lambda c: c, mm)
    mn, mx = score_group(n_wide - 1, mm, True)

    n_pairs = (n_wide + 1) // 2

    @pl.when(n_wide % 2 == 1)
    def _():
        sc_ref[pl.ds(pl.multiple_of(n_wide * wide, wide), wide), :] = jnp.full((wide, tq), -jnp.inf, F32)
        scb_ref[pl.ds(pl.multiple_of(n_wide * wide, wide), wide), :] = jnp.full((wide, tq), -jnp.inf, BF16)
    rmin = jnp.min(mn, axis=0, keepdims=True)
    rmax = jnp.max(mx, axis=0, keepdims=True)

    def count(pred):
        def body(j, acc):
            for g in (2 * j, 2 * j + 1):
                acc = acc + col_fold(pred(sc_ref[pl.ds(pl.multiple_of(g * wide, wide), wide), :]))
            return acc
        return jnp.sum(lax.fori_loop(0, n_pairs, body, jnp.zeros((8, tq), F32)), axis=0, keepdims=True)

    def max_below(x):
        def body(j, acc):
            for g in (2 * j, 2 * j + 1):
                blk = sc_ref[pl.ds(pl.multiple_of(g * wide, wide), wide), :]
                acc = jnp.maximum(acc, col_fold(jnp.where(blk < x, blk, -jnp.inf), jnp.maximum))
            return acc
        return jnp.max(lax.fori_loop(0, n_pairs, body, jnp.full((8, tq), -jnp.inf, F32)), axis=0, keepdims=True)

    n_adm = limit.astype(F32)
    all_sel = n_adm <= ksel

    def bisect(c):
        lo, hi, c_lo = c
        mid = 0.5 * lo + 0.5 * hi
        cm = count(lambda blk: _ind(blk >= mid))
        ge = cm >= ksel
        return jnp.where(ge, mid, lo), jnp.where(ge, hi, mid), jnp.where(ge, cm, c_lo)

    def pending(c_lo, tied):
        return jnp.where(all_sel, 0.0, jnp.where(tied > 0.5, 0.0, _ind(c_lo != ksel)))

    def bisect_coarse(_, c):
        lo, hi, c_lo = c
        mid = _floor_bf16(0.5 * lo + 0.5 * hi).astype(F32)
        t_b = jnp.broadcast_to(mid, (16, tq)).astype(BF16)
        one_b = jnp.ones((16, tq), BF16)
        zero_b = jnp.zeros((16, tq), BF16)

        def body(j, acc):
            for g in (2 * j, 2 * j + 1):
                blk = scb_ref[pl.ds(pl.multiple_of(g * wide, wide), wide), :]
                ind = [jnp.where(blk[r * 16:(r + 1) * 16] >= t_b, one_b, zero_b) for r in range(wide // 16)]
                acc = acc + tree(ind, jnp.add).astype(F32)
            return acc

        acc = lax.fori_loop(0, n_pairs, body, jnp.zeros((16, tq), F32))
        cm = jnp.sum(acc, axis=0, keepdims=True)
        ge = cm >= ksel
        return jnp.where(ge, mid, lo), jnp.where(ge, hi, mid), jnp.where(ge, cm, c_lo)

    lo0 = _floor_bf16(rmin).astype(F32)
    hi0 = _floor_bf16(rmax + (jnp.abs(rmax) * (2.0 ** -6) + 1e-30)).astype(F32)
    state = lax.fori_loop(0, BISECT_COARSE, bisect_coarse, (lo0, hi0, n_adm))
    state = lax.fori_loop(0, BISECT_FIXED, lambda _, c: bisect(c), state)

    def round_cond(c):
        return jnp.max(pending(c[0][2], c[1])) > 0.5

    def round_body(c):
        st, tied, v, need = c

        def more_cond(s):
            return jnp.logical_and(s[0] < BISECT_EXTRA, jnp.max(pending(s[1][2], tied)) > 0.5)

        _, st = lax.while_loop(more_cond, lambda s: (s[0] + 1, bisect(s[1])), (jnp.int32(0), st))
        pend = pending(st[2], tied)

        def check(_):
            cand = max_below(st[1])
            c_ge = count(lambda blk: _ind(blk >= cand))
            c_gt = count(lambda blk: _ind(blk > cand))
            ok = jnp.where(pend > 0.5, _ind(c_ge >= ksel), 0.0)
            return (jnp.where(ok > 0.5, 1.0, tied), jnp.where(ok > 0.5, cand, v),
                    jnp.where(ok > 0.5, ksel - c_gt, need))

        tied, v, need = lax.cond(jnp.max(pend) > 0.5, check, lambda _: (tied, v, need), 0)
        return st, tied, v, need

    zeros1 = jnp.zeros((1, tq), F32)
    (lo_f, _, _), tied, v_tie, need = lax.while_loop(round_cond, round_body, (state, zeros1, zeros1, zeros1))
    vth = jnp.where(all_sel, F32_LOWEST, jnp.where(tied > 0.5, v_tie, lo_f))

    @pl.when(jnp.max(tied) > 0.5)
    def _():
        v_eq = jnp.where(tied > 0.5, v_tie, jnp.inf)
        incl = (_iota((tk, tk), 1) <= _iota((tk, tk), 0)).astype(BF16)

        def demote(g, seen):
            g0 = pl.multiple_of(g * wide, wide)
            xs = [sc_ref[pl.ds(g0 + pb * tk, tk), :] for pb in range(per_wide)]
            eqs = [_ind(x == v_eq) for x in xs]
            inblk = [jnp.dot(incl, e.astype(BF16), preferred_element_type=F32) for e in eqs]
            for pb in range(per_wide):
                rank = inblk[pb] + seen
                sc_ref[pl.ds(g0 + pb * tk, tk), :] = jnp.where(eqs[pb] * _ind(rank > need) > 0.5,
                                                               -jnp.inf, xs[pb])
                seen = seen + jnp.sum(col_fold(eqs[pb]), axis=0, keepdims=True)
            return seen

        lax.fori_loop(0, n_wide, demote, zeros1)

    g_near = jnp.maximum(i - 1, 0) // per_wide

    def logit_group(g, mx, near):
        out = list(mx)
        for sb in range(wide // sub):
            k0 = pl.multiple_of(g * wide + sb * sub, sub)
            sel = sc_ref[pl.ds(k0, sub), :] >= vth
            for p in range(nh // 2):
                pair = jnp.dot(k_ref[pl.ds(k0, sub), 2 * p * d:(2 * p + 2) * d], bd_ref[p],
                               preferred_element_type=F32)
                for hh in (2 * p, 2 * p + 1):
                    lm = pair[:, (hh - 2 * p) * tq:(hh - 2 * p + 1) * tq]
                    if near:
                        back = [jnp.clip(i - (g * per_wide + sb * (sub // tk) + pb), 0, 2)
                                for pb in range(sub // tk)]
                        lm = lm + jnp.concatenate([bias_ref[bk, hh] for bk in back], axis=0)
                    lm = jnp.where(sel, lm, NEG_BIG)
                    lg_ref[hh, pl.ds(k0, sub), :] = lm
                    out[hh] = jnp.maximum(out[hh], col_fold(lm, jnp.maximum))
        return tuple(out)

    mx = tuple(jnp.full((8, tq), NEG_BIG, F32) for _ in range(nh))
    def logit_pair(j, mx, near):
        return logit_group(2 * j + 1, logit_group(2 * j, mx, near), near)

    far_pairs = g_near // 2
    mx = lax.fori_loop(0, far_pairs, functools.partial(logit_pair, near=False), mx)
    mx = lax.fori_loop(far_pairs, n_pairs, functools.partial(logit_pair, near=True), mx)
    m_q = [jnp.max(mx[hh], axis=0, keepdims=True) for hh in range(nh)]

    def pv_body(g, carry):
        g0 = pl.multiple_of(g * wide, wide)
        ls, accs = carry
        new_l, new_a = [], []
        for hh in range(nh):
            p = jnp.exp2(lg_ref[hh, pl.ds(g0, wide), :] - m_q[hh])
            lhs = jnp.concatenate([vt_ref[hh * d:(hh + 1) * d, pl.ds(g0, wide)], ones_rows], axis=0)
            out = jnp.dot(lhs, p.astype(BF16), preferred_element_type=F32)
            new_l.append(ls[hh] + out[d:])
            new_a.append(accs[hh] + out[:d])
        return tuple(new_l), tuple(new_a)

    ones_rows = jnp.ones((8, wide), BF16)

    ls, accs = lax.fori_loop(0, n_pairs, lambda j, c: pv_body(2 * j + 1, pv_body(2 * j, c)),
                             (tuple(jnp.zeros((8, tq), F32) for _ in range(nh)),
                              tuple(jnp.zeros((d, tq), F32) for _ in range(nh))))
    for hh in range(nh):
        o_ref[:, hh * d:(hh + 1) * d] = (accs[hh] / ls[hh][0:1]).T


def _dsa(p32, p16, vt, bias_tiles, *, tq, cols):
    bsz, s, _ = p32.shape
    d = HEAD_DIM
    nh = N_HEADS
    wide = 4 * tq
    k_sel = min(TOPK_MAX, s // 4)
    w512 = nh * d
    kernel = functools.partial(_dsa_kernel, tq=tq, k_sel=k_sel, wi_lane=cols["wi_lane"], wide=wide)
    resident = dict(pipeline_mode=pl.Buffered(1))
    return pl.pallas_call(
        kernel,
        grid=(bsz, s // tq),
        in_specs=[pl.BlockSpec((None, tq, w512), lambda b, i: (b, i, cols["qi"] // nh)),
                  pl.BlockSpec((None, tq, d), lambda b, i: (b, i, cols["small"])),
                  pl.BlockSpec((None, tq, w512), lambda b, i: (b, i, cols["qb"] // nh)),
                  pl.BlockSpec((None, s, d), lambda b, i: (b, 0, cols["small"]), **resident),
                  pl.BlockSpec((None, s, w512), lambda b, i: (b, 0, cols["kb"] // nh), **resident),
                  pl.BlockSpec((w512, s), lambda b, i: (0, b), **resident),
                  pl.BlockSpec((3, nh, tq, tq), lambda b, i: (0, 0, 0, 0), **resident)],
        out_specs=pl.BlockSpec((None, tq, w512), lambda b, i: (b, i, 0)),
        out_shape=jax.ShapeDtypeStruct((bsz, s, w512), F32),
        scratch_shapes=[pltpu.VMEM((s, tq), F32),
                        pltpu.VMEM((s, tq), BF16),
                        pltpu.VMEM((IDX_HEADS, tq, tq), F32),
                        pltpu.VMEM((IDX_HEADS, tq, 3 * IDX_DIM), BF16),
                        pltpu.VMEM((3 * IDX_DIM, s), BF16),
                        pltpu.VMEM((nh // 2, 2 * d, 2 * tq), BF16),
                        pltpu.VMEM((nh, s, tq), F32)],
        compiler_params=pltpu.CompilerParams(
            dimension_semantics=("parallel", "arbitrary"), vmem_limit_bytes=VMEM_LIMIT),
        name="dsa",
    )(p32, p32, p32, p32, p16, vt, bias_tiles)


def _t5_bucket(rel):
    nb = REL_BUCKETS // 2
    max_exact = nb // 2
    ret = jnp.where(rel > 0, nb, 0)
    n = jnp.abs(rel)
    large = max_exact + (jnp.log(jnp.maximum(n, 1).astype(F32) / max_exact)
                         / math.log(REL_MAX_DIST / max_exact) * (nb - max_exact)).astype(jnp.int32)
    large = jnp.minimum(large, nb - 1)
    return ret + jnp.where(n < max_exact, n, large)


def _bias_tiles(rel_table, tq):
    assert tq >= REL_MAX_DIST
    t = jnp.arange(tq)
    back = jnp.arange(3)
    rel = (t[None, None, :] - back[:, None, None] * tq) - t[None, :, None]
    onehot = (_t5_bucket(rel)[..., None] == jnp.arange(REL_BUCKETS)).astype(F32)
    tiles = jnp.einsum("bqkn,nh->bhkq", onehot, rel_table.astype(F32),
                       precision=HIGHEST)
    return (tiles - tiles[2:3]) * LOG2E


def _even_layout(w_in):
    d = HEAD_DIM
    a_w = 2 * N_HEADS * d + N_HEADS * d
    offs = {}
    o = 0
    for name, w in (("qkv", a_w), ("z", N_HEADS * d), ("a", N_HEADS), ("b", N_HEADS),
                    ("qb", N_HEADS * d), ("kb", N_HEADS * d), ("vb", N_HEADS * d),
                    ("qi", IDX_HEADS * IDX_DIM), ("ki", IDX_DIM), ("wi", IDX_HEADS)):
        offs[name] = (o, o + w)
        o += w
    assert o == w_in.shape[1]
    sl = lambda n: w_in[:, offs[n][0]:offs[n][1]]
    small_w = IDX_DIM + 2 * N_HEADS + IDX_HEADS
    small_pad = -small_w % d
    zeros = lambda n: jnp.zeros((w_in.shape[0], n), w_in.dtype)
    w32 = jnp.concatenate([sl("qkv"), sl("z"), sl("qb"), sl("qi"),
                           sl("ki"), sl("a"), sl("b"), sl("wi"), zeros(small_pad)], axis=1)
    n32 = w32.shape[1]
    tn = n32 // 5
    assert tn * 5 == n32 and tn % d == 0
    w16 = jnp.concatenate([sl("kb"), zeros(tn - N_HEADS * d)], axis=1)
    nh = N_HEADS
    cols = dict(qa=0, ka=nh, va=2 * nh, za=3 * nh, qb=4 * nh, qi=5 * nh, small=6 * nh, kb=0,
                a_lane=IDX_DIM, b_lane=IDX_DIM + nh, wi_lane=IDX_DIM + 2 * nh, n32=n32, tn=tn)
    return jnp.concatenate([w32, w16], axis=1).astype(BF16), sl("vb").T.astype(BF16), cols


def kernel(x, norm_g, w_in_even, conv_w_even, a_log_even, dt_bias_even, a_norm_even, w_out_even,
           rel_bias, w_in_odd, lb_logits, d_norm_odd, w_out_odd, w_gate, w_up, w_down):
    bsz, s, d = x.shape
    t = bsz * s
    depth = norm_g.shape[0]
    nh = N_HEADS
    tq = Q_TILE
    lb_all = jnp.cumsum(jax.nn.softmax(lb_logits.astype(F32), axis=0), axis=0)
    lb_all = lb_all - lb_all[:1]
    odd_cols = dict(qc=0, kc=nh, vc=2 * nh, qd=0, fd=nh, id=2 * nh, gd=3 * nh)
    bias_tiles = _bias_tiles(rel_bias, tq)

    h = x.reshape(t, d)
    for l in range(depth):
        if l % 2 == 0:
            e = l // 2
            w_even, w_vt, cols = _even_layout(w_in_even[e])
            p32, p16, vt = _norm_matmul(h, norm_g[l, 0], w_even, tm=ROW_TILE, tn=cols["tn"], n32=cols["n32"],
                                        w_t=w_vt)
            p32 = p32.reshape(bsz, s, -1)
            p16 = p16.reshape(bsz, s, -1)
            o_1 = _deltanet(p32, conv_w_even[e], a_log_even[e], dt_bias_even[e], a_norm_even[e],
                            ts=min(SEQ_TILE, s), cols=cols)
            o_2 = _dsa(p32, p16, vt, bias_tiles, tq=tq, cols=cols)
            w_out = w_out_even[e]
        else:
            o = l // 2
            n16 = 3 * nh * HEAD_DIM
            w_odd = jnp.concatenate([w_in_odd[o][:, n16:], w_in_odd[o][:, :n16]], axis=1).astype(BF16)
            p32, p16 = _norm_matmul(h, norm_g[l, 0], w_odd, tm=ROW_TILE, tn=ODD_COL_TILE, n32=w_odd.shape[1] - n16)
            p32 = p32.reshape(bsz, s, -1)
            p16 = p16.reshape(bsz, s, -1)
            o_1 = _stickbreak(p16, tq=tq, cols=odd_cols)
            o_2 = _hgrn2(p32, lb_all[l], d_norm_odd[o], ts=min(SEQ_TILE, s), cols=odd_cols)
            w_out = w_out_odd[o]
        h = _outproj(o_1.reshape(t, -1), o_2.reshape(t, -1), w_out, h, norm_g[l, 1], tm=OUT_TILE)
        h = _ffn(h, norm_g[l, 2], norm_g[l, 3], w_gate[l], w_up[l], w_down[l], tm=ROW_TILE, tf=FFN_TILE)
    return h.reshape(bsz, s, d)
```

```python
import functools
import math

import jax
import jax.numpy as jnp
from jax import lax
from jax.experimental import pallas as pl
from jax.experimental.pallas import tpu as pltpu

F32 = jnp.float32
BF16 = jnp.bfloat16
HIGHEST = lax.Precision.HIGHEST

CHUNK = 64
HEAD_DIM = 128
N_HEADS = 4
IDX_HEADS = 8
IDX_DIM = 64
TOPK_MAX = 256
CONV_WIDTH = 4
REL_BUCKETS = 32
REL_MAX_DIST = 128
EPS = 1e-6
NEG_BIG = -1e30
LOG2E = 1.4426950408889634
BISECT_COARSE = 12
BISECT_FIXED = 8
BISECT_EXTRA = 6
F32_LOWEST = -3.4028234663852886e38
EXP_ZERO_BELOW = -104.0
VMEM_LIMIT = 56 * 1024 * 1024

ROW_TILE = 1024
OUT_TILE = 512
SEQ_TILE = 512
Q_TILE = 128
ODD_COL_TILE = 512
FFN_TILE = 256


def _mm(a, b):
    return jnp.dot(a.astype(BF16), b.astype(BF16), preferred_element_type=F32)


def _mm_nt(a, b):
    return lax.dot_general(a.astype(BF16), b.astype(BF16), (((1,), (1,)), ((), ())),
                           preferred_element_type=F32)


def _mm_tn(a, b):
    return lax.dot_general(a.astype(BF16), b.astype(BF16), (((0,), (0,)), ((), ())),
                           preferred_element_type=F32)


def _split(x):
    hi = x.astype(BF16)
    return hi, (x - hi.astype(F32)).astype(BF16)


def _floor_bf16(x):
    bits = pltpu.bitcast(x, jnp.int32)
    down = jnp.where(bits >= 0, bits, bits + 0xFFFF) & jnp.int32(-65536)
    return pltpu.bitcast(down, F32).astype(BF16)


def _sigmoid(x):
    return 1.0 / (1.0 + jnp.exp(-x))


def _silu(x):
    return x * _sigmoid(x)


def _softplus(x):
    return jnp.maximum(x, 0.0) + jnp.log1p(jnp.exp(-jnp.abs(x)))


def _rms(x, g):
    return x * lax.rsqrt(jnp.mean(x * x, axis=-1, keepdims=True) + EPS) * g


def _iota(shape, dim):
    return lax.broadcasted_iota(jnp.int32, shape, dim)


def _ind(mask):
    return jnp.where(mask, 1.0, 0.0)


def _norm_matmul_kernel(x_ref, g_ref, w_ref, *rest, n_t, tiles32):
    if n_t:
        wt_ref, o32_ref, o16_ref, ot_ref, xn_ref = rest
    else:
        o32_ref, o16_ref, xn_ref = rest
    j = pl.program_id(1)

    @pl.when(j == 0)
    def _():
        xn_ref[...] = _rms(x_ref[...], g_ref[...]).astype(BF16)
        if n_t:
            ot_ref[...] = lax.dot_general(wt_ref[...], xn_ref[...], (((1,), (1,)), ((), ())),
                                          preferred_element_type=F32).astype(BF16)

    y = jnp.dot(xn_ref[...], w_ref[...], preferred_element_type=F32)

    @pl.when(j < tiles32)
    def _():
        o32_ref[...] = y

    @pl.when(j >= tiles32)
    def _():
        o16_ref[...] = y.astype(BF16)


def _norm_matmul(x, g, w, *, tm, tn, n32, w_t=None):
    t, d = x.shape
    n = w.shape[1]
    n_t = 0 if w_t is None else w_t.shape[0]
    tiles32 = n32 // tn
    assert tiles32 * tn == n32 and (n - n32) % tn == 0 and 0 < n32 < n
    in_specs = [pl.BlockSpec((tm, d), lambda i, j: (i, 0)),
                pl.BlockSpec((1, d), lambda i, j: (0, 0)),
                pl.BlockSpec((d, tn), lambda i, j: (0, j))]
    out_specs = [pl.BlockSpec((tm, tn), lambda i, j: (i, jnp.minimum(j, tiles32 - 1))),
                 pl.BlockSpec((tm, tn), lambda i, j: (i, jnp.maximum(j - tiles32, 0)))]
    out_shape = [jax.ShapeDtypeStruct((t, n32), F32), jax.ShapeDtypeStruct((t, n - n32), BF16)]
    args = [x, g.reshape(1, d), w]
    if n_t:
        in_specs.append(pl.BlockSpec((n_t, d), lambda i, j: (0, 0)))
        out_specs.append(pl.BlockSpec((n_t, tm), lambda i, j: (0, i)))
        out_shape.append(jax.ShapeDtypeStruct((n_t, t), BF16))
        args.append(w_t)
    return pl.pallas_call(
        functools.partial(_norm_matmul_kernel, n_t=n_t, tiles32=tiles32),
        grid=(t // tm, n // tn),
        in_specs=in_specs,
        out_specs=out_specs,
        out_shape=out_shape,
        scratch_shapes=[pltpu.VMEM((tm, d), BF16)],
        compiler_params=pltpu.CompilerParams(
            dimension_semantics=("parallel", "arbitrary"), vmem_limit_bytes=VMEM_LIMIT),
        name="norm_matmul",
    )(*args)


def _outproj_kernel(ca_ref, cb_ref, wa_ref, wb_ref, h_ref, g_ref, o_ref):
    y = (jnp.dot(ca_ref[...].astype(BF16), wa_ref[...], preferred_element_type=F32)
         + jnp.dot(cb_ref[...].astype(BF16), wb_ref[...], preferred_element_type=F32))
    o_ref[...] = h_ref[...] + _rms(y, g_ref[...])


def _outproj(ca, cb, w, h, g, *, tm):
    t, d = h.shape
    wa_n = ca.shape[1]
    wb_n = cb.shape[1]
    wa = w[:wa_n].astype(BF16)
    wb = w[wa_n:].astype(BF16)
    return pl.pallas_call(
        _outproj_kernel,
        grid=(t // tm,),
        in_specs=[pl.BlockSpec((tm, wa_n), lambda i: (i, 0)),
                  pl.BlockSpec((tm, wb_n), lambda i: (i, 0)),
                  pl.BlockSpec((wa_n, d), lambda i: (0, 0)),
                  pl.BlockSpec((wb_n, d), lambda i: (0, 0)),
                  pl.BlockSpec((tm, d), lambda i: (i, 0)),
                  pl.BlockSpec((1, d), lambda i: (0, 0))],
        out_specs=pl.BlockSpec((tm, d), lambda i: (i, 0)),
        out_shape=jax.ShapeDtypeStruct((t, d), F32),
        compiler_params=pltpu.CompilerParams(
            dimension_semantics=("parallel",), vmem_limit_bytes=VMEM_LIMIT),
        name="outproj",
    )(ca, cb, wa, wb, h, g.reshape(1, d))


def _ffn_kernel(h_ref, gpre_ref, gpost_ref, wg_ref, wu_ref, wd_ref, o_ref, xn_ref, acc_ref):
    f = pl.program_id(1)

    @pl.when(f == 0)
    def _():
        xn_ref[...] = _rms(h_ref[...], gpre_ref[...]).astype(BF16)
        acc_ref[...] = jnp.zeros_like(acc_ref)

    xn = xn_ref[...]
    gate = jnp.dot(xn, wg_ref[...], preferred_element_type=F32)
    up = jnp.dot(xn, wu_ref[...], preferred_element_type=F32)
    act = (_silu(gate) * up).astype(BF16)
    acc_ref[...] += jnp.dot(act, wd_ref[...], preferred_element_type=F32)

    @pl.when(f == pl.num_programs(1) - 1)
    def _():
        o_ref[...] = h_ref[...] + _rms(acc_ref[...], gpost_ref[...])


def _ffn(h, g_pre, g_post, wg, wu, wd, *, tm, tf):
    t, d = h.shape
    ff = wg.shape[1]
    return pl.pallas_call(
        _ffn_kernel,
        grid=(t // tm, ff // tf),
        in_specs=[pl.BlockSpec((tm, d), lambda i, f: (i, 0)),
                  pl.BlockSpec((1, d), lambda i, f: (0, 0)),
                  pl.BlockSpec((1, d), lambda i, f: (0, 0)),
                  pl.BlockSpec((d, tf), lambda i, f: (0, f)),
                  pl.BlockSpec((d, tf), lambda i, f: (0, f)),
                  pl.BlockSpec((tf, d), lambda i, f: (f, 0))],
        out_specs=pl.BlockSpec((tm, d), lambda i, f: (i, 0)),
        out_shape=jax.ShapeDtypeStruct((t, d), F32),
        scratch_shapes=[pltpu.VMEM((tm, d), BF16), pltpu.VMEM((tm, d), F32)],
        compiler_params=pltpu.CompilerParams(
            dimension_semantics=("parallel", "arbitrary"), vmem_limit_bytes=VMEM_LIMIT),
        name="ffn",
    )(h, g_pre.reshape(1, d), g_post.reshape(1, d),
      wg.astype(BF16), wu.astype(BF16), wd.astype(BF16))


def _deltanet_kernel(xq_ref, xk_ref, xv_ref, z_ref, sm_ref, cwq_ref, cwk_ref, cwv_ref,
                     alog_ref, dtb_ref, gn_ref, o_ref,
                     xpad_ref, q_ref, k_ref, v_ref, gb_ref, bb_ref, u_ref, w_ref, qk_ref, st_ref,
                     *, ts, a_col, b_col):
    s = pl.program_id(1)
    c = CHUNK
    d = HEAD_DIM
    nh = N_HEADS

    @pl.when(s == 0)
    def _():
        xpad_ref[:, 0:8, :] = jnp.zeros((3, 8, nh * d), F32)
        st_ref[...] = jnp.zeros_like(st_ref)

    @pl.when(s != 0)
    def _():
        xpad_ref[:, 0:8, :] = xpad_ref[:, ts:ts + 8, :]

    xpad_ref[0, 8:ts + 8, :] = xq_ref[...]
    xpad_ref[1, 8:ts + 8, :] = xk_ref[...]
    xpad_ref[2, 8:ts + 8, :] = xv_ref[...]

    def conv_silu(idx, cw_ref, hs):
        cw = cw_ref[:, hs]
        acc = xpad_ref[idx, 8 - (CONV_WIDTH - 1):8 - (CONV_WIDTH - 1) + ts, hs] * cw[0:1, :]
        for j in range(1, CONV_WIDTH):
            off = 8 - (CONV_WIDTH - 1) + j
            acc = acc + xpad_ref[idx, off:off + ts, hs] * cw[j:j + 1, :]
        return _silu(acc)

    def l2norm(t):
        return t * lax.rsqrt(jnp.sum(t * t, axis=-1, keepdims=True) + EPS)

    row = _iota((c, c), 0)
    col = _iota((c, c), 1)
    tri = (col <= row)
    strict = (col < row)
    tri_f = tri.astype(F32)
    upper_f = (row <= col).astype(F32)
    eye = (row == col).astype(F32)
    gnorm = gn_ref[...]
    chunks = range(ts // c)
    rs = [slice(ci * c, (ci + 1) * c) for ci in chunks]
    tri2 = jnp.concatenate([tri_f, tri_f], axis=1).astype(BF16)
    ones2 = jnp.ones((c, 2 * c), BF16)

    def cum2(lhs2, x):
        hi, lo = _split(x)
        return jnp.dot(lhs2, jnp.concatenate([hi, lo], axis=0), preferred_element_type=F32)

    for hh in range(nh):
        hs = slice(hh * d, (hh + 1) * d)
        q_ref[:, hs] = l2norm(conv_silu(0, cwq_ref, hs)) * (d ** -0.5)
        k_ref[:, hs] = l2norm(conv_silu(1, cwk_ref, hs))
        v_ref[:, hs] = conv_silu(2, cwv_ref, hs)

        a_raw = sm_ref[:, a_col + hh:a_col + hh + 1]
        b_raw = sm_ref[:, b_col + hh:b_col + hh + 1]
        g = -jnp.exp(alog_ref[:, hh:hh + 1]) * _softplus(a_raw + dtb_ref[:, hh:hh + 1])
        gb_ref[:, hs] = jnp.broadcast_to(g, (ts, d))
        bb_ref[:, hs] = jnp.broadcast_to(_sigmoid(b_raw), (ts, d))

        q = [q_ref[r, hs] for r in rs]
        k = [k_ref[r, hs] for r in rs]
        beta = [bb_ref[r, hs] for r in rs]
        gb = [gb_ref[r, hs] for r in rs]
        gc = [cum2(tri2, x) for x in gb]
        gc_row = [cum2(ones2, x[:, :c] * upper_f) for x in gb]
        decay = [jnp.where(tri, jnp.exp(jnp.minimum(a[:, :c] - b, 0.0)), 0.0) for a, b in zip(gc, gc_row)]
        kk = [_mm_nt(x, x) for x in k]
        n = [-jnp.where(strict, b[:, :c] * x * dc, 0.0) for b, x, dc in zip(beta, kk, decay)]
        inv = [eye + x for x in n]
        for step in range(5):
            nb = [x.astype(BF16) for x in n]
            n = [jnp.dot(x, x, preferred_element_type=F32) for x in nb]
            inv = [iv + _mm(iv, x) for iv, x in zip(inv, n)]
        egc = [jnp.exp(x) for x in gc]
        gl = [x[c - 1:c, :] for x in gc]
        inv_l = [x.astype(BF16) for x in inv]
        u = [_mm(a, v_ref[r, hs] * b) for a, r, b in zip(inv_l, rs, beta)]
        w = [_mm(a, x * (b * e)) for a, x, b, e in zip(inv_l, k, beta, egc)]
        qk = [_mm_nt(a, b) * dc for a, b, dc in zip(q, k, decay)]
        for ci in chunks:
            r = rs[ci]
            u_ref[r, hs] = u[ci]
            w_ref[r, hs] = w[ci]
            qk_ref[hh, r, :] = qk[ci]
            q_ref[r, hs] = q[ci] * egc[ci]
            k_ref[r, hs] = k[ci] * jnp.exp(gl[ci] - gc[ci])
            gb_ref[r, hs] = jnp.broadcast_to(jnp.exp(gl[ci]), (c, d))

    def chunk_body(ci, carry):
        r0 = pl.multiple_of(ci * c, c)
        rows = pl.ds(r0, c)
        hss = [slice(hh * d, (hh + 1) * d) for hh in range(nh)]
        st = [st_ref[hh] for hh in range(nh)]
        w_st = [_mm(w_ref[rows, hs], s_) for hs, s_ in zip(hss, st)]
        q_st = [_mm(q_ref[rows, hs], s_) for hs, s_ in zip(hss, st)]
        v_new = [u_ref[rows, hs] - x for hs, x in zip(hss, w_st)]
        o = [a + _mm(qk_ref[hh, rows, :], v) for hh, (a, v) in enumerate(zip(q_st, v_new))]
        kv = [_mm_tn(k_ref[rows, hs], v) for hs, v in zip(hss, v_new)]
        for hh, hs in enumerate(hss):
            st_ref[hh] = st[hh] * gb_ref[pl.ds(r0, 1), hs] + kv[hh]
            o_ref[rows, hs] = _rms(o[hh], gnorm) * _silu(z_ref[rows, hs])
        return carry

    lax.fori_loop(0, ts // c, chunk_body, 0)


def _deltanet(p32, conv_w, a_log, dt_bias, a_norm_g, *, ts, cols):
    bsz, s, _ = p32.shape
    d = HEAD_DIM
    nh = N_HEADS
    w = nh * d
    pad = lambda t: jnp.pad(t.astype(F32), (0, d - t.shape[0])).reshape(1, d)
    kernel = functools.partial(_deltanet_kernel, ts=ts, a_col=cols["a_lane"], b_col=cols["b_lane"])
    tile = lambda name: pl.BlockSpec((None, ts, w), lambda b, i: (b, i, cols[name] // nh))
    conv = lambda k: pl.BlockSpec((CONV_WIDTH, w), lambda b, i: (0, k))
    row = pl.BlockSpec((1, d), lambda b, i: (0, 0))
    return pl.pallas_call(
        kernel,
        grid=(bsz, s // ts),
        in_specs=[tile("qa"), tile("ka"), tile("va"), tile("za"),
                  pl.BlockSpec((None, ts, d), lambda b, i: (b, i, cols["small"])),
                  conv(0), conv(1), conv(2), row, row, row],
        out_specs=pl.BlockSpec((None, ts, w), lambda b, i: (b, i, 0)),
        out_shape=jax.ShapeDtypeStruct((bsz, s, w), F32),
        scratch_shapes=[pltpu.VMEM((3, ts + 8, w), F32)]
        + [pltpu.VMEM((ts, w), F32) for _ in range(7)]
        + [pltpu.VMEM((nh, ts, CHUNK), F32), pltpu.VMEM((nh, d, d), F32)],
        compiler_params=pltpu.CompilerParams(
            dimension_semantics=("parallel", "arbitrary"), vmem_limit_bytes=VMEM_LIMIT),
        name="deltanet",
    )(p32, p32, p32, p32, p32, conv_w.astype(F32), conv_w.astype(F32), conv_w.astype(F32),
      pad(a_log), pad(dt_bias), a_norm_g.astype(F32).reshape(1, d))


def _hgrn2_kernel(q_ref, f_ref, i_ref, gate_ref, lb_ref, gn_ref, o_ref,
                  qs_ref, ks_ref, gc_ref, st_ref, *, ts):
    s = pl.program_id(1)
    c = CHUNK
    d = HEAD_DIM
    nh = N_HEADS
    SUB = 16

    @pl.when(s == 0)
    def _():
        st_ref[...] = jnp.zeros_like(st_ref)

    lb = lb_ref[...]
    f_raw = f_ref[...]
    log_sig = jnp.minimum(f_raw, 0.0) - jnp.log1p(jnp.exp(-jnp.abs(f_raw)))
    la = jnp.log(lb)
    lbb = jnp.log1p(-lb) + log_sig
    log_f = jnp.maximum(la, lbb) + jnp.log1p(jnp.exp(-jnp.abs(la - lbb)))
    qs_ref[...] = _silu(q_ref[...])
    ks_ref[...] = (1.0 - lb) * _sigmoid(-f_raw)

    row = _iota((c, c), 0)
    col = _iota((c, c), 1)
    tri_f = (col <= row).astype(F32)
    ones_dd = jnp.ones((d, d), BF16)
    rows_8d = _iota((8, d), 0)
    gnorm = gn_ref[...]

    tri2 = jnp.concatenate([tri_f, tri_f], axis=1).astype(BF16)
    for ci in range(ts // c):
        hi, lo = _split(log_f[ci * c:(ci + 1) * c, :])
        gc_ref[ci * c:(ci + 1) * c, :] = jnp.dot(tri2, jnp.concatenate([hi, lo], axis=0),
                                                 preferred_element_type=F32)

    blocks = [(sb * SUB, (sb + 1) * SUB) for sb in range(c // SUB)]

    def chunk_loop(ci, carry):
        r0 = pl.multiple_of(ci * c, c)
        rows = pl.ds(r0, c)
        hss = [slice(hh * d, (hh + 1) * d) for hh in range(nh)]
        q = [qs_ref[rows, hs] for hs in hss]
        k = [ks_ref[rows, hs] for hs in hss]
        v = [i_ref[rows, hs] for hs in hss]
        gc = [gc_ref[rows, hs] for hs in hss]

        def near_products(q, k, gc):
            prods = []
            for top, end in blocks:
                for j in range(top, end):
                    lo = (j // 8) * 8
                    e = jnp.exp(gc[lo:end, :] - gc[j:j + 1, :])
                    if j % 8:
                        head = jnp.where(rows_8d >= j - lo, e[:8], 0.0)
                        e = jnp.concatenate([head, e[8:]], axis=0) if lo + 8 < end else head
                    prods.append(q[lo:end, :] * k[j:j + 1, :] * e)
            return jnp.concatenate(prods, axis=0).astype(BF16)

        def far_operands(q, k, gc):
            out = []
            for top, end in blocks[1:]:
                g_b = gc[top - 1:top, :]
                out.append((q[top:end, :] * jnp.exp(gc[top:end, :] - g_b),
                            k[:top, :] * jnp.exp(jnp.minimum(g_b - gc[:top, :], 0.0))))
            return out

        near = [near_products(*x) for x in zip(q, k, gc)]
        far_ops = [far_operands(*x) for x in zip(q, k, gc)]
        st = [st_ref[hh] for hh in range(nh)]
        gl = [x[c - 1:c, :] for x in gc]
        sums = [jnp.dot(x, ones_dd, preferred_element_type=F32) for x in near]
        qk_far = [[_mm_nt(qe, ke) for qe, ke in ops] for ops in far_ops]
        far = [[_mm(a, vv[:top, :]) for a, (top, _) in zip(qs, blocks[1:])] for qs, vv in zip(qk_far, v)]
        o_st = [_mm_nt(a * jnp.exp(g), s_) for a, g, s_ in zip(q, gc, st)]
        kv = [_mm_tn(vv, kk * jnp.exp(g_l - g)) for vv, kk, g_l, g in zip(v, k, gl, gc)]

        for hh, hs in enumerate(hss):
            groups = [jnp.zeros((8, d), F32) for _ in range(c // 8)]
            at = 0
            for top, end in blocks:
                for j in range(top, end):
                    v_j = v[hh][j:j + 1, :]
                    for g in range(j // 8, end // 8):
                        groups[g] = groups[g] + sums[hh][at:at + 8, :] * v_j
                        at += 8
            for f, (top, end) in zip(far[hh], blocks[1:]):
                for g in range(top // 8, end // 8):
                    groups[g] = groups[g] + f[(g * 8 - top):(g * 8 - top + 8), :]
            o = jnp.concatenate(groups, axis=0) + o_st[hh]
            st_ref[hh] = st[hh] * jnp.exp(gl[hh]) + kv[hh]
            o_ref[rows, hs] = _rms(o, gnorm) * _silu(gate_ref[rows, hs])
        return carry

    lax.fori_loop(0, ts // c, chunk_loop, 0)


def _hgrn2(p32, lb, d_norm_g, *, ts, cols):
    bsz, s, _ = p32.shape
    d = HEAD_DIM
    nh = N_HEADS
    w = nh * d
    kernel = functools.partial(_hgrn2_kernel, ts=ts)
    tile = lambda name: pl.BlockSpec((None, ts, w), lambda b, i: (b, i, cols[name] // nh))
    return pl.pallas_call(
        kernel,
        grid=(bsz, s // ts),
        in_specs=[tile("qd"), tile("fd"), tile("id"), tile("gd"),
                  pl.BlockSpec((1, w), lambda b, i: (0, 0)),
                  pl.BlockSpec((1, d), lambda b, i: (0, 0))],
        out_specs=pl.BlockSpec((None, ts, w), lambda b, i: (b, i, 0)),
        out_shape=jax.ShapeDtypeStruct((bsz, s, w), F32),
        scratch_shapes=[pltpu.VMEM((ts, w), F32), pltpu.VMEM((ts, w), F32),
                        pltpu.VMEM((ts, w), F32), pltpu.VMEM((nh, d, d), F32)],
        compiler_params=pltpu.CompilerParams(
            dimension_semantics=("parallel", "arbitrary"), vmem_limit_bytes=VMEM_LIMIT),
        name="hgrn2",
    )(p32, p32, p32, p32, lb.astype(F32).reshape(1, w), d_norm_g.astype(F32).reshape(1, d))


def _stickbreak_kernel(q_ref, k_ref, v_ref, o_ref, *, tq):
    i = pl.program_id(1)
    d = HEAD_DIM
    nh = N_HEADS
    row = _iota((tq, tq), 0)
    col = _iota((tq, tq), 1)
    causal = col < row
    later = (row > col).astype(BF16)
    later2 = jnp.concatenate([later, later], axis=0)

    heads = [slice(hh * d, (hh + 1) * d) for hh in range(nh)]

    def scores(blocks):
        jobs = [(j, dg, hs) for j, dg in blocks for hs in heads]
        z = [_mm_nt(q_ref[:, hs], k_ref[pl.ds(pl.multiple_of(j * tq, tq), tq), hs]) * (d ** -0.5)
             for j, _, hs in jobs]
        sp = [_softplus(x) for x in z]
        l1m = [jnp.where(causal, -x, 0.0) if dg else -x for x, (_, dg, _) in zip(sp, jobs)]
        rest = [jnp.dot(jnp.concatenate(_split(x), axis=1), later2, preferred_element_type=F32)
                for x in l1m]
        out = [((a - b) + r, l) for a, b, r, l in zip(z, sp, rest, l1m)]
        return [out[b * nh:(b + 1) * nh] for b in range(len(blocks))]

    def block(j, carries):
        (sc,) = scores([(j, False)])
        ps = [jnp.exp(logw + c) for (logw, _), c in zip(sc, carries)]
        pv = [_mm(p, v_ref[pl.ds(pl.multiple_of(j * tq, tq), tq), hs]) for p, hs in zip(ps, heads)]
        for hs, x in zip(heads, pv):
            o_ref[:, hs] += x
        return tuple(c + jnp.sum(l1m, axis=-1, keepdims=True) for (_, l1m), c in zip(sc, carries))

    jp = jnp.maximum(i - 1, 0)
    live = jnp.where(i > 0, 1.0, 0.0)
    sd, sp_ = scores([(i, True), (jp, False)])
    carries = []
    for hh, hs in enumerate(heads):
        c1 = jnp.sum(sd[hh][1], axis=-1, keepdims=True)
        p_d = jnp.where(causal, jnp.exp(sd[hh][0]), 0.0)
        p_p = jnp.exp(sp_[hh][0] + c1) * live
        o_ref[:, hs] = (_mm(p_d, v_ref[pl.ds(pl.multiple_of(i * tq, tq), tq), hs])
                        + _mm(p_p, v_ref[pl.ds(pl.multiple_of(jp * tq, tq), tq), hs]))
        carries.append(c1 + jnp.sum(sp_[hh][1], axis=-1, keepdims=True))
    carries = tuple(carries)

    def cond(c):
        worst = functools.reduce(jnp.maximum, c[1])
        return jnp.logical_and(c[0] >= 0, jnp.max(worst) >= EXP_ZERO_BELOW)

    def body(c):
        return c[0] - 1, block(c[0], c[1])

    lax.while_loop(cond, body, (i - 2, carries))


def _stickbreak(p16, *, tq, cols):
    bsz, s, _ = p16.shape
    nh = N_HEADS
    w = nh * HEAD_DIM
    kernel = functools.partial(_stickbreak_kernel, tq=tq)
    resident = dict(pipeline_mode=pl.Buffered(1))
    return pl.pallas_call(
        kernel,
        grid=(bsz, s // tq),
        in_specs=[pl.BlockSpec((None, tq, w), lambda b, i: (b, i, cols["qc"] // nh)),
                  pl.BlockSpec((None, s, w), lambda b, i: (b, 0, cols["kc"] // nh), **resident),
                  pl.BlockSpec((None, s, w), lambda b, i: (b, 0, cols["vc"] // nh), **resident)],
        out_specs=pl.BlockSpec((None, tq, w), lambda b, i: (b, i, 0)),
        out_shape=jax.ShapeDtypeStruct((bsz, s, w), F32),
        compiler_params=pltpu.CompilerParams(
            dimension_semantics=("parallel", "arbitrary"), vmem_limit_bytes=VMEM_LIMIT),
        name="stickbreak",
    )(p16, p16, p16)


def _dsa_kernel(qi_ref, smq_ref, q_ref, sm_ref, k_ref, vt_ref, bias_ref, o_ref,
                sc_ref, scb_ref, wb_ref, qc_ref, kct_ref, bd_ref, lg_ref, *, tq, k_sel, wi_lane, wide):
    i = pl.program_id(1)
    tk = tq
    d = HEAD_DIM
    nh = N_HEADS
    ksel = float(k_sel)
    per_wide = wide // tk
    n_wide = (i + per_wide) // per_wide
    sub = 2 * tk
    lane_q = _iota((1, tq), 1)

    def tree(parts, op):
        while len(parts) > 1:
            parts = [op(parts[j], parts[j + 1]) if j + 1 < len(parts) else parts[j]
                     for j in range(0, len(parts), 2)]
        return parts[0]

    def col_fold(x, op=jnp.add, rows=8):
        return tree([x[r * rows:(r + 1) * rows] for r in range(x.shape[0] // rows)], op)

    @pl.when(i == 0)
    def _():
        def prep(g, carry):
            g0 = pl.multiple_of(g * wide, wide)
            kt = sm_ref[pl.ds(g0, wide), :].T[:IDX_DIM, :]
            hi, lo = _split(kt)
            kct_ref[:, pl.ds(g0, wide)] = jnp.concatenate([hi, lo, hi], axis=0)
            return carry
        lax.fori_loop(0, sm_ref.shape[0] // wide, prep, 0)

    smq = smq_ref[...]
    lane = _iota(smq.shape, 1)
    for hh in range(IDX_HEADS):
        qh = qi_ref[:, hh * IDX_DIM:(hh + 1) * IDX_DIM]
        hi, lo = _split(qh)
        qc_ref[hh] = jnp.concatenate([hi, hi, lo], axis=-1)
        w = jnp.sum(jnp.where(lane == wi_lane + hh, smq, 0.0), axis=-1, keepdims=True)
        wb_ref[hh] = jnp.broadcast_to(w * ((IDX_HEADS ** -0.5) * (IDX_DIM ** -0.5)), (tq, tk))

    q2t = (q_ref[...] * ((d ** -0.5) * LOG2E)).T.astype(BF16)
    zero_dq = jnp.zeros((d, tq), BF16)
    for p in range(nh // 2):
        top = jnp.concatenate([q2t[2 * p * d:(2 * p + 1) * d], zero_dq], axis=1)
        bot = jnp.concatenate([zero_dq, q2t[(2 * p + 1) * d:(2 * p + 2) * d]], axis=1)
        bd_ref[p] = jnp.concatenate([top, bot], axis=0)

    limit = i * tq + (lane_q // CHUNK + 1) * CHUNK
    rows_t = _iota((tk, tq), 0)

    def score_group(g, mm, masked):
        mn, mx = mm
        for sb in range(wide // sub):
            k0 = pl.multiple_of(g * wide + sb * sub, sub)
            kct = kct_ref[:, pl.ds(k0, sub)]
            tiles = [jnp.zeros((tq, tk), F32) for _ in range(sub // tk)]
            for hh in range(IDX_HEADS):
                s_h = jnp.dot(qc_ref[hh], kct, preferred_element_type=F32)
                for ti in range(sub // tk):
                    tiles[ti] = tiles[ti] + jnp.maximum(s_h[:, ti * tk:(ti + 1) * tk], 0.0) * wb_ref[hh]
            for ti in range(sub // tk):
                kb = pl.multiple_of(k0 + ti * tk, tk)
                sct = tiles[ti].T
                if masked:
                    adm = (kb + rows_t) < limit
                    mn = jnp.minimum(mn, col_fold(jnp.where(adm, sct, jnp.inf), jnp.minimum))
                    sct = jnp.where(adm, sct, -jnp.inf)
                else:
                    mn = jnp.minimum(mn, col_fold(sct, jnp.minimum))
                mx = jnp.maximum(mx, col_fold(sct, jnp.maximum))
                sc_ref[pl.ds(kb, tk), :] = sct
                scb_ref[pl.ds(kb, tk), :] = _floor_bf16(sct)
        return mn, mx

    def score_pair(j, mm):
        return score_group(2 * j + 1, score_group(2 * j, mm, False), False)

    n_full = n_wide - 1
    mm = lax.fori_loop(0, n_full // 2, score_pair,
                       (jnp.full((8, tq), jnp.inf, F32), jnp.full((8, tq), -jnp.inf, F32)))
    mm = lax.cond(n_full % 2 == 1, lambda c: score_group(n_full - 1, c, False), lambda c: c, mm)
    mn, mx = score_group(n_wide - 1, mm, True)

    n_pairs = (n_wide + 1) // 2

    @pl.when(n_wide % 2 == 1)
    def _():
        sc_ref[pl.ds(pl.multiple_of(n_wide * wide, wide), wide), :] = jnp.full((wide, tq), -jnp.inf, F32)
        scb_ref[pl.ds(pl.multiple_of(n_wide * wide, wide), wide), :] = jnp.full((wide, tq), -jnp.inf, BF16)
    rmin = jnp.min(mn, axis=0, keepdims=True)
    rmax = jnp.max(mx, axis=0, keepdims=True)

    def count(pred):
        def body(j, acc):
            for g in (2 * j, 2 * j + 1):
                acc = acc + col_fold(pred(sc_ref[pl.ds(pl.multiple_of(g * wide, wide), wide), :]))
            return acc
        return jnp.sum(lax.fori_loop(0, n_pairs, body, jnp.zeros((8, tq), F32)), axis=0, keepdims=True)

    def max_below(x):
        def body(j, acc):
            for g in (2 * j, 2 * j + 1):
                blk = sc_ref[pl.ds(pl.multiple_of(g * wide, wide), wide), :]
                acc = jnp.maximum(acc, col_fold(jnp.where(blk < x, blk, -jnp.inf), jnp.maximum))
            return acc
        return jnp.max(lax.fori_loop(0, n_pairs, body, jnp.full((8, tq), -jnp.inf, F32)), axis=0, keepdims=True)

    n_adm = limit.astype(F32)
    all_sel = n_adm <= ksel

    def bisect(c):
        lo, hi, c_lo = c
        mid = 0.5 * lo + 0.5 * hi
        cm = count(lambda blk: _ind(blk >= mid))
        ge = cm >= ksel
        return jnp.where(ge, mid, lo), jnp.where(ge, hi, mid), jnp.where(ge, cm, c_lo)

    def pending(c_lo, tied):
        return jnp.where(all_sel, 0.0, jnp.where(tied > 0.5, 0.0, _ind(c_lo != ksel)))

    def bisect_coarse(_, c):
        lo, hi, c_lo = c
        mid = _floor_bf16(0.5 * lo + 0.5 * hi).astype(F32)
        t_b = jnp.broadcast_to(mid, (16, tq)).astype(BF16)
        one_b = jnp.ones((16, tq), BF16)
        zero_b = jnp.zeros((16, tq), BF16)

        def body(j, acc):
            for g in (2 * j, 2 * j + 1):
                blk = scb_ref[pl.ds(pl.multiple_of(g * wide, wide), wide), :]
                ind = [jnp.where(blk[r * 16:(r + 1) * 16] >= t_b, one_b, zero_b) for r in range(wide // 16)]
                acc = acc + tree(ind, jnp.add).astype(F32)
            return acc

        acc = lax.fori_loop(0, n_pairs, body, jnp.zeros((16, tq), F32))
        cm = jnp.sum(acc, axis=0, keepdims=True)
        ge = cm >= ksel
        return jnp.where(ge, mid, lo), jnp.where(ge, hi, mid), jnp.where(ge, cm, c_lo)

    lo0 = _floor_bf16(rmin).astype(F32)
    hi0 = _floor_bf16(rmax + (jnp.abs(rmax) * (2.0 ** -6) + 1e-30)).astype(F32)
    state = lax.fori_loop(0, BISECT_COARSE, bisect_coarse, (lo0, hi0, n_adm))
    state = lax.fori_loop(0, BISECT_FIXED, lambda _, c: bisect(c), state)

    def round_cond(c):
        return jnp.max(pending(c[0][2], c[1])) > 0.5

    def round_body(c):
        st, tied, v, need = c

        def more_cond(s):
            return jnp.logical_and(s[0] < BISECT_EXTRA, jnp.max(pending(s[1][2], tied)) > 0.5)

        _, st = lax.while_loop(more_cond, lambda s: (s[0] + 1, bisect(s[1])), (jnp.int32(0), st))
        pend = pending(st[2], tied)

        def check(_):
            cand = max_below(st[1])
            c_ge = count(lambda blk: _ind(blk >= cand))
            c_gt = count(lambda blk: _ind(blk > cand))
            ok = jnp.where(pend > 0.5, _ind(c_ge >= ksel), 0.0)
            return (jnp.where(ok > 0.5, 1.0, tied), jnp.where(ok > 0.5, cand, v),
                    jnp.where(ok > 0.5, ksel - c_gt, need))

        tied, v, need = lax.cond(jnp.max(pend) > 0.5, check, lambda _: (tied, v, need), 0)
        return st, tied, v, need

    zeros1 = jnp.zeros((1, tq), F32)
    (lo_f, _, _), tied, v_tie, need = lax.while_loop(round_cond, round_body, (state, zeros1, zeros1, zeros1))
    vth = jnp.where(all_sel, F32_LOWEST, jnp.where(tied > 0.5, v_tie, lo_f))

    @pl.when(jnp.max(tied) > 0.5)
    def _():
        v_eq = jnp.where(tied > 0.5, v_tie, jnp.inf)
        incl = (_iota((tk, tk), 1) <= _iota((tk, tk), 0)).astype(BF16)

        def demote(g, seen):
            g0 = pl.multiple_of(g * wide, wide)
            xs = [sc_ref[pl.ds(g0 + pb * tk, tk), :] for pb in range(per_wide)]
            eqs = [_ind(x == v_eq) for x in xs]
            inblk = [jnp.dot(incl, e.astype(BF16), preferred_element_type=F32) for e in eqs]
            for pb in range(per_wide):
                rank = inblk[pb] + seen
                sc_ref[pl.ds(g0 + pb * tk, tk), :] = jnp.where(eqs[pb] * _ind(rank > need) > 0.5,
                                                               -jnp.inf, xs[pb])
                seen = seen + jnp.sum(col_fold(eqs[pb]), axis=0, keepdims=True)
            return seen

        lax.fori_loop(0, n_wide, demote, zeros1)

    g_near = jnp.maximum(i - 1, 0) // per_wide

    def logit_group(g, mx, near):
        out = list(mx)
        for sb in range(wide // sub):
            k0 = pl.multiple_of(g * wide + sb * sub, sub)
            sel = sc_ref[pl.ds(k0, sub), :] >= vth
            for p in range(nh // 2):
                pair = jnp.dot(k_ref[pl.ds(k0, sub), 2 * p * d:(2 * p + 2) * d], bd_ref[p],
                               preferred_element_type=F32)
                for hh in (2 * p, 2 * p + 1):
                    lm = pair[:, (hh - 2 * p) * tq:(hh - 2 * p + 1) * tq]
                    if near:
                        back = [jnp.clip(i - (g * per_wide + sb * (sub // tk) + pb), 0, 2)
                                for pb in range(sub // tk)]
                        lm = lm + jnp.concatenate([bias_ref[bk, hh] for bk in back], axis=0)
                    lm = jnp.where(sel, lm, NEG_BIG)
                    lg_ref[hh, pl.ds(k0, sub), :] = lm
                    out[hh] = jnp.maximum(out[hh], col_fold(lm, jnp.maximum))
        return tuple(out)

    mx = tuple(jnp.full((8, tq), NEG_BIG, F32) for _ in range(nh))
    def logit_pair(j, mx, near):
        return logit_group(2 * j + 1, logit_group(2 * j, mx, near), near)

    far_pairs = g_near // 2
    mx = lax.fori_loop(0, far_pairs, functools.partial(logit_pair, near=False), mx)
    mx = lax.fori_loop(far_pairs, n_pairs, functools.partial(logit_pair, near=True), mx)
    m_q = [jnp.max(mx[hh], axis=0, keepdims=True) for hh in range(nh)]

    ones_rows = jnp.ones((8, wide), BF16)

    def pv_pair(j, carry):
        ls, accs = list(carry[0]), list(carry[1])
        jobs = [(pl.multiple_of(g * wide, wide), hh) for g in (2 * j, 2 * j + 1) for hh in range(nh)]
        ps = [jnp.exp2(lg_ref[hh, pl.ds(g0, wide), :] - m_q[hh]).astype(BF16) for g0, hh in jobs]
        outs = [jnp.dot(jnp.concatenate([vt_ref[hh * d:(hh + 1) * d, pl.ds(g0, wide)], ones_rows], axis=0),
                        p, preferred_element_type=F32) for (g0, hh), p in zip(jobs, ps)]
        for (_, hh), out in zip(jobs, outs):
            ls[hh] = ls[hh] + out[d:]
            accs[hh] = accs[hh] + out[:d]
        return tuple(ls), tuple(accs)

    ls, accs = lax.fori_loop(0, n_pairs, pv_pair,
                             (tuple(jnp.zeros((8, tq), F32) for _ in range(nh)),
                              tuple(jnp.zeros((d, tq), F32) for _ in range(nh))))
    for hh in range(nh):
        o_ref[:, hh * d:(hh + 1) * d] = (accs[hh] / ls[hh][0:1]).T


def _dsa(p32, p16, vt, bias_tiles, *, tq, cols):
    bsz, s, _ = p32.shape
    d = HEAD_DIM
    nh = N_HEADS
    wide = 4 * tq
    k_sel = min(TOPK_MAX, s // 4)
    w512 = nh * d
    kernel = functools.partial(_dsa_kernel, tq=tq, k_sel=k_sel, wi_lane=cols["wi_lane"], wide=wide)
    resident = dict(pipeline_mode=pl.Buffered(1))
    return pl.pallas_call(
        kernel,
        grid=(bsz, s // tq),
        in_specs=[pl.BlockSpec((None, tq, w512), lambda b, i: (b, i, cols["qi"] // nh)),
                  pl.BlockSpec((None, tq, d), lambda b, i: (b, i, cols["small"])),
                  pl.BlockSpec((None, tq, w512), lambda b, i: (b, i, cols["qb"] // nh)),
                  pl.BlockSpec((None, s, d), lambda b, i: (b, 0, cols["small"]), **resident),
                  pl.BlockSpec((None, s, w512), lambda b, i: (b, 0, cols["kb"] // nh), **resident),
                  pl.BlockSpec((w512, s), lambda b, i: (0, b), **resident),
                  pl.BlockSpec((3, nh, tq, tq), lambda b, i: (0, 0, 0, 0), **resident)],
        out_specs=pl.BlockSpec((None, tq, w512), lambda b, i: (b, i, 0)),
        out_shape=jax.ShapeDtypeStruct((bsz, s, w512), F32),
        scratch_shapes=[pltpu.VMEM((s, tq), F32),
                        pltpu.VMEM((s, tq), BF16),
                        pltpu.VMEM((IDX_HEADS, tq, tq), F32),
                        pltpu.VMEM((IDX_HEADS, tq, 3 * IDX_DIM), BF16),
                        pltpu.VMEM((3 * IDX_DIM, s), BF16),
                        pltpu.VMEM((nh // 2, 2 * d, 2 * tq), BF16),
                        pltpu.VMEM((nh, s, tq), F32)],
        compiler_params=pltpu.CompilerParams(
            dimension_semantics=("parallel", "arbitrary"), vmem_limit_bytes=VMEM_LIMIT),
        name="dsa",
    )(p32, p32, p32, p32, p16, vt, bias_tiles)


def _t5_bucket(rel):
    nb = REL_BUCKETS // 2
    max_exact = nb // 2
    ret = jnp.where(rel > 0, nb, 0)
    n = jnp.abs(rel)
    large = max_exact + (jnp.log(jnp.maximum(n, 1).astype(F32) / max_exact)
                         / math.log(REL_MAX_DIST / max_exact) * (nb - max_exact)).astype(jnp.int32)
    large = jnp.minimum(large, nb - 1)
    return ret + jnp.where(n < max_exact, n, large)


def _bias_tiles(rel_table, tq):
    assert tq >= REL_MAX_DIST
    t = jnp.arange(tq)
    back = jnp.arange(3)
    rel = (t[None, None, :] - back[:, None, None] * tq) - t[None, :, None]
    onehot = (_t5_bucket(rel)[..., None] == jnp.arange(REL_BUCKETS)).astype(F32)
    tiles = jnp.einsum("bqkn,nh->bhkq", onehot, rel_table.astype(F32),
                       precision=HIGHEST)
    return (tiles - tiles[2:3]) * LOG2E


def _even_layout(w_in):
    d = HEAD_DIM
    a_w = 2 * N_HEADS * d + N_HEADS * d
    offs = {}
    o = 0
    for name, w in (("qkv", a_w), ("z", N_HEADS * d), ("a", N_HEADS), ("b", N_HEADS),
                    ("qb", N_HEADS * d), ("kb", N_HEADS * d), ("vb", N_HEADS * d),
                    ("qi", IDX_HEADS * IDX_DIM), ("ki", IDX_DIM), ("wi", IDX_HEADS)):
        offs[name] = (o, o + w)
        o += w
    assert o == w_in.shape[1]
    sl = lambda n: w_in[:, offs[n][0]:offs[n][1]]
    small_w = IDX_DIM + 2 * N_HEADS + IDX_HEADS
    small_pad = -small_w % d
    zeros = lambda n: jnp.zeros((w_in.shape[0], n), w_in.dtype)
    w32 = jnp.concatenate([sl("qkv"), sl("z"), sl("qb"), sl("qi"),
                           sl("ki"), sl("a"), sl("b"), sl("wi"), zeros(small_pad)], axis=1)
    n32 = w32.shape[1]
    tn = n32 // 5
    assert tn * 5 == n32 and tn % d == 0
    w16 = jnp.concatenate([sl("kb"), zeros(tn - N_HEADS * d)], axis=1)
    nh = N_HEADS
    cols = dict(qa=0, ka=nh, va=2 * nh, za=3 * nh, qb=4 * nh, qi=5 * nh, small=6 * nh, kb=0,
                a_lane=IDX_DIM, b_lane=IDX_DIM + nh, wi_lane=IDX_DIM + 2 * nh, n32=n32, tn=tn)
    return jnp.concatenate([w32, w16], axis=1).astype(BF16), sl("vb").T.astype(BF16), cols


def kernel(x, norm_g, w_in_even, conv_w_even, a_log_even, dt_bias_even, a_norm_even, w_out_even,
           rel_bias, w_in_odd, lb_logits, d_norm_odd, w_out_odd, w_gate, w_up, w_down):
    bsz, s, d = x.shape
    t = bsz * s
    depth = norm_g.shape[0]
    nh = N_HEADS
    tq = Q_TILE
    lb_all = jnp.cumsum(jax.nn.softmax(lb_logits.astype(F32), axis=0), axis=0)
    lb_all = lb_all - lb_all[:1]
    odd_cols = dict(qc=0, kc=nh, vc=2 * nh, qd=0, fd=nh, id=2 * nh, gd=3 * nh)
    bias_tiles = _bias_tiles(rel_bias, tq)

    h = x.reshape(t, d)
    for l in range(depth):
        if l % 2 == 0:
            e = l // 2
            w_even, w_vt, cols = _even_layout(w_in_even[e])
            p32, p16, vt = _norm_matmul(h, norm_g[l, 0], w_even, tm=ROW_TILE, tn=cols["tn"], n32=cols["n32"],
                                        w_t=w_vt)
            p32 = p32.reshape(bsz, s, -1)
            p16 = p16.reshape(bsz, s, -1)
            o_1 = _deltanet(p32, conv_w_even[e], a_log_even[e], dt_bias_even[e], a_norm_even[e],
                            ts=min(SEQ_TILE, s), cols=cols)
            o_2 = _dsa(p32, p16, vt, bias_tiles, tq=tq, cols=cols)
            w_out = w_out_even[e]
        else:
            o = l // 2
            n16 = 3 * nh * HEAD_DIM
            w_odd = jnp.concatenate([w_in_odd[o][:, n16:], w_in_odd[o][:, :n16]], axis=1).astype(BF16)
            p32, p16 = _norm_matmul(h, norm_g[l, 0], w_odd, tm=ROW_TILE, tn=ODD_COL_TILE, n32=w_odd.shape[1] - n16)
            p32 = p32.reshape(bsz, s, -1)
            p16 = p16.reshape(bsz, s, -1)
            o_1 = _stickbreak(p16, tq=tq, cols=odd_cols)
            o_2 = _hgrn2(p32, lb_all[l], d_norm_odd[o], ts=min(SEQ_TILE, s), cols=odd_cols)
            w_out = w_out_odd[o]
        h = _outproj(o_1.reshape(t, -1), o_2.reshape(t, -1), w_out, h, norm_g[l, 1], tm=OUT_TILE)
        h = _ffn(h, norm_g[l, 2], norm_g[l, 3], w_gate[l], w_up[l], w_down[l], tm=ROW_TILE, tf=FFN_TILE)
    return h.reshape(bsz, s, d)
```

```python
import functools
import math

import jax
import jax.numpy as jnp
from jax import lax
from jax.experimental import pallas as pl
from jax.experimental.pallas import tpu as pltpu

F32 = jnp.float32
BF16 = jnp.bfloat16
HIGHEST = lax.Precision.HIGHEST

CHUNK = 64
HEAD_DIM = 128
N_HEADS = 4
IDX_HEADS = 8
IDX_DIM = 64
TOPK_MAX = 256
CONV_WIDTH = 4
REL_BUCKETS = 32
REL_MAX_DIST = 128
EPS = 1e-6
NEG_BIG = -1e30
LOG2E = 1.4426950408889634
BISECT_COARSE = 12
BISECT_FIXED = 8
BISECT_EXTRA = 6
F32_LOWEST = -3.4028234663852886e38
EXP_ZERO_BELOW = -104.0
VMEM_LIMIT = 56 * 1024 * 1024

ROW_TILE = 1024
SEQ_TILE = 512
Q_TILE = 128
ODD_COL_TILE = 512
FFN_TILE = 256


def _mm(a, b):
    return jnp.dot(a.astype(BF16), b.astype(BF16), preferred_element_type=F32)


def _mm_nt(a, b):
    return lax.dot_general(a.astype(BF16), b.astype(BF16), (((1,), (1,)), ((), ())),
                           preferred_element_type=F32)


def _mm_tn(a, b):
    return lax.dot_general(a.astype(BF16), b.astype(BF16), (((0,), (0,)), ((), ())),
                           preferred_element_type=F32)


def _split(x):
    hi = x.astype(BF16)
    return hi, (x - hi.astype(F32)).astype(BF16)


def _floor_bf16(x):
    bits = pltpu.bitcast(x, jnp.int32)
    down = jnp.where(bits >= 0, bits, bits + 0xFFFF) & jnp.int32(-65536)
    return pltpu.bitcast(down, F32).astype(BF16)


def _sigmoid(x):
    return 1.0 / (1.0 + jnp.exp(-x))


def _silu(x):
    return x * _sigmoid(x)


def _softplus(x):
    return jnp.maximum(x, 0.0) + jnp.log1p(jnp.exp(-jnp.abs(x)))


def _rms(x, g):
    return x * lax.rsqrt(jnp.mean(x * x, axis=-1, keepdims=True) + EPS) * g


def _iota(shape, dim):
    return lax.broadcasted_iota(jnp.int32, shape, dim)


def _ind(mask):
    return jnp.where(mask, 1.0, 0.0)


def _norm_matmul_kernel(x_ref, g_ref, w_ref, *rest, n_t, tiles32):
    if n_t:
        wt_ref, o32_ref, o16_ref, ot_ref, xn_ref = rest
    else:
        o32_ref, o16_ref, xn_ref = rest
    j = pl.program_id(1)

    @pl.when(j == 0)
    def _():
        xn_ref[...] = _rms(x_ref[...], g_ref[...]).astype(BF16)
        if n_t:
            ot_ref[...] = lax.dot_general(wt_ref[...], xn_ref[...], (((1,), (1,)), ((), ())),
                                          preferred_element_type=F32).astype(BF16)

    y = jnp.dot(xn_ref[...], w_ref[...], preferred_element_type=F32)

    @pl.when(j < tiles32)
    def _():
        o32_ref[...] = y

    @pl.when(j >= tiles32)
    def _():
        o16_ref[...] = y.astype(BF16)


def _norm_matmul(x, g, w, *, tm, tn, n32, w_t=None):
    t, d = x.shape
    n = w.shape[1]
    n_t = 0 if w_t is None else w_t.shape[0]
    tiles32 = n32 // tn
    assert tiles32 * tn == n32 and (n - n32) % tn == 0 and 0 < n32 < n
    in_specs = [pl.BlockSpec((tm, d), lambda i, j: (i, 0)),
                pl.BlockSpec((1, d), lambda i, j: (0, 0)),
                pl.BlockSpec((d, tn), lambda i, j: (0, j))]
    out_specs = [pl.BlockSpec((tm, tn), lambda i, j: (i, jnp.minimum(j, tiles32 - 1))),
                 pl.BlockSpec((tm, tn), lambda i, j: (i, jnp.maximum(j - tiles32, 0)))]
    out_shape = [jax.ShapeDtypeStruct((t, n32), F32), jax.ShapeDtypeStruct((t, n - n32), BF16)]
    args = [x, g.reshape(1, d), w]
    if n_t:
        in_specs.append(pl.BlockSpec((n_t, d), lambda i, j: (0, 0)))
        out_specs.append(pl.BlockSpec((n_t, tm), lambda i, j: (0, i)))
        out_shape.append(jax.ShapeDtypeStruct((n_t, t), BF16))
        args.append(w_t)
    return pl.pallas_call(
        functools.partial(_norm_matmul_kernel, n_t=n_t, tiles32=tiles32),
        grid=(t // tm, n // tn),
        in_specs=in_specs,
        out_specs=out_specs,
        out_shape=out_shape,
        scratch_shapes=[pltpu.VMEM((tm, d), BF16)],
        compiler_params=pltpu.CompilerParams(
            dimension_semantics=("parallel", "arbitrary"), vmem_limit_bytes=VMEM_LIMIT),
        name="norm_matmul",
    )(*args)


def _mix_ffn_kernel(ca_ref, cb_ref, wa_ref, wb_ref, h_ref, gmix_ref, gpre_ref, gpost_ref,
                    wg_ref, wu_ref, wd_ref, o_ref, h1_ref, xn_ref, acc_ref):
    f = pl.program_id(1)

    @pl.when(f == 0)
    def _():
        y = (jnp.dot(ca_ref[...].astype(BF16), wa_ref[...], preferred_element_type=F32)
             + jnp.dot(cb_ref[...].astype(BF16), wb_ref[...], preferred_element_type=F32))
        h1 = h_ref[...] + _rms(y, gmix_ref[...])
        h1_ref[...] = h1
        xn_ref[...] = _rms(h1, gpre_ref[...]).astype(BF16)
        acc_ref[...] = jnp.zeros_like(acc_ref)

    xn = xn_ref[...]
    gate = jnp.dot(xn, wg_ref[...], preferred_element_type=F32)
    up = jnp.dot(xn, wu_ref[...], preferred_element_type=F32)
    act = (_silu(gate) * up).astype(BF16)
    acc_ref[...] += jnp.dot(act, wd_ref[...], preferred_element_type=F32)

    @pl.when(f == pl.num_programs(1) - 1)
    def _():
        o_ref[...] = h1_ref[...] + _rms(acc_ref[...], gpost_ref[...])


def _mix_ffn(ca, cb, w_out, h, g_mix, g_pre, g_post, wg, wu, wd, *, tm, tf):
    t, d = h.shape
    ff = wg.shape[1]
    wa_n = ca.shape[1]
    wb_n = cb.shape[1]
    row = pl.BlockSpec((1, d), lambda i, f: (0, 0))
    return pl.pallas_call(
        _mix_ffn_kernel,
        grid=(t // tm, ff // tf),
        in_specs=[pl.BlockSpec((tm, wa_n), lambda i, f: (i, 0)),
                  pl.BlockSpec((tm, wb_n), lambda i, f: (i, 0)),
                  pl.BlockSpec((wa_n, d), lambda i, f: (0, 0)),
                  pl.BlockSpec((wb_n, d), lambda i, f: (0, 0)),
                  pl.BlockSpec((tm, d), lambda i, f: (i, 0)),
                  row, row, row,
                  pl.BlockSpec((d, tf), lambda i, f: (0, f)),
                  pl.BlockSpec((d, tf), lambda i, f: (0, f)),
                  pl.BlockSpec((tf, d), lambda i, f: (f, 0))],
        out_specs=pl.BlockSpec((tm, d), lambda i, f: (i, 0)),
        out_shape=jax.ShapeDtypeStruct((t, d), F32),
        scratch_shapes=[pltpu.VMEM((tm, d), F32), pltpu.VMEM((tm, d), BF16), pltpu.VMEM((tm, d), F32)],
        compiler_params=pltpu.CompilerParams(
            dimension_semantics=("parallel", "arbitrary"), vmem_limit_bytes=VMEM_LIMIT),
        name="mix_ffn",
    )(ca, cb, w_out[:wa_n].astype(BF16), w_out[wa_n:].astype(BF16), h,
      g_mix.reshape(1, d), g_pre.reshape(1, d), g_post.reshape(1, d),
      wg.astype(BF16), wu.astype(BF16), wd.astype(BF16))


def _deltanet_kernel(xq_ref, xk_ref, xv_ref, z_ref, sm_ref, cwq_ref, cwk_ref, cwv_ref,
                     alog_ref, dtb_ref, gn_ref, o_ref,
                     xpad_ref, q_ref, k_ref, v_ref, gb_ref, bb_ref, u_ref, w_ref, qk_ref, st_ref,
                     *, ts, a_col, b_col):
    s = pl.program_id(1)
    c = CHUNK
    d = HEAD_DIM
    nh = N_HEADS

    @pl.when(s == 0)
    def _():
        xpad_ref[:, 0:8, :] = jnp.zeros((3, 8, nh * d), F32)
        st_ref[...] = jnp.zeros_like(st_ref)

    @pl.when(s != 0)
    def _():
        xpad_ref[:, 0:8, :] = xpad_ref[:, ts:ts + 8, :]

    xpad_ref[0, 8:ts + 8, :] = xq_ref[...]
    xpad_ref[1, 8:ts + 8, :] = xk_ref[...]
    xpad_ref[2, 8:ts + 8, :] = xv_ref[...]

    def conv_silu(idx, cw_ref, hs):
        cw = cw_ref[:, hs]
        acc = xpad_ref[idx, 8 - (CONV_WIDTH - 1):8 - (CONV_WIDTH - 1) + ts, hs] * cw[0:1, :]
        for j in range(1, CONV_WIDTH):
            off = 8 - (CONV_WIDTH - 1) + j
            acc = acc + xpad_ref[idx, off:off + ts, hs] * cw[j:j + 1, :]
        return _silu(acc)

    def l2norm(t):
        return t * lax.rsqrt(jnp.sum(t * t, axis=-1, keepdims=True) + EPS)

    row = _iota((c, c), 0)
    col = _iota((c, c), 1)
    tri = (col <= row)
    strict = (col < row)
    tri_f = tri.astype(F32)
    upper_f = (row <= col).astype(F32)
    eye = (row == col).astype(F32)
    gnorm = gn_ref[...]
    chunks = range(ts // c)
    rs = [slice(ci * c, (ci + 1) * c) for ci in chunks]
    tri2 = jnp.concatenate([tri_f, tri_f], axis=1).astype(BF16)
    ones2 = jnp.ones((c, 2 * c), BF16)

    def cum2(lhs2, x):
        hi, lo = _split(x)
        return jnp.dot(lhs2, jnp.concatenate([hi, lo], axis=0), preferred_element_type=F32)

    for hh in range(nh):
        hs = slice(hh * d, (hh + 1) * d)
        q_ref[:, hs] = l2norm(conv_silu(0, cwq_ref, hs)) * (d ** -0.5)
        k_ref[:, hs] = l2norm(conv_silu(1, cwk_ref, hs))
        v_ref[:, hs] = conv_silu(2, cwv_ref, hs)

        a_raw = sm_ref[:, a_col + hh:a_col + hh + 1]
        b_raw = sm_ref[:, b_col + hh:b_col + hh + 1]
        g = -jnp.exp(alog_ref[:, hh:hh + 1]) * _softplus(a_raw + dtb_ref[:, hh:hh + 1])
        gb_ref[:, hs] = jnp.broadcast_to(g, (ts, d))
        bb_ref[:, hs] = jnp.broadcast_to(_sigmoid(b_raw), (ts, d))

        q = [q_ref[r, hs] for r in rs]
        k = [k_ref[r, hs] for r in rs]
        beta = [bb_ref[r, hs] for r in rs]
        gb = [gb_ref[r, hs] for r in rs]
        gc = [cum2(tri2, x) for x in gb]
        gc_row = [cum2(ones2, x[:, :c] * upper_f) for x in gb]
        decay = [jnp.where(tri, jnp.exp(jnp.minimum(a[:, :c] - b, 0.0)), 0.0) for a, b in zip(gc, gc_row)]
        kk = [_mm_nt(x, x) for x in k]
        n = [-jnp.where(strict, b[:, :c] * x * dc, 0.0) for b, x, dc in zip(beta, kk, decay)]
        inv = [eye + x for x in n]
        for step in range(5):
            nb = [x.astype(BF16) for x in n]
            n = [jnp.dot(x, x, preferred_element_type=F32) for x in nb]
            inv = [iv + _mm(iv, x) for iv, x in zip(inv, n)]
        egc = [jnp.exp(x) for x in gc]
        gl = [x[c - 1:c, :] for x in gc]
        inv_l = [x.astype(BF16) for x in inv]
        u = [_mm(a, v_ref[r, hs] * b) for a, r, b in zip(inv_l, rs, beta)]
        w = [_mm(a, x * (b * e)) for a, x, b, e in zip(inv_l, k, beta, egc)]
        qk = [_mm_nt(a, b) * dc for a, b, dc in zip(q, k, decay)]
        for ci in chunks:
            r = rs[ci]
            u_ref[r, hs] = u[ci]
            w_ref[r, hs] = w[ci]
            qk_ref[hh, r, :] = qk[ci]
            q_ref[r, hs] = q[ci] * egc[ci]
            k_ref[r, hs] = k[ci] * jnp.exp(gl[ci] - gc[ci])
            gb_ref[r, hs] = jnp.broadcast_to(jnp.exp(gl[ci]), (c, d))

    def chunk_body(ci, carry):
        r0 = pl.multiple_of(ci * c, c)
        rows = pl.ds(r0, c)
        hss = [slice(hh * d, (hh + 1) * d) for hh in range(nh)]
        st = [st_ref[hh] for hh in range(nh)]
        w_st = [_mm(w_ref[rows, hs], s_) for hs, s_ in zip(hss, st)]
        q_st = [_mm(q_ref[rows, hs], s_) for hs, s_ in zip(hss, st)]
        v_new = [u_ref[rows, hs] - x for hs, x in zip(hss, w_st)]
        o = [a + _mm(qk_ref[hh, rows, :], v) for hh, (a, v) in enumerate(zip(q_st, v_new))]
        kv = [_mm_tn(k_ref[rows, hs], v) for hs, v in zip(hss, v_new)]
        for hh, hs in enumerate(hss):
            st_ref[hh] = st[hh] * gb_ref[pl.ds(r0, 1), hs] + kv[hh]
            o_ref[rows, hs] = _rms(o[hh], gnorm) * _silu(z_ref[rows, hs])
        return carry

    lax.fori_loop(0, ts // c, chunk_body, 0)


def _deltanet(p32, conv_w, a_log, dt_bias, a_norm_g, *, ts, cols):
    bsz, s, _ = p32.shape
    d = HEAD_DIM
    nh = N_HEADS
    w = nh * d
    pad = lambda t: jnp.pad(t.astype(F32), (0, d - t.shape[0])).reshape(1, d)
    kernel = functools.partial(_deltanet_kernel, ts=ts, a_col=cols["a_lane"], b_col=cols["b_lane"])
    tile = lambda name: pl.BlockSpec((None, ts, w), lambda b, i: (b, i, cols[name] // nh))
    conv = lambda k: pl.BlockSpec((CONV_WIDTH, w), lambda b, i: (0, k))
    row = pl.BlockSpec((1, d), lambda b, i: (0, 0))
    return pl.pallas_call(
        kernel,
        grid=(bsz, s // ts),
        in_specs=[tile("qa"), tile("ka"), tile("va"), tile("za"),
                  pl.BlockSpec((None, ts, d), lambda b, i: (b, i, cols["small"])),
                  conv(0), conv(1), conv(2), row, row, row],
        out_specs=pl.BlockSpec((None, ts, w), lambda b, i: (b, i, 0)),
        out_shape=jax.ShapeDtypeStruct((bsz, s, w), F32),
        scratch_shapes=[pltpu.VMEM((3, ts + 8, w), F32)]
        + [pltpu.VMEM((ts, w), F32) for _ in range(7)]
        + [pltpu.VMEM((nh, ts, CHUNK), F32), pltpu.VMEM((nh, d, d), F32)],
        compiler_params=pltpu.CompilerParams(
            dimension_semantics=("parallel", "arbitrary"), vmem_limit_bytes=VMEM_LIMIT),
        name="deltanet",
    )(p32, p32, p32, p32, p32, conv_w.astype(F32), conv_w.astype(F32), conv_w.astype(F32),
      pad(a_log), pad(dt_bias), a_norm_g.astype(F32).reshape(1, d))


def _hgrn2_kernel(q_ref, f_ref, i_ref, gate_ref, lb_ref, gn_ref, o_ref,
                  qs_ref, ks_ref, gc_ref, st_ref, *, ts):
    s = pl.program_id(1)
    c = CHUNK
    d = HEAD_DIM
    nh = N_HEADS
    SUB = 16

    @pl.when(s == 0)
    def _():
        st_ref[...] = jnp.zeros_like(st_ref)

    lb = lb_ref[...]
    f_raw = f_ref[...]
    log_sig = jnp.minimum(f_raw, 0.0) - jnp.log1p(jnp.exp(-jnp.abs(f_raw)))
    la = jnp.log(lb)
    lbb = jnp.log1p(-lb) + log_sig
    log_f = jnp.maximum(la, lbb) + jnp.log1p(jnp.exp(-jnp.abs(la - lbb)))
    qs_ref[...] = _silu(q_ref[...])
    ks_ref[...] = (1.0 - lb) * _sigmoid(-f_raw)

    row = _iota((c, c), 0)
    col = _iota((c, c), 1)
    tri_f = (col <= row).astype(F32)
    ones_dd = jnp.ones((d, d), BF16)
    rows_8d = _iota((8, d), 0)
    gnorm = gn_ref[...]

    tri2 = jnp.concatenate([tri_f, tri_f], axis=1).astype(BF16)
    for ci in range(ts // c):
        hi, lo = _split(log_f[ci * c:(ci + 1) * c, :])
        gc_ref[ci * c:(ci + 1) * c, :] = jnp.dot(tri2, jnp.concatenate([hi, lo], axis=0),
                                                 preferred_element_type=F32)

    blocks = [(sb * SUB, (sb + 1) * SUB) for sb in range(c // SUB)]

    def chunk_loop(ci, carry):
        r0 = pl.multiple_of(ci * c, c)
        rows = pl.ds(r0, c)
        hss = [slice(hh * d, (hh + 1) * d) for hh in range(nh)]
        q = [qs_ref[rows, hs] for hs in hss]
        k = [ks_ref[rows, hs] for hs in hss]
        v = [i_ref[rows, hs] for hs in hss]
        gc = [gc_ref[rows, hs] for hs in hss]

        def near_products(q, k, gc):
            prods = []
            for top, end in blocks:
                for j in range(top, end):
                    lo = (j // 8) * 8
                    e = jnp.exp2(gc[lo:end, :] - gc[j:j + 1, :])
                    if j % 8:
                        head = jnp.where(rows_8d >= j - lo, e[:8], 0.0)
                        e = jnp.concatenate([head, e[8:]], axis=0) if lo + 8 < end else head
                    prods.append(q[lo:end, :] * k[j:j + 1, :] * e)
            return jnp.concatenate(prods, axis=0).astype(BF16)

        def far_operands(q, k, gc):
            out = []
            for top, end in blocks[1:]:
                g_b = gc[top - 1:top, :]
                out.append((q[top:end, :] * jnp.exp(gc[top:end, :] - g_b),
                            k[:top, :] * jnp.exp(jnp.minimum(g_b - gc[:top, :], 0.0))))
            return out

        near = [near_products(a, b, g * LOG2E) for a, b, g in zip(q, k, gc)]
        far_ops = [far_operands(*x) for x in zip(q, k, gc)]
        st = [st_ref[hh] for hh in range(nh)]
        gl = [x[c - 1:c, :] for x in gc]
        sums = [jnp.dot(x, ones_dd, preferred_element_type=F32) for x in near]
        qk_far = [[_mm_nt(qe, ke) for qe, ke in ops] for ops in far_ops]
        far = [[_mm(a, vv[:top, :]) for a, (top, _) in zip(qs, blocks[1:])] for qs, vv in zip(qk_far, v)]
        o_st = [_mm_nt(a * jnp.exp(g), s_) for a, g, s_ in zip(q, gc, st)]
        kv = [_mm_tn(vv, kk * jnp.exp(g_l - g)) for vv, kk, g_l, g in zip(v, k, gl, gc)]

        for hh, hs in enumerate(hss):
            groups = [jnp.zeros((8, d), F32) for _ in range(c // 8)]
            at = 0
            for top, end in blocks:
                for j in range(top, end):
                    v_j = v[hh][j:j + 1, :]
                    for g in range(j // 8, end // 8):
                        groups[g] = groups[g] + sums[hh][at:at + 8, :] * v_j
                        at += 8
            for f, (top, end) in zip(far[hh], blocks[1:]):
                for g in range(top // 8, end // 8):
                    groups[g] = groups[g] + f[(g * 8 - top):(g * 8 - top + 8), :]
            o = jnp.concatenate(groups, axis=0) + o_st[hh]
            st_ref[hh] = st[hh] * jnp.exp(gl[hh]) + kv[hh]
            o_ref[rows, hs] = _rms(o, gnorm) * _silu(gate_ref[rows, hs])
        return carry

    lax.fori_loop(0, ts // c, chunk_loop, 0)


def _hgrn2(p32, lb, d_norm_g, *, ts, cols):
    bsz, s, _ = p32.shape
    d = HEAD_DIM
    nh = N_HEADS
    w = nh * d
    kernel = functools.partial(_hgrn2_kernel, ts=ts)
    tile = lambda name: pl.BlockSpec((None, ts, w), lambda b, i: (b, i, cols[name] // nh))
    return pl.pallas_call(
        kernel,
        grid=(bsz, s // ts),
        in_specs=[tile("qd"), tile("fd"), tile("id"), tile("gd"),
                  pl.BlockSpec((1, w), lambda b, i: (0, 0)),
                  pl.BlockSpec((1, d), lambda b, i: (0, 0))],
        out_specs=pl.BlockSpec((None, ts, w), lambda b, i: (b, i, 0)),
        out_shape=jax.ShapeDtypeStruct((bsz, s, w), F32),
        scratch_shapes=[pltpu.VMEM((ts, w), F32), pltpu.VMEM((ts, w), F32),
                        pltpu.VMEM((ts, w), F32), pltpu.VMEM((nh, d, d), F32)],
        compiler_params=pltpu.CompilerParams(
            dimension_semantics=("parallel", "arbitrary"), vmem_limit_bytes=VMEM_LIMIT),
        name="hgrn2",
    )(p32, p32, p32, p32, lb.astype(F32).reshape(1, w), d_norm_g.astype(F32).reshape(1, d))


def _stickbreak_kernel(q_ref, k_ref, v_ref, o_ref, *, tq):
    i = pl.program_id(1)
    d = HEAD_DIM
    nh = N_HEADS
    row = _iota((tq, tq), 0)
    col = _iota((tq, tq), 1)
    causal = col < row
    later = (row > col).astype(BF16)
    later2 = jnp.concatenate([later, later], axis=0)

    heads = [slice(hh * d, (hh + 1) * d) for hh in range(nh)]

    def scores(blocks):
        jobs = [(j, dg, hs) for j, dg in blocks for hs in heads]
        z = [_mm_nt(q_ref[:, hs], k_ref[pl.ds(pl.multiple_of(j * tq, tq), tq), hs]) * (d ** -0.5)
             for j, _, hs in jobs]
        sp = [_softplus(x) for x in z]
        l1m = [jnp.where(causal, -x, 0.0) if dg else -x for x, (_, dg, _) in zip(sp, jobs)]
        rest = [jnp.dot(jnp.concatenate(_split(x), axis=1), later2, preferred_element_type=F32)
                for x in l1m]
        out = [((a - b) + r, l) for a, b, r, l in zip(z, sp, rest, l1m)]
        return [out[b * nh:(b + 1) * nh] for b in range(len(blocks))]

    def block(j, carries):
        (sc,) = scores([(j, False)])
        ps = [jnp.exp(logw + c) for (logw, _), c in zip(sc, carries)]
        pv = [_mm(p, v_ref[pl.ds(pl.multiple_of(j * tq, tq), tq), hs]) for p, hs in zip(ps, heads)]
        for hs, x in zip(heads, pv):
            o_ref[:, hs] += x
        return tuple(c + jnp.sum(l1m, axis=-1, keepdims=True) for (_, l1m), c in zip(sc, carries))

    jp = jnp.maximum(i - 1, 0)
    live = jnp.where(i > 0, 1.0, 0.0)
    sd, sp_ = scores([(i, True), (jp, False)])
    carries = []
    for hh, hs in enumerate(heads):
        c1 = jnp.sum(sd[hh][1], axis=-1, keepdims=True)
        p_d = jnp.where(causal, jnp.exp(sd[hh][0]), 0.0)
        p_p = jnp.exp(sp_[hh][0] + c1) * live
        o_ref[:, hs] = (_mm(p_d, v_ref[pl.ds(pl.multiple_of(i * tq, tq), tq), hs])
                        + _mm(p_p, v_ref[pl.ds(pl.multiple_of(jp * tq, tq), tq), hs]))
        carries.append(c1 + jnp.sum(sp_[hh][1], axis=-1, keepdims=True))
    carries = tuple(carries)

    def cond(c):
        worst = functools.reduce(jnp.maximum, c[1])
        return jnp.logical_and(c[0] >= 0, jnp.max(worst) >= EXP_ZERO_BELOW)

    def body(c):
        return c[0] - 1, block(c[0], c[1])

    lax.while_loop(cond, body, (i - 2, carries))


def _stickbreak(p16, *, tq, cols):
    bsz, s, _ = p16.shape
    nh = N_HEADS
    w = nh * HEAD_DIM
    kernel = functools.partial(_stickbreak_kernel, tq=tq)
    resident = dict(pipeline_mode=pl.Buffered(1))
    return pl.pallas_call(
        kernel,
        grid=(bsz, s // tq),
        in_specs=[pl.BlockSpec((None, tq, w), lambda b, i: (b, i, cols["qc"] // nh)),
                  pl.BlockSpec((None, s, w), lambda b, i: (b, 0, cols["kc"] // nh), **resident),
                  pl.BlockSpec((None, s, w), lambda b, i: (b, 0, cols["vc"] // nh), **resident)],
        out_specs=pl.BlockSpec((None, tq, w), lambda b, i: (b, i, 0)),
        out_shape=jax.ShapeDtypeStruct((bsz, s, w), F32),
        compiler_params=pltpu.CompilerParams(
            dimension_semantics=("parallel", "arbitrary"), vmem_limit_bytes=VMEM_LIMIT),
        name="stickbreak",
    )(p16, p16, p16)


def _dsa_kernel(qi_ref, smq_ref, q_ref, sm_ref, k_ref, vt_ref, bias_ref, o_ref,
                sc_ref, scb_ref, wb_ref, qc_ref, kct_ref, bd_ref, lg_ref, *, tq, k_sel, wi_lane, wide):
    i = pl.program_id(1)
    tk = tq
    d = HEAD_DIM
    nh = N_HEADS
    ksel = float(k_sel)
    per_wide = wide // tk
    n_wide = (i + per_wide) // per_wide
    sub = 2 * tk
    lane_q = _iota((1, tq), 1)

    def tree(parts, op):
        while len(parts) > 1:
            parts = [op(parts[j], parts[j + 1]) if j + 1 < len(parts) else parts[j]
                     for j in range(0, len(parts), 2)]
        return parts[0]

    def col_fold(x, op=jnp.add, rows=8):
        return tree([x[r * rows:(r + 1) * rows] for r in range(x.shape[0] // rows)], op)

    @pl.when(i == 0)
    def _():
        def prep(g, carry):
            g0 = pl.multiple_of(g * wide, wide)
            kt = sm_ref[pl.ds(g0, wide), :].T[:IDX_DIM, :]
            hi, lo = _split(kt)
            kct_ref[:, pl.ds(g0, wide)] = jnp.concatenate([hi, lo, hi], axis=0)
            return carry
        lax.fori_loop(0, sm_ref.shape[0] // wide, prep, 0)

    smq = smq_ref[...]
    lane = _iota(smq.shape, 1)
    for hh in range(IDX_HEADS):
        qh = qi_ref[:, hh * IDX_DIM:(hh + 1) * IDX_DIM]
        hi, lo = _split(qh)
        qc_ref[hh] = jnp.concatenate([hi, hi, lo], axis=-1)
        w = jnp.sum(jnp.where(lane == wi_lane + hh, smq, 0.0), axis=-1, keepdims=True)
        wb_ref[hh] = jnp.broadcast_to(w * ((IDX_HEADS ** -0.5) * (IDX_DIM ** -0.5)), (tq, tk))

    q2t = (q_ref[...] * ((d ** -0.5) * LOG2E)).T.astype(BF16)
    zero_dq = jnp.zeros((d, tq), BF16)
    for p in range(nh // 2):
        top = jnp.concatenate([q2t[2 * p * d:(2 * p + 1) * d], zero_dq], axis=1)
        bot = jnp.concatenate([zero_dq, q2t[(2 * p + 1) * d:(2 * p + 2) * d]], axis=1)
        bd_ref[p] = jnp.concatenate([top, bot], axis=0)

    limit = i * tq + (lane_q // CHUNK + 1) * CHUNK
    rows_t = _iota((tk, tq), 0)

    def score_group(g, mm, masked):
        mn, mx = mm
        for sb in range(wide // sub):
            k0 = pl.multiple_of(g * wide + sb * sub, sub)
            kct = kct_ref[:, pl.ds(k0, sub)]
            tiles = [jnp.zeros((tq, tk), F32) for _ in range(sub // tk)]
            for hh in range(IDX_HEADS):
                s_h = jnp.dot(qc_ref[hh], kct, preferred_element_type=F32)
                for ti in range(sub // tk):
                    tiles[ti] = tiles[ti] + jnp.maximum(s_h[:, ti * tk:(ti + 1) * tk], 0.0) * wb_ref[hh]
            for ti in range(sub // tk):
                kb = pl.multiple_of(k0 + ti * tk, tk)
                sct = tiles[ti].T
                if masked:
                    adm = (kb + rows_t) < limit
                    mn = jnp.minimum(mn, col_fold(jnp.where(adm, sct, jnp.inf), jnp.minimum))
                    sct = jnp.where(adm, sct, -jnp.inf)
                else:
                    mn = jnp.minimum(mn, col_fold(sct, jnp.minimum))
                mx = jnp.maximum(mx, col_fold(sct, jnp.maximum))
                sc_ref[pl.ds(kb, tk), :] = sct
                scb_ref[pl.ds(kb, tk), :] = _floor_bf16(sct)
        return mn, mx

    def score_pair(j, mm):
        return score_group(2 * j + 1, score_group(2 * j, mm, False), False)

    n_full = n_wide - 1
    mm = lax.fori_loop(0, n_full // 2, score_pair,
                       (jnp.full((8, tq), jnp.inf, F32), jnp.full((8, tq), -jnp.inf, F32)))
    mm = lax.cond(n_full % 2 == 1, lambda c: score_group(n_full - 1, c, False), lambda c: c, mm)
    mn, mx = score_group(n_wide - 1, mm, True)

    n_pairs = (n_wide + 1) // 2

    @pl.when(n_wide % 2 == 1)
    def _():
        sc_ref[pl.ds(pl.multiple_of(n_wide * wide, wide), wide), :] = jnp.full((wide, tq), -jnp.inf, F32)
        scb_ref[pl.ds(pl.multiple_of(n_wide * wide, wide), wide), :] = jnp.full((wide, tq), -jnp.inf, BF16)
    rmin = jnp.min(mn, axis=0, keepdims=True)
    rmax = jnp.max(mx, axis=0, keepdims=True)

    def count(pred):
        def body(j, acc):
            for g in (2 * j, 2 * j + 1):
                acc = acc + col_fold(pred(sc_ref[pl.ds(pl.multiple_of(g * wide, wide), wide), :]))
            return acc
        return jnp.sum(lax.fori_loop(0, n_pairs, body, jnp.zeros((8, tq), F32)), axis=0, keepdims=True)

    def max_below(x):
        def body(j, acc):
            for g in (2 * j, 2 * j + 1):
                blk = sc_ref[pl.ds(pl.multiple_of(g * wide, wide), wide), :]
                acc = jnp.maximum(acc, col_fold(jnp.where(blk < x, blk, -jnp.inf), jnp.maximum))
            return acc
        return jnp.max(lax.fori_loop(0, n_pairs, body, jnp.full((8, tq), -jnp.inf, F32)), axis=0, keepdims=True)

    n_adm = limit.astype(F32)
    all_sel = n_adm <= ksel

    def bisect(c):
        lo, hi, c_lo = c
        mid = 0.5 * lo + 0.5 * hi
        cm = count(lambda blk: _ind(blk >= mid))
        ge = cm >= ksel
        return jnp.where(ge, mid, lo), jnp.where(ge, hi, mid), jnp.where(ge, cm, c_lo)

    def pending(c_lo, tied):
        return jnp.where(all_sel, 0.0, jnp.where(tied > 0.5, 0.0, _ind(c_lo != ksel)))

    def bisect_coarse(_, c):
        lo, hi, c_lo = c
        mid = _floor_bf16(0.5 * lo + 0.5 * hi).astype(F32)
        t_b = jnp.broadcast_to(mid, (16, tq)).astype(BF16)
        one_b = jnp.ones((16, tq), BF16)
        zero_b = jnp.zeros((16, tq), BF16)

        def body(j, acc):
            for g in (2 * j, 2 * j + 1):
                blk = scb_ref[pl.ds(pl.multiple_of(g * wide, wide), wide), :]
                ind = [jnp.where(blk[r * 16:(r + 1) * 16] >= t_b, one_b, zero_b) for r in range(wide // 16)]
                acc = acc + tree(ind, jnp.add).astype(F32)
            return acc

        acc = lax.fori_loop(0, n_pairs, body, jnp.zeros((16, tq), F32))
        cm = jnp.sum(acc, axis=0, keepdims=True)
        ge = cm >= ksel
        return jnp.where(ge, mid, lo), jnp.where(ge, hi, mid), jnp.where(ge, cm, c_lo)

    lo0 = _floor_bf16(rmin).astype(F32)
    hi0 = _floor_bf16(rmax + (jnp.abs(rmax) * (2.0 ** -6) + 1e-30)).astype(F32)
    state = lax.fori_loop(0, BISECT_COARSE, bisect_coarse, (lo0, hi0, n_adm))
    state = lax.fori_loop(0, BISECT_FIXED, lambda _, c: bisect(c), state)

    def round_cond(c):
        return jnp.max(pending(c[0][2], c[1])) > 0.5

    def round_body(c):
        st, tied, v, need = c

        def more_cond(s):
            return jnp.logical_and(s[0] < BISECT_EXTRA, jnp.max(pending(s[1][2], tied)) > 0.5)

        _, st = lax.while_loop(more_cond, lambda s: (s[0] + 1, bisect(s[1])), (jnp.int32(0), st))
        pend = pending(st[2], tied)

        def check(_):
            cand = max_below(st[1])
            c_ge = count(lambda blk: _ind(blk >= cand))
            c_gt = count(lambda blk: _ind(blk > cand))
            ok = jnp.where(pend > 0.5, _ind(c_ge >= ksel), 0.0)
            return (jnp.where(ok > 0.5, 1.0, tied), jnp.where(ok > 0.5, cand, v),
                    jnp.where(ok > 0.5, ksel - c_gt, need))

        tied, v, need = lax.cond(jnp.max(pend) > 0.5, check, lambda _: (tied, v, need), 0)
        return st, tied, v, need

    zeros1 = jnp.zeros((1, tq), F32)
    (lo_f, _, _), tied, v_tie, need = lax.while_loop(round_cond, round_body, (state, zeros1, zeros1, zeros1))
    vth = jnp.where(all_sel, F32_LOWEST, jnp.where(tied > 0.5, v_tie, lo_f))

    @pl.when(jnp.max(tied) > 0.5)
    def _():
        v_eq = jnp.where(tied > 0.5, v_tie, jnp.inf)
        incl = (_iota((tk, tk), 1) <= _iota((tk, tk), 0)).astype(BF16)

        def demote(g, seen):
            g0 = pl.multiple_of(g * wide, wide)
            xs = [sc_ref[pl.ds(g0 + pb * tk, tk), :] for pb in range(per_wide)]
            eqs = [_ind(x == v_eq) for x in xs]
            inblk = [jnp.dot(incl, e.astype(BF16), preferred_element_type=F32) for e in eqs]
            for pb in range(per_wide):
                rank = inblk[pb] + seen
                sc_ref[pl.ds(g0 + pb * tk, tk), :] = jnp.where(eqs[pb] * _ind(rank > need) > 0.5,
                                                               -jnp.inf, xs[pb])
                seen = seen + jnp.sum(col_fold(eqs[pb]), axis=0, keepdims=True)
            return seen

        lax.fori_loop(0, n_wide, demote, zeros1)

    g_near = jnp.maximum(i - 1, 0) // per_wide

    def logit_group(g, mx, near):
        out = list(mx)
        for sb in range(wide // sub):
            k0 = pl.multiple_of(g * wide + sb * sub, sub)
            sel = sc_ref[pl.ds(k0, sub), :] >= vth
            for p in range(nh // 2):
                pair = jnp.dot(k_ref[pl.ds(k0, sub), 2 * p * d:(2 * p + 2) * d], bd_ref[p],
                               preferred_element_type=F32)
                for hh in (2 * p, 2 * p + 1):
                    lm = pair[:, (hh - 2 * p) * tq:(hh - 2 * p + 1) * tq]
                    if near:
                        back = [jnp.clip(i - (g * per_wide + sb * (sub // tk) + pb), 0, 2)
                                for pb in range(sub // tk)]
                        lm = lm + jnp.concatenate([bias_ref[bk, hh] for bk in back], axis=0)
                    lm = jnp.where(sel, lm, NEG_BIG)
                    lg_ref[hh, pl.ds(k0, sub), :] = lm
                    out[hh] = jnp.maximum(out[hh], col_fold(lm, jnp.maximum))
        return tuple(out)

    mx = tuple(jnp.full((8, tq), NEG_BIG, F32) for _ in range(nh))
    def logit_pair(j, mx, near):
        return logit_group(2 * j + 1, logit_group(2 * j, mx, near), near)

    far_pairs = g_near // 2
    mx = lax.fori_loop(0, far_pairs, functools.partial(logit_pair, near=False), mx)
    mx = lax.fori_loop(far_pairs, n_pairs, functools.partial(logit_pair, near=True), mx)
    m_q = [jnp.max(mx[hh], axis=0, keepdims=True) for hh in range(nh)]

    ones_rows = jnp.ones((8, wide), BF16)

    def pv_pair(j, carry):
        ls, accs = list(carry[0]), list(carry[1])
        jobs = [(pl.multiple_of(g * wide, wide), hh) for g in (2 * j, 2 * j + 1) for hh in range(nh)]
        ps = [jnp.exp2(lg_ref[hh, pl.ds(g0, wide), :] - m_q[hh]).astype(BF16) for g0, hh in jobs]
        outs = [jnp.dot(jnp.concatenate([vt_ref[hh * d:(hh + 1) * d, pl.ds(g0, wide)], ones_rows], axis=0),
                        p, preferred_element_type=F32) for (g0, hh), p in zip(jobs, ps)]
        for (_, hh), out in zip(jobs, outs):
            ls[hh] = ls[hh] + out[d:]
            accs[hh] = accs[hh] + out[:d]
        return tuple(ls), tuple(accs)

    ls, accs = lax.fori_loop(0, n_pairs, pv_pair,
                             (tuple(jnp.zeros((8, tq), F32) for _ in range(nh)),
                              tuple(jnp.zeros((d, tq), F32) for _ in range(nh))))
    for hh in range(nh):
        o_ref[:, hh * d:(hh + 1) * d] = (accs[hh] / ls[hh][0:1]).T


def _dsa(p32, p16, vt, bias_tiles, *, tq, cols):
    bsz, s, _ = p32.shape
    d = HEAD_DIM
    nh = N_HEADS
    wide = 4 * tq
    k_sel = min(TOPK_MAX, s // 4)
    w512 = nh * d
    kernel = functools.partial(_dsa_kernel, tq=tq, k_sel=k_sel, wi_lane=cols["wi_lane"], wide=wide)
    resident = dict(pipeline_mode=pl.Buffered(1))
    return pl.pallas_call(
        kernel,
        grid=(bsz, s // tq),
        in_specs=[pl.BlockSpec((None, tq, w512), lambda b, i: (b, i, cols["qi"] // nh)),
                  pl.BlockSpec((None, tq, d), lambda b, i: (b, i, cols["small"])),
                  pl.BlockSpec((None, tq, w512), lambda b, i: (b, i, cols["qb"] // nh)),
                  pl.BlockSpec((None, s, d), lambda b, i: (b, 0, cols["small"]), **resident),
                  pl.BlockSpec((None, s, w512), lambda b, i: (b, 0, cols["kb"] // nh), **resident),
                  pl.BlockSpec((w512, s), lambda b, i: (0, b), **resident),
                  pl.BlockSpec((3, nh, tq, tq), lambda b, i: (0, 0, 0, 0), **resident)],
        out_specs=pl.BlockSpec((None, tq, w512), lambda b, i: (b, i, 0)),
        out_shape=jax.ShapeDtypeStruct((bsz, s, w512), F32),
        scratch_shapes=[pltpu.VMEM((s, tq), F32),
                        pltpu.VMEM((s, tq), BF16),
                        pltpu.VMEM((IDX_HEADS, tq, tq), F32),
                        pltpu.VMEM((IDX_HEADS, tq, 3 * IDX_DIM), BF16),
                        pltpu.VMEM((3 * IDX_DIM, s), BF16),
                        pltpu.VMEM((nh // 2, 2 * d, 2 * tq), BF16),
                        pltpu.VMEM((nh, s, tq), F32)],
        compiler_params=pltpu.CompilerParams(
            dimension_semantics=("parallel", "arbitrary"), vmem_limit_bytes=VMEM_LIMIT),
        name="dsa",
    )(p32, p32, p32, p32, p16, vt, bias_tiles)


def _t5_bucket(rel):
    nb = REL_BUCKETS // 2
    max_exact = nb // 2
    ret = jnp.where(rel > 0, nb, 0)
    n = jnp.abs(rel)
    large = max_exact + (jnp.log(jnp.maximum(n, 1).astype(F32) / max_exact)
                         / math.log(REL_MAX_DIST / max_exact) * (nb - max_exact)).astype(jnp.int32)
    large = jnp.minimum(large, nb - 1)
    return ret + jnp.where(n < max_exact, n, large)


def _bias_tiles(rel_table, tq):
    assert tq >= REL_MAX_DIST
    t = jnp.arange(tq)
    back = jnp.arange(3)
    rel = (t[None, None, :] - back[:, None, None] * tq) - t[None, :, None]
    onehot = (_t5_bucket(rel)[..., None] == jnp.arange(REL_BUCKETS)).astype(F32)
    tiles = jnp.einsum("bqkn,nh->bhkq", onehot, rel_table.astype(F32),
                       precision=HIGHEST)
    return (tiles - tiles[2:3]) * LOG2E


def _even_layout(w_in):
    d = HEAD_DIM
    a_w = 2 * N_HEADS * d + N_HEADS * d
    offs = {}
    o = 0
    for name, w in (("qkv", a_w), ("z", N_HEADS * d), ("a", N_HEADS), ("b", N_HEADS),
                    ("qb", N_HEADS * d), ("kb", N_HEADS * d), ("vb", N_HEADS * d),
                    ("qi", IDX_HEADS * IDX_DIM), ("ki", IDX_DIM), ("wi", IDX_HEADS)):
        offs[name] = (o, o + w)
        o += w
    assert o == w_in.shape[1]
    sl = lambda n: w_in[:, offs[n][0]:offs[n][1]]
    small_w = IDX_DIM + 2 * N_HEADS + IDX_HEADS
    small_pad = -small_w % d
    zeros = lambda n: jnp.zeros((w_in.shape[0], n), w_in.dtype)
    w32 = jnp.concatenate([sl("qkv"), sl("z"), sl("qb"), sl("qi"),
                           sl("ki"), sl("a"), sl("b"), sl("wi"), zeros(small_pad)], axis=1)
    n32 = w32.shape[1]
    tn = n32 // 5
    assert tn * 5 == n32 and tn % d == 0
    w16 = jnp.concatenate([sl("kb"), zeros(tn - N_HEADS * d)], axis=1)
    nh = N_HEADS
    cols = dict(qa=0, ka=nh, va=2 * nh, za=3 * nh, qb=4 * nh, qi=5 * nh, small=6 * nh, kb=0,
                a_lane=IDX_DIM, b_lane=IDX_DIM + nh, wi_lane=IDX_DIM + 2 * nh, n32=n32, tn=tn)
    return jnp.concatenate([w32, w16], axis=1).astype(BF16), sl("vb").T.astype(BF16), cols


def kernel(x, norm_g, w_in_even, conv_w_even, a_log_even, dt_bias_even, a_norm_even, w_out_even,
           rel_bias, w_in_odd, lb_logits, d_norm_odd, w_out_odd, w_gate, w_up, w_down):
    bsz, s, d = x.shape
    t = bsz * s
    depth = norm_g.shape[0]
    nh = N_HEADS
    tq = Q_TILE
    lb_all = jnp.cumsum(jax.nn.softmax(lb_logits.astype(F32), axis=0), axis=0)
    lb_all = lb_all - lb_all[:1]
    odd_cols = dict(qc=0, kc=nh, vc=2 * nh, qd=0, fd=nh, id=2 * nh, gd=3 * nh)
    bias_tiles = _bias_tiles(rel_bias, tq)

    h = x.reshape(t, d)
    for l in range(depth):
        if l % 2 == 0:
            e = l // 2
            w_even, w_vt, cols = _even_layout(w_in_even[e])
            p32, p16, vt = _norm_matmul(h, norm_g[l, 0], w_even, tm=ROW_TILE, tn=cols["tn"], n32=cols["n32"],
                                        w_t=w_vt)
            p32 = p32.reshape(bsz, s, -1)
            p16 = p16.reshape(bsz, s, -1)
            o_1 = _deltanet(p32, conv_w_even[e], a_log_even[e], dt_bias_even[e], a_norm_even[e],
                            ts=min(SEQ_TILE, s), cols=cols)
            o_2 = _dsa(p32, p16, vt, bias_tiles, tq=tq, cols=cols)
            w_out = w_out_even[e]
        else:
            o = l // 2
            n16 = 3 * nh * HEAD_DIM
            w_odd = jnp.concatenate([w_in_odd[o][:, n16:], w_in_odd[o][:, :n16]], axis=1).astype(BF16)
            p32, p16 = _norm_matmul(h, norm_g[l, 0], w_odd, tm=ROW_TILE, tn=ODD_COL_TILE, n32=w_odd.shape[1] - n16)
            p32 = p32.reshape(bsz, s, -1)
            p16 = p16.reshape(bsz, s, -1)
            o_1 = _stickbreak(p16, tq=tq, cols=odd_cols)
            o_2 = _hgrn2(p32, lb_all[l], d_norm_odd[o], ts=min(SEQ_TILE, s), cols=odd_cols)
            w_out = w_out_odd[o]
        h = _mix_ffn(o_1.reshape(t, -1), o_2.reshape(t, -1), w_out, h, norm_g[l, 1], norm_g[l, 2], norm_g[l, 3],
                     w_gate[l], w_up[l], w_down[l], tm=ROW_TILE, tf=FFN_TILE)
    return h.reshape(bsz, s, d)
```

```python
import functools
import math

import jax
import jax.numpy as jnp
from jax import lax
from jax.experimental import pallas as pl
from jax.experimental.pallas import tpu as pltpu

F32 = jnp.float32
BF16 = jnp.bfloat16
HIGHEST = lax.Precision.HIGHEST

CHUNK = 64
HEAD_DIM = 128
N_HEADS = 4
IDX_HEADS = 8
IDX_DIM = 64
TOPK_MAX = 256
CONV_WIDTH = 4
REL_BUCKETS = 32
REL_MAX_DIST = 128
EPS = 1e-6
NEG_BIG = -1e30
LOG2E = 1.4426950408889634
BISECT_COARSE = 12
BISECT_FIXED = 8
BISECT_EXTRA = 6
F32_LOWEST = -3.4028234663852886e38
EXP_ZERO_BELOW = -104.0
VMEM_LIMIT = 56 * 1024 * 1024

ROW_TILE = 1024
SEQ_TILE = 512
Q_TILE = 128
ODD_COL_TILE = 512
FFN_TILE = 256


def _mm(a, b):
    return jnp.dot(a.astype(BF16), b.astype(BF16), preferred_element_type=F32)


def _mm_nt(a, b):
    return lax.dot_general(a.astype(BF16), b.astype(BF16), (((1,), (1,)), ((), ())),
                           preferred_element_type=F32)


def _mm_tn(a, b):
    return lax.dot_general(a.astype(BF16), b.astype(BF16), (((0,), (0,)), ((), ())),
                           preferred_element_type=F32)


def _split(x):
    hi = x.astype(BF16)
    return hi, (x - hi.astype(F32)).astype(BF16)


def _floor_bf16(x):
    bits = pltpu.bitcast(x, jnp.int32)
    down = jnp.where(bits >= 0, bits, bits + 0xFFFF) & jnp.int32(-65536)
    return pltpu.bitcast(down, F32).astype(BF16)


def _sigmoid(x):
    return 1.0 / (1.0 + jnp.exp(-x))


def _silu(x):
    return x * _sigmoid(x)


def _softplus(x):
    return jnp.maximum(x, 0.0) + jnp.log1p(jnp.exp(-jnp.abs(x)))


def _rms(x, g):
    return x * lax.rsqrt(jnp.mean(x * x, axis=-1, keepdims=True) + EPS) * g


def _iota(shape, dim):
    return lax.broadcasted_iota(jnp.int32, shape, dim)


def _ind(mask):
    return jnp.where(mask, 1.0, 0.0)


def _norm_matmul_kernel(x_ref, g_ref, w_ref, *rest, n_t, tiles32):
    if n_t:
        wt_ref, o32_ref, o16_ref, ot_ref, xn_ref = rest
    else:
        o32_ref, o16_ref, xn_ref = rest
    j = pl.program_id(1)

    @pl.when(j == 0)
    def _():
        xn_ref[...] = _rms(x_ref[...], g_ref[...]).astype(BF16)
        if n_t:
            ot_ref[...] = lax.dot_general(wt_ref[...], xn_ref[...], (((1,), (1,)), ((), ())),
                                          preferred_element_type=F32).astype(BF16)

    y = jnp.dot(xn_ref[...], w_ref[...], preferred_element_type=F32)

    @pl.when(j < tiles32)
    def _():
        o32_ref[...] = y

    @pl.when(j >= tiles32)
    def _():
        o16_ref[...] = y.astype(BF16)


def _norm_matmul(x, g, w, *, tm, tn, n32, w_t=None):
    t, d = x.shape
    n = w.shape[1]
    n_t = 0 if w_t is None else w_t.shape[0]
    tiles32 = n32 // tn
    assert tiles32 * tn == n32 and (n - n32) % tn == 0 and 0 < n32 < n
    in_specs = [pl.BlockSpec((tm, d), lambda i, j: (i, 0)),
                pl.BlockSpec((1, d), lambda i, j: (0, 0)),
                pl.BlockSpec((d, tn), lambda i, j: (0, j))]
    out_specs = [pl.BlockSpec((tm, tn), lambda i, j: (i, jnp.minimum(j, tiles32 - 1))),
                 pl.BlockSpec((tm, tn), lambda i, j: (i, jnp.maximum(j - tiles32, 0)))]
    out_shape = [jax.ShapeDtypeStruct((t, n32), F32), jax.ShapeDtypeStruct((t, n - n32), BF16)]
    args = [x, g.reshape(1, d), w]
    if n_t:
        in_specs.append(pl.BlockSpec((n_t, d), lambda i, j: (0, 0)))
        out_specs.append(pl.BlockSpec((n_t, tm), lambda i, j: (0, i)))
        out_shape.append(jax.ShapeDtypeStruct((n_t, t), BF16))
        args.append(w_t)
    return pl.pallas_call(
        functools.partial(_norm_matmul_kernel, n_t=n_t, tiles32=tiles32),
        grid=(t // tm, n // tn),
        in_specs=in_specs,
        out_specs=out_specs,
        out_shape=out_shape,
        scratch_shapes=[pltpu.VMEM((tm, d), BF16)],
        compiler_params=pltpu.CompilerParams(
            dimension_semantics=("parallel", "arbitrary"), vmem_limit_bytes=VMEM_LIMIT),
        name="norm_matmul",
    )(*args)


def _mix_ffn_kernel(ca_ref, cb_ref, wa_ref, wb_ref, h_ref, gmix_ref, gpre_ref, gpost_ref,
                    wg_ref, wu_ref, wd_ref, o_ref, h1_ref, xn_ref, acc_ref):
    f = pl.program_id(1)

    @pl.when(f == 0)
    def _():
        y = (jnp.dot(ca_ref[...].astype(BF16), wa_ref[...], preferred_element_type=F32)
             + jnp.dot(cb_ref[...].astype(BF16), wb_ref[...], preferred_element_type=F32))
        h1 = h_ref[...] + _rms(y, gmix_ref[...])
        h1_ref[...] = h1
        xn_ref[...] = _rms(h1, gpre_ref[...]).astype(BF16)
        acc_ref[...] = jnp.zeros_like(acc_ref)

    xn = xn_ref[...]
    gate = jnp.dot(xn, wg_ref[...], preferred_element_type=F32)
    up = jnp.dot(xn, wu_ref[...], preferred_element_type=F32)
    act = (_silu(gate) * up).astype(BF16)
    acc_ref[...] += jnp.dot(act, wd_ref[...], preferred_element_type=F32)

    @pl.when(f == pl.num_programs(1) - 1)
    def _():
        o_ref[...] = h1_ref[...] + _rms(acc_ref[...], gpost_ref[...])


def _mix_ffn(ca, cb, w_out, h, g_mix, g_pre, g_post, wg, wu, wd, *, tm, tf):
    t, d = h.shape
    ff = wg.shape[1]
    wa_n = ca.shape[1]
    wb_n = cb.shape[1]
    row = pl.BlockSpec((1, d), lambda i, f: (0, 0))
    return pl.pallas_call(
        _mix_ffn_kernel,
        grid=(t // tm, ff // tf),
        in_specs=[pl.BlockSpec((tm, wa_n), lambda i, f: (i, 0)),
                  pl.BlockSpec((tm, wb_n), lambda i, f: (i, 0)),
                  pl.BlockSpec((wa_n, d), lambda i, f: (0, 0)),
                  pl.BlockSpec((wb_n, d), lambda i, f: (0, 0)),
                  pl.BlockSpec((tm, d), lambda i, f: (i, 0)),
                  row, row, row,
                  pl.BlockSpec((d, tf), lambda i, f: (0, f)),
                  pl.BlockSpec((d, tf), lambda i, f: (0, f)),
                  pl.BlockSpec((tf, d), lambda i, f: (f, 0))],
        out_specs=pl.BlockSpec((tm, d), lambda i, f: (i, 0)),
        out_shape=jax.ShapeDtypeStruct((t, d), F32),
        scratch_shapes=[pltpu.VMEM((tm, d), F32), pltpu.VMEM((tm, d), BF16), pltpu.VMEM((tm, d), F32)],
        compiler_params=pltpu.CompilerParams(
            dimension_semantics=("parallel", "arbitrary"), vmem_limit_bytes=VMEM_LIMIT),
        name="mix_ffn",
    )(ca, cb, w_out[:wa_n].astype(BF16), w_out[wa_n:].astype(BF16), h,
      g_mix.reshape(1, d), g_pre.reshape(1, d), g_post.reshape(1, d),
      wg.astype(BF16), wu.astype(BF16), wd.astype(BF16))


def _deltanet_kernel(xq_ref, xk_ref, xv_ref, z_ref, sm_ref, cwq_ref, cwk_ref, cwv_ref,
                     alog_ref, dtb_ref, gn_ref, o_ref,
                     xpad_ref, q_ref, k_ref, v_ref, gb_ref, bb_ref, u_ref, w_ref, qk_ref, st_ref,
                     *, ts, a_col, b_col):
    s = pl.program_id(1)
    c = CHUNK
    d = HEAD_DIM
    nh = N_HEADS

    @pl.when(s == 0)
    def _():
        xpad_ref[:, 0:8, :] = jnp.zeros((3, 8, nh * d), F32)
        st_ref[...] = jnp.zeros_like(st_ref)

    @pl.when(s != 0)
    def _():
        xpad_ref[:, 0:8, :] = xpad_ref[:, ts:ts + 8, :]

    xpad_ref[0, 8:ts + 8, :] = xq_ref[...]
    xpad_ref[1, 8:ts + 8, :] = xk_ref[...]
    xpad_ref[2, 8:ts + 8, :] = xv_ref[...]

    def conv_silu(idx, cw_ref, hs):
        cw = cw_ref[:, hs]
        acc = xpad_ref[idx, 8 - (CONV_WIDTH - 1):8 - (CONV_WIDTH - 1) + ts, hs] * cw[0:1, :]
        for j in range(1, CONV_WIDTH):
            off = 8 - (CONV_WIDTH - 1) + j
            acc = acc + xpad_ref[idx, off:off + ts, hs] * cw[j:j + 1, :]
        return _silu(acc)

    def l2norm(t):
        return t * lax.rsqrt(jnp.sum(t * t, axis=-1, keepdims=True) + EPS)

    row = _iota((c, c), 0)
    col = _iota((c, c), 1)
    tri = (col <= row)
    strict = (col < row)
    tri_f = tri.astype(F32)
    upper_f = (row <= col).astype(F32)
    eye = (row == col).astype(F32)
    gnorm = gn_ref[...]
    chunks = range(ts // c)
    rs = [slice(ci * c, (ci + 1) * c) for ci in chunks]
    tri2 = jnp.concatenate([tri_f, tri_f], axis=1).astype(BF16)
    ones2 = jnp.ones((c, 2 * c), BF16)

    def cum2(lhs2, x):
        hi, lo = _split(x)
        return jnp.dot(lhs2, jnp.concatenate([hi, lo], axis=0), preferred_element_type=F32)

    for hh in range(nh):
        hs = slice(hh * d, (hh + 1) * d)
        q_ref[:, hs] = l2norm(conv_silu(0, cwq_ref, hs)) * (d ** -0.5)
        k_ref[:, hs] = l2norm(conv_silu(1, cwk_ref, hs))
        v_ref[:, hs] = conv_silu(2, cwv_ref, hs)

        a_raw = sm_ref[:, a_col + hh:a_col + hh + 1]
        b_raw = sm_ref[:, b_col + hh:b_col + hh + 1]
        g = -jnp.exp(alog_ref[:, hh:hh + 1]) * _softplus(a_raw + dtb_ref[:, hh:hh + 1])
        gb_ref[:, hs] = jnp.broadcast_to(g, (ts, d))
        bb_ref[:, hs] = jnp.broadcast_to(_sigmoid(b_raw), (ts, d))

        q = [q_ref[r, hs] for r in rs]
        k = [k_ref[r, hs] for r in rs]
        beta = [bb_ref[r, hs] for r in rs]
        gb = [gb_ref[r, hs] for r in rs]
        gc = [cum2(tri2, x) for x in gb]
        gc_row = [cum2(ones2, x[:, :c] * upper_f) for x in gb]
        decay = [jnp.where(tri, jnp.exp(jnp.minimum(a[:, :c] - b, 0.0)), 0.0) for a, b in zip(gc, gc_row)]
        kk = [_mm_nt(x, x) for x in k]
        n = [-jnp.where(strict, b[:, :c] * x * dc, 0.0) for b, x, dc in zip(beta, kk, decay)]
        inv = [eye + x for x in n]
        for step in range(5):
            nb = [x.astype(BF16) for x in n]
            n = [jnp.dot(x, x, preferred_element_type=F32) for x in nb]
            inv = [iv + _mm(iv, x) for iv, x in zip(inv, n)]
        egc = [jnp.exp(x) for x in gc]
        gl = [x[c - 1:c, :] for x in gc]
        inv_l = [x.astype(BF16) for x in inv]
        u = [_mm(a, v_ref[r, hs] * b) for a, r, b in zip(inv_l, rs, beta)]
        w = [_mm(a, x * (b * e)) for a, x, b, e in zip(inv_l, k, beta, egc)]
        qk = [_mm_nt(a, b) * dc for a, b, dc in zip(q, k, decay)]
        for ci in chunks:
            r = rs[ci]
            u_ref[r, hs] = u[ci]
            w_ref[r, hs] = w[ci]
            qk_ref[hh, r, :] = qk[ci]
            q_ref[r, hs] = q[ci] * egc[ci]
            k_ref[r, hs] = k[ci] * jnp.exp(gl[ci] - gc[ci])
            gb_ref[r, hs] = jnp.broadcast_to(jnp.exp(gl[ci]), (c, d))

    def chunk_body(ci, carry):
        r0 = pl.multiple_of(ci * c, c)
        rows = pl.ds(r0, c)
        hss = [slice(hh * d, (hh + 1) * d) for hh in range(nh)]
        st = [st_ref[hh] for hh in range(nh)]
        w_st = [_mm(w_ref[rows, hs], s_) for hs, s_ in zip(hss, st)]
        q_st = [_mm(q_ref[rows, hs], s_) for hs, s_ in zip(hss, st)]
        v_new = [u_ref[rows, hs] - x for hs, x in zip(hss, w_st)]
        o = [a + _mm(qk_ref[hh, rows, :], v) for hh, (a, v) in enumerate(zip(q_st, v_new))]
        kv = [_mm_tn(k_ref[rows, hs], v) for hs, v in zip(hss, v_new)]
        for hh, hs in enumerate(hss):
            st_ref[hh] = st[hh] * gb_ref[pl.ds(r0, 1), hs] + kv[hh]
            o_ref[rows, hs] = _rms(o[hh], gnorm) * _silu(z_ref[rows, hs])
        return carry

    lax.fori_loop(0, ts // c, chunk_body, 0)


def _deltanet(p32, conv_w, a_log, dt_bias, a_norm_g, *, ts, cols):
    bsz, s, _ = p32.shape
    d = HEAD_DIM
    nh = N_HEADS
    w = nh * d
    pad = lambda t: jnp.pad(t.astype(F32), (0, d - t.shape[0])).reshape(1, d)
    kernel = functools.partial(_deltanet_kernel, ts=ts, a_col=cols["a_lane"], b_col=cols["b_lane"])
    tile = lambda name: pl.BlockSpec((None, ts, w), lambda b, i: (b, i, cols[name] // nh))
    conv = lambda k: pl.BlockSpec((CONV_WIDTH, w), lambda b, i: (0, k))
    row = pl.BlockSpec((1, d), lambda b, i: (0, 0))
    return pl.pallas_call(
        kernel,
        grid=(bsz, s // ts),
        in_specs=[tile("qa"), tile("ka"), tile("va"), tile("za"),
                  pl.BlockSpec((None, ts, d), lambda b, i: (b, i, cols["small"])),
                  conv(0), conv(1), conv(2), row, row, row],
        out_specs=pl.BlockSpec((None, ts, w), lambda b, i: (b, i, 0)),
        out_shape=jax.ShapeDtypeStruct((bsz, s, w), F32),
        scratch_shapes=[pltpu.VMEM((3, ts + 8, w), F32)]
        + [pltpu.VMEM((ts, w), F32) for _ in range(7)]
        + [pltpu.VMEM((nh, ts, CHUNK), F32), pltpu.VMEM((nh, d, d), F32)],
        compiler_params=pltpu.CompilerParams(
            dimension_semantics=("parallel", "arbitrary"), vmem_limit_bytes=VMEM_LIMIT),
        name="deltanet",
    )(p32, p32, p32, p32, p32, conv_w.astype(F32), conv_w.astype(F32), conv_w.astype(F32),
      pad(a_log), pad(dt_bias), a_norm_g.astype(F32).reshape(1, d))


def _hgrn2_kernel(q_ref, f_ref, i_ref, gate_ref, lb_ref, gn_ref, o_ref,
                  qs_ref, ks_ref, gc_ref, st_ref, *, ts):
    s = pl.program_id(1)
    c = CHUNK
    d = HEAD_DIM
    nh = N_HEADS
    SUB = 16

    @pl.when(s == 0)
    def _():
        st_ref[...] = jnp.zeros_like(st_ref)

    lb = lb_ref[...]
    f_raw = f_ref[...]
    log_sig = jnp.minimum(f_raw, 0.0) - jnp.log1p(jnp.exp(-jnp.abs(f_raw)))
    la = jnp.log(lb)
    lbb = jnp.log1p(-lb) + log_sig
    log_f = jnp.maximum(la, lbb) + jnp.log1p(jnp.exp(-jnp.abs(la - lbb)))
    qs_ref[...] = _silu(q_ref[...])
    ks_ref[...] = (1.0 - lb) * _sigmoid(-f_raw)

    row = _iota((c, c), 0)
    col = _iota((c, c), 1)
    tri_f = (col <= row).astype(F32)
    ones_dd = jnp.ones((d, d), BF16)
    rows_8d = _iota((8, d), 0)
    gnorm = gn_ref[...]

    tri2 = jnp.concatenate([tri_f, tri_f], axis=1).astype(BF16)
    for ci in range(ts // c):
        hi, lo = _split(log_f[ci * c:(ci + 1) * c, :])
        gc_ref[ci * c:(ci + 1) * c, :] = jnp.dot(tri2, jnp.concatenate([hi, lo], axis=0),
                                                 preferred_element_type=F32)

    blocks = [(sb * SUB, (sb + 1) * SUB) for sb in range(c // SUB)]

    def chunk_loop(ci, carry):
        r0 = pl.multiple_of(ci * c, c)
        rows = pl.ds(r0, c)
        hss = [slice(hh * d, (hh + 1) * d) for hh in range(nh)]
        q = [qs_ref[rows, hs] for hs in hss]
        k = [ks_ref[rows, hs] for hs in hss]
        v = [i_ref[rows, hs] for hs in hss]
        gc = [gc_ref[rows, hs] for hs in hss]

        def near_products(q, k, gc):
            gk = gc - jnp.log2(k)
            prods = []
            for top, end in blocks:
                for j in range(top, end):
                    lo = (j // 8) * 8
                    e = jnp.exp2(gc[lo:end, :] - gk[j:j + 1, :])
                    if j % 8:
                        head = jnp.where(rows_8d >= j - lo, e[:8], 0.0)
                        e = jnp.concatenate([head, e[8:]], axis=0) if lo + 8 < end else head
                    prods.append(q[lo:end, :] * e)
            return jnp.concatenate(prods, axis=0).astype(BF16)

        def far_operands(q, k, gc):
            out = []
            for top, end in blocks[1:]:
                g_b = gc[top - 1:top, :]
                out.append((q[top:end, :] * jnp.exp(gc[top:end, :] - g_b),
                            k[:top, :] * jnp.exp(jnp.minimum(g_b - gc[:top, :], 0.0))))
            return out

        near = [near_products(a, b, g * LOG2E) for a, b, g in zip(q, k, gc)]
        far_ops = [far_operands(*x) for x in zip(q, k, gc)]
        st = [st_ref[hh] for hh in range(nh)]
        gl = [x[c - 1:c, :] for x in gc]
        sums = [jnp.dot(x, ones_dd, preferred_element_type=F32) for x in near]
        qk_far = [[_mm_nt(qe, ke) for qe, ke in ops] for ops in far_ops]
        far = [[_mm(a, vv[:top, :]) for a, (top, _) in zip(qs, blocks[1:])] for qs, vv in zip(qk_far, v)]
        o_st = [_mm_nt(a * jnp.exp(g), s_) for a, g, s_ in zip(q, gc, st)]
        kv = [_mm_tn(vv, kk * jnp.exp(g_l - g)) for vv, kk, g_l, g in zip(v, k, gl, gc)]

        for hh, hs in enumerate(hss):
            groups = [jnp.zeros((8, d), F32) for _ in range(c // 8)]
            at = 0
            for top, end in blocks:
                for j in range(top, end):
                    v_j = v[hh][j:j + 1, :]
                    for g in range(j // 8, end // 8):
                        groups[g] = groups[g] + sums[hh][at:at + 8, :] * v_j
                        at += 8
            for f, (top, end) in zip(far[hh], blocks[1:]):
                for g in range(top // 8, end // 8):
                    groups[g] = groups[g] + f[(g * 8 - top):(g * 8 - top + 8), :]
            o = jnp.concatenate(groups, axis=0) + o_st[hh]
            st_ref[hh] = st[hh] * jnp.exp(gl[hh]) + kv[hh]
            o_ref[rows, hs] = _rms(o, gnorm) * _silu(gate_ref[rows, hs])
        return carry

    lax.fori_loop(0, ts // c, chunk_loop, 0)


def _hgrn2(p32, lb, d_norm_g, *, ts, cols):
    bsz, s, _ = p32.shape
    d = HEAD_DIM
    nh = N_HEADS
    w = nh * d
    kernel = functools.partial(_hgrn2_kernel, ts=ts)
    tile = lambda name: pl.BlockSpec((None, ts, w), lambda b, i: (b, i, cols[name] // nh))
    return pl.pallas_call(
        kernel,
        grid=(bsz, s // ts),
        in_specs=[tile("qd"), tile("fd"), tile("id"), tile("gd"),
                  pl.BlockSpec((1, w), lambda b, i: (0, 0)),
                  pl.BlockSpec((1, d), lambda b, i: (0, 0))],
        out_specs=pl.BlockSpec((None, ts, w), lambda b, i: (b, i, 0)),
        out_shape=jax.ShapeDtypeStruct((bsz, s, w), F32),
        scratch_shapes=[pltpu.VMEM((ts, w), F32), pltpu.VMEM((ts, w), F32),
                        pltpu.VMEM((ts, w), F32), pltpu.VMEM((nh, d, d), F32)],
        compiler_params=pltpu.CompilerParams(
            dimension_semantics=("parallel", "arbitrary"), vmem_limit_bytes=VMEM_LIMIT),
        name="hgrn2",
    )(p32, p32, p32, p32, lb.astype(F32).reshape(1, w), d_norm_g.astype(F32).reshape(1, d))


def _stickbreak_kernel(q_ref, k_ref, v_ref, o_ref, *, tq):
    i = pl.program_id(1)
    d = HEAD_DIM
    nh = N_HEADS
    row = _iota((tq, tq), 0)
    col = _iota((tq, tq), 1)
    causal = col < row
    later = (row > col).astype(BF16)
    later2 = jnp.concatenate([later, later], axis=0)

    heads = [slice(hh * d, (hh + 1) * d) for hh in range(nh)]

    def scores(blocks):
        jobs = [(j, dg, hs) for j, dg in blocks for hs in heads]
        z = [_mm_nt(q_ref[:, hs], k_ref[pl.ds(pl.multiple_of(j * tq, tq), tq), hs]) * (d ** -0.5)
             for j, _, hs in jobs]
        sp = [_softplus(x) for x in z]
        l1m = [jnp.where(causal, -x, 0.0) if dg else -x for x, (_, dg, _) in zip(sp, jobs)]
        rest = [jnp.dot(jnp.concatenate(_split(x), axis=1), later2, preferred_element_type=F32)
                for x in l1m]
        out = [((a - b) + r, l) for a, b, r, l in zip(z, sp, rest, l1m)]
        return [out[b * nh:(b + 1) * nh] for b in range(len(blocks))]

    def block(j, carries):
        (sc,) = scores([(j, False)])
        ps = [jnp.exp(logw + c) for (logw, _), c in zip(sc, carries)]
        pv = [_mm(p, v_ref[pl.ds(pl.multiple_of(j * tq, tq), tq), hs]) for p, hs in zip(ps, heads)]
        for hs, x in zip(heads, pv):
            o_ref[:, hs] += x
        return tuple(c + jnp.sum(l1m, axis=-1, keepdims=True) for (_, l1m), c in zip(sc, carries))

    jp = jnp.maximum(i - 1, 0)
    live = jnp.where(i > 0, 1.0, 0.0)
    sd, sp_ = scores([(i, True), (jp, False)])
    carries = []
    for hh, hs in enumerate(heads):
        c1 = jnp.sum(sd[hh][1], axis=-1, keepdims=True)
        p_d = jnp.where(causal, jnp.exp(sd[hh][0]), 0.0)
        p_p = jnp.exp(sp_[hh][0] + c1) * live
        o_ref[:, hs] = (_mm(p_d, v_ref[pl.ds(pl.multiple_of(i * tq, tq), tq), hs])
                        + _mm(p_p, v_ref[pl.ds(pl.multiple_of(jp * tq, tq), tq), hs]))
        carries.append(c1 + jnp.sum(sp_[hh][1], axis=-1, keepdims=True))
    carries = tuple(carries)

    def cond(c):
        worst = functools.reduce(jnp.maximum, c[1])
        return jnp.logical_and(c[0] >= 0, jnp.max(worst) >= EXP_ZERO_BELOW)

    def body(c):
        return c[0] - 1, block(c[0], c[1])

    lax.while_loop(cond, body, (i - 2, carries))


def _stickbreak(p16, *, tq, cols):
    bsz, s, _ = p16.shape
    nh = N_HEADS
    w = nh * HEAD_DIM
    kernel = functools.partial(_stickbreak_kernel, tq=tq)
    resident = dict(pipeline_mode=pl.Buffered(1))
    return pl.pallas_call(
        kernel,
        grid=(bsz, s // tq),
        in_specs=[pl.BlockSpec((None, tq, w), lambda b, i: (b, i, cols["qc"] // nh)),
                  pl.BlockSpec((None, s, w), lambda b, i: (b, 0, cols["kc"] // nh), **resident),
                  pl.BlockSpec((None, s, w), lambda b, i: (b, 0, cols["vc"] // nh), **resident)],
        out_specs=pl.BlockSpec((None, tq, w), lambda b, i: (b, i, 0)),
        out_shape=jax.ShapeDtypeStruct((bsz, s, w), F32),
        compiler_params=pltpu.CompilerParams(
            dimension_semantics=("parallel", "arbitrary"), vmem_limit_bytes=VMEM_LIMIT),
        name="stickbreak",
    )(p16, p16, p16)


def _dsa_kernel(qi_ref, smq_ref, q_ref, sm_ref, k_ref, vt_ref, bias_ref, o_ref,
                sc_ref, scb_ref, wb_ref, qc_ref, kct_ref, bd_ref, lg_ref, *, tq, k_sel, wi_lane, wide):
    i = pl.program_id(1)
    tk = tq
    d = HEAD_DIM
    nh = N_HEADS
    ksel = float(k_sel)
    per_wide = wide // tk
    n_wide = (i + per_wide) // per_wide
    sub = 2 * tk
    lane_q = _iota((1, tq), 1)

    def tree(parts, op):
        while len(parts) > 1:
            parts = [op(parts[j], parts[j + 1]) if j + 1 < len(parts) else parts[j]
                     for j in range(0, len(parts), 2)]
        return parts[0]

    def col_fold(x, op=jnp.add, rows=8):
        return tree([x[r * rows:(r + 1) * rows] for r in range(x.shape[0] // rows)], op)

    @pl.when(i == 0)
    def _():
        def prep(g, carry):
            g0 = pl.multiple_of(g * wide, wide)
            kt = sm_ref[pl.ds(g0, wide), :].T[:IDX_DIM, :]
            hi, lo = _split(kt)
            kct_ref[:, pl.ds(g0, wide)] = jnp.concatenate([hi, lo, hi], axis=0)
            return carry
        lax.fori_loop(0, sm_ref.shape[0] // wide, prep, 0)

    smq = smq_ref[...]
    lane = _iota(smq.shape, 1)
    for hh in range(IDX_HEADS):
        qh = qi_ref[:, hh * IDX_DIM:(hh + 1) * IDX_DIM]
        hi, lo = _split(qh)
        qc_ref[hh] = jnp.concatenate([hi, hi, lo], axis=-1)
        w = jnp.sum(jnp.where(lane == wi_lane + hh, smq, 0.0), axis=-1, keepdims=True)
        wb_ref[hh] = jnp.broadcast_to(w * ((IDX_HEADS ** -0.5) * (IDX_DIM ** -0.5)), (tq, tk))

    q2t = (q_ref[...] * ((d ** -0.5) * LOG2E)).T.astype(BF16)
    zero_dq = jnp.zeros((d, tq), BF16)
    for p in range(nh // 2):
        top = jnp.concatenate([q2t[2 * p * d:(2 * p + 1) * d], zero_dq], axis=1)
        bot = jnp.concatenate([zero_dq, q2t[(2 * p + 1) * d:(2 * p + 2) * d]], axis=1)
        bd_ref[p] = jnp.concatenate([top, bot], axis=0)

    limit = i * tq + (lane_q // CHUNK + 1) * CHUNK
    rows_t = _iota((tk, tq), 0)

    def score_group(g, mm, masked):
        mn, mx = mm
        for sb in range(wide // sub):
            k0 = pl.multiple_of(g * wide + sb * sub, sub)
            kct = kct_ref[:, pl.ds(k0, sub)]
            tiles = [jnp.zeros((tq, tk), F32) for _ in range(sub // tk)]
            for hh in range(IDX_HEADS):
                s_h = jnp.dot(qc_ref[hh], kct, preferred_element_type=F32)
                for ti in range(sub // tk):
                    tiles[ti] = tiles[ti] + jnp.maximum(s_h[:, ti * tk:(ti + 1) * tk], 0.0) * wb_ref[hh]
            for ti in range(sub // tk):
                kb = pl.multiple_of(k0 + ti * tk, tk)
                sct = tiles[ti].T
                if masked:
                    adm = (kb + rows_t) < limit
                    mn = jnp.minimum(mn, col_fold(jnp.where(adm, sct, jnp.inf), jnp.minimum))
                    sct = jnp.where(adm, sct, -jnp.inf)
                else:
                    mn = jnp.minimum(mn, col_fold(sct, jnp.minimum))
                mx = jnp.maximum(mx, col_fold(sct, jnp.maximum))
                sc_ref[pl.ds(kb, tk), :] = sct
                scb_ref[pl.ds(kb, tk), :] = _floor_bf16(sct)
        return mn, mx

    def score_pair(j, mm):
        return score_group(2 * j + 1, score_group(2 * j, mm, False), False)

    n_full = n_wide - 1
    mm = lax.fori_loop(0, n_full // 2, score_pair,
                       (jnp.full((8, tq), jnp.inf, F32), jnp.full((8, tq), -jnp.inf, F32)))
    mm = lax.cond(n_full % 2 == 1, lambda c: score_group(n_full - 1, c, False), lambda c: c, mm)
    mn, mx = score_group(n_wide - 1, mm, True)

    n_pairs = (n_wide + 1) // 2

    @pl.when(n_wide % 2 == 1)
    def _():
        sc_ref[pl.ds(pl.multiple_of(n_wide * wide, wide), wide), :] = jnp.full((wide, tq), -jnp.inf, F32)
        scb_ref[pl.ds(pl.multiple_of(n_wide * wide, wide), wide), :] = jnp.full((wide, tq), -jnp.inf, BF16)
    rmin = jnp.min(mn, axis=0, keepdims=True)
    rmax = jnp.max(mx, axis=0, keepdims=True)

    def count(pred):
        def body(j, acc):
            for g in (2 * j, 2 * j + 1):
                acc = acc + col_fold(pred(sc_ref[pl.ds(pl.multiple_of(g * wide, wide), wide), :]))
            return acc
        return jnp.sum(lax.fori_loop(0, n_pairs, body, jnp.zeros((8, tq), F32)), axis=0, keepdims=True)

    def max_below(x):
        def body(j, acc):
            for g in (2 * j, 2 * j + 1):
                blk = sc_ref[pl.ds(pl.multiple_of(g * wide, wide), wide), :]
                acc = jnp.maximum(acc, col_fold(jnp.where(blk < x, blk, -jnp.inf), jnp.maximum))
            return acc
        return jnp.max(lax.fori_loop(0, n_pairs, body, jnp.full((8, tq), -jnp.inf, F32)), axis=0, keepdims=True)

    n_adm = limit.astype(F32)
    all_sel = n_adm <= ksel

    def bisect(c):
        lo, hi, c_lo = c
        mid = 0.5 * lo + 0.5 * hi
        cm = count(lambda blk: _ind(blk >= mid))
        ge = cm >= ksel
        return jnp.where(ge, mid, lo), jnp.where(ge, hi, mid), jnp.where(ge, cm, c_lo)

    def pending(c_lo, tied):
        return jnp.where(all_sel, 0.0, jnp.where(tied > 0.5, 0.0, _ind(c_lo != ksel)))

    def bisect_coarse(_, c):
        lo, hi, c_lo = c
        mid = _floor_bf16(0.5 * lo + 0.5 * hi).astype(F32)
        t_b = jnp.broadcast_to(mid, (16, tq)).astype(BF16)
        one_b = jnp.ones((16, tq), BF16)
        zero_b = jnp.zeros((16, tq), BF16)

        def body(j, acc):
            for g in (2 * j, 2 * j + 1):
                blk = scb_ref[pl.ds(pl.multiple_of(g * wide, wide), wide), :]
                ind = [jnp.where(blk[r * 16:(r + 1) * 16] >= t_b, one_b, zero_b) for r in range(wide // 16)]
                acc = acc + tree(ind, jnp.add).astype(F32)
            return acc

        acc = lax.fori_loop(0, n_pairs, body, jnp.zeros((16, tq), F32))
        cm = jnp.sum(acc, axis=0, keepdims=True)
        ge = cm >= ksel
        return jnp.where(ge, mid, lo), jnp.where(ge, hi, mid), jnp.where(ge, cm, c_lo)

    lo0 = _floor_bf16(rmin).astype(F32)
    hi0 = _floor_bf16(rmax + (jnp.abs(rmax) * (2.0 ** -6) + 1e-30)).astype(F32)
    state = lax.fori_loop(0, BISECT_COARSE, bisect_coarse, (lo0, hi0, n_adm))
    state = lax.fori_loop(0, BISECT_FIXED, lambda _, c: bisect(c), state)

    def round_cond(c):
        return jnp.max(pending(c[0][2], c[1])) > 0.5

    def round_body(c):
        st, tied, v, need = c

        def more_cond(s):
            return jnp.logical_and(s[0] < BISECT_EXTRA, jnp.max(pending(s[1][2], tied)) > 0.5)

        _, st = lax.while_loop(more_cond, lambda s: (s[0] + 1, bisect(s[1])), (jnp.int32(0), st))
        pend = pending(st[2], tied)

        def check(_):
            cand = max_below(st[1])
            c_ge = count(lambda blk: _ind(blk >= cand))
            c_gt = count(lambda blk: _ind(blk > cand))
            ok = jnp.where(pend > 0.5, _ind(c_ge >= ksel), 0.0)
            return (jnp.where(ok > 0.5, 1.0, tied), jnp.where(ok > 0.5, cand, v),
                    jnp.where(ok > 0.5, ksel - c_gt, need))

        tied, v, need = lax.cond(jnp.max(pend) > 0.5, check, lambda _: (tied, v, need), 0)
        return st, tied, v, need

    zeros1 = jnp.zeros((1, tq), F32)
    (lo_f, _, _), tied, v_tie, need = lax.while_loop(round_cond, round_body, (state, zeros1, zeros1, zeros1))
    vth = jnp.where(all_sel, F32_LOWEST, jnp.where(tied > 0.5, v_tie, lo_f))

    @pl.when(jnp.max(tied) > 0.5)
    def _():
        v_eq = jnp.where(tied > 0.5, v_tie, jnp.inf)
        incl = (_iota((tk, tk), 1) <= _iota((tk, tk), 0)).astype(BF16)

        def demote(g, seen):
            g0 = pl.multiple_of(g * wide, wide)
            xs = [sc_ref[pl.ds(g0 + pb * tk, tk), :] for pb in range(per_wide)]
            eqs = [_ind(x == v_eq) for x in xs]
            inblk = [jnp.dot(incl, e.astype(BF16), preferred_element_type=F32) for e in eqs]
            for pb in range(per_wide):
                rank = inblk[pb] + seen
                sc_ref[pl.ds(g0 + pb * tk, tk), :] = jnp.where(eqs[pb] * _ind(rank > need) > 0.5,
                                                               -jnp.inf, xs[pb])
                seen = seen + jnp.sum(col_fold(eqs[pb]), axis=0, keepdims=True)
            return seen

        lax.fori_loop(0, n_wide, demote, zeros1)

    g_near = jnp.maximum(i - 1, 0) // per_wide

    def logit_group(g, mx, near):
        out = list(mx)
        for sb in range(wide // sub):
            k0 = pl.multiple_of(g * wide + sb * sub, sub)
            sel = sc_ref[pl.ds(k0, sub), :] >= vth
            for p in range(nh // 2):
                pair = jnp.dot(k_ref[pl.ds(k0, sub), 2 * p * d:(2 * p + 2) * d], bd_ref[p],
                               preferred_element_type=F32)
                for hh in (2 * p, 2 * p + 1):
                    lm = pair[:, (hh - 2 * p) * tq:(hh - 2 * p + 1) * tq]
                    if near:
                        back = [jnp.clip(i - (g * per_wide + sb * (sub // tk) + pb), 0, 2)
                                for pb in range(sub // tk)]
                        lm = lm + jnp.concatenate([bias_ref[bk, hh] for bk in back], axis=0)
                    lm = jnp.where(sel, lm, NEG_BIG)
                    lg_ref[hh, pl.ds(k0, sub), :] = lm
                    out[hh] = jnp.maximum(out[hh], col_fold(lm, jnp.maximum))
        return tuple(out)

    mx = tuple(jnp.full((8, tq), NEG_BIG, F32) for _ in range(nh))
    def logit_pair(j, mx, near):
        return logit_group(2 * j + 1, logit_group(2 * j, mx, near), near)

    far_pairs = g_near // 2
    mx = lax.fori_loop(0, far_pairs, functools.partial(logit_pair, near=False), mx)
    mx = lax.fori_loop(far_pairs, n_pairs, functools.partial(logit_pair, near=True), mx)
    m_q = [jnp.max(mx[hh], axis=0, keepdims=True) for hh in range(nh)]

    ones_rows = jnp.ones((8, wide), BF16)

    def pv_pair(j, carry):
        ls, accs = list(carry[0]), list(carry[1])
        jobs = [(pl.multiple_of(g * wide, wide), hh) for g in (2 * j, 2 * j + 1) for hh in range(nh)]
        ps = [jnp.exp2(lg_ref[hh, pl.ds(g0, wide), :] - m_q[hh]).astype(BF16) for g0, hh in jobs]
        outs = [jnp.dot(jnp.concatenate([vt_ref[hh * d:(hh + 1) * d, pl.ds(g0, wide)], ones_rows], axis=0),
                        p, preferred_element_type=F32) for (g0, hh), p in zip(jobs, ps)]
        for (_, hh), out in zip(jobs, outs):
            ls[hh] = ls[hh] + out[d:]
            accs[hh] = accs[hh] + out[:d]
        return tuple(ls), tuple(accs)

    ls, accs = lax.fori_loop(0, n_pairs, pv_pair,
                             (tuple(jnp.zeros((8, tq), F32) for _ in range(nh)),
                              tuple(jnp.zeros((d, tq), F32) for _ in range(nh))))
    for hh in range(nh):
        o_ref[:, hh * d:(hh + 1) * d] = (accs[hh] / ls[hh][0:1]).T


def _dsa(p32, p16, vt, bias_tiles, *, tq, cols):
    bsz, s, _ = p32.shape
    d = HEAD_DIM
    nh = N_HEADS
    wide = 4 * tq
    k_sel = min(TOPK_MAX, s // 4)
    w512 = nh * d
    kernel = functools.partial(_dsa_kernel, tq=tq, k_sel=k_sel, wi_lane=cols["wi_lane"], wide=wide)
    resident = dict(pipeline_mode=pl.Buffered(1))
    return pl.pallas_call(
        kernel,
        grid=(bsz, s // tq),
        in_specs=[pl.BlockSpec((None, tq, w512), lambda b, i: (b, i, cols["qi"] // nh)),
                  pl.BlockSpec((None, tq, d), lambda b, i: (b, i, cols["small"])),
                  pl.BlockSpec((None, tq, w512), lambda b, i: (b, i, cols["qb"] // nh)),
                  pl.BlockSpec((None, s, d), lambda b, i: (b, 0, cols["small"]), **resident),
                  pl.BlockSpec((None, s, w512), lambda b, i: (b, 0, cols["kb"] // nh), **resident),
                  pl.BlockSpec((w512, s), lambda b, i: (0, b), **resident),
                  pl.BlockSpec((3, nh, tq, tq), lambda b, i: (0, 0, 0, 0), **resident)],
        out_specs=pl.BlockSpec((None, tq, w512), lambda b, i: (b, i, 0)),
        out_shape=jax.ShapeDtypeStruct((bsz, s, w512), F32),
        scratch_shapes=[pltpu.VMEM((s, tq), F32),
                        pltpu.VMEM((s, tq), BF16),
                        pltpu.VMEM((IDX_HEADS, tq, tq), F32),
                        pltpu.VMEM((IDX_HEADS, tq, 3 * IDX_DIM), BF16),
                        pltpu.VMEM((3 * IDX_DIM, s), BF16),
                        pltpu.VMEM((nh // 2, 2 * d, 2 * tq), BF16),
                        pltpu.VMEM((nh, s, tq), F32)],
        compiler_params=pltpu.CompilerParams(
            dimension_semantics=("parallel", "arbitrary"), vmem_limit_bytes=VMEM_LIMIT),
        name="dsa",
    )(p32, p32, p32, p32, p16, vt, bias_tiles)


def _t5_bucket(rel):
    nb = REL_BUCKETS // 2
    max_exact = nb // 2
    ret = jnp.where(rel > 0, nb, 0)
    n = jnp.abs(rel)
    large = max_exact + (jnp.log(jnp.maximum(n, 1).astype(F32) / max_exact)
                         / math.log(REL_MAX_DIST / max_exact) * (nb - max_exact)).astype(jnp.int32)
    large = jnp.minimum(large, nb - 1)
    return ret + jnp.where(n < max_exact, n, large)


def _bias_tiles(rel_table, tq):
    assert tq >= REL_MAX_DIST
    t = jnp.arange(tq)
    back = jnp.arange(3)
    rel = (t[None, None, :] - back[:, None, None] * tq) - t[None, :, None]
    onehot = (_t5_bucket(rel)[..., None] == jnp.arange(REL_BUCKETS)).astype(F32)
    tiles = jnp.einsum("bqkn,nh->bhkq", onehot, rel_table.astype(F32),
                       precision=HIGHEST)
    return (tiles - tiles[2:3]) * LOG2E


def _even_layout(w_in):
    d = HEAD_DIM
    a_w = 2 * N_HEADS * d + N_HEADS * d
    offs = {}
    o = 0
    for name, w in (("qkv", a_w), ("z", N_HEADS * d), ("a", N_HEADS), ("b", N_HEADS),
                    ("qb", N_HEADS * d), ("kb", N_HEADS * d), ("vb", N_HEADS * d),
                    ("qi", IDX_HEADS * IDX_DIM), ("ki", IDX_DIM), ("wi", IDX_HEADS)):
        offs[name] = (o, o + w)
        o += w
    assert o == w_in.shape[1]
    sl = lambda n: w_in[:, offs[n][0]:offs[n][1]]
    small_w = IDX_DIM + 2 * N_HEADS + IDX_HEADS
    small_pad = -small_w % d
    zeros = lambda n: jnp.zeros((w_in.shape[0], n), w_in.dtype)
    w32 = jnp.concatenate([sl("qkv"), sl("z"), sl("qb"), sl("qi"),
                           sl("ki"), sl("a"), sl("b"), sl("wi"), zeros(small_pad)], axis=1)
    n32 = w32.shape[1]
    tn = n32 // 5
    assert tn * 5 == n32 and tn % d == 0
    w16 = jnp.concatenate([sl("kb"), zeros(tn - N_HEADS * d)], axis=1)
    nh = N_HEADS
    cols = dict(qa=0, ka=nh, va=2 * nh, za=3 * nh, qb=4 * nh, qi=5 * nh, small=6 * nh, kb=0,
                a_lane=IDX_DIM, b_lane=IDX_DIM + nh, wi_lane=IDX_DIM + 2 * nh, n32=n32, tn=tn)
    return jnp.concatenate([w32, w16], axis=1).astype(BF16), sl("vb").T.astype(BF16), cols


def kernel(x, norm_g, w_in_even, conv_w_even, a_log_even, dt_bias_even, a_norm_even, w_out_even,
           rel_bias, w_in_odd, lb_logits, d_norm_odd, w_out_odd, w_gate, w_up, w_down):
    bsz, s, d = x.shape
    t = bsz * s
    depth = norm_g.shape[0]
    nh = N_HEADS
    tq = Q_TILE
    lb_all = jnp.cumsum(jax.nn.softmax(lb_logits.astype(F32), axis=0), axis=0)
    lb_all = lb_all - lb_all[:1]
    odd_cols = dict(qc=0, kc=nh, vc=2 * nh, qd=0, fd=nh, id=2 * nh, gd=3 * nh)
    bias_tiles = _bias_tiles(rel_bias, tq)

    h = x.reshape(t, d)
    for l in range(depth):
        if l % 2 == 0:
            e = l // 2
            w_even, w_vt, cols = _even_layout(w_in_even[e])
            p32, p16, vt = _norm_matmul(h, norm_g[l, 0], w_even, tm=ROW_TILE, tn=cols["tn"], n32=cols["n32"],
                                        w_t=w_vt)
            p32 = p32.reshape(bsz, s, -1)
            p16 = p16.reshape(bsz, s, -1)
            o_1 = _deltanet(p32, conv_w_even[e], a_log_even[e], dt_bias_even[e], a_norm_even[e],
                            ts=min(SEQ_TILE, s), cols=cols)
            o_2 = _dsa(p32, p16, vt, bias_tiles, tq=tq, cols=cols)
            w_out = w_out_even[e]
        else:
            o = l // 2
            n16 = 3 * nh * HEAD_DIM
            w_odd = jnp.concatenate([w_in_odd[o][:, n16:], w_in_odd[o][:, :n16]], axis=1).astype(BF16)
            p32, p16 = _norm_matmul(h, norm_g[l, 0], w_odd, tm=ROW_TILE, tn=ODD_COL_TILE, n32=w_odd.shape[1] - n16)
            p32 = p32.reshape(bsz, s, -1)
            p16 = p16.reshape(bsz, s, -1)
            o_1 = _stickbreak(p16, tq=tq, cols=odd_cols)
            o_2 = _hgrn2(p32, lb_all[l], d_norm_odd[o], ts=min(SEQ_TILE, s), cols=odd_cols)
            w_out = w_out_odd[o]
        h = _mix_ffn(o_1.reshape(t, -1), o_2.reshape(t, -1), w_out, h, norm_g[l, 1], norm_g[l, 2], norm_g[l, 3],
                     w_gate[l], w_up[l], w_down[l], tm=ROW_TILE, tf=FFN_TILE)
    return h.reshape(bsz, s, d)
```

```python
import functools
import math

import jax
import jax.numpy as jnp
from jax import lax
from jax.experimental import pallas as pl
from jax.experimental.pallas import tpu as pltpu

F32 = jnp.float32
BF16 = jnp.bfloat16
HIGHEST = lax.Precision.HIGHEST

CHUNK = 64
HEAD_DIM = 128
N_HEADS = 4
IDX_HEADS = 8
IDX_DIM = 64
TOPK_MAX = 256
CONV_WIDTH = 4
REL_BUCKETS = 32
REL_MAX_DIST = 128
EPS = 1e-6
NEG_BIG = -1e30
LOG2E = 1.4426950408889634
BISECT_COARSE = 12
BISECT_FIXED = 8
BISECT_EXTRA = 6
F32_LOWEST = -3.4028234663852886e38
EXP_ZERO_BELOW = -104.0
VMEM_LIMIT = 56 * 1024 * 1024

PROJ_TILE = 2048
ROW_TILE = 1024
SEQ_TILE = 512
Q_TILE = 128
ODD_COL_TILE = 512
FFN_TILE = 256


def _mm(a, b):
    return jnp.dot(a.astype(BF16), b.astype(BF16), preferred_element_type=F32)


def _mm_nt(a, b):
    return lax.dot_general(a.astype(BF16), b.astype(BF16), (((1,), (1,)), ((), ())),
                           preferred_element_type=F32)


def _mm_tn(a, b):
    return lax.dot_general(a.astype(BF16), b.astype(BF16), (((0,), (0,)), ((), ())),
                           preferred_element_type=F32)


def _split(x):
    hi = x.astype(BF16)
    return hi, (x - hi.astype(F32)).astype(BF16)


def _floor_bf16(x):
    bits = pltpu.bitcast(x, jnp.int32)
    down = jnp.where(bits >= 0, bits, bits + 0xFFFF) & jnp.int32(-65536)
    return pltpu.bitcast(down, F32).astype(BF16)


def _sigmoid(x):
    return 1.0 / (1.0 + jnp.exp(-x))


def _silu(x):
    return x * _sigmoid(x)


def _softplus(x):
    return jnp.maximum(x, 0.0) + jnp.log1p(jnp.exp(-jnp.abs(x)))


def _rms(x, g):
    return x * lax.rsqrt(jnp.mean(x * x, axis=-1, keepdims=True) + EPS) * g


def _iota(shape, dim):
    return lax.broadcasted_iota(jnp.int32, shape, dim)


def _ind(mask):
    return jnp.where(mask, 1.0, 0.0)


def _norm_matmul_kernel(x_ref, g_ref, w_ref, *rest, n_t, tiles32):
    if n_t:
        wt_ref, o32_ref, o16_ref, ot_ref, xn_ref = rest
    else:
        o32_ref, o16_ref, xn_ref = rest
    j = pl.program_id(1)

    @pl.when(j == 0)
    def _():
        xn_ref[...] = _rms(x_ref[...], g_ref[...]).astype(BF16)
        if n_t:
            ot_ref[...] = lax.dot_general(wt_ref[...], xn_ref[...], (((1,), (1,)), ((), ())),
                                          preferred_element_type=F32).astype(BF16)

    y = jnp.dot(xn_ref[...], w_ref[...], preferred_element_type=F32)

    @pl.when(j < tiles32)
    def _():
        o32_ref[...] = y

    @pl.when(j >= tiles32)
    def _():
        o16_ref[...] = y.astype(BF16)


def _norm_matmul(x, g, w, *, tm, tn, n32, w_t=None):
    t, d = x.shape
    n = w.shape[1]
    n_t = 0 if w_t is None else w_t.shape[0]
    tiles32 = n32 // tn
    assert tiles32 * tn == n32 and (n - n32) % tn == 0 and 0 < n32 < n
    in_specs = [pl.BlockSpec((tm, d), lambda i, j: (i, 0)),
                pl.BlockSpec((1, d), lambda i, j: (0, 0)),
                pl.BlockSpec((d, tn), lambda i, j: (0, j))]
    out_specs = [pl.BlockSpec((tm, tn), lambda i, j: (i, jnp.minimum(j, tiles32 - 1))),
                 pl.BlockSpec((tm, tn), lambda i, j: (i, jnp.maximum(j - tiles32, 0)))]
    out_shape = [jax.ShapeDtypeStruct((t, n32), F32), jax.ShapeDtypeStruct((t, n - n32), BF16)]
    args = [x, g.reshape(1, d), w]
    if n_t:
        in_specs.append(pl.BlockSpec((n_t, d), lambda i, j: (0, 0)))
        out_specs.append(pl.BlockSpec((n_t, tm), lambda i, j: (0, i)))
        out_shape.append(jax.ShapeDtypeStruct((n_t, t), BF16))
        args.append(w_t)
    return pl.pallas_call(
        functools.partial(_norm_matmul_kernel, n_t=n_t, tiles32=tiles32),
        grid=(t // tm, n // tn),
        in_specs=in_specs,
        out_specs=out_specs,
        out_shape=out_shape,
        scratch_shapes=[pltpu.VMEM((tm, d), BF16)],
        compiler_params=pltpu.CompilerParams(
            dimension_semantics=("parallel", "arbitrary"), vmem_limit_bytes=VMEM_LIMIT),
        name="norm_matmul",
    )(*args)


def _mix_ffn_kernel(ca_ref, cb_ref, wa_ref, wb_ref, h_ref, gmix_ref, gpre_ref, gpost_ref,
                    wg_ref, wu_ref, wd_ref, o_ref, h1_ref, xn_ref, acc_ref):
    f = pl.program_id(1)

    @pl.when(f == 0)
    def _():
        y = (jnp.dot(ca_ref[...].astype(BF16), wa_ref[...], preferred_element_type=F32)
             + jnp.dot(cb_ref[...].astype(BF16), wb_ref[...], preferred_element_type=F32))
        h1 = h_ref[...] + _rms(y, gmix_ref[...])
        h1_ref[...] = h1
        xn_ref[...] = _rms(h1, gpre_ref[...]).astype(BF16)
        acc_ref[...] = jnp.zeros_like(acc_ref)

    xn = xn_ref[...]
    gate = jnp.dot(xn, wg_ref[...], preferred_element_type=F32)
    up = jnp.dot(xn, wu_ref[...], preferred_element_type=F32)
    act = (_silu(gate) * up).astype(BF16)
    acc_ref[...] += jnp.dot(act, wd_ref[...], preferred_element_type=F32)

    @pl.when(f == pl.num_programs(1) - 1)
    def _():
        o_ref[...] = h1_ref[...] + _rms(acc_ref[...], gpost_ref[...])


def _mix_ffn(ca, cb, w_out, h, g_mix, g_pre, g_post, wg, wu, wd, *, tm, tf):
    t, d = h.shape
    ff = wg.shape[1]
    wa_n = ca.shape[1]
    wb_n = cb.shape[1]
    row = pl.BlockSpec((1, d), lambda i, f: (0, 0))
    return pl.pallas_call(
        _mix_ffn_kernel,
        grid=(t // tm, ff // tf),
        in_specs=[pl.BlockSpec((tm, wa_n), lambda i, f: (i, 0)),
                  pl.BlockSpec((tm, wb_n), lambda i, f: (i, 0)),
                  pl.BlockSpec((wa_n, d), lambda i, f: (0, 0)),
                  pl.BlockSpec((wb_n, d), lambda i, f: (0, 0)),
                  pl.BlockSpec((tm, d), lambda i, f: (i, 0)),
                  row, row, row,
                  pl.BlockSpec((d, tf), lambda i, f: (0, f)),
                  pl.BlockSpec((d, tf), lambda i, f: (0, f)),
                  pl.BlockSpec((tf, d), lambda i, f: (f, 0))],
        out_specs=pl.BlockSpec((tm, d), lambda i, f: (i, 0)),
        out_shape=jax.ShapeDtypeStruct((t, d), F32),
        scratch_shapes=[pltpu.VMEM((tm, d), F32), pltpu.VMEM((tm, d), BF16), pltpu.VMEM((tm, d), F32)],
        compiler_params=pltpu.CompilerParams(
            dimension_semantics=("parallel", "arbitrary"), vmem_limit_bytes=VMEM_LIMIT),
        name="mix_ffn",
    )(ca, cb, w_out[:wa_n].astype(BF16), w_out[wa_n:].astype(BF16), h,
      g_mix.reshape(1, d), g_pre.reshape(1, d), g_post.reshape(1, d),
      wg.astype(BF16), wu.astype(BF16), wd.astype(BF16))


def _deltanet_kernel(xq_ref, xk_ref, xv_ref, z_ref, sm_ref, cwq_ref, cwk_ref, cwv_ref,
                     alog_ref, dtb_ref, gn_ref, o_ref,
                     xpad_ref, q_ref, k_ref, v_ref, gb_ref, bb_ref, u_ref, w_ref, qk_ref, st_ref,
                     *, ts, a_col, b_col):
    s = pl.program_id(1)
    c = CHUNK
    d = HEAD_DIM
    nh = N_HEADS

    @pl.when(s == 0)
    def _():
        xpad_ref[:, 0:8, :] = jnp.zeros((3, 8, nh * d), F32)
        st_ref[...] = jnp.zeros_like(st_ref)

    @pl.when(s != 0)
    def _():
        xpad_ref[:, 0:8, :] = xpad_ref[:, ts:ts + 8, :]

    xpad_ref[0, 8:ts + 8, :] = xq_ref[...]
    xpad_ref[1, 8:ts + 8, :] = xk_ref[...]
    xpad_ref[2, 8:ts + 8, :] = xv_ref[...]

    def conv_silu(idx, cw_ref, hs):
        cw = cw_ref[:, hs]
        acc = xpad_ref[idx, 8 - (CONV_WIDTH - 1):8 - (CONV_WIDTH - 1) + ts, hs] * cw[0:1, :]
        for j in range(1, CONV_WIDTH):
            off = 8 - (CONV_WIDTH - 1) + j
            acc = acc + xpad_ref[idx, off:off + ts, hs] * cw[j:j + 1, :]
        return _silu(acc)

    def l2norm(t):
        return t * lax.rsqrt(jnp.sum(t * t, axis=-1, keepdims=True) + EPS)

    row = _iota((c, c), 0)
    col = _iota((c, c), 1)
    tri = (col <= row)
    strict = (col < row)
    tri_f = tri.astype(F32)
    upper_f = (row <= col).astype(F32)
    eye = (row == col).astype(F32)
    gnorm = gn_ref[...]
    chunks = range(ts // c)
    rs = [slice(ci * c, (ci + 1) * c) for ci in chunks]
    tri2 = jnp.concatenate([tri_f, tri_f], axis=1).astype(BF16)
    ones2 = jnp.ones((c, 2 * c), BF16)

    def cum2(lhs2, x):
        hi, lo = _split(x)
        return jnp.dot(lhs2, jnp.concatenate([hi, lo], axis=0), preferred_element_type=F32)

    for hh in range(nh):
        hs = slice(hh * d, (hh + 1) * d)
        q_ref[:, hs] = l2norm(conv_silu(0, cwq_ref, hs)) * (d ** -0.5)
        k_ref[:, hs] = l2norm(conv_silu(1, cwk_ref, hs))
        v_ref[:, hs] = conv_silu(2, cwv_ref, hs)

        a_raw = sm_ref[:, a_col + hh:a_col + hh + 1]
        b_raw = sm_ref[:, b_col + hh:b_col + hh + 1]
        g = -jnp.exp(alog_ref[:, hh:hh + 1]) * _softplus(a_raw + dtb_ref[:, hh:hh + 1])
        gb_ref[:, hs] = jnp.broadcast_to(g, (ts, d))
        bb_ref[:, hs] = jnp.broadcast_to(_sigmoid(b_raw), (ts, d))

        q = [q_ref[r, hs] for r in rs]
        k = [k_ref[r, hs] for r in rs]
        beta = [bb_ref[r, hs] for r in rs]
        gb = [gb_ref[r, hs] for r in rs]
        gc = [cum2(tri2, x) for x in gb]
        gc_row = [cum2(ones2, x[:, :c] * upper_f) for x in gb]
        decay = [jnp.where(tri, jnp.exp(jnp.minimum(a[:, :c] - b, 0.0)), 0.0) for a, b in zip(gc, gc_row)]
        kk = [_mm_nt(x, x) for x in k]
        n = [-jnp.where(strict, b[:, :c] * x * dc, 0.0) for b, x, dc in zip(beta, kk, decay)]
        inv = [eye + x for x in n]
        for step in range(5):
            nb = [x.astype(BF16) for x in n]
            n = [jnp.dot(x, x, preferred_element_type=F32) for x in nb]
            inv = [iv + _mm(iv, x) for iv, x in zip(inv, n)]
        egc = [jnp.exp(x) for x in gc]
        gl = [x[c - 1:c, :] for x in gc]
        inv_l = [x.astype(BF16) for x in inv]
        u = [_mm(a, v_ref[r, hs] * b) for a, r, b in zip(inv_l, rs, beta)]
        w = [_mm(a, x * (b * e)) for a, x, b, e in zip(inv_l, k, beta, egc)]
        qk = [_mm_nt(a, b) * dc for a, b, dc in zip(q, k, decay)]
        for ci in chunks:
            r = rs[ci]
            u_ref[r, hs] = u[ci]
            w_ref[r, hs] = w[ci]
            qk_ref[hh, r, :] = qk[ci]
            q_ref[r, hs] = q[ci] * egc[ci]
            k_ref[r, hs] = k[ci] * jnp.exp(gl[ci] - gc[ci])
            gb_ref[r, hs] = jnp.broadcast_to(jnp.exp(gl[ci]), (c, d))

    def chunk_body(ci, carry):
        r0 = pl.multiple_of(ci * c, c)
        rows = pl.ds(r0, c)
        hss = [slice(hh * d, (hh + 1) * d) for hh in range(nh)]
        st = [st_ref[hh] for hh in range(nh)]
        w_st = [_mm(w_ref[rows, hs], s_) for hs, s_ in zip(hss, st)]
        q_st = [_mm(q_ref[rows, hs], s_) for hs, s_ in zip(hss, st)]
        v_new = [u_ref[rows, hs] - x for hs, x in zip(hss, w_st)]
        o = [a + _mm(qk_ref[hh, rows, :], v) for hh, (a, v) in enumerate(zip(q_st, v_new))]
        kv = [_mm_tn(k_ref[rows, hs], v) for hs, v in zip(hss, v_new)]
        for hh, hs in enumerate(hss):
            st_ref[hh] = st[hh] * gb_ref[pl.ds(r0, 1), hs] + kv[hh]
            o_ref[rows, hs] = _rms(o[hh], gnorm) * _silu(z_ref[rows, hs])
        return carry

    lax.fori_loop(0, ts // c, chunk_body, 0)


def _deltanet(p32, conv_w, a_log, dt_bias, a_norm_g, *, ts, cols):
    bsz, s, _ = p32.shape
    d = HEAD_DIM
    nh = N_HEADS
    w = nh * d
    pad = lambda t: jnp.pad(t.astype(F32), (0, d - t.shape[0])).reshape(1, d)
    kernel = functools.partial(_deltanet_kernel, ts=ts, a_col=cols["a_lane"], b_col=cols["b_lane"])
    tile = lambda name: pl.BlockSpec((None, ts, w), lambda b, i: (b, i, cols[name] // nh))
    conv = lambda k: pl.BlockSpec((CONV_WIDTH, w), lambda b, i: (0, k))
    row = pl.BlockSpec((1, d), lambda b, i: (0, 0))
    return pl.pallas_call(
        kernel,
        grid=(bsz, s // ts),
        in_specs=[tile("qa"), tile("ka"), tile("va"), tile("za"),
                  pl.BlockSpec((None, ts, d), lambda b, i: (b, i, cols["small"])),
                  conv(0), conv(1), conv(2), row, row, row],
        out_specs=pl.BlockSpec((None, ts, w), lambda b, i: (b, i, 0)),
        out_shape=jax.ShapeDtypeStruct((bsz, s, w), F32),
        scratch_shapes=[pltpu.VMEM((3, ts + 8, w), F32)]
        + [pltpu.VMEM((ts, w), F32) for _ in range(7)]
        + [pltpu.VMEM((nh, ts, CHUNK), F32), pltpu.VMEM((nh, d, d), F32)],
        compiler_params=pltpu.CompilerParams(
            dimension_semantics=("parallel", "arbitrary"), vmem_limit_bytes=VMEM_LIMIT),
        name="deltanet",
    )(p32, p32, p32, p32, p32, conv_w.astype(F32), conv_w.astype(F32), conv_w.astype(F32),
      pad(a_log), pad(dt_bias), a_norm_g.astype(F32).reshape(1, d))


def _hgrn2_kernel(q_ref, f_ref, i_ref, gate_ref, lb_ref, gn_ref, o_ref,
                  qs_ref, ks_ref, gc_ref, st_ref, *, ts):
    s = pl.program_id(1)
    c = CHUNK
    d = HEAD_DIM
    nh = N_HEADS
    SUB = 16

    @pl.when(s == 0)
    def _():
        st_ref[...] = jnp.zeros_like(st_ref)

    lb = lb_ref[...]
    f_raw = f_ref[...]
    log_sig = jnp.minimum(f_raw, 0.0) - jnp.log1p(jnp.exp(-jnp.abs(f_raw)))
    la = jnp.log(lb)
    lbb = jnp.log1p(-lb) + log_sig
    log_f = jnp.maximum(la, lbb) + jnp.log1p(jnp.exp(-jnp.abs(la - lbb)))
    qs_ref[...] = _silu(q_ref[...])
    ks_ref[...] = (1.0 - lb) * _sigmoid(-f_raw)

    row = _iota((c, c), 0)
    col = _iota((c, c), 1)
    tri_f = (col <= row).astype(F32)
    ones_dd = jnp.ones((d, d), BF16)
    rows_8d = _iota((8, d), 0)
    gnorm = gn_ref[...]

    tri2 = jnp.concatenate([tri_f, tri_f], axis=1).astype(BF16)
    for ci in range(ts // c):
        hi, lo = _split(log_f[ci * c:(ci + 1) * c, :])
        gc_ref[ci * c:(ci + 1) * c, :] = jnp.dot(tri2, jnp.concatenate([hi, lo], axis=0),
                                                 preferred_element_type=F32)

    blocks = [(sb * SUB, (sb + 1) * SUB) for sb in range(c // SUB)]

    def chunk_loop(ci, carry):
        r0 = pl.multiple_of(ci * c, c)
        rows = pl.ds(r0, c)
        hss = [slice(hh * d, (hh + 1) * d) for hh in range(nh)]
        q = [qs_ref[rows, hs] for hs in hss]
        k = [ks_ref[rows, hs] for hs in hss]
        v = [i_ref[rows, hs] for hs in hss]
        gc = [gc_ref[rows, hs] for hs in hss]

        def near_products(q, k, gc):
            prods = []
            for top, end in blocks:
                for j in range(top, end):
                    lo = (j // 8) * 8
                    e = jnp.exp2(gc[lo:end, :] - gc[j:j + 1, :])
                    if j % 8:
                        head = jnp.where(rows_8d >= j - lo, e[:8], 0.0)
                        e = jnp.concatenate([head, e[8:]], axis=0) if lo + 8 < end else head
                    prods.append(q[lo:end, :] * k[j:j + 1, :] * e)
            return jnp.concatenate(prods, axis=0).astype(BF16)

        def far_operands(q, k, gc):
            out = []
            for top, end in blocks[1:]:
                g_b = gc[top - 1:top, :]
                out.append((q[top:end, :] * jnp.exp(gc[top:end, :] - g_b),
                            k[:top, :] * jnp.exp(jnp.minimum(g_b - gc[:top, :], 0.0))))
            return out

        near = [near_products(a, b, g * LOG2E) for a, b, g in zip(q, k, gc)]
        far_ops = [far_operands(*x) for x in zip(q, k, gc)]
        st = [st_ref[hh] for hh in range(nh)]
        gl = [x[c - 1:c, :] for x in gc]
        sums = [jnp.dot(x, ones_dd, preferred_element_type=F32) for x in near]
        qk_far = [[_mm_nt(qe, ke) for qe, ke in ops] for ops in far_ops]
        far = [[_mm(a, vv[:top, :]) for a, (top, _) in zip(qs, blocks[1:])] for qs, vv in zip(qk_far, v)]
        o_st = [_mm_nt(a * jnp.exp(g), s_) for a, g, s_ in zip(q, gc, st)]
        kv = [_mm_tn(vv, kk * jnp.exp(g_l - g)) for vv, kk, g_l, g in zip(v, k, gl, gc)]

        for hh, hs in enumerate(hss):
            groups = [jnp.zeros((8, d), F32) for _ in range(c // 8)]
            at = 0
            for top, end in blocks:
                for j in range(top, end):
                    v_j = v[hh][j:j + 1, :]
                    for g in range(j // 8, end // 8):
                        groups[g] = groups[g] + sums[hh][at:at + 8, :] * v_j
                        at += 8
            for f, (top, end) in zip(far[hh], blocks[1:]):
                for g in range(top // 8, end // 8):
                    groups[g] = groups[g] + f[(g * 8 - top):(g * 8 - top + 8), :]
            o = jnp.concatenate(groups, axis=0) + o_st[hh]
            st_ref[hh] = st[hh] * jnp.exp(gl[hh]) + kv[hh]
            o_ref[rows, hs] = _rms(o, gnorm) * _silu(gate_ref[rows, hs])
        return carry

    lax.fori_loop(0, ts // c, chunk_loop, 0)


def _hgrn2(p32, lb, d_norm_g, *, ts, cols):
    bsz, s, _ = p32.shape
    d = HEAD_DIM
    nh = N_HEADS
    w = nh * d
    kernel = functools.partial(_hgrn2_kernel, ts=ts)
    tile = lambda name: pl.BlockSpec((None, ts, w), lambda b, i: (b, i, cols[name] // nh))
    return pl.pallas_call(
        kernel,
        grid=(bsz, s // ts),
        in_specs=[tile("qd"), tile("fd"), tile("id"), tile("gd"),
                  pl.BlockSpec((1, w), lambda b, i: (0, 0)),
                  pl.BlockSpec((1, d), lambda b, i: (0, 0))],
        out_specs=pl.BlockSpec((None, ts, w), lambda b, i: (b, i, 0)),
        out_shape=jax.ShapeDtypeStruct((bsz, s, w), F32),
        scratch_shapes=[pltpu.VMEM((ts, w), F32), pltpu.VMEM((ts, w), F32),
                        pltpu.VMEM((ts, w), F32), pltpu.VMEM((nh, d, d), F32)],
        compiler_params=pltpu.CompilerParams(
            dimension_semantics=("parallel", "arbitrary"), vmem_limit_bytes=VMEM_LIMIT),
        name="hgrn2",
    )(p32, p32, p32, p32, lb.astype(F32).reshape(1, w), d_norm_g.astype(F32).reshape(1, d))


def _stickbreak_kernel(q_ref, k_ref, v_ref, o_ref, *, tq):
    i = pl.program_id(1)
    d = HEAD_DIM
    nh = N_HEADS
    row = _iota((tq, tq), 0)
    col = _iota((tq, tq), 1)
    causal = col < row
    later = (row > col).astype(BF16)
    later2 = jnp.concatenate([later, later], axis=0)

    heads = [slice(hh * d, (hh + 1) * d) for hh in range(nh)]

    def scores(blocks):
        jobs = [(j, dg, hs) for j, dg in blocks for hs in heads]
        z = [_mm_nt(q_ref[:, hs], k_ref[pl.ds(pl.multiple_of(j * tq, tq), tq), hs]) * (d ** -0.5)
             for j, _, hs in jobs]
        sp = [_softplus(x) for x in z]
        l1m = [jnp.where(causal, -x, 0.0) if dg else -x for x, (_, dg, _) in zip(sp, jobs)]
        rest = [jnp.dot(jnp.concatenate(_split(x), axis=1), later2, preferred_element_type=F32)
                for x in l1m]
        out = [((a - b) + r, l) for a, b, r, l in zip(z, sp, rest, l1m)]
        return [out[b * nh:(b + 1) * nh] for b in range(len(blocks))]

    def block(j, carries):
        (sc,) = scores([(j, False)])
        ps = [jnp.exp(logw + c) for (logw, _), c in zip(sc, carries)]
        pv = [_mm(p, v_ref[pl.ds(pl.multiple_of(j * tq, tq), tq), hs]) for p, hs in zip(ps, heads)]
        for hs, x in zip(heads, pv):
            o_ref[:, hs] += x
        return tuple(c + jnp.sum(l1m, axis=-1, keepdims=True) for (_, l1m), c in zip(sc, carries))

    jp = jnp.maximum(i - 1, 0)
    live = jnp.where(i > 0, 1.0, 0.0)
    sd, sp_ = scores([(i, True), (jp, False)])
    carries = []
    for hh, hs in enumerate(heads):
        c1 = jnp.sum(sd[hh][1], axis=-1, keepdims=True)
        p_d = jnp.where(causal, jnp.exp(sd[hh][0]), 0.0)
        p_p = jnp.exp(sp_[hh][0] + c1) * live
        o_ref[:, hs] = (_mm(p_d, v_ref[pl.ds(pl.multiple_of(i * tq, tq), tq), hs])
                        + _mm(p_p, v_ref[pl.ds(pl.multiple_of(jp * tq, tq), tq), hs]))
        carries.append(c1 + jnp.sum(sp_[hh][1], axis=-1, keepdims=True))
    carries = tuple(carries)

    def cond(c):
        worst = functools.reduce(jnp.maximum, c[1])
        return jnp.logical_and(c[0] >= 0, jnp.max(worst) >= EXP_ZERO_BELOW)

    def body(c):
        return c[0] - 1, block(c[0], c[1])

    lax.while_loop(cond, body, (i - 2, carries))


def _stickbreak(p16, *, tq, cols):
    bsz, s, _ = p16.shape
    nh = N_HEADS
    w = nh * HEAD_DIM
    kernel = functools.partial(_stickbreak_kernel, tq=tq)
    resident = dict(pipeline_mode=pl.Buffered(1))
    return pl.pallas_call(
        kernel,
        grid=(bsz, s // tq),
        in_specs=[pl.BlockSpec((None, tq, w), lambda b, i: (b, i, cols["qc"] // nh)),
                  pl.BlockSpec((None, s, w), lambda b, i: (b, 0, cols["kc"] // nh), **resident),
                  pl.BlockSpec((None, s, w), lambda b, i: (b, 0, cols["vc"] // nh), **resident)],
        out_specs=pl.BlockSpec((None, tq, w), lambda b, i: (b, i, 0)),
        out_shape=jax.ShapeDtypeStruct((bsz, s, w), F32),
        compiler_params=pltpu.CompilerParams(
            dimension_semantics=("parallel", "arbitrary"), vmem_limit_bytes=VMEM_LIMIT),
        name="stickbreak",
    )(p16, p16, p16)


def _dsa_kernel(qi_ref, smq_ref, q_ref, sm_ref, k_ref, vt_ref, bias_ref, o_ref,
                sc_ref, scb_ref, wb_ref, qc_ref, kct_ref, bd_ref, lg_ref, *, tq, k_sel, wi_lane, wide):
    i = pl.program_id(1)
    tk = tq
    d = HEAD_DIM
    nh = N_HEADS
    ksel = float(k_sel)
    per_wide = wide // tk
    n_wide = (i + per_wide) // per_wide
    sub = 2 * tk
    lane_q = _iota((1, tq), 1)

    def tree(parts, op):
        while len(parts) > 1:
            parts = [op(parts[j], parts[j + 1]) if j + 1 < len(parts) else parts[j]
                     for j in range(0, len(parts), 2)]
        return parts[0]

    def col_fold(x, op=jnp.add, rows=8):
        return tree([x[r * rows:(r + 1) * rows] for r in range(x.shape[0] // rows)], op)

    @pl.when(i == 0)
    def _():
        def prep(g, carry):
            g0 = pl.multiple_of(g * wide, wide)
            kt = sm_ref[pl.ds(g0, wide), :].T[:IDX_DIM, :]
            hi, lo = _split(kt)
            kct_ref[:, pl.ds(g0, wide)] = jnp.concatenate([hi, lo, hi], axis=0)
            return carry
        lax.fori_loop(0, sm_ref.shape[0] // wide, prep, 0)

    smq = smq_ref[...]
    lane = _iota(smq.shape, 1)
    for hh in range(IDX_HEADS):
        qh = qi_ref[:, hh * IDX_DIM:(hh + 1) * IDX_DIM]
        hi, lo = _split(qh)
        qc_ref[hh] = jnp.concatenate([hi, hi, lo], axis=-1)
        w = jnp.sum(jnp.where(lane == wi_lane + hh, smq, 0.0), axis=-1, keepdims=True)
        wb_ref[hh] = jnp.broadcast_to(w * ((IDX_HEADS ** -0.5) * (IDX_DIM ** -0.5)), (tq, tk))

    q2t = (q_ref[...] * ((d ** -0.5) * LOG2E)).T.astype(BF16)
    zero_dq = jnp.zeros((d, tq), BF16)
    for p in range(nh // 2):
        top = jnp.concatenate([q2t[2 * p * d:(2 * p + 1) * d], zero_dq], axis=1)
        bot = jnp.concatenate([zero_dq, q2t[(2 * p + 1) * d:(2 * p + 2) * d]], axis=1)
        bd_ref[p] = jnp.concatenate([top, bot], axis=0)

    limit = i * tq + (lane_q // CHUNK + 1) * CHUNK
    rows_t = _iota((tk, tq), 0)

    def score_group(g, mm, masked):
        mn, mx = mm
        for sb in range(wide // sub):
            k0 = pl.multiple_of(g * wide + sb * sub, sub)
            kct = kct_ref[:, pl.ds(k0, sub)]
            tiles = [jnp.zeros((tq, tk), F32) for _ in range(sub // tk)]
            for hh in range(IDX_HEADS):
                s_h = jnp.dot(qc_ref[hh], kct, preferred_element_type=F32)
                for ti in range(sub // tk):
                    tiles[ti] = tiles[ti] + jnp.maximum(s_h[:, ti * tk:(ti + 1) * tk], 0.0) * wb_ref[hh]
            for ti in range(sub // tk):
                kb = pl.multiple_of(k0 + ti * tk, tk)
                sct = tiles[ti].T
                if masked:
                    adm = (kb + rows_t) < limit
                    mn = jnp.minimum(mn, col_fold(jnp.where(adm, sct, jnp.inf), jnp.minimum))
                    sct = jnp.where(adm, sct, -jnp.inf)
                else:
                    mn = jnp.minimum(mn, col_fold(sct, jnp.minimum))
                mx = jnp.maximum(mx, col_fold(sct, jnp.maximum))
                sc_ref[pl.ds(kb, tk), :] = sct
                scb_ref[pl.ds(kb, tk), :] = _floor_bf16(sct)
        return mn, mx

    def score_pair(j, mm):
        return score_group(2 * j + 1, score_group(2 * j, mm, False), False)

    n_full = n_wide - 1
    mm = lax.fori_loop(0, n_full // 2, score_pair,
                       (jnp.full((8, tq), jnp.inf, F32), jnp.full((8, tq), -jnp.inf, F32)))
    mm = lax.cond(n_full % 2 == 1, lambda c: score_group(n_full - 1, c, False), lambda c: c, mm)
    mn, mx = score_group(n_wide - 1, mm, True)

    n_pairs = (n_wide + 1) // 2

    @pl.when(n_wide % 2 == 1)
    def _():
        sc_ref[pl.ds(pl.multiple_of(n_wide * wide, wide), wide), :] = jnp.full((wide, tq), -jnp.inf, F32)
        scb_ref[pl.ds(pl.multiple_of(n_wide * wide, wide), wide), :] = jnp.full((wide, tq), -jnp.inf, BF16)
    rmin = jnp.min(mn, axis=0, keepdims=True)
    rmax = jnp.max(mx, axis=0, keepdims=True)

    def count(pred):
        def body(j, acc):
            for g in (2 * j, 2 * j + 1):
                acc = acc + col_fold(pred(sc_ref[pl.ds(pl.multiple_of(g * wide, wide), wide), :]))
            return acc
        return jnp.sum(lax.fori_loop(0, n_pairs, body, jnp.zeros((8, tq), F32)), axis=0, keepdims=True)

    def max_below(x):
        def body(j, acc):
            for g in (2 * j, 2 * j + 1):
                blk = sc_ref[pl.ds(pl.multiple_of(g * wide, wide), wide), :]
                acc = jnp.maximum(acc, col_fold(jnp.where(blk < x, blk, -jnp.inf), jnp.maximum))
            return acc
        return jnp.max(lax.fori_loop(0, n_pairs, body, jnp.full((8, tq), -jnp.inf, F32)), axis=0, keepdims=True)

    n_adm = limit.astype(F32)
    all_sel = n_adm <= ksel

    def bisect(c):
        lo, hi, c_lo = c
        mid = 0.5 * lo + 0.5 * hi
        cm = count(lambda blk: _ind(blk >= mid))
        ge = cm >= ksel
        return jnp.where(ge, mid, lo), jnp.where(ge, hi, mid), jnp.where(ge, cm, c_lo)

    def pending(c_lo, tied):
        return jnp.where(all_sel, 0.0, jnp.where(tied > 0.5, 0.0, _ind(c_lo != ksel)))

    def bisect_coarse(_, c):
        lo, hi, c_lo = c
        mid = _floor_bf16(0.5 * lo + 0.5 * hi).astype(F32)
        t_b = jnp.broadcast_to(mid, (16, tq)).astype(BF16)
        one_b = jnp.ones((16, tq), BF16)
        zero_b = jnp.zeros((16, tq), BF16)

        def body(j, acc):
            for g in (2 * j, 2 * j + 1):
                blk = scb_ref[pl.ds(pl.multiple_of(g * wide, wide), wide), :]
                ind = [jnp.where(blk[r * 16:(r + 1) * 16] >= t_b, one_b, zero_b) for r in range(wide // 16)]
                acc = acc + tree(ind, jnp.add).astype(F32)
            return acc

        acc = lax.fori_loop(0, n_pairs, body, jnp.zeros((16, tq), F32))
        cm = jnp.sum(acc, axis=0, keepdims=True)
        ge = cm >= ksel
        return jnp.where(ge, mid, lo), jnp.where(ge, hi, mid), jnp.where(ge, cm, c_lo)

    lo0 = _floor_bf16(rmin).astype(F32)
    hi0 = _floor_bf16(rmax + (jnp.abs(rmax) * (2.0 ** -6) + 1e-30)).astype(F32)
    state = lax.fori_loop(0, BISECT_COARSE, bisect_coarse, (lo0, hi0, n_adm))
    state = lax.fori_loop(0, BISECT_FIXED, lambda _, c: bisect(c), state)

    def round_cond(c):
        return jnp.max(pending(c[0][2], c[1])) > 0.5

    def round_body(c):
        st, tied, v, need = c

        def more_cond(s):
            return jnp.logical_and(s[0] < BISECT_EXTRA, jnp.max(pending(s[1][2], tied)) > 0.5)

        _, st = lax.while_loop(more_cond, lambda s: (s[0] + 1, bisect(s[1])), (jnp.int32(0), st))
        pend = pending(st[2], tied)

        def check(_):
            cand = max_below(st[1])
            c_ge = count(lambda blk: _ind(blk >= cand))
            c_gt = count(lambda blk: _ind(blk > cand))
            ok = jnp.where(pend > 0.5, _ind(c_ge >= ksel), 0.0)
            return (jnp.where(ok > 0.5, 1.0, tied), jnp.where(ok > 0.5, cand, v),
                    jnp.where(ok > 0.5, ksel - c_gt, need))

        tied, v, need = lax.cond(jnp.max(pend) > 0.5, check, lambda _: (tied, v, need), 0)
        return st, tied, v, need

    zeros1 = jnp.zeros((1, tq), F32)
    (lo_f, _, _), tied, v_tie, need = lax.while_loop(round_cond, round_body, (state, zeros1, zeros1, zeros1))
    vth = jnp.where(all_sel, F32_LOWEST, jnp.where(tied > 0.5, v_tie, lo_f))

    @pl.when(jnp.max(tied) > 0.5)
    def _():
        v_eq = jnp.where(tied > 0.5, v_tie, jnp.inf)
        incl = (_iota((tk, tk), 1) <= _iota((tk, tk), 0)).astype(BF16)

        def demote(g, seen):
            g0 = pl.multiple_of(g * wide, wide)
            xs = [sc_ref[pl.ds(g0 + pb * tk, tk), :] for pb in range(per_wide)]
            eqs = [_ind(x == v_eq) for x in xs]
            inblk = [jnp.dot(incl, e.astype(BF16), preferred_element_type=F32) for e in eqs]
            for pb in range(per_wide):
                rank = inblk[pb] + seen
                sc_ref[pl.ds(g0 + pb * tk, tk), :] = jnp.where(eqs[pb] * _ind(rank > need) > 0.5,
                                                               -jnp.inf, xs[pb])
                seen = seen + jnp.sum(col_fold(eqs[pb]), axis=0, keepdims=True)
            return seen

        lax.fori_loop(0, n_wide, demote, zeros1)

    g_near = jnp.maximum(i - 1, 0) // per_wide

    def logit_group(g, mx, near):
        out = list(mx)
        for sb in range(wide // sub):
            k0 = pl.multiple_of(g * wide + sb * sub, sub)
            sel = sc_ref[pl.ds(k0, sub), :] >= vth
            for p in range(nh // 2):
                pair = jnp.dot(k_ref[pl.ds(k0, sub), 2 * p * d:(2 * p + 2) * d], bd_ref[p],
                               preferred_element_type=F32)
                for hh in (2 * p, 2 * p + 1):
                    lm = pair[:, (hh - 2 * p) * tq:(hh - 2 * p + 1) * tq]
                    if near:
                        back = [jnp.clip(i - (g * per_wide + sb * (sub // tk) + pb), 0, 2)
                                for pb in range(sub // tk)]
                        lm = lm + jnp.concatenate([bias_ref[bk, hh] for bk in back], axis=0)
                    lm = jnp.where(sel, lm, NEG_BIG)
                    lg_ref[hh, pl.ds(k0, sub), :] = lm
                    out[hh] = jnp.maximum(out[hh], col_fold(lm, jnp.maximum))
        return tuple(out)

    mx = tuple(jnp.full((8, tq), NEG_BIG, F32) for _ in range(nh))
    def logit_pair(j, mx, near):
        return logit_group(2 * j + 1, logit_group(2 * j, mx, near), near)

    far_pairs = g_near // 2
    mx = lax.fori_loop(0, far_pairs, functools.partial(logit_pair, near=False), mx)
    mx = lax.fori_loop(far_pairs, n_pairs, functools.partial(logit_pair, near=True), mx)
    m_q = [jnp.max(mx[hh], axis=0, keepdims=True) for hh in range(nh)]

    ones_rows = jnp.ones((8, wide), BF16)

    def pv_pair(j, carry):
        ls, accs = list(carry[0]), list(carry[1])
        jobs = [(pl.multiple_of(g * wide, wide), hh) for g in (2 * j, 2 * j + 1) for hh in range(nh)]
        ps = [jnp.exp2(lg_ref[hh, pl.ds(g0, wide), :] - m_q[hh]).astype(BF16) for g0, hh in jobs]
        outs = [jnp.dot(jnp.concatenate([vt_ref[hh * d:(hh + 1) * d, pl.ds(g0, wide)], ones_rows], axis=0),
                        p, preferred_element_type=F32) for (g0, hh), p in zip(jobs, ps)]
        for (_, hh), out in zip(jobs, outs):
            ls[hh] = ls[hh] + out[d:]
            accs[hh] = accs[hh] + out[:d]
        return tuple(ls), tuple(accs)

    ls, accs = lax.fori_loop(0, n_pairs, pv_pair,
                             (tuple(jnp.zeros((8, tq), F32) for _ in range(nh)),
                              tuple(jnp.zeros((d, tq), F32) for _ in range(nh))))
    for hh in range(nh):
        o_ref[:, hh * d:(hh + 1) * d] = (accs[hh] / ls[hh][0:1]).T


def _dsa(p32, p16, vt, bias_tiles, *, tq, cols):
    bsz, s, _ = p32.shape
    d = HEAD_DIM
    nh = N_HEADS
    wide = 4 * tq
    k_sel = min(TOPK_MAX, s // 4)
    w512 = nh * d
    kernel = functools.partial(_dsa_kernel, tq=tq, k_sel=k_sel, wi_lane=cols["wi_lane"], wide=wide)
    resident = dict(pipeline_mode=pl.Buffered(1))
    return pl.pallas_call(
        kernel,
        grid=(bsz, s // tq),
        in_specs=[pl.BlockSpec((None, tq, w512), lambda b, i: (b, i, cols["qi"] // nh)),
                  pl.BlockSpec((None, tq, d), lambda b, i: (b, i, cols["small"])),
                  pl.BlockSpec((None, tq, w512), lambda b, i: (b, i, cols["qb"] // nh)),
                  pl.BlockSpec((None, s, d), lambda b, i: (b, 0, cols["small"]), **resident),
                  pl.BlockSpec((None, s, w512), lambda b, i: (b, 0, cols["kb"] // nh), **resident),
                  pl.BlockSpec((w512, s), lambda b, i: (0, b), **resident),
                  pl.BlockSpec((3, nh, tq, tq), lambda b, i: (0, 0, 0, 0), **resident)],
        out_specs=pl.BlockSpec((None, tq, w512), lambda b, i: (b, i, 0)),
        out_shape=jax.ShapeDtypeStruct((bsz, s, w512), F32),
        scratch_shapes=[pltpu.VMEM((s, tq), F32),
                        pltpu.VMEM((s, tq), BF16),
                        pltpu.VMEM((IDX_HEADS, tq, tq), F32),
                        pltpu.VMEM((IDX_HEADS, tq, 3 * IDX_DIM), BF16),
                        pltpu.VMEM((3 * IDX_DIM, s), BF16),
                        pltpu.VMEM((nh // 2, 2 * d, 2 * tq), BF16),
                        pltpu.VMEM((nh, s, tq), F32)],
        compiler_params=pltpu.CompilerParams(
            dimension_semantics=("parallel", "arbitrary"), vmem_limit_bytes=VMEM_LIMIT),
        name="dsa",
    )(p32, p32, p32, p32, p16, vt, bias_tiles)


def _t5_bucket(rel):
    nb = REL_BUCKETS // 2
    max_exact = nb // 2
    ret = jnp.where(rel > 0, nb, 0)
    n = jnp.abs(rel)
    large = max_exact + (jnp.log(jnp.maximum(n, 1).astype(F32) / max_exact)
                         / math.log(REL_MAX_DIST / max_exact) * (nb - max_exact)).astype(jnp.int32)
    large = jnp.minimum(large, nb - 1)
    return ret + jnp.where(n < max_exact, n, large)


def _bias_tiles(rel_table, tq):
    assert tq >= REL_MAX_DIST
    t = jnp.arange(tq)
    back = jnp.arange(3)
    rel = (t[None, None, :] - back[:, None, None] * tq) - t[None, :, None]
    onehot = (_t5_bucket(rel)[..., None] == jnp.arange(REL_BUCKETS)).astype(F32)
    tiles = jnp.einsum("bqkn,nh->bhkq", onehot, rel_table.astype(F32),
                       precision=HIGHEST)
    return (tiles - tiles[2:3]) * LOG2E


def _even_layout(w_in):
    d = HEAD_DIM
    a_w = 2 * N_HEADS * d + N_HEADS * d
    offs = {}
    o = 0
    for name, w in (("qkv", a_w), ("z", N_HEADS * d), ("a", N_HEADS), ("b", N_HEADS),
                    ("qb", N_HEADS * d), ("kb", N_HEADS * d), ("vb", N_HEADS * d),
                    ("qi", IDX_HEADS * IDX_DIM), ("ki", IDX_DIM), ("wi", IDX_HEADS)):
        offs[name] = (o, o + w)
        o += w
    assert o == w_in.shape[1]
    sl = lambda n: w_in[:, offs[n][0]:offs[n][1]]
    small_w = IDX_DIM + 2 * N_HEADS + IDX_HEADS
    small_pad = -small_w % d
    zeros = lambda n: jnp.zeros((w_in.shape[0], n), w_in.dtype)
    w32 = jnp.concatenate([sl("qkv"), sl("z"), sl("qb"), sl("qi"),
                           sl("ki"), sl("a"), sl("b"), sl("wi"), zeros(small_pad)], axis=1)
    n32 = w32.shape[1]
    tn = n32 // 5
    assert tn * 5 == n32 and tn % d == 0
    w16 = jnp.concatenate([sl("kb"), zeros(tn - N_HEADS * d)], axis=1)
    nh = N_HEADS
    cols = dict(qa=0, ka=nh, va=2 * nh, za=3 * nh, qb=4 * nh, qi=5 * nh, small=6 * nh, kb=0,
                a_lane=IDX_DIM, b_lane=IDX_DIM + nh, wi_lane=IDX_DIM + 2 * nh, n32=n32, tn=tn)
    return jnp.concatenate([w32, w16], axis=1).astype(BF16), sl("vb").T.astype(BF16), cols


def kernel(x, norm_g, w_in_even, conv_w_even, a_log_even, dt_bias_even, a_norm_even, w_out_even,
           rel_bias, w_in_odd, lb_logits, d_norm_odd, w_out_odd, w_gate, w_up, w_down):
    bsz, s, d = x.shape
    t = bsz * s
    depth = norm_g.shape[0]
    nh = N_HEADS
    tq = Q_TILE
    lb_all = jnp.cumsum(jax.nn.softmax(lb_logits.astype(F32), axis=0), axis=0)
    lb_all = lb_all - lb_all[:1]
    odd_cols = dict(qc=0, kc=nh, vc=2 * nh, qd=0, fd=nh, id=2 * nh, gd=3 * nh)
    bias_tiles = _bias_tiles(rel_bias, tq)

    h = x.reshape(t, d)
    for l in range(depth):
        if l % 2 == 0:
            e = l // 2
            w_even, w_vt, cols = _even_layout(w_in_even[e])
            p32, p16, vt = _norm_matmul(h, norm_g[l, 0], w_even, tm=PROJ_TILE, tn=cols["tn"], n32=cols["n32"],
                                        w_t=w_vt)
            p32 = p32.reshape(bsz, s, -1)
            p16 = p16.reshape(bsz, s, -1)
            o_1 = _deltanet(p32, conv_w_even[e], a_log_even[e], dt_bias_even[e], a_norm_even[e],
                            ts=min(SEQ_TILE, s), cols=cols)
            o_2 = _dsa(p32, p16, vt, bias_tiles, tq=tq, cols=cols)
            w_out = w_out_even[e]
        else:
            o = l // 2
            n16 = 3 * nh * HEAD_DIM
            w_odd = jnp.concatenate([w_in_odd[o][:, n16:], w_in_odd[o][:, :n16]], axis=1).astype(BF16)
            p32, p16 = _norm_matmul(h, norm_g[l, 0], w_odd, tm=PROJ_TILE, tn=ODD_COL_TILE, n32=w_odd.shape[1] - n16)
            p32 = p32.reshape(bsz, s, -1)
            p16 = p16.reshape(bsz, s, -1)
            o_1 = _stickbreak(p16, tq=tq, cols=odd_cols)
            o_2 = _hgrn2(p32, lb_all[l], d_norm_odd[o], ts=min(SEQ_TILE, s), cols=odd_cols)
            w_out = w_out_odd[o]
        h = _mix_ffn(o_1.reshape(t, -1), o_2.reshape(t, -1), w_out, h, norm_g[l, 1], norm_g[l, 2], norm_g[l, 3],
                     w_gate[l], w_up[l], w_down[l], tm=ROW_TILE, tf=FFN_TILE)
    return h.reshape(bsz, s, d)
```

```python
import functools
import math

import jax
import jax.numpy as jnp
from jax import lax
from jax.experimental import pallas as pl
from jax.experimental.pallas import tpu as pltpu

F32 = jnp.float32
BF16 = jnp.bfloat16
HIGHEST = lax.Precision.HIGHEST

CHUNK = 64
HEAD_DIM = 128
N_HEADS = 4
IDX_HEADS = 8
IDX_DIM = 64
TOPK_MAX = 256
CONV_WIDTH = 4
REL_BUCKETS = 32
REL_MAX_DIST = 128
EPS = 1e-6
NEG_BIG = -1e30
LOG2E = 1.4426950408889634
BISECT_COARSE = 12
BISECT_FIXED = 8
BISECT_EXTRA = 6
F32_LOWEST = -3.4028234663852886e38
EXP_ZERO_BELOW = -104.0
VMEM_LIMIT = 56 * 1024 * 1024

PROJ_TILE = 2048
ROW_TILE = 1024
SEQ_TILE = 512
Q_TILE = 128
ODD_COL_TILE = 512
FFN_TILE = 256


def _mm(a, b):
    return jnp.dot(a.astype(BF16), b.astype(BF16), preferred_element_type=F32)


def _mm_nt(a, b):
    return lax.dot_general(a.astype(BF16), b.astype(BF16), (((1,), (1,)), ((), ())),
                           preferred_element_type=F32)


def _mm_tn(a, b):
    return lax.dot_general(a.astype(BF16), b.astype(BF16), (((0,), (0,)), ((), ())),
                           preferred_element_type=F32)


def _split(x):
    hi = x.astype(BF16)
    return hi, (x - hi.astype(F32)).astype(BF16)


def _floor_bf16(x):
    bits = pltpu.bitcast(x, jnp.int32)
    down = jnp.where(bits >= 0, bits, bits + 0xFFFF) & jnp.int32(-65536)
    return pltpu.bitcast(down, F32).astype(BF16)


def _sigmoid(x):
    return 1.0 / (1.0 + jnp.exp(-x))


def _silu(x):
    return x * _sigmoid(x)


def _softplus(x):
    return jnp.maximum(x, 0.0) + jnp.log1p(jnp.exp(-jnp.abs(x)))


def _rms(x, g):
    return x * lax.rsqrt(jnp.mean(x * x, axis=-1, keepdims=True) + EPS) * g


def _iota(shape, dim):
    return lax.broadcasted_iota(jnp.int32, shape, dim)


def _ind(mask):
    return jnp.where(mask, 1.0, 0.0)


def _norm_matmul_kernel(x_ref, g_ref, w_ref, *rest, n_t, tiles32):
    if n_t:
        wt_ref, o32_ref, o16_ref, ot_ref, xn_ref = rest
    else:
        o32_ref, o16_ref, xn_ref = rest
    j = pl.program_id(1)

    @pl.when(j == 0)
    def _():
        xn_ref[...] = _rms(x_ref[...], g_ref[...]).astype(BF16)
        if n_t:
            ot_ref[...] = lax.dot_general(wt_ref[...], xn_ref[...], (((1,), (1,)), ((), ())),
                                          preferred_element_type=F32).astype(BF16)

    y = jnp.dot(xn_ref[...], w_ref[...], preferred_element_type=F32)

    @pl.when(j < tiles32)
    def _():
        o32_ref[...] = y

    @pl.when(j >= tiles32)
    def _():
        o16_ref[...] = y.astype(BF16)


def _norm_matmul(x, g, w, *, tm, tn, n32, w_t=None):
    t, d = x.shape
    n = w.shape[1]
    n_t = 0 if w_t is None else w_t.shape[0]
    tiles32 = n32 // tn
    assert tiles32 * tn == n32 and (n - n32) % tn == 0 and 0 < n32 < n
    in_specs = [pl.BlockSpec((tm, d), lambda i, j: (i, 0)),
                pl.BlockSpec((1, d), lambda i, j: (0, 0)),
                pl.BlockSpec((d, tn), lambda i, j: (0, j))]
    out_specs = [pl.BlockSpec((tm, tn), lambda i, j: (i, jnp.minimum(j, tiles32 - 1))),
                 pl.BlockSpec((tm, tn), lambda i, j: (i, jnp.maximum(j - tiles32, 0)))]
    out_shape = [jax.ShapeDtypeStruct((t, n32), F32), jax.ShapeDtypeStruct((t, n - n32), BF16)]
    args = [x, g.reshape(1, d), w]
    if n_t:
        in_specs.append(pl.BlockSpec((n_t, d), lambda i, j: (0, 0)))
        out_specs.append(pl.BlockSpec((n_t, tm), lambda i, j: (0, i)))
        out_shape.append(jax.ShapeDtypeStruct((n_t, t), BF16))
        args.append(w_t)
    return pl.pallas_call(
        functools.partial(_norm_matmul_kernel, n_t=n_t, tiles32=tiles32),
        grid=(t // tm, n // tn),
        in_specs=in_specs,
        out_specs=out_specs,
        out_shape=out_shape,
        scratch_shapes=[pltpu.VMEM((tm, d), BF16)],
        compiler_params=pltpu.CompilerParams(
            dimension_semantics=("parallel", "arbitrary"), vmem_limit_bytes=VMEM_LIMIT),
        name="norm_matmul",
    )(*args)


def _mix_ffn_kernel(ca_ref, cb_ref, wa_ref, wb_ref, h_ref, gmix_ref, gpre_ref, gpost_ref,
                    wg_ref, wu_ref, wd_ref, o_ref, h1_ref, xn_ref, acc_ref):
    f = pl.program_id(1)

    @pl.when(f == 0)
    def _():
        y = (jnp.dot(ca_ref[...], wa_ref[...], preferred_element_type=F32)
             + jnp.dot(cb_ref[...], wb_ref[...], preferred_element_type=F32))
        h1 = h_ref[...] + _rms(y, gmix_ref[...])
        h1_ref[...] = h1
        xn_ref[...] = _rms(h1, gpre_ref[...]).astype(BF16)
        acc_ref[...] = jnp.zeros_like(acc_ref)

    xn = xn_ref[...]
    gate = jnp.dot(xn, wg_ref[...], preferred_element_type=F32)
    up = jnp.dot(xn, wu_ref[...], preferred_element_type=F32)
    act = (_silu(gate) * up).astype(BF16)
    acc_ref[...] += jnp.dot(act, wd_ref[...], preferred_element_type=F32)

    @pl.when(f == pl.num_programs(1) - 1)
    def _():
        o_ref[...] = h1_ref[...] + _rms(acc_ref[...], gpost_ref[...])


def _mix_ffn(ca, cb, w_out, h, g_mix, g_pre, g_post, wg, wu, wd, *, tm, tf):
    t, d = h.shape
    ff = wg.shape[1]
    wa_n = ca.shape[1]
    wb_n = cb.shape[1]
    row = pl.BlockSpec((1, d), lambda i, f: (0, 0))
    return pl.pallas_call(
        _mix_ffn_kernel,
        grid=(t // tm, ff // tf),
        in_specs=[pl.BlockSpec((tm, wa_n), lambda i, f: (i, 0)),
                  pl.BlockSpec((tm, wb_n), lambda i, f: (i, 0)),
                  pl.BlockSpec((wa_n, d), lambda i, f: (0, 0)),
                  pl.BlockSpec((wb_n, d), lambda i, f: (0, 0)),
                  pl.BlockSpec((tm, d), lambda i, f: (i, 0)),
                  row, row, row,
                  pl.BlockSpec((d, tf), lambda i, f: (0, f)),
                  pl.BlockSpec((d, tf), lambda i, f: (0, f)),
                  pl.BlockSpec((tf, d), lambda i, f: (f, 0))],
        out_specs=pl.BlockSpec((tm, d), lambda i, f: (i, 0)),
        out_shape=jax.ShapeDtypeStruct((t, d), F32),
        scratch_shapes=[pltpu.VMEM((tm, d), F32), pltpu.VMEM((tm, d), BF16), pltpu.VMEM((tm, d), F32)],
        compiler_params=pltpu.CompilerParams(
            dimension_semantics=("parallel", "arbitrary"), vmem_limit_bytes=VMEM_LIMIT),
        name="mix_ffn",
    )(ca, cb, w_out[:wa_n].astype(BF16), w_out[wa_n:].astype(BF16), h,
      g_mix.reshape(1, d), g_pre.reshape(1, d), g_post.reshape(1, d),
      wg.astype(BF16), wu.astype(BF16), wd.astype(BF16))


def _deltanet_kernel(xq_ref, xk_ref, xv_ref, z_ref, sm_ref, cwq_ref, cwk_ref, cwv_ref,
                     alog_ref, dtb_ref, gn_ref, o_ref,
                     xpad_ref, q_ref, k_ref, v_ref, gb_ref, bb_ref, u_ref, w_ref, qk_ref, st_ref,
                     *, ts, a_col, b_col):
    s = pl.program_id(1)
    c = CHUNK
    d = HEAD_DIM
    nh = N_HEADS

    @pl.when(s == 0)
    def _():
        xpad_ref[:, 0:8, :] = jnp.zeros((3, 8, nh * d), F32)
        st_ref[...] = jnp.zeros_like(st_ref)

    @pl.when(s != 0)
    def _():
        xpad_ref[:, 0:8, :] = xpad_ref[:, ts:ts + 8, :]

    xpad_ref[0, 8:ts + 8, :] = xq_ref[...]
    xpad_ref[1, 8:ts + 8, :] = xk_ref[...]
    xpad_ref[2, 8:ts + 8, :] = xv_ref[...]

    def conv_silu(idx, cw_ref, hs):
        cw = cw_ref[:, hs]
        acc = xpad_ref[idx, 8 - (CONV_WIDTH - 1):8 - (CONV_WIDTH - 1) + ts, hs] * cw[0:1, :]
        for j in range(1, CONV_WIDTH):
            off = 8 - (CONV_WIDTH - 1) + j
            acc = acc + xpad_ref[idx, off:off + ts, hs] * cw[j:j + 1, :]
        return _silu(acc)

    def l2norm(t):
        return t * lax.rsqrt(jnp.sum(t * t, axis=-1, keepdims=True) + EPS)

    row = _iota((c, c), 0)
    col = _iota((c, c), 1)
    tri = (col <= row)
    strict = (col < row)
    tri_f = tri.astype(F32)
    upper_f = (row <= col).astype(F32)
    eye = (row == col).astype(F32)
    gnorm = gn_ref[...]
    chunks = range(ts // c)
    rs = [slice(ci * c, (ci + 1) * c) for ci in chunks]
    tri2 = jnp.concatenate([tri_f, tri_f], axis=1).astype(BF16)
    ones2 = jnp.ones((c, 2 * c), BF16)

    def cum2(lhs2, x):
        hi, lo = _split(x)
        return jnp.dot(lhs2, jnp.concatenate([hi, lo], axis=0), preferred_element_type=F32)

    for hh in range(nh):
        hs = slice(hh * d, (hh + 1) * d)
        q_ref[:, hs] = l2norm(conv_silu(0, cwq_ref, hs)) * (d ** -0.5)
        k_ref[:, hs] = l2norm(conv_silu(1, cwk_ref, hs))
        v_ref[:, hs] = conv_silu(2, cwv_ref, hs)

        a_raw = sm_ref[:, a_col + hh:a_col + hh + 1]
        b_raw = sm_ref[:, b_col + hh:b_col + hh + 1]
        g = -jnp.exp(alog_ref[:, hh:hh + 1]) * _softplus(a_raw + dtb_ref[:, hh:hh + 1])
        gb_ref[:, hs] = jnp.broadcast_to(g, (ts, d))
        bb_ref[:, hs] = jnp.broadcast_to(_sigmoid(b_raw), (ts, d))

        q = [q_ref[r, hs] for r in rs]
        k = [k_ref[r, hs] for r in rs]
        beta = [bb_ref[r, hs] for r in rs]
        gb = [gb_ref[r, hs] for r in rs]
        gc = [cum2(tri2, x) for x in gb]
        gc_row = [cum2(ones2, x[:, :c] * upper_f) for x in gb]
        decay = [jnp.where(tri, jnp.exp(jnp.minimum(a[:, :c] - b, 0.0)), 0.0) for a, b in zip(gc, gc_row)]
        kk = [_mm_nt(x, x) for x in k]
        n = [-jnp.where(strict, b[:, :c] * x * dc, 0.0) for b, x, dc in zip(beta, kk, decay)]
        inv = [eye + x for x in n]
        for step in range(5):
            nb = [x.astype(BF16) for x in n]
            n = [jnp.dot(x, x, preferred_element_type=F32) for x in nb]
            inv = [iv + _mm(iv, x) for iv, x in zip(inv, n)]
        egc = [jnp.exp(x) for x in gc]
        gl = [x[c - 1:c, :] for x in gc]
        inv_l = [x.astype(BF16) for x in inv]
        u = [_mm(a, v_ref[r, hs] * b) for a, r, b in zip(inv_l, rs, beta)]
        w = [_mm(a, x * (b * e)) for a, x, b, e in zip(inv_l, k, beta, egc)]
        qk = [_mm_nt(a, b) * dc for a, b, dc in zip(q, k, decay)]
        for ci in chunks:
            r = rs[ci]
            u_ref[r, hs] = u[ci]
            w_ref[r, hs] = w[ci]
            qk_ref[hh, r, :] = qk[ci]
            q_ref[r, hs] = q[ci] * egc[ci]
            k_ref[r, hs] = k[ci] * jnp.exp(gl[ci] - gc[ci])
            gb_ref[r, hs] = jnp.broadcast_to(jnp.exp(gl[ci]), (c, d))

    def chunk_body(ci, carry):
        r0 = pl.multiple_of(ci * c, c)
        rows = pl.ds(r0, c)
        hss = [slice(hh * d, (hh + 1) * d) for hh in range(nh)]
        st = [st_ref[hh] for hh in range(nh)]
        w_st = [_mm(w_ref[rows, hs], s_) for hs, s_ in zip(hss, st)]
        q_st = [_mm(q_ref[rows, hs], s_) for hs, s_ in zip(hss, st)]
        v_new = [u_ref[rows, hs] - x for hs, x in zip(hss, w_st)]
        o = [a + _mm(qk_ref[hh, rows, :], v) for hh, (a, v) in enumerate(zip(q_st, v_new))]
        kv = [_mm_tn(k_ref[rows, hs], v) for hs, v in zip(hss, v_new)]
        for hh, hs in enumerate(hss):
            st_ref[hh] = st[hh] * gb_ref[pl.ds(r0, 1), hs] + kv[hh]
            o_ref[rows, hs] = (_rms(o[hh], gnorm) * _silu(z_ref[rows, hs])).astype(o_ref.dtype)
        return carry

    lax.fori_loop(0, ts // c, chunk_body, 0)


def _deltanet(p32, conv_w, a_log, dt_bias, a_norm_g, *, ts, cols):
    bsz, s, _ = p32.shape
    d = HEAD_DIM
    nh = N_HEADS
    w = nh * d
    pad = lambda t: jnp.pad(t.astype(F32), (0, d - t.shape[0])).reshape(1, d)
    kernel = functools.partial(_deltanet_kernel, ts=ts, a_col=cols["a_lane"], b_col=cols["b_lane"])
    tile = lambda name: pl.BlockSpec((None, ts, w), lambda b, i: (b, i, cols[name] // nh))
    conv = lambda k: pl.BlockSpec((CONV_WIDTH, w), lambda b, i: (0, k))
    row = pl.BlockSpec((1, d), lambda b, i: (0, 0))
    return pl.pallas_call(
        kernel,
        grid=(bsz, s // ts),
        in_specs=[tile("qa"), tile("ka"), tile("va"), tile("za"),
                  pl.BlockSpec((None, ts, d), lambda b, i: (b, i, cols["small"])),
                  conv(0), conv(1), conv(2), row, row, row],
        out_specs=pl.BlockSpec((None, ts, w), lambda b, i: (b, i, 0)),
        out_shape=jax.ShapeDtypeStruct((bsz, s, w), BF16),
        scratch_shapes=[pltpu.VMEM((3, ts + 8, w), F32)]
        + [pltpu.VMEM((ts, w), F32) for _ in range(7)]
        + [pltpu.VMEM((nh, ts, CHUNK), F32), pltpu.VMEM((nh, d, d), F32)],
        compiler_params=pltpu.CompilerParams(
            dimension_semantics=("parallel", "arbitrary"), vmem_limit_bytes=VMEM_LIMIT),
        name="deltanet",
    )(p32, p32, p32, p32, p32, conv_w.astype(F32), conv_w.astype(F32), conv_w.astype(F32),
      pad(a_log), pad(dt_bias), a_norm_g.astype(F32).reshape(1, d))


def _hgrn2_kernel(q_ref, f_ref, i_ref, gate_ref, lb_ref, gn_ref, o_ref,
                  qs_ref, ks_ref, gc_ref, st_ref, *, ts):
    s = pl.program_id(1)
    c = CHUNK
    d = HEAD_DIM
    nh = N_HEADS
    SUB = 16

    @pl.when(s == 0)
    def _():
        st_ref[...] = jnp.zeros_like(st_ref)

    lb = lb_ref[...]
    f_raw = f_ref[...]
    log_sig = jnp.minimum(f_raw, 0.0) - jnp.log1p(jnp.exp(-jnp.abs(f_raw)))
    la = jnp.log(lb)
    lbb = jnp.log1p(-lb) + log_sig
    log_f = jnp.maximum(la, lbb) + jnp.log1p(jnp.exp(-jnp.abs(la - lbb)))
    qs_ref[...] = _silu(q_ref[...])
    ks_ref[...] = (1.0 - lb) * _sigmoid(-f_raw)

    row = _iota((c, c), 0)
    col = _iota((c, c), 1)
    tri_f = (col <= row).astype(F32)
    ones_dd = jnp.ones((d, d), BF16)
    rows_8d = _iota((8, d), 0)
    gnorm = gn_ref[...]

    tri2 = jnp.concatenate([tri_f, tri_f], axis=1).astype(BF16)
    for ci in range(ts // c):
        hi, lo = _split(log_f[ci * c:(ci + 1) * c, :])
        gc_ref[ci * c:(ci + 1) * c, :] = jnp.dot(tri2, jnp.concatenate([hi, lo], axis=0),
                                                 preferred_element_type=F32)

    blocks = [(sb * SUB, (sb + 1) * SUB) for sb in range(c // SUB)]

    def chunk_loop(ci, carry):
        r0 = pl.multiple_of(ci * c, c)
        rows = pl.ds(r0, c)
        hss = [slice(hh * d, (hh + 1) * d) for hh in range(nh)]
        q = [qs_ref[rows, hs] for hs in hss]
        k = [ks_ref[rows, hs] for hs in hss]
        v = [i_ref[rows, hs] for hs in hss]
        gc = [gc_ref[rows, hs] for hs in hss]

        def near_products(q, k, gc):
            prods = []
            for top, end in blocks:
                for j in range(top, end):
                    lo = (j // 8) * 8
                    e = jnp.exp2(gc[lo:end, :] - gc[j:j + 1, :])
                    if j % 8:
                        head = jnp.where(rows_8d >= j - lo, e[:8], 0.0)
                        e = jnp.concatenate([head, e[8:]], axis=0) if lo + 8 < end else head
                    prods.append(q[lo:end, :] * k[j:j + 1, :] * e)
            return jnp.concatenate(prods, axis=0).astype(BF16)

        def far_operands(q, k, gc):
            out = []
            for top, end in blocks[1:]:
                g_b = gc[top - 1:top, :]
                out.append((q[top:end, :] * jnp.exp(gc[top:end, :] - g_b),
                            k[:top, :] * jnp.exp(jnp.minimum(g_b - gc[:top, :], 0.0))))
            return out

        near = [near_products(a, b, g * LOG2E) for a, b, g in zip(q, k, gc)]
        far_ops = [far_operands(*x) for x in zip(q, k, gc)]
        st = [st_ref[hh] for hh in range(nh)]
        gl = [x[c - 1:c, :] for x in gc]
        sums = [jnp.dot(x, ones_dd, preferred_element_type=F32) for x in near]
        qk_far = [[_mm_nt(qe, ke) for qe, ke in ops] for ops in far_ops]
        far = [[_mm(a, vv[:top, :]) for a, (top, _) in zip(qs, blocks[1:])] for qs, vv in zip(qk_far, v)]
        o_st = [_mm_nt(a * jnp.exp(g), s_) for a, g, s_ in zip(q, gc, st)]
        kv = [_mm_tn(vv, kk * jnp.exp(g_l - g)) for vv, kk, g_l, g in zip(v, k, gl, gc)]

        for hh, hs in enumerate(hss):
            groups = [jnp.zeros((8, d), F32) for _ in range(c // 8)]
            at = 0
            for top, end in blocks:
                for j in range(top, end):
                    v_j = v[hh][j:j + 1, :]
                    for g in range(j // 8, end // 8):
                        groups[g] = groups[g] + sums[hh][at:at + 8, :] * v_j
                        at += 8
            for f, (top, end) in zip(far[hh], blocks[1:]):
                for g in range(top // 8, end // 8):
                    groups[g] = groups[g] + f[(g * 8 - top):(g * 8 - top + 8), :]
            o = jnp.concatenate(groups, axis=0) + o_st[hh]
            st_ref[hh] = st[hh] * jnp.exp(gl[hh]) + kv[hh]
            o_ref[rows, hs] = (_rms(o, gnorm) * _silu(gate_ref[rows, hs])).astype(o_ref.dtype)
        return carry

    lax.fori_loop(0, ts // c, chunk_loop, 0)


def _hgrn2(p32, lb, d_norm_g, *, ts, cols):
    bsz, s, _ = p32.shape
    d = HEAD_DIM
    nh = N_HEADS
    w = nh * d
    kernel = functools.partial(_hgrn2_kernel, ts=ts)
    tile = lambda name: pl.BlockSpec((None, ts, w), lambda b, i: (b, i, cols[name] // nh))
    return pl.pallas_call(
        kernel,
        grid=(bsz, s // ts),
        in_specs=[tile("qd"), tile("fd"), tile("id"), tile("gd"),
                  pl.BlockSpec((1, w), lambda b, i: (0, 0)),
                  pl.BlockSpec((1, d), lambda b, i: (0, 0))],
        out_specs=pl.BlockSpec((None, ts, w), lambda b, i: (b, i, 0)),
        out_shape=jax.ShapeDtypeStruct((bsz, s, w), BF16),
        scratch_shapes=[pltpu.VMEM((ts, w), F32), pltpu.VMEM((ts, w), F32),
                        pltpu.VMEM((ts, w), F32), pltpu.VMEM((nh, d, d), F32)],
        compiler_params=pltpu.CompilerParams(
            dimension_semantics=("parallel", "arbitrary"), vmem_limit_bytes=VMEM_LIMIT),
        name="hgrn2",
    )(p32, p32, p32, p32, lb.astype(F32).reshape(1, w), d_norm_g.astype(F32).reshape(1, d))


def _stickbreak_kernel(q_ref, k_ref, v_ref, o_ref, acc_ref, *, tq):
    i = pl.program_id(1)
    d = HEAD_DIM
    nh = N_HEADS
    row = _iota((tq, tq), 0)
    col = _iota((tq, tq), 1)
    causal = col < row
    later = (row > col).astype(BF16)
    later2 = jnp.concatenate([later, later], axis=0)

    heads = [slice(hh * d, (hh + 1) * d) for hh in range(nh)]

    def scores(blocks):
        jobs = [(j, dg, hs) for j, dg in blocks for hs in heads]
        z = [_mm_nt(q_ref[:, hs], k_ref[pl.ds(pl.multiple_of(j * tq, tq), tq), hs]) * (d ** -0.5)
             for j, _, hs in jobs]
        sp = [_softplus(x) for x in z]
        l1m = [jnp.where(causal, -x, 0.0) if dg else -x for x, (_, dg, _) in zip(sp, jobs)]
        rest = [jnp.dot(jnp.concatenate(_split(x), axis=1), later2, preferred_element_type=F32)
                for x in l1m]
        out = [((a - b) + r, l) for a, b, r, l in zip(z, sp, rest, l1m)]
        return [out[b * nh:(b + 1) * nh] for b in range(len(blocks))]

    def block(j, carries):
        (sc,) = scores([(j, False)])
        ps = [jnp.exp(logw + c) for (logw, _), c in zip(sc, carries)]
        pv = [_mm(p, v_ref[pl.ds(pl.multiple_of(j * tq, tq), tq), hs]) for p, hs in zip(ps, heads)]
        for hs, x in zip(heads, pv):
            acc_ref[:, hs] += x
        return tuple(c + jnp.sum(l1m, axis=-1, keepdims=True) for (_, l1m), c in zip(sc, carries))

    jp = jnp.maximum(i - 1, 0)
    live = jnp.where(i > 0, 1.0, 0.0)
    sd, sp_ = scores([(i, True), (jp, False)])
    carries = []
    for hh, hs in enumerate(heads):
        c1 = jnp.sum(sd[hh][1], axis=-1, keepdims=True)
        p_d = jnp.where(causal, jnp.exp(sd[hh][0]), 0.0)
        p_p = jnp.exp(sp_[hh][0] + c1) * live
        acc_ref[:, hs] = (_mm(p_d, v_ref[pl.ds(pl.multiple_of(i * tq, tq), tq), hs])
                          + _mm(p_p, v_ref[pl.ds(pl.multiple_of(jp * tq, tq), tq), hs]))
        carries.append(c1 + jnp.sum(sp_[hh][1], axis=-1, keepdims=True))
    carries = tuple(carries)

    def cond(c):
        worst = functools.reduce(jnp.maximum, c[1])
        return jnp.logical_and(c[0] >= 0, jnp.max(worst) >= EXP_ZERO_BELOW)

    def body(c):
        return c[0] - 1, block(c[0], c[1])

    lax.while_loop(cond, body, (i - 2, carries))
    o_ref[...] = acc_ref[...].astype(o_ref.dtype)


def _stickbreak(p16, *, tq, cols):
    bsz, s, _ = p16.shape
    nh = N_HEADS
    w = nh * HEAD_DIM
    kernel = functools.partial(_stickbreak_kernel, tq=tq)
    resident = dict(pipeline_mode=pl.Buffered(1))
    return pl.pallas_call(
        kernel,
        grid=(bsz, s // tq),
        in_specs=[pl.BlockSpec((None, tq, w), lambda b, i: (b, i, cols["qc"] // nh)),
                  pl.BlockSpec((None, s, w), lambda b, i: (b, 0, cols["kc"] // nh), **resident),
                  pl.BlockSpec((None, s, w), lambda b, i: (b, 0, cols["vc"] // nh), **resident)],
        out_specs=pl.BlockSpec((None, tq, w), lambda b, i: (b, i, 0)),
        out_shape=jax.ShapeDtypeStruct((bsz, s, w), BF16),
        scratch_shapes=[pltpu.VMEM((tq, w), F32)],
        compiler_params=pltpu.CompilerParams(
            dimension_semantics=("parallel", "arbitrary"), vmem_limit_bytes=VMEM_LIMIT),
        name="stickbreak",
    )(p16, p16, p16)


def _dsa_kernel(qi_ref, smq_ref, q_ref, sm_ref, k_ref, vt_ref, bias_ref, o_ref,
                sc_ref, scb_ref, wb_ref, qc_ref, kct_ref, bd_ref, lg_ref, *, tq, k_sel, wi_lane, wide):
    i = pl.program_id(1)
    tk = tq
    d = HEAD_DIM
    nh = N_HEADS
    ksel = float(k_sel)
    per_wide = wide // tk
    n_wide = (i + per_wide) // per_wide
    sub = 2 * tk
    lane_q = _iota((1, tq), 1)

    def tree(parts, op):
        while len(parts) > 1:
            parts = [op(parts[j], parts[j + 1]) if j + 1 < len(parts) else parts[j]
                     for j in range(0, len(parts), 2)]
        return parts[0]

    def col_fold(x, op=jnp.add, rows=8):
        return tree([x[r * rows:(r + 1) * rows] for r in range(x.shape[0] // rows)], op)

    @pl.when(i == 0)
    def _():
        def prep(g, carry):
            g0 = pl.multiple_of(g * wide, wide)
            kt = sm_ref[pl.ds(g0, wide), :].T[:IDX_DIM, :]
            hi, lo = _split(kt)
            kct_ref[:, pl.ds(g0, wide)] = jnp.concatenate([hi, lo, hi], axis=0)
            return carry
        lax.fori_loop(0, sm_ref.shape[0] // wide, prep, 0)

    smq = smq_ref[...]
    lane = _iota(smq.shape, 1)
    for hh in range(IDX_HEADS):
        qh = qi_ref[:, hh * IDX_DIM:(hh + 1) * IDX_DIM]
        hi, lo = _split(qh)
        qc_ref[hh] = jnp.concatenate([hi, hi, lo], axis=-1)
        w = jnp.sum(jnp.where(lane == wi_lane + hh, smq, 0.0), axis=-1, keepdims=True)
        wb_ref[hh] = jnp.broadcast_to(w * ((IDX_HEADS ** -0.5) * (IDX_DIM ** -0.5)), (tq, tk))

    q2t = (q_ref[...] * ((d ** -0.5) * LOG2E)).T.astype(BF16)
    zero_dq = jnp.zeros((d, tq), BF16)
    for p in range(nh // 2):
        top = jnp.concatenate([q2t[2 * p * d:(2 * p + 1) * d], zero_dq], axis=1)
        bot = jnp.concatenate([zero_dq, q2t[(2 * p + 1) * d:(2 * p + 2) * d]], axis=1)
        bd_ref[p] = jnp.concatenate([top, bot], axis=0)

    limit = i * tq + (lane_q // CHUNK + 1) * CHUNK
    rows_t = _iota((tk, tq), 0)

    def score_group(g, mm, masked):
        mn, mx = mm
        for sb in range(wide // sub):
            k0 = pl.multiple_of(g * wide + sb * sub, sub)
            kct = kct_ref[:, pl.ds(k0, sub)]
            tiles = [jnp.zeros((tq, tk), F32) for _ in range(sub // tk)]
            for hh in range(IDX_HEADS):
                s_h = jnp.dot(qc_ref[hh], kct, preferred_element_type=F32)
                for ti in range(sub // tk):
                    tiles[ti] = tiles[ti] + jnp.maximum(s_h[:, ti * tk:(ti + 1) * tk], 0.0) * wb_ref[hh]
            for ti in range(sub // tk):
                kb = pl.multiple_of(k0 + ti * tk, tk)
                sct = tiles[ti].T
                if masked:
                    adm = (kb + rows_t) < limit
                    mn = jnp.minimum(mn, col_fold(jnp.where(adm, sct, jnp.inf), jnp.minimum))
                    sct = jnp.where(adm, sct, -jnp.inf)
                else:
                    mn = jnp.minimum(mn, col_fold(sct, jnp.minimum))
                mx = jnp.maximum(mx, col_fold(sct, jnp.maximum))
                sc_ref[pl.ds(kb, tk), :] = sct
                scb_ref[pl.ds(kb, tk), :] = _floor_bf16(sct)
        return mn, mx

    def score_pair(j, mm):
        return score_group(2 * j + 1, score_group(2 * j, mm, False), False)

    n_full = n_wide - 1
    mm = lax.fori_loop(0, n_full // 2, score_pair,
                       (jnp.full((8, tq), jnp.inf, F32), jnp.full((8, tq), -jnp.inf, F32)))
    mm = lax.cond(n_full % 2 == 1, lambda c: score_group(n_full - 1, c, False), lambda c: c, mm)
    mn, mx = score_group(n_wide - 1, mm, True)

    n_pairs = (n_wide + 1) // 2

    @pl.when(n_wide % 2 == 1)
    def _():
        sc_ref[pl.ds(pl.multiple_of(n_wide * wide, wide), wide), :] = jnp.full((wide, tq), -jnp.inf, F32)
        scb_ref[pl.ds(pl.multiple_of(n_wide * wide, wide), wide), :] = jnp.full((wide, tq), -jnp.inf, BF16)
    rmin = jnp.min(mn, axis=0, keepdims=True)
    rmax = jnp.max(mx, axis=0, keepdims=True)

    def count(pred):
        def body(j, acc):
            for g in (2 * j, 2 * j + 1):
                acc = acc + col_fold(pred(sc_ref[pl.ds(pl.multiple_of(g * wide, wide), wide), :]))
            return acc
        return jnp.sum(lax.fori_loop(0, n_pairs, body, jnp.zeros((8, tq), F32)), axis=0, keepdims=True)

    def max_below(x):
        def body(j, acc):
            for g in (2 * j, 2 * j + 1):
                blk = sc_ref[pl.ds(pl.multiple_of(g * wide, wide), wide), :]
                acc = jnp.maximum(acc, col_fold(jnp.where(blk < x, blk, -jnp.inf), jnp.maximum))
            return acc
        return jnp.max(lax.fori_loop(0, n_pairs, body, jnp.full((8, tq), -jnp.inf, F32)), axis=0, keepdims=True)

    n_adm = limit.astype(F32)
    all_sel = n_adm <= ksel

    def bisect(c):
        lo, hi, c_lo = c
        mid = 0.5 * lo + 0.5 * hi
        cm = count(lambda blk: _ind(blk >= mid))
        ge = cm >= ksel
        return jnp.where(ge, mid, lo), jnp.where(ge, hi, mid), jnp.where(ge, cm, c_lo)

    def pending(c_lo, tied):
        return jnp.where(all_sel, 0.0, jnp.where(tied > 0.5, 0.0, _ind(c_lo != ksel)))

    def bisect_coarse(_, c):
        lo, hi, c_lo = c
        mid = _floor_bf16(0.5 * lo + 0.5 * hi).astype(F32)
        t_b = jnp.broadcast_to(mid, (16, tq)).astype(BF16)
        one_b = jnp.ones((16, tq), BF16)
        zero_b = jnp.zeros((16, tq), BF16)

        def body(j, acc):
            for g in (2 * j, 2 * j + 1):
                blk = scb_ref[pl.ds(pl.multiple_of(g * wide, wide), wide), :]
                ind = [jnp.where(blk[r * 16:(r + 1) * 16] >= t_b, one_b, zero_b) for r in range(wide // 16)]
                acc = acc + tree(ind, jnp.add).astype(F32)
            return acc

        acc = lax.fori_loop(0, n_pairs, body, jnp.zeros((16, tq), F32))
        cm = jnp.sum(acc, axis=0, keepdims=True)
        ge = cm >= ksel
        return jnp.where(ge, mid, lo), jnp.where(ge, hi, mid), jnp.where(ge, cm, c_lo)

    lo0 = _floor_bf16(rmin).astype(F32)
    hi0 = _floor_bf16(rmax + (jnp.abs(rmax) * (2.0 ** -6) + 1e-30)).astype(F32)
    state = lax.fori_loop(0, BISECT_COARSE, bisect_coarse, (lo0, hi0, n_adm))
    state = lax.fori_loop(0, BISECT_FIXED, lambda _, c: bisect(c), state)

    def round_cond(c):
        return jnp.max(pending(c[0][2], c[1])) > 0.5

    def round_body(c):
        st, tied, v, need = c

        def more_cond(s):
            return jnp.logical_and(s[0] < BISECT_EXTRA, jnp.max(pending(s[1][2], tied)) > 0.5)

        _, st = lax.while_loop(more_cond, lambda s: (s[0] + 1, bisect(s[1])), (jnp.int32(0), st))
        pend = pending(st[2], tied)

        def check(_):
            cand = max_below(st[1])
            c_ge = count(lambda blk: _ind(blk >= cand))
            c_gt = count(lambda blk: _ind(blk > cand))
            ok = jnp.where(pend > 0.5, _ind(c_ge >= ksel), 0.0)
            return (jnp.where(ok > 0.5, 1.0, tied), jnp.where(ok > 0.5, cand, v),
                    jnp.where(ok > 0.5, ksel - c_gt, need))

        tied, v, need = lax.cond(jnp.max(pend) > 0.5, check, lambda _: (tied, v, need), 0)
        return st, tied, v, need

    zeros1 = jnp.zeros((1, tq), F32)
    (lo_f, _, _), tied, v_tie, need = lax.while_loop(round_cond, round_body, (state, zeros1, zeros1, zeros1))
    vth = jnp.where(all_sel, F32_LOWEST, jnp.where(tied > 0.5, v_tie, lo_f))

    @pl.when(jnp.max(tied) > 0.5)
    def _():
        v_eq = jnp.where(tied > 0.5, v_tie, jnp.inf)
        incl = (_iota((tk, tk), 1) <= _iota((tk, tk), 0)).astype(BF16)

        def demote(g, seen):
            g0 = pl.multiple_of(g * wide, wide)
            xs = [sc_ref[pl.ds(g0 + pb * tk, tk), :] for pb in range(per_wide)]
            eqs = [_ind(x == v_eq) for x in xs]
            inblk = [jnp.dot(incl, e.astype(BF16), preferred_element_type=F32) for e in eqs]
            for pb in range(per_wide):
                rank = inblk[pb] + seen
                sc_ref[pl.ds(g0 + pb * tk, tk), :] = jnp.where(eqs[pb] * _ind(rank > need) > 0.5,
                                                               -jnp.inf, xs[pb])
                seen = seen + jnp.sum(col_fold(eqs[pb]), axis=0, keepdims=True)
            return seen

        lax.fori_loop(0, n_wide, demote, zeros1)

    g_near = jnp.maximum(i - 1, 0) // per_wide

    def logit_group(g, mx, near):
        out = list(mx)
        for sb in range(wide // sub):
            k0 = pl.multiple_of(g * wide + sb * sub, sub)
            sel = sc_ref[pl.ds(k0, sub), :] >= vth
            for p in range(nh // 2):
                pair = jnp.dot(k_ref[pl.ds(k0, sub), 2 * p * d:(2 * p + 2) * d], bd_ref[p],
                               preferred_element_type=F32)
                for hh in (2 * p, 2 * p + 1):
                    lm = pair[:, (hh - 2 * p) * tq:(hh - 2 * p + 1) * tq]
                    if near:
                        back = [jnp.clip(i - (g * per_wide + sb * (sub // tk) + pb), 0, 2)
                                for pb in range(sub // tk)]
                        lm = lm + jnp.concatenate([bias_ref[bk, hh] for bk in back], axis=0)
                    lm = jnp.where(sel, lm, NEG_BIG)
                    lg_ref[hh, pl.ds(k0, sub), :] = lm
                    out[hh] = jnp.maximum(out[hh], col_fold(lm, jnp.maximum))
        return tuple(out)

    mx = tuple(jnp.full((8, tq), NEG_BIG, F32) for _ in range(nh))
    def logit_pair(j, mx, near):
        return logit_group(2 * j + 1, logit_group(2 * j, mx, near), near)

    far_pairs = g_near // 2
    mx = lax.fori_loop(0, far_pairs, functools.partial(logit_pair, near=False), mx)
    mx = lax.fori_loop(far_pairs, n_pairs, functools.partial(logit_pair, near=True), mx)
    m_q = [jnp.max(mx[hh], axis=0, keepdims=True) for hh in range(nh)]

    ones_rows = jnp.ones((8, wide), BF16)

    def pv_pair(j, carry):
        ls, accs = list(carry[0]), list(carry[1])
        jobs = [(pl.multiple_of(g * wide, wide), hh) for g in (2 * j, 2 * j + 1) for hh in range(nh)]
        ps = [jnp.exp2(lg_ref[hh, pl.ds(g0, wide), :] - m_q[hh]).astype(BF16) for g0, hh in jobs]
        outs = [jnp.dot(jnp.concatenate([vt_ref[hh * d:(hh + 1) * d, pl.ds(g0, wide)], ones_rows], axis=0),
                        p, preferred_element_type=F32) for (g0, hh), p in zip(jobs, ps)]
        for (_, hh), out in zip(jobs, outs):
            ls[hh] = ls[hh] + out[d:]
            accs[hh] = accs[hh] + out[:d]
        return tuple(ls), tuple(accs)

    ls, accs = lax.fori_loop(0, n_pairs, pv_pair,
                             (tuple(jnp.zeros((8, tq), F32) for _ in range(nh)),
                              tuple(jnp.zeros((d, tq), F32) for _ in range(nh))))
    for hh in range(nh):
        o_ref[:, hh * d:(hh + 1) * d] = (accs[hh] / ls[hh][0:1]).T.astype(o_ref.dtype)


def _dsa(p32, p16, vt, bias_tiles, *, tq, cols):
    bsz, s, _ = p32.shape
    d = HEAD_DIM
    nh = N_HEADS
    wide = 4 * tq
    k_sel = min(TOPK_MAX, s // 4)
    w512 = nh * d
    kernel = functools.partial(_dsa_kernel, tq=tq, k_sel=k_sel, wi_lane=cols["wi_lane"], wide=wide)
    resident = dict(pipeline_mode=pl.Buffered(1))
    return pl.pallas_call(
        kernel,
        grid=(bsz, s // tq),
        in_specs=[pl.BlockSpec((None, tq, w512), lambda b, i: (b, i, cols["qi"] // nh)),
                  pl.BlockSpec((None, tq, d), lambda b, i: (b, i, cols["small"])),
                  pl.BlockSpec((None, tq, w512), lambda b, i: (b, i, cols["qb"] // nh)),
                  pl.BlockSpec((None, s, d), lambda b, i: (b, 0, cols["small"]), **resident),
                  pl.BlockSpec((None, s, w512), lambda b, i: (b, 0, cols["kb"] // nh), **resident),
                  pl.BlockSpec((w512, s), lambda b, i: (0, b), **resident),
                  pl.BlockSpec((3, nh, tq, tq), lambda b, i: (0, 0, 0, 0), **resident)],
        out_specs=pl.BlockSpec((None, tq, w512), lambda b, i: (b, i, 0)),
        out_shape=jax.ShapeDtypeStruct((bsz, s, w512), BF16),
        scratch_shapes=[pltpu.VMEM((s, tq), F32),
                        pltpu.VMEM((s, tq), BF16),
                        pltpu.VMEM((IDX_HEADS, tq, tq), F32),
                        pltpu.VMEM((IDX_HEADS, tq, 3 * IDX_DIM), BF16),
                        pltpu.VMEM((3 * IDX_DIM, s), BF16),
                        pltpu.VMEM((nh // 2, 2 * d, 2 * tq), BF16),
                        pltpu.VMEM((nh, s, tq), F32)],
        compiler_params=pltpu.CompilerParams(
            dimension_semantics=("parallel", "arbitrary"), vmem_limit_bytes=VMEM_LIMIT),
        name="dsa",
    )(p32, p32, p32, p32, p16, vt, bias_tiles)


def _t5_bucket(rel):
    nb = REL_BUCKETS // 2
    max_exact = nb // 2
    ret = jnp.where(rel > 0, nb, 0)
    n = jnp.abs(rel)
    large = max_exact + (jnp.log(jnp.maximum(n, 1).astype(F32) / max_exact)
                         / math.log(REL_MAX_DIST / max_exact) * (nb - max_exact)).astype(jnp.int32)
    large = jnp.minimum(large, nb - 1)
    return ret + jnp.where(n < max_exact, n, large)


def _bias_tiles(rel_table, tq):
    assert tq >= REL_MAX_DIST
    t = jnp.arange(tq)
    back = jnp.arange(3)
    rel = (t[None, None, :] - back[:, None, None] * tq) - t[None, :, None]
    onehot = (_t5_bucket(rel)[..., None] == jnp.arange(REL_BUCKETS)).astype(F32)
    tiles = jnp.einsum("bqkn,nh->bhkq", onehot, rel_table.astype(F32),
                       precision=HIGHEST)
    return (tiles - tiles[2:3]) * LOG2E


def _even_layout(w_in):
    d = HEAD_DIM
    a_w = 2 * N_HEADS * d + N_HEADS * d
    offs = {}
    o = 0
    for name, w in (("qkv", a_w), ("z", N_HEADS * d), ("a", N_HEADS), ("b", N_HEADS),
                    ("qb", N_HEADS * d), ("kb", N_HEADS * d), ("vb", N_HEADS * d),
                    ("qi", IDX_HEADS * IDX_DIM), ("ki", IDX_DIM), ("wi", IDX_HEADS)):
        offs[name] = (o, o + w)
        o += w
    assert o == w_in.shape[1]
    sl = lambda n: w_in[:, offs[n][0]:offs[n][1]]
    small_w = IDX_DIM + 2 * N_HEADS + IDX_HEADS
    small_pad = -small_w % d
    zeros = lambda n: jnp.zeros((w_in.shape[0], n), w_in.dtype)
    w32 = jnp.concatenate([sl("qkv"), sl("z"), sl("qb"), sl("qi"),
                           sl("ki"), sl("a"), sl("b"), sl("wi"), zeros(small_pad)], axis=1)
    n32 = w32.shape[1]
    tn = n32 // 5
    assert tn * 5 == n32 and tn % d == 0
    w16 = jnp.concatenate([sl("kb"), zeros(tn - N_HEADS * d)], axis=1)
    nh = N_HEADS
    cols = dict(qa=0, ka=nh, va=2 * nh, za=3 * nh, qb=4 * nh, qi=5 * nh, small=6 * nh, kb=0,
                a_lane=IDX_DIM, b_lane=IDX_DIM + nh, wi_lane=IDX_DIM + 2 * nh, n32=n32, tn=tn)
    return jnp.concatenate([w32, w16], axis=1).astype(BF16), sl("vb").T.astype(BF16), cols


def kernel(x, norm_g, w_in_even, conv_w_even, a_log_even, dt_bias_even, a_norm_even, w_out_even,
           rel_bias, w_in_odd, lb_logits, d_norm_odd, w_out_odd, w_gate, w_up, w_down):
    bsz, s, d = x.shape
    t = bsz * s
    depth = norm_g.shape[0]
    nh = N_HEADS
    tq = Q_TILE
    lb_all = jnp.cumsum(jax.nn.softmax(lb_logits.astype(F32), axis=0), axis=0)
    lb_all = lb_all - lb_all[:1]
    odd_cols = dict(qc=0, kc=nh, vc=2 * nh, qd=0, fd=nh, id=2 * nh, gd=3 * nh)
    bias_tiles = _bias_tiles(rel_bias, tq)

    h = x.reshape(t, d)
    for l in range(depth):
        if l % 2 == 0:
            e = l // 2
            w_even, w_vt, cols = _even_layout(w_in_even[e])
            p32, p16, vt = _norm_matmul(h, norm_g[l, 0], w_even, tm=PROJ_TILE, tn=cols["tn"], n32=cols["n32"],
                                        w_t=w_vt)
            p32 = p32.reshape(bsz, s, -1)
            p16 = p16.reshape(bsz, s, -1)
            o_1 = _deltanet(p32, conv_w_even[e], a_log_even[e], dt_bias_even[e], a_norm_even[e],
                            ts=min(SEQ_TILE, s), cols=cols)
            o_2 = _dsa(p32, p16, vt, bias_tiles, tq=tq, cols=cols)
            w_out = w_out_even[e]
        else:
            o = l // 2
            n16 = 3 * nh * HEAD_DIM
            w_odd = jnp.concatenate([w_in_odd[o][:, n16:], w_in_odd[o][:, :n16]], axis=1).astype(BF16)
            p32, p16 = _norm_matmul(h, norm_g[l, 0], w_odd, tm=PROJ_TILE, tn=ODD_COL_TILE, n32=w_odd.shape[1] - n16)
            p32 = p32.reshape(bsz, s, -1)
            p16 = p16.reshape(bsz, s, -1)
            o_1 = _stickbreak(p16, tq=tq, cols=odd_cols)
            o_2 = _hgrn2(p32, lb_all[l], d_norm_odd[o], ts=min(SEQ_TILE, s), cols=odd_cols)
            w_out = w_out_odd[o]
        h = _mix_ffn(o_1.reshape(t, -1), o_2.reshape(t, -1), w_out, h, norm_g[l, 1], norm_g[l, 2], norm_g[l, 3],
                     w_gate[l], w_up[l], w_down[l], tm=ROW_TILE, tf=FFN_TILE)
    return h.reshape(bsz, s, d)
```

```python
import functools
import math

import jax
import jax.numpy as jnp
from jax import lax
from jax.experimental import pallas as pl
from jax.experimental.pallas import tpu as pltpu

F32 = jnp.float32
BF16 = jnp.bfloat16
HIGHEST = lax.Precision.HIGHEST

CHUNK = 64
HEAD_DIM = 128
N_HEADS = 4
IDX_HEADS = 8
IDX_DIM = 64
TOPK_MAX = 256
CONV_WIDTH = 4
REL_BUCKETS = 32
REL_MAX_DIST = 128
EPS = 1e-6
NEG_BIG = -1e30
LOG2E = 1.4426950408889634
BISECT_COARSE = 12
BISECT_FIXED = 8
BISECT_EXTRA = 6
F32_LOWEST = -3.4028234663852886e38
EXP_ZERO_BELOW = -104.0
VMEM_LIMIT = 56 * 1024 * 1024

PROJ_TILE = 2048
ROW_TILE = 512
SEQ_TILE = 512
Q_TILE = 128
ODD_COL_TILE = 512
FFN_TILE = 1408


def _mm(a, b):
    return jnp.dot(a.astype(BF16), b.astype(BF16), preferred_element_type=F32)


def _mm_nt(a, b):
    return lax.dot_general(a.astype(BF16), b.astype(BF16), (((1,), (1,)), ((), ())),
                           preferred_element_type=F32)


def _mm_tn(a, b):
    return lax.dot_general(a.astype(BF16), b.astype(BF16), (((0,), (0,)), ((), ())),
                           preferred_element_type=F32)


def _split(x):
    hi = x.astype(BF16)
    return hi, (x - hi.astype(F32)).astype(BF16)


def _floor_bf16(x):
    bits = pltpu.bitcast(x, jnp.int32)
    down = jnp.where(bits >= 0, bits, bits + 0xFFFF) & jnp.int32(-65536)
    return pltpu.bitcast(down, F32).astype(BF16)


def _sigmoid(x):
    return 1.0 / (1.0 + jnp.exp(-x))


def _silu(x):
    return x * _sigmoid(x)


def _softplus(x):
    return jnp.maximum(x, 0.0) + jnp.log1p(jnp.exp(-jnp.abs(x)))


def _rms(x, g):
    return x * lax.rsqrt(jnp.mean(x * x, axis=-1, keepdims=True) + EPS) * g


def _iota(shape, dim):
    return lax.broadcasted_iota(jnp.int32, shape, dim)


def _ind(mask):
    return jnp.where(mask, 1.0, 0.0)


def _norm_matmul_kernel(x_ref, g_ref, w_ref, *rest, n_t, tiles32):
    if n_t:
        wt_ref, o32_ref, o16_ref, ot_ref, xn_ref = rest
    else:
        o32_ref, o16_ref, xn_ref = rest
    j = pl.program_id(1)

    @pl.when(j == 0)
    def _():
        xn_ref[...] = _rms(x_ref[...], g_ref[...]).astype(BF16)
        if n_t:
            ot_ref[...] = lax.dot_general(wt_ref[...], xn_ref[...], (((1,), (1,)), ((), ())),
                                          preferred_element_type=F32).astype(BF16)

    y = jnp.dot(xn_ref[...], w_ref[...], preferred_element_type=F32)

    @pl.when(j < tiles32)
    def _():
        o32_ref[...] = y

    @pl.when(j >= tiles32)
    def _():
        o16_ref[...] = y.astype(BF16)


def _norm_matmul(x, g, w, *, tm, tn, n32, w_t=None):
    t, d = x.shape
    n = w.shape[1]
    n_t = 0 if w_t is None else w_t.shape[0]
    tiles32 = n32 // tn
    assert tiles32 * tn == n32 and (n - n32) % tn == 0 and 0 < n32 < n
    in_specs = [pl.BlockSpec((tm, d), lambda i, j: (i, 0)),
                pl.BlockSpec((1, d), lambda i, j: (0, 0)),
                pl.BlockSpec((d, tn), lambda i, j: (0, j))]
    out_specs = [pl.BlockSpec((tm, tn), lambda i, j: (i, jnp.minimum(j, tiles32 - 1))),
                 pl.BlockSpec((tm, tn), lambda i, j: (i, jnp.maximum(j - tiles32, 0)))]
    out_shape = [jax.ShapeDtypeStruct((t, n32), F32), jax.ShapeDtypeStruct((t, n - n32), BF16)]
    args = [x, g.reshape(1, d), w]
    if n_t:
        in_specs.append(pl.BlockSpec((n_t, d), lambda i, j: (0, 0)))
        out_specs.append(pl.BlockSpec((n_t, tm), lambda i, j: (0, i)))
        out_shape.append(jax.ShapeDtypeStruct((n_t, t), BF16))
        args.append(w_t)
    return pl.pallas_call(
        functools.partial(_norm_matmul_kernel, n_t=n_t, tiles32=tiles32),
        grid=(t // tm, n // tn),
        in_specs=in_specs,
        out_specs=out_specs,
        out_shape=out_shape,
        scratch_shapes=[pltpu.VMEM((tm, d), BF16)],
        compiler_params=pltpu.CompilerParams(
            dimension_semantics=("parallel", "arbitrary"), vmem_limit_bytes=VMEM_LIMIT),
        name="norm_matmul",
    )(*args)


def _mix_ffn_kernel(ca_ref, cb_ref, wa_ref, wb_ref, h_ref, gmix_ref, gpre_ref, gpost_ref,
                    wg_ref, wu_ref, wd_ref, o_ref, h1_ref, xn_ref, acc_ref):
    f = pl.program_id(1)

    @pl.when(f == 0)
    def _():
        y = (jnp.dot(ca_ref[...], wa_ref[...], preferred_element_type=F32)
             + jnp.dot(cb_ref[...], wb_ref[...], preferred_element_type=F32))
        h1 = h_ref[...] + _rms(y, gmix_ref[...])
        h1_ref[...] = h1
        xn_ref[...] = _rms(h1, gpre_ref[...]).astype(BF16)
        acc_ref[...] = jnp.zeros_like(acc_ref)

    xn = xn_ref[...]
    gate = jnp.dot(xn, wg_ref[...], preferred_element_type=F32)
    up = jnp.dot(xn, wu_ref[...], preferred_element_type=F32)
    act = (_silu(gate) * up).astype(BF16)
    acc_ref[...] += jnp.dot(act, wd_ref[...], preferred_element_type=F32)

    @pl.when(f == pl.num_programs(1) - 1)
    def _():
        o_ref[...] = h1_ref[...] + _rms(acc_ref[...], gpost_ref[...])


def _mix_ffn(ca, cb, w_out, h, g_mix, g_pre, g_post, wg, wu, wd, *, tm, tf):
    t, d = h.shape
    ff = wg.shape[1]
    wa_n = ca.shape[1]
    wb_n = cb.shape[1]
    row = pl.BlockSpec((1, d), lambda i, f: (0, 0))
    return pl.pallas_call(
        _mix_ffn_kernel,
        grid=(t // tm, ff // tf),
        in_specs=[pl.BlockSpec((tm, wa_n), lambda i, f: (i, 0)),
                  pl.BlockSpec((tm, wb_n), lambda i, f: (i, 0)),
                  pl.BlockSpec((wa_n, d), lambda i, f: (0, 0)),
                  pl.BlockSpec((wb_n, d), lambda i, f: (0, 0)),
                  pl.BlockSpec((tm, d), lambda i, f: (i, 0)),
                  row, row, row,
                  pl.BlockSpec((d, tf), lambda i, f: (0, f)),
                  pl.BlockSpec((d, tf), lambda i, f: (0, f)),
                  pl.BlockSpec((tf, d), lambda i, f: (f, 0))],
        out_specs=pl.BlockSpec((tm, d), lambda i, f: (i, 0)),
        out_shape=jax.ShapeDtypeStruct((t, d), F32),
        scratch_shapes=[pltpu.VMEM((tm, d), F32), pltpu.VMEM((tm, d), BF16), pltpu.VMEM((tm, d), F32)],
        compiler_params=pltpu.CompilerParams(
            dimension_semantics=("parallel", "arbitrary"), vmem_limit_bytes=VMEM_LIMIT),
        name="mix_ffn",
    )(ca, cb, w_out[:wa_n].astype(BF16), w_out[wa_n:].astype(BF16), h,
      g_mix.reshape(1, d), g_pre.reshape(1, d), g_post.reshape(1, d),
      wg.astype(BF16), wu.astype(BF16), wd.astype(BF16))


def _deltanet_kernel(xq_ref, xk_ref, xv_ref, z_ref, sm_ref, cwq_ref, cwk_ref, cwv_ref,
                     alog_ref, dtb_ref, gn_ref, o_ref,
                     xpad_ref, q_ref, k_ref, v_ref, gb_ref, bb_ref, u_ref, w_ref, qk_ref, st_ref,
                     *, ts, a_col, b_col):
    s = pl.program_id(1)
    c = CHUNK
    d = HEAD_DIM
    nh = N_HEADS

    @pl.when(s == 0)
    def _():
        xpad_ref[:, 0:8, :] = jnp.zeros((3, 8, nh * d), F32)
        st_ref[...] = jnp.zeros_like(st_ref)

    @pl.when(s != 0)
    def _():
        xpad_ref[:, 0:8, :] = xpad_ref[:, ts:ts + 8, :]

    xpad_ref[0, 8:ts + 8, :] = xq_ref[...]
    xpad_ref[1, 8:ts + 8, :] = xk_ref[...]
    xpad_ref[2, 8:ts + 8, :] = xv_ref[...]

    def conv_silu(idx, cw_ref, hs):
        cw = cw_ref[:, hs]
        acc = xpad_ref[idx, 8 - (CONV_WIDTH - 1):8 - (CONV_WIDTH - 1) + ts, hs] * cw[0:1, :]
        for j in range(1, CONV_WIDTH):
            off = 8 - (CONV_WIDTH - 1) + j
            acc = acc + xpad_ref[idx, off:off + ts, hs] * cw[j:j + 1, :]
        return _silu(acc)

    def l2norm(t):
        return t * lax.rsqrt(jnp.sum(t * t, axis=-1, keepdims=True) + EPS)

    row = _iota((c, c), 0)
    col = _iota((c, c), 1)
    tri = (col <= row)
    strict = (col < row)
    tri_f = tri.astype(F32)
    upper_f = (row <= col).astype(F32)
    eye = (row == col).astype(F32)
    gnorm = gn_ref[...]
    chunks = range(ts // c)
    rs = [slice(ci * c, (ci + 1) * c) for ci in chunks]
    tri2 = jnp.concatenate([tri_f, tri_f], axis=1).astype(BF16)
    ones2 = jnp.ones((c, 2 * c), BF16)

    def cum2(lhs2, x):
        hi, lo = _split(x)
        return jnp.dot(lhs2, jnp.concatenate([hi, lo], axis=0), preferred_element_type=F32)

    for hh in range(nh):
        hs = slice(hh * d, (hh + 1) * d)
        q_ref[:, hs] = l2norm(conv_silu(0, cwq_ref, hs)) * (d ** -0.5)
        k_ref[:, hs] = l2norm(conv_silu(1, cwk_ref, hs))
        v_ref[:, hs] = conv_silu(2, cwv_ref, hs)

        a_raw = sm_ref[:, a_col + hh:a_col + hh + 1]
        b_raw = sm_ref[:, b_col + hh:b_col + hh + 1]
        g = -jnp.exp(alog_ref[:, hh:hh + 1]) * _softplus(a_raw + dtb_ref[:, hh:hh + 1])
        gb_ref[:, hs] = jnp.broadcast_to(g, (ts, d))
        bb_ref[:, hs] = jnp.broadcast_to(_sigmoid(b_raw), (ts, d))

        q = [q_ref[r, hs] for r in rs]
        k = [k_ref[r, hs] for r in rs]
        beta = [bb_ref[r, hs] for r in rs]
        gb = [gb_ref[r, hs] for r in rs]
        gc = [cum2(tri2, x) for x in gb]
        gc_row = [cum2(ones2, x[:, :c] * upper_f) for x in gb]
        decay = [jnp.where(tri, jnp.exp(jnp.minimum(a[:, :c] - b, 0.0)), 0.0) for a, b in zip(gc, gc_row)]
        kk = [_mm_nt(x, x) for x in k]
        n = [-jnp.where(strict, b[:, :c] * x * dc, 0.0) for b, x, dc in zip(beta, kk, decay)]
        inv = [eye + x for x in n]
        for step in range(5):
            nb = [x.astype(BF16) for x in n]
            n = [jnp.dot(x, x, preferred_element_type=F32) for x in nb]
            inv = [iv + _mm(iv, x) for iv, x in zip(inv, n)]
        egc = [jnp.exp(x) for x in gc]
        gl = [x[c - 1:c, :] for x in gc]
        inv_l = [x.astype(BF16) for x in inv]
        u = [_mm(a, v_ref[r, hs] * b) for a, r, b in zip(inv_l, rs, beta)]
        w = [_mm(a, x * (b * e)) for a, x, b, e in zip(inv_l, k, beta, egc)]
        qk = [_mm_nt(a, b) * dc for a, b, dc in zip(q, k, decay)]
        for ci in chunks:
            r = rs[ci]
            u_ref[r, hs] = u[ci]
            w_ref[r, hs] = w[ci]
            qk_ref[hh, r, :] = qk[ci]
            q_ref[r, hs] = q[ci] * egc[ci]
            k_ref[r, hs] = k[ci] * jnp.exp(gl[ci] - gc[ci])
            gb_ref[r, hs] = jnp.broadcast_to(jnp.exp(gl[ci]), (c, d))

    def chunk_body(ci, carry):
        r0 = pl.multiple_of(ci * c, c)
        rows = pl.ds(r0, c)
        hss = [slice(hh * d, (hh + 1) * d) for hh in range(nh)]
        st = [st_ref[hh] for hh in range(nh)]
        w_st = [_mm(w_ref[rows, hs], s_) for hs, s_ in zip(hss, st)]
        q_st = [_mm(q_ref[rows, hs], s_) for hs, s_ in zip(hss, st)]
        v_new = [u_ref[rows, hs] - x for hs, x in zip(hss, w_st)]
        o = [a + _mm(qk_ref[hh, rows, :], v) for hh, (a, v) in enumerate(zip(q_st, v_new))]
        kv = [_mm_tn(k_ref[rows, hs], v) for hs, v in zip(hss, v_new)]
        for hh, hs in enumerate(hss):
            st_ref[hh] = st[hh] * gb_ref[pl.ds(r0, 1), hs] + kv[hh]
            o_ref[rows, hs] = (_rms(o[hh], gnorm) * _silu(z_ref[rows, hs])).astype(o_ref.dtype)
        return carry

    lax.fori_loop(0, ts // c, chunk_body, 0)


def _deltanet(p32, conv_w, a_log, dt_bias, a_norm_g, *, ts, cols):
    bsz, s, _ = p32.shape
    d = HEAD_DIM
    nh = N_HEADS
    w = nh * d
    pad = lambda t: jnp.pad(t.astype(F32), (0, d - t.shape[0])).reshape(1, d)
    kernel = functools.partial(_deltanet_kernel, ts=ts, a_col=cols["a_lane"], b_col=cols["b_lane"])
    tile = lambda name: pl.BlockSpec((None, ts, w), lambda b, i: (b, i, cols[name] // nh))
    conv = lambda k: pl.BlockSpec((CONV_WIDTH, w), lambda b, i: (0, k))
    row = pl.BlockSpec((1, d), lambda b, i: (0, 0))
    return pl.pallas_call(
        kernel,
        grid=(bsz, s // ts),
        in_specs=[tile("qa"), tile("ka"), tile("va"), tile("za"),
                  pl.BlockSpec((None, ts, d), lambda b, i: (b, i, cols["small"])),
                  conv(0), conv(1), conv(2), row, row, row],
        out_specs=pl.BlockSpec((None, ts, w), lambda b, i: (b, i, 0)),
        out_shape=jax.ShapeDtypeStruct((bsz, s, w), BF16),
        scratch_shapes=[pltpu.VMEM((3, ts + 8, w), F32)]
        + [pltpu.VMEM((ts, w), F32) for _ in range(7)]
        + [pltpu.VMEM((nh, ts, CHUNK), F32), pltpu.VMEM((nh, d, d), F32)],
        compiler_params=pltpu.CompilerParams(
            dimension_semantics=("parallel", "arbitrary"), vmem_limit_bytes=VMEM_LIMIT),
        name="deltanet",
    )(p32, p32, p32, p32, p32, conv_w.astype(F32), conv_w.astype(F32), conv_w.astype(F32),
      pad(a_log), pad(dt_bias), a_norm_g.astype(F32).reshape(1, d))


def _hgrn2_kernel(q_ref, f_ref, i_ref, gate_ref, lb_ref, gn_ref, o_ref,
                  qs_ref, ks_ref, gc_ref, st_ref, *, ts):
    s = pl.program_id(1)
    c = CHUNK
    d = HEAD_DIM
    nh = N_HEADS
    SUB = 16

    @pl.when(s == 0)
    def _():
        st_ref[...] = jnp.zeros_like(st_ref)

    lb = lb_ref[...]
    f_raw = f_ref[...]
    log_sig = jnp.minimum(f_raw, 0.0) - jnp.log1p(jnp.exp(-jnp.abs(f_raw)))
    la = jnp.log(lb)
    lbb = jnp.log1p(-lb) + log_sig
    log_f = jnp.maximum(la, lbb) + jnp.log1p(jnp.exp(-jnp.abs(la - lbb)))
    qs_ref[...] = _silu(q_ref[...])
    ks_ref[...] = (1.0 - lb) * _sigmoid(-f_raw)

    row = _iota((c, c), 0)
    col = _iota((c, c), 1)
    tri_f = (col <= row).astype(F32)
    ones_dd = jnp.ones((d, d), BF16)
    rows_8d = _iota((8, d), 0)
    gnorm = gn_ref[...]

    tri2 = jnp.concatenate([tri_f, tri_f], axis=1).astype(BF16)
    for ci in range(ts // c):
        hi, lo = _split(log_f[ci * c:(ci + 1) * c, :])
        gc_ref[ci * c:(ci + 1) * c, :] = jnp.dot(tri2, jnp.concatenate([hi, lo], axis=0),
                                                 preferred_element_type=F32)

    blocks = [(sb * SUB, (sb + 1) * SUB) for sb in range(c // SUB)]

    def chunk_loop(ci, carry):
        r0 = pl.multiple_of(ci * c, c)
        rows = pl.ds(r0, c)
        hss = [slice(hh * d, (hh + 1) * d) for hh in range(nh)]
        q = [qs_ref[rows, hs] for hs in hss]
        k = [ks_ref[rows, hs] for hs in hss]
        v = [i_ref[rows, hs] for hs in hss]
        gc = [gc_ref[rows, hs] for hs in hss]

        def near_products(q, k, gc):
            prods = []
            for top, end in blocks:
                for j in range(top, end):
                    lo = (j // 8) * 8
                    e = jnp.exp2(gc[lo:end, :] - gc[j:j + 1, :])
                    if j % 8:
                        head = jnp.where(rows_8d >= j - lo, e[:8], 0.0)
                        e = jnp.concatenate([head, e[8:]], axis=0) if lo + 8 < end else head
                    prods.append(q[lo:end, :] * k[j:j + 1, :] * e)
            return jnp.concatenate(prods, axis=0).astype(BF16)

        def far_operands(q, k, gc):
            out = []
            for top, end in blocks[1:]:
                g_b = gc[top - 1:top, :]
                out.append((q[top:end, :] * jnp.exp(gc[top:end, :] - g_b),
                            k[:top, :] * jnp.exp(jnp.minimum(g_b - gc[:top, :], 0.0))))
            return out

        near = [near_products(a, b, g * LOG2E) for a, b, g in zip(q, k, gc)]
        far_ops = [far_operands(*x) for x in zip(q, k, gc)]
        st = [st_ref[hh] for hh in range(nh)]
        gl = [x[c - 1:c, :] for x in gc]
        sums = [jnp.dot(x, ones_dd, preferred_element_type=F32) for x in near]
        qk_far = [[_mm_nt(qe, ke) for qe, ke in ops] for ops in far_ops]
        far = [[_mm(a, vv[:top, :]) for a, (top, _) in zip(qs, blocks[1:])] for qs, vv in zip(qk_far, v)]
        o_st = [_mm_nt(a * jnp.exp(g), s_) for a, g, s_ in zip(q, gc, st)]
        kv = [_mm_tn(vv, kk * jnp.exp(g_l - g)) for vv, kk, g_l, g in zip(v, k, gl, gc)]

        for hh, hs in enumerate(hss):
            groups = [jnp.zeros((8, d), F32) for _ in range(c // 8)]
            at = 0
            for top, end in blocks:
                for j in range(top, end):
                    v_j = v[hh][j:j + 1, :]
                    for g in range(j // 8, end // 8):
                        groups[g] = groups[g] + sums[hh][at:at + 8, :] * v_j
                        at += 8
            for f, (top, end) in zip(far[hh], blocks[1:]):
                for g in range(top // 8, end // 8):
                    groups[g] = groups[g] + f[(g * 8 - top):(g * 8 - top + 8), :]
            o = jnp.concatenate(groups, axis=0) + o_st[hh]
            st_ref[hh] = st[hh] * jnp.exp(gl[hh]) + kv[hh]
            o_ref[rows, hs] = (_rms(o, gnorm) * _silu(gate_ref[rows, hs])).astype(o_ref.dtype)
        return carry

    lax.fori_loop(0, ts // c, chunk_loop, 0)


def _hgrn2(p32, lb, d_norm_g, *, ts, cols):
    bsz, s, _ = p32.shape
    d = HEAD_DIM
    nh = N_HEADS
    w = nh * d
    kernel = functools.partial(_hgrn2_kernel, ts=ts)
    tile = lambda name: pl.BlockSpec((None, ts, w), lambda b, i: (b, i, cols[name] // nh))
    return pl.pallas_call(
        kernel,
        grid=(bsz, s // ts),
        in_specs=[tile("qd"), tile("fd"), tile("id"), tile("gd"),
                  pl.BlockSpec((1, w), lambda b, i: (0, 0)),
                  pl.BlockSpec((1, d), lambda b, i: (0, 0))],
        out_specs=pl.BlockSpec((None, ts, w), lambda b, i: (b, i, 0)),
        out_shape=jax.ShapeDtypeStruct((bsz, s, w), BF16),
        scratch_shapes=[pltpu.VMEM((ts, w), F32), pltpu.VMEM((ts, w), F32),
                        pltpu.VMEM((ts, w), F32), pltpu.VMEM((nh, d, d), F32)],
        compiler_params=pltpu.CompilerParams(
            dimension_semantics=("parallel", "arbitrary"), vmem_limit_bytes=VMEM_LIMIT),
        name="hgrn2",
    )(p32, p32, p32, p32, lb.astype(F32).reshape(1, w), d_norm_g.astype(F32).reshape(1, d))


def _stickbreak_kernel(q_ref, k_ref, v_ref, o_ref, acc_ref, *, tq):
    i = pl.program_id(1)
    d = HEAD_DIM
    nh = N_HEADS
    row = _iota((tq, tq), 0)
    col = _iota((tq, tq), 1)
    causal = col < row
    later = (row > col).astype(BF16)
    later2 = jnp.concatenate([later, later], axis=0)

    heads = [slice(hh * d, (hh + 1) * d) for hh in range(nh)]

    def scores(blocks):
        jobs = [(j, dg, hs) for j, dg in blocks for hs in heads]
        z = [_mm_nt(q_ref[:, hs], k_ref[pl.ds(pl.multiple_of(j * tq, tq), tq), hs]) * (d ** -0.5)
             for j, _, hs in jobs]
        sp = [_softplus(x) for x in z]
        l1m = [jnp.where(causal, -x, 0.0) if dg else -x for x, (_, dg, _) in zip(sp, jobs)]
        rest = [jnp.dot(jnp.concatenate(_split(x), axis=1), later2, preferred_element_type=F32)
                for x in l1m]
        out = [((a - b) + r, l) for a, b, r, l in zip(z, sp, rest, l1m)]
        return [out[b * nh:(b + 1) * nh] for b in range(len(blocks))]

    def block(j, carries):
        (sc,) = scores([(j, False)])
        ps = [jnp.exp(logw + c) for (logw, _), c in zip(sc, carries)]
        pv = [_mm(p, v_ref[pl.ds(pl.multiple_of(j * tq, tq), tq), hs]) for p, hs in zip(ps, heads)]
        for hs, x in zip(heads, pv):
            acc_ref[:, hs] += x
        return tuple(c + jnp.sum(l1m, axis=-1, keepdims=True) for (_, l1m), c in zip(sc, carries))

    jp = jnp.maximum(i - 1, 0)
    live = jnp.where(i > 0, 1.0, 0.0)
    sd, sp_ = scores([(i, True), (jp, False)])
    carries = []
    for hh, hs in enumerate(heads):
        c1 = jnp.sum(sd[hh][1], axis=-1, keepdims=True)
        p_d = jnp.where(causal, jnp.exp(sd[hh][0]), 0.0)
        p_p = jnp.exp(sp_[hh][0] + c1) * live
        acc_ref[:, hs] = (_mm(p_d, v_ref[pl.ds(pl.multiple_of(i * tq, tq), tq), hs])
                          + _mm(p_p, v_ref[pl.ds(pl.multiple_of(jp * tq, tq), tq), hs]))
        carries.append(c1 + jnp.sum(sp_[hh][1], axis=-1, keepdims=True))
    carries = tuple(carries)

    def cond(c):
        worst = functools.reduce(jnp.maximum, c[1])
        return jnp.logical_and(c[0] >= 0, jnp.max(worst) >= EXP_ZERO_BELOW)

    def body(c):
        return c[0] - 1, block(c[0], c[1])

    lax.while_loop(cond, body, (i - 2, carries))
    o_ref[...] = acc_ref[...].astype(o_ref.dtype)


def _stickbreak(p16, *, tq, cols):
    bsz, s, _ = p16.shape
    nh = N_HEADS
    w = nh * HEAD_DIM
    kernel = functools.partial(_stickbreak_kernel, tq=tq)
    resident = dict(pipeline_mode=pl.Buffered(1))
    return pl.pallas_call(
        kernel,
        grid=(bsz, s // tq),
        in_specs=[pl.BlockSpec((None, tq, w), lambda b, i: (b, i, cols["qc"] // nh)),
                  pl.BlockSpec((None, s, w), lambda b, i: (b, 0, cols["kc"] // nh), **resident),
                  pl.BlockSpec((None, s, w), lambda b, i: (b, 0, cols["vc"] // nh), **resident)],
        out_specs=pl.BlockSpec((None, tq, w), lambda b, i: (b, i, 0)),
        out_shape=jax.ShapeDtypeStruct((bsz, s, w), BF16),
        scratch_shapes=[pltpu.VMEM((tq, w), F32)],
        compiler_params=pltpu.CompilerParams(
            dimension_semantics=("parallel", "arbitrary"), vmem_limit_bytes=VMEM_LIMIT),
        name="stickbreak",
    )(p16, p16, p16)


def _dsa_kernel(qi_ref, smq_ref, q_ref, sm_ref, k_ref, vt_ref, bias_ref, o_ref,
                sc_ref, scb_ref, wb_ref, qc_ref, kct_ref, bd_ref, lg_ref, *, tq, k_sel, wi_lane, wide):
    i = pl.program_id(1)
    tk = tq
    d = HEAD_DIM
    nh = N_HEADS
    ksel = float(k_sel)
    per_wide = wide // tk
    n_wide = (i + per_wide) // per_wide
    sub = 2 * tk
    lane_q = _iota((1, tq), 1)

    def tree(parts, op):
        while len(parts) > 1:
            parts = [op(parts[j], parts[j + 1]) if j + 1 < len(parts) else parts[j]
                     for j in range(0, len(parts), 2)]
        return parts[0]

    def col_fold(x, op=jnp.add, rows=8):
        return tree([x[r * rows:(r + 1) * rows] for r in range(x.shape[0] // rows)], op)

    @pl.when(i == 0)
    def _():
        def prep(g, carry):
            g0 = pl.multiple_of(g * wide, wide)
            kt = sm_ref[pl.ds(g0, wide), :].T[:IDX_DIM, :]
            hi, lo = _split(kt)
            kct_ref[:, pl.ds(g0, wide)] = jnp.concatenate([hi, lo, hi], axis=0)
            return carry
        lax.fori_loop(0, sm_ref.shape[0] // wide, prep, 0)

    smq = smq_ref[...]
    lane = _iota(smq.shape, 1)
    for hh in range(IDX_HEADS):
        qh = qi_ref[:, hh * IDX_DIM:(hh + 1) * IDX_DIM]
        hi, lo = _split(qh)
        qc_ref[hh] = jnp.concatenate([hi, hi, lo], axis=-1)
        w = jnp.sum(jnp.where(lane == wi_lane + hh, smq, 0.0), axis=-1, keepdims=True)
        wb_ref[hh] = jnp.broadcast_to(w * ((IDX_HEADS ** -0.5) * (IDX_DIM ** -0.5)), (tq, tk))

    q2t = (q_ref[...] * ((d ** -0.5) * LOG2E)).T.astype(BF16)
    zero_dq = jnp.zeros((d, tq), BF16)
    for p in range(nh // 2):
        top = jnp.concatenate([q2t[2 * p * d:(2 * p + 1) * d], zero_dq], axis=1)
        bot = jnp.concatenate([zero_dq, q2t[(2 * p + 1) * d:(2 * p + 2) * d]], axis=1)
        bd_ref[p] = jnp.concatenate([top, bot], axis=0)

    limit = i * tq + (lane_q // CHUNK + 1) * CHUNK
    rows_t = _iota((tk, tq), 0)

    def score_group(g, mm, masked):
        mn, mx = mm
        for sb in range(wide // sub):
            k0 = pl.multiple_of(g * wide + sb * sub, sub)
            kct = kct_ref[:, pl.ds(k0, sub)]
            tiles = [jnp.zeros((tq, tk), F32) for _ in range(sub // tk)]
            for hh in range(IDX_HEADS):
                s_h = jnp.dot(qc_ref[hh], kct, preferred_element_type=F32)
                for ti in range(sub // tk):
                    tiles[ti] = tiles[ti] + jnp.maximum(s_h[:, ti * tk:(ti + 1) * tk], 0.0) * wb_ref[hh]
            for ti in range(sub // tk):
                kb = pl.multiple_of(k0 + ti * tk, tk)
                sct = tiles[ti].T
                if masked:
                    adm = (kb + rows_t) < limit
                    mn = jnp.minimum(mn, col_fold(jnp.where(adm, sct, jnp.inf), jnp.minimum))
                    sct = jnp.where(adm, sct, -jnp.inf)
                else:
                    mn = jnp.minimum(mn, col_fold(sct, jnp.minimum))
                mx = jnp.maximum(mx, col_fold(sct, jnp.maximum))
                sc_ref[pl.ds(kb, tk), :] = sct
                scb_ref[pl.ds(kb, tk), :] = _floor_bf16(sct)
        return mn, mx

    def score_pair(j, mm):
        return score_group(2 * j + 1, score_group(2 * j, mm, False), False)

    n_full = n_wide - 1
    mm = lax.fori_loop(0, n_full // 2, score_pair,
                       (jnp.full((8, tq), jnp.inf, F32), jnp.full((8, tq), -jnp.inf, F32)))
    mm = lax.cond(n_full % 2 == 1, lambda c: score_group(n_full - 1, c, False), lambda c: c, mm)
    mn, mx = score_group(n_wide - 1, mm, True)

    n_pairs = (n_wide + 1) // 2

    @pl.when(n_wide % 2 == 1)
    def _():
        sc_ref[pl.ds(pl.multiple_of(n_wide * wide, wide), wide), :] = jnp.full((wide, tq), -jnp.inf, F32)
        scb_ref[pl.ds(pl.multiple_of(n_wide * wide, wide), wide), :] = jnp.full((wide, tq), -jnp.inf, BF16)
    rmin = jnp.min(mn, axis=0, keepdims=True)
    rmax = jnp.max(mx, axis=0, keepdims=True)

    def count(pred):
        def body(j, acc):
            for g in (2 * j, 2 * j + 1):
                acc = acc + col_fold(pred(sc_ref[pl.ds(pl.multiple_of(g * wide, wide), wide), :]))
            return acc
        return jnp.sum(lax.fori_loop(0, n_pairs, body, jnp.zeros((8, tq), F32)), axis=0, keepdims=True)

    def max_below(x):
        def body(j, acc):
            for g in (2 * j, 2 * j + 1):
                blk = sc_ref[pl.ds(pl.multiple_of(g * wide, wide), wide), :]
                acc = jnp.maximum(acc, col_fold(jnp.where(blk < x, blk, -jnp.inf), jnp.maximum))
            return acc
        return jnp.max(lax.fori_loop(0, n_pairs, body, jnp.full((8, tq), -jnp.inf, F32)), axis=0, keepdims=True)

    n_adm = limit.astype(F32)
    all_sel = n_adm <= ksel

    def bisect(c):
        lo, hi, c_lo = c
        mid = 0.5 * lo + 0.5 * hi
        cm = count(lambda blk: _ind(blk >= mid))
        ge = cm >= ksel
        return jnp.where(ge, mid, lo), jnp.where(ge, hi, mid), jnp.where(ge, cm, c_lo)

    def pending(c_lo, tied):
        return jnp.where(all_sel, 0.0, jnp.where(tied > 0.5, 0.0, _ind(c_lo != ksel)))

    def bisect_coarse(_, c):
        lo, hi, c_lo = c
        mid = _floor_bf16(0.5 * lo + 0.5 * hi).astype(F32)
        t_b = jnp.broadcast_to(mid, (16, tq)).astype(BF16)
        one_b = jnp.ones((16, tq), BF16)
        zero_b = jnp.zeros((16, tq), BF16)

        def body(j, acc):
            for g in (2 * j, 2 * j + 1):
                blk = scb_ref[pl.ds(pl.multiple_of(g * wide, wide), wide), :]
                ind = [jnp.where(blk[r * 16:(r + 1) * 16] >= t_b, one_b, zero_b) for r in range(wide // 16)]
                acc = acc + tree(ind, jnp.add).astype(F32)
            return acc

        acc = lax.fori_loop(0, n_pairs, body, jnp.zeros((16, tq), F32))
        cm = jnp.sum(acc, axis=0, keepdims=True)
        ge = cm >= ksel
        return jnp.where(ge, mid, lo), jnp.where(ge, hi, mid), jnp.where(ge, cm, c_lo)

    lo0 = _floor_bf16(rmin).astype(F32)
    hi0 = _floor_bf16(rmax + (jnp.abs(rmax) * (2.0 ** -6) + 1e-30)).astype(F32)
    state = lax.fori_loop(0, BISECT_COARSE, bisect_coarse, (lo0, hi0, n_adm))
    state = lax.fori_loop(0, BISECT_FIXED, lambda _, c: bisect(c), state)

    def round_cond(c):
        return jnp.max(pending(c[0][2], c[1])) > 0.5

    def round_body(c):
        st, tied, v, need = c

        def more_cond(s):
            return jnp.logical_and(s[0] < BISECT_EXTRA, jnp.max(pending(s[1][2], tied)) > 0.5)

        _, st = lax.while_loop(more_cond, lambda s: (s[0] + 1, bisect(s[1])), (jnp.int32(0), st))
        pend = pending(st[2], tied)

        def check(_):
            cand = max_below(st[1])
            c_ge = count(lambda blk: _ind(blk >= cand))
            c_gt = count(lambda blk: _ind(blk > cand))
            ok = jnp.where(pend > 0.5, _ind(c_ge >= ksel), 0.0)
            return (jnp.where(ok > 0.5, 1.0, tied), jnp.where(ok > 0.5, cand, v),
                    jnp.where(ok > 0.5, ksel - c_gt, need))

        tied, v, need = lax.cond(jnp.max(pend) > 0.5, check, lambda _: (tied, v, need), 0)
        return st, tied, v, need

    zeros1 = jnp.zeros((1, tq), F32)
    (lo_f, _, _), tied, v_tie, need = lax.while_loop(round_cond, round_body, (state, zeros1, zeros1, zeros1))
    vth = jnp.where(all_sel, F32_LOWEST, jnp.where(tied > 0.5, v_tie, lo_f))

    @pl.when(jnp.max(tied) > 0.5)
    def _():
        v_eq = jnp.where(tied > 0.5, v_tie, jnp.inf)
        incl = (_iota((tk, tk), 1) <= _iota((tk, tk), 0)).astype(BF16)

        def demote(g, seen):
            g0 = pl.multiple_of(g * wide, wide)
            xs = [sc_ref[pl.ds(g0 + pb * tk, tk), :] for pb in range(per_wide)]
            eqs = [_ind(x == v_eq) for x in xs]
            inblk = [jnp.dot(incl, e.astype(BF16), preferred_element_type=F32) for e in eqs]
            for pb in range(per_wide):
                rank = inblk[pb] + seen
                sc_ref[pl.ds(g0 + pb * tk, tk), :] = jnp.where(eqs[pb] * _ind(rank > need) > 0.5,
                                                               -jnp.inf, xs[pb])
                seen = seen + jnp.sum(col_fold(eqs[pb]), axis=0, keepdims=True)
            return seen

        lax.fori_loop(0, n_wide, demote, zeros1)

    g_near = jnp.maximum(i - 1, 0) // per_wide

    def logit_group(g, mx, near):
        out = list(mx)
        for sb in range(wide // sub):
            k0 = pl.multiple_of(g * wide + sb * sub, sub)
            sel = sc_ref[pl.ds(k0, sub), :] >= vth
            for p in range(nh // 2):
                pair = jnp.dot(k_ref[pl.ds(k0, sub), 2 * p * d:(2 * p + 2) * d], bd_ref[p],
                               preferred_element_type=F32)
                for hh in (2 * p, 2 * p + 1):
                    lm = pair[:, (hh - 2 * p) * tq:(hh - 2 * p + 1) * tq]
                    if near:
                        back = [jnp.clip(i - (g * per_wide + sb * (sub // tk) + pb), 0, 2)
                                for pb in range(sub // tk)]
                        lm = lm + jnp.concatenate([bias_ref[bk, hh] for bk in back], axis=0)
                    lm = jnp.where(sel, lm, NEG_BIG)
                    lg_ref[hh, pl.ds(k0, sub), :] = lm
                    out[hh] = jnp.maximum(out[hh], col_fold(lm, jnp.maximum))
        return tuple(out)

    mx = tuple(jnp.full((8, tq), NEG_BIG, F32) for _ in range(nh))
    def logit_pair(j, mx, near):
        return logit_group(2 * j + 1, logit_group(2 * j, mx, near), near)

    far_pairs = g_near // 2
    mx = lax.fori_loop(0, far_pairs, functools.partial(logit_pair, near=False), mx)
    mx = lax.fori_loop(far_pairs, n_pairs, functools.partial(logit_pair, near=True), mx)
    m_q = [jnp.max(mx[hh], axis=0, keepdims=True) for hh in range(nh)]

    ones_rows = jnp.ones((8, wide), BF16)

    def pv_pair(j, carry):
        ls, accs = list(carry[0]), list(carry[1])
        jobs = [(pl.multiple_of(g * wide, wide), hh) for g in (2 * j, 2 * j + 1) for hh in range(nh)]
        ps = [jnp.exp2(lg_ref[hh, pl.ds(g0, wide), :] - m_q[hh]).astype(BF16) for g0, hh in jobs]
        outs = [jnp.dot(jnp.concatenate([vt_ref[hh * d:(hh + 1) * d, pl.ds(g0, wide)], ones_rows], axis=0),
                        p, preferred_element_type=F32) for (g0, hh), p in zip(jobs, ps)]
        for (_, hh), out in zip(jobs, outs):
            ls[hh] = ls[hh] + out[d:]
            accs[hh] = accs[hh] + out[:d]
        return tuple(ls), tuple(accs)

    ls, accs = lax.fori_loop(0, n_pairs, pv_pair,
                             (tuple(jnp.zeros((8, tq), F32) for _ in range(nh)),
                              tuple(jnp.zeros((d, tq), F32) for _ in range(nh))))
    for hh in range(nh):
        o_ref[:, hh * d:(hh + 1) * d] = (accs[hh] / ls[hh][0:1]).T.astype(o_ref.dtype)


def _dsa(p32, p16, vt, bias_tiles, *, tq, cols):
    bsz, s, _ = p32.shape
    d = HEAD_DIM
    nh = N_HEADS
    wide = 4 * tq
    k_sel = min(TOPK_MAX, s // 4)
    w512 = nh * d
    kernel = functools.partial(_dsa_kernel, tq=tq, k_sel=k_sel, wi_lane=cols["wi_lane"], wide=wide)
    resident = dict(pipeline_mode=pl.Buffered(1))
    return pl.pallas_call(
        kernel,
        grid=(bsz, s // tq),
        in_specs=[pl.BlockSpec((None, tq, w512), lambda b, i: (b, i, cols["qi"] // nh)),
                  pl.BlockSpec((None, tq, d), lambda b, i: (b, i, cols["small"])),
                  pl.BlockSpec((None, tq, w512), lambda b, i: (b, i, cols["qb"] // nh)),
                  pl.BlockSpec((None, s, d), lambda b, i: (b, 0, cols["small"]), **resident),
                  pl.BlockSpec((None, s, w512), lambda b, i: (b, 0, cols["kb"] // nh), **resident),
                  pl.BlockSpec((w512, s), lambda b, i: (0, b), **resident),
                  pl.BlockSpec((3, nh, tq, tq), lambda b, i: (0, 0, 0, 0), **resident)],
        out_specs=pl.BlockSpec((None, tq, w512), lambda b, i: (b, i, 0)),
        out_shape=jax.ShapeDtypeStruct((bsz, s, w512), BF16),
        scratch_shapes=[pltpu.VMEM((s, tq), F32),
                        pltpu.VMEM((s, tq), BF16),
                        pltpu.VMEM((IDX_HEADS, tq, tq), F32),
                        pltpu.VMEM((IDX_HEADS, tq, 3 * IDX_DIM), BF16),
                        pltpu.VMEM((3 * IDX_DIM, s), BF16),
                        pltpu.VMEM((nh // 2, 2 * d, 2 * tq), BF16),
                        pltpu.VMEM((nh, s, tq), F32)],
        compiler_params=pltpu.CompilerParams(
            dimension_semantics=("parallel", "arbitrary"), vmem_limit_bytes=VMEM_LIMIT),
        name="dsa",
    )(p32, p32, p32, p32, p16, vt, bias_tiles)


def _t5_bucket(rel):
    nb = REL_BUCKETS // 2
    max_exact = nb // 2
    ret = jnp.where(rel > 0, nb, 0)
    n = jnp.abs(rel)
    large = max_exact + (jnp.log(jnp.maximum(n, 1).astype(F32) / max_exact)
                         / math.log(REL_MAX_DIST / max_exact) * (nb - max_exact)).astype(jnp.int32)
    large = jnp.minimum(large, nb - 1)
    return ret + jnp.where(n < max_exact, n, large)


def _bias_tiles(rel_table, tq):
    assert tq >= REL_MAX_DIST
    t = jnp.arange(tq)
    back = jnp.arange(3)
    rel = (t[None, None, :] - back[:, None, None] * tq) - t[None, :, None]
    onehot = (_t5_bucket(rel)[..., None] == jnp.arange(REL_BUCKETS)).astype(F32)
    tiles = jnp.einsum("bqkn,nh->bhkq", onehot, rel_table.astype(F32),
                       precision=HIGHEST)
    return (tiles - tiles[2:3]) * LOG2E


def _even_layout(w_in):
    d = HEAD_DIM
    a_w = 2 * N_HEADS * d + N_HEADS * d
    offs = {}
    o = 0
    for name, w in (("qkv", a_w), ("z", N_HEADS * d), ("a", N_HEADS), ("b", N_HEADS),
                    ("qb", N_HEADS * d), ("kb", N_HEADS * d), ("vb", N_HEADS * d),
                    ("qi", IDX_HEADS * IDX_DIM), ("ki", IDX_DIM), ("wi", IDX_HEADS)):
        offs[name] = (o, o + w)
        o += w
    assert o == w_in.shape[1]
    sl = lambda n: w_in[:, offs[n][0]:offs[n][1]]
    small_w = IDX_DIM + 2 * N_HEADS + IDX_HEADS
    small_pad = -small_w % d
    zeros = lambda n: jnp.zeros((w_in.shape[0], n), w_in.dtype)
    w32 = jnp.concatenate([sl("qkv"), sl("z"), sl("qb"), sl("qi"),
                           sl("ki"), sl("a"), sl("b"), sl("wi"), zeros(small_pad)], axis=1)
    n32 = w32.shape[1]
    tn = n32 // 5
    assert tn * 5 == n32 and tn % d == 0
    w16 = jnp.concatenate([sl("kb"), zeros(tn - N_HEADS * d)], axis=1)
    nh = N_HEADS
    cols = dict(qa=0, ka=nh, va=2 * nh, za=3 * nh, qb=4 * nh, qi=5 * nh, small=6 * nh, kb=0,
                a_lane=IDX_DIM, b_lane=IDX_DIM + nh, wi_lane=IDX_DIM + 2 * nh, n32=n32, tn=tn)
    return jnp.concatenate([w32, w16], axis=1).astype(BF16), sl("vb").T.astype(BF16), cols


def kernel(x, norm_g, w_in_even, conv_w_even, a_log_even, dt_bias_even, a_norm_even, w_out_even,
           rel_bias, w_in_odd, lb_logits, d_norm_odd, w_out_odd, w_gate, w_up, w_down):
    bsz, s, d = x.shape
    t = bsz * s
    depth = norm_g.shape[0]
    nh = N_HEADS
    tq = Q_TILE
    lb_all = jnp.cumsum(jax.nn.softmax(lb_logits.astype(F32), axis=0), axis=0)
    lb_all = lb_all - lb_all[:1]
    odd_cols = dict(qc=0, kc=nh, vc=2 * nh, qd=0, fd=nh, id=2 * nh, gd=3 * nh)
    bias_tiles = _bias_tiles(rel_bias, tq)

    h = x.reshape(t, d)
    for l in range(depth):
        if l % 2 == 0:
            e = l // 2
            w_even, w_vt, cols = _even_layout(w_in_even[e])
            p32, p16, vt = _norm_matmul(h, norm_g[l, 0], w_even, tm=PROJ_TILE, tn=cols["tn"], n32=cols["n32"],
                                        w_t=w_vt)
            p32 = p32.reshape(bsz, s, -1)
            p16 = p16.reshape(bsz, s, -1)
            o_1 = _deltanet(p32, conv_w_even[e], a_log_even[e], dt_bias_even[e], a_norm_even[e],
                            ts=min(SEQ_TILE, s), cols=cols)
            o_2 = _dsa(p32, p16, vt, bias_tiles, tq=tq, cols=cols)
            w_out = w_out_even[e]
        else:
            o = l // 2
            n16 = 3 * nh * HEAD_DIM
            w_odd = jnp.concatenate([w_in_odd[o][:, n16:], w_in_odd[o][:, :n16]], axis=1).astype(BF16)
            p32, p16 = _norm_matmul(h, norm_g[l, 0], w_odd, tm=PROJ_TILE, tn=ODD_COL_TILE, n32=w_odd.shape[1] - n16)
            p32 = p32.reshape(bsz, s, -1)
            p16 = p16.reshape(bsz, s, -1)
            o_1 = _stickbreak(p16, tq=tq, cols=odd_cols)
            o_2 = _hgrn2(p32, lb_all[l], d_norm_odd[o], ts=min(SEQ_TILE, s), cols=odd_cols)
            w_out = w_out_odd[o]
        h = _mix_ffn(o_1.reshape(t, -1), o_2.reshape(t, -1), w_out, h, norm_g[l, 1], norm_g[l, 2], norm_g[l, 3],
                     w_gate[l], w_up[l], w_down[l], tm=ROW_TILE, tf=FFN_TILE)
    return h.reshape(bsz, s, d)
```

```python
import functools
import math

import jax
import jax.numpy as jnp
from jax import lax
from jax.experimental import pallas as pl
from jax.experimental.pallas import tpu as pltpu

F32 = jnp.float32
BF16 = jnp.bfloat16
HIGHEST = lax.Precision.HIGHEST

CHUNK = 64
HEAD_DIM = 128
N_HEADS = 4
IDX_HEADS = 8
IDX_DIM = 64
TOPK_MAX = 256
CONV_WIDTH = 4
REL_BUCKETS = 32
REL_MAX_DIST = 128
EPS = 1e-6
NEG_BIG = -1e30
LOG2E = 1.4426950408889634
BISECT_COARSE = 12
BISECT_FIXED = 8
BISECT_EXTRA = 6
F32_LOWEST = -3.4028234663852886e38
EXP_ZERO_BELOW = -104.0
VMEM_LIMIT = 56 * 1024 * 1024

PROJ_TILE = 2048
ROW_TILE = 512
SEQ_TILE = 512
Q_TILE = 128
ODD_COL_TILE = 512
FFN_TILE = 2816


def _mm(a, b):
    return jnp.dot(a.astype(BF16), b.astype(BF16), preferred_element_type=F32)


def _mm_nt(a, b):
    return lax.dot_general(a.astype(BF16), b.astype(BF16), (((1,), (1,)), ((), ())),
                           preferred_element_type=F32)


def _mm_tn(a, b):
    return lax.dot_general(a.astype(BF16), b.astype(BF16), (((0,), (0,)), ((), ())),
                           preferred_element_type=F32)


def _split(x):
    hi = x.astype(BF16)
    return hi, (x - hi.astype(F32)).astype(BF16)


def _floor_bf16(x):
    bits = pltpu.bitcast(x, jnp.int32)
    down = jnp.where(bits >= 0, bits, bits + 0xFFFF) & jnp.int32(-65536)
    return pltpu.bitcast(down, F32).astype(BF16)


def _sigmoid(x):
    return 1.0 / (1.0 + jnp.exp(-x))


def _silu(x):
    return x * _sigmoid(x)


def _softplus(x):
    return jnp.maximum(x, 0.0) + jnp.log1p(jnp.exp(-jnp.abs(x)))


def _rms(x, g):
    return x * lax.rsqrt(jnp.mean(x * x, axis=-1, keepdims=True) + EPS) * g


def _iota(shape, dim):
    return lax.broadcasted_iota(jnp.int32, shape, dim)


def _ind(mask):
    return jnp.where(mask, 1.0, 0.0)


def _norm_matmul_kernel(x_ref, g_ref, w_ref, *rest, n_t, tiles32):
    if n_t:
        wt_ref, o32_ref, o16_ref, ot_ref, xn_ref = rest
    else:
        o32_ref, o16_ref, xn_ref = rest
    j = pl.program_id(1)

    @pl.when(j == 0)
    def _():
        xn_ref[...] = _rms(x_ref[...], g_ref[...]).astype(BF16)
        if n_t:
            ot_ref[...] = lax.dot_general(wt_ref[...], xn_ref[...], (((1,), (1,)), ((), ())),
                                          preferred_element_type=F32).astype(BF16)

    y = jnp.dot(xn_ref[...], w_ref[...], preferred_element_type=F32)

    @pl.when(j < tiles32)
    def _():
        o32_ref[...] = y

    @pl.when(j >= tiles32)
    def _():
        o16_ref[...] = y.astype(BF16)


def _norm_matmul(x, g, w, *, tm, tn, n32, w_t=None):
    t, d = x.shape
    n = w.shape[1]
    n_t = 0 if w_t is None else w_t.shape[0]
    tiles32 = n32 // tn
    assert tiles32 * tn == n32 and (n - n32) % tn == 0 and 0 < n32 < n
    in_specs = [pl.BlockSpec((tm, d), lambda i, j: (i, 0)),
                pl.BlockSpec((1, d), lambda i, j: (0, 0)),
                pl.BlockSpec((d, tn), lambda i, j: (0, j))]
    out_specs = [pl.BlockSpec((tm, tn), lambda i, j: (i, jnp.minimum(j, tiles32 - 1))),
                 pl.BlockSpec((tm, tn), lambda i, j: (i, jnp.maximum(j - tiles32, 0)))]
    out_shape = [jax.ShapeDtypeStruct((t, n32), F32), jax.ShapeDtypeStruct((t, n - n32), BF16)]
    args = [x, g.reshape(1, d), w]
    if n_t:
        in_specs.append(pl.BlockSpec((n_t, d), lambda i, j: (0, 0)))
        out_specs.append(pl.BlockSpec((n_t, tm), lambda i, j: (0, i)))
        out_shape.append(jax.ShapeDtypeStruct((n_t, t), BF16))
        args.append(w_t)
    return pl.pallas_call(
        functools.partial(_norm_matmul_kernel, n_t=n_t, tiles32=tiles32),
        grid=(t // tm, n // tn),
        in_specs=in_specs,
        out_specs=out_specs,
        out_shape=out_shape,
        scratch_shapes=[pltpu.VMEM((tm, d), BF16)],
        compiler_params=pltpu.CompilerParams(
            dimension_semantics=("parallel", "arbitrary"), vmem_limit_bytes=VMEM_LIMIT),
        name="norm_matmul",
    )(*args)


def _mix_ffn_kernel(ca_ref, cb_ref, wa_ref, wb_ref, h_ref, gmix_ref, gpre_ref, gpost_ref,
                    wg_ref, wu_ref, wd_ref, o_ref, h1_ref, xn_ref, acc_ref):
    f = pl.program_id(1)

    @pl.when(f == 0)
    def _():
        y = (jnp.dot(ca_ref[...], wa_ref[...], preferred_element_type=F32)
             + jnp.dot(cb_ref[...], wb_ref[...], preferred_element_type=F32))
        h1 = h_ref[...] + _rms(y, gmix_ref[...])
        h1_ref[...] = h1
        xn_ref[...] = _rms(h1, gpre_ref[...]).astype(BF16)
        acc_ref[...] = jnp.zeros_like(acc_ref)

    xn = xn_ref[...]
    gate = jnp.dot(xn, wg_ref[...], preferred_element_type=F32)
    up = jnp.dot(xn, wu_ref[...], preferred_element_type=F32)
    act = (_silu(gate) * up).astype(BF16)
    acc_ref[...] += jnp.dot(act, wd_ref[...], preferred_element_type=F32)

    @pl.when(f == pl.num_programs(1) - 1)
    def _():
        o_ref[...] = h1_ref[...] + _rms(acc_ref[...], gpost_ref[...])


def _mix_ffn(ca, cb, w_out, h, g_mix, g_pre, g_post, wg, wu, wd, *, tm, tf):
    t, d = h.shape
    ff = wg.shape[1]
    wa_n = ca.shape[1]
    wb_n = cb.shape[1]
    row = pl.BlockSpec((1, d), lambda i, f: (0, 0))
    once = dict(pipeline_mode=pl.Buffered(1)) if tf == ff else {}
    return pl.pallas_call(
        _mix_ffn_kernel,
        grid=(t // tm, ff // tf),
        in_specs=[pl.BlockSpec((tm, wa_n), lambda i, f: (i, 0)),
                  pl.BlockSpec((tm, wb_n), lambda i, f: (i, 0)),
                  pl.BlockSpec((wa_n, d), lambda i, f: (0, 0)),
                  pl.BlockSpec((wb_n, d), lambda i, f: (0, 0)),
                  pl.BlockSpec((tm, d), lambda i, f: (i, 0)),
                  row, row, row,
                  pl.BlockSpec((d, tf), lambda i, f: (0, f), **once),
                  pl.BlockSpec((d, tf), lambda i, f: (0, f), **once),
                  pl.BlockSpec((tf, d), lambda i, f: (f, 0), **once)],
        out_specs=pl.BlockSpec((tm, d), lambda i, f: (i, 0)),
        out_shape=jax.ShapeDtypeStruct((t, d), F32),
        scratch_shapes=[pltpu.VMEM((tm, d), F32), pltpu.VMEM((tm, d), BF16), pltpu.VMEM((tm, d), F32)],
        compiler_params=pltpu.CompilerParams(
            dimension_semantics=("parallel", "arbitrary"), vmem_limit_bytes=VMEM_LIMIT),
        name="mix_ffn",
    )(ca, cb, w_out[:wa_n].astype(BF16), w_out[wa_n:].astype(BF16), h,
      g_mix.reshape(1, d), g_pre.reshape(1, d), g_post.reshape(1, d),
      wg.astype(BF16), wu.astype(BF16), wd.astype(BF16))


def _deltanet_kernel(xq_ref, xk_ref, xv_ref, z_ref, sm_ref, cwq_ref, cwk_ref, cwv_ref,
                     alog_ref, dtb_ref, gn_ref, o_ref,
                     xpad_ref, q_ref, k_ref, v_ref, gb_ref, bb_ref, u_ref, w_ref, qk_ref, st_ref,
                     *, ts, a_col, b_col):
    s = pl.program_id(1)
    c = CHUNK
    d = HEAD_DIM
    nh = N_HEADS

    @pl.when(s == 0)
    def _():
        xpad_ref[:, 0:8, :] = jnp.zeros((3, 8, nh * d), F32)
        st_ref[...] = jnp.zeros_like(st_ref)

    @pl.when(s != 0)
    def _():
        xpad_ref[:, 0:8, :] = xpad_ref[:, ts:ts + 8, :]

    xpad_ref[0, 8:ts + 8, :] = xq_ref[...]
    xpad_ref[1, 8:ts + 8, :] = xk_ref[...]
    xpad_ref[2, 8:ts + 8, :] = xv_ref[...]

    def conv_silu(idx, cw_ref, hs):
        cw = cw_ref[:, hs]
        acc = xpad_ref[idx, 8 - (CONV_WIDTH - 1):8 - (CONV_WIDTH - 1) + ts, hs] * cw[0:1, :]
        for j in range(1, CONV_WIDTH):
            off = 8 - (CONV_WIDTH - 1) + j
            acc = acc + xpad_ref[idx, off:off + ts, hs] * cw[j:j + 1, :]
        return _silu(acc)

    def l2norm(t):
        return t * lax.rsqrt(jnp.sum(t * t, axis=-1, keepdims=True) + EPS)

    row = _iota((c, c), 0)
    col = _iota((c, c), 1)
    tri = (col <= row)
    strict = (col < row)
    tri_f = tri.astype(F32)
    upper_f = (row <= col).astype(F32)
    eye = (row == col).astype(F32)
    gnorm = gn_ref[...]
    chunks = range(ts // c)
    rs = [slice(ci * c, (ci + 1) * c) for ci in chunks]
    tri2 = jnp.concatenate([tri_f, tri_f], axis=1).astype(BF16)
    ones2 = jnp.ones((c, 2 * c), BF16)

    def cum2(lhs2, x):
        hi, lo = _split(x)
        return jnp.dot(lhs2, jnp.concatenate([hi, lo], axis=0), preferred_element_type=F32)

    for hh in range(nh):
        hs = slice(hh * d, (hh + 1) * d)
        q_ref[:, hs] = l2norm(conv_silu(0, cwq_ref, hs)) * (d ** -0.5)
        k_ref[:, hs] = l2norm(conv_silu(1, cwk_ref, hs))
        v_ref[:, hs] = conv_silu(2, cwv_ref, hs)

        a_raw = sm_ref[:, a_col + hh:a_col + hh + 1]
        b_raw = sm_ref[:, b_col + hh:b_col + hh + 1]
        g = -jnp.exp(alog_ref[:, hh:hh + 1]) * _softplus(a_raw + dtb_ref[:, hh:hh + 1])
        gb_ref[:, hs] = jnp.broadcast_to(g, (ts, d))
        bb_ref[:, hs] = jnp.broadcast_to(_sigmoid(b_raw), (ts, d))

        q = [q_ref[r, hs] for r in rs]
        k = [k_ref[r, hs] for r in rs]
        beta = [bb_ref[r, hs] for r in rs]
        gb = [gb_ref[r, hs] for r in rs]
        gc = [cum2(tri2, x) for x in gb]
        gc_row = [cum2(ones2, x[:, :c] * upper_f) for x in gb]
        decay = [jnp.where(tri, jnp.exp(jnp.minimum(a[:, :c] - b, 0.0)), 0.0) for a, b in zip(gc, gc_row)]
        kk = [_mm_nt(x, x) for x in k]
        n = [-jnp.where(strict, b[:, :c] * x * dc, 0.0) for b, x, dc in zip(beta, kk, decay)]
        inv = [eye + x for x in n]
        for step in range(5):
            nb = [x.astype(BF16) for x in n]
            n = [jnp.dot(x, x, preferred_element_type=F32) for x in nb]
            inv = [iv + _mm(iv, x) for iv, x in zip(inv, n)]
        egc = [jnp.exp(x) for x in gc]
        gl = [x[c - 1:c, :] for x in gc]
        inv_l = [x.astype(BF16) for x in inv]
        u = [_mm(a, v_ref[r, hs] * b) for a, r, b in zip(inv_l, rs, beta)]
        w = [_mm(a, x * (b * e)) for a, x, b, e in zip(inv_l, k, beta, egc)]
        qk = [_mm_nt(a, b) * dc for a, b, dc in zip(q, k, decay)]
        for ci in chunks:
            r = rs[ci]
            u_ref[r, hs] = u[ci]
            w_ref[r, hs] = w[ci]
            qk_ref[hh, r, :] = qk[ci]
            q_ref[r, hs] = q[ci] * egc[ci]
            k_ref[r, hs] = k[ci] * jnp.exp(gl[ci] - gc[ci])
            gb_ref[r, hs] = jnp.broadcast_to(jnp.exp(gl[ci]), (c, d))

    def chunk_body(ci, carry):
        r0 = pl.multiple_of(ci * c, c)
        rows = pl.ds(r0, c)
        hss = [slice(hh * d, (hh + 1) * d) for hh in range(nh)]
        st = [st_ref[hh] for hh in range(nh)]
        w_st = [_mm(w_ref[rows, hs], s_) for hs, s_ in zip(hss, st)]
        q_st = [_mm(q_ref[rows, hs], s_) for hs, s_ in zip(hss, st)]
        v_new = [u_ref[rows, hs] - x for hs, x in zip(hss, w_st)]
        o = [a + _mm(qk_ref[hh, rows, :], v) for hh, (a, v) in enumerate(zip(q_st, v_new))]
        kv = [_mm_tn(k_ref[rows, hs], v) for hs, v in zip(hss, v_new)]
        for hh, hs in enumerate(hss):
            st_ref[hh] = st[hh] * gb_ref[pl.ds(r0, 1), hs] + kv[hh]
            o_ref[rows, hs] = (_rms(o[hh], gnorm) * _silu(z_ref[rows, hs])).astype(o_ref.dtype)
        return carry

    lax.fori_loop(0, ts // c, chunk_body, 0)


def _deltanet(p32, conv_w, a_log, dt_bias, a_norm_g, *, ts, cols):
    bsz, s, _ = p32.shape
    d = HEAD_DIM
    nh = N_HEADS
    w = nh * d
    pad = lambda t: jnp.pad(t.astype(F32), (0, d - t.shape[0])).reshape(1, d)
    kernel = functools.partial(_deltanet_kernel, ts=ts, a_col=cols["a_lane"], b_col=cols["b_lane"])
    tile = lambda name: pl.BlockSpec((None, ts, w), lambda b, i: (b, i, cols[name] // nh))
    conv = lambda k: pl.BlockSpec((CONV_WIDTH, w), lambda b, i: (0, k))
    row = pl.BlockSpec((1, d), lambda b, i: (0, 0))
    return pl.pallas_call(
        kernel,
        grid=(bsz, s // ts),
        in_specs=[tile("qa"), tile("ka"), tile("va"), tile("za"),
                  pl.BlockSpec((None, ts, d), lambda b, i: (b, i, cols["small"])),
                  conv(0), conv(1), conv(2), row, row, row],
        out_specs=pl.BlockSpec((None, ts, w), lambda b, i: (b, i, 0)),
        out_shape=jax.ShapeDtypeStruct((bsz, s, w), BF16),
        scratch_shapes=[pltpu.VMEM((3, ts + 8, w), F32)]
        + [pltpu.VMEM((ts, w), F32) for _ in range(7)]
        + [pltpu.VMEM((nh, ts, CHUNK), F32), pltpu.VMEM((nh, d, d), F32)],
        compiler_params=pltpu.CompilerParams(
            dimension_semantics=("parallel", "arbitrary"), vmem_limit_bytes=VMEM_LIMIT),
        name="deltanet",
    )(p32, p32, p32, p32, p32, conv_w.astype(F32), conv_w.astype(F32), conv_w.astype(F32),
      pad(a_log), pad(dt_bias), a_norm_g.astype(F32).reshape(1, d))


def _hgrn2_kernel(q_ref, f_ref, i_ref, gate_ref, lb_ref, gn_ref, o_ref,
                  qs_ref, ks_ref, gc_ref, st_ref, *, ts):
    s = pl.program_id(1)
    c = CHUNK
    d = HEAD_DIM
    nh = N_HEADS
    SUB = 16

    @pl.when(s == 0)
    def _():
        st_ref[...] = jnp.zeros_like(st_ref)

    lb = lb_ref[...]
    f_raw = f_ref[...]
    log_sig = jnp.minimum(f_raw, 0.0) - jnp.log1p(jnp.exp(-jnp.abs(f_raw)))
    la = jnp.log(lb)
    lbb = jnp.log1p(-lb) + log_sig
    log_f = jnp.maximum(la, lbb) + jnp.log1p(jnp.exp(-jnp.abs(la - lbb)))
    qs_ref[...] = _silu(q_ref[...])
    ks_ref[...] = (1.0 - lb) * _sigmoid(-f_raw)

    row = _iota((c, c), 0)
    col = _iota((c, c), 1)
    tri_f = (col <= row).astype(F32)
    ones_dd = jnp.ones((d, d), BF16)
    rows_8d = _iota((8, d), 0)
    gnorm = gn_ref[...]

    tri2 = jnp.concatenate([tri_f, tri_f], axis=1).astype(BF16)
    for ci in range(ts // c):
        hi, lo = _split(log_f[ci * c:(ci + 1) * c, :])
        gc_ref[ci * c:(ci + 1) * c, :] = jnp.dot(tri2, jnp.concatenate([hi, lo], axis=0),
                                                 preferred_element_type=F32)

    blocks = [(sb * SUB, (sb + 1) * SUB) for sb in range(c // SUB)]

    def chunk_loop(ci, carry):
        r0 = pl.multiple_of(ci * c, c)
        rows = pl.ds(r0, c)
        hss = [slice(hh * d, (hh + 1) * d) for hh in range(nh)]
        q = [qs_ref[rows, hs] for hs in hss]
        k = [ks_ref[rows, hs] for hs in hss]
        v = [i_ref[rows, hs] for hs in hss]
        gc = [gc_ref[rows, hs] for hs in hss]

        def near_products(q, k, gc):
            prods = []
            for top, end in blocks:
                for j in range(top, end):
                    lo = (j // 8) * 8
                    e = jnp.exp2(gc[lo:end, :] - gc[j:j + 1, :])
                    if j % 8:
                        head = jnp.where(rows_8d >= j - lo, e[:8], 0.0)
                        e = jnp.concatenate([head, e[8:]], axis=0) if lo + 8 < end else head
                    prods.append(q[lo:end, :] * k[j:j + 1, :] * e)
            return jnp.concatenate(prods, axis=0).astype(BF16)

        def far_operands(q, k, gc):
            out = []
            for top, end in blocks[1:]:
                g_b = gc[top - 1:top, :]
                out.append((q[top:end, :] * jnp.exp(gc[top:end, :] - g_b),
                            k[:top, :] * jnp.exp(jnp.minimum(g_b - gc[:top, :], 0.0))))
            return out

        near = [near_products(a, b, g * LOG2E) for a, b, g in zip(q, k, gc)]
        far_ops = [far_operands(*x) for x in zip(q, k, gc)]
        st = [st_ref[hh] for hh in range(nh)]
        gl = [x[c - 1:c, :] for x in gc]
        sums = [jnp.dot(x, ones_dd, preferred_element_type=F32) for x in near]
        qk_far = [[_mm_nt(qe, ke) for qe, ke in ops] for ops in far_ops]
        far = [[_mm(a, vv[:top, :]) for a, (top, _) in zip(qs, blocks[1:])] for qs, vv in zip(qk_far, v)]
        o_st = [_mm_nt(a * jnp.exp(g), s_) for a, g, s_ in zip(q, gc, st)]
        kv = [_mm_tn(vv, kk * jnp.exp(g_l - g)) for vv, kk, g_l, g in zip(v, k, gl, gc)]

        for hh, hs in enumerate(hss):
            groups = [jnp.zeros((8, d), F32) for _ in range(c // 8)]
            at = 0
            for top, end in blocks:
                for j in range(top, end):
                    v_j = v[hh][j:j + 1, :]
                    for g in range(j // 8, end // 8):
                        groups[g] = groups[g] + sums[hh][at:at + 8, :] * v_j
                        at += 8
            for f, (top, end) in zip(far[hh], blocks[1:]):
                for g in range(top // 8, end // 8):
                    groups[g] = groups[g] + f[(g * 8 - top):(g * 8 - top + 8), :]
            o = jnp.concatenate(groups, axis=0) + o_st[hh]
            st_ref[hh] = st[hh] * jnp.exp(gl[hh]) + kv[hh]
            o_ref[rows, hs] = (_rms(o, gnorm) * _silu(gate_ref[rows, hs])).astype(o_ref.dtype)
        return carry

    lax.fori_loop(0, ts // c, chunk_loop, 0)


def _hgrn2(p32, lb, d_norm_g, *, ts, cols):
    bsz, s, _ = p32.shape
    d = HEAD_DIM
    nh = N_HEADS
    w = nh * d
    kernel = functools.partial(_hgrn2_kernel, ts=ts)
    tile = lambda name: pl.BlockSpec((None, ts, w), lambda b, i: (b, i, cols[name] // nh))
    return pl.pallas_call(
        kernel,
        grid=(bsz, s // ts),
        in_specs=[tile("qd"), tile("fd"), tile("id"), tile("gd"),
                  pl.BlockSpec((1, w), lambda b, i: (0, 0)),
                  pl.BlockSpec((1, d), lambda b, i: (0, 0))],
        out_specs=pl.BlockSpec((None, ts, w), lambda b, i: (b, i, 0)),
        out_shape=jax.ShapeDtypeStruct((bsz, s, w), BF16),
        scratch_shapes=[pltpu.VMEM((ts, w), F32), pltpu.VMEM((ts, w), F32),
                        pltpu.VMEM((ts, w), F32), pltpu.VMEM((nh, d, d), F32)],
        compiler_params=pltpu.CompilerParams(
            dimension_semantics=("parallel", "arbitrary"), vmem_limit_bytes=VMEM_LIMIT),
        name="hgrn2",
    )(p32, p32, p32, p32, lb.astype(F32).reshape(1, w), d_norm_g.astype(F32).reshape(1, d))


def _stickbreak_kernel(q_ref, k_ref, v_ref, o_ref, acc_ref, *, tq):
    i = pl.program_id(1)
    d = HEAD_DIM
    nh = N_HEADS
    row = _iota((tq, tq), 0)
    col = _iota((tq, tq), 1)
    causal = col < row
    later = (row > col).astype(BF16)
    later2 = jnp.concatenate([later, later], axis=0)

    heads = [slice(hh * d, (hh + 1) * d) for hh in range(nh)]

    def scores(blocks):
        jobs = [(j, dg, hs) for j, dg in blocks for hs in heads]
        z = [_mm_nt(q_ref[:, hs], k_ref[pl.ds(pl.multiple_of(j * tq, tq), tq), hs]) * (d ** -0.5)
             for j, _, hs in jobs]
        sp = [_softplus(x) for x in z]
        l1m = [jnp.where(causal, -x, 0.0) if dg else -x for x, (_, dg, _) in zip(sp, jobs)]
        rest = [jnp.dot(jnp.concatenate(_split(x), axis=1), later2, preferred_element_type=F32)
                for x in l1m]
        out = [((a - b) + r, l) for a, b, r, l in zip(z, sp, rest, l1m)]
        return [out[b * nh:(b + 1) * nh] for b in range(len(blocks))]

    def block(j, carries):
        (sc,) = scores([(j, False)])
        ps = [jnp.exp(logw + c) for (logw, _), c in zip(sc, carries)]
        pv = [_mm(p, v_ref[pl.ds(pl.multiple_of(j * tq, tq), tq), hs]) for p, hs in zip(ps, heads)]
        for hs, x in zip(heads, pv):
            acc_ref[:, hs] += x
        return tuple(c + jnp.sum(l1m, axis=-1, keepdims=True) for (_, l1m), c in zip(sc, carries))

    jp = jnp.maximum(i - 1, 0)
    live = jnp.where(i > 0, 1.0, 0.0)
    sd, sp_ = scores([(i, True), (jp, False)])
    carries = []
    for hh, hs in enumerate(heads):
        c1 = jnp.sum(sd[hh][1], axis=-1, keepdims=True)
        p_d = jnp.where(causal, jnp.exp(sd[hh][0]), 0.0)
        p_p = jnp.exp(sp_[hh][0] + c1) * live
        acc_ref[:, hs] = (_mm(p_d, v_ref[pl.ds(pl.multiple_of(i * tq, tq), tq), hs])
                          + _mm(p_p, v_ref[pl.ds(pl.multiple_of(jp * tq, tq), tq), hs]))
        carries.append(c1 + jnp.sum(sp_[hh][1], axis=-1, keepdims=True))
    carries = tuple(carries)

    def cond(c):
        worst = functools.reduce(jnp.maximum, c[1])
        return jnp.logical_and(c[0] >= 0, jnp.max(worst) >= EXP_ZERO_BELOW)

    def body(c):
        return c[0] - 1, block(c[0], c[1])

    lax.while_loop(cond, body, (i - 2, carries))
    o_ref[...] = acc_ref[...].astype(o_ref.dtype)


def _stickbreak(p16, *, tq, cols):
    bsz, s, _ = p16.shape
    nh = N_HEADS
    w = nh * HEAD_DIM
    kernel = functools.partial(_stickbreak_kernel, tq=tq)
    resident = dict(pipeline_mode=pl.Buffered(1))
    return pl.pallas_call(
        kernel,
        grid=(bsz, s // tq),
        in_specs=[pl.BlockSpec((None, tq, w), lambda b, i: (b, i, cols["qc"] // nh)),
                  pl.BlockSpec((None, s, w), lambda b, i: (b, 0, cols["kc"] // nh), **resident),
                  pl.BlockSpec((None, s, w), lambda b, i: (b, 0, cols["vc"] // nh), **resident)],
        out_specs=pl.BlockSpec((None, tq, w), lambda b, i: (b, i, 0)),
        out_shape=jax.ShapeDtypeStruct((bsz, s, w), BF16),
        scratch_shapes=[pltpu.VMEM((tq, w), F32)],
        compiler_params=pltpu.CompilerParams(
            dimension_semantics=("parallel", "arbitrary"), vmem_limit_bytes=VMEM_LIMIT),
        name="stickbreak",
    )(p16, p16, p16)


def _dsa_kernel(qi_ref, smq_ref, q_ref, sm_ref, k_ref, vt_ref, bias_ref, o_ref,
                sc_ref, scb_ref, wb_ref, qc_ref, kct_ref, bd_ref, lg_ref, *, tq, k_sel, wi_lane, wide):
    i = pl.program_id(1)
    tk = tq
    d = HEAD_DIM
    nh = N_HEADS
    ksel = float(k_sel)
    per_wide = wide // tk
    n_wide = (i + per_wide) // per_wide
    sub = 2 * tk
    lane_q = _iota((1, tq), 1)

    def tree(parts, op):
        while len(parts) > 1:
            parts = [op(parts[j], parts[j + 1]) if j + 1 < len(parts) else parts[j]
                     for j in range(0, len(parts), 2)]
        return parts[0]

    def col_fold(x, op=jnp.add, rows=8):
        return tree([x[r * rows:(r + 1) * rows] for r in range(x.shape[0] // rows)], op)

    @pl.when(i == 0)
    def _():
        def prep(g, carry):
            g0 = pl.multiple_of(g * wide, wide)
            kt = sm_ref[pl.ds(g0, wide), :].T[:IDX_DIM, :]
            hi, lo = _split(kt)
            kct_ref[:, pl.ds(g0, wide)] = jnp.concatenate([hi, lo, hi], axis=0)
            return carry
        lax.fori_loop(0, sm_ref.shape[0] // wide, prep, 0)

    smq = smq_ref[...]
    lane = _iota(smq.shape, 1)
    for hh in range(IDX_HEADS):
        qh = qi_ref[:, hh * IDX_DIM:(hh + 1) * IDX_DIM]
        hi, lo = _split(qh)
        qc_ref[hh] = jnp.concatenate([hi, hi, lo], axis=-1)
        w = jnp.sum(jnp.where(lane == wi_lane + hh, smq, 0.0), axis=-1, keepdims=True)
        wb_ref[hh] = jnp.broadcast_to(w * ((IDX_HEADS ** -0.5) * (IDX_DIM ** -0.5)), (tq, tk))

    q2t = (q_ref[...] * ((d ** -0.5) * LOG2E)).T.astype(BF16)
    zero_dq = jnp.zeros((d, tq), BF16)
    for p in range(nh // 2):
        top = jnp.concatenate([q2t[2 * p * d:(2 * p + 1) * d], zero_dq], axis=1)
        bot = jnp.concatenate([zero_dq, q2t[(2 * p + 1) * d:(2 * p + 2) * d]], axis=1)
        bd_ref[p] = jnp.concatenate([top, bot], axis=0)

    limit = i * tq + (lane_q // CHUNK + 1) * CHUNK
    rows_t = _iota((tk, tq), 0)

    def score_group(g, mm, masked):
        mn, mx = mm
        for sb in range(wide // sub):
            k0 = pl.multiple_of(g * wide + sb * sub, sub)
            kct = kct_ref[:, pl.ds(k0, sub)]
            tiles = [jnp.zeros((tq, tk), F32) for _ in range(sub // tk)]
            for hh in range(IDX_HEADS):
                s_h = jnp.dot(qc_ref[hh], kct, preferred_element_type=F32)
                for ti in range(sub // tk):
                    tiles[ti] = tiles[ti] + jnp.maximum(s_h[:, ti * tk:(ti + 1) * tk], 0.0) * wb_ref[hh]
            for ti in range(sub // tk):
                kb = pl.multiple_of(k0 + ti * tk, tk)
                sct = tiles[ti].T
                if masked:
                    adm = (kb + rows_t) < limit
                    mn = jnp.minimum(mn, col_fold(jnp.where(adm, sct, jnp.inf), jnp.minimum))
                    sct = jnp.where(adm, sct, -jnp.inf)
                else:
                    mn = jnp.minimum(mn, col_fold(sct, jnp.minimum))
                mx = jnp.maximum(mx, col_fold(sct, jnp.maximum))
                sc_ref[pl.ds(kb, tk), :] = sct
                scb_ref[pl.ds(kb, tk), :] = _floor_bf16(sct)
        return mn, mx

    def score_pair(j, mm):
        return score_group(2 * j + 1, score_group(2 * j, mm, False), False)

    n_full = n_wide - 1
    mm = lax.fori_loop(0, n_full // 2, score_pair,
                       (jnp.full((8, tq), jnp.inf, F32), jnp.full((8, tq), -jnp.inf, F32)))
    mm = lax.cond(n_full % 2 == 1, lambda c: score_group(n_full - 1, c, False), lambda c: c, mm)
    mn, mx = score_group(n_wide - 1, mm, True)

    n_pairs = (n_wide + 1) // 2

    @pl.when(n_wide % 2 == 1)
    def _():
        sc_ref[pl.ds(pl.multiple_of(n_wide * wide, wide), wide), :] = jnp.full((wide, tq), -jnp.inf, F32)
        scb_ref[pl.ds(pl.multiple_of(n_wide * wide, wide), wide), :] = jnp.full((wide, tq), -jnp.inf, BF16)
    rmin = jnp.min(mn, axis=0, keepdims=True)
    rmax = jnp.max(mx, axis=0, keepdims=True)

    def count(pred):
        def body(j, acc):
            for g in (2 * j, 2 * j + 1):
                acc = acc + col_fold(pred(sc_ref[pl.ds(pl.multiple_of(g * wide, wide), wide), :]))
            return acc
        return jnp.sum(lax.fori_loop(0, n_pairs, body, jnp.zeros((8, tq), F32)), axis=0, keepdims=True)

    def max_below(x):
        def body(j, acc):
            for g in (2 * j, 2 * j + 1):
                blk = sc_ref[pl.ds(pl.multiple_of(g * wide, wide), wide), :]
                acc = jnp.maximum(acc, col_fold(jnp.where(blk < x, blk, -jnp.inf), jnp.maximum))
            return acc
        return jnp.max(lax.fori_loop(0, n_pairs, body, jnp.full((8, tq), -jnp.inf, F32)), axis=0, keepdims=True)

    n_adm = limit.astype(F32)
    all_sel = n_adm <= ksel

    def bisect(c):
        lo, hi, c_lo = c
        mid = 0.5 * lo + 0.5 * hi
        cm = count(lambda blk: _ind(blk >= mid))
        ge = cm >= ksel
        return jnp.where(ge, mid, lo), jnp.where(ge, hi, mid), jnp.where(ge, cm, c_lo)

    def pending(c_lo, tied):
        return jnp.where(all_sel, 0.0, jnp.where(tied > 0.5, 0.0, _ind(c_lo != ksel)))

    def bisect_coarse(_, c):
        lo, hi, c_lo = c
        mid = _floor_bf16(0.5 * lo + 0.5 * hi).astype(F32)
        t_b = jnp.broadcast_to(mid, (16, tq)).astype(BF16)
        one_b = jnp.ones((16, tq), BF16)
        zero_b = jnp.zeros((16, tq), BF16)

        def body(j, acc):
            for g in (2 * j, 2 * j + 1):
                blk = scb_ref[pl.ds(pl.multiple_of(g * wide, wide), wide), :]
                ind = [jnp.where(blk[r * 16:(r + 1) * 16] >= t_b, one_b, zero_b) for r in range(wide // 16)]
                acc = acc + tree(ind, jnp.add).astype(F32)
            return acc

        acc = lax.fori_loop(0, n_pairs, body, jnp.zeros((16, tq), F32))
        cm = jnp.sum(acc, axis=0, keepdims=True)
        ge = cm >= ksel
        return jnp.where(ge, mid, lo), jnp.where(ge, hi, mid), jnp.where(ge, cm, c_lo)

    lo0 = _floor_bf16(rmin).astype(F32)
    hi0 = _floor_bf16(rmax + (jnp.abs(rmax) * (2.0 ** -6) + 1e-30)).astype(F32)
    state = lax.fori_loop(0, BISECT_COARSE, bisect_coarse, (lo0, hi0, n_adm))
    state = lax.fori_loop(0, BISECT_FIXED, lambda _, c: bisect(c), state)

    def round_cond(c):
        return jnp.max(pending(c[0][2], c[1])) > 0.5

    def round_body(c):
        st, tied, v, need = c

        def more_cond(s):
            return jnp.logical_and(s[0] < BISECT_EXTRA, jnp.max(pending(s[1][2], tied)) > 0.5)

        _, st = lax.while_loop(more_cond, lambda s: (s[0] + 1, bisect(s[1])), (jnp.int32(0), st))
        pend = pending(st[2], tied)

        def check(_):
            cand = max_below(st[1])
            c_ge = count(lambda blk: _ind(blk >= cand))
            c_gt = count(lambda blk: _ind(blk > cand))
            ok = jnp.where(pend > 0.5, _ind(c_ge >= ksel), 0.0)
            return (jnp.where(ok > 0.5, 1.0, tied), jnp.where(ok > 0.5, cand, v),
                    jnp.where(ok > 0.5, ksel - c_gt, need))

        tied, v, need = lax.cond(jnp.max(pend) > 0.5, check, lambda _: (tied, v, need), 0)
        return st, tied, v, need

    zeros1 = jnp.zeros((1, tq), F32)
    (lo_f, _, _), tied, v_tie, need = lax.while_loop(round_cond, round_body, (state, zeros1, zeros1, zeros1))
    vth = jnp.where(all_sel, F32_LOWEST, jnp.where(tied > 0.5, v_tie, lo_f))

    @pl.when(jnp.max(tied) > 0.5)
    def _():
        v_eq = jnp.where(tied > 0.5, v_tie, jnp.inf)
        incl = (_iota((tk, tk), 1) <= _iota((tk, tk), 0)).astype(BF16)

        def demote(g, seen):
            g0 = pl.multiple_of(g * wide, wide)
            xs = [sc_ref[pl.ds(g0 + pb * tk, tk), :] for pb in range(per_wide)]
            eqs = [_ind(x == v_eq) for x in xs]
            inblk = [jnp.dot(incl, e.astype(BF16), preferred_element_type=F32) for e in eqs]
            for pb in range(per_wide):
                rank = inblk[pb] + seen
                sc_ref[pl.ds(g0 + pb * tk, tk), :] = jnp.where(eqs[pb] * _ind(rank > need) > 0.5,
                                                               -jnp.inf, xs[pb])
                seen = seen + jnp.sum(col_fold(eqs[pb]), axis=0, keepdims=True)
            return seen

        lax.fori_loop(0, n_wide, demote, zeros1)

    g_near = jnp.maximum(i - 1, 0) // per_wide

    def logit_group(g, mx, near):
        out = list(mx)
        for sb in range(wide // sub):
            k0 = pl.multiple_of(g * wide + sb * sub, sub)
            sel = sc_ref[pl.ds(k0, sub), :] >= vth
            for p in range(nh // 2):
                pair = jnp.dot(k_ref[pl.ds(k0, sub), 2 * p * d:(2 * p + 2) * d], bd_ref[p],
                               preferred_element_type=F32)
                for hh in (2 * p, 2 * p + 1):
                    lm = pair[:, (hh - 2 * p) * tq:(hh - 2 * p + 1) * tq]
                    if near:
                        back = [jnp.clip(i - (g * per_wide + sb * (sub // tk) + pb), 0, 2)
                                for pb in range(sub // tk)]
                        lm = lm + jnp.concatenate([bias_ref[bk, hh] for bk in back], axis=0)
                    lm = jnp.where(sel, lm, NEG_BIG)
                    lg_ref[hh, pl.ds(k0, sub), :] = lm
                    out[hh] = jnp.maximum(out[hh], col_fold(lm, jnp.maximum))
        return tuple(out)

    mx = tuple(jnp.full((8, tq), NEG_BIG, F32) for _ in range(nh))
    def logit_pair(j, mx, near):
        return logit_group(2 * j + 1, logit_group(2 * j, mx, near), near)

    far_pairs = g_near // 2
    mx = lax.fori_loop(0, far_pairs, functools.partial(logit_pair, near=False), mx)
    mx = lax.fori_loop(far_pairs, n_pairs, functools.partial(logit_pair, near=True), mx)
    m_q = [jnp.max(mx[hh], axis=0, keepdims=True) for hh in range(nh)]

    ones_rows = jnp.ones((8, wide), BF16)

    def pv_pair(j, carry):
        ls, accs = list(carry[0]), list(carry[1])
        jobs = [(pl.multiple_of(g * wide, wide), hh) for g in (2 * j, 2 * j + 1) for hh in range(nh)]
        ps = [jnp.exp2(lg_ref[hh, pl.ds(g0, wide), :] - m_q[hh]).astype(BF16) for g0, hh in jobs]
        outs = [jnp.dot(jnp.concatenate([vt_ref[hh * d:(hh + 1) * d, pl.ds(g0, wide)], ones_rows], axis=0),
                        p, preferred_element_type=F32) for (g0, hh), p in zip(jobs, ps)]
        for (_, hh), out in zip(jobs, outs):
            ls[hh] = ls[hh] + out[d:]
            accs[hh] = accs[hh] + out[:d]
        return tuple(ls), tuple(accs)

    ls, accs = lax.fori_loop(0, n_pairs, pv_pair,
                             (tuple(jnp.zeros((8, tq), F32) for _ in range(nh)),
                              tuple(jnp.zeros((d, tq), F32) for _ in range(nh))))
    for hh in range(nh):
        o_ref[:, hh * d:(hh + 1) * d] = (accs[hh] / ls[hh][0:1]).T.astype(o_ref.dtype)


def _dsa(p32, p16, vt, bias_tiles, *, tq, cols):
    bsz, s, _ = p32.shape
    d = HEAD_DIM
    nh = N_HEADS
    wide = 4 * tq
    k_sel = min(TOPK_MAX, s // 4)
    w512 = nh * d
    kernel = functools.partial(_dsa_kernel, tq=tq, k_sel=k_sel, wi_lane=cols["wi_lane"], wide=wide)
    resident = dict(pipeline_mode=pl.Buffered(1))
    return pl.pallas_call(
        kernel,
        grid=(bsz, s // tq),
        in_specs=[pl.BlockSpec((None, tq, w512), lambda b, i: (b, i, cols["qi"] // nh)),
                  pl.BlockSpec((None, tq, d), lambda b, i: (b, i, cols["small"])),
                  pl.BlockSpec((None, tq, w512), lambda b, i: (b, i, cols["qb"] // nh)),
                  pl.BlockSpec((None, s, d), lambda b, i: (b, 0, cols["small"]), **resident),
                  pl.BlockSpec((None, s, w512), lambda b, i: (b, 0, cols["kb"] // nh), **resident),
                  pl.BlockSpec((w512, s), lambda b, i: (0, b), **resident),
                  pl.BlockSpec((3, nh, tq, tq), lambda b, i: (0, 0, 0, 0), **resident)],
        out_specs=pl.BlockSpec((None, tq, w512), lambda b, i: (b, i, 0)),
        out_shape=jax.ShapeDtypeStruct((bsz, s, w512), BF16),
        scratch_shapes=[pltpu.VMEM((s, tq), F32),
                        pltpu.VMEM((s, tq), BF16),
                        pltpu.VMEM((IDX_HEADS, tq, tq), F32),
                        pltpu.VMEM((IDX_HEADS, tq, 3 * IDX_DIM), BF16),
                        pltpu.VMEM((3 * IDX_DIM, s), BF16),
                        pltpu.VMEM((nh // 2, 2 * d, 2 * tq), BF16),
                        pltpu.VMEM((nh, s, tq), F32)],
        compiler_params=pltpu.CompilerParams(
            dimension_semantics=("parallel", "arbitrary"), vmem_limit_bytes=VMEM_LIMIT),
        name="dsa",
    )(p32, p32, p32, p32, p16, vt, bias_tiles)


def _t5_bucket(rel):
    nb = REL_BUCKETS // 2
    max_exact = nb // 2
    ret = jnp.where(rel > 0, nb, 0)
    n = jnp.abs(rel)
    large = max_exact + (jnp.log(jnp.maximum(n, 1).astype(F32) / max_exact)
                         / math.log(REL_MAX_DIST / max_exact) * (nb - max_exact)).astype(jnp.int32)
    large = jnp.minimum(large, nb - 1)
    return ret + jnp.where(n < max_exact, n, large)


def _bias_tiles(rel_table, tq):
    assert tq >= REL_MAX_DIST
    t = jnp.arange(tq)
    back = jnp.arange(3)
    rel = (t[None, None, :] - back[:, None, None] * tq) - t[None, :, None]
    onehot = (_t5_bucket(rel)[..., None] == jnp.arange(REL_BUCKETS)).astype(F32)
    tiles = jnp.einsum("bqkn,nh->bhkq", onehot, rel_table.astype(F32),
                       precision=HIGHEST)
    return (tiles - tiles[2:3]) * LOG2E


def _even_layout(w_in):
    d = HEAD_DIM
    a_w = 2 * N_HEADS * d + N_HEADS * d
    offs = {}
    o = 0
    for name, w in (("qkv", a_w), ("z", N_HEADS * d), ("a", N_HEADS), ("b", N_HEADS),
                    ("qb", N_HEADS * d), ("kb", N_HEADS * d), ("vb", N_HEADS * d),
                    ("qi", IDX_HEADS * IDX_DIM), ("ki", IDX_DIM), ("wi", IDX_HEADS)):
        offs[name] = (o, o + w)
        o += w
    assert o == w_in.shape[1]
    sl = lambda n: w_in[:, offs[n][0]:offs[n][1]]
    small_w = IDX_DIM + 2 * N_HEADS + IDX_HEADS
    small_pad = -small_w % d
    zeros = lambda n: jnp.zeros((w_in.shape[0], n), w_in.dtype)
    w32 = jnp.concatenate([sl("qkv"), sl("z"), sl("qb"), sl("qi"),
                           sl("ki"), sl("a"), sl("b"), sl("wi"), zeros(small_pad)], axis=1)
    n32 = w32.shape[1]
    tn = n32 // 5
    assert tn * 5 == n32 and tn % d == 0
    w16 = jnp.concatenate([sl("kb"), zeros(tn - N_HEADS * d)], axis=1)
    nh = N_HEADS
    cols = dict(qa=0, ka=nh, va=2 * nh, za=3 * nh, qb=4 * nh, qi=5 * nh, small=6 * nh, kb=0,
                a_lane=IDX_DIM, b_lane=IDX_DIM + nh, wi_lane=IDX_DIM + 2 * nh, n32=n32, tn=tn)
    return jnp.concatenate([w32, w16], axis=1).astype(BF16), sl("vb").T.astype(BF16), cols


def kernel(x, norm_g, w_in_even, conv_w_even, a_log_even, dt_bias_even, a_norm_even, w_out_even,
           rel_bias, w_in_odd, lb_logits, d_norm_odd, w_out_odd, w_gate, w_up, w_down):
    bsz, s, d = x.shape
    t = bsz * s
    depth = norm_g.shape[0]
    nh = N_HEADS
    tq = Q_TILE
    lb_all = jnp.cumsum(jax.nn.softmax(lb_logits.astype(F32), axis=0), axis=0)
    lb_all = lb_all - lb_all[:1]
    odd_cols = dict(qc=0, kc=nh, vc=2 * nh, qd=0, fd=nh, id=2 * nh, gd=3 * nh)
    bias_tiles = _bias_tiles(rel_bias, tq)

    h = x.reshape(t, d)
    for l in range(depth):
        if l % 2 == 0:
            e = l // 2
            w_even, w_vt, cols = _even_layout(w_in_even[e])
            p32, p16, vt = _norm_matmul(h, norm_g[l, 0], w_even, tm=PROJ_TILE, tn=cols["tn"], n32=cols["n32"],
                                        w_t=w_vt)
            p32 = p32.reshape(bsz, s, -1)
            p16 = p16.reshape(bsz, s, -1)
            o_1 = _deltanet(p32, conv_w_even[e], a_log_even[e], dt_bias_even[e], a_norm_even[e],
                            ts=min(SEQ_TILE, s), cols=cols)
            o_2 = _dsa(p32, p16, vt, bias_tiles, tq=tq, cols=cols)
            w_out = w_out_even[e]
        else:
            o = l // 2
            n16 = 3 * nh * HEAD_DIM
            w_odd = jnp.concatenate([w_in_odd[o][:, n16:], w_in_odd[o][:, :n16]], axis=1).astype(BF16)
            p32, p16 = _norm_matmul(h, norm_g[l, 0], w_odd, tm=PROJ_TILE, tn=ODD_COL_TILE, n32=w_odd.shape[1] - n16)
            p32 = p32.reshape(bsz, s, -1)
            p16 = p16.reshape(bsz, s, -1)
            o_1 = _stickbreak(p16, tq=tq, cols=odd_cols)
            o_2 = _hgrn2(p32, lb_all[l], d_norm_odd[o], ts=min(SEQ_TILE, s), cols=odd_cols)
            w_out = w_out_odd[o]
        h = _mix_ffn(o_1.reshape(t, -1), o_2.reshape(t, -1), w_out, h, norm_g[l, 1], norm_g[l, 2], norm_g[l, 3],
                     w_gate[l], w_up[l], w_down[l], tm=ROW_TILE, tf=FFN_TILE)
    return h.reshape(bsz, s, d)
```

```python
import functools
import math

import jax
import jax.numpy as jnp
from jax import lax
from jax.experimental import pallas as pl
from jax.experimental.pallas import tpu as pltpu

F32 = jnp.float32
BF16 = jnp.bfloat16
HIGHEST = lax.Precision.HIGHEST

CHUNK = 64
HEAD_DIM = 128
N_HEADS = 4
IDX_HEADS = 8
IDX_DIM = 64
TOPK_MAX = 256
CONV_WIDTH = 4
REL_BUCKETS = 32
REL_MAX_DIST = 128
EPS = 1e-6
NEG_BIG = -1e30
LOG2E = 1.4426950408889634
BISECT_COARSE = 12
BISECT_FIXED = 8
BISECT_EXTRA = 6
F32_LOWEST = -3.4028234663852886e38
EXP_ZERO_BELOW = -104.0
VMEM_LIMIT = 56 * 1024 * 1024

PROJ_TILE = 2048
ROW_TILE = 512
SEQ_TILE = 512
Q_TILE = 128
ODD_COL_TILE = 512
FFN_TILE = 2816


def _mm(a, b):
    return jnp.dot(a.astype(BF16), b.astype(BF16), preferred_element_type=F32)


def _mm_nt(a, b):
    return lax.dot_general(a.astype(BF16), b.astype(BF16), (((1,), (1,)), ((), ())),
                           preferred_element_type=F32)


def _mm_tn(a, b):
    return lax.dot_general(a.astype(BF16), b.astype(BF16), (((0,), (0,)), ((), ())),
                           preferred_element_type=F32)


def _split(x):
    hi = x.astype(BF16)
    return hi, (x - hi.astype(F32)).astype(BF16)


def _floor_bf16(x):
    bits = pltpu.bitcast(x, jnp.int32)
    down = jnp.where(bits >= 0, bits, bits + 0xFFFF) & jnp.int32(-65536)
    return pltpu.bitcast(down, F32).astype(BF16)


def _sigmoid(x):
    return 1.0 / (1.0 + jnp.exp(-x))


def _silu(x):
    return x * _sigmoid(x)


def _softplus(x):
    return jnp.maximum(x, 0.0) + jnp.log1p(jnp.exp(-jnp.abs(x)))


def _rms(x, g):
    return x * lax.rsqrt(jnp.mean(x * x, axis=-1, keepdims=True) + EPS) * g


def _iota(shape, dim):
    return lax.broadcasted_iota(jnp.int32, shape, dim)


def _ind(mask):
    return jnp.where(mask, 1.0, 0.0)


def _norm_matmul_kernel(x_ref, g_ref, w_ref, *rest, n_t, tiles32):
    if n_t:
        wt_ref, o32_ref, o16_ref, ot_ref, xn_ref = rest
    else:
        o32_ref, o16_ref, xn_ref = rest
    j = pl.program_id(1)

    @pl.when(j == 0)
    def _():
        xn_ref[...] = _rms(x_ref[...], g_ref[...]).astype(BF16)
        if n_t:
            ot_ref[...] = lax.dot_general(wt_ref[...], xn_ref[...], (((1,), (1,)), ((), ())),
                                          preferred_element_type=F32).astype(BF16)

    tn = o32_ref.shape[1]
    y = jnp.dot(xn_ref[...], w_ref[:, pl.ds(pl.multiple_of(j * tn, tn), tn)], preferred_element_type=F32)

    @pl.when(j < tiles32)
    def _():
        o32_ref[...] = y

    @pl.when(j >= tiles32)
    def _():
        o16_ref[...] = y.astype(BF16)


def _norm_matmul(x, g, w, *, tm, tn, n32, w_t=None):
    t, d = x.shape
    n = w.shape[1]
    n_t = 0 if w_t is None else w_t.shape[0]
    tiles32 = n32 // tn
    assert tiles32 * tn == n32 and (n - n32) % tn == 0 and 0 < n32 < n
    in_specs = [pl.BlockSpec((tm, d), lambda i, j: (i, 0)),
                pl.BlockSpec((1, d), lambda i, j: (0, 0)),
                pl.BlockSpec((d, n), lambda i, j: (0, 0), pipeline_mode=pl.Buffered(1))]
    out_specs = [pl.BlockSpec((tm, tn), lambda i, j: (i, jnp.minimum(j, tiles32 - 1))),
                 pl.BlockSpec((tm, tn), lambda i, j: (i, jnp.maximum(j - tiles32, 0)))]
    out_shape = [jax.ShapeDtypeStruct((t, n32), F32), jax.ShapeDtypeStruct((t, n - n32), BF16)]
    args = [x, g.reshape(1, d), w]
    if n_t:
        in_specs.append(pl.BlockSpec((n_t, d), lambda i, j: (0, 0)))
        out_specs.append(pl.BlockSpec((n_t, tm), lambda i, j: (0, i)))
        out_shape.append(jax.ShapeDtypeStruct((n_t, t), BF16))
        args.append(w_t)
    return pl.pallas_call(
        functools.partial(_norm_matmul_kernel, n_t=n_t, tiles32=tiles32),
        grid=(t // tm, n // tn),
        in_specs=in_specs,
        out_specs=out_specs,
        out_shape=out_shape,
        scratch_shapes=[pltpu.VMEM((tm, d), BF16)],
        compiler_params=pltpu.CompilerParams(
            dimension_semantics=("parallel", "arbitrary"), vmem_limit_bytes=VMEM_LIMIT),
        name="norm_matmul",
    )(*args)


def _mix_ffn_kernel(ca_ref, cb_ref, wa_ref, wb_ref, h_ref, gmix_ref, gpre_ref, gpost_ref,
                    wg_ref, wu_ref, wd_ref, o_ref, h1_ref, xn_ref, acc_ref):
    f = pl.program_id(1)

    @pl.when(f == 0)
    def _():
        y = (jnp.dot(ca_ref[...], wa_ref[...], preferred_element_type=F32)
             + jnp.dot(cb_ref[...], wb_ref[...], preferred_element_type=F32))
        h1 = h_ref[...] + _rms(y, gmix_ref[...])
        h1_ref[...] = h1
        xn_ref[...] = _rms(h1, gpre_ref[...]).astype(BF16)
        acc_ref[...] = jnp.zeros_like(acc_ref)

    xn = xn_ref[...]
    gate = jnp.dot(xn, wg_ref[...], preferred_element_type=F32)
    up = jnp.dot(xn, wu_ref[...], preferred_element_type=F32)
    act = (_silu(gate) * up).astype(BF16)
    acc_ref[...] += jnp.dot(act, wd_ref[...], preferred_element_type=F32)

    @pl.when(f == pl.num_programs(1) - 1)
    def _():
        o_ref[...] = h1_ref[...] + _rms(acc_ref[...], gpost_ref[...])


def _mix_ffn(ca, cb, w_out, h, g_mix, g_pre, g_post, wg, wu, wd, *, tm, tf):
    t, d = h.shape
    ff = wg.shape[1]
    wa_n = ca.shape[1]
    wb_n = cb.shape[1]
    row = pl.BlockSpec((1, d), lambda i, f: (0, 0))
    once = dict(pipeline_mode=pl.Buffered(1)) if tf == ff else {}
    return pl.pallas_call(
        _mix_ffn_kernel,
        grid=(t // tm, ff // tf),
        in_specs=[pl.BlockSpec((tm, wa_n), lambda i, f: (i, 0)),
                  pl.BlockSpec((tm, wb_n), lambda i, f: (i, 0)),
                  pl.BlockSpec((wa_n, d), lambda i, f: (0, 0)),
                  pl.BlockSpec((wb_n, d), lambda i, f: (0, 0)),
                  pl.BlockSpec((tm, d), lambda i, f: (i, 0)),
                  row, row, row,
                  pl.BlockSpec((d, tf), lambda i, f: (0, f), **once),
                  pl.BlockSpec((d, tf), lambda i, f: (0, f), **once),
                  pl.BlockSpec((tf, d), lambda i, f: (f, 0), **once)],
        out_specs=pl.BlockSpec((tm, d), lambda i, f: (i, 0)),
        out_shape=jax.ShapeDtypeStruct((t, d), F32),
        scratch_shapes=[pltpu.VMEM((tm, d), F32), pltpu.VMEM((tm, d), BF16), pltpu.VMEM((tm, d), F32)],
        compiler_params=pltpu.CompilerParams(
            dimension_semantics=("parallel", "arbitrary"), vmem_limit_bytes=VMEM_LIMIT),
        name="mix_ffn",
    )(ca, cb, w_out[:wa_n].astype(BF16), w_out[wa_n:].astype(BF16), h,
      g_mix.reshape(1, d), g_pre.reshape(1, d), g_post.reshape(1, d),
      wg.astype(BF16), wu.astype(BF16), wd.astype(BF16))


def _deltanet_kernel(xq_ref, xk_ref, xv_ref, z_ref, sm_ref, cwq_ref, cwk_ref, cwv_ref,
                     alog_ref, dtb_ref, gn_ref, o_ref,
                     xpad_ref, q_ref, k_ref, v_ref, gb_ref, bb_ref, u_ref, w_ref, qk_ref, st_ref,
                     *, ts, a_col, b_col):
    s = pl.program_id(1)
    c = CHUNK
    d = HEAD_DIM
    nh = N_HEADS

    @pl.when(s == 0)
    def _():
        xpad_ref[:, 0:8, :] = jnp.zeros((3, 8, nh * d), F32)
        st_ref[...] = jnp.zeros_like(st_ref)

    @pl.when(s != 0)
    def _():
        xpad_ref[:, 0:8, :] = xpad_ref[:, ts:ts + 8, :]

    xpad_ref[0, 8:ts + 8, :] = xq_ref[...]
    xpad_ref[1, 8:ts + 8, :] = xk_ref[...]
    xpad_ref[2, 8:ts + 8, :] = xv_ref[...]

    def conv_silu(idx, cw_ref, hs):
        cw = cw_ref[:, hs]
        acc = xpad_ref[idx, 8 - (CONV_WIDTH - 1):8 - (CONV_WIDTH - 1) + ts, hs] * cw[0:1, :]
        for j in range(1, CONV_WIDTH):
            off = 8 - (CONV_WIDTH - 1) + j
            acc = acc + xpad_ref[idx, off:off + ts, hs] * cw[j:j + 1, :]
        return _silu(acc)

    def l2norm(t):
        return t * lax.rsqrt(jnp.sum(t * t, axis=-1, keepdims=True) + EPS)

    row = _iota((c, c), 0)
    col = _iota((c, c), 1)
    tri = (col <= row)
    strict = (col < row)
    tri_f = tri.astype(F32)
    upper_f = (row <= col).astype(F32)
    eye = (row == col).astype(F32)
    gnorm = gn_ref[...]
    chunks = range(ts // c)
    rs = [slice(ci * c, (ci + 1) * c) for ci in chunks]
    tri2 = jnp.concatenate([tri_f, tri_f], axis=1).astype(BF16)
    ones2 = jnp.ones((c, 2 * c), BF16)

    def cum2(lhs2, x):
        hi, lo = _split(x)
        return jnp.dot(lhs2, jnp.concatenate([hi, lo], axis=0), preferred_element_type=F32)

    for hh in range(nh):
        hs = slice(hh * d, (hh + 1) * d)
        q_ref[:, hs] = l2norm(conv_silu(0, cwq_ref, hs)) * (d ** -0.5)
        k_ref[:, hs] = l2norm(conv_silu(1, cwk_ref, hs))
        v_ref[:, hs] = conv_silu(2, cwv_ref, hs)

        a_raw = sm_ref[:, a_col + hh:a_col + hh + 1]
        b_raw = sm_ref[:, b_col + hh:b_col + hh + 1]
        g = -jnp.exp(alog_ref[:, hh:hh + 1]) * _softplus(a_raw + dtb_ref[:, hh:hh + 1])
        gb_ref[:, hs] = jnp.broadcast_to(g, (ts, d))
        bb_ref[:, hs] = jnp.broadcast_to(_sigmoid(b_raw), (ts, d))

        q = [q_ref[r, hs] for r in rs]
        k = [k_ref[r, hs] for r in rs]
        beta = [bb_ref[r, hs] for r in rs]
        gb = [gb_ref[r, hs] for r in rs]
        gc = [cum2(tri2, x) for x in gb]
        gc_row = [cum2(ones2, x[:, :c] * upper_f) for x in gb]
        decay = [jnp.where(tri, jnp.exp(jnp.minimum(a[:, :c] - b, 0.0)), 0.0) for a, b in zip(gc, gc_row)]
        kk = [_mm_nt(x, x) for x in k]
        n = [-jnp.where(strict, b[:, :c] * x * dc, 0.0) for b, x, dc in zip(beta, kk, decay)]
        inv = [eye + x for x in n]
        for step in range(5):
            nb = [x.astype(BF16) for x in n]
            n = [jnp.dot(x, x, preferred_element_type=F32) for x in nb]
            inv = [iv + _mm(iv, x) for iv, x in zip(inv, n)]
        egc = [jnp.exp(x) for x in gc]
        gl = [x[c - 1:c, :] for x in gc]
        inv_l = [x.astype(BF16) for x in inv]
        u = [_mm(a, v_ref[r, hs] * b) for a, r, b in zip(inv_l, rs, beta)]
        w = [_mm(a, x * (b * e)) for a, x, b, e in zip(inv_l, k, beta, egc)]
        qk = [_mm_nt(a, b) * dc for a, b, dc in zip(q, k, decay)]
        for ci in chunks:
            r = rs[ci]
            u_ref[r, hs] = u[ci]
            w_ref[r, hs] = w[ci]
            qk_ref[hh, r, :] = qk[ci]
            q_ref[r, hs] = q[ci] * egc[ci]
            k_ref[r, hs] = k[ci] * jnp.exp(gl[ci] - gc[ci])
            gb_ref[r, hs] = jnp.broadcast_to(jnp.exp(gl[ci]), (c, d))

    def chunk_body(ci, carry):
        r0 = pl.multiple_of(ci * c, c)
        rows = pl.ds(r0, c)
        hss = [slice(hh * d, (hh + 1) * d) for hh in range(nh)]
        st = [st_ref[hh] for hh in range(nh)]
        w_st = [_mm(w_ref[rows, hs], s_) for hs, s_ in zip(hss, st)]
        q_st = [_mm(q_ref[rows, hs], s_) for hs, s_ in zip(hss, st)]
        v_new = [u_ref[rows, hs] - x for hs, x in zip(hss, w_st)]
        o = [a + _mm(qk_ref[hh, rows, :], v) for hh, (a, v) in enumerate(zip(q_st, v_new))]
        kv = [_mm_tn(k_ref[rows, hs], v) for hs, v in zip(hss, v_new)]
        for hh, hs in enumerate(hss):
            st_ref[hh] = st[hh] * gb_ref[pl.ds(r0, 1), hs] + kv[hh]
            o_ref[rows, hs] = (_rms(o[hh], gnorm) * _silu(z_ref[rows, hs])).astype(o_ref.dtype)
        return carry

    lax.fori_loop(0, ts // c, chunk_body, 0)


def _deltanet(p32, conv_w, a_log, dt_bias, a_norm_g, *, ts, cols):
    bsz, s, _ = p32.shape
    d = HEAD_DIM
    nh = N_HEADS
    w = nh * d
    pad = lambda t: jnp.pad(t.astype(F32), (0, d - t.shape[0])).reshape(1, d)
    kernel = functools.partial(_deltanet_kernel, ts=ts, a_col=cols["a_lane"], b_col=cols["b_lane"])
    tile = lambda name: pl.BlockSpec((None, ts, w), lambda b, i: (b, i, cols[name] // nh))
    conv = lambda k: pl.BlockSpec((CONV_WIDTH, w), lambda b, i: (0, k))
    row = pl.BlockSpec((1, d), lambda b, i: (0, 0))
    return pl.pallas_call(
        kernel,
        grid=(bsz, s // ts),
        in_specs=[tile("qa"), tile("ka"), tile("va"), tile("za"),
                  pl.BlockSpec((None, ts, d), lambda b, i: (b, i, cols["small"])),
                  conv(0), conv(1), conv(2), row, row, row],
        out_specs=pl.BlockSpec((None, ts, w), lambda b, i: (b, i, 0)),
        out_shape=jax.ShapeDtypeStruct((bsz, s, w), BF16),
        scratch_shapes=[pltpu.VMEM((3, ts + 8, w), F32)]
        + [pltpu.VMEM((ts, w), F32) for _ in range(7)]
        + [pltpu.VMEM((nh, ts, CHUNK), F32), pltpu.VMEM((nh, d, d), F32)],
        compiler_params=pltpu.CompilerParams(
            dimension_semantics=("parallel", "arbitrary"), vmem_limit_bytes=VMEM_LIMIT),
        name="deltanet",
    )(p32, p32, p32, p32, p32, conv_w.astype(F32), conv_w.astype(F32), conv_w.astype(F32),
      pad(a_log), pad(dt_bias), a_norm_g.astype(F32).reshape(1, d))


def _hgrn2_kernel(q_ref, f_ref, i_ref, gate_ref, lb_ref, gn_ref, o_ref,
                  qs_ref, ks_ref, gc_ref, st_ref, *, ts):
    s = pl.program_id(1)
    c = CHUNK
    d = HEAD_DIM
    nh = N_HEADS
    SUB = 16

    @pl.when(s == 0)
    def _():
        st_ref[...] = jnp.zeros_like(st_ref)

    lb = lb_ref[...]
    f_raw = f_ref[...]
    log_sig = jnp.minimum(f_raw, 0.0) - jnp.log1p(jnp.exp(-jnp.abs(f_raw)))
    la = jnp.log(lb)
    lbb = jnp.log1p(-lb) + log_sig
    log_f = jnp.maximum(la, lbb) + jnp.log1p(jnp.exp(-jnp.abs(la - lbb)))
    qs_ref[...] = _silu(q_ref[...])
    ks_ref[...] = (1.0 - lb) * _sigmoid(-f_raw)

    row = _iota((c, c), 0)
    col = _iota((c, c), 1)
    tri_f = (col <= row).astype(F32)
    ones_dd = jnp.ones((d, d), BF16)
    rows_8d = _iota((8, d), 0)
    gnorm = gn_ref[...]

    tri2 = jnp.concatenate([tri_f, tri_f], axis=1).astype(BF16)
    for ci in range(ts // c):
        hi, lo = _split(log_f[ci * c:(ci + 1) * c, :])
        gc_ref[ci * c:(ci + 1) * c, :] = jnp.dot(tri2, jnp.concatenate([hi, lo], axis=0),
                                                 preferred_element_type=F32)

    blocks = [(sb * SUB, (sb + 1) * SUB) for sb in range(c // SUB)]

    def chunk_loop(ci, carry):
        r0 = pl.multiple_of(ci * c, c)
        rows = pl.ds(r0, c)
        hss = [slice(hh * d, (hh + 1) * d) for hh in range(nh)]
        q = [qs_ref[rows, hs] for hs in hss]
        k = [ks_ref[rows, hs] for hs in hss]
        v = [i_ref[rows, hs] for hs in hss]
        gc = [gc_ref[rows, hs] for hs in hss]

        def near_products(q, k, gc):
            prods = []
            for top, end in blocks:
                for j in range(top, end):
                    lo = (j // 8) * 8
                    e = jnp.exp2(gc[lo:end, :] - gc[j:j + 1, :])
                    if j % 8:
                        head = jnp.where(rows_8d >= j - lo, e[:8], 0.0)
                        e = jnp.concatenate([head, e[8:]], axis=0) if lo + 8 < end else head
                    prods.append(q[lo:end, :] * k[j:j + 1, :] * e)
            return jnp.concatenate(prods, axis=0).astype(BF16)

        def far_operands(q, k, gc):
            out = []
            for top, end in blocks[1:]:
                g_b = gc[top - 1:top, :]
                out.append((q[top:end, :] * jnp.exp(gc[top:end, :] - g_b),
                            k[:top, :] * jnp.exp(jnp.minimum(g_b - gc[:top, :], 0.0))))
            return out

        near = [near_products(a, b, g * LOG2E) for a, b, g in zip(q, k, gc)]
        far_ops = [far_operands(*x) for x in zip(q, k, gc)]
        st = [st_ref[hh] for hh in range(nh)]
        gl = [x[c - 1:c, :] for x in gc]
        sums = [jnp.dot(x, ones_dd, preferred_element_type=F32) for x in near]
        qk_far = [[_mm_nt(qe, ke) for qe, ke in ops] for ops in far_ops]
        far = [[_mm(a, vv[:top, :]) for a, (top, _) in zip(qs, blocks[1:])] for qs, vv in zip(qk_far, v)]
        o_st = [_mm_nt(a * jnp.exp(g), s_) for a, g, s_ in zip(q, gc, st)]
        kv = [_mm_tn(vv, kk * jnp.exp(g_l - g)) for vv, kk, g_l, g in zip(v, k, gl, gc)]

        for hh, hs in enumerate(hss):
            groups = [jnp.zeros((8, d), F32) for _ in range(c // 8)]
            at = 0
            for top, end in blocks:
                for j in range(top, end):
                    v_j = v[hh][j:j + 1, :]
                    for g in range(j // 8, end // 8):
                        groups[g] = groups[g] + sums[hh][at:at + 8, :] * v_j
                        at += 8
            for f, (top, end) in zip(far[hh], blocks[1:]):
                for g in range(top // 8, end // 8):
                    groups[g] = groups[g] + f[(g * 8 - top):(g * 8 - top + 8), :]
            o = jnp.concatenate(groups, axis=0) + o_st[hh]
            st_ref[hh] = st[hh] * jnp.exp(gl[hh]) + kv[hh]
            o_ref[rows, hs] = (_rms(o, gnorm) * _silu(gate_ref[rows, hs])).astype(o_ref.dtype)
        return carry

    lax.fori_loop(0, ts // c, chunk_loop, 0)


def _hgrn2(p32, lb, d_norm_g, *, ts, cols):
    bsz, s, _ = p32.shape
    d = HEAD_DIM
    nh = N_HEADS
    w = nh * d
    kernel = functools.partial(_hgrn2_kernel, ts=ts)
    tile = lambda name: pl.BlockSpec((None, ts, w), lambda b, i: (b, i, cols[name] // nh))
    return pl.pallas_call(
        kernel,
        grid=(bsz, s // ts),
        in_specs=[tile("qd"), tile("fd"), tile("id"), tile("gd"),
                  pl.BlockSpec((1, w), lambda b, i: (0, 0)),
                  pl.BlockSpec((1, d), lambda b, i: (0, 0))],
        out_specs=pl.BlockSpec((None, ts, w), lambda b, i: (b, i, 0)),
        out_shape=jax.ShapeDtypeStruct((bsz, s, w), BF16),
        scratch_shapes=[pltpu.VMEM((ts, w), F32), pltpu.VMEM((ts, w), F32),
                        pltpu.VMEM((ts, w), F32), pltpu.VMEM((nh, d, d), F32)],
        compiler_params=pltpu.CompilerParams(
            dimension_semantics=("parallel", "arbitrary"), vmem_limit_bytes=VMEM_LIMIT),
        name="hgrn2",
    )(p32, p32, p32, p32, lb.astype(F32).reshape(1, w), d_norm_g.astype(F32).reshape(1, d))


def _stickbreak_kernel(q_ref, k_ref, v_ref, o_ref, acc_ref, *, tq):
    i = pl.program_id(1)
    d = HEAD_DIM
    nh = N_HEADS
    row = _iota((tq, tq), 0)
    col = _iota((tq, tq), 1)
    causal = col < row
    later = (row > col).astype(BF16)
    later2 = jnp.concatenate([later, later], axis=0)

    heads = [slice(hh * d, (hh + 1) * d) for hh in range(nh)]

    def scores(blocks):
        jobs = [(j, dg, hs) for j, dg in blocks for hs in heads]
        z = [_mm_nt(q_ref[:, hs], k_ref[pl.ds(pl.multiple_of(j * tq, tq), tq), hs]) * (d ** -0.5)
             for j, _, hs in jobs]
        sp = [_softplus(x) for x in z]
        l1m = [jnp.where(causal, -x, 0.0) if dg else -x for x, (_, dg, _) in zip(sp, jobs)]
        rest = [jnp.dot(jnp.concatenate(_split(x), axis=1), later2, preferred_element_type=F32)
                for x in l1m]
        out = [((a - b) + r, l) for a, b, r, l in zip(z, sp, rest, l1m)]
        return [out[b * nh:(b + 1) * nh] for b in range(len(blocks))]

    def block(j, carries):
        (sc,) = scores([(j, False)])
        ps = [jnp.exp(logw + c) for (logw, _), c in zip(sc, carries)]
        pv = [_mm(p, v_ref[pl.ds(pl.multiple_of(j * tq, tq), tq), hs]) for p, hs in zip(ps, heads)]
        for hs, x in zip(heads, pv):
            acc_ref[:, hs] += x
        return tuple(c + jnp.sum(l1m, axis=-1, keepdims=True) for (_, l1m), c in zip(sc, carries))

    jp = jnp.maximum(i - 1, 0)
    live = jnp.where(i > 0, 1.0, 0.0)
    sd, sp_ = scores([(i, True), (jp, False)])
    carries = []
    for hh, hs in enumerate(heads):
        c1 = jnp.sum(sd[hh][1], axis=-1, keepdims=True)
        p_d = jnp.where(causal, jnp.exp(sd[hh][0]), 0.0)
        p_p = jnp.exp(sp_[hh][0] + c1) * live
        acc_ref[:, hs] = (_mm(p_d, v_ref[pl.ds(pl.multiple_of(i * tq, tq), tq), hs])
                          + _mm(p_p, v_ref[pl.ds(pl.multiple_of(jp * tq, tq), tq), hs]))
        carries.append(c1 + jnp.sum(sp_[hh][1], axis=-1, keepdims=True))
    carries = tuple(carries)

    def cond(c):
        worst = functools.reduce(jnp.maximum, c[1])
        return jnp.logical_and(c[0] >= 0, jnp.max(worst) >= EXP_ZERO_BELOW)

    def body(c):
        return c[0] - 1, block(c[0], c[1])

    lax.while_loop(cond, body, (i - 2, carries))
    o_ref[...] = acc_ref[...].astype(o_ref.dtype)


def _stickbreak(p16, *, tq, cols):
    bsz, s, _ = p16.shape
    nh = N_HEADS
    w = nh * HEAD_DIM
    kernel = functools.partial(_stickbreak_kernel, tq=tq)
    resident = dict(pipeline_mode=pl.Buffered(1))
    return pl.pallas_call(
        kernel,
        grid=(bsz, s // tq),
        in_specs=[pl.BlockSpec((None, tq, w), lambda b, i: (b, i, cols["qc"] // nh)),
                  pl.BlockSpec((None, s, w), lambda b, i: (b, 0, cols["kc"] // nh), **resident),
                  pl.BlockSpec((None, s, w), lambda b, i: (b, 0, cols["vc"] // nh), **resident)],
        out_specs=pl.BlockSpec((None, tq, w), lambda b, i: (b, i, 0)),
        out_shape=jax.ShapeDtypeStruct((bsz, s, w), BF16),
        scratch_shapes=[pltpu.VMEM((tq, w), F32)],
        compiler_params=pltpu.CompilerParams(
            dimension_semantics=("parallel", "arbitrary"), vmem_limit_bytes=VMEM_LIMIT),
        name="stickbreak",
    )(p16, p16, p16)


def _dsa_kernel(qi_ref, smq_ref, q_ref, sm_ref, k_ref, vt_ref, bias_ref, o_ref,
                sc_ref, scb_ref, wb_ref, qc_ref, kct_ref, bd_ref, lg_ref, *, tq, k_sel, wi_lane, wide):
    i = pl.program_id(1)
    tk = tq
    d = HEAD_DIM
    nh = N_HEADS
    ksel = float(k_sel)
    per_wide = wide // tk
    n_wide = (i + per_wide) // per_wide
    sub = 2 * tk
    lane_q = _iota((1, tq), 1)

    def tree(parts, op):
        while len(parts) > 1:
            parts = [op(parts[j], parts[j + 1]) if j + 1 < len(parts) else parts[j]
                     for j in range(0, len(parts), 2)]
        return parts[0]

    def col_fold(x, op=jnp.add, rows=8):
        return tree([x[r * rows:(r + 1) * rows] for r in range(x.shape[0] // rows)], op)

    @pl.when(i == 0)
    def _():
        def prep(g, carry):
            g0 = pl.multiple_of(g * wide, wide)
            kt = sm_ref[pl.ds(g0, wide), :].T[:IDX_DIM, :]
            hi, lo = _split(kt)
            kct_ref[:, pl.ds(g0, wide)] = jnp.concatenate([hi, lo, hi], axis=0)
            return carry
        lax.fori_loop(0, sm_ref.shape[0] // wide, prep, 0)

    smq = smq_ref[...]
    lane = _iota(smq.shape, 1)
    for hh in range(IDX_HEADS):
        qh = qi_ref[:, hh * IDX_DIM:(hh + 1) * IDX_DIM]
        hi, lo = _split(qh)
        qc_ref[hh] = jnp.concatenate([hi, hi, lo], axis=-1)
        w = jnp.sum(jnp.where(lane == wi_lane + hh, smq, 0.0), axis=-1, keepdims=True)
        wb_ref[hh] = jnp.broadcast_to(w * ((IDX_HEADS ** -0.5) * (IDX_DIM ** -0.5)), (tq, tk))

    q2t = (q_ref[...] * ((d ** -0.5) * LOG2E)).T.astype(BF16)
    zero_dq = jnp.zeros((d, tq), BF16)
    for p in range(nh // 2):
        top = jnp.concatenate([q2t[2 * p * d:(2 * p + 1) * d], zero_dq], axis=1)
        bot = jnp.concatenate([zero_dq, q2t[(2 * p + 1) * d:(2 * p + 2) * d]], axis=1)
        bd_ref[p] = jnp.concatenate([top, bot], axis=0)

    limit = i * tq + (lane_q // CHUNK + 1) * CHUNK
    rows_t = _iota((tk, tq), 0)

    def score_group(g, mm, masked):
        mn, mx = mm
        for sb in range(wide // sub):
            k0 = pl.multiple_of(g * wide + sb * sub, sub)
            kct = kct_ref[:, pl.ds(k0, sub)]
            tiles = [jnp.zeros((tq, tk), F32) for _ in range(sub // tk)]
            for hh in range(IDX_HEADS):
                s_h = jnp.dot(qc_ref[hh], kct, preferred_element_type=F32)
                for ti in range(sub // tk):
                    tiles[ti] = tiles[ti] + jnp.maximum(s_h[:, ti * tk:(ti + 1) * tk], 0.0) * wb_ref[hh]
            for ti in range(sub // tk):
                kb = pl.multiple_of(k0 + ti * tk, tk)
                sct = tiles[ti].T
                if masked:
                    adm = (kb + rows_t) < limit
                    mn = jnp.minimum(mn, col_fold(jnp.where(adm, sct, jnp.inf), jnp.minimum))
                    sct = jnp.where(adm, sct, -jnp.inf)
                else:
                    mn = jnp.minimum(mn, col_fold(sct, jnp.minimum))
                mx = jnp.maximum(mx, col_fold(sct, jnp.maximum))
                sc_ref[pl.ds(kb, tk), :] = sct
                scb_ref[pl.ds(kb, tk), :] = _floor_bf16(sct)
        return mn, mx

    def score_pair(j, mm):
        return score_group(2 * j + 1, score_group(2 * j, mm, False), False)

    n_full = n_wide - 1
    mm = lax.fori_loop(0, n_full // 2, score_pair,
                       (jnp.full((8, tq), jnp.inf, F32), jnp.full((8, tq), -jnp.inf, F32)))
    mm = lax.cond(n_full % 2 == 1, lambda c: score_group(n_full - 1, c, False), lambda c: c, mm)
    mn, mx = score_group(n_wide - 1, mm, True)

    n_pairs = (n_wide + 1) // 2

    @pl.when(n_wide % 2 == 1)
    def _():
        sc_ref[pl.ds(pl.multiple_of(n_wide * wide, wide), wide), :] = jnp.full((wide, tq), -jnp.inf, F32)
        scb_ref[pl.ds(pl.multiple_of(n_wide * wide, wide), wide), :] = jnp.full((wide, tq), -jnp.inf, BF16)
    rmin = jnp.min(mn, axis=0, keepdims=True)
    rmax = jnp.max(mx, axis=0, keepdims=True)

    def count(pred):
        def body(j, acc):
            for g in (2 * j, 2 * j + 1):
                acc = acc + col_fold(pred(sc_ref[pl.ds(pl.multiple_of(g * wide, wide), wide), :]))
            return acc
        return jnp.sum(lax.fori_loop(0, n_pairs, body, jnp.zeros((8, tq), F32)), axis=0, keepdims=True)

    def max_below(x):
        def body(j, acc):
            for g in (2 * j, 2 * j + 1):
                blk = sc_ref[pl.ds(pl.multiple_of(g * wide, wide), wide), :]
                acc = jnp.maximum(acc, col_fold(jnp.where(blk < x, blk, -jnp.inf), jnp.maximum))
            return acc
        return jnp.max(lax.fori_loop(0, n_pairs, body, jnp.full((8, tq), -jnp.inf, F32)), axis=0, keepdims=True)

    n_adm = limit.astype(F32)
    all_sel = n_adm <= ksel

    def bisect(c):
        lo, hi, c_lo = c
        mid = 0.5 * lo + 0.5 * hi
        cm = count(lambda blk: _ind(blk >= mid))
        ge = cm >= ksel
        return jnp.where(ge, mid, lo), jnp.where(ge, hi, mid), jnp.where(ge, cm, c_lo)

    def pending(c_lo, tied):
        return jnp.where(all_sel, 0.0, jnp.where(tied > 0.5, 0.0, _ind(c_lo != ksel)))

    def bisect_coarse(_, c):
        lo, hi, c_lo = c
        mid = _floor_bf16(0.5 * lo + 0.5 * hi).astype(F32)
        t_b = jnp.broadcast_to(mid, (16, tq)).astype(BF16)
        one_b = jnp.ones((16, tq), BF16)
        zero_b = jnp.zeros((16, tq), BF16)

        def body(j, acc):
            for g in (2 * j, 2 * j + 1):
                blk = scb_ref[pl.ds(pl.multiple_of(g * wide, wide), wide), :]
                ind = [jnp.where(blk[r * 16:(r + 1) * 16] >= t_b, one_b, zero_b) for r in range(wide // 16)]
                acc = acc + tree(ind, jnp.add).astype(F32)
            return acc

        acc = lax.fori_loop(0, n_pairs, body, jnp.zeros((16, tq), F32))
        cm = jnp.sum(acc, axis=0, keepdims=True)
        ge = cm >= ksel
        return jnp.where(ge, mid, lo), jnp.where(ge, hi, mid), jnp.where(ge, cm, c_lo)

    lo0 = _floor_bf16(rmin).astype(F32)
    hi0 = _floor_bf16(rmax + (jnp.abs(rmax) * (2.0 ** -6) + 1e-30)).astype(F32)
    state = lax.fori_loop(0, BISECT_COARSE, bisect_coarse, (lo0, hi0, n_adm))
    state = lax.fori_loop(0, BISECT_FIXED, lambda _, c: bisect(c), state)

    def round_cond(c):
        return jnp.max(pending(c[0][2], c[1])) > 0.5

    def round_body(c):
        st, tied, v, need = c

        def more_cond(s):
            return jnp.logical_and(s[0] < BISECT_EXTRA, jnp.max(pending(s[1][2], tied)) > 0.5)

        _, st = lax.while_loop(more_cond, lambda s: (s[0] + 1, bisect(s[1])), (jnp.int32(0), st))
        pend = pending(st[2], tied)

        def check(_):
            cand = max_below(st[1])
            c_ge = count(lambda blk: _ind(blk >= cand))
            c_gt = count(lambda blk: _ind(blk > cand))
            ok = jnp.where(pend > 0.5, _ind(c_ge >= ksel), 0.0)
            return (jnp.where(ok > 0.5, 1.0, tied), jnp.where(ok > 0.5, cand, v),
                    jnp.where(ok > 0.5, ksel - c_gt, need))

        tied, v, need = lax.cond(jnp.max(pend) > 0.5, check, lambda _: (tied, v, need), 0)
        return st, tied, v, need

    zeros1 = jnp.zeros((1, tq), F32)
    (lo_f, _, _), tied, v_tie, need = lax.while_loop(round_cond, round_body, (state, zeros1, zeros1, zeros1))
    vth = jnp.where(all_sel, F32_LOWEST, jnp.where(tied > 0.5, v_tie, lo_f))

    @pl.when(jnp.max(tied) > 0.5)
    def _():
        v_eq = jnp.where(tied > 0.5, v_tie, jnp.inf)
        incl = (_iota((tk, tk), 1) <= _iota((tk, tk), 0)).astype(BF16)

        def demote(g, seen):
            g0 = pl.multiple_of(g * wide, wide)
            xs = [sc_ref[pl.ds(g0 + pb * tk, tk), :] for pb in range(per_wide)]
            eqs = [_ind(x == v_eq) for x in xs]
            inblk = [jnp.dot(incl, e.astype(BF16), preferred_element_type=F32) for e in eqs]
            for pb in range(per_wide):
                rank = inblk[pb] + seen
                sc_ref[pl.ds(g0 + pb * tk, tk), :] = jnp.where(eqs[pb] * _ind(rank > need) > 0.5,
                                                               -jnp.inf, xs[pb])
                seen = seen + jnp.sum(col_fold(eqs[pb]), axis=0, keepdims=True)
            return seen

        lax.fori_loop(0, n_wide, demote, zeros1)

    g_near = jnp.maximum(i - 1, 0) // per_wide

    def logit_group(g, mx, near):
        out = list(mx)
        for sb in range(wide // sub):
            k0 = pl.multiple_of(g * wide + sb * sub, sub)
            sel = sc_ref[pl.ds(k0, sub), :] >= vth
            for p in range(nh // 2):
                pair = jnp.dot(k_ref[pl.ds(k0, sub), 2 * p * d:(2 * p + 2) * d], bd_ref[p],
                               preferred_element_type=F32)
                for hh in (2 * p, 2 * p + 1):
                    lm = pair[:, (hh - 2 * p) * tq:(hh - 2 * p + 1) * tq]
                    if near:
                        back = [jnp.clip(i - (g * per_wide + sb * (sub // tk) + pb), 0, 2)
                                for pb in range(sub // tk)]
                        lm = lm + jnp.concatenate([bias_ref[bk, hh] for bk in back], axis=0)
                    lm = jnp.where(sel, lm, NEG_BIG)
                    lg_ref[hh, pl.ds(k0, sub), :] = lm
                    out[hh] = jnp.maximum(out[hh], col_fold(lm, jnp.maximum))
        return tuple(out)

    mx = tuple(jnp.full((8, tq), NEG_BIG, F32) for _ in range(nh))
    def logit_pair(j, mx, near):
        return logit_group(2 * j + 1, logit_group(2 * j, mx, near), near)

    far_pairs = g_near // 2
    mx = lax.fori_loop(0, far_pairs, functools.partial(logit_pair, near=False), mx)
    mx = lax.fori_loop(far_pairs, n_pairs, functools.partial(logit_pair, near=True), mx)
    m_q = [jnp.max(mx[hh], axis=0, keepdims=True) for hh in range(nh)]

    ones_rows = jnp.ones((8, wide), BF16)

    def pv_pair(j, carry):
        ls, accs = list(carry[0]), list(carry[1])
        jobs = [(pl.multiple_of(g * wide, wide), hh) for g in (2 * j, 2 * j + 1) for hh in range(nh)]
        ps = [jnp.exp2(lg_ref[hh, pl.ds(g0, wide), :] - m_q[hh]).astype(BF16) for g0, hh in jobs]
        outs = [jnp.dot(jnp.concatenate([vt_ref[hh * d:(hh + 1) * d, pl.ds(g0, wide)], ones_rows], axis=0),
                        p, preferred_element_type=F32) for (g0, hh), p in zip(jobs, ps)]
        for (_, hh), out in zip(jobs, outs):
            ls[hh] = ls[hh] + out[d:]
            accs[hh] = accs[hh] + out[:d]
        return tuple(ls), tuple(accs)

    ls, accs = lax.fori_loop(0, n_pairs, pv_pair,
                             (tuple(jnp.zeros((8, tq), F32) for _ in range(nh)),
                              tuple(jnp.zeros((d, tq), F32) for _ in range(nh))))
    for hh in range(nh):
        o_ref[:, hh * d:(hh + 1) * d] = (accs[hh] / ls[hh][0:1]).T.astype(o_ref.dtype)


def _dsa(p32, p16, vt, bias_tiles, *, tq, cols):
    bsz, s, _ = p32.shape
    d = HEAD_DIM
    nh = N_HEADS
    wide = 4 * tq
    k_sel = min(TOPK_MAX, s // 4)
    w512 = nh * d
    kernel = functools.partial(_dsa_kernel, tq=tq, k_sel=k_sel, wi_lane=cols["wi_lane"], wide=wide)
    resident = dict(pipeline_mode=pl.Buffered(1))
    return pl.pallas_call(
        kernel,
        grid=(bsz, s // tq),
        in_specs=[pl.BlockSpec((None, tq, w512), lambda b, i: (b, i, cols["qi"] // nh)),
                  pl.BlockSpec((None, tq, d), lambda b, i: (b, i, cols["small"])),
                  pl.BlockSpec((None, tq, w512), lambda b, i: (b, i, cols["qb"] // nh)),
                  pl.BlockSpec((None, s, d), lambda b, i: (b, 0, cols["small"]), **resident),
                  pl.BlockSpec((None, s, w512), lambda b, i: (b, 0, cols["kb"] // nh), **resident),
                  pl.BlockSpec((w512, s), lambda b, i: (0, b), **resident),
                  pl.BlockSpec((3, nh, tq, tq), lambda b, i: (0, 0, 0, 0), **resident)],
        out_specs=pl.BlockSpec((None, tq, w512), lambda b, i: (b, i, 0)),
        out_shape=jax.ShapeDtypeStruct((bsz, s, w512), BF16),
        scratch_shapes=[pltpu.VMEM((s, tq), F32),
                        pltpu.VMEM((s, tq), BF16),
                        pltpu.VMEM((IDX_HEADS, tq, tq), F32),
                        pltpu.VMEM((IDX_HEADS, tq, 3 * IDX_DIM), BF16),
                        pltpu.VMEM((3 * IDX_DIM, s), BF16),
                        pltpu.VMEM((nh // 2, 2 * d, 2 * tq), BF16),
                        pltpu.VMEM((nh, s, tq), F32)],
        compiler_params=pltpu.CompilerParams(
            dimension_semantics=("parallel", "arbitrary"), vmem_limit_bytes=VMEM_LIMIT),
        name="dsa",
    )(p32, p32, p32, p32, p16, vt, bias_tiles)


def _t5_bucket(rel):
    nb = REL_BUCKETS // 2
    max_exact = nb // 2
    ret = jnp.where(rel > 0, nb, 0)
    n = jnp.abs(rel)
    large = max_exact + (jnp.log(jnp.maximum(n, 1).astype(F32) / max_exact)
                         / math.log(REL_MAX_DIST / max_exact) * (nb - max_exact)).astype(jnp.int32)
    large = jnp.minimum(large, nb - 1)
    return ret + jnp.where(n < max_exact, n, large)


def _bias_tiles(rel_table, tq):
    assert tq >= REL_MAX_DIST
    t = jnp.arange(tq)
    back = jnp.arange(3)
    rel = (t[None, None, :] - back[:, None, None] * tq) - t[None, :, None]
    onehot = (_t5_bucket(rel)[..., None] == jnp.arange(REL_BUCKETS)).astype(F32)
    tiles = jnp.einsum("bqkn,nh->bhkq", onehot, rel_table.astype(F32),
                       precision=HIGHEST)
    return (tiles - tiles[2:3]) * LOG2E


def _even_layout(w_in):
    d = HEAD_DIM
    a_w = 2 * N_HEADS * d + N_HEADS * d
    offs = {}
    o = 0
    for name, w in (("qkv", a_w), ("z", N_HEADS * d), ("a", N_HEADS), ("b", N_HEADS),
                    ("qb", N_HEADS * d), ("kb", N_HEADS * d), ("vb", N_HEADS * d),
                    ("qi", IDX_HEADS * IDX_DIM), ("ki", IDX_DIM), ("wi", IDX_HEADS)):
        offs[name] = (o, o + w)
        o += w
    assert o == w_in.shape[1]
    sl = lambda n: w_in[:, offs[n][0]:offs[n][1]]
    small_w = IDX_DIM + 2 * N_HEADS + IDX_HEADS
    small_pad = -small_w % d
    zeros = lambda n: jnp.zeros((w_in.shape[0], n), w_in.dtype)
    w32 = jnp.concatenate([sl("qkv"), sl("z"), sl("qb"), sl("qi"),
                           sl("ki"), sl("a"), sl("b"), sl("wi"), zeros(small_pad)], axis=1)
    n32 = w32.shape[1]
    tn = n32 // 5
    assert tn * 5 == n32 and tn % d == 0
    w16 = jnp.concatenate([sl("kb"), zeros(tn - N_HEADS * d)], axis=1)
    nh = N_HEADS
    cols = dict(qa=0, ka=nh, va=2 * nh, za=3 * nh, qb=4 * nh, qi=5 * nh, small=6 * nh, kb=0,
                a_lane=IDX_DIM, b_lane=IDX_DIM + nh, wi_lane=IDX_DIM + 2 * nh, n32=n32, tn=tn)
    return jnp.concatenate([w32, w16], axis=1).astype(BF16), sl("vb").T.astype(BF16), cols


def kernel(x, norm_g, w_in_even, conv_w_even, a_log_even, dt_bias_even, a_norm_even, w_out_even,
           rel_bias, w_in_odd, lb_logits, d_norm_odd, w_out_odd, w_gate, w_up, w_down):
    bsz, s, d = x.shape
    t = bsz * s
    depth = norm_g.shape[0]
    nh = N_HEADS
    tq = Q_TILE
    lb_all = jnp.cumsum(jax.nn.softmax(lb_logits.astype(F32), axis=0), axis=0)
    lb_all = lb_all - lb_all[:1]
    odd_cols = dict(qc=0, kc=nh, vc=2 * nh, qd=0, fd=nh, id=2 * nh, gd=3 * nh)
    bias_tiles = _bias_tiles(rel_bias, tq)

    h = x.reshape(t, d)
    for l in range(depth):
        if l % 2 == 0:
            e = l // 2
            w_even, w_vt, cols = _even_layout(w_in_even[e])
            p32, p16, vt = _norm_matmul(h, norm_g[l, 0], w_even, tm=PROJ_TILE, tn=cols["tn"], n32=cols["n32"],
                                        w_t=w_vt)
            p32 = p32.reshape(bsz, s, -1)
            p16 = p16.reshape(bsz, s, -1)
            o_1 = _deltanet(p32, conv_w_even[e], a_log_even[e], dt_bias_even[e], a_norm_even[e],
                            ts=min(SEQ_TILE, s), cols=cols)
            o_2 = _dsa(p32, p16, vt, bias_tiles, tq=tq, cols=cols)
            w_out = w_out_even[e]
        else:
            o = l // 2
            n16 = 3 * nh * HEAD_DIM
            w_odd = jnp.concatenate([w_in_odd[o][:, n16:], w_in_odd[o][:, :n16]], axis=1).astype(BF16)
            p32, p16 = _norm_matmul(h, norm_g[l, 0], w_odd, tm=PROJ_TILE, tn=ODD_COL_TILE, n32=w_odd.shape[1] - n16)
            p32 = p32.reshape(bsz, s, -1)
            p16 = p16.reshape(bsz, s, -1)
            o_1 = _stickbreak(p16, tq=tq, cols=odd_cols)
            o_2 = _hgrn2(p32, lb_all[l], d_norm_odd[o], ts=min(SEQ_TILE, s), cols=odd_cols)
            w_out = w_out_odd[o]
        h = _mix_ffn(o_1.reshape(t, -1), o_2.reshape(t, -1), w_out, h, norm_g[l, 1], norm_g[l, 2], norm_g[l, 3],
                     w_gate[l], w_up[l], w_down[l], tm=ROW_TILE, tf=FFN_TILE)
    return h.reshape(bsz, s, d)
```

```python
import functools
import math

import jax
import jax.numpy as jnp
from jax import lax
from jax.experimental import pallas as pl
from jax.experimental.pallas import tpu as pltpu

F32 = jnp.float32
BF16 = jnp.bfloat16
HIGHEST = lax.Precision.HIGHEST

CHUNK = 64
HEAD_DIM = 128
N_HEADS = 4
IDX_HEADS = 8
IDX_DIM = 64
TOPK_MAX = 256
CONV_WIDTH = 4
REL_BUCKETS = 32
REL_MAX_DIST = 128
EPS = 1e-6
NEG_BIG = -1e30
LOG2E = 1.4426950408889634
BISECT_COARSE = 12
BISECT_FIXED = 10
BISECT_EXTRA = 6
F32_LOWEST = -3.4028234663852886e38
EXP_ZERO_BELOW = -104.0
VMEM_LIMIT = 56 * 1024 * 1024

PROJ_TILE = 2048
ROW_TILE = 512
SEQ_TILE = 512
Q_TILE = 128
ODD_COL_TILE = 512
FFN_TILE = 2816


def _mm(a, b):
    return jnp.dot(a.astype(BF16), b.astype(BF16), preferred_element_type=F32)


def _mm_nt(a, b):
    return lax.dot_general(a.astype(BF16), b.astype(BF16), (((1,), (1,)), ((), ())),
                           preferred_element_type=F32)


def _mm_tn(a, b):
    return lax.dot_general(a.astype(BF16), b.astype(BF16), (((0,), (0,)), ((), ())),
                           preferred_element_type=F32)


def _split(x):
    hi = x.astype(BF16)
    return hi, (x - hi.astype(F32)).astype(BF16)


def _floor_bf16(x):
    bits = pltpu.bitcast(x, jnp.int32)
    down = jnp.where(bits >= 0, bits, bits + 0xFFFF) & jnp.int32(-65536)
    return pltpu.bitcast(down, F32).astype(BF16)


def _sigmoid(x):
    return 1.0 / (1.0 + jnp.exp(-x))


def _silu(x):
    return x * _sigmoid(x)


def _softplus(x):
    return jnp.maximum(x, 0.0) + jnp.log1p(jnp.exp(-jnp.abs(x)))


def _rms(x, g):
    return x * lax.rsqrt(jnp.mean(x * x, axis=-1, keepdims=True) + EPS) * g


def _iota(shape, dim):
    return lax.broadcasted_iota(jnp.int32, shape, dim)


def _ind(mask):
    return jnp.where(mask, 1.0, 0.0)


def _norm_matmul_kernel(x_ref, g_ref, w_ref, *rest, n_t, tiles32):
    if n_t:
        wt_ref, o32_ref, o16_ref, ot_ref, xn_ref = rest
    else:
        o32_ref, o16_ref, xn_ref = rest
    j = pl.program_id(1)

    @pl.when(j == 0)
    def _():
        xn_ref[...] = _rms(x_ref[...], g_ref[...]).astype(BF16)
        if n_t:
            ot_ref[...] = lax.dot_general(wt_ref[...], xn_ref[...], (((1,), (1,)), ((), ())),
                                          preferred_element_type=F32).astype(BF16)

    y = jnp.dot(xn_ref[...], w_ref[...], preferred_element_type=F32)

    @pl.when(j < tiles32)
    def _():
        o32_ref[...] = y

    @pl.when(j >= tiles32)
    def _():
        o16_ref[...] = y.astype(BF16)


def _norm_matmul(x, g, w, *, tm, tn, n32, w_t=None):
    t, d = x.shape
    n = w.shape[1]
    n_t = 0 if w_t is None else w_t.shape[0]
    tiles32 = n32 // tn
    assert tiles32 * tn == n32 and (n - n32) % tn == 0 and 0 < n32 < n
    in_specs = [pl.BlockSpec((tm, d), lambda i, j: (i, 0)),
                pl.BlockSpec((1, d), lambda i, j: (0, 0)),
                pl.BlockSpec((d, tn), lambda i, j: (0, j))]
    out_specs = [pl.BlockSpec((tm, tn), lambda i, j: (i, jnp.minimum(j, tiles32 - 1))),
                 pl.BlockSpec((tm, tn), lambda i, j: (i, jnp.maximum(j - tiles32, 0)))]
    out_shape = [jax.ShapeDtypeStruct((t, n32), F32), jax.ShapeDtypeStruct((t, n - n32), BF16)]
    args = [x, g.reshape(1, d), w]
    if n_t:
        in_specs.append(pl.BlockSpec((n_t, d), lambda i, j: (0, 0)))
        out_specs.append(pl.BlockSpec((n_t, tm), lambda i, j: (0, i)))
        out_shape.append(jax.ShapeDtypeStruct((n_t, t), BF16))
        args.append(w_t)
    return pl.pallas_call(
        functools.partial(_norm_matmul_kernel, n_t=n_t, tiles32=tiles32),
        grid=(t // tm, n // tn),
        in_specs=in_specs,
        out_specs=out_specs,
        out_shape=out_shape,
        scratch_shapes=[pltpu.VMEM((tm, d), BF16)],
        compiler_params=pltpu.CompilerParams(
            dimension_semantics=("parallel", "arbitrary"), vmem_limit_bytes=VMEM_LIMIT),
        name="norm_matmul",
    )(*args)


def _mix_ffn_kernel(ca_ref, cb_ref, wa_ref, wb_ref, h_ref, gmix_ref, gpre_ref, gpost_ref,
                    wg_ref, wu_ref, wd_ref, o_ref, h1_ref, xn_ref, acc_ref):
    f = pl.program_id(1)

    @pl.when(f == 0)
    def _():
        y = (jnp.dot(ca_ref[...], wa_ref[...], preferred_element_type=F32)
             + jnp.dot(cb_ref[...], wb_ref[...], preferred_element_type=F32))
        h1 = h_ref[...] + _rms(y, gmix_ref[...])
        h1_ref[...] = h1
        xn_ref[...] = _rms(h1, gpre_ref[...]).astype(BF16)
        acc_ref[...] = jnp.zeros_like(acc_ref)

    xn = xn_ref[...]
    gate = jnp.dot(xn, wg_ref[...], preferred_element_type=F32)
    up = jnp.dot(xn, wu_ref[...], preferred_element_type=F32)
    act = (_silu(gate) * up).astype(BF16)
    acc_ref[...] += jnp.dot(act, wd_ref[...], preferred_element_type=F32)

    @pl.when(f == pl.num_programs(1) - 1)
    def _():
        o_ref[...] = h1_ref[...] + _rms(acc_ref[...], gpost_ref[...])


def _mix_ffn(ca, cb, w_out, h, g_mix, g_pre, g_post, wg, wu, wd, *, tm, tf):
    t, d = h.shape
    ff = wg.shape[1]
    wa_n = ca.shape[1]
    wb_n = cb.shape[1]
    row = pl.BlockSpec((1, d), lambda i, f: (0, 0))
    once = dict(pipeline_mode=pl.Buffered(1)) if tf == ff else {}
    return pl.pallas_call(
        _mix_ffn_kernel,
        grid=(t // tm, ff // tf),
        in_specs=[pl.BlockSpec((tm, wa_n), lambda i, f: (i, 0)),
                  pl.BlockSpec((tm, wb_n), lambda i, f: (i, 0)),
                  pl.BlockSpec((wa_n, d), lambda i, f: (0, 0)),
                  pl.BlockSpec((wb_n, d), lambda i, f: (0, 0)),
                  pl.BlockSpec((tm, d), lambda i, f: (i, 0)),
                  row, row, row,
                  pl.BlockSpec((d, tf), lambda i, f: (0, f), **once),
                  pl.BlockSpec((d, tf), lambda i, f: (0, f), **once),
                  pl.BlockSpec((tf, d), lambda i, f: (f, 0), **once)],
        out_specs=pl.BlockSpec((tm, d), lambda i, f: (i, 0)),
        out_shape=jax.ShapeDtypeStruct((t, d), F32),
        scratch_shapes=[pltpu.VMEM((tm, d), F32), pltpu.VMEM((tm, d), BF16), pltpu.VMEM((tm, d), F32)],
        compiler_params=pltpu.CompilerParams(
            dimension_semantics=("parallel", "arbitrary"), vmem_limit_bytes=VMEM_LIMIT),
        name="mix_ffn",
    )(ca, cb, w_out[:wa_n].astype(BF16), w_out[wa_n:].astype(BF16), h,
      g_mix.reshape(1, d), g_pre.reshape(1, d), g_post.reshape(1, d),
      wg.astype(BF16), wu.astype(BF16), wd.astype(BF16))


def _deltanet_kernel(xq_ref, xk_ref, xv_ref, z_ref, sm_ref, cwq_ref, cwk_ref, cwv_ref,
                     alog_ref, dtb_ref, gn_ref, o_ref,
                     xpad_ref, q_ref, k_ref, v_ref, gb_ref, bb_ref, u_ref, w_ref, qk_ref, st_ref,
                     *, ts, a_col, b_col):
    s = pl.program_id(1)
    c = CHUNK
    d = HEAD_DIM
    nh = N_HEADS

    @pl.when(s == 0)
    def _():
        xpad_ref[:, 0:8, :] = jnp.zeros((3, 8, nh * d), F32)
        st_ref[...] = jnp.zeros_like(st_ref)

    @pl.when(s != 0)
    def _():
        xpad_ref[:, 0:8, :] = xpad_ref[:, ts:ts + 8, :]

    xpad_ref[0, 8:ts + 8, :] = xq_ref[...]
    xpad_ref[1, 8:ts + 8, :] = xk_ref[...]
    xpad_ref[2, 8:ts + 8, :] = xv_ref[...]

    def conv_silu(idx, cw_ref, hs):
        cw = cw_ref[:, hs]
        acc = xpad_ref[idx, 8 - (CONV_WIDTH - 1):8 - (CONV_WIDTH - 1) + ts, hs] * cw[0:1, :]
        for j in range(1, CONV_WIDTH):
            off = 8 - (CONV_WIDTH - 1) + j
            acc = acc + xpad_ref[idx, off:off + ts, hs] * cw[j:j + 1, :]
        return _silu(acc)

    def l2norm(t):
        return t * lax.rsqrt(jnp.sum(t * t, axis=-1, keepdims=True) + EPS)

    row = _iota((c, c), 0)
    col = _iota((c, c), 1)
    tri = (col <= row)
    strict = (col < row)
    tri_f = tri.astype(F32)
    upper_f = (row <= col).astype(F32)
    eye = (row == col).astype(F32)
    gnorm = gn_ref[...]
    chunks = range(ts // c)
    rs = [slice(ci * c, (ci + 1) * c) for ci in chunks]
    tri2 = jnp.concatenate([tri_f, tri_f], axis=1).astype(BF16)
    ones2 = jnp.ones((c, 2 * c), BF16)

    def cum2(lhs2, x):
        hi, lo = _split(x)
        return jnp.dot(lhs2, jnp.concatenate([hi, lo], axis=0), preferred_element_type=F32)

    for hh in range(nh):
        hs = slice(hh * d, (hh + 1) * d)
        q_ref[:, hs] = l2norm(conv_silu(0, cwq_ref, hs)) * (d ** -0.5)
        k_ref[:, hs] = l2norm(conv_silu(1, cwk_ref, hs))
        v_ref[:, hs] = conv_silu(2, cwv_ref, hs)

        a_raw = sm_ref[:, a_col + hh:a_col + hh + 1]
        b_raw = sm_ref[:, b_col + hh:b_col + hh + 1]
        g = -jnp.exp(alog_ref[:, hh:hh + 1]) * _softplus(a_raw + dtb_ref[:, hh:hh + 1])
        gb_ref[:, hs] = jnp.broadcast_to(g, (ts, d))
        bb_ref[:, hs] = jnp.broadcast_to(_sigmoid(b_raw), (ts, d))

        q = [q_ref[r, hs] for r in rs]
        k = [k_ref[r, hs] for r in rs]
        beta = [bb_ref[r, hs] for r in rs]
        gb = [gb_ref[r, hs] for r in rs]
        gc = [cum2(tri2, x) for x in gb]
        gc_row = [cum2(ones2, x[:, :c] * upper_f) for x in gb]
        decay = [jnp.where(tri, jnp.exp(jnp.minimum(a[:, :c] - b, 0.0)), 0.0) for a, b in zip(gc, gc_row)]
        kk = [_mm_nt(x, x) for x in k]
        n = [-jnp.where(strict, b[:, :c] * x * dc, 0.0) for b, x, dc in zip(beta, kk, decay)]
        inv = [eye + x for x in n]
        for step in range(5):
            nb = [x.astype(BF16) for x in n]
            n = [jnp.dot(x, x, preferred_element_type=F32) for x in nb]
            inv = [iv + _mm(iv, x) for iv, x in zip(inv, n)]
        egc = [jnp.exp(x) for x in gc]
        gl = [x[c - 1:c, :] for x in gc]
        inv_l = [x.astype(BF16) for x in inv]
        u = [_mm(a, v_ref[r, hs] * b) for a, r, b in zip(inv_l, rs, beta)]
        w = [_mm(a, x * (b * e)) for a, x, b, e in zip(inv_l, k, beta, egc)]
        qk = [_mm_nt(a, b) * dc for a, b, dc in zip(q, k, decay)]
        for ci in chunks:
            r = rs[ci]
            u_ref[r, hs] = u[ci]
            w_ref[r, hs] = w[ci]
            qk_ref[hh, r, :] = qk[ci]
            q_ref[r, hs] = q[ci] * egc[ci]
            k_ref[r, hs] = k[ci] * jnp.exp(gl[ci] - gc[ci])
            gb_ref[r, hs] = jnp.broadcast_to(jnp.exp(gl[ci]), (c, d))

    def chunk_body(ci, carry):
        r0 = pl.multiple_of(ci * c, c)
        rows = pl.ds(r0, c)
        hss = [slice(hh * d, (hh + 1) * d) for hh in range(nh)]
        st = [st_ref[hh] for hh in range(nh)]
        w_st = [_mm(w_ref[rows, hs], s_) for hs, s_ in zip(hss, st)]
        q_st = [_mm(q_ref[rows, hs], s_) for hs, s_ in zip(hss, st)]
        v_new = [u_ref[rows, hs] - x for hs, x in zip(hss, w_st)]
        o = [a + _mm(qk_ref[hh, rows, :], v) for hh, (a, v) in enumerate(zip(q_st, v_new))]
        kv = [_mm_tn(k_ref[rows, hs], v) for hs, v in zip(hss, v_new)]
        for hh, hs in enumerate(hss):
            st_ref[hh] = st[hh] * gb_ref[pl.ds(r0, 1), hs] + kv[hh]
            o_ref[rows, hs] = (_rms(o[hh], gnorm) * _silu(z_ref[rows, hs])).astype(o_ref.dtype)
        return carry

    lax.fori_loop(0, ts // c, chunk_body, 0)


def _deltanet(p32, conv_w, a_log, dt_bias, a_norm_g, *, ts, cols):
    bsz, s, _ = p32.shape
    d = HEAD_DIM
    nh = N_HEADS
    w = nh * d
    pad = lambda t: jnp.pad(t.astype(F32), (0, d - t.shape[0])).reshape(1, d)
    kernel = functools.partial(_deltanet_kernel, ts=ts, a_col=cols["a_lane"], b_col=cols["b_lane"])
    tile = lambda name: pl.BlockSpec((None, ts, w), lambda b, i: (b, i, cols[name] // nh))
    conv = lambda k: pl.BlockSpec((CONV_WIDTH, w), lambda b, i: (0, k))
    row = pl.BlockSpec((1, d), lambda b, i: (0, 0))
    return pl.pallas_call(
        kernel,
        grid=(bsz, s // ts),
        in_specs=[tile("qa"), tile("ka"), tile("va"), tile("za"),
                  pl.BlockSpec((None, ts, d), lambda b, i: (b, i, cols["small"])),
                  conv(0), conv(1), conv(2), row, row, row],
        out_specs=pl.BlockSpec((None, ts, w), lambda b, i: (b, i, 0)),
        out_shape=jax.ShapeDtypeStruct((bsz, s, w), BF16),
        scratch_shapes=[pltpu.VMEM((3, ts + 8, w), F32)]
        + [pltpu.VMEM((ts, w), F32) for _ in range(7)]
        + [pltpu.VMEM((nh, ts, CHUNK), F32), pltpu.VMEM((nh, d, d), F32)],
        compiler_params=pltpu.CompilerParams(
            dimension_semantics=("parallel", "arbitrary"), vmem_limit_bytes=VMEM_LIMIT),
        name="deltanet",
    )(p32, p32, p32, p32, p32, conv_w.astype(F32), conv_w.astype(F32), conv_w.astype(F32),
      pad(a_log), pad(dt_bias), a_norm_g.astype(F32).reshape(1, d))


def _hgrn2_kernel(q_ref, f_ref, i_ref, gate_ref, lb_ref, gn_ref, o_ref,
                  qs_ref, ks_ref, gc_ref, st_ref, *, ts):
    s = pl.program_id(1)
    c = CHUNK
    d = HEAD_DIM
    nh = N_HEADS
    SUB = 16

    @pl.when(s == 0)
    def _():
        st_ref[...] = jnp.zeros_like(st_ref)

    lb = lb_ref[...]
    f_raw = f_ref[...]
    log_sig = jnp.minimum(f_raw, 0.0) - jnp.log1p(jnp.exp(-jnp.abs(f_raw)))
    la = jnp.log(lb)
    lbb = jnp.log1p(-lb) + log_sig
    log_f = jnp.maximum(la, lbb) + jnp.log1p(jnp.exp(-jnp.abs(la - lbb)))
    qs_ref[...] = _silu(q_ref[...])
    ks_ref[...] = (1.0 - lb) * _sigmoid(-f_raw)

    row = _iota((c, c), 0)
    col = _iota((c, c), 1)
    tri_f = (col <= row).astype(F32)
    ones_dd = jnp.ones((d, d), BF16)
    rows_8d = _iota((8, d), 0)
    gnorm = gn_ref[...]

    tri2 = jnp.concatenate([tri_f, tri_f], axis=1).astype(BF16)
    for ci in range(ts // c):
        hi, lo = _split(log_f[ci * c:(ci + 1) * c, :])
        gc_ref[ci * c:(ci + 1) * c, :] = jnp.dot(tri2, jnp.concatenate([hi, lo], axis=0),
                                                 preferred_element_type=F32)

    blocks = [(sb * SUB, (sb + 1) * SUB) for sb in range(c // SUB)]

    def chunk_loop(ci, carry):
        r0 = pl.multiple_of(ci * c, c)
        rows = pl.ds(r0, c)
        hss = [slice(hh * d, (hh + 1) * d) for hh in range(nh)]
        q = [qs_ref[rows, hs] for hs in hss]
        k = [ks_ref[rows, hs] for hs in hss]
        v = [i_ref[rows, hs] for hs in hss]
        gc = [gc_ref[rows, hs] for hs in hss]

        def near_products(q, k, gc):
            prods = []
            for top, end in blocks:
                for j in range(top, end):
                    lo = (j // 8) * 8
                    e = jnp.exp2(gc[lo:end, :] - gc[j:j + 1, :])
                    if j % 8:
                        head = jnp.where(rows_8d >= j - lo, e[:8], 0.0)
                        e = jnp.concatenate([head, e[8:]], axis=0) if lo + 8 < end else head
                    prods.append(q[lo:end, :] * k[j:j + 1, :] * e)
            return jnp.concatenate(prods, axis=0).astype(BF16)

        def far_operands(q, k, gc):
            out = []
            for top, end in blocks[1:]:
                g_b = gc[top - 1:top, :]
                out.append((q[top:end, :] * jnp.exp(gc[top:end, :] - g_b),
                            k[:top, :] * jnp.exp(jnp.minimum(g_b - gc[:top, :], 0.0))))
            return out

        near = [near_products(a, b, g * LOG2E) for a, b, g in zip(q, k, gc)]
        far_ops = [far_operands(*x) for x in zip(q, k, gc)]
        st = [st_ref[hh] for hh in range(nh)]
        gl = [x[c - 1:c, :] for x in gc]
        sums = [jnp.dot(x, ones_dd, preferred_element_type=F32) for x in near]
        qk_far = [[_mm_nt(qe, ke) for qe, ke in ops] for ops in far_ops]
        far = [[_mm(a, vv[:top, :]) for a, (top, _) in zip(qs, blocks[1:])] for qs, vv in zip(qk_far, v)]
        o_st = [_mm_nt(a * jnp.exp(g), s_) for a, g, s_ in zip(q, gc, st)]
        kv = [_mm_tn(vv, kk * jnp.exp(g_l - g)) for vv, kk, g_l, g in zip(v, k, gl, gc)]

        for hh, hs in enumerate(hss):
            groups = [jnp.zeros((8, d), F32) for _ in range(c // 8)]
            at = 0
            for top, end in blocks:
                for j in range(top, end):
                    v_j = v[hh][j:j + 1, :]
                    for g in range(j // 8, end // 8):
                        groups[g] = groups[g] + sums[hh][at:at + 8, :] * v_j
                        at += 8
            for f, (top, end) in zip(far[hh], blocks[1:]):
                for g in range(top // 8, end // 8):
                    groups[g] = groups[g] + f[(g * 8 - top):(g * 8 - top + 8), :]
            o = jnp.concatenate(groups, axis=0) + o_st[hh]
            st_ref[hh] = st[hh] * jnp.exp(gl[hh]) + kv[hh]
            o_ref[rows, hs] = (_rms(o, gnorm) * _silu(gate_ref[rows, hs])).astype(o_ref.dtype)
        return carry

    lax.fori_loop(0, ts // c, chunk_loop, 0)


def _hgrn2(p32, lb, d_norm_g, *, ts, cols):
    bsz, s, _ = p32.shape
    d = HEAD_DIM
    nh = N_HEADS
    w = nh * d
    kernel = functools.partial(_hgrn2_kernel, ts=ts)
    tile = lambda name: pl.BlockSpec((None, ts, w), lambda b, i: (b, i, cols[name] // nh))
    return pl.pallas_call(
        kernel,
        grid=(bsz, s // ts),
        in_specs=[tile("qd"), tile("fd"), tile("id"), tile("gd"),
                  pl.BlockSpec((1, w), lambda b, i: (0, 0)),
                  pl.BlockSpec((1, d), lambda b, i: (0, 0))],
        out_specs=pl.BlockSpec((None, ts, w), lambda b, i: (b, i, 0)),
        out_shape=jax.ShapeDtypeStruct((bsz, s, w), BF16),
        scratch_shapes=[pltpu.VMEM((ts, w), F32), pltpu.VMEM((ts, w), F32),
                        pltpu.VMEM((ts, w), F32), pltpu.VMEM((nh, d, d), F32)],
        compiler_params=pltpu.CompilerParams(
            dimension_semantics=("parallel", "arbitrary"), vmem_limit_bytes=VMEM_LIMIT),
        name="hgrn2",
    )(p32, p32, p32, p32, lb.astype(F32).reshape(1, w), d_norm_g.astype(F32).reshape(1, d))


def _stickbreak_kernel(q_ref, k_ref, v_ref, o_ref, acc_ref, *, tq):
    i = pl.program_id(1)
    d = HEAD_DIM
    nh = N_HEADS
    row = _iota((tq, tq), 0)
    col = _iota((tq, tq), 1)
    causal = col < row
    later = (row > col).astype(BF16)
    later2 = jnp.concatenate([later, later], axis=0)

    heads = [slice(hh * d, (hh + 1) * d) for hh in range(nh)]

    def scores(blocks):
        jobs = [(j, dg, hs) for j, dg in blocks for hs in heads]
        z = [_mm_nt(q_ref[:, hs], k_ref[pl.ds(pl.multiple_of(j * tq, tq), tq), hs]) * (d ** -0.5)
             for j, _, hs in jobs]
        sp = [_softplus(x) for x in z]
        l1m = [jnp.where(causal, -x, 0.0) if dg else -x for x, (_, dg, _) in zip(sp, jobs)]
        rest = [jnp.dot(jnp.concatenate(_split(x), axis=1), later2, preferred_element_type=F32)
                for x in l1m]
        out = [((a - b) + r, l) for a, b, r, l in zip(z, sp, rest, l1m)]
        return [out[b * nh:(b + 1) * nh] for b in range(len(blocks))]

    def block(j, carries):
        (sc,) = scores([(j, False)])
        ps = [jnp.exp(logw + c) for (logw, _), c in zip(sc, carries)]
        pv = [_mm(p, v_ref[pl.ds(pl.multiple_of(j * tq, tq), tq), hs]) for p, hs in zip(ps, heads)]
        for hs, x in zip(heads, pv):
            acc_ref[:, hs] += x
        return tuple(c + jnp.sum(l1m, axis=-1, keepdims=True) for (_, l1m), c in zip(sc, carries))

    jp = jnp.maximum(i - 1, 0)
    live = jnp.where(i > 0, 1.0, 0.0)
    sd, sp_ = scores([(i, True), (jp, False)])
    carries = []
    for hh, hs in enumerate(heads):
        c1 = jnp.sum(sd[hh][1], axis=-1, keepdims=True)
        p_d = jnp.where(causal, jnp.exp(sd[hh][0]), 0.0)
        p_p = jnp.exp(sp_[hh][0] + c1) * live
        acc_ref[:, hs] = (_mm(p_d, v_ref[pl.ds(pl.multiple_of(i * tq, tq), tq), hs])
                          + _mm(p_p, v_ref[pl.ds(pl.multiple_of(jp * tq, tq), tq), hs]))
        carries.append(c1 + jnp.sum(sp_[hh][1], axis=-1, keepdims=True))
    carries = tuple(carries)

    def cond(c):
        worst = functools.reduce(jnp.maximum, c[1])
        return jnp.logical_and(c[0] >= 0, jnp.max(worst) >= EXP_ZERO_BELOW)

    def body(c):
        return c[0] - 1, block(c[0], c[1])

    lax.while_loop(cond, body, (i - 2, carries))
    o_ref[...] = acc_ref[...].astype(o_ref.dtype)


def _stickbreak(p16, *, tq, cols):
    bsz, s, _ = p16.shape
    nh = N_HEADS
    w = nh * HEAD_DIM
    kernel = functools.partial(_stickbreak_kernel, tq=tq)
    resident = dict(pipeline_mode=pl.Buffered(1))
    return pl.pallas_call(
        kernel,
        grid=(bsz, s // tq),
        in_specs=[pl.BlockSpec((None, tq, w), lambda b, i: (b, i, cols["qc"] // nh)),
                  pl.BlockSpec((None, s, w), lambda b, i: (b, 0, cols["kc"] // nh), **resident),
                  pl.BlockSpec((None, s, w), lambda b, i: (b, 0, cols["vc"] // nh), **resident)],
        out_specs=pl.BlockSpec((None, tq, w), lambda b, i: (b, i, 0)),
        out_shape=jax.ShapeDtypeStruct((bsz, s, w), BF16),
        scratch_shapes=[pltpu.VMEM((tq, w), F32)],
        compiler_params=pltpu.CompilerParams(
            dimension_semantics=("parallel", "arbitrary"), vmem_limit_bytes=VMEM_LIMIT),
        name="stickbreak",
    )(p16, p16, p16)


def _dsa_kernel(qi_ref, smq_ref, q_ref, sm_ref, k_ref, vt_ref, bias_ref, o_ref,
                sc_ref, scb_ref, wb_ref, qc_ref, kct_ref, bd_ref, lg_ref, *, tq, k_sel, wi_lane, wide):
    i = pl.program_id(1)
    tk = tq
    d = HEAD_DIM
    nh = N_HEADS
    ksel = float(k_sel)
    per_wide = wide // tk
    n_wide = (i + per_wide) // per_wide
    sub = 2 * tk
    lane_q = _iota((1, tq), 1)

    def tree(parts, op):
        while len(parts) > 1:
            parts = [op(parts[j], parts[j + 1]) if j + 1 < len(parts) else parts[j]
                     for j in range(0, len(parts), 2)]
        return parts[0]

    def col_fold(x, op=jnp.add, rows=8):
        return tree([x[r * rows:(r + 1) * rows] for r in range(x.shape[0] // rows)], op)

    @pl.when(i == 0)
    def _():
        def prep(g, carry):
            g0 = pl.multiple_of(g * wide, wide)
            kt = sm_ref[pl.ds(g0, wide), :].T[:IDX_DIM, :]
            hi, lo = _split(kt)
            kct_ref[:, pl.ds(g0, wide)] = jnp.concatenate([hi, lo, hi], axis=0)
            return carry
        lax.fori_loop(0, sm_ref.shape[0] // wide, prep, 0)

    smq = smq_ref[...]
    lane = _iota(smq.shape, 1)
    for hh in range(IDX_HEADS):
        qh = qi_ref[:, hh * IDX_DIM:(hh + 1) * IDX_DIM]
        hi, lo = _split(qh)
        qc_ref[hh] = jnp.concatenate([hi, hi, lo], axis=-1)
        w = jnp.sum(jnp.where(lane == wi_lane + hh, smq, 0.0), axis=-1, keepdims=True)
        wb_ref[hh] = jnp.broadcast_to(w * ((IDX_HEADS ** -0.5) * (IDX_DIM ** -0.5)), (tq, tk))

    q2t = (q_ref[...] * ((d ** -0.5) * LOG2E)).T.astype(BF16)
    zero_dq = jnp.zeros((d, tq), BF16)
    for p in range(nh // 2):
        top = jnp.concatenate([q2t[2 * p * d:(2 * p + 1) * d], zero_dq], axis=1)
        bot = jnp.concatenate([zero_dq, q2t[(2 * p + 1) * d:(2 * p + 2) * d]], axis=1)
        bd_ref[p] = jnp.concatenate([top, bot], axis=0)

    limit = i * tq + (lane_q // CHUNK + 1) * CHUNK
    rows_t = _iota((tk, tq), 0)

    def score_group(g, mm, masked):
        mn, mx = mm
        for sb in range(wide // sub):
            k0 = pl.multiple_of(g * wide + sb * sub, sub)
            kct = kct_ref[:, pl.ds(k0, sub)]
            tiles = [jnp.zeros((tq, tk), F32) for _ in range(sub // tk)]
            for hh in range(IDX_HEADS):
                s_h = jnp.dot(qc_ref[hh], kct, preferred_element_type=F32)
                for ti in range(sub // tk):
                    tiles[ti] = tiles[ti] + jnp.maximum(s_h[:, ti * tk:(ti + 1) * tk], 0.0) * wb_ref[hh]
            for ti in range(sub // tk):
                kb = pl.multiple_of(k0 + ti * tk, tk)
                sct = tiles[ti].T
                if masked:
                    adm = (kb + rows_t) < limit
                    mn = jnp.minimum(mn, col_fold(jnp.where(adm, sct, jnp.inf), jnp.minimum))
                    sct = jnp.where(adm, sct, -jnp.inf)
                else:
                    mn = jnp.minimum(mn, col_fold(sct, jnp.minimum))
                mx = jnp.maximum(mx, col_fold(sct, jnp.maximum))
                sc_ref[pl.ds(kb, tk), :] = sct
                scb_ref[pl.ds(kb, tk), :] = _floor_bf16(sct)
        return mn, mx

    def score_pair(j, mm):
        return score_group(2 * j + 1, score_group(2 * j, mm, False), False)

    n_full = n_wide - 1
    mm = lax.fori_loop(0, n_full // 2, score_pair,
                       (jnp.full((8, tq), jnp.inf, F32), jnp.full((8, tq), -jnp.inf, F32)))
    mm = lax.cond(n_full % 2 == 1, lambda c: score_group(n_full - 1, c, False), lambda c: c, mm)
    mn, mx = score_group(n_wide - 1, mm, True)

    n_pairs = (n_wide + 1) // 2

    @pl.when(n_wide % 2 == 1)
    def _():
        sc_ref[pl.ds(pl.multiple_of(n_wide * wide, wide), wide), :] = jnp.full((wide, tq), -jnp.inf, F32)
        scb_ref[pl.ds(pl.multiple_of(n_wide * wide, wide), wide), :] = jnp.full((wide, tq), -jnp.inf, BF16)
    rmin = jnp.min(mn, axis=0, keepdims=True)
    rmax = jnp.max(mx, axis=0, keepdims=True)

    def count(pred):
        def body(j, acc):
            for g in (2 * j, 2 * j + 1):
                acc = acc + col_fold(pred(sc_ref[pl.ds(pl.multiple_of(g * wide, wide), wide), :]))
            return acc
        return jnp.sum(lax.fori_loop(0, n_pairs, body, jnp.zeros((8, tq), F32)), axis=0, keepdims=True)

    def max_below(x):
        def body(j, acc):
            for g in (2 * j, 2 * j + 1):
                blk = sc_ref[pl.ds(pl.multiple_of(g * wide, wide), wide), :]
                acc = jnp.maximum(acc, col_fold(jnp.where(blk < x, blk, -jnp.inf), jnp.maximum))
            return acc
        return jnp.max(lax.fori_loop(0, n_pairs, body, jnp.full((8, tq), -jnp.inf, F32)), axis=0, keepdims=True)

    n_adm = limit.astype(F32)
    all_sel = n_adm <= ksel

    def bisect(c):
        lo, hi, c_lo = c
        mid = 0.5 * lo + 0.5 * hi
        cm = count(lambda blk: _ind(blk >= mid))
        ge = cm >= ksel
        return jnp.where(ge, mid, lo), jnp.where(ge, hi, mid), jnp.where(ge, cm, c_lo)

    def pending(c_lo, tied):
        return jnp.where(all_sel, 0.0, jnp.where(tied > 0.5, 0.0, _ind(c_lo != ksel)))

    def bisect_coarse(_, c):
        lo, hi, c_lo = c
        mid = _floor_bf16(0.5 * lo + 0.5 * hi).astype(F32)
        t_b = jnp.broadcast_to(mid, (16, tq)).astype(BF16)
        one_b = jnp.ones((16, tq), BF16)
        zero_b = jnp.zeros((16, tq), BF16)

        def body(j, acc):
            for g in (2 * j, 2 * j + 1):
                blk = scb_ref[pl.ds(pl.multiple_of(g * wide, wide), wide), :]
                ind = [jnp.where(blk[r * 16:(r + 1) * 16] >= t_b, one_b, zero_b) for r in range(wide // 16)]
                acc = acc + tree(ind, jnp.add).astype(F32)
            return acc

        acc = lax.fori_loop(0, n_pairs, body, jnp.zeros((16, tq), F32))
        cm = jnp.sum(acc, axis=0, keepdims=True)
        ge = cm >= ksel
        return jnp.where(ge, mid, lo), jnp.where(ge, hi, mid), jnp.where(ge, cm, c_lo)

    lo0 = _floor_bf16(rmin).astype(F32)
    hi0 = _floor_bf16(rmax + (jnp.abs(rmax) * (2.0 ** -6) + 1e-30)).astype(F32)
    state = lax.fori_loop(0, BISECT_COARSE, bisect_coarse, (lo0, hi0, n_adm))
    state = lax.fori_loop(0, BISECT_FIXED, lambda _, c: bisect(c), state)

    def round_cond(c):
        return jnp.max(pending(c[0][2], c[1])) > 0.5

    def round_body(c):
        st, tied, v, need = c

        def more_cond(s):
            return jnp.logical_and(s[0] < BISECT_EXTRA, jnp.max(pending(s[1][2], tied)) > 0.5)

        _, st = lax.while_loop(more_cond, lambda s: (s[0] + 1, bisect(s[1])), (jnp.int32(0), st))
        pend = pending(st[2], tied)

        def check(_):
            cand = max_below(st[1])
            c_ge = count(lambda blk: _ind(blk >= cand))
            c_gt = count(lambda blk: _ind(blk > cand))
            ok = jnp.where(pend > 0.5, _ind(c_ge >= ksel), 0.0)
            return (jnp.where(ok > 0.5, 1.0, tied), jnp.where(ok > 0.5, cand, v),
                    jnp.where(ok > 0.5, ksel - c_gt, need))

        tied, v, need = lax.cond(jnp.max(pend) > 0.5, check, lambda _: (tied, v, need), 0)
        return st, tied, v, need

    zeros1 = jnp.zeros((1, tq), F32)
    (lo_f, _, _), tied, v_tie, need = lax.while_loop(round_cond, round_body, (state, zeros1, zeros1, zeros1))
    vth = jnp.where(all_sel, F32_LOWEST, jnp.where(tied > 0.5, v_tie, lo_f))

    @pl.when(jnp.max(tied) > 0.5)
    def _():
        v_eq = jnp.where(tied > 0.5, v_tie, jnp.inf)
        incl = (_iota((tk, tk), 1) <= _iota((tk, tk), 0)).astype(BF16)

        def demote(g, seen):
            g0 = pl.multiple_of(g * wide, wide)
            xs = [sc_ref[pl.ds(g0 + pb * tk, tk), :] for pb in range(per_wide)]
            eqs = [_ind(x == v_eq) for x in xs]
            inblk = [jnp.dot(incl, e.astype(BF16), preferred_element_type=F32) for e in eqs]
            for pb in range(per_wide):
                rank = inblk[pb] + seen
                sc_ref[pl.ds(g0 + pb * tk, tk), :] = jnp.where(eqs[pb] * _ind(rank > need) > 0.5,
                                                               -jnp.inf, xs[pb])
                seen = seen + jnp.sum(col_fold(eqs[pb]), axis=0, keepdims=True)
            return seen

        lax.fori_loop(0, n_wide, demote, zeros1)

    g_near = jnp.maximum(i - 1, 0) // per_wide

    def logit_group(g, mx, near):
        out = list(mx)
        for sb in range(wide // sub):
            k0 = pl.multiple_of(g * wide + sb * sub, sub)
            sel = sc_ref[pl.ds(k0, sub), :] >= vth
            for p in range(nh // 2):
                pair = jnp.dot(k_ref[pl.ds(k0, sub), 2 * p * d:(2 * p + 2) * d], bd_ref[p],
                               preferred_element_type=F32)
                for hh in (2 * p, 2 * p + 1):
                    lm = pair[:, (hh - 2 * p) * tq:(hh - 2 * p + 1) * tq]
                    if near:
                        back = [jnp.clip(i - (g * per_wide + sb * (sub // tk) + pb), 0, 2)
                                for pb in range(sub // tk)]
                        lm = lm + jnp.concatenate([bias_ref[bk, hh] for bk in back], axis=0)
                    lm = jnp.where(sel, lm, NEG_BIG)
                    lg_ref[hh, pl.ds(k0, sub), :] = lm
                    out[hh] = jnp.maximum(out[hh], col_fold(lm, jnp.maximum))
        return tuple(out)

    mx = tuple(jnp.full((8, tq), NEG_BIG, F32) for _ in range(nh))
    def logit_pair(j, mx, near):
        return logit_group(2 * j + 1, logit_group(2 * j, mx, near), near)

    far_pairs = g_near // 2
    mx = lax.fori_loop(0, far_pairs, functools.partial(logit_pair, near=False), mx)
    mx = lax.fori_loop(far_pairs, n_pairs, functools.partial(logit_pair, near=True), mx)
    m_q = [jnp.max(mx[hh], axis=0, keepdims=True) for hh in range(nh)]

    ones_rows = jnp.ones((8, wide), BF16)

    def pv_pair(j, carry):
        ls, accs = list(carry[0]), list(carry[1])
        jobs = [(pl.multiple_of(g * wide, wide), hh) for g in (2 * j, 2 * j + 1) for hh in range(nh)]
        ps = [jnp.exp2(lg_ref[hh, pl.ds(g0, wide), :] - m_q[hh]).astype(BF16) for g0, hh in jobs]
        outs = [jnp.dot(jnp.concatenate([vt_ref[hh * d:(hh + 1) * d, pl.ds(g0, wide)], ones_rows], axis=0),
                        p, preferred_element_type=F32) for (g0, hh), p in zip(jobs, ps)]
        for (_, hh), out in zip(jobs, outs):
            ls[hh] = ls[hh] + out[d:]
            accs[hh] = accs[hh] + out[:d]
        return tuple(ls), tuple(accs)

    ls, accs = lax.fori_loop(0, n_pairs, pv_pair,
                             (tuple(jnp.zeros((8, tq), F32) for _ in range(nh)),
                              tuple(jnp.zeros((d, tq), F32) for _ in range(nh))))
    for hh in range(nh):
        o_ref[:, hh * d:(hh + 1) * d] = (accs[hh] / ls[hh][0:1]).T.astype(o_ref.dtype)


def _dsa(p32, p16, vt, bias_tiles, *, tq, cols):
    bsz, s, _ = p32.shape
    d = HEAD_DIM
    nh = N_HEADS
    wide = 4 * tq
    k_sel = min(TOPK_MAX, s // 4)
    w512 = nh * d
    kernel = functools.partial(_dsa_kernel, tq=tq, k_sel=k_sel, wi_lane=cols["wi_lane"], wide=wide)
    resident = dict(pipeline_mode=pl.Buffered(1))
    return pl.pallas_call(
        kernel,
        grid=(bsz, s // tq),
        in_specs=[pl.BlockSpec((None, tq, w512), lambda b, i: (b, i, cols["qi"] // nh)),
                  pl.BlockSpec((None, tq, d), lambda b, i: (b, i, cols["small"])),
                  pl.BlockSpec((None, tq, w512), lambda b, i: (b, i, cols["qb"] // nh)),
                  pl.BlockSpec((None, s, d), lambda b, i: (b, 0, cols["small"]), **resident),
                  pl.BlockSpec((None, s, w512), lambda b, i: (b, 0, cols["kb"] // nh), **resident),
                  pl.BlockSpec((w512, s), lambda b, i: (0, b), **resident),
                  pl.BlockSpec((3, nh, tq, tq), lambda b, i: (0, 0, 0, 0), **resident)],
        out_specs=pl.BlockSpec((None, tq, w512), lambda b, i: (b, i, 0)),
        out_shape=jax.ShapeDtypeStruct((bsz, s, w512), BF16),
        scratch_shapes=[pltpu.VMEM((s, tq), F32),
                        pltpu.VMEM((s, tq), BF16),
                        pltpu.VMEM((IDX_HEADS, tq, tq), F32),
                        pltpu.VMEM((IDX_HEADS, tq, 3 * IDX_DIM), BF16),
                        pltpu.VMEM((3 * IDX_DIM, s), BF16),
                        pltpu.VMEM((nh // 2, 2 * d, 2 * tq), BF16),
                        pltpu.VMEM((nh, s, tq), F32)],
        compiler_params=pltpu.CompilerParams(
            dimension_semantics=("parallel", "arbitrary"), vmem_limit_bytes=VMEM_LIMIT),
        name="dsa",
    )(p32, p32, p32, p32, p16, vt, bias_tiles)


def _t5_bucket(rel):
    nb = REL_BUCKETS // 2
    max_exact = nb // 2
    ret = jnp.where(rel > 0, nb, 0)
    n = jnp.abs(rel)
    large = max_exact + (jnp.log(jnp.maximum(n, 1).astype(F32) / max_exact)
                         / math.log(REL_MAX_DIST / max_exact) * (nb - max_exact)).astype(jnp.int32)
    large = jnp.minimum(large, nb - 1)
    return ret + jnp.where(n < max_exact, n, large)


def _bias_tiles(rel_table, tq):
    assert tq >= REL_MAX_DIST
    t = jnp.arange(tq)
    back = jnp.arange(3)
    rel = (t[None, None, :] - back[:, None, None] * tq) - t[None, :, None]
    onehot = (_t5_bucket(rel)[..., None] == jnp.arange(REL_BUCKETS)).astype(F32)
    tiles = jnp.einsum("bqkn,nh->bhkq", onehot, rel_table.astype(F32),
                       precision=HIGHEST)
    return (tiles - tiles[2:3]) * LOG2E


def _even_layout(w_in):
    d = HEAD_DIM
    a_w = 2 * N_HEADS * d + N_HEADS * d
    offs = {}
    o = 0
    for name, w in (("qkv", a_w), ("z", N_HEADS * d), ("a", N_HEADS), ("b", N_HEADS),
                    ("qb", N_HEADS * d), ("kb", N_HEADS * d), ("vb", N_HEADS * d),
                    ("qi", IDX_HEADS * IDX_DIM), ("ki", IDX_DIM), ("wi", IDX_HEADS)):
        offs[name] = (o, o + w)
        o += w
    assert o == w_in.shape[1]
    sl = lambda n: w_in[:, offs[n][0]:offs[n][1]]
    small_w = IDX_DIM + 2 * N_HEADS + IDX_HEADS
    small_pad = -small_w % d
    zeros = lambda n: jnp.zeros((w_in.shape[0], n), w_in.dtype)
    w32 = jnp.concatenate([sl("qkv"), sl("z"), sl("qb"), sl("qi"),
                           sl("ki"), sl("a"), sl("b"), sl("wi"), zeros(small_pad)], axis=1)
    n32 = w32.shape[1]
    tn = n32 // 5
    assert tn * 5 == n32 and tn % d == 0
    w16 = jnp.concatenate([sl("kb"), zeros(tn - N_HEADS * d)], axis=1)
    nh = N_HEADS
    cols = dict(qa=0, ka=nh, va=2 * nh, za=3 * nh, qb=4 * nh, qi=5 * nh, small=6 * nh, kb=0,
                a_lane=IDX_DIM, b_lane=IDX_DIM + nh, wi_lane=IDX_DIM + 2 * nh, n32=n32, tn=tn)
    return jnp.concatenate([w32, w16], axis=1).astype(BF16), sl("vb").T.astype(BF16), cols


def kernel(x, norm_g, w_in_even, conv_w_even, a_log_even, dt_bias_even, a_norm_even, w_out_even,
           rel_bias, w_in_odd, lb_logits, d_norm_odd, w_out_odd, w_gate, w_up, w_down):
    bsz, s, d = x.shape
    t = bsz * s
    depth = norm_g.shape[0]
    nh = N_HEADS
    tq = Q_TILE
    lb_all = jnp.cumsum(jax.nn.softmax(lb_logits.astype(F32), axis=0), axis=0)
    lb_all = lb_all - lb_all[:1]
    odd_cols = dict(qc=0, kc=nh, vc=2 * nh, qd=0, fd=nh, id=2 * nh, gd=3 * nh)
    bias_tiles = _bias_tiles(rel_bias, tq)

    h = x.reshape(t, d)
    for l in range(depth):
        if l % 2 == 0:
            e = l // 2
            w_even, w_vt, cols = _even_layout(w_in_even[e])
            p32, p16, vt = _norm_matmul(h, norm_g[l, 0], w_even, tm=PROJ_TILE, tn=cols["tn"], n32=cols["n32"],
                                        w_t=w_vt)
            p32 = p32.reshape(bsz, s, -1)
            p16 = p16.reshape(bsz, s, -1)
            o_1 = _deltanet(p32, conv_w_even[e], a_log_even[e], dt_bias_even[e], a_norm_even[e],
                            ts=min(SEQ_TILE, s), cols=cols)
            o_2 = _dsa(p32, p16, vt, bias_tiles, tq=tq, cols=cols)
            w_out = w_out_even[e]
        else:
            o = l // 2
            n16 = 3 * nh * HEAD_DIM
            w_odd = jnp.concatenate([w_in_odd[o][:, n16:], w_in_odd[o][:, :n16]], axis=1).astype(BF16)
            p32, p16 = _norm_matmul(h, norm_g[l, 0], w_odd, tm=PROJ_TILE, tn=ODD_COL_TILE, n32=w_odd.shape[1] - n16)
            p32 = p32.reshape(bsz, s, -1)
            p16 = p16.reshape(bsz, s, -1)
            o_1 = _stickbreak(p16, tq=tq, cols=odd_cols)
            o_2 = _hgrn2(p32, lb_all[l], d_norm_odd[o], ts=min(SEQ_TILE, s), cols=odd_cols)
            w_out = w_out_odd[o]
        h = _mix_ffn(o_1.reshape(t, -1), o_2.reshape(t, -1), w_out, h, norm_g[l, 1], norm_g[l, 2], norm_g[l, 3],
                     w_gate[l], w_up[l], w_down[l], tm=ROW_TILE, tf=FFN_TILE)
    return h.reshape(bsz, s, d)
```

```python
import functools
import math

import jax
import jax.numpy as jnp
from jax import lax
from jax.experimental import pallas as pl
from jax.experimental.pallas import tpu as pltpu

F32 = jnp.float32
BF16 = jnp.bfloat16
HIGHEST = lax.Precision.HIGHEST

CHUNK = 64
HEAD_DIM = 128
N_HEADS = 4
IDX_HEADS = 8
IDX_DIM = 64
TOPK_MAX = 256
CONV_WIDTH = 4
REL_BUCKETS = 32
REL_MAX_DIST = 128
EPS = 1e-6
NEG_BIG = -1e30
LOG2E = 1.4426950408889634
BISECT_COARSE = 12
BISECT_FIXED = 8
BISECT_EXTRA = 6
F32_LOWEST = -3.4028234663852886e38
EXP_ZERO_BELOW = -104.0
VMEM_LIMIT = 56 * 1024 * 1024

PROJ_TILE = 2048
ROW_TILE = 512
SEQ_TILE = 512
Q_TILE = 128
ODD_COL_TILE = 512
FFN_TILE = 2816


def _mm(a, b):
    return jnp.dot(a.astype(BF16), b.astype(BF16), preferred_element_type=F32)


def _mm_nt(a, b):
    return lax.dot_general(a.astype(BF16), b.astype(BF16), (((1,), (1,)), ((), ())),
                           preferred_element_type=F32)


def _mm_tn(a, b):
    return lax.dot_general(a.astype(BF16), b.astype(BF16), (((0,), (0,)), ((), ())),
                           preferred_element_type=F32)


def _split(x):
    hi = x.astype(BF16)
    return hi, (x - hi.astype(F32)).astype(BF16)


def _floor_bf16(x):
    bits = pltpu.bitcast(x, jnp.int32)
    down = jnp.where(bits >= 0, bits, bits + 0xFFFF) & jnp.int32(-65536)
    return pltpu.bitcast(down, F32).astype(BF16)


def _sigmoid(x):
    return 1.0 / (1.0 + jnp.exp(-x))


def _silu(x):
    return x * _sigmoid(x)


def _softplus(x):
    return jnp.maximum(x, 0.0) + jnp.log1p(jnp.exp(-jnp.abs(x)))


def _rms(x, g):
    return x * lax.rsqrt(jnp.mean(x * x, axis=-1, keepdims=True) + EPS) * g


def _iota(shape, dim):
    return lax.broadcasted_iota(jnp.int32, shape, dim)


def _ind(mask):
    return jnp.where(mask, 1.0, 0.0)


def _norm_matmul_kernel(x_ref, g_ref, w_ref, *rest, n_t, tiles32):
    if n_t:
        wt_ref, o32_ref, o16_ref, ot_ref, xn_ref = rest
    else:
        o32_ref, o16_ref, xn_ref = rest
    j = pl.program_id(1)

    @pl.when(j == 0)
    def _():
        xn_ref[...] = _rms(x_ref[...], g_ref[...]).astype(BF16)
        if n_t:
            ot_ref[...] = lax.dot_general(wt_ref[...], xn_ref[...], (((1,), (1,)), ((), ())),
                                          preferred_element_type=F32).astype(BF16)

    y = jnp.dot(xn_ref[...], w_ref[...], preferred_element_type=F32)

    @pl.when(j < tiles32)
    def _():
        o32_ref[...] = y

    @pl.when(j >= tiles32)
    def _():
        o16_ref[...] = y.astype(BF16)


def _norm_matmul(x, g, w, *, tm, tn, n32, w_t=None):
    t, d = x.shape
    n = w.shape[1]
    n_t = 0 if w_t is None else w_t.shape[0]
    tiles32 = n32 // tn
    assert tiles32 * tn == n32 and (n - n32) % tn == 0 and 0 < n32 < n
    in_specs = [pl.BlockSpec((tm, d), lambda i, j: (i, 0)),
                pl.BlockSpec((1, d), lambda i, j: (0, 0)),
                pl.BlockSpec((d, tn), lambda i, j: (0, j))]
    out_specs = [pl.BlockSpec((tm, tn), lambda i, j: (i, jnp.minimum(j, tiles32 - 1))),
                 pl.BlockSpec((tm, tn), lambda i, j: (i, jnp.maximum(j - tiles32, 0)))]
    out_shape = [jax.ShapeDtypeStruct((t, n32), F32), jax.ShapeDtypeStruct((t, n - n32), BF16)]
    args = [x, g.reshape(1, d), w]
    if n_t:
        in_specs.append(pl.BlockSpec((n_t, d), lambda i, j: (0, 0)))
        out_specs.append(pl.BlockSpec((n_t, tm), lambda i, j: (0, i)))
        out_shape.append(jax.ShapeDtypeStruct((n_t, t), BF16))
        args.append(w_t)
    return pl.pallas_call(
        functools.partial(_norm_matmul_kernel, n_t=n_t, tiles32=tiles32),
        grid=(t // tm, n // tn),
        in_specs=in_specs,
        out_specs=out_specs,
        out_shape=out_shape,
        scratch_shapes=[pltpu.VMEM((tm, d), BF16)],
        compiler_params=pltpu.CompilerParams(
            dimension_semantics=("parallel", "arbitrary"), vmem_limit_bytes=VMEM_LIMIT),
        name="norm_matmul",
    )(*args)


def _mix_ffn_kernel(ca_ref, cb_ref, wa_ref, wb_ref, h_ref, gmix_ref, gpre_ref, gpost_ref,
                    wg_ref, wu_ref, wd_ref, o_ref, h1_ref, xn_ref, acc_ref):
    f = pl.program_id(1)

    @pl.when(f == 0)
    def _():
        y = (jnp.dot(ca_ref[...], wa_ref[...], preferred_element_type=F32)
             + jnp.dot(cb_ref[...], wb_ref[...], preferred_element_type=F32))
        h1 = h_ref[...] + _rms(y, gmix_ref[...])
        h1_ref[...] = h1
        xn_ref[...] = _rms(h1, gpre_ref[...]).astype(BF16)
        acc_ref[...] = jnp.zeros_like(acc_ref)

    xn = xn_ref[...]
    gate = jnp.dot(xn, wg_ref[...], preferred_element_type=F32)
    up = jnp.dot(xn, wu_ref[...], preferred_element_type=F32)
    act = (_silu(gate) * up).astype(BF16)
    acc_ref[...] += jnp.dot(act, wd_ref[...], preferred_element_type=F32)

    @pl.when(f == pl.num_programs(1) - 1)
    def _():
        o_ref[...] = h1_ref[...] + _rms(acc_ref[...], gpost_ref[...])


def _mix_ffn(ca, cb, w_out, h, g_mix, g_pre, g_post, wg, wu, wd, *, tm, tf):
    t, d = h.shape
    ff = wg.shape[1]
    wa_n = ca.shape[1]
    wb_n = cb.shape[1]
    row = pl.BlockSpec((1, d), lambda i, f: (0, 0))
    once = dict(pipeline_mode=pl.Buffered(1)) if tf == ff else {}
    return pl.pallas_call(
        _mix_ffn_kernel,
        grid=(t // tm, ff // tf),
        in_specs=[pl.BlockSpec((tm, wa_n), lambda i, f: (i, 0)),
                  pl.BlockSpec((tm, wb_n), lambda i, f: (i, 0)),
                  pl.BlockSpec((wa_n, d), lambda i, f: (0, 0)),
                  pl.BlockSpec((wb_n, d), lambda i, f: (0, 0)),
                  pl.BlockSpec((tm, d), lambda i, f: (i, 0)),
                  row, row, row,
                  pl.BlockSpec((d, tf), lambda i, f: (0, f), **once),
                  pl.BlockSpec((d, tf), lambda i, f: (0, f), **once),
                  pl.BlockSpec((tf, d), lambda i, f: (f, 0), **once)],
        out_specs=pl.BlockSpec((tm, d), lambda i, f: (i, 0)),
        out_shape=jax.ShapeDtypeStruct((t, d), F32),
        scratch_shapes=[pltpu.VMEM((tm, d), F32), pltpu.VMEM((tm, d), BF16), pltpu.VMEM((tm, d), F32)],
        compiler_params=pltpu.CompilerParams(
            dimension_semantics=("parallel", "arbitrary"), vmem_limit_bytes=VMEM_LIMIT),
        name="mix_ffn",
    )(ca, cb, w_out[:wa_n].astype(BF16), w_out[wa_n:].astype(BF16), h,
      g_mix.reshape(1, d), g_pre.reshape(1, d), g_post.reshape(1, d),
      wg.astype(BF16), wu.astype(BF16), wd.astype(BF16))


def _deltanet_kernel(xq_ref, xk_ref, xv_ref, z_ref, sm_ref, cwq_ref, cwk_ref, cwv_ref,
                     alog_ref, dtb_ref, gn_ref, o_ref,
                     xpad_ref, q_ref, k_ref, v_ref, gb_ref, bb_ref, u_ref, w_ref, qk_ref, st_ref,
                     *, ts, a_col, b_col):
    s = pl.program_id(1)
    c = CHUNK
    d = HEAD_DIM
    nh = N_HEADS

    @pl.when(s == 0)
    def _():
        xpad_ref[:, 0:8, :] = jnp.zeros((3, 8, nh * d), F32)
        st_ref[...] = jnp.zeros_like(st_ref)

    @pl.when(s != 0)
    def _():
        xpad_ref[:, 0:8, :] = xpad_ref[:, ts:ts + 8, :]

    xpad_ref[0, 8:ts + 8, :] = xq_ref[...]
    xpad_ref[1, 8:ts + 8, :] = xk_ref[...]
    xpad_ref[2, 8:ts + 8, :] = xv_ref[...]

    def conv_silu(idx, cw_ref, hs):
        cw = cw_ref[:, hs]
        acc = xpad_ref[idx, 8 - (CONV_WIDTH - 1):8 - (CONV_WIDTH - 1) + ts, hs] * cw[0:1, :]
        for j in range(1, CONV_WIDTH):
            off = 8 - (CONV_WIDTH - 1) + j
            acc = acc + xpad_ref[idx, off:off + ts, hs] * cw[j:j + 1, :]
        return _silu(acc)

    def l2norm(t):
        return t * lax.rsqrt(jnp.sum(t * t, axis=-1, keepdims=True) + EPS)

    row = _iota((c, c), 0)
    col = _iota((c, c), 1)
    tri = (col <= row)
    strict = (col < row)
    tri_f = tri.astype(F32)
    upper_f = (row <= col).astype(F32)
    eye = (row == col).astype(F32)
    gnorm = gn_ref[...]
    chunks = range(ts // c)
    rs = [slice(ci * c, (ci + 1) * c) for ci in chunks]
    tri2 = jnp.concatenate([tri_f, tri_f], axis=1).astype(BF16)
    ones2 = jnp.ones((c, 2 * c), BF16)

    def cum2(lhs2, x):
        hi, lo = _split(x)
        return jnp.dot(lhs2, jnp.concatenate([hi, lo], axis=0), preferred_element_type=F32)

    for hh in range(nh):
        hs = slice(hh * d, (hh + 1) * d)
        q_ref[:, hs] = l2norm(conv_silu(0, cwq_ref, hs)) * (d ** -0.5)
        k_ref[:, hs] = l2norm(conv_silu(1, cwk_ref, hs))
        v_ref[:, hs] = conv_silu(2, cwv_ref, hs)

        a_raw = sm_ref[:, a_col + hh:a_col + hh + 1]
        b_raw = sm_ref[:, b_col + hh:b_col + hh + 1]
        g = -jnp.exp(alog_ref[:, hh:hh + 1]) * _softplus(a_raw + dtb_ref[:, hh:hh + 1])
        gb_ref[:, hs] = jnp.broadcast_to(g, (ts, d))
        bb_ref[:, hs] = jnp.broadcast_to(_sigmoid(b_raw), (ts, d))

        q = [q_ref[r, hs] for r in rs]
        k = [k_ref[r, hs] for r in rs]
        beta = [bb_ref[r, hs] for r in rs]
        gb = [gb_ref[r, hs] for r in rs]
        gc = [cum2(tri2, x) for x in gb]
        gc_row = [cum2(ones2, x[:, :c] * upper_f) for x in gb]
        decay = [jnp.where(tri, jnp.exp(jnp.minimum(a[:, :c] - b, 0.0)), 0.0) for a, b in zip(gc, gc_row)]
        kk = [_mm_nt(x, x) for x in k]
        n = [-jnp.where(strict, b[:, :c] * x * dc, 0.0) for b, x, dc in zip(beta, kk, decay)]
        inv = [eye + x for x in n]
        for step in range(5):
            nb = [x.astype(BF16) for x in n]
            n = [jnp.dot(x, x, preferred_element_type=F32) for x in nb]
            inv = [iv + _mm(iv, x) for iv, x in zip(inv, n)]
        egc = [jnp.exp(x) for x in gc]
        gl = [x[c - 1:c, :] for x in gc]
        inv_l = [x.astype(BF16) for x in inv]
        u = [_mm(a, v_ref[r, hs] * b) for a, r, b in zip(inv_l, rs, beta)]
        w = [_mm(a, x * (b * e)) for a, x, b, e in zip(inv_l, k, beta, egc)]
        qk = [_mm_nt(a, b) * dc for a, b, dc in zip(q, k, decay)]
        for ci in chunks:
            r = rs[ci]
            u_ref[r, hs] = u[ci]
            w_ref[r, hs] = w[ci]
            qk_ref[hh, r, :] = qk[ci]
            q_ref[r, hs] = q[ci] * egc[ci]
            k_ref[r, hs] = k[ci] * jnp.exp(gl[ci] - gc[ci])
            gb_ref[r, hs] = jnp.broadcast_to(jnp.exp(gl[ci]), (c, d))

    def chunk_body(ci, carry):
        r0 = pl.multiple_of(ci * c, c)
        rows = pl.ds(r0, c)
        hss = [slice(hh * d, (hh + 1) * d) for hh in range(nh)]
        st = [st_ref[hh] for hh in range(nh)]
        w_st = [_mm(w_ref[rows, hs], s_) for hs, s_ in zip(hss, st)]
        q_st = [_mm(q_ref[rows, hs], s_) for hs, s_ in zip(hss, st)]
        v_new = [u_ref[rows, hs] - x for hs, x in zip(hss, w_st)]
        o = [a + _mm(qk_ref[hh, rows, :], v) for hh, (a, v) in enumerate(zip(q_st, v_new))]
        kv = [_mm_tn(k_ref[rows, hs], v) for hs, v in zip(hss, v_new)]
        for hh, hs in enumerate(hss):
            st_ref[hh] = st[hh] * gb_ref[pl.ds(r0, 1), hs] + kv[hh]
            o_ref[rows, hs] = (_rms(o[hh], gnorm) * _silu(z_ref[rows, hs])).astype(o_ref.dtype)
        return carry

    lax.fori_loop(0, ts // c, chunk_body, 0)


def _deltanet(p32, conv_w, a_log, dt_bias, a_norm_g, *, ts, cols):
    bsz, s, _ = p32.shape
    d = HEAD_DIM
    nh = N_HEADS
    w = nh * d
    pad = lambda t: jnp.pad(t.astype(F32), (0, d - t.shape[0])).reshape(1, d)
    kernel = functools.partial(_deltanet_kernel, ts=ts, a_col=cols["a_lane"], b_col=cols["b_lane"])
    tile = lambda name: pl.BlockSpec((None, ts, w), lambda b, i: (b, i, cols[name] // nh))
    conv = lambda k: pl.BlockSpec((CONV_WIDTH, w), lambda b, i: (0, k))
    row = pl.BlockSpec((1, d), lambda b, i: (0, 0))
    return pl.pallas_call(
        kernel,
        grid=(bsz, s // ts),
        in_specs=[tile("qa"), tile("ka"), tile("va"), tile("za"),
                  pl.BlockSpec((None, ts, d), lambda b, i: (b, i, cols["small"])),
                  conv(0), conv(1), conv(2), row, row, row],
        out_specs=pl.BlockSpec((None, ts, w), lambda b, i: (b, i, 0)),
        out_shape=jax.ShapeDtypeStruct((bsz, s, w), BF16),
        scratch_shapes=[pltpu.VMEM((3, ts + 8, w), F32)]
        + [pltpu.VMEM((ts, w), F32) for _ in range(7)]
        + [pltpu.VMEM((nh, ts, CHUNK), F32), pltpu.VMEM((nh, d, d), F32)],
        compiler_params=pltpu.CompilerParams(
            dimension_semantics=("parallel", "arbitrary"), vmem_limit_bytes=VMEM_LIMIT),
        name="deltanet",
    )(p32, p32, p32, p32, p32, conv_w.astype(F32), conv_w.astype(F32), conv_w.astype(F32),
      pad(a_log), pad(dt_bias), a_norm_g.astype(F32).reshape(1, d))


def _hgrn2_kernel(q_ref, f_ref, i_ref, gate_ref, lb_ref, gn_ref, o_ref,
                  qs_ref, ks_ref, gc_ref, st_ref, *, ts):
    s = pl.program_id(1)
    c = CHUNK
    d = HEAD_DIM
    nh = N_HEADS
    SUB = 16

    @pl.when(s == 0)
    def _():
        st_ref[...] = jnp.zeros_like(st_ref)

    lb = lb_ref[...]
    f_raw = f_ref[...]
    log_sig = jnp.minimum(f_raw, 0.0) - jnp.log1p(jnp.exp(-jnp.abs(f_raw)))
    la = jnp.log(lb)
    lbb = jnp.log1p(-lb) + log_sig
    log_f = jnp.maximum(la, lbb) + jnp.log1p(jnp.exp(-jnp.abs(la - lbb)))
    qs_ref[...] = _silu(q_ref[...])
    ks_ref[...] = (1.0 - lb) * _sigmoid(-f_raw)

    row = _iota((c, c), 0)
    col = _iota((c, c), 1)
    tri_f = (col <= row).astype(F32)
    ones_dd = jnp.ones((d, d), BF16)
    rows_8d = _iota((8, d), 0)
    gnorm = gn_ref[...]

    tri2 = jnp.concatenate([tri_f, tri_f], axis=1).astype(BF16)
    for ci in range(ts // c):
        hi, lo = _split(log_f[ci * c:(ci + 1) * c, :])
        gc_ref[ci * c:(ci + 1) * c, :] = jnp.dot(tri2, jnp.concatenate([hi, lo], axis=0),
                                                 preferred_element_type=F32)

    blocks = [(sb * SUB, (sb + 1) * SUB) for sb in range(c // SUB)]

    def chunk_loop(ci, carry):
        r0 = pl.multiple_of(ci * c, c)
        rows = pl.ds(r0, c)
        hss = [slice(hh * d, (hh + 1) * d) for hh in range(nh)]
        q = [qs_ref[rows, hs] for hs in hss]
        k = [ks_ref[rows, hs] for hs in hss]
        v = [i_ref[rows, hs] for hs in hss]
        gc = [gc_ref[rows, hs] for hs in hss]

        def near_products(q, k, gc):
            prods = []
            for top, end in blocks:
                for j in range(top, end):
                    lo = (j // 8) * 8
                    e = jnp.exp2(gc[lo:end, :] - gc[j:j + 1, :])
                    if j % 8:
                        head = jnp.where(rows_8d >= j - lo, e[:8], 0.0)
                        e = jnp.concatenate([head, e[8:]], axis=0) if lo + 8 < end else head
                    prods.append(q[lo:end, :] * k[j:j + 1, :] * e)
            return jnp.concatenate(prods, axis=0).astype(BF16)

        def far_operands(q, k, gc):
            out = []
            for top, end in blocks[1:]:
                g_b = gc[top - 1:top, :]
                out.append((q[top:end, :] * jnp.exp(gc[top:end, :] - g_b),
                            k[:top, :] * jnp.exp(jnp.minimum(g_b - gc[:top, :], 0.0))))
            return out

        near = [near_products(a, b, g * LOG2E) for a, b, g in zip(q, k, gc)]
        far_ops = [far_operands(*x) for x in zip(q, k, gc)]
        st = [st_ref[hh] for hh in range(nh)]
        gl = [x[c - 1:c, :] for x in gc]
        sums = [jnp.dot(x, ones_dd, preferred_element_type=F32) for x in near]
        qk_far = [[_mm_nt(qe, ke) for qe, ke in ops] for ops in far_ops]
        far = [[_mm(a, vv[:top, :]) for a, (top, _) in zip(qs, blocks[1:])] for qs, vv in zip(qk_far, v)]
        o_st = [_mm_nt(a * jnp.exp(g), s_) for a, g, s_ in zip(q, gc, st)]
        kv = [_mm_tn(vv, kk * jnp.exp(g_l - g)) for vv, kk, g_l, g in zip(v, k, gl, gc)]

        for hh, hs in enumerate(hss):
            groups = [jnp.zeros((8, d), F32) for _ in range(c // 8)]
            at = 0
            for top, end in blocks:
                for j in range(top, end):
                    v_j = v[hh][j:j + 1, :]
                    for g in range(j // 8, end // 8):
                        groups[g] = groups[g] + sums[hh][at:at + 8, :] * v_j
                        at += 8
            for f, (top, end) in zip(far[hh], blocks[1:]):
                for g in range(top // 8, end // 8):
                    groups[g] = groups[g] + f[(g * 8 - top):(g * 8 - top + 8), :]
            o = jnp.concatenate(groups, axis=0) + o_st[hh]
            st_ref[hh] = st[hh] * jnp.exp(gl[hh]) + kv[hh]
            o_ref[rows, hs] = (_rms(o, gnorm) * _silu(gate_ref[rows, hs])).astype(o_ref.dtype)
        return carry

    lax.fori_loop(0, ts // c, chunk_loop, 0)


def _hgrn2(p32, lb, d_norm_g, *, ts, cols):
    bsz, s, _ = p32.shape
    d = HEAD_DIM
    nh = N_HEADS
    w = nh * d
    kernel = functools.partial(_hgrn2_kernel, ts=ts)
    tile = lambda name: pl.BlockSpec((None, ts, w), lambda b, i: (b, i, cols[name] // nh))
    return pl.pallas_call(
        kernel,
        grid=(bsz, s // ts),
        in_specs=[tile("qd"), tile("fd"), tile("id"), tile("gd"),
                  pl.BlockSpec((1, w), lambda b, i: (0, 0)),
                  pl.BlockSpec((1, d), lambda b, i: (0, 0))],
        out_specs=pl.BlockSpec((None, ts, w), lambda b, i: (b, i, 0)),
        out_shape=jax.ShapeDtypeStruct((bsz, s, w), BF16),
        scratch_shapes=[pltpu.VMEM((ts, w), F32), pltpu.VMEM((ts, w), F32),
                        pltpu.VMEM((ts, w), F32), pltpu.VMEM((nh, d, d), F32)],
        compiler_params=pltpu.CompilerParams(
            dimension_semantics=("parallel", "arbitrary"), vmem_limit_bytes=VMEM_LIMIT),
        name="hgrn2",
    )(p32, p32, p32, p32, lb.astype(F32).reshape(1, w), d_norm_g.astype(F32).reshape(1, d))


def _stickbreak_kernel(q_ref, k_ref, v_ref, o_ref, acc_ref, *, tq):
    i = pl.program_id(1)
    d = HEAD_DIM
    nh = N_HEADS
    row = _iota((tq, tq), 0)
    col = _iota((tq, tq), 1)
    causal = col < row
    later = (row > col).astype(BF16)
    later2 = jnp.concatenate([later, later], axis=0)

    heads = [slice(hh * d, (hh + 1) * d) for hh in range(nh)]

    def scores(blocks):
        jobs = [(j, dg, hs) for j, dg in blocks for hs in heads]
        z = [_mm_nt(q_ref[:, hs], k_ref[pl.ds(pl.multiple_of(j * tq, tq), tq), hs]) * (d ** -0.5)
             for j, _, hs in jobs]
        sp = [_softplus(x) for x in z]
        l1m = [jnp.where(causal, -x, 0.0) if dg else -x for x, (_, dg, _) in zip(sp, jobs)]
        rest = [jnp.dot(jnp.concatenate(_split(x), axis=1), later2, preferred_element_type=F32)
                for x in l1m]
        out = [((a - b) + r, l) for a, b, r, l in zip(z, sp, rest, l1m)]
        return [out[b * nh:(b + 1) * nh] for b in range(len(blocks))]

    def block(j, carries):
        (sc,) = scores([(j, False)])
        ps = [jnp.exp(logw + c) for (logw, _), c in zip(sc, carries)]
        pv = [_mm(p, v_ref[pl.ds(pl.multiple_of(j * tq, tq), tq), hs]) for p, hs in zip(ps, heads)]
        for hs, x in zip(heads, pv):
            acc_ref[:, hs] += x
        return tuple(c + jnp.sum(l1m, axis=-1, keepdims=True) for (_, l1m), c in zip(sc, carries))

    jp = jnp.maximum(i - 1, 0)
    live = jnp.where(i > 0, 1.0, 0.0)
    sd, sp_ = scores([(i, True), (jp, False)])
    carries = []
    for hh, hs in enumerate(heads):
        c1 = jnp.sum(sd[hh][1], axis=-1, keepdims=True)
        p_d = jnp.where(causal, jnp.exp(sd[hh][0]), 0.0)
        p_p = jnp.exp(sp_[hh][0] + c1) * live
        acc_ref[:, hs] = (_mm(p_d, v_ref[pl.ds(pl.multiple_of(i * tq, tq), tq), hs])
                          + _mm(p_p, v_ref[pl.ds(pl.multiple_of(jp * tq, tq), tq), hs]))
        carries.append(c1 + jnp.sum(sp_[hh][1], axis=-1, keepdims=True))
    carries = tuple(carries)

    def cond(c):
        worst = functools.reduce(jnp.maximum, c[1])
        return jnp.logical_and(c[0] >= 0, jnp.max(worst) >= EXP_ZERO_BELOW)

    def body(c):
        return c[0] - 1, block(c[0], c[1])

    lax.while_loop(cond, body, (i - 2, carries))
    o_ref[...] = acc_ref[...].astype(o_ref.dtype)


def _stickbreak(p16, *, tq, cols):
    bsz, s, _ = p16.shape
    nh = N_HEADS
    w = nh * HEAD_DIM
    kernel = functools.partial(_stickbreak_kernel, tq=tq)
    resident = dict(pipeline_mode=pl.Buffered(1))
    return pl.pallas_call(
        kernel,
        grid=(bsz, s // tq),
        in_specs=[pl.BlockSpec((None, tq, w), lambda b, i: (b, i, cols["qc"] // nh)),
                  pl.BlockSpec((None, s, w), lambda b, i: (b, 0, cols["kc"] // nh), **resident),
                  pl.BlockSpec((None, s, w), lambda b, i: (b, 0, cols["vc"] // nh), **resident)],
        out_specs=pl.BlockSpec((None, tq, w), lambda b, i: (b, i, 0)),
        out_shape=jax.ShapeDtypeStruct((bsz, s, w), BF16),
        scratch_shapes=[pltpu.VMEM((tq, w), F32)],
        compiler_params=pltpu.CompilerParams(
            dimension_semantics=("parallel", "arbitrary"), vmem_limit_bytes=VMEM_LIMIT),
        name="stickbreak",
    )(p16, p16, p16)


def _dsa_kernel(qi_ref, smq_ref, q_ref, sm_ref, k_ref, vt_ref, bias_ref, o_ref,
                sc_ref, scb_ref, qct_ref, kc_ref, bd_ref, lg_ref, *, tq, k_sel, wi_lane, wide):
    i = pl.program_id(1)
    tk = tq
    d = HEAD_DIM
    nh = N_HEADS
    ksel = float(k_sel)
    per_wide = wide // tk
    n_wide = (i + per_wide) // per_wide
    sub = 2 * tk
    lane_q = _iota((1, tq), 1)

    def tree(parts, op):
        while len(parts) > 1:
            parts = [op(parts[j], parts[j + 1]) if j + 1 < len(parts) else parts[j]
                     for j in range(0, len(parts), 2)]
        return parts[0]

    def col_fold(x, op=jnp.add, rows=8):
        return tree([x[r * rows:(r + 1) * rows] for r in range(x.shape[0] // rows)], op)

    @pl.when(i == 0)
    def _():
        def prep(g, carry):
            g0 = pl.multiple_of(g * wide, wide)
            hi, lo = _split(sm_ref[pl.ds(g0, wide), :][:, :IDX_DIM])
            kc_ref[pl.ds(g0, wide), :] = jnp.concatenate([hi, lo, hi], axis=1)
            return carry
        lax.fori_loop(0, sm_ref.shape[0] // wide, prep, 0)

    qit = qi_ref[...].T
    for p in range(IDX_HEADS // 2):
        halves = []
        for hh in (2 * p, 2 * p + 1):
            hi, lo = _split(qit[hh * IDX_DIM:(hh + 1) * IDX_DIM, :])
            halves.append(jnp.concatenate([hi, hi, lo], axis=0))
        qct_ref[p] = jnp.concatenate(halves, axis=1)
    w_rows = smq_ref[...].T[wi_lane:wi_lane + IDX_HEADS, :] * ((IDX_HEADS ** -0.5) * (IDX_DIM ** -0.5))

    q2t = (q_ref[...] * ((d ** -0.5) * LOG2E)).T.astype(BF16)
    zero_dq = jnp.zeros((d, tq), BF16)
    for p in range(nh // 2):
        top = jnp.concatenate([q2t[2 * p * d:(2 * p + 1) * d], zero_dq], axis=1)
        bot = jnp.concatenate([zero_dq, q2t[(2 * p + 1) * d:(2 * p + 2) * d]], axis=1)
        bd_ref[p] = jnp.concatenate([top, bot], axis=0)

    limit = i * tq + (lane_q // CHUNK + 1) * CHUNK

    rows_s = _iota((sub, tq), 0)

    def score_groups(gs, mm, masked):
        mn, mx = mm
        k0s = [pl.multiple_of(g * wide + sb * sub, sub) for g in gs for sb in range(wide // sub)]
        keys = [kc_ref[pl.ds(k0, sub), :] for k0 in k0s]
        accs = [jnp.zeros((sub, tq), F32) for _ in k0s]
        for p in range(IDX_HEADS // 2):
            rhs = qct_ref[p]
            for n, kk in enumerate(keys):
                s2 = jnp.dot(kk, rhs, preferred_element_type=F32)
                accs[n] = (accs[n] + jnp.maximum(s2[:, :tq], 0.0) * w_rows[2 * p:2 * p + 1, :]
                           + jnp.maximum(s2[:, tq:], 0.0) * w_rows[2 * p + 1:2 * p + 2, :])
        for k0, sct in zip(k0s, accs):
            if masked:
                adm = (k0 + rows_s) < limit
                mn = jnp.minimum(mn, col_fold(jnp.where(adm, sct, jnp.inf), jnp.minimum))
                sct = jnp.where(adm, sct, -jnp.inf)
            else:
                mn = jnp.minimum(mn, col_fold(sct, jnp.minimum))
            mx = jnp.maximum(mx, col_fold(sct, jnp.maximum))
            sc_ref[pl.ds(k0, sub), :] = sct
            scb_ref[pl.ds(k0, sub), :] = _floor_bf16(sct)
        return mn, mx

    def score_pair(j, mm):
        return score_groups((2 * j, 2 * j + 1), mm, False)

    n_full = n_wide - 1
    mm = lax.fori_loop(0, n_full // 2, score_pair,
                       (jnp.full((8, tq), jnp.inf, F32), jnp.full((8, tq), -jnp.inf, F32)))
    mm = lax.cond(n_full % 2 == 1, lambda c: score_groups((n_full - 1,), c, False), lambda c: c, mm)
    mn, mx = score_groups((n_wide - 1,), mm, True)

    n_pairs = (n_wide + 1) // 2

    @pl.when(n_wide % 2 == 1)
    def _():
        sc_ref[pl.ds(pl.multiple_of(n_wide * wide, wide), wide), :] = jnp.full((wide, tq), -jnp.inf, F32)
        scb_ref[pl.ds(pl.multiple_of(n_wide * wide, wide), wide), :] = jnp.full((wide, tq), -jnp.inf, BF16)
    rmin = jnp.min(mn, axis=0, keepdims=True)
    rmax = jnp.max(mx, axis=0, keepdims=True)

    def count(pred):
        def body(j, acc):
            for g in (2 * j, 2 * j + 1):
                acc = acc + col_fold(pred(sc_ref[pl.ds(pl.multiple_of(g * wide, wide), wide), :]))
            return acc
        return jnp.sum(lax.fori_loop(0, n_pairs, body, jnp.zeros((8, tq), F32)), axis=0, keepdims=True)

    def max_below(x):
        def body(j, acc):
            for g in (2 * j, 2 * j + 1):
                blk = sc_ref[pl.ds(pl.multiple_of(g * wide, wide), wide), :]
                acc = jnp.maximum(acc, col_fold(jnp.where(blk < x, blk, -jnp.inf), jnp.maximum))
            return acc
        return jnp.max(lax.fori_loop(0, n_pairs, body, jnp.full((8, tq), -jnp.inf, F32)), axis=0, keepdims=True)

    n_adm = limit.astype(F32)
    all_sel = n_adm <= ksel

    def bisect(c):
        lo, hi, c_lo = c
        mid = 0.5 * lo + 0.5 * hi
        cm = count(lambda blk: _ind(blk >= mid))
        ge = cm >= ksel
        return jnp.where(ge, mid, lo), jnp.where(ge, hi, mid), jnp.where(ge, cm, c_lo)

    def pending(c_lo, tied):
        return jnp.where(all_sel, 0.0, jnp.where(tied > 0.5, 0.0, _ind(c_lo != ksel)))

    def bisect_coarse(_, c):
        lo, hi, c_lo = c
        mid = _floor_bf16(0.5 * lo + 0.5 * hi).astype(F32)
        t_b = jnp.broadcast_to(mid, (16, tq)).astype(BF16)
        one_b = jnp.ones((16, tq), BF16)
        zero_b = jnp.zeros((16, tq), BF16)

        def body(j, acc):
            for g in (2 * j, 2 * j + 1):
                blk = scb_ref[pl.ds(pl.multiple_of(g * wide, wide), wide), :]
                ind = [jnp.where(blk[r * 16:(r + 1) * 16] >= t_b, one_b, zero_b) for r in range(wide // 16)]
                acc = acc + tree(ind, jnp.add).astype(F32)
            return acc

        acc = lax.fori_loop(0, n_pairs, body, jnp.zeros((16, tq), F32))
        cm = jnp.sum(acc, axis=0, keepdims=True)
        ge = cm >= ksel
        return jnp.where(ge, mid, lo), jnp.where(ge, hi, mid), jnp.where(ge, cm, c_lo)

    lo0 = _floor_bf16(rmin).astype(F32)
    hi0 = _floor_bf16(rmax + (jnp.abs(rmax) * (2.0 ** -6) + 1e-30)).astype(F32)
    state = lax.fori_loop(0, BISECT_COARSE, bisect_coarse, (lo0, hi0, n_adm))
    state = lax.fori_loop(0, BISECT_FIXED, lambda _, c: bisect(c), state)

    def round_cond(c):
        return jnp.max(pending(c[0][2], c[1])) > 0.5

    def round_body(c):
        st, tied, v, need = c

        def more_cond(s):
            return jnp.logical_and(s[0] < BISECT_EXTRA, jnp.max(pending(s[1][2], tied)) > 0.5)

        _, st = lax.while_loop(more_cond, lambda s: (s[0] + 1, bisect(s[1])), (jnp.int32(0), st))
        pend = pending(st[2], tied)

        def check(_):
            cand = max_below(st[1])
            c_ge = count(lambda blk: _ind(blk >= cand))
            c_gt = count(lambda blk: _ind(blk > cand))
            ok = jnp.where(pend > 0.5, _ind(c_ge >= ksel), 0.0)
            return (jnp.where(ok > 0.5, 1.0, tied), jnp.where(ok > 0.5, cand, v),
                    jnp.where(ok > 0.5, ksel - c_gt, need))

        tied, v, need = lax.cond(jnp.max(pend) > 0.5, check, lambda _: (tied, v, need), 0)
        return st, tied, v, need

    zeros1 = jnp.zeros((1, tq), F32)
    (lo_f, _, _), tied, v_tie, need = lax.while_loop(round_cond, round_body, (state, zeros1, zeros1, zeros1))
    vth = jnp.where(all_sel, F32_LOWEST, jnp.where(tied > 0.5, v_tie, lo_f))

    @pl.when(jnp.max(tied) > 0.5)
    def _():
        v_eq = jnp.where(tied > 0.5, v_tie, jnp.inf)
        incl = (_iota((tk, tk), 1) <= _iota((tk, tk), 0)).astype(BF16)

        def demote(g, seen):
            g0 = pl.multiple_of(g * wide, wide)
            xs = [sc_ref[pl.ds(g0 + pb * tk, tk), :] for pb in range(per_wide)]
            eqs = [_ind(x == v_eq) for x in xs]
            inblk = [jnp.dot(incl, e.astype(BF16), preferred_element_type=F32) for e in eqs]
            for pb in range(per_wide):
                rank = inblk[pb] + seen
                sc_ref[pl.ds(g0 + pb * tk, tk), :] = jnp.where(eqs[pb] * _ind(rank > need) > 0.5,
                                                               -jnp.inf, xs[pb])
                seen = seen + jnp.sum(col_fold(eqs[pb]), axis=0, keepdims=True)
            return seen

        lax.fori_loop(0, n_wide, demote, zeros1)

    g_near = jnp.maximum(i - 1, 0) // per_wide

    def logit_group(g, mx, near):
        out = list(mx)
        for sb in range(wide // sub):
            k0 = pl.multiple_of(g * wide + sb * sub, sub)
            sel = sc_ref[pl.ds(k0, sub), :] >= vth
            for p in range(nh // 2):
                pair = jnp.dot(k_ref[pl.ds(k0, sub), 2 * p * d:(2 * p + 2) * d], bd_ref[p],
                               preferred_element_type=F32)
                for hh in (2 * p, 2 * p + 1):
                    lm = pair[:, (hh - 2 * p) * tq:(hh - 2 * p + 1) * tq]
                    if near:
                        back = [jnp.clip(i - (g * per_wide + sb * (sub // tk) + pb), 0, 2)
                                for pb in range(sub // tk)]
                        lm = lm + jnp.concatenate([bias_ref[bk, hh] for bk in back], axis=0)
                    lm = jnp.where(sel, lm, NEG_BIG)
                    lg_ref[hh, pl.ds(k0, sub), :] = lm
                    out[hh] = jnp.maximum(out[hh], col_fold(lm, jnp.maximum))
        return tuple(out)

    mx = tuple(jnp.full((8, tq), NEG_BIG, F32) for _ in range(nh))
    def logit_pair(j, mx, near):
        return logit_group(2 * j + 1, logit_group(2 * j, mx, near), near)

    far_pairs = g_near // 2
    mx = lax.fori_loop(0, far_pairs, functools.partial(logit_pair, near=False), mx)
    mx = lax.fori_loop(far_pairs, n_pairs, functools.partial(logit_pair, near=True), mx)
    m_q = [jnp.max(mx[hh], axis=0, keepdims=True) for hh in range(nh)]

    ones_rows = jnp.ones((8, wide), BF16)

    def pv_pair(j, carry):
        ls, accs = list(carry[0]), list(carry[1])
        jobs = [(pl.multiple_of(g * wide, wide), hh) for g in (2 * j, 2 * j + 1) for hh in range(nh)]
        ps = [jnp.exp2(lg_ref[hh, pl.ds(g0, wide), :] - m_q[hh]).astype(BF16) for g0, hh in jobs]
        outs = [jnp.dot(jnp.concatenate([vt_ref[hh * d:(hh + 1) * d, pl.ds(g0, wide)], ones_rows], axis=0),
                        p, preferred_element_type=F32) for (g0, hh), p in zip(jobs, ps)]
        for (_, hh), out in zip(jobs, outs):
            ls[hh] = ls[hh] + out[d:]
            accs[hh] = accs[hh] + out[:d]
        return tuple(ls), tuple(accs)

    ls, accs = lax.fori_loop(0, n_pairs, pv_pair,
                             (tuple(jnp.zeros((8, tq), F32) for _ in range(nh)),
                              tuple(jnp.zeros((d, tq), F32) for _ in range(nh))))
    for hh in range(nh):
        o_ref[:, hh * d:(hh + 1) * d] = (accs[hh] / ls[hh][0:1]).T.astype(o_ref.dtype)


def _dsa(p32, p16, vt, bias_tiles, *, tq, cols):
    bsz, s, _ = p32.shape
    d = HEAD_DIM
    nh = N_HEADS
    wide = 4 * tq
    k_sel = min(TOPK_MAX, s // 4)
    w512 = nh * d
    kernel = functools.partial(_dsa_kernel, tq=tq, k_sel=k_sel, wi_lane=cols["wi_lane"], wide=wide)
    resident = dict(pipeline_mode=pl.Buffered(1))
    return pl.pallas_call(
        kernel,
        grid=(bsz, s // tq),
        in_specs=[pl.BlockSpec((None, tq, w512), lambda b, i: (b, i, cols["qi"] // nh)),
                  pl.BlockSpec((None, tq, d), lambda b, i: (b, i, cols["small"])),
                  pl.BlockSpec((None, tq, w512), lambda b, i: (b, i, cols["qb"] // nh)),
                  pl.BlockSpec((None, s, d), lambda b, i: (b, 0, cols["small"]), **resident),
                  pl.BlockSpec((None, s, w512), lambda b, i: (b, 0, cols["kb"] // nh), **resident),
                  pl.BlockSpec((w512, s), lambda b, i: (0, b), **resident),
                  pl.BlockSpec((3, nh, tq, tq), lambda b, i: (0, 0, 0, 0), **resident)],
        out_specs=pl.BlockSpec((None, tq, w512), lambda b, i: (b, i, 0)),
        out_shape=jax.ShapeDtypeStruct((bsz, s, w512), BF16),
        scratch_shapes=[pltpu.VMEM((s, tq), F32),
                        pltpu.VMEM((s, tq), BF16),
                        pltpu.VMEM((IDX_HEADS // 2, 3 * IDX_DIM, 2 * tq), BF16),
                        pltpu.VMEM((s, 3 * IDX_DIM), BF16),
                        pltpu.VMEM((nh // 2, 2 * d, 2 * tq), BF16),
                        pltpu.VMEM((nh, s, tq), F32)],
        compiler_params=pltpu.CompilerParams(
            dimension_semantics=("parallel", "arbitrary"), vmem_limit_bytes=VMEM_LIMIT),
        name="dsa",
    )(p32, p32, p32, p32, p16, vt, bias_tiles)


def _t5_bucket(rel):
    nb = REL_BUCKETS // 2
    max_exact = nb // 2
    ret = jnp.where(rel > 0, nb, 0)
    n = jnp.abs(rel)
    large = max_exact + (jnp.log(jnp.maximum(n, 1).astype(F32) / max_exact)
                         / math.log(REL_MAX_DIST / max_exact) * (nb - max_exact)).astype(jnp.int32)
    large = jnp.minimum(large, nb - 1)
    return ret + jnp.where(n < max_exact, n, large)


def _bias_tiles(rel_table, tq):
    assert tq >= REL_MAX_DIST
    t = jnp.arange(tq)
    back = jnp.arange(3)
    rel = (t[None, None, :] - back[:, None, None] * tq) - t[None, :, None]
    onehot = (_t5_bucket(rel)[..., None] == jnp.arange(REL_BUCKETS)).astype(F32)
    tiles = jnp.einsum("bqkn,nh->bhkq", onehot, rel_table.astype(F32),
                       precision=HIGHEST)
    return (tiles - tiles[2:3]) * LOG2E


def _even_layout(w_in):
    d = HEAD_DIM
    a_w = 2 * N_HEADS * d + N_HEADS * d
    offs = {}
    o = 0
    for name, w in (("qkv", a_w), ("z", N_HEADS * d), ("a", N_HEADS), ("b", N_HEADS),
                    ("qb", N_HEADS * d), ("kb", N_HEADS * d), ("vb", N_HEADS * d),
                    ("qi", IDX_HEADS * IDX_DIM), ("ki", IDX_DIM), ("wi", IDX_HEADS)):
        offs[name] = (o, o + w)
        o += w
    assert o == w_in.shape[1]
    sl = lambda n: w_in[:, offs[n][0]:offs[n][1]]
    small_w = IDX_DIM + 2 * N_HEADS + IDX_HEADS
    small_pad = -small_w % d
    zeros = lambda n: jnp.zeros((w_in.shape[0], n), w_in.dtype)
    w32 = jnp.concatenate([sl("qkv"), sl("z"), sl("qb"), sl("qi"),
                           sl("ki"), sl("a"), sl("b"), sl("wi"), zeros(small_pad)], axis=1)
    n32 = w32.shape[1]
    tn = n32 // 5
    assert tn * 5 == n32 and tn % d == 0
    w16 = jnp.concatenate([sl("kb"), zeros(tn - N_HEADS * d)], axis=1)
    nh = N_HEADS
    cols = dict(qa=0, ka=nh, va=2 * nh, za=3 * nh, qb=4 * nh, qi=5 * nh, small=6 * nh, kb=0,
                a_lane=IDX_DIM, b_lane=IDX_DIM + nh, wi_lane=IDX_DIM + 2 * nh, n32=n32, tn=tn)
    return jnp.concatenate([w32, w16], axis=1).astype(BF16), sl("vb").T.astype(BF16), cols


def kernel(x, norm_g, w_in_even, conv_w_even, a_log_even, dt_bias_even, a_norm_even, w_out_even,
           rel_bias, w_in_odd, lb_logits, d_norm_odd, w_out_odd, w_gate, w_up, w_down):
    bsz, s, d = x.shape
    t = bsz * s
    depth = norm_g.shape[0]
    nh = N_HEADS
    tq = Q_TILE
    lb_all = jnp.cumsum(jax.nn.softmax(lb_logits.astype(F32), axis=0), axis=0)
    lb_all = lb_all - lb_all[:1]
    odd_cols = dict(qc=0, kc=nh, vc=2 * nh, qd=0, fd=nh, id=2 * nh, gd=3 * nh)
    bias_tiles = _bias_tiles(rel_bias, tq)

    h = x.reshape(t, d)
    for l in range(depth):
        if l % 2 == 0:
            e = l // 2
            w_even, w_vt, cols = _even_layout(w_in_even[e])
            p32, p16, vt = _norm_matmul(h, norm_g[l, 0], w_even, tm=PROJ_TILE, tn=cols["tn"], n32=cols["n32"],
                                        w_t=w_vt)
            p32 = p32.reshape(bsz, s, -1)
            p16 = p16.reshape(bsz, s, -1)
            o_1 = _deltanet(p32, conv_w_even[e], a_log_even[e], dt_bias_even[e], a_norm_even[e],
                            ts=min(SEQ_TILE, s), cols=cols)
            o_2 = _dsa(p32, p16, vt, bias_tiles, tq=tq, cols=cols)
            w_out = w_out_even[e]
        else:
            o = l // 2
            n16 = 3 * nh * HEAD_DIM
            w_odd = jnp.concatenate([w_in_odd[o][:, n16:], w_in_odd[o][:, :n16]], axis=1).astype(BF16)
            p32, p16 = _norm_matmul(h, norm_g[l, 0], w_odd, tm=PROJ_TILE, tn=ODD_COL_TILE, n32=w_odd.shape[1] - n16)
            p32 = p32.reshape(bsz, s, -1)
            p16 = p16.reshape(bsz, s, -1)
            o_1 = _stickbreak(p16, tq=tq, cols=odd_cols)
            o_2 = _hgrn2(p32, lb_all[l], d_norm_odd[o], ts=min(SEQ_TILE, s), cols=odd_cols)
            w_out = w_out_odd[o]
        h = _mix_ffn(o_1.reshape(t, -1), o_2.reshape(t, -1), w_out, h, norm_g[l, 1], norm_g[l, 2], norm_g[l, 3],
                     w_gate[l], w_up[l], w_down[l], tm=ROW_TILE, tf=FFN_TILE)
    return h.reshape(bsz, s, d)
```

```python
import functools
import math

import jax
import jax.numpy as jnp
from jax import lax
from jax.experimental import pallas as pl
from jax.experimental.pallas import tpu as pltpu

F32 = jnp.float32
BF16 = jnp.bfloat16
HIGHEST = lax.Precision.HIGHEST

CHUNK = 64
HEAD_DIM = 128
N_HEADS = 4
IDX_HEADS = 8
IDX_DIM = 64
TOPK_MAX = 256
CONV_WIDTH = 4
REL_BUCKETS = 32
REL_MAX_DIST = 128
EPS = 1e-6
NEG_BIG = -1e30
LOG2E = 1.4426950408889634
BISECT_COARSE = 12
BISECT_FIXED = 8
BISECT_EXTRA = 6
F32_LOWEST = -3.4028234663852886e38
EXP_ZERO_BELOW = -104.0
VMEM_LIMIT = 56 * 1024 * 1024

PROJ_TILE = 2048
ROW_TILE = 512
SEQ_TILE = 512
Q_TILE = 128
ODD_COL_TILE = 512
FFN_TILE = 2816


def _mm(a, b):
    return jnp.dot(a.astype(BF16), b.astype(BF16), preferred_element_type=F32)


def _mm_nt(a, b):
    return lax.dot_general(a.astype(BF16), b.astype(BF16), (((1,), (1,)), ((), ())),
                           preferred_element_type=F32)


def _mm_tn(a, b):
    return lax.dot_general(a.astype(BF16), b.astype(BF16), (((0,), (0,)), ((), ())),
                           preferred_element_type=F32)


def _split(x):
    hi = x.astype(BF16)
    return hi, (x - hi.astype(F32)).astype(BF16)


def _floor_bf16(x):
    bits = pltpu.bitcast(x, jnp.int32)
    down = jnp.where(bits >= 0, bits, bits + 0xFFFF) & jnp.int32(-65536)
    return pltpu.bitcast(down, F32).astype(BF16)


def _sigmoid(x):
    return 1.0 / (1.0 + jnp.exp(-x))


def _silu(x):
    return x * _sigmoid(x)


def _softplus(x):
    return jnp.maximum(x, 0.0) + jnp.log1p(jnp.exp(-jnp.abs(x)))


def _rms(x, g):
    return x * lax.rsqrt(jnp.mean(x * x, axis=-1, keepdims=True) + EPS) * g


def _iota(shape, dim):
    return lax.broadcasted_iota(jnp.int32, shape, dim)


def _ind(mask):
    return jnp.where(mask, 1.0, 0.0)


def _norm_matmul_kernel(x_ref, g_ref, w_ref, *rest, n_t, tiles32):
    if n_t:
        wt_ref, o32_ref, o16_ref, ot_ref, xn_ref = rest
    else:
        o32_ref, o16_ref, xn_ref = rest
    j = pl.program_id(1)

    @pl.when(j == 0)
    def _():
        xn_ref[...] = _rms(x_ref[...], g_ref[...]).astype(BF16)
        if n_t:
            ot_ref[...] = lax.dot_general(wt_ref[...], xn_ref[...], (((1,), (1,)), ((), ())),
                                          preferred_element_type=F32).astype(BF16)

    y = jnp.dot(xn_ref[...], w_ref[...], preferred_element_type=F32)

    @pl.when(j < tiles32)
    def _():
        o32_ref[...] = y

    @pl.when(j >= tiles32)
    def _():
        o16_ref[...] = y.astype(BF16)


def _norm_matmul(x, g, w, *, tm, tn, n32, w_t=None):
    t, d = x.shape
    n = w.shape[1]
    n_t = 0 if w_t is None else w_t.shape[0]
    tiles32 = n32 // tn
    assert tiles32 * tn == n32 and (n - n32) % tn == 0 and 0 < n32 < n
    in_specs = [pl.BlockSpec((tm, d), lambda i, j: (i, 0)),
                pl.BlockSpec((1, d), lambda i, j: (0, 0)),
                pl.BlockSpec((d, tn), lambda i, j: (0, j))]
    out_specs = [pl.BlockSpec((tm, tn), lambda i, j: (i, jnp.minimum(j, tiles32 - 1))),
                 pl.BlockSpec((tm, tn), lambda i, j: (i, jnp.maximum(j - tiles32, 0)))]
    out_shape = [jax.ShapeDtypeStruct((t, n32), F32), jax.ShapeDtypeStruct((t, n - n32), BF16)]
    args = [x, g.reshape(1, d), w]
    if n_t:
        in_specs.append(pl.BlockSpec((n_t, d), lambda i, j: (0, 0)))
        out_specs.append(pl.BlockSpec((n_t, tm), lambda i, j: (0, i)))
        out_shape.append(jax.ShapeDtypeStruct((n_t, t), BF16))
        args.append(w_t)
    return pl.pallas_call(
        functools.partial(_norm_matmul_kernel, n_t=n_t, tiles32=tiles32),
        grid=(t // tm, n // tn),
        in_specs=in_specs,
        out_specs=out_specs,
        out_shape=out_shape,
        scratch_shapes=[pltpu.VMEM((tm, d), BF16)],
        compiler_params=pltpu.CompilerParams(
            dimension_semantics=("parallel", "arbitrary"), vmem_limit_bytes=VMEM_LIMIT),
        name="norm_matmul",
    )(*args)


def _mix_ffn_kernel(ca_ref, cb_ref, wa_ref, wb_ref, h_ref, gmix_ref, gpre_ref, gpost_ref,
                    wg_ref, wu_ref, wd_ref, o_ref, h1_ref, xn_ref, acc_ref):
    f = pl.program_id(1)

    @pl.when(f == 0)
    def _():
        y = (jnp.dot(ca_ref[...], wa_ref[...], preferred_element_type=F32)
             + jnp.dot(cb_ref[...], wb_ref[...], preferred_element_type=F32))
        h1 = h_ref[...] + _rms(y, gmix_ref[...])
        h1_ref[...] = h1
        xn_ref[...] = _rms(h1, gpre_ref[...]).astype(BF16)
        acc_ref[...] = jnp.zeros_like(acc_ref)

    xn = xn_ref[...]
    gate = jnp.dot(xn, wg_ref[...], preferred_element_type=F32)
    up = jnp.dot(xn, wu_ref[...], preferred_element_type=F32)
    act = (_silu(gate) * up).astype(BF16)
    acc_ref[...] += jnp.dot(act, wd_ref[...], preferred_element_type=F32)

    @pl.when(f == pl.num_programs(1) - 1)
    def _():
        o_ref[...] = h1_ref[...] + _rms(acc_ref[...], gpost_ref[...])


def _mix_ffn(ca, cb, w_out, h, g_mix, g_pre, g_post, wg, wu, wd, *, tm, tf):
    t, d = h.shape
    ff = wg.shape[1]
    wa_n = ca.shape[1]
    wb_n = cb.shape[1]
    row = pl.BlockSpec((1, d), lambda i, f: (0, 0))
    once = dict(pipeline_mode=pl.Buffered(1)) if tf == ff else {}
    return pl.pallas_call(
        _mix_ffn_kernel,
        grid=(t // tm, ff // tf),
        in_specs=[pl.BlockSpec((tm, wa_n), lambda i, f: (i, 0)),
                  pl.BlockSpec((tm, wb_n), lambda i, f: (i, 0)),
                  pl.BlockSpec((wa_n, d), lambda i, f: (0, 0)),
                  pl.BlockSpec((wb_n, d), lambda i, f: (0, 0)),
                  pl.BlockSpec((tm, d), lambda i, f: (i, 0)),
                  row, row, row,
                  pl.BlockSpec((d, tf), lambda i, f: (0, f), **once),
                  pl.BlockSpec((d, tf), lambda i, f: (0, f), **once),
                  pl.BlockSpec((tf, d), lambda i, f: (f, 0), **once)],
        out_specs=pl.BlockSpec((tm, d), lambda i, f: (i, 0)),
        out_shape=jax.ShapeDtypeStruct((t, d), F32),
        scratch_shapes=[pltpu.VMEM((tm, d), F32), pltpu.VMEM((tm, d), BF16), pltpu.VMEM((tm, d), F32)],
        compiler_params=pltpu.CompilerParams(
            dimension_semantics=("parallel", "arbitrary"), vmem_limit_bytes=VMEM_LIMIT),
        name="mix_ffn",
    )(ca, cb, w_out[:wa_n].astype(BF16), w_out[wa_n:].astype(BF16), h,
      g_mix.reshape(1, d), g_pre.reshape(1, d), g_post.reshape(1, d),
      wg.astype(BF16), wu.astype(BF16), wd.astype(BF16))


def _deltanet_kernel(xq_ref, xk_ref, xv_ref, z_ref, sm_ref, cwq_ref, cwk_ref, cwv_ref,
                     alog_ref, dtb_ref, gn_ref, o_ref,
                     xpad_ref, q_ref, k_ref, v_ref, gb_ref, bb_ref, u_ref, w_ref, qk_ref, st_ref,
                     *, ts, a_col, b_col):
    s = pl.program_id(1)
    c = CHUNK
    d = HEAD_DIM
    nh = N_HEADS

    @pl.when(s == 0)
    def _():
        xpad_ref[:, 0:8, :] = jnp.zeros((3, 8, nh * d), F32)
        st_ref[...] = jnp.zeros_like(st_ref)

    @pl.when(s != 0)
    def _():
        xpad_ref[:, 0:8, :] = xpad_ref[:, ts:ts + 8, :]

    xpad_ref[0, 8:ts + 8, :] = xq_ref[...]
    xpad_ref[1, 8:ts + 8, :] = xk_ref[...]
    xpad_ref[2, 8:ts + 8, :] = xv_ref[...]

    def conv_silu(idx, cw_ref, hs):
        cw = cw_ref[:, hs]
        acc = xpad_ref[idx, 8 - (CONV_WIDTH - 1):8 - (CONV_WIDTH - 1) + ts, hs] * cw[0:1, :]
        for j in range(1, CONV_WIDTH):
            off = 8 - (CONV_WIDTH - 1) + j
            acc = acc + xpad_ref[idx, off:off + ts, hs] * cw[j:j + 1, :]
        return _silu(acc)

    def l2norm(t):
        return t * lax.rsqrt(jnp.sum(t * t, axis=-1, keepdims=True) + EPS)

    row = _iota((c, c), 0)
    col = _iota((c, c), 1)
    tri = (col <= row)
    strict = (col < row)
    tri_f = tri.astype(F32)
    upper_f = (row <= col).astype(F32)
    eye = (row == col).astype(F32)
    gnorm = gn_ref[...]
    chunks = range(ts // c)
    rs = [slice(ci * c, (ci + 1) * c) for ci in chunks]
    tri2 = jnp.concatenate([tri_f, tri_f], axis=1).astype(BF16)
    ones2 = jnp.ones((c, 2 * c), BF16)

    def cum2(lhs2, x):
        hi, lo = _split(x)
        return jnp.dot(lhs2, jnp.concatenate([hi, lo], axis=0), preferred_element_type=F32)

    for hh in range(nh):
        hs = slice(hh * d, (hh + 1) * d)
        q_ref[:, hs] = l2norm(conv_silu(0, cwq_ref, hs)) * (d ** -0.5)
        k_ref[:, hs] = l2norm(conv_silu(1, cwk_ref, hs))
        v_ref[:, hs] = conv_silu(2, cwv_ref, hs)

        a_raw = sm_ref[:, a_col + hh:a_col + hh + 1]
        b_raw = sm_ref[:, b_col + hh:b_col + hh + 1]
        g = -jnp.exp(alog_ref[:, hh:hh + 1]) * _softplus(a_raw + dtb_ref[:, hh:hh + 1])
        gb_ref[:, hs] = jnp.broadcast_to(g, (ts, d))
        bb_ref[:, hs] = jnp.broadcast_to(_sigmoid(b_raw), (ts, d))

        q = [q_ref[r, hs] for r in rs]
        k = [k_ref[r, hs] for r in rs]
        beta = [bb_ref[r, hs] for r in rs]
        gb = [gb_ref[r, hs] for r in rs]
        gc = [cum2(tri2, x) for x in gb]
        gc_row = [cum2(ones2, x[:, :c] * upper_f) for x in gb]
        decay = [jnp.where(tri, jnp.exp(jnp.minimum(a[:, :c] - b, 0.0)), 0.0) for a, b in zip(gc, gc_row)]
        kk = [_mm_nt(x, x) for x in k]
        n = [-jnp.where(strict, b[:, :c] * x * dc, 0.0) for b, x, dc in zip(beta, kk, decay)]
        inv = [eye + x for x in n]
        for step in range(5):
            nb = [x.astype(BF16) for x in n]
            n = [jnp.dot(x, x, preferred_element_type=F32) for x in nb]
            inv = [iv + _mm(iv, x) for iv, x in zip(inv, n)]
        egc = [jnp.exp(x) for x in gc]
        gl = [x[c - 1:c, :] for x in gc]
        inv_l = [x.astype(BF16) for x in inv]
        u = [_mm(a, v_ref[r, hs] * b) for a, r, b in zip(inv_l, rs, beta)]
        w = [_mm(a, x * (b * e)) for a, x, b, e in zip(inv_l, k, beta, egc)]
        qk = [_mm_nt(a, b) * dc for a, b, dc in zip(q, k, decay)]
        for ci in chunks:
            r = rs[ci]
            u_ref[r, hs] = u[ci]
            w_ref[r, hs] = w[ci]
            qk_ref[hh, r, :] = qk[ci]
            q_ref[r, hs] = q[ci] * egc[ci]
            k_ref[r, hs] = k[ci] * jnp.exp(gl[ci] - gc[ci])
            gb_ref[r, hs] = jnp.broadcast_to(jnp.exp(gl[ci]), (c, d))

    def chunk_body(ci, carry):
        r0 = pl.multiple_of(ci * c, c)
        rows = pl.ds(r0, c)
        hss = [slice(hh * d, (hh + 1) * d) for hh in range(nh)]
        st = [st_ref[hh] for hh in range(nh)]
        w_st = [_mm(w_ref[rows, hs], s_) for hs, s_ in zip(hss, st)]
        q_st = [_mm(q_ref[rows, hs], s_) for hs, s_ in zip(hss, st)]
        v_new = [u_ref[rows, hs] - x for hs, x in zip(hss, w_st)]
        o = [a + _mm(qk_ref[hh, rows, :], v) for hh, (a, v) in enumerate(zip(q_st, v_new))]
        kv = [_mm_tn(k_ref[rows, hs], v) for hs, v in zip(hss, v_new)]
        for hh, hs in enumerate(hss):
            st_ref[hh] = st[hh] * gb_ref[pl.ds(r0, 1), hs] + kv[hh]
            o_ref[rows, hs] = (_rms(o[hh], gnorm) * _silu(z_ref[rows, hs])).astype(o_ref.dtype)
        return carry

    lax.fori_loop(0, ts // c, chunk_body, 0)


def _deltanet(p32, conv_w, a_log, dt_bias, a_norm_g, *, ts, cols):
    bsz, s, _ = p32.shape
    d = HEAD_DIM
    nh = N_HEADS
    w = nh * d
    pad = lambda t: jnp.pad(t.astype(F32), (0, d - t.shape[0])).reshape(1, d)
    kernel = functools.partial(_deltanet_kernel, ts=ts, a_col=cols["a_lane"], b_col=cols["b_lane"])
    tile = lambda name: pl.BlockSpec((None, ts, w), lambda b, i: (b, i, cols[name] // nh))
    conv = lambda k: pl.BlockSpec((CONV_WIDTH, w), lambda b, i: (0, k))
    row = pl.BlockSpec((1, d), lambda b, i: (0, 0))
    return pl.pallas_call(
        kernel,
        grid=(bsz, s // ts),
        in_specs=[tile("qa"), tile("ka"), tile("va"), tile("za"),
                  pl.BlockSpec((None, ts, d), lambda b, i: (b, i, cols["small"])),
                  conv(0), conv(1), conv(2), row, row, row],
        out_specs=pl.BlockSpec((None, ts, w), lambda b, i: (b, i, 0)),
        out_shape=jax.ShapeDtypeStruct((bsz, s, w), BF16),
        scratch_shapes=[pltpu.VMEM((3, ts + 8, w), F32)]
        + [pltpu.VMEM((ts, w), F32) for _ in range(7)]
        + [pltpu.VMEM((nh, ts, CHUNK), F32), pltpu.VMEM((nh, d, d), F32)],
        compiler_params=pltpu.CompilerParams(
            dimension_semantics=("parallel", "arbitrary"), vmem_limit_bytes=VMEM_LIMIT),
        name="deltanet",
    )(p32, p32, p32, p32, p32, conv_w.astype(F32), conv_w.astype(F32), conv_w.astype(F32),
      pad(a_log), pad(dt_bias), a_norm_g.astype(F32).reshape(1, d))


def _hgrn2_kernel(q_ref, f_ref, i_ref, gate_ref, lb_ref, gn_ref, o_ref,
                  qs_ref, ks_ref, gc_ref, st_ref, *, ts):
    s = pl.program_id(1)
    c = CHUNK
    d = HEAD_DIM
    nh = N_HEADS
    SUB = 8

    @pl.when(s == 0)
    def _():
        st_ref[...] = jnp.zeros_like(st_ref)

    lb = lb_ref[...]
    f_raw = f_ref[...]
    log_sig = jnp.minimum(f_raw, 0.0) - jnp.log1p(jnp.exp(-jnp.abs(f_raw)))
    la = jnp.log(lb)
    lbb = jnp.log1p(-lb) + log_sig
    log_f = jnp.maximum(la, lbb) + jnp.log1p(jnp.exp(-jnp.abs(la - lbb)))
    qs_ref[...] = _silu(q_ref[...])
    ks_ref[...] = (1.0 - lb) * _sigmoid(-f_raw)

    row = _iota((c, c), 0)
    col = _iota((c, c), 1)
    tri_f = (col <= row).astype(F32)
    ones_dd = jnp.ones((d, d), BF16)
    rows_8d = _iota((8, d), 0)
    gnorm = gn_ref[...]

    tri2 = jnp.concatenate([tri_f, tri_f], axis=1).astype(BF16)
    for ci in range(ts // c):
        hi, lo = _split(log_f[ci * c:(ci + 1) * c, :])
        gc_ref[ci * c:(ci + 1) * c, :] = jnp.dot(tri2, jnp.concatenate([hi, lo], axis=0),
                                                 preferred_element_type=F32)

    blocks = [(sb * SUB, (sb + 1) * SUB) for sb in range(c // SUB)]

    def chunk_loop(ci, carry):
        r0 = pl.multiple_of(ci * c, c)
        rows = pl.ds(r0, c)
        hss = [slice(hh * d, (hh + 1) * d) for hh in range(nh)]
        q = [qs_ref[rows, hs] for hs in hss]
        k = [ks_ref[rows, hs] for hs in hss]
        v = [i_ref[rows, hs] for hs in hss]
        gc = [gc_ref[rows, hs] for hs in hss]

        def near_products(q, k, gc):
            prods = []
            for top, end in blocks:
                for j in range(top, end):
                    lo = (j // 8) * 8
                    e = jnp.exp2(gc[lo:end, :] - gc[j:j + 1, :])
                    if j % 8:
                        head = jnp.where(rows_8d >= j - lo, e[:8], 0.0)
                        e = jnp.concatenate([head, e[8:]], axis=0) if lo + 8 < end else head
                    prods.append(q[lo:end, :] * k[j:j + 1, :] * e)
            return jnp.concatenate(prods, axis=0).astype(BF16)

        def far_operands(q, k, gc):
            out = []
            for top, end in blocks[1:]:
                g_b = gc[top - 1:top, :]
                out.append((q[top:end, :] * jnp.exp(gc[top:end, :] - g_b),
                            k[:top, :] * jnp.exp(jnp.minimum(g_b - gc[:top, :], 0.0))))
            return out

        near = [near_products(a, b, g * LOG2E) for a, b, g in zip(q, k, gc)]
        far_ops = [far_operands(*x) for x in zip(q, k, gc)]
        st = [st_ref[hh] for hh in range(nh)]
        gl = [x[c - 1:c, :] for x in gc]
        sums = [jnp.dot(x, ones_dd, preferred_element_type=F32) for x in near]
        qk_far = [[_mm_nt(qe, ke) for qe, ke in ops] for ops in far_ops]
        far = [[_mm(a, vv[:top, :]) for a, (top, _) in zip(qs, blocks[1:])] for qs, vv in zip(qk_far, v)]
        o_st = [_mm_nt(a * jnp.exp(g), s_) for a, g, s_ in zip(q, gc, st)]
        kv = [_mm_tn(vv, kk * jnp.exp(g_l - g)) for vv, kk, g_l, g in zip(v, k, gl, gc)]

        for hh, hs in enumerate(hss):
            groups = [jnp.zeros((8, d), F32) for _ in range(c // 8)]
            at = 0
            for top, end in blocks:
                for j in range(top, end):
                    v_j = v[hh][j:j + 1, :]
                    for g in range(j // 8, end // 8):
                        groups[g] = groups[g] + sums[hh][at:at + 8, :] * v_j
                        at += 8
            for f, (top, end) in zip(far[hh], blocks[1:]):
                for g in range(top // 8, end // 8):
                    groups[g] = groups[g] + f[(g * 8 - top):(g * 8 - top + 8), :]
            o = jnp.concatenate(groups, axis=0) + o_st[hh]
            st_ref[hh] = st[hh] * jnp.exp(gl[hh]) + kv[hh]
            o_ref[rows, hs] = (_rms(o, gnorm) * _silu(gate_ref[rows, hs])).astype(o_ref.dtype)
        return carry

    lax.fori_loop(0, ts // c, chunk_loop, 0)


def _hgrn2(p32, lb, d_norm_g, *, ts, cols):
    bsz, s, _ = p32.shape
    d = HEAD_DIM
    nh = N_HEADS
    w = nh * d
    kernel = functools.partial(_hgrn2_kernel, ts=ts)
    tile = lambda name: pl.BlockSpec((None, ts, w), lambda b, i: (b, i, cols[name] // nh))
    return pl.pallas_call(
        kernel,
        grid=(bsz, s // ts),
        in_specs=[tile("qd"), tile("fd"), tile("id"), tile("gd"),
                  pl.BlockSpec((1, w), lambda b, i: (0, 0)),
                  pl.BlockSpec((1, d), lambda b, i: (0, 0))],
        out_specs=pl.BlockSpec((None, ts, w), lambda b, i: (b, i, 0)),
        out_shape=jax.ShapeDtypeStruct((bsz, s, w), BF16),
        scratch_shapes=[pltpu.VMEM((ts, w), F32), pltpu.VMEM((ts, w), F32),
                        pltpu.VMEM((ts, w), F32), pltpu.VMEM((nh, d, d), F32)],
        compiler_params=pltpu.CompilerParams(
            dimension_semantics=("parallel", "arbitrary"), vmem_limit_bytes=VMEM_LIMIT),
        name="hgrn2",
    )(p32, p32, p32, p32, lb.astype(F32).reshape(1, w), d_norm_g.astype(F32).reshape(1, d))


def _stickbreak_kernel(q_ref, k_ref, v_ref, o_ref, acc_ref, *, tq):
    i = pl.program_id(1)
    d = HEAD_DIM
    nh = N_HEADS
    row = _iota((tq, tq), 0)
    col = _iota((tq, tq), 1)
    causal = col < row
    later = (row > col).astype(BF16)
    later2 = jnp.concatenate([later, later], axis=0)

    heads = [slice(hh * d, (hh + 1) * d) for hh in range(nh)]

    def scores(blocks):
        jobs = [(j, dg, hs) for j, dg in blocks for hs in heads]
        z = [_mm_nt(q_ref[:, hs], k_ref[pl.ds(pl.multiple_of(j * tq, tq), tq), hs]) * (d ** -0.5)
             for j, _, hs in jobs]
        sp = [_softplus(x) for x in z]
        l1m = [jnp.where(causal, -x, 0.0) if dg else -x for x, (_, dg, _) in zip(sp, jobs)]
        rest = [jnp.dot(jnp.concatenate(_split(x), axis=1), later2, preferred_element_type=F32)
                for x in l1m]
        out = [((a - b) + r, l) for a, b, r, l in zip(z, sp, rest, l1m)]
        return [out[b * nh:(b + 1) * nh] for b in range(len(blocks))]

    def block(j, carries):
        (sc,) = scores([(j, False)])
        ps = [jnp.exp(logw + c) for (logw, _), c in zip(sc, carries)]
        pv = [_mm(p, v_ref[pl.ds(pl.multiple_of(j * tq, tq), tq), hs]) for p, hs in zip(ps, heads)]
        for hs, x in zip(heads, pv):
            acc_ref[:, hs] += x
        return tuple(c + jnp.sum(l1m, axis=-1, keepdims=True) for (_, l1m), c in zip(sc, carries))

    jp = jnp.maximum(i - 1, 0)
    live = jnp.where(i > 0, 1.0, 0.0)
    sd, sp_ = scores([(i, True), (jp, False)])
    carries = []
    for hh, hs in enumerate(heads):
        c1 = jnp.sum(sd[hh][1], axis=-1, keepdims=True)
        p_d = jnp.where(causal, jnp.exp(sd[hh][0]), 0.0)
        p_p = jnp.exp(sp_[hh][0] + c1) * live
        acc_ref[:, hs] = (_mm(p_d, v_ref[pl.ds(pl.multiple_of(i * tq, tq), tq), hs])
                          + _mm(p_p, v_ref[pl.ds(pl.multiple_of(jp * tq, tq), tq), hs]))
        carries.append(c1 + jnp.sum(sp_[hh][1], axis=-1, keepdims=True))
    carries = tuple(carries)

    def cond(c):
        worst = functools.reduce(jnp.maximum, c[1])
        return jnp.logical_and(c[0] >= 0, jnp.max(worst) >= EXP_ZERO_BELOW)

    def body(c):
        return c[0] - 1, block(c[0], c[1])

    lax.while_loop(cond, body, (i - 2, carries))
    o_ref[...] = acc_ref[...].astype(o_ref.dtype)


def _stickbreak(p16, *, tq, cols):
    bsz, s, _ = p16.shape
    nh = N_HEADS
    w = nh * HEAD_DIM
    kernel = functools.partial(_stickbreak_kernel, tq=tq)
    resident = dict(pipeline_mode=pl.Buffered(1))
    return pl.pallas_call(
        kernel,
        grid=(bsz, s // tq),
        in_specs=[pl.BlockSpec((None, tq, w), lambda b, i: (b, i, cols["qc"] // nh)),
                  pl.BlockSpec((None, s, w), lambda b, i: (b, 0, cols["kc"] // nh), **resident),
                  pl.BlockSpec((None, s, w), lambda b, i: (b, 0, cols["vc"] // nh), **resident)],
        out_specs=pl.BlockSpec((None, tq, w), lambda b, i: (b, i, 0)),
        out_shape=jax.ShapeDtypeStruct((bsz, s, w), BF16),
        scratch_shapes=[pltpu.VMEM((tq, w), F32)],
        compiler_params=pltpu.CompilerParams(
            dimension_semantics=("parallel", "arbitrary"), vmem_limit_bytes=VMEM_LIMIT),
        name="stickbreak",
    )(p16, p16, p16)


def _dsa_kernel(qi_ref, smq_ref, q_ref, sm_ref, k_ref, vt_ref, bias_ref, o_ref,
                sc_ref, scb_ref, qct_ref, kc_ref, bd_ref, lg_ref, *, tq, k_sel, wi_lane, wide):
    i = pl.program_id(1)
    tk = tq
    d = HEAD_DIM
    nh = N_HEADS
    ksel = float(k_sel)
    per_wide = wide // tk
    n_wide = (i + per_wide) // per_wide
    sub = 2 * tk
    lane_q = _iota((1, tq), 1)

    def tree(parts, op):
        while len(parts) > 1:
            parts = [op(parts[j], parts[j + 1]) if j + 1 < len(parts) else parts[j]
                     for j in range(0, len(parts), 2)]
        return parts[0]

    def col_fold(x, op=jnp.add, rows=8):
        return tree([x[r * rows:(r + 1) * rows] for r in range(x.shape[0] // rows)], op)

    @pl.when(i == 0)
    def _():
        def prep(g, carry):
            g0 = pl.multiple_of(g * wide, wide)
            hi, lo = _split(sm_ref[pl.ds(g0, wide), :][:, :IDX_DIM])
            kc_ref[pl.ds(g0, wide), :] = jnp.concatenate([hi, lo, hi], axis=1)
            return carry
        lax.fori_loop(0, sm_ref.shape[0] // wide, prep, 0)

    qit = qi_ref[...].T
    for p in range(IDX_HEADS // 2):
        halves = []
        for hh in (2 * p, 2 * p + 1):
            hi, lo = _split(qit[hh * IDX_DIM:(hh + 1) * IDX_DIM, :])
            halves.append(jnp.concatenate([hi, hi, lo], axis=0))
        qct_ref[p] = jnp.concatenate(halves, axis=1)
    w_rows = smq_ref[...].T[wi_lane:wi_lane + IDX_HEADS, :] * ((IDX_HEADS ** -0.5) * (IDX_DIM ** -0.5))

    q2t = (q_ref[...] * ((d ** -0.5) * LOG2E)).T.astype(BF16)
    zero_dq = jnp.zeros((d, tq), BF16)
    for p in range(nh // 2):
        top = jnp.concatenate([q2t[2 * p * d:(2 * p + 1) * d], zero_dq], axis=1)
        bot = jnp.concatenate([zero_dq, q2t[(2 * p + 1) * d:(2 * p + 2) * d]], axis=1)
        bd_ref[p] = jnp.concatenate([top, bot], axis=0)

    limit = i * tq + (lane_q // CHUNK + 1) * CHUNK

    rows_s = _iota((sub, tq), 0)

    def score_groups(gs, mm, masked):
        mn, mx = mm
        k0s = [pl.multiple_of(g * wide + sb * sub, sub) for g in gs for sb in range(wide // sub)]
        keys = [kc_ref[pl.ds(k0, sub), :] for k0 in k0s]
        accs = [jnp.zeros((sub, tq), F32) for _ in k0s]
        for p in range(IDX_HEADS // 2):
            rhs = qct_ref[p]
            for n, kk in enumerate(keys):
                s2 = jnp.dot(kk, rhs, preferred_element_type=F32)
                accs[n] = (accs[n] + jnp.maximum(s2[:, :tq], 0.0) * w_rows[2 * p:2 * p + 1, :]
                           + jnp.maximum(s2[:, tq:], 0.0) * w_rows[2 * p + 1:2 * p + 2, :])
        for k0, sct in zip(k0s, accs):
            if masked:
                adm = (k0 + rows_s) < limit
                mn = jnp.minimum(mn, col_fold(jnp.where(adm, sct, jnp.inf), jnp.minimum))
                sct = jnp.where(adm, sct, -jnp.inf)
            else:
                mn = jnp.minimum(mn, col_fold(sct, jnp.minimum))
            mx = jnp.maximum(mx, col_fold(sct, jnp.maximum))
            sc_ref[pl.ds(k0, sub), :] = sct
            scb_ref[pl.ds(k0, sub), :] = _floor_bf16(sct)
        return mn, mx

    def score_pair(j, mm):
        return score_groups((2 * j, 2 * j + 1), mm, False)

    n_full = n_wide - 1
    mm = lax.fori_loop(0, n_full // 2, score_pair,
                       (jnp.full((8, tq), jnp.inf, F32), jnp.full((8, tq), -jnp.inf, F32)))
    mm = lax.cond(n_full % 2 == 1, lambda c: score_groups((n_full - 1,), c, False), lambda c: c, mm)
    mn, mx = score_groups((n_wide - 1,), mm, True)

    n_pairs = (n_wide + 1) // 2

    @pl.when(n_wide % 2 == 1)
    def _():
        sc_ref[pl.ds(pl.multiple_of(n_wide * wide, wide), wide), :] = jnp.full((wide, tq), -jnp.inf, F32)
        scb_ref[pl.ds(pl.multiple_of(n_wide * wide, wide), wide), :] = jnp.full((wide, tq), -jnp.inf, BF16)
    rmin = jnp.min(mn, axis=0, keepdims=True)
    rmax = jnp.max(mx, axis=0, keepdims=True)

    def count(pred):
        def body(j, acc):
            for g in (2 * j, 2 * j + 1):
                acc = acc + col_fold(pred(sc_ref[pl.ds(pl.multiple_of(g * wide, wide), wide), :]))
            return acc
        return jnp.sum(lax.fori_loop(0, n_pairs, body, jnp.zeros((8, tq), F32)), axis=0, keepdims=True)

    def max_below(x):
        def body(j, acc):
            for g in (2 * j, 2 * j + 1):
                blk = sc_ref[pl.ds(pl.multiple_of(g * wide, wide), wide), :]
                acc = jnp.maximum(acc, col_fold(jnp.where(blk < x, blk, -jnp.inf), jnp.maximum))
            return acc
        return jnp.max(lax.fori_loop(0, n_pairs, body, jnp.full((8, tq), -jnp.inf, F32)), axis=0, keepdims=True)

    n_adm = limit.astype(F32)
    all_sel = n_adm <= ksel

    def bisect(c):
        lo, hi, c_lo = c
        mid = 0.5 * lo + 0.5 * hi
        cm = count(lambda blk: _ind(blk >= mid))
        ge = cm >= ksel
        return jnp.where(ge, mid, lo), jnp.where(ge, hi, mid), jnp.where(ge, cm, c_lo)

    def pending(c_lo, tied):
        return jnp.where(all_sel, 0.0, jnp.where(tied > 0.5, 0.0, _ind(c_lo != ksel)))

    def bisect_coarse(_, c):
        lo, hi, c_lo = c
        mid = _floor_bf16(0.5 * lo + 0.5 * hi).astype(F32)
        t_b = jnp.broadcast_to(mid, (16, tq)).astype(BF16)
        one_b = jnp.ones((16, tq), BF16)
        zero_b = jnp.zeros((16, tq), BF16)

        def body(j, acc):
            for g in (2 * j, 2 * j + 1):
                blk = scb_ref[pl.ds(pl.multiple_of(g * wide, wide), wide), :]
                ind = [jnp.where(blk[r * 16:(r + 1) * 16] >= t_b, one_b, zero_b) for r in range(wide // 16)]
                acc = acc + tree(ind, jnp.add).astype(F32)
            return acc

        acc = lax.fori_loop(0, n_pairs, body, jnp.zeros((16, tq), F32))
        cm = jnp.sum(acc, axis=0, keepdims=True)
        ge = cm >= ksel
        return jnp.where(ge, mid, lo), jnp.where(ge, hi, mid), jnp.where(ge, cm, c_lo)

    lo0 = _floor_bf16(rmin).astype(F32)
    hi0 = _floor_bf16(rmax + (jnp.abs(rmax) * (2.0 ** -6) + 1e-30)).astype(F32)
    state = lax.fori_loop(0, BISECT_COARSE, bisect_coarse, (lo0, hi0, n_adm))
    state = lax.fori_loop(0, BISECT_FIXED, lambda _, c: bisect(c), state)

    def round_cond(c):
        return jnp.max(pending(c[0][2], c[1])) > 0.5

    def round_body(c):
        st, tied, v, need = c

        def more_cond(s):
            return jnp.logical_and(s[0] < BISECT_EXTRA, jnp.max(pending(s[1][2], tied)) > 0.5)

        _, st = lax.while_loop(more_cond, lambda s: (s[0] + 1, bisect(s[1])), (jnp.int32(0), st))
        pend = pending(st[2], tied)

        def check(_):
            cand = max_below(st[1])
            c_ge = count(lambda blk: _ind(blk >= cand))
            c_gt = count(lambda blk: _ind(blk > cand))
            ok = jnp.where(pend > 0.5, _ind(c_ge >= ksel), 0.0)
            return (jnp.where(ok > 0.5, 1.0, tied), jnp.where(ok > 0.5, cand, v),
                    jnp.where(ok > 0.5, ksel - c_gt, need))

        tied, v, need = lax.cond(jnp.max(pend) > 0.5, check, lambda _: (tied, v, need), 0)
        return st, tied, v, need

    zeros1 = jnp.zeros((1, tq), F32)
    (lo_f, _, _), tied, v_tie, need = lax.while_loop(round_cond, round_body, (state, zeros1, zeros1, zeros1))
    vth = jnp.where(all_sel, F32_LOWEST, jnp.where(tied > 0.5, v_tie, lo_f))

    @pl.when(jnp.max(tied) > 0.5)
    def _():
        v_eq = jnp.where(tied > 0.5, v_tie, jnp.inf)
        incl = (_iota((tk, tk), 1) <= _iota((tk, tk), 0)).astype(BF16)

        def demote(g, seen):
            g0 = pl.multiple_of(g * wide, wide)
            xs = [sc_ref[pl.ds(g0 + pb * tk, tk), :] for pb in range(per_wide)]
            eqs = [_ind(x == v_eq) for x in xs]
            inblk = [jnp.dot(incl, e.astype(BF16), preferred_element_type=F32) for e in eqs]
            for pb in range(per_wide):
                rank = inblk[pb] + seen
                sc_ref[pl.ds(g0 + pb * tk, tk), :] = jnp.where(eqs[pb] * _ind(rank > need) > 0.5,
                                                               -jnp.inf, xs[pb])
                seen = seen + jnp.sum(col_fold(eqs[pb]), axis=0, keepdims=True)
            return seen

        lax.fori_loop(0, n_wide, demote, zeros1)

    g_near = jnp.maximum(i - 1, 0) // per_wide

    def logit_group(g, mx, near):
        out = list(mx)
        for sb in range(wide // sub):
            k0 = pl.multiple_of(g * wide + sb * sub, sub)
            sel = sc_ref[pl.ds(k0, sub), :] >= vth
            for p in range(nh // 2):
                pair = jnp.dot(k_ref[pl.ds(k0, sub), 2 * p * d:(2 * p + 2) * d], bd_ref[p],
                               preferred_element_type=F32)
                for hh in (2 * p, 2 * p + 1):
                    lm = pair[:, (hh - 2 * p) * tq:(hh - 2 * p + 1) * tq]
                    if near:
                        back = [jnp.clip(i - (g * per_wide + sb * (sub // tk) + pb), 0, 2)
                                for pb in range(sub // tk)]
                        lm = lm + jnp.concatenate([bias_ref[bk, hh] for bk in back], axis=0)
                    lm = jnp.where(sel, lm, NEG_BIG)
                    lg_ref[hh, pl.ds(k0, sub), :] = lm
                    out[hh] = jnp.maximum(out[hh], col_fold(lm, jnp.maximum))
        return tuple(out)

    mx = tuple(jnp.full((8, tq), NEG_BIG, F32) for _ in range(nh))
    def logit_pair(j, mx, near):
        return logit_group(2 * j + 1, logit_group(2 * j, mx, near), near)

    far_pairs = g_near // 2
    mx = lax.fori_loop(0, far_pairs, functools.partial(logit_pair, near=False), mx)
    mx = lax.fori_loop(far_pairs, n_pairs, functools.partial(logit_pair, near=True), mx)
    m_q = [jnp.max(mx[hh], axis=0, keepdims=True) for hh in range(nh)]

    ones_rows = jnp.ones((8, wide), BF16)

    def pv_pair(j, carry):
        ls, accs = list(carry[0]), list(carry[1])
        jobs = [(pl.multiple_of(g * wide, wide), hh) for g in (2 * j, 2 * j + 1) for hh in range(nh)]
        ps = [jnp.exp2(lg_ref[hh, pl.ds(g0, wide), :] - m_q[hh]).astype(BF16) for g0, hh in jobs]
        outs = [jnp.dot(jnp.concatenate([vt_ref[hh * d:(hh + 1) * d, pl.ds(g0, wide)], ones_rows], axis=0),
                        p, preferred_element_type=F32) for (g0, hh), p in zip(jobs, ps)]
        for (_, hh), out in zip(jobs, outs):
            ls[hh] = ls[hh] + out[d:]
            accs[hh] = accs[hh] + out[:d]
        return tuple(ls), tuple(accs)

    ls, accs = lax.fori_loop(0, n_pairs, pv_pair,
                             (tuple(jnp.zeros((8, tq), F32) for _ in range(nh)),
                              tuple(jnp.zeros((d, tq), F32) for _ in range(nh))))
    for hh in range(nh):
        o_ref[:, hh * d:(hh + 1) * d] = (accs[hh] / ls[hh][0:1]).T.astype(o_ref.dtype)


def _dsa(p32, p16, vt, bias_tiles, *, tq, cols):
    bsz, s, _ = p32.shape
    d = HEAD_DIM
    nh = N_HEADS
    wide = 4 * tq
    k_sel = min(TOPK_MAX, s // 4)
    w512 = nh * d
    kernel = functools.partial(_dsa_kernel, tq=tq, k_sel=k_sel, wi_lane=cols["wi_lane"], wide=wide)
    resident = dict(pipeline_mode=pl.Buffered(1))
    return pl.pallas_call(
        kernel,
        grid=(bsz, s // tq),
        in_specs=[pl.BlockSpec((None, tq, w512), lambda b, i: (b, i, cols["qi"] // nh)),
                  pl.BlockSpec((None, tq, d), lambda b, i: (b, i, cols["small"])),
                  pl.BlockSpec((None, tq, w512), lambda b, i: (b, i, cols["qb"] // nh)),
                  pl.BlockSpec((None, s, d), lambda b, i: (b, 0, cols["small"]), **resident),
                  pl.BlockSpec((None, s, w512), lambda b, i: (b, 0, cols["kb"] // nh), **resident),
                  pl.BlockSpec((w512, s), lambda b, i: (0, b), **resident),
                  pl.BlockSpec((3, nh, tq, tq), lambda b, i: (0, 0, 0, 0), **resident)],
        out_specs=pl.BlockSpec((None, tq, w512), lambda b, i: (b, i, 0)),
        out_shape=jax.ShapeDtypeStruct((bsz, s, w512), BF16),
        scratch_shapes=[pltpu.VMEM((s, tq), F32),
                        pltpu.VMEM((s, tq), BF16),
                        pltpu.VMEM((IDX_HEADS // 2, 3 * IDX_DIM, 2 * tq), BF16),
                        pltpu.VMEM((s, 3 * IDX_DIM), BF16),
                        pltpu.VMEM((nh // 2, 2 * d, 2 * tq), BF16),
                        pltpu.VMEM((nh, s, tq), F32)],
        compiler_params=pltpu.CompilerParams(
            dimension_semantics=("parallel", "arbitrary"), vmem_limit_bytes=VMEM_LIMIT),
        name="dsa",
    )(p32, p32, p32, p32, p16, vt, bias_tiles)


def _t5_bucket(rel):
    nb = REL_BUCKETS // 2
    max_exact = nb // 2
    ret = jnp.where(rel > 0, nb, 0)
    n = jnp.abs(rel)
    large = max_exact + (jnp.log(jnp.maximum(n, 1).astype(F32) / max_exact)
                         / math.log(REL_MAX_DIST / max_exact) * (nb - max_exact)).astype(jnp.int32)
    large = jnp.minimum(large, nb - 1)
    return ret + jnp.where(n < max_exact, n, large)


def _bias_tiles(rel_table, tq):
    assert tq >= REL_MAX_DIST
    t = jnp.arange(tq)
    back = jnp.arange(3)
    rel = (t[None, None, :] - back[:, None, None] * tq) - t[None, :, None]
    onehot = (_t5_bucket(rel)[..., None] == jnp.arange(REL_BUCKETS)).astype(F32)
    tiles = jnp.einsum("bqkn,nh->bhkq", onehot, rel_table.astype(F32),
                       precision=HIGHEST)
    return (tiles - tiles[2:3]) * LOG2E


def _even_layout(w_in):
    d = HEAD_DIM
    a_w = 2 * N_HEADS * d + N_HEADS * d
    offs = {}
    o = 0
    for name, w in (("qkv", a_w), ("z", N_HEADS * d), ("a", N_HEADS), ("b", N_HEADS),
                    ("qb", N_HEADS * d), ("kb", N_HEADS * d), ("vb", N_HEADS * d),
                    ("qi", IDX_HEADS * IDX_DIM), ("ki", IDX_DIM), ("wi", IDX_HEADS)):
        offs[name] = (o, o + w)
        o += w
    assert o == w_in.shape[1]
    sl = lambda n: w_in[:, offs[n][0]:offs[n][1]]
    small_w = IDX_DIM + 2 * N_HEADS + IDX_HEADS
    small_pad = -small_w % d
    zeros = lambda n: jnp.zeros((w_in.shape[0], n), w_in.dtype)
    w32 = jnp.concatenate([sl("qkv"), sl("z"), sl("qb"), sl("qi"),
                           sl("ki"), sl("a"), sl("b"), sl("wi"), zeros(small_pad)], axis=1)
    n32 = w32.shape[1]
    tn = n32 // 5
    assert tn * 5 == n32 and tn % d == 0
    w16 = jnp.concatenate([sl("kb"), zeros(tn - N_HEADS * d)], axis=1)
    nh = N_HEADS
    cols = dict(qa=0, ka=nh, va=2 * nh, za=3 * nh, qb=4 * nh, qi=5 * nh, small=6 * nh, kb=0,
                a_lane=IDX_DIM, b_lane=IDX_DIM + nh, wi_lane=IDX_DIM + 2 * nh, n32=n32, tn=tn)
    return jnp.concatenate([w32, w16], axis=1).astype(BF16), sl("vb").T.astype(BF16), cols


def kernel(x, norm_g, w_in_even, conv_w_even, a_log_even, dt_bias_even, a_norm_even, w_out_even,
           rel_bias, w_in_odd, lb_logits, d_norm_odd, w_out_odd, w_gate, w_up, w_down):
    bsz, s, d = x.shape
    t = bsz * s
    depth = norm_g.shape[0]
    nh = N_HEADS
    tq = Q_TILE
    lb_all = jnp.cumsum(jax.nn.softmax(lb_logits.astype(F32), axis=0), axis=0)
    lb_all = lb_all - lb_all[:1]
    odd_cols = dict(qc=0, kc=nh, vc=2 * nh, qd=0, fd=nh, id=2 * nh, gd=3 * nh)
    bias_tiles = _bias_tiles(rel_bias, tq)

    h = x.reshape(t, d)
    for l in range(depth):
        if l % 2 == 0:
            e = l // 2
            w_even, w_vt, cols = _even_layout(w_in_even[e])
            p32, p16, vt = _norm_matmul(h, norm_g[l, 0], w_even, tm=PROJ_TILE, tn=cols["tn"], n32=cols["n32"],
                                        w_t=w_vt)
            p32 = p32.reshape(bsz, s, -1)
            p16 = p16.reshape(bsz, s, -1)
            o_1 = _deltanet(p32, conv_w_even[e], a_log_even[e], dt_bias_even[e], a_norm_even[e],
                            ts=min(SEQ_TILE, s), cols=cols)
            o_2 = _dsa(p32, p16, vt, bias_tiles, tq=tq, cols=cols)
            w_out = w_out_even[e]
        else:
            o = l // 2
            n16 = 3 * nh * HEAD_DIM
            w_odd = jnp.concatenate([w_in_odd[o][:, n16:], w_in_odd[o][:, :n16]], axis=1).astype(BF16)
            p32, p16 = _norm_matmul(h, norm_g[l, 0], w_odd, tm=PROJ_TILE, tn=ODD_COL_TILE, n32=w_odd.shape[1] - n16)
            p32 = p32.reshape(bsz, s, -1)
            p16 = p16.reshape(bsz, s, -1)
            o_1 = _stickbreak(p16, tq=tq, cols=odd_cols)
            o_2 = _hgrn2(p32, lb_all[l], d_norm_odd[o], ts=min(SEQ_TILE, s), cols=odd_cols)
            w_out = w_out_odd[o]
        h = _mix_ffn(o_1.reshape(t, -1), o_2.reshape(t, -1), w_out, h, norm_g[l, 1], norm_g[l, 2], norm_g[l, 3],
                     w_gate[l], w_up[l], w_down[l], tm=ROW_TILE, tf=FFN_TILE)
    return h.reshape(bsz, s, d)
```

```python
import functools
import math

import jax
import jax.numpy as jnp
from jax import lax
from jax.experimental import pallas as pl
from jax.experimental.pallas import tpu as pltpu

F32 = jnp.float32
BF16 = jnp.bfloat16
HIGHEST = lax.Precision.HIGHEST

CHUNK = 64
HEAD_DIM = 128
N_HEADS = 4
IDX_HEADS = 8
IDX_DIM = 64
TOPK_MAX = 256
CONV_WIDTH = 4
REL_BUCKETS = 32
REL_MAX_DIST = 128
EPS = 1e-6
NEG_BIG = -1e30
LOG2E = 1.4426950408889634
BISECT_COARSE = 12
BISECT_FIXED = 8
BISECT_EXTRA = 6
F32_LOWEST = -3.4028234663852886e38
EXP_ZERO_BELOW = -104.0
VMEM_LIMIT = 56 * 1024 * 1024

PROJ_TILE = 2048
ROW_TILE = 512
SEQ_TILE = 512
Q_TILE = 128
ODD_COL_TILE = 512
FFN_TILE = 2816


def _mm(a, b):
    return jnp.dot(a.astype(BF16), b.astype(BF16), preferred_element_type=F32)


def _mm_nt(a, b):
    return lax.dot_general(a.astype(BF16), b.astype(BF16), (((1,), (1,)), ((), ())),
                           preferred_element_type=F32)


def _mm_tn(a, b):
    return lax.dot_general(a.astype(BF16), b.astype(BF16), (((0,), (0,)), ((), ())),
                           preferred_element_type=F32)


def _split(x):
    hi = x.astype(BF16)
    return hi, (x - hi.astype(F32)).astype(BF16)


def _floor_bf16(x):
    bits = pltpu.bitcast(x, jnp.int32)
    down = jnp.where(bits >= 0, bits, bits + 0xFFFF) & jnp.int32(-65536)
    return pltpu.bitcast(down, F32).astype(BF16)


def _sigmoid(x):
    return 1.0 / (1.0 + jnp.exp(-x))


def _silu(x):
    return x * _sigmoid(x)


def _softplus(x):
    return jnp.maximum(x, 0.0) + jnp.log1p(jnp.exp(-jnp.abs(x)))


def _rms(x, g):
    return x * lax.rsqrt(jnp.mean(x * x, axis=-1, keepdims=True) + EPS) * g


def _iota(shape, dim):
    return lax.broadcasted_iota(jnp.int32, shape, dim)


def _ind(mask):
    return jnp.where(mask, 1.0, 0.0)


def _norm_matmul_kernel(x_ref, g_ref, w_ref, *rest, n_t, tiles32):
    if n_t:
        wt_ref, o32_ref, o16_ref, ot_ref, xn_ref = rest
    else:
        o32_ref, o16_ref, xn_ref = rest
    j = pl.program_id(1)

    @pl.when(j == 0)
    def _():
        xn_ref[...] = _rms(x_ref[...], g_ref[...]).astype(BF16)
        if n_t:
            ot_ref[...] = lax.dot_general(wt_ref[...], xn_ref[...], (((1,), (1,)), ((), ())),
                                          preferred_element_type=F32).astype(BF16)

    y = jnp.dot(xn_ref[...], w_ref[...], preferred_element_type=F32)

    @pl.when(j < tiles32)
    def _():
        o32_ref[...] = y

    @pl.when(j >= tiles32)
    def _():
        o16_ref[...] = y.astype(BF16)


def _norm_matmul(x, g, w, *, tm, tn, n32, w_t=None):
    t, d = x.shape
    n = w.shape[1]
    n_t = 0 if w_t is None else w_t.shape[0]
    tiles32 = n32 // tn
    assert tiles32 * tn == n32 and (n - n32) % tn == 0 and 0 < n32 < n
    in_specs = [pl.BlockSpec((tm, d), lambda i, j: (i, 0)),
                pl.BlockSpec((1, d), lambda i, j: (0, 0)),
                pl.BlockSpec((d, tn), lambda i, j: (0, j))]
    out_specs = [pl.BlockSpec((tm, tn), lambda i, j: (i, jnp.minimum(j, tiles32 - 1))),
                 pl.BlockSpec((tm, tn), lambda i, j: (i, jnp.maximum(j - tiles32, 0)))]
    out_shape = [jax.ShapeDtypeStruct((t, n32), F32), jax.ShapeDtypeStruct((t, n - n32), BF16)]
    args = [x, g.reshape(1, d), w]
    if n_t:
        in_specs.append(pl.BlockSpec((n_t, d), lambda i, j: (0, 0)))
        out_specs.append(pl.BlockSpec((n_t, tm), lambda i, j: (0, i)))
        out_shape.append(jax.ShapeDtypeStruct((n_t, t), BF16))
        args.append(w_t)
    return pl.pallas_call(
        functools.partial(_norm_matmul_kernel, n_t=n_t, tiles32=tiles32),
        grid=(t // tm, n // tn),
        in_specs=in_specs,
        out_specs=out_specs,
        out_shape=out_shape,
        scratch_shapes=[pltpu.VMEM((tm, d), BF16)],
        compiler_params=pltpu.CompilerParams(
            dimension_semantics=("parallel", "arbitrary"), vmem_limit_bytes=VMEM_LIMIT),
        name="norm_matmul",
    )(*args)


def _mix_ffn_kernel(ca_ref, cb_ref, wa_ref, wb_ref, h_ref, gmix_ref, gpre_ref, gpost_ref,
                    wg_ref, wu_ref, wd_ref, o_ref, h1_ref, xn_ref, acc_ref):
    f = pl.program_id(1)

    @pl.when(f == 0)
    def _():
        y = (jnp.dot(ca_ref[...], wa_ref[...], preferred_element_type=F32)
             + jnp.dot(cb_ref[...], wb_ref[...], preferred_element_type=F32))
        h1 = h_ref[...] + _rms(y, gmix_ref[...])
        h1_ref[...] = h1
        xn_ref[...] = _rms(h1, gpre_ref[...]).astype(BF16)
        acc_ref[...] = jnp.zeros_like(acc_ref)

    xn = xn_ref[...]
    gate = jnp.dot(xn, wg_ref[...], preferred_element_type=F32)
    up = jnp.dot(xn, wu_ref[...], preferred_element_type=F32)
    act = (_silu(gate) * up).astype(BF16)
    acc_ref[...] += jnp.dot(act, wd_ref[...], preferred_element_type=F32)

    @pl.when(f == pl.num_programs(1) - 1)
    def _():
        o_ref[...] = h1_ref[...] + _rms(acc_ref[...], gpost_ref[...])


def _mix_ffn(ca, cb, w_out, h, g_mix, g_pre, g_post, wg, wu, wd, *, tm, tf):
    t, d = h.shape
    ff = wg.shape[1]
    wa_n = ca.shape[1]
    wb_n = cb.shape[1]
    row = pl.BlockSpec((1, d), lambda i, f: (0, 0))
    once = dict(pipeline_mode=pl.Buffered(1)) if tf == ff else {}
    return pl.pallas_call(
        _mix_ffn_kernel,
        grid=(t // tm, ff // tf),
        in_specs=[pl.BlockSpec((tm, wa_n), lambda i, f: (i, 0)),
                  pl.BlockSpec((tm, wb_n), lambda i, f: (i, 0)),
                  pl.BlockSpec((wa_n, d), lambda i, f: (0, 0)),
                  pl.BlockSpec((wb_n, d), lambda i, f: (0, 0)),
                  pl.BlockSpec((tm, d), lambda i, f: (i, 0)),
                  row, row, row,
                  pl.BlockSpec((d, tf), lambda i, f: (0, f), **once),
                  pl.BlockSpec((d, tf), lambda i, f: (0, f), **once),
                  pl.BlockSpec((tf, d), lambda i, f: (f, 0), **once)],
        out_specs=pl.BlockSpec((tm, d), lambda i, f: (i, 0)),
        out_shape=jax.ShapeDtypeStruct((t, d), F32),
        scratch_shapes=[pltpu.VMEM((tm, d), F32), pltpu.VMEM((tm, d), BF16), pltpu.VMEM((tm, d), F32)],
        compiler_params=pltpu.CompilerParams(
            dimension_semantics=("parallel", "arbitrary"), vmem_limit_bytes=VMEM_LIMIT),
        name="mix_ffn",
    )(ca, cb, w_out[:wa_n].astype(BF16), w_out[wa_n:].astype(BF16), h,
      g_mix.reshape(1, d), g_pre.reshape(1, d), g_post.reshape(1, d),
      wg.astype(BF16), wu.astype(BF16), wd.astype(BF16))


def _deltanet_kernel(xq_ref, xk_ref, xv_ref, z_ref, sm_ref, cwq_ref, cwk_ref, cwv_ref,
                     alog_ref, dtb_ref, gn_ref, o_ref,
                     xpad_ref, q_ref, k_ref, v_ref, gb_ref, bb_ref, u_ref, w_ref, qk_ref, st_ref,
                     *, ts, a_col, b_col):
    s = pl.program_id(1)
    c = CHUNK
    d = HEAD_DIM
    nh = N_HEADS

    @pl.when(s == 0)
    def _():
        xpad_ref[:, 0:8, :] = jnp.zeros((3, 8, nh * d), F32)
        st_ref[...] = jnp.zeros_like(st_ref)

    @pl.when(s != 0)
    def _():
        xpad_ref[:, 0:8, :] = xpad_ref[:, ts:ts + 8, :]

    xpad_ref[0, 8:ts + 8, :] = xq_ref[...]
    xpad_ref[1, 8:ts + 8, :] = xk_ref[...]
    xpad_ref[2, 8:ts + 8, :] = xv_ref[...]

    def conv_silu(idx, cw_ref, hs):
        cw = cw_ref[:, hs]
        acc = xpad_ref[idx, 8 - (CONV_WIDTH - 1):8 - (CONV_WIDTH - 1) + ts, hs] * cw[0:1, :]
        for j in range(1, CONV_WIDTH):
            off = 8 - (CONV_WIDTH - 1) + j
            acc = acc + xpad_ref[idx, off:off + ts, hs] * cw[j:j + 1, :]
        return _silu(acc)

    def l2norm(t):
        return t * lax.rsqrt(jnp.sum(t * t, axis=-1, keepdims=True) + EPS)

    row = _iota((c, c), 0)
    col = _iota((c, c), 1)
    tri = (col <= row)
    strict = (col < row)
    tri_f = tri.astype(F32)
    upper_f = (row <= col).astype(F32)
    eye = (row == col).astype(F32)
    gnorm = gn_ref[...]
    chunks = range(ts // c)
    rs = [slice(ci * c, (ci + 1) * c) for ci in chunks]
    tri2 = jnp.concatenate([tri_f, tri_f], axis=1).astype(BF16)
    ones2 = jnp.ones((c, 2 * c), BF16)

    def cum2(lhs2, x):
        hi, lo = _split(x)
        return jnp.dot(lhs2, jnp.concatenate([hi, lo], axis=0), preferred_element_type=F32)

    for hh in range(nh):
        hs = slice(hh * d, (hh + 1) * d)
        q_ref[:, hs] = l2norm(conv_silu(0, cwq_ref, hs)) * (d ** -0.5)
        k_ref[:, hs] = l2norm(conv_silu(1, cwk_ref, hs))
        v_ref[:, hs] = conv_silu(2, cwv_ref, hs)

        a_raw = sm_ref[:, a_col + hh:a_col + hh + 1]
        b_raw = sm_ref[:, b_col + hh:b_col + hh + 1]
        g = -jnp.exp(alog_ref[:, hh:hh + 1]) * _softplus(a_raw + dtb_ref[:, hh:hh + 1])
        gb_ref[:, hs] = jnp.broadcast_to(g, (ts, d))
        bb_ref[:, hs] = jnp.broadcast_to(_sigmoid(b_raw), (ts, d))

        q = [q_ref[r, hs] for r in rs]
        k = [k_ref[r, hs] for r in rs]
        beta = [bb_ref[r, hs] for r in rs]
        gb = [gb_ref[r, hs] for r in rs]
        gc = [cum2(tri2, x) for x in gb]
        gc_row = [cum2(ones2, x[:, :c] * upper_f) for x in gb]
        decay = [jnp.where(tri, jnp.exp(jnp.minimum(a[:, :c] - b, 0.0)), 0.0) for a, b in zip(gc, gc_row)]
        kk = [_mm_nt(x, x) for x in k]
        n = [-jnp.where(strict, b[:, :c] * x * dc, 0.0) for b, x, dc in zip(beta, kk, decay)]
        inv = [eye + x for x in n]
        for step in range(5):
            nb = [x.astype(BF16) for x in n]
            n = [jnp.dot(x, x, preferred_element_type=F32) for x in nb]
            inv = [iv + _mm(iv, x) for iv, x in zip(inv, n)]
        egc = [jnp.exp(x) for x in gc]
        gl = [x[c - 1:c, :] for x in gc]
        inv_l = [x.astype(BF16) for x in inv]
        u = [_mm(a, v_ref[r, hs] * b) for a, r, b in zip(inv_l, rs, beta)]
        w = [_mm(a, x * (b * e)) for a, x, b, e in zip(inv_l, k, beta, egc)]
        qk = [_mm_nt(a, b) * dc for a, b, dc in zip(q, k, decay)]
        for ci in chunks:
            r = rs[ci]
            u_ref[r, hs] = u[ci]
            w_ref[r, hs] = w[ci]
            qk_ref[hh, r, :] = qk[ci]
            q_ref[r, hs] = q[ci] * egc[ci]
            k_ref[r, hs] = k[ci] * jnp.exp(gl[ci] - gc[ci])
            gb_ref[r, hs] = jnp.broadcast_to(jnp.exp(gl[ci]), (c, d))

    def chunk_body(ci, carry):
        r0 = pl.multiple_of(ci * c, c)
        rows = pl.ds(r0, c)
        hss = [slice(hh * d, (hh + 1) * d) for hh in range(nh)]
        st = [st_ref[hh] for hh in range(nh)]
        w_st = [_mm(w_ref[rows, hs], s_) for hs, s_ in zip(hss, st)]
        q_st = [_mm(q_ref[rows, hs], s_) for hs, s_ in zip(hss, st)]
        v_new = [u_ref[rows, hs] - x for hs, x in zip(hss, w_st)]
        o = [a + _mm(qk_ref[hh, rows, :], v) for hh, (a, v) in enumerate(zip(q_st, v_new))]
        kv = [_mm_tn(k_ref[rows, hs], v) for hs, v in zip(hss, v_new)]
        for hh, hs in enumerate(hss):
            st_ref[hh] = st[hh] * gb_ref[pl.ds(r0, 1), hs] + kv[hh]
            o_ref[rows, hs] = (_rms(o[hh], gnorm) * _silu(z_ref[rows, hs])).astype(o_ref.dtype)
        return carry

    lax.fori_loop(0, ts // c, chunk_body, 0)


def _deltanet(p32, conv_w, a_log, dt_bias, a_norm_g, *, ts, cols):
    bsz, s, _ = p32.shape
    d = HEAD_DIM
    nh = N_HEADS
    w = nh * d
    pad = lambda t: jnp.pad(t.astype(F32), (0, d - t.shape[0])).reshape(1, d)
    kernel = functools.partial(_deltanet_kernel, ts=ts, a_col=cols["a_lane"], b_col=cols["b_lane"])
    tile = lambda name: pl.BlockSpec((None, ts, w), lambda b, i: (b, i, cols[name] // nh))
    conv = lambda k: pl.BlockSpec((CONV_WIDTH, w), lambda b, i: (0, k))
    row = pl.BlockSpec((1, d), lambda b, i: (0, 0))
    return pl.pallas_call(
        kernel,
        grid=(bsz, s // ts),
        in_specs=[tile("qa"), tile("ka"), tile("va"), tile("za"),
                  pl.BlockSpec((None, ts, d), lambda b, i: (b, i, cols["small"])),
                  conv(0), conv(1), conv(2), row, row, row],
        out_specs=pl.BlockSpec((None, ts, w), lambda b, i: (b, i, 0)),
        out_shape=jax.ShapeDtypeStruct((bsz, s, w), BF16),
        scratch_shapes=[pltpu.VMEM((3, ts + 8, w), F32)]
        + [pltpu.VMEM((ts, w), F32) for _ in range(7)]
        + [pltpu.VMEM((nh, ts, CHUNK), F32), pltpu.VMEM((nh, d, d), F32)],
        compiler_params=pltpu.CompilerParams(
            dimension_semantics=("parallel", "arbitrary"), vmem_limit_bytes=VMEM_LIMIT),
        name="deltanet",
    )(p32, p32, p32, p32, p32, conv_w.astype(F32), conv_w.astype(F32), conv_w.astype(F32),
      pad(a_log), pad(dt_bias), a_norm_g.astype(F32).reshape(1, d))


def _hgrn2_kernel(q_ref, f_ref, i_ref, gate_ref, lb_ref, gn_ref, o_ref,
                  qs_ref, ks_ref, gc_ref, st_ref, *, ts):
    s = pl.program_id(1)
    c = CHUNK
    d = HEAD_DIM
    nh = N_HEADS
    SUB = 16

    @pl.when(s == 0)
    def _():
        st_ref[...] = jnp.zeros_like(st_ref)

    lb = lb_ref[...]
    f_raw = f_ref[...]
    log_sig = jnp.minimum(f_raw, 0.0) - jnp.log1p(jnp.exp(-jnp.abs(f_raw)))
    la = jnp.log(lb)
    lbb = jnp.log1p(-lb) + log_sig
    log_f = jnp.maximum(la, lbb) + jnp.log1p(jnp.exp(-jnp.abs(la - lbb)))
    qs_ref[...] = _silu(q_ref[...])
    ks_ref[...] = (1.0 - lb) * _sigmoid(-f_raw)

    row = _iota((c, c), 0)
    col = _iota((c, c), 1)
    tri_f = (col <= row).astype(F32)
    ones_dd = jnp.ones((d, d), BF16)
    rows_8d = _iota((8, d), 0)
    gnorm = gn_ref[...]

    tri2 = jnp.concatenate([tri_f, tri_f], axis=1).astype(BF16)
    for ci in range(ts // c):
        hi, lo = _split(log_f[ci * c:(ci + 1) * c, :])
        gc_ref[ci * c:(ci + 1) * c, :] = jnp.dot(tri2, jnp.concatenate([hi, lo], axis=0),
                                                 preferred_element_type=F32)

    blocks = [(sb * SUB, (sb + 1) * SUB) for sb in range(c // SUB)]

    def chunk_loop(ci, carry):
        r0 = pl.multiple_of(ci * c, c)
        rows = pl.ds(r0, c)
        hss = [slice(hh * d, (hh + 1) * d) for hh in range(nh)]
        q = [qs_ref[rows, hs] for hs in hss]
        k = [ks_ref[rows, hs] for hs in hss]
        v = [i_ref[rows, hs] for hs in hss]
        gc = [gc_ref[rows, hs] for hs in hss]

        def near_products(q, k, gc):
            prods = []
            for top, end in blocks:
                for j in range(top, end):
                    lo = (j // 8) * 8
                    e = jnp.exp2(gc[lo:end, :] - gc[j:j + 1, :])
                    if j % 8:
                        head = jnp.where(rows_8d >= j - lo, e[:8], 0.0)
                        e = jnp.concatenate([head, e[8:]], axis=0) if lo + 8 < end else head
                    prods.append(q[lo:end, :] * k[j:j + 1, :] * e)
            return jnp.concatenate(prods, axis=0).astype(BF16)

        def far_operands(q, k, gc):
            out = []
            for top, end in blocks[1:]:
                g_b = gc[top - 1:top, :]
                out.append((q[top:end, :] * jnp.exp(gc[top:end, :] - g_b),
                            k[:top, :] * jnp.exp(jnp.minimum(g_b - gc[:top, :], 0.0))))
            return out

        near = [near_products(a, b, g * LOG2E) for a, b, g in zip(q, k, gc)]
        far_ops = [far_operands(*x) for x in zip(q, k, gc)]
        st = [st_ref[hh] for hh in range(nh)]
        gl = [x[c - 1:c, :] for x in gc]
        sums = [jnp.dot(x, ones_dd, preferred_element_type=F32) for x in near]
        qk_far = [[_mm_nt(qe, ke) for qe, ke in ops] for ops in far_ops]
        far = [[_mm(a, vv[:top, :]) for a, (top, _) in zip(qs, blocks[1:])] for qs, vv in zip(qk_far, v)]
        o_st = [_mm_nt(a * jnp.exp(g), s_) for a, g, s_ in zip(q, gc, st)]
        kv = [_mm_tn(vv, kk * jnp.exp(g_l - g)) for vv, kk, g_l, g in zip(v, k, gl, gc)]

        for hh, hs in enumerate(hss):
            groups = [jnp.zeros((8, d), F32) for _ in range(c // 8)]
            at = 0
            for top, end in blocks:
                for j in range(top, end):
                    v_j = v[hh][j:j + 1, :]
                    for g in range(j // 8, end // 8):
                        groups[g] = groups[g] + sums[hh][at:at + 8, :] * v_j
                        at += 8
            for f, (top, end) in zip(far[hh], blocks[1:]):
                for g in range(top // 8, end // 8):
                    groups[g] = groups[g] + f[(g * 8 - top):(g * 8 - top + 8), :]
            o = jnp.concatenate(groups, axis=0) + o_st[hh]
            st_ref[hh] = st[hh] * jnp.exp(gl[hh]) + kv[hh]
            o_ref[rows, hs] = (_rms(o, gnorm) * _silu(gate_ref[rows, hs])).astype(o_ref.dtype)
        return carry

    lax.fori_loop(0, ts // c, chunk_loop, 0)


def _hgrn2(p32, lb, d_norm_g, *, ts, cols):
    bsz, s, _ = p32.shape
    d = HEAD_DIM
    nh = N_HEADS
    w = nh * d
    kernel = functools.partial(_hgrn2_kernel, ts=ts)
    tile = lambda name: pl.BlockSpec((None, ts, w), lambda b, i: (b, i, cols[name] // nh))
    return pl.pallas_call(
        kernel,
        grid=(bsz, s // ts),
        in_specs=[tile("qd"), tile("fd"), tile("id"), tile("gd"),
                  pl.BlockSpec((1, w), lambda b, i: (0, 0)),
                  pl.BlockSpec((1, d), lambda b, i: (0, 0))],
        out_specs=pl.BlockSpec((None, ts, w), lambda b, i: (b, i, 0)),
        out_shape=jax.ShapeDtypeStruct((bsz, s, w), BF16),
        scratch_shapes=[pltpu.VMEM((ts, w), F32), pltpu.VMEM((ts, w), F32),
                        pltpu.VMEM((ts, w), F32), pltpu.VMEM((nh, d, d), F32)],
        compiler_params=pltpu.CompilerParams(
            dimension_semantics=("parallel", "arbitrary"), vmem_limit_bytes=VMEM_LIMIT),
        name="hgrn2",
    )(p32, p32, p32, p32, lb.astype(F32).reshape(1, w), d_norm_g.astype(F32).reshape(1, d))


def _stickbreak_kernel(q_ref, k_ref, v_ref, o_ref, acc_ref, *, tq):
    i = pl.program_id(1)
    d = HEAD_DIM
    nh = N_HEADS
    row = _iota((tq, tq), 0)
    col = _iota((tq, tq), 1)
    causal = col < row
    later = (row > col).astype(BF16)
    later2 = jnp.concatenate([later, later], axis=0)

    heads = [slice(hh * d, (hh + 1) * d) for hh in range(nh)]

    def scores(blocks):
        jobs = [(j, dg, hs) for j, dg in blocks for hs in heads]
        z = [_mm_nt(q_ref[:, hs], k_ref[pl.ds(pl.multiple_of(j * tq, tq), tq), hs]) * (d ** -0.5)
             for j, _, hs in jobs]
        sp = [_softplus(x) for x in z]
        l1m = [jnp.where(causal, -x, 0.0) if dg else -x for x, (_, dg, _) in zip(sp, jobs)]
        rest = [jnp.dot(jnp.concatenate(_split(x), axis=1), later2, preferred_element_type=F32)
                for x in l1m]
        out = [((a - b) + r, l) for a, b, r, l in zip(z, sp, rest, l1m)]
        return [out[b * nh:(b + 1) * nh] for b in range(len(blocks))]

    def block(j, carries):
        (sc,) = scores([(j, False)])
        ps = [jnp.exp(logw + c) for (logw, _), c in zip(sc, carries)]
        pv = [_mm(p, v_ref[pl.ds(pl.multiple_of(j * tq, tq), tq), hs]) for p, hs in zip(ps, heads)]
        for hs, x in zip(heads, pv):
            acc_ref[:, hs] += x
        return tuple(c + jnp.sum(l1m, axis=-1, keepdims=True) for (_, l1m), c in zip(sc, carries))

    j1 = jnp.maximum(i - 1, 0)
    j2 = jnp.maximum(i - 2, 0)
    live1 = jnp.where(i > 0, 1.0, 0.0)
    live2 = jnp.where(i > 1, 1.0, 0.0)
    s0, s1, s2 = scores([(i, True), (j1, False), (j2, False)])
    carries = []
    for hh, hs in enumerate(heads):
        c0 = jnp.sum(s0[hh][1], axis=-1, keepdims=True)
        c1 = c0 + jnp.sum(s1[hh][1], axis=-1, keepdims=True)
        p0 = jnp.where(causal, jnp.exp(s0[hh][0]), 0.0)
        p1 = jnp.exp(s1[hh][0] + c0) * live1
        p2 = jnp.exp(s2[hh][0] + c1) * live2
        acc_ref[:, hs] = (_mm(p0, v_ref[pl.ds(pl.multiple_of(i * tq, tq), tq), hs])
                          + _mm(p1, v_ref[pl.ds(pl.multiple_of(j1 * tq, tq), tq), hs])
                          + _mm(p2, v_ref[pl.ds(pl.multiple_of(j2 * tq, tq), tq), hs]))
        carries.append(c1 + jnp.sum(s2[hh][1], axis=-1, keepdims=True))
    carries = tuple(carries)

    def cond(c):
        worst = functools.reduce(jnp.maximum, c[1])
        return jnp.logical_and(c[0] >= 0, jnp.max(worst) >= EXP_ZERO_BELOW)

    def body(c):
        return c[0] - 1, block(c[0], c[1])

    lax.while_loop(cond, body, (i - 3, carries))
    o_ref[...] = acc_ref[...].astype(o_ref.dtype)


def _stickbreak(p16, *, tq, cols):
    bsz, s, _ = p16.shape
    nh = N_HEADS
    w = nh * HEAD_DIM
    kernel = functools.partial(_stickbreak_kernel, tq=tq)
    resident = dict(pipeline_mode=pl.Buffered(1))
    return pl.pallas_call(
        kernel,
        grid=(bsz, s // tq),
        in_specs=[pl.BlockSpec((None, tq, w), lambda b, i: (b, i, cols["qc"] // nh)),
                  pl.BlockSpec((None, s, w), lambda b, i: (b, 0, cols["kc"] // nh), **resident),
                  pl.BlockSpec((None, s, w), lambda b, i: (b, 0, cols["vc"] // nh), **resident)],
        out_specs=pl.BlockSpec((None, tq, w), lambda b, i: (b, i, 0)),
        out_shape=jax.ShapeDtypeStruct((bsz, s, w), BF16),
        scratch_shapes=[pltpu.VMEM((tq, w), F32)],
        compiler_params=pltpu.CompilerParams(
            dimension_semantics=("parallel", "arbitrary"), vmem_limit_bytes=VMEM_LIMIT),
        name="stickbreak",
    )(p16, p16, p16)


def _dsa_kernel(qi_ref, smq_ref, q_ref, sm_ref, k_ref, vt_ref, bias_ref, o_ref,
                sc_ref, scb_ref, qct_ref, kc_ref, bd_ref, lg_ref, *, tq, k_sel, wi_lane, wide):
    i = pl.program_id(1)
    tk = tq
    d = HEAD_DIM
    nh = N_HEADS
    ksel = float(k_sel)
    per_wide = wide // tk
    n_wide = (i + per_wide) // per_wide
    sub = 2 * tk
    lane_q = _iota((1, tq), 1)

    def tree(parts, op):
        while len(parts) > 1:
            parts = [op(parts[j], parts[j + 1]) if j + 1 < len(parts) else parts[j]
                     for j in range(0, len(parts), 2)]
        return parts[0]

    def col_fold(x, op=jnp.add, rows=8):
        return tree([x[r * rows:(r + 1) * rows] for r in range(x.shape[0] // rows)], op)

    @pl.when(i == 0)
    def _():
        def prep(g, carry):
            g0 = pl.multiple_of(g * wide, wide)
            hi, lo = _split(sm_ref[pl.ds(g0, wide), :][:, :IDX_DIM])
            kc_ref[pl.ds(g0, wide), :] = jnp.concatenate([hi, lo, hi], axis=1)
            return carry
        lax.fori_loop(0, sm_ref.shape[0] // wide, prep, 0)

    qit = qi_ref[...].T
    for p in range(IDX_HEADS // 2):
        halves = []
        for hh in (2 * p, 2 * p + 1):
            hi, lo = _split(qit[hh * IDX_DIM:(hh + 1) * IDX_DIM, :])
            halves.append(jnp.concatenate([hi, hi, lo], axis=0))
        qct_ref[p] = jnp.concatenate(halves, axis=1)
    w_rows = smq_ref[...].T[wi_lane:wi_lane + IDX_HEADS, :] * ((IDX_HEADS ** -0.5) * (IDX_DIM ** -0.5))

    q2t = (q_ref[...] * ((d ** -0.5) * LOG2E)).T.astype(BF16)
    zero_dq = jnp.zeros((d, tq), BF16)
    for p in range(nh // 2):
        top = jnp.concatenate([q2t[2 * p * d:(2 * p + 1) * d], zero_dq], axis=1)
        bot = jnp.concatenate([zero_dq, q2t[(2 * p + 1) * d:(2 * p + 2) * d]], axis=1)
        bd_ref[p] = jnp.concatenate([top, bot], axis=0)

    limit = i * tq + (lane_q // CHUNK + 1) * CHUNK

    rows_s = _iota((sub, tq), 0)

    def score_groups(gs, mm, masked):
        mn, mx = mm
        k0s = [pl.multiple_of(g * wide + sb * sub, sub) for g in gs for sb in range(wide // sub)]
        keys = [kc_ref[pl.ds(k0, sub), :] for k0 in k0s]
        accs = [jnp.zeros((sub, tq), F32) for _ in k0s]
        for p in range(IDX_HEADS // 2):
            rhs = qct_ref[p]
            for n, kk in enumerate(keys):
                s2 = jnp.dot(kk, rhs, preferred_element_type=F32)
                accs[n] = (accs[n] + jnp.maximum(s2[:, :tq], 0.0) * w_rows[2 * p:2 * p + 1, :]
                           + jnp.maximum(s2[:, tq:], 0.0) * w_rows[2 * p + 1:2 * p + 2, :])
        for k0, sct in zip(k0s, accs):
            if masked:
                adm = (k0 + rows_s) < limit
                mn = jnp.minimum(mn, col_fold(jnp.where(adm, sct, jnp.inf), jnp.minimum))
                sct = jnp.where(adm, sct, -jnp.inf)
            else:
                mn = jnp.minimum(mn, col_fold(sct, jnp.minimum))
            mx = jnp.maximum(mx, col_fold(sct, jnp.maximum))
            sc_ref[pl.ds(k0, sub), :] = sct
            scb_ref[pl.ds(k0, sub), :] = _floor_bf16(sct)
        return mn, mx

    def score_pair(j, mm):
        return score_groups((2 * j, 2 * j + 1), mm, False)

    n_full = n_wide - 1
    mm = lax.fori_loop(0, n_full // 2, score_pair,
                       (jnp.full((8, tq), jnp.inf, F32), jnp.full((8, tq), -jnp.inf, F32)))
    mm = lax.cond(n_full % 2 == 1, lambda c: score_groups((n_full - 1,), c, False), lambda c: c, mm)
    mn, mx = score_groups((n_wide - 1,), mm, True)

    n_pairs = (n_wide + 1) // 2

    @pl.when(n_wide % 2 == 1)
    def _():
        sc_ref[pl.ds(pl.multiple_of(n_wide * wide, wide), wide), :] = jnp.full((wide, tq), -jnp.inf, F32)
        scb_ref[pl.ds(pl.multiple_of(n_wide * wide, wide), wide), :] = jnp.full((wide, tq), -jnp.inf, BF16)
    rmin = jnp.min(mn, axis=0, keepdims=True)
    rmax = jnp.max(mx, axis=0, keepdims=True)

    def count(pred):
        def body(j, acc):
            for g in (2 * j, 2 * j + 1):
                acc = acc + col_fold(pred(sc_ref[pl.ds(pl.multiple_of(g * wide, wide), wide), :]))
            return acc
        return jnp.sum(lax.fori_loop(0, n_pairs, body, jnp.zeros((8, tq), F32)), axis=0, keepdims=True)

    def max_below(x):
        def body(j, acc):
            for g in (2 * j, 2 * j + 1):
                blk = sc_ref[pl.ds(pl.multiple_of(g * wide, wide), wide), :]
                acc = jnp.maximum(acc, col_fold(jnp.where(blk < x, blk, -jnp.inf), jnp.maximum))
            return acc
        return jnp.max(lax.fori_loop(0, n_pairs, body, jnp.full((8, tq), -jnp.inf, F32)), axis=0, keepdims=True)

    n_adm = limit.astype(F32)
    all_sel = n_adm <= ksel

    def bisect(c):
        lo, hi, c_lo = c
        mid = 0.5 * lo + 0.5 * hi
        cm = count(lambda blk: _ind(blk >= mid))
        ge = cm >= ksel
        return jnp.where(ge, mid, lo), jnp.where(ge, hi, mid), jnp.where(ge, cm, c_lo)

    def pending(c_lo, tied):
        return jnp.where(all_sel, 0.0, jnp.where(tied > 0.5, 0.0, _ind(c_lo != ksel)))

    def bisect_coarse(_, c):
        lo, hi, c_lo = c
        mid = _floor_bf16(0.5 * lo + 0.5 * hi).astype(F32)
        t_b = jnp.broadcast_to(mid, (16, tq)).astype(BF16)
        one_b = jnp.ones((16, tq), BF16)
        zero_b = jnp.zeros((16, tq), BF16)

        def body(j, acc):
            for g in (2 * j, 2 * j + 1):
                blk = scb_ref[pl.ds(pl.multiple_of(g * wide, wide), wide), :]
                ind = [jnp.where(blk[r * 16:(r + 1) * 16] >= t_b, one_b, zero_b) for r in range(wide // 16)]
                acc = acc + tree(ind, jnp.add).astype(F32)
            return acc

        acc = lax.fori_loop(0, n_pairs, body, jnp.zeros((16, tq), F32))
        cm = jnp.sum(acc, axis=0, keepdims=True)
        ge = cm >= ksel
        return jnp.where(ge, mid, lo), jnp.where(ge, hi, mid), jnp.where(ge, cm, c_lo)

    lo0 = _floor_bf16(rmin).astype(F32)
    hi0 = _floor_bf16(rmax + (jnp.abs(rmax) * (2.0 ** -6) + 1e-30)).astype(F32)
    state = lax.fori_loop(0, BISECT_COARSE, bisect_coarse, (lo0, hi0, n_adm))
    state = lax.fori_loop(0, BISECT_FIXED, lambda _, c: bisect(c), state)

    def round_cond(c):
        return jnp.max(pending(c[0][2], c[1])) > 0.5

    def round_body(c):
        st, tied, v, need = c

        def more_cond(s):
            return jnp.logical_and(s[0] < BISECT_EXTRA, jnp.max(pending(s[1][2], tied)) > 0.5)

        _, st = lax.while_loop(more_cond, lambda s: (s[0] + 1, bisect(s[1])), (jnp.int32(0), st))
        pend = pending(st[2], tied)

        def check(_):
            cand = max_below(st[1])
            c_ge = count(lambda blk: _ind(blk >= cand))
            c_gt = count(lambda blk: _ind(blk > cand))
            ok = jnp.where(pend > 0.5, _ind(c_ge >= ksel), 0.0)
            return (jnp.where(ok > 0.5, 1.0, tied), jnp.where(ok > 0.5, cand, v),
                    jnp.where(ok > 0.5, ksel - c_gt, need))

        tied, v, need = lax.cond(jnp.max(pend) > 0.5, check, lambda _: (tied, v, need), 0)
        return st, tied, v, need

    zeros1 = jnp.zeros((1, tq), F32)
    (lo_f, _, _), tied, v_tie, need = lax.while_loop(round_cond, round_body, (state, zeros1, zeros1, zeros1))
    vth = jnp.where(all_sel, F32_LOWEST, jnp.where(tied > 0.5, v_tie, lo_f))

    @pl.when(jnp.max(tied) > 0.5)
    def _():
        v_eq = jnp.where(tied > 0.5, v_tie, jnp.inf)
        incl = (_iota((tk, tk), 1) <= _iota((tk, tk), 0)).astype(BF16)

        def demote(g, seen):
            g0 = pl.multiple_of(g * wide, wide)
            xs = [sc_ref[pl.ds(g0 + pb * tk, tk), :] for pb in range(per_wide)]
            eqs = [_ind(x == v_eq) for x in xs]
            inblk = [jnp.dot(incl, e.astype(BF16), preferred_element_type=F32) for e in eqs]
            for pb in range(per_wide):
                rank = inblk[pb] + seen
                sc_ref[pl.ds(g0 + pb * tk, tk), :] = jnp.where(eqs[pb] * _ind(rank > need) > 0.5,
                                                               -jnp.inf, xs[pb])
                seen = seen + jnp.sum(col_fold(eqs[pb]), axis=0, keepdims=True)
            return seen

        lax.fori_loop(0, n_wide, demote, zeros1)

    g_near = jnp.maximum(i - 1, 0) // per_wide

    def logit_group(g, mx, near):
        out = list(mx)
        for sb in range(wide // sub):
            k0 = pl.multiple_of(g * wide + sb * sub, sub)
            sel = sc_ref[pl.ds(k0, sub), :] >= vth
            for p in range(nh // 2):
                pair = jnp.dot(k_ref[pl.ds(k0, sub), 2 * p * d:(2 * p + 2) * d], bd_ref[p],
                               preferred_element_type=F32)
                for hh in (2 * p, 2 * p + 1):
                    lm = pair[:, (hh - 2 * p) * tq:(hh - 2 * p + 1) * tq]
                    if near:
                        back = [jnp.clip(i - (g * per_wide + sb * (sub // tk) + pb), 0, 2)
                                for pb in range(sub // tk)]
                        lm = lm + jnp.concatenate([bias_ref[bk, hh] for bk in back], axis=0)
                    lm = jnp.where(sel, lm, NEG_BIG)
                    lg_ref[hh, pl.ds(k0, sub), :] = lm
                    out[hh] = jnp.maximum(out[hh], col_fold(lm, jnp.maximum))
        return tuple(out)

    mx = tuple(jnp.full((8, tq), NEG_BIG, F32) for _ in range(nh))
    def logit_pair(j, mx, near):
        return logit_group(2 * j + 1, logit_group(2 * j, mx, near), near)

    far_pairs = g_near // 2
    mx = lax.fori_loop(0, far_pairs, functools.partial(logit_pair, near=False), mx)
    mx = lax.fori_loop(far_pairs, n_pairs, functools.partial(logit_pair, near=True), mx)
    m_q = [jnp.max(mx[hh], axis=0, keepdims=True) for hh in range(nh)]

    ones_rows = jnp.ones((8, wide), BF16)

    def pv_pair(j, carry):
        ls, accs = list(carry[0]), list(carry[1])
        jobs = [(pl.multiple_of(g * wide, wide), hh) for g in (2 * j, 2 * j + 1) for hh in range(nh)]
        ps = [jnp.exp2(lg_ref[hh, pl.ds(g0, wide), :] - m_q[hh]).astype(BF16) for g0, hh in jobs]
        outs = [jnp.dot(jnp.concatenate([vt_ref[hh * d:(hh + 1) * d, pl.ds(g0, wide)], ones_rows], axis=0),
                        p, preferred_element_type=F32) for (g0, hh), p in zip(jobs, ps)]
        for (_, hh), out in zip(jobs, outs):
            ls[hh] = ls[hh] + out[d:]
            accs[hh] = accs[hh] + out[:d]
        return tuple(ls), tuple(accs)

    ls, accs = lax.fori_loop(0, n_pairs, pv_pair,
                             (tuple(jnp.zeros((8, tq), F32) for _ in range(nh)),
                              tuple(jnp.zeros((d, tq), F32) for _ in range(nh))))
    for hh in range(nh):
        o_ref[:, hh * d:(hh + 1) * d] = (accs[hh] / ls[hh][0:1]).T.astype(o_ref.dtype)


def _dsa(p32, p16, vt, bias_tiles, *, tq, cols):
    bsz, s, _ = p32.shape
    d = HEAD_DIM
    nh = N_HEADS
    wide = 4 * tq
    k_sel = min(TOPK_MAX, s // 4)
    w512 = nh * d
    kernel = functools.partial(_dsa_kernel, tq=tq, k_sel=k_sel, wi_lane=cols["wi_lane"], wide=wide)
    resident = dict(pipeline_mode=pl.Buffered(1))
    return pl.pallas_call(
        kernel,
        grid=(bsz, s // tq),
        in_specs=[pl.BlockSpec((None, tq, w512), lambda b, i: (b, i, cols["qi"] // nh)),
                  pl.BlockSpec((None, tq, d), lambda b, i: (b, i, cols["small"])),
                  pl.BlockSpec((None, tq, w512), lambda b, i: (b, i, cols["qb"] // nh)),
                  pl.BlockSpec((None, s, d), lambda b, i: (b, 0, cols["small"]), **resident),
                  pl.BlockSpec((None, s, w512), lambda b, i: (b, 0, cols["kb"] // nh), **resident),
                  pl.BlockSpec((w512, s), lambda b, i: (0, b), **resident),
                  pl.BlockSpec((3, nh, tq, tq), lambda b, i: (0, 0, 0, 0), **resident)],
        out_specs=pl.BlockSpec((None, tq, w512), lambda b, i: (b, i, 0)),
        out_shape=jax.ShapeDtypeStruct((bsz, s, w512), BF16),
        scratch_shapes=[pltpu.VMEM((s, tq), F32),
                        pltpu.VMEM((s, tq), BF16),
                        pltpu.VMEM((IDX_HEADS // 2, 3 * IDX_DIM, 2 * tq), BF16),
                        pltpu.VMEM((s, 3 * IDX_DIM), BF16),
                        pltpu.VMEM((nh // 2, 2 * d, 2 * tq), BF16),
                        pltpu.VMEM((nh, s, tq), F32)],
        compiler_params=pltpu.CompilerParams(
            dimension_semantics=("parallel", "arbitrary"), vmem_limit_bytes=VMEM_LIMIT),
        name="dsa",
    )(p32, p32, p32, p32, p16, vt, bias_tiles)


def _t5_bucket(rel):
    nb = REL_BUCKETS // 2
    max_exact = nb // 2
    ret = jnp.where(rel > 0, nb, 0)
    n = jnp.abs(rel)
    large = max_exact + (jnp.log(jnp.maximum(n, 1).astype(F32) / max_exact)
                         / math.log(REL_MAX_DIST / max_exact) * (nb - max_exact)).astype(jnp.int32)
    large = jnp.minimum(large, nb - 1)
    return ret + jnp.where(n < max_exact, n, large)


def _bias_tiles(rel_table, tq):
    assert tq >= REL_MAX_DIST
    t = jnp.arange(tq)
    back = jnp.arange(3)
    rel = (t[None, None, :] - back[:, None, None] * tq) - t[None, :, None]
    onehot = (_t5_bucket(rel)[..., None] == jnp.arange(REL_BUCKETS)).astype(F32)
    tiles = jnp.einsum("bqkn,nh->bhkq", onehot, rel_table.astype(F32),
                       precision=HIGHEST)
    return (tiles - tiles[2:3]) * LOG2E


def _even_layout(w_in):
    d = HEAD_DIM
    a_w = 2 * N_HEADS * d + N_HEADS * d
    offs = {}
    o = 0
    for name, w in (("qkv", a_w), ("z", N_HEADS * d), ("a", N_HEADS), ("b", N_HEADS),
                    ("qb", N_HEADS * d), ("kb", N_HEADS * d), ("vb", N_HEADS * d),
                    ("qi", IDX_HEADS * IDX_DIM), ("ki", IDX_DIM), ("wi", IDX_HEADS)):
        offs[name] = (o, o + w)
        o += w
    assert o == w_in.shape[1]
    sl = lambda n: w_in[:, offs[n][0]:offs[n][1]]
    small_w = IDX_DIM + 2 * N_HEADS + IDX_HEADS
    small_pad = -small_w % d
    zeros = lambda n: jnp.zeros((w_in.shape[0], n), w_in.dtype)
    w32 = jnp.concatenate([sl("qkv"), sl("z"), sl("qb"), sl("qi"),
                           sl("ki"), sl("a"), sl("b"), sl("wi"), zeros(small_pad)], axis=1)
    n32 = w32.shape[1]
    tn = n32 // 5
    assert tn * 5 == n32 and tn % d == 0
    w16 = jnp.concatenate([sl("kb"), zeros(tn - N_HEADS * d)], axis=1)
    nh = N_HEADS
    cols = dict(qa=0, ka=nh, va=2 * nh, za=3 * nh, qb=4 * nh, qi=5 * nh, small=6 * nh, kb=0,
                a_lane=IDX_DIM, b_lane=IDX_DIM + nh, wi_lane=IDX_DIM + 2 * nh, n32=n32, tn=tn)
    return jnp.concatenate([w32, w16], axis=1).astype(BF16), sl("vb").T.astype(BF16), cols


def kernel(x, norm_g, w_in_even, conv_w_even, a_log_even, dt_bias_even, a_norm_even, w_out_even,
           rel_bias, w_in_odd, lb_logits, d_norm_odd, w_out_odd, w_gate, w_up, w_down):
    bsz, s, d = x.shape
    t = bsz * s
    depth = norm_g.shape[0]
    nh = N_HEADS
    tq = Q_TILE
    lb_all = jnp.cumsum(jax.nn.softmax(lb_logits.astype(F32), axis=0), axis=0)
    lb_all = lb_all - lb_all[:1]
    odd_cols = dict(qc=0, kc=nh, vc=2 * nh, qd=0, fd=nh, id=2 * nh, gd=3 * nh)
    bias_tiles = _bias_tiles(rel_bias, tq)

    h = x.reshape(t, d)
    for l in range(depth):
        if l % 2 == 0:
            e = l // 2
            w_even, w_vt, cols = _even_layout(w_in_even[e])
            p32, p16, vt = _norm_matmul(h, norm_g[l, 0], w_even, tm=PROJ_TILE, tn=cols["tn"], n32=cols["n32"],
                                        w_t=w_vt)
            p32 = p32.reshape(bsz, s, -1)
            p16 = p16.reshape(bsz, s, -1)
            o_1 = _deltanet(p32, conv_w_even[e], a_log_even[e], dt_bias_even[e], a_norm_even[e],
                            ts=min(SEQ_TILE, s), cols=cols)
            o_2 = _dsa(p32, p16, vt, bias_tiles, tq=tq, cols=cols)
            w_out = w_out_even[e]
        else:
            o = l // 2
            n16 = 3 * nh * HEAD_DIM
            w_odd = jnp.concatenate([w_in_odd[o][:, n16:], w_in_odd[o][:, :n16]], axis=1).astype(BF16)
            p32, p16 = _norm_matmul(h, norm_g[l, 0], w_odd, tm=PROJ_TILE, tn=ODD_COL_TILE, n32=w_odd.shape[1] - n16)
            p32 = p32.reshape(bsz, s, -1)
            p16 = p16.reshape(bsz, s, -1)
            o_1 = _stickbreak(p16, tq=tq, cols=odd_cols)
            o_2 = _hgrn2(p32, lb_all[l], d_norm_odd[o], ts=min(SEQ_TILE, s), cols=odd_cols)
            w_out = w_out_odd[o]
        h = _mix_ffn(o_1.reshape(t, -1), o_2.reshape(t, -1), w_out, h, norm_g[l, 1], norm_g[l, 2], norm_g[l, 3],
                     w_gate[l], w_up[l], w_down[l], tm=ROW_TILE, tf=FFN_TILE)
    return h.reshape(bsz, s, d)
```

```python
import functools
import math

import jax
import jax.numpy as jnp
from jax import lax
from jax.experimental import pallas as pl
from jax.experimental.pallas import tpu as pltpu

F32 = jnp.float32
BF16 = jnp.bfloat16
HIGHEST = lax.Precision.HIGHEST

CHUNK = 64
HEAD_DIM = 128
N_HEADS = 4
IDX_HEADS = 8
IDX_DIM = 64
TOPK_MAX = 256
CONV_WIDTH = 4
REL_BUCKETS = 32
REL_MAX_DIST = 128
EPS = 1e-6
NEG_BIG = -1e30
LOG2E = 1.4426950408889634
BISECT_COARSE = 12
BISECT_FIXED = 8
BISECT_EXTRA = 6
F32_LOWEST = -3.4028234663852886e38
EXP_ZERO_BELOW = -104.0
VMEM_LIMIT = 56 * 1024 * 1024

PROJ_TILE = 2048
ROW_TILE = 512
DELTANET_TILE = 1024
SEQ_TILE = 512
Q_TILE = 128
ODD_COL_TILE = 512
FFN_TILE = 2816


def _mm(a, b):
    return jnp.dot(a.astype(BF16), b.astype(BF16), preferred_element_type=F32)


def _mm_nt(a, b):
    return lax.dot_general(a.astype(BF16), b.astype(BF16), (((1,), (1,)), ((), ())),
                           preferred_element_type=F32)


def _mm_tn(a, b):
    return lax.dot_general(a.astype(BF16), b.astype(BF16), (((0,), (0,)), ((), ())),
                           preferred_element_type=F32)


def _split(x):
    hi = x.astype(BF16)
    return hi, (x - hi.astype(F32)).astype(BF16)


def _floor_bf16(x):
    bits = pltpu.bitcast(x, jnp.int32)
    down = jnp.where(bits >= 0, bits, bits + 0xFFFF) & jnp.int32(-65536)
    return pltpu.bitcast(down, F32).astype(BF16)


def _sigmoid(x):
    return 1.0 / (1.0 + jnp.exp(-x))


def _silu(x):
    return x * _sigmoid(x)


def _softplus(x):
    return jnp.maximum(x, 0.0) + jnp.log1p(jnp.exp(-jnp.abs(x)))


def _rms(x, g):
    return x * lax.rsqrt(jnp.mean(x * x, axis=-1, keepdims=True) + EPS) * g


def _iota(shape, dim):
    return lax.broadcasted_iota(jnp.int32, shape, dim)


def _ind(mask):
    return jnp.where(mask, 1.0, 0.0)


def _norm_matmul_kernel(x_ref, g_ref, w_ref, *rest, n_t, tiles32):
    if n_t:
        wt_ref, o32_ref, o16_ref, ot_ref, xn_ref = rest
    else:
        o32_ref, o16_ref, xn_ref = rest
    j = pl.program_id(1)

    @pl.when(j == 0)
    def _():
        xn_ref[...] = _rms(x_ref[...], g_ref[...]).astype(BF16)
        if n_t:
            ot_ref[...] = lax.dot_general(wt_ref[...], xn_ref[...], (((1,), (1,)), ((), ())),
                                          preferred_element_type=F32).astype(BF16)

    y = jnp.dot(xn_ref[...], w_ref[...], preferred_element_type=F32)

    @pl.when(j < tiles32)
    def _():
        o32_ref[...] = y

    @pl.when(j >= tiles32)
    def _():
        o16_ref[...] = y.astype(BF16)


def _norm_matmul(x, g, w, *, tm, tn, n32, w_t=None):
    t, d = x.shape
    n = w.shape[1]
    n_t = 0 if w_t is None else w_t.shape[0]
    tiles32 = n32 // tn
    assert tiles32 * tn == n32 and (n - n32) % tn == 0 and 0 < n32 < n
    in_specs = [pl.BlockSpec((tm, d), lambda i, j: (i, 0)),
                pl.BlockSpec((1, d), lambda i, j: (0, 0)),
                pl.BlockSpec((d, tn), lambda i, j: (0, j))]
    out_specs = [pl.BlockSpec((tm, tn), lambda i, j: (i, jnp.minimum(j, tiles32 - 1))),
                 pl.BlockSpec((tm, tn), lambda i, j: (i, jnp.maximum(j - tiles32, 0)))]
    out_shape = [jax.ShapeDtypeStruct((t, n32), F32), jax.ShapeDtypeStruct((t, n - n32), BF16)]
    args = [x, g.reshape(1, d), w]
    if n_t:
        in_specs.append(pl.BlockSpec((n_t, d), lambda i, j: (0, 0)))
        out_specs.append(pl.BlockSpec((n_t, tm), lambda i, j: (0, i)))
        out_shape.append(jax.ShapeDtypeStruct((n_t, t), BF16))
        args.append(w_t)
    return pl.pallas_call(
        functools.partial(_norm_matmul_kernel, n_t=n_t, tiles32=tiles32),
        grid=(t // tm, n // tn),
        in_specs=in_specs,
        out_specs=out_specs,
        out_shape=out_shape,
        scratch_shapes=[pltpu.VMEM((tm, d), BF16)],
        compiler_params=pltpu.CompilerParams(
            dimension_semantics=("parallel", "arbitrary"), vmem_limit_bytes=VMEM_LIMIT),
        name="norm_matmul",
    )(*args)


def _mix_ffn_kernel(ca_ref, cb_ref, wa_ref, wb_ref, h_ref, gmix_ref, gpre_ref, gpost_ref,
                    wg_ref, wu_ref, wd_ref, o_ref, h1_ref, xn_ref, acc_ref):
    f = pl.program_id(1)

    @pl.when(f == 0)
    def _():
        y = (jnp.dot(ca_ref[...], wa_ref[...], preferred_element_type=F32)
             + jnp.dot(cb_ref[...], wb_ref[...], preferred_element_type=F32))
        h1 = h_ref[...] + _rms(y, gmix_ref[...])
        h1_ref[...] = h1
        xn_ref[...] = _rms(h1, gpre_ref[...]).astype(BF16)
        acc_ref[...] = jnp.zeros_like(acc_ref)

    xn = xn_ref[...]
    gate = jnp.dot(xn, wg_ref[...], preferred_element_type=F32)
    up = jnp.dot(xn, wu_ref[...], preferred_element_type=F32)
    act = (_silu(gate) * up).astype(BF16)
    acc_ref[...] += jnp.dot(act, wd_ref[...], preferred_element_type=F32)

    @pl.when(f == pl.num_programs(1) - 1)
    def _():
        o_ref[...] = h1_ref[...] + _rms(acc_ref[...], gpost_ref[...])


def _mix_ffn(ca, cb, w_out, h, g_mix, g_pre, g_post, wg, wu, wd, *, tm, tf):
    t, d = h.shape
    ff = wg.shape[1]
    wa_n = ca.shape[1]
    wb_n = cb.shape[1]
    row = pl.BlockSpec((1, d), lambda i, f: (0, 0))
    once = dict(pipeline_mode=pl.Buffered(1)) if tf == ff else {}
    return pl.pallas_call(
        _mix_ffn_kernel,
        grid=(t // tm, ff // tf),
        in_specs=[pl.BlockSpec((tm, wa_n), lambda i, f: (i, 0)),
                  pl.BlockSpec((tm, wb_n), lambda i, f: (i, 0)),
                  pl.BlockSpec((wa_n, d), lambda i, f: (0, 0)),
                  pl.BlockSpec((wb_n, d), lambda i, f: (0, 0)),
                  pl.BlockSpec((tm, d), lambda i, f: (i, 0)),
                  row, row, row,
                  pl.BlockSpec((d, tf), lambda i, f: (0, f), **once),
                  pl.BlockSpec((d, tf), lambda i, f: (0, f), **once),
                  pl.BlockSpec((tf, d), lambda i, f: (f, 0), **once)],
        out_specs=pl.BlockSpec((tm, d), lambda i, f: (i, 0)),
        out_shape=jax.ShapeDtypeStruct((t, d), F32),
        scratch_shapes=[pltpu.VMEM((tm, d), F32), pltpu.VMEM((tm, d), BF16), pltpu.VMEM((tm, d), F32)],
        compiler_params=pltpu.CompilerParams(
            dimension_semantics=("parallel", "arbitrary"), vmem_limit_bytes=VMEM_LIMIT),
        name="mix_ffn",
    )(ca, cb, w_out[:wa_n].astype(BF16), w_out[wa_n:].astype(BF16), h,
      g_mix.reshape(1, d), g_pre.reshape(1, d), g_post.reshape(1, d),
      wg.astype(BF16), wu.astype(BF16), wd.astype(BF16))


def _deltanet_kernel(xq_ref, xk_ref, xv_ref, z_ref, sm_ref, cwq_ref, cwk_ref, cwv_ref,
                     alog_ref, dtb_ref, gn_ref, o_ref,
                     xpad_ref, q_ref, k_ref, v_ref, gb_ref, bb_ref, u_ref, w_ref, qk_ref, st_ref,
                     *, ts, a_col, b_col):
    s = pl.program_id(1)
    c = CHUNK
    d = HEAD_DIM
    nh = N_HEADS

    @pl.when(s == 0)
    def _():
        xpad_ref[:, 0:8, :] = jnp.zeros((3, 8, nh * d), F32)
        st_ref[...] = jnp.zeros_like(st_ref)

    @pl.when(s != 0)
    def _():
        xpad_ref[:, 0:8, :] = xpad_ref[:, ts:ts + 8, :]

    xpad_ref[0, 8:ts + 8, :] = xq_ref[...]
    xpad_ref[1, 8:ts + 8, :] = xk_ref[...]
    xpad_ref[2, 8:ts + 8, :] = xv_ref[...]

    def conv_silu(idx, cw_ref, hs):
        cw = cw_ref[:, hs]
        acc = xpad_ref[idx, 8 - (CONV_WIDTH - 1):8 - (CONV_WIDTH - 1) + ts, hs] * cw[0:1, :]
        for j in range(1, CONV_WIDTH):
            off = 8 - (CONV_WIDTH - 1) + j
            acc = acc + xpad_ref[idx, off:off + ts, hs] * cw[j:j + 1, :]
        return _silu(acc)

    def l2norm(t):
        return t * lax.rsqrt(jnp.sum(t * t, axis=-1, keepdims=True) + EPS)

    row = _iota((c, c), 0)
    col = _iota((c, c), 1)
    tri = (col <= row)
    strict = (col < row)
    tri_f = tri.astype(F32)
    upper_f = (row <= col).astype(F32)
    eye = (row == col).astype(F32)
    gnorm = gn_ref[...]
    chunks = range(ts // c)
    rs = [slice(ci * c, (ci + 1) * c) for ci in chunks]
    tri2 = jnp.concatenate([tri_f, tri_f], axis=1).astype(BF16)
    ones2 = jnp.ones((c, 2 * c), BF16)

    def cum2(lhs2, x):
        hi, lo = _split(x)
        return jnp.dot(lhs2, jnp.concatenate([hi, lo], axis=0), preferred_element_type=F32)

    for hh in range(nh):
        hs = slice(hh * d, (hh + 1) * d)
        q_ref[:, hs] = l2norm(conv_silu(0, cwq_ref, hs)) * (d ** -0.5)
        k_ref[:, hs] = l2norm(conv_silu(1, cwk_ref, hs))
        v_ref[:, hs] = conv_silu(2, cwv_ref, hs)

        a_raw = sm_ref[:, a_col + hh:a_col + hh + 1]
        b_raw = sm_ref[:, b_col + hh:b_col + hh + 1]
        g = -jnp.exp(alog_ref[:, hh:hh + 1]) * _softplus(a_raw + dtb_ref[:, hh:hh + 1])
        gb_ref[:, hs] = jnp.broadcast_to(g, (ts, d))
        bb_ref[:, hs] = jnp.broadcast_to(_sigmoid(b_raw), (ts, d))

        q = [q_ref[r, hs] for r in rs]
        k = [k_ref[r, hs] for r in rs]
        beta = [bb_ref[r, hs] for r in rs]
        gb = [gb_ref[r, hs] for r in rs]
        gc = [cum2(tri2, x) for x in gb]
        gc_row = [cum2(ones2, x[:, :c] * upper_f) for x in gb]
        decay = [jnp.where(tri, jnp.exp(jnp.minimum(a[:, :c] - b, 0.0)), 0.0) for a, b in zip(gc, gc_row)]
        kk = [_mm_nt(x, x) for x in k]
        n = [-jnp.where(strict, b[:, :c] * x * dc, 0.0) for b, x, dc in zip(beta, kk, decay)]
        inv = [eye + x for x in n]
        for step in range(5):
            nb = [x.astype(BF16) for x in n]
            n = [jnp.dot(x, x, preferred_element_type=F32) for x in nb]
            inv = [iv + _mm(iv, x) for iv, x in zip(inv, n)]
        egc = [jnp.exp(x) for x in gc]
        gl = [x[c - 1:c, :] for x in gc]
        inv_l = [x.astype(BF16) for x in inv]
        u = [_mm(a, v_ref[r, hs] * b) for a, r, b in zip(inv_l, rs, beta)]
        w = [_mm(a, x * (b * e)) for a, x, b, e in zip(inv_l, k, beta, egc)]
        qk = [_mm_nt(a, b) * dc for a, b, dc in zip(q, k, decay)]
        for ci in chunks:
            r = rs[ci]
            u_ref[r, hs] = u[ci]
            w_ref[r, hs] = w[ci]
            qk_ref[hh, r, :] = qk[ci]
            q_ref[r, hs] = q[ci] * egc[ci]
            k_ref[r, hs] = k[ci] * jnp.exp(gl[ci] - gc[ci])
            gb_ref[r, hs] = jnp.broadcast_to(jnp.exp(gl[ci]), (c, d))

    def chunk_body(ci, carry):
        r0 = pl.multiple_of(ci * c, c)
        rows = pl.ds(r0, c)
        hss = [slice(hh * d, (hh + 1) * d) for hh in range(nh)]
        st = [st_ref[hh] for hh in range(nh)]
        w_st = [_mm(w_ref[rows, hs], s_) for hs, s_ in zip(hss, st)]
        q_st = [_mm(q_ref[rows, hs], s_) for hs, s_ in zip(hss, st)]
        v_new = [u_ref[rows, hs] - x for hs, x in zip(hss, w_st)]
        o = [a + _mm(qk_ref[hh, rows, :], v) for hh, (a, v) in enumerate(zip(q_st, v_new))]
        kv = [_mm_tn(k_ref[rows, hs], v) for hs, v in zip(hss, v_new)]
        for hh, hs in enumerate(hss):
            st_ref[hh] = st[hh] * gb_ref[pl.ds(r0, 1), hs] + kv[hh]
            o_ref[rows, hs] = (_rms(o[hh], gnorm) * _silu(z_ref[rows, hs])).astype(o_ref.dtype)
        return carry

    lax.fori_loop(0, ts // c, chunk_body, 0)


def _deltanet(p32, conv_w, a_log, dt_bias, a_norm_g, *, ts, cols):
    bsz, s, _ = p32.shape
    d = HEAD_DIM
    nh = N_HEADS
    w = nh * d
    pad = lambda t: jnp.pad(t.astype(F32), (0, d - t.shape[0])).reshape(1, d)
    kernel = functools.partial(_deltanet_kernel, ts=ts, a_col=cols["a_lane"], b_col=cols["b_lane"])
    tile = lambda name: pl.BlockSpec((None, ts, w), lambda b, i: (b, i, cols[name] // nh))
    conv = lambda k: pl.BlockSpec((CONV_WIDTH, w), lambda b, i: (0, k))
    row = pl.BlockSpec((1, d), lambda b, i: (0, 0))
    return pl.pallas_call(
        kernel,
        grid=(bsz, s // ts),
        in_specs=[tile("qa"), tile("ka"), tile("va"), tile("za"),
                  pl.BlockSpec((None, ts, d), lambda b, i: (b, i, cols["small"])),
                  conv(0), conv(1), conv(2), row, row, row],
        out_specs=pl.BlockSpec((None, ts, w), lambda b, i: (b, i, 0)),
        out_shape=jax.ShapeDtypeStruct((bsz, s, w), BF16),
        scratch_shapes=[pltpu.VMEM((3, ts + 8, w), F32)]
        + [pltpu.VMEM((ts, w), F32) for _ in range(7)]
        + [pltpu.VMEM((nh, ts, CHUNK), F32), pltpu.VMEM((nh, d, d), F32)],
        compiler_params=pltpu.CompilerParams(
            dimension_semantics=("parallel", "arbitrary"), vmem_limit_bytes=VMEM_LIMIT),
        name="deltanet",
    )(p32, p32, p32, p32, p32, conv_w.astype(F32), conv_w.astype(F32), conv_w.astype(F32),
      pad(a_log), pad(dt_bias), a_norm_g.astype(F32).reshape(1, d))


def _hgrn2_kernel(q_ref, f_ref, i_ref, gate_ref, lb_ref, gn_ref, o_ref,
                  qs_ref, ks_ref, gc_ref, st_ref, *, ts):
    s = pl.program_id(1)
    c = CHUNK
    d = HEAD_DIM
    nh = N_HEADS
    SUB = 16

    @pl.when(s == 0)
    def _():
        st_ref[...] = jnp.zeros_like(st_ref)

    lb = lb_ref[...]
    f_raw = f_ref[...]
    log_sig = jnp.minimum(f_raw, 0.0) - jnp.log1p(jnp.exp(-jnp.abs(f_raw)))
    la = jnp.log(lb)
    lbb = jnp.log1p(-lb) + log_sig
    log_f = jnp.maximum(la, lbb) + jnp.log1p(jnp.exp(-jnp.abs(la - lbb)))
    qs_ref[...] = _silu(q_ref[...])
    ks_ref[...] = (1.0 - lb) * _sigmoid(-f_raw)

    row = _iota((c, c), 0)
    col = _iota((c, c), 1)
    tri_f = (col <= row).astype(F32)
    ones_dd = jnp.ones((d, d), BF16)
    rows_8d = _iota((8, d), 0)
    gnorm = gn_ref[...]

    tri2 = jnp.concatenate([tri_f, tri_f], axis=1).astype(BF16)
    for ci in range(ts // c):
        hi, lo = _split(log_f[ci * c:(ci + 1) * c, :])
        gc_ref[ci * c:(ci + 1) * c, :] = jnp.dot(tri2, jnp.concatenate([hi, lo], axis=0),
                                                 preferred_element_type=F32)

    blocks = [(sb * SUB, (sb + 1) * SUB) for sb in range(c // SUB)]

    def chunk_loop(ci, carry):
        r0 = pl.multiple_of(ci * c, c)
        rows = pl.ds(r0, c)
        hss = [slice(hh * d, (hh + 1) * d) for hh in range(nh)]
        q = [qs_ref[rows, hs] for hs in hss]
        k = [ks_ref[rows, hs] for hs in hss]
        v = [i_ref[rows, hs] for hs in hss]
        gc = [gc_ref[rows, hs] for hs in hss]

        def near_products(q, k, gc):
            prods = []
            for top, end in blocks:
                for j in range(top, end):
                    lo = (j // 8) * 8
                    e = jnp.exp2(gc[lo:end, :] - gc[j:j + 1, :])
                    if j % 8:
                        head = jnp.where(rows_8d >= j - lo, e[:8], 0.0)
                        e = jnp.concatenate([head, e[8:]], axis=0) if lo + 8 < end else head
                    prods.append(q[lo:end, :] * k[j:j + 1, :] * e)
            return jnp.concatenate(prods, axis=0).astype(BF16)

        def far_operands(q, k, gc):
            out = []
            for top, end in blocks[1:]:
                g_b = gc[top - 1:top, :]
                out.append((q[top:end, :] * jnp.exp(gc[top:end, :] - g_b),
                            k[:top, :] * jnp.exp(jnp.minimum(g_b - gc[:top, :], 0.0))))
            return out

        near = [near_products(a, b, g * LOG2E) for a, b, g in zip(q, k, gc)]
        far_ops = [far_operands(*x) for x in zip(q, k, gc)]
        st = [st_ref[hh] for hh in range(nh)]
        gl = [x[c - 1:c, :] for x in gc]
        sums = [jnp.dot(x, ones_dd, preferred_element_type=F32) for x in near]
        qk_far = [[_mm_nt(qe, ke) for qe, ke in ops] for ops in far_ops]
        far = [[_mm(a, vv[:top, :]) for a, (top, _) in zip(qs, blocks[1:])] for qs, vv in zip(qk_far, v)]
        o_st = [_mm_nt(a * jnp.exp(g), s_) for a, g, s_ in zip(q, gc, st)]
        kv = [_mm_tn(vv, kk * jnp.exp(g_l - g)) for vv, kk, g_l, g in zip(v, k, gl, gc)]

        for hh, hs in enumerate(hss):
            groups = [jnp.zeros((8, d), F32) for _ in range(c // 8)]
            at = 0
            for top, end in blocks:
                for j in range(top, end):
                    v_j = v[hh][j:j + 1, :]
                    for g in range(j // 8, end // 8):
                        groups[g] = groups[g] + sums[hh][at:at + 8, :] * v_j
                        at += 8
            for f, (top, end) in zip(far[hh], blocks[1:]):
                for g in range(top // 8, end // 8):
                    groups[g] = groups[g] + f[(g * 8 - top):(g * 8 - top + 8), :]
            o = jnp.concatenate(groups, axis=0) + o_st[hh]
            st_ref[hh] = st[hh] * jnp.exp(gl[hh]) + kv[hh]
            o_ref[rows, hs] = (_rms(o, gnorm) * _silu(gate_ref[rows, hs])).astype(o_ref.dtype)
        return carry

    lax.fori_loop(0, ts // c, chunk_loop, 0)


def _hgrn2(p32, lb, d_norm_g, *, ts, cols):
    bsz, s, _ = p32.shape
    d = HEAD_DIM
    nh = N_HEADS
    w = nh * d
    kernel = functools.partial(_hgrn2_kernel, ts=ts)
    tile = lambda name: pl.BlockSpec((None, ts, w), lambda b, i: (b, i, cols[name] // nh))
    return pl.pallas_call(
        kernel,
        grid=(bsz, s // ts),
        in_specs=[tile("qd"), tile("fd"), tile("id"), tile("gd"),
                  pl.BlockSpec((1, w), lambda b, i: (0, 0)),
                  pl.BlockSpec((1, d), lambda b, i: (0, 0))],
        out_specs=pl.BlockSpec((None, ts, w), lambda b, i: (b, i, 0)),
        out_shape=jax.ShapeDtypeStruct((bsz, s, w), BF16),
        scratch_shapes=[pltpu.VMEM((ts, w), F32), pltpu.VMEM((ts, w), F32),
                        pltpu.VMEM((ts, w), F32), pltpu.VMEM((nh, d, d), F32)],
        compiler_params=pltpu.CompilerParams(
            dimension_semantics=("parallel", "arbitrary"), vmem_limit_bytes=VMEM_LIMIT),
        name="hgrn2",
    )(p32, p32, p32, p32, lb.astype(F32).reshape(1, w), d_norm_g.astype(F32).reshape(1, d))


def _stickbreak_kernel(q_ref, k_ref, v_ref, o_ref, acc_ref, *, tq):
    i = pl.program_id(1)
    d = HEAD_DIM
    nh = N_HEADS
    row = _iota((tq, tq), 0)
    col = _iota((tq, tq), 1)
    causal = col < row
    later = (row > col).astype(BF16)
    later2 = jnp.concatenate([later, later], axis=0)

    heads = [slice(hh * d, (hh + 1) * d) for hh in range(nh)]

    def scores(blocks):
        jobs = [(j, dg, hs) for j, dg in blocks for hs in heads]
        z = [_mm_nt(q_ref[:, hs], k_ref[pl.ds(pl.multiple_of(j * tq, tq), tq), hs]) * (d ** -0.5)
             for j, _, hs in jobs]
        sp = [_softplus(x) for x in z]
        l1m = [jnp.where(causal, -x, 0.0) if dg else -x for x, (_, dg, _) in zip(sp, jobs)]
        rest = [jnp.dot(jnp.concatenate(_split(x), axis=1), later2, preferred_element_type=F32)
                for x in l1m]
        out = [((a - b) + r, l) for a, b, r, l in zip(z, sp, rest, l1m)]
        return [out[b * nh:(b + 1) * nh] for b in range(len(blocks))]

    def block(j, carries):
        (sc,) = scores([(j, False)])
        ps = [jnp.exp(logw + c) for (logw, _), c in zip(sc, carries)]
        pv = [_mm(p, v_ref[pl.ds(pl.multiple_of(j * tq, tq), tq), hs]) for p, hs in zip(ps, heads)]
        for hs, x in zip(heads, pv):
            acc_ref[:, hs] += x
        return tuple(c + jnp.sum(l1m, axis=-1, keepdims=True) for (_, l1m), c in zip(sc, carries))

    j1 = jnp.maximum(i - 1, 0)
    j2 = jnp.maximum(i - 2, 0)
    live1 = jnp.where(i > 0, 1.0, 0.0)
    live2 = jnp.where(i > 1, 1.0, 0.0)
    s0, s1, s2 = scores([(i, True), (j1, False), (j2, False)])
    carries = []
    for hh, hs in enumerate(heads):
        c0 = jnp.sum(s0[hh][1], axis=-1, keepdims=True)
        c1 = c0 + jnp.sum(s1[hh][1], axis=-1, keepdims=True)
        p0 = jnp.where(causal, jnp.exp(s0[hh][0]), 0.0)
        p1 = jnp.exp(s1[hh][0] + c0) * live1
        p2 = jnp.exp(s2[hh][0] + c1) * live2
        acc_ref[:, hs] = (_mm(p0, v_ref[pl.ds(pl.multiple_of(i * tq, tq), tq), hs])
                          + _mm(p1, v_ref[pl.ds(pl.multiple_of(j1 * tq, tq), tq), hs])
                          + _mm(p2, v_ref[pl.ds(pl.multiple_of(j2 * tq, tq), tq), hs]))
        carries.append(c1 + jnp.sum(s2[hh][1], axis=-1, keepdims=True))
    carries = tuple(carries)

    def cond(c):
        worst = functools.reduce(jnp.maximum, c[1])
        return jnp.logical_and(c[0] >= 0, jnp.max(worst) >= EXP_ZERO_BELOW)

    def body(c):
        return c[0] - 1, block(c[0], c[1])

    lax.while_loop(cond, body, (i - 3, carries))
    o_ref[...] = acc_ref[...].astype(o_ref.dtype)


def _stickbreak(p16, *, tq, cols):
    bsz, s, _ = p16.shape
    nh = N_HEADS
    w = nh * HEAD_DIM
    kernel = functools.partial(_stickbreak_kernel, tq=tq)
    resident = dict(pipeline_mode=pl.Buffered(1))
    return pl.pallas_call(
        kernel,
        grid=(bsz, s // tq),
        in_specs=[pl.BlockSpec((None, tq, w), lambda b, i: (b, i, cols["qc"] // nh)),
                  pl.BlockSpec((None, s, w), lambda b, i: (b, 0, cols["kc"] // nh), **resident),
                  pl.BlockSpec((None, s, w), lambda b, i: (b, 0, cols["vc"] // nh), **resident)],
        out_specs=pl.BlockSpec((None, tq, w), lambda b, i: (b, i, 0)),
        out_shape=jax.ShapeDtypeStruct((bsz, s, w), BF16),
        scratch_shapes=[pltpu.VMEM((tq, w), F32)],
        compiler_params=pltpu.CompilerParams(
            dimension_semantics=("parallel", "arbitrary"), vmem_limit_bytes=VMEM_LIMIT),
        name="stickbreak",
    )(p16, p16, p16)


def _dsa_kernel(qi_ref, smq_ref, q_ref, sm_ref, k_ref, vt_ref, bias_ref, o_ref,
                sc_ref, scb_ref, qct_ref, kc_ref, bd_ref, lg_ref, *, tq, k_sel, wi_lane, wide):
    i = pl.program_id(1)
    tk = tq
    d = HEAD_DIM
    nh = N_HEADS
    ksel = float(k_sel)
    per_wide = wide // tk
    n_wide = (i + per_wide) // per_wide
    sub = 2 * tk
    lane_q = _iota((1, tq), 1)

    def tree(parts, op):
        while len(parts) > 1:
            parts = [op(parts[j], parts[j + 1]) if j + 1 < len(parts) else parts[j]
                     for j in range(0, len(parts), 2)]
        return parts[0]

    def col_fold(x, op=jnp.add, rows=8):
        return tree([x[r * rows:(r + 1) * rows] for r in range(x.shape[0] // rows)], op)

    @pl.when(i == 0)
    def _():
        def prep(g, carry):
            g0 = pl.multiple_of(g * wide, wide)
            hi, lo = _split(sm_ref[pl.ds(g0, wide), :][:, :IDX_DIM])
            kc_ref[pl.ds(g0, wide), :] = jnp.concatenate([hi, lo, hi], axis=1)
            return carry
        lax.fori_loop(0, sm_ref.shape[0] // wide, prep, 0)

    qit = qi_ref[...].T
    for p in range(IDX_HEADS // 2):
        halves = []
        for hh in (2 * p, 2 * p + 1):
            hi, lo = _split(qit[hh * IDX_DIM:(hh + 1) * IDX_DIM, :])
            halves.append(jnp.concatenate([hi, hi, lo], axis=0))
        qct_ref[p] = jnp.concatenate(halves, axis=1)
    w_rows = smq_ref[...].T[wi_lane:wi_lane + IDX_HEADS, :] * ((IDX_HEADS ** -0.5) * (IDX_DIM ** -0.5))

    q2t = (q_ref[...] * ((d ** -0.5) * LOG2E)).T.astype(BF16)
    zero_dq = jnp.zeros((d, tq), BF16)
    for p in range(nh // 2):
        top = jnp.concatenate([q2t[2 * p * d:(2 * p + 1) * d], zero_dq], axis=1)
        bot = jnp.concatenate([zero_dq, q2t[(2 * p + 1) * d:(2 * p + 2) * d]], axis=1)
        bd_ref[p] = jnp.concatenate([top, bot], axis=0)

    limit = i * tq + (lane_q // CHUNK + 1) * CHUNK

    rows_s = _iota((sub, tq), 0)

    def score_groups(gs, mm, masked):
        mn, mx = mm
        k0s = [pl.multiple_of(g * wide + sb * sub, sub) for g in gs for sb in range(wide // sub)]
        keys = [kc_ref[pl.ds(k0, sub), :] for k0 in k0s]
        accs = [jnp.zeros((sub, tq), F32) for _ in k0s]
        for p in range(IDX_HEADS // 2):
            rhs = qct_ref[p]
            for n, kk in enumerate(keys):
                s2 = jnp.dot(kk, rhs, preferred_element_type=F32)
                accs[n] = (accs[n] + jnp.maximum(s2[:, :tq], 0.0) * w_rows[2 * p:2 * p + 1, :]
                           + jnp.maximum(s2[:, tq:], 0.0) * w_rows[2 * p + 1:2 * p + 2, :])
        for k0, sct in zip(k0s, accs):
            if masked:
                adm = (k0 + rows_s) < limit
                mn = jnp.minimum(mn, col_fold(jnp.where(adm, sct, jnp.inf), jnp.minimum))
                sct = jnp.where(adm, sct, -jnp.inf)
            else:
                mn = jnp.minimum(mn, col_fold(sct, jnp.minimum))
            mx = jnp.maximum(mx, col_fold(sct, jnp.maximum))
            sc_ref[pl.ds(k0, sub), :] = sct
            scb_ref[pl.ds(k0, sub), :] = _floor_bf16(sct)
        return mn, mx

    def score_pair(j, mm):
        return score_groups((2 * j, 2 * j + 1), mm, False)

    n_full = n_wide - 1
    mm = lax.fori_loop(0, n_full // 2, score_pair,
                       (jnp.full((8, tq), jnp.inf, F32), jnp.full((8, tq), -jnp.inf, F32)))
    mm = lax.cond(n_full % 2 == 1, lambda c: score_groups((n_full - 1,), c, False), lambda c: c, mm)
    mn, mx = score_groups((n_wide - 1,), mm, True)

    n_pairs = (n_wide + 1) // 2

    @pl.when(n_wide % 2 == 1)
    def _():
        sc_ref[pl.ds(pl.multiple_of(n_wide * wide, wide), wide), :] = jnp.full((wide, tq), -jnp.inf, F32)
        scb_ref[pl.ds(pl.multiple_of(n_wide * wide, wide), wide), :] = jnp.full((wide, tq), -jnp.inf, BF16)
    rmin = jnp.min(mn, axis=0, keepdims=True)
    rmax = jnp.max(mx, axis=0, keepdims=True)

    def count(pred):
        def body(j, acc):
            for g in (2 * j, 2 * j + 1):
                acc = acc + col_fold(pred(sc_ref[pl.ds(pl.multiple_of(g * wide, wide), wide), :]))
            return acc
        return jnp.sum(lax.fori_loop(0, n_pairs, body, jnp.zeros((8, tq), F32)), axis=0, keepdims=True)

    def max_below(x):
        def body(j, acc):
            for g in (2 * j, 2 * j + 1):
                blk = sc_ref[pl.ds(pl.multiple_of(g * wide, wide), wide), :]
                acc = jnp.maximum(acc, col_fold(jnp.where(blk < x, blk, -jnp.inf), jnp.maximum))
            return acc
        return jnp.max(lax.fori_loop(0, n_pairs, body, jnp.full((8, tq), -jnp.inf, F32)), axis=0, keepdims=True)

    n_adm = limit.astype(F32)
    all_sel = n_adm <= ksel

    def bisect(c):
        lo, hi, c_lo = c
        mid = 0.5 * lo + 0.5 * hi
        cm = count(lambda blk: _ind(blk >= mid))
        ge = cm >= ksel
        return jnp.where(ge, mid, lo), jnp.where(ge, hi, mid), jnp.where(ge, cm, c_lo)

    def pending(c_lo, tied):
        return jnp.where(all_sel, 0.0, jnp.where(tied > 0.5, 0.0, _ind(c_lo != ksel)))

    def bisect_coarse(_, c):
        lo, hi, c_lo = c
        mid = _floor_bf16(0.5 * lo + 0.5 * hi).astype(F32)
        t_b = jnp.broadcast_to(mid, (16, tq)).astype(BF16)
        one_b = jnp.ones((16, tq), BF16)
        zero_b = jnp.zeros((16, tq), BF16)

        def body(j, acc):
            for g in (2 * j, 2 * j + 1):
                blk = scb_ref[pl.ds(pl.multiple_of(g * wide, wide), wide), :]
                ind = [jnp.where(blk[r * 16:(r + 1) * 16] >= t_b, one_b, zero_b) for r in range(wide // 16)]
                acc = acc + tree(ind, jnp.add).astype(F32)
            return acc

        acc = lax.fori_loop(0, n_pairs, body, jnp.zeros((16, tq), F32))
        cm = jnp.sum(acc, axis=0, keepdims=True)
        ge = cm >= ksel
        return jnp.where(ge, mid, lo), jnp.where(ge, hi, mid), jnp.where(ge, cm, c_lo)

    lo0 = _floor_bf16(rmin).astype(F32)
    hi0 = _floor_bf16(rmax + (jnp.abs(rmax) * (2.0 ** -6) + 1e-30)).astype(F32)
    state = lax.fori_loop(0, BISECT_COARSE, bisect_coarse, (lo0, hi0, n_adm))
    state = lax.fori_loop(0, BISECT_FIXED, lambda _, c: bisect(c), state)

    def round_cond(c):
        return jnp.max(pending(c[0][2], c[1])) > 0.5

    def round_body(c):
        st, tied, v, need = c

        def more_cond(s):
            return jnp.logical_and(s[0] < BISECT_EXTRA, jnp.max(pending(s[1][2], tied)) > 0.5)

        _, st = lax.while_loop(more_cond, lambda s: (s[0] + 1, bisect(s[1])), (jnp.int32(0), st))
        pend = pending(st[2], tied)

        def check(_):
            cand = max_below(st[1])
            c_ge = count(lambda blk: _ind(blk >= cand))
            c_gt = count(lambda blk: _ind(blk > cand))
            ok = jnp.where(pend > 0.5, _ind(c_ge >= ksel), 0.0)
            return (jnp.where(ok > 0.5, 1.0, tied), jnp.where(ok > 0.5, cand, v),
                    jnp.where(ok > 0.5, ksel - c_gt, need))

        tied, v, need = lax.cond(jnp.max(pend) > 0.5, check, lambda _: (tied, v, need), 0)
        return st, tied, v, need

    zeros1 = jnp.zeros((1, tq), F32)
    (lo_f, _, _), tied, v_tie, need = lax.while_loop(round_cond, round_body, (state, zeros1, zeros1, zeros1))
    vth = jnp.where(all_sel, F32_LOWEST, jnp.where(tied > 0.5, v_tie, lo_f))

    @pl.when(jnp.max(tied) > 0.5)
    def _():
        v_eq = jnp.where(tied > 0.5, v_tie, jnp.inf)
        incl = (_iota((tk, tk), 1) <= _iota((tk, tk), 0)).astype(BF16)

        def demote(g, seen):
            g0 = pl.multiple_of(g * wide, wide)
            xs = [sc_ref[pl.ds(g0 + pb * tk, tk), :] for pb in range(per_wide)]
            eqs = [_ind(x == v_eq) for x in xs]
            inblk = [jnp.dot(incl, e.astype(BF16), preferred_element_type=F32) for e in eqs]
            for pb in range(per_wide):
                rank = inblk[pb] + seen
                sc_ref[pl.ds(g0 + pb * tk, tk), :] = jnp.where(eqs[pb] * _ind(rank > need) > 0.5,
                                                               -jnp.inf, xs[pb])
                seen = seen + jnp.sum(col_fold(eqs[pb]), axis=0, keepdims=True)
            return seen

        lax.fori_loop(0, n_wide, demote, zeros1)

    g_near = jnp.maximum(i - 1, 0) // per_wide

    def logit_group(g, mx, near):
        out = list(mx)
        for sb in range(wide // sub):
            k0 = pl.multiple_of(g * wide + sb * sub, sub)
            sel = sc_ref[pl.ds(k0, sub), :] >= vth
            for p in range(nh // 2):
                pair = jnp.dot(k_ref[pl.ds(k0, sub), 2 * p * d:(2 * p + 2) * d], bd_ref[p],
                               preferred_element_type=F32)
                for hh in (2 * p, 2 * p + 1):
                    lm = pair[:, (hh - 2 * p) * tq:(hh - 2 * p + 1) * tq]
                    if near:
                        back = [jnp.clip(i - (g * per_wide + sb * (sub // tk) + pb), 0, 2)
                                for pb in range(sub // tk)]
                        lm = lm + jnp.concatenate([bias_ref[bk, hh] for bk in back], axis=0)
                    lm = jnp.where(sel, lm, NEG_BIG)
                    lg_ref[hh, pl.ds(k0, sub), :] = lm
                    out[hh] = jnp.maximum(out[hh], col_fold(lm, jnp.maximum))
        return tuple(out)

    mx = tuple(jnp.full((8, tq), NEG_BIG, F32) for _ in range(nh))
    def logit_pair(j, mx, near):
        return logit_group(2 * j + 1, logit_group(2 * j, mx, near), near)

    far_pairs = g_near // 2
    mx = lax.fori_loop(0, far_pairs, functools.partial(logit_pair, near=False), mx)
    mx = lax.fori_loop(far_pairs, n_pairs, functools.partial(logit_pair, near=True), mx)
    m_q = [jnp.max(mx[hh], axis=0, keepdims=True) for hh in range(nh)]

    ones_rows = jnp.ones((8, wide), BF16)

    def pv_pair(j, carry):
        ls, accs = list(carry[0]), list(carry[1])
        jobs = [(pl.multiple_of(g * wide, wide), hh) for g in (2 * j, 2 * j + 1) for hh in range(nh)]
        ps = [jnp.exp2(lg_ref[hh, pl.ds(g0, wide), :] - m_q[hh]).astype(BF16) for g0, hh in jobs]
        outs = [jnp.dot(jnp.concatenate([vt_ref[hh * d:(hh + 1) * d, pl.ds(g0, wide)], ones_rows], axis=0),
                        p, preferred_element_type=F32) for (g0, hh), p in zip(jobs, ps)]
        for (_, hh), out in zip(jobs, outs):
            ls[hh] = ls[hh] + out[d:]
            accs[hh] = accs[hh] + out[:d]
        return tuple(ls), tuple(accs)

    ls, accs = lax.fori_loop(0, n_pairs, pv_pair,
                             (tuple(jnp.zeros((8, tq), F32) for _ in range(nh)),
                              tuple(jnp.zeros((d, tq), F32) for _ in range(nh))))
    for hh in range(nh):
        o_ref[:, hh * d:(hh + 1) * d] = (accs[hh] / ls[hh][0:1]).T.astype(o_ref.dtype)


def _dsa(p32, p16, vt, bias_tiles, *, tq, cols):
    bsz, s, _ = p32.shape
    d = HEAD_DIM
    nh = N_HEADS
    wide = 4 * tq
    k_sel = min(TOPK_MAX, s // 4)
    w512 = nh * d
    kernel = functools.partial(_dsa_kernel, tq=tq, k_sel=k_sel, wi_lane=cols["wi_lane"], wide=wide)
    resident = dict(pipeline_mode=pl.Buffered(1))
    return pl.pallas_call(
        kernel,
        grid=(bsz, s // tq),
        in_specs=[pl.BlockSpec((None, tq, w512), lambda b, i: (b, i, cols["qi"] // nh)),
                  pl.BlockSpec((None, tq, d), lambda b, i: (b, i, cols["small"])),
                  pl.BlockSpec((None, tq, w512), lambda b, i: (b, i, cols["qb"] // nh)),
                  pl.BlockSpec((None, s, d), lambda b, i: (b, 0, cols["small"]), **resident),
                  pl.BlockSpec((None, s, w512), lambda b, i: (b, 0, cols["kb"] // nh), **resident),
                  pl.BlockSpec((w512, s), lambda b, i: (0, b), **resident),
                  pl.BlockSpec((3, nh, tq, tq), lambda b, i: (0, 0, 0, 0), **resident)],
        out_specs=pl.BlockSpec((None, tq, w512), lambda b, i: (b, i, 0)),
        out_shape=jax.ShapeDtypeStruct((bsz, s, w512), BF16),
        scratch_shapes=[pltpu.VMEM((s, tq), F32),
                        pltpu.VMEM((s, tq), BF16),
                        pltpu.VMEM((IDX_HEADS // 2, 3 * IDX_DIM, 2 * tq), BF16),
                        pltpu.VMEM((s, 3 * IDX_DIM), BF16),
                        pltpu.VMEM((nh // 2, 2 * d, 2 * tq), BF16),
                        pltpu.VMEM((nh, s, tq), F32)],
        compiler_params=pltpu.CompilerParams(
            dimension_semantics=("parallel", "arbitrary"), vmem_limit_bytes=VMEM_LIMIT),
        name="dsa",
    )(p32, p32, p32, p32, p16, vt, bias_tiles)


def _t5_bucket(rel):
    nb = REL_BUCKETS // 2
    max_exact = nb // 2
    ret = jnp.where(rel > 0, nb, 0)
    n = jnp.abs(rel)
    large = max_exact + (jnp.log(jnp.maximum(n, 1).astype(F32) / max_exact)
                         / math.log(REL_MAX_DIST / max_exact) * (nb - max_exact)).astype(jnp.int32)
    large = jnp.minimum(large, nb - 1)
    return ret + jnp.where(n < max_exact, n, large)


def _bias_tiles(rel_table, tq):
    assert tq >= REL_MAX_DIST
    t = jnp.arange(tq)
    back = jnp.arange(3)
    rel = (t[None, None, :] - back[:, None, None] * tq) - t[None, :, None]
    onehot = (_t5_bucket(rel)[..., None] == jnp.arange(REL_BUCKETS)).astype(F32)
    tiles = jnp.einsum("bqkn,nh->bhkq", onehot, rel_table.astype(F32),
                       precision=HIGHEST)
    return (tiles - tiles[2:3]) * LOG2E


def _even_layout(w_in):
    d = HEAD_DIM
    a_w = 2 * N_HEADS * d + N_HEADS * d
    offs = {}
    o = 0
    for name, w in (("qkv", a_w), ("z", N_HEADS * d), ("a", N_HEADS), ("b", N_HEADS),
                    ("qb", N_HEADS * d), ("kb", N_HEADS * d), ("vb", N_HEADS * d),
                    ("qi", IDX_HEADS * IDX_DIM), ("ki", IDX_DIM), ("wi", IDX_HEADS)):
        offs[name] = (o, o + w)
        o += w
    assert o == w_in.shape[1]
    sl = lambda n: w_in[:, offs[n][0]:offs[n][1]]
    small_w = IDX_DIM + 2 * N_HEADS + IDX_HEADS
    small_pad = -small_w % d
    zeros = lambda n: jnp.zeros((w_in.shape[0], n), w_in.dtype)
    w32 = jnp.concatenate([sl("qkv"), sl("z"), sl("qb"), sl("qi"),
                           sl("ki"), sl("a"), sl("b"), sl("wi"), zeros(small_pad)], axis=1)
    n32 = w32.shape[1]
    tn = n32 // 5
    assert tn * 5 == n32 and tn % d == 0
    w16 = jnp.concatenate([sl("kb"), zeros(tn - N_HEADS * d)], axis=1)
    nh = N_HEADS
    cols = dict(qa=0, ka=nh, va=2 * nh, za=3 * nh, qb=4 * nh, qi=5 * nh, small=6 * nh, kb=0,
                a_lane=IDX_DIM, b_lane=IDX_DIM + nh, wi_lane=IDX_DIM + 2 * nh, n32=n32, tn=tn)
    return jnp.concatenate([w32, w16], axis=1).astype(BF16), sl("vb").T.astype(BF16), cols


def kernel(x, norm_g, w_in_even, conv_w_even, a_log_even, dt_bias_even, a_norm_even, w_out_even,
           rel_bias, w_in_odd, lb_logits, d_norm_odd, w_out_odd, w_gate, w_up, w_down):
    bsz, s, d = x.shape
    t = bsz * s
    depth = norm_g.shape[0]
    nh = N_HEADS
    tq = Q_TILE
    lb_all = jnp.cumsum(jax.nn.softmax(lb_logits.astype(F32), axis=0), axis=0)
    lb_all = lb_all - lb_all[:1]
    odd_cols = dict(qc=0, kc=nh, vc=2 * nh, qd=0, fd=nh, id=2 * nh, gd=3 * nh)
    bias_tiles = _bias_tiles(rel_bias, tq)

    h = x.reshape(t, d)
    for l in range(depth):
        if l % 2 == 0:
            e = l // 2
            w_even, w_vt, cols = _even_layout(w_in_even[e])
            p32, p16, vt = _norm_matmul(h, norm_g[l, 0], w_even, tm=PROJ_TILE, tn=cols["tn"], n32=cols["n32"],
                                        w_t=w_vt)
            p32 = p32.reshape(bsz, s, -1)
            p16 = p16.reshape(bsz, s, -1)
            o_1 = _deltanet(p32, conv_w_even[e], a_log_even[e], dt_bias_even[e], a_norm_even[e],
                            ts=min(DELTANET_TILE, s), cols=cols)
            o_2 = _dsa(p32, p16, vt, bias_tiles, tq=tq, cols=cols)
            w_out = w_out_even[e]
        else:
            o = l // 2
            n16 = 3 * nh * HEAD_DIM
            w_odd = jnp.concatenate([w_in_odd[o][:, n16:], w_in_odd[o][:, :n16]], axis=1).astype(BF16)
            p32, p16 = _norm_matmul(h, norm_g[l, 0], w_odd, tm=PROJ_TILE, tn=ODD_COL_TILE, n32=w_odd.shape[1] - n16)
            p32 = p32.reshape(bsz, s, -1)
            p16 = p16.reshape(bsz, s, -1)
            o_1 = _stickbreak(p16, tq=tq, cols=odd_cols)
            o_2 = _hgrn2(p32, lb_all[l], d_norm_odd[o], ts=min(SEQ_TILE, s), cols=odd_cols)
            w_out = w_out_odd[o]
        h = _mix_ffn(o_1.reshape(t, -1), o_2.reshape(t, -1), w_out, h, norm_g[l, 1], norm_g[l, 2], norm_g[l, 3],
                     w_gate[l], w_up[l], w_down[l], tm=ROW_TILE, tf=FFN_TILE)
    return h.reshape(bsz, s, d)
```

```python
import functools
import math

import jax
import jax.numpy as jnp
from jax import lax
from jax.experimental import pallas as pl
from jax.experimental.pallas import tpu as pltpu

F32 = jnp.float32
BF16 = jnp.bfloat16
HIGHEST = lax.Precision.HIGHEST

CHUNK = 64
HEAD_DIM = 128
N_HEADS = 4
IDX_HEADS = 8
IDX_DIM = 64
TOPK_MAX = 256
CONV_WIDTH = 4
REL_BUCKETS = 32
REL_MAX_DIST = 128
EPS = 1e-6
NEG_BIG = -1e30
LOG2E = 1.4426950408889634
BISECT_COARSE = 10
BISECT_FIXED = 8
BISECT_EXTRA = 6
F32_LOWEST = -3.4028234663852886e38
EXP_ZERO_BELOW = -104.0
VMEM_LIMIT = 56 * 1024 * 1024

PROJ_TILE = 2048
ROW_TILE = 512
DELTANET_TILE = 1024
SEQ_TILE = 512
Q_TILE = 128
ODD_COL_TILE = 512
FFN_TILE = 2816


def _mm(a, b):
    return jnp.dot(a.astype(BF16), b.astype(BF16), preferred_element_type=F32)


def _mm_nt(a, b):
    return lax.dot_general(a.astype(BF16), b.astype(BF16), (((1,), (1,)), ((), ())),
                           preferred_element_type=F32)


def _mm_tn(a, b):
    return lax.dot_general(a.astype(BF16), b.astype(BF16), (((0,), (0,)), ((), ())),
                           preferred_element_type=F32)


def _split(x):
    hi = x.astype(BF16)
    return hi, (x - hi.astype(F32)).astype(BF16)


def _floor_bf16(x):
    bits = pltpu.bitcast(x, jnp.int32)
    down = jnp.where(bits >= 0, bits, bits + 0xFFFF) & jnp.int32(-65536)
    return pltpu.bitcast(down, F32).astype(BF16)


def _sigmoid(x):
    return 1.0 / (1.0 + jnp.exp(-x))


def _silu(x):
    return x * _sigmoid(x)


def _softplus(x):
    return jnp.maximum(x, 0.0) + jnp.log1p(jnp.exp(-jnp.abs(x)))


def _rms(x, g):
    return x * lax.rsqrt(jnp.mean(x * x, axis=-1, keepdims=True) + EPS) * g


def _iota(shape, dim):
    return lax.broadcasted_iota(jnp.int32, shape, dim)


def _ind(mask):
    return jnp.where(mask, 1.0, 0.0)


def _norm_matmul_kernel(x_ref, g_ref, w_ref, *rest, n_t, tiles32):
    if n_t:
        wt_ref, o32_ref, o16_ref, ot_ref, xn_ref = rest
    else:
        o32_ref, o16_ref, xn_ref = rest
    j = pl.program_id(1)

    @pl.when(j == 0)
    def _():
        xn_ref[...] = _rms(x_ref[...], g_ref[...]).astype(BF16)
        if n_t:
            ot_ref[...] = lax.dot_general(wt_ref[...], xn_ref[...], (((1,), (1,)), ((), ())),
                                          preferred_element_type=F32).astype(BF16)

    y = jnp.dot(xn_ref[...], w_ref[...], preferred_element_type=F32)

    @pl.when(j < tiles32)
    def _():
        o32_ref[...] = y

    @pl.when(j >= tiles32)
    def _():
        o16_ref[...] = y.astype(BF16)


def _norm_matmul(x, g, w, *, tm, tn, n32, w_t=None):
    t, d = x.shape
    n = w.shape[1]
    n_t = 0 if w_t is None else w_t.shape[0]
    tiles32 = n32 // tn
    assert tiles32 * tn == n32 and (n - n32) % tn == 0 and 0 < n32 < n
    in_specs = [pl.BlockSpec((tm, d), lambda i, j: (i, 0)),
                pl.BlockSpec((1, d), lambda i, j: (0, 0)),
                pl.BlockSpec((d, tn), lambda i, j: (0, j))]
    out_specs = [pl.BlockSpec((tm, tn), lambda i, j: (i, jnp.minimum(j, tiles32 - 1))),
                 pl.BlockSpec((tm, tn), lambda i, j: (i, jnp.maximum(j - tiles32, 0)))]
    out_shape = [jax.ShapeDtypeStruct((t, n32), F32), jax.ShapeDtypeStruct((t, n - n32), BF16)]
    args = [x, g.reshape(1, d), w]
    if n_t:
        in_specs.append(pl.BlockSpec((n_t, d), lambda i, j: (0, 0)))
        out_specs.append(pl.BlockSpec((n_t, tm), lambda i, j: (0, i)))
        out_shape.append(jax.ShapeDtypeStruct((n_t, t), BF16))
        args.append(w_t)
    return pl.pallas_call(
        functools.partial(_norm_matmul_kernel, n_t=n_t, tiles32=tiles32),
        grid=(t // tm, n // tn),
        in_specs=in_specs,
        out_specs=out_specs,
        out_shape=out_shape,
        scratch_shapes=[pltpu.VMEM((tm, d), BF16)],
        compiler_params=pltpu.CompilerParams(
            dimension_semantics=("parallel", "arbitrary"), vmem_limit_bytes=VMEM_LIMIT),
        name="norm_matmul",
    )(*args)


def _mix_ffn_kernel(ca_ref, cb_ref, wa_ref, wb_ref, h_ref, gmix_ref, gpre_ref, gpost_ref,
                    wg_ref, wu_ref, wd_ref, o_ref, h1_ref, xn_ref, acc_ref):
    f = pl.program_id(1)

    @pl.when(f == 0)
    def _():
        y = (jnp.dot(ca_ref[...], wa_ref[...], preferred_element_type=F32)
             + jnp.dot(cb_ref[...], wb_ref[...], preferred_element_type=F32))
        h1 = h_ref[...] + _rms(y, gmix_ref[...])
        h1_ref[...] = h1
        xn_ref[...] = _rms(h1, gpre_ref[...]).astype(BF16)
        acc_ref[...] = jnp.zeros_like(acc_ref)

    xn = xn_ref[...]
    gate = jnp.dot(xn, wg_ref[...], preferred_element_type=F32)
    up = jnp.dot(xn, wu_ref[...], preferred_element_type=F32)
    act = (_silu(gate) * up).astype(BF16)
    acc_ref[...] += jnp.dot(act, wd_ref[...], preferred_element_type=F32)

    @pl.when(f == pl.num_programs(1) - 1)
    def _():
        o_ref[...] = h1_ref[...] + _rms(acc_ref[...], gpost_ref[...])


def _mix_ffn(ca, cb, w_out, h, g_mix, g_pre, g_post, wg, wu, wd, *, tm, tf):
    t, d = h.shape
    ff = wg.shape[1]
    wa_n = ca.shape[1]
    wb_n = cb.shape[1]
    row = pl.BlockSpec((1, d), lambda i, f: (0, 0))
    once = dict(pipeline_mode=pl.Buffered(1)) if tf == ff else {}
    return pl.pallas_call(
        _mix_ffn_kernel,
        grid=(t // tm, ff // tf),
        in_specs=[pl.BlockSpec((tm, wa_n), lambda i, f: (i, 0)),
                  pl.BlockSpec((tm, wb_n), lambda i, f: (i, 0)),
                  pl.BlockSpec((wa_n, d), lambda i, f: (0, 0)),
                  pl.BlockSpec((wb_n, d), lambda i, f: (0, 0)),
                  pl.BlockSpec((tm, d), lambda i, f: (i, 0)),
                  row, row, row,
                  pl.BlockSpec((d, tf), lambda i, f: (0, f), **once),
                  pl.BlockSpec((d, tf), lambda i, f: (0, f), **once),
                  pl.BlockSpec((tf, d), lambda i, f: (f, 0), **once)],
        out_specs=pl.BlockSpec((tm, d), lambda i, f: (i, 0)),
        out_shape=jax.ShapeDtypeStruct((t, d), F32),
        scratch_shapes=[pltpu.VMEM((tm, d), F32), pltpu.VMEM((tm, d), BF16), pltpu.VMEM((tm, d), F32)],
        compiler_params=pltpu.CompilerParams(
            dimension_semantics=("parallel", "arbitrary"), vmem_limit_bytes=VMEM_LIMIT),
        name="mix_ffn",
    )(ca, cb, w_out[:wa_n].astype(BF16), w_out[wa_n:].astype(BF16), h,
      g_mix.reshape(1, d), g_pre.reshape(1, d), g_post.reshape(1, d),
      wg.astype(BF16), wu.astype(BF16), wd.astype(BF16))


def _deltanet_kernel(xq_ref, xk_ref, xv_ref, z_ref, sm_ref, cwq_ref, cwk_ref, cwv_ref,
                     alog_ref, dtb_ref, gn_ref, o_ref,
                     xpad_ref, q_ref, k_ref, v_ref, gb_ref, bb_ref, u_ref, w_ref, qk_ref, st_ref,
                     *, ts, a_col, b_col):
    s = pl.program_id(1)
    c = CHUNK
    d = HEAD_DIM
    nh = N_HEADS

    @pl.when(s == 0)
    def _():
        xpad_ref[:, 0:8, :] = jnp.zeros((3, 8, nh * d), F32)
        st_ref[...] = jnp.zeros_like(st_ref)

    @pl.when(s != 0)
    def _():
        xpad_ref[:, 0:8, :] = xpad_ref[:, ts:ts + 8, :]

    xpad_ref[0, 8:ts + 8, :] = xq_ref[...]
    xpad_ref[1, 8:ts + 8, :] = xk_ref[...]
    xpad_ref[2, 8:ts + 8, :] = xv_ref[...]

    def conv_silu(idx, cw_ref, hs):
        cw = cw_ref[:, hs]
        acc = xpad_ref[idx, 8 - (CONV_WIDTH - 1):8 - (CONV_WIDTH - 1) + ts, hs] * cw[0:1, :]
        for j in range(1, CONV_WIDTH):
            off = 8 - (CONV_WIDTH - 1) + j
            acc = acc + xpad_ref[idx, off:off + ts, hs] * cw[j:j + 1, :]
        return _silu(acc)

    def l2norm(t):
        return t * lax.rsqrt(jnp.sum(t * t, axis=-1, keepdims=True) + EPS)

    row = _iota((c, c), 0)
    col = _iota((c, c), 1)
    tri = (col <= row)
    strict = (col < row)
    tri_f = tri.astype(F32)
    upper_f = (row <= col).astype(F32)
    eye = (row == col).astype(F32)
    gnorm = gn_ref[...]
    chunks = range(ts // c)
    rs = [slice(ci * c, (ci + 1) * c) for ci in chunks]
    tri2 = jnp.concatenate([tri_f, tri_f], axis=1).astype(BF16)
    ones2 = jnp.ones((c, 2 * c), BF16)

    def cum2(lhs2, x):
        hi, lo = _split(x)
        return jnp.dot(lhs2, jnp.concatenate([hi, lo], axis=0), preferred_element_type=F32)

    for hh in range(nh):
        hs = slice(hh * d, (hh + 1) * d)
        q_ref[:, hs] = l2norm(conv_silu(0, cwq_ref, hs)) * (d ** -0.5)
        k_ref[:, hs] = l2norm(conv_silu(1, cwk_ref, hs))
        v_ref[:, hs] = conv_silu(2, cwv_ref, hs)

        a_raw = sm_ref[:, a_col + hh:a_col + hh + 1]
        b_raw = sm_ref[:, b_col + hh:b_col + hh + 1]
        g = -jnp.exp(alog_ref[:, hh:hh + 1]) * _softplus(a_raw + dtb_ref[:, hh:hh + 1])
        gb_ref[:, hs] = jnp.broadcast_to(g, (ts, d))
        bb_ref[:, hs] = jnp.broadcast_to(_sigmoid(b_raw), (ts, d))

        q = [q_ref[r, hs] for r in rs]
        k = [k_ref[r, hs] for r in rs]
        beta = [bb_ref[r, hs] for r in rs]
        gb = [gb_ref[r, hs] for r in rs]
        gc = [cum2(tri2, x) for x in gb]
        gc_row = [cum2(ones2, x[:, :c] * upper_f) for x in gb]
        decay = [jnp.where(tri, jnp.exp(jnp.minimum(a[:, :c] - b, 0.0)), 0.0) for a, b in zip(gc, gc_row)]
        kk = [_mm_nt(x, x) for x in k]
        n = [-jnp.where(strict, b[:, :c] * x * dc, 0.0) for b, x, dc in zip(beta, kk, decay)]
        inv = [eye + x for x in n]
        for step in range(5):
            nb = [x.astype(BF16) for x in n]
            n = [jnp.dot(x, x, preferred_element_type=F32) for x in nb]
            inv = [iv + _mm(iv, x) for iv, x in zip(inv, n)]
        egc = [jnp.exp(x) for x in gc]
        gl = [x[c - 1:c, :] for x in gc]
        inv_l = [x.astype(BF16) for x in inv]
        u = [_mm(a, v_ref[r, hs] * b) for a, r, b in zip(inv_l, rs, beta)]
        w = [_mm(a, x * (b * e)) for a, x, b, e in zip(inv_l, k, beta, egc)]
        qk = [_mm_nt(a, b) * dc for a, b, dc in zip(q, k, decay)]
        for ci in chunks:
            r = rs[ci]
            u_ref[r, hs] = u[ci]
            w_ref[r, hs] = w[ci]
            qk_ref[hh, r, :] = qk[ci]
            q_ref[r, hs] = q[ci] * egc[ci]
            k_ref[r, hs] = k[ci] * jnp.exp(gl[ci] - gc[ci])
            gb_ref[r, hs] = jnp.broadcast_to(jnp.exp(gl[ci]), (c, d))

    def chunk_body(ci, carry):
        r0 = pl.multiple_of(ci * c, c)
        rows = pl.ds(r0, c)
        hss = [slice(hh * d, (hh + 1) * d) for hh in range(nh)]
        st = [st_ref[hh] for hh in range(nh)]
        w_st = [_mm(w_ref[rows, hs], s_) for hs, s_ in zip(hss, st)]
        q_st = [_mm(q_ref[rows, hs], s_) for hs, s_ in zip(hss, st)]
        v_new = [u_ref[rows, hs] - x for hs, x in zip(hss, w_st)]
        o = [a + _mm(qk_ref[hh, rows, :], v) for hh, (a, v) in enumerate(zip(q_st, v_new))]
        kv = [_mm_tn(k_ref[rows, hs], v) for hs, v in zip(hss, v_new)]
        for hh, hs in enumerate(hss):
            st_ref[hh] = st[hh] * gb_ref[pl.ds(r0, 1), hs] + kv[hh]
            o_ref[rows, hs] = (_rms(o[hh], gnorm) * _silu(z_ref[rows, hs])).astype(o_ref.dtype)
        return carry

    lax.fori_loop(0, ts // c, chunk_body, 0)


def _deltanet(p32, conv_w, a_log, dt_bias, a_norm_g, *, ts, cols):
    bsz, s, _ = p32.shape
    d = HEAD_DIM
    nh = N_HEADS
    w = nh * d
    pad = lambda t: jnp.pad(t.astype(F32), (0, d - t.shape[0])).reshape(1, d)
    kernel = functools.partial(_deltanet_kernel, ts=ts, a_col=cols["a_lane"], b_col=cols["b_lane"])
    tile = lambda name: pl.BlockSpec((None, ts, w), lambda b, i: (b, i, cols[name] // nh))
    conv = lambda k: pl.BlockSpec((CONV_WIDTH, w), lambda b, i: (0, k))
    row = pl.BlockSpec((1, d), lambda b, i: (0, 0))
    return pl.pallas_call(
        kernel,
        grid=(bsz, s // ts),
        in_specs=[tile("qa"), tile("ka"), tile("va"), tile("za"),
                  pl.BlockSpec((None, ts, d), lambda b, i: (b, i, cols["small"])),
                  conv(0), conv(1), conv(2), row, row, row],
        out_specs=pl.BlockSpec((None, ts, w), lambda b, i: (b, i, 0)),
        out_shape=jax.ShapeDtypeStruct((bsz, s, w), BF16),
        scratch_shapes=[pltpu.VMEM((3, ts + 8, w), F32)]
        + [pltpu.VMEM((ts, w), F32) for _ in range(7)]
        + [pltpu.VMEM((nh, ts, CHUNK), F32), pltpu.VMEM((nh, d, d), F32)],
        compiler_params=pltpu.CompilerParams(
            dimension_semantics=("parallel", "arbitrary"), vmem_limit_bytes=VMEM_LIMIT),
        name="deltanet",
    )(p32, p32, p32, p32, p32, conv_w.astype(F32), conv_w.astype(F32), conv_w.astype(F32),
      pad(a_log), pad(dt_bias), a_norm_g.astype(F32).reshape(1, d))


def _hgrn2_kernel(q_ref, f_ref, i_ref, gate_ref, lb_ref, gn_ref, o_ref,
                  qs_ref, ks_ref, gc_ref, st_ref, *, ts):
    s = pl.program_id(1)
    c = CHUNK
    d = HEAD_DIM
    nh = N_HEADS
    SUB = 16

    @pl.when(s == 0)
    def _():
        st_ref[...] = jnp.zeros_like(st_ref)

    lb = lb_ref[...]
    f_raw = f_ref[...]
    log_sig = jnp.minimum(f_raw, 0.0) - jnp.log1p(jnp.exp(-jnp.abs(f_raw)))
    la = jnp.log(lb)
    lbb = jnp.log1p(-lb) + log_sig
    log_f = jnp.maximum(la, lbb) + jnp.log1p(jnp.exp(-jnp.abs(la - lbb)))
    qs_ref[...] = _silu(q_ref[...])
    ks_ref[...] = (1.0 - lb) * _sigmoid(-f_raw)

    row = _iota((c, c), 0)
    col = _iota((c, c), 1)
    tri_f = (col <= row).astype(F32)
    ones_dd = jnp.ones((d, d), BF16)
    rows_8d = _iota((8, d), 0)
    gnorm = gn_ref[...]

    tri2 = jnp.concatenate([tri_f, tri_f], axis=1).astype(BF16)
    for ci in range(ts // c):
        hi, lo = _split(log_f[ci * c:(ci + 1) * c, :])
        gc_ref[ci * c:(ci + 1) * c, :] = jnp.dot(tri2, jnp.concatenate([hi, lo], axis=0),
                                                 preferred_element_type=F32)

    blocks = [(sb * SUB, (sb + 1) * SUB) for sb in range(c // SUB)]

    def chunk_loop(ci, carry):
        r0 = pl.multiple_of(ci * c, c)
        rows = pl.ds(r0, c)
        hss = [slice(hh * d, (hh + 1) * d) for hh in range(nh)]
        q = [qs_ref[rows, hs] for hs in hss]
        k = [ks_ref[rows, hs] for hs in hss]
        v = [i_ref[rows, hs] for hs in hss]
        gc = [gc_ref[rows, hs] for hs in hss]

        def near_products(q, k, gc):
            prods = []
            for top, end in blocks:
                for j in range(top, end):
                    lo = (j // 8) * 8
                    e = jnp.exp2(gc[lo:end, :] - gc[j:j + 1, :])
                    if j % 8:
                        head = jnp.where(rows_8d >= j - lo, e[:8], 0.0)
                        e = jnp.concatenate([head, e[8:]], axis=0) if lo + 8 < end else head
                    prods.append(q[lo:end, :] * k[j:j + 1, :] * e)
            return jnp.concatenate(prods, axis=0).astype(BF16)

        def far_operands(q, k, gc):
            out = []
            for top, end in blocks[1:]:
                g_b = gc[top - 1:top, :]
                out.append((q[top:end, :] * jnp.exp(gc[top:end, :] - g_b),
                            k[:top, :] * jnp.exp(jnp.minimum(g_b - gc[:top, :], 0.0))))
            return out

        near = [near_products(a, b, g * LOG2E) for a, b, g in zip(q, k, gc)]
        far_ops = [far_operands(*x) for x in zip(q, k, gc)]
        st = [st_ref[hh] for hh in range(nh)]
        gl = [x[c - 1:c, :] for x in gc]
        sums = [jnp.dot(x, ones_dd, preferred_element_type=F32) for x in near]
        qk_far = [[_mm_nt(qe, ke) for qe, ke in ops] for ops in far_ops]
        far = [[_mm(a, vv[:top, :]) for a, (top, _) in zip(qs, blocks[1:])] for qs, vv in zip(qk_far, v)]
        o_st = [_mm_nt(a * jnp.exp(g), s_) for a, g, s_ in zip(q, gc, st)]
        kv = [_mm_tn(vv, kk * jnp.exp(g_l - g)) for vv, kk, g_l, g in zip(v, k, gl, gc)]

        for hh, hs in enumerate(hss):
            groups = [jnp.zeros((8, d), F32) for _ in range(c // 8)]
            at = 0
            for top, end in blocks:
                for j in range(top, end):
                    v_j = v[hh][j:j + 1, :]
                    for g in range(j // 8, end // 8):
                        groups[g] = groups[g] + sums[hh][at:at + 8, :] * v_j
                        at += 8
            for f, (top, end) in zip(far[hh], blocks[1:]):
                for g in range(top // 8, end // 8):
                    groups[g] = groups[g] + f[(g * 8 - top):(g * 8 - top + 8), :]
            o = jnp.concatenate(groups, axis=0) + o_st[hh]
            st_ref[hh] = st[hh] * jnp.exp(gl[hh]) + kv[hh]
            o_ref[rows, hs] = (_rms(o, gnorm) * _silu(gate_ref[rows, hs])).astype(o_ref.dtype)
        return carry

    lax.fori_loop(0, ts // c, chunk_loop, 0)


def _hgrn2(p32, lb, d_norm_g, *, ts, cols):
    bsz, s, _ = p32.shape
    d = HEAD_DIM
    nh = N_HEADS
    w = nh * d
    kernel = functools.partial(_hgrn2_kernel, ts=ts)
    tile = lambda name: pl.BlockSpec((None, ts, w), lambda b, i: (b, i, cols[name] // nh))
    return pl.pallas_call(
        kernel,
        grid=(bsz, s // ts),
        in_specs=[tile("qd"), tile("fd"), tile("id"), tile("gd"),
                  pl.BlockSpec((1, w), lambda b, i: (0, 0)),
                  pl.BlockSpec((1, d), lambda b, i: (0, 0))],
        out_specs=pl.BlockSpec((None, ts, w), lambda b, i: (b, i, 0)),
        out_shape=jax.ShapeDtypeStruct((bsz, s, w), BF16),
        scratch_shapes=[pltpu.VMEM((ts, w), F32), pltpu.VMEM((ts, w), F32),
                        pltpu.VMEM((ts, w), F32), pltpu.VMEM((nh, d, d), F32)],
        compiler_params=pltpu.CompilerParams(
            dimension_semantics=("parallel", "arbitrary"), vmem_limit_bytes=VMEM_LIMIT),
        name="hgrn2",
    )(p32, p32, p32, p32, lb.astype(F32).reshape(1, w), d_norm_g.astype(F32).reshape(1, d))


def _stickbreak_kernel(q_ref, k_ref, v_ref, o_ref, acc_ref, *, tq):
    i = pl.program_id(1)
    d = HEAD_DIM
    nh = N_HEADS
    row = _iota((tq, tq), 0)
    col = _iota((tq, tq), 1)
    causal = col < row
    later = (row > col).astype(BF16)
    later2 = jnp.concatenate([later, later], axis=0)

    heads = [slice(hh * d, (hh + 1) * d) for hh in range(nh)]

    def scores(blocks):
        jobs = [(j, dg, hs) for j, dg in blocks for hs in heads]
        z = [_mm_nt(q_ref[:, hs], k_ref[pl.ds(pl.multiple_of(j * tq, tq), tq), hs]) * (d ** -0.5)
             for j, _, hs in jobs]
        sp = [_softplus(x) for x in z]
        l1m = [jnp.where(causal, -x, 0.0) if dg else -x for x, (_, dg, _) in zip(sp, jobs)]
        rest = [jnp.dot(jnp.concatenate(_split(x), axis=1), later2, preferred_element_type=F32)
                for x in l1m]
        out = [((a - b) + r, l) for a, b, r, l in zip(z, sp, rest, l1m)]
        return [out[b * nh:(b + 1) * nh] for b in range(len(blocks))]

    def block(j, carries):
        (sc,) = scores([(j, False)])
        ps = [jnp.exp(logw + c) for (logw, _), c in zip(sc, carries)]
        pv = [_mm(p, v_ref[pl.ds(pl.multiple_of(j * tq, tq), tq), hs]) for p, hs in zip(ps, heads)]
        for hs, x in zip(heads, pv):
            acc_ref[:, hs] += x
        return tuple(c + jnp.sum(l1m, axis=-1, keepdims=True) for (_, l1m), c in zip(sc, carries))

    j1 = jnp.maximum(i - 1, 0)
    j2 = jnp.maximum(i - 2, 0)
    live1 = jnp.where(i > 0, 1.0, 0.0)
    live2 = jnp.where(i > 1, 1.0, 0.0)
    s0, s1, s2 = scores([(i, True), (j1, False), (j2, False)])
    carries = []
    for hh, hs in enumerate(heads):
        c0 = jnp.sum(s0[hh][1], axis=-1, keepdims=True)
        c1 = c0 + jnp.sum(s1[hh][1], axis=-1, keepdims=True)
        p0 = jnp.where(causal, jnp.exp(s0[hh][0]), 0.0)
        p1 = jnp.exp(s1[hh][0] + c0) * live1
        p2 = jnp.exp(s2[hh][0] + c1) * live2
        acc_ref[:, hs] = (_mm(p0, v_ref[pl.ds(pl.multiple_of(i * tq, tq), tq), hs])
                          + _mm(p1, v_ref[pl.ds(pl.multiple_of(j1 * tq, tq), tq), hs])
                          + _mm(p2, v_ref[pl.ds(pl.multiple_of(j2 * tq, tq), tq), hs]))
        carries.append(c1 + jnp.sum(s2[hh][1], axis=-1, keepdims=True))
    carries = tuple(carries)

    def cond(c):
        worst = functools.reduce(jnp.maximum, c[1])
        return jnp.logical_and(c[0] >= 0, jnp.max(worst) >= EXP_ZERO_BELOW)

    def body(c):
        return c[0] - 1, block(c[0], c[1])

    lax.while_loop(cond, body, (i - 3, carries))
    o_ref[...] = acc_ref[...].astype(o_ref.dtype)


def _stickbreak(p16, *, tq, cols):
    bsz, s, _ = p16.shape
    nh = N_HEADS
    w = nh * HEAD_DIM
    kernel = functools.partial(_stickbreak_kernel, tq=tq)
    resident = dict(pipeline_mode=pl.Buffered(1))
    return pl.pallas_call(
        kernel,
        grid=(bsz, s // tq),
        in_specs=[pl.BlockSpec((None, tq, w), lambda b, i: (b, i, cols["qc"] // nh)),
                  pl.BlockSpec((None, s, w), lambda b, i: (b, 0, cols["kc"] // nh), **resident),
                  pl.BlockSpec((None, s, w), lambda b, i: (b, 0, cols["vc"] // nh), **resident)],
        out_specs=pl.BlockSpec((None, tq, w), lambda b, i: (b, i, 0)),
        out_shape=jax.ShapeDtypeStruct((bsz, s, w), BF16),
        scratch_shapes=[pltpu.VMEM((tq, w), F32)],
        compiler_params=pltpu.CompilerParams(
            dimension_semantics=("parallel", "arbitrary"), vmem_limit_bytes=VMEM_LIMIT),
        name="stickbreak",
    )(p16, p16, p16)


def _dsa_kernel(qi_ref, smq_ref, q_ref, sm_ref, k_ref, vt_ref, bias_ref, o_ref,
                sc_ref, scb_ref, qct_ref, kc_ref, bd_ref, lg_ref, *, tq, k_sel, wi_lane, wide):
    i = pl.program_id(1)
    tk = tq
    d = HEAD_DIM
    nh = N_HEADS
    ksel = float(k_sel)
    per_wide = wide // tk
    n_wide = (i + per_wide) // per_wide
    sub = 2 * tk
    lane_q = _iota((1, tq), 1)

    def tree(parts, op):
        while len(parts) > 1:
            parts = [op(parts[j], parts[j + 1]) if j + 1 < len(parts) else parts[j]
                     for j in range(0, len(parts), 2)]
        return parts[0]

    def col_fold(x, op=jnp.add, rows=8):
        return tree([x[r * rows:(r + 1) * rows] for r in range(x.shape[0] // rows)], op)

    @pl.when(i == 0)
    def _():
        def prep(g, carry):
            g0 = pl.multiple_of(g * wide, wide)
            hi, lo = _split(sm_ref[pl.ds(g0, wide), :][:, :IDX_DIM])
            kc_ref[pl.ds(g0, wide), :] = jnp.concatenate([hi, lo, hi], axis=1)
            return carry
        lax.fori_loop(0, sm_ref.shape[0] // wide, prep, 0)

    qit = qi_ref[...].T
    for p in range(IDX_HEADS // 2):
        halves = []
        for hh in (2 * p, 2 * p + 1):
            hi, lo = _split(qit[hh * IDX_DIM:(hh + 1) * IDX_DIM, :])
            halves.append(jnp.concatenate([hi, hi, lo], axis=0))
        qct_ref[p] = jnp.concatenate(halves, axis=1)
    w_rows = smq_ref[...].T[wi_lane:wi_lane + IDX_HEADS, :] * ((IDX_HEADS ** -0.5) * (IDX_DIM ** -0.5))

    q2t = (q_ref[...] * ((d ** -0.5) * LOG2E)).T.astype(BF16)
    zero_dq = jnp.zeros((d, tq), BF16)
    for p in range(nh // 2):
        top = jnp.concatenate([q2t[2 * p * d:(2 * p + 1) * d], zero_dq], axis=1)
        bot = jnp.concatenate([zero_dq, q2t[(2 * p + 1) * d:(2 * p + 2) * d]], axis=1)
        bd_ref[p] = jnp.concatenate([top, bot], axis=0)

    limit = i * tq + (lane_q // CHUNK + 1) * CHUNK

    rows_s = _iota((sub, tq), 0)

    def score_groups(gs, mm, masked):
        mn, mx = mm
        k0s = [pl.multiple_of(g * wide + sb * sub, sub) for g in gs for sb in range(wide // sub)]
        keys = [kc_ref[pl.ds(k0, sub), :] for k0 in k0s]
        accs = [jnp.zeros((sub, tq), F32) for _ in k0s]
        for p in range(IDX_HEADS // 2):
            rhs = qct_ref[p]
            for n, kk in enumerate(keys):
                s2 = jnp.dot(kk, rhs, preferred_element_type=F32)
                accs[n] = (accs[n] + jnp.maximum(s2[:, :tq], 0.0) * w_rows[2 * p:2 * p + 1, :]
                           + jnp.maximum(s2[:, tq:], 0.0) * w_rows[2 * p + 1:2 * p + 2, :])
        for k0, sct in zip(k0s, accs):
            if masked:
                adm = (k0 + rows_s) < limit
                mn = jnp.minimum(mn, col_fold(jnp.where(adm, sct, jnp.inf), jnp.minimum))
                sct = jnp.where(adm, sct, -jnp.inf)
            else:
                mn = jnp.minimum(mn, col_fold(sct, jnp.minimum))
            mx = jnp.maximum(mx, col_fold(sct, jnp.maximum))
            sc_ref[pl.ds(k0, sub), :] = sct
            scb_ref[pl.ds(k0, sub), :] = _floor_bf16(sct)
        return mn, mx

    def score_pair(j, mm):
        return score_groups((2 * j, 2 * j + 1), mm, False)

    n_full = n_wide - 1
    mm = lax.fori_loop(0, n_full // 2, score_pair,
                       (jnp.full((8, tq), jnp.inf, F32), jnp.full((8, tq), -jnp.inf, F32)))
    mm = lax.cond(n_full % 2 == 1, lambda c: score_groups((n_full - 1,), c, False), lambda c: c, mm)
    mn, mx = score_groups((n_wide - 1,), mm, True)

    n_pairs = (n_wide + 1) // 2

    @pl.when(n_wide % 2 == 1)
    def _():
        sc_ref[pl.ds(pl.multiple_of(n_wide * wide, wide), wide), :] = jnp.full((wide, tq), -jnp.inf, F32)
        scb_ref[pl.ds(pl.multiple_of(n_wide * wide, wide), wide), :] = jnp.full((wide, tq), -jnp.inf, BF16)
    rmin = jnp.min(mn, axis=0, keepdims=True)
    rmax = jnp.max(mx, axis=0, keepdims=True)

    def count(pred):
        def body(j, acc):
            for g in (2 * j, 2 * j + 1):
                acc = acc + col_fold(pred(sc_ref[pl.ds(pl.multiple_of(g * wide, wide), wide), :]))
            return acc
        return jnp.sum(lax.fori_loop(0, n_pairs, body, jnp.zeros((8, tq), F32)), axis=0, keepdims=True)

    def max_below(x):
        def body(j, acc):
            for g in (2 * j, 2 * j + 1):
                blk = sc_ref[pl.ds(pl.multiple_of(g * wide, wide), wide), :]
                acc = jnp.maximum(acc, col_fold(jnp.where(blk < x, blk, -jnp.inf), jnp.maximum))
            return acc
        return jnp.max(lax.fori_loop(0, n_pairs, body, jnp.full((8, tq), -jnp.inf, F32)), axis=0, keepdims=True)

    n_adm = limit.astype(F32)
    all_sel = n_adm <= ksel

    def bisect(c):
        lo, hi, c_lo = c
        mid = 0.5 * lo + 0.5 * hi
        cm = count(lambda blk: _ind(blk >= mid))
        ge = cm >= ksel
        return jnp.where(ge, mid, lo), jnp.where(ge, hi, mid), jnp.where(ge, cm, c_lo)

    def pending(c_lo, tied):
        return jnp.where(all_sel, 0.0, jnp.where(tied > 0.5, 0.0, _ind(c_lo != ksel)))

    def bisect_coarse(_, c):
        lo, hi, c_lo = c
        mid = _floor_bf16(0.5 * lo + 0.5 * hi).astype(F32)
        t_b = jnp.broadcast_to(mid, (16, tq)).astype(BF16)
        one_b = jnp.ones((16, tq), BF16)
        zero_b = jnp.zeros((16, tq), BF16)

        def body(j, acc):
            for g in (2 * j, 2 * j + 1):
                blk = scb_ref[pl.ds(pl.multiple_of(g * wide, wide), wide), :]
                ind = [jnp.where(blk[r * 16:(r + 1) * 16] >= t_b, one_b, zero_b) for r in range(wide // 16)]
                acc = acc + tree(ind, jnp.add).astype(F32)
            return acc

        acc = lax.fori_loop(0, n_pairs, body, jnp.zeros((16, tq), F32))
        cm = jnp.sum(acc, axis=0, keepdims=True)
        ge = cm >= ksel
        return jnp.where(ge, mid, lo), jnp.where(ge, hi, mid), jnp.where(ge, cm, c_lo)

    lo0 = _floor_bf16(rmin).astype(F32)
    hi0 = _floor_bf16(rmax + (jnp.abs(rmax) * (2.0 ** -6) + 1e-30)).astype(F32)
    state = lax.fori_loop(0, BISECT_COARSE, bisect_coarse, (lo0, hi0, n_adm))
    state = lax.fori_loop(0, BISECT_FIXED, lambda _, c: bisect(c), state)

    def round_cond(c):
        return jnp.max(pending(c[0][2], c[1])) > 0.5

    def round_body(c):
        st, tied, v, need = c

        def more_cond(s):
            return jnp.logical_and(s[0] < BISECT_EXTRA, jnp.max(pending(s[1][2], tied)) > 0.5)

        _, st = lax.while_loop(more_cond, lambda s: (s[0] + 1, bisect(s[1])), (jnp.int32(0), st))
        pend = pending(st[2], tied)

        def check(_):
            cand = max_below(st[1])
            c_ge = count(lambda blk: _ind(blk >= cand))
            c_gt = count(lambda blk: _ind(blk > cand))
            ok = jnp.where(pend > 0.5, _ind(c_ge >= ksel), 0.0)
            return (jnp.where(ok > 0.5, 1.0, tied), jnp.where(ok > 0.5, cand, v),
                    jnp.where(ok > 0.5, ksel - c_gt, need))

        tied, v, need = lax.cond(jnp.max(pend) > 0.5, check, lambda _: (tied, v, need), 0)
        return st, tied, v, need

    zeros1 = jnp.zeros((1, tq), F32)
    (lo_f, _, _), tied, v_tie, need = lax.while_loop(round_cond, round_body, (state, zeros1, zeros1, zeros1))
    vth = jnp.where(all_sel, F32_LOWEST, jnp.where(tied > 0.5, v_tie, lo_f))

    @pl.when(jnp.max(tied) > 0.5)
    def _():
        v_eq = jnp.where(tied > 0.5, v_tie, jnp.inf)
        incl = (_iota((tk, tk), 1) <= _iota((tk, tk), 0)).astype(BF16)

        def demote(g, seen):
            g0 = pl.multiple_of(g * wide, wide)
            xs = [sc_ref[pl.ds(g0 + pb * tk, tk), :] for pb in range(per_wide)]
            eqs = [_ind(x == v_eq) for x in xs]
            inblk = [jnp.dot(incl, e.astype(BF16), preferred_element_type=F32) for e in eqs]
            for pb in range(per_wide):
                rank = inblk[pb] + seen
                sc_ref[pl.ds(g0 + pb * tk, tk), :] = jnp.where(eqs[pb] * _ind(rank > need) > 0.5,
                                                               -jnp.inf, xs[pb])
                seen = seen + jnp.sum(col_fold(eqs[pb]), axis=0, keepdims=True)
            return seen

        lax.fori_loop(0, n_wide, demote, zeros1)

    g_near = jnp.maximum(i - 1, 0) // per_wide

    def logit_group(g, mx, near):
        out = list(mx)
        for sb in range(wide // sub):
            k0 = pl.multiple_of(g * wide + sb * sub, sub)
            sel = sc_ref[pl.ds(k0, sub), :] >= vth
            for p in range(nh // 2):
                pair = jnp.dot(k_ref[pl.ds(k0, sub), 2 * p * d:(2 * p + 2) * d], bd_ref[p],
                               preferred_element_type=F32)
                for hh in (2 * p, 2 * p + 1):
                    lm = pair[:, (hh - 2 * p) * tq:(hh - 2 * p + 1) * tq]
                    if near:
                        back = [jnp.clip(i - (g * per_wide + sb * (sub // tk) + pb), 0, 2)
                                for pb in range(sub // tk)]
                        lm = lm + jnp.concatenate([bias_ref[bk, hh] for bk in back], axis=0)
                    lm = jnp.where(sel, lm, NEG_BIG)
                    lg_ref[hh, pl.ds(k0, sub), :] = lm
                    out[hh] = jnp.maximum(out[hh], col_fold(lm, jnp.maximum))
        return tuple(out)

    mx = tuple(jnp.full((8, tq), NEG_BIG, F32) for _ in range(nh))
    def logit_pair(j, mx, near):
        return logit_group(2 * j + 1, logit_group(2 * j, mx, near), near)

    far_pairs = g_near // 2
    mx = lax.fori_loop(0, far_pairs, functools.partial(logit_pair, near=False), mx)
    mx = lax.fori_loop(far_pairs, n_pairs, functools.partial(logit_pair, near=True), mx)
    m_q = [jnp.max(mx[hh], axis=0, keepdims=True) for hh in range(nh)]

    ones_rows = jnp.ones((8, wide), BF16)

    def pv_pair(j, carry):
        ls, accs = list(carry[0]), list(carry[1])
        jobs = [(pl.multiple_of(g * wide, wide), hh) for g in (2 * j, 2 * j + 1) for hh in range(nh)]
        ps = [jnp.exp2(lg_ref[hh, pl.ds(g0, wide), :] - m_q[hh]).astype(BF16) for g0, hh in jobs]
        outs = [jnp.dot(jnp.concatenate([vt_ref[hh * d:(hh + 1) * d, pl.ds(g0, wide)], ones_rows], axis=0),
                        p, preferred_element_type=F32) for (g0, hh), p in zip(jobs, ps)]
        for (_, hh), out in zip(jobs, outs):
            ls[hh] = ls[hh] + out[d:]
            accs[hh] = accs[hh] + out[:d]
        return tuple(ls), tuple(accs)

    ls, accs = lax.fori_loop(0, n_pairs, pv_pair,
                             (tuple(jnp.zeros((8, tq), F32) for _ in range(nh)),
                              tuple(jnp.zeros((d, tq), F32) for _ in range(nh))))
    for hh in range(nh):
        o_ref[:, hh * d:(hh + 1) * d] = (accs[hh] / ls[hh][0:1]).T.astype(o_ref.dtype)


def _dsa(p32, p16, vt, bias_tiles, *, tq, cols):
    bsz, s, _ = p32.shape
    d = HEAD_DIM
    nh = N_HEADS
    wide = 4 * tq
    k_sel = min(TOPK_MAX, s // 4)
    w512 = nh * d
    kernel = functools.partial(_dsa_kernel, tq=tq, k_sel=k_sel, wi_lane=cols["wi_lane"], wide=wide)
    resident = dict(pipeline_mode=pl.Buffered(1))
    return pl.pallas_call(
        kernel,
        grid=(bsz, s // tq),
        in_specs=[pl.BlockSpec((None, tq, w512), lambda b, i: (b, i, cols["qi"] // nh)),
                  pl.BlockSpec((None, tq, d), lambda b, i: (b, i, cols["small"])),
                  pl.BlockSpec((None, tq, w512), lambda b, i: (b, i, cols["qb"] // nh)),
                  pl.BlockSpec((None, s, d), lambda b, i: (b, 0, cols["small"]), **resident),
                  pl.BlockSpec((None, s, w512), lambda b, i: (b, 0, cols["kb"] // nh), **resident),
                  pl.BlockSpec((w512, s), lambda b, i: (0, b), **resident),
                  pl.BlockSpec((3, nh, tq, tq), lambda b, i: (0, 0, 0, 0), **resident)],
        out_specs=pl.BlockSpec((None, tq, w512), lambda b, i: (b, i, 0)),
        out_shape=jax.ShapeDtypeStruct((bsz, s, w512), BF16),
        scratch_shapes=[pltpu.VMEM((s, tq), F32),
                        pltpu.VMEM((s, tq), BF16),
                        pltpu.VMEM((IDX_HEADS // 2, 3 * IDX_DIM, 2 * tq), BF16),
                        pltpu.VMEM((s, 3 * IDX_DIM), BF16),
                        pltpu.VMEM((nh // 2, 2 * d, 2 * tq), BF16),
                        pltpu.VMEM((nh, s, tq), F32)],
        compiler_params=pltpu.CompilerParams(
            dimension_semantics=("parallel", "arbitrary"), vmem_limit_bytes=VMEM_LIMIT),
        name="dsa",
    )(p32, p32, p32, p32, p16, vt, bias_tiles)


def _t5_bucket(rel):
    nb = REL_BUCKETS // 2
    max_exact = nb // 2
    ret = jnp.where(rel > 0, nb, 0)
    n = jnp.abs(rel)
    large = max_exact + (jnp.log(jnp.maximum(n, 1).astype(F32) / max_exact)
                         / math.log(REL_MAX_DIST / max_exact) * (nb - max_exact)).astype(jnp.int32)
    large = jnp.minimum(large, nb - 1)
    return ret + jnp.where(n < max_exact, n, large)


def _bias_tiles(rel_table, tq):
    assert tq >= REL_MAX_DIST
    t = jnp.arange(tq)
    back = jnp.arange(3)
    rel = (t[None, None, :] - back[:, None, None] * tq) - t[None, :, None]
    onehot = (_t5_bucket(rel)[..., None] == jnp.arange(REL_BUCKETS)).astype(F32)
    tiles = jnp.einsum("bqkn,nh->bhkq", onehot, rel_table.astype(F32),
                       precision=HIGHEST)
    return (tiles - tiles[2:3]) * LOG2E


def _even_layout(w_in):
    d = HEAD_DIM
    a_w = 2 * N_HEADS * d + N_HEADS * d
    offs = {}
    o = 0
    for name, w in (("qkv", a_w), ("z", N_HEADS * d), ("a", N_HEADS), ("b", N_HEADS),
                    ("qb", N_HEADS * d), ("kb", N_HEADS * d), ("vb", N_HEADS * d),
                    ("qi", IDX_HEADS * IDX_DIM), ("ki", IDX_DIM), ("wi", IDX_HEADS)):
        offs[name] = (o, o + w)
        o += w
    assert o == w_in.shape[1]
    sl = lambda n: w_in[:, offs[n][0]:offs[n][1]]
    small_w = IDX_DIM + 2 * N_HEADS + IDX_HEADS
    small_pad = -small_w % d
    zeros = lambda n: jnp.zeros((w_in.shape[0], n), w_in.dtype)
    w32 = jnp.concatenate([sl("qkv"), sl("z"), sl("qb"), sl("qi"),
                           sl("ki"), sl("a"), sl("b"), sl("wi"), zeros(small_pad)], axis=1)
    n32 = w32.shape[1]
    tn = n32 // 5
    assert tn * 5 == n32 and tn % d == 0
    w16 = jnp.concatenate([sl("kb"), zeros(tn - N_HEADS * d)], axis=1)
    nh = N_HEADS
    cols = dict(qa=0, ka=nh, va=2 * nh, za=3 * nh, qb=4 * nh, qi=5 * nh, small=6 * nh, kb=0,
                a_lane=IDX_DIM, b_lane=IDX_DIM + nh, wi_lane=IDX_DIM + 2 * nh, n32=n32, tn=tn)
    return jnp.concatenate([w32, w16], axis=1).astype(BF16), sl("vb").T.astype(BF16), cols


def kernel(x, norm_g, w_in_even, conv_w_even, a_log_even, dt_bias_even, a_norm_even, w_out_even,
           rel_bias, w_in_odd, lb_logits, d_norm_odd, w_out_odd, w_gate, w_up, w_down):
    bsz, s, d = x.shape
    t = bsz * s
    depth = norm_g.shape[0]
    nh = N_HEADS
    tq = Q_TILE
    lb_all = jnp.cumsum(jax.nn.softmax(lb_logits.astype(F32), axis=0), axis=0)
    lb_all = lb_all - lb_all[:1]
    odd_cols = dict(qc=0, kc=nh, vc=2 * nh, qd=0, fd=nh, id=2 * nh, gd=3 * nh)
    bias_tiles = _bias_tiles(rel_bias, tq)

    h = x.reshape(t, d)
    for l in range(depth):
        if l % 2 == 0:
            e = l // 2
            w_even, w_vt, cols = _even_layout(w_in_even[e])
            p32, p16, vt = _norm_matmul(h, norm_g[l, 0], w_even, tm=PROJ_TILE, tn=cols["tn"], n32=cols["n32"],
                                        w_t=w_vt)
            p32 = p32.reshape(bsz, s, -1)
            p16 = p16.reshape(bsz, s, -1)
            o_1 = _deltanet(p32, conv_w_even[e], a_log_even[e], dt_bias_even[e], a_norm_even[e],
                            ts=min(DELTANET_TILE, s), cols=cols)
            o_2 = _dsa(p32, p16, vt, bias_tiles, tq=tq, cols=cols)
            w_out = w_out_even[e]
        else:
            o = l // 2
            n16 = 3 * nh * HEAD_DIM
            w_odd = jnp.concatenate([w_in_odd[o][:, n16:], w_in_odd[o][:, :n16]], axis=1).astype(BF16)
            p32, p16 = _norm_matmul(h, norm_g[l, 0], w_odd, tm=PROJ_TILE, tn=ODD_COL_TILE, n32=w_odd.shape[1] - n16)
            p32 = p32.reshape(bsz, s, -1)
            p16 = p16.reshape(bsz, s, -1)
            o_1 = _stickbreak(p16, tq=tq, cols=odd_cols)
            o_2 = _hgrn2(p32, lb_all[l], d_norm_odd[o], ts=min(SEQ_TILE, s), cols=odd_cols)
            w_out = w_out_odd[o]
        h = _mix_ffn(o_1.reshape(t, -1), o_2.reshape(t, -1), w_out, h, norm_g[l, 1], norm_g[l, 2], norm_g[l, 3],
                     w_gate[l], w_up[l], w_down[l], tm=ROW_TILE, tf=FFN_TILE)
    return h.reshape(bsz, s, d)
```

```python
import functools
import math

import jax
import jax.numpy as jnp
from jax import lax
from jax.experimental import pallas as pl
from jax.experimental.pallas import tpu as pltpu

F32 = jnp.float32
BF16 = jnp.bfloat16
HIGHEST = lax.Precision.HIGHEST

CHUNK = 64
HEAD_DIM = 128
N_HEADS = 4
IDX_HEADS = 8
IDX_DIM = 64
TOPK_MAX = 256
CONV_WIDTH = 4
REL_BUCKETS = 32
REL_MAX_DIST = 128
EPS = 1e-6
NEG_BIG = -1e30
LOG2E = 1.4426950408889634
BISECT_COARSE = 8
BISECT_FIXED = 8
BISECT_EXTRA = 6
F32_LOWEST = -3.4028234663852886e38
EXP_ZERO_BELOW = -104.0
VMEM_LIMIT = 56 * 1024 * 1024

PROJ_TILE = 2048
ROW_TILE = 512
DELTANET_TILE = 1024
SEQ_TILE = 512
Q_TILE = 128
ODD_COL_TILE = 512
FFN_TILE = 2816


def _mm(a, b):
    return jnp.dot(a.astype(BF16), b.astype(BF16), preferred_element_type=F32)


def _mm_nt(a, b):
    return lax.dot_general(a.astype(BF16), b.astype(BF16), (((1,), (1,)), ((), ())),
                           preferred_element_type=F32)


def _mm_tn(a, b):
    return lax.dot_general(a.astype(BF16), b.astype(BF16), (((0,), (0,)), ((), ())),
                           preferred_element_type=F32)


def _split(x):
    hi = x.astype(BF16)
    return hi, (x - hi.astype(F32)).astype(BF16)


def _floor_bf16(x):
    bits = pltpu.bitcast(x, jnp.int32)
    down = jnp.where(bits >= 0, bits, bits + 0xFFFF) & jnp.int32(-65536)
    return pltpu.bitcast(down, F32).astype(BF16)


def _sigmoid(x):
    return 1.0 / (1.0 + jnp.exp(-x))


def _silu(x):
    return x * _sigmoid(x)


def _softplus(x):
    return jnp.maximum(x, 0.0) + jnp.log1p(jnp.exp(-jnp.abs(x)))


def _rms(x, g):
    return x * lax.rsqrt(jnp.mean(x * x, axis=-1, keepdims=True) + EPS) * g


def _iota(shape, dim):
    return lax.broadcasted_iota(jnp.int32, shape, dim)


def _ind(mask):
    return jnp.where(mask, 1.0, 0.0)


def _norm_matmul_kernel(x_ref, g_ref, w_ref, *rest, n_t, tiles32):
    if n_t:
        wt_ref, o32_ref, o16_ref, ot_ref, xn_ref = rest
    else:
        o32_ref, o16_ref, xn_ref = rest
    j = pl.program_id(1)

    @pl.when(j == 0)
    def _():
        xn_ref[...] = _rms(x_ref[...], g_ref[...]).astype(BF16)
        if n_t:
            ot_ref[...] = lax.dot_general(wt_ref[...], xn_ref[...], (((1,), (1,)), ((), ())),
                                          preferred_element_type=F32).astype(BF16)

    y = jnp.dot(xn_ref[...], w_ref[...], preferred_element_type=F32)

    @pl.when(j < tiles32)
    def _():
        o32_ref[...] = y

    @pl.when(j >= tiles32)
    def _():
        o16_ref[...] = y.astype(BF16)


def _norm_matmul(x, g, w, *, tm, tn, n32, w_t=None):
    t, d = x.shape
    n = w.shape[1]
    n_t = 0 if w_t is None else w_t.shape[0]
    tiles32 = n32 // tn
    assert tiles32 * tn == n32 and (n - n32) % tn == 0 and 0 < n32 < n
    in_specs = [pl.BlockSpec((tm, d), lambda i, j: (i, 0)),
                pl.BlockSpec((1, d), lambda i, j: (0, 0)),
                pl.BlockSpec((d, tn), lambda i, j: (0, j))]
    out_specs = [pl.BlockSpec((tm, tn), lambda i, j: (i, jnp.minimum(j, tiles32 - 1))),
                 pl.BlockSpec((tm, tn), lambda i, j: (i, jnp.maximum(j - tiles32, 0)))]
    out_shape = [jax.ShapeDtypeStruct((t, n32), F32), jax.ShapeDtypeStruct((t, n - n32), BF16)]
    args = [x, g.reshape(1, d), w]
    if n_t:
        in_specs.append(pl.BlockSpec((n_t, d), lambda i, j: (0, 0)))
        out_specs.append(pl.BlockSpec((n_t, tm), lambda i, j: (0, i)))
        out_shape.append(jax.ShapeDtypeStruct((n_t, t), BF16))
        args.append(w_t)
    return pl.pallas_call(
        functools.partial(_norm_matmul_kernel, n_t=n_t, tiles32=tiles32),
        grid=(t // tm, n // tn),
        in_specs=in_specs,
        out_specs=out_specs,
        out_shape=out_shape,
        scratch_shapes=[pltpu.VMEM((tm, d), BF16)],
        compiler_params=pltpu.CompilerParams(
            dimension_semantics=("parallel", "arbitrary"), vmem_limit_bytes=VMEM_LIMIT),
        name="norm_matmul",
    )(*args)


def _mix_ffn_kernel(ca_ref, cb_ref, wa_ref, wb_ref, h_ref, gmix_ref, gpre_ref, gpost_ref,
                    wg_ref, wu_ref, wd_ref, o_ref, h1_ref, xn_ref, acc_ref):
    f = pl.program_id(1)

    @pl.when(f == 0)
    def _():
        y = (jnp.dot(ca_ref[...], wa_ref[...], preferred_element_type=F32)
             + jnp.dot(cb_ref[...], wb_ref[...], preferred_element_type=F32))
        h1 = h_ref[...] + _rms(y, gmix_ref[...])
        h1_ref[...] = h1
        xn_ref[...] = _rms(h1, gpre_ref[...]).astype(BF16)
        acc_ref[...] = jnp.zeros_like(acc_ref)

    xn = xn_ref[...]
    gate = jnp.dot(xn, wg_ref[...], preferred_element_type=F32)
    up = jnp.dot(xn, wu_ref[...], preferred_element_type=F32)
    act = (_silu(gate) * up).astype(BF16)
    acc_ref[...] += jnp.dot(act, wd_ref[...], preferred_element_type=F32)

    @pl.when(f == pl.num_programs(1) - 1)
    def _():
        o_ref[...] = h1_ref[...] + _rms(acc_ref[...], gpost_ref[...])


def _mix_ffn(ca, cb, w_out, h, g_mix, g_pre, g_post, wg, wu, wd, *, tm, tf):
    t, d = h.shape
    ff = wg.shape[1]
    wa_n = ca.shape[1]
    wb_n = cb.shape[1]
    row = pl.BlockSpec((1, d), lambda i, f: (0, 0))
    once = dict(pipeline_mode=pl.Buffered(1)) if tf == ff else {}
    return pl.pallas_call(
        _mix_ffn_kernel,
        grid=(t // tm, ff // tf),
        in_specs=[pl.BlockSpec((tm, wa_n), lambda i, f: (i, 0)),
                  pl.BlockSpec((tm, wb_n), lambda i, f: (i, 0)),
                  pl.BlockSpec((wa_n, d), lambda i, f: (0, 0)),
                  pl.BlockSpec((wb_n, d), lambda i, f: (0, 0)),
                  pl.BlockSpec((tm, d), lambda i, f: (i, 0)),
                  row, row, row,
                  pl.BlockSpec((d, tf), lambda i, f: (0, f), **once),
                  pl.BlockSpec((d, tf), lambda i, f: (0, f), **once),
                  pl.BlockSpec((tf, d), lambda i, f: (f, 0), **once)],
        out_specs=pl.BlockSpec((tm, d), lambda i, f: (i, 0)),
        out_shape=jax.ShapeDtypeStruct((t, d), F32),
        scratch_shapes=[pltpu.VMEM((tm, d), F32), pltpu.VMEM((tm, d), BF16), pltpu.VMEM((tm, d), F32)],
        compiler_params=pltpu.CompilerParams(
            dimension_semantics=("parallel", "arbitrary"), vmem_limit_bytes=VMEM_LIMIT),
        name="mix_ffn",
    )(ca, cb, w_out[:wa_n].astype(BF16), w_out[wa_n:].astype(BF16), h,
      g_mix.reshape(1, d), g_pre.reshape(1, d), g_post.reshape(1, d),
      wg.astype(BF16), wu.astype(BF16), wd.astype(BF16))


def _deltanet_kernel(xq_ref, xk_ref, xv_ref, z_ref, sm_ref, cwq_ref, cwk_ref, cwv_ref,
                     alog_ref, dtb_ref, gn_ref, o_ref,
                     xpad_ref, q_ref, k_ref, v_ref, gb_ref, bb_ref, u_ref, w_ref, qk_ref, st_ref,
                     *, ts, a_col, b_col):
    s = pl.program_id(1)
    c = CHUNK
    d = HEAD_DIM
    nh = N_HEADS

    @pl.when(s == 0)
    def _():
        xpad_ref[:, 0:8, :] = jnp.zeros((3, 8, nh * d), F32)
        st_ref[...] = jnp.zeros_like(st_ref)

    @pl.when(s != 0)
    def _():
        xpad_ref[:, 0:8, :] = xpad_ref[:, ts:ts + 8, :]

    xpad_ref[0, 8:ts + 8, :] = xq_ref[...]
    xpad_ref[1, 8:ts + 8, :] = xk_ref[...]
    xpad_ref[2, 8:ts + 8, :] = xv_ref[...]

    def conv_silu(idx, cw_ref, hs):
        cw = cw_ref[:, hs]
        acc = xpad_ref[idx, 8 - (CONV_WIDTH - 1):8 - (CONV_WIDTH - 1) + ts, hs] * cw[0:1, :]
        for j in range(1, CONV_WIDTH):
            off = 8 - (CONV_WIDTH - 1) + j
            acc = acc + xpad_ref[idx, off:off + ts, hs] * cw[j:j + 1, :]
        return _silu(acc)

    def l2norm(t):
        return t * lax.rsqrt(jnp.sum(t * t, axis=-1, keepdims=True) + EPS)

    row = _iota((c, c), 0)
    col = _iota((c, c), 1)
    tri = (col <= row)
    strict = (col < row)
    tri_f = tri.astype(F32)
    upper_f = (row <= col).astype(F32)
    eye = (row == col).astype(F32)
    gnorm = gn_ref[...]
    chunks = range(ts // c)
    rs = [slice(ci * c, (ci + 1) * c) for ci in chunks]
    tri2 = jnp.concatenate([tri_f, tri_f], axis=1).astype(BF16)
    ones2 = jnp.ones((c, 2 * c), BF16)

    def cum2(lhs2, x):
        hi, lo = _split(x)
        return jnp.dot(lhs2, jnp.concatenate([hi, lo], axis=0), preferred_element_type=F32)

    for hh in range(nh):
        hs = slice(hh * d, (hh + 1) * d)
        q_ref[:, hs] = l2norm(conv_silu(0, cwq_ref, hs)) * (d ** -0.5)
        k_ref[:, hs] = l2norm(conv_silu(1, cwk_ref, hs))
        v_ref[:, hs] = conv_silu(2, cwv_ref, hs)

        a_raw = sm_ref[:, a_col + hh:a_col + hh + 1]
        b_raw = sm_ref[:, b_col + hh:b_col + hh + 1]
        g = -jnp.exp(alog_ref[:, hh:hh + 1]) * _softplus(a_raw + dtb_ref[:, hh:hh + 1])
        gb_ref[:, hs] = jnp.broadcast_to(g, (ts, d))
        bb_ref[:, hs] = jnp.broadcast_to(_sigmoid(b_raw), (ts, d))

        q = [q_ref[r, hs] for r in rs]
        k = [k_ref[r, hs] for r in rs]
        beta = [bb_ref[r, hs] for r in rs]
        gb = [gb_ref[r, hs] for r in rs]
        gc = [cum2(tri2, x) for x in gb]
        gc_row = [cum2(ones2, x[:, :c] * upper_f) for x in gb]
        decay = [jnp.where(tri, jnp.exp(jnp.minimum(a[:, :c] - b, 0.0)), 0.0) for a, b in zip(gc, gc_row)]
        kk = [_mm_nt(x, x) for x in k]
        n = [-jnp.where(strict, b[:, :c] * x * dc, 0.0) for b, x, dc in zip(beta, kk, decay)]
        inv = [eye + x for x in n]
        for step in range(5):
            nb = [x.astype(BF16) for x in n]
            n = [jnp.dot(x, x, preferred_element_type=F32) for x in nb]
            inv = [iv + _mm(iv, x) for iv, x in zip(inv, n)]
        egc = [jnp.exp(x) for x in gc]
        gl = [x[c - 1:c, :] for x in gc]
        inv_l = [x.astype(BF16) for x in inv]
        u = [_mm(a, v_ref[r, hs] * b) for a, r, b in zip(inv_l, rs, beta)]
        w = [_mm(a, x * (b * e)) for a, x, b, e in zip(inv_l, k, beta, egc)]
        qk = [_mm_nt(a, b) * dc for a, b, dc in zip(q, k, decay)]
        for ci in chunks:
            r = rs[ci]
            u_ref[r, hs] = u[ci]
            w_ref[r, hs] = w[ci]
            qk_ref[hh, r, :] = qk[ci]
            q_ref[r, hs] = q[ci] * egc[ci]
            k_ref[r, hs] = k[ci] * jnp.exp(gl[ci] - gc[ci])
            gb_ref[r, hs] = jnp.broadcast_to(jnp.exp(gl[ci]), (c, d))

    def chunk_body(ci, carry):
        r0 = pl.multiple_of(ci * c, c)
        rows = pl.ds(r0, c)
        hss = [slice(hh * d, (hh + 1) * d) for hh in range(nh)]
        st = [st_ref[hh] for hh in range(nh)]
        w_st = [_mm(w_ref[rows, hs], s_) for hs, s_ in zip(hss, st)]
        q_st = [_mm(q_ref[rows, hs], s_) for hs, s_ in zip(hss, st)]
        v_new = [u_ref[rows, hs] - x for hs, x in zip(hss, w_st)]
        o = [a + _mm(qk_ref[hh, rows, :], v) for hh, (a, v) in enumerate(zip(q_st, v_new))]
        kv = [_mm_tn(k_ref[rows, hs], v) for hs, v in zip(hss, v_new)]
        for hh, hs in enumerate(hss):
            st_ref[hh] = st[hh] * gb_ref[pl.ds(r0, 1), hs] + kv[hh]
            o_ref[rows, hs] = (_rms(o[hh], gnorm) * _silu(z_ref[rows, hs])).astype(o_ref.dtype)
        return carry

    lax.fori_loop(0, ts // c, chunk_body, 0)


def _deltanet(p32, conv_w, a_log, dt_bias, a_norm_g, *, ts, cols):
    bsz, s, _ = p32.shape
    d = HEAD_DIM
    nh = N_HEADS
    w = nh * d
    pad = lambda t: jnp.pad(t.astype(F32), (0, d - t.shape[0])).reshape(1, d)
    kernel = functools.partial(_deltanet_kernel, ts=ts, a_col=cols["a_lane"], b_col=cols["b_lane"])
    tile = lambda name: pl.BlockSpec((None, ts, w), lambda b, i: (b, i, cols[name] // nh))
    conv = lambda k: pl.BlockSpec((CONV_WIDTH, w), lambda b, i: (0, k))
    row = pl.BlockSpec((1, d), lambda b, i: (0, 0))
    return pl.pallas_call(
        kernel,
        grid=(bsz, s // ts),
        in_specs=[tile("qa"), tile("ka"), tile("va"), tile("za"),
                  pl.BlockSpec((None, ts, d), lambda b, i: (b, i, cols["small"])),
                  conv(0), conv(1), conv(2), row, row, row],
        out_specs=pl.BlockSpec((None, ts, w), lambda b, i: (b, i, 0)),
        out_shape=jax.ShapeDtypeStruct((bsz, s, w), BF16),
        scratch_shapes=[pltpu.VMEM((3, ts + 8, w), F32)]
        + [pltpu.VMEM((ts, w), F32) for _ in range(7)]
        + [pltpu.VMEM((nh, ts, CHUNK), F32), pltpu.VMEM((nh, d, d), F32)],
        compiler_params=pltpu.CompilerParams(
            dimension_semantics=("parallel", "arbitrary"), vmem_limit_bytes=VMEM_LIMIT),
        name="deltanet",
    )(p32, p32, p32, p32, p32, conv_w.astype(F32), conv_w.astype(F32), conv_w.astype(F32),
      pad(a_log), pad(dt_bias), a_norm_g.astype(F32).reshape(1, d))


def _hgrn2_kernel(q_ref, f_ref, i_ref, gate_ref, lb_ref, gn_ref, o_ref,
                  qs_ref, ks_ref, gc_ref, st_ref, *, ts):
    s = pl.program_id(1)
    c = CHUNK
    d = HEAD_DIM
    nh = N_HEADS
    SUB = 16

    @pl.when(s == 0)
    def _():
        st_ref[...] = jnp.zeros_like(st_ref)

    lb = lb_ref[...]
    f_raw = f_ref[...]
    log_sig = jnp.minimum(f_raw, 0.0) - jnp.log1p(jnp.exp(-jnp.abs(f_raw)))
    la = jnp.log(lb)
    lbb = jnp.log1p(-lb) + log_sig
    log_f = jnp.maximum(la, lbb) + jnp.log1p(jnp.exp(-jnp.abs(la - lbb)))
    qs_ref[...] = _silu(q_ref[...])
    ks_ref[...] = (1.0 - lb) * _sigmoid(-f_raw)

    row = _iota((c, c), 0)
    col = _iota((c, c), 1)
    tri_f = (col <= row).astype(F32)
    ones_dd = jnp.ones((d, d), BF16)
    rows_8d = _iota((8, d), 0)
    gnorm = gn_ref[...]

    tri2 = jnp.concatenate([tri_f, tri_f], axis=1).astype(BF16)
    for ci in range(ts // c):
        hi, lo = _split(log_f[ci * c:(ci + 1) * c, :])
        gc_ref[ci * c:(ci + 1) * c, :] = jnp.dot(tri2, jnp.concatenate([hi, lo], axis=0),
                                                 preferred_element_type=F32)

    blocks = [(sb * SUB, (sb + 1) * SUB) for sb in range(c // SUB)]

    def chunk_loop(ci, carry):
        r0 = pl.multiple_of(ci * c, c)
        rows = pl.ds(r0, c)
        hss = [slice(hh * d, (hh + 1) * d) for hh in range(nh)]
        q = [qs_ref[rows, hs] for hs in hss]
        k = [ks_ref[rows, hs] for hs in hss]
        v = [i_ref[rows, hs] for hs in hss]
        gc = [gc_ref[rows, hs] for hs in hss]

        def near_products(q, k, gc):
            prods = []
            for top, end in blocks:
                for j in range(top, end):
                    lo = (j // 8) * 8
                    e = jnp.exp2(gc[lo:end, :] - gc[j:j + 1, :])
                    if j % 8:
                        head = jnp.where(rows_8d >= j - lo, e[:8], 0.0)
                        e = jnp.concatenate([head, e[8:]], axis=0) if lo + 8 < end else head
                    prods.append(q[lo:end, :] * k[j:j + 1, :] * e)
            return jnp.concatenate(prods, axis=0).astype(BF16)

        def far_operands(q, k, gc):
            out = []
            for top, end in blocks[1:]:
                g_b = gc[top - 1:top, :]
                out.append((q[top:end, :] * jnp.exp(gc[top:end, :] - g_b),
                            k[:top, :] * jnp.exp(jnp.minimum(g_b - gc[:top, :], 0.0))))
            return out

        near = [near_products(a, b, g * LOG2E) for a, b, g in zip(q, k, gc)]
        far_ops = [far_operands(*x) for x in zip(q, k, gc)]
        st = [st_ref[hh] for hh in range(nh)]
        gl = [x[c - 1:c, :] for x in gc]
        sums = [jnp.dot(x, ones_dd, preferred_element_type=F32) for x in near]
        qk_far = [[_mm_nt(qe, ke) for qe, ke in ops] for ops in far_ops]
        far = [[_mm(a, vv[:top, :]) for a, (top, _) in zip(qs, blocks[1:])] for qs, vv in zip(qk_far, v)]
        o_st = [_mm_nt(a * jnp.exp(g), s_) for a, g, s_ in zip(q, gc, st)]
        kv = [_mm_tn(vv, kk * jnp.exp(g_l - g)) for vv, kk, g_l, g in zip(v, k, gl, gc)]

        for hh, hs in enumerate(hss):
            groups = [jnp.zeros((8, d), F32) for _ in range(c // 8)]
            at = 0
            for top, end in blocks:
                for j in range(top, end):
                    v_j = v[hh][j:j + 1, :]
                    for g in range(j // 8, end // 8):
                        groups[g] = groups[g] + sums[hh][at:at + 8, :] * v_j
                        at += 8
            for f, (top, end) in zip(far[hh], blocks[1:]):
                for g in range(top // 8, end // 8):
                    groups[g] = groups[g] + f[(g * 8 - top):(g * 8 - top + 8), :]
            o = jnp.concatenate(groups, axis=0) + o_st[hh]
            st_ref[hh] = st[hh] * jnp.exp(gl[hh]) + kv[hh]
            o_ref[rows, hs] = (_rms(o, gnorm) * _silu(gate_ref[rows, hs])).astype(o_ref.dtype)
        return carry

    lax.fori_loop(0, ts // c, chunk_loop, 0)


def _hgrn2(p32, lb, d_norm_g, *, ts, cols):
    bsz, s, _ = p32.shape
    d = HEAD_DIM
    nh = N_HEADS
    w = nh * d
    kernel = functools.partial(_hgrn2_kernel, ts=ts)
    tile = lambda name: pl.BlockSpec((None, ts, w), lambda b, i: (b, i, cols[name] // nh))
    return pl.pallas_call(
        kernel,
        grid=(bsz, s // ts),
        in_specs=[tile("qd"), tile("fd"), tile("id"), tile("gd"),
                  pl.BlockSpec((1, w), lambda b, i: (0, 0)),
                  pl.BlockSpec((1, d), lambda b, i: (0, 0))],
        out_specs=pl.BlockSpec((None, ts, w), lambda b, i: (b, i, 0)),
        out_shape=jax.ShapeDtypeStruct((bsz, s, w), BF16),
        scratch_shapes=[pltpu.VMEM((ts, w), F32), pltpu.VMEM((ts, w), F32),
                        pltpu.VMEM((ts, w), F32), pltpu.VMEM((nh, d, d), F32)],
        compiler_params=pltpu.CompilerParams(
            dimension_semantics=("parallel", "arbitrary"), vmem_limit_bytes=VMEM_LIMIT),
        name="hgrn2",
    )(p32, p32, p32, p32, lb.astype(F32).reshape(1, w), d_norm_g.astype(F32).reshape(1, d))


def _stickbreak_kernel(q_ref, k_ref, v_ref, o_ref, acc_ref, *, tq):
    i = pl.program_id(1)
    d = HEAD_DIM
    nh = N_HEADS
    row = _iota((tq, tq), 0)
    col = _iota((tq, tq), 1)
    causal = col < row
    later = (row > col).astype(BF16)
    later2 = jnp.concatenate([later, later], axis=0)

    heads = [slice(hh * d, (hh + 1) * d) for hh in range(nh)]

    def scores(blocks):
        jobs = [(j, dg, hs) for j, dg in blocks for hs in heads]
        z = [_mm_nt(q_ref[:, hs], k_ref[pl.ds(pl.multiple_of(j * tq, tq), tq), hs]) * (d ** -0.5)
             for j, _, hs in jobs]
        sp = [_softplus(x) for x in z]
        l1m = [jnp.where(causal, -x, 0.0) if dg else -x for x, (_, dg, _) in zip(sp, jobs)]
        rest = [jnp.dot(jnp.concatenate(_split(x), axis=1), later2, preferred_element_type=F32)
                for x in l1m]
        out = [((a - b) + r, l) for a, b, r, l in zip(z, sp, rest, l1m)]
        return [out[b * nh:(b + 1) * nh] for b in range(len(blocks))]

    def block(j, carries):
        (sc,) = scores([(j, False)])
        ps = [jnp.exp(logw + c) for (logw, _), c in zip(sc, carries)]
        pv = [_mm(p, v_ref[pl.ds(pl.multiple_of(j * tq, tq), tq), hs]) for p, hs in zip(ps, heads)]
        for hs, x in zip(heads, pv):
            acc_ref[:, hs] += x
        return tuple(c + jnp.sum(l1m, axis=-1, keepdims=True) for (_, l1m), c in zip(sc, carries))

    j1 = jnp.maximum(i - 1, 0)
    j2 = jnp.maximum(i - 2, 0)
    live1 = jnp.where(i > 0, 1.0, 0.0)
    live2 = jnp.where(i > 1, 1.0, 0.0)
    s0, s1, s2 = scores([(i, True), (j1, False), (j2, False)])
    carries = []
    for hh, hs in enumerate(heads):
        c0 = jnp.sum(s0[hh][1], axis=-1, keepdims=True)
        c1 = c0 + jnp.sum(s1[hh][1], axis=-1, keepdims=True)
        p0 = jnp.where(causal, jnp.exp(s0[hh][0]), 0.0)
        p1 = jnp.exp(s1[hh][0] + c0) * live1
        p2 = jnp.exp(s2[hh][0] + c1) * live2
        acc_ref[:, hs] = (_mm(p0, v_ref[pl.ds(pl.multiple_of(i * tq, tq), tq), hs])
                          + _mm(p1, v_ref[pl.ds(pl.multiple_of(j1 * tq, tq), tq), hs])
                          + _mm(p2, v_ref[pl.ds(pl.multiple_of(j2 * tq, tq), tq), hs]))
        carries.append(c1 + jnp.sum(s2[hh][1], axis=-1, keepdims=True))
    carries = tuple(carries)

    def cond(c):
        worst = functools.reduce(jnp.maximum, c[1])
        return jnp.logical_and(c[0] >= 0, jnp.max(worst) >= EXP_ZERO_BELOW)

    def body(c):
        return c[0] - 1, block(c[0], c[1])

    lax.while_loop(cond, body, (i - 3, carries))
    o_ref[...] = acc_ref[...].astype(o_ref.dtype)


def _stickbreak(p16, *, tq, cols):
    bsz, s, _ = p16.shape
    nh = N_HEADS
    w = nh * HEAD_DIM
    kernel = functools.partial(_stickbreak_kernel, tq=tq)
    resident = dict(pipeline_mode=pl.Buffered(1))
    return pl.pallas_call(
        kernel,
        grid=(bsz, s // tq),
        in_specs=[pl.BlockSpec((None, tq, w), lambda b, i: (b, i, cols["qc"] // nh)),
                  pl.BlockSpec((None, s, w), lambda b, i: (b, 0, cols["kc"] // nh), **resident),
                  pl.BlockSpec((None, s, w), lambda b, i: (b, 0, cols["vc"] // nh), **resident)],
        out_specs=pl.BlockSpec((None, tq, w), lambda b, i: (b, i, 0)),
        out_shape=jax.ShapeDtypeStruct((bsz, s, w), BF16),
        scratch_shapes=[pltpu.VMEM((tq, w), F32)],
        compiler_params=pltpu.CompilerParams(
            dimension_semantics=("parallel", "arbitrary"), vmem_limit_bytes=VMEM_LIMIT),
        name="stickbreak",
    )(p16, p16, p16)


def _dsa_kernel(qi_ref, smq_ref, q_ref, sm_ref, k_ref, vt_ref, bias_ref, o_ref,
                sc_ref, scb_ref, qct_ref, kc_ref, bd_ref, lg_ref, *, tq, k_sel, wi_lane, wide):
    i = pl.program_id(1)
    tk = tq
    d = HEAD_DIM
    nh = N_HEADS
    ksel = float(k_sel)
    per_wide = wide // tk
    n_wide = (i + per_wide) // per_wide
    sub = 2 * tk
    lane_q = _iota((1, tq), 1)

    def tree(parts, op):
        while len(parts) > 1:
            parts = [op(parts[j], parts[j + 1]) if j + 1 < len(parts) else parts[j]
                     for j in range(0, len(parts), 2)]
        return parts[0]

    def col_fold(x, op=jnp.add, rows=8):
        return tree([x[r * rows:(r + 1) * rows] for r in range(x.shape[0] // rows)], op)

    @pl.when(i == 0)
    def _():
        def prep(g, carry):
            g0 = pl.multiple_of(g * wide, wide)
            hi, lo = _split(sm_ref[pl.ds(g0, wide), :][:, :IDX_DIM])
            kc_ref[pl.ds(g0, wide), :] = jnp.concatenate([hi, lo, hi], axis=1)
            return carry
        lax.fori_loop(0, sm_ref.shape[0] // wide, prep, 0)

    qit = qi_ref[...].T
    for p in range(IDX_HEADS // 2):
        halves = []
        for hh in (2 * p, 2 * p + 1):
            hi, lo = _split(qit[hh * IDX_DIM:(hh + 1) * IDX_DIM, :])
            halves.append(jnp.concatenate([hi, hi, lo], axis=0))
        qct_ref[p] = jnp.concatenate(halves, axis=1)
    w_rows = smq_ref[...].T[wi_lane:wi_lane + IDX_HEADS, :] * ((IDX_HEADS ** -0.5) * (IDX_DIM ** -0.5))

    q2t = (q_ref[...] * ((d ** -0.5) * LOG2E)).T.astype(BF16)
    zero_dq = jnp.zeros((d, tq), BF16)
    for p in range(nh // 2):
        top = jnp.concatenate([q2t[2 * p * d:(2 * p + 1) * d], zero_dq], axis=1)
        bot = jnp.concatenate([zero_dq, q2t[(2 * p + 1) * d:(2 * p + 2) * d]], axis=1)
        bd_ref[p] = jnp.concatenate([top, bot], axis=0)

    limit = i * tq + (lane_q // CHUNK + 1) * CHUNK

    rows_s = _iota((sub, tq), 0)

    def score_groups(gs, mm, masked):
        mn, mx = mm
        k0s = [pl.multiple_of(g * wide + sb * sub, sub) for g in gs for sb in range(wide // sub)]
        keys = [kc_ref[pl.ds(k0, sub), :] for k0 in k0s]
        accs = [jnp.zeros((sub, tq), F32) for _ in k0s]
        for p in range(IDX_HEADS // 2):
            rhs = qct_ref[p]
            for n, kk in enumerate(keys):
                s2 = jnp.dot(kk, rhs, preferred_element_type=F32)
                accs[n] = (accs[n] + jnp.maximum(s2[:, :tq], 0.0) * w_rows[2 * p:2 * p + 1, :]
                           + jnp.maximum(s2[:, tq:], 0.0) * w_rows[2 * p + 1:2 * p + 2, :])
        for k0, sct in zip(k0s, accs):
            if masked:
                adm = (k0 + rows_s) < limit
                mn = jnp.minimum(mn, col_fold(jnp.where(adm, sct, jnp.inf), jnp.minimum))
                sct = jnp.where(adm, sct, -jnp.inf)
            else:
                mn = jnp.minimum(mn, col_fold(sct, jnp.minimum))
            mx = jnp.maximum(mx, col_fold(sct, jnp.maximum))
            sc_ref[pl.ds(k0, sub), :] = sct
            scb_ref[pl.ds(k0, sub), :] = _floor_bf16(sct)
        return mn, mx

    def score_pair(j, mm):
        return score_groups((2 * j, 2 * j + 1), mm, False)

    n_full = n_wide - 1
    mm = lax.fori_loop(0, n_full // 2, score_pair,
                       (jnp.full((8, tq), jnp.inf, F32), jnp.full((8, tq), -jnp.inf, F32)))
    mm = lax.cond(n_full % 2 == 1, lambda c: score_groups((n_full - 1,), c, False), lambda c: c, mm)
    mn, mx = score_groups((n_wide - 1,), mm, True)

    n_pairs = (n_wide + 1) // 2

    @pl.when(n_wide % 2 == 1)
    def _():
        sc_ref[pl.ds(pl.multiple_of(n_wide * wide, wide), wide), :] = jnp.full((wide, tq), -jnp.inf, F32)
        scb_ref[pl.ds(pl.multiple_of(n_wide * wide, wide), wide), :] = jnp.full((wide, tq), -jnp.inf, BF16)
    rmin = jnp.min(mn, axis=0, keepdims=True)
    rmax = jnp.max(mx, axis=0, keepdims=True)

    def count(pred):
        def body(j, acc):
            for g in (2 * j, 2 * j + 1):
                acc = acc + col_fold(pred(sc_ref[pl.ds(pl.multiple_of(g * wide, wide), wide), :]))
            return acc
        return jnp.sum(lax.fori_loop(0, n_pairs, body, jnp.zeros((8, tq), F32)), axis=0, keepdims=True)

    def max_below(x):
        def body(j, acc):
            for g in (2 * j, 2 * j + 1):
                blk = sc_ref[pl.ds(pl.multiple_of(g * wide, wide), wide), :]
                acc = jnp.maximum(acc, col_fold(jnp.where(blk < x, blk, -jnp.inf), jnp.maximum))
            return acc
        return jnp.max(lax.fori_loop(0, n_pairs, body, jnp.full((8, tq), -jnp.inf, F32)), axis=0, keepdims=True)

    n_adm = limit.astype(F32)
    all_sel = n_adm <= ksel

    def bisect(c):
        lo, hi, c_lo = c
        mid = 0.5 * lo + 0.5 * hi
        cm = count(lambda blk: _ind(blk >= mid))
        ge = cm >= ksel
        return jnp.where(ge, mid, lo), jnp.where(ge, hi, mid), jnp.where(ge, cm, c_lo)

    def pending(c_lo, tied):
        return jnp.where(all_sel, 0.0, jnp.where(tied > 0.5, 0.0, _ind(c_lo != ksel)))

    def bisect_coarse(_, c):
        lo, hi, c_lo = c
        mid = _floor_bf16(0.5 * lo + 0.5 * hi).astype(F32)
        t_b = jnp.broadcast_to(mid, (16, tq)).astype(BF16)
        one_b = jnp.ones((16, tq), BF16)
        zero_b = jnp.zeros((16, tq), BF16)

        def body(j, acc):
            for g in (2 * j, 2 * j + 1):
                blk = scb_ref[pl.ds(pl.multiple_of(g * wide, wide), wide), :]
                ind = [jnp.where(blk[r * 16:(r + 1) * 16] >= t_b, one_b, zero_b) for r in range(wide // 16)]
                acc = acc + tree(ind, jnp.add).astype(F32)
            return acc

        acc = lax.fori_loop(0, n_pairs, body, jnp.zeros((16, tq), F32))
        cm = jnp.sum(acc, axis=0, keepdims=True)
        ge = cm >= ksel
        return jnp.where(ge, mid, lo), jnp.where(ge, hi, mid), jnp.where(ge, cm, c_lo)

    lo0 = _floor_bf16(rmin).astype(F32)
    hi0 = _floor_bf16(rmax + (jnp.abs(rmax) * (2.0 ** -6) + 1e-30)).astype(F32)
    state = lax.fori_loop(0, BISECT_COARSE, bisect_coarse, (lo0, hi0, n_adm))
    state = lax.fori_loop(0, BISECT_FIXED, lambda _, c: bisect(c), state)

    def round_cond(c):
        return jnp.max(pending(c[0][2], c[1])) > 0.5

    def round_body(c):
        st, tied, v, need = c

        def more_cond(s):
            return jnp.logical_and(s[0] < BISECT_EXTRA, jnp.max(pending(s[1][2], tied)) > 0.5)

        _, st = lax.while_loop(more_cond, lambda s: (s[0] + 1, bisect(s[1])), (jnp.int32(0), st))
        pend = pending(st[2], tied)

        def check(_):
            cand = max_below(st[1])
            c_ge = count(lambda blk: _ind(blk >= cand))
            c_gt = count(lambda blk: _ind(blk > cand))
            ok = jnp.where(pend > 0.5, _ind(c_ge >= ksel), 0.0)
            return (jnp.where(ok > 0.5, 1.0, tied), jnp.where(ok > 0.5, cand, v),
                    jnp.where(ok > 0.5, ksel - c_gt, need))

        tied, v, need = lax.cond(jnp.max(pend) > 0.5, check, lambda _: (tied, v, need), 0)
        return st, tied, v, need

    zeros1 = jnp.zeros((1, tq), F32)
    (lo_f, _, _), tied, v_tie, need = lax.while_loop(round_cond, round_body, (state, zeros1, zeros1, zeros1))
    vth = jnp.where(all_sel, F32_LOWEST, jnp.where(tied > 0.5, v_tie, lo_f))

    @pl.when(jnp.max(tied) > 0.5)
    def _():
        v_eq = jnp.where(tied > 0.5, v_tie, jnp.inf)
        incl = (_iota((tk, tk), 1) <= _iota((tk, tk), 0)).astype(BF16)

        def demote(g, seen):
            g0 = pl.multiple_of(g * wide, wide)
            xs = [sc_ref[pl.ds(g0 + pb * tk, tk), :] for pb in range(per_wide)]
            eqs = [_ind(x == v_eq) for x in xs]
            inblk = [jnp.dot(incl, e.astype(BF16), preferred_element_type=F32) for e in eqs]
            for pb in range(per_wide):
                rank = inblk[pb] + seen
                sc_ref[pl.ds(g0 + pb * tk, tk), :] = jnp.where(eqs[pb] * _ind(rank > need) > 0.5,
                                                               -jnp.inf, xs[pb])
                seen = seen + jnp.sum(col_fold(eqs[pb]), axis=0, keepdims=True)
            return seen

        lax.fori_loop(0, n_wide, demote, zeros1)

    g_near = jnp.maximum(i - 1, 0) // per_wide

    def logit_group(g, mx, near):
        out = list(mx)
        for sb in range(wide // sub):
            k0 = pl.multiple_of(g * wide + sb * sub, sub)
            sel = sc_ref[pl.ds(k0, sub), :] >= vth
            for p in range(nh // 2):
                pair = jnp.dot(k_ref[pl.ds(k0, sub), 2 * p * d:(2 * p + 2) * d], bd_ref[p],
                               preferred_element_type=F32)
                for hh in (2 * p, 2 * p + 1):
                    lm = pair[:, (hh - 2 * p) * tq:(hh - 2 * p + 1) * tq]
                    if near:
                        back = [jnp.clip(i - (g * per_wide + sb * (sub // tk) + pb), 0, 2)
                                for pb in range(sub // tk)]
                        lm = lm + jnp.concatenate([bias_ref[bk, hh] for bk in back], axis=0)
                    lm = jnp.where(sel, lm, NEG_BIG)
                    lg_ref[hh, pl.ds(k0, sub), :] = lm
                    out[hh] = jnp.maximum(out[hh], col_fold(lm, jnp.maximum))
        return tuple(out)

    mx = tuple(jnp.full((8, tq), NEG_BIG, F32) for _ in range(nh))
    def logit_pair(j, mx, near):
        return logit_group(2 * j + 1, logit_group(2 * j, mx, near), near)

    far_pairs = g_near // 2
    mx = lax.fori_loop(0, far_pairs, functools.partial(logit_pair, near=False), mx)
    mx = lax.fori_loop(far_pairs, n_pairs, functools.partial(logit_pair, near=True), mx)
    m_q = [jnp.max(mx[hh], axis=0, keepdims=True) for hh in range(nh)]

    ones_rows = jnp.ones((8, wide), BF16)

    def pv_pair(j, carry):
        ls, accs = list(carry[0]), list(carry[1])
        jobs = [(pl.multiple_of(g * wide, wide), hh) for g in (2 * j, 2 * j + 1) for hh in range(nh)]
        ps = [jnp.exp2(lg_ref[hh, pl.ds(g0, wide), :] - m_q[hh]).astype(BF16) for g0, hh in jobs]
        outs = [jnp.dot(jnp.concatenate([vt_ref[hh * d:(hh + 1) * d, pl.ds(g0, wide)], ones_rows], axis=0),
                        p, preferred_element_type=F32) for (g0, hh), p in zip(jobs, ps)]
        for (_, hh), out in zip(jobs, outs):
            ls[hh] = ls[hh] + out[d:]
            accs[hh] = accs[hh] + out[:d]
        return tuple(ls), tuple(accs)

    ls, accs = lax.fori_loop(0, n_pairs, pv_pair,
                             (tuple(jnp.zeros((8, tq), F32) for _ in range(nh)),
                              tuple(jnp.zeros((d, tq), F32) for _ in range(nh))))
    for hh in range(nh):
        o_ref[:, hh * d:(hh + 1) * d] = (accs[hh] / ls[hh][0:1]).T.astype(o_ref.dtype)


def _dsa(p32, p16, vt, bias_tiles, *, tq, cols):
    bsz, s, _ = p32.shape
    d = HEAD_DIM
    nh = N_HEADS
    wide = 4 * tq
    k_sel = min(TOPK_MAX, s // 4)
    w512 = nh * d
    kernel = functools.partial(_dsa_kernel, tq=tq, k_sel=k_sel, wi_lane=cols["wi_lane"], wide=wide)
    resident = dict(pipeline_mode=pl.Buffered(1))
    return pl.pallas_call(
        kernel,
        grid=(bsz, s // tq),
        in_specs=[pl.BlockSpec((None, tq, w512), lambda b, i: (b, i, cols["qi"] // nh)),
                  pl.BlockSpec((None, tq, d), lambda b, i: (b, i, cols["small"])),
                  pl.BlockSpec((None, tq, w512), lambda b, i: (b, i, cols["qb"] // nh)),
                  pl.BlockSpec((None, s, d), lambda b, i: (b, 0, cols["small"]), **resident),
                  pl.BlockSpec((None, s, w512), lambda b, i: (b, 0, cols["kb"] // nh), **resident),
                  pl.BlockSpec((w512, s), lambda b, i: (0, b), **resident),
                  pl.BlockSpec((3, nh, tq, tq), lambda b, i: (0, 0, 0, 0), **resident)],
        out_specs=pl.BlockSpec((None, tq, w512), lambda b, i: (b, i, 0)),
        out_shape=jax.ShapeDtypeStruct((bsz, s, w512), BF16),
        scratch_shapes=[pltpu.VMEM((s, tq), F32),
                        pltpu.VMEM((s, tq), BF16),
                        pltpu.VMEM((IDX_HEADS // 2, 3 * IDX_DIM, 2 * tq), BF16),
                        pltpu.VMEM((s, 3 * IDX_DIM), BF16),
                        pltpu.VMEM((nh // 2, 2 * d, 2 * tq), BF16),
                        pltpu.VMEM((nh, s, tq), F32)],
        compiler_params=pltpu.CompilerParams(
            dimension_semantics=("parallel", "arbitrary"), vmem_limit_bytes=VMEM_LIMIT),
        name="dsa",
    )(p32, p32, p32, p32, p16, vt, bias_tiles)


def _t5_bucket(rel):
    nb = REL_BUCKETS // 2
    max_exact = nb // 2
    ret = jnp.where(rel > 0, nb, 0)
    n = jnp.abs(rel)
    large = max_exact + (jnp.log(jnp.maximum(n, 1).astype(F32) / max_exact)
                         / math.log(REL_MAX_DIST / max_exact) * (nb - max_exact)).astype(jnp.int32)
    large = jnp.minimum(large, nb - 1)
    return ret + jnp.where(n < max_exact, n, large)


def _bias_tiles(rel_table, tq):
    assert tq >= REL_MAX_DIST
    t = jnp.arange(tq)
    back = jnp.arange(3)
    rel = (t[None, None, :] - back[:, None, None] * tq) - t[None, :, None]
    onehot = (_t5_bucket(rel)[..., None] == jnp.arange(REL_BUCKETS)).astype(F32)
    tiles = jnp.einsum("bqkn,nh->bhkq", onehot, rel_table.astype(F32),
                       precision=HIGHEST)
    return (tiles - tiles[2:3]) * LOG2E


def _even_layout(w_in):
    d = HEAD_DIM
    a_w = 2 * N_HEADS * d + N_HEADS * d
    offs = {}
    o = 0
    for name, w in (("qkv", a_w), ("z", N_HEADS * d), ("a", N_HEADS), ("b", N_HEADS),
                    ("qb", N_HEADS * d), ("kb", N_HEADS * d), ("vb", N_HEADS * d),
                    ("qi", IDX_HEADS * IDX_DIM), ("ki", IDX_DIM), ("wi", IDX_HEADS)):
        offs[name] = (o, o + w)
        o += w
    assert o == w_in.shape[1]
    sl = lambda n: w_in[:, offs[n][0]:offs[n][1]]
    small_w = IDX_DIM + 2 * N_HEADS + IDX_HEADS
    small_pad = -small_w % d
    zeros = lambda n: jnp.zeros((w_in.shape[0], n), w_in.dtype)
    w32 = jnp.concatenate([sl("qkv"), sl("z"), sl("qb"), sl("qi"),
                           sl("ki"), sl("a"), sl("b"), sl("wi"), zeros(small_pad)], axis=1)
    n32 = w32.shape[1]
    tn = n32 // 5
    assert tn * 5 == n32 and tn % d == 0
    w16 = jnp.concatenate([sl("kb"), zeros(tn - N_HEADS * d)], axis=1)
    nh = N_HEADS
    cols = dict(qa=0, ka=nh, va=2 * nh, za=3 * nh, qb=4 * nh, qi=5 * nh, small=6 * nh, kb=0,
                a_lane=IDX_DIM, b_lane=IDX_DIM + nh, wi_lane=IDX_DIM + 2 * nh, n32=n32, tn=tn)
    return jnp.concatenate([w32, w16], axis=1).astype(BF16), sl("vb").T.astype(BF16), cols


def kernel(x, norm_g, w_in_even, conv_w_even, a_log_even, dt_bias_even, a_norm_even, w_out_even,
           rel_bias, w_in_odd, lb_logits, d_norm_odd, w_out_odd, w_gate, w_up, w_down):
    bsz, s, d = x.shape
    t = bsz * s
    depth = norm_g.shape[0]
    nh = N_HEADS
    tq = Q_TILE
    lb_all = jnp.cumsum(jax.nn.softmax(lb_logits.astype(F32), axis=0), axis=0)
    lb_all = lb_all - lb_all[:1]
    odd_cols = dict(qc=0, kc=nh, vc=2 * nh, qd=0, fd=nh, id=2 * nh, gd=3 * nh)
    bias_tiles = _bias_tiles(rel_bias, tq)

    h = x.reshape(t, d)
    for l in range(depth):
        if l % 2 == 0:
            e = l // 2
            w_even, w_vt, cols = _even_layout(w_in_even[e])
            p32, p16, vt = _norm_matmul(h, norm_g[l, 0], w_even, tm=PROJ_TILE, tn=cols["tn"], n32=cols["n32"],
                                        w_t=w_vt)
            p32 = p32.reshape(bsz, s, -1)
            p16 = p16.reshape(bsz, s, -1)
            o_1 = _deltanet(p32, conv_w_even[e], a_log_even[e], dt_bias_even[e], a_norm_even[e],
                            ts=min(DELTANET_TILE, s), cols=cols)
            o_2 = _dsa(p32, p16, vt, bias_tiles, tq=tq, cols=cols)
            w_out = w_out_even[e]
        else:
            o = l // 2
            n16 = 3 * nh * HEAD_DIM
            w_odd = jnp.concatenate([w_in_odd[o][:, n16:], w_in_odd[o][:, :n16]], axis=1).astype(BF16)
            p32, p16 = _norm_matmul(h, norm_g[l, 0], w_odd, tm=PROJ_TILE, tn=ODD_COL_TILE, n32=w_odd.shape[1] - n16)
            p32 = p32.reshape(bsz, s, -1)
            p16 = p16.reshape(bsz, s, -1)
            o_1 = _stickbreak(p16, tq=tq, cols=odd_cols)
            o_2 = _hgrn2(p32, lb_all[l], d_norm_odd[o], ts=min(SEQ_TILE, s), cols=odd_cols)
            w_out = w_out_odd[o]
        h = _mix_ffn(o_1.reshape(t, -1), o_2.reshape(t, -1), w_out, h, norm_g[l, 1], norm_g[l, 2], norm_g[l, 3],
                     w_gate[l], w_up[l], w_down[l], tm=ROW_TILE, tf=FFN_TILE)
    return h.reshape(bsz, s, d)
```

```python
import functools
import math

import jax
import jax.numpy as jnp
from jax import lax
from jax.experimental import pallas as pl
from jax.experimental.pallas import tpu as pltpu

F32 = jnp.float32
BF16 = jnp.bfloat16
HIGHEST = lax.Precision.HIGHEST

CHUNK = 64
HEAD_DIM = 128
N_HEADS = 4
IDX_HEADS = 8
IDX_DIM = 64
TOPK_MAX = 256
CONV_WIDTH = 4
REL_BUCKETS = 32
REL_MAX_DIST = 128
EPS = 1e-6
NEG_BIG = -1e30
LOG2E = 1.4426950408889634
BISECT_COARSE = 10
BISECT_FIXED = 8
BISECT_EXTRA = 6
F32_LOWEST = -3.4028234663852886e38
EXP_ZERO_BELOW = -104.0
VMEM_LIMIT = 56 * 1024 * 1024

PROJ_TILE = 2048
ROW_TILE = 512
DELTANET_TILE = 1024
SEQ_TILE = 512
Q_TILE = 128
ODD_COL_TILE = 512
FFN_TILE = 2816


def _mm(a, b):
    return jnp.dot(a.astype(BF16), b.astype(BF16), preferred_element_type=F32)


def _mm_nt(a, b):
    return lax.dot_general(a.astype(BF16), b.astype(BF16), (((1,), (1,)), ((), ())),
                           preferred_element_type=F32)


def _mm_tn(a, b):
    return lax.dot_general(a.astype(BF16), b.astype(BF16), (((0,), (0,)), ((), ())),
                           preferred_element_type=F32)


def _split(x):
    hi = x.astype(BF16)
    return hi, (x - hi.astype(F32)).astype(BF16)


def _floor_bf16(x):
    bits = pltpu.bitcast(x, jnp.int32)
    down = jnp.where(bits >= 0, bits, bits + 0xFFFF) & jnp.int32(-65536)
    return pltpu.bitcast(down, F32).astype(BF16)


def _sigmoid(x):
    return 1.0 / (1.0 + jnp.exp(-x))


def _silu(x):
    return x * _sigmoid(x)


def _softplus(x):
    return jnp.maximum(x, 0.0) + jnp.log1p(jnp.exp(-jnp.abs(x)))


def _rms(x, g):
    return x * lax.rsqrt(jnp.mean(x * x, axis=-1, keepdims=True) + EPS) * g


def _iota(shape, dim):
    return lax.broadcasted_iota(jnp.int32, shape, dim)


def _ind(mask):
    return jnp.where(mask, 1.0, 0.0)


def _norm_matmul_kernel(x_ref, g_ref, w_ref, *rest, n_t, tiles32):
    if n_t:
        wt_ref, o32_ref, o16_ref, ot_ref, xn_ref = rest
    else:
        o32_ref, o16_ref, xn_ref = rest
    j = pl.program_id(1)

    @pl.when(j == 0)
    def _():
        xn_ref[...] = _rms(x_ref[...], g_ref[...]).astype(BF16)
        if n_t:
            ot_ref[...] = lax.dot_general(wt_ref[...], xn_ref[...], (((1,), (1,)), ((), ())),
                                          preferred_element_type=F32).astype(BF16)

    y = jnp.dot(xn_ref[...], w_ref[...], preferred_element_type=F32)

    @pl.when(j < tiles32)
    def _():
        o32_ref[...] = y

    @pl.when(j >= tiles32)
    def _():
        o16_ref[...] = y.astype(BF16)


def _norm_matmul(x, g, w, *, tm, tn, n32, w_t=None):
    t, d = x.shape
    n = w.shape[1]
    n_t = 0 if w_t is None else w_t.shape[0]
    tiles32 = n32 // tn
    assert tiles32 * tn == n32 and (n - n32) % tn == 0 and 0 < n32 < n
    in_specs = [pl.BlockSpec((tm, d), lambda i, j: (i, 0)),
                pl.BlockSpec((1, d), lambda i, j: (0, 0)),
                pl.BlockSpec((d, tn), lambda i, j: (0, j))]
    out_specs = [pl.BlockSpec((tm, tn), lambda i, j: (i, jnp.minimum(j, tiles32 - 1))),
                 pl.BlockSpec((tm, tn), lambda i, j: (i, jnp.maximum(j - tiles32, 0)))]
    out_shape = [jax.ShapeDtypeStruct((t, n32), F32), jax.ShapeDtypeStruct((t, n - n32), BF16)]
    args = [x, g.reshape(1, d), w]
    if n_t:
        in_specs.append(pl.BlockSpec((n_t, d), lambda i, j: (0, 0)))
        out_specs.append(pl.BlockSpec((n_t, tm), lambda i, j: (0, i)))
        out_shape.append(jax.ShapeDtypeStruct((n_t, t), BF16))
        args.append(w_t)
    return pl.pallas_call(
        functools.partial(_norm_matmul_kernel, n_t=n_t, tiles32=tiles32),
        grid=(t // tm, n // tn),
        in_specs=in_specs,
        out_specs=out_specs,
        out_shape=out_shape,
        scratch_shapes=[pltpu.VMEM((tm, d), BF16)],
        compiler_params=pltpu.CompilerParams(
            dimension_semantics=("parallel", "arbitrary"), vmem_limit_bytes=VMEM_LIMIT),
        name="norm_matmul",
    )(*args)


def _mix_ffn_kernel(ca_ref, cb_ref, wa_ref, wb_ref, h_ref, gmix_ref, gpre_ref, gpost_ref,
                    wg_ref, wu_ref, wd_ref, o_ref, h1_ref, xn_ref, acc_ref):
    f = pl.program_id(1)

    @pl.when(f == 0)
    def _():
        y = (jnp.dot(ca_ref[...], wa_ref[...], preferred_element_type=F32)
             + jnp.dot(cb_ref[...], wb_ref[...], preferred_element_type=F32))
        h1 = h_ref[...] + _rms(y, gmix_ref[...])
        h1_ref[...] = h1
        xn_ref[...] = _rms(h1, gpre_ref[...]).astype(BF16)
        acc_ref[...] = jnp.zeros_like(acc_ref)

    xn = xn_ref[...]
    gate = jnp.dot(xn, wg_ref[...], preferred_element_type=F32)
    up = jnp.dot(xn, wu_ref[...], preferred_element_type=F32)
    act = (_silu(gate) * up).astype(BF16)
    acc_ref[...] += jnp.dot(act, wd_ref[...], preferred_element_type=F32)

    @pl.when(f == pl.num_programs(1) - 1)
    def _():
        o_ref[...] = h1_ref[...] + _rms(acc_ref[...], gpost_ref[...])


def _mix_ffn(ca, cb, w_out, h, g_mix, g_pre, g_post, wg, wu, wd, *, tm, tf):
    t, d = h.shape
    ff = wg.shape[1]
    wa_n = ca.shape[1]
    wb_n = cb.shape[1]
    row = pl.BlockSpec((1, d), lambda i, f: (0, 0))
    once = dict(pipeline_mode=pl.Buffered(1)) if tf == ff else {}
    return pl.pallas_call(
        _mix_ffn_kernel,
        grid=(t // tm, ff // tf),
        in_specs=[pl.BlockSpec((tm, wa_n), lambda i, f: (i, 0)),
                  pl.BlockSpec((tm, wb_n), lambda i, f: (i, 0)),
                  pl.BlockSpec((wa_n, d), lambda i, f: (0, 0)),
                  pl.BlockSpec((wb_n, d), lambda i, f: (0, 0)),
                  pl.BlockSpec((tm, d), lambda i, f: (i, 0)),
                  row, row, row,
                  pl.BlockSpec((d, tf), lambda i, f: (0, f), **once),
                  pl.BlockSpec((d, tf), lambda i, f: (0, f), **once),
                  pl.BlockSpec((tf, d), lambda i, f: (f, 0), **once)],
        out_specs=pl.BlockSpec((tm, d), lambda i, f: (i, 0)),
        out_shape=jax.ShapeDtypeStruct((t, d), F32),
        scratch_shapes=[pltpu.VMEM((tm, d), F32), pltpu.VMEM((tm, d), BF16), pltpu.VMEM((tm, d), F32)],
        compiler_params=pltpu.CompilerParams(
            dimension_semantics=("parallel", "arbitrary"), vmem_limit_bytes=VMEM_LIMIT),
        name="mix_ffn",
    )(ca, cb, w_out[:wa_n].astype(BF16), w_out[wa_n:].astype(BF16), h,
      g_mix.reshape(1, d), g_pre.reshape(1, d), g_post.reshape(1, d),
      wg.astype(BF16), wu.astype(BF16), wd.astype(BF16))


def _deltanet_kernel(xq_ref, xk_ref, xv_ref, z_ref, sm_ref, cwq_ref, cwk_ref, cwv_ref,
                     alog_ref, dtb_ref, gn_ref, o_ref,
                     xpad_ref, q_ref, k_ref, v_ref, gb_ref, bb_ref, u_ref, w_ref, qk_ref, st_ref,
                     *, ts, a_col, b_col):
    s = pl.program_id(1)
    c = CHUNK
    d = HEAD_DIM
    nh = N_HEADS

    @pl.when(s == 0)
    def _():
        xpad_ref[:, 0:8, :] = jnp.zeros((3, 8, nh * d), F32)
        st_ref[...] = jnp.zeros_like(st_ref)

    @pl.when(s != 0)
    def _():
        xpad_ref[:, 0:8, :] = xpad_ref[:, ts:ts + 8, :]

    xpad_ref[0, 8:ts + 8, :] = xq_ref[...]
    xpad_ref[1, 8:ts + 8, :] = xk_ref[...]
    xpad_ref[2, 8:ts + 8, :] = xv_ref[...]

    def conv_silu(idx, cw_ref, hs):
        cw = cw_ref[:, hs]
        acc = xpad_ref[idx, 8 - (CONV_WIDTH - 1):8 - (CONV_WIDTH - 1) + ts, hs] * cw[0:1, :]
        for j in range(1, CONV_WIDTH):
            off = 8 - (CONV_WIDTH - 1) + j
            acc = acc + xpad_ref[idx, off:off + ts, hs] * cw[j:j + 1, :]
        return _silu(acc)

    def l2norm(t):
        return t * lax.rsqrt(jnp.sum(t * t, axis=-1, keepdims=True) + EPS)

    row = _iota((c, c), 0)
    col = _iota((c, c), 1)
    tri = (col <= row)
    strict = (col < row)
    tri_f = tri.astype(F32)
    upper_f = (row <= col).astype(F32)
    eye = (row == col).astype(F32)
    gnorm = gn_ref[...]
    chunks = range(ts // c)
    rs = [slice(ci * c, (ci + 1) * c) for ci in chunks]
    tri2 = jnp.concatenate([tri_f, tri_f], axis=1).astype(BF16)
    ones2 = jnp.ones((c, 2 * c), BF16)

    def cum2(lhs2, x):
        hi, lo = _split(x)
        return jnp.dot(lhs2, jnp.concatenate([hi, lo], axis=0), preferred_element_type=F32)

    for hh in range(nh):
        hs = slice(hh * d, (hh + 1) * d)
        q_ref[:, hs] = l2norm(conv_silu(0, cwq_ref, hs)) * (d ** -0.5)
        k_ref[:, hs] = l2norm(conv_silu(1, cwk_ref, hs))
        v_ref[:, hs] = conv_silu(2, cwv_ref, hs)

        a_raw = sm_ref[:, a_col + hh:a_col + hh + 1]
        b_raw = sm_ref[:, b_col + hh:b_col + hh + 1]
        g = -jnp.exp(alog_ref[:, hh:hh + 1]) * _softplus(a_raw + dtb_ref[:, hh:hh + 1])
        gb_ref[:, hs] = jnp.broadcast_to(g, (ts, d))
        bb_ref[:, hs] = jnp.broadcast_to(_sigmoid(b_raw), (ts, d))

        q = [q_ref[r, hs] for r in rs]
        k = [k_ref[r, hs] for r in rs]
        beta = [bb_ref[r, hs] for r in rs]
        gb = [gb_ref[r, hs] for r in rs]
        gc = [cum2(tri2, x) for x in gb]
        gc_row = [cum2(ones2, x[:, :c] * upper_f) for x in gb]
        decay = [jnp.where(tri, jnp.exp(jnp.minimum(a[:, :c] - b, 0.0)), 0.0) for a, b in zip(gc, gc_row)]
        kk = [_mm_nt(x, x) for x in k]
        n = [-jnp.where(strict, b[:, :c] * x * dc, 0.0) for b, x, dc in zip(beta, kk, decay)]
        inv = [eye + x for x in n]
        for step in range(5):
            nb = [x.astype(BF16) for x in n]
            n = [jnp.dot(x, x, preferred_element_type=F32) for x in nb]
            inv = [iv + _mm(iv, x) for iv, x in zip(inv, n)]
        egc = [jnp.exp(x) for x in gc]
        gl = [x[c - 1:c, :] for x in gc]
        inv_l = [x.astype(BF16) for x in inv]
        u = [_mm(a, v_ref[r, hs] * b) for a, r, b in zip(inv_l, rs, beta)]
        w = [_mm(a, x * (b * e)) for a, x, b, e in zip(inv_l, k, beta, egc)]
        qk = [_mm_nt(a, b) * dc for a, b, dc in zip(q, k, decay)]
        for ci in chunks:
            r = rs[ci]
            u_ref[r, hs] = u[ci]
            w_ref[r, hs] = w[ci]
            qk_ref[hh, r, :] = qk[ci]
            q_ref[r, hs] = q[ci] * egc[ci]
            k_ref[r, hs] = k[ci] * jnp.exp(gl[ci] - gc[ci])
            gb_ref[r, hs] = jnp.broadcast_to(jnp.exp(gl[ci]), (c, d))

    def chunk_body(ci, carry):
        r0 = pl.multiple_of(ci * c, c)
        rows = pl.ds(r0, c)
        hss = [slice(hh * d, (hh + 1) * d) for hh in range(nh)]
        st = [st_ref[hh] for hh in range(nh)]
        w_st = [_mm(w_ref[rows, hs], s_) for hs, s_ in zip(hss, st)]
        q_st = [_mm(q_ref[rows, hs], s_) for hs, s_ in zip(hss, st)]
        v_new = [u_ref[rows, hs] - x for hs, x in zip(hss, w_st)]
        o = [a + _mm(qk_ref[hh, rows, :], v) for hh, (a, v) in enumerate(zip(q_st, v_new))]
        kv = [_mm_tn(k_ref[rows, hs], v) for hs, v in zip(hss, v_new)]
        for hh, hs in enumerate(hss):
            st_ref[hh] = st[hh] * gb_ref[pl.ds(r0, 1), hs] + kv[hh]
            o_ref[rows, hs] = (_rms(o[hh], gnorm) * _silu(z_ref[rows, hs])).astype(o_ref.dtype)
        return carry

    lax.fori_loop(0, ts // c, chunk_body, 0)


def _deltanet(p32, conv_w, a_log, dt_bias, a_norm_g, *, ts, cols):
    bsz, s, _ = p32.shape
    d = HEAD_DIM
    nh = N_HEADS
    w = nh * d
    pad = lambda t: jnp.pad(t.astype(F32), (0, d - t.shape[0])).reshape(1, d)
    kernel = functools.partial(_deltanet_kernel, ts=ts, a_col=cols["a_lane"], b_col=cols["b_lane"])
    tile = lambda name: pl.BlockSpec((None, ts, w), lambda b, i: (b, i, cols[name] // nh))
    conv = lambda k: pl.BlockSpec((CONV_WIDTH, w), lambda b, i: (0, k))
    row = pl.BlockSpec((1, d), lambda b, i: (0, 0))
    return pl.pallas_call(
        kernel,
        grid=(bsz, s // ts),
        in_specs=[tile("qa"), tile("ka"), tile("va"), tile("za"),
                  pl.BlockSpec((None, ts, d), lambda b, i: (b, i, cols["small"])),
                  conv(0), conv(1), conv(2), row, row, row],
        out_specs=pl.BlockSpec((None, ts, w), lambda b, i: (b, i, 0)),
        out_shape=jax.ShapeDtypeStruct((bsz, s, w), BF16),
        scratch_shapes=[pltpu.VMEM((3, ts + 8, w), F32)]
        + [pltpu.VMEM((ts, w), F32) for _ in range(7)]
        + [pltpu.VMEM((nh, ts, CHUNK), F32), pltpu.VMEM((nh, d, d), F32)],
        compiler_params=pltpu.CompilerParams(
            dimension_semantics=("parallel", "arbitrary"), vmem_limit_bytes=VMEM_LIMIT),
        name="deltanet",
    )(p32, p32, p32, p32, p32, conv_w.astype(F32), conv_w.astype(F32), conv_w.astype(F32),
      pad(a_log), pad(dt_bias), a_norm_g.astype(F32).reshape(1, d))


def _hgrn2_kernel(q_ref, f_ref, i_ref, gate_ref, lb_ref, gn_ref, o_ref,
                  qs_ref, ks_ref, gc_ref, st_ref, *, ts):
    s = pl.program_id(1)
    c = CHUNK
    d = HEAD_DIM
    nh = N_HEADS
    SUB = 16

    @pl.when(s == 0)
    def _():
        st_ref[...] = jnp.zeros_like(st_ref)

    lb = lb_ref[...]
    f_raw = f_ref[...]
    log_sig = jnp.minimum(f_raw, 0.0) - jnp.log1p(jnp.exp(-jnp.abs(f_raw)))
    la = jnp.log(lb)
    lbb = jnp.log1p(-lb) + log_sig
    log_f = jnp.maximum(la, lbb) + jnp.log1p(jnp.exp(-jnp.abs(la - lbb)))
    qs_ref[...] = _silu(q_ref[...])
    ks_ref[...] = (1.0 - lb) * _sigmoid(-f_raw)

    row = _iota((c, c), 0)
    col = _iota((c, c), 1)
    tri_f = (col <= row).astype(F32)
    ones_dd = jnp.ones((d, d), BF16)
    rows_8d = _iota((8, d), 0)
    gnorm = gn_ref[...]

    tri2 = jnp.concatenate([tri_f, tri_f], axis=1).astype(BF16)
    for ci in range(ts // c):
        hi, lo = _split(log_f[ci * c:(ci + 1) * c, :])
        gc_ref[ci * c:(ci + 1) * c, :] = jnp.dot(tri2, jnp.concatenate([hi, lo], axis=0),
                                                 preferred_element_type=F32)

    blocks = [(sb * SUB, (sb + 1) * SUB) for sb in range(c // SUB)]

    def chunk_loop(ci, carry):
        r0 = pl.multiple_of(ci * c, c)
        rows = pl.ds(r0, c)
        hss = [slice(hh * d, (hh + 1) * d) for hh in range(nh)]
        q = [qs_ref[rows, hs] for hs in hss]
        k = [ks_ref[rows, hs] for hs in hss]
        v = [i_ref[rows, hs] for hs in hss]
        gc = [gc_ref[rows, hs] for hs in hss]

        def near_products(q, k, gc):
            prods = []
            for top, end in blocks:
                for j in range(top, end):
                    lo = (j // 8) * 8
                    e = jnp.exp2(gc[lo:end, :] - gc[j:j + 1, :])
                    if j % 8:
                        head = jnp.where(rows_8d >= j - lo, e[:8], 0.0)
                        e = jnp.concatenate([head, e[8:]], axis=0) if lo + 8 < end else head
                    prods.append(q[lo:end, :] * k[j:j + 1, :] * e)
            return jnp.concatenate(prods, axis=0).astype(BF16)

        def far_operands(q, k, gc):
            out = []
            for top, end in blocks[1:]:
                g_b = gc[top - 1:top, :]
                out.append((q[top:end, :] * jnp.exp(gc[top:end, :] - g_b),
                            k[:top, :] * jnp.exp(jnp.minimum(g_b - gc[:top, :], 0.0))))
            return out

        near = [near_products(a, b, g * LOG2E) for a, b, g in zip(q, k, gc)]
        far_ops = [far_operands(*x) for x in zip(q, k, gc)]
        st = [st_ref[hh] for hh in range(nh)]
        gl = [x[c - 1:c, :] for x in gc]
        sums = [jnp.dot(x, ones_dd, preferred_element_type=F32) for x in near]
        qk_far = [[_mm_nt(qe, ke) for qe, ke in ops] for ops in far_ops]
        far = [[_mm(a, vv[:top, :]) for a, (top, _) in zip(qs, blocks[1:])] for qs, vv in zip(qk_far, v)]
        o_st = [_mm_nt(a * jnp.exp(g), s_) for a, g, s_ in zip(q, gc, st)]
        kv = [_mm_tn(vv, kk * jnp.exp(g_l - g)) for vv, kk, g_l, g in zip(v, k, gl, gc)]

        for hh, hs in enumerate(hss):
            groups = [jnp.zeros((8, d), F32) for _ in range(c // 8)]
            at = 0
            for top, end in blocks:
                for j in range(top, end):
                    v_j = v[hh][j:j + 1, :]
                    for g in range(j // 8, end // 8):
                        groups[g] = groups[g] + sums[hh][at:at + 8, :] * v_j
                        at += 8
            for f, (top, end) in zip(far[hh], blocks[1:]):
                for g in range(top // 8, end // 8):
                    groups[g] = groups[g] + f[(g * 8 - top):(g * 8 - top + 8), :]
            o = jnp.concatenate(groups, axis=0) + o_st[hh]
            st_ref[hh] = st[hh] * jnp.exp(gl[hh]) + kv[hh]
            o_ref[rows, hs] = (_rms(o, gnorm) * _silu(gate_ref[rows, hs])).astype(o_ref.dtype)
        return carry

    lax.fori_loop(0, ts // c, chunk_loop, 0)


def _hgrn2(p32, lb, d_norm_g, *, ts, cols):
    bsz, s, _ = p32.shape
    d = HEAD_DIM
    nh = N_HEADS
    w = nh * d
    kernel = functools.partial(_hgrn2_kernel, ts=ts)
    tile = lambda name: pl.BlockSpec((None, ts, w), lambda b, i: (b, i, cols[name] // nh))
    return pl.pallas_call(
        kernel,
        grid=(bsz, s // ts),
        in_specs=[tile("qd"), tile("fd"), tile("id"), tile("gd"),
                  pl.BlockSpec((1, w), lambda b, i: (0, 0)),
                  pl.BlockSpec((1, d), lambda b, i: (0, 0))],
        out_specs=pl.BlockSpec((None, ts, w), lambda b, i: (b, i, 0)),
        out_shape=jax.ShapeDtypeStruct((bsz, s, w), BF16),
        scratch_shapes=[pltpu.VMEM((ts, w), F32), pltpu.VMEM((ts, w), F32),
                        pltpu.VMEM((ts, w), F32), pltpu.VMEM((nh, d, d), F32)],
        compiler_params=pltpu.CompilerParams(
            dimension_semantics=("parallel", "arbitrary"), vmem_limit_bytes=VMEM_LIMIT),
        name="hgrn2",
    )(p32, p32, p32, p32, lb.astype(F32).reshape(1, w), d_norm_g.astype(F32).reshape(1, d))


def _stickbreak_kernel(q_ref, k_ref, v_ref, o_ref, acc_ref, *, tq):
    i = pl.program_id(1)
    d = HEAD_DIM
    nh = N_HEADS
    row = _iota((tq, tq), 0)
    col = _iota((tq, tq), 1)
    causal = col < row
    later = (row > col).astype(BF16)
    later2 = jnp.concatenate([later, later], axis=0)

    heads = [slice(hh * d, (hh + 1) * d) for hh in range(nh)]

    def scores(blocks):
        jobs = [(j, dg, hs) for j, dg in blocks for hs in heads]
        z = [_mm_nt(q_ref[:, hs], k_ref[pl.ds(pl.multiple_of(j * tq, tq), tq), hs]) * (d ** -0.5)
             for j, _, hs in jobs]
        sp = [_softplus(x) for x in z]
        l1m = [jnp.where(causal, -x, 0.0) if dg else -x for x, (_, dg, _) in zip(sp, jobs)]
        rest = [jnp.dot(jnp.concatenate(_split(x), axis=1), later2, preferred_element_type=F32)
                for x in l1m]
        out = [((a - b) + r, l) for a, b, r, l in zip(z, sp, rest, l1m)]
        return [out[b * nh:(b + 1) * nh] for b in range(len(blocks))]

    def block(j, carries):
        (sc,) = scores([(j, False)])
        ps = [jnp.exp(logw + c) for (logw, _), c in zip(sc, carries)]
        pv = [_mm(p, v_ref[pl.ds(pl.multiple_of(j * tq, tq), tq), hs]) for p, hs in zip(ps, heads)]
        for hs, x in zip(heads, pv):
            acc_ref[:, hs] += x
        return tuple(c + jnp.sum(l1m, axis=-1, keepdims=True) for (_, l1m), c in zip(sc, carries))

    j1 = jnp.maximum(i - 1, 0)
    j2 = jnp.maximum(i - 2, 0)
    live1 = jnp.where(i > 0, 1.0, 0.0)
    live2 = jnp.where(i > 1, 1.0, 0.0)
    s0, s1, s2 = scores([(i, True), (j1, False), (j2, False)])
    carries = []
    for hh, hs in enumerate(heads):
        c0 = jnp.sum(s0[hh][1], axis=-1, keepdims=True)
        c1 = c0 + jnp.sum(s1[hh][1], axis=-1, keepdims=True)
        p0 = jnp.where(causal, jnp.exp(s0[hh][0]), 0.0)
        p1 = jnp.exp(s1[hh][0] + c0) * live1
        p2 = jnp.exp(s2[hh][0] + c1) * live2
        acc_ref[:, hs] = (_mm(p0, v_ref[pl.ds(pl.multiple_of(i * tq, tq), tq), hs])
                          + _mm(p1, v_ref[pl.ds(pl.multiple_of(j1 * tq, tq), tq), hs])
                          + _mm(p2, v_ref[pl.ds(pl.multiple_of(j2 * tq, tq), tq), hs]))
        carries.append(c1 + jnp.sum(s2[hh][1], axis=-1, keepdims=True))
    carries = tuple(carries)

    def cond(c):
        worst = functools.reduce(jnp.maximum, c[1])
        return jnp.logical_and(c[0] >= 0, jnp.max(worst) >= EXP_ZERO_BELOW)

    def body(c):
        return c[0] - 1, block(c[0], c[1])

    lax.while_loop(cond, body, (i - 3, carries))
    o_ref[...] = acc_ref[...].astype(o_ref.dtype)


def _stickbreak(p16, *, tq, cols):
    bsz, s, _ = p16.shape
    nh = N_HEADS
    w = nh * HEAD_DIM
    kernel = functools.partial(_stickbreak_kernel, tq=tq)
    resident = dict(pipeline_mode=pl.Buffered(1))
    return pl.pallas_call(
        kernel,
        grid=(bsz, s // tq),
        in_specs=[pl.BlockSpec((None, tq, w), lambda b, i: (b, i, cols["qc"] // nh)),
                  pl.BlockSpec((None, s, w), lambda b, i: (b, 0, cols["kc"] // nh), **resident),
                  pl.BlockSpec((None, s, w), lambda b, i: (b, 0, cols["vc"] // nh), **resident)],
        out_specs=pl.BlockSpec((None, tq, w), lambda b, i: (b, i, 0)),
        out_shape=jax.ShapeDtypeStruct((bsz, s, w), BF16),
        scratch_shapes=[pltpu.VMEM((tq, w), F32)],
        compiler_params=pltpu.CompilerParams(
            dimension_semantics=("parallel", "arbitrary"), vmem_limit_bytes=VMEM_LIMIT),
        name="stickbreak",
    )(p16, p16, p16)


def _dsa_kernel(qi_ref, smq_ref, q_ref, sm_ref, k_ref, vt_ref, bias_ref, o_ref,
                sc_ref, scb_ref, qct_ref, kc_ref, bd_ref, lg_ref, *, tq, k_sel, wi_lane, wide):
    i = pl.program_id(1)
    tk = tq
    d = HEAD_DIM
    nh = N_HEADS
    ksel = float(k_sel)
    per_wide = wide // tk
    n_wide = (i + per_wide) // per_wide
    sub = 2 * tk
    lane_q = _iota((1, tq), 1)

    def tree(parts, op):
        while len(parts) > 1:
            parts = [op(parts[j], parts[j + 1]) if j + 1 < len(parts) else parts[j]
                     for j in range(0, len(parts), 2)]
        return parts[0]

    def col_fold(x, op=jnp.add, rows=8):
        return tree([x[r * rows:(r + 1) * rows] for r in range(x.shape[0] // rows)], op)

    @pl.when(i == 0)
    def _():
        def prep(g, carry):
            g0 = pl.multiple_of(g * wide, wide)
            hi, lo = _split(sm_ref[pl.ds(g0, wide), :][:, :IDX_DIM])
            kc_ref[pl.ds(g0, wide), :] = jnp.concatenate([hi, lo, hi], axis=1)
            return carry
        lax.fori_loop(0, sm_ref.shape[0] // wide, prep, 0)

    qit = qi_ref[...].T
    for p in range(IDX_HEADS // 2):
        halves = []
        for hh in (2 * p, 2 * p + 1):
            hi, lo = _split(qit[hh * IDX_DIM:(hh + 1) * IDX_DIM, :])
            halves.append(jnp.concatenate([hi, hi, lo], axis=0))
        qct_ref[p] = jnp.concatenate(halves, axis=1)
    w_rows = smq_ref[...].T[wi_lane:wi_lane + IDX_HEADS, :] * ((IDX_HEADS ** -0.5) * (IDX_DIM ** -0.5))

    q2t = (q_ref[...] * ((d ** -0.5) * LOG2E)).T.astype(BF16)
    zero_dq = jnp.zeros((d, tq), BF16)
    for p in range(nh // 2):
        top = jnp.concatenate([q2t[2 * p * d:(2 * p + 1) * d], zero_dq], axis=1)
        bot = jnp.concatenate([zero_dq, q2t[(2 * p + 1) * d:(2 * p + 2) * d]], axis=1)
        bd_ref[p] = jnp.concatenate([top, bot], axis=0)

    limit = i * tq + (lane_q // CHUNK + 1) * CHUNK

    rows_s = _iota((sub, tq), 0)

    def score_groups(gs, mm, masked):
        mn, mx = mm
        k0s = [pl.multiple_of(g * wide + sb * sub, sub) for g in gs for sb in range(wide // sub)]
        keys = [kc_ref[pl.ds(k0, sub), :] for k0 in k0s]
        accs = [jnp.zeros((sub, tq), F32) for _ in k0s]
        for p in range(IDX_HEADS // 2):
            rhs = qct_ref[p]
            for n, kk in enumerate(keys):
                s2 = jnp.dot(kk, rhs, preferred_element_type=F32)
                accs[n] = (accs[n] + jnp.maximum(s2[:, :tq], 0.0) * w_rows[2 * p:2 * p + 1, :]
                           + jnp.maximum(s2[:, tq:], 0.0) * w_rows[2 * p + 1:2 * p + 2, :])
        for k0, sct in zip(k0s, accs):
            if masked:
                adm = (k0 + rows_s) < limit
                mn = jnp.minimum(mn, col_fold(jnp.where(adm, sct, jnp.inf), jnp.minimum))
                sct = jnp.where(adm, sct, -jnp.inf)
            else:
                mn = jnp.minimum(mn, col_fold(sct, jnp.minimum))
            mx = jnp.maximum(mx, col_fold(sct, jnp.maximum))
            sc_ref[pl.ds(k0, sub), :] = sct
            scb_ref[pl.ds(k0, sub), :] = _floor_bf16(sct)
        return mn, mx

    def score_pair(j, mm):
        return score_groups((2 * j, 2 * j + 1), mm, False)

    n_full = n_wide - 1
    mm = lax.fori_loop(0, n_full // 2, score_pair,
                       (jnp.full((8, tq), jnp.inf, F32), jnp.full((8, tq), -jnp.inf, F32)))
    mm = lax.cond(n_full % 2 == 1, lambda c: score_groups((n_full - 1,), c, False), lambda c: c, mm)
    mn, mx = score_groups((n_wide - 1,), mm, True)

    n_pairs = (n_wide + 1) // 2

    @pl.when(n_wide % 2 == 1)
    def _():
        sc_ref[pl.ds(pl.multiple_of(n_wide * wide, wide), wide), :] = jnp.full((wide, tq), -jnp.inf, F32)
        scb_ref[pl.ds(pl.multiple_of(n_wide * wide, wide), wide), :] = jnp.full((wide, tq), -jnp.inf, BF16)
    rmin = jnp.min(mn, axis=0, keepdims=True)
    rmax = jnp.max(mx, axis=0, keepdims=True)

    def count(pred):
        def body(j, acc):
            for g in (2 * j, 2 * j + 1):
                acc = acc + col_fold(pred(sc_ref[pl.ds(pl.multiple_of(g * wide, wide), wide), :]))
            return acc
        return jnp.sum(lax.fori_loop(0, n_pairs, body, jnp.zeros((8, tq), F32)), axis=0, keepdims=True)

    def max_below(x):
        def body(j, acc):
            for g in (2 * j, 2 * j + 1):
                blk = sc_ref[pl.ds(pl.multiple_of(g * wide, wide), wide), :]
                acc = jnp.maximum(acc, col_fold(jnp.where(blk < x, blk, -jnp.inf), jnp.maximum))
            return acc
        return jnp.max(lax.fori_loop(0, n_pairs, body, jnp.full((8, tq), -jnp.inf, F32)), axis=0, keepdims=True)

    n_adm = limit.astype(F32)
    all_sel = n_adm <= ksel

    def bisect(c):
        lo, hi, c_lo = c
        mid = 0.5 * lo + 0.5 * hi
        cm = count(lambda blk: _ind(blk >= mid))
        ge = cm >= ksel
        return jnp.where(ge, mid, lo), jnp.where(ge, hi, mid), jnp.where(ge, cm, c_lo)

    def pending(c_lo, tied):
        return jnp.where(all_sel, 0.0, jnp.where(tied > 0.5, 0.0, _ind(c_lo != ksel)))

    def bisect_coarse(_, c):
        lo, hi, c_lo = c
        mid = _floor_bf16(0.5 * lo + 0.5 * hi).astype(F32)
        t_b = jnp.broadcast_to(mid, (16, tq)).astype(BF16)
        one_b = jnp.ones((16, tq), BF16)
        zero_b = jnp.zeros((16, tq), BF16)

        def body(j, acc):
            for g in (2 * j, 2 * j + 1):
                blk = scb_ref[pl.ds(pl.multiple_of(g * wide, wide), wide), :]
                ind = [jnp.where(blk[r * 16:(r + 1) * 16] >= t_b, one_b, zero_b) for r in range(wide // 16)]
                acc = acc + tree(ind, jnp.add).astype(F32)
            return acc

        acc = lax.fori_loop(0, n_pairs, body, jnp.zeros((16, tq), F32))
        cm = jnp.sum(acc, axis=0, keepdims=True)
        ge = cm >= ksel
        return jnp.where(ge, mid, lo), jnp.where(ge, hi, mid), jnp.where(ge, cm, c_lo)

    lo0 = _floor_bf16(rmin).astype(F32)
    hi0 = _floor_bf16(rmax + (jnp.abs(rmax) * (2.0 ** -6) + 1e-30)).astype(F32)
    state = lax.fori_loop(0, BISECT_COARSE, bisect_coarse, (lo0, hi0, n_adm))
    state = lax.fori_loop(0, BISECT_FIXED, lambda _, c: bisect(c), state)

    def round_cond(c):
        return jnp.max(pending(c[0][2], c[1])) > 0.5

    def round_body(c):
        st, tied, v, need = c

        def more_cond(s):
            return jnp.logical_and(s[0] < BISECT_EXTRA, jnp.max(pending(s[1][2], tied)) > 0.5)

        _, st = lax.while_loop(more_cond, lambda s: (s[0] + 1, bisect(s[1])), (jnp.int32(0), st))
        pend = pending(st[2], tied)

        def check(_):
            cand = max_below(st[1])
            c_ge = count(lambda blk: _ind(blk >= cand))
            c_gt = count(lambda blk: _ind(blk > cand))
            ok = jnp.where(pend > 0.5, _ind(c_ge >= ksel), 0.0)
            return (jnp.where(ok > 0.5, 1.0, tied), jnp.where(ok > 0.5, cand, v),
                    jnp.where(ok > 0.5, ksel - c_gt, need))

        tied, v, need = lax.cond(jnp.max(pend) > 0.5, check, lambda _: (tied, v, need), 0)
        return st, tied, v, need

    zeros1 = jnp.zeros((1, tq), F32)
    (lo_f, _, _), tied, v_tie, need = lax.while_loop(round_cond, round_body, (state, zeros1, zeros1, zeros1))
    vth = jnp.where(all_sel, F32_LOWEST, jnp.where(tied > 0.5, v_tie, lo_f))

    @pl.when(jnp.max(tied) > 0.5)
    def _():
        v_eq = jnp.where(tied > 0.5, v_tie, jnp.inf)
        incl = (_iota((tk, tk), 1) <= _iota((tk, tk), 0)).astype(BF16)

        def demote(g, seen):
            g0 = pl.multiple_of(g * wide, wide)
            xs = [sc_ref[pl.ds(g0 + pb * tk, tk), :] for pb in range(per_wide)]
            eqs = [_ind(x == v_eq) for x in xs]
            inblk = [jnp.dot(incl, e.astype(BF16), preferred_element_type=F32) for e in eqs]
            for pb in range(per_wide):
                rank = inblk[pb] + seen
                sc_ref[pl.ds(g0 + pb * tk, tk), :] = jnp.where(eqs[pb] * _ind(rank > need) > 0.5,
                                                               -jnp.inf, xs[pb])
                seen = seen + jnp.sum(col_fold(eqs[pb]), axis=0, keepdims=True)
            return seen

        lax.fori_loop(0, n_wide, demote, zeros1)

    g_near = jnp.maximum(i - 1, 0) // per_wide

    def logit_group(g, mx, near):
        out = list(mx)
        for sb in range(wide // sub):
            k0 = pl.multiple_of(g * wide + sb * sub, sub)
            sel = sc_ref[pl.ds(k0, sub), :] >= vth
            for p in range(nh // 2):
                pair = jnp.dot(k_ref[pl.ds(k0, sub), 2 * p * d:(2 * p + 2) * d], bd_ref[p],
                               preferred_element_type=F32)
                for hh in (2 * p, 2 * p + 1):
                    lm = pair[:, (hh - 2 * p) * tq:(hh - 2 * p + 1) * tq]
                    if near:
                        back = [jnp.clip(i - (g * per_wide + sb * (sub // tk) + pb), 0, 2)
                                for pb in range(sub // tk)]
                        lm = lm + jnp.concatenate([bias_ref[bk, hh] for bk in back], axis=0)
                    lm = jnp.where(sel, lm, NEG_BIG)
                    lg_ref[hh, pl.ds(k0, sub), :] = lm
                    out[hh] = jnp.maximum(out[hh], col_fold(lm, jnp.maximum))
        return tuple(out)

    mx = tuple(jnp.full((8, tq), NEG_BIG, F32) for _ in range(nh))
    def logit_pair(j, mx, near):
        return logit_group(2 * j + 1, logit_group(2 * j, mx, near), near)

    far_pairs = g_near // 2
    full_pairs = n_wide // 2
    odd = n_wide % 2 == 1
    mx = lax.fori_loop(0, far_pairs, functools.partial(logit_pair, near=False), mx)
    mx = lax.fori_loop(far_pairs, full_pairs, functools.partial(logit_pair, near=True), mx)
    mx = lax.cond(odd, lambda m: logit_group(n_wide - 1, m, True), lambda m: m, mx)
    m_q = [jnp.max(mx[hh], axis=0, keepdims=True) for hh in range(nh)]

    ones_rows = jnp.ones((8, wide), BF16)

    def pv_groups(gs, carry):
        ls, accs = list(carry[0]), list(carry[1])
        jobs = [(pl.multiple_of(g * wide, wide), hh) for g in gs for hh in range(nh)]
        ps = [jnp.exp2(lg_ref[hh, pl.ds(g0, wide), :] - m_q[hh]).astype(BF16) for g0, hh in jobs]
        outs = [jnp.dot(jnp.concatenate([vt_ref[hh * d:(hh + 1) * d, pl.ds(g0, wide)], ones_rows], axis=0),
                        p, preferred_element_type=F32) for (g0, hh), p in zip(jobs, ps)]
        for (_, hh), out in zip(jobs, outs):
            ls[hh] = ls[hh] + out[d:]
            accs[hh] = accs[hh] + out[:d]
        return tuple(ls), tuple(accs)

    acc = lax.fori_loop(0, full_pairs, lambda j, cr: pv_groups((2 * j, 2 * j + 1), cr),
                        (tuple(jnp.zeros((8, tq), F32) for _ in range(nh)),
                         tuple(jnp.zeros((d, tq), F32) for _ in range(nh))))
    ls, accs = lax.cond(odd, lambda cr: pv_groups((n_wide - 1,), cr), lambda cr: cr, acc)
    for hh in range(nh):
        o_ref[:, hh * d:(hh + 1) * d] = (accs[hh] / ls[hh][0:1]).T.astype(o_ref.dtype)


def _dsa(p32, p16, vt, bias_tiles, *, tq, cols):
    bsz, s, _ = p32.shape
    d = HEAD_DIM
    nh = N_HEADS
    wide = 4 * tq
    k_sel = min(TOPK_MAX, s // 4)
    w512 = nh * d
    kernel = functools.partial(_dsa_kernel, tq=tq, k_sel=k_sel, wi_lane=cols["wi_lane"], wide=wide)
    resident = dict(pipeline_mode=pl.Buffered(1))
    return pl.pallas_call(
        kernel,
        grid=(bsz, s // tq),
        in_specs=[pl.BlockSpec((None, tq, w512), lambda b, i: (b, i, cols["qi"] // nh)),
                  pl.BlockSpec((None, tq, d), lambda b, i: (b, i, cols["small"])),
                  pl.BlockSpec((None, tq, w512), lambda b, i: (b, i, cols["qb"] // nh)),
                  pl.BlockSpec((None, s, d), lambda b, i: (b, 0, cols["small"]), **resident),
                  pl.BlockSpec((None, s, w512), lambda b, i: (b, 0, cols["kb"] // nh), **resident),
                  pl.BlockSpec((w512, s), lambda b, i: (0, b), **resident),
                  pl.BlockSpec((3, nh, tq, tq), lambda b, i: (0, 0, 0, 0), **resident)],
        out_specs=pl.BlockSpec((None, tq, w512), lambda b, i: (b, i, 0)),
        out_shape=jax.ShapeDtypeStruct((bsz, s, w512), BF16),
        scratch_shapes=[pltpu.VMEM((s, tq), F32),
                        pltpu.VMEM((s, tq), BF16),
                        pltpu.VMEM((IDX_HEADS // 2, 3 * IDX_DIM, 2 * tq), BF16),
                        pltpu.VMEM((s, 3 * IDX_DIM), BF16),
                        pltpu.VMEM((nh // 2, 2 * d, 2 * tq), BF16),
                        pltpu.VMEM((nh, s, tq), F32)],
        compiler_params=pltpu.CompilerParams(
            dimension_semantics=("parallel", "arbitrary"), vmem_limit_bytes=VMEM_LIMIT),
        name="dsa",
    )(p32, p32, p32, p32, p16, vt, bias_tiles)


def _t5_bucket(rel):
    nb = REL_BUCKETS // 2
    max_exact = nb // 2
    ret = jnp.where(rel > 0, nb, 0)
    n = jnp.abs(rel)
    large = max_exact + (jnp.log(jnp.maximum(n, 1).astype(F32) / max_exact)
                         / math.log(REL_MAX_DIST / max_exact) * (nb - max_exact)).astype(jnp.int32)
    large = jnp.minimum(large, nb - 1)
    return ret + jnp.where(n < max_exact, n, large)


def _bias_tiles(rel_table, tq):
    assert tq >= REL_MAX_DIST
    t = jnp.arange(tq)
    back = jnp.arange(3)
    rel = (t[None, None, :] - back[:, None, None] * tq) - t[None, :, None]
    onehot = (_t5_bucket(rel)[..., None] == jnp.arange(REL_BUCKETS)).astype(F32)
    tiles = jnp.einsum("bqkn,nh->bhkq", onehot, rel_table.astype(F32),
                       precision=HIGHEST)
    return (tiles - tiles[2:3]) * LOG2E


def _even_layout(w_in):
    d = HEAD_DIM
    a_w = 2 * N_HEADS * d + N_HEADS * d
    offs = {}
    o = 0
    for name, w in (("qkv", a_w), ("z", N_HEADS * d), ("a", N_HEADS), ("b", N_HEADS),
                    ("qb", N_HEADS * d), ("kb", N_HEADS * d), ("vb", N_HEADS * d),
                    ("qi", IDX_HEADS * IDX_DIM), ("ki", IDX_DIM), ("wi", IDX_HEADS)):
        offs[name] = (o, o + w)
        o += w
    assert o == w_in.shape[1]
    sl = lambda n: w_in[:, offs[n][0]:offs[n][1]]
    small_w = IDX_DIM + 2 * N_HEADS + IDX_HEADS
    small_pad = -small_w % d
    zeros = lambda n: jnp.zeros((w_in.shape[0], n), w_in.dtype)
    w32 = jnp.concatenate([sl("qkv"), sl("z"), sl("qb"), sl("qi"),
                           sl("ki"), sl("a"), sl("b"), sl("wi"), zeros(small_pad)], axis=1)
    n32 = w32.shape[1]
    tn = n32 // 5
    assert tn * 5 == n32 and tn % d == 0
    w16 = jnp.concatenate([sl("kb"), zeros(tn - N_HEADS * d)], axis=1)
    nh = N_HEADS
    cols = dict(qa=0, ka=nh, va=2 * nh, za=3 * nh, qb=4 * nh, qi=5 * nh, small=6 * nh, kb=0,
                a_lane=IDX_DIM, b_lane=IDX_DIM + nh, wi_lane=IDX_DIM + 2 * nh, n32=n32, tn=tn)
    return jnp.concatenate([w32, w16], axis=1).astype(BF16), sl("vb").T.astype(BF16), cols


def kernel(x, norm_g, w_in_even, conv_w_even, a_log_even, dt_bias_even, a_norm_even, w_out_even,
           rel_bias, w_in_odd, lb_logits, d_norm_odd, w_out_odd, w_gate, w_up, w_down):
    bsz, s, d = x.shape
    t = bsz * s
    depth = norm_g.shape[0]
    nh = N_HEADS
    tq = Q_TILE
    lb_all = jnp.cumsum(jax.nn.softmax(lb_logits.astype(F32), axis=0), axis=0)
    lb_all = lb_all - lb_all[:1]
    odd_cols = dict(qc=0, kc=nh, vc=2 * nh, qd=0, fd=nh, id=2 * nh, gd=3 * nh)
    bias_tiles = _bias_tiles(rel_bias, tq)

    h = x.reshape(t, d)
    for l in range(depth):
        if l % 2 == 0:
            e = l // 2
            w_even, w_vt, cols = _even_layout(w_in_even[e])
            p32, p16, vt = _norm_matmul(h, norm_g[l, 0], w_even, tm=PROJ_TILE, tn=cols["tn"], n32=cols["n32"],
                                        w_t=w_vt)
            p32 = p32.reshape(bsz, s, -1)
            p16 = p16.reshape(bsz, s, -1)
            o_1 = _deltanet(p32, conv_w_even[e], a_log_even[e], dt_bias_even[e], a_norm_even[e],
                            ts=min(DELTANET_TILE, s), cols=cols)
            o_2 = _dsa(p32, p16, vt, bias_tiles, tq=tq, cols=cols)
            w_out = w_out_even[e]
        else:
            o = l // 2
            n16 = 3 * nh * HEAD_DIM
            w_odd = jnp.concatenate([w_in_odd[o][:, n16:], w_in_odd[o][:, :n16]], axis=1).astype(BF16)
            p32, p16 = _norm_matmul(h, norm_g[l, 0], w_odd, tm=PROJ_TILE, tn=ODD_COL_TILE, n32=w_odd.shape[1] - n16)
            p32 = p32.reshape(bsz, s, -1)
            p16 = p16.reshape(bsz, s, -1)
            o_1 = _stickbreak(p16, tq=tq, cols=odd_cols)
            o_2 = _hgrn2(p32, lb_all[l], d_norm_odd[o], ts=min(SEQ_TILE, s), cols=odd_cols)
            w_out = w_out_odd[o]
        h = _mix_ffn(o_1.reshape(t, -1), o_2.reshape(t, -1), w_out, h, norm_g[l, 1], norm_g[l, 2], norm_g[l, 3],
                     w_gate[l], w_up[l], w_down[l], tm=ROW_TILE, tf=FFN_TILE)
    return h.reshape(bsz, s, d)
```

```python
import functools
import math

import jax
import jax.numpy as jnp
from jax import lax
from jax.experimental import pallas as pl
from jax.experimental.pallas import tpu as pltpu

F32 = jnp.float32
BF16 = jnp.bfloat16
HIGHEST = lax.Precision.HIGHEST

CHUNK = 64
HEAD_DIM = 128
N_HEADS = 4
IDX_HEADS = 8
IDX_DIM = 64
TOPK_MAX = 256
CONV_WIDTH = 4
REL_BUCKETS = 32
REL_MAX_DIST = 128
EPS = 1e-6
NEG_BIG = -1e30
LOG2E = 1.4426950408889634
BISECT_COARSE = 10
BISECT_FIXED = 8
BISECT_EXTRA = 6
F32_LOWEST = -3.4028234663852886e38
EXP_ZERO_BELOW = -104.0
VMEM_LIMIT = 56 * 1024 * 1024

PROJ_TILE = 2048
ROW_TILE = 512
DELTANET_TILE = 1024
SEQ_TILE = 512
Q_TILE = 128
ODD_COL_TILE = 512
FFN_TILE = 2816


def _mm(a, b):
    return jnp.dot(a.astype(BF16), b.astype(BF16), preferred_element_type=F32)


def _mm_nt(a, b):
    return lax.dot_general(a.astype(BF16), b.astype(BF16), (((1,), (1,)), ((), ())),
                           preferred_element_type=F32)


def _mm_tn(a, b):
    return lax.dot_general(a.astype(BF16), b.astype(BF16), (((0,), (0,)), ((), ())),
                           preferred_element_type=F32)


def _split(x):
    hi = x.astype(BF16)
    return hi, (x - hi.astype(F32)).astype(BF16)


def _floor_bf16(x):
    bits = pltpu.bitcast(x, jnp.int32)
    down = jnp.where(bits >= 0, bits, bits + 0xFFFF) & jnp.int32(-65536)
    return pltpu.bitcast(down, F32).astype(BF16)


def _sigmoid(x):
    return 1.0 / (1.0 + jnp.exp(-x))


def _silu(x):
    return x * _sigmoid(x)


def _softplus(x):
    return jnp.maximum(x, 0.0) + jnp.log1p(jnp.exp(-jnp.abs(x)))


def _rms(x, g):
    return x * lax.rsqrt(jnp.mean(x * x, axis=-1, keepdims=True) + EPS) * g


def _iota(shape, dim):
    return lax.broadcasted_iota(jnp.int32, shape, dim)


def _ind(mask):
    return jnp.where(mask, 1.0, 0.0)


def _norm_matmul_kernel(x_ref, g_ref, w_ref, *rest, n_t, tiles32):
    if n_t:
        wt_ref, o32_ref, o16_ref, ot_ref, xn_ref = rest
    else:
        o32_ref, o16_ref, xn_ref = rest
    j = pl.program_id(1)

    @pl.when(j == 0)
    def _():
        xn_ref[...] = _rms(x_ref[...], g_ref[...]).astype(BF16)
        if n_t:
            ot_ref[...] = lax.dot_general(wt_ref[...], xn_ref[...], (((1,), (1,)), ((), ())),
                                          preferred_element_type=F32).astype(BF16)

    y = jnp.dot(xn_ref[...], w_ref[...], preferred_element_type=F32)

    @pl.when(j < tiles32)
    def _():
        o32_ref[...] = y

    @pl.when(j >= tiles32)
    def _():
        o16_ref[...] = y.astype(BF16)


def _norm_matmul(x, g, w, *, tm, tn, n32, w_t=None):
    t, d = x.shape
    n = w.shape[1]
    n_t = 0 if w_t is None else w_t.shape[0]
    tiles32 = n32 // tn
    assert tiles32 * tn == n32 and (n - n32) % tn == 0 and 0 < n32 < n
    in_specs = [pl.BlockSpec((tm, d), lambda i, j: (i, 0)),
                pl.BlockSpec((1, d), lambda i, j: (0, 0)),
                pl.BlockSpec((d, tn), lambda i, j: (0, j))]
    out_specs = [pl.BlockSpec((tm, tn), lambda i, j: (i, jnp.minimum(j, tiles32 - 1))),
                 pl.BlockSpec((tm, tn), lambda i, j: (i, jnp.maximum(j - tiles32, 0)))]
    out_shape = [jax.ShapeDtypeStruct((t, n32), F32), jax.ShapeDtypeStruct((t, n - n32), BF16)]
    args = [x, g.reshape(1, d), w]
    if n_t:
        in_specs.append(pl.BlockSpec((n_t, d), lambda i, j: (0, 0)))
        out_specs.append(pl.BlockSpec((n_t, tm), lambda i, j: (0, i)))
        out_shape.append(jax.ShapeDtypeStruct((n_t, t), BF16))
        args.append(w_t)
    return pl.pallas_call(
        functools.partial(_norm_matmul_kernel, n_t=n_t, tiles32=tiles32),
        grid=(t // tm, n // tn),
        in_specs=in_specs,
        out_specs=out_specs,
        out_shape=out_shape,
        scratch_shapes=[pltpu.VMEM((tm, d), BF16)],
        compiler_params=pltpu.CompilerParams(
            dimension_semantics=("parallel", "arbitrary"), vmem_limit_bytes=VMEM_LIMIT),
        name="norm_matmul",
    )(*args)


def _mix_ffn_kernel(ca_ref, cb_ref, wa_ref, wb_ref, h_ref, gmix_ref, gpre_ref, gpost_ref,
                    wg_ref, wu_ref, wd_ref, o_ref, h1_ref, xn_ref, acc_ref):
    f = pl.program_id(1)

    @pl.when(f == 0)
    def _():
        y = (jnp.dot(ca_ref[...], wa_ref[...], preferred_element_type=F32)
             + jnp.dot(cb_ref[...], wb_ref[...], preferred_element_type=F32))
        h1 = h_ref[...] + _rms(y, gmix_ref[...])
        h1_ref[...] = h1
        xn_ref[...] = _rms(h1, gpre_ref[...]).astype(BF16)
        acc_ref[...] = jnp.zeros_like(acc_ref)

    xn = xn_ref[...]
    gate = jnp.dot(xn, wg_ref[...], preferred_element_type=F32)
    up = jnp.dot(xn, wu_ref[...], preferred_element_type=F32)
    act = (_silu(gate) * up).astype(BF16)
    acc_ref[...] += jnp.dot(act, wd_ref[...], preferred_element_type=F32)

    @pl.when(f == pl.num_programs(1) - 1)
    def _():
        o_ref[...] = h1_ref[...] + _rms(acc_ref[...], gpost_ref[...])


def _mix_ffn(ca, cb, w_out, h, g_mix, g_pre, g_post, wg, wu, wd, *, tm, tf):
    t, d = h.shape
    ff = wg.shape[1]
    wa_n = ca.shape[1]
    wb_n = cb.shape[1]
    row = pl.BlockSpec((1, d), lambda i, f: (0, 0))
    once = dict(pipeline_mode=pl.Buffered(1)) if tf == ff else {}
    return pl.pallas_call(
        _mix_ffn_kernel,
        grid=(t // tm, ff // tf),
        in_specs=[pl.BlockSpec((tm, wa_n), lambda i, f: (i, 0)),
                  pl.BlockSpec((tm, wb_n), lambda i, f: (i, 0)),
                  pl.BlockSpec((wa_n, d), lambda i, f: (0, 0)),
                  pl.BlockSpec((wb_n, d), lambda i, f: (0, 0)),
                  pl.BlockSpec((tm, d), lambda i, f: (i, 0)),
                  row, row, row,
                  pl.BlockSpec((d, tf), lambda i, f: (0, f), **once),
                  pl.BlockSpec((d, tf), lambda i, f: (0, f), **once),
                  pl.BlockSpec((tf, d), lambda i, f: (f, 0), **once)],
        out_specs=pl.BlockSpec((tm, d), lambda i, f: (i, 0)),
        out_shape=jax.ShapeDtypeStruct((t, d), F32),
        scratch_shapes=[pltpu.VMEM((tm, d), F32), pltpu.VMEM((tm, d), BF16), pltpu.VMEM((tm, d), F32)],
        compiler_params=pltpu.CompilerParams(
            dimension_semantics=("parallel", "arbitrary"), vmem_limit_bytes=VMEM_LIMIT),
        name="mix_ffn",
    )(ca, cb, w_out[:wa_n].astype(BF16), w_out[wa_n:].astype(BF16), h,
      g_mix.reshape(1, d), g_pre.reshape(1, d), g_post.reshape(1, d),
      wg.astype(BF16), wu.astype(BF16), wd.astype(BF16))


def _deltanet_kernel(xq_ref, xk_ref, xv_ref, z_ref, sm_ref, cwq_ref, cwk_ref, cwv_ref,
                     alog_ref, dtb_ref, gn_ref, o_ref,
                     xpad_ref, q_ref, k_ref, v_ref, gb_ref, bb_ref, u_ref, w_ref, qk_ref, st_ref,
                     *, ts, a_col, b_col):
    s = pl.program_id(1)
    c = CHUNK
    d = HEAD_DIM
    nh = N_HEADS

    @pl.when(s == 0)
    def _():
        xpad_ref[:, 0:8, :] = jnp.zeros((3, 8, nh * d), F32)
        st_ref[...] = jnp.zeros_like(st_ref)

    @pl.when(s != 0)
    def _():
        xpad_ref[:, 0:8, :] = xpad_ref[:, ts:ts + 8, :]

    xpad_ref[0, 8:ts + 8, :] = xq_ref[...]
    xpad_ref[1, 8:ts + 8, :] = xk_ref[...]
    xpad_ref[2, 8:ts + 8, :] = xv_ref[...]

    def conv_silu(idx, cw_ref, hs):
        cw = cw_ref[:, hs]
        acc = xpad_ref[idx, 8 - (CONV_WIDTH - 1):8 - (CONV_WIDTH - 1) + ts, hs] * cw[0:1, :]
        for j in range(1, CONV_WIDTH):
            off = 8 - (CONV_WIDTH - 1) + j
            acc = acc + xpad_ref[idx, off:off + ts, hs] * cw[j:j + 1, :]
        return _silu(acc)

    def l2norm(t):
        return t * lax.rsqrt(jnp.sum(t * t, axis=-1, keepdims=True) + EPS)

    row = _iota((c, c), 0)
    col = _iota((c, c), 1)
    tri = (col <= row)
    strict = (col < row)
    tri_f = tri.astype(F32)
    upper_f = (row <= col).astype(F32)
    eye = (row == col).astype(F32)
    gnorm = gn_ref[...]
    chunks = range(ts // c)
    rs = [slice(ci * c, (ci + 1) * c) for ci in chunks]
    tri2 = jnp.concatenate([tri_f, tri_f], axis=1).astype(BF16)
    ones2 = jnp.ones((c, 2 * c), BF16)

    def cum2(lhs2, x):
        hi, lo = _split(x)
        return jnp.dot(lhs2, jnp.concatenate([hi, lo], axis=0), preferred_element_type=F32)

    for hh in range(nh):
        hs = slice(hh * d, (hh + 1) * d)
        q_ref[:, hs] = l2norm(conv_silu(0, cwq_ref, hs)) * (d ** -0.5)
        k_ref[:, hs] = l2norm(conv_silu(1, cwk_ref, hs))
        v_ref[:, hs] = conv_silu(2, cwv_ref, hs)

        a_raw = sm_ref[:, a_col + hh:a_col + hh + 1]
        b_raw = sm_ref[:, b_col + hh:b_col + hh + 1]
        g = -jnp.exp(alog_ref[:, hh:hh + 1]) * _softplus(a_raw + dtb_ref[:, hh:hh + 1])
        gb_ref[:, hs] = jnp.broadcast_to(g, (ts, d))
        bb_ref[:, hs] = jnp.broadcast_to(_sigmoid(b_raw), (ts, d))

        q = [q_ref[r, hs] for r in rs]
        k = [k_ref[r, hs] for r in rs]
        beta = [bb_ref[r, hs] for r in rs]
        gb = [gb_ref[r, hs] for r in rs]
        gc = [cum2(tri2, x) for x in gb]
        gc_row = [cum2(ones2, x[:, :c] * upper_f) for x in gb]
        decay = [jnp.where(tri, jnp.exp(jnp.minimum(a[:, :c] - b, 0.0)), 0.0) for a, b in zip(gc, gc_row)]
        kk = [_mm_nt(x, x) for x in k]
        n = [-jnp.where(strict, b[:, :c] * x * dc, 0.0) for b, x, dc in zip(beta, kk, decay)]
        inv = [eye + x for x in n]
        for step in range(5):
            nb = [x.astype(BF16) for x in n]
            n = [jnp.dot(x, x, preferred_element_type=F32) for x in nb]
            inv = [iv + _mm(iv, x) for iv, x in zip(inv, n)]
        egc = [jnp.exp(x) for x in gc]
        gl = [x[c - 1:c, :] for x in gc]
        inv_l = [x.astype(BF16) for x in inv]
        u = [_mm(a, v_ref[r, hs] * b) for a, r, b in zip(inv_l, rs, beta)]
        w = [_mm(a, x * (b * e)) for a, x, b, e in zip(inv_l, k, beta, egc)]
        qk = [_mm_nt(a, b) * dc for a, b, dc in zip(q, k, decay)]
        for ci in chunks:
            r = rs[ci]
            u_ref[r, hs] = u[ci]
            w_ref[r, hs] = w[ci]
            qk_ref[hh, r, :] = qk[ci]
            q_ref[r, hs] = q[ci] * egc[ci]
            k_ref[r, hs] = k[ci] * jnp.exp(gl[ci] - gc[ci])
            gb_ref[r, hs] = jnp.broadcast_to(jnp.exp(gl[ci]), (c, d))

    hss = [slice(hh * d, (hh + 1) * d) for hh in range(nh)]

    def chunk_step(ci, st):
        r0 = pl.multiple_of(ci * c, c)
        rows = pl.ds(r0, c)
        w_st = [_mm(w_ref[rows, hs], s_) for hs, s_ in zip(hss, st)]
        q_st = [_mm(q_ref[rows, hs], s_) for hs, s_ in zip(hss, st)]
        v_new = [u_ref[rows, hs] - x for hs, x in zip(hss, w_st)]
        o = [a + _mm(qk_ref[hh, rows, :], v) for hh, (a, v) in enumerate(zip(q_st, v_new))]
        kv = [_mm_tn(k_ref[rows, hs], v) for hs, v in zip(hss, v_new)]
        for hh, hs in enumerate(hss):
            o_ref[rows, hs] = (_rms(o[hh], gnorm) * _silu(z_ref[rows, hs])).astype(o_ref.dtype)
        return [s_ * gb_ref[pl.ds(r0, 1), hs] + x for s_, hs, x in zip(st, hss, kv)]

    def chunk_pair(j, carry):
        st = chunk_step(2 * j + 1, chunk_step(2 * j, [st_ref[hh] for hh in range(nh)]))
        for hh in range(nh):
            st_ref[hh] = st[hh]
        return carry

    assert (ts // c) % 2 == 0
    lax.fori_loop(0, ts // (2 * c), chunk_pair, 0)


def _deltanet(p32, conv_w, a_log, dt_bias, a_norm_g, *, ts, cols):
    bsz, s, _ = p32.shape
    d = HEAD_DIM
    nh = N_HEADS
    w = nh * d
    pad = lambda t: jnp.pad(t.astype(F32), (0, d - t.shape[0])).reshape(1, d)
    kernel = functools.partial(_deltanet_kernel, ts=ts, a_col=cols["a_lane"], b_col=cols["b_lane"])
    tile = lambda name: pl.BlockSpec((None, ts, w), lambda b, i: (b, i, cols[name] // nh))
    conv = lambda k: pl.BlockSpec((CONV_WIDTH, w), lambda b, i: (0, k))
    row = pl.BlockSpec((1, d), lambda b, i: (0, 0))
    return pl.pallas_call(
        kernel,
        grid=(bsz, s // ts),
        in_specs=[tile("qa"), tile("ka"), tile("va"), tile("za"),
                  pl.BlockSpec((None, ts, d), lambda b, i: (b, i, cols["small"])),
                  conv(0), conv(1), conv(2), row, row, row],
        out_specs=pl.BlockSpec((None, ts, w), lambda b, i: (b, i, 0)),
        out_shape=jax.ShapeDtypeStruct((bsz, s, w), BF16),
        scratch_shapes=[pltpu.VMEM((3, ts + 8, w), F32)]
        + [pltpu.VMEM((ts, w), F32) for _ in range(7)]
        + [pltpu.VMEM((nh, ts, CHUNK), F32), pltpu.VMEM((nh, d, d), F32)],
        compiler_params=pltpu.CompilerParams(
            dimension_semantics=("parallel", "arbitrary"), vmem_limit_bytes=VMEM_LIMIT),
        name="deltanet",
    )(p32, p32, p32, p32, p32, conv_w.astype(F32), conv_w.astype(F32), conv_w.astype(F32),
      pad(a_log), pad(dt_bias), a_norm_g.astype(F32).reshape(1, d))


def _hgrn2_kernel(q_ref, f_ref, i_ref, gate_ref, lb_ref, gn_ref, o_ref,
                  qs_ref, ks_ref, gc_ref, st_ref, *, ts):
    s = pl.program_id(1)
    c = CHUNK
    d = HEAD_DIM
    nh = N_HEADS
    SUB = 16

    @pl.when(s == 0)
    def _():
        st_ref[...] = jnp.zeros_like(st_ref)

    lb = lb_ref[...]
    f_raw = f_ref[...]
    log_sig = jnp.minimum(f_raw, 0.0) - jnp.log1p(jnp.exp(-jnp.abs(f_raw)))
    la = jnp.log(lb)
    lbb = jnp.log1p(-lb) + log_sig
    log_f = jnp.maximum(la, lbb) + jnp.log1p(jnp.exp(-jnp.abs(la - lbb)))
    qs_ref[...] = _silu(q_ref[...])
    ks_ref[...] = (1.0 - lb) * _sigmoid(-f_raw)

    row = _iota((c, c), 0)
    col = _iota((c, c), 1)
    tri_f = (col <= row).astype(F32)
    ones_dd = jnp.ones((d, d), BF16)
    rows_8d = _iota((8, d), 0)
    gnorm = gn_ref[...]

    tri2 = jnp.concatenate([tri_f, tri_f], axis=1).astype(BF16)
    for ci in range(ts // c):
        hi, lo = _split(log_f[ci * c:(ci + 1) * c, :])
        gc_ref[ci * c:(ci + 1) * c, :] = jnp.dot(tri2, jnp.concatenate([hi, lo], axis=0),
                                                 preferred_element_type=F32)

    blocks = [(sb * SUB, (sb + 1) * SUB) for sb in range(c // SUB)]

    def chunk_loop(ci, carry):
        r0 = pl.multiple_of(ci * c, c)
        rows = pl.ds(r0, c)
        hss = [slice(hh * d, (hh + 1) * d) for hh in range(nh)]
        q = [qs_ref[rows, hs] for hs in hss]
        k = [ks_ref[rows, hs] for hs in hss]
        v = [i_ref[rows, hs] for hs in hss]
        gc = [gc_ref[rows, hs] for hs in hss]

        def near_products(q, k, gc):
            prods = []
            for top, end in blocks:
                for j in range(top, end):
                    lo = (j // 8) * 8
                    e = jnp.exp2(gc[lo:end, :] - gc[j:j + 1, :])
                    if j % 8:
                        head = jnp.where(rows_8d >= j - lo, e[:8], 0.0)
                        e = jnp.concatenate([head, e[8:]], axis=0) if lo + 8 < end else head
                    prods.append(q[lo:end, :] * k[j:j + 1, :] * e)
            return jnp.concatenate(prods, axis=0).astype(BF16)

        def far_operands(q, k, gc):
            out = []
            for top, end in blocks[1:]:
                g_b = gc[top - 1:top, :]
                out.append((q[top:end, :] * jnp.exp(gc[top:end, :] - g_b),
                            k[:top, :] * jnp.exp(jnp.minimum(g_b - gc[:top, :], 0.0))))
            return out

        near = [near_products(a, b, g * LOG2E) for a, b, g in zip(q, k, gc)]
        far_ops = [far_operands(*x) for x in zip(q, k, gc)]
        st = [st_ref[hh] for hh in range(nh)]
        gl = [x[c - 1:c, :] for x in gc]
        sums = [jnp.dot(x, ones_dd, preferred_element_type=F32) for x in near]
        qk_far = [[_mm_nt(qe, ke) for qe, ke in ops] for ops in far_ops]
        far = [[_mm(a, vv[:top, :]) for a, (top, _) in zip(qs, blocks[1:])] for qs, vv in zip(qk_far, v)]
        o_st = [_mm_nt(a * jnp.exp(g), s_) for a, g, s_ in zip(q, gc, st)]
        kv = [_mm_tn(vv, kk * jnp.exp(g_l - g)) for vv, kk, g_l, g in zip(v, k, gl, gc)]

        for hh, hs in enumerate(hss):
            groups = [jnp.zeros((8, d), F32) for _ in range(c // 8)]
            at = 0
            for top, end in blocks:
                for j in range(top, end):
                    v_j = v[hh][j:j + 1, :]
                    for g in range(j // 8, end // 8):
                        groups[g] = groups[g] + sums[hh][at:at + 8, :] * v_j
                        at += 8
            for f, (top, end) in zip(far[hh], blocks[1:]):
                for g in range(top // 8, end // 8):
                    groups[g] = groups[g] + f[(g * 8 - top):(g * 8 - top + 8), :]
            o = jnp.concatenate(groups, axis=0) + o_st[hh]
            st_ref[hh] = st[hh] * jnp.exp(gl[hh]) + kv[hh]
            o_ref[rows, hs] = (_rms(o, gnorm) * _silu(gate_ref[rows, hs])).astype(o_ref.dtype)
        return carry

    lax.fori_loop(0, ts // c, chunk_loop, 0)


def _hgrn2(p32, lb, d_norm_g, *, ts, cols):
    bsz, s, _ = p32.shape
    d = HEAD_DIM
    nh = N_HEADS
    w = nh * d
    kernel = functools.partial(_hgrn2_kernel, ts=ts)
    tile = lambda name: pl.BlockSpec((None, ts, w), lambda b, i: (b, i, cols[name] // nh))
    return pl.pallas_call(
        kernel,
        grid=(bsz, s // ts),
        in_specs=[tile("qd"), tile("fd"), tile("id"), tile("gd"),
                  pl.BlockSpec((1, w), lambda b, i: (0, 0)),
                  pl.BlockSpec((1, d), lambda b, i: (0, 0))],
        out_specs=pl.BlockSpec((None, ts, w), lambda b, i: (b, i, 0)),
        out_shape=jax.ShapeDtypeStruct((bsz, s, w), BF16),
        scratch_shapes=[pltpu.VMEM((ts, w), F32), pltpu.VMEM((ts, w), F32),
                        pltpu.VMEM((ts, w), F32), pltpu.VMEM((nh, d, d), F32)],
        compiler_params=pltpu.CompilerParams(
            dimension_semantics=("parallel", "arbitrary"), vmem_limit_bytes=VMEM_LIMIT),
        name="hgrn2",
    )(p32, p32, p32, p32, lb.astype(F32).reshape(1, w), d_norm_g.astype(F32).reshape(1, d))


def _stickbreak_kernel(q_ref, k_ref, v_ref, o_ref, acc_ref, *, tq):
    i = pl.program_id(1)
    d = HEAD_DIM
    nh = N_HEADS
    row = _iota((tq, tq), 0)
    col = _iota((tq, tq), 1)
    causal = col < row
    later = (row > col).astype(BF16)
    later2 = jnp.concatenate([later, later], axis=0)

    heads = [slice(hh * d, (hh + 1) * d) for hh in range(nh)]

    def scores(blocks):
        jobs = [(j, dg, hs) for j, dg in blocks for hs in heads]
        z = [_mm_nt(q_ref[:, hs], k_ref[pl.ds(pl.multiple_of(j * tq, tq), tq), hs]) * (d ** -0.5)
             for j, _, hs in jobs]
        sp = [_softplus(x) for x in z]
        l1m = [jnp.where(causal, -x, 0.0) if dg else -x for x, (_, dg, _) in zip(sp, jobs)]
        rest = [jnp.dot(jnp.concatenate(_split(x), axis=1), later2, preferred_element_type=F32)
                for x in l1m]
        out = [((a - b) + r, l) for a, b, r, l in zip(z, sp, rest, l1m)]
        return [out[b * nh:(b + 1) * nh] for b in range(len(blocks))]

    def block(j, carries):
        (sc,) = scores([(j, False)])
        ps = [jnp.exp(logw + c) for (logw, _), c in zip(sc, carries)]
        pv = [_mm(p, v_ref[pl.ds(pl.multiple_of(j * tq, tq), tq), hs]) for p, hs in zip(ps, heads)]
        for hs, x in zip(heads, pv):
            acc_ref[:, hs] += x
        return tuple(c + jnp.sum(l1m, axis=-1, keepdims=True) for (_, l1m), c in zip(sc, carries))

    j1 = jnp.maximum(i - 1, 0)
    j2 = jnp.maximum(i - 2, 0)
    live1 = jnp.where(i > 0, 1.0, 0.0)
    live2 = jnp.where(i > 1, 1.0, 0.0)
    s0, s1, s2 = scores([(i, True), (j1, False), (j2, False)])
    carries = []
    for hh, hs in enumerate(heads):
        c0 = jnp.sum(s0[hh][1], axis=-1, keepdims=True)
        c1 = c0 + jnp.sum(s1[hh][1], axis=-1, keepdims=True)
        p0 = jnp.where(causal, jnp.exp(s0[hh][0]), 0.0)
        p1 = jnp.exp(s1[hh][0] + c0) * live1
        p2 = jnp.exp(s2[hh][0] + c1) * live2
        acc_ref[:, hs] = (_mm(p0, v_ref[pl.ds(pl.multiple_of(i * tq, tq), tq), hs])
                          + _mm(p1, v_ref[pl.ds(pl.multiple_of(j1 * tq, tq), tq), hs])
                          + _mm(p2, v_ref[pl.ds(pl.multiple_of(j2 * tq, tq), tq), hs]))
        carries.append(c1 + jnp.sum(s2[hh][1], axis=-1, keepdims=True))
    carries = tuple(carries)

    def cond(c):
        worst = functools.reduce(jnp.maximum, c[1])
        return jnp.logical_and(c[0] >= 0, jnp.max(worst) >= EXP_ZERO_BELOW)

    def body(c):
        return c[0] - 1, block(c[0], c[1])

    lax.while_loop(cond, body, (i - 3, carries))
    o_ref[...] = acc_ref[...].astype(o_ref.dtype)


def _stickbreak(p16, *, tq, cols):
    bsz, s, _ = p16.shape
    nh = N_HEADS
    w = nh * HEAD_DIM
    kernel = functools.partial(_stickbreak_kernel, tq=tq)
    resident = dict(pipeline_mode=pl.Buffered(1))
    return pl.pallas_call(
        kernel,
        grid=(bsz, s // tq),
        in_specs=[pl.BlockSpec((None, tq, w), lambda b, i: (b, i, cols["qc"] // nh)),
                  pl.BlockSpec((None, s, w), lambda b, i: (b, 0, cols["kc"] // nh), **resident),
                  pl.BlockSpec((None, s, w), lambda b, i: (b, 0, cols["vc"] // nh), **resident)],
        out_specs=pl.BlockSpec((None, tq, w), lambda b, i: (b, i, 0)),
        out_shape=jax.ShapeDtypeStruct((bsz, s, w), BF16),
        scratch_shapes=[pltpu.VMEM((tq, w), F32)],
        compiler_params=pltpu.CompilerParams(
            dimension_semantics=("parallel", "arbitrary"), vmem_limit_bytes=VMEM_LIMIT),
        name="stickbreak",
    )(p16, p16, p16)


def _dsa_kernel(qi_ref, smq_ref, q_ref, sm_ref, k_ref, vt_ref, bias_ref, o_ref,
                sc_ref, scb_ref, qct_ref, kc_ref, bd_ref, lg_ref, *, tq, k_sel, wi_lane, wide):
    i = pl.program_id(1)
    tk = tq
    d = HEAD_DIM
    nh = N_HEADS
    ksel = float(k_sel)
    per_wide = wide // tk
    n_wide = (i + per_wide) // per_wide
    sub = 2 * tk
    lane_q = _iota((1, tq), 1)

    def tree(parts, op):
        while len(parts) > 1:
            parts = [op(parts[j], parts[j + 1]) if j + 1 < len(parts) else parts[j]
                     for j in range(0, len(parts), 2)]
        return parts[0]

    def col_fold(x, op=jnp.add, rows=8):
        return tree([x[r * rows:(r + 1) * rows] for r in range(x.shape[0] // rows)], op)

    @pl.when(i == 0)
    def _():
        def prep(g, carry):
            g0 = pl.multiple_of(g * wide, wide)
            hi, lo = _split(sm_ref[pl.ds(g0, wide), :][:, :IDX_DIM])
            kc_ref[pl.ds(g0, wide), :] = jnp.concatenate([hi, lo, hi], axis=1)
            return carry
        lax.fori_loop(0, sm_ref.shape[0] // wide, prep, 0)

    qit = qi_ref[...].T
    for p in range(IDX_HEADS // 2):
        halves = []
        for hh in (2 * p, 2 * p + 1):
            hi, lo = _split(qit[hh * IDX_DIM:(hh + 1) * IDX_DIM, :])
            halves.append(jnp.concatenate([hi, hi, lo], axis=0))
        qct_ref[p] = jnp.concatenate(halves, axis=1)
    w_rows = smq_ref[...].T[wi_lane:wi_lane + IDX_HEADS, :] * ((IDX_HEADS ** -0.5) * (IDX_DIM ** -0.5))

    q2t = (q_ref[...] * ((d ** -0.5) * LOG2E)).T.astype(BF16)
    zero_dq = jnp.zeros((d, tq), BF16)
    for p in range(nh // 2):
        top = jnp.concatenate([q2t[2 * p * d:(2 * p + 1) * d], zero_dq], axis=1)
        bot = jnp.concatenate([zero_dq, q2t[(2 * p + 1) * d:(2 * p + 2) * d]], axis=1)
        bd_ref[p] = jnp.concatenate([top, bot], axis=0)

    limit = i * tq + (lane_q // CHUNK + 1) * CHUNK

    rows_s = _iota((sub, tq), 0)

    def score_groups(gs, mm, masked):
        mn, mx = mm
        k0s = [pl.multiple_of(g * wide + sb * sub, sub) for g in gs for sb in range(wide // sub)]
        keys = [kc_ref[pl.ds(k0, sub), :] for k0 in k0s]
        accs = [jnp.zeros((sub, tq), F32) for _ in k0s]
        for p in range(IDX_HEADS // 2):
            rhs = qct_ref[p]
            for n, kk in enumerate(keys):
                s2 = jnp.dot(kk, rhs, preferred_element_type=F32)
                accs[n] = (accs[n] + jnp.maximum(s2[:, :tq], 0.0) * w_rows[2 * p:2 * p + 1, :]
                           + jnp.maximum(s2[:, tq:], 0.0) * w_rows[2 * p + 1:2 * p + 2, :])
        for k0, sct in zip(k0s, accs):
            if masked:
                adm = (k0 + rows_s) < limit
                mn = jnp.minimum(mn, col_fold(jnp.where(adm, sct, jnp.inf), jnp.minimum))
                sct = jnp.where(adm, sct, -jnp.inf)
            else:
                mn = jnp.minimum(mn, col_fold(sct, jnp.minimum))
            mx = jnp.maximum(mx, col_fold(sct, jnp.maximum))
            sc_ref[pl.ds(k0, sub), :] = sct
            scb_ref[pl.ds(k0, sub), :] = _floor_bf16(sct)
        return mn, mx

    def score_pair(j, mm):
        return score_groups((2 * j, 2 * j + 1), mm, False)

    n_full = n_wide - 1
    mm = lax.fori_loop(0, n_full // 2, score_pair,
                       (jnp.full((8, tq), jnp.inf, F32), jnp.full((8, tq), -jnp.inf, F32)))
    mm = lax.cond(n_full % 2 == 1, lambda c: score_groups((n_full - 1,), c, False), lambda c: c, mm)
    mn, mx = score_groups((n_wide - 1,), mm, True)

    n_pairs = (n_wide + 1) // 2

    @pl.when(n_wide % 2 == 1)
    def _():
        sc_ref[pl.ds(pl.multiple_of(n_wide * wide, wide), wide), :] = jnp.full((wide, tq), -jnp.inf, F32)
        scb_ref[pl.ds(pl.multiple_of(n_wide * wide, wide), wide), :] = jnp.full((wide, tq), -jnp.inf, BF16)
    rmin = jnp.min(mn, axis=0, keepdims=True)
    rmax = jnp.max(mx, axis=0, keepdims=True)

    def count(pred):
        def body(j, acc):
            for g in (2 * j, 2 * j + 1):
                acc = acc + col_fold(pred(sc_ref[pl.ds(pl.multiple_of(g * wide, wide), wide), :]))
            return acc
        return jnp.sum(lax.fori_loop(0, n_pairs, body, jnp.zeros((8, tq), F32)), axis=0, keepdims=True)

    def max_below(x):
        def body(j, acc):
            for g in (2 * j, 2 * j + 1):
                blk = sc_ref[pl.ds(pl.multiple_of(g * wide, wide), wide), :]
                acc = jnp.maximum(acc, col_fold(jnp.where(blk < x, blk, -jnp.inf), jnp.maximum))
            return acc
        return jnp.max(lax.fori_loop(0, n_pairs, body, jnp.full((8, tq), -jnp.inf, F32)), axis=0, keepdims=True)

    n_adm = limit.astype(F32)
    all_sel = n_adm <= ksel

    def bisect(c):
        lo, hi, c_lo = c
        mid = 0.5 * lo + 0.5 * hi
        cm = count(lambda blk: _ind(blk >= mid))
        ge = cm >= ksel
        return jnp.where(ge, mid, lo), jnp.where(ge, hi, mid), jnp.where(ge, cm, c_lo)

    def pending(c_lo, tied):
        return jnp.where(all_sel, 0.0, jnp.where(tied > 0.5, 0.0, _ind(c_lo != ksel)))

    def bisect_coarse(_, c):
        lo, hi, c_lo = c
        mid = _floor_bf16(0.5 * lo + 0.5 * hi).astype(F32)
        t_b = jnp.broadcast_to(mid, (16, tq)).astype(BF16)
        one_b = jnp.ones((16, tq), BF16)
        zero_b = jnp.zeros((16, tq), BF16)

        def body(j, acc):
            for g in (2 * j, 2 * j + 1):
                blk = scb_ref[pl.ds(pl.multiple_of(g * wide, wide), wide), :]
                ind = [jnp.where(blk[r * 16:(r + 1) * 16] >= t_b, one_b, zero_b) for r in range(wide // 16)]
                acc = acc + tree(ind, jnp.add).astype(F32)
            return acc

        acc = lax.fori_loop(0, n_pairs, body, jnp.zeros((16, tq), F32))
        cm = jnp.sum(acc, axis=0, keepdims=True)
        ge = cm >= ksel
        return jnp.where(ge, mid, lo), jnp.where(ge, hi, mid), jnp.where(ge, cm, c_lo)

    lo0 = _floor_bf16(rmin).astype(F32)
    hi0 = _floor_bf16(rmax + (jnp.abs(rmax) * (2.0 ** -6) + 1e-30)).astype(F32)
    state = lax.fori_loop(0, BISECT_COARSE, bisect_coarse, (lo0, hi0, n_adm))
    state = lax.fori_loop(0, BISECT_FIXED, lambda _, c: bisect(c), state)

    def round_cond(c):
        return jnp.max(pending(c[0][2], c[1])) > 0.5

    def round_body(c):
        st, tied, v, need = c

        def more_cond(s):
            return jnp.logical_and(s[0] < BISECT_EXTRA, jnp.max(pending(s[1][2], tied)) > 0.5)

        _, st = lax.while_loop(more_cond, lambda s: (s[0] + 1, bisect(s[1])), (jnp.int32(0), st))
        pend = pending(st[2], tied)

        def check(_):
            cand = max_below(st[1])
            c_ge = count(lambda blk: _ind(blk >= cand))
            c_gt = count(lambda blk: _ind(blk > cand))
            ok = jnp.where(pend > 0.5, _ind(c_ge >= ksel), 0.0)
            return (jnp.where(ok > 0.5, 1.0, tied), jnp.where(ok > 0.5, cand, v),
                    jnp.where(ok > 0.5, ksel - c_gt, need))

        tied, v, need = lax.cond(jnp.max(pend) > 0.5, check, lambda _: (tied, v, need), 0)
        return st, tied, v, need

    zeros1 = jnp.zeros((1, tq), F32)
    (lo_f, _, _), tied, v_tie, need = lax.while_loop(round_cond, round_body, (state, zeros1, zeros1, zeros1))
    vth = jnp.where(all_sel, F32_LOWEST, jnp.where(tied > 0.5, v_tie, lo_f))

    @pl.when(jnp.max(tied) > 0.5)
    def _():
        v_eq = jnp.where(tied > 0.5, v_tie, jnp.inf)
        incl = (_iota((tk, tk), 1) <= _iota((tk, tk), 0)).astype(BF16)

        def demote(g, seen):
            g0 = pl.multiple_of(g * wide, wide)
            xs = [sc_ref[pl.ds(g0 + pb * tk, tk), :] for pb in range(per_wide)]
            eqs = [_ind(x == v_eq) for x in xs]
            inblk = [jnp.dot(incl, e.astype(BF16), preferred_element_type=F32) for e in eqs]
            for pb in range(per_wide):
                rank = inblk[pb] + seen
                sc_ref[pl.ds(g0 + pb * tk, tk), :] = jnp.where(eqs[pb] * _ind(rank > need) > 0.5,
                                                               -jnp.inf, xs[pb])
                seen = seen + jnp.sum(col_fold(eqs[pb]), axis=0, keepdims=True)
            return seen

        lax.fori_loop(0, n_wide, demote, zeros1)

    g_near = jnp.maximum(i - 1, 0) // per_wide

    def logit_group(g, mx, near):
        out = list(mx)
        for sb in range(wide // sub):
            k0 = pl.multiple_of(g * wide + sb * sub, sub)
            sel = sc_ref[pl.ds(k0, sub), :] >= vth
            for p in range(nh // 2):
                pair = jnp.dot(k_ref[pl.ds(k0, sub), 2 * p * d:(2 * p + 2) * d], bd_ref[p],
                               preferred_element_type=F32)
                for hh in (2 * p, 2 * p + 1):
                    lm = pair[:, (hh - 2 * p) * tq:(hh - 2 * p + 1) * tq]
                    if near:
                        back = [jnp.clip(i - (g * per_wide + sb * (sub // tk) + pb), 0, 2)
                                for pb in range(sub // tk)]
                        lm = lm + jnp.concatenate([bias_ref[bk, hh] for bk in back], axis=0)
                    lm = jnp.where(sel, lm, NEG_BIG)
                    lg_ref[hh, pl.ds(k0, sub), :] = lm
                    out[hh] = jnp.maximum(out[hh], col_fold(lm, jnp.maximum))
        return tuple(out)

    mx = tuple(jnp.full((8, tq), NEG_BIG, F32) for _ in range(nh))
    def logit_pair(j, mx, near):
        return logit_group(2 * j + 1, logit_group(2 * j, mx, near), near)

    far_pairs = g_near // 2
    full_pairs = n_wide // 2
    odd = n_wide % 2 == 1
    mx = lax.fori_loop(0, far_pairs, functools.partial(logit_pair, near=False), mx)
    mx = lax.fori_loop(far_pairs, full_pairs, functools.partial(logit_pair, near=True), mx)
    mx = lax.cond(odd, lambda m: logit_group(n_wide - 1, m, True), lambda m: m, mx)
    m_q = [jnp.max(mx[hh], axis=0, keepdims=True) for hh in range(nh)]

    ones_rows = jnp.ones((8, wide), BF16)

    def pv_groups(gs, carry):
        ls, accs = list(carry[0]), list(carry[1])
        jobs = [(pl.multiple_of(g * wide, wide), hh) for g in gs for hh in range(nh)]
        ps = [jnp.exp2(lg_ref[hh, pl.ds(g0, wide), :] - m_q[hh]).astype(BF16) for g0, hh in jobs]
        outs = [jnp.dot(jnp.concatenate([vt_ref[hh * d:(hh + 1) * d, pl.ds(g0, wide)], ones_rows], axis=0),
                        p, preferred_element_type=F32) for (g0, hh), p in zip(jobs, ps)]
        for (_, hh), out in zip(jobs, outs):
            ls[hh] = ls[hh] + out[d:]
            accs[hh] = accs[hh] + out[:d]
        return tuple(ls), tuple(accs)

    acc = lax.fori_loop(0, full_pairs, lambda j, cr: pv_groups((2 * j, 2 * j + 1), cr),
                        (tuple(jnp.zeros((8, tq), F32) for _ in range(nh)),
                         tuple(jnp.zeros((d, tq), F32) for _ in range(nh))))
    ls, accs = lax.cond(odd, lambda cr: pv_groups((n_wide - 1,), cr), lambda cr: cr, acc)
    for hh in range(nh):
        o_ref[:, hh * d:(hh + 1) * d] = (accs[hh] / ls[hh][0:1]).T.astype(o_ref.dtype)


def _dsa(p32, p16, vt, bias_tiles, *, tq, cols):
    bsz, s, _ = p32.shape
    d = HEAD_DIM
    nh = N_HEADS
    wide = 4 * tq
    k_sel = min(TOPK_MAX, s // 4)
    w512 = nh * d
    kernel = functools.partial(_dsa_kernel, tq=tq, k_sel=k_sel, wi_lane=cols["wi_lane"], wide=wide)
    resident = dict(pipeline_mode=pl.Buffered(1))
    return pl.pallas_call(
        kernel,
        grid=(bsz, s // tq),
        in_specs=[pl.BlockSpec((None, tq, w512), lambda b, i: (b, i, cols["qi"] // nh)),
                  pl.BlockSpec((None, tq, d), lambda b, i: (b, i, cols["small"])),
                  pl.BlockSpec((None, tq, w512), lambda b, i: (b, i, cols["qb"] // nh)),
                  pl.BlockSpec((None, s, d), lambda b, i: (b, 0, cols["small"]), **resident),
                  pl.BlockSpec((None, s, w512), lambda b, i: (b, 0, cols["kb"] // nh), **resident),
                  pl.BlockSpec((w512, s), lambda b, i: (0, b), **resident),
                  pl.BlockSpec((3, nh, tq, tq), lambda b, i: (0, 0, 0, 0), **resident)],
        out_specs=pl.BlockSpec((None, tq, w512), lambda b, i: (b, i, 0)),
        out_shape=jax.ShapeDtypeStruct((bsz, s, w512), BF16),
        scratch_shapes=[pltpu.VMEM((s, tq), F32),
                        pltpu.VMEM((s, tq), BF16),
                        pltpu.VMEM((IDX_HEADS // 2, 3 * IDX_DIM, 2 * tq), BF16),
                        pltpu.VMEM((s, 3 * IDX_DIM), BF16),
                        pltpu.VMEM((nh // 2, 2 * d, 2 * tq), BF16),
                        pltpu.VMEM((nh, s, tq), F32)],
        compiler_params=pltpu.CompilerParams(
            dimension_semantics=("parallel", "arbitrary"), vmem_limit_bytes=VMEM_LIMIT),
        name="dsa",
    )(p32, p32, p32, p32, p16, vt, bias_tiles)


def _t5_bucket(rel):
    nb = REL_BUCKETS // 2
    max_exact = nb // 2
    ret = jnp.where(rel > 0, nb, 0)
    n = jnp.abs(rel)
    large = max_exact + (jnp.log(jnp.maximum(n, 1).astype(F32) / max_exact)
                         / math.log(REL_MAX_DIST / max_exact) * (nb - max_exact)).astype(jnp.int32)
    large = jnp.minimum(large, nb - 1)
    return ret + jnp.where(n < max_exact, n, large)


def _bias_tiles(rel_table, tq):
    assert tq >= REL_MAX_DIST
    t = jnp.arange(tq)
    back = jnp.arange(3)
    rel = (t[None, None, :] - back[:, None, None] * tq) - t[None, :, None]
    onehot = (_t5_bucket(rel)[..., None] == jnp.arange(REL_BUCKETS)).astype(F32)
    tiles = jnp.einsum("bqkn,nh->bhkq", onehot, rel_table.astype(F32),
                       precision=HIGHEST)
    return (tiles - tiles[2:3]) * LOG2E


def _even_layout(w_in):
    d = HEAD_DIM
    a_w = 2 * N_HEADS * d + N_HEADS * d
    offs = {}
    o = 0
    for name, w in (("qkv", a_w), ("z", N_HEADS * d), ("a", N_HEADS), ("b", N_HEADS),
                    ("qb", N_HEADS * d), ("kb", N_HEADS * d), ("vb", N_HEADS * d),
                    ("qi", IDX_HEADS * IDX_DIM), ("ki", IDX_DIM), ("wi", IDX_HEADS)):
        offs[name] = (o, o + w)
        o += w
    assert o == w_in.shape[1]
    sl = lambda n: w_in[:, offs[n][0]:offs[n][1]]
    small_w = IDX_DIM + 2 * N_HEADS + IDX_HEADS
    small_pad = -small_w % d
    zeros = lambda n: jnp.zeros((w_in.shape[0], n), w_in.dtype)
    w32 = jnp.concatenate([sl("qkv"), sl("z"), sl("qb"), sl("qi"),
                           sl("ki"), sl("a"), sl("b"), sl("wi"), zeros(small_pad)], axis=1)
    n32 = w32.shape[1]
    tn = n32 // 5
    assert tn * 5 == n32 and tn % d == 0
    w16 = jnp.concatenate([sl("kb"), zeros(tn - N_HEADS * d)], axis=1)
    nh = N_HEADS
    cols = dict(qa=0, ka=nh, va=2 * nh, za=3 * nh, qb=4 * nh, qi=5 * nh, small=6 * nh, kb=0,
                a_lane=IDX_DIM, b_lane=IDX_DIM + nh, wi_lane=IDX_DIM + 2 * nh, n32=n32, tn=tn)
    return jnp.concatenate([w32, w16], axis=1).astype(BF16), sl("vb").T.astype(BF16), cols


def kernel(x, norm_g, w_in_even, conv_w_even, a_log_even, dt_bias_even, a_norm_even, w_out_even,
           rel_bias, w_in_odd, lb_logits, d_norm_odd, w_out_odd, w_gate, w_up, w_down):
    bsz, s, d = x.shape
    t = bsz * s
    depth = norm_g.shape[0]
    nh = N_HEADS
    tq = Q_TILE
    lb_all = jnp.cumsum(jax.nn.softmax(lb_logits.astype(F32), axis=0), axis=0)
    lb_all = lb_all - lb_all[:1]
    odd_cols = dict(qc=0, kc=nh, vc=2 * nh, qd=0, fd=nh, id=2 * nh, gd=3 * nh)
    bias_tiles = _bias_tiles(rel_bias, tq)

    h = x.reshape(t, d)
    for l in range(depth):
        if l % 2 == 0:
            e = l // 2
            w_even, w_vt, cols = _even_layout(w_in_even[e])
            p32, p16, vt = _norm_matmul(h, norm_g[l, 0], w_even, tm=PROJ_TILE, tn=cols["tn"], n32=cols["n32"],
                                        w_t=w_vt)
            p32 = p32.reshape(bsz, s, -1)
            p16 = p16.reshape(bsz, s, -1)
            o_1 = _deltanet(p32, conv_w_even[e], a_log_even[e], dt_bias_even[e], a_norm_even[e],
                            ts=min(DELTANET_TILE, s), cols=cols)
            o_2 = _dsa(p32, p16, vt, bias_tiles, tq=tq, cols=cols)
            w_out = w_out_even[e]
        else:
            o = l // 2
            n16 = 3 * nh * HEAD_DIM
            w_odd = jnp.concatenate([w_in_odd[o][:, n16:], w_in_odd[o][:, :n16]], axis=1).astype(BF16)
            p32, p16 = _norm_matmul(h, norm_g[l, 0], w_odd, tm=PROJ_TILE, tn=ODD_COL_TILE, n32=w_odd.shape[1] - n16)
            p32 = p32.reshape(bsz, s, -1)
            p16 = p16.reshape(bsz, s, -1)
            o_1 = _stickbreak(p16, tq=tq, cols=odd_cols)
            o_2 = _hgrn2(p32, lb_all[l], d_norm_odd[o], ts=min(SEQ_TILE, s), cols=odd_cols)
            w_out = w_out_odd[o]
        h = _mix_ffn(o_1.reshape(t, -1), o_2.reshape(t, -1), w_out, h, norm_g[l, 1], norm_g[l, 2], norm_g[l, 3],
                     w_gate[l], w_up[l], w_down[l], tm=ROW_TILE, tf=FFN_TILE)
    return h.reshape(bsz, s, d)
```

```python
import functools
import math

import jax
import jax.numpy as jnp
from jax import lax
from jax.experimental import pallas as pl
from jax.experimental.pallas import tpu as pltpu

F32 = jnp.float32
BF16 = jnp.bfloat16
HIGHEST = lax.Precision.HIGHEST

CHUNK = 64
HEAD_DIM = 128
N_HEADS = 4
IDX_HEADS = 8
IDX_DIM = 64
TOPK_MAX = 256
CONV_WIDTH = 4
REL_BUCKETS = 32
REL_MAX_DIST = 128
EPS = 1e-6
NEG_BIG = -1e30
LOG2E = 1.4426950408889634
BISECT_COARSE = 10
BISECT_FIXED = 8
BISECT_EXTRA = 6
F32_LOWEST = -3.4028234663852886e38
EXP_ZERO_BELOW = -104.0
VMEM_LIMIT = 56 * 1024 * 1024

PROJ_TILE = 2048
ROW_TILE = 512
DELTANET_TILE = 1024
SEQ_TILE = 512
Q_TILE = 128
ODD_COL_TILE = 512
FFN_TILE = 2816


def _mm(a, b):
    return jnp.dot(a.astype(BF16), b.astype(BF16), preferred_element_type=F32)


def _mm_nt(a, b):
    return lax.dot_general(a.astype(BF16), b.astype(BF16), (((1,), (1,)), ((), ())),
                           preferred_element_type=F32)


def _mm_tn(a, b):
    return lax.dot_general(a.astype(BF16), b.astype(BF16), (((0,), (0,)), ((), ())),
                           preferred_element_type=F32)


def _split(x):
    hi = x.astype(BF16)
    return hi, (x - hi.astype(F32)).astype(BF16)


def _floor_bf16(x):
    bits = pltpu.bitcast(x, jnp.int32)
    down = jnp.where(bits >= 0, bits, bits + 0xFFFF) & jnp.int32(-65536)
    return pltpu.bitcast(down, F32).astype(BF16)


def _sigmoid(x):
    return 1.0 / (1.0 + jnp.exp(-x))


def _silu(x):
    return x * _sigmoid(x)


def _softplus(x):
    return jnp.maximum(x, 0.0) + jnp.log1p(jnp.exp(-jnp.abs(x)))


def _rms(x, g):
    return x * lax.rsqrt(jnp.mean(x * x, axis=-1, keepdims=True) + EPS) * g


def _iota(shape, dim):
    return lax.broadcasted_iota(jnp.int32, shape, dim)


def _ind(mask):
    return jnp.where(mask, 1.0, 0.0)


def _norm_matmul_kernel(x_ref, g_ref, w_ref, *rest, n_t, tiles32):
    if n_t:
        wt_ref, o32_ref, o16_ref, ot_ref, xn_ref = rest
    else:
        o32_ref, o16_ref, xn_ref = rest
    j = pl.program_id(1)

    @pl.when(j == 0)
    def _():
        xn_ref[...] = _rms(x_ref[...], g_ref[...]).astype(BF16)
        if n_t:
            ot_ref[...] = lax.dot_general(wt_ref[...], xn_ref[...], (((1,), (1,)), ((), ())),
                                          preferred_element_type=F32).astype(BF16)

    y = jnp.dot(xn_ref[...], w_ref[...], preferred_element_type=F32)

    @pl.when(j < tiles32)
    def _():
        o32_ref[...] = y

    @pl.when(j >= tiles32)
    def _():
        o16_ref[...] = y.astype(BF16)


def _norm_matmul(x, g, w, *, tm, tn, n32, w_t=None):
    t, d = x.shape
    n = w.shape[1]
    n_t = 0 if w_t is None else w_t.shape[0]
    tiles32 = n32 // tn
    assert tiles32 * tn == n32 and (n - n32) % tn == 0 and 0 < n32 < n
    in_specs = [pl.BlockSpec((tm, d), lambda i, j: (i, 0)),
                pl.BlockSpec((1, d), lambda i, j: (0, 0)),
                pl.BlockSpec((d, tn), lambda i, j: (0, j))]
    out_specs = [pl.BlockSpec((tm, tn), lambda i, j: (i, jnp.minimum(j, tiles32 - 1))),
                 pl.BlockSpec((tm, tn), lambda i, j: (i, jnp.maximum(j - tiles32, 0)))]
    out_shape = [jax.ShapeDtypeStruct((t, n32), F32), jax.ShapeDtypeStruct((t, n - n32), BF16)]
    args = [x, g.reshape(1, d), w]
    if n_t:
        in_specs.append(pl.BlockSpec((n_t, d), lambda i, j: (0, 0)))
        out_specs.append(pl.BlockSpec((n_t, tm), lambda i, j: (0, i)))
        out_shape.append(jax.ShapeDtypeStruct((n_t, t), BF16))
        args.append(w_t)
    return pl.pallas_call(
        functools.partial(_norm_matmul_kernel, n_t=n_t, tiles32=tiles32),
        grid=(t // tm, n // tn),
        in_specs=in_specs,
        out_specs=out_specs,
        out_shape=out_shape,
        scratch_shapes=[pltpu.VMEM((tm, d), BF16)],
        compiler_params=pltpu.CompilerParams(
            dimension_semantics=("parallel", "arbitrary"), vmem_limit_bytes=VMEM_LIMIT),
        name="norm_matmul",
    )(*args)


def _mix_ffn_kernel(ca_ref, cb_ref, wa_ref, wb_ref, h_ref, gmix_ref, gpre_ref, gpost_ref,
                    wg_ref, wu_ref, wd_ref, o_ref, h1_ref, xn_ref, acc_ref):
    f = pl.program_id(1)

    @pl.when(f == 0)
    def _():
        y = (jnp.dot(ca_ref[...], wa_ref[...], preferred_element_type=F32)
             + jnp.dot(cb_ref[...], wb_ref[...], preferred_element_type=F32))
        h1 = h_ref[...] + _rms(y, gmix_ref[...])
        h1_ref[...] = h1
        xn_ref[...] = _rms(h1, gpre_ref[...]).astype(BF16)
        acc_ref[...] = jnp.zeros_like(acc_ref)

    xn = xn_ref[...]
    gate = jnp.dot(xn, wg_ref[...], preferred_element_type=F32)
    up = jnp.dot(xn, wu_ref[...], preferred_element_type=F32)
    act = (_silu(gate) * up).astype(BF16)
    acc_ref[...] += jnp.dot(act, wd_ref[...], preferred_element_type=F32)

    @pl.when(f == pl.num_programs(1) - 1)
    def _():
        o_ref[...] = h1_ref[...] + _rms(acc_ref[...], gpost_ref[...])


def _mix_ffn(ca, cb, w_out, h, g_mix, g_pre, g_post, wg, wu, wd, *, tm, tf):
    t, d = h.shape
    ff = wg.shape[1]
    wa_n = ca.shape[1]
    wb_n = cb.shape[1]
    row = pl.BlockSpec((1, d), lambda i, f: (0, 0))
    once = dict(pipeline_mode=pl.Buffered(1)) if tf == ff else {}
    return pl.pallas_call(
        _mix_ffn_kernel,
        grid=(t // tm, ff // tf),
        in_specs=[pl.BlockSpec((tm, wa_n), lambda i, f: (i, 0)),
                  pl.BlockSpec((tm, wb_n), lambda i, f: (i, 0)),
                  pl.BlockSpec((wa_n, d), lambda i, f: (0, 0)),
                  pl.BlockSpec((wb_n, d), lambda i, f: (0, 0)),
                  pl.BlockSpec((tm, d), lambda i, f: (i, 0)),
                  row, row, row,
                  pl.BlockSpec((d, tf), lambda i, f: (0, f), **once),
                  pl.BlockSpec((d, tf), lambda i, f: (0, f), **once),
                  pl.BlockSpec((tf, d), lambda i, f: (f, 0), **once)],
        out_specs=pl.BlockSpec((tm, d), lambda i, f: (i, 0)),
        out_shape=jax.ShapeDtypeStruct((t, d), F32),
        scratch_shapes=[pltpu.VMEM((tm, d), F32), pltpu.VMEM((tm, d), BF16), pltpu.VMEM((tm, d), F32)],
        compiler_params=pltpu.CompilerParams(
            dimension_semantics=("parallel", "arbitrary"), vmem_limit_bytes=VMEM_LIMIT),
        name="mix_ffn",
    )(ca, cb, w_out[:wa_n].astype(BF16), w_out[wa_n:].astype(BF16), h,
      g_mix.reshape(1, d), g_pre.reshape(1, d), g_post.reshape(1, d),
      wg.astype(BF16), wu.astype(BF16), wd.astype(BF16))


def _deltanet_kernel(xq_ref, xk_ref, xv_ref, z_ref, sm_ref, cwq_ref, cwk_ref, cwv_ref,
                     alog_ref, dtb_ref, gn_ref, o_ref,
                     xpad_ref, q_ref, k_ref, v_ref, gb_ref, bb_ref, u_ref, w_ref, qk_ref, st_ref,
                     *, ts, a_col, b_col):
    s = pl.program_id(1)
    c = CHUNK
    d = HEAD_DIM
    nh = N_HEADS

    @pl.when(s == 0)
    def _():
        xpad_ref[:, 0:8, :] = jnp.zeros((3, 8, nh * d), F32)
        st_ref[...] = jnp.zeros_like(st_ref)

    @pl.when(s != 0)
    def _():
        xpad_ref[:, 0:8, :] = xpad_ref[:, ts:ts + 8, :]

    xpad_ref[0, 8:ts + 8, :] = xq_ref[...]
    xpad_ref[1, 8:ts + 8, :] = xk_ref[...]
    xpad_ref[2, 8:ts + 8, :] = xv_ref[...]

    def conv_silu(idx, cw_ref, hs):
        cw = cw_ref[:, hs]
        acc = xpad_ref[idx, 8 - (CONV_WIDTH - 1):8 - (CONV_WIDTH - 1) + ts, hs] * cw[0:1, :]
        for j in range(1, CONV_WIDTH):
            off = 8 - (CONV_WIDTH - 1) + j
            acc = acc + xpad_ref[idx, off:off + ts, hs] * cw[j:j + 1, :]
        return _silu(acc)

    def l2norm(t):
        return t * lax.rsqrt(jnp.sum(t * t, axis=-1, keepdims=True) + EPS)

    row = _iota((c, c), 0)
    col = _iota((c, c), 1)
    tri = (col <= row)
    strict = (col < row)
    tri_f = tri.astype(F32)
    upper_f = (row <= col).astype(F32)
    eye = (row == col).astype(F32)
    gnorm = gn_ref[...]
    chunks = range(ts // c)
    rs = [slice(ci * c, (ci + 1) * c) for ci in chunks]
    tri2 = jnp.concatenate([tri_f, tri_f], axis=1).astype(BF16)
    ones2 = jnp.ones((c, 2 * c), BF16)

    def cum2(lhs2, x):
        hi, lo = _split(x)
        return jnp.dot(lhs2, jnp.concatenate([hi, lo], axis=0), preferred_element_type=F32)

    for hh in range(nh):
        hs = slice(hh * d, (hh + 1) * d)
        q_ref[:, hs] = l2norm(conv_silu(0, cwq_ref, hs)) * (d ** -0.5)
        k_ref[:, hs] = l2norm(conv_silu(1, cwk_ref, hs))
        v_ref[:, hs] = conv_silu(2, cwv_ref, hs)

        a_raw = sm_ref[:, a_col + hh:a_col + hh + 1]
        b_raw = sm_ref[:, b_col + hh:b_col + hh + 1]
        g = -jnp.exp(alog_ref[:, hh:hh + 1]) * _softplus(a_raw + dtb_ref[:, hh:hh + 1])
        gb_ref[:, hs] = jnp.broadcast_to(g, (ts, d))
        bb_ref[:, hs] = jnp.broadcast_to(_sigmoid(b_raw), (ts, d))

        q = [q_ref[r, hs] for r in rs]
        k = [k_ref[r, hs] for r in rs]
        beta = [bb_ref[r, hs] for r in rs]
        gb = [gb_ref[r, hs] for r in rs]
        gc = [cum2(tri2, x) for x in gb]
        gc_row = [cum2(ones2, x[:, :c] * upper_f) for x in gb]
        decay = [jnp.where(tri, jnp.exp(jnp.minimum(a[:, :c] - b, 0.0)), 0.0) for a, b in zip(gc, gc_row)]
        kk = [_mm_nt(x, x) for x in k]
        n = [-jnp.where(strict, b[:, :c] * x * dc, 0.0) for b, x, dc in zip(beta, kk, decay)]
        inv = [eye + x for x in n]
        for step in range(5):
            nb = [x.astype(BF16) for x in n]
            n = [jnp.dot(x, x, preferred_element_type=F32) for x in nb]
            inv = [iv + _mm(iv, x) for iv, x in zip(inv, n)]
        egc = [jnp.exp(x) for x in gc]
        gl = [x[c - 1:c, :] for x in gc]
        inv_l = [x.astype(BF16) for x in inv]
        u = [_mm(a, v_ref[r, hs] * b) for a, r, b in zip(inv_l, rs, beta)]
        w = [_mm(a, x * (b * e)) for a, x, b, e in zip(inv_l, k, beta, egc)]
        qk = [_mm_nt(a, b) * dc for a, b, dc in zip(q, k, decay)]
        for ci in chunks:
            r = rs[ci]
            u_ref[r, hs] = u[ci]
            w_ref[r, hs] = w[ci]
            qk_ref[hh, r, :] = qk[ci]
            q_ref[r, hs] = q[ci] * egc[ci]
            k_ref[r, hs] = k[ci] * jnp.exp(gl[ci] - gc[ci])
            gb_ref[r, hs] = jnp.broadcast_to(jnp.exp(gl[ci]), (c, d))

    hss = [slice(hh * d, (hh + 1) * d) for hh in range(nh)]

    def chunk_step(ci, st):
        r0 = pl.multiple_of(ci * c, c)
        rows = pl.ds(r0, c)
        w_st = [_mm(w_ref[rows, hs], s_) for hs, s_ in zip(hss, st)]
        q_st = [_mm(q_ref[rows, hs], s_) for hs, s_ in zip(hss, st)]
        v_new = [u_ref[rows, hs] - x for hs, x in zip(hss, w_st)]
        o = [a + _mm(qk_ref[hh, rows, :], v) for hh, (a, v) in enumerate(zip(q_st, v_new))]
        kv = [_mm_tn(k_ref[rows, hs], v) for hs, v in zip(hss, v_new)]
        for hh, hs in enumerate(hss):
            o_ref[rows, hs] = (_rms(o[hh], gnorm) * _silu(z_ref[rows, hs])).astype(o_ref.dtype)
        return [s_ * gb_ref[pl.ds(r0, 1), hs] + x for s_, hs, x in zip(st, hss, kv)]

    per_step = math.gcd(ts // c, 4)

    def chunk_group(j, carry):
        st = [st_ref[hh] for hh in range(nh)]
        for n_ in range(per_step):
            st = chunk_step(per_step * j + n_, st)
        for hh in range(nh):
            st_ref[hh] = st[hh]
        return carry

    lax.fori_loop(0, ts // (per_step * c), chunk_group, 0)


def _deltanet(p32, conv_w, a_log, dt_bias, a_norm_g, *, ts, cols):
    bsz, s, _ = p32.shape
    d = HEAD_DIM
    nh = N_HEADS
    w = nh * d
    pad = lambda t: jnp.pad(t.astype(F32), (0, d - t.shape[0])).reshape(1, d)
    kernel = functools.partial(_deltanet_kernel, ts=ts, a_col=cols["a_lane"], b_col=cols["b_lane"])
    tile = lambda name: pl.BlockSpec((None, ts, w), lambda b, i: (b, i, cols[name] // nh))
    conv = lambda k: pl.BlockSpec((CONV_WIDTH, w), lambda b, i: (0, k))
    row = pl.BlockSpec((1, d), lambda b, i: (0, 0))
    return pl.pallas_call(
        kernel,
        grid=(bsz, s // ts),
        in_specs=[tile("qa"), tile("ka"), tile("va"), tile("za"),
                  pl.BlockSpec((None, ts, d), lambda b, i: (b, i, cols["small"])),
                  conv(0), conv(1), conv(2), row, row, row],
        out_specs=pl.BlockSpec((None, ts, w), lambda b, i: (b, i, 0)),
        out_shape=jax.ShapeDtypeStruct((bsz, s, w), BF16),
        scratch_shapes=[pltpu.VMEM((3, ts + 8, w), F32)]
        + [pltpu.VMEM((ts, w), F32) for _ in range(7)]
        + [pltpu.VMEM((nh, ts, CHUNK), F32), pltpu.VMEM((nh, d, d), F32)],
        compiler_params=pltpu.CompilerParams(
            dimension_semantics=("parallel", "arbitrary"), vmem_limit_bytes=VMEM_LIMIT),
        name="deltanet",
    )(p32, p32, p32, p32, p32, conv_w.astype(F32), conv_w.astype(F32), conv_w.astype(F32),
      pad(a_log), pad(dt_bias), a_norm_g.astype(F32).reshape(1, d))


def _hgrn2_kernel(q_ref, f_ref, i_ref, gate_ref, lb_ref, gn_ref, o_ref,
                  qs_ref, ks_ref, gc_ref, st_ref, *, ts):
    s = pl.program_id(1)
    c = CHUNK
    d = HEAD_DIM
    nh = N_HEADS
    SUB = 16

    @pl.when(s == 0)
    def _():
        st_ref[...] = jnp.zeros_like(st_ref)

    lb = lb_ref[...]
    f_raw = f_ref[...]
    log_sig = jnp.minimum(f_raw, 0.0) - jnp.log1p(jnp.exp(-jnp.abs(f_raw)))
    la = jnp.log(lb)
    lbb = jnp.log1p(-lb) + log_sig
    log_f = jnp.maximum(la, lbb) + jnp.log1p(jnp.exp(-jnp.abs(la - lbb)))
    qs_ref[...] = _silu(q_ref[...])
    ks_ref[...] = (1.0 - lb) * _sigmoid(-f_raw)

    row = _iota((c, c), 0)
    col = _iota((c, c), 1)
    tri_f = (col <= row).astype(F32)
    ones_dd = jnp.ones((d, d), BF16)
    rows_8d = _iota((8, d), 0)
    gnorm = gn_ref[...]

    tri2 = jnp.concatenate([tri_f, tri_f], axis=1).astype(BF16)
    for ci in range(ts // c):
        hi, lo = _split(log_f[ci * c:(ci + 1) * c, :])
        gc_ref[ci * c:(ci + 1) * c, :] = jnp.dot(tri2, jnp.concatenate([hi, lo], axis=0),
                                                 preferred_element_type=F32)

    blocks = [(sb * SUB, (sb + 1) * SUB) for sb in range(c // SUB)]

    def chunk_loop(ci, carry):
        r0 = pl.multiple_of(ci * c, c)
        rows = pl.ds(r0, c)
        hss = [slice(hh * d, (hh + 1) * d) for hh in range(nh)]
        q = [qs_ref[rows, hs] for hs in hss]
        k = [ks_ref[rows, hs] for hs in hss]
        v = [i_ref[rows, hs] for hs in hss]
        gc = [gc_ref[rows, hs] for hs in hss]

        def near_products(q, k, gc):
            prods = []
            for top, end in blocks:
                for j in range(top, end):
                    lo = (j // 8) * 8
                    e = jnp.exp2(gc[lo:end, :] - gc[j:j + 1, :])
                    if j % 8:
                        head = jnp.where(rows_8d >= j - lo, e[:8], 0.0)
                        e = jnp.concatenate([head, e[8:]], axis=0) if lo + 8 < end else head
                    prods.append(q[lo:end, :] * k[j:j + 1, :] * e)
            return jnp.concatenate(prods, axis=0).astype(BF16)

        def far_operands(q, k, gc):
            out = []
            for top, end in blocks[1:]:
                g_b = gc[top - 1:top, :]
                out.append((q[top:end, :] * jnp.exp(gc[top:end, :] - g_b),
                            k[:top, :] * jnp.exp(jnp.minimum(g_b - gc[:top, :], 0.0))))
            return out

        near = [near_products(a, b, g * LOG2E) for a, b, g in zip(q, k, gc)]
        far_ops = [far_operands(*x) for x in zip(q, k, gc)]
        st = [st_ref[hh] for hh in range(nh)]
        gl = [x[c - 1:c, :] for x in gc]
        sums = [jnp.dot(x, ones_dd, preferred_element_type=F32) for x in near]
        qk_far = [[_mm_nt(qe, ke) for qe, ke in ops] for ops in far_ops]
        far = [[_mm(a, vv[:top, :]) for a, (top, _) in zip(qs, blocks[1:])] for qs, vv in zip(qk_far, v)]
        o_st = [_mm_nt(a * jnp.exp(g), s_) for a, g, s_ in zip(q, gc, st)]
        kv = [_mm_tn(vv, kk * jnp.exp(g_l - g)) for vv, kk, g_l, g in zip(v, k, gl, gc)]

        for hh, hs in enumerate(hss):
            groups = [jnp.zeros((8, d), F32) for _ in range(c // 8)]
            at = 0
            for top, end in blocks:
                for j in range(top, end):
                    v_j = v[hh][j:j + 1, :]
                    for g in range(j // 8, end // 8):
                        groups[g] = groups[g] + sums[hh][at:at + 8, :] * v_j
                        at += 8
            for f, (top, end) in zip(far[hh], blocks[1:]):
                for g in range(top // 8, end // 8):
                    groups[g] = groups[g] + f[(g * 8 - top):(g * 8 - top + 8), :]
            o = jnp.concatenate(groups, axis=0) + o_st[hh]
            st_ref[hh] = st[hh] * jnp.exp(gl[hh]) + kv[hh]
            o_ref[rows, hs] = (_rms(o, gnorm) * _silu(gate_ref[rows, hs])).astype(o_ref.dtype)
        return carry

    lax.fori_loop(0, ts // c, chunk_loop, 0)


def _hgrn2(p32, lb, d_norm_g, *, ts, cols):
    bsz, s, _ = p32.shape
    d = HEAD_DIM
    nh = N_HEADS
    w = nh * d
    kernel = functools.partial(_hgrn2_kernel, ts=ts)
    tile = lambda name: pl.BlockSpec((None, ts, w), lambda b, i: (b, i, cols[name] // nh))
    return pl.pallas_call(
        kernel,
        grid=(bsz, s // ts),
        in_specs=[tile("qd"), tile("fd"), tile("id"), tile("gd"),
                  pl.BlockSpec((1, w), lambda b, i: (0, 0)),
                  pl.BlockSpec((1, d), lambda b, i: (0, 0))],
        out_specs=pl.BlockSpec((None, ts, w), lambda b, i: (b, i, 0)),
        out_shape=jax.ShapeDtypeStruct((bsz, s, w), BF16),
        scratch_shapes=[pltpu.VMEM((ts, w), F32), pltpu.VMEM((ts, w), F32),
                        pltpu.VMEM((ts, w), F32), pltpu.VMEM((nh, d, d), F32)],
        compiler_params=pltpu.CompilerParams(
            dimension_semantics=("parallel", "arbitrary"), vmem_limit_bytes=VMEM_LIMIT),
        name="hgrn2",
    )(p32, p32, p32, p32, lb.astype(F32).reshape(1, w), d_norm_g.astype(F32).reshape(1, d))


def _stickbreak_kernel(q_ref, k_ref, v_ref, o_ref, acc_ref, *, tq):
    i = pl.program_id(1)
    d = HEAD_DIM
    nh = N_HEADS
    row = _iota((tq, tq), 0)
    col = _iota((tq, tq), 1)
    causal = col < row
    later = (row > col).astype(BF16)
    later2 = jnp.concatenate([later, later], axis=0)

    heads = [slice(hh * d, (hh + 1) * d) for hh in range(nh)]

    def scores(blocks):
        jobs = [(j, dg, hs) for j, dg in blocks for hs in heads]
        z = [_mm_nt(q_ref[:, hs], k_ref[pl.ds(pl.multiple_of(j * tq, tq), tq), hs]) * (d ** -0.5)
             for j, _, hs in jobs]
        sp = [_softplus(x) for x in z]
        l1m = [jnp.where(causal, -x, 0.0) if dg else -x for x, (_, dg, _) in zip(sp, jobs)]
        rest = [jnp.dot(jnp.concatenate(_split(x), axis=1), later2, preferred_element_type=F32)
                for x in l1m]
        out = [((a - b) + r, l) for a, b, r, l in zip(z, sp, rest, l1m)]
        return [out[b * nh:(b + 1) * nh] for b in range(len(blocks))]

    def block(j, carries):
        (sc,) = scores([(j, False)])
        ps = [jnp.exp(logw + c) for (logw, _), c in zip(sc, carries)]
        pv = [_mm(p, v_ref[pl.ds(pl.multiple_of(j * tq, tq), tq), hs]) for p, hs in zip(ps, heads)]
        for hs, x in zip(heads, pv):
            acc_ref[:, hs] += x
        return tuple(c + jnp.sum(l1m, axis=-1, keepdims=True) for (_, l1m), c in zip(sc, carries))

    j1 = jnp.maximum(i - 1, 0)
    j2 = jnp.maximum(i - 2, 0)
    live1 = jnp.where(i > 0, 1.0, 0.0)
    live2 = jnp.where(i > 1, 1.0, 0.0)
    s0, s1, s2 = scores([(i, True), (j1, False), (j2, False)])
    carries = []
    for hh, hs in enumerate(heads):
        c0 = jnp.sum(s0[hh][1], axis=-1, keepdims=True)
        c1 = c0 + jnp.sum(s1[hh][1], axis=-1, keepdims=True)
        p0 = jnp.where(causal, jnp.exp(s0[hh][0]), 0.0)
        p1 = jnp.exp(s1[hh][0] + c0) * live1
        p2 = jnp.exp(s2[hh][0] + c1) * live2
        acc_ref[:, hs] = (_mm(p0, v_ref[pl.ds(pl.multiple_of(i * tq, tq), tq), hs])
                          + _mm(p1, v_ref[pl.ds(pl.multiple_of(j1 * tq, tq), tq), hs])
                          + _mm(p2, v_ref[pl.ds(pl.multiple_of(j2 * tq, tq), tq), hs]))
        carries.append(c1 + jnp.sum(s2[hh][1], axis=-1, keepdims=True))
    carries = tuple(carries)

    def cond(c):
        worst = functools.reduce(jnp.maximum, c[1])
        return jnp.logical_and(c[0] >= 0, jnp.max(worst) >= EXP_ZERO_BELOW)

    def body(c):
        return c[0] - 1, block(c[0], c[1])

    lax.while_loop(cond, body, (i - 3, carries))
    o_ref[...] = acc_ref[...].astype(o_ref.dtype)


def _stickbreak(p16, *, tq, cols):
    bsz, s, _ = p16.shape
    nh = N_HEADS
    w = nh * HEAD_DIM
    kernel = functools.partial(_stickbreak_kernel, tq=tq)
    resident = dict(pipeline_mode=pl.Buffered(1))
    return pl.pallas_call(
        kernel,
        grid=(bsz, s // tq),
        in_specs=[pl.BlockSpec((None, tq, w), lambda b, i: (b, i, cols["qc"] // nh)),
                  pl.BlockSpec((None, s, w), lambda b, i: (b, 0, cols["kc"] // nh), **resident),
                  pl.BlockSpec((None, s, w), lambda b, i: (b, 0, cols["vc"] // nh), **resident)],
        out_specs=pl.BlockSpec((None, tq, w), lambda b, i: (b, i, 0)),
        out_shape=jax.ShapeDtypeStruct((bsz, s, w), BF16),
        scratch_shapes=[pltpu.VMEM((tq, w), F32)],
        compiler_params=pltpu.CompilerParams(
            dimension_semantics=("parallel", "arbitrary"), vmem_limit_bytes=VMEM_LIMIT),
        name="stickbreak",
    )(p16, p16, p16)


def _dsa_kernel(qi_ref, smq_ref, q_ref, sm_ref, k_ref, vt_ref, bias_ref, o_ref,
                sc_ref, scb_ref, qct_ref, kc_ref, bd_ref, lg_ref, *, tq, k_sel, wi_lane, wide):
    i = pl.program_id(1)
    tk = tq
    d = HEAD_DIM
    nh = N_HEADS
    ksel = float(k_sel)
    per_wide = wide // tk
    n_wide = (i + per_wide) // per_wide
    sub = 2 * tk
    lane_q = _iota((1, tq), 1)

    def tree(parts, op):
        while len(parts) > 1:
            parts = [op(parts[j], parts[j + 1]) if j + 1 < len(parts) else parts[j]
                     for j in range(0, len(parts), 2)]
        return parts[0]

    def col_fold(x, op=jnp.add, rows=8):
        return tree([x[r * rows:(r + 1) * rows] for r in range(x.shape[0] // rows)], op)

    @pl.when(i == 0)
    def _():
        def prep(g, carry):
            g0 = pl.multiple_of(g * wide, wide)
            hi, lo = _split(sm_ref[pl.ds(g0, wide), :][:, :IDX_DIM])
            kc_ref[pl.ds(g0, wide), :] = jnp.concatenate([hi, lo, hi], axis=1)
            return carry
        lax.fori_loop(0, sm_ref.shape[0] // wide, prep, 0)

    qit = qi_ref[...].T
    for p in range(IDX_HEADS // 2):
        halves = []
        for hh in (2 * p, 2 * p + 1):
            hi, lo = _split(qit[hh * IDX_DIM:(hh + 1) * IDX_DIM, :])
            halves.append(jnp.concatenate([hi, hi, lo], axis=0))
        qct_ref[p] = jnp.concatenate(halves, axis=1)
    w_rows = smq_ref[...].T[wi_lane:wi_lane + IDX_HEADS, :] * ((IDX_HEADS ** -0.5) * (IDX_DIM ** -0.5))

    q2t = (q_ref[...] * ((d ** -0.5) * LOG2E)).T.astype(BF16)
    zero_dq = jnp.zeros((d, tq), BF16)
    for p in range(nh // 2):
        top = jnp.concatenate([q2t[2 * p * d:(2 * p + 1) * d], zero_dq], axis=1)
        bot = jnp.concatenate([zero_dq, q2t[(2 * p + 1) * d:(2 * p + 2) * d]], axis=1)
        bd_ref[p] = jnp.concatenate([top, bot], axis=0)

    limit = i * tq + (lane_q // CHUNK + 1) * CHUNK

    rows_s = _iota((sub, tq), 0)

    def score_groups(gs, mm, masked):
        mn, mx = mm
        k0s = [pl.multiple_of(g * wide + sb * sub, sub) for g in gs for sb in range(wide // sub)]
        keys = [kc_ref[pl.ds(k0, sub), :] for k0 in k0s]
        accs = [jnp.zeros((sub, tq), F32) for _ in k0s]
        for p in range(IDX_HEADS // 2):
            rhs = qct_ref[p]
            for n, kk in enumerate(keys):
                s2 = jnp.dot(kk, rhs, preferred_element_type=F32)
                accs[n] = (accs[n] + jnp.maximum(s2[:, :tq], 0.0) * w_rows[2 * p:2 * p + 1, :]
                           + jnp.maximum(s2[:, tq:], 0.0) * w_rows[2 * p + 1:2 * p + 2, :])
        for k0, sct in zip(k0s, accs):
            if masked:
                adm = (k0 + rows_s) < limit
                mn = jnp.minimum(mn, col_fold(jnp.where(adm, sct, jnp.inf), jnp.minimum))
                sct = jnp.where(adm, sct, -jnp.inf)
            else:
                mn = jnp.minimum(mn, col_fold(sct, jnp.minimum))
            mx = jnp.maximum(mx, col_fold(sct, jnp.maximum))
            sc_ref[pl.ds(k0, sub), :] = sct
            scb_ref[pl.ds(k0, sub), :] = _floor_bf16(sct)
        return mn, mx

    def score_pair(j, mm):
        return score_groups((2 * j, 2 * j + 1), mm, False)

    n_full = n_wide - 1
    mm = lax.fori_loop(0, n_full // 2, score_pair,
                       (jnp.full((8, tq), jnp.inf, F32), jnp.full((8, tq), -jnp.inf, F32)))
    mm = lax.cond(n_full % 2 == 1, lambda c: score_groups((n_full - 1,), c, False), lambda c: c, mm)
    mn, mx = score_groups((n_wide - 1,), mm, True)

    n_pairs = (n_wide + 1) // 2

    @pl.when(n_wide % 2 == 1)
    def _():
        sc_ref[pl.ds(pl.multiple_of(n_wide * wide, wide), wide), :] = jnp.full((wide, tq), -jnp.inf, F32)
        scb_ref[pl.ds(pl.multiple_of(n_wide * wide, wide), wide), :] = jnp.full((wide, tq), -jnp.inf, BF16)
    rmin = jnp.min(mn, axis=0, keepdims=True)
    rmax = jnp.max(mx, axis=0, keepdims=True)

    def count(pred):
        def body(j, acc):
            for g in (2 * j, 2 * j + 1):
                acc = acc + col_fold(pred(sc_ref[pl.ds(pl.multiple_of(g * wide, wide), wide), :]))
            return acc
        return jnp.sum(lax.fori_loop(0, n_pairs, body, jnp.zeros((8, tq), F32)), axis=0, keepdims=True)

    def max_below(x):
        def body(j, acc):
            for g in (2 * j, 2 * j + 1):
                blk = sc_ref[pl.ds(pl.multiple_of(g * wide, wide), wide), :]
                acc = jnp.maximum(acc, col_fold(jnp.where(blk < x, blk, -jnp.inf), jnp.maximum))
            return acc
        return jnp.max(lax.fori_loop(0, n_pairs, body, jnp.full((8, tq), -jnp.inf, F32)), axis=0, keepdims=True)

    n_adm = limit.astype(F32)
    all_sel = n_adm <= ksel

    def bisect(c):
        lo, hi, c_lo = c
        mid = 0.5 * lo + 0.5 * hi
        cm = count(lambda blk: _ind(blk >= mid))
        ge = cm >= ksel
        return jnp.where(ge, mid, lo), jnp.where(ge, hi, mid), jnp.where(ge, cm, c_lo)

    def pending(c_lo, tied):
        return jnp.where(all_sel, 0.0, jnp.where(tied > 0.5, 0.0, _ind(c_lo != ksel)))

    def bisect_coarse(_, c):
        lo, hi, c_lo = c
        mid = _floor_bf16(0.5 * lo + 0.5 * hi).astype(F32)
        t_b = jnp.broadcast_to(mid, (16, tq)).astype(BF16)
        one_b = jnp.ones((16, tq), BF16)
        zero_b = jnp.zeros((16, tq), BF16)

        def body(j, acc):
            for g in (2 * j, 2 * j + 1):
                blk = scb_ref[pl.ds(pl.multiple_of(g * wide, wide), wide), :]
                ind = [jnp.where(blk[r * 16:(r + 1) * 16] >= t_b, one_b, zero_b) for r in range(wide // 16)]
                acc = acc + tree(ind, jnp.add).astype(F32)
            return acc

        acc = lax.fori_loop(0, n_pairs, body, jnp.zeros((16, tq), F32))
        cm = jnp.sum(acc, axis=0, keepdims=True)
        ge = cm >= ksel
        return jnp.where(ge, mid, lo), jnp.where(ge, hi, mid), jnp.where(ge, cm, c_lo)

    lo0 = _floor_bf16(rmin).astype(F32)
    hi0 = _floor_bf16(rmax + (jnp.abs(rmax) * (2.0 ** -6) + 1e-30)).astype(F32)
    state = lax.fori_loop(0, BISECT_COARSE, bisect_coarse, (lo0, hi0, n_adm))
    state = lax.fori_loop(0, BISECT_FIXED, lambda _, c: bisect(c), state)

    def round_cond(c):
        return jnp.max(pending(c[0][2], c[1])) > 0.5

    def round_body(c):
        st, tied, v, need = c

        def more_cond(s):
            return jnp.logical_and(s[0] < BISECT_EXTRA, jnp.max(pending(s[1][2], tied)) > 0.5)

        _, st = lax.while_loop(more_cond, lambda s: (s[0] + 1, bisect(s[1])), (jnp.int32(0), st))
        pend = pending(st[2], tied)

        def check(_):
            cand = max_below(st[1])
            c_ge = count(lambda blk: _ind(blk >= cand))
            c_gt = count(lambda blk: _ind(blk > cand))
            ok = jnp.where(pend > 0.5, _ind(c_ge >= ksel), 0.0)
            return (jnp.where(ok > 0.5, 1.0, tied), jnp.where(ok > 0.5, cand, v),
                    jnp.where(ok > 0.5, ksel - c_gt, need))

        tied, v, need = lax.cond(jnp.max(pend) > 0.5, check, lambda _: (tied, v, need), 0)
        return st, tied, v, need

    zeros1 = jnp.zeros((1, tq), F32)
    (lo_f, _, _), tied, v_tie, need = lax.while_loop(round_cond, round_body, (state, zeros1, zeros1, zeros1))
    vth = jnp.where(all_sel, F32_LOWEST, jnp.where(tied > 0.5, v_tie, lo_f))

    @pl.when(jnp.max(tied) > 0.5)
    def _():
        v_eq = jnp.where(tied > 0.5, v_tie, jnp.inf)
        incl = (_iota((tk, tk), 1) <= _iota((tk, tk), 0)).astype(BF16)

        def demote(g, seen):
            g0 = pl.multiple_of(g * wide, wide)
            xs = [sc_ref[pl.ds(g0 + pb * tk, tk), :] for pb in range(per_wide)]
            eqs = [_ind(x == v_eq) for x in xs]
            inblk = [jnp.dot(incl, e.astype(BF16), preferred_element_type=F32) for e in eqs]
            for pb in range(per_wide):
                rank = inblk[pb] + seen
                sc_ref[pl.ds(g0 + pb * tk, tk), :] = jnp.where(eqs[pb] * _ind(rank > need) > 0.5,
                                                               -jnp.inf, xs[pb])
                seen = seen + jnp.sum(col_fold(eqs[pb]), axis=0, keepdims=True)
            return seen

        lax.fori_loop(0, n_wide, demote, zeros1)

    g_near = jnp.maximum(i - 1, 0) // per_wide

    def logit_group(g, mx, near):
        out = list(mx)
        for sb in range(wide // sub):
            k0 = pl.multiple_of(g * wide + sb * sub, sub)
            sel = sc_ref[pl.ds(k0, sub), :] >= vth
            for p in range(nh // 2):
                pair = jnp.dot(k_ref[pl.ds(k0, sub), 2 * p * d:(2 * p + 2) * d], bd_ref[p],
                               preferred_element_type=F32)
                for hh in (2 * p, 2 * p + 1):
                    lm = pair[:, (hh - 2 * p) * tq:(hh - 2 * p + 1) * tq]
                    if near:
                        back = [jnp.clip(i - (g * per_wide + sb * (sub // tk) + pb), 0, 2)
                                for pb in range(sub // tk)]
                        lm = lm + jnp.concatenate([bias_ref[bk, hh] for bk in back], axis=0)
                    lm = jnp.where(sel, lm, NEG_BIG)
                    lg_ref[hh, pl.ds(k0, sub), :] = lm
                    out[hh] = jnp.maximum(out[hh], col_fold(lm, jnp.maximum))
        return tuple(out)

    mx = tuple(jnp.full((8, tq), NEG_BIG, F32) for _ in range(nh))
    def logit_pair(j, mx, near):
        return logit_group(2 * j + 1, logit_group(2 * j, mx, near), near)

    far_pairs = g_near // 2
    full_pairs = n_wide // 2
    odd = n_wide % 2 == 1
    mx = lax.fori_loop(0, far_pairs, functools.partial(logit_pair, near=False), mx)
    mx = lax.fori_loop(far_pairs, full_pairs, functools.partial(logit_pair, near=True), mx)
    mx = lax.cond(odd, lambda m: logit_group(n_wide - 1, m, True), lambda m: m, mx)
    m_q = [jnp.max(mx[hh], axis=0, keepdims=True) for hh in range(nh)]

    ones_rows = jnp.ones((8, wide), BF16)

    def pv_groups(gs, carry):
        ls, accs = list(carry[0]), list(carry[1])
        jobs = [(pl.multiple_of(g * wide, wide), hh) for g in gs for hh in range(nh)]
        ps = [jnp.exp2(lg_ref[hh, pl.ds(g0, wide), :] - m_q[hh]).astype(BF16) for g0, hh in jobs]
        outs = [jnp.dot(jnp.concatenate([vt_ref[hh * d:(hh + 1) * d, pl.ds(g0, wide)], ones_rows], axis=0),
                        p, preferred_element_type=F32) for (g0, hh), p in zip(jobs, ps)]
        for (_, hh), out in zip(jobs, outs):
            ls[hh] = ls[hh] + out[d:]
            accs[hh] = accs[hh] + out[:d]
        return tuple(ls), tuple(accs)

    acc = lax.fori_loop(0, full_pairs, lambda j, cr: pv_groups((2 * j, 2 * j + 1), cr),
                        (tuple(jnp.zeros((8, tq), F32) for _ in range(nh)),
                         tuple(jnp.zeros((d, tq), F32) for _ in range(nh))))
    ls, accs = lax.cond(odd, lambda cr: pv_groups((n_wide - 1,), cr), lambda cr: cr, acc)
    for hh in range(nh):
        o_ref[:, hh * d:(hh + 1) * d] = (accs[hh] / ls[hh][0:1]).T.astype(o_ref.dtype)


def _dsa(p32, p16, vt, bias_tiles, *, tq, cols):
    bsz, s, _ = p32.shape
    d = HEAD_DIM
    nh = N_HEADS
    wide = 4 * tq
    k_sel = min(TOPK_MAX, s // 4)
    w512 = nh * d
    kernel = functools.partial(_dsa_kernel, tq=tq, k_sel=k_sel, wi_lane=cols["wi_lane"], wide=wide)
    resident = dict(pipeline_mode=pl.Buffered(1))
    return pl.pallas_call(
        kernel,
        grid=(bsz, s // tq),
        in_specs=[pl.BlockSpec((None, tq, w512), lambda b, i: (b, i, cols["qi"] // nh)),
                  pl.BlockSpec((None, tq, d), lambda b, i: (b, i, cols["small"])),
                  pl.BlockSpec((None, tq, w512), lambda b, i: (b, i, cols["qb"] // nh)),
                  pl.BlockSpec((None, s, d), lambda b, i: (b, 0, cols["small"]), **resident),
                  pl.BlockSpec((None, s, w512), lambda b, i: (b, 0, cols["kb"] // nh), **resident),
                  pl.BlockSpec((w512, s), lambda b, i: (0, b), **resident),
                  pl.BlockSpec((3, nh, tq, tq), lambda b, i: (0, 0, 0, 0), **resident)],
        out_specs=pl.BlockSpec((None, tq, w512), lambda b, i: (b, i, 0)),
        out_shape=jax.ShapeDtypeStruct((bsz, s, w512), BF16),
        scratch_shapes=[pltpu.VMEM((s, tq), F32),
                        pltpu.VMEM((s, tq), BF16),
                        pltpu.VMEM((IDX_HEADS // 2, 3 * IDX_DIM, 2 * tq), BF16),
                        pltpu.VMEM((s, 3 * IDX_DIM), BF16),
                        pltpu.VMEM((nh // 2, 2 * d, 2 * tq), BF16),
                        pltpu.VMEM((nh, s, tq), F32)],
        compiler_params=pltpu.CompilerParams(
            dimension_semantics=("parallel", "arbitrary"), vmem_limit_bytes=VMEM_LIMIT),
        name="dsa",
    )(p32, p32, p32, p32, p16, vt, bias_tiles)


def _t5_bucket(rel):
    nb = REL_BUCKETS // 2
    max_exact = nb // 2
    ret = jnp.where(rel > 0, nb, 0)
    n = jnp.abs(rel)
    large = max_exact + (jnp.log(jnp.maximum(n, 1).astype(F32) / max_exact)
                         / math.log(REL_MAX_DIST / max_exact) * (nb - max_exact)).astype(jnp.int32)
    large = jnp.minimum(large, nb - 1)
    return ret + jnp.where(n < max_exact, n, large)


def _bias_tiles(rel_table, tq):
    assert tq >= REL_MAX_DIST
    t = jnp.arange(tq)
    back = jnp.arange(3)
    rel = (t[None, None, :] - back[:, None, None] * tq) - t[None, :, None]
    onehot = (_t5_bucket(rel)[..., None] == jnp.arange(REL_BUCKETS)).astype(F32)
    tiles = jnp.einsum("bqkn,nh->bhkq", onehot, rel_table.astype(F32),
                       precision=HIGHEST)
    return (tiles - tiles[2:3]) * LOG2E


def _even_layout(w_in):
    d = HEAD_DIM
    a_w = 2 * N_HEADS * d + N_HEADS * d
    offs = {}
    o = 0
    for name, w in (("qkv", a_w), ("z", N_HEADS * d), ("a", N_HEADS), ("b", N_HEADS),
                    ("qb", N_HEADS * d), ("kb", N_HEADS * d), ("vb", N_HEADS * d),
                    ("qi", IDX_HEADS * IDX_DIM), ("ki", IDX_DIM), ("wi", IDX_HEADS)):
        offs[name] = (o, o + w)
        o += w
    assert o == w_in.shape[1]
    sl = lambda n: w_in[:, offs[n][0]:offs[n][1]]
    small_w = IDX_DIM + 2 * N_HEADS + IDX_HEADS
    small_pad = -small_w % d
    zeros = lambda n: jnp.zeros((w_in.shape[0], n), w_in.dtype)
    w32 = jnp.concatenate([sl("qkv"), sl("z"), sl("qb"), sl("qi"),
                           sl("ki"), sl("a"), sl("b"), sl("wi"), zeros(small_pad)], axis=1)
    n32 = w32.shape[1]
    tn = n32 // 5
    assert tn * 5 == n32 and tn % d == 0
    w16 = jnp.concatenate([sl("kb"), zeros(tn - N_HEADS * d)], axis=1)
    nh = N_HEADS
    cols = dict(qa=0, ka=nh, va=2 * nh, za=3 * nh, qb=4 * nh, qi=5 * nh, small=6 * nh, kb=0,
                a_lane=IDX_DIM, b_lane=IDX_DIM + nh, wi_lane=IDX_DIM + 2 * nh, n32=n32, tn=tn)
    return jnp.concatenate([w32, w16], axis=1).astype(BF16), sl("vb").T.astype(BF16), cols


def kernel(x, norm_g, w_in_even, conv_w_even, a_log_even, dt_bias_even, a_norm_even, w_out_even,
           rel_bias, w_in_odd, lb_logits, d_norm_odd, w_out_odd, w_gate, w_up, w_down):
    bsz, s, d = x.shape
    t = bsz * s
    depth = norm_g.shape[0]
    nh = N_HEADS
    tq = Q_TILE
    lb_all = jnp.cumsum(jax.nn.softmax(lb_logits.astype(F32), axis=0), axis=0)
    lb_all = lb_all - lb_all[:1]
    odd_cols = dict(qc=0, kc=nh, vc=2 * nh, qd=0, fd=nh, id=2 * nh, gd=3 * nh)
    bias_tiles = _bias_tiles(rel_bias, tq)

    h = x.reshape(t, d)
    for l in range(depth):
        if l % 2 == 0:
            e = l // 2
            w_even, w_vt, cols = _even_layout(w_in_even[e])
            p32, p16, vt = _norm_matmul(h, norm_g[l, 0], w_even, tm=PROJ_TILE, tn=cols["tn"], n32=cols["n32"],
                                        w_t=w_vt)
            p32 = p32.reshape(bsz, s, -1)
            p16 = p16.reshape(bsz, s, -1)
            o_1 = _deltanet(p32, conv_w_even[e], a_log_even[e], dt_bias_even[e], a_norm_even[e],
                            ts=min(DELTANET_TILE, s), cols=cols)
            o_2 = _dsa(p32, p16, vt, bias_tiles, tq=tq, cols=cols)
            w_out = w_out_even[e]
        else:
            o = l // 2
            n16 = 3 * nh * HEAD_DIM
            w_odd = jnp.concatenate([w_in_odd[o][:, n16:], w_in_odd[o][:, :n16]], axis=1).astype(BF16)
            p32, p16 = _norm_matmul(h, norm_g[l, 0], w_odd, tm=PROJ_TILE, tn=ODD_COL_TILE, n32=w_odd.shape[1] - n16)
            p32 = p32.reshape(bsz, s, -1)
            p16 = p16.reshape(bsz, s, -1)
            o_1 = _stickbreak(p16, tq=tq, cols=odd_cols)
            o_2 = _hgrn2(p32, lb_all[l], d_norm_odd[o], ts=min(SEQ_TILE, s), cols=odd_cols)
            w_out = w_out_odd[o]
        h = _mix_ffn(o_1.reshape(t, -1), o_2.reshape(t, -1), w_out, h, norm_g[l, 1], norm_g[l, 2], norm_g[l, 3],
                     w_gate[l], w_up[l], w_down[l], tm=ROW_TILE, tf=FFN_TILE)
    return h.reshape(bsz, s, d)
```

```python
import functools
import math

import jax
import jax.numpy as jnp
from jax import lax
from jax.experimental import pallas as pl
from jax.experimental.pallas import tpu as pltpu

F32 = jnp.float32
BF16 = jnp.bfloat16
HIGHEST = lax.Precision.HIGHEST

CHUNK = 64
HEAD_DIM = 128
N_HEADS = 4
IDX_HEADS = 8
IDX_DIM = 64
TOPK_MAX = 256
CONV_WIDTH = 4
REL_BUCKETS = 32
REL_MAX_DIST = 128
EPS = 1e-6
NEG_BIG = -1e30
LOG2E = 1.4426950408889634
BISECT_COARSE = 10
BISECT_FIXED = 8
BISECT_EXTRA = 6
F32_LOWEST = -3.4028234663852886e38
EXP_ZERO_BELOW = -104.0
VMEM_LIMIT = 56 * 1024 * 1024

PROJ_TILE = 2048
ROW_TILE = 512
DELTANET_TILE = 1024
SEQ_TILE = 512
Q_TILE = 128
ODD_COL_TILE = 512
FFN_TILE = 2816


def _mm(a, b):
    return jnp.dot(a.astype(BF16), b.astype(BF16), preferred_element_type=F32)


def _mm_nt(a, b):
    return lax.dot_general(a.astype(BF16), b.astype(BF16), (((1,), (1,)), ((), ())),
                           preferred_element_type=F32)


def _mm_tn(a, b):
    return lax.dot_general(a.astype(BF16), b.astype(BF16), (((0,), (0,)), ((), ())),
                           preferred_element_type=F32)


def _split(x):
    hi = x.astype(BF16)
    return hi, (x - hi.astype(F32)).astype(BF16)


def _floor_bf16(x):
    bits = pltpu.bitcast(x, jnp.int32)
    down = jnp.where(bits >= 0, bits, bits + 0xFFFF) & jnp.int32(-65536)
    return pltpu.bitcast(down, F32).astype(BF16)


def _sigmoid(x):
    return 1.0 / (1.0 + jnp.exp(-x))


def _silu(x):
    return x * _sigmoid(x)


def _softplus(x):
    return jnp.maximum(x, 0.0) + jnp.log1p(jnp.exp(-jnp.abs(x)))


def _rms(x, g):
    return x * lax.rsqrt(jnp.mean(x * x, axis=-1, keepdims=True) + EPS) * g


def _iota(shape, dim):
    return lax.broadcasted_iota(jnp.int32, shape, dim)


def _ind(mask):
    return jnp.where(mask, 1.0, 0.0)


def _norm_matmul_kernel(x_ref, g_ref, w_ref, *rest, n_t, tiles32):
    if n_t:
        wt_ref, o32_ref, o16_ref, ot_ref, xn_ref = rest
    else:
        o32_ref, o16_ref, xn_ref = rest
    j = pl.program_id(1)

    @pl.when(j == 0)
    def _():
        xn_ref[...] = _rms(x_ref[...], g_ref[...]).astype(BF16)
        if n_t:
            ot_ref[...] = lax.dot_general(wt_ref[...], xn_ref[...], (((1,), (1,)), ((), ())),
                                          preferred_element_type=F32).astype(BF16)

    y = jnp.dot(xn_ref[...], w_ref[...], preferred_element_type=F32)

    @pl.when(j < tiles32)
    def _():
        o32_ref[...] = y

    @pl.when(j >= tiles32)
    def _():
        o16_ref[...] = y.astype(BF16)


def _norm_matmul(x, g, w, *, tm, tn, n32, w_t=None):
    t, d = x.shape
    n = w.shape[1]
    n_t = 0 if w_t is None else w_t.shape[0]
    tiles32 = n32 // tn
    assert tiles32 * tn == n32 and (n - n32) % tn == 0 and 0 < n32 < n
    in_specs = [pl.BlockSpec((tm, d), lambda i, j: (i, 0)),
                pl.BlockSpec((1, d), lambda i, j: (0, 0)),
                pl.BlockSpec((d, tn), lambda i, j: (0, j))]
    out_specs = [pl.BlockSpec((tm, tn), lambda i, j: (i, jnp.minimum(j, tiles32 - 1))),
                 pl.BlockSpec((tm, tn), lambda i, j: (i, jnp.maximum(j - tiles32, 0)))]
    out_shape = [jax.ShapeDtypeStruct((t, n32), F32), jax.ShapeDtypeStruct((t, n - n32), BF16)]
    args = [x, g.reshape(1, d), w]
    if n_t:
        in_specs.append(pl.BlockSpec((n_t, d), lambda i, j: (0, 0)))
        out_specs.append(pl.BlockSpec((n_t, tm), lambda i, j: (0, i)))
        out_shape.append(jax.ShapeDtypeStruct((n_t, t), BF16))
        args.append(w_t)
    return pl.pallas_call(
        functools.partial(_norm_matmul_kernel, n_t=n_t, tiles32=tiles32),
        grid=(t // tm, n // tn),
        in_specs=in_specs,
        out_specs=out_specs,
        out_shape=out_shape,
        scratch_shapes=[pltpu.VMEM((tm, d), BF16)],
        compiler_params=pltpu.CompilerParams(
            dimension_semantics=("parallel", "arbitrary"), vmem_limit_bytes=VMEM_LIMIT),
        name="norm_matmul",
    )(*args)


def _mix_ffn_kernel(ca_ref, cb_ref, wa_ref, wb_ref, h_ref, gmix_ref, gpre_ref, gpost_ref,
                    wg_ref, wu_ref, wd_ref, o_ref, h1_ref, xn_ref, acc_ref):
    f = pl.program_id(1)

    @pl.when(f == 0)
    def _():
        y = (jnp.dot(ca_ref[...], wa_ref[...], preferred_element_type=F32)
             + jnp.dot(cb_ref[...], wb_ref[...], preferred_element_type=F32))
        h1 = h_ref[...] + _rms(y, gmix_ref[...])
        h1_ref[...] = h1
        xn_ref[...] = _rms(h1, gpre_ref[...]).astype(BF16)
        acc_ref[...] = jnp.zeros_like(acc_ref)

    xn = xn_ref[...]
    gate = jnp.dot(xn, wg_ref[...], preferred_element_type=F32)
    up = jnp.dot(xn, wu_ref[...], preferred_element_type=F32)
    act = (_silu(gate) * up).astype(BF16)
    acc_ref[...] += jnp.dot(act, wd_ref[...], preferred_element_type=F32)

    @pl.when(f == pl.num_programs(1) - 1)
    def _():
        o_ref[...] = h1_ref[...] + _rms(acc_ref[...], gpost_ref[...])


def _mix_ffn(ca, cb, w_out, h, g_mix, g_pre, g_post, wg, wu, wd, *, tm, tf):
    t, d = h.shape
    ff = wg.shape[1]
    wa_n = ca.shape[1]
    wb_n = cb.shape[1]
    row = pl.BlockSpec((1, d), lambda i, f: (0, 0))
    once = dict(pipeline_mode=pl.Buffered(1)) if tf == ff else {}
    return pl.pallas_call(
        _mix_ffn_kernel,
        grid=(t // tm, ff // tf),
        in_specs=[pl.BlockSpec((tm, wa_n), lambda i, f: (i, 0)),
                  pl.BlockSpec((tm, wb_n), lambda i, f: (i, 0)),
                  pl.BlockSpec((wa_n, d), lambda i, f: (0, 0)),
                  pl.BlockSpec((wb_n, d), lambda i, f: (0, 0)),
                  pl.BlockSpec((tm, d), lambda i, f: (i, 0)),
                  row, row, row,
                  pl.BlockSpec((d, tf), lambda i, f: (0, f), **once),
                  pl.BlockSpec((d, tf), lambda i, f: (0, f), **once),
                  pl.BlockSpec((tf, d), lambda i, f: (f, 0), **once)],
        out_specs=pl.BlockSpec((tm, d), lambda i, f: (i, 0)),
        out_shape=jax.ShapeDtypeStruct((t, d), F32),
        scratch_shapes=[pltpu.VMEM((tm, d), F32), pltpu.VMEM((tm, d), BF16), pltpu.VMEM((tm, d), F32)],
        compiler_params=pltpu.CompilerParams(
            dimension_semantics=("parallel", "arbitrary"), vmem_limit_bytes=VMEM_LIMIT),
        name="mix_ffn",
    )(ca, cb, w_out[:wa_n].astype(BF16), w_out[wa_n:].astype(BF16), h,
      g_mix.reshape(1, d), g_pre.reshape(1, d), g_post.reshape(1, d),
      wg.astype(BF16), wu.astype(BF16), wd.astype(BF16))


def _deltanet_kernel(xq_ref, xk_ref, xv_ref, z_ref, sm_ref, cwq_ref, cwk_ref, cwv_ref,
                     alog_ref, dtb_ref, gn_ref, o_ref,
                     xpad_ref, q_ref, k_ref, v_ref, gb_ref, bb_ref, u_ref, w_ref, qk_ref, st_ref,
                     *, ts, a_col, b_col):
    s = pl.program_id(1)
    c = CHUNK
    d = HEAD_DIM
    nh = N_HEADS

    @pl.when(s == 0)
    def _():
        xpad_ref[:, 0:8, :] = jnp.zeros((3, 8, nh * d), F32)
        st_ref[...] = jnp.zeros_like(st_ref)

    @pl.when(s != 0)
    def _():
        xpad_ref[:, 0:8, :] = xpad_ref[:, ts:ts + 8, :]

    xpad_ref[0, 8:ts + 8, :] = xq_ref[...]
    xpad_ref[1, 8:ts + 8, :] = xk_ref[...]
    xpad_ref[2, 8:ts + 8, :] = xv_ref[...]

    def conv_silu(idx, cw_ref, hs):
        cw = cw_ref[:, hs]
        acc = xpad_ref[idx, 8 - (CONV_WIDTH - 1):8 - (CONV_WIDTH - 1) + ts, hs] * cw[0:1, :]
        for j in range(1, CONV_WIDTH):
            off = 8 - (CONV_WIDTH - 1) + j
            acc = acc + xpad_ref[idx, off:off + ts, hs] * cw[j:j + 1, :]
        return _silu(acc)

    def l2norm(t):
        return t * lax.rsqrt(jnp.sum(t * t, axis=-1, keepdims=True) + EPS)

    row = _iota((c, c), 0)
    col = _iota((c, c), 1)
    tri = (col <= row)
    strict = (col < row)
    tri_f = tri.astype(F32)
    upper_f = (row <= col).astype(F32)
    eye = (row == col).astype(F32)
    gnorm = gn_ref[...]
    chunks = range(ts // c)
    rs = [slice(ci * c, (ci + 1) * c) for ci in chunks]
    tri2 = jnp.concatenate([tri_f, tri_f], axis=1).astype(BF16)
    ones2 = jnp.ones((c, 2 * c), BF16)

    def cum2(lhs2, x):
        hi, lo = _split(x)
        return jnp.dot(lhs2, jnp.concatenate([hi, lo], axis=0), preferred_element_type=F32)

    for hh in range(nh):
        hs = slice(hh * d, (hh + 1) * d)
        q_ref[:, hs] = l2norm(conv_silu(0, cwq_ref, hs)) * (d ** -0.5)
        k_ref[:, hs] = l2norm(conv_silu(1, cwk_ref, hs))
        v_ref[:, hs] = conv_silu(2, cwv_ref, hs)

        a_raw = sm_ref[:, a_col + hh:a_col + hh + 1]
        b_raw = sm_ref[:, b_col + hh:b_col + hh + 1]
        g = -jnp.exp(alog_ref[:, hh:hh + 1]) * _softplus(a_raw + dtb_ref[:, hh:hh + 1])
        gb_ref[:, hs] = jnp.broadcast_to(g, (ts, d))
        bb_ref[:, hs] = jnp.broadcast_to(_sigmoid(b_raw), (ts, d))

        q = [q_ref[r, hs] for r in rs]
        k = [k_ref[r, hs] for r in rs]
        beta = [bb_ref[r, hs] for r in rs]
        gb = [gb_ref[r, hs] for r in rs]
        gc = [cum2(tri2, x) for x in gb]
        gc_row = [cum2(ones2, x[:, :c] * upper_f) for x in gb]
        decay = [jnp.where(tri, jnp.exp(jnp.minimum(a[:, :c] - b, 0.0)), 0.0) for a, b in zip(gc, gc_row)]
        kk = [_mm_nt(x, x) for x in k]
        n = [-jnp.where(strict, b[:, :c] * x * dc, 0.0) for b, x, dc in zip(beta, kk, decay)]
        inv = [eye + x for x in n]
        for step in range(5):
            nb = [x.astype(BF16) for x in n]
            n = [jnp.dot(x, x, preferred_element_type=F32) for x in nb]
            inv = [iv + _mm(iv, x) for iv, x in zip(inv, n)]
        egc = [jnp.exp(x) for x in gc]
        gl = [x[c - 1:c, :] for x in gc]
        inv_l = [x.astype(BF16) for x in inv]
        u = [_mm(a, v_ref[r, hs] * b) for a, r, b in zip(inv_l, rs, beta)]
        w = [_mm(a, x * (b * e)) for a, x, b, e in zip(inv_l, k, beta, egc)]
        qk = [_mm_nt(a, b) * dc for a, b, dc in zip(q, k, decay)]
        for ci in chunks:
            r = rs[ci]
            u_ref[r, hs] = u[ci]
            w_ref[r, hs] = w[ci]
            qk_ref[hh, r, :] = qk[ci]
            q_ref[r, hs] = q[ci] * egc[ci]
            k_ref[r, hs] = k[ci] * jnp.exp(gl[ci] - gc[ci])
            gb_ref[r, hs] = jnp.broadcast_to(jnp.exp(gl[ci]), (c, d))

    hss = [slice(hh * d, (hh + 1) * d) for hh in range(nh)]

    def chunk_step(ci, st):
        r0 = pl.multiple_of(ci * c, c)
        rows = pl.ds(r0, c)
        w_st = [_mm(w_ref[rows, hs], s_) for hs, s_ in zip(hss, st)]
        q_st = [_mm(q_ref[rows, hs], s_) for hs, s_ in zip(hss, st)]
        v_new = [u_ref[rows, hs] - x for hs, x in zip(hss, w_st)]
        o = [a + _mm(qk_ref[hh, rows, :], v) for hh, (a, v) in enumerate(zip(q_st, v_new))]
        kv = [_mm_tn(k_ref[rows, hs], v) for hs, v in zip(hss, v_new)]
        for hh, hs in enumerate(hss):
            o_ref[rows, hs] = (_rms(o[hh], gnorm) * _silu(z_ref[rows, hs])).astype(o_ref.dtype)
        return [s_ * gb_ref[pl.ds(r0, 1), hs] + x for s_, hs, x in zip(st, hss, kv)]

    per_step = math.gcd(ts // c, 4)

    def chunk_group(j, carry):
        st = [st_ref[hh] for hh in range(nh)]
        for n_ in range(per_step):
            st = chunk_step(per_step * j + n_, st)
        for hh in range(nh):
            st_ref[hh] = st[hh]
        return carry

    lax.fori_loop(0, ts // (per_step * c), chunk_group, 0)


def _deltanet(p32, conv_w, a_log, dt_bias, a_norm_g, *, ts, cols):
    bsz, s, _ = p32.shape
    d = HEAD_DIM
    nh = N_HEADS
    w = nh * d
    pad = lambda t: jnp.pad(t.astype(F32), (0, d - t.shape[0])).reshape(1, d)
    kernel = functools.partial(_deltanet_kernel, ts=ts, a_col=cols["a_lane"], b_col=cols["b_lane"])
    tile = lambda name: pl.BlockSpec((None, ts, w), lambda b, i: (b, i, cols[name] // nh))
    conv = lambda k: pl.BlockSpec((CONV_WIDTH, w), lambda b, i: (0, k))
    row = pl.BlockSpec((1, d), lambda b, i: (0, 0))
    return pl.pallas_call(
        kernel,
        grid=(bsz, s // ts),
        in_specs=[tile("qa"), tile("ka"), tile("va"), tile("za"),
                  pl.BlockSpec((None, ts, d), lambda b, i: (b, i, cols["small"])),
                  conv(0), conv(1), conv(2), row, row, row],
        out_specs=pl.BlockSpec((None, ts, w), lambda b, i: (b, i, 0)),
        out_shape=jax.ShapeDtypeStruct((bsz, s, w), BF16),
        scratch_shapes=[pltpu.VMEM((3, ts + 8, w), F32)]
        + [pltpu.VMEM((ts, w), F32) for _ in range(7)]
        + [pltpu.VMEM((nh, ts, CHUNK), F32), pltpu.VMEM((nh, d, d), F32)],
        compiler_params=pltpu.CompilerParams(
            dimension_semantics=("parallel", "arbitrary"), vmem_limit_bytes=VMEM_LIMIT),
        name="deltanet",
    )(p32, p32, p32, p32, p32, conv_w.astype(F32), conv_w.astype(F32), conv_w.astype(F32),
      pad(a_log), pad(dt_bias), a_norm_g.astype(F32).reshape(1, d))


def _hgrn2_kernel(q_ref, f_ref, i_ref, gate_ref, lb_ref, gn_ref, o_ref,
                  qs_ref, ks_ref, gc_ref, st_ref, *, ts):
    s = pl.program_id(1)
    c = CHUNK
    d = HEAD_DIM
    nh = N_HEADS
    SUB = 16

    @pl.when(s == 0)
    def _():
        st_ref[...] = jnp.zeros_like(st_ref)

    lb = lb_ref[...]
    f_raw = f_ref[...]
    log_sig = jnp.minimum(f_raw, 0.0) - jnp.log1p(jnp.exp(-jnp.abs(f_raw)))
    la = jnp.log(lb)
    lbb = jnp.log1p(-lb) + log_sig
    log_f = jnp.maximum(la, lbb) + jnp.log1p(jnp.exp(-jnp.abs(la - lbb)))
    qs_ref[...] = _silu(q_ref[...])
    ks_ref[...] = (1.0 - lb) * _sigmoid(-f_raw)

    row = _iota((c, c), 0)
    col = _iota((c, c), 1)
    tri_f = (col <= row).astype(F32)
    ones_dd = jnp.ones((d, d), BF16)
    rows_8d = _iota((8, d), 0)
    gnorm = gn_ref[...]

    tri2 = jnp.concatenate([tri_f, tri_f], axis=1).astype(BF16)
    for ci in range(ts // c):
        hi, lo = _split(log_f[ci * c:(ci + 1) * c, :])
        gc_ref[ci * c:(ci + 1) * c, :] = jnp.dot(tri2, jnp.concatenate([hi, lo], axis=0),
                                                 preferred_element_type=F32)

    blocks = [(sb * SUB, (sb + 1) * SUB) for sb in range(c // SUB)]

    def chunk_loop(ci, carry):
        r0 = pl.multiple_of(ci * c, c)
        rows = pl.ds(r0, c)
        hss = [slice(hh * d, (hh + 1) * d) for hh in range(nh)]
        q = [qs_ref[rows, hs] for hs in hss]
        k = [ks_ref[rows, hs] for hs in hss]
        v = [i_ref[rows, hs] for hs in hss]
        gc = [gc_ref[rows, hs] for hs in hss]

        def near_products(q, k, gc):
            prods = []
            for top, end in blocks:
                for j in range(top, end):
                    lo = (j // 8) * 8
                    e = jnp.exp2(gc[lo:end, :] - gc[j:j + 1, :])
                    if j % 8:
                        head = jnp.where(rows_8d >= j - lo, e[:8], 0.0)
                        e = jnp.concatenate([head, e[8:]], axis=0) if lo + 8 < end else head
                    prods.append(q[lo:end, :] * k[j:j + 1, :] * e)
            return jnp.concatenate(prods, axis=0).astype(BF16)

        def far_operands(q, k, gc):
            out = []
            for top, end in blocks[1:]:
                g_b = gc[top - 1:top, :]
                out.append((q[top:end, :] * jnp.exp(gc[top:end, :] - g_b),
                            k[:top, :] * jnp.exp(jnp.minimum(g_b - gc[:top, :], 0.0))))
            return out

        near = [near_products(a, b, g * LOG2E) for a, b, g in zip(q, k, gc)]
        far_ops = [far_operands(*x) for x in zip(q, k, gc)]
        st = [st_ref[hh] for hh in range(nh)]
        gl = [x[c - 1:c, :] for x in gc]
        sums = [jnp.dot(x, ones_dd, preferred_element_type=F32) for x in near]
        qk_far = [[_mm_nt(qe, ke) for qe, ke in ops] for ops in far_ops]
        far = [[_mm(a, vv[:top, :]) for a, (top, _) in zip(qs, blocks[1:])] for qs, vv in zip(qk_far, v)]
        o_st = [_mm_nt(a * jnp.exp(g), s_) for a, g, s_ in zip(q, gc, st)]
        kv = [_mm_tn(vv, kk * jnp.exp(g_l - g)) for vv, kk, g_l, g in zip(v, k, gl, gc)]

        for hh, hs in enumerate(hss):
            groups = [jnp.zeros((8, d), F32) for _ in range(c // 8)]
            at = 0
            for top, end in blocks:
                for j in range(top, end):
                    v_j = v[hh][j:j + 1, :]
                    for g in range(j // 8, end // 8):
                        groups[g] = groups[g] + sums[hh][at:at + 8, :] * v_j
                        at += 8
            for f, (top, end) in zip(far[hh], blocks[1:]):
                for g in range(top // 8, end // 8):
                    groups[g] = groups[g] + f[(g * 8 - top):(g * 8 - top + 8), :]
            o = jnp.concatenate(groups, axis=0) + o_st[hh]
            st_ref[hh] = st[hh] * jnp.exp(gl[hh]) + kv[hh]
            o_ref[rows, hs] = (_rms(o, gnorm) * _silu(gate_ref[rows, hs])).astype(o_ref.dtype)
        return carry

    per_step = math.gcd(ts // c, 2)

    def chunk_group(j, carry):
        for n_ in range(per_step):
            chunk_loop(per_step * j + n_, carry)
        return carry

    lax.fori_loop(0, ts // (per_step * c), chunk_group, 0)


def _hgrn2(p32, lb, d_norm_g, *, ts, cols):
    bsz, s, _ = p32.shape
    d = HEAD_DIM
    nh = N_HEADS
    w = nh * d
    kernel = functools.partial(_hgrn2_kernel, ts=ts)
    tile = lambda name: pl.BlockSpec((None, ts, w), lambda b, i: (b, i, cols[name] // nh))
    return pl.pallas_call(
        kernel,
        grid=(bsz, s // ts),
        in_specs=[tile("qd"), tile("fd"), tile("id"), tile("gd"),
                  pl.BlockSpec((1, w), lambda b, i: (0, 0)),
                  pl.BlockSpec((1, d), lambda b, i: (0, 0))],
        out_specs=pl.BlockSpec((None, ts, w), lambda b, i: (b, i, 0)),
        out_shape=jax.ShapeDtypeStruct((bsz, s, w), BF16),
        scratch_shapes=[pltpu.VMEM((ts, w), F32), pltpu.VMEM((ts, w), F32),
                        pltpu.VMEM((ts, w), F32), pltpu.VMEM((nh, d, d), F32)],
        compiler_params=pltpu.CompilerParams(
            dimension_semantics=("parallel", "arbitrary"), vmem_limit_bytes=VMEM_LIMIT),
        name="hgrn2",
    )(p32, p32, p32, p32, lb.astype(F32).reshape(1, w), d_norm_g.astype(F32).reshape(1, d))


def _stickbreak_kernel(q_ref, k_ref, v_ref, o_ref, acc_ref, *, tq):
    i = pl.program_id(1)
    d = HEAD_DIM
    nh = N_HEADS
    row = _iota((tq, tq), 0)
    col = _iota((tq, tq), 1)
    causal = col < row
    later = (row > col).astype(BF16)
    later2 = jnp.concatenate([later, later], axis=0)

    heads = [slice(hh * d, (hh + 1) * d) for hh in range(nh)]

    def scores(blocks):
        jobs = [(j, dg, hs) for j, dg in blocks for hs in heads]
        z = [_mm_nt(q_ref[:, hs], k_ref[pl.ds(pl.multiple_of(j * tq, tq), tq), hs]) * (d ** -0.5)
             for j, _, hs in jobs]
        sp = [_softplus(x) for x in z]
        l1m = [jnp.where(causal, -x, 0.0) if dg else -x for x, (_, dg, _) in zip(sp, jobs)]
        rest = [jnp.dot(jnp.concatenate(_split(x), axis=1), later2, preferred_element_type=F32)
                for x in l1m]
        out = [((a - b) + r, l) for a, b, r, l in zip(z, sp, rest, l1m)]
        return [out[b * nh:(b + 1) * nh] for b in range(len(blocks))]

    def block(j, carries):
        (sc,) = scores([(j, False)])
        ps = [jnp.exp(logw + c) for (logw, _), c in zip(sc, carries)]
        pv = [_mm(p, v_ref[pl.ds(pl.multiple_of(j * tq, tq), tq), hs]) for p, hs in zip(ps, heads)]
        for hs, x in zip(heads, pv):
            acc_ref[:, hs] += x
        return tuple(c + jnp.sum(l1m, axis=-1, keepdims=True) for (_, l1m), c in zip(sc, carries))

    j1 = jnp.maximum(i - 1, 0)
    j2 = jnp.maximum(i - 2, 0)
    live1 = jnp.where(i > 0, 1.0, 0.0)
    live2 = jnp.where(i > 1, 1.0, 0.0)
    s0, s1, s2 = scores([(i, True), (j1, False), (j2, False)])
    carries = []
    for hh, hs in enumerate(heads):
        c0 = jnp.sum(s0[hh][1], axis=-1, keepdims=True)
        c1 = c0 + jnp.sum(s1[hh][1], axis=-1, keepdims=True)
        p0 = jnp.where(causal, jnp.exp(s0[hh][0]), 0.0)
        p1 = jnp.exp(s1[hh][0] + c0) * live1
        p2 = jnp.exp(s2[hh][0] + c1) * live2
        acc_ref[:, hs] = (_mm(p0, v_ref[pl.ds(pl.multiple_of(i * tq, tq), tq), hs])
                          + _mm(p1, v_ref[pl.ds(pl.multiple_of(j1 * tq, tq), tq), hs])
                          + _mm(p2, v_ref[pl.ds(pl.multiple_of(j2 * tq, tq), tq), hs]))
        carries.append(c1 + jnp.sum(s2[hh][1], axis=-1, keepdims=True))
    carries = tuple(carries)

    def cond(c):
        worst = functools.reduce(jnp.maximum, c[1])
        return jnp.logical_and(c[0] >= 0, jnp.max(worst) >= EXP_ZERO_BELOW)

    def body(c):
        return c[0] - 1, block(c[0], c[1])

    lax.while_loop(cond, body, (i - 3, carries))
    o_ref[...] = acc_ref[...].astype(o_ref.dtype)


def _stickbreak(p16, *, tq, cols):
    bsz, s, _ = p16.shape
    nh = N_HEADS
    w = nh * HEAD_DIM
    kernel = functools.partial(_stickbreak_kernel, tq=tq)
    resident = dict(pipeline_mode=pl.Buffered(1))
    return pl.pallas_call(
        kernel,
        grid=(bsz, s // tq),
        in_specs=[pl.BlockSpec((None, tq, w), lambda b, i: (b, i, cols["qc"] // nh)),
                  pl.BlockSpec((None, s, w), lambda b, i: (b, 0, cols["kc"] // nh), **resident),
                  pl.BlockSpec((None, s, w), lambda b, i: (b, 0, cols["vc"] // nh), **resident)],
        out_specs=pl.BlockSpec((None, tq, w), lambda b, i: (b, i, 0)),
        out_shape=jax.ShapeDtypeStruct((bsz, s, w), BF16),
        scratch_shapes=[pltpu.VMEM((tq, w), F32)],
        compiler_params=pltpu.CompilerParams(
            dimension_semantics=("parallel", "arbitrary"), vmem_limit_bytes=VMEM_LIMIT),
        name="stickbreak",
    )(p16, p16, p16)


def _dsa_kernel(qi_ref, smq_ref, q_ref, sm_ref, k_ref, vt_ref, bias_ref, o_ref,
                sc_ref, scb_ref, qct_ref, kc_ref, bd_ref, lg_ref, *, tq, k_sel, wi_lane, wide):
    i = pl.program_id(1)
    tk = tq
    d = HEAD_DIM
    nh = N_HEADS
    ksel = float(k_sel)
    per_wide = wide // tk
    n_wide = (i + per_wide) // per_wide
    sub = 2 * tk
    lane_q = _iota((1, tq), 1)

    def tree(parts, op):
        while len(parts) > 1:
            parts = [op(parts[j], parts[j + 1]) if j + 1 < len(parts) else parts[j]
                     for j in range(0, len(parts), 2)]
        return parts[0]

    def col_fold(x, op=jnp.add, rows=8):
        return tree([x[r * rows:(r + 1) * rows] for r in range(x.shape[0] // rows)], op)

    @pl.when(i == 0)
    def _():
        def prep(g, carry):
            g0 = pl.multiple_of(g * wide, wide)
            hi, lo = _split(sm_ref[pl.ds(g0, wide), :][:, :IDX_DIM])
            kc_ref[pl.ds(g0, wide), :] = jnp.concatenate([hi, lo, hi], axis=1)
            return carry
        lax.fori_loop(0, sm_ref.shape[0] // wide, prep, 0)

    qit = qi_ref[...].T
    for p in range(IDX_HEADS // 2):
        halves = []
        for hh in (2 * p, 2 * p + 1):
            hi, lo = _split(qit[hh * IDX_DIM:(hh + 1) * IDX_DIM, :])
            halves.append(jnp.concatenate([hi, hi, lo], axis=0))
        qct_ref[p] = jnp.concatenate(halves, axis=1)
    w_rows = smq_ref[...].T[wi_lane:wi_lane + IDX_HEADS, :] * ((IDX_HEADS ** -0.5) * (IDX_DIM ** -0.5))

    q2t = (q_ref[...] * ((d ** -0.5) * LOG2E)).T.astype(BF16)
    zero_dq = jnp.zeros((d, tq), BF16)
    for p in range(nh // 2):
        top = jnp.concatenate([q2t[2 * p * d:(2 * p + 1) * d], zero_dq], axis=1)
        bot = jnp.concatenate([zero_dq, q2t[(2 * p + 1) * d:(2 * p + 2) * d]], axis=1)
        bd_ref[p] = jnp.concatenate([top, bot], axis=0)

    limit = i * tq + (lane_q // CHUNK + 1) * CHUNK

    rows_s = _iota((sub, tq), 0)

    def score_groups(gs, mm, masked):
        mn, mx = mm
        k0s = [pl.multiple_of(g * wide + sb * sub, sub) for g in gs for sb in range(wide // sub)]
        keys = [kc_ref[pl.ds(k0, sub), :] for k0 in k0s]
        accs = [jnp.zeros((sub, tq), F32) for _ in k0s]
        for p in range(IDX_HEADS // 2):
            rhs = qct_ref[p]
            for n, kk in enumerate(keys):
                s2 = jnp.dot(kk, rhs, preferred_element_type=F32)
                accs[n] = (accs[n] + jnp.maximum(s2[:, :tq], 0.0) * w_rows[2 * p:2 * p + 1, :]
                           + jnp.maximum(s2[:, tq:], 0.0) * w_rows[2 * p + 1:2 * p + 2, :])
        for k0, sct in zip(k0s, accs):
            if masked:
                adm = (k0 + rows_s) < limit
                mn = jnp.minimum(mn, col_fold(jnp.where(adm, sct, jnp.inf), jnp.minimum))
                sct = jnp.where(adm, sct, -jnp.inf)
            else:
                mn = jnp.minimum(mn, col_fold(sct, jnp.minimum))
            mx = jnp.maximum(mx, col_fold(sct, jnp.maximum))
            sc_ref[pl.ds(k0, sub), :] = sct
            scb_ref[pl.ds(k0, sub), :] = _floor_bf16(sct)
        return mn, mx

    def score_pair(j, mm):
        return score_groups((2 * j, 2 * j + 1), mm, False)

    n_full = n_wide - 1
    mm = lax.fori_loop(0, n_full // 2, score_pair,
                       (jnp.full((8, tq), jnp.inf, F32), jnp.full((8, tq), -jnp.inf, F32)))
    mm = lax.cond(n_full % 2 == 1, lambda c: score_groups((n_full - 1,), c, False), lambda c: c, mm)
    mn, mx = score_groups((n_wide - 1,), mm, True)

    n_pairs = (n_wide + 1) // 2

    @pl.when(n_wide % 2 == 1)
    def _():
        sc_ref[pl.ds(pl.multiple_of(n_wide * wide, wide), wide), :] = jnp.full((wide, tq), -jnp.inf, F32)
        scb_ref[pl.ds(pl.multiple_of(n_wide * wide, wide), wide), :] = jnp.full((wide, tq), -jnp.inf, BF16)
    rmin = jnp.min(mn, axis=0, keepdims=True)
    rmax = jnp.max(mx, axis=0, keepdims=True)

    def count(pred):
        def body(j, acc):
            for g in (2 * j, 2 * j + 1):
                acc = acc + col_fold(pred(sc_ref[pl.ds(pl.multiple_of(g * wide, wide), wide), :]))
            return acc
        return jnp.sum(lax.fori_loop(0, n_pairs, body, jnp.zeros((8, tq), F32)), axis=0, keepdims=True)

    def max_below(x):
        def body(j, acc):
            for g in (2 * j, 2 * j + 1):
                blk = sc_ref[pl.ds(pl.multiple_of(g * wide, wide), wide), :]
                acc = jnp.maximum(acc, col_fold(jnp.where(blk < x, blk, -jnp.inf), jnp.maximum))
            return acc
        return jnp.max(lax.fori_loop(0, n_pairs, body, jnp.full((8, tq), -jnp.inf, F32)), axis=0, keepdims=True)

    n_adm = limit.astype(F32)
    all_sel = n_adm <= ksel

    def bisect(c):
        lo, hi, c_lo = c
        mid = 0.5 * lo + 0.5 * hi
        cm = count(lambda blk: _ind(blk >= mid))
        ge = cm >= ksel
        return jnp.where(ge, mid, lo), jnp.where(ge, hi, mid), jnp.where(ge, cm, c_lo)

    def pending(c_lo, tied):
        return jnp.where(all_sel, 0.0, jnp.where(tied > 0.5, 0.0, _ind(c_lo != ksel)))

    def bisect_coarse(_, c):
        lo, hi, c_lo = c
        mid = _floor_bf16(0.5 * lo + 0.5 * hi).astype(F32)
        t_b = jnp.broadcast_to(mid, (16, tq)).astype(BF16)
        one_b = jnp.ones((16, tq), BF16)
        zero_b = jnp.zeros((16, tq), BF16)

        def body(j, acc):
            for g in (2 * j, 2 * j + 1):
                blk = scb_ref[pl.ds(pl.multiple_of(g * wide, wide), wide), :]
                ind = [jnp.where(blk[r * 16:(r + 1) * 16] >= t_b, one_b, zero_b) for r in range(wide // 16)]
                acc = acc + tree(ind, jnp.add).astype(F32)
            return acc

        acc = lax.fori_loop(0, n_pairs, body, jnp.zeros((16, tq), F32))
        cm = jnp.sum(acc, axis=0, keepdims=True)
        ge = cm >= ksel
        return jnp.where(ge, mid, lo), jnp.where(ge, hi, mid), jnp.where(ge, cm, c_lo)

    lo0 = _floor_bf16(rmin).astype(F32)
    hi0 = _floor_bf16(rmax + (jnp.abs(rmax) * (2.0 ** -6) + 1e-30)).astype(F32)
    state = lax.fori_loop(0, BISECT_COARSE, bisect_coarse, (lo0, hi0, n_adm))
    state = lax.fori_loop(0, BISECT_FIXED, lambda _, c: bisect(c), state)

    def round_cond(c):
        return jnp.max(pending(c[0][2], c[1])) > 0.5

    def round_body(c):
        st, tied, v, need = c

        def more_cond(s):
            return jnp.logical_and(s[0] < BISECT_EXTRA, jnp.max(pending(s[1][2], tied)) > 0.5)

        _, st = lax.while_loop(more_cond, lambda s: (s[0] + 1, bisect(s[1])), (jnp.int32(0), st))
        pend = pending(st[2], tied)

        def check(_):
            cand = max_below(st[1])
            c_ge = count(lambda blk: _ind(blk >= cand))
            c_gt = count(lambda blk: _ind(blk > cand))
            ok = jnp.where(pend > 0.5, _ind(c_ge >= ksel), 0.0)
            return (jnp.where(ok > 0.5, 1.0, tied), jnp.where(ok > 0.5, cand, v),
                    jnp.where(ok > 0.5, ksel - c_gt, need))

        tied, v, need = lax.cond(jnp.max(pend) > 0.5, check, lambda _: (tied, v, need), 0)
        return st, tied, v, need

    zeros1 = jnp.zeros((1, tq), F32)
    (lo_f, _, _), tied, v_tie, need = lax.while_loop(round_cond, round_body, (state, zeros1, zeros1, zeros1))
    vth = jnp.where(all_sel, F32_LOWEST, jnp.where(tied > 0.5, v_tie, lo_f))

    @pl.when(jnp.max(tied) > 0.5)
    def _():
        v_eq = jnp.where(tied > 0.5, v_tie, jnp.inf)
        incl = (_iota((tk, tk), 1) <= _iota((tk, tk), 0)).astype(BF16)

        def demote(g, seen):
            g0 = pl.multiple_of(g * wide, wide)
            xs = [sc_ref[pl.ds(g0 + pb * tk, tk), :] for pb in range(per_wide)]
            eqs = [_ind(x == v_eq) for x in xs]
            inblk = [jnp.dot(incl, e.astype(BF16), preferred_element_type=F32) for e in eqs]
            for pb in range(per_wide):
                rank = inblk[pb] + seen
                sc_ref[pl.ds(g0 + pb * tk, tk), :] = jnp.where(eqs[pb] * _ind(rank > need) > 0.5,
                                                               -jnp.inf, xs[pb])
                seen = seen + jnp.sum(col_fold(eqs[pb]), axis=0, keepdims=True)
            return seen

        lax.fori_loop(0, n_wide, demote, zeros1)

    g_near = jnp.maximum(i - 1, 0) // per_wide

    def logit_group(g, mx, near):
        out = list(mx)
        for sb in range(wide // sub):
            k0 = pl.multiple_of(g * wide + sb * sub, sub)
            sel = sc_ref[pl.ds(k0, sub), :] >= vth
            for p in range(nh // 2):
                pair = jnp.dot(k_ref[pl.ds(k0, sub), 2 * p * d:(2 * p + 2) * d], bd_ref[p],
                               preferred_element_type=F32)
                for hh in (2 * p, 2 * p + 1):
                    lm = pair[:, (hh - 2 * p) * tq:(hh - 2 * p + 1) * tq]
                    if near:
                        back = [jnp.clip(i - (g * per_wide + sb * (sub // tk) + pb), 0, 2)
                                for pb in range(sub // tk)]
                        lm = lm + jnp.concatenate([bias_ref[bk, hh] for bk in back], axis=0)
                    lm = jnp.where(sel, lm, NEG_BIG)
                    lg_ref[hh, pl.ds(k0, sub), :] = lm
                    out[hh] = jnp.maximum(out[hh], col_fold(lm, jnp.maximum))
        return tuple(out)

    mx = tuple(jnp.full((8, tq), NEG_BIG, F32) for _ in range(nh))
    def logit_pair(j, mx, near):
        return logit_group(2 * j + 1, logit_group(2 * j, mx, near), near)

    far_pairs = g_near // 2
    full_pairs = n_wide // 2
    odd = n_wide % 2 == 1
    mx = lax.fori_loop(0, far_pairs, functools.partial(logit_pair, near=False), mx)
    mx = lax.fori_loop(far_pairs, full_pairs, functools.partial(logit_pair, near=True), mx)
    mx = lax.cond(odd, lambda m: logit_group(n_wide - 1, m, True), lambda m: m, mx)
    m_q = [jnp.max(mx[hh], axis=0, keepdims=True) for hh in range(nh)]

    ones_rows = jnp.ones((8, wide), BF16)

    def pv_groups(gs, carry):
        ls, accs = list(carry[0]), list(carry[1])
        jobs = [(pl.multiple_of(g * wide, wide), hh) for g in gs for hh in range(nh)]
        ps = [jnp.exp2(lg_ref[hh, pl.ds(g0, wide), :] - m_q[hh]).astype(BF16) for g0, hh in jobs]
        outs = [jnp.dot(jnp.concatenate([vt_ref[hh * d:(hh + 1) * d, pl.ds(g0, wide)], ones_rows], axis=0),
                        p, preferred_element_type=F32) for (g0, hh), p in zip(jobs, ps)]
        for (_, hh), out in zip(jobs, outs):
            ls[hh] = ls[hh] + out[d:]
            accs[hh] = accs[hh] + out[:d]
        return tuple(ls), tuple(accs)

    acc = lax.fori_loop(0, full_pairs, lambda j, cr: pv_groups((2 * j, 2 * j + 1), cr),
                        (tuple(jnp.zeros((8, tq), F32) for _ in range(nh)),
                         tuple(jnp.zeros((d, tq), F32) for _ in range(nh))))
    ls, accs = lax.cond(odd, lambda cr: pv_groups((n_wide - 1,), cr), lambda cr: cr, acc)
    for hh in range(nh):
        o_ref[:, hh * d:(hh + 1) * d] = (accs[hh] / ls[hh][0:1]).T.astype(o_ref.dtype)


def _dsa(p32, p16, vt, bias_tiles, *, tq, cols):
    bsz, s, _ = p32.shape
    d = HEAD_DIM
    nh = N_HEADS
    wide = 4 * tq
    k_sel = min(TOPK_MAX, s // 4)
    w512 = nh * d
    kernel = functools.partial(_dsa_kernel, tq=tq, k_sel=k_sel, wi_lane=cols["wi_lane"], wide=wide)
    resident = dict(pipeline_mode=pl.Buffered(1))
    return pl.pallas_call(
        kernel,
        grid=(bsz, s // tq),
        in_specs=[pl.BlockSpec((None, tq, w512), lambda b, i: (b, i, cols["qi"] // nh)),
                  pl.BlockSpec((None, tq, d), lambda b, i: (b, i, cols["small"])),
                  pl.BlockSpec((None, tq, w512), lambda b, i: (b, i, cols["qb"] // nh)),
                  pl.BlockSpec((None, s, d), lambda b, i: (b, 0, cols["small"]), **resident),
                  pl.BlockSpec((None, s, w512), lambda b, i: (b, 0, cols["kb"] // nh), **resident),
                  pl.BlockSpec((w512, s), lambda b, i: (0, b), **resident),
                  pl.BlockSpec((3, nh, tq, tq), lambda b, i: (0, 0, 0, 0), **resident)],
        out_specs=pl.BlockSpec((None, tq, w512), lambda b, i: (b, i, 0)),
        out_shape=jax.ShapeDtypeStruct((bsz, s, w512), BF16),
        scratch_shapes=[pltpu.VMEM((s, tq), F32),
                        pltpu.VMEM((s, tq), BF16),
                        pltpu.VMEM((IDX_HEADS // 2, 3 * IDX_DIM, 2 * tq), BF16),
                        pltpu.VMEM((s, 3 * IDX_DIM), BF16),
                        pltpu.VMEM((nh // 2, 2 * d, 2 * tq), BF16),
                        pltpu.VMEM((nh, s, tq), F32)],
        compiler_params=pltpu.CompilerParams(
            dimension_semantics=("parallel", "arbitrary"), vmem_limit_bytes=VMEM_LIMIT),
        name="dsa",
    )(p32, p32, p32, p32, p16, vt, bias_tiles)


def _t5_bucket(rel):
    nb = REL_BUCKETS // 2
    max_exact = nb // 2
    ret = jnp.where(rel > 0, nb, 0)
    n = jnp.abs(rel)
    large = max_exact + (jnp.log(jnp.maximum(n, 1).astype(F32) / max_exact)
                         / math.log(REL_MAX_DIST / max_exact) * (nb - max_exact)).astype(jnp.int32)
    large = jnp.minimum(large, nb - 1)
    return ret + jnp.where(n < max_exact, n, large)


def _bias_tiles(rel_table, tq):
    assert tq >= REL_MAX_DIST
    t = jnp.arange(tq)
    back = jnp.arange(3)
    rel = (t[None, None, :] - back[:, None, None] * tq) - t[None, :, None]
    onehot = (_t5_bucket(rel)[..., None] == jnp.arange(REL_BUCKETS)).astype(F32)
    tiles = jnp.einsum("bqkn,nh->bhkq", onehot, rel_table.astype(F32),
                       precision=HIGHEST)
    return (tiles - tiles[2:3]) * LOG2E


def _even_layout(w_in):
    d = HEAD_DIM
    a_w = 2 * N_HEADS * d + N_HEADS * d
    offs = {}
    o = 0
    for name, w in (("qkv", a_w), ("z", N_HEADS * d), ("a", N_HEADS), ("b", N_HEADS),
                    ("qb", N_HEADS * d), ("kb", N_HEADS * d), ("vb", N_HEADS * d),
                    ("qi", IDX_HEADS * IDX_DIM), ("ki", IDX_DIM), ("wi", IDX_HEADS)):
        offs[name] = (o, o + w)
        o += w
    assert o == w_in.shape[1]
    sl = lambda n: w_in[:, offs[n][0]:offs[n][1]]
    small_w = IDX_DIM + 2 * N_HEADS + IDX_HEADS
    small_pad = -small_w % d
    zeros = lambda n: jnp.zeros((w_in.shape[0], n), w_in.dtype)
    w32 = jnp.concatenate([sl("qkv"), sl("z"), sl("qb"), sl("qi"),
                           sl("ki"), sl("a"), sl("b"), sl("wi"), zeros(small_pad)], axis=1)
    n32 = w32.shape[1]
    tn = n32 // 5
    assert tn * 5 == n32 and tn % d == 0
    w16 = jnp.concatenate([sl("kb"), zeros(tn - N_HEADS * d)], axis=1)
    nh = N_HEADS
    cols = dict(qa=0, ka=nh, va=2 * nh, za=3 * nh, qb=4 * nh, qi=5 * nh, small=6 * nh, kb=0,
                a_lane=IDX_DIM, b_lane=IDX_DIM + nh, wi_lane=IDX_DIM + 2 * nh, n32=n32, tn=tn)
    return jnp.concatenate([w32, w16], axis=1).astype(BF16), sl("vb").T.astype(BF16), cols


def kernel(x, norm_g, w_in_even, conv_w_even, a_log_even, dt_bias_even, a_norm_even, w_out_even,
           rel_bias, w_in_odd, lb_logits, d_norm_odd, w_out_odd, w_gate, w_up, w_down):
    bsz, s, d = x.shape
    t = bsz * s
    depth = norm_g.shape[0]
    nh = N_HEADS
    tq = Q_TILE
    lb_all = jnp.cumsum(jax.nn.softmax(lb_logits.astype(F32), axis=0), axis=0)
    lb_all = lb_all - lb_all[:1]
    odd_cols = dict(qc=0, kc=nh, vc=2 * nh, qd=0, fd=nh, id=2 * nh, gd=3 * nh)
    bias_tiles = _bias_tiles(rel_bias, tq)

    h = x.reshape(t, d)
    for l in range(depth):
        if l % 2 == 0:
            e = l // 2
            w_even, w_vt, cols = _even_layout(w_in_even[e])
            p32, p16, vt = _norm_matmul(h, norm_g[l, 0], w_even, tm=PROJ_TILE, tn=cols["tn"], n32=cols["n32"],
                                        w_t=w_vt)
            p32 = p32.reshape(bsz, s, -1)
            p16 = p16.reshape(bsz, s, -1)
            o_1 = _deltanet(p32, conv_w_even[e], a_log_even[e], dt_bias_even[e], a_norm_even[e],
                            ts=min(DELTANET_TILE, s), cols=cols)
            o_2 = _dsa(p32, p16, vt, bias_tiles, tq=tq, cols=cols)
            w_out = w_out_even[e]
        else:
            o = l // 2
            n16 = 3 * nh * HEAD_DIM
            w_odd = jnp.concatenate([w_in_odd[o][:, n16:], w_in_odd[o][:, :n16]], axis=1).astype(BF16)
            p32, p16 = _norm_matmul(h, norm_g[l, 0], w_odd, tm=PROJ_TILE, tn=ODD_COL_TILE, n32=w_odd.shape[1] - n16)
            p32 = p32.reshape(bsz, s, -1)
            p16 = p16.reshape(bsz, s, -1)
            o_1 = _stickbreak(p16, tq=tq, cols=odd_cols)
            o_2 = _hgrn2(p32, lb_all[l], d_norm_odd[o], ts=min(SEQ_TILE, s), cols=odd_cols)
            w_out = w_out_odd[o]
        h = _mix_ffn(o_1.reshape(t, -1), o_2.reshape(t, -1), w_out, h, norm_g[l, 1], norm_g[l, 2], norm_g[l, 3],
                     w_gate[l], w_up[l], w_down[l], tm=ROW_TILE, tf=FFN_TILE)
    return h.reshape(bsz, s, d)
```

```python
import functools
import math

import jax
import jax.numpy as jnp
from jax import lax
from jax.experimental import pallas as pl
from jax.experimental.pallas import tpu as pltpu

F32 = jnp.float32
BF16 = jnp.bfloat16
HIGHEST = lax.Precision.HIGHEST

CHUNK = 64
HEAD_DIM = 128
N_HEADS = 4
IDX_HEADS = 8
IDX_DIM = 64
TOPK_MAX = 256
CONV_WIDTH = 4
REL_BUCKETS = 32
REL_MAX_DIST = 128
EPS = 1e-6
NEG_BIG = -1e30
LOG2E = 1.4426950408889634
BISECT_COARSE = 10
BISECT_FIXED = 8
BISECT_EXTRA = 6
F32_LOWEST = -3.4028234663852886e38
EXP_ZERO_BELOW = -104.0
VMEM_LIMIT = 56 * 1024 * 1024

PROJ_TILE = 2048
ROW_TILE = 512
DELTANET_TILE = 1024
SEQ_TILE = 512
Q_TILE = 128
ODD_COL_TILE = 512
FFN_TILE = 2816


def _mm(a, b):
    return jnp.dot(a.astype(BF16), b.astype(BF16), preferred_element_type=F32)


def _mm_nt(a, b):
    return lax.dot_general(a.astype(BF16), b.astype(BF16), (((1,), (1,)), ((), ())),
                           preferred_element_type=F32)


def _mm_tn(a, b):
    return lax.dot_general(a.astype(BF16), b.astype(BF16), (((0,), (0,)), ((), ())),
                           preferred_element_type=F32)


def _split(x):
    hi = x.astype(BF16)
    return hi, (x - hi.astype(F32)).astype(BF16)


def _floor_bf16(x):
    bits = pltpu.bitcast(x, jnp.int32)
    down = jnp.where(bits >= 0, bits, bits + 0xFFFF) & jnp.int32(-65536)
    return pltpu.bitcast(down, F32).astype(BF16)


def _sigmoid(x):
    return 1.0 / (1.0 + jnp.exp(-x))


def _silu(x):
    return x * _sigmoid(x)


def _softplus(x):
    return jnp.maximum(x, 0.0) + jnp.log1p(jnp.exp(-jnp.abs(x)))


def _rms(x, g):
    return x * lax.rsqrt(jnp.mean(x * x, axis=-1, keepdims=True) + EPS) * g


def _iota(shape, dim):
    return lax.broadcasted_iota(jnp.int32, shape, dim)


def _ind(mask):
    return jnp.where(mask, 1.0, 0.0)


def _norm_matmul_kernel(x_ref, g_ref, w_ref, *rest, n_t, tiles32):
    if n_t:
        wt_ref, o32_ref, o16_ref, ot_ref, xn_ref = rest
    else:
        o32_ref, o16_ref, xn_ref = rest
    j = pl.program_id(1)

    @pl.when(j == 0)
    def _():
        xn_ref[...] = _rms(x_ref[...], g_ref[...]).astype(BF16)
        if n_t:
            ot_ref[...] = lax.dot_general(wt_ref[...], xn_ref[...], (((1,), (1,)), ((), ())),
                                          preferred_element_type=F32).astype(BF16)

    y = jnp.dot(xn_ref[...], w_ref[...], preferred_element_type=F32)

    @pl.when(j < tiles32)
    def _():
        o32_ref[...] = y

    @pl.when(j >= tiles32)
    def _():
        o16_ref[...] = y.astype(BF16)


def _norm_matmul(x, g, w, *, tm, tn, n32, w_t=None):
    t, d = x.shape
    n = w.shape[1]
    n_t = 0 if w_t is None else w_t.shape[0]
    tiles32 = n32 // tn
    assert tiles32 * tn == n32 and (n - n32) % tn == 0 and 0 < n32 < n
    in_specs = [pl.BlockSpec((tm, d), lambda i, j: (i, 0)),
                pl.BlockSpec((1, d), lambda i, j: (0, 0)),
                pl.BlockSpec((d, tn), lambda i, j: (0, j))]
    out_specs = [pl.BlockSpec((tm, tn), lambda i, j: (i, jnp.minimum(j, tiles32 - 1))),
                 pl.BlockSpec((tm, tn), lambda i, j: (i, jnp.maximum(j - tiles32, 0)))]
    out_shape = [jax.ShapeDtypeStruct((t, n32), F32), jax.ShapeDtypeStruct((t, n - n32), BF16)]
    args = [x, g.reshape(1, d), w]
    if n_t:
        in_specs.append(pl.BlockSpec((n_t, d), lambda i, j: (0, 0)))
        out_specs.append(pl.BlockSpec((n_t, tm), lambda i, j: (0, i)))
        out_shape.append(jax.ShapeDtypeStruct((n_t, t), BF16))
        args.append(w_t)
    return pl.pallas_call(
        functools.partial(_norm_matmul_kernel, n_t=n_t, tiles32=tiles32),
        grid=(t // tm, n // tn),
        in_specs=in_specs,
        out_specs=out_specs,
        out_shape=out_shape,
        scratch_shapes=[pltpu.VMEM((tm, d), BF16)],
        compiler_params=pltpu.CompilerParams(
            dimension_semantics=("parallel", "arbitrary"), vmem_limit_bytes=VMEM_LIMIT),
        name="norm_matmul",
    )(*args)


def _mix_ffn_kernel(ca_ref, cb_ref, wa_ref, wb_ref, h_ref, gmix_ref, gpre_ref, gpost_ref,
                    wg_ref, wu_ref, wd_ref, o_ref, h1_ref, xn_ref, acc_ref):
    f = pl.program_id(1)

    @pl.when(f == 0)
    def _():
        y = (jnp.dot(ca_ref[...], wa_ref[...], preferred_element_type=F32)
             + jnp.dot(cb_ref[...], wb_ref[...], preferred_element_type=F32))
        h1 = h_ref[...] + _rms(y, gmix_ref[...])
        h1_ref[...] = h1
        xn_ref[...] = _rms(h1, gpre_ref[...]).astype(BF16)
        acc_ref[...] = jnp.zeros_like(acc_ref)

    xn = xn_ref[...]
    gate = jnp.dot(xn, wg_ref[...], preferred_element_type=F32)
    up = jnp.dot(xn, wu_ref[...], preferred_element_type=F32)
    act = (_silu(gate) * up).astype(BF16)
    acc_ref[...] += jnp.dot(act, wd_ref[...], preferred_element_type=F32)

    @pl.when(f == pl.num_programs(1) - 1)
    def _():
        o_ref[...] = h1_ref[...] + _rms(acc_ref[...], gpost_ref[...])


def _mix_ffn(ca, cb, w_out, h, g_mix, g_pre, g_post, wg, wu, wd, *, tm, tf):
    t, d = h.shape
    ff = wg.shape[1]
    wa_n = ca.shape[1]
    wb_n = cb.shape[1]
    row = pl.BlockSpec((1, d), lambda i, f: (0, 0))
    once = dict(pipeline_mode=pl.Buffered(1)) if tf == ff else {}
    return pl.pallas_call(
        _mix_ffn_kernel,
        grid=(t // tm, ff // tf),
        in_specs=[pl.BlockSpec((tm, wa_n), lambda i, f: (i, 0)),
                  pl.BlockSpec((tm, wb_n), lambda i, f: (i, 0)),
                  pl.BlockSpec((wa_n, d), lambda i, f: (0, 0)),
                  pl.BlockSpec((wb_n, d), lambda i, f: (0, 0)),
                  pl.BlockSpec((tm, d), lambda i, f: (i, 0)),
                  row, row, row,
                  pl.BlockSpec((d, tf), lambda i, f: (0, f), **once),
                  pl.BlockSpec((d, tf), lambda i, f: (0, f), **once),
                  pl.BlockSpec((tf, d), lambda i, f: (f, 0), **once)],
        out_specs=pl.BlockSpec((tm, d), lambda i, f: (i, 0)),
        out_shape=jax.ShapeDtypeStruct((t, d), F32),
        scratch_shapes=[pltpu.VMEM((tm, d), F32), pltpu.VMEM((tm, d), BF16), pltpu.VMEM((tm, d), F32)],
        compiler_params=pltpu.CompilerParams(
            dimension_semantics=("parallel", "arbitrary"), vmem_limit_bytes=VMEM_LIMIT),
        name="mix_ffn",
    )(ca, cb, w_out[:wa_n].astype(BF16), w_out[wa_n:].astype(BF16), h,
      g_mix.reshape(1, d), g_pre.reshape(1, d), g_post.reshape(1, d),
      wg.astype(BF16), wu.astype(BF16), wd.astype(BF16))


def _deltanet_kernel(xq_ref, xk_ref, xv_ref, z_ref, sm_ref, cwq_ref, cwk_ref, cwv_ref,
                     alog_ref, dtb_ref, gn_ref, o_ref,
                     xpad_ref, q_ref, k_ref, v_ref, gb_ref, bb_ref, u_ref, w_ref, qk_ref, st_ref,
                     *, ts, a_col, b_col):
    s = pl.program_id(1)
    c = CHUNK
    d = HEAD_DIM
    nh = N_HEADS

    @pl.when(s == 0)
    def _():
        xpad_ref[:, 0:8, :] = jnp.zeros((3, 8, nh * d), F32)
        st_ref[...] = jnp.zeros_like(st_ref)

    @pl.when(s != 0)
    def _():
        xpad_ref[:, 0:8, :] = xpad_ref[:, ts:ts + 8, :]

    xpad_ref[0, 8:ts + 8, :] = xq_ref[...]
    xpad_ref[1, 8:ts + 8, :] = xk_ref[...]
    xpad_ref[2, 8:ts + 8, :] = xv_ref[...]

    def conv_silu(idx, cw_ref, hs):
        cw = cw_ref[:, hs]
        acc = xpad_ref[idx, 8 - (CONV_WIDTH - 1):8 - (CONV_WIDTH - 1) + ts, hs] * cw[0:1, :]
        for j in range(1, CONV_WIDTH):
            off = 8 - (CONV_WIDTH - 1) + j
            acc = acc + xpad_ref[idx, off:off + ts, hs] * cw[j:j + 1, :]
        return _silu(acc)

    def l2norm(t):
        return t * lax.rsqrt(jnp.sum(t * t, axis=-1, keepdims=True) + EPS)

    row = _iota((c, c), 0)
    col = _iota((c, c), 1)
    tri = (col <= row)
    strict = (col < row)
    tri_f = tri.astype(F32)
    upper_f = (row <= col).astype(F32)
    eye = (row == col).astype(F32)
    gnorm = gn_ref[...]
    chunks = range(ts // c)
    rs = [slice(ci * c, (ci + 1) * c) for ci in chunks]
    tri2 = jnp.concatenate([tri_f, tri_f], axis=1).astype(BF16)
    ones2 = jnp.ones((c, 2 * c), BF16)

    def cum2(lhs2, x):
        hi, lo = _split(x)
        return jnp.dot(lhs2, jnp.concatenate([hi, lo], axis=0), preferred_element_type=F32)

    for hh in range(nh):
        hs = slice(hh * d, (hh + 1) * d)
        q_ref[:, hs] = l2norm(conv_silu(0, cwq_ref, hs)) * (d ** -0.5)
        k_ref[:, hs] = l2norm(conv_silu(1, cwk_ref, hs))
        v_ref[:, hs] = conv_silu(2, cwv_ref, hs)

        a_raw = sm_ref[:, a_col + hh:a_col + hh + 1]
        b_raw = sm_ref[:, b_col + hh:b_col + hh + 1]
        g = -jnp.exp(alog_ref[:, hh:hh + 1]) * _softplus(a_raw + dtb_ref[:, hh:hh + 1])
        gb_ref[:, hs] = jnp.broadcast_to(g, (ts, d))
        bb_ref[:, hs] = jnp.broadcast_to(_sigmoid(b_raw), (ts, d))

        q = [q_ref[r, hs] for r in rs]
        k = [k_ref[r, hs] for r in rs]
        beta = [bb_ref[r, hs] for r in rs]
        gb = [gb_ref[r, hs] for r in rs]
        gc = [cum2(tri2, x) for x in gb]
        gc_row = [cum2(ones2, x[:, :c] * upper_f) for x in gb]
        decay = [jnp.where(tri, jnp.exp(jnp.minimum(a[:, :c] - b, 0.0)), 0.0) for a, b in zip(gc, gc_row)]
        kk = [_mm_nt(x, x) for x in k]
        n = [-jnp.where(strict, b[:, :c] * x * dc, 0.0) for b, x, dc in zip(beta, kk, decay)]
        inv = [eye + x for x in n]
        for step in range(5):
            nb = [x.astype(BF16) for x in n]
            n = [jnp.dot(x, x, preferred_element_type=F32) for x in nb]
            inv = [iv + _mm(iv, x) for iv, x in zip(inv, n)]
        egc = [jnp.exp(x) for x in gc]
        gl = [x[c - 1:c, :] for x in gc]
        inv_l = [x.astype(BF16) for x in inv]
        u = [_mm(a, v_ref[r, hs] * b) for a, r, b in zip(inv_l, rs, beta)]
        w = [_mm(a, x * (b * e)) for a, x, b, e in zip(inv_l, k, beta, egc)]
        qk = [_mm_nt(a, b) * dc for a, b, dc in zip(q, k, decay)]
        for ci in chunks:
            r = rs[ci]
            u_ref[r, hs] = u[ci]
            w_ref[r, hs] = w[ci]
            qk_ref[hh, r, :] = qk[ci]
            q_ref[r, hs] = q[ci] * egc[ci]
            k_ref[r, hs] = k[ci] * jnp.exp(gl[ci] - gc[ci])
            gb_ref[r, hs] = jnp.broadcast_to(jnp.exp(gl[ci]), (c, d))

    hss = [slice(hh * d, (hh + 1) * d) for hh in range(nh)]

    def chunk_step(ci, st):
        r0 = pl.multiple_of(ci * c, c)
        rows = pl.ds(r0, c)
        w_st = [_mm(w_ref[rows, hs], s_) for hs, s_ in zip(hss, st)]
        q_st = [_mm(q_ref[rows, hs], s_) for hs, s_ in zip(hss, st)]
        v_new = [u_ref[rows, hs] - x for hs, x in zip(hss, w_st)]
        o = [a + _mm(qk_ref[hh, rows, :], v) for hh, (a, v) in enumerate(zip(q_st, v_new))]
        kv = [_mm_tn(k_ref[rows, hs], v) for hs, v in zip(hss, v_new)]
        for hh, hs in enumerate(hss):
            o_ref[rows, hs] = (_rms(o[hh], gnorm) * _silu(z_ref[rows, hs])).astype(o_ref.dtype)
        return [s_ * gb_ref[pl.ds(r0, 1), hs] + x for s_, hs, x in zip(st, hss, kv)]

    per_step = math.gcd(ts // c, 4)

    def chunk_group(j, carry):
        st = [st_ref[hh] for hh in range(nh)]
        for n_ in range(per_step):
            st = chunk_step(per_step * j + n_, st)
        for hh in range(nh):
            st_ref[hh] = st[hh]
        return carry

    lax.fori_loop(0, ts // (per_step * c), chunk_group, 0)


def _deltanet(p32, conv_w, a_log, dt_bias, a_norm_g, *, ts, cols):
    bsz, s, _ = p32.shape
    d = HEAD_DIM
    nh = N_HEADS
    w = nh * d
    pad = lambda t: jnp.pad(t.astype(F32), (0, d - t.shape[0])).reshape(1, d)
    kernel = functools.partial(_deltanet_kernel, ts=ts, a_col=cols["a_lane"], b_col=cols["b_lane"])
    tile = lambda name: pl.BlockSpec((None, ts, w), lambda b, i: (b, i, cols[name] // nh))
    conv = lambda k: pl.BlockSpec((CONV_WIDTH, w), lambda b, i: (0, k))
    row = pl.BlockSpec((1, d), lambda b, i: (0, 0))
    return pl.pallas_call(
        kernel,
        grid=(bsz, s // ts),
        in_specs=[tile("qa"), tile("ka"), tile("va"), tile("za"),
                  pl.BlockSpec((None, ts, d), lambda b, i: (b, i, cols["small"])),
                  conv(0), conv(1), conv(2), row, row, row],
        out_specs=pl.BlockSpec((None, ts, w), lambda b, i: (b, i, 0)),
        out_shape=jax.ShapeDtypeStruct((bsz, s, w), BF16),
        scratch_shapes=[pltpu.VMEM((3, ts + 8, w), F32)]
        + [pltpu.VMEM((ts, w), F32) for _ in range(7)]
        + [pltpu.VMEM((nh, ts, CHUNK), F32), pltpu.VMEM((nh, d, d), F32)],
        compiler_params=pltpu.CompilerParams(
            dimension_semantics=("parallel", "arbitrary"), vmem_limit_bytes=VMEM_LIMIT),
        name="deltanet",
    )(p32, p32, p32, p32, p32, conv_w.astype(F32), conv_w.astype(F32), conv_w.astype(F32),
      pad(a_log), pad(dt_bias), a_norm_g.astype(F32).reshape(1, d))


def _hgrn2_kernel(q_ref, f_ref, i_ref, gate_ref, lb_ref, gn_ref, o_ref,
                  qs_ref, ks_ref, gc_ref, st_ref, *, ts):
    s = pl.program_id(1)
    c = CHUNK
    d = HEAD_DIM
    nh = N_HEADS
    SUB = 16

    @pl.when(s == 0)
    def _():
        st_ref[...] = jnp.zeros_like(st_ref)

    lb = lb_ref[...]
    f_raw = f_ref[...]
    log_sig = jnp.minimum(f_raw, 0.0) - jnp.log1p(jnp.exp(-jnp.abs(f_raw)))
    la = jnp.log(lb)
    lbb = jnp.log1p(-lb) + log_sig
    log_f = jnp.maximum(la, lbb) + jnp.log1p(jnp.exp(-jnp.abs(la - lbb)))
    qs_ref[...] = _silu(q_ref[...])
    ks_ref[...] = (1.0 - lb) * _sigmoid(-f_raw)

    row = _iota((c, c), 0)
    col = _iota((c, c), 1)
    tri_f = (col <= row).astype(F32)
    ones_dd = jnp.ones((d, d), BF16)
    rows_8d = _iota((8, d), 0)
    gnorm = gn_ref[...]

    tri2 = jnp.concatenate([tri_f, tri_f], axis=1).astype(BF16)
    for ci in range(ts // c):
        hi, lo = _split(log_f[ci * c:(ci + 1) * c, :])
        gc_ref[ci * c:(ci + 1) * c, :] = jnp.dot(tri2, jnp.concatenate([hi, lo], axis=0),
                                                 preferred_element_type=F32)

    blocks = [(sb * SUB, (sb + 1) * SUB) for sb in range(c // SUB)]

    def chunk_loop(ci, carry):
        r0 = pl.multiple_of(ci * c, c)
        rows = pl.ds(r0, c)
        hss = [slice(hh * d, (hh + 1) * d) for hh in range(nh)]
        q = [qs_ref[rows, hs] for hs in hss]
        k = [ks_ref[rows, hs] for hs in hss]
        v = [i_ref[rows, hs] for hs in hss]
        gc = [gc_ref[rows, hs] for hs in hss]

        def near_products(q, k, gc):
            prods = []
            for top, end in blocks:
                for j in range(top, end):
                    lo = (j // 8) * 8
                    e = jnp.exp2(gc[lo:end, :] - gc[j:j + 1, :])
                    if j % 8:
                        head = jnp.where(rows_8d >= j - lo, e[:8], 0.0)
                        e = jnp.concatenate([head, e[8:]], axis=0) if lo + 8 < end else head
                    prods.append(q[lo:end, :] * k[j:j + 1, :] * e)
            return jnp.concatenate(prods, axis=0).astype(BF16)

        def far_operands(q, k, gc):
            out = []
            for top, end in blocks[1:]:
                g_b = gc[top - 1:top, :]
                out.append((q[top:end, :] * jnp.exp(gc[top:end, :] - g_b),
                            k[:top, :] * jnp.exp(jnp.minimum(g_b - gc[:top, :], 0.0))))
            return out

        near = [near_products(a, b, g * LOG2E) for a, b, g in zip(q, k, gc)]
        far_ops = [far_operands(*x) for x in zip(q, k, gc)]
        st = [st_ref[hh] for hh in range(nh)]
        gl = [x[c - 1:c, :] for x in gc]
        sums = [jnp.dot(x, ones_dd, preferred_element_type=F32) for x in near]
        qk_far = [[_mm_nt(qe, ke) for qe, ke in ops] for ops in far_ops]
        far = [[_mm(a, vv[:top, :]) for a, (top, _) in zip(qs, blocks[1:])] for qs, vv in zip(qk_far, v)]
        o_st = [_mm_nt(a * jnp.exp(g), s_) for a, g, s_ in zip(q, gc, st)]
        kv = [_mm_tn(vv, kk * jnp.exp(g_l - g)) for vv, kk, g_l, g in zip(v, k, gl, gc)]

        for hh, hs in enumerate(hss):
            groups = [jnp.zeros((8, d), F32) for _ in range(c // 8)]
            at = 0
            for top, end in blocks:
                for j in range(top, end):
                    v_j = v[hh][j:j + 1, :]
                    for g in range(j // 8, end // 8):
                        groups[g] = groups[g] + sums[hh][at:at + 8, :] * v_j
                        at += 8
            for f, (top, end) in zip(far[hh], blocks[1:]):
                for g in range(top // 8, end // 8):
                    groups[g] = groups[g] + f[(g * 8 - top):(g * 8 - top + 8), :]
            o = jnp.concatenate(groups, axis=0) + o_st[hh]
            st_ref[hh] = st[hh] * jnp.exp(gl[hh]) + kv[hh]
            o_ref[rows, hs] = (_rms(o, gnorm) * _silu(gate_ref[rows, hs])).astype(o_ref.dtype)
        return carry

    per_step = math.gcd(ts // c, 4)

    def chunk_group(j, carry):
        for n_ in range(per_step):
            chunk_loop(per_step * j + n_, carry)
        return carry

    lax.fori_loop(0, ts // (per_step * c), chunk_group, 0)


def _hgrn2(p32, lb, d_norm_g, *, ts, cols):
    bsz, s, _ = p32.shape
    d = HEAD_DIM
    nh = N_HEADS
    w = nh * d
    kernel = functools.partial(_hgrn2_kernel, ts=ts)
    tile = lambda name: pl.BlockSpec((None, ts, w), lambda b, i: (b, i, cols[name] // nh))
    return pl.pallas_call(
        kernel,
        grid=(bsz, s // ts),
        in_specs=[tile("qd"), tile("fd"), tile("id"), tile("gd"),
                  pl.BlockSpec((1, w), lambda b, i: (0, 0)),
                  pl.BlockSpec((1, d), lambda b, i: (0, 0))],
        out_specs=pl.BlockSpec((None, ts, w), lambda b, i: (b, i, 0)),
        out_shape=jax.ShapeDtypeStruct((bsz, s, w), BF16),
        scratch_shapes=[pltpu.VMEM((ts, w), F32), pltpu.VMEM((ts, w), F32),
                        pltpu.VMEM((ts, w), F32), pltpu.VMEM((nh, d, d), F32)],
        compiler_params=pltpu.CompilerParams(
            dimension_semantics=("parallel", "arbitrary"), vmem_limit_bytes=VMEM_LIMIT),
        name="hgrn2",
    )(p32, p32, p32, p32, lb.astype(F32).reshape(1, w), d_norm_g.astype(F32).reshape(1, d))


def _stickbreak_kernel(q_ref, k_ref, v_ref, o_ref, acc_ref, *, tq):
    i = pl.program_id(1)
    d = HEAD_DIM
    nh = N_HEADS
    row = _iota((tq, tq), 0)
    col = _iota((tq, tq), 1)
    causal = col < row
    later = (row > col).astype(BF16)
    later2 = jnp.concatenate([later, later], axis=0)

    heads = [slice(hh * d, (hh + 1) * d) for hh in range(nh)]

    def scores(blocks):
        jobs = [(j, dg, hs) for j, dg in blocks for hs in heads]
        z = [_mm_nt(q_ref[:, hs], k_ref[pl.ds(pl.multiple_of(j * tq, tq), tq), hs]) * (d ** -0.5)
             for j, _, hs in jobs]
        sp = [_softplus(x) for x in z]
        l1m = [jnp.where(causal, -x, 0.0) if dg else -x for x, (_, dg, _) in zip(sp, jobs)]
        rest = [jnp.dot(jnp.concatenate(_split(x), axis=1), later2, preferred_element_type=F32)
                for x in l1m]
        out = [((a - b) + r, l) for a, b, r, l in zip(z, sp, rest, l1m)]
        return [out[b * nh:(b + 1) * nh] for b in range(len(blocks))]

    def block(j, carries):
        (sc,) = scores([(j, False)])
        ps = [jnp.exp(logw + c) for (logw, _), c in zip(sc, carries)]
        pv = [_mm(p, v_ref[pl.ds(pl.multiple_of(j * tq, tq), tq), hs]) for p, hs in zip(ps, heads)]
        for hs, x in zip(heads, pv):
            acc_ref[:, hs] += x
        return tuple(c + jnp.sum(l1m, axis=-1, keepdims=True) for (_, l1m), c in zip(sc, carries))

    j1 = jnp.maximum(i - 1, 0)
    j2 = jnp.maximum(i - 2, 0)
    live1 = jnp.where(i > 0, 1.0, 0.0)
    live2 = jnp.where(i > 1, 1.0, 0.0)
    s0, s1, s2 = scores([(i, True), (j1, False), (j2, False)])
    carries = []
    for hh, hs in enumerate(heads):
        c0 = jnp.sum(s0[hh][1], axis=-1, keepdims=True)
        c1 = c0 + jnp.sum(s1[hh][1], axis=-1, keepdims=True)
        p0 = jnp.where(causal, jnp.exp(s0[hh][0]), 0.0)
        p1 = jnp.exp(s1[hh][0] + c0) * live1
        p2 = jnp.exp(s2[hh][0] + c1) * live2
        acc_ref[:, hs] = (_mm(p0, v_ref[pl.ds(pl.multiple_of(i * tq, tq), tq), hs])
                          + _mm(p1, v_ref[pl.ds(pl.multiple_of(j1 * tq, tq), tq), hs])
                          + _mm(p2, v_ref[pl.ds(pl.multiple_of(j2 * tq, tq), tq), hs]))
        carries.append(c1 + jnp.sum(s2[hh][1], axis=-1, keepdims=True))
    carries = tuple(carries)

    def cond(c):
        worst = functools.reduce(jnp.maximum, c[1])
        return jnp.logical_and(c[0] >= 0, jnp.max(worst) >= EXP_ZERO_BELOW)

    def body(c):
        return c[0] - 1, block(c[0], c[1])

    lax.while_loop(cond, body, (i - 3, carries))
    o_ref[...] = acc_ref[...].astype(o_ref.dtype)


def _stickbreak(p16, *, tq, cols):
    bsz, s, _ = p16.shape
    nh = N_HEADS
    w = nh * HEAD_DIM
    kernel = functools.partial(_stickbreak_kernel, tq=tq)
    resident = dict(pipeline_mode=pl.Buffered(1))
    return pl.pallas_call(
        kernel,
        grid=(bsz, s // tq),
        in_specs=[pl.BlockSpec((None, tq, w), lambda b, i: (b, i, cols["qc"] // nh)),
                  pl.BlockSpec((None, s, w), lambda b, i: (b, 0, cols["kc"] // nh), **resident),
                  pl.BlockSpec((None, s, w), lambda b, i: (b, 0, cols["vc"] // nh), **resident)],
        out_specs=pl.BlockSpec((None, tq, w), lambda b, i: (b, i, 0)),
        out_shape=jax.ShapeDtypeStruct((bsz, s, w), BF16),
        scratch_shapes=[pltpu.VMEM((tq, w), F32)],
        compiler_params=pltpu.CompilerParams(
            dimension_semantics=("parallel", "arbitrary"), vmem_limit_bytes=VMEM_LIMIT),
        name="stickbreak",
    )(p16, p16, p16)


def _dsa_kernel(qi_ref, smq_ref, q_ref, sm_ref, k_ref, vt_ref, bias_ref, o_ref,
                sc_ref, scb_ref, qct_ref, kc_ref, bd_ref, lg_ref, *, tq, k_sel, wi_lane, wide):
    i = pl.program_id(1)
    tk = tq
    d = HEAD_DIM
    nh = N_HEADS
    ksel = float(k_sel)
    per_wide = wide // tk
    n_wide = (i + per_wide) // per_wide
    sub = 2 * tk
    lane_q = _iota((1, tq), 1)

    def tree(parts, op):
        while len(parts) > 1:
            parts = [op(parts[j], parts[j + 1]) if j + 1 < len(parts) else parts[j]
                     for j in range(0, len(parts), 2)]
        return parts[0]

    def col_fold(x, op=jnp.add, rows=8):
        return tree([x[r * rows:(r + 1) * rows] for r in range(x.shape[0] // rows)], op)

    @pl.when(i == 0)
    def _():
        def prep(g, carry):
            g0 = pl.multiple_of(g * wide, wide)
            hi, lo = _split(sm_ref[pl.ds(g0, wide), :][:, :IDX_DIM])
            kc_ref[pl.ds(g0, wide), :] = jnp.concatenate([hi, lo, hi], axis=1)
            return carry
        lax.fori_loop(0, sm_ref.shape[0] // wide, prep, 0)

    qit = qi_ref[...].T
    for p in range(IDX_HEADS // 2):
        halves = []
        for hh in (2 * p, 2 * p + 1):
            hi, lo = _split(qit[hh * IDX_DIM:(hh + 1) * IDX_DIM, :])
            halves.append(jnp.concatenate([hi, hi, lo], axis=0))
        qct_ref[p] = jnp.concatenate(halves, axis=1)
    w_rows = smq_ref[...].T[wi_lane:wi_lane + IDX_HEADS, :] * ((IDX_HEADS ** -0.5) * (IDX_DIM ** -0.5))

    q2t = (q_ref[...] * ((d ** -0.5) * LOG2E)).T.astype(BF16)
    zero_dq = jnp.zeros((d, tq), BF16)
    for p in range(nh // 2):
        top = jnp.concatenate([q2t[2 * p * d:(2 * p + 1) * d], zero_dq], axis=1)
        bot = jnp.concatenate([zero_dq, q2t[(2 * p + 1) * d:(2 * p + 2) * d]], axis=1)
        bd_ref[p] = jnp.concatenate([top, bot], axis=0)

    limit = i * tq + (lane_q // CHUNK + 1) * CHUNK

    rows_s = _iota((sub, tq), 0)

    def score_groups(gs, mm, masked):
        mn, mx = mm
        k0s = [pl.multiple_of(g * wide + sb * sub, sub) for g in gs for sb in range(wide // sub)]
        keys = [kc_ref[pl.ds(k0, sub), :] for k0 in k0s]
        accs = [jnp.zeros((sub, tq), F32) for _ in k0s]
        for p in range(IDX_HEADS // 2):
            rhs = qct_ref[p]
            for n, kk in enumerate(keys):
                s2 = jnp.dot(kk, rhs, preferred_element_type=F32)
                accs[n] = (accs[n] + jnp.maximum(s2[:, :tq], 0.0) * w_rows[2 * p:2 * p + 1, :]
                           + jnp.maximum(s2[:, tq:], 0.0) * w_rows[2 * p + 1:2 * p + 2, :])
        for k0, sct in zip(k0s, accs):
            if masked:
                adm = (k0 + rows_s) < limit
                mn = jnp.minimum(mn, col_fold(jnp.where(adm, sct, jnp.inf), jnp.minimum))
                sct = jnp.where(adm, sct, -jnp.inf)
            else:
                mn = jnp.minimum(mn, col_fold(sct, jnp.minimum))
            mx = jnp.maximum(mx, col_fold(sct, jnp.maximum))
            sc_ref[pl.ds(k0, sub), :] = sct
            scb_ref[pl.ds(k0, sub), :] = _floor_bf16(sct)
        return mn, mx

    def score_pair(j, mm):
        return score_groups((2 * j, 2 * j + 1), mm, False)

    n_full = n_wide - 1
    mm = lax.fori_loop(0, n_full // 2, score_pair,
                       (jnp.full((8, tq), jnp.inf, F32), jnp.full((8, tq), -jnp.inf, F32)))
    mm = lax.cond(n_full % 2 == 1, lambda c: score_groups((n_full - 1,), c, False), lambda c: c, mm)
    mn, mx = score_groups((n_wide - 1,), mm, True)

    n_pairs = (n_wide + 1) // 2

    @pl.when(n_wide % 2 == 1)
    def _():
        sc_ref[pl.ds(pl.multiple_of(n_wide * wide, wide), wide), :] = jnp.full((wide, tq), -jnp.inf, F32)
        scb_ref[pl.ds(pl.multiple_of(n_wide * wide, wide), wide), :] = jnp.full((wide, tq), -jnp.inf, BF16)
    rmin = jnp.min(mn, axis=0, keepdims=True)
    rmax = jnp.max(mx, axis=0, keepdims=True)

    def count(pred):
        def body(j, acc):
            for g in (2 * j, 2 * j + 1):
                acc = acc + col_fold(pred(sc_ref[pl.ds(pl.multiple_of(g * wide, wide), wide), :]))
            return acc
        return jnp.sum(lax.fori_loop(0, n_pairs, body, jnp.zeros((8, tq), F32)), axis=0, keepdims=True)

    def max_below(x):
        def body(j, acc):
            for g in (2 * j, 2 * j + 1):
                blk = sc_ref[pl.ds(pl.multiple_of(g * wide, wide), wide), :]
                acc = jnp.maximum(acc, col_fold(jnp.where(blk < x, blk, -jnp.inf), jnp.maximum))
            return acc
        return jnp.max(lax.fori_loop(0, n_pairs, body, jnp.full((8, tq), -jnp.inf, F32)), axis=0, keepdims=True)

    n_adm = limit.astype(F32)
    all_sel = n_adm <= ksel

    def bisect(c):
        lo, hi, c_lo = c
        mid = 0.5 * lo + 0.5 * hi
        cm = count(lambda blk: _ind(blk >= mid))
        ge = cm >= ksel
        return jnp.where(ge, mid, lo), jnp.where(ge, hi, mid), jnp.where(ge, cm, c_lo)

    def pending(c_lo, tied):
        return jnp.where(all_sel, 0.0, jnp.where(tied > 0.5, 0.0, _ind(c_lo != ksel)))

    def bisect_coarse(_, c):
        lo, hi, c_lo = c
        mid = _floor_bf16(0.5 * lo + 0.5 * hi).astype(F32)
        t_b = jnp.broadcast_to(mid, (16, tq)).astype(BF16)
        one_b = jnp.ones((16, tq), BF16)
        zero_b = jnp.zeros((16, tq), BF16)

        def body(j, acc):
            for g in (2 * j, 2 * j + 1):
                blk = scb_ref[pl.ds(pl.multiple_of(g * wide, wide), wide), :]
                ind = [jnp.where(blk[r * 16:(r + 1) * 16] >= t_b, one_b, zero_b) for r in range(wide // 16)]
                acc = acc + tree(ind, jnp.add).astype(F32)
            return acc

        acc = lax.fori_loop(0, n_pairs, body, jnp.zeros((16, tq), F32))
        cm = jnp.sum(acc, axis=0, keepdims=True)
        ge = cm >= ksel
        return jnp.where(ge, mid, lo), jnp.where(ge, hi, mid), jnp.where(ge, cm, c_lo)

    lo0 = _floor_bf16(rmin).astype(F32)
    hi0 = _floor_bf16(rmax + (jnp.abs(rmax) * (2.0 ** -6) + 1e-30)).astype(F32)
    state = lax.fori_loop(0, BISECT_COARSE, bisect_coarse, (lo0, hi0, n_adm))
    state = lax.fori_loop(0, BISECT_FIXED, lambda _, c: bisect(c), state)

    def round_cond(c):
        return jnp.max(pending(c[0][2], c[1])) > 0.5

    def round_body(c):
        st, tied, v, need = c

        def more_cond(s):
            return jnp.logical_and(s[0] < BISECT_EXTRA, jnp.max(pending(s[1][2], tied)) > 0.5)

        _, st = lax.while_loop(more_cond, lambda s: (s[0] + 1, bisect(s[1])), (jnp.int32(0), st))
        pend = pending(st[2], tied)

        def check(_):
            cand = max_below(st[1])
            c_ge = count(lambda blk: _ind(blk >= cand))
            c_gt = count(lambda blk: _ind(blk > cand))
            ok = jnp.where(pend > 0.5, _ind(c_ge >= ksel), 0.0)
            return (jnp.where(ok > 0.5, 1.0, tied), jnp.where(ok > 0.5, cand, v),
                    jnp.where(ok > 0.5, ksel - c_gt, need))

        tied, v, need = lax.cond(jnp.max(pend) > 0.5, check, lambda _: (tied, v, need), 0)
        return st, tied, v, need

    zeros1 = jnp.zeros((1, tq), F32)
    (lo_f, _, _), tied, v_tie, need = lax.while_loop(round_cond, round_body, (state, zeros1, zeros1, zeros1))
    vth = jnp.where(all_sel, F32_LOWEST, jnp.where(tied > 0.5, v_tie, lo_f))

    @pl.when(jnp.max(tied) > 0.5)
    def _():
        v_eq = jnp.where(tied > 0.5, v_tie, jnp.inf)
        incl = (_iota((tk, tk), 1) <= _iota((tk, tk), 0)).astype(BF16)

        def demote(g, seen):
            g0 = pl.multiple_of(g * wide, wide)
            xs = [sc_ref[pl.ds(g0 + pb * tk, tk), :] for pb in range(per_wide)]
            eqs = [_ind(x == v_eq) for x in xs]
            inblk = [jnp.dot(incl, e.astype(BF16), preferred_element_type=F32) for e in eqs]
            for pb in range(per_wide):
                rank = inblk[pb] + seen
                sc_ref[pl.ds(g0 + pb * tk, tk), :] = jnp.where(eqs[pb] * _ind(rank > need) > 0.5,
                                                               -jnp.inf, xs[pb])
                seen = seen + jnp.sum(col_fold(eqs[pb]), axis=0, keepdims=True)
            return seen

        lax.fori_loop(0, n_wide, demote, zeros1)

    g_near = jnp.maximum(i - 1, 0) // per_wide

    def logit_group(g, mx, near):
        out = list(mx)
        for sb in range(wide // sub):
            k0 = pl.multiple_of(g * wide + sb * sub, sub)
            sel = sc_ref[pl.ds(k0, sub), :] >= vth
            for p in range(nh // 2):
                pair = jnp.dot(k_ref[pl.ds(k0, sub), 2 * p * d:(2 * p + 2) * d], bd_ref[p],
                               preferred_element_type=F32)
                for hh in (2 * p, 2 * p + 1):
                    lm = pair[:, (hh - 2 * p) * tq:(hh - 2 * p + 1) * tq]
                    if near:
                        back = [jnp.clip(i - (g * per_wide + sb * (sub // tk) + pb), 0, 2)
                                for pb in range(sub // tk)]
                        lm = lm + jnp.concatenate([bias_ref[bk, hh] for bk in back], axis=0)
                    lm = jnp.where(sel, lm, NEG_BIG)
                    lg_ref[hh, pl.ds(k0, sub), :] = lm
                    out[hh] = jnp.maximum(out[hh], col_fold(lm, jnp.maximum))
        return tuple(out)

    mx = tuple(jnp.full((8, tq), NEG_BIG, F32) for _ in range(nh))
    def logit_pair(j, mx, near):
        return logit_group(2 * j + 1, logit_group(2 * j, mx, near), near)

    far_pairs = g_near // 2
    full_pairs = n_wide // 2
    odd = n_wide % 2 == 1
    mx = lax.fori_loop(0, far_pairs, functools.partial(logit_pair, near=False), mx)
    mx = lax.fori_loop(far_pairs, full_pairs, functools.partial(logit_pair, near=True), mx)
    mx = lax.cond(odd, lambda m: logit_group(n_wide - 1, m, True), lambda m: m, mx)
    m_q = [jnp.max(mx[hh], axis=0, keepdims=True) for hh in range(nh)]

    ones_rows = jnp.ones((8, wide), BF16)

    def pv_groups(gs, carry):
        ls, accs = list(carry[0]), list(carry[1])
        jobs = [(pl.multiple_of(g * wide, wide), hh) for g in gs for hh in range(nh)]
        ps = [jnp.exp2(lg_ref[hh, pl.ds(g0, wide), :] - m_q[hh]).astype(BF16) for g0, hh in jobs]
        outs = [jnp.dot(jnp.concatenate([vt_ref[hh * d:(hh + 1) * d, pl.ds(g0, wide)], ones_rows], axis=0),
                        p, preferred_element_type=F32) for (g0, hh), p in zip(jobs, ps)]
        for (_, hh), out in zip(jobs, outs):
            ls[hh] = ls[hh] + out[d:]
            accs[hh] = accs[hh] + out[:d]
        return tuple(ls), tuple(accs)

    acc = lax.fori_loop(0, full_pairs, lambda j, cr: pv_groups((2 * j, 2 * j + 1), cr),
                        (tuple(jnp.zeros((8, tq), F32) for _ in range(nh)),
                         tuple(jnp.zeros((d, tq), F32) for _ in range(nh))))
    ls, accs = lax.cond(odd, lambda cr: pv_groups((n_wide - 1,), cr), lambda cr: cr, acc)
    for hh in range(nh):
        o_ref[:, hh * d:(hh + 1) * d] = (accs[hh] / ls[hh][0:1]).T.astype(o_ref.dtype)


def _dsa(p32, p16, vt, bias_tiles, *, tq, cols):
    bsz, s, _ = p32.shape
    d = HEAD_DIM
    nh = N_HEADS
    wide = 4 * tq
    k_sel = min(TOPK_MAX, s // 4)
    w512 = nh * d
    kernel = functools.partial(_dsa_kernel, tq=tq, k_sel=k_sel, wi_lane=cols["wi_lane"], wide=wide)
    resident = dict(pipeline_mode=pl.Buffered(1))
    return pl.pallas_call(
        kernel,
        grid=(bsz, s // tq),
        in_specs=[pl.BlockSpec((None, tq, w512), lambda b, i: (b, i, cols["qi"] // nh)),
                  pl.BlockSpec((None, tq, d), lambda b, i: (b, i, cols["small"])),
                  pl.BlockSpec((None, tq, w512), lambda b, i: (b, i, cols["qb"] // nh)),
                  pl.BlockSpec((None, s, d), lambda b, i: (b, 0, cols["small"]), **resident),
                  pl.BlockSpec((None, s, w512), lambda b, i: (b, 0, cols["kb"] // nh), **resident),
                  pl.BlockSpec((w512, s), lambda b, i: (0, b), **resident),
                  pl.BlockSpec((3, nh, tq, tq), lambda b, i: (0, 0, 0, 0), **resident)],
        out_specs=pl.BlockSpec((None, tq, w512), lambda b, i: (b, i, 0)),
        out_shape=jax.ShapeDtypeStruct((bsz, s, w512), BF16),
        scratch_shapes=[pltpu.VMEM((s, tq), F32),
                        pltpu.VMEM((s, tq), BF16),
                        pltpu.VMEM((IDX_HEADS // 2, 3 * IDX_DIM, 2 * tq), BF16),
                        pltpu.VMEM((s, 3 * IDX_DIM), BF16),
                        pltpu.VMEM((nh // 2, 2 * d, 2 * tq), BF16),
                        pltpu.VMEM((nh, s, tq), F32)],
        compiler_params=pltpu.CompilerParams(
            dimension_semantics=("parallel", "arbitrary"), vmem_limit_bytes=VMEM_LIMIT),
        name="dsa",
    )(p32, p32, p32, p32, p16, vt, bias_tiles)


def _t5_bucket(rel):
    nb = REL_BUCKETS // 2
    max_exact = nb // 2
    ret = jnp.where(rel > 0, nb, 0)
    n = jnp.abs(rel)
    large = max_exact + (jnp.log(jnp.maximum(n, 1).astype(F32) / max_exact)
                         / math.log(REL_MAX_DIST / max_exact) * (nb - max_exact)).astype(jnp.int32)
    large = jnp.minimum(large, nb - 1)
    return ret + jnp.where(n < max_exact, n, large)


def _bias_tiles(rel_table, tq):
    assert tq >= REL_MAX_DIST
    t = jnp.arange(tq)
    back = jnp.arange(3)
    rel = (t[None, None, :] - back[:, None, None] * tq) - t[None, :, None]
    onehot = (_t5_bucket(rel)[..., None] == jnp.arange(REL_BUCKETS)).astype(F32)
    tiles = jnp.einsum("bqkn,nh->bhkq", onehot, rel_table.astype(F32),
                       precision=HIGHEST)
    return (tiles - tiles[2:3]) * LOG2E


def _even_layout(w_in):
    d = HEAD_DIM
    a_w = 2 * N_HEADS * d + N_HEADS * d
    offs = {}
    o = 0
    for name, w in (("qkv", a_w), ("z", N_HEADS * d), ("a", N_HEADS), ("b", N_HEADS),
                    ("qb", N_HEADS * d), ("kb", N_HEADS * d), ("vb", N_HEADS * d),
                    ("qi", IDX_HEADS * IDX_DIM), ("ki", IDX_DIM), ("wi", IDX_HEADS)):
        offs[name] = (o, o + w)
        o += w
    assert o == w_in.shape[1]
    sl = lambda n: w_in[:, offs[n][0]:offs[n][1]]
    small_w = IDX_DIM + 2 * N_HEADS + IDX_HEADS
    small_pad = -small_w % d
    zeros = lambda n: jnp.zeros((w_in.shape[0], n), w_in.dtype)
    w32 = jnp.concatenate([sl("qkv"), sl("z"), sl("qb"), sl("qi"),
                           sl("ki"), sl("a"), sl("b"), sl("wi"), zeros(small_pad)], axis=1)
    n32 = w32.shape[1]
    tn = n32 // 5
    assert tn * 5 == n32 and tn % d == 0
    w16 = jnp.concatenate([sl("kb"), zeros(tn - N_HEADS * d)], axis=1)
    nh = N_HEADS
    cols = dict(qa=0, ka=nh, va=2 * nh, za=3 * nh, qb=4 * nh, qi=5 * nh, small=6 * nh, kb=0,
                a_lane=IDX_DIM, b_lane=IDX_DIM + nh, wi_lane=IDX_DIM + 2 * nh, n32=n32, tn=tn)
    return jnp.concatenate([w32, w16], axis=1).astype(BF16), sl("vb").T.astype(BF16), cols


def kernel(x, norm_g, w_in_even, conv_w_even, a_log_even, dt_bias_even, a_norm_even, w_out_even,
           rel_bias, w_in_odd, lb_logits, d_norm_odd, w_out_odd, w_gate, w_up, w_down):
    bsz, s, d = x.shape
    t = bsz * s
    depth = norm_g.shape[0]
    nh = N_HEADS
    tq = Q_TILE
    lb_all = jnp.cumsum(jax.nn.softmax(lb_logits.astype(F32), axis=0), axis=0)
    lb_all = lb_all - lb_all[:1]
    odd_cols = dict(qc=0, kc=nh, vc=2 * nh, qd=0, fd=nh, id=2 * nh, gd=3 * nh)
    bias_tiles = _bias_tiles(rel_bias, tq)

    h = x.reshape(t, d)
    for l in range(depth):
        if l % 2 == 0:
            e = l // 2
            w_even, w_vt, cols = _even_layout(w_in_even[e])
            p32, p16, vt = _norm_matmul(h, norm_g[l, 0], w_even, tm=PROJ_TILE, tn=cols["tn"], n32=cols["n32"],
                                        w_t=w_vt)
            p32 = p32.reshape(bsz, s, -1)
            p16 = p16.reshape(bsz, s, -1)
            o_1 = _deltanet(p32, conv_w_even[e], a_log_even[e], dt_bias_even[e], a_norm_even[e],
                            ts=min(DELTANET_TILE, s), cols=cols)
            o_2 = _dsa(p32, p16, vt, bias_tiles, tq=tq, cols=cols)
            w_out = w_out_even[e]
        else:
            o = l // 2
            n16 = 3 * nh * HEAD_DIM
            w_odd = jnp.concatenate([w_in_odd[o][:, n16:], w_in_odd[o][:, :n16]], axis=1).astype(BF16)
            p32, p16 = _norm_matmul(h, norm_g[l, 0], w_odd, tm=PROJ_TILE, tn=ODD_COL_TILE, n32=w_odd.shape[1] - n16)
            p32 = p32.reshape(bsz, s, -1)
            p16 = p16.reshape(bsz, s, -1)
            o_1 = _stickbreak(p16, tq=tq, cols=odd_cols)
            o_2 = _hgrn2(p32, lb_all[l], d_norm_odd[o], ts=min(SEQ_TILE, s), cols=odd_cols)
            w_out = w_out_odd[o]
        h = _mix_ffn(o_1.reshape(t, -1), o_2.reshape(t, -1), w_out, h, norm_g[l, 1], norm_g[l, 2], norm_g[l, 3],
                     w_gate[l], w_up[l], w_down[l], tm=ROW_TILE, tf=FFN_TILE)
    return h.reshape(bsz, s, d)
```

```python
import functools
import math

import jax
import jax.numpy as jnp
from jax import lax
from jax.experimental import pallas as pl
from jax.experimental.pallas import tpu as pltpu

F32 = jnp.float32
BF16 = jnp.bfloat16
HIGHEST = lax.Precision.HIGHEST

CHUNK = 64
HEAD_DIM = 128
N_HEADS = 4
IDX_HEADS = 8
IDX_DIM = 64
TOPK_MAX = 256
CONV_WIDTH = 4
REL_BUCKETS = 32
REL_MAX_DIST = 128
EPS = 1e-6
NEG_BIG = -1e30
LOG2E = 1.4426950408889634
BISECT_COARSE = 10
BISECT_FIXED = 8
BISECT_EXTRA = 6
F32_LOWEST = -3.4028234663852886e38
EXP_ZERO_BELOW = -104.0
VMEM_LIMIT = 56 * 1024 * 1024

PROJ_TILE = 2048
ROW_TILE = 512
DELTANET_TILE = 1024
SEQ_TILE = 512
Q_TILE = 128
ODD_COL_TILE = 512
FFN_TILE = 2816


def _mm(a, b):
    return jnp.dot(a.astype(BF16), b.astype(BF16), preferred_element_type=F32)


def _mm_nt(a, b):
    return lax.dot_general(a.astype(BF16), b.astype(BF16), (((1,), (1,)), ((), ())),
                           preferred_element_type=F32)


def _mm_tn(a, b):
    return lax.dot_general(a.astype(BF16), b.astype(BF16), (((0,), (0,)), ((), ())),
                           preferred_element_type=F32)


def _split(x):
    hi = x.astype(BF16)
    return hi, (x - hi.astype(F32)).astype(BF16)


def _floor_bf16(x):
    bits = pltpu.bitcast(x, jnp.int32)
    down = jnp.where(bits >= 0, bits, bits + 0xFFFF) & jnp.int32(-65536)
    return pltpu.bitcast(down, F32).astype(BF16)


def _sigmoid(x):
    return 1.0 / (1.0 + jnp.exp(-x))


def _silu(x):
    return x * _sigmoid(x)


def _softplus(x):
    return jnp.maximum(x, 0.0) + jnp.log1p(jnp.exp(-jnp.abs(x)))


def _rms(x, g):
    return x * lax.rsqrt(jnp.mean(x * x, axis=-1, keepdims=True) + EPS) * g


def _iota(shape, dim):
    return lax.broadcasted_iota(jnp.int32, shape, dim)


def _ind(mask):
    return jnp.where(mask, 1.0, 0.0)


def _norm_matmul_kernel(x_ref, g_ref, w_ref, *rest, n_t, tiles32):
    if n_t:
        wt_ref, o32_ref, o16_ref, ot_ref, xn_ref = rest
    else:
        o32_ref, o16_ref, xn_ref = rest
    j = pl.program_id(1)

    @pl.when(j == 0)
    def _():
        xn_ref[...] = _rms(x_ref[...], g_ref[...]).astype(BF16)
        if n_t:
            ot_ref[...] = lax.dot_general(wt_ref[...], xn_ref[...], (((1,), (1,)), ((), ())),
                                          preferred_element_type=F32).astype(BF16)

    y = jnp.dot(xn_ref[...], w_ref[...], preferred_element_type=F32)

    @pl.when(j < tiles32)
    def _():
        o32_ref[...] = y

    @pl.when(j >= tiles32)
    def _():
        o16_ref[...] = y.astype(BF16)


def _norm_matmul(x, g, w, *, tm, tn, n32, w_t=None):
    t, d = x.shape
    n = w.shape[1]
    n_t = 0 if w_t is None else w_t.shape[0]
    tiles32 = n32 // tn
    assert tiles32 * tn == n32 and (n - n32) % tn == 0 and 0 < n32 < n
    in_specs = [pl.BlockSpec((tm, d), lambda i, j: (i, 0)),
                pl.BlockSpec((1, d), lambda i, j: (0, 0)),
                pl.BlockSpec((d, tn), lambda i, j: (0, j))]
    out_specs = [pl.BlockSpec((tm, tn), lambda i, j: (i, jnp.minimum(j, tiles32 - 1))),
                 pl.BlockSpec((tm, tn), lambda i, j: (i, jnp.maximum(j - tiles32, 0)))]
    out_shape = [jax.ShapeDtypeStruct((t, n32), F32), jax.ShapeDtypeStruct((t, n - n32), BF16)]
    args = [x, g.reshape(1, d), w]
    if n_t:
        in_specs.append(pl.BlockSpec((n_t, d), lambda i, j: (0, 0)))
        out_specs.append(pl.BlockSpec((n_t, tm), lambda i, j: (0, i)))
        out_shape.append(jax.ShapeDtypeStruct((n_t, t), BF16))
        args.append(w_t)
    return pl.pallas_call(
        functools.partial(_norm_matmul_kernel, n_t=n_t, tiles32=tiles32),
        grid=(t // tm, n // tn),
        in_specs=in_specs,
        out_specs=out_specs,
        out_shape=out_shape,
        scratch_shapes=[pltpu.VMEM((tm, d), BF16)],
        compiler_params=pltpu.CompilerParams(
            dimension_semantics=("parallel", "arbitrary"), vmem_limit_bytes=VMEM_LIMIT),
        name="norm_matmul",
    )(*args)


def _mix_ffn_kernel(ca_ref, cb_ref, wa_ref, wb_ref, h_ref, gmix_ref, gpre_ref, gpost_ref,
                    wg_ref, wu_ref, wd_ref, o_ref, h1_ref, xn_ref, acc_ref):
    f = pl.program_id(1)

    @pl.when(f == 0)
    def _():
        y = (jnp.dot(ca_ref[...], wa_ref[...], preferred_element_type=F32)
             + jnp.dot(cb_ref[...], wb_ref[...], preferred_element_type=F32))
        h1 = h_ref[...] + _rms(y, gmix_ref[...])
        h1_ref[...] = h1
        xn_ref[...] = _rms(h1, gpre_ref[...]).astype(BF16)
        acc_ref[...] = jnp.zeros_like(acc_ref)

    xn = xn_ref[...]
    gate = jnp.dot(xn, wg_ref[...], preferred_element_type=F32)
    up = jnp.dot(xn, wu_ref[...], preferred_element_type=F32)
    act = (_silu(gate) * up).astype(BF16)
    acc_ref[...] += jnp.dot(act, wd_ref[...], preferred_element_type=F32)

    @pl.when(f == pl.num_programs(1) - 1)
    def _():
        o_ref[...] = h1_ref[...] + _rms(acc_ref[...], gpost_ref[...])


def _mix_ffn(ca, cb, w_out, h, g_mix, g_pre, g_post, wg, wu, wd, *, tm, tf):
    t, d = h.shape
    ff = wg.shape[1]
    wa_n = ca.shape[1]
    wb_n = cb.shape[1]
    row = pl.BlockSpec((1, d), lambda i, f: (0, 0))
    once = dict(pipeline_mode=pl.Buffered(1)) if tf == ff else {}
    return pl.pallas_call(
        _mix_ffn_kernel,
        grid=(t // tm, ff // tf),
        in_specs=[pl.BlockSpec((tm, wa_n), lambda i, f: (i, 0)),
                  pl.BlockSpec((tm, wb_n), lambda i, f: (i, 0)),
                  pl.BlockSpec((wa_n, d), lambda i, f: (0, 0)),
                  pl.BlockSpec((wb_n, d), lambda i, f: (0, 0)),
                  pl.BlockSpec((tm, d), lambda i, f: (i, 0)),
                  row, row, row,
                  pl.BlockSpec((d, tf), lambda i, f: (0, f), **once),
                  pl.BlockSpec((d, tf), lambda i, f: (0, f), **once),
                  pl.BlockSpec((tf, d), lambda i, f: (f, 0), **once)],
        out_specs=pl.BlockSpec((tm, d), lambda i, f: (i, 0)),
        out_shape=jax.ShapeDtypeStruct((t, d), F32),
        scratch_shapes=[pltpu.VMEM((tm, d), F32), pltpu.VMEM((tm, d), BF16), pltpu.VMEM((tm, d), F32)],
        compiler_params=pltpu.CompilerParams(
            dimension_semantics=("parallel", "arbitrary"), vmem_limit_bytes=VMEM_LIMIT),
        name="mix_ffn",
    )(ca, cb, w_out[:wa_n].astype(BF16), w_out[wa_n:].astype(BF16), h,
      g_mix.reshape(1, d), g_pre.reshape(1, d), g_post.reshape(1, d),
      wg.astype(BF16), wu.astype(BF16), wd.astype(BF16))


def _deltanet_kernel(xq_ref, xk_ref, xv_ref, z_ref, sm_ref, cwq_ref, cwk_ref, cwv_ref,
                     alog_ref, dtb_ref, gn_ref, o_ref,
                     xpad_ref, q_ref, k_ref, v_ref, gb_ref, bb_ref, u_ref, w_ref, qk_ref, st_ref,
                     *, ts, a_col, b_col):
    s = pl.program_id(1)
    c = CHUNK
    d = HEAD_DIM
    nh = N_HEADS

    @pl.when(s == 0)
    def _():
        xpad_ref[:, 0:8, :] = jnp.zeros((3, 8, nh * d), F32)
        st_ref[...] = jnp.zeros_like(st_ref)

    @pl.when(s != 0)
    def _():
        xpad_ref[:, 0:8, :] = xpad_ref[:, ts:ts + 8, :]

    xpad_ref[0, 8:ts + 8, :] = xq_ref[...]
    xpad_ref[1, 8:ts + 8, :] = xk_ref[...]
    xpad_ref[2, 8:ts + 8, :] = xv_ref[...]

    def conv_silu(idx, cw_ref, hs):
        cw = cw_ref[:, hs]
        acc = xpad_ref[idx, 8 - (CONV_WIDTH - 1):8 - (CONV_WIDTH - 1) + ts, hs] * cw[0:1, :]
        for j in range(1, CONV_WIDTH):
            off = 8 - (CONV_WIDTH - 1) + j
            acc = acc + xpad_ref[idx, off:off + ts, hs] * cw[j:j + 1, :]
        return _silu(acc)

    def l2norm(t):
        return t * lax.rsqrt(jnp.sum(t * t, axis=-1, keepdims=True) + EPS)

    row = _iota((c, c), 0)
    col = _iota((c, c), 1)
    tri = (col <= row)
    strict = (col < row)
    tri_f = tri.astype(F32)
    upper_f = (row <= col).astype(F32)
    eye = (row == col).astype(F32)
    gnorm = gn_ref[...]
    chunks = range(ts // c)
    rs = [slice(ci * c, (ci + 1) * c) for ci in chunks]
    tri2 = jnp.concatenate([tri_f, tri_f], axis=1).astype(BF16)
    ones2 = jnp.ones((c, 2 * c), BF16)

    def cum2(lhs2, x):
        hi, lo = _split(x)
        return jnp.dot(lhs2, jnp.concatenate([hi, lo], axis=0), preferred_element_type=F32)

    for hh in range(nh):
        hs = slice(hh * d, (hh + 1) * d)
        q_ref[:, hs] = l2norm(conv_silu(0, cwq_ref, hs)) * (d ** -0.5)
        k_ref[:, hs] = l2norm(conv_silu(1, cwk_ref, hs))
        v_ref[:, hs] = conv_silu(2, cwv_ref, hs)

        a_raw = sm_ref[:, a_col + hh:a_col + hh + 1]
        b_raw = sm_ref[:, b_col + hh:b_col + hh + 1]
        g = -jnp.exp(alog_ref[:, hh:hh + 1]) * _softplus(a_raw + dtb_ref[:, hh:hh + 1])
        gb_ref[:, hs] = jnp.broadcast_to(g, (ts, d))
        bb_ref[:, hs] = jnp.broadcast_to(_sigmoid(b_raw), (ts, d))

        q = [q_ref[r, hs] for r in rs]
        k = [k_ref[r, hs] for r in rs]
        beta = [bb_ref[r, hs] for r in rs]
        gb = [gb_ref[r, hs] for r in rs]
        gc = [cum2(tri2, x) for x in gb]
        gc_row = [cum2(ones2, x[:, :c] * upper_f) for x in gb]
        decay = [jnp.where(tri, jnp.exp(jnp.minimum(a[:, :c] - b, 0.0)), 0.0) for a, b in zip(gc, gc_row)]
        kk = [_mm_nt(x, x) for x in k]
        n = [-jnp.where(strict, b[:, :c] * x * dc, 0.0) for b, x, dc in zip(beta, kk, decay)]
        inv = [eye + x for x in n]
        for step in range(5):
            nb = [x.astype(BF16) for x in n]
            n = [jnp.dot(x, x, preferred_element_type=F32) for x in nb]
            inv = [iv + _mm(iv, x) for iv, x in zip(inv, n)]
        egc = [jnp.exp(x) for x in gc]
        gl = [x[c - 1:c, :] for x in gc]
        inv_l = [x.astype(BF16) for x in inv]
        u = [_mm(a, v_ref[r, hs] * b) for a, r, b in zip(inv_l, rs, beta)]
        w = [_mm(a, x * (b * e)) for a, x, b, e in zip(inv_l, k, beta, egc)]
        qk = [_mm_nt(a, b) * dc for a, b, dc in zip(q, k, decay)]
        for ci in chunks:
            r = rs[ci]
            u_ref[r, hs] = u[ci]
            w_ref[r, hs] = w[ci]
            qk_ref[hh, r, :] = qk[ci]
            q_ref[r, hs] = q[ci] * egc[ci]
            k_ref[r, hs] = k[ci] * jnp.exp(gl[ci] - gc[ci])
            gb_ref[r, hs] = jnp.broadcast_to(jnp.exp(gl[ci]), (c, d))

    hss = [slice(hh * d, (hh + 1) * d) for hh in range(nh)]

    def chunk_step(ci, st):
        r0 = pl.multiple_of(ci * c, c)
        rows = pl.ds(r0, c)
        w_st = [_mm(w_ref[rows, hs], s_) for hs, s_ in zip(hss, st)]
        q_st = [_mm(q_ref[rows, hs], s_) for hs, s_ in zip(hss, st)]
        v_new = [u_ref[rows, hs] - x for hs, x in zip(hss, w_st)]
        o = [a + _mm(qk_ref[hh, rows, :], v) for hh, (a, v) in enumerate(zip(q_st, v_new))]
        kv = [_mm_tn(k_ref[rows, hs], v) for hs, v in zip(hss, v_new)]
        for hh, hs in enumerate(hss):
            o_ref[rows, hs] = (_rms(o[hh], gnorm) * _silu(z_ref[rows, hs])).astype(o_ref.dtype)
        return [s_ * gb_ref[pl.ds(r0, 1), hs] + x for s_, hs, x in zip(st, hss, kv)]

    per_step = math.gcd(ts // c, 8)

    def chunk_group(j, carry):
        st = [st_ref[hh] for hh in range(nh)]
        for n_ in range(per_step):
            st = chunk_step(per_step * j + n_, st)
        for hh in range(nh):
            st_ref[hh] = st[hh]
        return carry

    lax.fori_loop(0, ts // (per_step * c), chunk_group, 0)


def _deltanet(p32, conv_w, a_log, dt_bias, a_norm_g, *, ts, cols):
    bsz, s, _ = p32.shape
    d = HEAD_DIM
    nh = N_HEADS
    w = nh * d
    pad = lambda t: jnp.pad(t.astype(F32), (0, d - t.shape[0])).reshape(1, d)
    kernel = functools.partial(_deltanet_kernel, ts=ts, a_col=cols["a_lane"], b_col=cols["b_lane"])
    tile = lambda name: pl.BlockSpec((None, ts, w), lambda b, i: (b, i, cols[name] // nh))
    conv = lambda k: pl.BlockSpec((CONV_WIDTH, w), lambda b, i: (0, k))
    row = pl.BlockSpec((1, d), lambda b, i: (0, 0))
    return pl.pallas_call(
        kernel,
        grid=(bsz, s // ts),
        in_specs=[tile("qa"), tile("ka"), tile("va"), tile("za"),
                  pl.BlockSpec((None, ts, d), lambda b, i: (b, i, cols["small"])),
                  conv(0), conv(1), conv(2), row, row, row],
        out_specs=pl.BlockSpec((None, ts, w), lambda b, i: (b, i, 0)),
        out_shape=jax.ShapeDtypeStruct((bsz, s, w), BF16),
        scratch_shapes=[pltpu.VMEM((3, ts + 8, w), F32)]
        + [pltpu.VMEM((ts, w), F32) for _ in range(7)]
        + [pltpu.VMEM((nh, ts, CHUNK), F32), pltpu.VMEM((nh, d, d), F32)],
        compiler_params=pltpu.CompilerParams(
            dimension_semantics=("parallel", "arbitrary"), vmem_limit_bytes=VMEM_LIMIT),
        name="deltanet",
    )(p32, p32, p32, p32, p32, conv_w.astype(F32), conv_w.astype(F32), conv_w.astype(F32),
      pad(a_log), pad(dt_bias), a_norm_g.astype(F32).reshape(1, d))


def _hgrn2_kernel(q_ref, f_ref, i_ref, gate_ref, lb_ref, gn_ref, o_ref,
                  qs_ref, ks_ref, gc_ref, st_ref, *, ts):
    s = pl.program_id(1)
    c = CHUNK
    d = HEAD_DIM
    nh = N_HEADS
    SUB = 16

    @pl.when(s == 0)
    def _():
        st_ref[...] = jnp.zeros_like(st_ref)

    lb = lb_ref[...]
    f_raw = f_ref[...]
    log_sig = jnp.minimum(f_raw, 0.0) - jnp.log1p(jnp.exp(-jnp.abs(f_raw)))
    la = jnp.log(lb)
    lbb = jnp.log1p(-lb) + log_sig
    log_f = jnp.maximum(la, lbb) + jnp.log1p(jnp.exp(-jnp.abs(la - lbb)))
    qs_ref[...] = _silu(q_ref[...])
    ks_ref[...] = (1.0 - lb) * _sigmoid(-f_raw)

    row = _iota((c, c), 0)
    col = _iota((c, c), 1)
    tri_f = (col <= row).astype(F32)
    ones_dd = jnp.ones((d, d), BF16)
    rows_8d = _iota((8, d), 0)
    gnorm = gn_ref[...]

    tri2 = jnp.concatenate([tri_f, tri_f], axis=1).astype(BF16)
    for ci in range(ts // c):
        hi, lo = _split(log_f[ci * c:(ci + 1) * c, :])
        gc_ref[ci * c:(ci + 1) * c, :] = jnp.dot(tri2, jnp.concatenate([hi, lo], axis=0),
                                                 preferred_element_type=F32)

    blocks = [(sb * SUB, (sb + 1) * SUB) for sb in range(c // SUB)]

    def chunk_loop(ci, carry):
        r0 = pl.multiple_of(ci * c, c)
        rows = pl.ds(r0, c)
        hss = [slice(hh * d, (hh + 1) * d) for hh in range(nh)]
        q = [qs_ref[rows, hs] for hs in hss]
        k = [ks_ref[rows, hs] for hs in hss]
        v = [i_ref[rows, hs] for hs in hss]
        gc = [gc_ref[rows, hs] for hs in hss]

        def near_products(q, k, gc):
            prods = []
            for top, end in blocks:
                for j in range(top, end):
                    lo = (j // 8) * 8
                    e = jnp.exp2(gc[lo:end, :] - gc[j:j + 1, :])
                    if j % 8:
                        head = jnp.where(rows_8d >= j - lo, e[:8], 0.0)
                        e = jnp.concatenate([head, e[8:]], axis=0) if lo + 8 < end else head
                    prods.append(q[lo:end, :] * k[j:j + 1, :] * e)
            return jnp.concatenate(prods, axis=0).astype(BF16)

        def far_operands(q, k, gc):
            out = []
            for top, end in blocks[1:]:
                g_b = gc[top - 1:top, :]
                out.append((q[top:end, :] * jnp.exp(gc[top:end, :] - g_b),
                            k[:top, :] * jnp.exp(jnp.minimum(g_b - gc[:top, :], 0.0))))
            return out

        near = [near_products(a, b, g * LOG2E) for a, b, g in zip(q, k, gc)]
        far_ops = [far_operands(*x) for x in zip(q, k, gc)]
        st = [st_ref[hh] for hh in range(nh)]
        gl = [x[c - 1:c, :] for x in gc]
        sums = [jnp.dot(x, ones_dd, preferred_element_type=F32) for x in near]
        qk_far = [[_mm_nt(qe, ke) for qe, ke in ops] for ops in far_ops]
        far = [[_mm(a, vv[:top, :]) for a, (top, _) in zip(qs, blocks[1:])] for qs, vv in zip(qk_far, v)]
        o_st = [_mm_nt(a * jnp.exp(g), s_) for a, g, s_ in zip(q, gc, st)]
        kv = [_mm_tn(vv, kk * jnp.exp(g_l - g)) for vv, kk, g_l, g in zip(v, k, gl, gc)]

        for hh, hs in enumerate(hss):
            groups = [jnp.zeros((8, d), F32) for _ in range(c // 8)]
            at = 0
            for top, end in blocks:
                for j in range(top, end):
                    v_j = v[hh][j:j + 1, :]
                    for g in range(j // 8, end // 8):
                        groups[g] = groups[g] + sums[hh][at:at + 8, :] * v_j
                        at += 8
            for f, (top, end) in zip(far[hh], blocks[1:]):
                for g in range(top // 8, end // 8):
                    groups[g] = groups[g] + f[(g * 8 - top):(g * 8 - top + 8), :]
            o = jnp.concatenate(groups, axis=0) + o_st[hh]
            st_ref[hh] = st[hh] * jnp.exp(gl[hh]) + kv[hh]
            o_ref[rows, hs] = (_rms(o, gnorm) * _silu(gate_ref[rows, hs])).astype(o_ref.dtype)
        return carry

    per_step = math.gcd(ts // c, 4)

    def chunk_group(j, carry):
        for n_ in range(per_step):
            chunk_loop(per_step * j + n_, carry)
        return carry

    lax.fori_loop(0, ts // (per_step * c), chunk_group, 0)


def _hgrn2(p32, lb, d_norm_g, *, ts, cols):
    bsz, s, _ = p32.shape
    d = HEAD_DIM
    nh = N_HEADS
    w = nh * d
    kernel = functools.partial(_hgrn2_kernel, ts=ts)
    tile = lambda name: pl.BlockSpec((None, ts, w), lambda b, i: (b, i, cols[name] // nh))
    return pl.pallas_call(
        kernel,
        grid=(bsz, s // ts),
        in_specs=[tile("qd"), tile("fd"), tile("id"), tile("gd"),
                  pl.BlockSpec((1, w), lambda b, i: (0, 0)),
                  pl.BlockSpec((1, d), lambda b, i: (0, 0))],
        out_specs=pl.BlockSpec((None, ts, w), lambda b, i: (b, i, 0)),
        out_shape=jax.ShapeDtypeStruct((bsz, s, w), BF16),
        scratch_shapes=[pltpu.VMEM((ts, w), F32), pltpu.VMEM((ts, w), F32),
                        pltpu.VMEM((ts, w), F32), pltpu.VMEM((nh, d, d), F32)],
        compiler_params=pltpu.CompilerParams(
            dimension_semantics=("parallel", "arbitrary"), vmem_limit_bytes=VMEM_LIMIT),
        name="hgrn2",
    )(p32, p32, p32, p32, lb.astype(F32).reshape(1, w), d_norm_g.astype(F32).reshape(1, d))


def _stickbreak_kernel(q_ref, k_ref, v_ref, o_ref, acc_ref, *, tq):
    i = pl.program_id(1)
    d = HEAD_DIM
    nh = N_HEADS
    row = _iota((tq, tq), 0)
    col = _iota((tq, tq), 1)
    causal = col < row
    later = (row > col).astype(BF16)
    later2 = jnp.concatenate([later, later], axis=0)

    heads = [slice(hh * d, (hh + 1) * d) for hh in range(nh)]

    def scores(blocks):
        jobs = [(j, dg, hs) for j, dg in blocks for hs in heads]
        z = [_mm_nt(q_ref[:, hs], k_ref[pl.ds(pl.multiple_of(j * tq, tq), tq), hs]) * (d ** -0.5)
             for j, _, hs in jobs]
        sp = [_softplus(x) for x in z]
        l1m = [jnp.where(causal, -x, 0.0) if dg else -x for x, (_, dg, _) in zip(sp, jobs)]
        rest = [jnp.dot(jnp.concatenate(_split(x), axis=1), later2, preferred_element_type=F32)
                for x in l1m]
        out = [((a - b) + r, l) for a, b, r, l in zip(z, sp, rest, l1m)]
        return [out[b * nh:(b + 1) * nh] for b in range(len(blocks))]

    def block(j, carries):
        (sc,) = scores([(j, False)])
        ps = [jnp.exp(logw + c) for (logw, _), c in zip(sc, carries)]
        pv = [_mm(p, v_ref[pl.ds(pl.multiple_of(j * tq, tq), tq), hs]) for p, hs in zip(ps, heads)]
        for hs, x in zip(heads, pv):
            acc_ref[:, hs] += x
        return tuple(c + jnp.sum(l1m, axis=-1, keepdims=True) for (_, l1m), c in zip(sc, carries))

    j1 = jnp.maximum(i - 1, 0)
    j2 = jnp.maximum(i - 2, 0)
    live1 = jnp.where(i > 0, 1.0, 0.0)
    live2 = jnp.where(i > 1, 1.0, 0.0)
    s0, s1, s2 = scores([(i, True), (j1, False), (j2, False)])
    carries = []
    for hh, hs in enumerate(heads):
        c0 = jnp.sum(s0[hh][1], axis=-1, keepdims=True)
        c1 = c0 + jnp.sum(s1[hh][1], axis=-1, keepdims=True)
        p0 = jnp.where(causal, jnp.exp(s0[hh][0]), 0.0)
        p1 = jnp.exp(s1[hh][0] + c0) * live1
        p2 = jnp.exp(s2[hh][0] + c1) * live2
        acc_ref[:, hs] = (_mm(p0, v_ref[pl.ds(pl.multiple_of(i * tq, tq), tq), hs])
                          + _mm(p1, v_ref[pl.ds(pl.multiple_of(j1 * tq, tq), tq), hs])
                          + _mm(p2, v_ref[pl.ds(pl.multiple_of(j2 * tq, tq), tq), hs]))
        carries.append(c1 + jnp.sum(s2[hh][1], axis=-1, keepdims=True))
    carries = tuple(carries)

    def cond(c):
        worst = functools.reduce(jnp.maximum, c[1])
        return jnp.logical_and(c[0] >= 0, jnp.max(worst) >= EXP_ZERO_BELOW)

    def body(c):
        return c[0] - 1, block(c[0], c[1])

    lax.while_loop(cond, body, (i - 3, carries))
    o_ref[...] = acc_ref[...].astype(o_ref.dtype)


def _stickbreak(p16, *, tq, cols):
    bsz, s, _ = p16.shape
    nh = N_HEADS
    w = nh * HEAD_DIM
    kernel = functools.partial(_stickbreak_kernel, tq=tq)
    resident = dict(pipeline_mode=pl.Buffered(1))
    return pl.pallas_call(
        kernel,
        grid=(bsz, s // tq),
        in_specs=[pl.BlockSpec((None, tq, w), lambda b, i: (b, i, cols["qc"] // nh)),
                  pl.BlockSpec((None, s, w), lambda b, i: (b, 0, cols["kc"] // nh), **resident),
                  pl.BlockSpec((None, s, w), lambda b, i: (b, 0, cols["vc"] // nh), **resident)],
        out_specs=pl.BlockSpec((None, tq, w), lambda b, i: (b, i, 0)),
        out_shape=jax.ShapeDtypeStruct((bsz, s, w), BF16),
        scratch_shapes=[pltpu.VMEM((tq, w), F32)],
        compiler_params=pltpu.CompilerParams(
            dimension_semantics=("parallel", "arbitrary"), vmem_limit_bytes=VMEM_LIMIT),
        name="stickbreak",
    )(p16, p16, p16)


def _dsa_kernel(qi_ref, smq_ref, q_ref, sm_ref, k_ref, vt_ref, bias_ref, o_ref,
                sc_ref, scb_ref, qct_ref, kc_ref, bd_ref, lg_ref, *, tq, k_sel, wi_lane, wide):
    i = pl.program_id(1)
    tk = tq
    d = HEAD_DIM
    nh = N_HEADS
    ksel = float(k_sel)
    per_wide = wide // tk
    n_wide = (i + per_wide) // per_wide
    sub = 2 * tk
    lane_q = _iota((1, tq), 1)

    def tree(parts, op):
        while len(parts) > 1:
            parts = [op(parts[j], parts[j + 1]) if j + 1 < len(parts) else parts[j]
                     for j in range(0, len(parts), 2)]
        return parts[0]

    def col_fold(x, op=jnp.add, rows=8):
        return tree([x[r * rows:(r + 1) * rows] for r in range(x.shape[0] // rows)], op)

    @pl.when(i == 0)
    def _():
        def prep(g, carry):
            g0 = pl.multiple_of(g * wide, wide)
            hi, lo = _split(sm_ref[pl.ds(g0, wide), :][:, :IDX_DIM])
            kc_ref[pl.ds(g0, wide), :] = jnp.concatenate([hi, lo, hi], axis=1)
            return carry
        lax.fori_loop(0, sm_ref.shape[0] // wide, prep, 0)

    qit = qi_ref[...].T
    for p in range(IDX_HEADS // 2):
        halves = []
        for hh in (2 * p, 2 * p + 1):
            hi, lo = _split(qit[hh * IDX_DIM:(hh + 1) * IDX_DIM, :])
            halves.append(jnp.concatenate([hi, hi, lo], axis=0))
        qct_ref[p] = jnp.concatenate(halves, axis=1)
    w_rows = smq_ref[...].T[wi_lane:wi_lane + IDX_HEADS, :] * ((IDX_HEADS ** -0.5) * (IDX_DIM ** -0.5))

    q2t = (q_ref[...] * ((d ** -0.5) * LOG2E)).T.astype(BF16)
    zero_dq = jnp.zeros((d, tq), BF16)
    for p in range(nh // 2):
        top = jnp.concatenate([q2t[2 * p * d:(2 * p + 1) * d], zero_dq], axis=1)
        bot = jnp.concatenate([zero_dq, q2t[(2 * p + 1) * d:(2 * p + 2) * d]], axis=1)
        bd_ref[p] = jnp.concatenate([top, bot], axis=0)

    limit = i * tq + (lane_q // CHUNK + 1) * CHUNK

    rows_s = _iota((sub, tq), 0)

    def score_groups(gs, mm, masked):
        mn, mx = mm
        k0s = [pl.multiple_of(g * wide + sb * sub, sub) for g in gs for sb in range(wide // sub)]
        keys = [kc_ref[pl.ds(k0, sub), :] for k0 in k0s]
        accs = [jnp.zeros((sub, tq), F32) for _ in k0s]
        for p in range(IDX_HEADS // 2):
            rhs = qct_ref[p]
            for n, kk in enumerate(keys):
                s2 = jnp.dot(kk, rhs, preferred_element_type=F32)
                accs[n] = (accs[n] + jnp.maximum(s2[:, :tq], 0.0) * w_rows[2 * p:2 * p + 1, :]
                           + jnp.maximum(s2[:, tq:], 0.0) * w_rows[2 * p + 1:2 * p + 2, :])
        for k0, sct in zip(k0s, accs):
            if masked:
                adm = (k0 + rows_s) < limit
                mn = jnp.minimum(mn, col_fold(jnp.where(adm, sct, jnp.inf), jnp.minimum))
                sct = jnp.where(adm, sct, -jnp.inf)
            else:
                mn = jnp.minimum(mn, col_fold(sct, jnp.minimum))
            mx = jnp.maximum(mx, col_fold(sct, jnp.maximum))
            sc_ref[pl.ds(k0, sub), :] = sct
            scb_ref[pl.ds(k0, sub), :] = _floor_bf16(sct)
        return mn, mx

    def score_pair(j, mm):
        return score_groups((2 * j, 2 * j + 1), mm, False)

    n_full = n_wide - 1
    mm = lax.fori_loop(0, n_full // 2, score_pair,
                       (jnp.full((8, tq), jnp.inf, F32), jnp.full((8, tq), -jnp.inf, F32)))
    mm = lax.cond(n_full % 2 == 1, lambda c: score_groups((n_full - 1,), c, False), lambda c: c, mm)
    mn, mx = score_groups((n_wide - 1,), mm, True)

    n_pairs = (n_wide + 1) // 2

    @pl.when(n_wide % 2 == 1)
    def _():
        sc_ref[pl.ds(pl.multiple_of(n_wide * wide, wide), wide), :] = jnp.full((wide, tq), -jnp.inf, F32)
        scb_ref[pl.ds(pl.multiple_of(n_wide * wide, wide), wide), :] = jnp.full((wide, tq), -jnp.inf, BF16)
    rmin = jnp.min(mn, axis=0, keepdims=True)
    rmax = jnp.max(mx, axis=0, keepdims=True)

    def count(pred):
        def body(j, acc):
            for g in (2 * j, 2 * j + 1):
                acc = acc + col_fold(pred(sc_ref[pl.ds(pl.multiple_of(g * wide, wide), wide), :]))
            return acc
        return jnp.sum(lax.fori_loop(0, n_pairs, body, jnp.zeros((8, tq), F32)), axis=0, keepdims=True)

    def max_below(x):
        def body(j, acc):
            for g in (2 * j, 2 * j + 1):
                blk = sc_ref[pl.ds(pl.multiple_of(g * wide, wide), wide), :]
                acc = jnp.maximum(acc, col_fold(jnp.where(blk < x, blk, -jnp.inf), jnp.maximum))
            return acc
        return jnp.max(lax.fori_loop(0, n_pairs, body, jnp.full((8, tq), -jnp.inf, F32)), axis=0, keepdims=True)

    n_adm = limit.astype(F32)
    all_sel = n_adm <= ksel

    def bisect(c):
        lo, hi, c_lo = c
        mid = 0.5 * lo + 0.5 * hi
        cm = count(lambda blk: _ind(blk >= mid))
        ge = cm >= ksel
        return jnp.where(ge, mid, lo), jnp.where(ge, hi, mid), jnp.where(ge, cm, c_lo)

    def pending(c_lo, tied):
        return jnp.where(all_sel, 0.0, jnp.where(tied > 0.5, 0.0, _ind(c_lo != ksel)))

    def bisect_coarse(_, c):
        lo, hi, c_lo = c
        mid = _floor_bf16(0.5 * lo + 0.5 * hi).astype(F32)
        t_b = jnp.broadcast_to(mid, (16, tq)).astype(BF16)
        one_b = jnp.ones((16, tq), BF16)
        zero_b = jnp.zeros((16, tq), BF16)

        def body(j, acc):
            for g in (2 * j, 2 * j + 1):
                blk = scb_ref[pl.ds(pl.multiple_of(g * wide, wide), wide), :]
                ind = [jnp.where(blk[r * 16:(r + 1) * 16] >= t_b, one_b, zero_b) for r in range(wide // 16)]
                acc = acc + tree(ind, jnp.add).astype(F32)
            return acc

        acc = lax.fori_loop(0, n_pairs, body, jnp.zeros((16, tq), F32))
        cm = jnp.sum(acc, axis=0, keepdims=True)
        ge = cm >= ksel
        return jnp.where(ge, mid, lo), jnp.where(ge, hi, mid), jnp.where(ge, cm, c_lo)

    lo0 = _floor_bf16(rmin).astype(F32)
    hi0 = _floor_bf16(rmax + (jnp.abs(rmax) * (2.0 ** -6) + 1e-30)).astype(F32)
    state = lax.fori_loop(0, BISECT_COARSE, bisect_coarse, (lo0, hi0, n_adm))
    state = lax.fori_loop(0, BISECT_FIXED, lambda _, c: bisect(c), state)

    def round_cond(c):
        return jnp.max(pending(c[0][2], c[1])) > 0.5

    def round_body(c):
        st, tied, v, need = c

        def more_cond(s):
            return jnp.logical_and(s[0] < BISECT_EXTRA, jnp.max(pending(s[1][2], tied)) > 0.5)

        _, st = lax.while_loop(more_cond, lambda s: (s[0] + 1, bisect(s[1])), (jnp.int32(0), st))
        pend = pending(st[2], tied)

        def check(_):
            cand = max_below(st[1])
            c_ge = count(lambda blk: _ind(blk >= cand))
            c_gt = count(lambda blk: _ind(blk > cand))
            ok = jnp.where(pend > 0.5, _ind(c_ge >= ksel), 0.0)
            return (jnp.where(ok > 0.5, 1.0, tied), jnp.where(ok > 0.5, cand, v),
                    jnp.where(ok > 0.5, ksel - c_gt, need))

        tied, v, need = lax.cond(jnp.max(pend) > 0.5, check, lambda _: (tied, v, need), 0)
        return st, tied, v, need

    zeros1 = jnp.zeros((1, tq), F32)
    (lo_f, _, _), tied, v_tie, need = lax.while_loop(round_cond, round_body, (state, zeros1, zeros1, zeros1))
    vth = jnp.where(all_sel, F32_LOWEST, jnp.where(tied > 0.5, v_tie, lo_f))

    @pl.when(jnp.max(tied) > 0.5)
    def _():
        v_eq = jnp.where(tied > 0.5, v_tie, jnp.inf)
        incl = (_iota((tk, tk), 1) <= _iota((tk, tk), 0)).astype(BF16)

        def demote(g, seen):
            g0 = pl.multiple_of(g * wide, wide)
            xs = [sc_ref[pl.ds(g0 + pb * tk, tk), :] for pb in range(per_wide)]
            eqs = [_ind(x == v_eq) for x in xs]
            inblk = [jnp.dot(incl, e.astype(BF16), preferred_element_type=F32) for e in eqs]
            for pb in range(per_wide):
                rank = inblk[pb] + seen
                sc_ref[pl.ds(g0 + pb * tk, tk), :] = jnp.where(eqs[pb] * _ind(rank > need) > 0.5,
                                                               -jnp.inf, xs[pb])
                seen = seen + jnp.sum(col_fold(eqs[pb]), axis=0, keepdims=True)
            return seen

        lax.fori_loop(0, n_wide, demote, zeros1)

    g_near = jnp.maximum(i - 1, 0) // per_wide

    def logit_group(g, mx, near):
        out = list(mx)
        for sb in range(wide // sub):
            k0 = pl.multiple_of(g * wide + sb * sub, sub)
            sel = sc_ref[pl.ds(k0, sub), :] >= vth
            for p in range(nh // 2):
                pair = jnp.dot(k_ref[pl.ds(k0, sub), 2 * p * d:(2 * p + 2) * d], bd_ref[p],
                               preferred_element_type=F32)
                for hh in (2 * p, 2 * p + 1):
                    lm = pair[:, (hh - 2 * p) * tq:(hh - 2 * p + 1) * tq]
                    if near:
                        back = [jnp.clip(i - (g * per_wide + sb * (sub // tk) + pb), 0, 2)
                                for pb in range(sub // tk)]
                        lm = lm + jnp.concatenate([bias_ref[bk, hh] for bk in back], axis=0)
                    lm = jnp.where(sel, lm, NEG_BIG)
                    lg_ref[hh, pl.ds(k0, sub), :] = lm
                    out[hh] = jnp.maximum(out[hh], col_fold(lm, jnp.maximum))
        return tuple(out)

    mx = tuple(jnp.full((8, tq), NEG_BIG, F32) for _ in range(nh))
    def logit_pair(j, mx, near):
        return logit_group(2 * j + 1, logit_group(2 * j, mx, near), near)

    far_pairs = g_near // 2
    full_pairs = n_wide // 2
    odd = n_wide % 2 == 1
    mx = lax.fori_loop(0, far_pairs, functools.partial(logit_pair, near=False), mx)
    mx = lax.fori_loop(far_pairs, full_pairs, functools.partial(logit_pair, near=True), mx)
    mx = lax.cond(odd, lambda m: logit_group(n_wide - 1, m, True), lambda m: m, mx)
    m_q = [jnp.max(mx[hh], axis=0, keepdims=True) for hh in range(nh)]

    ones_rows = jnp.ones((8, wide), BF16)

    def pv_groups(gs, carry):
        ls, accs = list(carry[0]), list(carry[1])
        jobs = [(pl.multiple_of(g * wide, wide), hh) for g in gs for hh in range(nh)]
        ps = [jnp.exp2(lg_ref[hh, pl.ds(g0, wide), :] - m_q[hh]).astype(BF16) for g0, hh in jobs]
        outs = [jnp.dot(jnp.concatenate([vt_ref[hh * d:(hh + 1) * d, pl.ds(g0, wide)], ones_rows], axis=0),
                        p, preferred_element_type=F32) for (g0, hh), p in zip(jobs, ps)]
        for (_, hh), out in zip(jobs, outs):
            ls[hh] = ls[hh] + out[d:]
            accs[hh] = accs[hh] + out[:d]
        return tuple(ls), tuple(accs)

    acc = lax.fori_loop(0, full_pairs, lambda j, cr: pv_groups((2 * j, 2 * j + 1), cr),
                        (tuple(jnp.zeros((8, tq), F32) for _ in range(nh)),
                         tuple(jnp.zeros((d, tq), F32) for _ in range(nh))))
    ls, accs = lax.cond(odd, lambda cr: pv_groups((n_wide - 1,), cr), lambda cr: cr, acc)
    for hh in range(nh):
        o_ref[:, hh * d:(hh + 1) * d] = (accs[hh] / ls[hh][0:1]).T.astype(o_ref.dtype)


def _dsa(p32, p16, vt, bias_tiles, *, tq, cols):
    bsz, s, _ = p32.shape
    d = HEAD_DIM
    nh = N_HEADS
    wide = 4 * tq
    k_sel = min(TOPK_MAX, s // 4)
    w512 = nh * d
    kernel = functools.partial(_dsa_kernel, tq=tq, k_sel=k_sel, wi_lane=cols["wi_lane"], wide=wide)
    resident = dict(pipeline_mode=pl.Buffered(1))
    return pl.pallas_call(
        kernel,
        grid=(bsz, s // tq),
        in_specs=[pl.BlockSpec((None, tq, w512), lambda b, i: (b, i, cols["qi"] // nh)),
                  pl.BlockSpec((None, tq, d), lambda b, i: (b, i, cols["small"])),
                  pl.BlockSpec((None, tq, w512), lambda b, i: (b, i, cols["qb"] // nh)),
                  pl.BlockSpec((None, s, d), lambda b, i: (b, 0, cols["small"]), **resident),
                  pl.BlockSpec((None, s, w512), lambda b, i: (b, 0, cols["kb"] // nh), **resident),
                  pl.BlockSpec((w512, s), lambda b, i: (0, b), **resident),
                  pl.BlockSpec((3, nh, tq, tq), lambda b, i: (0, 0, 0, 0), **resident)],
        out_specs=pl.BlockSpec((None, tq, w512), lambda b, i: (b, i, 0)),
        out_shape=jax.ShapeDtypeStruct((bsz, s, w512), BF16),
        scratch_shapes=[pltpu.VMEM((s, tq), F32),
                        pltpu.VMEM((s, tq), BF16),
                        pltpu.VMEM((IDX_HEADS // 2, 3 * IDX_DIM, 2 * tq), BF16),
                        pltpu.VMEM((s, 3 * IDX_DIM), BF16),
                        pltpu.VMEM((nh // 2, 2 * d, 2 * tq), BF16),
                        pltpu.VMEM((nh, s, tq), F32)],
        compiler_params=pltpu.CompilerParams(
            dimension_semantics=("parallel", "arbitrary"), vmem_limit_bytes=VMEM_LIMIT),
        name="dsa",
    )(p32, p32, p32, p32, p16, vt, bias_tiles)


def _t5_bucket(rel):
    nb = REL_BUCKETS // 2
    max_exact = nb // 2
    ret = jnp.where(rel > 0, nb, 0)
    n = jnp.abs(rel)
    large = max_exact + (jnp.log(jnp.maximum(n, 1).astype(F32) / max_exact)
                         / math.log(REL_MAX_DIST / max_exact) * (nb - max_exact)).astype(jnp.int32)
    large = jnp.minimum(large, nb - 1)
    return ret + jnp.where(n < max_exact, n, large)


def _bias_tiles(rel_table, tq):
    assert tq >= REL_MAX_DIST
    t = jnp.arange(tq)
    back = jnp.arange(3)
    rel = (t[None, None, :] - back[:, None, None] * tq) - t[None, :, None]
    onehot = (_t5_bucket(rel)[..., None] == jnp.arange(REL_BUCKETS)).astype(F32)
    tiles = jnp.einsum("bqkn,nh->bhkq", onehot, rel_table.astype(F32),
                       precision=HIGHEST)
    return (tiles - tiles[2:3]) * LOG2E


def _even_layout(w_in):
    d = HEAD_DIM
    a_w = 2 * N_HEADS * d + N_HEADS * d
    offs = {}
    o = 0
    for name, w in (("qkv", a_w), ("z", N_HEADS * d), ("a", N_HEADS), ("b", N_HEADS),
                    ("qb", N_HEADS * d), ("kb", N_HEADS * d), ("vb", N_HEADS * d),
                    ("qi", IDX_HEADS * IDX_DIM), ("ki", IDX_DIM), ("wi", IDX_HEADS)):
        offs[name] = (o, o + w)
        o += w
    assert o == w_in.shape[1]
    sl = lambda n: w_in[:, offs[n][0]:offs[n][1]]
    small_w = IDX_DIM + 2 * N_HEADS + IDX_HEADS
    small_pad = -small_w % d
    zeros = lambda n: jnp.zeros((w_in.shape[0], n), w_in.dtype)
    w32 = jnp.concatenate([sl("qkv"), sl("z"), sl("qb"), sl("qi"),
                           sl("ki"), sl("a"), sl("b"), sl("wi"), zeros(small_pad)], axis=1)
    n32 = w32.shape[1]
    tn = n32 // 5
    assert tn * 5 == n32 and tn % d == 0
    w16 = jnp.concatenate([sl("kb"), zeros(tn - N_HEADS * d)], axis=1)
    nh = N_HEADS
    cols = dict(qa=0, ka=nh, va=2 * nh, za=3 * nh, qb=4 * nh, qi=5 * nh, small=6 * nh, kb=0,
                a_lane=IDX_DIM, b_lane=IDX_DIM + nh, wi_lane=IDX_DIM + 2 * nh, n32=n32, tn=tn)
    return jnp.concatenate([w32, w16], axis=1).astype(BF16), sl("vb").T.astype(BF16), cols


def kernel(x, norm_g, w_in_even, conv_w_even, a_log_even, dt_bias_even, a_norm_even, w_out_even,
           rel_bias, w_in_odd, lb_logits, d_norm_odd, w_out_odd, w_gate, w_up, w_down):
    bsz, s, d = x.shape
    t = bsz * s
    depth = norm_g.shape[0]
    nh = N_HEADS
    tq = Q_TILE
    lb_all = jnp.cumsum(jax.nn.softmax(lb_logits.astype(F32), axis=0), axis=0)
    lb_all = lb_all - lb_all[:1]
    odd_cols = dict(qc=0, kc=nh, vc=2 * nh, qd=0, fd=nh, id=2 * nh, gd=3 * nh)
    bias_tiles = _bias_tiles(rel_bias, tq)

    h = x.reshape(t, d)
    for l in range(depth):
        if l % 2 == 0:
            e = l // 2
            w_even, w_vt, cols = _even_layout(w_in_even[e])
            p32, p16, vt = _norm_matmul(h, norm_g[l, 0], w_even, tm=PROJ_TILE, tn=cols["tn"], n32=cols["n32"],
                                        w_t=w_vt)
            p32 = p32.reshape(bsz, s, -1)
            p16 = p16.reshape(bsz, s, -1)
            o_1 = _deltanet(p32, conv_w_even[e], a_log_even[e], dt_bias_even[e], a_norm_even[e],
                            ts=min(DELTANET_TILE, s), cols=cols)
            o_2 = _dsa(p32, p16, vt, bias_tiles, tq=tq, cols=cols)
            w_out = w_out_even[e]
        else:
            o = l // 2
            n16 = 3 * nh * HEAD_DIM
            w_odd = jnp.concatenate([w_in_odd[o][:, n16:], w_in_odd[o][:, :n16]], axis=1).astype(BF16)
            p32, p16 = _norm_matmul(h, norm_g[l, 0], w_odd, tm=PROJ_TILE, tn=ODD_COL_TILE, n32=w_odd.shape[1] - n16)
            p32 = p32.reshape(bsz, s, -1)
            p16 = p16.reshape(bsz, s, -1)
            o_1 = _stickbreak(p16, tq=tq, cols=odd_cols)
            o_2 = _hgrn2(p32, lb_all[l], d_norm_odd[o], ts=min(SEQ_TILE, s), cols=odd_cols)
            w_out = w_out_odd[o]
        h = _mix_ffn(o_1.reshape(t, -1), o_2.reshape(t, -1), w_out, h, norm_g[l, 1], norm_g[l, 2], norm_g[l, 3],
                     w_gate[l], w_up[l], w_down[l], tm=ROW_TILE, tf=FFN_TILE)
    return h.reshape(bsz, s, d)
```

```python
import functools
import math

import jax
import jax.numpy as jnp
from jax import lax
from jax.experimental import pallas as pl
from jax.experimental.pallas import tpu as pltpu

F32 = jnp.float32
BF16 = jnp.bfloat16
HIGHEST = lax.Precision.HIGHEST

CHUNK = 64
HEAD_DIM = 128
N_HEADS = 4
IDX_HEADS = 8
IDX_DIM = 64
TOPK_MAX = 256
CONV_WIDTH = 4
REL_BUCKETS = 32
REL_MAX_DIST = 128
EPS = 1e-6
NEG_BIG = -1e30
LOG2E = 1.4426950408889634
BISECT_COARSE = 10
BISECT_FIXED = 7
BISECT_EXTRA = 6
F32_LOWEST = -3.4028234663852886e38
EXP_ZERO_BELOW = -104.0
VMEM_LIMIT = 56 * 1024 * 1024

PROJ_TILE = 2048
ROW_TILE = 512
DELTANET_TILE = 1024
SEQ_TILE = 512
Q_TILE = 128
ODD_COL_TILE = 512
FFN_TILE = 2816


def _mm(a, b):
    return jnp.dot(a.astype(BF16), b.astype(BF16), preferred_element_type=F32)


def _mm_nt(a, b):
    return lax.dot_general(a.astype(BF16), b.astype(BF16), (((1,), (1,)), ((), ())),
                           preferred_element_type=F32)


def _mm_tn(a, b):
    return lax.dot_general(a.astype(BF16), b.astype(BF16), (((0,), (0,)), ((), ())),
                           preferred_element_type=F32)


def _split(x):
    hi = x.astype(BF16)
    return hi, (x - hi.astype(F32)).astype(BF16)


def _floor_bf16(x):
    bits = pltpu.bitcast(x, jnp.int32)
    down = jnp.where(bits >= 0, bits, bits + 0xFFFF) & jnp.int32(-65536)
    return pltpu.bitcast(down, F32).astype(BF16)


def _sigmoid(x):
    return 1.0 / (1.0 + jnp.exp(-x))


def _silu(x):
    return x * _sigmoid(x)


def _softplus(x):
    return jnp.maximum(x, 0.0) + jnp.log1p(jnp.exp(-jnp.abs(x)))


def _rms(x, g):
    return x * lax.rsqrt(jnp.mean(x * x, axis=-1, keepdims=True) + EPS) * g


def _iota(shape, dim):
    return lax.broadcasted_iota(jnp.int32, shape, dim)


def _ind(mask):
    return jnp.where(mask, 1.0, 0.0)


def _norm_matmul_kernel(x_ref, g_ref, w_ref, *rest, n_t, tiles32):
    if n_t:
        wt_ref, o32_ref, o16_ref, ot_ref, xn_ref = rest
    else:
        o32_ref, o16_ref, xn_ref = rest
    j = pl.program_id(1)

    @pl.when(j == 0)
    def _():
        xn_ref[...] = _rms(x_ref[...], g_ref[...]).astype(BF16)
        if n_t:
            ot_ref[...] = lax.dot_general(wt_ref[...], xn_ref[...], (((1,), (1,)), ((), ())),
                                          preferred_element_type=F32).astype(BF16)

    y = jnp.dot(xn_ref[...], w_ref[...], preferred_element_type=F32)

    @pl.when(j < tiles32)
    def _():
        o32_ref[...] = y

    @pl.when(j >= tiles32)
    def _():
        o16_ref[...] = y.astype(BF16)


def _norm_matmul(x, g, w, *, tm, tn, n32, w_t=None):
    t, d = x.shape
    n = w.shape[1]
    n_t = 0 if w_t is None else w_t.shape[0]
    tiles32 = n32 // tn
    assert tiles32 * tn == n32 and (n - n32) % tn == 0 and 0 < n32 < n
    in_specs = [pl.BlockSpec((tm, d), lambda i, j: (i, 0)),
                pl.BlockSpec((1, d), lambda i, j: (0, 0)),
                pl.BlockSpec((d, tn), lambda i, j: (0, j))]
    out_specs = [pl.BlockSpec((tm, tn), lambda i, j: (i, jnp.minimum(j, tiles32 - 1))),
                 pl.BlockSpec((tm, tn), lambda i, j: (i, jnp.maximum(j - tiles32, 0)))]
    out_shape = [jax.ShapeDtypeStruct((t, n32), F32), jax.ShapeDtypeStruct((t, n - n32), BF16)]
    args = [x, g.reshape(1, d), w]
    if n_t:
        in_specs.append(pl.BlockSpec((n_t, d), lambda i, j: (0, 0)))
        out_specs.append(pl.BlockSpec((n_t, tm), lambda i, j: (0, i)))
        out_shape.append(jax.ShapeDtypeStruct((n_t, t), BF16))
        args.append(w_t)
    return pl.pallas_call(
        functools.partial(_norm_matmul_kernel, n_t=n_t, tiles32=tiles32),
        grid=(t // tm, n // tn),
        in_specs=in_specs,
        out_specs=out_specs,
        out_shape=out_shape,
        scratch_shapes=[pltpu.VMEM((tm, d), BF16)],
        compiler_params=pltpu.CompilerParams(
            dimension_semantics=("parallel", "arbitrary"), vmem_limit_bytes=VMEM_LIMIT),
        name="norm_matmul",
    )(*args)


def _mix_ffn_kernel(ca_ref, cb_ref, wa_ref, wb_ref, h_ref, gmix_ref, gpre_ref, gpost_ref,
                    wg_ref, wu_ref, wd_ref, o_ref, h1_ref, xn_ref, acc_ref):
    f = pl.program_id(1)

    @pl.when(f == 0)
    def _():
        y = (jnp.dot(ca_ref[...], wa_ref[...], preferred_element_type=F32)
             + jnp.dot(cb_ref[...], wb_ref[...], preferred_element_type=F32))
        h1 = h_ref[...] + _rms(y, gmix_ref[...])
        h1_ref[...] = h1
        xn_ref[...] = _rms(h1, gpre_ref[...]).astype(BF16)
        acc_ref[...] = jnp.zeros_like(acc_ref)

    xn = xn_ref[...]
    gate = jnp.dot(xn, wg_ref[...], preferred_element_type=F32)
    up = jnp.dot(xn, wu_ref[...], preferred_element_type=F32)
    act = (_silu(gate) * up).astype(BF16)
    acc_ref[...] += jnp.dot(act, wd_ref[...], preferred_element_type=F32)

    @pl.when(f == pl.num_programs(1) - 1)
    def _():
        o_ref[...] = h1_ref[...] + _rms(acc_ref[...], gpost_ref[...])


def _mix_ffn(ca, cb, w_out, h, g_mix, g_pre, g_post, wg, wu, wd, *, tm, tf):
    t, d = h.shape
    ff = wg.shape[1]
    wa_n = ca.shape[1]
    wb_n = cb.shape[1]
    row = pl.BlockSpec((1, d), lambda i, f: (0, 0))
    once = dict(pipeline_mode=pl.Buffered(1)) if tf == ff else {}
    return pl.pallas_call(
        _mix_ffn_kernel,
        grid=(t // tm, ff // tf),
        in_specs=[pl.BlockSpec((tm, wa_n), lambda i, f: (i, 0)),
                  pl.BlockSpec((tm, wb_n), lambda i, f: (i, 0)),
                  pl.BlockSpec((wa_n, d), lambda i, f: (0, 0)),
                  pl.BlockSpec((wb_n, d), lambda i, f: (0, 0)),
                  pl.BlockSpec((tm, d), lambda i, f: (i, 0)),
                  row, row, row,
                  pl.BlockSpec((d, tf), lambda i, f: (0, f), **once),
                  pl.BlockSpec((d, tf), lambda i, f: (0, f), **once),
                  pl.BlockSpec((tf, d), lambda i, f: (f, 0), **once)],
        out_specs=pl.BlockSpec((tm, d), lambda i, f: (i, 0)),
        out_shape=jax.ShapeDtypeStruct((t, d), F32),
        scratch_shapes=[pltpu.VMEM((tm, d), F32), pltpu.VMEM((tm, d), BF16), pltpu.VMEM((tm, d), F32)],
        compiler_params=pltpu.CompilerParams(
            dimension_semantics=("parallel", "arbitrary"), vmem_limit_bytes=VMEM_LIMIT),
        name="mix_ffn",
    )(ca, cb, w_out[:wa_n].astype(BF16), w_out[wa_n:].astype(BF16), h,
      g_mix.reshape(1, d), g_pre.reshape(1, d), g_post.reshape(1, d),
      wg.astype(BF16), wu.astype(BF16), wd.astype(BF16))


def _deltanet_kernel(xq_ref, xk_ref, xv_ref, z_ref, sm_ref, cwq_ref, cwk_ref, cwv_ref,
                     alog_ref, dtb_ref, gn_ref, o_ref,
                     xpad_ref, q_ref, k_ref, v_ref, gb_ref, bb_ref, u_ref, w_ref, qk_ref, st_ref,
                     *, ts, a_col, b_col):
    s = pl.program_id(1)
    c = CHUNK
    d = HEAD_DIM
    nh = N_HEADS

    @pl.when(s == 0)
    def _():
        xpad_ref[:, 0:8, :] = jnp.zeros((3, 8, nh * d), F32)
        st_ref[...] = jnp.zeros_like(st_ref)

    @pl.when(s != 0)
    def _():
        xpad_ref[:, 0:8, :] = xpad_ref[:, ts:ts + 8, :]

    xpad_ref[0, 8:ts + 8, :] = xq_ref[...]
    xpad_ref[1, 8:ts + 8, :] = xk_ref[...]
    xpad_ref[2, 8:ts + 8, :] = xv_ref[...]

    def conv_silu(idx, cw_ref, hs):
        cw = cw_ref[:, hs]
        acc = xpad_ref[idx, 8 - (CONV_WIDTH - 1):8 - (CONV_WIDTH - 1) + ts, hs] * cw[0:1, :]
        for j in range(1, CONV_WIDTH):
            off = 8 - (CONV_WIDTH - 1) + j
            acc = acc + xpad_ref[idx, off:off + ts, hs] * cw[j:j + 1, :]
        return _silu(acc)

    def l2norm(t):
        return t * lax.rsqrt(jnp.sum(t * t, axis=-1, keepdims=True) + EPS)

    row = _iota((c, c), 0)
    col = _iota((c, c), 1)
    tri = (col <= row)
    strict = (col < row)
    tri_f = tri.astype(F32)
    upper_f = (row <= col).astype(F32)
    eye = (row == col).astype(F32)
    gnorm = gn_ref[...]
    chunks = range(ts // c)
    rs = [slice(ci * c, (ci + 1) * c) for ci in chunks]
    tri2 = jnp.concatenate([tri_f, tri_f], axis=1).astype(BF16)
    ones2 = jnp.ones((c, 2 * c), BF16)

    def cum2(lhs2, x):
        hi, lo = _split(x)
        return jnp.dot(lhs2, jnp.concatenate([hi, lo], axis=0), preferred_element_type=F32)

    for hh in range(nh):
        hs = slice(hh * d, (hh + 1) * d)
        q_ref[:, hs] = l2norm(conv_silu(0, cwq_ref, hs)) * (d ** -0.5)
        k_ref[:, hs] = l2norm(conv_silu(1, cwk_ref, hs))
        v_ref[:, hs] = conv_silu(2, cwv_ref, hs)

        a_raw = sm_ref[:, a_col + hh:a_col + hh + 1]
        b_raw = sm_ref[:, b_col + hh:b_col + hh + 1]
        g = -jnp.exp(alog_ref[:, hh:hh + 1]) * _softplus(a_raw + dtb_ref[:, hh:hh + 1])
        gb_ref[:, hs] = jnp.broadcast_to(g, (ts, d))
        bb_ref[:, hs] = jnp.broadcast_to(_sigmoid(b_raw), (ts, d))

        q = [q_ref[r, hs] for r in rs]
        k = [k_ref[r, hs] for r in rs]
        beta = [bb_ref[r, hs] for r in rs]
        gb = [gb_ref[r, hs] for r in rs]
        gc = [cum2(tri2, x) for x in gb]
        gc_row = [cum2(ones2, x[:, :c] * upper_f) for x in gb]
        decay = [jnp.where(tri, jnp.exp(jnp.minimum(a[:, :c] - b, 0.0)), 0.0) for a, b in zip(gc, gc_row)]
        kk = [_mm_nt(x, x) for x in k]
        n = [-jnp.where(strict, b[:, :c] * x * dc, 0.0) for b, x, dc in zip(beta, kk, decay)]
        inv = [eye + x for x in n]
        for step in range(5):
            nb = [x.astype(BF16) for x in n]
            n = [jnp.dot(x, x, preferred_element_type=F32) for x in nb]
            inv = [iv + _mm(iv, x) for iv, x in zip(inv, n)]
        egc = [jnp.exp(x) for x in gc]
        gl = [x[c - 1:c, :] for x in gc]
        inv_l = [x.astype(BF16) for x in inv]
        u = [_mm(a, v_ref[r, hs] * b) for a, r, b in zip(inv_l, rs, beta)]
        w = [_mm(a, x * (b * e)) for a, x, b, e in zip(inv_l, k, beta, egc)]
        qk = [_mm_nt(a, b) * dc for a, b, dc in zip(q, k, decay)]
        for ci in chunks:
            r = rs[ci]
            u_ref[r, hs] = u[ci]
            w_ref[r, hs] = w[ci]
            qk_ref[hh, r, :] = qk[ci]
            q_ref[r, hs] = q[ci] * egc[ci]
            k_ref[r, hs] = k[ci] * jnp.exp(gl[ci] - gc[ci])
            gb_ref[r, hs] = jnp.broadcast_to(jnp.exp(gl[ci]), (c, d))

    hss = [slice(hh * d, (hh + 1) * d) for hh in range(nh)]

    def chunk_step(ci, st):
        r0 = pl.multiple_of(ci * c, c)
        rows = pl.ds(r0, c)
        w_st = [_mm(w_ref[rows, hs], s_) for hs, s_ in zip(hss, st)]
        q_st = [_mm(q_ref[rows, hs], s_) for hs, s_ in zip(hss, st)]
        v_new = [u_ref[rows, hs] - x for hs, x in zip(hss, w_st)]
        o = [a + _mm(qk_ref[hh, rows, :], v) for hh, (a, v) in enumerate(zip(q_st, v_new))]
        kv = [_mm_tn(k_ref[rows, hs], v) for hs, v in zip(hss, v_new)]
        for hh, hs in enumerate(hss):
            o_ref[rows, hs] = (_rms(o[hh], gnorm) * _silu(z_ref[rows, hs])).astype(o_ref.dtype)
        return [s_ * gb_ref[pl.ds(r0, 1), hs] + x for s_, hs, x in zip(st, hss, kv)]

    per_step = math.gcd(ts // c, 4)

    def chunk_group(j, carry):
        st = [st_ref[hh] for hh in range(nh)]
        for n_ in range(per_step):
            st = chunk_step(per_step * j + n_, st)
        for hh in range(nh):
            st_ref[hh] = st[hh]
        return carry

    lax.fori_loop(0, ts // (per_step * c), chunk_group, 0)


def _deltanet(p32, conv_w, a_log, dt_bias, a_norm_g, *, ts, cols):
    bsz, s, _ = p32.shape
    d = HEAD_DIM
    nh = N_HEADS
    w = nh * d
    pad = lambda t: jnp.pad(t.astype(F32), (0, d - t.shape[0])).reshape(1, d)
    kernel = functools.partial(_deltanet_kernel, ts=ts, a_col=cols["a_lane"], b_col=cols["b_lane"])
    tile = lambda name: pl.BlockSpec((None, ts, w), lambda b, i: (b, i, cols[name] // nh))
    conv = lambda k: pl.BlockSpec((CONV_WIDTH, w), lambda b, i: (0, k))
    row = pl.BlockSpec((1, d), lambda b, i: (0, 0))
    return pl.pallas_call(
        kernel,
        grid=(bsz, s // ts),
        in_specs=[tile("qa"), tile("ka"), tile("va"), tile("za"),
                  pl.BlockSpec((None, ts, d), lambda b, i: (b, i, cols["small"])),
                  conv(0), conv(1), conv(2), row, row, row],
        out_specs=pl.BlockSpec((None, ts, w), lambda b, i: (b, i, 0)),
        out_shape=jax.ShapeDtypeStruct((bsz, s, w), BF16),
        scratch_shapes=[pltpu.VMEM((3, ts + 8, w), F32)]
        + [pltpu.VMEM((ts, w), F32) for _ in range(7)]
        + [pltpu.VMEM((nh, ts, CHUNK), F32), pltpu.VMEM((nh, d, d), F32)],
        compiler_params=pltpu.CompilerParams(
            dimension_semantics=("parallel", "arbitrary"), vmem_limit_bytes=VMEM_LIMIT),
        name="deltanet",
    )(p32, p32, p32, p32, p32, conv_w.astype(F32), conv_w.astype(F32), conv_w.astype(F32),
      pad(a_log), pad(dt_bias), a_norm_g.astype(F32).reshape(1, d))


def _hgrn2_kernel(q_ref, f_ref, i_ref, gate_ref, lb_ref, gn_ref, o_ref,
                  qs_ref, ks_ref, gc_ref, st_ref, *, ts):
    s = pl.program_id(1)
    c = CHUNK
    d = HEAD_DIM
    nh = N_HEADS
    SUB = 16

    @pl.when(s == 0)
    def _():
        st_ref[...] = jnp.zeros_like(st_ref)

    lb = lb_ref[...]
    f_raw = f_ref[...]
    log_sig = jnp.minimum(f_raw, 0.0) - jnp.log1p(jnp.exp(-jnp.abs(f_raw)))
    la = jnp.log(lb)
    lbb = jnp.log1p(-lb) + log_sig
    log_f = jnp.maximum(la, lbb) + jnp.log1p(jnp.exp(-jnp.abs(la - lbb)))
    qs_ref[...] = _silu(q_ref[...])
    ks_ref[...] = (1.0 - lb) * _sigmoid(-f_raw)

    row = _iota((c, c), 0)
    col = _iota((c, c), 1)
    tri_f = (col <= row).astype(F32)
    ones_dd = jnp.ones((d, d), BF16)
    rows_8d = _iota((8, d), 0)
    gnorm = gn_ref[...]

    tri2 = jnp.concatenate([tri_f, tri_f], axis=1).astype(BF16)
    for ci in range(ts // c):
        hi, lo = _split(log_f[ci * c:(ci + 1) * c, :])
        gc_ref[ci * c:(ci + 1) * c, :] = jnp.dot(tri2, jnp.concatenate([hi, lo], axis=0),
                                                 preferred_element_type=F32)

    blocks = [(sb * SUB, (sb + 1) * SUB) for sb in range(c // SUB)]

    def chunk_loop(ci, carry):
        r0 = pl.multiple_of(ci * c, c)
        rows = pl.ds(r0, c)
        hss = [slice(hh * d, (hh + 1) * d) for hh in range(nh)]
        q = [qs_ref[rows, hs] for hs in hss]
        k = [ks_ref[rows, hs] for hs in hss]
        v = [i_ref[rows, hs] for hs in hss]
        gc = [gc_ref[rows, hs] for hs in hss]

        def near_products(q, k, gc):
            prods = []
            for top, end in blocks:
                for j in range(top, end):
                    lo = (j // 8) * 8
                    e = jnp.exp2(gc[lo:end, :] - gc[j:j + 1, :])
                    if j % 8:
                        head = jnp.where(rows_8d >= j - lo, e[:8], 0.0)
                        e = jnp.concatenate([head, e[8:]], axis=0) if lo + 8 < end else head
                    prods.append(q[lo:end, :] * k[j:j + 1, :] * e)
            return jnp.concatenate(prods, axis=0).astype(BF16)

        def far_operands(q, k, gc):
            out = []
            for top, end in blocks[1:]:
                g_b = gc[top - 1:top, :]
                out.append((q[top:end, :] * jnp.exp(gc[top:end, :] - g_b),
                            k[:top, :] * jnp.exp(jnp.minimum(g_b - gc[:top, :], 0.0))))
            return out

        near = [near_products(a, b, g * LOG2E) for a, b, g in zip(q, k, gc)]
        far_ops = [far_operands(*x) for x in zip(q, k, gc)]
        st = [st_ref[hh] for hh in range(nh)]
        gl = [x[c - 1:c, :] for x in gc]
        sums = [jnp.dot(x, ones_dd, preferred_element_type=F32) for x in near]
        qk_far = [[_mm_nt(qe, ke) for qe, ke in ops] for ops in far_ops]
        far = [[_mm(a, vv[:top, :]) for a, (top, _) in zip(qs, blocks[1:])] for qs, vv in zip(qk_far, v)]
        o_st = [_mm_nt(a * jnp.exp(g), s_) for a, g, s_ in zip(q, gc, st)]
        kv = [_mm_tn(vv, kk * jnp.exp(g_l - g)) for vv, kk, g_l, g in zip(v, k, gl, gc)]

        for hh, hs in enumerate(hss):
            groups = [jnp.zeros((8, d), F32) for _ in range(c // 8)]
            at = 0
            for top, end in blocks:
                for j in range(top, end):
                    v_j = v[hh][j:j + 1, :]
                    for g in range(j // 8, end // 8):
                        groups[g] = groups[g] + sums[hh][at:at + 8, :] * v_j
                        at += 8
            for f, (top, end) in zip(far[hh], blocks[1:]):
                for g in range(top // 8, end // 8):
                    groups[g] = groups[g] + f[(g * 8 - top):(g * 8 - top + 8), :]
            o = jnp.concatenate(groups, axis=0) + o_st[hh]
            st_ref[hh] = st[hh] * jnp.exp(gl[hh]) + kv[hh]
            o_ref[rows, hs] = (_rms(o, gnorm) * _silu(gate_ref[rows, hs])).astype(o_ref.dtype)
        return carry

    per_step = math.gcd(ts // c, 4)

    def chunk_group(j, carry):
        for n_ in range(per_step):
            chunk_loop(per_step * j + n_, carry)
        return carry

    lax.fori_loop(0, ts // (per_step * c), chunk_group, 0)


def _hgrn2(p32, lb, d_norm_g, *, ts, cols):
    bsz, s, _ = p32.shape
    d = HEAD_DIM
    nh = N_HEADS
    w = nh * d
    kernel = functools.partial(_hgrn2_kernel, ts=ts)
    tile = lambda name: pl.BlockSpec((None, ts, w), lambda b, i: (b, i, cols[name] // nh))
    return pl.pallas_call(
        kernel,
        grid=(bsz, s // ts),
        in_specs=[tile("qd"), tile("fd"), tile("id"), tile("gd"),
                  pl.BlockSpec((1, w), lambda b, i: (0, 0)),
                  pl.BlockSpec((1, d), lambda b, i: (0, 0))],
        out_specs=pl.BlockSpec((None, ts, w), lambda b, i: (b, i, 0)),
        out_shape=jax.ShapeDtypeStruct((bsz, s, w), BF16),
        scratch_shapes=[pltpu.VMEM((ts, w), F32), pltpu.VMEM((ts, w), F32),
                        pltpu.VMEM((ts, w), F32), pltpu.VMEM((nh, d, d), F32)],
        compiler_params=pltpu.CompilerParams(
            dimension_semantics=("parallel", "arbitrary"), vmem_limit_bytes=VMEM_LIMIT),
        name="hgrn2",
    )(p32, p32, p32, p32, lb.astype(F32).reshape(1, w), d_norm_g.astype(F32).reshape(1, d))


def _stickbreak_kernel(q_ref, k_ref, v_ref, o_ref, acc_ref, *, tq):
    i = pl.program_id(1)
    d = HEAD_DIM
    nh = N_HEADS
    row = _iota((tq, tq), 0)
    col = _iota((tq, tq), 1)
    causal = col < row
    later = (row > col).astype(BF16)
    later2 = jnp.concatenate([later, later], axis=0)

    heads = [slice(hh * d, (hh + 1) * d) for hh in range(nh)]

    def scores(blocks):
        jobs = [(j, dg, hs) for j, dg in blocks for hs in heads]
        z = [_mm_nt(q_ref[:, hs], k_ref[pl.ds(pl.multiple_of(j * tq, tq), tq), hs]) * (d ** -0.5)
             for j, _, hs in jobs]
        sp = [_softplus(x) for x in z]
        l1m = [jnp.where(causal, -x, 0.0) if dg else -x for x, (_, dg, _) in zip(sp, jobs)]
        rest = [jnp.dot(jnp.concatenate(_split(x), axis=1), later2, preferred_element_type=F32)
                for x in l1m]
        out = [((a - b) + r, l) for a, b, r, l in zip(z, sp, rest, l1m)]
        return [out[b * nh:(b + 1) * nh] for b in range(len(blocks))]

    def block(j, carries):
        (sc,) = scores([(j, False)])
        ps = [jnp.exp(logw + c) for (logw, _), c in zip(sc, carries)]
        pv = [_mm(p, v_ref[pl.ds(pl.multiple_of(j * tq, tq), tq), hs]) for p, hs in zip(ps, heads)]
        for hs, x in zip(heads, pv):
            acc_ref[:, hs] += x
        return tuple(c + jnp.sum(l1m, axis=-1, keepdims=True) for (_, l1m), c in zip(sc, carries))

    j1 = jnp.maximum(i - 1, 0)
    j2 = jnp.maximum(i - 2, 0)
    live1 = jnp.where(i > 0, 1.0, 0.0)
    live2 = jnp.where(i > 1, 1.0, 0.0)
    s0, s1, s2 = scores([(i, True), (j1, False), (j2, False)])
    carries = []
    for hh, hs in enumerate(heads):
        c0 = jnp.sum(s0[hh][1], axis=-1, keepdims=True)
        c1 = c0 + jnp.sum(s1[hh][1], axis=-1, keepdims=True)
        p0 = jnp.where(causal, jnp.exp(s0[hh][0]), 0.0)
        p1 = jnp.exp(s1[hh][0] + c0) * live1
        p2 = jnp.exp(s2[hh][0] + c1) * live2
        acc_ref[:, hs] = (_mm(p0, v_ref[pl.ds(pl.multiple_of(i * tq, tq), tq), hs])
                          + _mm(p1, v_ref[pl.ds(pl.multiple_of(j1 * tq, tq), tq), hs])
                          + _mm(p2, v_ref[pl.ds(pl.multiple_of(j2 * tq, tq), tq), hs]))
        carries.append(c1 + jnp.sum(s2[hh][1], axis=-1, keepdims=True))
    carries = tuple(carries)

    def cond(c):
        worst = functools.reduce(jnp.maximum, c[1])
        return jnp.logical_and(c[0] >= 0, jnp.max(worst) >= EXP_ZERO_BELOW)

    def body(c):
        return c[0] - 1, block(c[0], c[1])

    lax.while_loop(cond, body, (i - 3, carries))
    o_ref[...] = acc_ref[...].astype(o_ref.dtype)


def _stickbreak(p16, *, tq, cols):
    bsz, s, _ = p16.shape
    nh = N_HEADS
    w = nh * HEAD_DIM
    kernel = functools.partial(_stickbreak_kernel, tq=tq)
    resident = dict(pipeline_mode=pl.Buffered(1))
    return pl.pallas_call(
        kernel,
        grid=(bsz, s // tq),
        in_specs=[pl.BlockSpec((None, tq, w), lambda b, i: (b, i, cols["qc"] // nh)),
                  pl.BlockSpec((None, s, w), lambda b, i: (b, 0, cols["kc"] // nh), **resident),
                  pl.BlockSpec((None, s, w), lambda b, i: (b, 0, cols["vc"] // nh), **resident)],
        out_specs=pl.BlockSpec((None, tq, w), lambda b, i: (b, i, 0)),
        out_shape=jax.ShapeDtypeStruct((bsz, s, w), BF16),
        scratch_shapes=[pltpu.VMEM((tq, w), F32)],
        compiler_params=pltpu.CompilerParams(
            dimension_semantics=("parallel", "arbitrary"), vmem_limit_bytes=VMEM_LIMIT),
        name="stickbreak",
    )(p16, p16, p16)


def _dsa_kernel(qi_ref, smq_ref, q_ref, sm_ref, k_ref, vt_ref, bias_ref, o_ref,
                sc_ref, scb_ref, qct_ref, kc_ref, bd_ref, lg_ref, *, tq, k_sel, wi_lane, wide):
    i = pl.program_id(1)
    tk = tq
    d = HEAD_DIM
    nh = N_HEADS
    ksel = float(k_sel)
    per_wide = wide // tk
    n_wide = (i + per_wide) // per_wide
    sub = 2 * tk
    lane_q = _iota((1, tq), 1)

    def tree(parts, op):
        while len(parts) > 1:
            parts = [op(parts[j], parts[j + 1]) if j + 1 < len(parts) else parts[j]
                     for j in range(0, len(parts), 2)]
        return parts[0]

    def col_fold(x, op=jnp.add, rows=8):
        return tree([x[r * rows:(r + 1) * rows] for r in range(x.shape[0] // rows)], op)

    @pl.when(i == 0)
    def _():
        def prep(g, carry):
            g0 = pl.multiple_of(g * wide, wide)
            hi, lo = _split(sm_ref[pl.ds(g0, wide), :][:, :IDX_DIM])
            kc_ref[pl.ds(g0, wide), :] = jnp.concatenate([hi, lo, hi], axis=1)
            return carry
        lax.fori_loop(0, sm_ref.shape[0] // wide, prep, 0)

    qit = qi_ref[...].T
    for p in range(IDX_HEADS // 2):
        halves = []
        for hh in (2 * p, 2 * p + 1):
            hi, lo = _split(qit[hh * IDX_DIM:(hh + 1) * IDX_DIM, :])
            halves.append(jnp.concatenate([hi, hi, lo], axis=0))
        qct_ref[p] = jnp.concatenate(halves, axis=1)
    w_rows = smq_ref[...].T[wi_lane:wi_lane + IDX_HEADS, :] * ((IDX_HEADS ** -0.5) * (IDX_DIM ** -0.5))

    q2t = (q_ref[...] * ((d ** -0.5) * LOG2E)).T.astype(BF16)
    zero_dq = jnp.zeros((d, tq), BF16)
    for p in range(nh // 2):
        top = jnp.concatenate([q2t[2 * p * d:(2 * p + 1) * d], zero_dq], axis=1)
        bot = jnp.concatenate([zero_dq, q2t[(2 * p + 1) * d:(2 * p + 2) * d]], axis=1)
        bd_ref[p] = jnp.concatenate([top, bot], axis=0)

    limit = i * tq + (lane_q // CHUNK + 1) * CHUNK

    rows_s = _iota((sub, tq), 0)

    def score_groups(gs, mm, masked):
        mn, mx = mm
        k0s = [pl.multiple_of(g * wide + sb * sub, sub) for g in gs for sb in range(wide // sub)]
        keys = [kc_ref[pl.ds(k0, sub), :] for k0 in k0s]
        accs = [jnp.zeros((sub, tq), F32) for _ in k0s]
        for p in range(IDX_HEADS // 2):
            rhs = qct_ref[p]
            for n, kk in enumerate(keys):
                s2 = jnp.dot(kk, rhs, preferred_element_type=F32)
                accs[n] = (accs[n] + jnp.maximum(s2[:, :tq], 0.0) * w_rows[2 * p:2 * p + 1, :]
                           + jnp.maximum(s2[:, tq:], 0.0) * w_rows[2 * p + 1:2 * p + 2, :])
        for k0, sct in zip(k0s, accs):
            if masked:
                adm = (k0 + rows_s) < limit
                mn = jnp.minimum(mn, col_fold(jnp.where(adm, sct, jnp.inf), jnp.minimum))
                sct = jnp.where(adm, sct, -jnp.inf)
            else:
                mn = jnp.minimum(mn, col_fold(sct, jnp.minimum))
            mx = jnp.maximum(mx, col_fold(sct, jnp.maximum))
            sc_ref[pl.ds(k0, sub), :] = sct
            scb_ref[pl.ds(k0, sub), :] = _floor_bf16(sct)
        return mn, mx

    def score_pair(j, mm):
        return score_groups((2 * j, 2 * j + 1), mm, False)

    n_full = n_wide - 1
    mm = lax.fori_loop(0, n_full // 2, score_pair,
                       (jnp.full((8, tq), jnp.inf, F32), jnp.full((8, tq), -jnp.inf, F32)))
    mm = lax.cond(n_full % 2 == 1, lambda c: score_groups((n_full - 1,), c, False), lambda c: c, mm)
    mn, mx = score_groups((n_wide - 1,), mm, True)

    n_pairs = (n_wide + 1) // 2

    @pl.when(n_wide % 2 == 1)
    def _():
        sc_ref[pl.ds(pl.multiple_of(n_wide * wide, wide), wide), :] = jnp.full((wide, tq), -jnp.inf, F32)
        scb_ref[pl.ds(pl.multiple_of(n_wide * wide, wide), wide), :] = jnp.full((wide, tq), -jnp.inf, BF16)
    rmin = jnp.min(mn, axis=0, keepdims=True)
    rmax = jnp.max(mx, axis=0, keepdims=True)

    def count(pred):
        def body(j, acc):
            for g in (2 * j, 2 * j + 1):
                acc = acc + col_fold(pred(sc_ref[pl.ds(pl.multiple_of(g * wide, wide), wide), :]))
            return acc
        return jnp.sum(lax.fori_loop(0, n_pairs, body, jnp.zeros((8, tq), F32)), axis=0, keepdims=True)

    def max_below(x):
        def body(j, acc):
            for g in (2 * j, 2 * j + 1):
                blk = sc_ref[pl.ds(pl.multiple_of(g * wide, wide), wide), :]
                acc = jnp.maximum(acc, col_fold(jnp.where(blk < x, blk, -jnp.inf), jnp.maximum))
            return acc
        return jnp.max(lax.fori_loop(0, n_pairs, body, jnp.full((8, tq), -jnp.inf, F32)), axis=0, keepdims=True)

    n_adm = limit.astype(F32)
    all_sel = n_adm <= ksel

    def bisect(c):
        lo, hi, c_lo = c
        mid = 0.5 * lo + 0.5 * hi
        cm = count(lambda blk: _ind(blk >= mid))
        ge = cm >= ksel
        return jnp.where(ge, mid, lo), jnp.where(ge, hi, mid), jnp.where(ge, cm, c_lo)

    def pending(c_lo, tied):
        return jnp.where(all_sel, 0.0, jnp.where(tied > 0.5, 0.0, _ind(c_lo != ksel)))

    def bisect_coarse(_, c):
        lo, hi, c_lo = c
        mid = _floor_bf16(0.5 * lo + 0.5 * hi).astype(F32)
        t_b = jnp.broadcast_to(mid, (16, tq)).astype(BF16)
        one_b = jnp.ones((16, tq), BF16)
        zero_b = jnp.zeros((16, tq), BF16)

        def body(j, acc):
            for g in (2 * j, 2 * j + 1):
                blk = scb_ref[pl.ds(pl.multiple_of(g * wide, wide), wide), :]
                ind = [jnp.where(blk[r * 16:(r + 1) * 16] >= t_b, one_b, zero_b) for r in range(wide // 16)]
                acc = acc + tree(ind, jnp.add).astype(F32)
            return acc

        acc = lax.fori_loop(0, n_pairs, body, jnp.zeros((16, tq), F32))
        cm = jnp.sum(acc, axis=0, keepdims=True)
        ge = cm >= ksel
        return jnp.where(ge, mid, lo), jnp.where(ge, hi, mid), jnp.where(ge, cm, c_lo)

    lo0 = _floor_bf16(rmin).astype(F32)
    hi0 = _floor_bf16(rmax + (jnp.abs(rmax) * (2.0 ** -6) + 1e-30)).astype(F32)
    state = lax.fori_loop(0, BISECT_COARSE, bisect_coarse, (lo0, hi0, n_adm))
    state = lax.fori_loop(0, BISECT_FIXED, lambda _, c: bisect(c), state)

    def round_cond(c):
        return jnp.max(pending(c[0][2], c[1])) > 0.5

    def round_body(c):
        st, tied, v, need = c

        def more_cond(s):
            return jnp.logical_and(s[0] < BISECT_EXTRA, jnp.max(pending(s[1][2], tied)) > 0.5)

        _, st = lax.while_loop(more_cond, lambda s: (s[0] + 1, bisect(s[1])), (jnp.int32(0), st))
        pend = pending(st[2], tied)

        def check(_):
            cand = max_below(st[1])
            c_ge = count(lambda blk: _ind(blk >= cand))
            c_gt = count(lambda blk: _ind(blk > cand))
            ok = jnp.where(pend > 0.5, _ind(c_ge >= ksel), 0.0)
            return (jnp.where(ok > 0.5, 1.0, tied), jnp.where(ok > 0.5, cand, v),
                    jnp.where(ok > 0.5, ksel - c_gt, need))

        tied, v, need = lax.cond(jnp.max(pend) > 0.5, check, lambda _: (tied, v, need), 0)
        return st, tied, v, need

    zeros1 = jnp.zeros((1, tq), F32)
    (lo_f, _, _), tied, v_tie, need = lax.while_loop(round_cond, round_body, (state, zeros1, zeros1, zeros1))
    vth = jnp.where(all_sel, F32_LOWEST, jnp.where(tied > 0.5, v_tie, lo_f))

    @pl.when(jnp.max(tied) > 0.5)
    def _():
        v_eq = jnp.where(tied > 0.5, v_tie, jnp.inf)
        incl = (_iota((tk, tk), 1) <= _iota((tk, tk), 0)).astype(BF16)

        def demote(g, seen):
            g0 = pl.multiple_of(g * wide, wide)
            xs = [sc_ref[pl.ds(g0 + pb * tk, tk), :] for pb in range(per_wide)]
            eqs = [_ind(x == v_eq) for x in xs]
            inblk = [jnp.dot(incl, e.astype(BF16), preferred_element_type=F32) for e in eqs]
            for pb in range(per_wide):
                rank = inblk[pb] + seen
                sc_ref[pl.ds(g0 + pb * tk, tk), :] = jnp.where(eqs[pb] * _ind(rank > need) > 0.5,
                                                               -jnp.inf, xs[pb])
                seen = seen + jnp.sum(col_fold(eqs[pb]), axis=0, keepdims=True)
            return seen

        lax.fori_loop(0, n_wide, demote, zeros1)

    g_near = jnp.maximum(i - 1, 0) // per_wide

    def logit_group(g, mx, near):
        out = list(mx)
        for sb in range(wide // sub):
            k0 = pl.multiple_of(g * wide + sb * sub, sub)
            sel = sc_ref[pl.ds(k0, sub), :] >= vth
            for p in range(nh // 2):
                pair = jnp.dot(k_ref[pl.ds(k0, sub), 2 * p * d:(2 * p + 2) * d], bd_ref[p],
                               preferred_element_type=F32)
                for hh in (2 * p, 2 * p + 1):
                    lm = pair[:, (hh - 2 * p) * tq:(hh - 2 * p + 1) * tq]
                    if near:
                        back = [jnp.clip(i - (g * per_wide + sb * (sub // tk) + pb), 0, 2)
                                for pb in range(sub // tk)]
                        lm = lm + jnp.concatenate([bias_ref[bk, hh] for bk in back], axis=0)
                    lm = jnp.where(sel, lm, NEG_BIG)
                    lg_ref[hh, pl.ds(k0, sub), :] = lm
                    out[hh] = jnp.maximum(out[hh], col_fold(lm, jnp.maximum))
        return tuple(out)

    mx = tuple(jnp.full((8, tq), NEG_BIG, F32) for _ in range(nh))
    def logit_pair(j, mx, near):
        return logit_group(2 * j + 1, logit_group(2 * j, mx, near), near)

    far_pairs = g_near // 2
    full_pairs = n_wide // 2
    odd = n_wide % 2 == 1
    mx = lax.fori_loop(0, far_pairs, functools.partial(logit_pair, near=False), mx)
    mx = lax.fori_loop(far_pairs, full_pairs, functools.partial(logit_pair, near=True), mx)
    mx = lax.cond(odd, lambda m: logit_group(n_wide - 1, m, True), lambda m: m, mx)
    m_q = [jnp.max(mx[hh], axis=0, keepdims=True) for hh in range(nh)]

    ones_rows = jnp.ones((8, wide), BF16)

    def pv_groups(gs, carry):
        ls, accs = list(carry[0]), list(carry[1])
        jobs = [(pl.multiple_of(g * wide, wide), hh) for g in gs for hh in range(nh)]
        ps = [jnp.exp2(lg_ref[hh, pl.ds(g0, wide), :] - m_q[hh]).astype(BF16) for g0, hh in jobs]
        outs = [jnp.dot(jnp.concatenate([vt_ref[hh * d:(hh + 1) * d, pl.ds(g0, wide)], ones_rows], axis=0),
                        p, preferred_element_type=F32) for (g0, hh), p in zip(jobs, ps)]
        for (_, hh), out in zip(jobs, outs):
            ls[hh] = ls[hh] + out[d:]
            accs[hh] = accs[hh] + out[:d]
        return tuple(ls), tuple(accs)

    acc = lax.fori_loop(0, full_pairs, lambda j, cr: pv_groups((2 * j, 2 * j + 1), cr),
                        (tuple(jnp.zeros((8, tq), F32) for _ in range(nh)),
                         tuple(jnp.zeros((d, tq), F32) for _ in range(nh))))
    ls, accs = lax.cond(odd, lambda cr: pv_groups((n_wide - 1,), cr), lambda cr: cr, acc)
    for hh in range(nh):
        o_ref[:, hh * d:(hh + 1) * d] = (accs[hh] / ls[hh][0:1]).T.astype(o_ref.dtype)


def _dsa(p32, p16, vt, bias_tiles, *, tq, cols):
    bsz, s, _ = p32.shape
    d = HEAD_DIM
    nh = N_HEADS
    wide = 4 * tq
    k_sel = min(TOPK_MAX, s // 4)
    w512 = nh * d
    kernel = functools.partial(_dsa_kernel, tq=tq, k_sel=k_sel, wi_lane=cols["wi_lane"], wide=wide)
    resident = dict(pipeline_mode=pl.Buffered(1))
    return pl.pallas_call(
        kernel,
        grid=(bsz, s // tq),
        in_specs=[pl.BlockSpec((None, tq, w512), lambda b, i: (b, i, cols["qi"] // nh)),
                  pl.BlockSpec((None, tq, d), lambda b, i: (b, i, cols["small"])),
                  pl.BlockSpec((None, tq, w512), lambda b, i: (b, i, cols["qb"] // nh)),
                  pl.BlockSpec((None, s, d), lambda b, i: (b, 0, cols["small"]), **resident),
                  pl.BlockSpec((None, s, w512), lambda b, i: (b, 0, cols["kb"] // nh), **resident),
                  pl.BlockSpec((w512, s), lambda b, i: (0, b), **resident),
                  pl.BlockSpec((3, nh, tq, tq), lambda b, i: (0, 0, 0, 0), **resident)],
        out_specs=pl.BlockSpec((None, tq, w512), lambda b, i: (b, i, 0)),
        out_shape=jax.ShapeDtypeStruct((bsz, s, w512), BF16),
        scratch_shapes=[pltpu.VMEM((s, tq), F32),
                        pltpu.VMEM((s, tq), BF16),
                        pltpu.VMEM((IDX_HEADS // 2, 3 * IDX_DIM, 2 * tq), BF16),
                        pltpu.VMEM((s, 3 * IDX_DIM), BF16),
                        pltpu.VMEM((nh // 2, 2 * d, 2 * tq), BF16),
                        pltpu.VMEM((nh, s, tq), F32)],
        compiler_params=pltpu.CompilerParams(
            dimension_semantics=("parallel", "arbitrary"), vmem_limit_bytes=VMEM_LIMIT),
        name="dsa",
    )(p32, p32, p32, p32, p16, vt, bias_tiles)


def _t5_bucket(rel):
    nb = REL_BUCKETS // 2
    max_exact = nb // 2
    ret = jnp.where(rel > 0, nb, 0)
    n = jnp.abs(rel)
    large = max_exact + (jnp.log(jnp.maximum(n, 1).astype(F32) / max_exact)
                         / math.log(REL_MAX_DIST / max_exact) * (nb - max_exact)).astype(jnp.int32)
    large = jnp.minimum(large, nb - 1)
    return ret + jnp.where(n < max_exact, n, large)


def _bias_tiles(rel_table, tq):
    assert tq >= REL_MAX_DIST
    t = jnp.arange(tq)
    back = jnp.arange(3)
    rel = (t[None, None, :] - back[:, None, None] * tq) - t[None, :, None]
    onehot = (_t5_bucket(rel)[..., None] == jnp.arange(REL_BUCKETS)).astype(F32)
    tiles = jnp.einsum("bqkn,nh->bhkq", onehot, rel_table.astype(F32),
                       precision=HIGHEST)
    return (tiles - tiles[2:3]) * LOG2E


def _even_layout(w_in):
    d = HEAD_DIM
    a_w = 2 * N_HEADS * d + N_HEADS * d
    offs = {}
    o = 0
    for name, w in (("qkv", a_w), ("z", N_HEADS * d), ("a", N_HEADS), ("b", N_HEADS),
                    ("qb", N_HEADS * d), ("kb", N_HEADS * d), ("vb", N_HEADS * d),
                    ("qi", IDX_HEADS * IDX_DIM), ("ki", IDX_DIM), ("wi", IDX_HEADS)):
        offs[name] = (o, o + w)
        o += w
    assert o == w_in.shape[1]
    sl = lambda n: w_in[:, offs[n][0]:offs[n][1]]
    small_w = IDX_DIM + 2 * N_HEADS + IDX_HEADS
    small_pad = -small_w % d
    zeros = lambda n: jnp.zeros((w_in.shape[0], n), w_in.dtype)
    w32 = jnp.concatenate([sl("qkv"), sl("z"), sl("qb"), sl("qi"),
                           sl("ki"), sl("a"), sl("b"), sl("wi"), zeros(small_pad)], axis=1)
    n32 = w32.shape[1]
    tn = n32 // 5
    assert tn * 5 == n32 and tn % d == 0
    w16 = jnp.concatenate([sl("kb"), zeros(tn - N_HEADS * d)], axis=1)
    nh = N_HEADS
    cols = dict(qa=0, ka=nh, va=2 * nh, za=3 * nh, qb=4 * nh, qi=5 * nh, small=6 * nh, kb=0,
                a_lane=IDX_DIM, b_lane=IDX_DIM + nh, wi_lane=IDX_DIM + 2 * nh, n32=n32, tn=tn)
    return jnp.concatenate([w32, w16], axis=1).astype(BF16), sl("vb").T.astype(BF16), cols


def kernel(x, norm_g, w_in_even, conv_w_even, a_log_even, dt_bias_even, a_norm_even, w_out_even,
           rel_bias, w_in_odd, lb_logits, d_norm_odd, w_out_odd, w_gate, w_up, w_down):
    bsz, s, d = x.shape
    t = bsz * s
    depth = norm_g.shape[0]
    nh = N_HEADS
    tq = Q_TILE
    lb_all = jnp.cumsum(jax.nn.softmax(lb_logits.astype(F32), axis=0), axis=0)
    lb_all = lb_all - lb_all[:1]
    odd_cols = dict(qc=0, kc=nh, vc=2 * nh, qd=0, fd=nh, id=2 * nh, gd=3 * nh)
    bias_tiles = _bias_tiles(rel_bias, tq)

    h = x.reshape(t, d)
    for l in range(depth):
        if l % 2 == 0:
            e = l // 2
            w_even, w_vt, cols = _even_layout(w_in_even[e])
            p32, p16, vt = _norm_matmul(h, norm_g[l, 0], w_even, tm=PROJ_TILE, tn=cols["tn"], n32=cols["n32"],
                                        w_t=w_vt)
            p32 = p32.reshape(bsz, s, -1)
            p16 = p16.reshape(bsz, s, -1)
            o_1 = _deltanet(p32, conv_w_even[e], a_log_even[e], dt_bias_even[e], a_norm_even[e],
                            ts=min(DELTANET_TILE, s), cols=cols)
            o_2 = _dsa(p32, p16, vt, bias_tiles, tq=tq, cols=cols)
            w_out = w_out_even[e]
        else:
            o = l // 2
            n16 = 3 * nh * HEAD_DIM
            w_odd = jnp.concatenate([w_in_odd[o][:, n16:], w_in_odd[o][:, :n16]], axis=1).astype(BF16)
            p32, p16 = _norm_matmul(h, norm_g[l, 0], w_odd, tm=PROJ_TILE, tn=ODD_COL_TILE, n32=w_odd.shape[1] - n16)
            p32 = p32.reshape(bsz, s, -1)
            p16 = p16.reshape(bsz, s, -1)
            o_1 = _stickbreak(p16, tq=tq, cols=odd_cols)
            o_2 = _hgrn2(p32, lb_all[l], d_norm_odd[o], ts=min(SEQ_TILE, s), cols=odd_cols)
            w_out = w_out_odd[o]
        h = _mix_ffn(o_1.reshape(t, -1), o_2.reshape(t, -1), w_out, h, norm_g[l, 1], norm_g[l, 2], norm_g[l, 3],
                     w_gate[l], w_up[l], w_down[l], tm=ROW_TILE, tf=FFN_TILE)
    return h.reshape(bsz, s, d)
```
